```python
import math
import jax, jax.numpy as jnp
from jax import lax
import numpy as np

D_MODEL = 1024
BATCH = 8
SEQ = 2048
DEPTH = 2

N_EVEN = (DEPTH + 1) // 2
N_ODD = DEPTH // 2
NORM_EPS = 1e-6

S5_WIDTH = D_MODEL // 2
S5_GROUP = 16
S5_GROUPS = S5_WIDTH // S5_GROUP
S5_STATE = 64
S5_DT_MIN = 1e-3
S5_DT_MAX = 1e-1

HG_WIDTH = D_MODEL // 2
HG_HEADS = 4
HG_KDIM = HG_WIDTH // HG_HEADS
HG_VDIM = HG_WIDTH // HG_HEADS
HG_CHUNK = 64

IN0_COLS = S5_WIDTH + 2 * HG_HEADS * HG_KDIM + 2 * HG_WIDTH

ATT_HEAD_DIM = 64
ATT_HEADS_PER_GROUP = 8
ATT_BRANCHES = ((128, 1), (512, 4), (2048, 16))
N_BRANCH = len(ATT_BRANCHES)
ATT_BLOCK = 128
ATT_GROUP_WIDTH = ATT_HEADS_PER_GROUP * ATT_HEAD_DIM
IN1_COLS = 3 * N_BRANCH * ATT_GROUP_WIDTH
ROT_DIM = ATT_HEAD_DIM // 4
ROPE_THETA = 500000.0

D_FF = 2816
CONV_W = 3

kernel_name = "hybrid_s5_hgrn2_dilated_convffn"


def _rmsnorm(x, g):
    xf = x.astype(jnp.float32)
    y = xf * lax.rsqrt(jnp.mean(xf * xf, axis=-1, keepdims=True) + NORM_EPS)
    return (y * g.astype(jnp.float32)).astype(x.dtype)


def _cplx_scan_op(e1, e2):
    a1r, a1i, b1r, b1i = e1
    a2r, a2i, b2r, b2i = e2
    ar = a2r * a1r - a2i * a1i
    ai = a2r * a1i + a2i * a1r
    br = a2r * b1r - a2i * b1i + b2r
    bi = a2r * b1i + a2i * b1r + b2i
    return (ar, ai, br, bi)


def _s5_mixer(u, A_re, A_im, log_dt, B_re, B_im, C_re, C_im, Dd, glu_w, glu_b):
    Bsz, L, _ = u.shape
    f32 = jnp.float32
    A_re, A_im = A_re.astype(f32), A_im.astype(f32)
    B_re, B_im = B_re.astype(f32), B_im.astype(f32)
    C_re, C_im = C_re.astype(f32), C_im.astype(f32)
    ug = u.astype(f32).reshape(Bsz, L, S5_GROUPS, S5_GROUP)
    dt = jnp.exp(log_dt.astype(f32))[:, None]
    mag = jnp.exp(A_re * dt)
    ab_re = mag * jnp.cos(A_im * dt)
    ab_im = mag * jnp.sin(A_im * dt)
    den = A_re * A_re + A_im * A_im
    nr, ni = ab_re - 1.0, ab_im
    c_re = (nr * A_re + ni * A_im) / den
    c_im = (ni * A_re - nr * A_im) / den
    Bb_re = c_re[..., None] * B_re - c_im[..., None] * B_im
    Bb_im = c_re[..., None] * B_im + c_im[..., None] * B_re
    bu_re = jnp.einsum('gpc,blgc->blgp', Bb_re, ug)
    bu_im = jnp.einsum('gpc,blgc->blgp', Bb_im, ug)
    a_re = jnp.broadcast_to(ab_re[None, None], (1, L, S5_GROUPS, S5_STATE))
    a_im = jnp.broadcast_to(ab_im[None, None], (1, L, S5_GROUPS, S5_STATE))
    _, _, x_re, x_im = lax.associative_scan(_cplx_scan_op, (a_re, a_im, bu_re, bu_im), axis=1)
    y = (jnp.einsum('gcp,blgp->blgc', C_re, x_re)
         - jnp.einsum('gcp,blgp->blgc', C_im, x_im)
         + Dd.astype(f32) * ug)
    z = jax.nn.gelu(y.reshape(Bsz, L, S5_WIDTH))
    return z * jax.nn.sigmoid(z @ glu_w.astype(f32) + glu_b.astype(f32))


def _hgrn2_mixer(xq, xf, xi, xg, lb, norm_g):
    Bsz, L, _ = xq.shape
    f32 = jnp.float32
    nc = L // HG_CHUNK
    q = jax.nn.silu(xq.astype(f32)).reshape(Bsz, L, HG_HEADS, HG_KDIM)
    f = lb + (1.0 - lb) * jax.nn.sigmoid(xf.astype(f32))
    k = (1.0 - f).reshape(Bsz, L, HG_HEADS, HG_KDIM)
    logf = jnp.log(f).reshape(Bsz, L, HG_HEADS, HG_KDIM)
    v = xi.astype(f32).reshape(Bsz, L, HG_HEADS, HG_VDIM)

    def to_chunks(t):
        return t.reshape(Bsz, nc, HG_CHUNK, HG_HEADS, -1).transpose(1, 0, 3, 2, 4)

    causal = jnp.tril(jnp.ones((HG_CHUNK, HG_CHUNK), dtype=bool))[None, None, :, :, None]

    def step(S, inp):
        qc, kc, gc, vc = inp
        b = jnp.cumsum(gc, axis=2)
        o_inter = jnp.einsum('bhtk,bhkv->bhtv', qc * jnp.exp(b), S)
        diff = b[:, :, :, None, :] - b[:, :, None, :, :]
        decay = jnp.exp(jnp.where(causal, diff, -jnp.inf))
        att = jnp.einsum('bhtk,bhsk,bhtsk->bhts', qc, kc, decay)
        o_intra = jnp.einsum('bhts,bhsv->bhtv', att, vc)
        b_last = b[:, :, -1, :]
        S_new = (jnp.exp(b_last)[..., None] * S
                 + jnp.einsum('bhsk,bhsv->bhkv', kc * jnp.exp(b_last[:, :, None, :] - b), vc))
        return S_new, o_inter + o_intra

    S0 = jnp.zeros((Bsz, HG_HEADS, HG_KDIM, HG_VDIM), f32)
    _, o = lax.scan(step, S0, (to_chunks(q), to_chunks(k), to_chunks(logf), to_chunks(v)))
    o = o.transpose(1, 0, 3, 2, 4).reshape(Bsz, L, HG_HEADS, HG_VDIM)
    o = o * lax.rsqrt(jnp.mean(o * o, axis=-1, keepdims=True) + NORM_EPS)
    o = o * norm_g.astype(f32).reshape(HG_HEADS, HG_VDIM)
    return o.reshape(Bsz, L, HG_WIDTH) * jax.nn.silu(xg.astype(f32))


def _partial_rotary(t, positions):
    half = ROT_DIM // 2
    inv_freq = ROPE_THETA ** (-jnp.arange(half, dtype=jnp.float32) * 2.0 / ROT_DIM)
    ang = positions.astype(jnp.float32)[..., None] * inv_freq
    cos = jnp.cos(ang)[:, :, None, :]
    sin = jnp.sin(ang)[:, :, None, :]
    x1 = t[..., :half]
    x2 = t[..., half:ROT_DIM]
    return jnp.concatenate([x1 * cos - x2 * sin, x2 * cos + x1 * sin, t[..., ROT_DIM:]], axis=-1)


def _dilated_branch(q, k, v, dil, steps):
    Bsz, L, H, E = q.shape
    M = L // dil
    nb = -(-M // ATT_BLOCK)
    Mp = nb * ATT_BLOCK

    def to_blocks(t):
        t = t.reshape(Bsz, M, dil, H, E).transpose(0, 2, 3, 1, 4)
        t = jnp.pad(t, ((0, 0), (0, 0), (0, 0), (0, Mp - M), (0, 0)))
        return t.reshape(Bsz, dil, H, nb, ATT_BLOCK, E)

    def with_prev(t):
        prev = jnp.pad(t[:, :, :, :-1], ((0, 0), (0, 0), (0, 0), (1, 0), (0, 0), (0, 0)))
        return jnp.concatenate([prev, t], axis=-2)

    qb = to_blocks(q) * (E ** -0.5)
    kc = with_prev(to_blocks(k))
    vc = with_prev(to_blocks(v))
    s = jnp.einsum('bdhnqe,bdhnke->bdhnqk', qb, kc)
    qi = jnp.arange(ATT_BLOCK)[:, None] + ATT_BLOCK
    kj = jnp.arange(2 * ATT_BLOCK)[None, :]
    back = qi - kj
    in_range = (jnp.arange(nb)[:, None, None] * ATT_BLOCK - ATT_BLOCK + kj[None]) >= 0
    valid = (back >= 0) & (back <= steps) & in_range
    s = jnp.where(valid, s, -jnp.inf)
    m = jnp.max(s, axis=-1, keepdims=True)
    p = jnp.exp(s - m)
    den = jnp.sum(p, axis=-1, keepdims=True)
    o = jnp.einsum('bdhnqk,bdhnke->bdhnqe', p, vc) / den
    lse = (m + jnp.log(den))[..., 0]
    o = o.reshape(Bsz, dil, H, Mp, E)[:, :, :, :M].transpose(0, 3, 1, 2, 4).reshape(Bsz, L, H, E)
    lse = lse.reshape(Bsz, dil, H, Mp)[..., :M].transpose(0, 3, 1, 2).reshape(Bsz, L, H)
    return o, lse


def _dilated_attention(h, positions, w_qkv, w_o):
    Bsz, L, _ = h.shape
    f32 = jnp.float32
    qkv = (h @ w_qkv).astype(f32).reshape(Bsz, L, 3, N_BRANCH * ATT_HEADS_PER_GROUP, ATT_HEAD_DIM)
    q = _partial_rotary(qkv[:, :, 0], positions)
    k = _partial_rotary(qkv[:, :, 1], positions)
    v = qkv[:, :, 2]
    outs, lses = [], []
    for g, (win, dil) in enumerate(ATT_BRANCHES):
        sl = slice(g * ATT_HEADS_PER_GROUP, (g + 1) * ATT_HEADS_PER_GROUP)
        o_g, lse_g = _dilated_branch(q[:, :, sl], k[:, :, sl], v[:, :, sl], dil, win // dil)
        outs.append(o_g)
        lses.append(lse_g)
    alpha = jax.nn.softmax(jnp.stack(lses, axis=0), axis=0)
    o = jnp.sum(alpha[..., None] * jnp.stack(outs, axis=0), axis=0)
    return o.reshape(Bsz, L, ATT_GROUP_WIDTH).astype(h.dtype) @ w_o


def _conv_ffn(h, w_in, conv_w, conv_b, w_out):
    hu = h @ w_in
    C = hu.shape[-1]
    hu = lax.conv_general_dilated(
        hu, conv_w.astype(hu.dtype)[:, None, :], window_strides=(1,),
        padding=[(CONV_W - 1, 0)], dimension_numbers=('NWC', 'WIO', 'NWC'),
        feature_group_count=C) + conv_b.astype(hu.dtype)
    a, b = hu[..., :D_FF], hu[..., D_FF:]
    return (jax.nn.silu(a) * b) @ w_out


def _fwd_setup_inputs(seed: int = 0) -> dict:
    key = jax.random.key(seed)
    ks = jax.random.split(key, 24)
    f32 = jnp.float32

    def nrm(k, shape, scale):
        return jax.random.normal(k, shape, f32) * scale

    mix_width0 = S5_WIDTH + HG_WIDTH
    n_idx = jnp.arange(S5_STATE, dtype=f32)
    return {
        "x": nrm(ks[0], (BATCH, SEQ, D_MODEL), 1.0),
        "positions": jnp.broadcast_to(jnp.arange(SEQ, dtype=jnp.int32), (BATCH, SEQ)),
        "norm_mix": 1.0 + nrm(ks[1], (DEPTH, D_MODEL), 0.02),
        "norm_ffn": 1.0 + nrm(ks[2], (DEPTH, D_MODEL), 0.02),
        "norm_final": 1.0 + nrm(ks[3], (D_MODEL,), 0.02),
        "mix_w_in": nrm(ks[4], (N_EVEN, D_MODEL, IN0_COLS), D_MODEL ** -0.5),
        "mix_w_out": nrm(ks[5], (N_EVEN, mix_width0, D_MODEL), mix_width0 ** -0.5),
        "s5_A_re": -0.5 + nrm(ks[6], (N_EVEN, S5_GROUPS, S5_STATE), 0.01),
        "s5_A_im": math.pi * n_idx + nrm(ks[7], (N_EVEN, S5_GROUPS, S5_STATE), 0.01),
        "s5_log_dt": jax.random.uniform(ks[8], (N_EVEN, S5_GROUPS), f32,
                                        math.log(S5_DT_MIN), math.log(S5_DT_MAX)),
        "s5_B_re": nrm(ks[9], (N_EVEN, S5_GROUPS, S5_STATE, S5_GROUP), (2 * S5_GROUP) ** -0.5),
        "s5_B_im": nrm(ks[10], (N_EVEN, S5_GROUPS, S5_STATE, S5_GROUP), (2 * S5_GROUP) ** -0.5),
        "s5_C_re": nrm(ks[11], (N_EVEN, S5_GROUPS, S5_GROUP, S5_STATE), (2 * S5_STATE) ** -0.5),
        "s5_C_im": nrm(ks[12], (N_EVEN, S5_GROUPS, S5_GROUP, S5_STATE), (2 * S5_STATE) ** -0.5),
        "s5_D": nrm(ks[13], (N_EVEN, S5_GROUPS, S5_GROUP), 1.0),
        "s5_glu_w": nrm(ks[14], (N_EVEN, S5_WIDTH, S5_WIDTH), S5_WIDTH ** -0.5),
        "s5_glu_b": nrm(ks[15], (N_EVEN, S5_WIDTH), 0.01),
        "hgrn_gamma": nrm(ks[16], (N_EVEN + 1, HG_HEADS * HG_KDIM), 0.1),
        "hgrn_norm": 1.0 + nrm(ks[17], (N_EVEN, HG_WIDTH), 0.02),
        "att_w_qkv": nrm(ks[18], (N_ODD, D_MODEL, IN1_COLS), D_MODEL ** -0.5),
        "att_w_o": nrm(ks[19], (N_ODD, ATT_GROUP_WIDTH, D_MODEL), ATT_GROUP_WIDTH ** -0.5),
        "ffn_w_in": nrm(ks[20], (DEPTH, D_MODEL, 2 * D_FF), D_MODEL ** -0.5),
        "ffn_conv_w": nrm(ks[21], (DEPTH, CONV_W, 2 * D_FF), CONV_W ** -0.5),
        "ffn_conv_b": nrm(ks[22], (DEPTH, 2 * D_FF), 0.01),
        "ffn_w_out": nrm(ks[23], (DEPTH, D_FF, D_MODEL), D_FF ** -0.5),
    }


def _fwd_reference(x, positions, norm_mix, norm_ffn, norm_final, mix_w_in, mix_w_out,
              s5_A_re, s5_A_im, s5_log_dt, s5_B_re, s5_B_im, s5_C_re, s5_C_im, s5_D,
              s5_glu_w, s5_glu_b, hgrn_gamma, hgrn_norm, att_w_qkv, att_w_o,
              ffn_w_in, ffn_conv_w, ffn_conv_b, ffn_w_out):
    lb_all = jnp.cumsum(jax.nn.softmax(hgrn_gamma.astype(jnp.float32), axis=0), axis=0)
    h = x
    c_q = S5_WIDTH
    c_f = c_q + HG_HEADS * HG_KDIM
    c_i = c_f + HG_HEADS * HG_KDIM
    c_g = c_i + HG_WIDTH
    for layer in range(DEPTH):
        hn = _rmsnorm(h, norm_mix[layer])
        j = layer // 2
        if layer % 2 == 0:
            proj = hn @ mix_w_in[j]
            oa = _s5_mixer(proj[..., :c_q], s5_A_re[j], s5_A_im[j], s5_log_dt[j],
                           s5_B_re[j], s5_B_im[j], s5_C_re[j], s5_C_im[j], s5_D[j],
                           s5_glu_w[j], s5_glu_b[j])
            ob = _hgrn2_mixer(proj[..., c_q:c_f], proj[..., c_f:c_i], proj[..., c_i:c_g],
                              proj[..., c_g:], lb_all[j], hgrn_norm[j])
            mix = jnp.concatenate([oa, ob], axis=-1).astype(h.dtype) @ mix_w_out[j]
        else:
            mix = _dilated_attention(hn, positions, att_w_qkv[j], att_w_o[j])
        h = h + mix.astype(h.dtype)
        ff = _conv_ffn(_rmsnorm(h, norm_ffn[layer]), ffn_w_in[layer], ffn_conv_w[layer],
                       ffn_conv_b[layer], ffn_w_out[layer])
        h = h + ff.astype(h.dtype)
    return _rmsnorm(h, norm_final)


import jax as _jax
import jax.numpy as _jnp

TWIN_FORMAT = 'train_step'
FWD_PARAMS = ['x', 'positions', 'norm_mix', 'norm_ffn', 'norm_final', 'mix_w_in', 'mix_w_out', 's5_A_re', 's5_A_im', 's5_log_dt', 's5_B_re', 's5_B_im', 's5_C_re', 's5_C_im', 's5_D', 's5_glu_w', 's5_glu_b', 'hgrn_gamma', 'hgrn_norm', 'att_w_qkv', 'att_w_o', 'ffn_w_in', 'ffn_conv_w', 'ffn_conv_b', 'ffn_w_out']
TWIN_WEIGHTS = ['norm_mix', 'norm_ffn', 'norm_final', 'mix_w_in', 'mix_w_out', 's5_A_re', 's5_A_im', 's5_log_dt', 's5_B_re', 's5_B_im', 's5_C_re', 's5_C_im', 's5_D', 's5_glu_w', 's5_glu_b', 'hgrn_gamma', 'hgrn_norm', 'att_w_qkv', 'att_w_o', 'ffn_w_in', 'ffn_conv_w', 'ffn_conv_b', 'ffn_w_out']
TWIN_DIFF_INPUT = 'x'
TWIN_INPUTS = ['x', 'positions', 'norm_mix', 'norm_ffn', 'norm_final', 'mix_w_in', 'mix_w_out', 's5_A_re', 's5_A_im', 's5_log_dt', 's5_B_re', 's5_B_im', 's5_C_re', 's5_C_im', 's5_D', 's5_glu_w', 's5_glu_b', 'hgrn_gamma', 'hgrn_norm', 'att_w_qkv', 'att_w_o', 'ffn_w_in', 'ffn_conv_w', 'ffn_conv_b', 'ffn_w_out', 'loss_target', 'm_norm_mix', 'm_norm_ffn', 'm_norm_final', 'm_mix_w_in', 'm_mix_w_out', 'm_s5_A_re', 'm_s5_A_im', 'm_s5_log_dt', 'm_s5_B_re', 'm_s5_B_im', 'm_s5_C_re', 'm_s5_C_im', 'm_s5_D', 'm_s5_glu_w', 'm_s5_glu_b', 'm_hgrn_gamma', 'm_hgrn_norm', 'm_att_w_qkv', 'm_att_w_o', 'm_ffn_w_in', 'm_ffn_conv_w', 'm_ffn_conv_b', 'm_ffn_w_out', 'v_norm_mix', 'v_norm_ffn', 'v_norm_final', 'v_mix_w_in', 'v_mix_w_out', 'v_s5_A_re', 'v_s5_A_im', 'v_s5_log_dt', 'v_s5_B_re', 'v_s5_B_im', 'v_s5_C_re', 'v_s5_C_im', 'v_s5_D', 'v_s5_glu_w', 'v_s5_glu_b', 'v_hgrn_gamma', 'v_hgrn_norm', 'v_att_w_qkv', 'v_att_w_o', 'v_ffn_w_in', 'v_ffn_conv_w', 'v_ffn_conv_b', 'v_ffn_w_out']
TWIN_OUTPUTS = ['loss', 'grad_x', 'grad_norm_mix', 'grad_norm_ffn', 'grad_norm_final', 'grad_mix_w_in', 'grad_mix_w_out', 'grad_s5_A_re', 'grad_s5_A_im', 'grad_s5_log_dt', 'grad_s5_B_re', 'grad_s5_B_im', 'grad_s5_C_re', 'grad_s5_C_im', 'grad_s5_D', 'grad_s5_glu_w', 'grad_s5_glu_b', 'grad_hgrn_gamma', 'grad_hgrn_norm', 'grad_att_w_qkv', 'grad_att_w_o', 'grad_ffn_w_in', 'grad_ffn_conv_w', 'grad_ffn_conv_b', 'grad_ffn_w_out', 'delta_norm_mix', 'delta_norm_ffn', 'delta_norm_final', 'delta_mix_w_in', 'delta_mix_w_out', 'delta_s5_A_re', 'delta_s5_A_im', 'delta_s5_log_dt', 'delta_s5_B_re', 'delta_s5_B_im', 'delta_s5_C_re', 'delta_s5_C_im', 'delta_s5_D', 'delta_s5_glu_w', 'delta_s5_glu_b', 'delta_hgrn_gamma', 'delta_hgrn_norm', 'delta_att_w_qkv', 'delta_att_w_o', 'delta_ffn_w_in', 'delta_ffn_conv_w', 'delta_ffn_conv_b', 'delta_ffn_w_out', 'new_m_norm_mix', 'new_m_norm_ffn', 'new_m_norm_final', 'new_m_mix_w_in', 'new_m_mix_w_out', 'new_m_s5_A_re', 'new_m_s5_A_im', 'new_m_s5_log_dt', 'new_m_s5_B_re', 'new_m_s5_B_im', 'new_m_s5_C_re', 'new_m_s5_C_im', 'new_m_s5_D', 'new_m_s5_glu_w', 'new_m_s5_glu_b', 'new_m_hgrn_gamma', 'new_m_hgrn_norm', 'new_m_att_w_qkv', 'new_m_att_w_o', 'new_m_ffn_w_in', 'new_m_ffn_conv_w', 'new_m_ffn_conv_b', 'new_m_ffn_w_out', 'new_v_norm_mix', 'new_v_norm_ffn', 'new_v_norm_final', 'new_v_mix_w_in', 'new_v_mix_w_out', 'new_v_s5_A_re', 'new_v_s5_A_im', 'new_v_s5_log_dt', 'new_v_s5_B_re', 'new_v_s5_B_im', 'new_v_s5_C_re', 'new_v_s5_C_im', 'new_v_s5_D', 'new_v_s5_glu_w', 'new_v_s5_glu_b', 'new_v_hgrn_gamma', 'new_v_hgrn_norm', 'new_v_att_w_qkv', 'new_v_att_w_o', 'new_v_ffn_w_in', 'new_v_ffn_conv_w', 'new_v_ffn_conv_b', 'new_v_ffn_w_out']
TWIN_LEAF_KINDS = {'loss': 'loss', 'grad_x': 'grad_x', 'grad_norm_mix': 'grad_w', 'grad_norm_ffn': 'grad_w', 'grad_norm_final': 'grad_w', 'grad_mix_w_in': 'grad_w', 'grad_mix_w_out': 'grad_w', 'grad_s5_A_re': 'grad_w', 'grad_s5_A_im': 'grad_w', 'grad_s5_log_dt': 'grad_w', 'grad_s5_B_re': 'grad_w', 'grad_s5_B_im': 'grad_w', 'grad_s5_C_re': 'grad_w', 'grad_s5_C_im': 'grad_w', 'grad_s5_D': 'grad_w', 'grad_s5_glu_w': 'grad_w', 'grad_s5_glu_b': 'grad_w', 'grad_hgrn_gamma': 'grad_w', 'grad_hgrn_norm': 'grad_w', 'grad_att_w_qkv': 'grad_w', 'grad_att_w_o': 'grad_w', 'grad_ffn_w_in': 'grad_w', 'grad_ffn_conv_w': 'grad_w', 'grad_ffn_conv_b': 'grad_w', 'grad_ffn_w_out': 'grad_w', 'delta_norm_mix': 'delta_w', 'delta_norm_ffn': 'delta_w', 'delta_norm_final': 'delta_w', 'delta_mix_w_in': 'delta_w', 'delta_mix_w_out': 'delta_w', 'delta_s5_A_re': 'delta_w', 'delta_s5_A_im': 'delta_w', 'delta_s5_log_dt': 'delta_w', 'delta_s5_B_re': 'delta_w', 'delta_s5_B_im': 'delta_w', 'delta_s5_C_re': 'delta_w', 'delta_s5_C_im': 'delta_w', 'delta_s5_D': 'delta_w', 'delta_s5_glu_w': 'delta_w', 'delta_s5_glu_b': 'delta_w', 'delta_hgrn_gamma': 'delta_w', 'delta_hgrn_norm': 'delta_w', 'delta_att_w_qkv': 'delta_w', 'delta_att_w_o': 'delta_w', 'delta_ffn_w_in': 'delta_w', 'delta_ffn_conv_w': 'delta_w', 'delta_ffn_conv_b': 'delta_w', 'delta_ffn_w_out': 'delta_w', 'new_m_norm_mix': 'new_m', 'new_m_norm_ffn': 'new_m', 'new_m_norm_final': 'new_m', 'new_m_mix_w_in': 'new_m', 'new_m_mix_w_out': 'new_m', 'new_m_s5_A_re': 'new_m', 'new_m_s5_A_im': 'new_m', 'new_m_s5_log_dt': 'new_m', 'new_m_s5_B_re': 'new_m', 'new_m_s5_B_im': 'new_m', 'new_m_s5_C_re': 'new_m', 'new_m_s5_C_im': 'new_m', 'new_m_s5_D': 'new_m', 'new_m_s5_glu_w': 'new_m', 'new_m_s5_glu_b': 'new_m', 'new_m_hgrn_gamma': 'new_m', 'new_m_hgrn_norm': 'new_m', 'new_m_att_w_qkv': 'new_m', 'new_m_att_w_o': 'new_m', 'new_m_ffn_w_in': 'new_m', 'new_m_ffn_conv_w': 'new_m', 'new_m_ffn_conv_b': 'new_m', 'new_m_ffn_w_out': 'new_m', 'new_v_norm_mix': 'new_v', 'new_v_norm_ffn': 'new_v', 'new_v_norm_final': 'new_v', 'new_v_mix_w_in': 'new_v', 'new_v_mix_w_out': 'new_v', 'new_v_s5_A_re': 'new_v', 'new_v_s5_A_im': 'new_v', 'new_v_s5_log_dt': 'new_v', 'new_v_s5_B_re': 'new_v', 'new_v_s5_B_im': 'new_v', 'new_v_s5_C_re': 'new_v', 'new_v_s5_C_im': 'new_v', 'new_v_s5_D': 'new_v', 'new_v_s5_glu_w': 'new_v', 'new_v_s5_glu_b': 'new_v', 'new_v_hgrn_gamma': 'new_v', 'new_v_hgrn_norm': 'new_v', 'new_v_att_w_qkv': 'new_v', 'new_v_att_w_o': 'new_v', 'new_v_ffn_w_in': 'new_v', 'new_v_ffn_conv_w': 'new_v', 'new_v_ffn_conv_b': 'new_v', 'new_v_ffn_w_out': 'new_v'}


def _forward(args):
    return _fwd_reference(*[args[k] for k in FWD_PARAMS])


def _output_shape():
    out = _jax.eval_shape(lambda: _forward(_fwd_setup_inputs(0)))
    return out.shape, out.dtype

N_MICROBATCH = 1
ADAM_LR = 0.001
ADAM_B1 = 0.9
ADAM_B2 = 0.999
ADAM_EPS = 1e-08
ADAM_WD = 0.01
ADAM_STEP = 10
PER_EXAMPLE_BATCH_AXIS = {'x': 0, 'positions': 0, 'loss_target': 0}
SHARED_INPUTS = []
_WEIGHT_DTYPES = {'norm_mix': _jnp.float32, 'norm_ffn': _jnp.float32, 'norm_final': _jnp.float32, 'mix_w_in': _jnp.float32, 'mix_w_out': _jnp.float32, 's5_A_re': _jnp.float32, 's5_A_im': _jnp.float32, 's5_log_dt': _jnp.float32, 's5_B_re': _jnp.float32, 's5_B_im': _jnp.float32, 's5_C_re': _jnp.float32, 's5_C_im': _jnp.float32, 's5_D': _jnp.float32, 's5_glu_w': _jnp.float32, 's5_glu_b': _jnp.float32, 'hgrn_gamma': _jnp.float32, 'hgrn_norm': _jnp.float32, 'att_w_qkv': _jnp.float32, 'att_w_o': _jnp.float32, 'ffn_w_in': _jnp.float32, 'ffn_conv_w': _jnp.float32, 'ffn_conv_b': _jnp.float32, 'ffn_w_out': _jnp.float32}
MOMENT_SCALE = {'norm_mix': 6.996192e-02, 'norm_ffn': 8.912160e-02, 'norm_final': 1.598155e+01, 'mix_w_in': 5.864978e-02, 'mix_w_out': 6.887799e-02, 's5_A_re': 2.803171e-03, 's5_A_im': 2.612980e-03, 's5_log_dt': 2.519027e+00, 's5_B_re': 1.553175e-03, 's5_B_im': 1.534206e-03, 's5_C_re': 3.088229e-03, 's5_C_im': 3.065340e-03, 's5_D': 5.429342e-02, 's5_glu_w': 1.494778e-02, 's5_glu_b': 2.164710e-02, 'hgrn_gamma': 7.603522e-03, 'hgrn_norm': 8.329200e-02, 'att_w_qkv': 1.543657e-02, 'att_w_o': 2.145973e-02, 'ffn_w_in': 3.733512e-02, 'ffn_conv_w': 3.726652e-02, 'ffn_conv_b': 3.679469e-02, 'ffn_w_out': 6.092053e-02}


def _to_microbatches(a, axis):
    t = _jnp.moveaxis(a, axis, 0)
    t = t.reshape((N_MICROBATCH, t.shape[0] // N_MICROBATCH) + t.shape[1:])
    return _jnp.moveaxis(t, 1, axis + 1)


def setup_inputs(seed: int = 0) -> dict:
    inp = _fwd_setup_inputs(seed)
    key = _jax.random.fold_in(_jax.random.key(seed), 7919)
    shape, _ = _output_shape()
    out = dict(inp)
    out["loss_target"] = _jax.random.normal(_jax.random.fold_in(key, 0), shape, _jnp.float32)
    for i, name in enumerate(TWIN_WEIGHTS):
        w = inp[name].astype(_jnp.float32)
        if MOMENT_SCALE is None:
            s = _jnp.sqrt(_jnp.mean(_jnp.square(w)) + 1e-30)
        else:
            s = MOMENT_SCALE[name]
        km, kv = _jax.random.split(_jax.random.fold_in(key, i + 1))
        out[name] = w
        out["m_" + name] = s * _jax.random.normal(km, w.shape, _jnp.float32)
        out["v_" + name] = (s * s) * _jax.random.uniform(kv, w.shape, _jnp.float32, 0.5, 1.5)
    if N_MICROBATCH > 1:
        for name, axis in PER_EXAMPLE_BATCH_AXIS.items():
            out[name] = _to_microbatches(out[name], axis)
    return {'x': out['x'], 'positions': out['positions'], 'norm_mix': out['norm_mix'], 'norm_ffn': out['norm_ffn'], 'norm_final': out['norm_final'], 'mix_w_in': out['mix_w_in'], 'mix_w_out': out['mix_w_out'], 's5_A_re': out['s5_A_re'], 's5_A_im': out['s5_A_im'], 's5_log_dt': out['s5_log_dt'], 's5_B_re': out['s5_B_re'], 's5_B_im': out['s5_B_im'], 's5_C_re': out['s5_C_re'], 's5_C_im': out['s5_C_im'], 's5_D': out['s5_D'], 's5_glu_w': out['s5_glu_w'], 's5_glu_b': out['s5_glu_b'], 'hgrn_gamma': out['hgrn_gamma'], 'hgrn_norm': out['hgrn_norm'], 'att_w_qkv': out['att_w_qkv'], 'att_w_o': out['att_w_o'], 'ffn_w_in': out['ffn_w_in'], 'ffn_conv_w': out['ffn_conv_w'], 'ffn_conv_b': out['ffn_conv_b'], 'ffn_w_out': out['ffn_w_out'], 'loss_target': out['loss_target'], 'm_norm_mix': out['m_norm_mix'], 'm_norm_ffn': out['m_norm_ffn'], 'm_norm_final': out['m_norm_final'], 'm_mix_w_in': out['m_mix_w_in'], 'm_mix_w_out': out['m_mix_w_out'], 'm_s5_A_re': out['m_s5_A_re'], 'm_s5_A_im': out['m_s5_A_im'], 'm_s5_log_dt': out['m_s5_log_dt'], 'm_s5_B_re': out['m_s5_B_re'], 'm_s5_B_im': out['m_s5_B_im'], 'm_s5_C_re': out['m_s5_C_re'], 'm_s5_C_im': out['m_s5_C_im'], 'm_s5_D': out['m_s5_D'], 'm_s5_glu_w': out['m_s5_glu_w'], 'm_s5_glu_b': out['m_s5_glu_b'], 'm_hgrn_gamma': out['m_hgrn_gamma'], 'm_hgrn_norm': out['m_hgrn_norm'], 'm_att_w_qkv': out['m_att_w_qkv'], 'm_att_w_o': out['m_att_w_o'], 'm_ffn_w_in': out['m_ffn_w_in'], 'm_ffn_conv_w': out['m_ffn_conv_w'], 'm_ffn_conv_b': out['m_ffn_conv_b'], 'm_ffn_w_out': out['m_ffn_w_out'], 'v_norm_mix': out['v_norm_mix'], 'v_norm_ffn': out['v_norm_ffn'], 'v_norm_final': out['v_norm_final'], 'v_mix_w_in': out['v_mix_w_in'], 'v_mix_w_out': out['v_mix_w_out'], 'v_s5_A_re': out['v_s5_A_re'], 'v_s5_A_im': out['v_s5_A_im'], 'v_s5_log_dt': out['v_s5_log_dt'], 'v_s5_B_re': out['v_s5_B_re'], 'v_s5_B_im': out['v_s5_B_im'], 'v_s5_C_re': out['v_s5_C_re'], 'v_s5_C_im': out['v_s5_C_im'], 'v_s5_D': out['v_s5_D'], 'v_s5_glu_w': out['v_s5_glu_w'], 'v_s5_glu_b': out['v_s5_glu_b'], 'v_hgrn_gamma': out['v_hgrn_gamma'], 'v_hgrn_norm': out['v_hgrn_norm'], 'v_att_w_qkv': out['v_att_w_qkv'], 'v_att_w_o': out['v_att_w_o'], 'v_ffn_w_in': out['v_ffn_w_in'], 'v_ffn_conv_w': out['v_ffn_conv_w'], 'v_ffn_conv_b': out['v_ffn_conv_b'], 'v_ffn_w_out': out['v_ffn_w_out']}


def _loss(weights, diff, rest, loss_target):
    with _jax.named_scope("forward"):
        args = {**rest, TWIN_DIFF_INPUT: diff, **{k: w.astype(_WEIGHT_DTYPES[k]) for k, w in weights.items()}}
        y = _forward(args)
    with _jax.named_scope("loss_head"):
        err = _jnp.square(y.astype(_jnp.float32) - loss_target)
        return 0.5 * _jnp.sum(_jnp.mean(err, axis=-1)) if err.ndim else 0.5 * err


def _adamw(w, g, m, v):
    m = ADAM_B1 * m + (1.0 - ADAM_B1) * g
    v = ADAM_B2 * v + (1.0 - ADAM_B2) * _jnp.square(g)
    m_hat = m / (1.0 - ADAM_B1 ** ADAM_STEP)
    v_hat = v / (1.0 - ADAM_B2 ** ADAM_STEP)
    delta = -ADAM_LR * (m_hat / (_jnp.sqrt(v_hat) + ADAM_EPS) + ADAM_WD * w)
    return delta, m, v


def reference(x, positions, norm_mix, norm_ffn, norm_final, mix_w_in, mix_w_out, s5_A_re, s5_A_im, s5_log_dt, s5_B_re, s5_B_im, s5_C_re, s5_C_im, s5_D, s5_glu_w, s5_glu_b, hgrn_gamma, hgrn_norm, att_w_qkv, att_w_o, ffn_w_in, ffn_conv_w, ffn_conv_b, ffn_w_out, loss_target, m_norm_mix, m_norm_ffn, m_norm_final, m_mix_w_in, m_mix_w_out, m_s5_A_re, m_s5_A_im, m_s5_log_dt, m_s5_B_re, m_s5_B_im, m_s5_C_re, m_s5_C_im, m_s5_D, m_s5_glu_w, m_s5_glu_b, m_hgrn_gamma, m_hgrn_norm, m_att_w_qkv, m_att_w_o, m_ffn_w_in, m_ffn_conv_w, m_ffn_conv_b, m_ffn_w_out, v_norm_mix, v_norm_ffn, v_norm_final, v_mix_w_in, v_mix_w_out, v_s5_A_re, v_s5_A_im, v_s5_log_dt, v_s5_B_re, v_s5_B_im, v_s5_C_re, v_s5_C_im, v_s5_D, v_s5_glu_w, v_s5_glu_b, v_hgrn_gamma, v_hgrn_norm, v_att_w_qkv, v_att_w_o, v_ffn_w_in, v_ffn_conv_w, v_ffn_conv_b, v_ffn_w_out):
    given = dict(x=x, positions=positions, norm_mix=norm_mix, norm_ffn=norm_ffn, norm_final=norm_final, mix_w_in=mix_w_in, mix_w_out=mix_w_out, s5_A_re=s5_A_re, s5_A_im=s5_A_im, s5_log_dt=s5_log_dt, s5_B_re=s5_B_re, s5_B_im=s5_B_im, s5_C_re=s5_C_re, s5_C_im=s5_C_im, s5_D=s5_D, s5_glu_w=s5_glu_w, s5_glu_b=s5_glu_b, hgrn_gamma=hgrn_gamma, hgrn_norm=hgrn_norm, att_w_qkv=att_w_qkv, att_w_o=att_w_o, ffn_w_in=ffn_w_in, ffn_conv_w=ffn_conv_w, ffn_conv_b=ffn_conv_b, ffn_w_out=ffn_w_out, loss_target=loss_target, m_norm_mix=m_norm_mix, m_norm_ffn=m_norm_ffn, m_norm_final=m_norm_final, m_mix_w_in=m_mix_w_in, m_mix_w_out=m_mix_w_out, m_s5_A_re=m_s5_A_re, m_s5_A_im=m_s5_A_im, m_s5_log_dt=m_s5_log_dt, m_s5_B_re=m_s5_B_re, m_s5_B_im=m_s5_B_im, m_s5_C_re=m_s5_C_re, m_s5_C_im=m_s5_C_im, m_s5_D=m_s5_D, m_s5_glu_w=m_s5_glu_w, m_s5_glu_b=m_s5_glu_b, m_hgrn_gamma=m_hgrn_gamma, m_hgrn_norm=m_hgrn_norm, m_att_w_qkv=m_att_w_qkv, m_att_w_o=m_att_w_o, m_ffn_w_in=m_ffn_w_in, m_ffn_conv_w=m_ffn_conv_w, m_ffn_conv_b=m_ffn_conv_b, m_ffn_w_out=m_ffn_w_out, v_norm_mix=v_norm_mix, v_norm_ffn=v_norm_ffn, v_norm_final=v_norm_final, v_mix_w_in=v_mix_w_in, v_mix_w_out=v_mix_w_out, v_s5_A_re=v_s5_A_re, v_s5_A_im=v_s5_A_im, v_s5_log_dt=v_s5_log_dt, v_s5_B_re=v_s5_B_re, v_s5_B_im=v_s5_B_im, v_s5_C_re=v_s5_C_re, v_s5_C_im=v_s5_C_im, v_s5_D=v_s5_D, v_s5_glu_w=v_s5_glu_w, v_s5_glu_b=v_s5_glu_b, v_hgrn_gamma=v_hgrn_gamma, v_hgrn_norm=v_hgrn_norm, v_att_w_qkv=v_att_w_qkv, v_att_w_o=v_att_w_o, v_ffn_w_in=v_ffn_w_in, v_ffn_conv_w=v_ffn_conv_w, v_ffn_conv_b=v_ffn_conv_b, v_ffn_w_out=v_ffn_w_out)
    weights = {n: given[n] for n in TWIN_WEIGHTS}
    shared = {n: given[n] for n in SHARED_INPUTS}
    per_example = {n: given[n] for n in ['x', 'positions']}
    grad_fn = _jax.value_and_grad(_loss, argnums=(0, 1))

    def one_microbatch(ex, loss_target):
        ex = dict(ex)
        diff = ex.pop(TWIN_DIFF_INPUT)
        return grad_fn(weights, diff, {**shared, **ex}, loss_target)

    if N_MICROBATCH == 1:
        loss, (grad_w, grad_x) = one_microbatch(per_example, given["loss_target"])
    else:
        def body(carry, xs):
            loss_sum, grad_sum = carry
            l_k, (gw_k, gx_k) = one_microbatch(xs[0], xs[1])
            with _jax.named_scope("update"):
                return (loss_sum + l_k, _jax.tree.map(_jnp.add, grad_sum, gw_k)), gx_k

        init = (_jnp.zeros((), _jnp.float32), _jax.tree.map(_jnp.zeros_like, weights))
        (loss, grad_w), grad_x = _jax.lax.scan(body, init, (per_example, given["loss_target"]))
    with _jax.named_scope("update"):
        delta_w, new_m, new_v = {}, {}, {}
        for n in TWIN_WEIGHTS:
            delta_w[n], new_m[n], new_v[n] = _adamw(weights[n], grad_w[n], given["m_" + n], given["v_" + n])
    return (loss, grad_x, *[grad_w[n] for n in TWIN_WEIGHTS], *[delta_w[n] for n in TWIN_WEIGHTS],
            *[new_m[n] for n in TWIN_WEIGHTS], *[new_v[n] for n in TWIN_WEIGHTS])
```

```python
import functools
import math

import numpy as np
import jax
import jax.numpy as jnp
from jax import lax
from jax.experimental import pallas as pl
from jax.experimental.pallas import tpu as pltpu

f32 = jnp.float32
BF = jnp.bfloat16
HI = lax.Precision.HIGHEST
S = jax.ShapeDtypeStruct
MESH = pl.DeviceIdType.MESH

L = 2048
D = 1024
EPS = 1e-6
S5W = 512
NST = 2048
HGC = 64
DFF = 2816
ROPE_THETA = 500000.0
LR, B1, B2, AEPS, WD, STEP = 0.001, 0.9, 0.999, 1e-08, 0.01, 10
VMEM_LIMIT = 56 * 1024 * 1024


def _cp(sem=None):
    return pltpu.CompilerParams(dimension_semantics=sem, vmem_limit_bytes=VMEM_LIMIT)


def _dg(a, b, ca, cb):
    return lax.dot_general(a.astype(BF), b.astype(BF), (((ca,), (cb,)), ((), ())), preferred_element_type=f32)


@jax.custom_vjp
def dot_nn(a, b):
    return _dg(a, b, 1, 0)


@jax.custom_vjp
def dot_nt(a, b):
    return _dg(a, b, 1, 1)


@jax.custom_vjp
def dot_tn(a, b):
    return _dg(a, b, 0, 0)


dot_nn.defvjp(lambda a, b: (dot_nn(a, b), (a, b)),
              lambda r, g: (dot_nt(g, r[1]).astype(r[0].dtype), dot_tn(r[0], g).astype(r[1].dtype)))
dot_nt.defvjp(lambda a, b: (dot_nt(a, b), (a, b)),
              lambda r, g: (dot_nn(g, r[1]).astype(r[0].dtype), dot_tn(g, r[0]).astype(r[1].dtype)))
dot_tn.defvjp(lambda a, b: (dot_tn(a, b), (a, b)),
              lambda r, g: (dot_nt(r[1], g).astype(r[0].dtype), dot_nn(r[0], g).astype(r[1].dtype)))


def matmul(a, b, *, mode, tm, tn, tk, out_dtype=f32, add=None, b_lead=None, name):
    if mode == "nn":
        (M, K), N = a.shape, b.shape[-1]
        a_spec = pl.BlockSpec((tm, tk), lambda i, j, k: (i, k))
        b_blk, b_idx, ca, cb = (tk, tn), (lambda i, j, k: (k, j)), 1, 0
    elif mode == "nt":
        (M, K), N = a.shape, b.shape[-2]
        a_spec = pl.BlockSpec((tm, tk), lambda i, j, k: (i, k))
        b_blk, b_idx, ca, cb = (tn, tk), (lambda i, j, k: (j, k)), 1, 1
    else:
        (K, M), N = a.shape, b.shape[-1]
        a_spec = pl.BlockSpec((tk, tm), lambda i, j, k: (k, i))
        b_blk, b_idx, ca, cb = (tk, tn), (lambda i, j, k: (k, j)), 0, 0
    assert M % tm == 0 and N % tn == 0 and K % tk == 0, (name, M, N, K, tm, tn, tk)
    if b_lead is None:
        b_spec = pl.BlockSpec(b_blk, b_idx)
    else:
        b_spec = pl.BlockSpec((None,) + b_blk, lambda i, j, k: (b_lead,) + b_idx(i, j, k))
    nk = K // tk
    has_add = add is not None

    def body(*refs):
        a_ref, b_ref = refs[0], refs[1]
        add_ref = refs[2] if has_add else None
        o_ref = refs[2 + has_add]
        p = _dg(a_ref[...], b_ref[...], ca, cb)

        def fin(v):
            if has_add:
                v = v + add_ref[...].astype(f32)
            o_ref[...] = v.astype(o_ref.dtype)

        if nk == 1:
            fin(p)
        else:
            acc = refs[3 + has_add]
            k = pl.program_id(2)

            @pl.when(k == 0)
            def _():
                acc[...] = p

            @pl.when(k > 0)
            def _():
                acc[...] += p

            @pl.when(k == nk - 1)
            def _():
                fin(acc[...])

    in_specs = [a_spec, b_spec]
    args = [a, b]
    if has_add:
        in_specs.append(pl.BlockSpec((tm, tn), lambda i, j, k: (i, j)))
        args.append(add)
    return pl.pallas_call(
        body, grid=(M // tm, N // tn, nk), in_specs=in_specs,
        out_specs=pl.BlockSpec((tm, tn), lambda i, j, k: (i, j)),
        out_shape=S((M, N), out_dtype),
        scratch_shapes=[pltpu.VMEM((tm, tn), f32)] if nk > 1 else [],
        compiler_params=_cp(("parallel", "parallel", "arbitrary")), name=name)(*args)


def _rms(xv, gv):
    return xv * lax.rsqrt(jnp.mean(xv * xv, axis=-1, keepdims=True) + EPS) * gv


TR = 256


def rms_fwd(x, g, name):
    def body(x_ref, g_ref, o_ref):
        o_ref[...] = _rms(x_ref[...], g_ref[...]).astype(o_ref.dtype)

    return pl.pallas_call(
        body, grid=(L // TR,),
        in_specs=[pl.BlockSpec((TR, D), lambda i: (i, 0)), pl.BlockSpec((1, D), lambda i: (0, 0))],
        out_specs=pl.BlockSpec((TR, D), lambda i: (i, 0)), out_shape=S((L, D), BF),
        compiler_params=_cp(("parallel",)), name=name)(x, g)


def rms_bwd(x, g, dy, dres, name):
    def body(x_ref, g_ref, dy_ref, dr_ref, dh_ref, dg_ref):
        _, vjp = jax.vjp(_rms, x_ref[...], g_ref[...])
        dx, dg = vjp(dy_ref[...].astype(f32))
        dh_ref[...] = dr_ref[...] + dx

        @pl.when(pl.program_id(0) == 0)
        def _():
            dg_ref[...] = jnp.zeros_like(dg_ref)

        dg_ref[...] += dg

    row = pl.BlockSpec((TR, D), lambda i: (i, 0))
    vec = pl.BlockSpec((1, D), lambda i: (0, 0))
    return pl.pallas_call(
        body, grid=(L // TR,), in_specs=[row, vec, row, row], out_specs=[row, vec],
        out_shape=[S((L, D), f32), S((1, D), f32)],
        compiler_params=_cp(("arbitrary",)), name=name)(x, g, dy, dres)


def loss_head(h, g, tgt):
    def f(hv, gv, tv):
        y = _rms(hv, gv)
        return 0.5 * jnp.sum(jnp.mean(jnp.square(y - tv), axis=-1))

    def body(h_ref, g_ref, t_ref, l_ref, dh_ref, dg_ref):
        val, vjp = jax.vjp(f, h_ref[...], g_ref[...], t_ref[...])
        dh, dg, _ = vjp(jnp.ones((), f32))
        dh_ref[...] = dh

        @pl.when(pl.program_id(0) == 0)
        def _():
            dg_ref[...] = jnp.zeros_like(dg_ref)
            l_ref[...] = jnp.zeros_like(l_ref)

        dg_ref[...] += dg
        l_ref[...] += jnp.full((1, 128), val, f32)

    row = pl.BlockSpec((TR, D), lambda i: (i, 0))
    vec = pl.BlockSpec((1, D), lambda i: (0, 0))
    return pl.pallas_call(
        body, grid=(L // TR,), in_specs=[row, vec, row],
        out_specs=[pl.BlockSpec((1, 128), lambda i: (0, 0)), row, vec],
        out_shape=[S((1, 128), f32), S((L, D), f32), S((1, D), f32)],
        compiler_params=_cp(("arbitrary",)), name="loss_head")(h, g, tgt)


def _col_to_row(c):
    n = c.shape[0]
    t = jnp.broadcast_to(c, (n, 128)).T
    r = lax.broadcasted_iota(jnp.int32, (128, n), 0)
    return jnp.sum(jnp.where(r == 0, t, 0.0), axis=0, keepdims=True)


def _s5_param_map(are, aim, ldt_row, bre, bim, cre, cim):
    n = NST
    gi = lax.broadcasted_iota(jnp.int32, (n, 32), 0) // 64
    gj = lax.broadcasted_iota(jnp.int32, (n, 32), 1)
    ldt = jnp.sum(jnp.where(gi == gj, ldt_row, 0.0), axis=1, keepdims=True)
    dt = jnp.exp(ldt)
    mag = jnp.exp(are * dt)
    abr = mag * jnp.cos(aim * dt)
    abi = mag * jnp.sin(aim * dt)
    den = are * are + aim * aim
    nr, ni = abr - 1.0, abi
    cr = (nr * are + ni * aim) / den
    ci = (ni * are - nr * aim) / den
    bbr = cr * bre - ci * bim
    bbi = cr * bim + ci * bre
    tc = lax.broadcasted_iota(jnp.int32, (16, 128), 0)
    tl = lax.broadcasted_iota(jnp.int32, (16, 128), 1)
    T = (tl % 16 == tc).astype(f32)
    mr = (lax.broadcasted_iota(jnp.int32, (n, 128), 0) // 64) % 8
    mc = lax.broadcasted_iota(jnp.int32, (n, 128), 1) // 16
    mask = (mr == mc).astype(f32)

    def expand(v):
        return jnp.dot(v, T, precision=HI, preferred_element_type=f32) * mask

    return expand(bbr), expand(bbi), expand(cre), expand(cim), _col_to_row(abr), _col_to_row(abi)


def s5_params_fwd(are, aim, ldt_row, bre, bim, cre, cim):
    def body(*refs):
        outs = _s5_param_map(*[r[...] for r in refs[:7]])
        for o_ref, o in zip(refs[7:], outs):
            o_ref[...] = o

    return pl.pallas_call(
        body, out_shape=[S((NST, 128), f32)] * 4 + [S((1, NST), f32)] * 2,
        compiler_params=_cp(), name="s5_params_fwd")(are, aim, ldt_row, bre, bim, cre, cim)


def s5_params_bwd(are, aim, ldt_row, bre, bim, cre, cim, cots):
    def body(*refs):
        _, vjp = jax.vjp(_s5_param_map, *[r[...] for r in refs[:7]])
        gs = vjp(tuple(r[...] for r in refs[7:13]))
        for o_ref, o in zip(refs[13:], gs):
            o_ref[...] = o

    return pl.pallas_call(
        body, out_shape=[S((NST, 1), f32)] * 2 + [S((1, 32), f32)] + [S((NST, 16), f32)] * 4,
        compiler_params=_cp(), name="s5_params_bwd")(are, aim, ldt_row, bre, bim, cre, cim, *cots)


NT5 = 4
RC = 256


def s5_scan_fwd(proj, wbr, wbi, wcr, wci, abr, abi, drow):
    def body(u_ref, wbr_ref, wbi_ref, wcr_ref, wci_ref, ar_ref, ai_ref, d_ref, xr_ref, xi_ref, y_ref):
        wbr_v, wbi_v = wbr_ref[...], wbi_ref[...]
        for r in range(L // RC):
            rows = pl.ds(r * RC, RC)
            ub = u_ref[rows, :]
            xr_ref[rows, :] = dot_nt(ub, wbr_v)
            xi_ref[rows, :] = dot_nt(ub, wbi_v)
        ar, ai = ar_ref[...], ai_ref[...]

        def step(t, c):
            cr, ci = c
            nr = ar * cr - ai * ci + xr_ref[pl.ds(t, 1), :]
            ni = ar * ci + ai * cr + xi_ref[pl.ds(t, 1), :]
            xr_ref[pl.ds(t, 1), :] = nr
            xi_ref[pl.ds(t, 1), :] = ni
            return nr, ni

        z = jnp.zeros((1, 512), f32)
        lax.fori_loop(0, L, step, (z, z), unroll=8)
        wcr_v, wci_v, dv = wcr_ref[...], wci_ref[...], d_ref[...]
        for r in range(L // RC):
            rows = pl.ds(r * RC, RC)
            y_ref[rows, :] = (dot_nn(xr_ref[rows, :], wcr_v) - dot_nn(xi_ref[rows, :], wci_v)
                              + dv * u_ref[rows, :])

    wspec = pl.BlockSpec((512, 128), lambda j: (j, 0))
    aspec = pl.BlockSpec((1, 512), lambda j: (0, j))
    return pl.pallas_call(
        body, grid=(NT5,),
        in_specs=[pl.BlockSpec((L, 128), lambda j: (0, j)), wspec, wspec, wspec, wspec, aspec, aspec,
                  pl.BlockSpec((1, 128), lambda j: (0, j))],
        out_specs=[pl.BlockSpec((L, 512), lambda j: (0, j)), pl.BlockSpec((L, 512), lambda j: (0, j)),
                   pl.BlockSpec((L, 128), lambda j: (0, j))],
        out_shape=[S((L, NST), f32), S((L, NST), f32), S((L, S5W), f32)],
        compiler_params=_cp(("parallel",)), name="s5_scan_fwd")(proj, wbr, wbi, wcr, wci, abr, abi, drow)


def s5_scan_bwd(dy, proj, xs_re, xs_im, wbr, wbi, wcr, wci, abr, abi, drow):
    def body(dy_ref, u_ref, xr_ref, xi_ref, wbr_ref, wbi_ref, wcr_ref, wci_ref, ar_ref, ai_ref, d_ref,
             du_ref, gwbr_ref, gwbi_ref, gwcr_ref, gwci_ref, gar_ref, gai_ref, gd_ref, lr_ref, li_ref):
        wcr_v, wci_v = wcr_ref[...], wci_ref[...]
        gwcr = jnp.zeros((512, 128), f32)
        gwci = jnp.zeros((512, 128), f32)
        gd = jnp.zeros((1, 128), f32)
        for r in range(L // RC):
            rows = pl.ds(r * RC, RC)
            dyv = dy_ref[rows, :]
            lr_ref[rows, :] = dot_nt(dyv, wcr_v)
            li_ref[rows, :] = -dot_nt(dyv, wci_v)
            gwcr += dot_tn(xr_ref[rows, :], dyv)
            gwci -= dot_tn(xi_ref[rows, :], dyv)
            gd += jnp.sum(dyv * u_ref[rows, :], axis=0, keepdims=True)
        gwcr_ref[...] = gwcr
        gwci_ref[...] = gwci
        gd_ref[...] = gd
        ar, ai = ar_ref[...], ai_ref[...]

        def step(i, c):
            lr, li, gar, gai = c
            t = L - 1 - i
            nr = lr_ref[pl.ds(t, 1), :] + ar * lr + ai * li
            ni = li_ref[pl.ds(t, 1), :] + ar * li - ai * lr
            lr_ref[pl.ds(t, 1), :] = nr
            li_ref[pl.ds(t, 1), :] = ni
            tp = jnp.maximum(t - 1, 0)
            live = (t > 0).astype(f32)
            xr = xr_ref[pl.ds(tp, 1), :] * live
            xi = xi_ref[pl.ds(tp, 1), :] * live
            return nr, ni, gar + xr * nr + xi * ni, gai + xr * ni - xi * nr

        z = jnp.zeros((1, 512), f32)
        _, _, gar, gai = lax.fori_loop(0, L, step, (z, z, z, z), unroll=8)
        gar_ref[...] = gar
        gai_ref[...] = gai
        wbr_v, wbi_v, dv = wbr_ref[...], wbi_ref[...], d_ref[...]
        gwbr = jnp.zeros((512, 128), f32)
        gwbi = jnp.zeros((512, 128), f32)
        for r in range(L // RC):
            rows = pl.ds(r * RC, RC)
            lrv, liv, uv = lr_ref[rows, :], li_ref[rows, :], u_ref[rows, :]
            du_ref[rows, :] = (dot_nn(lrv, wbr_v) + dot_nn(liv, wbi_v) + dv * dy_ref[rows, :]).astype(du_ref.dtype)
            gwbr += dot_tn(lrv, uv)
            gwbi += dot_tn(liv, uv)
        gwbr_ref[...] = gwbr
        gwbi_ref[...] = gwbi

    wspec = pl.BlockSpec((512, 128), lambda j: (j, 0))
    aspec = pl.BlockSpec((1, 512), lambda j: (0, j))
    col = pl.BlockSpec((L, 128), lambda j: (0, j))
    st = pl.BlockSpec((L, 512), lambda j: (0, j))
    dspec = pl.BlockSpec((1, 128), lambda j: (0, j))
    return pl.pallas_call(
        body, grid=(NT5,),
        in_specs=[col, col, st, st, wspec, wspec, wspec, wspec, aspec, aspec, dspec],
        out_specs=[col, wspec, wspec, wspec, wspec, aspec, aspec, dspec],
        out_shape=[S((L, S5W), BF)] + [S((NST, 128), f32)] * 4 + [S((1, NST), f32)] * 2 + [S((1, S5W), f32)],
        scratch_shapes=[pltpu.VMEM((L, 512), f32), pltpu.VMEM((L, 512), f32)],
        compiler_params=_cp(("parallel",)), name="s5_scan_bwd")(dy, proj, xs_re, xs_im, wbr, wbi, wcr, wci, abr, abi, drow)


def _glu(y, w, b):
    z = jax.nn.gelu(y)
    return z * jax.nn.sigmoid(dot_nn(z, w) + b)


def s5_glu_fwd(y, w, b):
    def body(y_ref, w_ref, b_ref, o_ref):
        o_ref[...] = _glu(y_ref[...], w_ref[...], b_ref[...]).astype(o_ref.dtype)

    return pl.pallas_call(
        body, grid=(L // TR,),
        in_specs=[pl.BlockSpec((TR, S5W), lambda i: (i, 0)), pl.BlockSpec((S5W, S5W), lambda i: (0, 0)),
                  pl.BlockSpec((1, S5W), lambda i: (0, 0))],
        out_specs=pl.BlockSpec((TR, S5W), lambda i: (i, 0)), out_shape=S((L, S5W), BF),
        compiler_params=_cp(("parallel",)), name="s5_glu_fwd")(y, w, b)


def s5_glu_bwd(y, w, b, dmix):
    def body(y_ref, w_ref, b_ref, g_ref, dy_ref, dw_ref, db_ref):
        _, vjp = jax.vjp(_glu, y_ref[...], w_ref[...].astype(f32), b_ref[...])
        dy, dw, db = vjp(g_ref[...])
        dy_ref[...] = dy

        @pl.when(pl.program_id(0) == 0)
        def _():
            dw_ref[...] = jnp.zeros_like(dw_ref)
            db_ref[...] = jnp.zeros_like(db_ref)

        dw_ref[...] += dw
        db_ref[...] += db

    row = pl.BlockSpec((TR, S5W), lambda i: (i, 0))
    return pl.pallas_call(
        body, grid=(L // TR,),
        in_specs=[row, pl.BlockSpec((S5W, S5W), lambda i: (0, 0)), pl.BlockSpec((1, S5W), lambda i: (0, 0)), row],
        out_specs=[row, pl.BlockSpec((S5W, S5W), lambda i: (0, 0)), pl.BlockSpec((1, S5W), lambda i: (0, 0))],
        out_shape=[S((L, S5W), f32), S((S5W, S5W), f32), S((1, S5W), f32)],
        compiler_params=_cp(("arbitrary",)), name="s5_glu_bwd")(y, w, b, dmix)


def _hi(a, b, ca, cb):
    return lax.dot_general(a, b, (((ca,), (cb,)), ((), ())), precision=HI, preferred_element_type=f32)


def _hgrn_chunk(St, xq, xf, xi, xg, gam, ng):
    lb = jax.nn.sigmoid(gam[0:1] - gam[1:2])
    q = jax.nn.silu(xq)
    f = lb + (1.0 - lb) * jax.nn.sigmoid(xf)
    k = 1.0 - f
    g = jnp.log(f)
    ti = lax.broadcasted_iota(jnp.int32, (HGC, HGC), 0)
    si = lax.broadcasted_iota(jnp.int32, (HGC, HGC), 1)
    causal = si <= ti
    b = jnp.dot(causal.astype(f32), g, precision=HI, preferred_element_type=f32)
    qe = q * jnp.exp(b)
    o = _hi(qe, St, 1, 1)
    att = jnp.where(causal, _hi(qe, k * jnp.exp(-b), 1, 1), 0.0)
    o = o + _hi(att, xi, 1, 0)
    bl = b[HGC - 1:HGC]
    St_new = St * jnp.exp(bl) + _hi(xi, k * jnp.exp(bl - b), 0, 0)
    o = o * lax.rsqrt(jnp.mean(o * o, axis=-1, keepdims=True) + EPS) * ng
    return St_new, o * jax.nn.silu(xg)


NCH = L // HGC


def hgrn_fwd(proj, gamma, hnorm):
    def body(q_ref, f_ref, i_ref, g_ref, gam_ref, ng_ref, o_ref, ss_ref, st):
        @pl.when(pl.program_id(0) == 0)
        def _():
            st[...] = jnp.zeros_like(st)

        for h in range(4):
            sl = slice(h * 128, (h + 1) * 128)
            s0 = st[h]
            ss_ref[0, h] = s0
            s1, o = _hgrn_chunk(s0, q_ref[:, sl], f_ref[:, sl], i_ref[:, sl], g_ref[:, sl], gam_ref[:, sl], ng_ref[:, sl])
            st[h] = s1
            o_ref[:, sl] = o.astype(o_ref.dtype)

    def pj(n):
        return pl.BlockSpec((HGC, 512), lambda c: (c, n))

    return pl.pallas_call(
        body, grid=(NCH,),
        in_specs=[pj(1), pj(2), pj(3), pj(4), pl.BlockSpec((2, 512), lambda c: (0, 0)), pl.BlockSpec((1, 512), lambda c: (0, 0))],
        out_specs=[pl.BlockSpec((HGC, 512), lambda c: (c, 0)), pl.BlockSpec((1, 4, 128, 128), lambda c: (c, 0, 0, 0))],
        out_shape=[S((L, 512), BF), S((NCH, 4, 128, 128), f32)],
        scratch_shapes=[pltpu.VMEM((4, 128, 128), f32)],
        compiler_params=_cp(("arbitrary",)), name="hgrn_fwd")(proj, proj, proj, proj, gamma, hnorm)


def hgrn_bwd(proj, gamma, hnorm, ssave, dmix, du):
    def body(q_ref, f_ref, i_ref, g_ref, gam_ref, ng_ref, ss_ref, do_ref, du_ref, dp_ref, dgam_ref, dng_ref, dst):
        @pl.when(pl.program_id(0) == 0)
        def _():
            dst[...] = jnp.zeros_like(dst)
            dgam_ref[...] = jnp.zeros_like(dgam_ref)
            dng_ref[...] = jnp.zeros_like(dng_ref)

        dp_ref[:, 0:512] = du_ref[...]
        for h in range(4):
            sl = slice(h * 128, (h + 1) * 128)
            _, vjp = jax.vjp(_hgrn_chunk, ss_ref[0, h], q_ref[:, sl], f_ref[:, sl], i_ref[:, sl], g_ref[:, sl],
                             gam_ref[:, sl], ng_ref[:, sl])
            ds, dq, df, di, dg, dgam, dng = vjp((dst[h], do_ref[:, sl]))
            dst[h] = ds
            for n, v in enumerate((dq, df, di, dg)):
                dp_ref[:, 512 * (n + 1) + h * 128: 512 * (n + 1) + (h + 1) * 128] = v.astype(dp_ref.dtype)
            dgam_ref[:, sl] += dgam
            dng_ref[:, sl] += dng

    def pj(n):
        return pl.BlockSpec((HGC, 512), lambda i: (NCH - 1 - i, n))

    return pl.pallas_call(
        body, grid=(NCH,),
        in_specs=[pj(1), pj(2), pj(3), pj(4), pl.BlockSpec((2, 512), lambda i: (0, 0)), pl.BlockSpec((1, 512), lambda i: (0, 0)),
                  pl.BlockSpec((1, 4, 128, 128), lambda i: (NCH - 1 - i, 0, 0, 0)), pj(1), pj(0)],
        out_specs=[pl.BlockSpec((HGC, 2560), lambda i: (NCH - 1 - i, 0)), pl.BlockSpec((2, 512), lambda i: (0, 0)),
                   pl.BlockSpec((1, 512), lambda i: (0, 0))],
        out_shape=[S((L, 2560), BF), S((2, 512), f32), S((1, 512), f32)],
        scratch_shapes=[pltpu.VMEM((4, 128, 128), f32)],
        compiler_params=_cp(("arbitrary",)), name="hgrn_bwd")(proj, proj, proj, proj, gamma, hnorm, ssave, dmix, du)


def _shift(x, k):
    return jnp.concatenate([jnp.zeros((k, x.shape[1]), x.dtype), x[:-k]], axis=0)


def _convact(ha, hb, wa, wb, ba, bb):
    ca = wa[2:3] * ha + wa[1:2] * _shift(ha, 1) + wa[0:1] * _shift(ha, 2) + ba
    cb = wb[2:3] * hb + wb[1:2] * _shift(hb, 1) + wb[0:1] * _shift(hb, 2) + bb
    return jax.nn.silu(ca) * cb


CT = 128
NCT = DFF // CT


def convact_fwd(hu, cw, cb, layer):
    def body(ha_ref, hb_ref, wa_ref, wb_ref, ba_ref, bb_ref, o_ref):
        o_ref[...] = _convact(ha_ref[...], hb_ref[...], wa_ref[...], wb_ref[...], ba_ref[...], bb_ref[...]).astype(o_ref.dtype)

    def h(off):
        return pl.BlockSpec((L, CT), lambda j: (0, j + off))

    def w(off):
        return pl.BlockSpec((None, 3, CT), lambda j: (layer, 0, j + off))

    def b(off):
        return pl.BlockSpec((None, 1, CT), lambda j: (layer, 0, j + off))

    return pl.pallas_call(
        body, grid=(NCT,), in_specs=[h(0), h(NCT), w(0), w(NCT), b(0), b(NCT)],
        out_specs=pl.BlockSpec((L, CT), lambda j: (0, j)), out_shape=S((L, DFF), BF),
        compiler_params=_cp(("parallel",)), name=f"convact_fwd{layer}")(hu, hu, cw, cw, cb, cb)


def convact_bwd(hu, cw, cb, dact, layer):
    def body(ha_ref, hb_ref, wa_ref, wb_ref, ba_ref, bb_ref, g_ref, dh_ref, dw_ref, db_ref, sh, sw, sb):
        j = pl.program_id(0)

        @pl.when(j < NCT)
        def _():
            _, vjp = jax.vjp(_convact, ha_ref[...], hb_ref[...], wa_ref[...], wb_ref[...], ba_ref[...], bb_ref[...])
            dha, dhb, dwa, dwb, dba, dbb = vjp(g_ref[...].astype(f32))
            dh_ref[...] = dha.astype(dh_ref.dtype)
            dw_ref[...] = dwa
            db_ref[...] = dba
            sh[j] = dhb.astype(sh.dtype)
            sw[j] = dwb
            sb[j] = dbb

        @pl.when(j >= NCT)
        def _():
            dh_ref[...] = sh[j - NCT]
            dw_ref[...] = sw[j - NCT]
            db_ref[...] = sb[j - NCT]

    def lo(j):
        return jnp.minimum(j, NCT - 1)

    in_specs = [pl.BlockSpec((L, CT), lambda j: (0, lo(j))), pl.BlockSpec((L, CT), lambda j: (0, lo(j) + NCT)),
                pl.BlockSpec((None, 3, CT), lambda j: (layer, 0, lo(j))), pl.BlockSpec((None, 3, CT), lambda j: (layer, 0, lo(j) + NCT)),
                pl.BlockSpec((None, 1, CT), lambda j: (layer, 0, lo(j))), pl.BlockSpec((None, 1, CT), lambda j: (layer, 0, lo(j) + NCT)),
                pl.BlockSpec((L, CT), lambda j: (0, lo(j)))]
    return pl.pallas_call(
        body, grid=(2 * NCT,), in_specs=in_specs,
        out_specs=[pl.BlockSpec((L, CT), lambda j: (0, j)), pl.BlockSpec((3, CT), lambda j: (0, j)), pl.BlockSpec((1, CT), lambda j: (0, j))],
        out_shape=[S((L, 2 * DFF), BF), S((3, 2 * DFF), f32), S((1, 2 * DFF), f32)],
        scratch_shapes=[pltpu.VMEM((NCT, L, CT), BF), pltpu.VMEM((NCT, 3, CT), f32), pltpu.VMEM((NCT, 1, CT), f32)],
        compiler_params=_cp(("arbitrary",)), name=f"convact_bwd{layer}")(hu, hu, cw, cw, cb, cb, dact)


DILS = (1, 4, 16)
AB = 128
NPAIR = 12


def _rope_tables(pos_ref, invf_ref):
    ang = pos_ref[...].astype(f32) * invf_ref[...]
    lane = lax.broadcasted_iota(jnp.int32, (1, 128), 1) % 64
    cosf = jnp.where(lane < 16, jnp.cos(ang), 1.0)
    sn = jnp.sin(ang)
    s_lo = jnp.where(lane < 8, -sn, 0.0)
    s_hi = jnp.where((lane >= 8) & (lane < 16), sn, 0.0)
    return cosf, s_lo, s_hi


def _rope(t, cosf, s_lo, s_hi):
    return t * cosf + pltpu.roll(t, 120, 1) * s_lo + pltpu.roll(t, 8, 1) * s_hi


def _rope_t(g, cosf, s_lo, s_hi):
    return g * cosf + pltpu.roll(g * s_lo, 8, 1) + pltpu.roll(g * s_hi, 120, 1)


def _att_block(q2, kp, kc, vp, vc, first):
    lane = lax.broadcasted_iota(jnp.int32, (1, 128), 1)
    qi = lax.broadcasted_iota(jnp.int32, (AB, 2 * AB), 0) + AB
    kj = lax.broadcasted_iota(jnp.int32, (AB, 2 * AB), 1)
    back = qi - kj
    valid = (back >= 0) & (back <= AB)
    if first:
        valid = valid & (kj >= AB)
    kk = jnp.concatenate([kp, kc], axis=0)
    vv = jnp.concatenate([vp, vc], axis=0)
    o2 = jnp.zeros((AB, 128), f32)
    lse2 = jnp.zeros((AB, 128), f32)
    for e in range(2):
        hm = ((lane >= 64 * e) & (lane < 64 * (e + 1))).astype(f32)
        s = dot_nt(q2 * (hm * 0.125), kk)
        s = jnp.where(valid, s, -jnp.inf)
        m = jnp.max(s, axis=-1, keepdims=True)
        p = jnp.exp(s - m)
        den = jnp.sum(p, axis=-1, keepdims=True)
        o2 = o2 + dot_nn(p, vv * hm) / den
        lse2 = lse2 + (m + jnp.log(den)) * hm
    return o2, lse2


def _att_rows(dil, r, n):
    if dil == 1:
        return pl.ds(n * AB, AB)
    return pl.ds(n * AB * dil + r, AB, stride=dil)


def _att_blocks(dil):
    nb = (L // dil) // AB
    return [(r, n) for r in range(dil) for n in range(nb)]


def attn_fwd(qkv, pos, invf):
    def body(q_ref, k_ref, v_ref, pos_ref, invf_ref, o_ref, l_ref, qr, kr):
        cosf, s_lo, s_hi = _rope_tables(pos_ref, invf_ref)
        qr[...] = _rope(q_ref[...], cosf, s_lo, s_hi)
        kr[...] = _rope(k_ref[...], cosf, s_lo, s_hi)
        p = pl.program_id(0)
        for g, dil in enumerate(DILS):
            @pl.when(p // 4 == g)
            def _(dil=dil):
                for r, n in _att_blocks(dil):
                    cur, prv = _att_rows(dil, r, n), _att_rows(dil, r, max(n - 1, 0))
                    o2, lse2 = _att_block(qr[cur, :], kr[prv, :], kr[cur, :], v_ref[prv, :], v_ref[cur, :], n == 0)
                    o_ref[cur, :] = o2
                    l_ref[cur, :] = lse2

    def sec(n):
        return pl.BlockSpec((L, 128), lambda p: (0, p + n * NPAIR))

    return pl.pallas_call(
        body, grid=(NPAIR,),
        in_specs=[sec(0), sec(1), sec(2), pl.BlockSpec((L, 1), lambda p: (0, 0)), pl.BlockSpec((1, 128), lambda p: (0, 0))],
        out_specs=[sec(0), sec(0)], out_shape=[S((L, 1536), f32), S((L, 1536), f32)],
        scratch_shapes=[pltpu.VMEM((L, 128), f32), pltpu.VMEM((L, 128), f32)],
        compiler_params=_cp(("parallel",)), name="attn_fwd")(qkv, qkv, qkv, pos, invf)


def attn_bwd(qkv, pos, invf, do_all, dl_all):
    def body(q_ref, k_ref, v_ref, pos_ref, invf_ref, do_ref, dl_ref, dq_ref, dk_ref, dv_ref, qr, kr, dqr, dkr, dvr):
        cosf, s_lo, s_hi = _rope_tables(pos_ref, invf_ref)
        qr[...] = _rope(q_ref[...], cosf, s_lo, s_hi)
        kr[...] = _rope(k_ref[...], cosf, s_lo, s_hi)
        dkr[...] = jnp.zeros_like(dkr)
        dvr[...] = jnp.zeros_like(dvr)
        p = pl.program_id(0)
        for g, dil in enumerate(DILS):
            @pl.when(p // 4 == g)
            def _(dil=dil):
                for r, n in _att_blocks(dil):
                    cur, prv = _att_rows(dil, r, n), _att_rows(dil, r, max(n - 1, 0))
                    fn = functools.partial(_att_block, first=(n == 0))
                    _, vjp = jax.vjp(fn, qr[cur, :], kr[prv, :], kr[cur, :], v_ref[prv, :], v_ref[cur, :])
                    dq2, dkp, dkc, dvp, dvc = vjp((do_ref[cur, :], dl_ref[cur, :]))
                    dqr[cur, :] = dq2
                    dkr[cur, :] += dkc
                    dvr[cur, :] += dvc
                    if n > 0:
                        dkr[prv, :] += dkp
                        dvr[prv, :] += dvp
        dq_ref[...] = _rope_t(dqr[...], cosf, s_lo, s_hi).astype(dq_ref.dtype)
        dk_ref[...] = _rope_t(dkr[...], cosf, s_lo, s_hi).astype(dk_ref.dtype)
        dv_ref[...] = dvr[...].astype(dv_ref.dtype)

    def sec(n):
        return pl.BlockSpec((L, 128), lambda p: (0, p + n * NPAIR))

    return pl.pallas_call(
        body, grid=(NPAIR,),
        in_specs=[sec(0), sec(1), sec(2), pl.BlockSpec((L, 1), lambda p: (0, 0)), pl.BlockSpec((1, 128), lambda p: (0, 0)),
                  sec(0), sec(0)],
        out_specs=[sec(0), sec(0), sec(0)], out_shape=[S((L, 1536), BF)] * 3,
        scratch_shapes=[pltpu.VMEM((L, 128), f32)] * 5,
        compiler_params=_cp(("parallel",)), name="attn_bwd")(qkv, qkv, qkv, pos, invf, do_all, dl_all)


def _merge(o0, o1, o2, l0, l1, l2):
    m = jnp.maximum(jnp.maximum(l0, l1), l2)
    e0, e1, e2 = jnp.exp(l0 - m), jnp.exp(l1 - m), jnp.exp(l2 - m)
    return (e0 * o0 + e1 * o1 + e2 * o2) / (e0 + e1 + e2)


def attn_merge_fwd(o_all, l_all):
    def body(o0, o1, o2, l0, l1, l2, o_ref):
        o_ref[...] = _merge(o0[...], o1[...], o2[...], l0[...], l1[...], l2[...]).astype(o_ref.dtype)

    def br(g):
        return pl.BlockSpec((TR, 512), lambda i: (i, g))

    return pl.pallas_call(
        body, grid=(L // TR,), in_specs=[br(0), br(1), br(2)] * 2,
        out_specs=pl.BlockSpec((TR, 512), lambda i: (i, 0)), out_shape=S((L, 512), BF),
        compiler_params=_cp(("parallel",)), name="attn_merge_fwd")(o_all, o_all, o_all, l_all, l_all, l_all)


def attn_merge_bwd(o_all, l_all, do):
    def body(o0, o1, o2, l0, l1, l2, g_ref, do_ref, dl_ref):
        _, vjp = jax.vjp(_merge, o0[...], o1[...], o2[...], l0[...], l1[...], l2[...])
        gs = vjp(g_ref[...])
        for g in range(3):
            do_ref[:, 512 * g:512 * (g + 1)] = gs[g]
            dl_ref[:, 512 * g:512 * (g + 1)] = gs[3 + g]

    def br(g):
        return pl.BlockSpec((TR, 512), lambda i: (i, g))

    full = pl.BlockSpec((TR, 1536), lambda i: (i, 0))
    return pl.pallas_call(
        body, grid=(L // TR,), in_specs=[br(0), br(1), br(2)] * 2 + [br(0)],
        out_specs=[full, full], out_shape=[S((L, 1536), f32)] * 2,
        compiler_params=_cp(("parallel",)), name="attn_merge_bwd")(o_all, o_all, o_all, l_all, l_all, l_all, do)


def _invf_lanes():
    half = 8
    inv = ROPE_THETA ** (-np.arange(half, dtype=np.float32) * 2.0 / 16.0)
    lane = np.arange(128) % 64
    return jnp.asarray(np.where(lane < 16, inv[lane % 8], 0.0).astype(np.float32)[None, :])


def _ffn_fwd(h, g_row, w_in, w_out, cw, cb, layer):
    hn = rms_fwd(h, g_row, f"rms_ffn{layer}")
    hu = matmul(hn, w_in, mode="nn", tm=512, tn=512, tk=1024, b_lead=layer, name=f"ffn_in{layer}")
    act = convact_fwd(hu, cw, cb, layer)
    h2 = matmul(act, w_out, mode="nn", tm=512, tn=512, tk=1408, add=h, b_lead=layer, name=f"ffn_out{layer}")
    return h2, (hn, hu, act)


def _ffn_bwd(dh, h, g_row, w_in, w_out, cw, cb, saved, layer):
    hn, hu, act = saved
    dact = matmul(dh, w_out, mode="nt", tm=512, tn=256, tk=1024, b_lead=layer, name=f"ffn_out_dx{layer}")
    g_wout = matmul(act, dh, mode="tn", tm=256, tn=512, tk=512, out_dtype=BF, name=f"ffn_out_dw{layer}")
    dhu, g_cw, g_cb = convact_bwd(hu, cw, cb, dact, layer)
    dhn = matmul(dhu, w_in, mode="nt", tm=512, tn=512, tk=1408, b_lead=layer, name=f"ffn_in_dx{layer}")
    g_win = matmul(hn, dhu, mode="tn", tm=512, tn=512, tk=512, out_dtype=BF, name=f"ffn_in_dw{layer}")
    dh2, g_norm = rms_bwd(h, g_row, dhn, dh, f"rms_ffn_bwd{layer}")
    return dh2, g_win, g_wout, g_cw, g_cb, g_norm


def local_step(x, pos, tgt, sm, W):
    nm, nf = sm["norm_mix"], sm["norm_ffn"]
    invf = _invf_lanes()
    are = sm["s5_A_re"].reshape(NST, 1)
    aim = sm["s5_A_im"].reshape(NST, 1)
    ldt = sm["s5_log_dt"].reshape(1, 32)
    bre = sm["s5_B_re"].reshape(NST, 16)
    bim = sm["s5_B_im"].reshape(NST, 16)
    cre = jnp.swapaxes(sm["s5_C_re"][0], 1, 2).reshape(NST, 16)
    cim = jnp.swapaxes(sm["s5_C_im"][0], 1, 2).reshape(NST, 16)
    drow = sm["s5_D"].reshape(1, S5W)
    wbr, wbi, wcr, wci, abr, abi = s5_params_fwd(are, aim, ldt, bre, bim, cre, cim)
    hn0 = rms_fwd(x, nm[0:1], "rms_mix0")
    proj = matmul(hn0, W["mix_w_in"], mode="nn", tm=512, tn=512, tk=1024, b_lead=0, name="mix_in")
    xs_re, xs_im, y5 = s5_scan_fwd(proj, wbr, wbi, wcr, wci, abr, abi, drow)
    oa = s5_glu_fwd(y5, W["s5_glu_w"][0], sm["s5_glu_b"])
    ob, ssave = hgrn_fwd(proj, sm["hgrn_gamma"], sm["hgrn_norm"])
    cat = jnp.concatenate([oa, ob], axis=1)
    h1 = matmul(cat, W["mix_w_out"], mode="nn", tm=512, tn=512, tk=1024, add=x, b_lead=0, name="mix_out")
    h2, ffn0 = _ffn_fwd(h1, nf[0:1], W["ffn_w_in"], W["ffn_w_out"], W["ffn_conv_w"], sm["ffn_conv_b3"], 0)
    hn2 = rms_fwd(h2, nm[1:2], "rms_mix1")
    qkv = matmul(hn2, W["att_w_qkv"], mode="nn", tm=512, tn=512, tk=1024, b_lead=0, name="att_qkv")
    o_all, l_all = attn_fwd(qkv, pos, invf)
    o = attn_merge_fwd(o_all, l_all)
    h3 = matmul(o, W["att_w_o"], mode="nn", tm=512, tn=512, tk=512, add=h2, b_lead=0, name="att_o")
    h4, ffn1 = _ffn_fwd(h3, nf[1:2], W["ffn_w_in"], W["ffn_w_out"], W["ffn_conv_w"], sm["ffn_conv_b3"], 1)
    loss, dh, g_nfinal = loss_head(h4, sm["norm_final"].reshape(1, D), tgt)
    dh, g_win1, g_wout1, g_cw1, g_cb1, g_nf1 = _ffn_bwd(dh, h3, nf[1:2], W["ffn_w_in"], W["ffn_w_out"], W["ffn_conv_w"],
                                                       sm["ffn_conv_b3"], ffn1, 1)
    do = matmul(dh, W["att_w_o"], mode="nt", tm=512, tn=512, tk=1024, b_lead=0, name="att_o_dx")
    g_wo = matmul(o, dh, mode="tn", tm=512, tn=512, tk=512, out_dtype=BF, name="att_o_dw")
    do_all, dl_all = attn_merge_bwd(o_all, l_all, do)
    dq, dk, dv = attn_bwd(qkv, pos, invf, do_all, dl_all)
    dqkv = jnp.concatenate([dq, dk, dv], axis=1)
    dhn2 = matmul(dqkv, W["att_w_qkv"], mode="nt", tm=512, tn=512, tk=1536, b_lead=0, name="att_qkv_dx")
    g_wqkv = matmul(hn2, dqkv, mode="tn", tm=512, tn=512, tk=512, out_dtype=BF, name="att_qkv_dw")
    dh, g_nm1 = rms_bwd(h2, nm[1:2], dhn2, dh, "rms_mix_bwd1")
    dh, g_win0, g_wout0, g_cw0, g_cb0, g_nf0 = _ffn_bwd(dh, h1, nf[0:1], W["ffn_w_in"], W["ffn_w_out"], W["ffn_conv_w"],
                                                       sm["ffn_conv_b3"], ffn0, 0)
    dmix = matmul(dh, W["mix_w_out"], mode="nt", tm=512, tn=512, tk=1024, b_lead=0, name="mix_out_dx")
    g_wmo = matmul(cat, dh, mode="tn", tm=512, tn=512, tk=512, out_dtype=BF, name="mix_out_dw")
    dy5, g_glu_w, g_glu_b = s5_glu_bwd(y5, W["s5_glu_w"][0], sm["s5_glu_b"], dmix)
    du, gwbr, gwbi, gwcr, gwci, gabr, gabi, g_d = s5_scan_bwd(dy5, proj, xs_re, xs_im, wbr, wbi, wcr, wci, abr, abi, drow)
    g_are, g_aim, g_ldt, g_bre, g_bim, g_cre, g_cim = s5_params_bwd(are, aim, ldt, bre, bim, cre, cim,
                                                                   (gwbr, gwbi, gwcr, gwci, gabr, gabi))
    dproj, g_gamma, g_hnorm = hgrn_bwd(proj, sm["hgrn_gamma"], sm["hgrn_norm"], ssave, dmix, du)
    dhn0 = matmul(dproj, W["mix_w_in"], mode="nt", tm=512, tn=512, tk=512, b_lead=0, name="mix_in_dx")
    g_wmi = matmul(hn0, dproj, mode="tn", tm=512, tn=512, tk=512, out_dtype=BF, name="mix_in_dw")
    gx, g_nm0 = rms_bwd(x, nm[0:1], dhn0, dh, "rms_mix_bwd0")
    big = {
        "mix_w_in": [g_wmi], "mix_w_out": [g_wmo], "s5_glu_w": [g_glu_w.astype(BF)], "att_w_qkv": [g_wqkv], "att_w_o": [g_wo],
        "ffn_w_in": [g_win0, g_win1], "ffn_w_out": [g_wout0, g_wout1], "ffn_conv_w": [g_cw0, g_cw1],
    }
    small = {
        "norm_mix": jnp.concatenate([g_nm0, g_nm1], axis=0), "norm_ffn": jnp.concatenate([g_nf0, g_nf1], axis=0),
        "norm_final": g_nfinal.reshape(D),
        "s5_A_re": g_are.reshape(1, 32, 64), "s5_A_im": g_aim.reshape(1, 32, 64), "s5_log_dt": g_ldt.reshape(1, 32),
        "s5_B_re": g_bre.reshape(1, 32, 64, 16), "s5_B_im": g_bim.reshape(1, 32, 64, 16),
        "s5_C_re": jnp.swapaxes(g_cre.reshape(1, 32, 64, 16), 2, 3), "s5_C_im": jnp.swapaxes(g_cim.reshape(1, 32, 64, 16), 2, 3),
        "s5_D": g_d.reshape(1, 32, 16), "s5_glu_b": g_glu_b, "hgrn_gamma": g_gamma, "hgrn_norm": g_hnorm,
        "ffn_conv_b": jnp.concatenate([g_cb0, g_cb1], axis=0),
    }
    return loss, gx, big, small


BIG = ("mix_w_in", "mix_w_out", "s5_glu_w", "att_w_qkv", "att_w_o", "ffn_w_in", "ffn_w_out", "ffn_conv_w")
ROW_SHARDED = ("mix_w_out", "s5_glu_w", "ffn_w_out")
SMALL = ("norm_mix", "norm_ffn", "norm_final", "s5_A_re", "s5_A_im", "s5_log_dt", "s5_B_re", "s5_B_im", "s5_C_re", "s5_C_im",
         "s5_D", "s5_glu_b", "hgrn_gamma", "hgrn_norm", "ffn_conv_b")
ANY = pl.BlockSpec(memory_space=pl.ANY)


def _chip_slice(ref, q, rows, n):
    start = pl.multiple_of(q * n, n)
    if rows:
        return ref.at[:, pl.ds(start, n), :]
    return ref.at[:, :, pl.ds(start, n)]


def cast_bf16(w, name):
    nl, r, c = w.shape
    w2 = w.reshape(nl * r, c)
    tr = 256 if (nl * r) % 256 == 0 else nl * r

    def body(w_ref, o_ref):
        o_ref[...] = w_ref[...].astype(BF)

    out = pl.pallas_call(
        body, grid=(nl * r // tr,), in_specs=[pl.BlockSpec((tr, c), lambda i: (i, 0))],
        out_specs=pl.BlockSpec((tr, c), lambda i: (i, 0)), out_shape=S((nl * r, c), BF),
        compiler_params=_cp(("parallel",)), name=name)(w2)
    return out.reshape(nl, r, c)


def gather_weights(shards):
    names = list(shards)
    nw = len(names)
    out_shape = []
    for nme in names:
        nl, r, c = shards[nme].shape
        out_shape.append(S((nl, 4 * r, c) if nme in ROW_SHARDED else (nl, r, 4 * c), shards[nme].dtype))

    def body(*refs):
        src, dst = refs[:nw], refs[nw:2 * nw]
        send_sems, recv_sems, loc_sems = refs[2 * nw:]
        x, y, c = lax.axis_index("x"), lax.axis_index("y"), lax.axis_index("c")
        me = 2 * x + y
        others = [(1 - x, y), (x, 1 - y), (1 - x, 1 - y)]

        def part(w, q):
            rows = names[w] in ROW_SHARDED
            return _chip_slice(dst[w], q, rows, src[w].shape[1] if rows else src[w].shape[2])

        def remote(w, k, q):
            px, py = others[k]
            return pltpu.make_async_remote_copy(src_ref=src[w], dst_ref=part(w, q), send_sem=send_sems.at[w, k],
                                                recv_sem=recv_sems.at[w, k], device_id=(px, py, c), device_id_type=MESH)

        local = [pltpu.make_async_copy(src[w], part(w, me), loc_sems.at[w]) for w in range(nw)]
        for w in range(nw):
            local[w].start()
            for k in range(3):
                remote(w, k, me).start()
        for w in range(nw):
            for k in range(3):
                px, py = others[k]
                remote(w, k, 2 * px + py).wait_recv()
        for w in range(nw):
            for k in range(3):
                remote(w, k, me).wait_send()
            local[w].wait()

    outs = pl.pallas_call(
        body, in_specs=[ANY] * nw, out_specs=[ANY] * nw, out_shape=out_shape,
        scratch_shapes=[pltpu.SemaphoreType.DMA((nw, 3)), pltpu.SemaphoreType.DMA((nw, 3)), pltpu.SemaphoreType.DMA((nw,))],
        compiler_params=pltpu.CompilerParams(has_side_effects=True), name="gather_weights")(*[shards[n] for n in names])
    return dict(zip(names, outs))


def reduce_grads(grads, shard_shapes):
    names = list(grads)
    flat = [(nme, l) for nme in names for l in range(len(grads[nme]))]
    ne = len(flat)
    nw = len(names)
    out_shape = [S((8,) + tuple(shard_shapes[nme]), grads[nme][0].dtype) for nme in names]

    def body(*refs):
        src, dst = refs[:ne], refs[ne:ne + nw]
        send_sems, recv_sems, loc_sems, fsend_sems, frecv_sems = refs[ne + nw:]
        x, y, c = lax.axis_index("x"), lax.axis_index("y"), lax.axis_index("c")
        me = 2 * x + y
        others = [(1 - x, y), (x, 1 - y), (1 - x, 1 - y)]
        sib = (x, y, 1 - c)

        def piece(e, q):
            nme, l = flat[e]
            rows = nme in ROW_SHARDED
            shp = shard_shapes[nme]
            if rows:
                return src[e].at[pl.ds(pl.multiple_of(q * shp[1], shp[1]), shp[1]), :]
            return src[e].at[:, pl.ds(pl.multiple_of(q * shp[2], shp[2]), shp[2])]

        def slot(e, s):
            nme, l = flat[e]
            return dst[names.index(nme)].at[s, l]

        def remote(e, k, q_to, q_from, core):
            px, py = others[k]
            return pltpu.make_async_remote_copy(src_ref=piece(e, q_to), dst_ref=slot(e, 2 * q_from + core),
                                                send_sem=send_sems.at[e, k], recv_sem=recv_sems.at[e, k],
                                                device_id=(px, py, c), device_id_type=MESH)

        def forward(e, j, q, core):
            return pltpu.make_async_remote_copy(src_ref=slot(e, 2 * q + core), dst_ref=slot(e, 2 * q + core),
                                                send_sem=fsend_sems.at[e, j], recv_sem=frecv_sems.at[e, j],
                                                device_id=sib, device_id_type=MESH)

        local = [pltpu.make_async_copy(piece(e, me), slot(e, 2 * me + c), loc_sems.at[e]) for e in range(ne)]
        for e in range(ne):
            local[e].start()
            for k in range(3):
                px, py = others[k]
                remote(e, k, 2 * px + py, me, c).start()
        for e in range(ne):
            local[e].wait()
            forward(e, 0, me, c).start()
        for e in range(ne):
            for k in range(3):
                px, py = others[k]
                q = 2 * px + py
                remote(e, k, me, q, c).wait_recv()
                forward(e, 1 + k, q, c).start()
        for e in range(ne):
            forward(e, 0, me, 1 - c).wait_recv()
            for k in range(3):
                px, py = others[k]
                forward(e, 1 + k, 2 * px + py, 1 - c).wait_recv()
        for e in range(ne):
            forward(e, 0, me, c).wait_send()
            for k in range(3):
                px, py = others[k]
                remote(e, k, 2 * px + py, me, c).wait_send()
                forward(e, 1 + k, 2 * px + py, c).wait_send()

    outs = pl.pallas_call(
        body, in_specs=[ANY] * ne, out_specs=[ANY] * nw, out_shape=out_shape,
        scratch_shapes=[pltpu.SemaphoreType.DMA((ne, 3)), pltpu.SemaphoreType.DMA((ne, 3)), pltpu.SemaphoreType.DMA((ne,)),
                        pltpu.SemaphoreType.DMA((ne, 4)), pltpu.SemaphoreType.DMA((ne, 4))],
        compiler_params=pltpu.CompilerParams(has_side_effects=True), name="reduce_grads")(*[grads[n][l] for n, l in flat])
    return dict(zip(names, outs))


def allreduce_small(g):
    R = g.shape[0]

    def body(g_ref, o_ref, send_sems, recv_sems, loc_sem):
        x, y, c = lax.axis_index("x"), lax.axis_index("y"), lax.axis_index("c")
        me = 4 * x + 2 * y + c
        peers = []
        for m in range(1, 8):
            fx, fy, fc = (m >> 2) & 1, (m >> 1) & 1, m & 1
            peers.append((jnp.where(fx, 1 - x, x), jnp.where(fy, 1 - y, y), jnp.where(fc, 1 - c, c)))

        def cp(k, s):
            return pltpu.make_async_remote_copy(src_ref=g_ref, dst_ref=o_ref.at[s], send_sem=send_sems.at[k],
                                                recv_sem=recv_sems.at[k], device_id=peers[k], device_id_type=MESH)

        loc = pltpu.make_async_copy(g_ref, o_ref.at[me], loc_sem)
        loc.start()
        for k in range(7):
            cp(k, me).start()
        for k in range(7):
            px, py, pc = peers[k]
            cp(k, 4 * px + 2 * py + pc).wait_recv()
        for k in range(7):
            cp(k, me).wait_send()
        loc.wait()

    return pl.pallas_call(
        body, in_specs=[ANY], out_specs=ANY, out_shape=S((8, R, 128), f32),
        scratch_shapes=[pltpu.SemaphoreType.DMA((7,)), pltpu.SemaphoreType.DMA((7,)), pltpu.SemaphoreType.DMA],
        compiler_params=pltpu.CompilerParams(has_side_effects=True), name="allreduce_small")(g)


def _adamw(w, g, m, v):
    m = B1 * m + (1.0 - B1) * g
    v = B2 * v + (1.0 - B2) * jnp.square(g)
    m_hat = m / (1.0 - B1 ** STEP)
    v_hat = v / (1.0 - B2 ** STEP)
    return -LR * (m_hat / (jnp.sqrt(v_hat) + AEPS) + WD * w), m, v


def adam_big(w, m, v, slots, name):
    nl, R, C = w.shape
    tr = 128 if R % 128 == 0 else (64 if R % 64 == 0 else R)

    def body(w_ref, m_ref, v_ref, s_ref, g_ref, d_ref, nm_ref, nv_ref):
        g = s_ref[0].astype(f32)
        for s in range(1, 8):
            g = g + s_ref[s].astype(f32)
        d, nm_, nv_ = _adamw(w_ref[...], g, m_ref[...], v_ref[...])
        g_ref[...] = g
        d_ref[...] = d
        nm_ref[...] = nm_
        nv_ref[...] = nv_

    blk = pl.BlockSpec((None, tr, C), lambda l, i: (l, i, 0))
    return pl.pallas_call(
        body, grid=(nl, R // tr),
        in_specs=[blk, blk, blk, pl.BlockSpec((8, None, tr, C), lambda l, i: (0, l, i, 0))],
        out_specs=[blk] * 4, out_shape=[S((nl, R, C), f32)] * 4,
        compiler_params=_cp(("parallel", "parallel")), name=name)(w, m, v, slots)


def adam_small(w, m, v, slots):
    R = w.shape[0]
    tr = 256

    def body(w_ref, m_ref, v_ref, s_ref, g_ref, d_ref, nm_ref, nv_ref):
        g = s_ref[0]
        for s in range(1, 8):
            g = g + s_ref[s]
        d, nm_, nv_ = _adamw(w_ref[...], g, m_ref[...], v_ref[...])
        g_ref[...] = g
        d_ref[...] = d
        nm_ref[...] = nm_
        nv_ref[...] = nv_

    blk = pl.BlockSpec((tr, 128), lambda i: (i, 0))
    return pl.pallas_call(
        body, grid=(R // tr,), in_specs=[blk, blk, blk, pl.BlockSpec((8, tr, 128), lambda i: (0, i, 0))],
        out_specs=[blk] * 4, out_shape=[S((R, 128), f32)] * 4,
        compiler_params=_cp(("parallel",)), name="adam_small")(w, m, v, slots)


def _pack(d):
    flat = jnp.concatenate([d[n].reshape(-1) for n in SMALL])
    n = flat.shape[0]
    rows = -(-n // (256 * 128)) * 256
    return jnp.pad(flat, (0, rows * 128 - n)).reshape(rows, 128)


def _unpack(p, like):
    flat = p.reshape(-1)
    out, off = {}, 0
    for n in SMALL:
        sz = math.prod(like[n].shape)
        out[n] = flat[off:off + sz].reshape(like[n].shape)
        off += sz
    return out


def kernel(x, positions, norm_mix, norm_ffn, norm_final, mix_w_in, mix_w_out, s5_A_re, s5_A_im, s5_log_dt, s5_B_re, s5_B_im, s5_C_re, s5_C_im, s5_D, s5_glu_w, s5_glu_b, hgrn_gamma, hgrn_norm, att_w_qkv, att_w_o, ffn_w_in, ffn_conv_w, ffn_conv_b, ffn_w_out, loss_target, m_norm_mix, m_norm_ffn, m_norm_final, m_mix_w_in, m_mix_w_out, m_s5_A_re, m_s5_A_im, m_s5_log_dt, m_s5_B_re, m_s5_B_im, m_s5_C_re, m_s5_C_im, m_s5_D, m_s5_glu_w, m_s5_glu_b, m_hgrn_gamma, m_hgrn_norm, m_att_w_qkv, m_att_w_o, m_ffn_w_in, m_ffn_conv_w, m_ffn_conv_b, m_ffn_w_out, v_norm_mix, v_norm_ffn, v_norm_final, v_mix_w_in, v_mix_w_out, v_s5_A_re, v_s5_A_im, v_s5_log_dt, v_s5_B_re, v_s5_B_im, v_s5_C_re, v_s5_C_im, v_s5_D, v_s5_glu_w, v_s5_glu_b, v_hgrn_gamma, v_hgrn_norm, v_att_w_qkv, v_att_w_o, v_ffn_w_in, v_ffn_conv_w, v_ffn_conv_b, v_ffn_w_out):
    a = dict(locals())
    weights = BIG + SMALL
    w = {n: a[n] for n in weights}
    m = {n: a["m_" + n] for n in weights}
    v = {n: a["v_" + n] for n in weights}
    shards = {n: (w[n] if n == "ffn_conv_w" else cast_bf16(w[n], "cast_" + n)) for n in BIG}
    W = gather_weights(shards)
    sm = {n: w[n] for n in SMALL}
    sm["ffn_conv_b3"] = ffn_conv_b.reshape(2, 1, 2 * DFF)
    loss, gx, gbig, gsmall = local_step(x[0], positions.reshape(L, 1), loss_target[0], sm, W)
    slots = reduce_grads(gbig, {n: w[n].shape for n in BIG})
    res = {}
    for n in BIG:
        res[n] = adam_big(w[n], m[n], v[n], slots[n], "adam_" + n)
    sslots = allreduce_small(_pack(gsmall))
    packed = adam_small(_pack({n: w[n] for n in SMALL}), _pack({n: m[n] for n in SMALL}), _pack({n: v[n] for n in SMALL}), sslots)
    small_out = [_unpack(p, {n: w[n] for n in SMALL}) for p in packed]
    for n in SMALL:
        res[n] = tuple(so[n] for so in small_out)
    total = lax.psum(loss[0, 0], ("x", "y", "c"))
    order = ("norm_mix", "norm_ffn", "norm_final", "mix_w_in", "mix_w_out", "s5_A_re", "s5_A_im", "s5_log_dt", "s5_B_re", "s5_B_im",
             "s5_C_re", "s5_C_im", "s5_D", "s5_glu_w", "s5_glu_b", "hgrn_gamma", "hgrn_norm", "att_w_qkv", "att_w_o", "ffn_w_in",
             "ffn_conv_w", "ffn_conv_b", "ffn_w_out")
    return (total, gx[None], *[res[n][0] for n in order], *[res[n][1] for n in order], *[res[n][2] for n in order],
            *[res[n][3] for n in order])
```

```python
import functools
import math

import numpy as np
import jax
import jax.numpy as jnp
from jax import lax
from jax.experimental import pallas as pl
from jax.experimental.pallas import tpu as pltpu

f32 = jnp.float32
BF = jnp.bfloat16
HI = lax.Precision.HIGHEST
S = jax.ShapeDtypeStruct
MESH = pl.DeviceIdType.MESH

L = 2048
D = 1024
EPS = 1e-6
S5W = 512
NST = 2048
HGC = 64
DFF = 2816
ROPE_THETA = 500000.0
LR, B1, B2, AEPS, WD, STEP = 0.001, 0.9, 0.999, 1e-08, 0.01, 10
VMEM_LIMIT = 56 * 1024 * 1024


def _cp(sem=None):
    return pltpu.CompilerParams(dimension_semantics=sem, vmem_limit_bytes=VMEM_LIMIT)


def _dg(a, b, ca, cb):
    return lax.dot_general(a.astype(BF), b.astype(BF), (((ca,), (cb,)), ((), ())), preferred_element_type=f32)


@jax.custom_vjp
def dot_nn(a, b):
    return _dg(a, b, 1, 0)


@jax.custom_vjp
def dot_nt(a, b):
    return _dg(a, b, 1, 1)


@jax.custom_vjp
def dot_tn(a, b):
    return _dg(a, b, 0, 0)


dot_nn.defvjp(lambda a, b: (dot_nn(a, b), (a, b)),
              lambda r, g: (dot_nt(g, r[1]).astype(r[0].dtype), dot_tn(r[0], g).astype(r[1].dtype)))
dot_nt.defvjp(lambda a, b: (dot_nt(a, b), (a, b)),
              lambda r, g: (dot_nn(g, r[1]).astype(r[0].dtype), dot_tn(g, r[0]).astype(r[1].dtype)))
dot_tn.defvjp(lambda a, b: (dot_tn(a, b), (a, b)),
              lambda r, g: (dot_nt(r[1], g).astype(r[0].dtype), dot_nn(r[0], g).astype(r[1].dtype)))


def matmul(a, b, *, mode, tm, tn, tk, out_dtype=f32, add=None, b_lead=None, a_spec=None, b_spec=None, dims=None, name):
    a_over, b_over = a_spec, b_spec
    if mode == "nn":
        (M, K), N = a.shape[-2:], b.shape[-1]
        a_spec = pl.BlockSpec((tm, tk), lambda i, j, k: (i, k))
        b_blk, b_idx, ca, cb = (tk, tn), (lambda i, j, k: (k, j)), 1, 0
    elif mode == "nt":
        (M, K), N = a.shape[-2:], b.shape[-2]
        a_spec = pl.BlockSpec((tm, tk), lambda i, j, k: (i, k))
        b_blk, b_idx, ca, cb = (tn, tk), (lambda i, j, k: (j, k)), 1, 1
    else:
        (K, M), N = a.shape[-2:], b.shape[-1]
        a_spec = pl.BlockSpec((tk, tm), lambda i, j, k: (k, i))
        b_blk, b_idx, ca, cb = (tk, tn), (lambda i, j, k: (k, j)), 0, 0
    if dims is not None:
        M, N, K = dims
    assert M % tm == 0 and N % tn == 0 and K % tk == 0, (name, M, N, K, tm, tn, tk)
    if b_lead is None:
        b_spec = pl.BlockSpec(b_blk, b_idx)
    else:
        b_spec = pl.BlockSpec((None,) + b_blk, lambda i, j, k: (b_lead,) + b_idx(i, j, k))
    if a_over is not None:
        a_spec = a_over
    if b_over is not None:
        b_spec = b_over
    nk = K // tk
    has_add = add is not None

    def body(*refs):
        a_ref, b_ref = refs[0], refs[1]
        add_ref = refs[2] if has_add else None
        o_ref = refs[2 + has_add]
        p = _dg(a_ref[...], b_ref[...], ca, cb)

        def fin(v):
            if has_add:
                v = v + add_ref[...].astype(f32)
            o_ref[...] = v.astype(o_ref.dtype)

        if nk == 1:
            fin(p)
        else:
            acc = refs[3 + has_add]
            k = pl.program_id(2)

            @pl.when(k == 0)
            def _():
                acc[...] = p

            @pl.when(k > 0)
            def _():
                acc[...] += p

            @pl.when(k == nk - 1)
            def _():
                fin(acc[...])

    in_specs = [a_spec, b_spec]
    args = [a, b]
    if has_add:
        in_specs.append(pl.BlockSpec((tm, tn), lambda i, j, k: (i, j)))
        args.append(add)
    return pl.pallas_call(
        body, grid=(M // tm, N // tn, nk), in_specs=in_specs,
        out_specs=pl.BlockSpec((tm, tn), lambda i, j, k: (i, j)),
        out_shape=S((M, N), out_dtype),
        scratch_shapes=[pltpu.VMEM((tm, tn), f32)] if nk > 1 else [],
        compiler_params=_cp(("parallel", "parallel", "arbitrary")), name=name)(*args)


def _rms(xv, gv):
    return xv * lax.rsqrt(jnp.mean(xv * xv, axis=-1, keepdims=True) + EPS) * gv


TR = 256


def rms_fwd(x, g, name):
    def body(x_ref, g_ref, o_ref):
        o_ref[...] = _rms(x_ref[...], g_ref[...]).astype(o_ref.dtype)

    return pl.pallas_call(
        body, grid=(L // TR,),
        in_specs=[pl.BlockSpec((TR, D), lambda i: (i, 0)), pl.BlockSpec((1, D), lambda i: (0, 0))],
        out_specs=pl.BlockSpec((TR, D), lambda i: (i, 0)), out_shape=S((L, D), BF),
        compiler_params=_cp(("parallel",)), name=name)(x, g)


def rms_bwd(x, g, dys, dres, name):
    nd = len(dys)

    def body(*refs):
        x_ref, g_ref = refs[0], refs[1]
        dr_ref, dh_ref, dg_ref = refs[2 + nd:]
        dy = refs[2][...].astype(f32)
        for r in refs[3:2 + nd]:
            dy = dy + r[...].astype(f32)
        _, vjp = jax.vjp(_rms, x_ref[...], g_ref[...])
        dx, dg = vjp(dy)
        dh_ref[...] = dr_ref[...] + dx

        @pl.when(pl.program_id(0) == 0)
        def _():
            dg_ref[...] = jnp.zeros_like(dg_ref)

        dg_ref[...] += dg

    row = pl.BlockSpec((TR, D), lambda i: (i, 0))
    vec = pl.BlockSpec((1, D), lambda i: (0, 0))
    return pl.pallas_call(
        body, grid=(L // TR,), in_specs=[row, vec] + [row] * (nd + 1), out_specs=[row, vec],
        out_shape=[S((L, D), f32), S((1, D), f32)],
        compiler_params=_cp(("arbitrary",)), name=name)(x, g, *dys, dres)


def loss_head(h, g, tgt):
    def f(hv, gv, tv):
        y = _rms(hv, gv)
        return 0.5 * jnp.sum(jnp.mean(jnp.square(y - tv), axis=-1))

    def body(h_ref, g_ref, t_ref, l_ref, dh_ref, dg_ref):
        val, vjp = jax.vjp(f, h_ref[...], g_ref[...], t_ref[...])
        dh, dg, _ = vjp(jnp.ones((), f32))
        dh_ref[...] = dh

        @pl.when(pl.program_id(0) == 0)
        def _():
            dg_ref[...] = jnp.zeros_like(dg_ref)
            l_ref[...] = jnp.zeros_like(l_ref)

        dg_ref[...] += dg
        l_ref[...] += jnp.full((1, 128), val, f32)

    row = pl.BlockSpec((TR, D), lambda i: (i, 0))
    vec = pl.BlockSpec((1, D), lambda i: (0, 0))
    return pl.pallas_call(
        body, grid=(L // TR,), in_specs=[row, vec, row],
        out_specs=[pl.BlockSpec((1, 128), lambda i: (0, 0)), row, vec],
        out_shape=[S((1, 128), f32), S((L, D), f32), S((1, D), f32)],
        compiler_params=_cp(("arbitrary",)), name="loss_head")(h, g, tgt)


def _col_to_row(c):
    n = c.shape[0]
    t = jnp.broadcast_to(c, (n, 128)).T
    r = lax.broadcasted_iota(jnp.int32, (128, n), 0)
    return jnp.sum(jnp.where(r == 0, t, 0.0), axis=0, keepdims=True)


def _s5_param_map(are, aim, ldt_row, bre, bim, cre, cim):
    n = NST
    gi = lax.broadcasted_iota(jnp.int32, (n, 32), 0) // 64
    gj = lax.broadcasted_iota(jnp.int32, (n, 32), 1)
    ldt = jnp.sum(jnp.where(gi == gj, ldt_row, 0.0), axis=1, keepdims=True)
    dt = jnp.exp(ldt)
    mag = jnp.exp(are * dt)
    abr = mag * jnp.cos(aim * dt)
    abi = mag * jnp.sin(aim * dt)
    den = are * are + aim * aim
    nr, ni = abr - 1.0, abi
    cr = (nr * are + ni * aim) / den
    ci = (ni * are - nr * aim) / den
    bbr = cr * bre - ci * bim
    bbi = cr * bim + ci * bre
    tc = lax.broadcasted_iota(jnp.int32, (16, 128), 0)
    tl = lax.broadcasted_iota(jnp.int32, (16, 128), 1)
    T = (tl % 16 == tc).astype(f32)
    mr = (lax.broadcasted_iota(jnp.int32, (n, 128), 0) // 64) % 8
    mc = lax.broadcasted_iota(jnp.int32, (n, 128), 1) // 16
    mask = (mr == mc).astype(f32)

    def expand(v):
        return jnp.dot(v, T, precision=HI, preferred_element_type=f32) * mask

    return expand(bbr), expand(bbi), expand(cre), expand(cim), _col_to_row(abr), _col_to_row(abi)


def s5_params_fwd(are, aim, ldt_row, bre, bim, cre, cim):
    def body(*refs):
        outs = _s5_param_map(*[r[...] for r in refs[:7]])
        for o_ref, o in zip(refs[7:], outs):
            o_ref[...] = o

    return pl.pallas_call(
        body, out_shape=[S((NST, 128), f32)] * 4 + [S((1, NST), f32)] * 2,
        compiler_params=_cp(), name="s5_params_fwd")(are, aim, ldt_row, bre, bim, cre, cim)


def s5_params_bwd(are, aim, ldt_row, bre, bim, cre, cim, cots):
    def body(*refs):
        _, vjp = jax.vjp(_s5_param_map, *[r[...] for r in refs[:7]])
        gs = vjp(tuple(r[...] for r in refs[7:13]))
        for o_ref, o in zip(refs[13:], gs):
            o_ref[...] = o

    return pl.pallas_call(
        body, out_shape=[S((NST, 1), f32)] * 2 + [S((1, 32), f32)] + [S((NST, 16), f32)] * 4,
        compiler_params=_cp(), name="s5_params_bwd")(are, aim, ldt_row, bre, bim, cre, cim, *cots)


NT5 = 4
RC = 256


def s5_scan_fwd(proj, wbr, wbi, wcr, wci, abr, abi, drow):
    def body(u_ref, wbr_ref, wbi_ref, wcr_ref, wci_ref, ar_ref, ai_ref, d_ref, xr_ref, xi_ref, y_ref):
        wbr_v, wbi_v = wbr_ref[...], wbi_ref[...]
        for r in range(L // RC):
            rows = pl.ds(r * RC, RC)
            ub = u_ref[rows, :]
            xr_ref[rows, :] = dot_nt(ub, wbr_v)
            xi_ref[rows, :] = dot_nt(ub, wbi_v)
        ar, ai = ar_ref[...], ai_ref[...]

        def step(t, c):
            cr, ci = c
            nr = ar * cr - ai * ci + xr_ref[pl.ds(t, 1), :]
            ni = ar * ci + ai * cr + xi_ref[pl.ds(t, 1), :]
            xr_ref[pl.ds(t, 1), :] = nr
            xi_ref[pl.ds(t, 1), :] = ni
            return nr, ni

        z = jnp.zeros((1, 512), f32)
        lax.fori_loop(0, L, step, (z, z), unroll=8)
        wcr_v, wci_v, dv = wcr_ref[...], wci_ref[...], d_ref[...]
        for r in range(L // RC):
            rows = pl.ds(r * RC, RC)
            y_ref[rows, :] = (dot_nn(xr_ref[rows, :], wcr_v) - dot_nn(xi_ref[rows, :], wci_v)
                              + dv * u_ref[rows, :])

    wspec = pl.BlockSpec((512, 128), lambda j: (j, 0))
    aspec = pl.BlockSpec((1, 512), lambda j: (0, j))
    return pl.pallas_call(
        body, grid=(NT5,),
        in_specs=[pl.BlockSpec((L, 128), lambda j: (0, j)), wspec, wspec, wspec, wspec, aspec, aspec,
                  pl.BlockSpec((1, 128), lambda j: (0, j))],
        out_specs=[pl.BlockSpec((L, 512), lambda j: (0, j)), pl.BlockSpec((L, 512), lambda j: (0, j)),
                   pl.BlockSpec((L, 128), lambda j: (0, j))],
        out_shape=[S((L, NST), f32), S((L, NST), f32), S((L, S5W), f32)],
        compiler_params=_cp(("parallel",)), name="s5_scan_fwd")(proj, wbr, wbi, wcr, wci, abr, abi, drow)


def s5_scan_bwd(dy, proj, xs_re, xs_im, wbr, wbi, wcr, wci, abr, abi, drow):
    def body(dy_ref, u_ref, xr_ref, xi_ref, wbr_ref, wbi_ref, wcr_ref, wci_ref, ar_ref, ai_ref, d_ref,
             du_ref, gwbr_ref, gwbi_ref, gwcr_ref, gwci_ref, gar_ref, gai_ref, gd_ref, lr_ref, li_ref):
        wcr_v, wci_v = wcr_ref[...], wci_ref[...]
        gwcr = jnp.zeros((512, 128), f32)
        gwci = jnp.zeros((512, 128), f32)
        gd = jnp.zeros((1, 128), f32)
        for r in range(L // RC):
            rows = pl.ds(r * RC, RC)
            dyv = dy_ref[rows, :]
            lr_ref[rows, :] = dot_nt(dyv, wcr_v)
            li_ref[rows, :] = -dot_nt(dyv, wci_v)
            gwcr += dot_tn(xr_ref[rows, :], dyv)
            gwci -= dot_tn(xi_ref[rows, :], dyv)
            gd += jnp.sum(dyv * u_ref[rows, :], axis=0, keepdims=True)
        gwcr_ref[...] = gwcr
        gwci_ref[...] = gwci
        gd_ref[...] = gd
        ar, ai = ar_ref[...], ai_ref[...]

        def step(i, c):
            lr, li, gar, gai = c
            t = L - 1 - i
            nr = lr_ref[pl.ds(t, 1), :] + ar * lr + ai * li
            ni = li_ref[pl.ds(t, 1), :] + ar * li - ai * lr
            lr_ref[pl.ds(t, 1), :] = nr
            li_ref[pl.ds(t, 1), :] = ni
            tp = jnp.maximum(t - 1, 0)
            live = (t > 0).astype(f32)
            xr = xr_ref[pl.ds(tp, 1), :] * live
            xi = xi_ref[pl.ds(tp, 1), :] * live
            return nr, ni, gar + xr * nr + xi * ni, gai + xr * ni - xi * nr

        z = jnp.zeros((1, 512), f32)
        _, _, gar, gai = lax.fori_loop(0, L, step, (z, z, z, z), unroll=8)
        gar_ref[...] = gar
        gai_ref[...] = gai
        wbr_v, wbi_v, dv = wbr_ref[...], wbi_ref[...], d_ref[...]
        gwbr = jnp.zeros((512, 128), f32)
        gwbi = jnp.zeros((512, 128), f32)
        for r in range(L // RC):
            rows = pl.ds(r * RC, RC)
            lrv, liv, uv = lr_ref[rows, :], li_ref[rows, :], u_ref[rows, :]
            du_ref[rows, :] = (dot_nn(lrv, wbr_v) + dot_nn(liv, wbi_v) + dv * dy_ref[rows, :]).astype(du_ref.dtype)
            gwbr += dot_tn(lrv, uv)
            gwbi += dot_tn(liv, uv)
        gwbr_ref[...] = gwbr
        gwbi_ref[...] = gwbi

    wspec = pl.BlockSpec((512, 128), lambda j: (j, 0))
    aspec = pl.BlockSpec((1, 512), lambda j: (0, j))
    col = pl.BlockSpec((L, 128), lambda j: (0, j))
    st = pl.BlockSpec((L, 512), lambda j: (0, j))
    dspec = pl.BlockSpec((1, 128), lambda j: (0, j))
    return pl.pallas_call(
        body, grid=(NT5,),
        in_specs=[col, col, st, st, wspec, wspec, wspec, wspec, aspec, aspec, dspec],
        out_specs=[col, wspec, wspec, wspec, wspec, aspec, aspec, dspec],
        out_shape=[S((L, S5W), BF)] + [S((NST, 128), f32)] * 4 + [S((1, NST), f32)] * 2 + [S((1, S5W), f32)],
        scratch_shapes=[pltpu.VMEM((L, 512), f32), pltpu.VMEM((L, 512), f32)],
        compiler_params=_cp(("parallel",)), name="s5_scan_bwd")(dy, proj, xs_re, xs_im, wbr, wbi, wcr, wci, abr, abi, drow)


def _glu(y, w, b):
    z = jax.nn.gelu(y)
    return z * jax.nn.sigmoid(dot_nn(z, w) + b)


def s5_glu_fwd(y, w, b):
    def body(y_ref, w_ref, b_ref, o_ref):
        o_ref[...] = _glu(y_ref[...], w_ref[...], b_ref[...]).astype(o_ref.dtype)

    return pl.pallas_call(
        body, grid=(L // TR,),
        in_specs=[pl.BlockSpec((TR, S5W), lambda i: (i, 0)), pl.BlockSpec((S5W, S5W), lambda i: (0, 0)),
                  pl.BlockSpec((1, S5W), lambda i: (0, 0))],
        out_specs=pl.BlockSpec((TR, S5W), lambda i: (i, 0)), out_shape=S((L, S5W), BF),
        compiler_params=_cp(("parallel",)), name="s5_glu_fwd")(y, w, b)


def s5_glu_bwd(y, w, b, dmix):
    def body(y_ref, w_ref, b_ref, g_ref, dy_ref, dw_ref, db_ref):
        _, vjp = jax.vjp(_glu, y_ref[...], w_ref[...].astype(f32), b_ref[...])
        dy, dw, db = vjp(g_ref[...])
        dy_ref[...] = dy

        @pl.when(pl.program_id(0) == 0)
        def _():
            dw_ref[...] = jnp.zeros_like(dw_ref)
            db_ref[...] = jnp.zeros_like(db_ref)

        dw_ref[...] += dw
        db_ref[...] += db

    row = pl.BlockSpec((TR, S5W), lambda i: (i, 0))
    return pl.pallas_call(
        body, grid=(L // TR,),
        in_specs=[row, pl.BlockSpec((S5W, S5W), lambda i: (0, 0)), pl.BlockSpec((1, S5W), lambda i: (0, 0)), row],
        out_specs=[row, pl.BlockSpec((S5W, S5W), lambda i: (0, 0)), pl.BlockSpec((1, S5W), lambda i: (0, 0))],
        out_shape=[S((L, S5W), f32), S((S5W, S5W), f32), S((1, S5W), f32)],
        compiler_params=_cp(("arbitrary",)), name="s5_glu_bwd")(y, w, b, dmix)


def _hi(a, b, ca, cb):
    return lax.dot_general(a, b, (((ca,), (cb,)), ((), ())), precision=HI, preferred_element_type=f32)


def _hgrn_chunk(St, xq, xf, xi, xg, gam, ng):
    lb = jax.nn.sigmoid(gam[0:1] - gam[1:2])
    q = jax.nn.silu(xq)
    f = lb + (1.0 - lb) * jax.nn.sigmoid(xf)
    k = 1.0 - f
    g = jnp.log(f)
    ti = lax.broadcasted_iota(jnp.int32, (HGC, HGC), 0)
    si = lax.broadcasted_iota(jnp.int32, (HGC, HGC), 1)
    causal = si <= ti
    b = jnp.dot(causal.astype(f32), g, precision=HI, preferred_element_type=f32)
    qe = q * jnp.exp(b)
    o = _hi(qe, St, 1, 1)
    att = jnp.where(causal, _hi(qe, k * jnp.exp(-b), 1, 1), 0.0)
    o = o + _hi(att, xi, 1, 0)
    bl = b[HGC - 1:HGC]
    St_new = St * jnp.exp(bl) + _hi(xi, k * jnp.exp(bl - b), 0, 0)
    o = o * lax.rsqrt(jnp.mean(o * o, axis=-1, keepdims=True) + EPS) * ng
    return St_new, o * jax.nn.silu(xg)


NCH = L // HGC


def hgrn_fwd(proj, gamma, hnorm):
    def body(q_ref, f_ref, i_ref, g_ref, gam_ref, ng_ref, o_ref, ss_ref, st):
        @pl.when(pl.program_id(0) == 0)
        def _():
            st[...] = jnp.zeros_like(st)

        for h in range(4):
            sl = slice(h * 128, (h + 1) * 128)
            s0 = st[h]
            ss_ref[0, h] = s0
            s1, o = _hgrn_chunk(s0, q_ref[:, sl], f_ref[:, sl], i_ref[:, sl], g_ref[:, sl], gam_ref[:, sl], ng_ref[:, sl])
            st[h] = s1
            o_ref[:, sl] = o.astype(o_ref.dtype)

    def pj(n):
        return pl.BlockSpec((HGC, 512), lambda c: (c, n))

    return pl.pallas_call(
        body, grid=(NCH,),
        in_specs=[pj(1), pj(2), pj(3), pj(4), pl.BlockSpec((2, 512), lambda c: (0, 0)), pl.BlockSpec((1, 512), lambda c: (0, 0))],
        out_specs=[pl.BlockSpec((HGC, 512), lambda c: (c, 0)), pl.BlockSpec((1, 4, 128, 128), lambda c: (c, 0, 0, 0))],
        out_shape=[S((L, 512), BF), S((NCH, 4, 128, 128), f32)],
        scratch_shapes=[pltpu.VMEM((4, 128, 128), f32)],
        compiler_params=_cp(("arbitrary",)), name="hgrn_fwd")(proj, proj, proj, proj, gamma, hnorm)


def hgrn_bwd(proj, gamma, hnorm, ssave, dmix, du):
    def body(q_ref, f_ref, i_ref, g_ref, gam_ref, ng_ref, ss_ref, do_ref, du_ref, dp_ref, dgam_ref, dng_ref, dst):
        @pl.when(pl.program_id(0) == 0)
        def _():
            dst[...] = jnp.zeros_like(dst)
            dgam_ref[...] = jnp.zeros_like(dgam_ref)
            dng_ref[...] = jnp.zeros_like(dng_ref)

        dp_ref[:, 0:512] = du_ref[...]
        for h in range(4):
            sl = slice(h * 128, (h + 1) * 128)
            _, vjp = jax.vjp(_hgrn_chunk, ss_ref[0, h], q_ref[:, sl], f_ref[:, sl], i_ref[:, sl], g_ref[:, sl],
                             gam_ref[:, sl], ng_ref[:, sl])
            ds, dq, df, di, dg, dgam, dng = vjp((dst[h], do_ref[:, sl]))
            dst[h] = ds
            for n, v in enumerate((dq, df, di, dg)):
                dp_ref[:, 512 * (n + 1) + h * 128: 512 * (n + 1) + (h + 1) * 128] = v.astype(dp_ref.dtype)
            dgam_ref[:, sl] += dgam
            dng_ref[:, sl] += dng

    def pj(n):
        return pl.BlockSpec((HGC, 512), lambda i: (NCH - 1 - i, n))

    return pl.pallas_call(
        body, grid=(NCH,),
        in_specs=[pj(1), pj(2), pj(3), pj(4), pl.BlockSpec((2, 512), lambda i: (0, 0)), pl.BlockSpec((1, 512), lambda i: (0, 0)),
                  pl.BlockSpec((1, 4, 128, 128), lambda i: (NCH - 1 - i, 0, 0, 0)), pj(1), pj(0)],
        out_specs=[pl.BlockSpec((HGC, 2560), lambda i: (NCH - 1 - i, 0)), pl.BlockSpec((2, 512), lambda i: (0, 0)),
                   pl.BlockSpec((1, 512), lambda i: (0, 0))],
        out_shape=[S((L, 2560), BF), S((2, 512), f32), S((1, 512), f32)],
        scratch_shapes=[pltpu.VMEM((4, 128, 128), f32)],
        compiler_params=_cp(("arbitrary",)), name="hgrn_bwd")(proj, proj, proj, proj, gamma, hnorm, ssave, dmix, du)


def _shift(x, k):
    return jnp.concatenate([jnp.zeros((k, x.shape[1]), x.dtype), x[:-k]], axis=0)


def _convact(ha, hb, wa, wb, ba, bb):
    ca = wa[2:3] * ha + wa[1:2] * _shift(ha, 1) + wa[0:1] * _shift(ha, 2) + ba
    cb = wb[2:3] * hb + wb[1:2] * _shift(hb, 1) + wb[0:1] * _shift(hb, 2) + bb
    return jax.nn.silu(ca) * cb


CT = 128
NCT = DFF // CT


def convact_fwd(hu, cw, cb, layer):
    def body(ha_ref, hb_ref, wa_ref, wb_ref, ba_ref, bb_ref, o_ref):
        o_ref[...] = _convact(ha_ref[...], hb_ref[...], wa_ref[...], wb_ref[...], ba_ref[...], bb_ref[...]).astype(o_ref.dtype)

    def h(off):
        return pl.BlockSpec((L, CT), lambda j: (0, j + off))

    def w(off):
        return pl.BlockSpec((None, 3, CT), lambda j: (layer, 0, j + off))

    def b(off):
        return pl.BlockSpec((None, 1, CT), lambda j: (layer, 0, j + off))

    return pl.pallas_call(
        body, grid=(NCT,), in_specs=[h(0), h(NCT), w(0), w(NCT), b(0), b(NCT)],
        out_specs=pl.BlockSpec((L, CT), lambda j: (0, j)), out_shape=S((L, DFF), BF),
        compiler_params=_cp(("parallel",)), name=f"convact_fwd{layer}")(hu, hu, cw, cw, cb, cb)


def convact_bwd(hu, cw, cb, dact, layer):
    def body(ha_ref, hb_ref, wa_ref, wb_ref, ba_ref, bb_ref, g_ref, dh_ref, dw_ref, db_ref, sh, sw, sb):
        j = pl.program_id(0)

        @pl.when(j < NCT)
        def _():
            _, vjp = jax.vjp(_convact, ha_ref[...], hb_ref[...], wa_ref[...], wb_ref[...], ba_ref[...], bb_ref[...])
            dha, dhb, dwa, dwb, dba, dbb = vjp(g_ref[...].astype(f32))
            dh_ref[...] = dha.astype(dh_ref.dtype)
            dw_ref[...] = dwa
            db_ref[...] = dba
            sh[j] = dhb.astype(sh.dtype)
            sw[j] = dwb
            sb[j] = dbb

        @pl.when(j >= NCT)
        def _():
            dh_ref[...] = sh[j - NCT]
            dw_ref[...] = sw[j - NCT]
            db_ref[...] = sb[j - NCT]

    def lo(j):
        return jnp.minimum(j, NCT - 1)

    in_specs = [pl.BlockSpec((L, CT), lambda j: (0, lo(j))), pl.BlockSpec((L, CT), lambda j: (0, lo(j) + NCT)),
                pl.BlockSpec((None, 3, CT), lambda j: (layer, 0, lo(j))), pl.BlockSpec((None, 3, CT), lambda j: (layer, 0, lo(j) + NCT)),
                pl.BlockSpec((None, 1, CT), lambda j: (layer, 0, lo(j))), pl.BlockSpec((None, 1, CT), lambda j: (layer, 0, lo(j) + NCT)),
                pl.BlockSpec((L, CT), lambda j: (0, lo(j)))]
    return pl.pallas_call(
        body, grid=(2 * NCT,), in_specs=in_specs,
        out_specs=[pl.BlockSpec((L, CT), lambda j: (0, j)), pl.BlockSpec((3, CT), lambda j: (0, j)), pl.BlockSpec((1, CT), lambda j: (0, j))],
        out_shape=[S((L, 2 * DFF), BF), S((3, 2 * DFF), f32), S((1, 2 * DFF), f32)],
        scratch_shapes=[pltpu.VMEM((NCT, L, CT), BF), pltpu.VMEM((NCT, 3, CT), f32), pltpu.VMEM((NCT, 1, CT), f32)],
        compiler_params=_cp(("arbitrary",)), name=f"convact_bwd{layer}")(hu, hu, cw, cw, cb, cb, dact)


DILS = (1, 4, 16)
AB = 128
NPAIR = 12


def _rope_tables(pos_ref, invf_ref):
    ang = pos_ref[...].astype(f32) * invf_ref[...]
    lane = lax.broadcasted_iota(jnp.int32, (1, 128), 1) % 64
    cosf = jnp.where(lane < 16, jnp.cos(ang), 1.0)
    sn = jnp.sin(ang)
    s_lo = jnp.where(lane < 8, -sn, 0.0)
    s_hi = jnp.where((lane >= 8) & (lane < 16), sn, 0.0)
    return cosf, s_lo, s_hi


def _rope(t, cosf, s_lo, s_hi):
    return t * cosf + pltpu.roll(t, 120, 1) * s_lo + pltpu.roll(t, 8, 1) * s_hi


def _rope_t(g, cosf, s_lo, s_hi):
    return g * cosf + pltpu.roll(g * s_lo, 8, 1) + pltpu.roll(g * s_hi, 120, 1)


def _att_block(q2, kp, kc, vp, vc, first):
    lane = lax.broadcasted_iota(jnp.int32, (1, 128), 1)
    qi = lax.broadcasted_iota(jnp.int32, (AB, 2 * AB), 0) + AB
    kj = lax.broadcasted_iota(jnp.int32, (AB, 2 * AB), 1)
    back = qi - kj
    valid = (back >= 0) & (back <= AB)
    if first:
        valid = valid & (kj >= AB)
    kk = jnp.concatenate([kp, kc], axis=0)
    vv = jnp.concatenate([vp, vc], axis=0)
    o2 = jnp.zeros((AB, 128), f32)
    lse2 = jnp.zeros((AB, 128), f32)
    for e in range(2):
        hm = ((lane >= 64 * e) & (lane < 64 * (e + 1))).astype(f32)
        s = dot_nt(q2 * (hm * 0.125), kk)
        s = jnp.where(valid, s, -jnp.inf)
        m = jnp.max(s, axis=-1, keepdims=True)
        p = jnp.exp(s - m)
        den = jnp.sum(p, axis=-1, keepdims=True)
        o2 = o2 + dot_nn(p, vv * hm) / den
        lse2 = lse2 + (m + jnp.log(den)) * hm
    return o2, lse2


def _att_blocks(dil):
    m = L // dil
    return [(r * m + n * AB, n == 0) for r in range(dil) for n in range(m // AB)]


def deinterleave(x, dil):
    return x if dil == 1 else x.reshape(L // dil, dil, x.shape[1]).swapaxes(0, 1).reshape(L, x.shape[1])


def interleave(x, dil):
    return x if dil == 1 else x.reshape(dil, L // dil, x.shape[1]).swapaxes(0, 1).reshape(L, x.shape[1])


def attn_fwd(qkv, pos, invf, g):
    blocks = _att_blocks(DILS[g])

    def body(q_ref, k_ref, v_ref, pos_ref, invf_ref, o_ref, l_ref, qr, kr):
        cosf, s_lo, s_hi = _rope_tables(pos_ref, invf_ref)
        qr[...] = _rope(q_ref[...], cosf, s_lo, s_hi)
        kr[...] = _rope(k_ref[...], cosf, s_lo, s_hi)
        for off, first in blocks:
            cur, prv = pl.ds(off, AB), pl.ds(off if first else off - AB, AB)
            o2, lse2 = _att_block(qr[cur, :], kr[prv, :], kr[cur, :], v_ref[prv, :], v_ref[cur, :], first)
            o_ref[cur, :] = o2
            l_ref[cur, :] = lse2

    def sec(n):
        return pl.BlockSpec((L, 128), lambda p: (0, p + 4 * n))

    return pl.pallas_call(
        body, grid=(4,),
        in_specs=[sec(0), sec(1), sec(2), pl.BlockSpec((L, 1), lambda p: (0, 0)), pl.BlockSpec((1, 128), lambda p: (0, 0))],
        out_specs=[sec(0), sec(0)], out_shape=[S((L, 512), f32), S((L, 512), f32)],
        scratch_shapes=[pltpu.VMEM((L, 128), f32), pltpu.VMEM((L, 128), f32)],
        compiler_params=_cp(("parallel",)), name=f"attn_fwd{g}")(qkv, qkv, qkv, pos, invf)


def attn_bwd(qkv, pos, invf, do, dl, g):
    blocks = _att_blocks(DILS[g])

    def body(q_ref, k_ref, v_ref, pos_ref, invf_ref, do_ref, dl_ref, d_ref, qr, kr, dqr, dkr, dvr):
        cosf, s_lo, s_hi = _rope_tables(pos_ref, invf_ref)
        qr[...] = _rope(q_ref[...], cosf, s_lo, s_hi)
        kr[...] = _rope(k_ref[...], cosf, s_lo, s_hi)
        for off, first in blocks:
            cur, prv = pl.ds(off, AB), pl.ds(off if first else off - AB, AB)
            fn = functools.partial(_att_block, first=first)
            _, vjp = jax.vjp(fn, qr[cur, :], kr[prv, :], kr[cur, :], v_ref[prv, :], v_ref[cur, :])
            dq2, dkp, dkc, dvp, dvc = vjp((do_ref[cur, :], dl_ref[cur, :]))
            dqr[cur, :] = dq2
            dkr[cur, :] = dkc
            dvr[cur, :] = dvc
            if not first:
                dkr[prv, :] += dkp
                dvr[prv, :] += dvp
        d_ref[0] = _rope_t(dqr[...], cosf, s_lo, s_hi).astype(d_ref.dtype)
        d_ref[1] = _rope_t(dkr[...], cosf, s_lo, s_hi).astype(d_ref.dtype)
        d_ref[2] = dvr[...].astype(d_ref.dtype)

    def sec(n):
        return pl.BlockSpec((L, 128), lambda p: (0, p + 4 * n))

    return pl.pallas_call(
        body, grid=(4,),
        in_specs=[sec(0), sec(1), sec(2), pl.BlockSpec((L, 1), lambda p: (0, 0)), pl.BlockSpec((1, 128), lambda p: (0, 0)),
                  sec(0), sec(0)],
        out_specs=pl.BlockSpec((3, L, 128), lambda p: (0, 0, p)), out_shape=S((3, L, 512), BF),
        scratch_shapes=[pltpu.VMEM((L, 128), f32)] * 5,
        compiler_params=_cp(("parallel",)), name=f"attn_bwd{g}")(qkv, qkv, qkv, pos, invf, do, dl)


def _merge(o0, o1, o2, l0, l1, l2):
    m = jnp.maximum(jnp.maximum(l0, l1), l2)
    e0, e1, e2 = jnp.exp(l0 - m), jnp.exp(l1 - m), jnp.exp(l2 - m)
    return (e0 * o0 + e1 * o1 + e2 * o2) / (e0 + e1 + e2)


def attn_merge_fwd(os_, ls_):
    def body(o0, o1, o2, l0, l1, l2, o_ref):
        o_ref[...] = _merge(o0[...], o1[...], o2[...], l0[...], l1[...], l2[...]).astype(o_ref.dtype)

    blk = pl.BlockSpec((TR, 512), lambda i: (i, 0))
    return pl.pallas_call(
        body, grid=(L // TR,), in_specs=[blk] * 6, out_specs=blk, out_shape=S((L, 512), BF),
        compiler_params=_cp(("parallel",)), name="attn_merge_fwd")(*os_, *ls_)


def attn_merge_bwd(os_, ls_, do):
    def body(o0, o1, o2, l0, l1, l2, g_ref, *outs):
        _, vjp = jax.vjp(_merge, o0[...], o1[...], o2[...], l0[...], l1[...], l2[...])
        for o_ref, v in zip(outs, vjp(g_ref[...])):
            o_ref[...] = v

    blk = pl.BlockSpec((TR, 512), lambda i: (i, 0))
    outs = pl.pallas_call(
        body, grid=(L // TR,), in_specs=[blk] * 7, out_specs=[blk] * 6, out_shape=[S((L, 512), f32)] * 6,
        compiler_params=_cp(("parallel",)), name="attn_merge_bwd")(*os_, *ls_, do)
    return outs[:3], outs[3:]


def _invf_lanes():
    half = 8
    inv = ROPE_THETA ** (-np.arange(half, dtype=np.float32) * 2.0 / 16.0)
    lane = np.arange(128) % 64
    return jnp.asarray(np.where(lane < 16, inv[lane % 8], 0.0).astype(np.float32)[None, :])


def _ffn_fwd(h, g_row, w_in, w_out, cw, cb, layer):
    hn = rms_fwd(h, g_row, f"rms_ffn{layer}")
    hu = matmul(hn, w_in, mode="nn", tm=1024, tn=1408, tk=1024, b_lead=layer, name=f"ffn_in{layer}")
    act = convact_fwd(hu, cw, cb, layer)
    h2 = matmul(act, w_out, mode="nn", tm=1024, tn=1024, tk=1408, add=h, b_lead=layer, name=f"ffn_out{layer}")
    return h2, (hn, hu, act)


def _ffn_bwd(dh, h, g_row, w_in, w_out, cw, cb, saved, layer):
    hn, hu, act = saved
    dact = matmul(dh, w_out, mode="nt", tm=1024, tn=1408, tk=1024, b_lead=layer, name=f"ffn_out_dx{layer}")
    g_wout = matmul(act, dh, mode="tn", tm=1408, tn=1024, tk=512, out_dtype=BF, name=f"ffn_out_dw{layer}")
    dhu, g_cw, g_cb = convact_bwd(hu, cw, cb, dact, layer)
    dhn = matmul(dhu, w_in, mode="nt", tm=1024, tn=1024, tk=1408, b_lead=layer, name=f"ffn_in_dx{layer}")
    g_win = matmul(hn, dhu, mode="tn", tm=1024, tn=1408, tk=512, out_dtype=BF, name=f"ffn_in_dw{layer}")
    dh2, g_norm = rms_bwd(h, g_row, [dhn], dh, f"rms_ffn_bwd{layer}")
    return dh2, g_win, g_wout, g_cw, g_cb, g_norm


def local_step(x, pos, tgt, sm, W):
    nm, nf = sm["norm_mix"], sm["norm_ffn"]
    invf = _invf_lanes()
    are = sm["s5_A_re"].reshape(NST, 1)
    aim = sm["s5_A_im"].reshape(NST, 1)
    ldt = sm["s5_log_dt"].reshape(1, 32)
    bre = sm["s5_B_re"].reshape(NST, 16)
    bim = sm["s5_B_im"].reshape(NST, 16)
    cre = jnp.swapaxes(sm["s5_C_re"][0], 1, 2).reshape(NST, 16)
    cim = jnp.swapaxes(sm["s5_C_im"][0], 1, 2).reshape(NST, 16)
    drow = sm["s5_D"].reshape(1, S5W)
    wbr, wbi, wcr, wci, abr, abi = s5_params_fwd(are, aim, ldt, bre, bim, cre, cim)
    hn0 = rms_fwd(x, nm[0:1], "rms_mix0")
    proj = matmul(hn0, W["mix_w_in"], mode="nn", tm=1024, tn=1280, tk=1024, b_lead=0, name="mix_in")
    xs_re, xs_im, y5 = s5_scan_fwd(proj, wbr, wbi, wcr, wci, abr, abi, drow)
    oa = s5_glu_fwd(y5, W["s5_glu_w"][0], sm["s5_glu_b"])
    ob, ssave = hgrn_fwd(proj, sm["hgrn_gamma"], sm["hgrn_norm"])
    cat = jnp.concatenate([oa, ob], axis=1)
    h1 = matmul(cat, W["mix_w_out"], mode="nn", tm=1024, tn=1024, tk=1024, add=x, b_lead=0, name="mix_out")
    h2, ffn0 = _ffn_fwd(h1, nf[0:1], W["ffn_w_in"], W["ffn_w_out"], W["ffn_conv_w"], sm["ffn_conv_b3"], 0)
    hn2 = rms_fwd(h2, nm[1:2], "rms_mix1")
    wqkv = W["att_w_qkv"]
    hn2_g, pos_g, qkv_g, o_g, l_g = [], [], [], [], []
    for g, dil in enumerate(DILS):
        hn2_g.append(deinterleave(hn2, dil))
        pos_g.append(deinterleave(pos, dil))
        qkv_g.append(matmul(hn2_g[g], wqkv, mode="nn", tm=1024, tn=512, tk=1024, dims=(L, 1536, D),
                            b_spec=pl.BlockSpec((None, D, 512), lambda i, j, k, g=g: (0, 0, 3 * j + g)), name=f"att_qkv{g}"))
        o_c, l_c = attn_fwd(qkv_g[g], pos_g[g], invf, g)
        o_g.append(interleave(o_c, dil))
        l_g.append(interleave(l_c, dil))
    o = attn_merge_fwd(o_g, l_g)
    h3 = matmul(o, W["att_w_o"], mode="nn", tm=1024, tn=1024, tk=512, add=h2, b_lead=0, name="att_o")
    h4, ffn1 = _ffn_fwd(h3, nf[1:2], W["ffn_w_in"], W["ffn_w_out"], W["ffn_conv_w"], sm["ffn_conv_b3"], 1)
    loss, dh, g_nfinal = loss_head(h4, sm["norm_final"].reshape(1, D), tgt)
    dh, g_win1, g_wout1, g_cw1, g_cb1, g_nf1 = _ffn_bwd(dh, h3, nf[1:2], W["ffn_w_in"], W["ffn_w_out"], W["ffn_conv_w"],
                                                       sm["ffn_conv_b3"], ffn1, 1)
    do = matmul(dh, W["att_w_o"], mode="nt", tm=1024, tn=512, tk=1024, b_lead=0, name="att_o_dx")
    g_wo = matmul(o, dh, mode="tn", tm=512, tn=1024, tk=512, out_dtype=BF, name="att_o_dw")
    do_g, dl_g = attn_merge_bwd(o_g, l_g, do)
    dhn2_g, gq = [], []
    for g, dil in enumerate(DILS):
        d3 = attn_bwd(qkv_g[g], pos_g[g], invf, deinterleave(do_g[g], dil), deinterleave(dl_g[g], dil), g)
        dx = matmul(d3, wqkv, mode="nt", tm=1024, tn=1024, tk=512, dims=(L, D, 1536),
                    a_spec=pl.BlockSpec((None, 1024, 512), lambda i, j, k: (k, i, 0)),
                    b_spec=pl.BlockSpec((None, D, 512), lambda i, j, k, g=g: (0, 0, 3 * k + g)), name=f"att_qkv_dx{g}")
        dhn2_g.append(interleave(dx, dil))
        gq.append(matmul(hn2_g[g], d3, mode="tn", tm=1024, tn=512, tk=512, out_dtype=BF, dims=(D, 1536, L),
                         b_spec=pl.BlockSpec((None, 512, 512), lambda i, j, k: (j, k, 0)), name=f"att_qkv_dw{g}"))
    g_wqkv = jnp.concatenate([gq[g][:, 512 * s:512 * (s + 1)] for s in range(3) for g in range(3)], axis=1)
    dh, g_nm1 = rms_bwd(h2, nm[1:2], dhn2_g, dh, "rms_mix_bwd1")
    dh, g_win0, g_wout0, g_cw0, g_cb0, g_nf0 = _ffn_bwd(dh, h1, nf[0:1], W["ffn_w_in"], W["ffn_w_out"], W["ffn_conv_w"],
                                                       sm["ffn_conv_b3"], ffn0, 0)
    dmix = matmul(dh, W["mix_w_out"], mode="nt", tm=1024, tn=1024, tk=1024, b_lead=0, name="mix_out_dx")
    g_wmo = matmul(cat, dh, mode="tn", tm=1024, tn=1024, tk=512, out_dtype=BF, name="mix_out_dw")
    dy5, g_glu_w, g_glu_b = s5_glu_bwd(y5, W["s5_glu_w"][0], sm["s5_glu_b"], dmix)
    du, gwbr, gwbi, gwcr, gwci, gabr, gabi, g_d = s5_scan_bwd(dy5, proj, xs_re, xs_im, wbr, wbi, wcr, wci, abr, abi, drow)
    g_are, g_aim, g_ldt, g_bre, g_bim, g_cre, g_cim = s5_params_bwd(are, aim, ldt, bre, bim, cre, cim,
                                                                   (gwbr, gwbi, gwcr, gwci, gabr, gabi))
    dproj, g_gamma, g_hnorm = hgrn_bwd(proj, sm["hgrn_gamma"], sm["hgrn_norm"], ssave, dmix, du)
    dhn0 = matmul(dproj, W["mix_w_in"], mode="nt", tm=1024, tn=1024, tk=1280, b_lead=0, name="mix_in_dx")
    g_wmi = matmul(hn0, dproj, mode="tn", tm=1024, tn=1280, tk=512, out_dtype=BF, name="mix_in_dw")
    gx, g_nm0 = rms_bwd(x, nm[0:1], [dhn0], dh, "rms_mix_bwd0")
    big = {
        "mix_w_in": [g_wmi], "mix_w_out": [g_wmo], "s5_glu_w": [g_glu_w.astype(BF)], "att_w_qkv": [g_wqkv], "att_w_o": [g_wo],
        "ffn_w_in": [g_win0, g_win1], "ffn_w_out": [g_wout0, g_wout1], "ffn_conv_w": [g_cw0, g_cw1],
    }
    small = {
        "norm_mix": jnp.concatenate([g_nm0, g_nm1], axis=0), "norm_ffn": jnp.concatenate([g_nf0, g_nf1], axis=0),
        "norm_final": g_nfinal.reshape(D),
        "s5_A_re": g_are.reshape(1, 32, 64), "s5_A_im": g_aim.reshape(1, 32, 64), "s5_log_dt": g_ldt.reshape(1, 32),
        "s5_B_re": g_bre.reshape(1, 32, 64, 16), "s5_B_im": g_bim.reshape(1, 32, 64, 16),
        "s5_C_re": jnp.swapaxes(g_cre.reshape(1, 32, 64, 16), 2, 3), "s5_C_im": jnp.swapaxes(g_cim.reshape(1, 32, 64, 16), 2, 3),
        "s5_D": g_d.reshape(1, 32, 16), "s5_glu_b": g_glu_b, "hgrn_gamma": g_gamma, "hgrn_norm": g_hnorm,
        "ffn_conv_b": jnp.concatenate([g_cb0, g_cb1], axis=0),
    }
    return loss, gx, big, small


BIG = ("mix_w_in", "mix_w_out", "s5_glu_w", "att_w_qkv", "att_w_o", "ffn_w_in", "ffn_w_out", "ffn_conv_w")
ROW_SHARDED = ("mix_w_out", "s5_glu_w", "ffn_w_out")
SMALL = ("norm_mix", "norm_ffn", "norm_final", "s5_A_re", "s5_A_im", "s5_log_dt", "s5_B_re", "s5_B_im", "s5_C_re", "s5_C_im",
         "s5_D", "s5_glu_b", "hgrn_gamma", "hgrn_norm", "ffn_conv_b")
ANY = pl.BlockSpec(memory_space=pl.ANY)


def _chip_slice(ref, q, rows, n):
    start = pl.multiple_of(q * n, n)
    if rows:
        return ref.at[:, pl.ds(start, n), :]
    return ref.at[:, :, pl.ds(start, n)]


def cast_bf16(w, name):
    nl, r, c = w.shape
    w2 = w.reshape(nl * r, c)
    tr = 256 if (nl * r) % 256 == 0 else nl * r

    def body(w_ref, o_ref):
        o_ref[...] = w_ref[...].astype(BF)

    out = pl.pallas_call(
        body, grid=(nl * r // tr,), in_specs=[pl.BlockSpec((tr, c), lambda i: (i, 0))],
        out_specs=pl.BlockSpec((tr, c), lambda i: (i, 0)), out_shape=S((nl * r, c), BF),
        compiler_params=_cp(("parallel",)), name=name)(w2)
    return out.reshape(nl, r, c)


def gather_weights(shards):
    names = list(shards)
    nw = len(names)
    out_shape = []
    for nme in names:
        nl, r, c = shards[nme].shape
        out_shape.append(S((nl, 4 * r, c) if nme in ROW_SHARDED else (nl, r, 4 * c), shards[nme].dtype))

    def body(*refs):
        src, dst = refs[:nw], refs[nw:2 * nw]
        send_sems, recv_sems, loc_sems = refs[2 * nw:]
        x, y, c = lax.axis_index("x"), lax.axis_index("y"), lax.axis_index("c")
        me = 2 * x + y
        others = [(1 - x, y), (x, 1 - y), (1 - x, 1 - y)]

        def part(w, q):
            rows = names[w] in ROW_SHARDED
            return _chip_slice(dst[w], q, rows, src[w].shape[1] if rows else src[w].shape[2])

        def remote(w, k, q):
            px, py = others[k]
            return pltpu.make_async_remote_copy(src_ref=src[w], dst_ref=part(w, q), send_sem=send_sems.at[w, k],
                                                recv_sem=recv_sems.at[w, k], device_id=(px, py, c), device_id_type=MESH)

        local = [pltpu.make_async_copy(src[w], part(w, me), loc_sems.at[w]) for w in range(nw)]
        for w in range(nw):
            local[w].start()
            for k in range(3):
                remote(w, k, me).start()
        for w in range(nw):
            for k in range(3):
                px, py = others[k]
                remote(w, k, 2 * px + py).wait_recv()
        for w in range(nw):
            for k in range(3):
                remote(w, k, me).wait_send()
            local[w].wait()

    outs = pl.pallas_call(
        body, in_specs=[ANY] * nw, out_specs=[ANY] * nw, out_shape=out_shape,
        scratch_shapes=[pltpu.SemaphoreType.DMA((nw, 3)), pltpu.SemaphoreType.DMA((nw, 3)), pltpu.SemaphoreType.DMA((nw,))],
        compiler_params=pltpu.CompilerParams(has_side_effects=True), name="gather_weights")(*[shards[n] for n in names])
    return dict(zip(names, outs))


def reduce_grads(grads, shard_shapes):
    names = list(grads)
    flat = [(nme, l) for nme in names for l in range(len(grads[nme]))]
    ne = len(flat)
    nw = len(names)
    out_shape = [S((8,) + tuple(shard_shapes[nme]), grads[nme][0].dtype) for nme in names]

    def body(*refs):
        src, dst = refs[:ne], refs[ne:ne + nw]
        send_sems, recv_sems, loc_sems, fsend_sems, frecv_sems = refs[ne + nw:]
        x, y, c = lax.axis_index("x"), lax.axis_index("y"), lax.axis_index("c")
        me = 2 * x + y
        others = [(1 - x, y), (x, 1 - y), (1 - x, 1 - y)]
        sib = (x, y, 1 - c)

        def piece(e, q):
            nme, l = flat[e]
            rows = nme in ROW_SHARDED
            shp = shard_shapes[nme]
            if rows:
                return src[e].at[pl.ds(pl.multiple_of(q * shp[1], shp[1]), shp[1]), :]
            return src[e].at[:, pl.ds(pl.multiple_of(q * shp[2], shp[2]), shp[2])]

        def slot(e, s):
            nme, l = flat[e]
            return dst[names.index(nme)].at[s, l]

        def remote(e, k, q_to, q_from, core):
            px, py = others[k]
            return pltpu.make_async_remote_copy(src_ref=piece(e, q_to), dst_ref=slot(e, 2 * q_from + core),
                                                send_sem=send_sems.at[e, k], recv_sem=recv_sems.at[e, k],
                                                device_id=(px, py, c), device_id_type=MESH)

        def forward(e, j, q, core):
            return pltpu.make_async_remote_copy(src_ref=slot(e, 2 * q + core), dst_ref=slot(e, 2 * q + core),
                                                send_sem=fsend_sems.at[e, j], recv_sem=frecv_sems.at[e, j],
                                                device_id=sib, device_id_type=MESH)

        local = [pltpu.make_async_copy(piece(e, me), slot(e, 2 * me + c), loc_sems.at[e]) for e in range(ne)]
        for e in range(ne):
            local[e].start()
            for k in range(3):
                px, py = others[k]
                remote(e, k, 2 * px + py, me, c).start()
        for e in range(ne):
            local[e].wait()
            forward(e, 0, me, c).start()
        for e in range(ne):
            for k in range(3):
                px, py = others[k]
                q = 2 * px + py
                remote(e, k, me, q, c).wait_recv()
                forward(e, 1 + k, q, c).start()
        for e in range(ne):
            forward(e, 0, me, 1 - c).wait_recv()
            for k in range(3):
                px, py = others[k]
                forward(e, 1 + k, 2 * px + py, 1 - c).wait_recv()
        for e in range(ne):
            forward(e, 0, me, c).wait_send()
            for k in range(3):
                px, py = others[k]
                remote(e, k, 2 * px + py, me, c).wait_send()
                forward(e, 1 + k, 2 * px + py, c).wait_send()

    outs = pl.pallas_call(
        body, in_specs=[ANY] * ne, out_specs=[ANY] * nw, out_shape=out_shape,
        scratch_shapes=[pltpu.SemaphoreType.DMA((ne, 3)), pltpu.SemaphoreType.DMA((ne, 3)), pltpu.SemaphoreType.DMA((ne,)),
                        pltpu.SemaphoreType.DMA((ne, 4)), pltpu.SemaphoreType.DMA((ne, 4))],
        compiler_params=pltpu.CompilerParams(has_side_effects=True), name="reduce_grads")(*[grads[n][l] for n, l in flat])
    return dict(zip(names, outs))


def allreduce_small(g):
    R = g.shape[0]

    def body(g_ref, o_ref, send_sems, recv_sems, loc_sem):
        x, y, c = lax.axis_index("x"), lax.axis_index("y"), lax.axis_index("c")
        me = 4 * x + 2 * y + c
        peers = []
        for m in range(1, 8):
            fx, fy, fc = (m >> 2) & 1, (m >> 1) & 1, m & 1
            peers.append((jnp.where(fx, 1 - x, x), jnp.where(fy, 1 - y, y), jnp.where(fc, 1 - c, c)))

        def cp(k, s):
            return pltpu.make_async_remote_copy(src_ref=g_ref, dst_ref=o_ref.at[s], send_sem=send_sems.at[k],
                                                recv_sem=recv_sems.at[k], device_id=peers[k], device_id_type=MESH)

        loc = pltpu.make_async_copy(g_ref, o_ref.at[me], loc_sem)
        loc.start()
        for k in range(7):
            cp(k, me).start()
        for k in range(7):
            px, py, pc = peers[k]
            cp(k, 4 * px + 2 * py + pc).wait_recv()
        for k in range(7):
            cp(k, me).wait_send()
        loc.wait()

    return pl.pallas_call(
        body, in_specs=[ANY], out_specs=ANY, out_shape=S((8, R, 128), f32),
        scratch_shapes=[pltpu.SemaphoreType.DMA((7,)), pltpu.SemaphoreType.DMA((7,)), pltpu.SemaphoreType.DMA],
        compiler_params=pltpu.CompilerParams(has_side_effects=True), name="allreduce_small")(g)


def _adamw(w, g, m, v):
    m = B1 * m + (1.0 - B1) * g
    v = B2 * v + (1.0 - B2) * jnp.square(g)
    m_hat = m / (1.0 - B1 ** STEP)
    v_hat = v / (1.0 - B2 ** STEP)
    return -LR * (m_hat / (jnp.sqrt(v_hat) + AEPS) + WD * w), m, v


def adam_big(w, m, v, slots, name):
    nl, R, C = w.shape
    tr = 128 if R % 128 == 0 else (64 if R % 64 == 0 else R)

    def body(w_ref, m_ref, v_ref, s_ref, g_ref, d_ref, nm_ref, nv_ref):
        g = s_ref[0].astype(f32)
        for s in range(1, 8):
            g = g + s_ref[s].astype(f32)
        d, nm_, nv_ = _adamw(w_ref[...], g, m_ref[...], v_ref[...])
        g_ref[...] = g
        d_ref[...] = d
        nm_ref[...] = nm_
        nv_ref[...] = nv_

    blk = pl.BlockSpec((None, tr, C), lambda l, i: (l, i, 0))
    return pl.pallas_call(
        body, grid=(nl, R // tr),
        in_specs=[blk, blk, blk, pl.BlockSpec((8, None, tr, C), lambda l, i: (0, l, i, 0))],
        out_specs=[blk] * 4, out_shape=[S((nl, R, C), f32)] * 4,
        compiler_params=_cp(("parallel", "parallel")), name=name)(w, m, v, slots)


def adam_small(w, m, v, slots):
    R = w.shape[0]
    tr = 256

    def body(w_ref, m_ref, v_ref, s_ref, g_ref, d_ref, nm_ref, nv_ref):
        g = s_ref[0]
        for s in range(1, 8):
            g = g + s_ref[s]
        d, nm_, nv_ = _adamw(w_ref[...], g, m_ref[...], v_ref[...])
        g_ref[...] = g
        d_ref[...] = d
        nm_ref[...] = nm_
        nv_ref[...] = nv_

    blk = pl.BlockSpec((tr, 128), lambda i: (i, 0))
    return pl.pallas_call(
        body, grid=(R // tr,), in_specs=[blk, blk, blk, pl.BlockSpec((8, tr, 128), lambda i: (0, i, 0))],
        out_specs=[blk] * 4, out_shape=[S((R, 128), f32)] * 4,
        compiler_params=_cp(("parallel",)), name="adam_small")(w, m, v, slots)


def _pack(d):
    flat = jnp.concatenate([d[n].reshape(-1) for n in SMALL])
    n = flat.shape[0]
    rows = -(-n // (256 * 128)) * 256
    return jnp.pad(flat, (0, rows * 128 - n)).reshape(rows, 128)


def _unpack(p, like):
    flat = p.reshape(-1)
    out, off = {}, 0
    for n in SMALL:
        sz = math.prod(like[n].shape)
        out[n] = flat[off:off + sz].reshape(like[n].shape)
        off += sz
    return out


def kernel(x, positions, norm_mix, norm_ffn, norm_final, mix_w_in, mix_w_out, s5_A_re, s5_A_im, s5_log_dt, s5_B_re, s5_B_im, s5_C_re, s5_C_im, s5_D, s5_glu_w, s5_glu_b, hgrn_gamma, hgrn_norm, att_w_qkv, att_w_o, ffn_w_in, ffn_conv_w, ffn_conv_b, ffn_w_out, loss_target, m_norm_mix, m_norm_ffn, m_norm_final, m_mix_w_in, m_mix_w_out, m_s5_A_re, m_s5_A_im, m_s5_log_dt, m_s5_B_re, m_s5_B_im, m_s5_C_re, m_s5_C_im, m_s5_D, m_s5_glu_w, m_s5_glu_b, m_hgrn_gamma, m_hgrn_norm, m_att_w_qkv, m_att_w_o, m_ffn_w_in, m_ffn_conv_w, m_ffn_conv_b, m_ffn_w_out, v_norm_mix, v_norm_ffn, v_norm_final, v_mix_w_in, v_mix_w_out, v_s5_A_re, v_s5_A_im, v_s5_log_dt, v_s5_B_re, v_s5_B_im, v_s5_C_re, v_s5_C_im, v_s5_D, v_s5_glu_w, v_s5_glu_b, v_hgrn_gamma, v_hgrn_norm, v_att_w_qkv, v_att_w_o, v_ffn_w_in, v_ffn_conv_w, v_ffn_conv_b, v_ffn_w_out):
    a = dict(locals())
    weights = BIG + SMALL
    w = {n: a[n] for n in weights}
    m = {n: a["m_" + n] for n in weights}
    v = {n: a["v_" + n] for n in weights}
    shards = {n: (w[n] if n == "ffn_conv_w" else cast_bf16(w[n], "cast_" + n)) for n in BIG}
    W = gather_weights(shards)
    sm = {n: w[n] for n in SMALL}
    sm["ffn_conv_b3"] = ffn_conv_b.reshape(2, 1, 2 * DFF)
    loss, gx, gbig, gsmall = local_step(x[0], positions.reshape(L, 1), loss_target[0], sm, W)
    slots = reduce_grads(gbig, {n: w[n].shape for n in BIG})
    res = {}
    for n in BIG:
        res[n] = adam_big(w[n], m[n], v[n], slots[n], "adam_" + n)
    sslots = allreduce_small(_pack(gsmall))
    packed = adam_small(_pack({n: w[n] for n in SMALL}), _pack({n: m[n] for n in SMALL}), _pack({n: v[n] for n in SMALL}), sslots)
    small_out = [_unpack(p, {n: w[n] for n in SMALL}) for p in packed]
    for n in SMALL:
        res[n] = tuple(so[n] for so in small_out)
    total = lax.psum(loss[0, 0], ("x", "y", "c"))
    order = ("norm_mix", "norm_ffn", "norm_final", "mix_w_in", "mix_w_out", "s5_A_re", "s5_A_im", "s5_log_dt", "s5_B_re", "s5_B_im",
             "s5_C_re", "s5_C_im", "s5_D", "s5_glu_w", "s5_glu_b", "hgrn_gamma", "hgrn_norm", "att_w_qkv", "att_w_o", "ffn_w_in",
             "ffn_conv_w", "ffn_conv_b", "ffn_w_out")
    return (total, gx[None], *[res[n][0] for n in order], *[res[n][1] for n in order], *[res[n][2] for n in order],
            *[res[n][3] for n in order])
```

```python
import functools
import math

import numpy as np
import jax
import jax.numpy as jnp
from jax import lax
from jax.experimental import pallas as pl
from jax.experimental.pallas import tpu as pltpu

f32 = jnp.float32
BF = jnp.bfloat16
HI = lax.Precision.HIGHEST
S = jax.ShapeDtypeStruct
MESH = pl.DeviceIdType.MESH

L = 2048
D = 1024
EPS = 1e-6
S5W = 512
NST = 2048
HGC = 64
DFF = 2816
ROPE_THETA = 500000.0
LR, B1, B2, AEPS, WD, STEP = 0.001, 0.9, 0.999, 1e-08, 0.01, 10
VMEM_LIMIT = 56 * 1024 * 1024


def _cp(sem=None):
    return pltpu.CompilerParams(dimension_semantics=sem, vmem_limit_bytes=VMEM_LIMIT)


ANY = pl.BlockSpec(memory_space=pl.ANY)
ROW_SHARDED = ("mix_w_out", "s5_glu_w", "ffn_w_out")


def _coords():
    x, y, c = lax.axis_index("x"), lax.axis_index("y"), lax.axis_index("c")
    return x, y, c, 2 * x + y, [(1 - x, y), (x, 1 - y), (1 - x, 1 - y)]


def _rows(start, n):
    return pl.ds(start if isinstance(start, int) else pl.multiple_of(start, 8), n)


def _cols(q, n):
    return pl.ds(pl.multiple_of(q * n, 128), n)


class Plan:
    def __init__(self):
        self.bufs, self.ops, self.nsem, self.out = {}, [], 0, {}

    def buf(self, key, arr=None, shape=None, write=False):
        b = self.bufs.setdefault(key, dict(arr=arr, shape=shape, write=False))
        b["write"] = b["write"] or write
        return key

    def add(self, op):
        op.base = self.nsem
        self.nsem += op.nsem
        self.ops.append(op)


class GatherOp:
    nsem = 13

    def __init__(self, plan, ksrc, kdst, l, shard_shape, rows, r0, nr, split):
        self.ksrc, self.kdst, self.l, (_, self.R, self.C), self.rows, self.r0, self.nr, self.split = (
            ksrc, kdst, l, shard_shape, rows, r0, nr, split)
        self.h = nr // 2 if split else nr
        plan.add(self)

    def _dst(self, R_, q, start, n):
        if self.rows:
            return R_[self.kdst].at[_rows(q * self.R + start, n), :]
        return R_[self.kdst].at[_rows(start, n), _cols(q, self.C)]

    def _mine(self, c):
        return self.r0 + (c * self.h if self.split else 0)

    def _theirs(self, c):
        return self.r0 + ((1 - c) * self.h if self.split else 0)

    def _copies(self, R_, sems):
        x, y, c, me, others = _coords()
        src = R_[self.ksrc]
        local = pltpu.make_async_copy(src.at[self.l, _rows(self.r0, self.nr), :], self._dst(R_, me, self.r0, self.nr),
                                      sems.at[self.base + 12])
        send, fwd = [], []
        for k, (px, py) in enumerate(others):
            q = 2 * px + py
            send.append((
                pltpu.make_async_remote_copy(src.at[self.l, _rows(self._mine(c), self.h), :], self._dst(R_, me, self._mine(c), self.h),
                                             sems.at[self.base + k], sems.at[self.base + 3 + k], device_id=(px, py, c), device_id_type=MESH),
                pltpu.make_async_remote_copy(src.at[self.l, _rows(self._mine(c), self.h), :], self._dst(R_, q, self._mine(c), self.h),
                                             sems.at[self.base + k], sems.at[self.base + 3 + k], device_id=(px, py, c), device_id_type=MESH)))
            fwd.append((
                pltpu.make_async_remote_copy(self._dst(R_, q, self._mine(c), self.h), self._dst(R_, q, self._mine(c), self.h),
                                             sems.at[self.base + 6 + k], sems.at[self.base + 9 + k], device_id=(x, y, 1 - c), device_id_type=MESH),
                pltpu.make_async_remote_copy(self._dst(R_, q, self._theirs(c), self.h), self._dst(R_, q, self._theirs(c), self.h),
                                             sems.at[self.base + 6 + k], sems.at[self.base + 9 + k], device_id=(x, y, 1 - c), device_id_type=MESH)))
        return local, send, fwd

    def start(self, R_, sems):
        local, send, _ = self._copies(R_, sems)
        local.start()
        for out, _ in send:
            out.start()

    def finish(self, R_, sems):
        local, send, fwd = self._copies(R_, sems)
        for k in range(3):
            send[k][1].wait_recv()
            if self.split:
                fwd[k][0].start()
        for k in range(3):
            if self.split:
                fwd[k][1].wait_recv()
                fwd[k][0].wait_send()
            send[k][0].wait_send()
        local.wait()


class ReduceOp:
    nsem = 7

    def __init__(self, plan, ksrc, kdst, l, shard_shape, rows, r0, nr, whole=False):
        self.ksrc, self.kdst, self.l, (self.R, self.C), self.rows, self.r0, self.nr, self.whole = (
            ksrc, kdst, l, shard_shape[-2:], rows, r0, nr, whole)
        plan.add(self)

    def _piece(self, R_, q):
        g = R_[self.ksrc]
        if self.whole:
            return g
        if self.rows:
            return g.at[_rows(q * self.R + self.r0, self.nr), :]
        return g.at[_rows(self.r0, self.nr), _cols(q, self.C)]

    def _slot(self, R_, s):
        if self.whole:
            return R_[self.kdst].at[s]
        return R_[self.kdst].at[s, self.l, _rows(self.r0, self.nr), :]

    def _copies(self, R_, sems):
        x, y, c, me, others = _coords()
        local = pltpu.make_async_copy(self._piece(R_, me), self._slot(R_, 2 * me + c), sems.at[self.base + 6])
        send = []
        for k, (px, py) in enumerate(others):
            q = 2 * px + py
            send.append((
                pltpu.make_async_remote_copy(self._piece(R_, q), self._slot(R_, 2 * me + c), sems.at[self.base + k],
                                             sems.at[self.base + 3 + k], device_id=(px, py, c), device_id_type=MESH),
                pltpu.make_async_remote_copy(self._piece(R_, q), self._slot(R_, 2 * q + c), sems.at[self.base + k],
                                             sems.at[self.base + 3 + k], device_id=(px, py, c), device_id_type=MESH)))
        return local, send

    def start(self, R_, sems):
        local, send = self._copies(R_, sems)
        local.start()
        for out, _ in send:
            out.start()

    def finish(self, R_, sems):
        local, send = self._copies(R_, sems)
        local.wait()
        for out, inn in send:
            inn.wait_recv()
            out.wait_send()


class ForwardOp:
    nsem = 8

    def __init__(self, plan, kdst, l, whole=False):
        self.kdst, self.l, self.whole = kdst, l, whole
        plan.add(self)

    def _slot(self, R_, s):
        return R_[self.kdst].at[s] if self.whole else R_[self.kdst].at[s, self.l]

    def _copies(self, R_, sems):
        x, y, c, me, others = _coords()
        return [(pltpu.make_async_remote_copy(self._slot(R_, 2 * q + c), self._slot(R_, 2 * q + c), sems.at[self.base + q],
                                              sems.at[self.base + 4 + q], device_id=(x, y, 1 - c), device_id_type=MESH),
                 pltpu.make_async_remote_copy(self._slot(R_, 2 * q + 1 - c), self._slot(R_, 2 * q + 1 - c), sems.at[self.base + q],
                                              sems.at[self.base + 4 + q], device_id=(x, y, 1 - c), device_id_type=MESH))
                for q in range(4)]

    def start(self, R_, sems):
        for out, _ in self._copies(R_, sems):
            out.start()

    def finish(self, R_, sems):
        for out, inn in self._copies(R_, sems):
            inn.wait_recv()
            out.wait_send()


def pcall(body, plan, *, grid, in_specs, out_specs, out_shape, scratch_shapes=(), sem, name, args):
    multi = isinstance(out_shape, (list, tuple))
    if plan is None or not plan.ops:
        return pl.pallas_call(body, grid=grid, in_specs=in_specs, out_specs=out_specs, out_shape=out_shape,
                              scratch_shapes=list(scratch_shapes), compiler_params=_cp(sem), name=name)(*args)
    outs = list(out_shape) if multi else [out_shape]
    ospecs = list(out_specs) if multi else [out_specs]
    kin = [k for k, b in plan.bufs.items() if b["arr"] is not None]
    kout = [k for k, b in plan.bufs.items() if b["write"]]
    n_in, n_out, n_scr = len(in_specs), len(outs), len(scratch_shapes)

    def wrapped(*refs):
        o0 = n_in + len(kin)
        s0 = o0 + n_out + len(kout)
        R_ = dict(zip(kin, refs[n_in:o0]))
        R_.update(zip(kout, refs[o0 + n_out:s0]))
        sems = refs[s0 + n_scr]
        first = functools.reduce(jnp.logical_and, [pl.program_id(d) == 0 for d in range(len(grid))])
        last = functools.reduce(jnp.logical_and, [pl.program_id(d) == grid[d] - 1 for d in range(len(grid))])

        @pl.when(first)
        def _():
            for op in plan.ops:
                op.start(R_, sems)

        body(*refs[:n_in], *refs[o0:o0 + n_out], *refs[s0:s0 + n_scr])

        @pl.when(last)
        def _():
            for op in plan.ops:
                op.finish(R_, sems)

    def shape_of(k):
        b = plan.bufs[k]
        return S(b["arr"].shape, b["arr"].dtype) if b["arr"] is not None else b["shape"]

    res = pl.pallas_call(
        wrapped, grid=grid, in_specs=list(in_specs) + [ANY] * len(kin), out_specs=ospecs + [ANY] * len(kout),
        out_shape=outs + [shape_of(k) for k in kout],
        scratch_shapes=list(scratch_shapes) + [pltpu.SemaphoreType.DMA((plan.nsem,))],
        input_output_aliases={n_in + kin.index(k): n_out + kout.index(k) for k in kout if plan.bufs[k]["arr"] is not None},
        compiler_params=pltpu.CompilerParams(dimension_semantics=("arbitrary",) * len(grid), vmem_limit_bytes=VMEM_LIMIT,
                                             has_side_effects=True),
        name=name)(*args, *[plan.bufs[k]["arr"] for k in kin])
    plan.out = dict(zip(kout, res[n_out:]))
    return list(res[:n_out]) if multi else res[0]


def comm_only(plan, name):
    def body(o_ref):
        o_ref[...] = jnp.zeros_like(o_ref)

    pcall(body, plan, grid=(1,), in_specs=[], out_specs=pl.BlockSpec((8, 128), lambda i: (0, 0)), out_shape=S((8, 128), f32),
          sem=None, name=name, args=[])


def _dg(a, b, ca, cb):
    return lax.dot_general(a.astype(BF), b.astype(BF), (((ca,), (cb,)), ((), ())), preferred_element_type=f32)


@jax.custom_vjp
def dot_nn(a, b):
    return _dg(a, b, 1, 0)


@jax.custom_vjp
def dot_nt(a, b):
    return _dg(a, b, 1, 1)


@jax.custom_vjp
def dot_tn(a, b):
    return _dg(a, b, 0, 0)


dot_nn.defvjp(lambda a, b: (dot_nn(a, b), (a, b)),
              lambda r, g: (dot_nt(g, r[1]).astype(r[0].dtype), dot_tn(r[0], g).astype(r[1].dtype)))
dot_nt.defvjp(lambda a, b: (dot_nt(a, b), (a, b)),
              lambda r, g: (dot_nn(g, r[1]).astype(r[0].dtype), dot_tn(g, r[0]).astype(r[1].dtype)))
dot_tn.defvjp(lambda a, b: (dot_tn(a, b), (a, b)),
              lambda r, g: (dot_nt(r[1], g).astype(r[0].dtype), dot_nn(r[0], g).astype(r[1].dtype)))


def matmul(a, b, *, mode, tm, tn, tk, out_dtype=f32, add=None, b_lead=None, a_spec=None, b_spec=None, dims=None, plan=None, name):
    a_over, b_over = a_spec, b_spec
    if mode == "nn":
        (M, K), N = a.shape[-2:], b.shape[-1]
        a_spec = pl.BlockSpec((tm, tk), lambda i, j, k: (i, k))
        b_blk, b_idx, ca, cb = (tk, tn), (lambda i, j, k: (k, j)), 1, 0
    elif mode == "nt":
        (M, K), N = a.shape[-2:], b.shape[-2]
        a_spec = pl.BlockSpec((tm, tk), lambda i, j, k: (i, k))
        b_blk, b_idx, ca, cb = (tn, tk), (lambda i, j, k: (j, k)), 1, 1
    else:
        (K, M), N = a.shape[-2:], b.shape[-1]
        a_spec = pl.BlockSpec((tk, tm), lambda i, j, k: (k, i))
        b_blk, b_idx, ca, cb = (tk, tn), (lambda i, j, k: (k, j)), 0, 0
    if dims is not None:
        M, N, K = dims
    assert M % tm == 0 and N % tn == 0 and K % tk == 0, (name, M, N, K, tm, tn, tk)
    if b_lead is None:
        b_spec = pl.BlockSpec(b_blk, b_idx)
    else:
        b_spec = pl.BlockSpec((None,) + b_blk, lambda i, j, k: (b_lead,) + b_idx(i, j, k))
    if a_over is not None:
        a_spec = a_over
    if b_over is not None:
        b_spec = b_over
    nk = K // tk
    has_add = add is not None

    def body(*refs):
        a_ref, b_ref = refs[0], refs[1]
        add_ref = refs[2] if has_add else None
        o_ref = refs[2 + has_add]
        p = _dg(a_ref[...], b_ref[...], ca, cb)

        def fin(v):
            if has_add:
                v = v + add_ref[...].astype(f32)
            o_ref[...] = v.astype(o_ref.dtype)

        if nk == 1:
            fin(p)
        else:
            acc = refs[3 + has_add]
            k = pl.program_id(2)

            @pl.when(k == 0)
            def _():
                acc[...] = p

            @pl.when(k > 0)
            def _():
                acc[...] += p

            @pl.when(k == nk - 1)
            def _():
                fin(acc[...])

    in_specs = [a_spec, b_spec]
    args = [a, b]
    if has_add:
        in_specs.append(pl.BlockSpec((tm, tn), lambda i, j, k: (i, j)))
        args.append(add)
    return pcall(body, plan, grid=(M // tm, N // tn, nk), in_specs=in_specs,
                 out_specs=pl.BlockSpec((tm, tn), lambda i, j, k: (i, j)), out_shape=S((M, N), out_dtype),
                 scratch_shapes=[pltpu.VMEM((tm, tn), f32)] if nk > 1 else [],
                 sem=("parallel", "parallel", "arbitrary"), name=name, args=args)


def _rms(xv, gv):
    return xv * lax.rsqrt(jnp.mean(xv * xv, axis=-1, keepdims=True) + EPS) * gv


TR = 256


def rms_fwd(x, g, name):
    def body(x_ref, g_ref, o_ref):
        o_ref[...] = _rms(x_ref[...], g_ref[...]).astype(o_ref.dtype)

    return pl.pallas_call(
        body, grid=(L // TR,),
        in_specs=[pl.BlockSpec((TR, D), lambda i: (i, 0)), pl.BlockSpec((1, D), lambda i: (0, 0))],
        out_specs=pl.BlockSpec((TR, D), lambda i: (i, 0)), out_shape=S((L, D), BF),
        compiler_params=_cp(("parallel",)), name=name)(x, g)


def rms_bwd(x, g, dys, dres, name, plan=None):
    nd = len(dys)

    def body(*refs):
        x_ref, g_ref = refs[0], refs[1]
        dr_ref, dh_ref, dg_ref = refs[2 + nd:]
        dy = refs[2][...].astype(f32)
        for r in refs[3:2 + nd]:
            dy = dy + r[...].astype(f32)
        _, vjp = jax.vjp(_rms, x_ref[...], g_ref[...])
        dx, dg = vjp(dy)
        dh_ref[...] = dr_ref[...] + dx

        @pl.when(pl.program_id(0) == 0)
        def _():
            dg_ref[...] = jnp.zeros_like(dg_ref)

        dg_ref[...] += dg

    row = pl.BlockSpec((TR, D), lambda i: (i, 0))
    vec = pl.BlockSpec((1, D), lambda i: (0, 0))
    return pcall(body, plan, grid=(L // TR,), in_specs=[row, vec] + [row] * (nd + 1), out_specs=[row, vec],
                 out_shape=[S((L, D), f32), S((1, D), f32)], sem=("arbitrary",), name=name, args=[x, g, *dys, dres])


def loss_head(h, g, tgt):
    def f(hv, gv, tv):
        y = _rms(hv, gv)
        return 0.5 * jnp.sum(jnp.mean(jnp.square(y - tv), axis=-1))

    def body(h_ref, g_ref, t_ref, l_ref, dh_ref, dg_ref):
        val, vjp = jax.vjp(f, h_ref[...], g_ref[...], t_ref[...])
        dh, dg, _ = vjp(jnp.ones((), f32))
        dh_ref[...] = dh

        @pl.when(pl.program_id(0) == 0)
        def _():
            dg_ref[...] = jnp.zeros_like(dg_ref)
            l_ref[...] = jnp.zeros_like(l_ref)

        dg_ref[...] += dg
        l_ref[...] += jnp.full((1, 128), val, f32)

    row = pl.BlockSpec((TR, D), lambda i: (i, 0))
    vec = pl.BlockSpec((1, D), lambda i: (0, 0))
    return pl.pallas_call(
        body, grid=(L // TR,), in_specs=[row, vec, row],
        out_specs=[pl.BlockSpec((1, 128), lambda i: (0, 0)), row, vec],
        out_shape=[S((1, 128), f32), S((L, D), f32), S((1, D), f32)],
        compiler_params=_cp(("arbitrary",)), name="loss_head")(h, g, tgt)


def _col_to_row(c):
    n = c.shape[0]
    t = jnp.broadcast_to(c, (n, 128)).T
    r = lax.broadcasted_iota(jnp.int32, (128, n), 0)
    return jnp.sum(jnp.where(r == 0, t, 0.0), axis=0, keepdims=True)


def _s5_param_map(are, aim, ldt_row, bre, bim, cre, cim):
    n = NST
    gi = lax.broadcasted_iota(jnp.int32, (n, 32), 0) // 64
    gj = lax.broadcasted_iota(jnp.int32, (n, 32), 1)
    ldt = jnp.sum(jnp.where(gi == gj, ldt_row, 0.0), axis=1, keepdims=True)
    dt = jnp.exp(ldt)
    mag = jnp.exp(are * dt)
    abr = mag * jnp.cos(aim * dt)
    abi = mag * jnp.sin(aim * dt)
    den = are * are + aim * aim
    nr, ni = abr - 1.0, abi
    cr = (nr * are + ni * aim) / den
    ci = (ni * are - nr * aim) / den
    bbr = cr * bre - ci * bim
    bbi = cr * bim + ci * bre
    tc = lax.broadcasted_iota(jnp.int32, (16, 128), 0)
    tl = lax.broadcasted_iota(jnp.int32, (16, 128), 1)
    T = (tl % 16 == tc).astype(f32)
    mr = (lax.broadcasted_iota(jnp.int32, (n, 128), 0) // 64) % 8
    mc = lax.broadcasted_iota(jnp.int32, (n, 128), 1) // 16
    mask = (mr == mc).astype(f32)

    def expand(v):
        return jnp.dot(v, T, precision=HI, preferred_element_type=f32) * mask

    return expand(bbr), expand(bbi), expand(cre), expand(cim), _col_to_row(abr), _col_to_row(abi)


def s5_params_fwd(are, aim, ldt_row, bre, bim, cre, cim):
    def body(*refs):
        outs = _s5_param_map(*[r[...] for r in refs[:7]])
        for o_ref, o in zip(refs[7:], outs):
            o_ref[...] = o

    return pl.pallas_call(
        body, out_shape=[S((NST, 128), f32)] * 4 + [S((1, NST), f32)] * 2,
        compiler_params=_cp(), name="s5_params_fwd")(are, aim, ldt_row, bre, bim, cre, cim)


def s5_params_bwd(are, aim, ldt_row, bre, bim, cre, cim, cots):
    def body(*refs):
        _, vjp = jax.vjp(_s5_param_map, *[r[...] for r in refs[:7]])
        gs = vjp(tuple(r[...] for r in refs[7:13]))
        for o_ref, o in zip(refs[13:], gs):
            o_ref[...] = o

    return pl.pallas_call(
        body, out_shape=[S((NST, 1), f32)] * 2 + [S((1, 32), f32)] + [S((NST, 16), f32)] * 4,
        compiler_params=_cp(), name="s5_params_bwd")(are, aim, ldt_row, bre, bim, cre, cim, *cots)


NT5 = 4
RC = 256


def s5_scan_fwd(proj, wbr, wbi, wcr, wci, abr, abi, drow, plan=None):
    def body(u_ref, wbr_ref, wbi_ref, wcr_ref, wci_ref, ar_ref, ai_ref, d_ref, xr_ref, xi_ref, y_ref):
        wbr_v, wbi_v = wbr_ref[...], wbi_ref[...]
        for r in range(L // RC):
            rows = pl.ds(r * RC, RC)
            ub = u_ref[rows, :]
            xr_ref[rows, :] = dot_nt(ub, wbr_v)
            xi_ref[rows, :] = dot_nt(ub, wbi_v)
        ar, ai = ar_ref[...], ai_ref[...]

        def step(t, c):
            cr, ci = c
            nr = ar * cr - ai * ci + xr_ref[pl.ds(t, 1), :]
            ni = ar * ci + ai * cr + xi_ref[pl.ds(t, 1), :]
            xr_ref[pl.ds(t, 1), :] = nr
            xi_ref[pl.ds(t, 1), :] = ni
            return nr, ni

        z = jnp.zeros((1, 512), f32)
        lax.fori_loop(0, L, step, (z, z), unroll=8)
        wcr_v, wci_v, dv = wcr_ref[...], wci_ref[...], d_ref[...]
        for r in range(L // RC):
            rows = pl.ds(r * RC, RC)
            y_ref[rows, :] = (dot_nn(xr_ref[rows, :], wcr_v) - dot_nn(xi_ref[rows, :], wci_v)
                              + dv * u_ref[rows, :])

    wspec = pl.BlockSpec((512, 128), lambda j: (j, 0))
    aspec = pl.BlockSpec((1, 512), lambda j: (0, j))
    return pcall(
        body, plan, grid=(NT5,),
        in_specs=[pl.BlockSpec((L, 128), lambda j: (0, j)), wspec, wspec, wspec, wspec, aspec, aspec,
                  pl.BlockSpec((1, 128), lambda j: (0, j))],
        out_specs=[pl.BlockSpec((L, 512), lambda j: (0, j)), pl.BlockSpec((L, 512), lambda j: (0, j)),
                   pl.BlockSpec((L, 128), lambda j: (0, j))],
        out_shape=[S((L, NST), f32), S((L, NST), f32), S((L, S5W), f32)],
        sem=("parallel",), name="s5_scan_fwd", args=[proj, wbr, wbi, wcr, wci, abr, abi, drow])


def s5_scan_bwd(dy, proj, xs_re, xs_im, wbr, wbi, wcr, wci, abr, abi, drow, plan=None):
    def body(dy_ref, u_ref, xr_ref, xi_ref, wbr_ref, wbi_ref, wcr_ref, wci_ref, ar_ref, ai_ref, d_ref,
             du_ref, gwbr_ref, gwbi_ref, gwcr_ref, gwci_ref, gar_ref, gai_ref, gd_ref, lr_ref, li_ref):
        wcr_v, wci_v = wcr_ref[...], wci_ref[...]
        gwcr = jnp.zeros((512, 128), f32)
        gwci = jnp.zeros((512, 128), f32)
        gd = jnp.zeros((1, 128), f32)
        for r in range(L // RC):
            rows = pl.ds(r * RC, RC)
            dyv = dy_ref[rows, :]
            lr_ref[rows, :] = dot_nt(dyv, wcr_v)
            li_ref[rows, :] = -dot_nt(dyv, wci_v)
            gwcr += dot_tn(xr_ref[rows, :], dyv)
            gwci -= dot_tn(xi_ref[rows, :], dyv)
            gd += jnp.sum(dyv * u_ref[rows, :], axis=0, keepdims=True)
        gwcr_ref[...] = gwcr
        gwci_ref[...] = gwci
        gd_ref[...] = gd
        ar, ai = ar_ref[...], ai_ref[...]

        def step(i, c):
            lr, li, gar, gai = c
            t = L - 1 - i
            nr = lr_ref[pl.ds(t, 1), :] + ar * lr + ai * li
            ni = li_ref[pl.ds(t, 1), :] + ar * li - ai * lr
            lr_ref[pl.ds(t, 1), :] = nr
            li_ref[pl.ds(t, 1), :] = ni
            tp = jnp.maximum(t - 1, 0)
            live = (t > 0).astype(f32)
            xr = xr_ref[pl.ds(tp, 1), :] * live
            xi = xi_ref[pl.ds(tp, 1), :] * live
            return nr, ni, gar + xr * nr + xi * ni, gai + xr * ni - xi * nr

        z = jnp.zeros((1, 512), f32)
        _, _, gar, gai = lax.fori_loop(0, L, step, (z, z, z, z), unroll=8)
        gar_ref[...] = gar
        gai_ref[...] = gai
        wbr_v, wbi_v, dv = wbr_ref[...], wbi_ref[...], d_ref[...]
        gwbr = jnp.zeros((512, 128), f32)
        gwbi = jnp.zeros((512, 128), f32)
        for r in range(L // RC):
            rows = pl.ds(r * RC, RC)
            lrv, liv, uv = lr_ref[rows, :], li_ref[rows, :], u_ref[rows, :]
            du_ref[rows, :] = (dot_nn(lrv, wbr_v) + dot_nn(liv, wbi_v) + dv * dy_ref[rows, :]).astype(du_ref.dtype)
            gwbr += dot_tn(lrv, uv)
            gwbi += dot_tn(liv, uv)
        gwbr_ref[...] = gwbr
        gwbi_ref[...] = gwbi

    wspec = pl.BlockSpec((512, 128), lambda j: (j, 0))
    aspec = pl.BlockSpec((1, 512), lambda j: (0, j))
    col = pl.BlockSpec((L, 128), lambda j: (0, j))
    st = pl.BlockSpec((L, 512), lambda j: (0, j))
    dspec = pl.BlockSpec((1, 128), lambda j: (0, j))
    return pcall(
        body, plan, grid=(NT5,),
        in_specs=[col, col, st, st, wspec, wspec, wspec, wspec, aspec, aspec, dspec],
        out_specs=[col, wspec, wspec, wspec, wspec, aspec, aspec, dspec],
        out_shape=[S((L, S5W), BF)] + [S((NST, 128), f32)] * 4 + [S((1, NST), f32)] * 2 + [S((1, S5W), f32)],
        scratch_shapes=[pltpu.VMEM((L, 512), f32), pltpu.VMEM((L, 512), f32)],
        sem=("parallel",), name="s5_scan_bwd", args=[dy, proj, xs_re, xs_im, wbr, wbi, wcr, wci, abr, abi, drow])


def _glu(y, w, b):
    z = jax.nn.gelu(y)
    return z * jax.nn.sigmoid(dot_nn(z, w) + b)


def s5_glu_fwd(y, w, b):
    def body(y_ref, w_ref, b_ref, o_ref):
        o_ref[...] = _glu(y_ref[...], w_ref[...], b_ref[...]).astype(o_ref.dtype)

    return pl.pallas_call(
        body, grid=(L // TR,),
        in_specs=[pl.BlockSpec((TR, S5W), lambda i: (i, 0)), pl.BlockSpec((S5W, S5W), lambda i: (0, 0)),
                  pl.BlockSpec((1, S5W), lambda i: (0, 0))],
        out_specs=pl.BlockSpec((TR, S5W), lambda i: (i, 0)), out_shape=S((L, S5W), BF),
        compiler_params=_cp(("parallel",)), name="s5_glu_fwd")(y, w, b)


def s5_glu_bwd(y, w, b, dmix):
    def body(y_ref, w_ref, b_ref, g_ref, dy_ref, dw_ref, db_ref):
        _, vjp = jax.vjp(_glu, y_ref[...], w_ref[...].astype(f32), b_ref[...])
        dy, dw, db = vjp(g_ref[...])
        dy_ref[...] = dy

        @pl.when(pl.program_id(0) == 0)
        def _():
            dw_ref[...] = jnp.zeros_like(dw_ref)
            db_ref[...] = jnp.zeros_like(db_ref)

        dw_ref[...] += dw
        db_ref[...] += db

    row = pl.BlockSpec((TR, S5W), lambda i: (i, 0))
    return pl.pallas_call(
        body, grid=(L // TR,),
        in_specs=[row, pl.BlockSpec((S5W, S5W), lambda i: (0, 0)), pl.BlockSpec((1, S5W), lambda i: (0, 0)), row],
        out_specs=[row, pl.BlockSpec((S5W, S5W), lambda i: (0, 0)), pl.BlockSpec((1, S5W), lambda i: (0, 0))],
        out_shape=[S((L, S5W), f32), S((S5W, S5W), f32), S((1, S5W), f32)],
        compiler_params=_cp(("arbitrary",)), name="s5_glu_bwd")(y, w, b, dmix)


def _hi(a, b, ca, cb):
    return lax.dot_general(a, b, (((ca,), (cb,)), ((), ())), precision=HI, preferred_element_type=f32)


def _hgrn_chunk(St, xq, xf, xi, xg, gam, ng):
    lb = jax.nn.sigmoid(gam[0:1] - gam[1:2])
    q = jax.nn.silu(xq)
    f = lb + (1.0 - lb) * jax.nn.sigmoid(xf)
    k = 1.0 - f
    g = jnp.log(f)
    ti = lax.broadcasted_iota(jnp.int32, (HGC, HGC), 0)
    si = lax.broadcasted_iota(jnp.int32, (HGC, HGC), 1)
    causal = si <= ti
    b = jnp.dot(causal.astype(f32), g, precision=HI, preferred_element_type=f32)
    qe = q * jnp.exp(b)
    o = _hi(qe, St, 1, 1)
    att = jnp.where(causal, _hi(qe, k * jnp.exp(-b), 1, 1), 0.0)
    o = o + _hi(att, xi, 1, 0)
    bl = b[HGC - 1:HGC]
    St_new = St * jnp.exp(bl) + _hi(xi, k * jnp.exp(bl - b), 0, 0)
    o = o * lax.rsqrt(jnp.mean(o * o, axis=-1, keepdims=True) + EPS) * ng
    return St_new, o * jax.nn.silu(xg)


NCH = L // HGC


def hgrn_fwd(proj, gamma, hnorm, plan=None):
    def body(q_ref, f_ref, i_ref, g_ref, gam_ref, ng_ref, o_ref, ss_ref, st):
        @pl.when(pl.program_id(0) == 0)
        def _():
            st[...] = jnp.zeros_like(st)

        for h in range(4):
            sl = slice(h * 128, (h + 1) * 128)
            s0 = st[h]
            ss_ref[0, h] = s0
            s1, o = _hgrn_chunk(s0, q_ref[:, sl], f_ref[:, sl], i_ref[:, sl], g_ref[:, sl], gam_ref[:, sl], ng_ref[:, sl])
            st[h] = s1
            o_ref[:, sl] = o.astype(o_ref.dtype)

    def pj(n):
        return pl.BlockSpec((HGC, 512), lambda c: (c, n))

    return pcall(
        body, plan, grid=(NCH,),
        in_specs=[pj(1), pj(2), pj(3), pj(4), pl.BlockSpec((2, 512), lambda c: (0, 0)), pl.BlockSpec((1, 512), lambda c: (0, 0))],
        out_specs=[pl.BlockSpec((HGC, 512), lambda c: (c, 0)), pl.BlockSpec((1, 4, 128, 128), lambda c: (c, 0, 0, 0))],
        out_shape=[S((L, 512), BF), S((NCH, 4, 128, 128), f32)],
        scratch_shapes=[pltpu.VMEM((4, 128, 128), f32)],
        sem=("arbitrary",), name="hgrn_fwd", args=[proj, proj, proj, proj, gamma, hnorm])


def hgrn_bwd(proj, gamma, hnorm, ssave, dmix, du, plan=None):
    def body(q_ref, f_ref, i_ref, g_ref, gam_ref, ng_ref, ss_ref, do_ref, du_ref, dp_ref, dgam_ref, dng_ref, dst):
        @pl.when(pl.program_id(0) == 0)
        def _():
            dst[...] = jnp.zeros_like(dst)
            dgam_ref[...] = jnp.zeros_like(dgam_ref)
            dng_ref[...] = jnp.zeros_like(dng_ref)

        dp_ref[:, 0:512] = du_ref[...]
        for h in range(4):
            sl = slice(h * 128, (h + 1) * 128)
            _, vjp = jax.vjp(_hgrn_chunk, ss_ref[0, h], q_ref[:, sl], f_ref[:, sl], i_ref[:, sl], g_ref[:, sl],
                             gam_ref[:, sl], ng_ref[:, sl])
            ds, dq, df, di, dg, dgam, dng = vjp((dst[h], do_ref[:, sl]))
            dst[h] = ds
            for n, v in enumerate((dq, df, di, dg)):
                dp_ref[:, 512 * (n + 1) + h * 128: 512 * (n + 1) + (h + 1) * 128] = v.astype(dp_ref.dtype)
            dgam_ref[:, sl] += dgam
            dng_ref[:, sl] += dng

    def pj(n):
        return pl.BlockSpec((HGC, 512), lambda i: (NCH - 1 - i, n))

    return pcall(
        body, plan, grid=(NCH,),
        in_specs=[pj(1), pj(2), pj(3), pj(4), pl.BlockSpec((2, 512), lambda i: (0, 0)), pl.BlockSpec((1, 512), lambda i: (0, 0)),
                  pl.BlockSpec((1, 4, 128, 128), lambda i: (NCH - 1 - i, 0, 0, 0)), pj(1), pj(0)],
        out_specs=[pl.BlockSpec((HGC, 2560), lambda i: (NCH - 1 - i, 0)), pl.BlockSpec((2, 512), lambda i: (0, 0)),
                   pl.BlockSpec((1, 512), lambda i: (0, 0))],
        out_shape=[S((L, 2560), BF), S((2, 512), f32), S((1, 512), f32)],
        scratch_shapes=[pltpu.VMEM((4, 128, 128), f32)],
        sem=("arbitrary",), name="hgrn_bwd", args=[proj, proj, proj, proj, gamma, hnorm, ssave, dmix, du])


def _shift(x, k):
    return jnp.concatenate([jnp.zeros((k, x.shape[1]), x.dtype), x[:-k]], axis=0)


def _convact(ha, hb, wa, wb, ba, bb):
    ca = wa[2:3] * ha + wa[1:2] * _shift(ha, 1) + wa[0:1] * _shift(ha, 2) + ba
    cb = wb[2:3] * hb + wb[1:2] * _shift(hb, 1) + wb[0:1] * _shift(hb, 2) + bb
    return jax.nn.silu(ca) * cb


CT = 128
NCT = DFF // CT


def convact_fwd(hu, cw, cb, layer, plan=None):
    def body(ha_ref, hb_ref, wa_ref, wb_ref, ba_ref, bb_ref, o_ref):
        o_ref[...] = _convact(ha_ref[...], hb_ref[...], wa_ref[...], wb_ref[...], ba_ref[...], bb_ref[...]).astype(o_ref.dtype)

    def h(off):
        return pl.BlockSpec((L, CT), lambda j: (0, j + off))

    def w(off):
        return pl.BlockSpec((3, CT), lambda j: (0, j + off))

    def b(off):
        return pl.BlockSpec((None, 1, CT), lambda j: (layer, 0, j + off))

    return pcall(body, plan, grid=(NCT,), in_specs=[h(0), h(NCT), w(0), w(NCT), b(0), b(NCT)],
                 out_specs=pl.BlockSpec((L, CT), lambda j: (0, j)), out_shape=S((L, DFF), BF),
                 sem=("parallel",), name=f"convact_fwd{layer}", args=[hu, hu, cw, cw, cb, cb])


def convact_bwd(hu, cw, cb, dact, layer, plan=None):
    def body(ha_ref, hb_ref, wa_ref, wb_ref, ba_ref, bb_ref, g_ref, dh_ref, dw_ref, db_ref, sh, sw, sb):
        j = pl.program_id(0)

        @pl.when(j < NCT)
        def _():
            _, vjp = jax.vjp(_convact, ha_ref[...], hb_ref[...], wa_ref[...], wb_ref[...], ba_ref[...], bb_ref[...])
            dha, dhb, dwa, dwb, dba, dbb = vjp(g_ref[...].astype(f32))
            dh_ref[...] = dha.astype(dh_ref.dtype)
            dw_ref[...] = dwa
            db_ref[...] = dba
            sh[j] = dhb.astype(sh.dtype)
            sw[j] = dwb
            sb[j] = dbb

        @pl.when(j >= NCT)
        def _():
            dh_ref[...] = sh[j - NCT]
            dw_ref[...] = sw[j - NCT]
            db_ref[...] = sb[j - NCT]

    def lo(j):
        return jnp.minimum(j, NCT - 1)

    in_specs = [pl.BlockSpec((L, CT), lambda j: (0, lo(j))), pl.BlockSpec((L, CT), lambda j: (0, lo(j) + NCT)),
                pl.BlockSpec((3, CT), lambda j: (0, lo(j))), pl.BlockSpec((3, CT), lambda j: (0, lo(j) + NCT)),
                pl.BlockSpec((None, 1, CT), lambda j: (layer, 0, lo(j))), pl.BlockSpec((None, 1, CT), lambda j: (layer, 0, lo(j) + NCT)),
                pl.BlockSpec((L, CT), lambda j: (0, lo(j)))]
    return pcall(
        body, plan, grid=(2 * NCT,), in_specs=in_specs,
        out_specs=[pl.BlockSpec((L, CT), lambda j: (0, j)), pl.BlockSpec((3, CT), lambda j: (0, j)), pl.BlockSpec((1, CT), lambda j: (0, j))],
        out_shape=[S((L, 2 * DFF), BF), S((3, 2 * DFF), f32), S((1, 2 * DFF), f32)],
        scratch_shapes=[pltpu.VMEM((NCT, L, CT), BF), pltpu.VMEM((NCT, 3, CT), f32), pltpu.VMEM((NCT, 1, CT), f32)],
        sem=("arbitrary",), name=f"convact_bwd{layer}", args=[hu, hu, cw, cw, cb, cb, dact])


DILS = (1, 4, 16)
AB = 128
NPAIR = 12


def _rope_tables(pos_ref, invf_ref):
    ang = pos_ref[...].astype(f32) * invf_ref[...]
    lane = lax.broadcasted_iota(jnp.int32, (1, 128), 1) % 64
    cosf = jnp.where(lane < 16, jnp.cos(ang), 1.0)
    sn = jnp.sin(ang)
    s_lo = jnp.where(lane < 8, -sn, 0.0)
    s_hi = jnp.where((lane >= 8) & (lane < 16), sn, 0.0)
    return cosf, s_lo, s_hi


def _rope(t, cosf, s_lo, s_hi):
    return t * cosf + pltpu.roll(t, 120, 1) * s_lo + pltpu.roll(t, 8, 1) * s_hi


def _rope_t(g, cosf, s_lo, s_hi):
    return g * cosf + pltpu.roll(g * s_lo, 8, 1) + pltpu.roll(g * s_hi, 120, 1)


def _att_block(q2, kp, kc, vp, vc, first):
    lane = lax.broadcasted_iota(jnp.int32, (1, 128), 1)
    qi = lax.broadcasted_iota(jnp.int32, (AB, 2 * AB), 0) + AB
    kj = lax.broadcasted_iota(jnp.int32, (AB, 2 * AB), 1)
    back = qi - kj
    valid = (back >= 0) & (back <= AB)
    if first:
        valid = valid & (kj >= AB)
    kk = jnp.concatenate([kp, kc], axis=0)
    vv = jnp.concatenate([vp, vc], axis=0)
    o2 = jnp.zeros((AB, 128), f32)
    lse2 = jnp.zeros((AB, 128), f32)
    for e in range(2):
        hm = ((lane >= 64 * e) & (lane < 64 * (e + 1))).astype(f32)
        s = dot_nt(q2 * (hm * 0.125), kk)
        s = jnp.where(valid, s, -jnp.inf)
        m = jnp.max(s, axis=-1, keepdims=True)
        p = jnp.exp(s - m)
        den = jnp.sum(p, axis=-1, keepdims=True)
        o2 = o2 + dot_nn(p, vv * hm) / den
        lse2 = lse2 + (m + jnp.log(den)) * hm
    return o2, lse2


def _att_blocks(dil):
    m = L // dil
    return [(r * m + n * AB, n == 0) for r in range(dil) for n in range(m // AB)]


def deinterleave(x, dil):
    return x if dil == 1 else x.reshape(L // dil, dil, x.shape[1]).swapaxes(0, 1).reshape(L, x.shape[1])


def interleave(x, dil):
    return x if dil == 1 else x.reshape(dil, L // dil, x.shape[1]).swapaxes(0, 1).reshape(L, x.shape[1])


def attn_fwd(qkv, pos, invf, g):
    blocks = _att_blocks(DILS[g])

    def body(q_ref, k_ref, v_ref, pos_ref, invf_ref, o_ref, l_ref, qr, kr):
        cosf, s_lo, s_hi = _rope_tables(pos_ref, invf_ref)
        qr[...] = _rope(q_ref[...], cosf, s_lo, s_hi)
        kr[...] = _rope(k_ref[...], cosf, s_lo, s_hi)
        for off, first in blocks:
            cur, prv = pl.ds(off, AB), pl.ds(off if first else off - AB, AB)
            o2, lse2 = _att_block(qr[cur, :], kr[prv, :], kr[cur, :], v_ref[prv, :], v_ref[cur, :], first)
            o_ref[cur, :] = o2
            l_ref[cur, :] = lse2

    def sec(n):
        return pl.BlockSpec((L, 128), lambda p: (0, p + 4 * n))

    return pl.pallas_call(
        body, grid=(4,),
        in_specs=[sec(0), sec(1), sec(2), pl.BlockSpec((L, 1), lambda p: (0, 0)), pl.BlockSpec((1, 128), lambda p: (0, 0))],
        out_specs=[sec(0), sec(0)], out_shape=[S((L, 512), f32), S((L, 512), f32)],
        scratch_shapes=[pltpu.VMEM((L, 128), f32), pltpu.VMEM((L, 128), f32)],
        compiler_params=_cp(("parallel",)), name=f"attn_fwd{g}")(qkv, qkv, qkv, pos, invf)


def attn_bwd(qkv, pos, invf, do, dl, g, plan=None):
    blocks = _att_blocks(DILS[g])

    def body(q_ref, k_ref, v_ref, pos_ref, invf_ref, do_ref, dl_ref, d_ref, qr, kr, dqr, dkr, dvr):
        cosf, s_lo, s_hi = _rope_tables(pos_ref, invf_ref)
        qr[...] = _rope(q_ref[...], cosf, s_lo, s_hi)
        kr[...] = _rope(k_ref[...], cosf, s_lo, s_hi)
        for off, first in blocks:
            cur, prv = pl.ds(off, AB), pl.ds(off if first else off - AB, AB)
            fn = functools.partial(_att_block, first=first)
            _, vjp = jax.vjp(fn, qr[cur, :], kr[prv, :], kr[cur, :], v_ref[prv, :], v_ref[cur, :])
            dq2, dkp, dkc, dvp, dvc = vjp((do_ref[cur, :], dl_ref[cur, :]))
            dqr[cur, :] = dq2
            dkr[cur, :] = dkc
            dvr[cur, :] = dvc
            if not first:
                dkr[prv, :] += dkp
                dvr[prv, :] += dvp
        d_ref[0] = _rope_t(dqr[...], cosf, s_lo, s_hi).astype(d_ref.dtype)
        d_ref[1] = _rope_t(dkr[...], cosf, s_lo, s_hi).astype(d_ref.dtype)
        d_ref[2] = dvr[...].astype(d_ref.dtype)

    def sec(n):
        return pl.BlockSpec((L, 128), lambda p: (0, p + 4 * n))

    return pcall(
        body, plan, grid=(4,),
        in_specs=[sec(0), sec(1), sec(2), pl.BlockSpec((L, 1), lambda p: (0, 0)), pl.BlockSpec((1, 128), lambda p: (0, 0)),
                  sec(0), sec(0)],
        out_specs=pl.BlockSpec((3, L, 128), lambda p: (0, 0, p)), out_shape=S((3, L, 512), BF),
        scratch_shapes=[pltpu.VMEM((L, 128), f32)] * 5,
        sem=("parallel",), name=f"attn_bwd{g}", args=[qkv, qkv, qkv, pos, invf, do, dl])


def _merge(o0, o1, o2, l0, l1, l2):
    m = jnp.maximum(jnp.maximum(l0, l1), l2)
    e0, e1, e2 = jnp.exp(l0 - m), jnp.exp(l1 - m), jnp.exp(l2 - m)
    return (e0 * o0 + e1 * o1 + e2 * o2) / (e0 + e1 + e2)


def attn_merge_fwd(os_, ls_):
    def body(o0, o1, o2, l0, l1, l2, o_ref):
        o_ref[...] = _merge(o0[...], o1[...], o2[...], l0[...], l1[...], l2[...]).astype(o_ref.dtype)

    blk = pl.BlockSpec((TR, 512), lambda i: (i, 0))
    return pl.pallas_call(
        body, grid=(L // TR,), in_specs=[blk] * 6, out_specs=blk, out_shape=S((L, 512), BF),
        compiler_params=_cp(("parallel",)), name="attn_merge_fwd")(*os_, *ls_)


def attn_merge_bwd(os_, ls_, do, plan=None):
    def body(o0, o1, o2, l0, l1, l2, g_ref, *outs):
        _, vjp = jax.vjp(_merge, o0[...], o1[...], o2[...], l0[...], l1[...], l2[...])
        for o_ref, v in zip(outs, vjp(g_ref[...])):
            o_ref[...] = v

    blk = pl.BlockSpec((TR, 512), lambda i: (i, 0))
    outs = pcall(body, plan, grid=(L // TR,), in_specs=[blk] * 7, out_specs=[blk] * 6, out_shape=[S((L, 512), f32)] * 6,
                 sem=("parallel",), name="attn_merge_bwd", args=[*os_, *ls_, do])
    return outs[:3], outs[3:]


def _invf_lanes():
    half = 8
    inv = ROPE_THETA ** (-np.arange(half, dtype=np.float32) * 2.0 / 16.0)
    lane = np.arange(128) % 64
    return jnp.asarray(np.where(lane < 16, inv[lane % 8], 0.0).astype(np.float32)[None, :])


def hosted(C, host, fn):
    p = C.plan(host) if C is not None else None
    out = fn(p)
    if p is not None:
        C.done(p)
    return out


def _ffn_fwd(h, g_row, W, cb, layer, C):
    hn = rms_fwd(h, g_row, f"rms_ffn{layer}")
    hu = hosted(C, f"ffn_in{layer}", lambda p: matmul(hn, W[("ffn_w_in", layer)], mode="nn", tm=1024, tn=1408, tk=1024,
                                                      plan=p, name=f"ffn_in{layer}"))
    act = hosted(C, f"convact_fwd{layer}", lambda p: convact_fwd(hu, W[("ffn_conv_w", layer)], cb, layer, plan=p))
    h2 = matmul(act, W[("ffn_w_out", layer)], mode="nn", tm=1024, tn=1024, tk=1408, add=h, name=f"ffn_out{layer}")
    return h2, (hn, hu, act)


def _ffn_bwd(dh, h, g_row, W, cb, saved, layer, C, G):
    hn, hu, act = saved
    w_in, w_out = W[("ffn_w_in", layer)], W[("ffn_w_out", layer)]
    dact = matmul(dh, w_out, mode="nt", tm=1024, tn=1408, tk=1024, name=f"ffn_out_dx{layer}")
    G[("ffn_w_out", layer)] = matmul(act, dh, mode="tn", tm=1408, tn=1024, tk=512, out_dtype=BF, name=f"ffn_out_dw{layer}")
    dhu, G[("ffn_conv_w", layer)], g_cb = hosted(
        C, f"convact_bwd{layer}", lambda p: convact_bwd(hu, W[("ffn_conv_w", layer)], cb, dact, layer, plan=p))
    dhn = hosted(C, f"ffn_in_dx{layer}", lambda p: matmul(dhu, w_in, mode="nt", tm=1024, tn=1024, tk=1408, plan=p,
                                                         name=f"ffn_in_dx{layer}"))
    G[("ffn_w_in", layer)] = hosted(C, f"ffn_in_dw{layer}", lambda p: matmul(
        hn, dhu, mode="tn", tm=1024, tn=1408, tk=512, out_dtype=BF, plan=p, name=f"ffn_in_dw{layer}"))
    dh2, g_norm = rms_bwd(h, g_row, [dhn], dh, f"rms_ffn_bwd{layer}")
    return dh2, g_cb, g_norm


def local_step(x, pos, tgt, sm, W, C=None):
    G = C.grads if C is not None else {}
    nm, nf = sm["norm_mix"], sm["norm_ffn"]
    invf = _invf_lanes()
    are = sm["s5_A_re"].reshape(NST, 1)
    aim = sm["s5_A_im"].reshape(NST, 1)
    ldt = sm["s5_log_dt"].reshape(1, 32)
    bre = sm["s5_B_re"].reshape(NST, 16)
    bim = sm["s5_B_im"].reshape(NST, 16)
    cre = jnp.swapaxes(sm["s5_C_re"][0], 1, 2).reshape(NST, 16)
    cim = jnp.swapaxes(sm["s5_C_im"][0], 1, 2).reshape(NST, 16)
    drow = sm["s5_D"].reshape(1, S5W)
    wbr, wbi, wcr, wci, abr, abi = s5_params_fwd(are, aim, ldt, bre, bim, cre, cim)
    hn0 = rms_fwd(x, nm[0:1], "rms_mix0")
    cb3 = sm["ffn_conv_b3"]
    proj = hosted(C, "mix_in", lambda p: matmul(hn0, W[("mix_w_in", 0)], mode="nn", tm=1024, tn=1280, tk=1024, plan=p, name="mix_in"))
    xs_re, xs_im, y5 = hosted(C, "s5_scan_fwd", lambda p: s5_scan_fwd(proj, wbr, wbi, wcr, wci, abr, abi, drow, plan=p))
    oa = s5_glu_fwd(y5, W[("s5_glu_w", 0)], sm["s5_glu_b"])
    ob, ssave = hosted(C, "hgrn_fwd", lambda p: hgrn_fwd(proj, sm["hgrn_gamma"], sm["hgrn_norm"], plan=p))
    cat = jnp.concatenate([oa, ob], axis=1)
    h1 = matmul(cat, W[("mix_w_out", 0)], mode="nn", tm=1024, tn=1024, tk=1024, add=x, name="mix_out")
    h2, ffn0 = _ffn_fwd(h1, nf[0:1], W, cb3, 0, C)
    hn2 = rms_fwd(h2, nm[1:2], "rms_mix1")
    wqkv = W[("att_w_qkv", 0)]
    hn2_g, pos_g, qkv_g, o_g, l_g = [], [], [], [], []
    for g, dil in enumerate(DILS):
        hn2_g.append(deinterleave(hn2, dil))
        pos_g.append(deinterleave(pos, dil))
        qkv_g.append(matmul(hn2_g[g], wqkv, mode="nn", tm=1024, tn=512, tk=1024, dims=(L, 1536, D),
                            b_spec=pl.BlockSpec((D, 512), lambda i, j, k, g=g: (0, 3 * j + g)), name=f"att_qkv{g}"))
        o_c, l_c = attn_fwd(qkv_g[g], pos_g[g], invf, g)
        o_g.append(interleave(o_c, dil))
        l_g.append(interleave(l_c, dil))
    o = attn_merge_fwd(o_g, l_g)
    h3 = matmul(o, W[("att_w_o", 0)], mode="nn", tm=1024, tn=1024, tk=512, add=h2, name="att_o")
    h4, ffn1 = _ffn_fwd(h3, nf[1:2], W, cb3, 1, C)
    loss, dh, g_nfinal = loss_head(h4, sm["norm_final"].reshape(1, D), tgt)
    dh, g_cb1, g_nf1 = _ffn_bwd(dh, h3, nf[1:2], W, cb3, ffn1, 1, C, G)
    do = matmul(dh, W[("att_w_o", 0)], mode="nt", tm=1024, tn=512, tk=1024, name="att_o_dx")
    G[("att_w_o", 0)] = matmul(o, dh, mode="tn", tm=512, tn=1024, tk=512, out_dtype=BF, name="att_o_dw")
    do_g, dl_g = hosted(C, "attn_merge_bwd", lambda p: attn_merge_bwd(o_g, l_g, do, plan=p))
    dhn2_g, gq = [], []
    for g, dil in enumerate(DILS):
        d3 = hosted(C, f"attn_bwd{g}", lambda p: attn_bwd(qkv_g[g], pos_g[g], invf, deinterleave(do_g[g], dil),
                                                        deinterleave(dl_g[g], dil), g, plan=p))
        dx = matmul(d3, wqkv, mode="nt", tm=1024, tn=1024, tk=512, dims=(L, D, 1536),
                    a_spec=pl.BlockSpec((None, 1024, 512), lambda i, j, k: (k, i, 0)),
                    b_spec=pl.BlockSpec((D, 512), lambda i, j, k, g=g: (0, 3 * k + g)), name=f"att_qkv_dx{g}")
        dhn2_g.append(interleave(dx, dil))
        gq.append(matmul(hn2_g[g], d3, mode="tn", tm=1024, tn=512, tk=512, out_dtype=BF, dims=(D, 1536, L),
                         b_spec=pl.BlockSpec((None, 512, 512), lambda i, j, k: (j, k, 0)), name=f"att_qkv_dw{g}"))
    G[("att_w_qkv", 0)] = jnp.concatenate([gq[g][:, 512 * s:512 * (s + 1)] for s in range(3) for g in range(3)], axis=1)
    dh, g_nm1 = rms_bwd(h2, nm[1:2], dhn2_g, dh, "rms_mix_bwd1")
    dh, g_cb0, g_nf0 = _ffn_bwd(dh, h1, nf[0:1], W, cb3, ffn0, 0, C, G)
    dmix = matmul(dh, W[("mix_w_out", 0)], mode="nt", tm=1024, tn=1024, tk=1024, name="mix_out_dx")
    G[("mix_w_out", 0)] = matmul(cat, dh, mode="tn", tm=1024, tn=1024, tk=512, out_dtype=BF, name="mix_out_dw")
    dy5, g_glu_w, g_glu_b = s5_glu_bwd(y5, W[("s5_glu_w", 0)], sm["s5_glu_b"], dmix)
    G[("s5_glu_w", 0)] = g_glu_w.astype(BF)
    du, gwbr, gwbi, gwcr, gwci, gabr, gabi, g_d = hosted(C, "s5_scan_bwd", lambda p: s5_scan_bwd(
        dy5, proj, xs_re, xs_im, wbr, wbi, wcr, wci, abr, abi, drow, plan=p))
    g_are, g_aim, g_ldt, g_bre, g_bim, g_cre, g_cim = s5_params_bwd(are, aim, ldt, bre, bim, cre, cim,
                                                                   (gwbr, gwbi, gwcr, gwci, gabr, gabi))
    dproj, g_gamma, g_hnorm = hosted(C, "hgrn_bwd", lambda p: hgrn_bwd(proj, sm["hgrn_gamma"], sm["hgrn_norm"], ssave, dmix, du,
                                                                       plan=p))
    dhn0 = hosted(C, "mix_in_dx", lambda p: matmul(dproj, W[("mix_w_in", 0)], mode="nt", tm=1024, tn=1024, tk=1280, plan=p,
                                                  name="mix_in_dx"))
    G[("mix_w_in", 0)] = matmul(hn0, dproj, mode="tn", tm=1024, tn=1280, tk=512, out_dtype=BF, name="mix_in_dw")
    gx, g_nm0 = hosted(C, "rms_mix_bwd0", lambda p: rms_bwd(x, nm[0:1], [dhn0], dh, "rms_mix_bwd0", plan=p))
    small = {
        "norm_mix": jnp.concatenate([g_nm0, g_nm1], axis=0), "norm_ffn": jnp.concatenate([g_nf0, g_nf1], axis=0),
        "norm_final": g_nfinal.reshape(D),
        "s5_A_re": g_are.reshape(1, 32, 64), "s5_A_im": g_aim.reshape(1, 32, 64), "s5_log_dt": g_ldt.reshape(1, 32),
        "s5_B_re": g_bre.reshape(1, 32, 64, 16), "s5_B_im": g_bim.reshape(1, 32, 64, 16),
        "s5_C_re": jnp.swapaxes(g_cre.reshape(1, 32, 64, 16), 2, 3), "s5_C_im": jnp.swapaxes(g_cim.reshape(1, 32, 64, 16), 2, 3),
        "s5_D": g_d.reshape(1, 32, 16), "s5_glu_b": g_glu_b, "hgrn_gamma": g_gamma, "hgrn_norm": g_hnorm,
        "ffn_conv_b": jnp.concatenate([g_cb0, g_cb1], axis=0),
    }
    return loss, gx, G, small


BIG = ("mix_w_in", "mix_w_out", "s5_glu_w", "att_w_qkv", "att_w_o", "ffn_w_in", "ffn_w_out", "ffn_conv_w")
SMALL = ("norm_mix", "norm_ffn", "norm_final", "s5_A_re", "s5_A_im", "s5_log_dt", "s5_B_re", "s5_B_im", "s5_C_re", "s5_C_im",
         "s5_D", "s5_glu_b", "hgrn_gamma", "hgrn_norm", "ffn_conv_b")


def cast_bf16(w, name):
    nl, r, c = w.shape
    w2 = w.reshape(nl * r, c)
    tr = 256 if (nl * r) % 256 == 0 else nl * r

    def body(w_ref, o_ref):
        o_ref[...] = w_ref[...].astype(BF)

    out = pl.pallas_call(
        body, grid=(nl * r // tr,), in_specs=[pl.BlockSpec((tr, c), lambda i: (i, 0))],
        out_specs=pl.BlockSpec((tr, c), lambda i: (i, 0)), out_shape=S((nl * r, c), BF),
        compiler_params=_cp(("parallel",)), name=name)(w2)
    return out.reshape(nl, r, c)


SCHEDULE = {
    "gather0": [("G", "mix_w_in", 0)],
    "mix_in": [("G", "mix_w_out", 0), ("G", "s5_glu_w", 0)],
    "s5_scan_fwd": [("G", "ffn_w_in", 0)],
    "hgrn_fwd": [("G", "ffn_w_out", 0), ("G", "ffn_conv_w", 0), ("G", "ffn_conv_w", 1), ("G", "att_w_qkv", 0)],
    "ffn_in0": [("G", "att_w_o", 0), ("G", "ffn_w_out", 1)],
    "convact_fwd0": [("G", "ffn_w_in", 1)],
    "convact_bwd1": [("A", "ffn_w_out", 1)],
    "ffn_in_dx1": [("B", "ffn_w_out", 1)],
    "attn_merge_bwd": [("A", "att_w_o", 0), ("A", "ffn_conv_w", 1)],
    "attn_bwd0": [("A", "ffn_w_in", 1, (0, 2))],
    "attn_bwd1": [("A", "ffn_w_in", 1, (1, 2)), ("B", "att_w_o", 0), ("B", "ffn_conv_w", 1)],
    "attn_bwd2": [("B", "ffn_w_in", 1)],
    "convact_bwd0": [("A", "att_w_qkv", 0, (0, 2))],
    "ffn_in_dx0": [("A", "att_w_qkv", 0, (1, 2))],
    "ffn_in_dw0": [("B", "att_w_qkv", 0)],
    "s5_scan_bwd": [("A", "ffn_w_out", 0), ("A", "mix_w_out", 0), ("A", "s5_glu_w", 0), ("A", "ffn_conv_w", 0)],
    "hgrn_bwd": [("A", "ffn_w_in", 0), ("B", "ffn_w_out", 0), ("B", "mix_w_out", 0), ("B", "s5_glu_w", 0), ("B", "ffn_conv_w", 0)],
    "mix_in_dx": [("B", "ffn_w_in", 0)],
    "rms_mix_bwd0": [("A", "mix_w_in", 0, (0, 2))],
    "adam_ffn_w_in": [("A", "mix_w_in", 0, (1, 2)), ("A", "small", 0)],
    "adam_ffn_w_out": [("B", "mix_w_in", 0), ("B", "small", 0)],
}


class Comm:
    def __init__(self, shards, shapes):
        self.shards, self.shapes = shards, shapes
        self.W, self.grads, self.slots = {}, {}, {}
        self.small = None

    def plan(self, host):
        items = SCHEDULE.get(host)
        if not items:
            return None
        p = Plan()
        for it in items:
            kind, name, l = it[:3]
            part, parts = it[3] if len(it) > 3 else (0, 1)
            if name == "small":
                kdst = p.buf("slots:small", arr=self.slots.get("small"), shape=S((8,) + self.small.shape, f32), write=True)
                if kind == "A":
                    ReduceOp(p, p.buf("g:small", arr=self.small), kdst, None, self.small.shape, False, 0, 0, whole=True)
                else:
                    ForwardOp(p, kdst, None, whole=True)
                continue
            nl, R, C_ = self.shapes[name]
            rows = name in ROW_SHARDED
            r0, nr = part * (R // parts), R // parts
            if kind == "G":
                sh = self.shards[name]
                kdst = p.buf(f"W:{name}:{l}", arr=self.W.get((name, l)), shape=S((4 * R, C_) if rows else (R, 4 * C_), sh.dtype),
                             write=True)
                GatherOp(p, p.buf("shard:" + name, arr=sh), kdst, l, self.shapes[name], rows, r0, nr, split=(nr % 32 == 0))
            else:
                g = self.grads[(name, l)]
                kdst = p.buf("slots:" + name, arr=self.slots.get(name), shape=S((8, nl, R, C_), g.dtype), write=True)
                if kind == "A":
                    ReduceOp(p, p.buf(f"g:{name}:{l}", arr=g), kdst, l, self.shapes[name], rows, r0, nr)
                else:
                    ForwardOp(p, kdst, l)
        return p

    def done(self, p):
        for k, arr in p.out.items():
            tag, name = k.split(":")[:2]
            if tag == "W":
                self.W[(name, int(k.split(":")[2]))] = arr
            else:
                self.slots[name] = arr


def _adamw(w, g, m, v):
    m = B1 * m + (1.0 - B1) * g
    v = B2 * v + (1.0 - B2) * jnp.square(g)
    m_hat = m / (1.0 - B1 ** STEP)
    v_hat = v / (1.0 - B2 ** STEP)
    return -LR * (m_hat / (jnp.sqrt(v_hat) + AEPS) + WD * w), m, v


def adam_big(w, m, v, slots, name, plan=None):
    nl, R, C = w.shape
    tr = 128 if R % 128 == 0 else (64 if R % 64 == 0 else R)

    def body(w_ref, m_ref, v_ref, s_ref, g_ref, d_ref, nm_ref, nv_ref):
        g = s_ref[0].astype(f32)
        for s in range(1, 8):
            g = g + s_ref[s].astype(f32)
        d, nm_, nv_ = _adamw(w_ref[...], g, m_ref[...], v_ref[...])
        g_ref[...] = g
        d_ref[...] = d
        nm_ref[...] = nm_
        nv_ref[...] = nv_

    blk = pl.BlockSpec((None, tr, C), lambda l, i: (l, i, 0))
    return pcall(body, plan, grid=(nl, R // tr),
                 in_specs=[blk, blk, blk, pl.BlockSpec((8, None, tr, C), lambda l, i: (0, l, i, 0))],
                 out_specs=[blk] * 4, out_shape=[S((nl, R, C), f32)] * 4,
                 sem=("parallel", "parallel"), name=name, args=[w, m, v, slots])


def adam_small(w, m, v, slots):
    R = w.shape[0]
    tr = 256

    def body(w_ref, m_ref, v_ref, s_ref, g_ref, d_ref, nm_ref, nv_ref):
        g = s_ref[0]
        for s in range(1, 8):
            g = g + s_ref[s]
        d, nm_, nv_ = _adamw(w_ref[...], g, m_ref[...], v_ref[...])
        g_ref[...] = g
        d_ref[...] = d
        nm_ref[...] = nm_
        nv_ref[...] = nv_

    blk = pl.BlockSpec((tr, 128), lambda i: (i, 0))
    return pl.pallas_call(
        body, grid=(R // tr,), in_specs=[blk, blk, blk, pl.BlockSpec((8, tr, 128), lambda i: (0, i, 0))],
        out_specs=[blk] * 4, out_shape=[S((R, 128), f32)] * 4,
        compiler_params=_cp(("parallel",)), name="adam_small")(w, m, v, slots)


def _pack(d):
    flat = jnp.concatenate([d[n].reshape(-1) for n in SMALL])
    n = flat.shape[0]
    rows = -(-n // (256 * 128)) * 256
    return jnp.pad(flat, (0, rows * 128 - n)).reshape(rows, 128)


def _unpack(p, like):
    flat = p.reshape(-1)
    out, off = {}, 0
    for n in SMALL:
        sz = math.prod(like[n].shape)
        out[n] = flat[off:off + sz].reshape(like[n].shape)
        off += sz
    return out


def kernel(x, positions, norm_mix, norm_ffn, norm_final, mix_w_in, mix_w_out, s5_A_re, s5_A_im, s5_log_dt, s5_B_re, s5_B_im, s5_C_re, s5_C_im, s5_D, s5_glu_w, s5_glu_b, hgrn_gamma, hgrn_norm, att_w_qkv, att_w_o, ffn_w_in, ffn_conv_w, ffn_conv_b, ffn_w_out, loss_target, m_norm_mix, m_norm_ffn, m_norm_final, m_mix_w_in, m_mix_w_out, m_s5_A_re, m_s5_A_im, m_s5_log_dt, m_s5_B_re, m_s5_B_im, m_s5_C_re, m_s5_C_im, m_s5_D, m_s5_glu_w, m_s5_glu_b, m_hgrn_gamma, m_hgrn_norm, m_att_w_qkv, m_att_w_o, m_ffn_w_in, m_ffn_conv_w, m_ffn_conv_b, m_ffn_w_out, v_norm_mix, v_norm_ffn, v_norm_final, v_mix_w_in, v_mix_w_out, v_s5_A_re, v_s5_A_im, v_s5_log_dt, v_s5_B_re, v_s5_B_im, v_s5_C_re, v_s5_C_im, v_s5_D, v_s5_glu_w, v_s5_glu_b, v_hgrn_gamma, v_hgrn_norm, v_att_w_qkv, v_att_w_o, v_ffn_w_in, v_ffn_conv_w, v_ffn_conv_b, v_ffn_w_out):
    a = dict(locals())
    weights = BIG + SMALL
    w = {n: a[n] for n in weights}
    m = {n: a["m_" + n] for n in weights}
    v = {n: a["v_" + n] for n in weights}
    shards = {n: (w[n] if n == "ffn_conv_w" else cast_bf16(w[n], "cast_" + n)) for n in BIG}
    C = Comm(shards, {n: w[n].shape for n in BIG})
    hosted(C, "gather0", lambda p: comm_only(p, "gather0"))
    sm = {n: w[n] for n in SMALL}
    sm["ffn_conv_b3"] = ffn_conv_b.reshape(2, 1, 2 * DFF)
    loss, gx, _, gsmall = local_step(x[0], positions.reshape(L, 1), loss_target[0], sm, C.W, C)
    C.small = _pack(gsmall)
    res = {}
    for n in ("ffn_w_in", "ffn_w_out", "att_w_qkv", "att_w_o", "mix_w_out", "s5_glu_w", "ffn_conv_w", "mix_w_in"):
        res[n] = hosted(C, "adam_" + n, lambda p: adam_big(w[n], m[n], v[n], C.slots[n], "adam_" + n, plan=p))
    packed = adam_small(_pack({n: w[n] for n in SMALL}), _pack({n: m[n] for n in SMALL}), _pack({n: v[n] for n in SMALL}),
                        C.slots["small"])
    small_out = [_unpack(p, {n: w[n] for n in SMALL}) for p in packed]
    for n in SMALL:
        res[n] = tuple(so[n] for so in small_out)
    total = lax.psum(loss[0, 0], ("x", "y", "c"))
    order = ("norm_mix", "norm_ffn", "norm_final", "mix_w_in", "mix_w_out", "s5_A_re", "s5_A_im", "s5_log_dt", "s5_B_re", "s5_B_im",
             "s5_C_re", "s5_C_im", "s5_D", "s5_glu_w", "s5_glu_b", "hgrn_gamma", "hgrn_norm", "att_w_qkv", "att_w_o", "ffn_w_in",
             "ffn_conv_w", "ffn_conv_b", "ffn_w_out")
    return (total, gx[None], *[res[n][0] for n in order], *[res[n][1] for n in order], *[res[n][2] for n in order],
            *[res[n][3] for n in order])
```

```python
import functools
import math

import numpy as np
import jax
import jax.numpy as jnp
from jax import lax
from jax.experimental import pallas as pl
from jax.experimental.pallas import tpu as pltpu

f32 = jnp.float32
BF = jnp.bfloat16
HI = lax.Precision.HIGHEST
S = jax.ShapeDtypeStruct
MESH = pl.DeviceIdType.MESH

L = 2048
D = 1024
EPS = 1e-6
S5W = 512
NST = 2048
HGC = 64
DFF = 2816
ROPE_THETA = 500000.0
LR, B1, B2, AEPS, WD, STEP = 0.001, 0.9, 0.999, 1e-08, 0.01, 10
VMEM_LIMIT = 56 * 1024 * 1024


def _cp(sem=None):
    return pltpu.CompilerParams(dimension_semantics=sem, vmem_limit_bytes=VMEM_LIMIT)


ANY = pl.BlockSpec(memory_space=pl.ANY)
ROW_SHARDED = ("mix_w_out", "s5_glu_w", "ffn_w_out")


def _coords():
    x, y, c = lax.axis_index("x"), lax.axis_index("y"), lax.axis_index("c")
    return x, y, c, 2 * x + y, [(1 - x, y), (x, 1 - y), (1 - x, 1 - y)]


def _rows(start, n):
    return pl.ds(start if isinstance(start, int) else pl.multiple_of(start, 8), n)


def _cols(q, n):
    return pl.ds(pl.multiple_of(q * n, 128), n)


class Plan:
    def __init__(self):
        self.bufs, self.ops, self.nsem, self.out = {}, [], 0, {}

    def buf(self, key, arr=None, shape=None, write=False):
        b = self.bufs.setdefault(key, dict(arr=arr, shape=shape, write=False))
        b["write"] = b["write"] or write
        return key

    def add(self, op):
        op.base = self.nsem
        self.nsem += op.nsem
        self.ops.append(op)


class GatherOp:
    nsem = 13

    def __init__(self, plan, ksrc, kdst, l, shard_shape, rows, r0, nr, split):
        self.ksrc, self.kdst, self.l, (_, self.R, self.C), self.rows, self.r0, self.nr, self.split = (
            ksrc, kdst, l, shard_shape, rows, r0, nr, split)
        self.h = nr // 2 if split else nr
        plan.add(self)

    def _dst(self, R_, q, start, n):
        if self.rows:
            return R_[self.kdst].at[_rows(q * self.R + start, n), :]
        return R_[self.kdst].at[_rows(start, n), _cols(q, self.C)]

    def _mine(self, c):
        return self.r0 + (c * self.h if self.split else 0)

    def _theirs(self, c):
        return self.r0 + ((1 - c) * self.h if self.split else 0)

    def _copies(self, R_, sems):
        x, y, c, me, others = _coords()
        src = R_[self.ksrc]
        local = pltpu.make_async_copy(src.at[self.l, _rows(self.r0, self.nr), :], self._dst(R_, me, self.r0, self.nr),
                                      sems.at[self.base + 12])
        send, fwd = [], []
        for k, (px, py) in enumerate(others):
            q = 2 * px + py
            send.append((
                pltpu.make_async_remote_copy(src.at[self.l, _rows(self._mine(c), self.h), :], self._dst(R_, me, self._mine(c), self.h),
                                             sems.at[self.base + k], sems.at[self.base + 3 + k], device_id=(px, py, c), device_id_type=MESH),
                pltpu.make_async_remote_copy(src.at[self.l, _rows(self._mine(c), self.h), :], self._dst(R_, q, self._mine(c), self.h),
                                             sems.at[self.base + k], sems.at[self.base + 3 + k], device_id=(px, py, c), device_id_type=MESH)))
            fwd.append((
                pltpu.make_async_remote_copy(self._dst(R_, q, self._mine(c), self.h), self._dst(R_, q, self._mine(c), self.h),
                                             sems.at[self.base + 6 + k], sems.at[self.base + 9 + k], device_id=(x, y, 1 - c), device_id_type=MESH),
                pltpu.make_async_remote_copy(self._dst(R_, q, self._theirs(c), self.h), self._dst(R_, q, self._theirs(c), self.h),
                                             sems.at[self.base + 6 + k], sems.at[self.base + 9 + k], device_id=(x, y, 1 - c), device_id_type=MESH)))
        return local, send, fwd

    def start(self, R_, sems):
        local, send, _ = self._copies(R_, sems)
        local.start()
        for out, _ in send:
            out.start()

    def finish(self, R_, sems):
        local, send, fwd = self._copies(R_, sems)
        for k in range(3):
            send[k][1].wait_recv()
            if self.split:
                fwd[k][0].start()
        for k in range(3):
            if self.split:
                fwd[k][1].wait_recv()
                fwd[k][0].wait_send()
            send[k][0].wait_send()
        local.wait()


class ReduceOp:
    nsem = 7

    def __init__(self, plan, ksrc, kdst, l, shard_shape, rows, r0, nr, whole=False):
        self.ksrc, self.kdst, self.l, (self.R, self.C), self.rows, self.r0, self.nr, self.whole = (
            ksrc, kdst, l, shard_shape[-2:], rows, r0, nr, whole)
        plan.add(self)

    def _piece(self, R_, q):
        g = R_[self.ksrc]
        if self.whole:
            return g
        if self.rows:
            return g.at[_rows(q * self.R + self.r0, self.nr), :]
        return g.at[_rows(self.r0, self.nr), _cols(q, self.C)]

    def _slot(self, R_, s):
        if self.whole:
            return R_[self.kdst].at[s]
        return R_[self.kdst].at[s, self.l, _rows(self.r0, self.nr), :]

    def _copies(self, R_, sems):
        x, y, c, me, others = _coords()
        local = pltpu.make_async_copy(self._piece(R_, me), self._slot(R_, 2 * me + c), sems.at[self.base + 6])
        send = []
        for k, (px, py) in enumerate(others):
            q = 2 * px + py
            send.append((
                pltpu.make_async_remote_copy(self._piece(R_, q), self._slot(R_, 2 * me + c), sems.at[self.base + k],
                                             sems.at[self.base + 3 + k], device_id=(px, py, c), device_id_type=MESH),
                pltpu.make_async_remote_copy(self._piece(R_, q), self._slot(R_, 2 * q + c), sems.at[self.base + k],
                                             sems.at[self.base + 3 + k], device_id=(px, py, c), device_id_type=MESH)))
        return local, send

    def start(self, R_, sems):
        local, send = self._copies(R_, sems)
        local.start()
        for out, _ in send:
            out.start()

    def finish(self, R_, sems):
        local, send = self._copies(R_, sems)
        local.wait()
        for out, inn in send:
            inn.wait_recv()
            out.wait_send()


class ForwardOp:
    nsem = 8

    def __init__(self, plan, kdst, l, whole=False):
        self.kdst, self.l, self.whole = kdst, l, whole
        plan.add(self)

    def _slot(self, R_, s):
        return R_[self.kdst].at[s] if self.whole else R_[self.kdst].at[s, self.l]

    def _copies(self, R_, sems):
        x, y, c, me, others = _coords()
        return [(pltpu.make_async_remote_copy(self._slot(R_, 2 * q + c), self._slot(R_, 2 * q + c), sems.at[self.base + q],
                                              sems.at[self.base + 4 + q], device_id=(x, y, 1 - c), device_id_type=MESH),
                 pltpu.make_async_remote_copy(self._slot(R_, 2 * q + 1 - c), self._slot(R_, 2 * q + 1 - c), sems.at[self.base + q],
                                              sems.at[self.base + 4 + q], device_id=(x, y, 1 - c), device_id_type=MESH))
                for q in range(4)]

    def start(self, R_, sems):
        for out, _ in self._copies(R_, sems):
            out.start()

    def finish(self, R_, sems):
        for out, inn in self._copies(R_, sems):
            inn.wait_recv()
            out.wait_send()


def pcall(body, plan, *, grid, in_specs, out_specs, out_shape, scratch_shapes=(), sem, name, args):
    multi = isinstance(out_shape, (list, tuple))
    if plan is None or not plan.ops:
        return pl.pallas_call(body, grid=grid, in_specs=in_specs, out_specs=out_specs, out_shape=out_shape,
                              scratch_shapes=list(scratch_shapes), compiler_params=_cp(sem), name=name)(*args)
    outs = list(out_shape) if multi else [out_shape]
    ospecs = list(out_specs) if multi else [out_specs]
    kin = [k for k, b in plan.bufs.items() if b["arr"] is not None]
    kout = [k for k, b in plan.bufs.items() if b["write"]]
    n_in, n_out, n_scr = len(in_specs), len(outs), len(scratch_shapes)

    def wrapped(*refs):
        o0 = n_in + len(kin)
        s0 = o0 + n_out + len(kout)
        R_ = dict(zip(kin, refs[n_in:o0]))
        R_.update(zip(kout, refs[o0 + n_out:s0]))
        sems = refs[s0 + n_scr]
        first = functools.reduce(jnp.logical_and, [pl.program_id(d) == 0 for d in range(len(grid))])
        last = functools.reduce(jnp.logical_and, [pl.program_id(d) == grid[d] - 1 for d in range(len(grid))])

        @pl.when(first)
        def _():
            for op in plan.ops:
                op.start(R_, sems)

        body(*refs[:n_in], *refs[o0:o0 + n_out], *refs[s0:s0 + n_scr])

        @pl.when(last)
        def _():
            for op in plan.ops:
                op.finish(R_, sems)

    def shape_of(k):
        b = plan.bufs[k]
        return S(b["arr"].shape, b["arr"].dtype) if b["arr"] is not None else b["shape"]

    res = pl.pallas_call(
        wrapped, grid=grid, in_specs=list(in_specs) + [ANY] * len(kin), out_specs=ospecs + [ANY] * len(kout),
        out_shape=outs + [shape_of(k) for k in kout],
        scratch_shapes=list(scratch_shapes) + [pltpu.SemaphoreType.DMA((plan.nsem,))],
        input_output_aliases={n_in + kin.index(k): n_out + kout.index(k) for k in kout if plan.bufs[k]["arr"] is not None},
        compiler_params=pltpu.CompilerParams(dimension_semantics=("arbitrary",) * len(grid), vmem_limit_bytes=VMEM_LIMIT,
                                             has_side_effects=True),
        name=name)(*args, *[plan.bufs[k]["arr"] for k in kin])
    plan.out = dict(zip(kout, res[n_out:]))
    return list(res[:n_out]) if multi else res[0]


def _dg(a, b, ca, cb):
    return lax.dot_general(a.astype(BF), b.astype(BF), (((ca,), (cb,)), ((), ())), preferred_element_type=f32)


@jax.custom_vjp
def dot_nn(a, b):
    return _dg(a, b, 1, 0)


@jax.custom_vjp
def dot_nt(a, b):
    return _dg(a, b, 1, 1)


@jax.custom_vjp
def dot_tn(a, b):
    return _dg(a, b, 0, 0)


dot_nn.defvjp(lambda a, b: (dot_nn(a, b), (a, b)),
              lambda r, g: (dot_nt(g, r[1]).astype(r[0].dtype), dot_tn(r[0], g).astype(r[1].dtype)))
dot_nt.defvjp(lambda a, b: (dot_nt(a, b), (a, b)),
              lambda r, g: (dot_nn(g, r[1]).astype(r[0].dtype), dot_tn(g, r[0]).astype(r[1].dtype)))
dot_tn.defvjp(lambda a, b: (dot_tn(a, b), (a, b)),
              lambda r, g: (dot_nt(r[1], g).astype(r[0].dtype), dot_nn(r[0], g).astype(r[1].dtype)))


def matmul(a, b, *, mode, tm, tn, tk, out_dtype=f32, add=None, b_lead=None, a_spec=None, b_spec=None, dims=None, plan=None, name):
    a_over, b_over = a_spec, b_spec
    if mode == "nn":
        (M, K), N = a.shape[-2:], b.shape[-1]
        a_spec = pl.BlockSpec((tm, tk), lambda i, j, k: (i, k))
        b_blk, b_idx, ca, cb = (tk, tn), (lambda i, j, k: (k, j)), 1, 0
    elif mode == "nt":
        (M, K), N = a.shape[-2:], b.shape[-2]
        a_spec = pl.BlockSpec((tm, tk), lambda i, j, k: (i, k))
        b_blk, b_idx, ca, cb = (tn, tk), (lambda i, j, k: (j, k)), 1, 1
    else:
        (K, M), N = a.shape[-2:], b.shape[-1]
        a_spec = pl.BlockSpec((tk, tm), lambda i, j, k: (k, i))
        b_blk, b_idx, ca, cb = (tk, tn), (lambda i, j, k: (k, j)), 0, 0
    if dims is not None:
        M, N, K = dims
    assert M % tm == 0 and N % tn == 0 and K % tk == 0, (name, M, N, K, tm, tn, tk)
    if b_lead is None:
        b_spec = pl.BlockSpec(b_blk, b_idx)
    else:
        b_spec = pl.BlockSpec((None,) + b_blk, lambda i, j, k: (b_lead,) + b_idx(i, j, k))
    if a_over is not None:
        a_spec = a_over
    if b_over is not None:
        b_spec = b_over
    nk = K // tk
    has_add = add is not None

    def body(*refs):
        a_ref, b_ref = refs[0], refs[1]
        add_ref = refs[2] if has_add else None
        o_ref = refs[2 + has_add]
        p = _dg(a_ref[...], b_ref[...], ca, cb)

        def fin(v):
            if has_add:
                v = v + add_ref[...].astype(f32)
            o_ref[...] = v.astype(o_ref.dtype)

        if nk == 1:
            fin(p)
        else:
            acc = refs[3 + has_add]
            k = pl.program_id(2)

            @pl.when(k == 0)
            def _():
                acc[...] = p

            @pl.when(k > 0)
            def _():
                acc[...] += p

            @pl.when(k == nk - 1)
            def _():
                fin(acc[...])

    in_specs = [a_spec, b_spec]
    args = [a, b]
    if has_add:
        in_specs.append(pl.BlockSpec((tm, tn), lambda i, j, k: (i, j)))
        args.append(add)
    return pcall(body, plan, grid=(M // tm, N // tn, nk), in_specs=in_specs,
                 out_specs=pl.BlockSpec((tm, tn), lambda i, j, k: (i, j)), out_shape=S((M, N), out_dtype),
                 scratch_shapes=[pltpu.VMEM((tm, tn), f32)] if nk > 1 else [],
                 sem=("parallel", "parallel", "arbitrary"), name=name, args=args)


def _rms(xv, gv):
    return xv * lax.rsqrt(jnp.mean(xv * xv, axis=-1, keepdims=True) + EPS) * gv


TR = 256


def rms_fwd(x, g, name):
    def body(x_ref, g_ref, o_ref):
        o_ref[...] = _rms(x_ref[...], g_ref[...]).astype(o_ref.dtype)

    return pl.pallas_call(
        body, grid=(L // TR,),
        in_specs=[pl.BlockSpec((TR, D), lambda i: (i, 0)), pl.BlockSpec((1, D), lambda i: (0, 0))],
        out_specs=pl.BlockSpec((TR, D), lambda i: (i, 0)), out_shape=S((L, D), BF),
        compiler_params=_cp(("parallel",)), name=name)(x, g)


def rms_bwd(x, g, dys, dres, name, plan=None):
    nd = len(dys)

    def body(*refs):
        x_ref, g_ref = refs[0], refs[1]
        dr_ref, dh_ref, dg_ref = refs[2 + nd:]
        dy = refs[2][...].astype(f32)
        for r in refs[3:2 + nd]:
            dy = dy + r[...].astype(f32)
        _, vjp = jax.vjp(_rms, x_ref[...], g_ref[...])
        dx, dg = vjp(dy)
        dh_ref[...] = dr_ref[...] + dx

        @pl.when(pl.program_id(0) == 0)
        def _():
            dg_ref[...] = jnp.zeros_like(dg_ref)

        dg_ref[...] += dg

    row = pl.BlockSpec((TR, D), lambda i: (i, 0))
    vec = pl.BlockSpec((1, D), lambda i: (0, 0))
    return pcall(body, plan, grid=(L // TR,), in_specs=[row, vec] + [row] * (nd + 1), out_specs=[row, vec],
                 out_shape=[S((L, D), f32), S((1, D), f32)], sem=("arbitrary",), name=name, args=[x, g, *dys, dres])


def loss_head(h, g, tgt):
    def f(hv, gv, tv):
        y = _rms(hv, gv)
        return 0.5 * jnp.sum(jnp.mean(jnp.square(y - tv), axis=-1))

    def body(h_ref, g_ref, t_ref, l_ref, dh_ref, dg_ref):
        val, vjp = jax.vjp(f, h_ref[...], g_ref[...], t_ref[...])
        dh, dg, _ = vjp(jnp.ones((), f32))
        dh_ref[...] = dh

        @pl.when(pl.program_id(0) == 0)
        def _():
            dg_ref[...] = jnp.zeros_like(dg_ref)
            l_ref[...] = jnp.zeros_like(l_ref)

        dg_ref[...] += dg
        l_ref[...] += jnp.full((1, 128), val, f32)

    row = pl.BlockSpec((TR, D), lambda i: (i, 0))
    vec = pl.BlockSpec((1, D), lambda i: (0, 0))
    return pl.pallas_call(
        body, grid=(L // TR,), in_specs=[row, vec, row],
        out_specs=[pl.BlockSpec((1, 128), lambda i: (0, 0)), row, vec],
        out_shape=[S((1, 128), f32), S((L, D), f32), S((1, D), f32)],
        compiler_params=_cp(("arbitrary",)), name="loss_head")(h, g, tgt)


def _col_to_row(c):
    n = c.shape[0]
    t = jnp.broadcast_to(c, (n, 128)).T
    r = lax.broadcasted_iota(jnp.int32, (128, n), 0)
    return jnp.sum(jnp.where(r == 0, t, 0.0), axis=0, keepdims=True)


def _s5_param_map(are, aim, ldt_row, bre, bim, cre, cim):
    n = NST
    gi = lax.broadcasted_iota(jnp.int32, (n, 32), 0) // 64
    gj = lax.broadcasted_iota(jnp.int32, (n, 32), 1)
    ldt = jnp.sum(jnp.where(gi == gj, ldt_row, 0.0), axis=1, keepdims=True)
    dt = jnp.exp(ldt)
    mag = jnp.exp(are * dt)
    abr = mag * jnp.cos(aim * dt)
    abi = mag * jnp.sin(aim * dt)
    den = are * are + aim * aim
    nr, ni = abr - 1.0, abi
    cr = (nr * are + ni * aim) / den
    ci = (ni * are - nr * aim) / den
    bbr = cr * bre - ci * bim
    bbi = cr * bim + ci * bre
    tc = lax.broadcasted_iota(jnp.int32, (16, 128), 0)
    tl = lax.broadcasted_iota(jnp.int32, (16, 128), 1)
    T = (tl % 16 == tc).astype(f32)
    mr = (lax.broadcasted_iota(jnp.int32, (n, 128), 0) // 64) % 8
    mc = lax.broadcasted_iota(jnp.int32, (n, 128), 1) // 16
    mask = (mr == mc).astype(f32)

    def expand(v):
        return jnp.dot(v, T, precision=HI, preferred_element_type=f32) * mask

    return expand(bbr), expand(bbi), expand(cre), expand(cim), _col_to_row(abr), _col_to_row(abi)


def s5_params_fwd(are, aim, ldt_row, bre, bim, cre, cim):
    def body(*refs):
        outs = _s5_param_map(*[r[...] for r in refs[:7]])
        for o_ref, o in zip(refs[7:], outs):
            o_ref[...] = o

    return pl.pallas_call(
        body, out_shape=[S((NST, 128), f32)] * 4 + [S((1, NST), f32)] * 2,
        compiler_params=_cp(), name="s5_params_fwd")(are, aim, ldt_row, bre, bim, cre, cim)


def s5_params_bwd(are, aim, ldt_row, bre, bim, cre, cim, cots):
    def body(*refs):
        _, vjp = jax.vjp(_s5_param_map, *[r[...] for r in refs[:7]])
        gs = vjp(tuple(r[...] for r in refs[7:13]))
        for o_ref, o in zip(refs[13:], gs):
            o_ref[...] = o

    return pl.pallas_call(
        body, out_shape=[S((NST, 1), f32)] * 2 + [S((1, 32), f32)] + [S((NST, 16), f32)] * 4,
        compiler_params=_cp(), name="s5_params_bwd")(are, aim, ldt_row, bre, bim, cre, cim, *cots)


def _cpowers(ar, ai):
    out = [(ar, ai)]
    for _ in range(7):
        pr, pi = out[-1]
        out.append((pr * ar - pi * ai, pr * ai + pi * ar))
    return out


def _ctable(pw, rid, power):
    tr_ = jnp.zeros(rid.shape, f32)
    ti_ = jnp.zeros(rid.shape, f32)
    for r in range(8):
        pr, pi = pw[power(r) - 1]
        tr_ = jnp.where(rid == r, pr, tr_)
        ti_ = jnp.where(rid == r, pi, ti_)
    return tr_, ti_


NT5 = 4
RC = 256


def s5_scan_fwd(proj, wbr, wbi, wcr, wci, abr, abi, drow, plan=None):
    def body(u_ref, wbr_ref, wbi_ref, wcr_ref, wci_ref, ar_ref, ai_ref, d_ref, xr_ref, xi_ref, y_ref):
        wbr_v, wbi_v = wbr_ref[...], wbi_ref[...]
        for r in range(L // RC):
            rows = pl.ds(r * RC, RC)
            ub = u_ref[rows, :]
            xr_ref[rows, :] = dot_nt(ub, wbr_v)
            xi_ref[rows, :] = dot_nt(ub, wbi_v)
        pw = _cpowers(ar_ref[...], ai_ref[...])
        rid = lax.broadcasted_iota(jnp.int32, (8, 512), 0)
        tr_, ti_ = _ctable(pw, rid, lambda r: r + 1)

        def group(j, c):
            cr, ci = c
            rows = pl.ds(pl.multiple_of(j * 8, 8), 8)
            br, bi = xr_ref[rows, :], xi_ref[rows, :]
            for s in (1, 2, 4):
                pr, pi = pw[s - 1]
                sr = jnp.where(rid >= s, pltpu.roll(br, s, 0), 0.0)
                si = jnp.where(rid >= s, pltpu.roll(bi, s, 0), 0.0)
                br, bi = br + pr * sr - pi * si, bi + pr * si + pi * sr
            br, bi = br + tr_ * cr - ti_ * ci, bi + tr_ * ci + ti_ * cr
            xr_ref[rows, :] = br
            xi_ref[rows, :] = bi
            return br[7:8], bi[7:8]

        z = jnp.zeros((1, 512), f32)
        lax.fori_loop(0, L // 8, group, (z, z), unroll=2)
        wcr_v, wci_v, dv = wcr_ref[...], wci_ref[...], d_ref[...]
        for r in range(L // RC):
            rows = pl.ds(r * RC, RC)
            y_ref[rows, :] = (dot_nn(xr_ref[rows, :], wcr_v) - dot_nn(xi_ref[rows, :], wci_v)
                              + dv * u_ref[rows, :])

    wspec = pl.BlockSpec((512, 128), lambda j: (j, 0))
    aspec = pl.BlockSpec((1, 512), lambda j: (0, j))
    return pcall(
        body, plan, grid=(NT5,),
        in_specs=[pl.BlockSpec((L, 128), lambda j: (0, j)), wspec, wspec, wspec, wspec, aspec, aspec,
                  pl.BlockSpec((1, 128), lambda j: (0, j))],
        out_specs=[pl.BlockSpec((L, 512), lambda j: (0, j)), pl.BlockSpec((L, 512), lambda j: (0, j)),
                   pl.BlockSpec((L, 128), lambda j: (0, j))],
        out_shape=[S((L, NST), f32), S((L, NST), f32), S((L, S5W), f32)],
        sem=("parallel",), name="s5_scan_fwd", args=[proj, wbr, wbi, wcr, wci, abr, abi, drow])


def s5_scan_bwd(dy, proj, xs_re, xs_im, wbr, wbi, wcr, wci, abr, abi, drow, plan=None):
    def body(dy_ref, u_ref, xr_ref, xi_ref, wbr_ref, wbi_ref, wcr_ref, wci_ref, ar_ref, ai_ref, d_ref,
             du_ref, gwbr_ref, gwbi_ref, gwcr_ref, gwci_ref, gar_ref, gai_ref, gd_ref, lr_ref, li_ref):
        wcr_v, wci_v = wcr_ref[...], wci_ref[...]
        gwcr = jnp.zeros((512, 128), f32)
        gwci = jnp.zeros((512, 128), f32)
        gd = jnp.zeros((1, 128), f32)
        for r in range(L // RC):
            rows = pl.ds(r * RC, RC)
            dyv = dy_ref[rows, :]
            lr_ref[rows, :] = dot_nt(dyv, wcr_v)
            li_ref[rows, :] = -dot_nt(dyv, wci_v)
            gwcr += dot_tn(xr_ref[rows, :], dyv)
            gwci -= dot_tn(xi_ref[rows, :], dyv)
            gd += jnp.sum(dyv * u_ref[rows, :], axis=0, keepdims=True)
        gwcr_ref[...] = gwcr
        gwci_ref[...] = gwci
        gd_ref[...] = gd
        pw = _cpowers(ar_ref[...], -ai_ref[...])
        rid = lax.broadcasted_iota(jnp.int32, (8, 512), 0)
        tr_, ti_ = _ctable(pw, rid, lambda r: 8 - r)

        def group(i, c):
            cr, ci, gar, gai = c
            j = L // 8 - 1 - i
            rows = pl.ds(pl.multiple_of(j * 8, 8), 8)
            br, bi = lr_ref[rows, :], li_ref[rows, :]
            for s in (1, 2, 4):
                pr, pi = pw[s - 1]
                sr = jnp.where(rid < 8 - s, pltpu.roll(br, 8 - s, 0), 0.0)
                si = jnp.where(rid < 8 - s, pltpu.roll(bi, 8 - s, 0), 0.0)
                br, bi = br + pr * sr - pi * si, bi + pr * si + pi * sr
            br, bi = br + tr_ * cr - ti_ * ci, bi + tr_ * ci + ti_ * cr
            lr_ref[rows, :] = br
            li_ref[rows, :] = bi
            nr = jnp.where(rid < 7, pltpu.roll(br, 7, 0), cr)
            ni = jnp.where(rid < 7, pltpu.roll(bi, 7, 0), ci)
            xr, xi = xr_ref[rows, :], xi_ref[rows, :]
            return br[0:1], bi[0:1], gar + xr * nr + xi * ni, gai + xr * ni - xi * nr

        z = jnp.zeros((1, 512), f32)
        z8 = jnp.zeros((8, 512), f32)
        _, _, gar, gai = lax.fori_loop(0, L // 8, group, (z, z, z8, z8), unroll=2)
        gar_ref[...] = jnp.sum(gar, axis=0, keepdims=True)
        gai_ref[...] = jnp.sum(gai, axis=0, keepdims=True)
        wbr_v, wbi_v, dv = wbr_ref[...], wbi_ref[...], d_ref[...]
        gwbr = jnp.zeros((512, 128), f32)
        gwbi = jnp.zeros((512, 128), f32)
        for r in range(L // RC):
            rows = pl.ds(r * RC, RC)
            lrv, liv, uv = lr_ref[rows, :], li_ref[rows, :], u_ref[rows, :]
            du_ref[rows, :] = (dot_nn(lrv, wbr_v) + dot_nn(liv, wbi_v) + dv * dy_ref[rows, :]).astype(du_ref.dtype)
            gwbr += dot_tn(lrv, uv)
            gwbi += dot_tn(liv, uv)
        gwbr_ref[...] = gwbr
        gwbi_ref[...] = gwbi

    wspec = pl.BlockSpec((512, 128), lambda j: (j, 0))
    aspec = pl.BlockSpec((1, 512), lambda j: (0, j))
    col = pl.BlockSpec((L, 128), lambda j: (0, j))
    st = pl.BlockSpec((L, 512), lambda j: (0, j))
    dspec = pl.BlockSpec((1, 128), lambda j: (0, j))
    return pcall(
        body, plan, grid=(NT5,),
        in_specs=[col, col, st, st, wspec, wspec, wspec, wspec, aspec, aspec, dspec],
        out_specs=[col, wspec, wspec, wspec, wspec, aspec, aspec, dspec],
        out_shape=[S((L, S5W), BF)] + [S((NST, 128), f32)] * 4 + [S((1, NST), f32)] * 2 + [S((1, S5W), f32)],
        scratch_shapes=[pltpu.VMEM((L, 512), f32), pltpu.VMEM((L, 512), f32)],
        sem=("parallel",), name="s5_scan_bwd", args=[dy, proj, xs_re, xs_im, wbr, wbi, wcr, wci, abr, abi, drow])


def _glu(y, w, b):
    z = jax.nn.gelu(y)
    return z * jax.nn.sigmoid(dot_nn(z, w) + b)


def s5_glu_fwd(y, w, b):
    def body(y_ref, w_ref, b_ref, o_ref):
        o_ref[...] = _glu(y_ref[...], w_ref[...], b_ref[...]).astype(o_ref.dtype)

    return pl.pallas_call(
        body, grid=(L // TR,),
        in_specs=[pl.BlockSpec((TR, S5W), lambda i: (i, 0)), pl.BlockSpec((S5W, S5W), lambda i: (0, 0)),
                  pl.BlockSpec((1, S5W), lambda i: (0, 0))],
        out_specs=pl.BlockSpec((TR, S5W), lambda i: (i, 0)), out_shape=S((L, S5W), BF),
        compiler_params=_cp(("parallel",)), name="s5_glu_fwd")(y, w, b)


def s5_glu_bwd(y, w, b, dmix):
    def body(y_ref, w_ref, b_ref, g_ref, dy_ref, dw_ref, db_ref):
        _, vjp = jax.vjp(_glu, y_ref[...], w_ref[...].astype(f32), b_ref[...])
        dy, dw, db = vjp(g_ref[...])
        dy_ref[...] = dy

        @pl.when(pl.program_id(0) == 0)
        def _():
            dw_ref[...] = jnp.zeros_like(dw_ref)
            db_ref[...] = jnp.zeros_like(db_ref)

        dw_ref[...] += dw
        db_ref[...] += db

    row = pl.BlockSpec((TR, S5W), lambda i: (i, 0))
    return pl.pallas_call(
        body, grid=(L // TR,),
        in_specs=[row, pl.BlockSpec((S5W, S5W), lambda i: (0, 0)), pl.BlockSpec((1, S5W), lambda i: (0, 0)), row],
        out_specs=[row, pl.BlockSpec((S5W, S5W), lambda i: (0, 0)), pl.BlockSpec((1, S5W), lambda i: (0, 0))],
        out_shape=[S((L, S5W), f32), S((S5W, S5W), f32), S((1, S5W), f32)],
        compiler_params=_cp(("arbitrary",)), name="s5_glu_bwd")(y, w, b, dmix)


def _hi(a, b, ca, cb):
    return lax.dot_general(a, b, (((ca,), (cb,)), ((), ())), precision=HI, preferred_element_type=f32)


def _hgrn_chunk(St, xq, xf, xi, xg, gam, ng):
    lb = jax.nn.sigmoid(gam[0:1] - gam[1:2])
    q = jax.nn.silu(xq)
    f = lb + (1.0 - lb) * jax.nn.sigmoid(xf)
    k = 1.0 - f
    g = jnp.log(f)
    ti = lax.broadcasted_iota(jnp.int32, (HGC, HGC), 0)
    si = lax.broadcasted_iota(jnp.int32, (HGC, HGC), 1)
    causal = si <= ti
    b = jnp.dot(causal.astype(f32), g, precision=HI, preferred_element_type=f32)
    qe = q * jnp.exp(b)
    o = _hi(qe, St, 1, 1)
    att = jnp.where(causal, _hi(qe, k * jnp.exp(-b), 1, 1), 0.0)
    o = o + _hi(att, xi, 1, 0)
    bl = b[HGC - 1:HGC]
    St_new = St * jnp.exp(bl) + _hi(xi, k * jnp.exp(bl - b), 0, 0)
    o = o * lax.rsqrt(jnp.mean(o * o, axis=-1, keepdims=True) + EPS) * ng
    return St_new, o * jax.nn.silu(xg)


NCH = L // HGC


def hgrn_fwd(proj, gamma, hnorm, plan=None):
    def body(q_ref, f_ref, i_ref, g_ref, gam_ref, ng_ref, o_ref, ss_ref, st):
        @pl.when(pl.program_id(0) == 0)
        def _():
            st[...] = jnp.zeros_like(st)

        for h in range(4):
            sl = slice(h * 128, (h + 1) * 128)
            s0 = st[h]
            ss_ref[0, h] = s0
            s1, o = _hgrn_chunk(s0, q_ref[:, sl], f_ref[:, sl], i_ref[:, sl], g_ref[:, sl], gam_ref[:, sl], ng_ref[:, sl])
            st[h] = s1
            o_ref[:, sl] = o.astype(o_ref.dtype)

    def pj(n):
        return pl.BlockSpec((HGC, 512), lambda c: (c, n))

    return pcall(
        body, plan, grid=(NCH,),
        in_specs=[pj(1), pj(2), pj(3), pj(4), pl.BlockSpec((2, 512), lambda c: (0, 0)), pl.BlockSpec((1, 512), lambda c: (0, 0))],
        out_specs=[pl.BlockSpec((HGC, 512), lambda c: (c, 0)), pl.BlockSpec((1, 4, 128, 128), lambda c: (c, 0, 0, 0))],
        out_shape=[S((L, 512), BF), S((NCH, 4, 128, 128), f32)],
        scratch_shapes=[pltpu.VMEM((4, 128, 128), f32)],
        sem=("arbitrary",), name="hgrn_fwd", args=[proj, proj, proj, proj, gamma, hnorm])


def hgrn_bwd(proj, gamma, hnorm, ssave, dmix, du, plan=None):
    def body(q_ref, f_ref, i_ref, g_ref, gam_ref, ng_ref, ss_ref, do_ref, du_ref, dp_ref, dgam_ref, dng_ref, dst):
        @pl.when(pl.program_id(0) == 0)
        def _():
            dst[...] = jnp.zeros_like(dst)
            dgam_ref[...] = jnp.zeros_like(dgam_ref)
            dng_ref[...] = jnp.zeros_like(dng_ref)

        dp_ref[:, 0:512] = du_ref[...]
        for h in range(4):
            sl = slice(h * 128, (h + 1) * 128)
            _, vjp = jax.vjp(_hgrn_chunk, ss_ref[0, h], q_ref[:, sl], f_ref[:, sl], i_ref[:, sl], g_ref[:, sl],
                             gam_ref[:, sl], ng_ref[:, sl])
            ds, dq, df, di, dg, dgam, dng = vjp((dst[h], do_ref[:, sl]))
            dst[h] = ds
            for n, v in enumerate((dq, df, di, dg)):
                dp_ref[:, 512 * (n + 1) + h * 128: 512 * (n + 1) + (h + 1) * 128] = v.astype(dp_ref.dtype)
            dgam_ref[:, sl] += dgam
            dng_ref[:, sl] += dng

    def pj(n):
        return pl.BlockSpec((HGC, 512), lambda i: (NCH - 1 - i, n))

    return pcall(
        body, plan, grid=(NCH,),
        in_specs=[pj(1), pj(2), pj(3), pj(4), pl.BlockSpec((2, 512), lambda i: (0, 0)), pl.BlockSpec((1, 512), lambda i: (0, 0)),
                  pl.BlockSpec((1, 4, 128, 128), lambda i: (NCH - 1 - i, 0, 0, 0)), pj(1), pj(0)],
        out_specs=[pl.BlockSpec((HGC, 2560), lambda i: (NCH - 1 - i, 0)), pl.BlockSpec((2, 512), lambda i: (0, 0)),
                   pl.BlockSpec((1, 512), lambda i: (0, 0))],
        out_shape=[S((L, 2560), BF), S((2, 512), f32), S((1, 512), f32)],
        scratch_shapes=[pltpu.VMEM((4, 128, 128), f32)],
        sem=("arbitrary",), name="hgrn_bwd", args=[proj, proj, proj, proj, gamma, hnorm, ssave, dmix, du])


def _shift(x, k):
    return jnp.concatenate([jnp.zeros((k, x.shape[1]), x.dtype), x[:-k]], axis=0)


def _convact(ha, hb, wa, wb, ba, bb):
    ca = wa[2:3] * ha + wa[1:2] * _shift(ha, 1) + wa[0:1] * _shift(ha, 2) + ba
    cb = wb[2:3] * hb + wb[1:2] * _shift(hb, 1) + wb[0:1] * _shift(hb, 2) + bb
    return jax.nn.silu(ca) * cb


CT = 128
NCT = DFF // CT


def convact_fwd(hu, cw, cb, layer, plan=None):
    def body(ha_ref, hb_ref, wa_ref, wb_ref, ba_ref, bb_ref, o_ref):
        o_ref[...] = _convact(ha_ref[...], hb_ref[...], wa_ref[...], wb_ref[...], ba_ref[...], bb_ref[...]).astype(o_ref.dtype)

    def h(off):
        return pl.BlockSpec((L, CT), lambda j: (0, j + off))

    def w(off):
        return pl.BlockSpec((3, CT), lambda j: (0, j + off))

    def b(off):
        return pl.BlockSpec((None, 1, CT), lambda j: (layer, 0, j + off))

    return pcall(body, plan, grid=(NCT,), in_specs=[h(0), h(NCT), w(0), w(NCT), b(0), b(NCT)],
                 out_specs=pl.BlockSpec((L, CT), lambda j: (0, j)), out_shape=S((L, DFF), BF),
                 sem=("parallel",), name=f"convact_fwd{layer}", args=[hu, hu, cw, cw, cb, cb])


def convact_bwd(hu, cw, cb, dact, layer, plan=None):
    def body(ha_ref, hb_ref, wa_ref, wb_ref, ba_ref, bb_ref, g_ref, dh_ref, dw_ref, db_ref, sh, sw, sb):
        j = pl.program_id(0)

        @pl.when(j < NCT)
        def _():
            _, vjp = jax.vjp(_convact, ha_ref[...], hb_ref[...], wa_ref[...], wb_ref[...], ba_ref[...], bb_ref[...])
            dha, dhb, dwa, dwb, dba, dbb = vjp(g_ref[...].astype(f32))
            dh_ref[...] = dha.astype(dh_ref.dtype)
            dw_ref[...] = dwa
            db_ref[...] = dba
            sh[j] = dhb.astype(sh.dtype)
            sw[j] = dwb
            sb[j] = dbb

        @pl.when(j >= NCT)
        def _():
            dh_ref[...] = sh[j - NCT]
            dw_ref[...] = sw[j - NCT]
            db_ref[...] = sb[j - NCT]

    def lo(j):
        return jnp.minimum(j, NCT - 1)

    in_specs = [pl.BlockSpec((L, CT), lambda j: (0, lo(j))), pl.BlockSpec((L, CT), lambda j: (0, lo(j) + NCT)),
                pl.BlockSpec((3, CT), lambda j: (0, lo(j))), pl.BlockSpec((3, CT), lambda j: (0, lo(j) + NCT)),
                pl.BlockSpec((None, 1, CT), lambda j: (layer, 0, lo(j))), pl.BlockSpec((None, 1, CT), lambda j: (layer, 0, lo(j) + NCT)),
                pl.BlockSpec((L, CT), lambda j: (0, lo(j)))]
    return pcall(
        body, plan, grid=(2 * NCT,), in_specs=in_specs,
        out_specs=[pl.BlockSpec((L, CT), lambda j: (0, j)), pl.BlockSpec((3, CT), lambda j: (0, j)), pl.BlockSpec((1, CT), lambda j: (0, j))],
        out_shape=[S((L, 2 * DFF), BF), S((3, 2 * DFF), f32), S((1, 2 * DFF), f32)],
        scratch_shapes=[pltpu.VMEM((NCT, L, CT), BF), pltpu.VMEM((NCT, 3, CT), f32), pltpu.VMEM((NCT, 1, CT), f32)],
        sem=("arbitrary",), name=f"convact_bwd{layer}", args=[hu, hu, cw, cw, cb, cb, dact])


DILS = (1, 4, 16)
AB = 128
NPAIR = 12


def _rope_tables(pos_ref, invf_ref):
    ang = pos_ref[...].astype(f32) * invf_ref[...]
    lane = lax.broadcasted_iota(jnp.int32, (1, 128), 1) % 64
    cosf = jnp.where(lane < 16, jnp.cos(ang), 1.0)
    sn = jnp.sin(ang)
    s_lo = jnp.where(lane < 8, -sn, 0.0)
    s_hi = jnp.where((lane >= 8) & (lane < 16), sn, 0.0)
    return cosf, s_lo, s_hi


def _rope(t, cosf, s_lo, s_hi):
    return t * cosf + pltpu.roll(t, 120, 1) * s_lo + pltpu.roll(t, 8, 1) * s_hi


def _rope_t(g, cosf, s_lo, s_hi):
    return g * cosf + pltpu.roll(g * s_lo, 8, 1) + pltpu.roll(g * s_hi, 120, 1)


def _att_block(q2, kp, kc, vp, vc, first):
    lane = lax.broadcasted_iota(jnp.int32, (1, 128), 1)
    qi = lax.broadcasted_iota(jnp.int32, (AB, 2 * AB), 0) + AB
    kj = lax.broadcasted_iota(jnp.int32, (AB, 2 * AB), 1)
    back = qi - kj
    valid = (back >= 0) & (back <= AB)
    if first:
        valid = valid & (kj >= AB)
    kk = jnp.concatenate([kp, kc], axis=0)
    vv = jnp.concatenate([vp, vc], axis=0)
    o2 = jnp.zeros((AB, 128), f32)
    lse2 = jnp.zeros((AB, 128), f32)
    for e in range(2):
        hm = ((lane >= 64 * e) & (lane < 64 * (e + 1))).astype(f32)
        s = dot_nt(q2 * (hm * 0.125), kk)
        s = jnp.where(valid, s, -jnp.inf)
        m = jnp.max(s, axis=-1, keepdims=True)
        p = jnp.exp(s - m)
        den = jnp.sum(p, axis=-1, keepdims=True)
        o2 = o2 + dot_nn(p, vv * hm) / den
        lse2 = lse2 + (m + jnp.log(den)) * hm
    return o2, lse2


def _att_blocks(dil):
    m = L // dil
    return [(r * m + n * AB, n == 0) for r in range(dil) for n in range(m // AB)]


def deinterleave(x, dil):
    return x if dil == 1 else x.reshape(L // dil, dil, x.shape[1]).swapaxes(0, 1).reshape(L, x.shape[1])


def interleave(x, dil):
    return x if dil == 1 else x.reshape(dil, L // dil, x.shape[1]).swapaxes(0, 1).reshape(L, x.shape[1])


def attn_fwd(qkv, pos, invf, g, plan=None):
    blocks = _att_blocks(DILS[g])

    def body(q_ref, k_ref, v_ref, pos_ref, invf_ref, o_ref, l_ref, qr, kr):
        cosf, s_lo, s_hi = _rope_tables(pos_ref, invf_ref)
        qr[...] = _rope(q_ref[...], cosf, s_lo, s_hi)
        kr[...] = _rope(k_ref[...], cosf, s_lo, s_hi)
        for off, first in blocks:
            cur, prv = pl.ds(off, AB), pl.ds(off if first else off - AB, AB)
            o2, lse2 = _att_block(qr[cur, :], kr[prv, :], kr[cur, :], v_ref[prv, :], v_ref[cur, :], first)
            o_ref[cur, :] = o2
            l_ref[cur, :] = lse2

    def sec(n):
        return pl.BlockSpec((L, 128), lambda p: (0, p + 4 * n))

    return pcall(
        body, plan, grid=(4,),
        in_specs=[sec(0), sec(1), sec(2), pl.BlockSpec((L, 1), lambda p: (0, 0)), pl.BlockSpec((1, 128), lambda p: (0, 0))],
        out_specs=[sec(0), sec(0)], out_shape=[S((L, 512), f32), S((L, 512), f32)],
        scratch_shapes=[pltpu.VMEM((L, 128), f32), pltpu.VMEM((L, 128), f32)],
        sem=("parallel",), name=f"attn_fwd{g}", args=[qkv, qkv, qkv, pos, invf])


def attn_bwd(qkv, pos, invf, do, dl, g, plan=None):
    blocks = _att_blocks(DILS[g])

    def body(q_ref, k_ref, v_ref, pos_ref, invf_ref, do_ref, dl_ref, d_ref, qr, kr, dqr, dkr, dvr):
        cosf, s_lo, s_hi = _rope_tables(pos_ref, invf_ref)
        qr[...] = _rope(q_ref[...], cosf, s_lo, s_hi)
        kr[...] = _rope(k_ref[...], cosf, s_lo, s_hi)
        for off, first in blocks:
            cur, prv = pl.ds(off, AB), pl.ds(off if first else off - AB, AB)
            fn = functools.partial(_att_block, first=first)
            _, vjp = jax.vjp(fn, qr[cur, :], kr[prv, :], kr[cur, :], v_ref[prv, :], v_ref[cur, :])
            dq2, dkp, dkc, dvp, dvc = vjp((do_ref[cur, :], dl_ref[cur, :]))
            dqr[cur, :] = dq2
            dkr[cur, :] = dkc
            dvr[cur, :] = dvc
            if not first:
                dkr[prv, :] += dkp
                dvr[prv, :] += dvp
        d_ref[0] = _rope_t(dqr[...], cosf, s_lo, s_hi).astype(d_ref.dtype)
        d_ref[1] = _rope_t(dkr[...], cosf, s_lo, s_hi).astype(d_ref.dtype)
        d_ref[2] = dvr[...].astype(d_ref.dtype)

    def sec(n):
        return pl.BlockSpec((L, 128), lambda p: (0, p + 4 * n))

    return pcall(
        body, plan, grid=(4,),
        in_specs=[sec(0), sec(1), sec(2), pl.BlockSpec((L, 1), lambda p: (0, 0)), pl.BlockSpec((1, 128), lambda p: (0, 0)),
                  sec(0), sec(0)],
        out_specs=pl.BlockSpec((3, L, 128), lambda p: (0, 0, p)), out_shape=S((3, L, 512), BF),
        scratch_shapes=[pltpu.VMEM((L, 128), f32)] * 5,
        sem=("parallel",), name=f"attn_bwd{g}", args=[qkv, qkv, qkv, pos, invf, do, dl])


def _merge(o0, o1, o2, l0, l1, l2):
    m = jnp.maximum(jnp.maximum(l0, l1), l2)
    e0, e1, e2 = jnp.exp(l0 - m), jnp.exp(l1 - m), jnp.exp(l2 - m)
    return (e0 * o0 + e1 * o1 + e2 * o2) / (e0 + e1 + e2)


def attn_merge_fwd(os_, ls_):
    def body(o0, o1, o2, l0, l1, l2, o_ref):
        o_ref[...] = _merge(o0[...], o1[...], o2[...], l0[...], l1[...], l2[...]).astype(o_ref.dtype)

    blk = pl.BlockSpec((TR, 512), lambda i: (i, 0))
    return pl.pallas_call(
        body, grid=(L // TR,), in_specs=[blk] * 6, out_specs=blk, out_shape=S((L, 512), BF),
        compiler_params=_cp(("parallel",)), name="attn_merge_fwd")(*os_, *ls_)


def attn_merge_bwd(os_, ls_, do, plan=None):
    def body(o0, o1, o2, l0, l1, l2, g_ref, *outs):
        _, vjp = jax.vjp(_merge, o0[...], o1[...], o2[...], l0[...], l1[...], l2[...])
        for o_ref, v in zip(outs, vjp(g_ref[...])):
            o_ref[...] = v

    blk = pl.BlockSpec((TR, 512), lambda i: (i, 0))
    outs = pcall(body, plan, grid=(L // TR,), in_specs=[blk] * 7, out_specs=[blk] * 6, out_shape=[S((L, 512), f32)] * 6,
                 sem=("parallel",), name="attn_merge_bwd", args=[*os_, *ls_, do])
    return outs[:3], outs[3:]


def _invf_lanes():
    half = 8
    inv = ROPE_THETA ** (-np.arange(half, dtype=np.float32) * 2.0 / 16.0)
    lane = np.arange(128) % 64
    return jnp.asarray(np.where(lane < 16, inv[lane % 8], 0.0).astype(np.float32)[None, :])


def hosted(C, host, fn):
    p = C.plan(host) if C is not None else None
    out = fn(p)
    if p is not None:
        C.done(p)
    return out


def _ffn_fwd(h, g_row, W, cb, layer, C):
    hn = rms_fwd(h, g_row, f"rms_ffn{layer}")
    hu = hosted(C, f"ffn_in{layer}", lambda p: matmul(hn, W[("ffn_w_in", layer)], mode="nn", tm=1024, tn=1408, tk=1024,
                                                      plan=p, name=f"ffn_in{layer}"))
    act = hosted(C, f"convact_fwd{layer}", lambda p: convact_fwd(hu, W[("ffn_conv_w", layer)], cb, layer, plan=p))
    h2 = matmul(act, W[("ffn_w_out", layer)], mode="nn", tm=1024, tn=1024, tk=1408, add=h, name=f"ffn_out{layer}")
    return h2, (hn, hu, act)


def _ffn_bwd(dh, h, g_row, W, cb, saved, layer, C, G):
    hn, hu, act = saved
    w_in, w_out = W[("ffn_w_in", layer)], W[("ffn_w_out", layer)]
    dact = matmul(dh, w_out, mode="nt", tm=1024, tn=1408, tk=1024, name=f"ffn_out_dx{layer}")
    G[("ffn_w_out", layer)] = matmul(act, dh, mode="tn", tm=1408, tn=1024, tk=512, out_dtype=BF, name=f"ffn_out_dw{layer}")
    dhu, G[("ffn_conv_w", layer)], g_cb = hosted(
        C, f"convact_bwd{layer}", lambda p: convact_bwd(hu, W[("ffn_conv_w", layer)], cb, dact, layer, plan=p))
    dhn = hosted(C, f"ffn_in_dx{layer}", lambda p: matmul(dhu, w_in, mode="nt", tm=1024, tn=1024, tk=1408, plan=p,
                                                         name=f"ffn_in_dx{layer}"))
    G[("ffn_w_in", layer)] = hosted(C, f"ffn_in_dw{layer}", lambda p: matmul(
        hn, dhu, mode="tn", tm=1024, tn=1408, tk=512, out_dtype=BF, plan=p, name=f"ffn_in_dw{layer}"))
    dh2, g_norm = rms_bwd(h, g_row, [dhn], dh, f"rms_ffn_bwd{layer}")
    return dh2, g_cb, g_norm


def local_step(x, pos, tgt, sm, W, C=None):
    G = C.grads if C is not None else {}
    nm, nf = sm["norm_mix"], sm["norm_ffn"]
    invf = _invf_lanes()
    are = sm["s5_A_re"].reshape(NST, 1)
    aim = sm["s5_A_im"].reshape(NST, 1)
    ldt = sm["s5_log_dt"].reshape(1, 32)
    bre = sm["s5_B_re"].reshape(NST, 16)
    bim = sm["s5_B_im"].reshape(NST, 16)
    cre = jnp.swapaxes(sm["s5_C_re"][0], 1, 2).reshape(NST, 16)
    cim = jnp.swapaxes(sm["s5_C_im"][0], 1, 2).reshape(NST, 16)
    drow = sm["s5_D"].reshape(1, S5W)
    wbr, wbi, wcr, wci, abr, abi = s5_params_fwd(are, aim, ldt, bre, bim, cre, cim)
    hn0 = rms_fwd(x, nm[0:1], "rms_mix0")
    cb3 = sm["ffn_conv_b3"]
    proj = hosted(C, "mix_in", lambda p: matmul(hn0, W[("mix_w_in", 0)], mode="nn", tm=1024, tn=1280, tk=1024, plan=p, name="mix_in"))
    xs_re, xs_im, y5 = hosted(C, "s5_scan_fwd", lambda p: s5_scan_fwd(proj, wbr, wbi, wcr, wci, abr, abi, drow, plan=p))
    oa = s5_glu_fwd(y5, W[("s5_glu_w", 0)], sm["s5_glu_b"])
    ob, ssave = hosted(C, "hgrn_fwd", lambda p: hgrn_fwd(proj, sm["hgrn_gamma"], sm["hgrn_norm"], plan=p))
    cat = jnp.concatenate([oa, ob], axis=1)
    h1 = matmul(cat, W[("mix_w_out", 0)], mode="nn", tm=1024, tn=1024, tk=1024, add=x, name="mix_out")
    h2, ffn0 = _ffn_fwd(h1, nf[0:1], W, cb3, 0, C)
    hn2 = rms_fwd(h2, nm[1:2], "rms_mix1")
    wqkv = W[("att_w_qkv", 0)]
    hn2_g, pos_g, qkv_g, o_g, l_g = [], [], [], [], []
    for g, dil in enumerate(DILS):
        hn2_g.append(deinterleave(hn2, dil))
        pos_g.append(deinterleave(pos, dil))
        qkv_g.append(hosted(C, f"att_qkv{g}", lambda p: matmul(
            hn2_g[g], wqkv, mode="nn", tm=1024, tn=512, tk=1024, dims=(L, 1536, D),
            b_spec=pl.BlockSpec((D, 512), lambda i, j, k, g=g: (0, 3 * j + g)), plan=p, name=f"att_qkv{g}")))
        o_c, l_c = hosted(C, f"attn_fwd{g}", lambda p: attn_fwd(qkv_g[g], pos_g[g], invf, g, plan=p))
        o_g.append(interleave(o_c, dil))
        l_g.append(interleave(l_c, dil))
    o = attn_merge_fwd(o_g, l_g)
    h3 = matmul(o, W[("att_w_o", 0)], mode="nn", tm=1024, tn=1024, tk=512, add=h2, name="att_o")
    h4, ffn1 = _ffn_fwd(h3, nf[1:2], W, cb3, 1, C)
    loss, dh, g_nfinal = loss_head(h4, sm["norm_final"].reshape(1, D), tgt)
    dh, g_cb1, g_nf1 = _ffn_bwd(dh, h3, nf[1:2], W, cb3, ffn1, 1, C, G)
    do = matmul(dh, W[("att_w_o", 0)], mode="nt", tm=1024, tn=512, tk=1024, name="att_o_dx")
    G[("att_w_o", 0)] = matmul(o, dh, mode="tn", tm=512, tn=1024, tk=512, out_dtype=BF, name="att_o_dw")
    do_g, dl_g = hosted(C, "attn_merge_bwd", lambda p: attn_merge_bwd(o_g, l_g, do, plan=p))
    dhn2_g, gq = [], []
    for g, dil in enumerate(DILS):
        d3 = hosted(C, f"attn_bwd{g}", lambda p: attn_bwd(qkv_g[g], pos_g[g], invf, deinterleave(do_g[g], dil),
                                                        deinterleave(dl_g[g], dil), g, plan=p))
        dx = matmul(d3, wqkv, mode="nt", tm=1024, tn=1024, tk=512, dims=(L, D, 1536),
                    a_spec=pl.BlockSpec((None, 1024, 512), lambda i, j, k: (k, i, 0)),
                    b_spec=pl.BlockSpec((D, 512), lambda i, j, k, g=g: (0, 3 * k + g)), name=f"att_qkv_dx{g}")
        dhn2_g.append(interleave(dx, dil))
        gq.append(matmul(hn2_g[g], d3, mode="tn", tm=1024, tn=512, tk=512, out_dtype=BF, dims=(D, 1536, L),
                         b_spec=pl.BlockSpec((None, 512, 512), lambda i, j, k: (j, k, 0)), name=f"att_qkv_dw{g}"))
    G[("att_w_qkv", 0)] = jnp.concatenate([gq[g][:, 512 * s:512 * (s + 1)] for s in range(3) for g in range(3)], axis=1)
    dh, g_nm1 = rms_bwd(h2, nm[1:2], dhn2_g, dh, "rms_mix_bwd1")
    dh, g_cb0, g_nf0 = _ffn_bwd(dh, h1, nf[0:1], W, cb3, ffn0, 0, C, G)
    dmix = matmul(dh, W[("mix_w_out", 0)], mode="nt", tm=1024, tn=1024, tk=1024, name="mix_out_dx")
    G[("mix_w_out", 0)] = matmul(cat, dh, mode="tn", tm=1024, tn=1024, tk=512, out_dtype=BF, name="mix_out_dw")
    dy5, g_glu_w, g_glu_b = s5_glu_bwd(y5, W[("s5_glu_w", 0)], sm["s5_glu_b"], dmix)
    G[("s5_glu_w", 0)] = g_glu_w.astype(BF)
    du, gwbr, gwbi, gwcr, gwci, gabr, gabi, g_d = hosted(C, "s5_scan_bwd", lambda p: s5_scan_bwd(
        dy5, proj, xs_re, xs_im, wbr, wbi, wcr, wci, abr, abi, drow, plan=p))
    g_are, g_aim, g_ldt, g_bre, g_bim, g_cre, g_cim = s5_params_bwd(are, aim, ldt, bre, bim, cre, cim,
                                                                   (gwbr, gwbi, gwcr, gwci, gabr, gabi))
    dproj, g_gamma, g_hnorm = hosted(C, "hgrn_bwd", lambda p: hgrn_bwd(proj, sm["hgrn_gamma"], sm["hgrn_norm"], ssave, dmix, du,
                                                                       plan=p))
    dhn0 = hosted(C, "mix_in_dx", lambda p: matmul(dproj, W[("mix_w_in", 0)], mode="nt", tm=1024, tn=1024, tk=1280, plan=p,
                                                  name="mix_in_dx"))
    G[("mix_w_in", 0)] = matmul(hn0, dproj, mode="tn", tm=1024, tn=1280, tk=512, out_dtype=BF, name="mix_in_dw")
    gx, g_nm0 = hosted(C, "rms_mix_bwd0", lambda p: rms_bwd(x, nm[0:1], [dhn0], dh, "rms_mix_bwd0", plan=p))
    small = {
        "norm_mix": jnp.concatenate([g_nm0, g_nm1], axis=0), "norm_ffn": jnp.concatenate([g_nf0, g_nf1], axis=0),
        "norm_final": g_nfinal.reshape(D),
        "s5_A_re": g_are.reshape(1, 32, 64), "s5_A_im": g_aim.reshape(1, 32, 64), "s5_log_dt": g_ldt.reshape(1, 32),
        "s5_B_re": g_bre.reshape(1, 32, 64, 16), "s5_B_im": g_bim.reshape(1, 32, 64, 16),
        "s5_C_re": jnp.swapaxes(g_cre.reshape(1, 32, 64, 16), 2, 3), "s5_C_im": jnp.swapaxes(g_cim.reshape(1, 32, 64, 16), 2, 3),
        "s5_D": g_d.reshape(1, 32, 16), "s5_glu_b": g_glu_b, "hgrn_gamma": g_gamma, "hgrn_norm": g_hnorm,
        "ffn_conv_b": jnp.concatenate([g_cb0, g_cb1], axis=0),
    }
    return loss, gx, G, small


BIG = ("mix_w_in", "mix_w_out", "s5_glu_w", "att_w_qkv", "att_w_o", "ffn_w_in", "ffn_w_out", "ffn_conv_w")
SMALL = ("norm_mix", "norm_ffn", "norm_final", "s5_A_re", "s5_A_im", "s5_log_dt", "s5_B_re", "s5_B_im", "s5_C_re", "s5_C_im",
         "s5_D", "s5_glu_b", "hgrn_gamma", "hgrn_norm", "ffn_conv_b")


def cast_bf16(w, name, plan=None):
    nl, r, c = w.shape
    w2 = w.reshape(nl * r, c)
    tr = 256 if (nl * r) % 256 == 0 else nl * r

    def body(w_ref, o_ref):
        o_ref[...] = w_ref[...].astype(BF)

    out = pcall(body, plan, grid=(nl * r // tr,), in_specs=[pl.BlockSpec((tr, c), lambda i: (i, 0))],
                out_specs=pl.BlockSpec((tr, c), lambda i: (i, 0)), out_shape=S((nl * r, c), BF),
                sem=("parallel",), name=name, args=[w2])
    return out.reshape(nl, r, c)


SCHEDULE = {
    "cast_ffn_w_in": [("G", "mix_w_in", 0)],
    "mix_in": [("G", "mix_w_out", 0), ("G", "s5_glu_w", 0)],
    "s5_scan_fwd": [("G", "ffn_w_in", 0, (0, 2))],
    "hgrn_fwd": [("G", "ffn_w_in", 0, (1, 2)), ("G", "ffn_conv_w", 0), ("G", "ffn_conv_w", 1), ("G", "att_w_qkv", 0, (0, 2))],
    "ffn_in0": [("G", "ffn_w_out", 0)],
    "convact_fwd0": [("G", "att_w_qkv", 0, (1, 2))],
    "att_qkv0": [("G", "att_w_o", 0)],
    "attn_fwd0": [("G", "ffn_w_in", 1, (0, 2))],
    "attn_fwd1": [("G", "ffn_w_in", 1, (1, 2))],
    "attn_fwd2": [("G", "ffn_w_out", 1)],
    "convact_bwd1": [("A", "ffn_w_out", 1)],
    "ffn_in_dx1": [("B", "ffn_w_out", 1)],
    "attn_merge_bwd": [("A", "att_w_o", 0), ("A", "ffn_conv_w", 1)],
    "attn_bwd0": [("A", "ffn_w_in", 1, (0, 2))],
    "attn_bwd1": [("A", "ffn_w_in", 1, (1, 2)), ("B", "att_w_o", 0), ("B", "ffn_conv_w", 1)],
    "attn_bwd2": [("B", "ffn_w_in", 1)],
    "convact_bwd0": [("A", "att_w_qkv", 0, (0, 2))],
    "ffn_in_dw0": [("A", "att_w_qkv", 0, (1, 2))],
    "s5_scan_bwd": [("A", "ffn_w_out", 0), ("A", "mix_w_out", 0), ("A", "s5_glu_w", 0), ("A", "ffn_conv_w", 0),
                    ("B", "att_w_qkv", 0)],
    "hgrn_bwd": [("A", "ffn_w_in", 0), ("B", "ffn_w_out", 0), ("B", "mix_w_out", 0), ("B", "s5_glu_w", 0), ("B", "ffn_conv_w", 0)],
    "mix_in_dx": [("B", "ffn_w_in", 0)],
    "adam_ffn_w_in": [("A", "mix_w_in", 0, (0, 2)), ("A", "small", 0)],
    "adam_ffn_w_out": [("A", "mix_w_in", 0, (1, 2))],
    "adam_att_w_qkv": [("B", "mix_w_in", 0), ("B", "small", 0)],
}


class Comm:
    def __init__(self, shards, shapes):
        self.shards, self.shapes = shards, shapes
        self.W, self.grads, self.slots = {}, {}, {}
        self.small = None

    def plan(self, host):
        items = SCHEDULE.get(host)
        if not items:
            return None
        p = Plan()
        for it in items:
            kind, name, l = it[:3]
            part, parts = it[3] if len(it) > 3 else (0, 1)
            if name == "small":
                kdst = p.buf("slots:small", arr=self.slots.get("small"), shape=S((8,) + self.small.shape, f32), write=True)
                if kind == "A":
                    ReduceOp(p, p.buf("g:small", arr=self.small), kdst, None, self.small.shape, False, 0, 0, whole=True)
                else:
                    ForwardOp(p, kdst, None, whole=True)
                continue
            nl, R, C_ = self.shapes[name]
            rows = name in ROW_SHARDED
            r0, nr = part * (R // parts), R // parts
            if kind == "G":
                sh = self.shards[name]
                kdst = p.buf(f"W:{name}:{l}", arr=self.W.get((name, l)), shape=S((4 * R, C_) if rows else (R, 4 * C_), sh.dtype),
                             write=True)
                GatherOp(p, p.buf("shard:" + name, arr=sh), kdst, l, self.shapes[name], rows, r0, nr, split=(nr % 32 == 0))
            else:
                g = self.grads[(name, l)]
                kdst = p.buf("slots:" + name, arr=self.slots.get(name), shape=S((8, nl, R, C_), g.dtype), write=True)
                if kind == "A":
                    ReduceOp(p, p.buf(f"g:{name}:{l}", arr=g), kdst, l, self.shapes[name], rows, r0, nr)
                else:
                    ForwardOp(p, kdst, l)
        return p

    def done(self, p):
        for k, arr in p.out.items():
            tag, name = k.split(":")[:2]
            if tag == "W":
                self.W[(name, int(k.split(":")[2]))] = arr
            else:
                self.slots[name] = arr


def _adamw(w, g, m, v):
    m = B1 * m + (1.0 - B1) * g
    v = B2 * v + (1.0 - B2) * jnp.square(g)
    m_hat = m / (1.0 - B1 ** STEP)
    v_hat = v / (1.0 - B2 ** STEP)
    return -LR * (m_hat / (jnp.sqrt(v_hat) + AEPS) + WD * w), m, v


def adam_big(w, m, v, slots, name, plan=None):
    nl, R, C = w.shape
    tr = 128 if R % 128 == 0 else (64 if R % 64 == 0 else R)

    def body(w_ref, m_ref, v_ref, s_ref, g_ref, d_ref, nm_ref, nv_ref):
        g = s_ref[0].astype(f32)
        for s in range(1, 8):
            g = g + s_ref[s].astype(f32)
        d, nm_, nv_ = _adamw(w_ref[...], g, m_ref[...], v_ref[...])
        g_ref[...] = g
        d_ref[...] = d
        nm_ref[...] = nm_
        nv_ref[...] = nv_

    blk = pl.BlockSpec((None, tr, C), lambda l, i: (l, i, 0))
    return pcall(body, plan, grid=(nl, R // tr),
                 in_specs=[blk, blk, blk, pl.BlockSpec((8, None, tr, C), lambda l, i: (0, l, i, 0))],
                 out_specs=[blk] * 4, out_shape=[S((nl, R, C), f32)] * 4,
                 sem=("parallel", "parallel"), name=name, args=[w, m, v, slots])


def adam_small(w, m, v, slots):
    R = w.shape[0]
    tr = 256

    def body(w_ref, m_ref, v_ref, s_ref, g_ref, d_ref, nm_ref, nv_ref):
        g = s_ref[0]
        for s in range(1, 8):
            g = g + s_ref[s]
        d, nm_, nv_ = _adamw(w_ref[...], g, m_ref[...], v_ref[...])
        g_ref[...] = g
        d_ref[...] = d
        nm_ref[...] = nm_
        nv_ref[...] = nv_

    blk = pl.BlockSpec((tr, 128), lambda i: (i, 0))
    return pl.pallas_call(
        body, grid=(R // tr,), in_specs=[blk, blk, blk, pl.BlockSpec((8, tr, 128), lambda i: (0, i, 0))],
        out_specs=[blk] * 4, out_shape=[S((R, 128), f32)] * 4,
        compiler_params=_cp(("parallel",)), name="adam_small")(w, m, v, slots)


def _pack(d):
    flat = jnp.concatenate([d[n].reshape(-1) for n in SMALL])
    n = flat.shape[0]
    rows = -(-n // (256 * 128)) * 256
    return jnp.pad(flat, (0, rows * 128 - n)).reshape(rows, 128)


def _unpack(p, like):
    flat = p.reshape(-1)
    out, off = {}, 0
    for n in SMALL:
        sz = math.prod(like[n].shape)
        out[n] = flat[off:off + sz].reshape(like[n].shape)
        off += sz
    return out


def kernel(x, positions, norm_mix, norm_ffn, norm_final, mix_w_in, mix_w_out, s5_A_re, s5_A_im, s5_log_dt, s5_B_re, s5_B_im, s5_C_re, s5_C_im, s5_D, s5_glu_w, s5_glu_b, hgrn_gamma, hgrn_norm, att_w_qkv, att_w_o, ffn_w_in, ffn_conv_w, ffn_conv_b, ffn_w_out, loss_target, m_norm_mix, m_norm_ffn, m_norm_final, m_mix_w_in, m_mix_w_out, m_s5_A_re, m_s5_A_im, m_s5_log_dt, m_s5_B_re, m_s5_B_im, m_s5_C_re, m_s5_C_im, m_s5_D, m_s5_glu_w, m_s5_glu_b, m_hgrn_gamma, m_hgrn_norm, m_att_w_qkv, m_att_w_o, m_ffn_w_in, m_ffn_conv_w, m_ffn_conv_b, m_ffn_w_out, v_norm_mix, v_norm_ffn, v_norm_final, v_mix_w_in, v_mix_w_out, v_s5_A_re, v_s5_A_im, v_s5_log_dt, v_s5_B_re, v_s5_B_im, v_s5_C_re, v_s5_C_im, v_s5_D, v_s5_glu_w, v_s5_glu_b, v_hgrn_gamma, v_hgrn_norm, v_att_w_qkv, v_att_w_o, v_ffn_w_in, v_ffn_conv_w, v_ffn_conv_b, v_ffn_w_out):
    a = dict(locals())
    weights = BIG + SMALL
    w = {n: a[n] for n in weights}
    m = {n: a["m_" + n] for n in weights}
    v = {n: a["v_" + n] for n in weights}
    shards = {"ffn_conv_w": ffn_conv_w}
    C = Comm(shards, {n: w[n].shape for n in BIG})
    for n in ("mix_w_in", "ffn_w_in", "mix_w_out", "s5_glu_w", "ffn_w_out", "att_w_qkv", "att_w_o"):
        shards[n] = hosted(C, "cast_" + n, lambda p: cast_bf16(w[n], "cast_" + n, plan=p))
    sm = {n: w[n] for n in SMALL}
    sm["ffn_conv_b3"] = ffn_conv_b.reshape(2, 1, 2 * DFF)
    loss, gx, _, gsmall = local_step(x[0], positions.reshape(L, 1), loss_target[0], sm, C.W, C)
    C.small = _pack(gsmall)
    res = {}
    for n in ("ffn_w_in", "ffn_w_out", "att_w_qkv", "att_w_o", "mix_w_out", "s5_glu_w", "ffn_conv_w", "mix_w_in"):
        res[n] = hosted(C, "adam_" + n, lambda p: adam_big(w[n], m[n], v[n], C.slots[n], "adam_" + n, plan=p))
    packed = adam_small(_pack({n: w[n] for n in SMALL}), _pack({n: m[n] for n in SMALL}), _pack({n: v[n] for n in SMALL}),
                        C.slots["small"])
    small_out = [_unpack(p, {n: w[n] for n in SMALL}) for p in packed]
    for n in SMALL:
        res[n] = tuple(so[n] for so in small_out)
    total = lax.psum(loss[0, 0], ("x", "y", "c"))
    order = ("norm_mix", "norm_ffn", "norm_final", "mix_w_in", "mix_w_out", "s5_A_re", "s5_A_im", "s5_log_dt", "s5_B_re", "s5_B_im",
             "s5_C_re", "s5_C_im", "s5_D", "s5_glu_w", "s5_glu_b", "hgrn_gamma", "hgrn_norm", "att_w_qkv", "att_w_o", "ffn_w_in",
             "ffn_conv_w", "ffn_conv_b", "ffn_w_out")
    return (total, gx[None], *[res[n][0] for n in order], *[res[n][1] for n in order], *[res[n][2] for n in order],
            *[res[n][3] for n in order])
```

```python
import functools
import math

import numpy as np
import jax
import jax.numpy as jnp
from jax import lax
from jax.experimental import pallas as pl
from jax.experimental.pallas import tpu as pltpu

f32 = jnp.float32
BF = jnp.bfloat16
HI = lax.Precision.HIGHEST
S = jax.ShapeDtypeStruct
MESH = pl.DeviceIdType.MESH

L = 2048
D = 1024
EPS = 1e-6
S5W = 512
NST = 2048
HGC = 64
DFF = 2816
ROPE_THETA = 500000.0
LR, B1, B2, AEPS, WD, STEP = 0.001, 0.9, 0.999, 1e-08, 0.01, 10
VMEM_LIMIT = 56 * 1024 * 1024


def _cp(sem=None):
    return pltpu.CompilerParams(dimension_semantics=sem, vmem_limit_bytes=VMEM_LIMIT)


ANY = pl.BlockSpec(memory_space=pl.ANY)
ROW_SHARDED = ("mix_w_out", "s5_glu_w", "ffn_w_out")


def _coords():
    x, y, c = lax.axis_index("x"), lax.axis_index("y"), lax.axis_index("c")
    return x, y, c, 2 * x + y, [(1 - x, y), (x, 1 - y), (1 - x, 1 - y)]


def _rows(start, n):
    return pl.ds(start if isinstance(start, int) else pl.multiple_of(start, 8), n)


def _cols(q, n):
    return pl.ds(pl.multiple_of(q * n, 128), n)


class Plan:
    def __init__(self):
        self.bufs, self.ops, self.nsem, self.out = {}, [], 0, {}

    def buf(self, key, arr=None, shape=None, write=False):
        b = self.bufs.setdefault(key, dict(arr=arr, shape=shape, write=False))
        b["write"] = b["write"] or write
        return key

    def add(self, op):
        op.base = self.nsem
        self.nsem += op.nsem
        self.ops.append(op)


class GatherOp:
    nsem = 13

    def __init__(self, plan, ksrc, kdst, l, shard_shape, rows, r0, nr, split):
        self.ksrc, self.kdst, self.l, (_, self.R, self.C), self.rows, self.r0, self.nr, self.split = (
            ksrc, kdst, l, shard_shape, rows, r0, nr, split)
        self.h = nr // 2 if split else nr
        plan.add(self)

    def _dst(self, R_, q, start, n):
        if self.rows:
            return R_[self.kdst].at[_rows(q * self.R + start, n), :]
        return R_[self.kdst].at[_rows(start, n), _cols(q, self.C)]

    def _mine(self, c):
        return self.r0 + (c * self.h if self.split else 0)

    def _theirs(self, c):
        return self.r0 + ((1 - c) * self.h if self.split else 0)

    def _copies(self, R_, sems):
        x, y, c, me, others = _coords()
        src = R_[self.ksrc]
        local = pltpu.make_async_copy(src.at[self.l, _rows(self.r0, self.nr), :], self._dst(R_, me, self.r0, self.nr),
                                      sems.at[self.base + 12])
        send, fwd = [], []
        for k, (px, py) in enumerate(others):
            q = 2 * px + py
            send.append((
                pltpu.make_async_remote_copy(src.at[self.l, _rows(self._mine(c), self.h), :], self._dst(R_, me, self._mine(c), self.h),
                                             sems.at[self.base + k], sems.at[self.base + 3 + k], device_id=(px, py, c), device_id_type=MESH),
                pltpu.make_async_remote_copy(src.at[self.l, _rows(self._mine(c), self.h), :], self._dst(R_, q, self._mine(c), self.h),
                                             sems.at[self.base + k], sems.at[self.base + 3 + k], device_id=(px, py, c), device_id_type=MESH)))
            fwd.append((
                pltpu.make_async_remote_copy(self._dst(R_, q, self._mine(c), self.h), self._dst(R_, q, self._mine(c), self.h),
                                             sems.at[self.base + 6 + k], sems.at[self.base + 9 + k], device_id=(x, y, 1 - c), device_id_type=MESH),
                pltpu.make_async_remote_copy(self._dst(R_, q, self._theirs(c), self.h), self._dst(R_, q, self._theirs(c), self.h),
                                             sems.at[self.base + 6 + k], sems.at[self.base + 9 + k], device_id=(x, y, 1 - c), device_id_type=MESH)))
        return local, send, fwd

    def start(self, R_, sems):
        local, send, _ = self._copies(R_, sems)
        local.start()
        for out, _ in send:
            out.start()

    def finish(self, R_, sems):
        local, send, fwd = self._copies(R_, sems)
        for k in range(3):
            send[k][1].wait_recv()
            if self.split:
                fwd[k][0].start()
        for k in range(3):
            if self.split:
                fwd[k][1].wait_recv()
                fwd[k][0].wait_send()
            send[k][0].wait_send()
        local.wait()


class ReduceOp:
    nsem = 7

    def __init__(self, plan, ksrc, kdst, l, shard_shape, rows, r0, nr, whole=False):
        self.ksrc, self.kdst, self.l, (self.R, self.C), self.rows, self.r0, self.nr, self.whole = (
            ksrc, kdst, l, shard_shape[-2:], rows, r0, nr, whole)
        plan.add(self)

    def _piece(self, R_, q):
        g = R_[self.ksrc]
        if self.whole:
            return g
        if self.rows:
            return g.at[_rows(q * self.R + self.r0, self.nr), :]
        return g.at[_rows(self.r0, self.nr), _cols(q, self.C)]

    def _slot(self, R_, s):
        if self.whole:
            return R_[self.kdst].at[s]
        return R_[self.kdst].at[s, self.l, _rows(self.r0, self.nr), :]

    def _copies(self, R_, sems):
        x, y, c, me, others = _coords()
        local = pltpu.make_async_copy(self._piece(R_, me), self._slot(R_, 2 * me + c), sems.at[self.base + 6])
        send = []
        for k, (px, py) in enumerate(others):
            q = 2 * px + py
            send.append((
                pltpu.make_async_remote_copy(self._piece(R_, q), self._slot(R_, 2 * me + c), sems.at[self.base + k],
                                             sems.at[self.base + 3 + k], device_id=(px, py, c), device_id_type=MESH),
                pltpu.make_async_remote_copy(self._piece(R_, q), self._slot(R_, 2 * q + c), sems.at[self.base + k],
                                             sems.at[self.base + 3 + k], device_id=(px, py, c), device_id_type=MESH)))
        return local, send

    def start(self, R_, sems):
        local, send = self._copies(R_, sems)
        local.start()
        for out, _ in send:
            out.start()

    def finish(self, R_, sems):
        local, send = self._copies(R_, sems)
        local.wait()
        for out, inn in send:
            inn.wait_recv()
            out.wait_send()


class ForwardOp:
    nsem = 8

    def __init__(self, plan, kdst, l, whole=False):
        self.kdst, self.l, self.whole = kdst, l, whole
        plan.add(self)

    def _slot(self, R_, s):
        return R_[self.kdst].at[s] if self.whole else R_[self.kdst].at[s, self.l]

    def _copies(self, R_, sems):
        x, y, c, me, others = _coords()
        return [(pltpu.make_async_remote_copy(self._slot(R_, 2 * q + c), self._slot(R_, 2 * q + c), sems.at[self.base + q],
                                              sems.at[self.base + 4 + q], device_id=(x, y, 1 - c), device_id_type=MESH),
                 pltpu.make_async_remote_copy(self._slot(R_, 2 * q + 1 - c), self._slot(R_, 2 * q + 1 - c), sems.at[self.base + q],
                                              sems.at[self.base + 4 + q], device_id=(x, y, 1 - c), device_id_type=MESH))
                for q in range(4)]

    def start(self, R_, sems):
        for out, _ in self._copies(R_, sems):
            out.start()

    def finish(self, R_, sems):
        for out, inn in self._copies(R_, sems):
            inn.wait_recv()
            out.wait_send()


def pcall(body, plan, *, grid, in_specs, out_specs, out_shape, scratch_shapes=(), sem, name, args):
    multi = isinstance(out_shape, (list, tuple))
    if plan is None or not plan.ops:
        return pl.pallas_call(body, grid=grid, in_specs=in_specs, out_specs=out_specs, out_shape=out_shape,
                              scratch_shapes=list(scratch_shapes), compiler_params=_cp(sem), name=name)(*args)
    outs = list(out_shape) if multi else [out_shape]
    ospecs = list(out_specs) if multi else [out_specs]
    kin = [k for k, b in plan.bufs.items() if b["arr"] is not None]
    kout = [k for k, b in plan.bufs.items() if b["write"]]
    n_in, n_out, n_scr = len(in_specs), len(outs), len(scratch_shapes)

    def wrapped(*refs):
        o0 = n_in + len(kin)
        s0 = o0 + n_out + len(kout)
        R_ = dict(zip(kin, refs[n_in:o0]))
        R_.update(zip(kout, refs[o0 + n_out:s0]))
        sems = refs[s0 + n_scr]
        first = functools.reduce(jnp.logical_and, [pl.program_id(d) == 0 for d in range(len(grid))])
        last = functools.reduce(jnp.logical_and, [pl.program_id(d) == grid[d] - 1 for d in range(len(grid))])

        @pl.when(first)
        def _():
            for op in plan.ops:
                op.start(R_, sems)

        body(*refs[:n_in], *refs[o0:o0 + n_out], *refs[s0:s0 + n_scr])

        @pl.when(last)
        def _():
            for op in plan.ops:
                op.finish(R_, sems)

    def shape_of(k):
        b = plan.bufs[k]
        return S(b["arr"].shape, b["arr"].dtype) if b["arr"] is not None else b["shape"]

    res = pl.pallas_call(
        wrapped, grid=grid, in_specs=list(in_specs) + [ANY] * len(kin), out_specs=ospecs + [ANY] * len(kout),
        out_shape=outs + [shape_of(k) for k in kout],
        scratch_shapes=list(scratch_shapes) + [pltpu.SemaphoreType.DMA((plan.nsem,))],
        input_output_aliases={n_in + kin.index(k): n_out + kout.index(k) for k in kout if plan.bufs[k]["arr"] is not None},
        compiler_params=pltpu.CompilerParams(dimension_semantics=("arbitrary",) * len(grid), vmem_limit_bytes=VMEM_LIMIT,
                                             has_side_effects=True),
        name=name)(*args, *[plan.bufs[k]["arr"] for k in kin])
    plan.out = dict(zip(kout, res[n_out:]))
    return list(res[:n_out]) if multi else res[0]


def _dg(a, b, ca, cb):
    return lax.dot_general(a.astype(BF), b.astype(BF), (((ca,), (cb,)), ((), ())), preferred_element_type=f32)


@jax.custom_vjp
def dot_nn(a, b):
    return _dg(a, b, 1, 0)


@jax.custom_vjp
def dot_nt(a, b):
    return _dg(a, b, 1, 1)


@jax.custom_vjp
def dot_tn(a, b):
    return _dg(a, b, 0, 0)


dot_nn.defvjp(lambda a, b: (dot_nn(a, b), (a, b)),
              lambda r, g: (dot_nt(g, r[1]).astype(r[0].dtype), dot_tn(r[0], g).astype(r[1].dtype)))
dot_nt.defvjp(lambda a, b: (dot_nt(a, b), (a, b)),
              lambda r, g: (dot_nn(g, r[1]).astype(r[0].dtype), dot_tn(g, r[0]).astype(r[1].dtype)))
dot_tn.defvjp(lambda a, b: (dot_tn(a, b), (a, b)),
              lambda r, g: (dot_nt(r[1], g).astype(r[0].dtype), dot_nn(r[0], g).astype(r[1].dtype)))


def matmul(a, b, *, mode, tm, tn, tk, out_dtype=f32, add=None, b_lead=None, a_spec=None, b_spec=None, dims=None, plan=None, name):
    a_over, b_over = a_spec, b_spec
    if mode == "nn":
        (M, K), N = a.shape[-2:], b.shape[-1]
        a_spec = pl.BlockSpec((tm, tk), lambda i, j, k: (i, k))
        b_blk, b_idx, ca, cb = (tk, tn), (lambda i, j, k: (k, j)), 1, 0
    elif mode == "nt":
        (M, K), N = a.shape[-2:], b.shape[-2]
        a_spec = pl.BlockSpec((tm, tk), lambda i, j, k: (i, k))
        b_blk, b_idx, ca, cb = (tn, tk), (lambda i, j, k: (j, k)), 1, 1
    else:
        (K, M), N = a.shape[-2:], b.shape[-1]
        a_spec = pl.BlockSpec((tk, tm), lambda i, j, k: (k, i))
        b_blk, b_idx, ca, cb = (tk, tn), (lambda i, j, k: (k, j)), 0, 0
    if dims is not None:
        M, N, K = dims
    assert M % tm == 0 and N % tn == 0 and K % tk == 0, (name, M, N, K, tm, tn, tk)
    if b_lead is None:
        b_spec = pl.BlockSpec(b_blk, b_idx)
    else:
        b_spec = pl.BlockSpec((None,) + b_blk, lambda i, j, k: (b_lead,) + b_idx(i, j, k))
    if a_over is not None:
        a_spec = a_over
    if b_over is not None:
        b_spec = b_over
    nk = K // tk
    has_add = add is not None

    def body(*refs):
        a_ref, b_ref = refs[0], refs[1]
        add_ref = refs[2] if has_add else None
        o_ref = refs[2 + has_add]
        p = _dg(a_ref[...], b_ref[...], ca, cb)

        def fin(v):
            if has_add:
                v = v + add_ref[...].astype(f32)
            o_ref[...] = v.astype(o_ref.dtype)

        if nk == 1:
            fin(p)
        else:
            acc = refs[3 + has_add]
            k = pl.program_id(2)

            @pl.when(k == 0)
            def _():
                acc[...] = p

            @pl.when(k > 0)
            def _():
                acc[...] += p

            @pl.when(k == nk - 1)
            def _():
                fin(acc[...])

    in_specs = [a_spec, b_spec]
    args = [a, b]
    if has_add:
        in_specs.append(pl.BlockSpec((tm, tn), lambda i, j, k: (i, j)))
        args.append(add)
    return pcall(body, plan, grid=(M // tm, N // tn, nk), in_specs=in_specs,
                 out_specs=pl.BlockSpec((tm, tn), lambda i, j, k: (i, j)), out_shape=S((M, N), out_dtype),
                 scratch_shapes=[pltpu.VMEM((tm, tn), f32)] if nk > 1 else [],
                 sem=("parallel", "parallel", "arbitrary"), name=name, args=args)


def _rms(xv, gv):
    return xv * lax.rsqrt(jnp.mean(xv * xv, axis=-1, keepdims=True) + EPS) * gv


TR = 256


def rms_fwd(x, g, name):
    def body(x_ref, g_ref, o_ref):
        o_ref[...] = _rms(x_ref[...], g_ref[...]).astype(o_ref.dtype)

    return pl.pallas_call(
        body, grid=(L // TR,),
        in_specs=[pl.BlockSpec((TR, D), lambda i: (i, 0)), pl.BlockSpec((1, D), lambda i: (0, 0))],
        out_specs=pl.BlockSpec((TR, D), lambda i: (i, 0)), out_shape=S((L, D), BF),
        compiler_params=_cp(("parallel",)), name=name)(x, g)


def rms_bwd(x, g, dys, dres, name, plan=None):
    nd = len(dys)

    def body(*refs):
        x_ref, g_ref = refs[0], refs[1]
        dr_ref, dh_ref, dg_ref = refs[2 + nd:]
        dy = refs[2][...].astype(f32)
        for r in refs[3:2 + nd]:
            dy = dy + r[...].astype(f32)
        _, vjp = jax.vjp(_rms, x_ref[...], g_ref[...])
        dx, dg = vjp(dy)
        dh_ref[...] = dr_ref[...] + dx

        @pl.when(pl.program_id(0) == 0)
        def _():
            dg_ref[...] = jnp.zeros_like(dg_ref)

        dg_ref[...] += dg

    row = pl.BlockSpec((TR, D), lambda i: (i, 0))
    vec = pl.BlockSpec((1, D), lambda i: (0, 0))
    return pcall(body, plan, grid=(L // TR,), in_specs=[row, vec] + [row] * (nd + 1), out_specs=[row, vec],
                 out_shape=[S((L, D), f32), S((1, D), f32)], sem=("arbitrary",), name=name, args=[x, g, *dys, dres])


def loss_head(h, g, tgt):
    def f(hv, gv, tv):
        y = _rms(hv, gv)
        return 0.5 * jnp.sum(jnp.mean(jnp.square(y - tv), axis=-1))

    def body(h_ref, g_ref, t_ref, l_ref, dh_ref, dg_ref):
        val, vjp = jax.vjp(f, h_ref[...], g_ref[...], t_ref[...])
        dh, dg, _ = vjp(jnp.ones((), f32))
        dh_ref[...] = dh

        @pl.when(pl.program_id(0) == 0)
        def _():
            dg_ref[...] = jnp.zeros_like(dg_ref)
            l_ref[...] = jnp.zeros_like(l_ref)

        dg_ref[...] += dg
        l_ref[...] += jnp.full((1, 128), val, f32)

    row = pl.BlockSpec((TR, D), lambda i: (i, 0))
    vec = pl.BlockSpec((1, D), lambda i: (0, 0))
    return pl.pallas_call(
        body, grid=(L // TR,), in_specs=[row, vec, row],
        out_specs=[pl.BlockSpec((1, 128), lambda i: (0, 0)), row, vec],
        out_shape=[S((1, 128), f32), S((L, D), f32), S((1, D), f32)],
        compiler_params=_cp(("arbitrary",)), name="loss_head")(h, g, tgt)


def _col_to_row(c):
    n = c.shape[0]
    t = jnp.broadcast_to(c, (n, 128)).T
    r = lax.broadcasted_iota(jnp.int32, (128, n), 0)
    return jnp.sum(jnp.where(r == 0, t, 0.0), axis=0, keepdims=True)


def _s5_param_map(are, aim, ldt_row, bre, bim, cre, cim):
    n = NST
    gi = lax.broadcasted_iota(jnp.int32, (n, 32), 0) // 64
    gj = lax.broadcasted_iota(jnp.int32, (n, 32), 1)
    ldt = jnp.sum(jnp.where(gi == gj, ldt_row, 0.0), axis=1, keepdims=True)
    dt = jnp.exp(ldt)
    mag = jnp.exp(are * dt)
    abr = mag * jnp.cos(aim * dt)
    abi = mag * jnp.sin(aim * dt)
    den = are * are + aim * aim
    nr, ni = abr - 1.0, abi
    cr = (nr * are + ni * aim) / den
    ci = (ni * are - nr * aim) / den
    bbr = cr * bre - ci * bim
    bbi = cr * bim + ci * bre
    tc = lax.broadcasted_iota(jnp.int32, (16, 128), 0)
    tl = lax.broadcasted_iota(jnp.int32, (16, 128), 1)
    T = (tl % 16 == tc).astype(f32)
    mr = (lax.broadcasted_iota(jnp.int32, (n, 128), 0) // 64) % 8
    mc = lax.broadcasted_iota(jnp.int32, (n, 128), 1) // 16
    mask = (mr == mc).astype(f32)

    def expand(v):
        return jnp.dot(v, T, precision=HI, preferred_element_type=f32) * mask

    return expand(bbr), expand(bbi), expand(cre), expand(cim), _col_to_row(abr), _col_to_row(abi)


def s5_params_fwd(are, aim, ldt_row, bre, bim, cre, cim):
    def body(*refs):
        outs = _s5_param_map(*[r[...] for r in refs[:7]])
        for o_ref, o in zip(refs[7:], outs):
            o_ref[...] = o

    return pl.pallas_call(
        body, out_shape=[S((NST, 128), f32)] * 4 + [S((1, NST), f32)] * 2,
        compiler_params=_cp(), name="s5_params_fwd")(are, aim, ldt_row, bre, bim, cre, cim)


def s5_params_bwd(are, aim, ldt_row, bre, bim, cre, cim, cots):
    def body(*refs):
        _, vjp = jax.vjp(_s5_param_map, *[r[...] for r in refs[:7]])
        gs = vjp(tuple(r[...] for r in refs[7:13]))
        for o_ref, o in zip(refs[13:], gs):
            o_ref[...] = o

    return pl.pallas_call(
        body, out_shape=[S((NST, 1), f32)] * 2 + [S((1, 32), f32)] + [S((NST, 16), f32)] * 4,
        compiler_params=_cp(), name="s5_params_bwd")(are, aim, ldt_row, bre, bim, cre, cim, *cots)


def _cpowers(ar, ai):
    out = [(ar, ai)]
    for _ in range(7):
        pr, pi = out[-1]
        out.append((pr * ar - pi * ai, pr * ai + pi * ar))
    return out


def _ctable(pw, rid, power):
    tr_ = jnp.zeros(rid.shape, f32)
    ti_ = jnp.zeros(rid.shape, f32)
    for r in range(8):
        pr, pi = pw[power(r) - 1]
        tr_ = jnp.where(rid == r, pr, tr_)
        ti_ = jnp.where(rid == r, pi, ti_)
    return tr_, ti_


NT5 = 4
RC = 256


def s5_scan_fwd(proj, wbr, wbi, wcr, wci, abr, abi, drow, plan=None):
    def body(u_ref, wbr_ref, wbi_ref, wcr_ref, wci_ref, ar_ref, ai_ref, d_ref, xr_ref, xi_ref, y_ref):
        wbr_v, wbi_v = wbr_ref[...], wbi_ref[...]
        for r in range(L // RC):
            rows = pl.ds(r * RC, RC)
            ub = u_ref[rows, :]
            xr_ref[rows, :] = dot_nt(ub, wbr_v)
            xi_ref[rows, :] = dot_nt(ub, wbi_v)
        pw = _cpowers(ar_ref[...], ai_ref[...])
        rid = lax.broadcasted_iota(jnp.int32, (8, 512), 0)
        tr_, ti_ = _ctable(pw, rid, lambda r: r + 1)

        def group(j, c):
            cr, ci = c
            rows = pl.ds(pl.multiple_of(j * 8, 8), 8)
            br, bi = xr_ref[rows, :], xi_ref[rows, :]
            for s in (1, 2, 4):
                pr, pi = pw[s - 1]
                sr = jnp.where(rid >= s, pltpu.roll(br, s, 0), 0.0)
                si = jnp.where(rid >= s, pltpu.roll(bi, s, 0), 0.0)
                br, bi = br + pr * sr - pi * si, bi + pr * si + pi * sr
            br, bi = br + tr_ * cr - ti_ * ci, bi + tr_ * ci + ti_ * cr
            xr_ref[rows, :] = br
            xi_ref[rows, :] = bi
            return br[7:8], bi[7:8]

        z = jnp.zeros((1, 512), f32)
        lax.fori_loop(0, L // 8, group, (z, z), unroll=2)
        wcr_v, wci_v, dv = wcr_ref[...], wci_ref[...], d_ref[...]
        for r in range(L // RC):
            rows = pl.ds(r * RC, RC)
            y_ref[rows, :] = (dot_nn(xr_ref[rows, :], wcr_v) - dot_nn(xi_ref[rows, :], wci_v)
                              + dv * u_ref[rows, :])

    wspec = pl.BlockSpec((512, 128), lambda j: (j, 0))
    aspec = pl.BlockSpec((1, 512), lambda j: (0, j))
    return pcall(
        body, plan, grid=(NT5,),
        in_specs=[pl.BlockSpec((L, 128), lambda j: (0, j)), wspec, wspec, wspec, wspec, aspec, aspec,
                  pl.BlockSpec((1, 128), lambda j: (0, j))],
        out_specs=[pl.BlockSpec((L, 512), lambda j: (0, j)), pl.BlockSpec((L, 512), lambda j: (0, j)),
                   pl.BlockSpec((L, 128), lambda j: (0, j))],
        out_shape=[S((L, NST), f32), S((L, NST), f32), S((L, S5W), f32)],
        sem=("parallel",), name="s5_scan_fwd", args=[proj, wbr, wbi, wcr, wci, abr, abi, drow])


def s5_scan_bwd(dy, proj, xs_re, xs_im, wbr, wbi, wcr, wci, abr, abi, drow, plan=None):
    def body(dy_ref, u_ref, xr_ref, xi_ref, wbr_ref, wbi_ref, wcr_ref, wci_ref, ar_ref, ai_ref, d_ref,
             du_ref, gwbr_ref, gwbi_ref, gwcr_ref, gwci_ref, gar_ref, gai_ref, gd_ref, lr_ref, li_ref):
        wcr_v, wci_v = wcr_ref[...], wci_ref[...]
        gwcr = jnp.zeros((512, 128), f32)
        gwci = jnp.zeros((512, 128), f32)
        gd = jnp.zeros((1, 128), f32)
        for r in range(L // RC):
            rows = pl.ds(r * RC, RC)
            dyv = dy_ref[rows, :]
            lr_ref[rows, :] = dot_nt(dyv, wcr_v)
            li_ref[rows, :] = -dot_nt(dyv, wci_v)
            gwcr += dot_tn(xr_ref[rows, :], dyv)
            gwci -= dot_tn(xi_ref[rows, :], dyv)
            gd += jnp.sum(dyv * u_ref[rows, :], axis=0, keepdims=True)
        gwcr_ref[...] = gwcr
        gwci_ref[...] = gwci
        gd_ref[...] = gd
        pw = _cpowers(ar_ref[...], -ai_ref[...])
        rid = lax.broadcasted_iota(jnp.int32, (8, 512), 0)
        tr_, ti_ = _ctable(pw, rid, lambda r: 8 - r)

        def group(i, c):
            cr, ci, gar, gai = c
            j = L // 8 - 1 - i
            rows = pl.ds(pl.multiple_of(j * 8, 8), 8)
            br, bi = lr_ref[rows, :], li_ref[rows, :]
            for s in (1, 2, 4):
                pr, pi = pw[s - 1]
                sr = jnp.where(rid < 8 - s, pltpu.roll(br, 8 - s, 0), 0.0)
                si = jnp.where(rid < 8 - s, pltpu.roll(bi, 8 - s, 0), 0.0)
                br, bi = br + pr * sr - pi * si, bi + pr * si + pi * sr
            br, bi = br + tr_ * cr - ti_ * ci, bi + tr_ * ci + ti_ * cr
            lr_ref[rows, :] = br
            li_ref[rows, :] = bi
            nr = jnp.where(rid < 7, pltpu.roll(br, 7, 0), cr)
            ni = jnp.where(rid < 7, pltpu.roll(bi, 7, 0), ci)
            xr, xi = xr_ref[rows, :], xi_ref[rows, :]
            return br[0:1], bi[0:1], gar + xr * nr + xi * ni, gai + xr * ni - xi * nr

        z = jnp.zeros((1, 512), f32)
        z8 = jnp.zeros((8, 512), f32)
        _, _, gar, gai = lax.fori_loop(0, L // 8, group, (z, z, z8, z8), unroll=2)
        gar_ref[...] = jnp.sum(gar, axis=0, keepdims=True)
        gai_ref[...] = jnp.sum(gai, axis=0, keepdims=True)
        wbr_v, wbi_v, dv = wbr_ref[...], wbi_ref[...], d_ref[...]
        gwbr = jnp.zeros((512, 128), f32)
        gwbi = jnp.zeros((512, 128), f32)
        for r in range(L // RC):
            rows = pl.ds(r * RC, RC)
            lrv, liv, uv = lr_ref[rows, :], li_ref[rows, :], u_ref[rows, :]
            du_ref[rows, :] = (dot_nn(lrv, wbr_v) + dot_nn(liv, wbi_v) + dv * dy_ref[rows, :]).astype(du_ref.dtype)
            gwbr += dot_tn(lrv, uv)
            gwbi += dot_tn(liv, uv)
        gwbr_ref[...] = gwbr
        gwbi_ref[...] = gwbi

    wspec = pl.BlockSpec((512, 128), lambda j: (j, 0))
    aspec = pl.BlockSpec((1, 512), lambda j: (0, j))
    col = pl.BlockSpec((L, 128), lambda j: (0, j))
    st = pl.BlockSpec((L, 512), lambda j: (0, j))
    dspec = pl.BlockSpec((1, 128), lambda j: (0, j))
    return pcall(
        body, plan, grid=(NT5,),
        in_specs=[col, col, st, st, wspec, wspec, wspec, wspec, aspec, aspec, dspec],
        out_specs=[col, wspec, wspec, wspec, wspec, aspec, aspec, dspec],
        out_shape=[S((L, S5W), BF)] + [S((NST, 128), f32)] * 4 + [S((1, NST), f32)] * 2 + [S((1, S5W), f32)],
        scratch_shapes=[pltpu.VMEM((L, 512), f32), pltpu.VMEM((L, 512), f32)],
        sem=("parallel",), name="s5_scan_bwd", args=[dy, proj, xs_re, xs_im, wbr, wbi, wcr, wci, abr, abi, drow])


def _glu(y, w, b):
    z = jax.nn.gelu(y)
    return z * jax.nn.sigmoid(dot_nn(z, w) + b)


def s5_glu_fwd(y, w, b):
    def body(y_ref, w_ref, b_ref, o_ref):
        o_ref[...] = _glu(y_ref[...], w_ref[...], b_ref[...]).astype(o_ref.dtype)

    return pl.pallas_call(
        body, grid=(L // TR,),
        in_specs=[pl.BlockSpec((TR, S5W), lambda i: (i, 0)), pl.BlockSpec((S5W, S5W), lambda i: (0, 0)),
                  pl.BlockSpec((1, S5W), lambda i: (0, 0))],
        out_specs=pl.BlockSpec((TR, S5W), lambda i: (i, 0)), out_shape=S((L, S5W), BF),
        compiler_params=_cp(("parallel",)), name="s5_glu_fwd")(y, w, b)


def s5_glu_bwd(y, w, b, dmix):
    def body(y_ref, w_ref, b_ref, g_ref, dy_ref, dw_ref, db_ref):
        _, vjp = jax.vjp(_glu, y_ref[...], w_ref[...].astype(f32), b_ref[...])
        dy, dw, db = vjp(g_ref[...])
        dy_ref[...] = dy

        @pl.when(pl.program_id(0) == 0)
        def _():
            dw_ref[...] = jnp.zeros_like(dw_ref)
            db_ref[...] = jnp.zeros_like(db_ref)

        dw_ref[...] += dw
        db_ref[...] += db

    row = pl.BlockSpec((TR, S5W), lambda i: (i, 0))
    return pl.pallas_call(
        body, grid=(L // TR,),
        in_specs=[row, pl.BlockSpec((S5W, S5W), lambda i: (0, 0)), pl.BlockSpec((1, S5W), lambda i: (0, 0)), row],
        out_specs=[row, pl.BlockSpec((S5W, S5W), lambda i: (0, 0)), pl.BlockSpec((1, S5W), lambda i: (0, 0))],
        out_shape=[S((L, S5W), f32), S((S5W, S5W), f32), S((1, S5W), f32)],
        compiler_params=_cp(("arbitrary",)), name="s5_glu_bwd")(y, w, b, dmix)


def _dg3(a, b, ca, cb):
    ah, bh = a.astype(BF), b.astype(BF)
    al, bl = (a - ah.astype(f32)).astype(BF), (b - bh.astype(f32)).astype(BF)
    return _dg(ah, bh, ca, cb) + _dg(ah, bl, ca, cb) + _dg(al, bh, ca, cb)


@jax.custom_vjp
def hi_nn(a, b):
    return _dg3(a, b, 1, 0)


@jax.custom_vjp
def hi_nt(a, b):
    return _dg3(a, b, 1, 1)


@jax.custom_vjp
def hi_tn(a, b):
    return _dg3(a, b, 0, 0)


hi_nn.defvjp(lambda a, b: (hi_nn(a, b), (a, b)), lambda r, g: (hi_nt(g, r[1]), hi_tn(r[0], g)))
hi_nt.defvjp(lambda a, b: (hi_nt(a, b), (a, b)), lambda r, g: (hi_nn(g, r[1]), hi_tn(g, r[0])))
hi_tn.defvjp(lambda a, b: (hi_tn(a, b), (a, b)), lambda r, g: (hi_nt(r[1], g), hi_nn(r[0], g)))


def _hgrn_chunk(St, xq, xf, xi, xg, gam, ng):
    lb = jax.nn.sigmoid(gam[0:1] - gam[1:2])
    q = jax.nn.silu(xq)
    f = lb + (1.0 - lb) * jax.nn.sigmoid(xf)
    k = 1.0 - f
    g = jnp.log(f)
    ti = lax.broadcasted_iota(jnp.int32, (HGC, HGC), 0)
    si = lax.broadcasted_iota(jnp.int32, (HGC, HGC), 1)
    causal = si <= ti
    b = jnp.dot(causal.astype(f32), g, precision=HI, preferred_element_type=f32)
    qe = q * jnp.exp(b)
    o = hi_nt(qe, St)
    att = jnp.where(causal, hi_nt(qe, k * jnp.exp(-b)), 0.0)
    o = o + hi_nn(att, xi)
    bl = b[HGC - 1:HGC]
    St_new = St * jnp.exp(bl) + hi_tn(xi, k * jnp.exp(bl - b))
    o = o * lax.rsqrt(jnp.mean(o * o, axis=-1, keepdims=True) + EPS) * ng
    return St_new, o * jax.nn.silu(xg)


NCH = L // HGC


def hgrn_fwd(proj, gamma, hnorm, plan=None):
    def body(q_ref, f_ref, i_ref, g_ref, gam_ref, ng_ref, o_ref, ss_ref, st):
        @pl.when(pl.program_id(0) == 0)
        def _():
            st[...] = jnp.zeros_like(st)

        for h in range(4):
            sl = slice(h * 128, (h + 1) * 128)
            s0 = st[h]
            ss_ref[0, h] = s0
            s1, o = _hgrn_chunk(s0, q_ref[:, sl], f_ref[:, sl], i_ref[:, sl], g_ref[:, sl], gam_ref[:, sl], ng_ref[:, sl])
            st[h] = s1
            o_ref[:, sl] = o.astype(o_ref.dtype)

    def pj(n):
        return pl.BlockSpec((HGC, 512), lambda c: (c, n))

    return pcall(
        body, plan, grid=(NCH,),
        in_specs=[pj(1), pj(2), pj(3), pj(4), pl.BlockSpec((2, 512), lambda c: (0, 0)), pl.BlockSpec((1, 512), lambda c: (0, 0))],
        out_specs=[pl.BlockSpec((HGC, 512), lambda c: (c, 0)), pl.BlockSpec((1, 4, 128, 128), lambda c: (c, 0, 0, 0))],
        out_shape=[S((L, 512), BF), S((NCH, 4, 128, 128), f32)],
        scratch_shapes=[pltpu.VMEM((4, 128, 128), f32)],
        sem=("arbitrary",), name="hgrn_fwd", args=[proj, proj, proj, proj, gamma, hnorm])


def hgrn_bwd(proj, gamma, hnorm, ssave, dmix, du, plan=None):
    def body(q_ref, f_ref, i_ref, g_ref, gam_ref, ng_ref, ss_ref, do_ref, du_ref, dp_ref, dgam_ref, dng_ref, dst):
        @pl.when(pl.program_id(0) == 0)
        def _():
            dst[...] = jnp.zeros_like(dst)
            dgam_ref[...] = jnp.zeros_like(dgam_ref)
            dng_ref[...] = jnp.zeros_like(dng_ref)

        dp_ref[:, 0:512] = du_ref[...]
        for h in range(4):
            sl = slice(h * 128, (h + 1) * 128)
            _, vjp = jax.vjp(_hgrn_chunk, ss_ref[0, h], q_ref[:, sl], f_ref[:, sl], i_ref[:, sl], g_ref[:, sl],
                             gam_ref[:, sl], ng_ref[:, sl])
            ds, dq, df, di, dg, dgam, dng = vjp((dst[h], do_ref[:, sl]))
            dst[h] = ds
            for n, v in enumerate((dq, df, di, dg)):
                dp_ref[:, 512 * (n + 1) + h * 128: 512 * (n + 1) + (h + 1) * 128] = v.astype(dp_ref.dtype)
            dgam_ref[:, sl] += dgam
            dng_ref[:, sl] += dng

    def pj(n):
        return pl.BlockSpec((HGC, 512), lambda i: (NCH - 1 - i, n))

    return pcall(
        body, plan, grid=(NCH,),
        in_specs=[pj(1), pj(2), pj(3), pj(4), pl.BlockSpec((2, 512), lambda i: (0, 0)), pl.BlockSpec((1, 512), lambda i: (0, 0)),
                  pl.BlockSpec((1, 4, 128, 128), lambda i: (NCH - 1 - i, 0, 0, 0)), pj(1), pj(0)],
        out_specs=[pl.BlockSpec((HGC, 2560), lambda i: (NCH - 1 - i, 0)), pl.BlockSpec((2, 512), lambda i: (0, 0)),
                   pl.BlockSpec((1, 512), lambda i: (0, 0))],
        out_shape=[S((L, 2560), BF), S((2, 512), f32), S((1, 512), f32)],
        scratch_shapes=[pltpu.VMEM((4, 128, 128), f32)],
        sem=("arbitrary",), name="hgrn_bwd", args=[proj, proj, proj, proj, gamma, hnorm, ssave, dmix, du])


def _shift(x, k):
    return jnp.concatenate([jnp.zeros((k, x.shape[1]), x.dtype), x[:-k]], axis=0)


def _convact(ha, hb, wa, wb, ba, bb):
    ca = wa[2:3] * ha + wa[1:2] * _shift(ha, 1) + wa[0:1] * _shift(ha, 2) + ba
    cb = wb[2:3] * hb + wb[1:2] * _shift(hb, 1) + wb[0:1] * _shift(hb, 2) + bb
    return jax.nn.silu(ca) * cb


CT = 128
NCT = DFF // CT


def convact_fwd(hu, cw, cb, layer, plan=None):
    def body(ha_ref, hb_ref, wa_ref, wb_ref, ba_ref, bb_ref, o_ref):
        o_ref[...] = _convact(ha_ref[...], hb_ref[...], wa_ref[...], wb_ref[...], ba_ref[...], bb_ref[...]).astype(o_ref.dtype)

    def h(off):
        return pl.BlockSpec((L, CT), lambda j: (0, j + off))

    def w(off):
        return pl.BlockSpec((3, CT), lambda j: (0, j + off))

    def b(off):
        return pl.BlockSpec((None, 1, CT), lambda j: (layer, 0, j + off))

    return pcall(body, plan, grid=(NCT,), in_specs=[h(0), h(NCT), w(0), w(NCT), b(0), b(NCT)],
                 out_specs=pl.BlockSpec((L, CT), lambda j: (0, j)), out_shape=S((L, DFF), BF),
                 sem=("parallel",), name=f"convact_fwd{layer}", args=[hu, hu, cw, cw, cb, cb])


def convact_bwd(hu, cw, cb, dact, layer, plan=None):
    def body(ha_ref, hb_ref, wa_ref, wb_ref, ba_ref, bb_ref, g_ref, dh_ref, dw_ref, db_ref, sh, sw, sb):
        j = pl.program_id(0)

        @pl.when(j < NCT)
        def _():
            _, vjp = jax.vjp(_convact, ha_ref[...], hb_ref[...], wa_ref[...], wb_ref[...], ba_ref[...], bb_ref[...])
            dha, dhb, dwa, dwb, dba, dbb = vjp(g_ref[...].astype(f32))
            dh_ref[...] = dha.astype(dh_ref.dtype)
            dw_ref[...] = dwa
            db_ref[...] = dba
            sh[j] = dhb.astype(sh.dtype)
            sw[j] = dwb
            sb[j] = dbb

        @pl.when(j >= NCT)
        def _():
            dh_ref[...] = sh[j - NCT]
            dw_ref[...] = sw[j - NCT]
            db_ref[...] = sb[j - NCT]

    def lo(j):
        return jnp.minimum(j, NCT - 1)

    in_specs = [pl.BlockSpec((L, CT), lambda j: (0, lo(j))), pl.BlockSpec((L, CT), lambda j: (0, lo(j) + NCT)),
                pl.BlockSpec((3, CT), lambda j: (0, lo(j))), pl.BlockSpec((3, CT), lambda j: (0, lo(j) + NCT)),
                pl.BlockSpec((None, 1, CT), lambda j: (layer, 0, lo(j))), pl.BlockSpec((None, 1, CT), lambda j: (layer, 0, lo(j) + NCT)),
                pl.BlockSpec((L, CT), lambda j: (0, lo(j)))]
    return pcall(
        body, plan, grid=(2 * NCT,), in_specs=in_specs,
        out_specs=[pl.BlockSpec((L, CT), lambda j: (0, j)), pl.BlockSpec((3, CT), lambda j: (0, j)), pl.BlockSpec((1, CT), lambda j: (0, j))],
        out_shape=[S((L, 2 * DFF), BF), S((3, 2 * DFF), f32), S((1, 2 * DFF), f32)],
        scratch_shapes=[pltpu.VMEM((NCT, L, CT), BF), pltpu.VMEM((NCT, 3, CT), f32), pltpu.VMEM((NCT, 1, CT), f32)],
        sem=("arbitrary",), name=f"convact_bwd{layer}", args=[hu, hu, cw, cw, cb, cb, dact])


DILS = (1, 4, 16)
AB = 128
NPAIR = 12


def _rope_tables(pos_ref, invf_ref):
    ang = pos_ref[...].astype(f32) * invf_ref[...]
    lane = lax.broadcasted_iota(jnp.int32, (1, 128), 1) % 64
    cosf = jnp.where(lane < 16, jnp.cos(ang), 1.0)
    sn = jnp.sin(ang)
    s_lo = jnp.where(lane < 8, -sn, 0.0)
    s_hi = jnp.where((lane >= 8) & (lane < 16), sn, 0.0)
    return cosf, s_lo, s_hi


def _rope(t, cosf, s_lo, s_hi):
    return t * cosf + pltpu.roll(t, 120, 1) * s_lo + pltpu.roll(t, 8, 1) * s_hi


def _rope_t(g, cosf, s_lo, s_hi):
    return g * cosf + pltpu.roll(g * s_lo, 8, 1) + pltpu.roll(g * s_hi, 120, 1)


def _att_block(q2, kp, kc, vp, vc, first):
    lane = lax.broadcasted_iota(jnp.int32, (1, 128), 1)
    qi = lax.broadcasted_iota(jnp.int32, (AB, 2 * AB), 0) + AB
    kj = lax.broadcasted_iota(jnp.int32, (AB, 2 * AB), 1)
    back = qi - kj
    valid = (back >= 0) & (back <= AB)
    if first:
        valid = valid & (kj >= AB)
    kk = jnp.concatenate([kp, kc], axis=0)
    vv = jnp.concatenate([vp, vc], axis=0)
    o2 = jnp.zeros((AB, 128), f32)
    lse2 = jnp.zeros((AB, 128), f32)
    for e in range(2):
        hm = ((lane >= 64 * e) & (lane < 64 * (e + 1))).astype(f32)
        s = dot_nt(q2 * (hm * 0.125), kk)
        s = jnp.where(valid, s, -jnp.inf)
        m = jnp.max(s, axis=-1, keepdims=True)
        p = jnp.exp(s - m)
        den = jnp.sum(p, axis=-1, keepdims=True)
        o2 = o2 + dot_nn(p, vv * hm) / den
        lse2 = lse2 + (m + jnp.log(den)) * hm
    return o2, lse2


def _att_blocks(dil):
    m = L // dil
    return [(r * m + n * AB, n == 0) for r in range(dil) for n in range(m // AB)]


def deinterleave(x, dil):
    return x if dil == 1 else x.reshape(L // dil, dil, x.shape[1]).swapaxes(0, 1).reshape(L, x.shape[1])


def interleave(x, dil):
    return x if dil == 1 else x.reshape(dil, L // dil, x.shape[1]).swapaxes(0, 1).reshape(L, x.shape[1])


def attn_fwd(qkv, pos, invf, g, plan=None):
    blocks = _att_blocks(DILS[g])

    def body(q_ref, k_ref, v_ref, pos_ref, invf_ref, o_ref, l_ref, qr, kr):
        cosf, s_lo, s_hi = _rope_tables(pos_ref, invf_ref)
        qr[...] = _rope(q_ref[...], cosf, s_lo, s_hi)
        kr[...] = _rope(k_ref[...], cosf, s_lo, s_hi)
        for off, first in blocks:
            cur, prv = pl.ds(off, AB), pl.ds(off if first else off - AB, AB)
            o2, lse2 = _att_block(qr[cur, :], kr[prv, :], kr[cur, :], v_ref[prv, :], v_ref[cur, :], first)
            o_ref[cur, :] = o2
            l_ref[cur, :] = lse2

    def sec(n):
        return pl.BlockSpec((L, 128), lambda p: (0, p + 4 * n))

    return pcall(
        body, plan, grid=(4,),
        in_specs=[sec(0), sec(1), sec(2), pl.BlockSpec((L, 1), lambda p: (0, 0)), pl.BlockSpec((1, 128), lambda p: (0, 0))],
        out_specs=[sec(0), sec(0)], out_shape=[S((L, 512), f32), S((L, 512), f32)],
        scratch_shapes=[pltpu.VMEM((L, 128), f32), pltpu.VMEM((L, 128), f32)],
        sem=("parallel",), name=f"attn_fwd{g}", args=[qkv, qkv, qkv, pos, invf])


def _att_block_bwd(q2, kp, kc, vp, vc, lse2, do2, dl2, first):
    lane = lax.broadcasted_iota(jnp.int32, (1, 128), 1)
    qi = lax.broadcasted_iota(jnp.int32, (AB, 2 * AB), 0) + AB
    kj = lax.broadcasted_iota(jnp.int32, (AB, 2 * AB), 1)
    back = qi - kj
    valid = (back >= 0) & (back <= AB)
    if first:
        valid = valid & (kj >= AB)
    kk = jnp.concatenate([kp, kc], axis=0)
    vv = jnp.concatenate([vp, vc], axis=0)
    dq2 = jnp.zeros((AB, 128), f32)
    dkk = jnp.zeros((2 * AB, 128), f32)
    dvv = jnp.zeros((2 * AB, 128), f32)
    for e in range(2):
        hb = (lane >= 64 * e) & (lane < 64 * (e + 1))
        hm = hb.astype(f32)
        qs = q2 * (hm * 0.125)
        lse = jnp.max(jnp.where(hb, lse2, -jnp.inf), axis=-1, keepdims=True)
        dls = jnp.sum(dl2 * hm, axis=-1, keepdims=True)
        p = jnp.where(valid, jnp.exp(dot_nt(qs, kk) - lse), 0.0)
        dov = do2 * hm
        dp = dot_nt(dov, vv)
        ds = p * (dp - jnp.sum(p * dp, axis=-1, keepdims=True) + dls)
        dq2 = dq2 + dot_nn(ds, kk) * (hm * 0.125)
        dkk = dkk + dot_tn(ds, qs)
        dvv = dvv + dot_tn(p, dov)
    return dq2, dkk[:AB], dkk[AB:], dvv[:AB], dvv[AB:]


def attn_bwd(qkv, pos, invf, lse, do, dl, g, plan=None):
    blocks = _att_blocks(DILS[g])

    def body(q_ref, k_ref, v_ref, pos_ref, invf_ref, l_ref, do_ref, dl_ref, d_ref, qr, kr, dqr, dkr, dvr):
        cosf, s_lo, s_hi = _rope_tables(pos_ref, invf_ref)
        qr[...] = _rope(q_ref[...], cosf, s_lo, s_hi)
        kr[...] = _rope(k_ref[...], cosf, s_lo, s_hi)
        for off, first in blocks:
            cur, prv = pl.ds(off, AB), pl.ds(off if first else off - AB, AB)
            dq2, dkp, dkc, dvp, dvc = _att_block_bwd(qr[cur, :], kr[prv, :], kr[cur, :], v_ref[prv, :], v_ref[cur, :],
                                                     l_ref[cur, :], do_ref[cur, :], dl_ref[cur, :], first)
            dqr[cur, :] = dq2
            dkr[cur, :] = dkc
            dvr[cur, :] = dvc
            if not first:
                dkr[prv, :] += dkp
                dvr[prv, :] += dvp
        d_ref[0] = _rope_t(dqr[...], cosf, s_lo, s_hi).astype(d_ref.dtype)
        d_ref[1] = _rope_t(dkr[...], cosf, s_lo, s_hi).astype(d_ref.dtype)
        d_ref[2] = dvr[...].astype(d_ref.dtype)

    def sec(n):
        return pl.BlockSpec((L, 128), lambda p: (0, p + 4 * n))

    return pcall(
        body, plan, grid=(4,),
        in_specs=[sec(0), sec(1), sec(2), pl.BlockSpec((L, 1), lambda p: (0, 0)), pl.BlockSpec((1, 128), lambda p: (0, 0)),
                  sec(0), sec(0), sec(0)],
        out_specs=pl.BlockSpec((3, L, 128), lambda p: (0, 0, p)), out_shape=S((3, L, 512), BF),
        scratch_shapes=[pltpu.VMEM((L, 128), f32)] * 5,
        sem=("parallel",), name=f"attn_bwd{g}", args=[qkv, qkv, qkv, pos, invf, lse, do, dl])


def _merge(o0, o1, o2, l0, l1, l2):
    m = jnp.maximum(jnp.maximum(l0, l1), l2)
    e0, e1, e2 = jnp.exp(l0 - m), jnp.exp(l1 - m), jnp.exp(l2 - m)
    return (e0 * o0 + e1 * o1 + e2 * o2) / (e0 + e1 + e2)


def attn_merge_fwd(os_, ls_):
    def body(o0, o1, o2, l0, l1, l2, o_ref):
        o_ref[...] = _merge(o0[...], o1[...], o2[...], l0[...], l1[...], l2[...]).astype(o_ref.dtype)

    blk = pl.BlockSpec((TR, 512), lambda i: (i, 0))
    return pl.pallas_call(
        body, grid=(L // TR,), in_specs=[blk] * 6, out_specs=blk, out_shape=S((L, 512), BF),
        compiler_params=_cp(("parallel",)), name="attn_merge_fwd")(*os_, *ls_)


def attn_merge_bwd(os_, ls_, do, plan=None):
    def body(o0, o1, o2, l0, l1, l2, g_ref, *outs):
        _, vjp = jax.vjp(_merge, o0[...], o1[...], o2[...], l0[...], l1[...], l2[...])
        for o_ref, v in zip(outs, vjp(g_ref[...])):
            o_ref[...] = v

    blk = pl.BlockSpec((TR, 512), lambda i: (i, 0))
    outs = pcall(body, plan, grid=(L // TR,), in_specs=[blk] * 7, out_specs=[blk] * 6, out_shape=[S((L, 512), f32)] * 6,
                 sem=("parallel",), name="attn_merge_bwd", args=[*os_, *ls_, do])
    return outs[:3], outs[3:]


def _invf_lanes():
    half = 8
    inv = ROPE_THETA ** (-np.arange(half, dtype=np.float32) * 2.0 / 16.0)
    lane = np.arange(128) % 64
    return jnp.asarray(np.where(lane < 16, inv[lane % 8], 0.0).astype(np.float32)[None, :])


def hosted(C, host, fn):
    p = C.plan(host) if C is not None else None
    out = fn(p)
    if p is not None:
        C.done(p)
    return out


def _ffn_fwd(h, g_row, W, cb, layer, C):
    hn = rms_fwd(h, g_row, f"rms_ffn{layer}")
    hu = hosted(C, f"ffn_in{layer}", lambda p: matmul(hn, W[("ffn_w_in", layer)], mode="nn", tm=1024, tn=1408, tk=1024,
                                                      plan=p, name=f"ffn_in{layer}"))
    act = hosted(C, f"convact_fwd{layer}", lambda p: convact_fwd(hu, W[("ffn_conv_w", layer)], cb, layer, plan=p))
    h2 = matmul(act, W[("ffn_w_out", layer)], mode="nn", tm=1024, tn=1024, tk=1408, add=h, name=f"ffn_out{layer}")
    return h2, (hn, hu, act)


def _ffn_bwd(dh, h, g_row, W, cb, saved, layer, C, G):
    hn, hu, act = saved
    w_in, w_out = W[("ffn_w_in", layer)], W[("ffn_w_out", layer)]
    dact = matmul(dh, w_out, mode="nt", tm=1024, tn=1408, tk=1024, name=f"ffn_out_dx{layer}")
    G[("ffn_w_out", layer)] = matmul(act, dh, mode="tn", tm=1408, tn=1024, tk=512, out_dtype=BF, name=f"ffn_out_dw{layer}")
    dhu, G[("ffn_conv_w", layer)], g_cb = hosted(
        C, f"convact_bwd{layer}", lambda p: convact_bwd(hu, W[("ffn_conv_w", layer)], cb, dact, layer, plan=p))
    dhn = hosted(C, f"ffn_in_dx{layer}", lambda p: matmul(dhu, w_in, mode="nt", tm=1024, tn=1024, tk=1408, plan=p,
                                                         name=f"ffn_in_dx{layer}"))
    G[("ffn_w_in", layer)] = hosted(C, f"ffn_in_dw{layer}", lambda p: matmul(
        hn, dhu, mode="tn", tm=1024, tn=1408, tk=512, out_dtype=BF, plan=p, name=f"ffn_in_dw{layer}"))
    dh2, g_norm = rms_bwd(h, g_row, [dhn], dh, f"rms_ffn_bwd{layer}")
    return dh2, g_cb, g_norm


def local_step(x, pos, tgt, sm, W, C=None):
    G = C.grads if C is not None else {}
    nm, nf = sm["norm_mix"], sm["norm_ffn"]
    invf = _invf_lanes()
    are = sm["s5_A_re"].reshape(NST, 1)
    aim = sm["s5_A_im"].reshape(NST, 1)
    ldt = sm["s5_log_dt"].reshape(1, 32)
    bre = sm["s5_B_re"].reshape(NST, 16)
    bim = sm["s5_B_im"].reshape(NST, 16)
    cre = jnp.swapaxes(sm["s5_C_re"][0], 1, 2).reshape(NST, 16)
    cim = jnp.swapaxes(sm["s5_C_im"][0], 1, 2).reshape(NST, 16)
    drow = sm["s5_D"].reshape(1, S5W)
    wbr, wbi, wcr, wci, abr, abi = s5_params_fwd(are, aim, ldt, bre, bim, cre, cim)
    hn0 = rms_fwd(x, nm[0:1], "rms_mix0")
    cb3 = sm["ffn_conv_b3"]
    proj = hosted(C, "mix_in", lambda p: matmul(hn0, W[("mix_w_in", 0)], mode="nn", tm=1024, tn=1280, tk=1024, plan=p, name="mix_in"))
    xs_re, xs_im, y5 = hosted(C, "s5_scan_fwd", lambda p: s5_scan_fwd(proj, wbr, wbi, wcr, wci, abr, abi, drow, plan=p))
    oa = s5_glu_fwd(y5, W[("s5_glu_w", 0)], sm["s5_glu_b"])
    ob, ssave = hosted(C, "hgrn_fwd", lambda p: hgrn_fwd(proj, sm["hgrn_gamma"], sm["hgrn_norm"], plan=p))
    cat = jnp.concatenate([oa, ob], axis=1)
    h1 = matmul(cat, W[("mix_w_out", 0)], mode="nn", tm=1024, tn=1024, tk=1024, add=x, name="mix_out")
    h2, ffn0 = _ffn_fwd(h1, nf[0:1], W, cb3, 0, C)
    hn2 = rms_fwd(h2, nm[1:2], "rms_mix1")
    wqkv = W[("att_w_qkv", 0)]
    hn2_g, pos_g, qkv_g, o_g, l_g, lc_g = [], [], [], [], [], []
    for g, dil in enumerate(DILS):
        hn2_g.append(deinterleave(hn2, dil))
        pos_g.append(deinterleave(pos, dil))
        qkv_g.append(hosted(C, f"att_qkv{g}", lambda p: matmul(
            hn2_g[g], wqkv, mode="nn", tm=1024, tn=512, tk=1024, dims=(L, 1536, D),
            b_spec=pl.BlockSpec((D, 512), lambda i, j, k, g=g: (0, 3 * j + g)), plan=p, name=f"att_qkv{g}")))
        o_c, l_c = hosted(C, f"attn_fwd{g}", lambda p: attn_fwd(qkv_g[g], pos_g[g], invf, g, plan=p))
        lc_g.append(l_c)
        o_g.append(interleave(o_c, dil))
        l_g.append(interleave(l_c, dil))
    o = attn_merge_fwd(o_g, l_g)
    h3 = matmul(o, W[("att_w_o", 0)], mode="nn", tm=1024, tn=1024, tk=512, add=h2, name="att_o")
    h4, ffn1 = _ffn_fwd(h3, nf[1:2], W, cb3, 1, C)
    loss, dh, g_nfinal = loss_head(h4, sm["norm_final"].reshape(1, D), tgt)
    dh, g_cb1, g_nf1 = _ffn_bwd(dh, h3, nf[1:2], W, cb3, ffn1, 1, C, G)
    do = matmul(dh, W[("att_w_o", 0)], mode="nt", tm=1024, tn=512, tk=1024, name="att_o_dx")
    G[("att_w_o", 0)] = matmul(o, dh, mode="tn", tm=512, tn=1024, tk=512, out_dtype=BF, name="att_o_dw")
    do_g, dl_g = hosted(C, "attn_merge_bwd", lambda p: attn_merge_bwd(o_g, l_g, do, plan=p))
    dhn2_g, gq = [], []
    for g, dil in enumerate(DILS):
        d3 = hosted(C, f"attn_bwd{g}", lambda p: attn_bwd(qkv_g[g], pos_g[g], invf, lc_g[g], deinterleave(do_g[g], dil),
                                                        deinterleave(dl_g[g], dil), g, plan=p))
        dx = matmul(d3, wqkv, mode="nt", tm=1024, tn=1024, tk=512, dims=(L, D, 1536),
                    a_spec=pl.BlockSpec((None, 1024, 512), lambda i, j, k: (k, i, 0)),
                    b_spec=pl.BlockSpec((D, 512), lambda i, j, k, g=g: (0, 3 * k + g)), name=f"att_qkv_dx{g}")
        dhn2_g.append(interleave(dx, dil))
        gq.append(matmul(hn2_g[g], d3, mode="tn", tm=1024, tn=512, tk=512, out_dtype=BF, dims=(D, 1536, L),
                         b_spec=pl.BlockSpec((None, 512, 512), lambda i, j, k: (j, k, 0)), name=f"att_qkv_dw{g}"))
    G[("att_w_qkv", 0)] = jnp.concatenate([gq[g][:, 512 * s:512 * (s + 1)] for s in range(3) for g in range(3)], axis=1)
    dh, g_nm1 = rms_bwd(h2, nm[1:2], dhn2_g, dh, "rms_mix_bwd1")
    dh, g_cb0, g_nf0 = _ffn_bwd(dh, h1, nf[0:1], W, cb3, ffn0, 0, C, G)
    dmix = matmul(dh, W[("mix_w_out", 0)], mode="nt", tm=1024, tn=1024, tk=1024, name="mix_out_dx")
    G[("mix_w_out", 0)] = matmul(cat, dh, mode="tn", tm=1024, tn=1024, tk=512, out_dtype=BF, name="mix_out_dw")
    dy5, g_glu_w, g_glu_b = s5_glu_bwd(y5, W[("s5_glu_w", 0)], sm["s5_glu_b"], dmix)
    G[("s5_glu_w", 0)] = g_glu_w.astype(BF)
    du, gwbr, gwbi, gwcr, gwci, gabr, gabi, g_d = hosted(C, "s5_scan_bwd", lambda p: s5_scan_bwd(
        dy5, proj, xs_re, xs_im, wbr, wbi, wcr, wci, abr, abi, drow, plan=p))
    g_are, g_aim, g_ldt, g_bre, g_bim, g_cre, g_cim = s5_params_bwd(are, aim, ldt, bre, bim, cre, cim,
                                                                   (gwbr, gwbi, gwcr, gwci, gabr, gabi))
    dproj, g_gamma, g_hnorm = hosted(C, "hgrn_bwd", lambda p: hgrn_bwd(proj, sm["hgrn_gamma"], sm["hgrn_norm"], ssave, dmix, du,
                                                                       plan=p))
    dhn0 = hosted(C, "mix_in_dx", lambda p: matmul(dproj, W[("mix_w_in", 0)], mode="nt", tm=1024, tn=1024, tk=1280, plan=p,
                                                  name="mix_in_dx"))
    G[("mix_w_in", 0)] = matmul(hn0, dproj, mode="tn", tm=1024, tn=1280, tk=512, out_dtype=BF, name="mix_in_dw")
    gx, g_nm0 = hosted(C, "rms_mix_bwd0", lambda p: rms_bwd(x, nm[0:1], [dhn0], dh, "rms_mix_bwd0", plan=p))
    small = {
        "norm_mix": jnp.concatenate([g_nm0, g_nm1], axis=0), "norm_ffn": jnp.concatenate([g_nf0, g_nf1], axis=0),
        "norm_final": g_nfinal.reshape(D),
        "s5_A_re": g_are.reshape(1, 32, 64), "s5_A_im": g_aim.reshape(1, 32, 64), "s5_log_dt": g_ldt.reshape(1, 32),
        "s5_B_re": g_bre.reshape(1, 32, 64, 16), "s5_B_im": g_bim.reshape(1, 32, 64, 16),
        "s5_C_re": jnp.swapaxes(g_cre.reshape(1, 32, 64, 16), 2, 3), "s5_C_im": jnp.swapaxes(g_cim.reshape(1, 32, 64, 16), 2, 3),
        "s5_D": g_d.reshape(1, 32, 16), "s5_glu_b": g_glu_b, "hgrn_gamma": g_gamma, "hgrn_norm": g_hnorm,
        "ffn_conv_b": jnp.concatenate([g_cb0, g_cb1], axis=0),
    }
    return loss, gx, G, small


BIG = ("mix_w_in", "mix_w_out", "s5_glu_w", "att_w_qkv", "att_w_o", "ffn_w_in", "ffn_w_out", "ffn_conv_w")
SMALL = ("norm_mix", "norm_ffn", "norm_final", "s5_A_re", "s5_A_im", "s5_log_dt", "s5_B_re", "s5_B_im", "s5_C_re", "s5_C_im",
         "s5_D", "s5_glu_b", "hgrn_gamma", "hgrn_norm", "ffn_conv_b")


def cast_bf16(w, name, plan=None):
    nl, r, c = w.shape
    w2 = w.reshape(nl * r, c)
    tr = 256 if (nl * r) % 256 == 0 else nl * r

    def body(w_ref, o_ref):
        o_ref[...] = w_ref[...].astype(BF)

    out = pcall(body, plan, grid=(nl * r // tr,), in_specs=[pl.BlockSpec((tr, c), lambda i: (i, 0))],
                out_specs=pl.BlockSpec((tr, c), lambda i: (i, 0)), out_shape=S((nl * r, c), BF),
                sem=("parallel",), name=name, args=[w2])
    return out.reshape(nl, r, c)


SCHEDULE = {
    "cast_ffn_w_in": [("G", "mix_w_in", 0)],
    "mix_in": [("G", "mix_w_out", 0), ("G", "s5_glu_w", 0)],
    "s5_scan_fwd": [("G", "ffn_w_in", 0, (0, 2))],
    "hgrn_fwd": [("G", "ffn_w_in", 0, (1, 2)), ("G", "ffn_conv_w", 0), ("G", "ffn_conv_w", 1), ("G", "att_w_qkv", 0, (0, 2))],
    "ffn_in0": [("G", "ffn_w_out", 0)],
    "convact_fwd0": [("G", "att_w_qkv", 0, (1, 2))],
    "att_qkv0": [("G", "att_w_o", 0)],
    "attn_fwd0": [("G", "ffn_w_in", 1, (0, 2))],
    "attn_fwd1": [("G", "ffn_w_in", 1, (1, 2))],
    "attn_fwd2": [("G", "ffn_w_out", 1)],
    "convact_bwd1": [("A", "ffn_w_out", 1)],
    "ffn_in_dx1": [("B", "ffn_w_out", 1)],
    "attn_merge_bwd": [("A", "att_w_o", 0), ("A", "ffn_conv_w", 1)],
    "attn_bwd0": [("A", "ffn_w_in", 1, (0, 2))],
    "attn_bwd1": [("A", "ffn_w_in", 1, (1, 2)), ("B", "att_w_o", 0), ("B", "ffn_conv_w", 1)],
    "attn_bwd2": [("B", "ffn_w_in", 1)],
    "convact_bwd0": [("A", "att_w_qkv", 0, (0, 2))],
    "ffn_in_dw0": [("A", "att_w_qkv", 0, (1, 2))],
    "s5_scan_bwd": [("A", "ffn_w_out", 0), ("A", "mix_w_out", 0), ("A", "s5_glu_w", 0), ("A", "ffn_conv_w", 0),
                    ("B", "att_w_qkv", 0)],
    "hgrn_bwd": [("A", "ffn_w_in", 0), ("B", "ffn_w_out", 0), ("B", "mix_w_out", 0), ("B", "s5_glu_w", 0), ("B", "ffn_conv_w", 0)],
    "mix_in_dx": [("B", "ffn_w_in", 0)],
    "adam_ffn_w_in": [("A", "mix_w_in", 0, (0, 2)), ("A", "small", 0)],
    "adam_ffn_w_out": [("A", "mix_w_in", 0, (1, 2))],
    "adam_att_w_qkv": [("B", "mix_w_in", 0), ("B", "small", 0)],
}


class Comm:
    def __init__(self, shards, shapes):
        self.shards, self.shapes = shards, shapes
        self.W, self.grads, self.slots = {}, {}, {}
        self.small = None

    def plan(self, host):
        items = SCHEDULE.get(host)
        if not items:
            return None
        p = Plan()
        for it in items:
            kind, name, l = it[:3]
            part, parts = it[3] if len(it) > 3 else (0, 1)
            if name == "small":
                kdst = p.buf("slots:small", arr=self.slots.get("small"), shape=S((8,) + self.small.shape, f32), write=True)
                if kind == "A":
                    ReduceOp(p, p.buf("g:small", arr=self.small), kdst, None, self.small.shape, False, 0, 0, whole=True)
                else:
                    ForwardOp(p, kdst, None, whole=True)
                continue
            nl, R, C_ = self.shapes[name]
            rows = name in ROW_SHARDED
            r0, nr = part * (R // parts), R // parts
            if kind == "G":
                sh = self.shards[name]
                kdst = p.buf(f"W:{name}:{l}", arr=self.W.get((name, l)), shape=S((4 * R, C_) if rows else (R, 4 * C_), sh.dtype),
                             write=True)
                GatherOp(p, p.buf("shard:" + name, arr=sh), kdst, l, self.shapes[name], rows, r0, nr, split=(nr % 32 == 0))
            else:
                g = self.grads[(name, l)]
                kdst = p.buf("slots:" + name, arr=self.slots.get(name), shape=S((8, nl, R, C_), g.dtype), write=True)
                if kind == "A":
                    ReduceOp(p, p.buf(f"g:{name}:{l}", arr=g), kdst, l, self.shapes[name], rows, r0, nr)
                else:
                    ForwardOp(p, kdst, l)
        return p

    def done(self, p):
        for k, arr in p.out.items():
            tag, name = k.split(":")[:2]
            if tag == "W":
                self.W[(name, int(k.split(":")[2]))] = arr
            else:
                self.slots[name] = arr


def _adamw(w, g, m, v):
    m = B1 * m + (1.0 - B1) * g
    v = B2 * v + (1.0 - B2) * jnp.square(g)
    m_hat = m / (1.0 - B1 ** STEP)
    v_hat = v / (1.0 - B2 ** STEP)
    return -LR * (m_hat / (jnp.sqrt(v_hat) + AEPS) + WD * w), m, v


def adam_big(w, m, v, slots, name, plan=None):
    nl, R, C = w.shape
    tr = 128 if R % 128 == 0 else (64 if R % 64 == 0 else R)

    def body(w_ref, m_ref, v_ref, s_ref, g_ref, d_ref, nm_ref, nv_ref):
        g = s_ref[0].astype(f32)
        for s in range(1, 8):
            g = g + s_ref[s].astype(f32)
        d, nm_, nv_ = _adamw(w_ref[...], g, m_ref[...], v_ref[...])
        g_ref[...] = g
        d_ref[...] = d
        nm_ref[...] = nm_
        nv_ref[...] = nv_

    blk = pl.BlockSpec((None, tr, C), lambda l, i: (l, i, 0))
    return pcall(body, plan, grid=(nl, R // tr),
                 in_specs=[blk, blk, blk, pl.BlockSpec((8, None, tr, C), lambda l, i: (0, l, i, 0))],
                 out_specs=[blk] * 4, out_shape=[S((nl, R, C), f32)] * 4,
                 sem=("parallel", "parallel"), name=name, args=[w, m, v, slots])


def adam_small(w, m, v, slots):
    R = w.shape[0]
    tr = 256

    def body(w_ref, m_ref, v_ref, s_ref, g_ref, d_ref, nm_ref, nv_ref):
        g = s_ref[0]
        for s in range(1, 8):
            g = g + s_ref[s]
        d, nm_, nv_ = _adamw(w_ref[...], g, m_ref[...], v_ref[...])
        g_ref[...] = g
        d_ref[...] = d
        nm_ref[...] = nm_
        nv_ref[...] = nv_

    blk = pl.BlockSpec((tr, 128), lambda i: (i, 0))
    return pl.pallas_call(
        body, grid=(R // tr,), in_specs=[blk, blk, blk, pl.BlockSpec((8, tr, 128), lambda i: (0, i, 0))],
        out_specs=[blk] * 4, out_shape=[S((R, 128), f32)] * 4,
        compiler_params=_cp(("parallel",)), name="adam_small")(w, m, v, slots)


def _pack(d):
    flat = jnp.concatenate([d[n].reshape(-1) for n in SMALL])
    n = flat.shape[0]
    rows = -(-n // (256 * 128)) * 256
    return jnp.pad(flat, (0, rows * 128 - n)).reshape(rows, 128)


def _unpack(p, like):
    flat = p.reshape(-1)
    out, off = {}, 0
    for n in SMALL:
        sz = math.prod(like[n].shape)
        out[n] = flat[off:off + sz].reshape(like[n].shape)
        off += sz
    return out


def kernel(x, positions, norm_mix, norm_ffn, norm_final, mix_w_in, mix_w_out, s5_A_re, s5_A_im, s5_log_dt, s5_B_re, s5_B_im, s5_C_re, s5_C_im, s5_D, s5_glu_w, s5_glu_b, hgrn_gamma, hgrn_norm, att_w_qkv, att_w_o, ffn_w_in, ffn_conv_w, ffn_conv_b, ffn_w_out, loss_target, m_norm_mix, m_norm_ffn, m_norm_final, m_mix_w_in, m_mix_w_out, m_s5_A_re, m_s5_A_im, m_s5_log_dt, m_s5_B_re, m_s5_B_im, m_s5_C_re, m_s5_C_im, m_s5_D, m_s5_glu_w, m_s5_glu_b, m_hgrn_gamma, m_hgrn_norm, m_att_w_qkv, m_att_w_o, m_ffn_w_in, m_ffn_conv_w, m_ffn_conv_b, m_ffn_w_out, v_norm_mix, v_norm_ffn, v_norm_final, v_mix_w_in, v_mix_w_out, v_s5_A_re, v_s5_A_im, v_s5_log_dt, v_s5_B_re, v_s5_B_im, v_s5_C_re, v_s5_C_im, v_s5_D, v_s5_glu_w, v_s5_glu_b, v_hgrn_gamma, v_hgrn_norm, v_att_w_qkv, v_att_w_o, v_ffn_w_in, v_ffn_conv_w, v_ffn_conv_b, v_ffn_w_out):
    a = dict(locals())
    weights = BIG + SMALL
    w = {n: a[n] for n in weights}
    m = {n: a["m_" + n] for n in weights}
    v = {n: a["v_" + n] for n in weights}
    shards = {"ffn_conv_w": ffn_conv_w}
    C = Comm(shards, {n: w[n].shape for n in BIG})
    for n in ("mix_w_in", "ffn_w_in", "mix_w_out", "s5_glu_w", "ffn_w_out", "att_w_qkv", "att_w_o"):
        shards[n] = hosted(C, "cast_" + n, lambda p: cast_bf16(w[n], "cast_" + n, plan=p))
    sm = {n: w[n] for n in SMALL}
    sm["ffn_conv_b3"] = ffn_conv_b.reshape(2, 1, 2 * DFF)
    loss, gx, _, gsmall = local_step(x[0], positions.reshape(L, 1), loss_target[0], sm, C.W, C)
    C.small = _pack(gsmall)
    res = {}
    for n in ("ffn_w_in", "ffn_w_out", "att_w_qkv", "att_w_o", "mix_w_out", "s5_glu_w", "ffn_conv_w", "mix_w_in"):
        res[n] = hosted(C, "adam_" + n, lambda p: adam_big(w[n], m[n], v[n], C.slots[n], "adam_" + n, plan=p))
    packed = adam_small(_pack({n: w[n] for n in SMALL}), _pack({n: m[n] for n in SMALL}), _pack({n: v[n] for n in SMALL}),
                        C.slots["small"])
    small_out = [_unpack(p, {n: w[n] for n in SMALL}) for p in packed]
    for n in SMALL:
        res[n] = tuple(so[n] for so in small_out)
    total = lax.psum(loss[0, 0], ("x", "y", "c"))
    order = ("norm_mix", "norm_ffn", "norm_final", "mix_w_in", "mix_w_out", "s5_A_re", "s5_A_im", "s5_log_dt", "s5_B_re", "s5_B_im",
             "s5_C_re", "s5_C_im", "s5_D", "s5_glu_w", "s5_glu_b", "hgrn_gamma", "hgrn_norm", "att_w_qkv", "att_w_o", "ffn_w_in",
             "ffn_conv_w", "ffn_conv_b", "ffn_w_out")
    return (total, gx[None], *[res[n][0] for n in order], *[res[n][1] for n in order], *[res[n][2] for n in order],
            *[res[n][3] for n in order])
```

```python
import functools
import math

import numpy as np
import jax
import jax.numpy as jnp
from jax import lax
from jax.experimental import pallas as pl
from jax.experimental.pallas import tpu as pltpu

f32 = jnp.float32
BF = jnp.bfloat16
HI = lax.Precision.HIGHEST
S = jax.ShapeDtypeStruct
MESH = pl.DeviceIdType.MESH

L = 2048
D = 1024
EPS = 1e-6
S5W = 512
NST = 2048
HGC = 64
DFF = 2816
ROPE_THETA = 500000.0
LR, B1, B2, AEPS, WD, STEP = 0.001, 0.9, 0.999, 1e-08, 0.01, 10
VMEM_LIMIT = 56 * 1024 * 1024


def _cp(sem=None):
    return pltpu.CompilerParams(dimension_semantics=sem, vmem_limit_bytes=VMEM_LIMIT)


ANY = pl.BlockSpec(memory_space=pl.ANY)
ROW_SHARDED = ("mix_w_out", "s5_glu_w", "ffn_w_out")


def _coords():
    x, y, c = lax.axis_index("x"), lax.axis_index("y"), lax.axis_index("c")
    return x, y, c, 2 * x + y, [(1 - x, y), (x, 1 - y), (1 - x, 1 - y)]


def _rows(start, n):
    return pl.ds(start if isinstance(start, int) else pl.multiple_of(start, 8), n)


def _cols(q, n):
    return pl.ds(pl.multiple_of(q * n, 128), n)


class Plan:
    def __init__(self):
        self.bufs, self.ops, self.nsem, self.out = {}, [], 0, {}

    def buf(self, key, arr=None, shape=None, write=False):
        b = self.bufs.setdefault(key, dict(arr=arr, shape=shape, write=False))
        b["write"] = b["write"] or write
        return key

    def add(self, op):
        op.base = self.nsem
        self.nsem += op.nsem
        self.ops.append(op)


class GatherOp:
    nsem = 13

    def __init__(self, plan, ksrc, kdst, l, shard_shape, rows, r0, nr, split):
        self.ksrc, self.kdst, self.l, (_, self.R, self.C), self.rows, self.r0, self.nr, self.split = (
            ksrc, kdst, l, shard_shape, rows, r0, nr, split)
        self.h = nr // 2 if split else nr
        plan.add(self)

    def _dst(self, R_, q, start, n):
        if self.rows:
            return R_[self.kdst].at[_rows(q * self.R + start, n), :]
        return R_[self.kdst].at[_rows(start, n), _cols(q, self.C)]

    def _mine(self, c):
        return self.r0 + (c * self.h if self.split else 0)

    def _theirs(self, c):
        return self.r0 + ((1 - c) * self.h if self.split else 0)

    def _copies(self, R_, sems):
        x, y, c, me, others = _coords()
        src = R_[self.ksrc]
        local = pltpu.make_async_copy(src.at[self.l, _rows(self.r0, self.nr), :], self._dst(R_, me, self.r0, self.nr),
                                      sems.at[self.base + 12])
        send, fwd = [], []
        for k, (px, py) in enumerate(others):
            q = 2 * px + py
            send.append((
                pltpu.make_async_remote_copy(src.at[self.l, _rows(self._mine(c), self.h), :], self._dst(R_, me, self._mine(c), self.h),
                                             sems.at[self.base + k], sems.at[self.base + 3 + k], device_id=(px, py, c), device_id_type=MESH),
                pltpu.make_async_remote_copy(src.at[self.l, _rows(self._mine(c), self.h), :], self._dst(R_, q, self._mine(c), self.h),
                                             sems.at[self.base + k], sems.at[self.base + 3 + k], device_id=(px, py, c), device_id_type=MESH)))
            fwd.append((
                pltpu.make_async_remote_copy(self._dst(R_, q, self._mine(c), self.h), self._dst(R_, q, self._mine(c), self.h),
                                             sems.at[self.base + 6 + k], sems.at[self.base + 9 + k], device_id=(x, y, 1 - c), device_id_type=MESH),
                pltpu.make_async_remote_copy(self._dst(R_, q, self._theirs(c), self.h), self._dst(R_, q, self._theirs(c), self.h),
                                             sems.at[self.base + 6 + k], sems.at[self.base + 9 + k], device_id=(x, y, 1 - c), device_id_type=MESH)))
        return local, send, fwd

    def start(self, R_, sems):
        local, send, _ = self._copies(R_, sems)
        local.start()
        for out, _ in send:
            out.start()

    def finish(self, R_, sems):
        local, send, fwd = self._copies(R_, sems)
        for k in range(3):
            send[k][1].wait_recv()
            if self.split:
                fwd[k][0].start()
        for k in range(3):
            if self.split:
                fwd[k][1].wait_recv()
                fwd[k][0].wait_send()
            send[k][0].wait_send()
        local.wait()


class ReduceOp:
    nsem = 7

    def __init__(self, plan, ksrc, kdst, l, shard_shape, rows, r0, nr, whole=False):
        self.ksrc, self.kdst, self.l, (self.R, self.C), self.rows, self.r0, self.nr, self.whole = (
            ksrc, kdst, l, shard_shape[-2:], rows, r0, nr, whole)
        plan.add(self)

    def _piece(self, R_, q):
        g = R_[self.ksrc]
        if self.whole:
            return g
        if self.rows:
            return g.at[_rows(q * self.R + self.r0, self.nr), :]
        return g.at[_rows(self.r0, self.nr), _cols(q, self.C)]

    def _slot(self, R_, s):
        if self.whole:
            return R_[self.kdst].at[s]
        return R_[self.kdst].at[s, self.l, _rows(self.r0, self.nr), :]

    def _copies(self, R_, sems):
        x, y, c, me, others = _coords()
        local = pltpu.make_async_copy(self._piece(R_, me), self._slot(R_, 2 * me + c), sems.at[self.base + 6])
        send = []
        for k, (px, py) in enumerate(others):
            q = 2 * px + py
            send.append((
                pltpu.make_async_remote_copy(self._piece(R_, q), self._slot(R_, 2 * me + c), sems.at[self.base + k],
                                             sems.at[self.base + 3 + k], device_id=(px, py, c), device_id_type=MESH),
                pltpu.make_async_remote_copy(self._piece(R_, q), self._slot(R_, 2 * q + c), sems.at[self.base + k],
                                             sems.at[self.base + 3 + k], device_id=(px, py, c), device_id_type=MESH)))
        return local, send

    def start(self, R_, sems):
        local, send = self._copies(R_, sems)
        local.start()
        for out, _ in send:
            out.start()

    def finish(self, R_, sems):
        local, send = self._copies(R_, sems)
        local.wait()
        for out, inn in send:
            inn.wait_recv()
            out.wait_send()


class ForwardOp:
    nsem = 8

    def __init__(self, plan, kdst, l, whole=False):
        self.kdst, self.l, self.whole = kdst, l, whole
        plan.add(self)

    def _slot(self, R_, s):
        return R_[self.kdst].at[s] if self.whole else R_[self.kdst].at[s, self.l]

    def _copies(self, R_, sems):
        x, y, c, me, others = _coords()
        return [(pltpu.make_async_remote_copy(self._slot(R_, 2 * q + c), self._slot(R_, 2 * q + c), sems.at[self.base + q],
                                              sems.at[self.base + 4 + q], device_id=(x, y, 1 - c), device_id_type=MESH),
                 pltpu.make_async_remote_copy(self._slot(R_, 2 * q + 1 - c), self._slot(R_, 2 * q + 1 - c), sems.at[self.base + q],
                                              sems.at[self.base + 4 + q], device_id=(x, y, 1 - c), device_id_type=MESH))
                for q in range(4)]

    def start(self, R_, sems):
        for out, _ in self._copies(R_, sems):
            out.start()

    def finish(self, R_, sems):
        for out, inn in self._copies(R_, sems):
            inn.wait_recv()
            out.wait_send()


def pcall(body, plan, *, grid, in_specs, out_specs, out_shape, scratch_shapes=(), sem, name, args):
    multi = isinstance(out_shape, (list, tuple))
    if plan is None or not plan.ops:
        return pl.pallas_call(body, grid=grid, in_specs=in_specs, out_specs=out_specs, out_shape=out_shape,
                              scratch_shapes=list(scratch_shapes), compiler_params=_cp(sem), name=name)(*args)
    outs = list(out_shape) if multi else [out_shape]
    ospecs = list(out_specs) if multi else [out_specs]
    kin = [k for k, b in plan.bufs.items() if b["arr"] is not None]
    kout = [k for k, b in plan.bufs.items() if b["write"]]
    n_in, n_out, n_scr = len(in_specs), len(outs), len(scratch_shapes)

    def wrapped(*refs):
        o0 = n_in + len(kin)
        s0 = o0 + n_out + len(kout)
        R_ = dict(zip(kin, refs[n_in:o0]))
        R_.update(zip(kout, refs[o0 + n_out:s0]))
        sems = refs[s0 + n_scr]
        first = functools.reduce(jnp.logical_and, [pl.program_id(d) == 0 for d in range(len(grid))])
        last = functools.reduce(jnp.logical_and, [pl.program_id(d) == grid[d] - 1 for d in range(len(grid))])

        @pl.when(first)
        def _():
            for op in plan.ops:
                op.start(R_, sems)

        body(*refs[:n_in], *refs[o0:o0 + n_out], *refs[s0:s0 + n_scr])

        @pl.when(last)
        def _():
            for op in plan.ops:
                op.finish(R_, sems)

    def shape_of(k):
        b = plan.bufs[k]
        return S(b["arr"].shape, b["arr"].dtype) if b["arr"] is not None else b["shape"]

    res = pl.pallas_call(
        wrapped, grid=grid, in_specs=list(in_specs) + [ANY] * len(kin), out_specs=ospecs + [ANY] * len(kout),
        out_shape=outs + [shape_of(k) for k in kout],
        scratch_shapes=list(scratch_shapes) + [pltpu.SemaphoreType.DMA((plan.nsem,))],
        input_output_aliases={n_in + kin.index(k): n_out + kout.index(k) for k in kout if plan.bufs[k]["arr"] is not None},
        compiler_params=pltpu.CompilerParams(dimension_semantics=("arbitrary",) * len(grid), vmem_limit_bytes=VMEM_LIMIT,
                                             has_side_effects=True),
        name=name)(*args, *[plan.bufs[k]["arr"] for k in kin])
    plan.out = dict(zip(kout, res[n_out:]))
    return list(res[:n_out]) if multi else res[0]


def _dg(a, b, ca, cb):
    return lax.dot_general(a.astype(BF), b.astype(BF), (((ca,), (cb,)), ((), ())), preferred_element_type=f32)


@jax.custom_vjp
def dot_nn(a, b):
    return _dg(a, b, 1, 0)


@jax.custom_vjp
def dot_nt(a, b):
    return _dg(a, b, 1, 1)


@jax.custom_vjp
def dot_tn(a, b):
    return _dg(a, b, 0, 0)


dot_nn.defvjp(lambda a, b: (dot_nn(a, b), (a, b)),
              lambda r, g: (dot_nt(g, r[1]).astype(r[0].dtype), dot_tn(r[0], g).astype(r[1].dtype)))
dot_nt.defvjp(lambda a, b: (dot_nt(a, b), (a, b)),
              lambda r, g: (dot_nn(g, r[1]).astype(r[0].dtype), dot_tn(g, r[0]).astype(r[1].dtype)))
dot_tn.defvjp(lambda a, b: (dot_tn(a, b), (a, b)),
              lambda r, g: (dot_nt(r[1], g).astype(r[0].dtype), dot_nn(r[0], g).astype(r[1].dtype)))


def matmul(a, b, *, mode, tm, tn, tk, out_dtype=f32, add=None, b_lead=None, a_spec=None, b_spec=None, dims=None, plan=None, name):
    a_over, b_over = a_spec, b_spec
    if mode == "nn":
        (M, K), N = a.shape[-2:], b.shape[-1]
        a_spec = pl.BlockSpec((tm, tk), lambda i, j, k: (i, k))
        b_blk, b_idx, ca, cb = (tk, tn), (lambda i, j, k: (k, j)), 1, 0
    elif mode == "nt":
        (M, K), N = a.shape[-2:], b.shape[-2]
        a_spec = pl.BlockSpec((tm, tk), lambda i, j, k: (i, k))
        b_blk, b_idx, ca, cb = (tn, tk), (lambda i, j, k: (j, k)), 1, 1
    else:
        (K, M), N = a.shape[-2:], b.shape[-1]
        a_spec = pl.BlockSpec((tk, tm), lambda i, j, k: (k, i))
        b_blk, b_idx, ca, cb = (tk, tn), (lambda i, j, k: (k, j)), 0, 0
    if dims is not None:
        M, N, K = dims
    assert M % tm == 0 and N % tn == 0 and K % tk == 0, (name, M, N, K, tm, tn, tk)
    if b_lead is None:
        b_spec = pl.BlockSpec(b_blk, b_idx)
    else:
        b_spec = pl.BlockSpec((None,) + b_blk, lambda i, j, k: (b_lead,) + b_idx(i, j, k))
    if a_over is not None:
        a_spec = a_over
    if b_over is not None:
        b_spec = b_over
    nk = K // tk
    has_add = add is not None

    def body(*refs):
        a_ref, b_ref = refs[0], refs[1]
        add_ref = refs[2] if has_add else None
        o_ref = refs[2 + has_add]
        p = _dg(a_ref[...], b_ref[...], ca, cb)

        def fin(v):
            if has_add:
                v = v + add_ref[...].astype(f32)
            o_ref[...] = v.astype(o_ref.dtype)

        if nk == 1:
            fin(p)
        else:
            acc = refs[3 + has_add]
            k = pl.program_id(2)

            @pl.when(k == 0)
            def _():
                acc[...] = p

            @pl.when(k > 0)
            def _():
                acc[...] += p

            @pl.when(k == nk - 1)
            def _():
                fin(acc[...])

    in_specs = [a_spec, b_spec]
    args = [a, b]
    if has_add:
        in_specs.append(pl.BlockSpec((tm, tn), lambda i, j, k: (i, j)))
        args.append(add)
    return pcall(body, plan, grid=(M // tm, N // tn, nk), in_specs=in_specs,
                 out_specs=pl.BlockSpec((tm, tn), lambda i, j, k: (i, j)), out_shape=S((M, N), out_dtype),
                 scratch_shapes=[pltpu.VMEM((tm, tn), f32)] if nk > 1 else [],
                 sem=("parallel", "parallel", "arbitrary"), name=name, args=args)


def _rms(xv, gv):
    return xv * lax.rsqrt(jnp.mean(xv * xv, axis=-1, keepdims=True) + EPS) * gv


TR = 256


def rms_fwd(x, g, name):
    def body(x_ref, g_ref, o_ref):
        o_ref[...] = _rms(x_ref[...], g_ref[...]).astype(o_ref.dtype)

    return pl.pallas_call(
        body, grid=(L // TR,),
        in_specs=[pl.BlockSpec((TR, D), lambda i: (i, 0)), pl.BlockSpec((1, D), lambda i: (0, 0))],
        out_specs=pl.BlockSpec((TR, D), lambda i: (i, 0)), out_shape=S((L, D), BF),
        compiler_params=_cp(("parallel",)), name=name)(x, g)


def rms_bwd(x, g, dys, dres, name, plan=None):
    nd = len(dys)

    def body(*refs):
        x_ref, g_ref = refs[0], refs[1]
        dr_ref, dh_ref, dg_ref = refs[2 + nd:]
        dy = refs[2][...].astype(f32)
        for r in refs[3:2 + nd]:
            dy = dy + r[...].astype(f32)
        _, vjp = jax.vjp(_rms, x_ref[...], g_ref[...])
        dx, dg = vjp(dy)
        dh_ref[...] = dr_ref[...] + dx

        @pl.when(pl.program_id(0) == 0)
        def _():
            dg_ref[...] = jnp.zeros_like(dg_ref)

        dg_ref[...] += dg

    row = pl.BlockSpec((TR, D), lambda i: (i, 0))
    vec = pl.BlockSpec((1, D), lambda i: (0, 0))
    return pcall(body, plan, grid=(L // TR,), in_specs=[row, vec] + [row] * (nd + 1), out_specs=[row, vec],
                 out_shape=[S((L, D), f32), S((1, D), f32)], sem=("arbitrary",), name=name, args=[x, g, *dys, dres])


def loss_head(h, g, tgt):
    def f(hv, gv, tv):
        y = _rms(hv, gv)
        return 0.5 * jnp.sum(jnp.mean(jnp.square(y - tv), axis=-1))

    def body(h_ref, g_ref, t_ref, l_ref, dh_ref, dg_ref):
        val, vjp = jax.vjp(f, h_ref[...], g_ref[...], t_ref[...])
        dh, dg, _ = vjp(jnp.ones((), f32))
        dh_ref[...] = dh

        @pl.when(pl.program_id(0) == 0)
        def _():
            dg_ref[...] = jnp.zeros_like(dg_ref)
            l_ref[...] = jnp.zeros_like(l_ref)

        dg_ref[...] += dg
        l_ref[...] += jnp.full((1, 128), val, f32)

    row = pl.BlockSpec((TR, D), lambda i: (i, 0))
    vec = pl.BlockSpec((1, D), lambda i: (0, 0))
    return pl.pallas_call(
        body, grid=(L // TR,), in_specs=[row, vec, row],
        out_specs=[pl.BlockSpec((1, 128), lambda i: (0, 0)), row, vec],
        out_shape=[S((1, 128), f32), S((L, D), f32), S((1, D), f32)],
        compiler_params=_cp(("arbitrary",)), name="loss_head")(h, g, tgt)


def _col_to_row(c):
    n = c.shape[0]
    t = jnp.broadcast_to(c, (n, 128)).T
    r = lax.broadcasted_iota(jnp.int32, (128, n), 0)
    return jnp.sum(jnp.where(r == 0, t, 0.0), axis=0, keepdims=True)


def _s5_param_map(are, aim, ldt_row, bre, bim, cre, cim):
    n = NST
    gi = lax.broadcasted_iota(jnp.int32, (n, 32), 0) // 64
    gj = lax.broadcasted_iota(jnp.int32, (n, 32), 1)
    ldt = jnp.sum(jnp.where(gi == gj, ldt_row, 0.0), axis=1, keepdims=True)
    dt = jnp.exp(ldt)
    mag = jnp.exp(are * dt)
    abr = mag * jnp.cos(aim * dt)
    abi = mag * jnp.sin(aim * dt)
    den = are * are + aim * aim
    nr, ni = abr - 1.0, abi
    cr = (nr * are + ni * aim) / den
    ci = (ni * are - nr * aim) / den
    bbr = cr * bre - ci * bim
    bbi = cr * bim + ci * bre
    tc = lax.broadcasted_iota(jnp.int32, (16, 128), 0)
    tl = lax.broadcasted_iota(jnp.int32, (16, 128), 1)
    T = (tl % 16 == tc).astype(f32)
    mr = (lax.broadcasted_iota(jnp.int32, (n, 128), 0) // 64) % 8
    mc = lax.broadcasted_iota(jnp.int32, (n, 128), 1) // 16
    mask = (mr == mc).astype(f32)

    def expand(v):
        return jnp.dot(v, T, precision=HI, preferred_element_type=f32) * mask

    return expand(bbr), expand(bbi), expand(cre), expand(cim), _col_to_row(abr), _col_to_row(abi)


def s5_params_fwd(are, aim, ldt_row, bre, bim, cre, cim):
    def body(*refs):
        outs = _s5_param_map(*[r[...] for r in refs[:7]])
        for o_ref, o in zip(refs[7:], outs):
            o_ref[...] = o

    return pl.pallas_call(
        body, out_shape=[S((NST, 128), f32)] * 4 + [S((1, NST), f32)] * 2,
        compiler_params=_cp(), name="s5_params_fwd")(are, aim, ldt_row, bre, bim, cre, cim)


def s5_params_bwd(are, aim, ldt_row, bre, bim, cre, cim, cots):
    def body(*refs):
        _, vjp = jax.vjp(_s5_param_map, *[r[...] for r in refs[:7]])
        gs = vjp(tuple(r[...] for r in refs[7:13]))
        for o_ref, o in zip(refs[13:], gs):
            o_ref[...] = o

    return pl.pallas_call(
        body, out_shape=[S((NST, 1), f32)] * 2 + [S((1, 32), f32)] + [S((NST, 16), f32)] * 4,
        compiler_params=_cp(), name="s5_params_bwd")(are, aim, ldt_row, bre, bim, cre, cim, *cots)


def _cpowers(ar, ai):
    out = [(ar, ai)]
    for _ in range(7):
        pr, pi = out[-1]
        out.append((pr * ar - pi * ai, pr * ai + pi * ar))
    return out


def _ctable(pw, rid, power):
    tr_ = jnp.zeros(rid.shape, f32)
    ti_ = jnp.zeros(rid.shape, f32)
    for r in range(8):
        pr, pi = pw[power(r) - 1]
        tr_ = jnp.where(rid == r, pr, tr_)
        ti_ = jnp.where(rid == r, pi, ti_)
    return tr_, ti_


NT5 = 4
RC = 256


def s5_scan_fwd(proj, wbr, wbi, wcr, wci, abr, abi, drow, plan=None):
    def body(u_ref, wbr_ref, wbi_ref, wcr_ref, wci_ref, ar_ref, ai_ref, d_ref, xr_ref, xi_ref, y_ref):
        wbr_v, wbi_v = wbr_ref[...], wbi_ref[...]
        for r in range(L // RC):
            rows = pl.ds(r * RC, RC)
            ub = u_ref[rows, :]
            xr_ref[rows, :] = dot_nt(ub, wbr_v)
            xi_ref[rows, :] = dot_nt(ub, wbi_v)
        pw = _cpowers(ar_ref[...], ai_ref[...])
        rid = lax.broadcasted_iota(jnp.int32, (8, 512), 0)
        tr_, ti_ = _ctable(pw, rid, lambda r: r + 1)

        def group(j, c):
            cr, ci = c
            rows = pl.ds(pl.multiple_of(j * 8, 8), 8)
            br, bi = xr_ref[rows, :], xi_ref[rows, :]
            for s in (1, 2, 4):
                pr, pi = pw[s - 1]
                sr = jnp.where(rid >= s, pltpu.roll(br, s, 0), 0.0)
                si = jnp.where(rid >= s, pltpu.roll(bi, s, 0), 0.0)
                br, bi = br + pr * sr - pi * si, bi + pr * si + pi * sr
            br, bi = br + tr_ * cr - ti_ * ci, bi + tr_ * ci + ti_ * cr
            xr_ref[rows, :] = br
            xi_ref[rows, :] = bi
            return br[7:8], bi[7:8]

        z = jnp.zeros((1, 512), f32)
        lax.fori_loop(0, L // 8, group, (z, z), unroll=2)
        wcr_v, wci_v, dv = wcr_ref[...], wci_ref[...], d_ref[...]
        for r in range(L // RC):
            rows = pl.ds(r * RC, RC)
            y_ref[rows, :] = (dot_nn(xr_ref[rows, :], wcr_v) - dot_nn(xi_ref[rows, :], wci_v)
                              + dv * u_ref[rows, :])

    wspec = pl.BlockSpec((512, 128), lambda j: (j, 0))
    aspec = pl.BlockSpec((1, 512), lambda j: (0, j))
    return pcall(
        body, plan, grid=(NT5,),
        in_specs=[pl.BlockSpec((L, 128), lambda j: (0, j)), wspec, wspec, wspec, wspec, aspec, aspec,
                  pl.BlockSpec((1, 128), lambda j: (0, j))],
        out_specs=[pl.BlockSpec((L, 512), lambda j: (0, j)), pl.BlockSpec((L, 512), lambda j: (0, j)),
                   pl.BlockSpec((L, 128), lambda j: (0, j))],
        out_shape=[S((L, NST), f32), S((L, NST), f32), S((L, S5W), f32)],
        sem=("parallel",), name="s5_scan_fwd", args=[proj, wbr, wbi, wcr, wci, abr, abi, drow])


def s5_scan_bwd(dy, proj, xs_re, xs_im, wbr, wbi, wcr, wci, abr, abi, drow, plan=None):
    def body(dy_ref, u_ref, xr_ref, xi_ref, wbr_ref, wbi_ref, wcr_ref, wci_ref, ar_ref, ai_ref, d_ref,
             du_ref, gwbr_ref, gwbi_ref, gwcr_ref, gwci_ref, gar_ref, gai_ref, gd_ref, lr_ref, li_ref):
        wcr_v, wci_v = wcr_ref[...], wci_ref[...]
        gwcr = jnp.zeros((512, 128), f32)
        gwci = jnp.zeros((512, 128), f32)
        gd = jnp.zeros((1, 128), f32)
        for r in range(L // RC):
            rows = pl.ds(r * RC, RC)
            dyv = dy_ref[rows, :]
            lr_ref[rows, :] = dot_nt(dyv, wcr_v)
            li_ref[rows, :] = -dot_nt(dyv, wci_v)
            gwcr += dot_tn(xr_ref[rows, :], dyv)
            gwci -= dot_tn(xi_ref[rows, :], dyv)
            gd += jnp.sum(dyv * u_ref[rows, :], axis=0, keepdims=True)
        gwcr_ref[...] = gwcr
        gwci_ref[...] = gwci
        gd_ref[...] = gd
        pw = _cpowers(ar_ref[...], -ai_ref[...])
        rid = lax.broadcasted_iota(jnp.int32, (8, 512), 0)
        tr_, ti_ = _ctable(pw, rid, lambda r: 8 - r)

        def group(i, c):
            cr, ci, gar, gai = c
            j = L // 8 - 1 - i
            rows = pl.ds(pl.multiple_of(j * 8, 8), 8)
            br, bi = lr_ref[rows, :], li_ref[rows, :]
            for s in (1, 2, 4):
                pr, pi = pw[s - 1]
                sr = jnp.where(rid < 8 - s, pltpu.roll(br, 8 - s, 0), 0.0)
                si = jnp.where(rid < 8 - s, pltpu.roll(bi, 8 - s, 0), 0.0)
                br, bi = br + pr * sr - pi * si, bi + pr * si + pi * sr
            br, bi = br + tr_ * cr - ti_ * ci, bi + tr_ * ci + ti_ * cr
            lr_ref[rows, :] = br
            li_ref[rows, :] = bi
            nr = jnp.where(rid < 7, pltpu.roll(br, 7, 0), cr)
            ni = jnp.where(rid < 7, pltpu.roll(bi, 7, 0), ci)
            xr, xi = xr_ref[rows, :], xi_ref[rows, :]
            return br[0:1], bi[0:1], gar + xr * nr + xi * ni, gai + xr * ni - xi * nr

        z = jnp.zeros((1, 512), f32)
        z8 = jnp.zeros((8, 512), f32)
        _, _, gar, gai = lax.fori_loop(0, L // 8, group, (z, z, z8, z8), unroll=2)
        gar_ref[...] = jnp.sum(gar, axis=0, keepdims=True)
        gai_ref[...] = jnp.sum(gai, axis=0, keepdims=True)
        wbr_v, wbi_v, dv = wbr_ref[...], wbi_ref[...], d_ref[...]
        gwbr = jnp.zeros((512, 128), f32)
        gwbi = jnp.zeros((512, 128), f32)
        for r in range(L // RC):
            rows = pl.ds(r * RC, RC)
            lrv, liv, uv = lr_ref[rows, :], li_ref[rows, :], u_ref[rows, :]
            du_ref[rows, :] = (dot_nn(lrv, wbr_v) + dot_nn(liv, wbi_v) + dv * dy_ref[rows, :]).astype(du_ref.dtype)
            gwbr += dot_tn(lrv, uv)
            gwbi += dot_tn(liv, uv)
        gwbr_ref[...] = gwbr
        gwbi_ref[...] = gwbi

    wspec = pl.BlockSpec((512, 128), lambda j: (j, 0))
    aspec = pl.BlockSpec((1, 512), lambda j: (0, j))
    col = pl.BlockSpec((L, 128), lambda j: (0, j))
    st = pl.BlockSpec((L, 512), lambda j: (0, j))
    dspec = pl.BlockSpec((1, 128), lambda j: (0, j))
    return pcall(
        body, plan, grid=(NT5,),
        in_specs=[col, col, st, st, wspec, wspec, wspec, wspec, aspec, aspec, dspec],
        out_specs=[col, wspec, wspec, wspec, wspec, aspec, aspec, dspec],
        out_shape=[S((L, S5W), BF)] + [S((NST, 128), f32)] * 4 + [S((1, NST), f32)] * 2 + [S((1, S5W), f32)],
        scratch_shapes=[pltpu.VMEM((L, 512), f32), pltpu.VMEM((L, 512), f32)],
        sem=("parallel",), name="s5_scan_bwd", args=[dy, proj, xs_re, xs_im, wbr, wbi, wcr, wci, abr, abi, drow])


def _glu(y, w, b):
    z = jax.nn.gelu(y)
    return z * jax.nn.sigmoid(dot_nn(z, w) + b)


def s5_glu_fwd(y, w, b):
    def body(y_ref, w_ref, b_ref, o_ref):
        o_ref[...] = _glu(y_ref[...], w_ref[...], b_ref[...]).astype(o_ref.dtype)

    return pl.pallas_call(
        body, grid=(L // TR,),
        in_specs=[pl.BlockSpec((TR, S5W), lambda i: (i, 0)), pl.BlockSpec((S5W, S5W), lambda i: (0, 0)),
                  pl.BlockSpec((1, S5W), lambda i: (0, 0))],
        out_specs=pl.BlockSpec((TR, S5W), lambda i: (i, 0)), out_shape=S((L, S5W), BF),
        compiler_params=_cp(("parallel",)), name="s5_glu_fwd")(y, w, b)


def s5_glu_bwd(y, w, b, dmix):
    def body(y_ref, w_ref, b_ref, g_ref, dy_ref, dw_ref, db_ref):
        _, vjp = jax.vjp(_glu, y_ref[...], w_ref[...].astype(f32), b_ref[...])
        dy, dw, db = vjp(g_ref[...])
        dy_ref[...] = dy

        @pl.when(pl.program_id(0) == 0)
        def _():
            dw_ref[...] = jnp.zeros_like(dw_ref)
            db_ref[...] = jnp.zeros_like(db_ref)

        dw_ref[...] += dw
        db_ref[...] += db

    row = pl.BlockSpec((TR, S5W), lambda i: (i, 0))
    return pl.pallas_call(
        body, grid=(L // TR,),
        in_specs=[row, pl.BlockSpec((S5W, S5W), lambda i: (0, 0)), pl.BlockSpec((1, S5W), lambda i: (0, 0)), row],
        out_specs=[row, pl.BlockSpec((S5W, S5W), lambda i: (0, 0)), pl.BlockSpec((1, S5W), lambda i: (0, 0))],
        out_shape=[S((L, S5W), f32), S((S5W, S5W), f32), S((1, S5W), f32)],
        compiler_params=_cp(("arbitrary",)), name="s5_glu_bwd")(y, w, b, dmix)


def _dg3(a, b, ca, cb):
    ah, bh = a.astype(BF), b.astype(BF)
    al, bl = (a - ah.astype(f32)).astype(BF), (b - bh.astype(f32)).astype(BF)
    return _dg(ah, bh, ca, cb) + _dg(ah, bl, ca, cb) + _dg(al, bh, ca, cb)


@jax.custom_vjp
def hi_nn(a, b):
    return _dg3(a, b, 1, 0)


@jax.custom_vjp
def hi_nt(a, b):
    return _dg3(a, b, 1, 1)


@jax.custom_vjp
def hi_tn(a, b):
    return _dg3(a, b, 0, 0)


hi_nn.defvjp(lambda a, b: (hi_nn(a, b), (a, b)), lambda r, g: (hi_nt(g, r[1]), hi_tn(r[0], g)))
hi_nt.defvjp(lambda a, b: (hi_nt(a, b), (a, b)), lambda r, g: (hi_nn(g, r[1]), hi_tn(g, r[0])))
hi_tn.defvjp(lambda a, b: (hi_tn(a, b), (a, b)), lambda r, g: (hi_nt(r[1], g), hi_nn(r[0], g)))


def _hgrn_chunk(St, xq, xf, xi, xg, gam, ng):
    lb = jax.nn.sigmoid(gam[0:1] - gam[1:2])
    q = jax.nn.silu(xq)
    f = lb + (1.0 - lb) * jax.nn.sigmoid(xf)
    k = 1.0 - f
    g = jnp.log(f)
    ti = lax.broadcasted_iota(jnp.int32, (HGC, HGC), 0)
    si = lax.broadcasted_iota(jnp.int32, (HGC, HGC), 1)
    causal = si <= ti
    b = jnp.dot(causal.astype(f32), g, precision=HI, preferred_element_type=f32)
    qe = q * jnp.exp(b)
    o = hi_nt(qe, St)
    att = jnp.where(causal, hi_nt(qe, k * jnp.exp(-b)), 0.0)
    o = o + hi_nn(att, xi)
    bl = b[HGC - 1:HGC]
    St_new = St * jnp.exp(bl) + hi_tn(xi, k * jnp.exp(bl - b))
    o = o * lax.rsqrt(jnp.mean(o * o, axis=-1, keepdims=True) + EPS) * ng
    return St_new, o * jax.nn.silu(xg)


NCH = L // HGC


def hgrn_fwd(proj, gamma, hnorm, plan=None):
    def body(q_ref, f_ref, i_ref, g_ref, gam_ref, ng_ref, o_ref, ss_ref, st):
        @pl.when(pl.program_id(0) == 0)
        def _():
            st[...] = jnp.zeros_like(st)

        for h in range(4):
            sl = slice(h * 128, (h + 1) * 128)
            s0 = st[h]
            ss_ref[0, h] = s0
            s1, o = _hgrn_chunk(s0, q_ref[:, sl], f_ref[:, sl], i_ref[:, sl], g_ref[:, sl], gam_ref[:, sl], ng_ref[:, sl])
            st[h] = s1
            o_ref[:, sl] = o.astype(o_ref.dtype)

    def pj(n):
        return pl.BlockSpec((HGC, 512), lambda c: (c, n))

    return pcall(
        body, plan, grid=(NCH,),
        in_specs=[pj(1), pj(2), pj(3), pj(4), pl.BlockSpec((2, 512), lambda c: (0, 0)), pl.BlockSpec((1, 512), lambda c: (0, 0))],
        out_specs=[pl.BlockSpec((HGC, 512), lambda c: (c, 0)), pl.BlockSpec((1, 4, 128, 128), lambda c: (c, 0, 0, 0))],
        out_shape=[S((L, 512), BF), S((NCH, 4, 128, 128), f32)],
        scratch_shapes=[pltpu.VMEM((4, 128, 128), f32)],
        sem=("arbitrary",), name="hgrn_fwd", args=[proj, proj, proj, proj, gamma, hnorm])


def hgrn_bwd(proj, gamma, hnorm, ssave, dmix, du, plan=None):
    def body(q_ref, f_ref, i_ref, g_ref, gam_ref, ng_ref, ss_ref, do_ref, du_ref, dp_ref, dgam_ref, dng_ref, dst):
        @pl.when(pl.program_id(0) == 0)
        def _():
            dst[...] = jnp.zeros_like(dst)
            dgam_ref[...] = jnp.zeros_like(dgam_ref)
            dng_ref[...] = jnp.zeros_like(dng_ref)

        dp_ref[:, 0:512] = du_ref[...]
        for h in range(4):
            sl = slice(h * 128, (h + 1) * 128)
            _, vjp = jax.vjp(_hgrn_chunk, ss_ref[0, h], q_ref[:, sl], f_ref[:, sl], i_ref[:, sl], g_ref[:, sl],
                             gam_ref[:, sl], ng_ref[:, sl])
            ds, dq, df, di, dg, dgam, dng = vjp((dst[h], do_ref[:, sl]))
            dst[h] = ds
            for n, v in enumerate((dq, df, di, dg)):
                dp_ref[:, 512 * (n + 1) + h * 128: 512 * (n + 1) + (h + 1) * 128] = v.astype(dp_ref.dtype)
            dgam_ref[:, sl] += dgam
            dng_ref[:, sl] += dng

    def pj(n):
        return pl.BlockSpec((HGC, 512), lambda i: (NCH - 1 - i, n))

    return pcall(
        body, plan, grid=(NCH,),
        in_specs=[pj(1), pj(2), pj(3), pj(4), pl.BlockSpec((2, 512), lambda i: (0, 0)), pl.BlockSpec((1, 512), lambda i: (0, 0)),
                  pl.BlockSpec((1, 4, 128, 128), lambda i: (NCH - 1 - i, 0, 0, 0)), pj(1), pj(0)],
        out_specs=[pl.BlockSpec((HGC, 2560), lambda i: (NCH - 1 - i, 0)), pl.BlockSpec((2, 512), lambda i: (0, 0)),
                   pl.BlockSpec((1, 512), lambda i: (0, 0))],
        out_shape=[S((L, 2560), BF), S((2, 512), f32), S((1, 512), f32)],
        scratch_shapes=[pltpu.VMEM((4, 128, 128), f32)],
        sem=("arbitrary",), name="hgrn_bwd", args=[proj, proj, proj, proj, gamma, hnorm, ssave, dmix, du])


def _shift(x, k):
    return jnp.concatenate([jnp.zeros((k, x.shape[1]), x.dtype), x[:-k]], axis=0)


def _convact(ha, hb, wa, wb, ba, bb):
    ca = wa[2:3] * ha + wa[1:2] * _shift(ha, 1) + wa[0:1] * _shift(ha, 2) + ba
    cb = wb[2:3] * hb + wb[1:2] * _shift(hb, 1) + wb[0:1] * _shift(hb, 2) + bb
    return jax.nn.silu(ca) * cb


CT = 128
NCT = DFF // CT


def convact_fwd(hu, cw, cb, layer, plan=None):
    def body(ha_ref, hb_ref, wa_ref, wb_ref, ba_ref, bb_ref, o_ref):
        o_ref[...] = _convact(ha_ref[...], hb_ref[...], wa_ref[...], wb_ref[...], ba_ref[...], bb_ref[...]).astype(o_ref.dtype)

    def h(off):
        return pl.BlockSpec((L, CT), lambda j: (0, j + off))

    def w(off):
        return pl.BlockSpec((3, CT), lambda j: (0, j + off))

    def b(off):
        return pl.BlockSpec((None, 1, CT), lambda j: (layer, 0, j + off))

    return pcall(body, plan, grid=(NCT,), in_specs=[h(0), h(NCT), w(0), w(NCT), b(0), b(NCT)],
                 out_specs=pl.BlockSpec((L, CT), lambda j: (0, j)), out_shape=S((L, DFF), BF),
                 sem=("parallel",), name=f"convact_fwd{layer}", args=[hu, hu, cw, cw, cb, cb])


def convact_bwd(hu, cw, cb, dact, layer, plan=None):
    def body(ha_ref, hb_ref, wa_ref, wb_ref, ba_ref, bb_ref, g_ref, dh_ref, dw_ref, db_ref, sh, sw, sb):
        j = pl.program_id(0)

        @pl.when(j < NCT)
        def _():
            _, vjp = jax.vjp(_convact, ha_ref[...], hb_ref[...], wa_ref[...], wb_ref[...], ba_ref[...], bb_ref[...])
            dha, dhb, dwa, dwb, dba, dbb = vjp(g_ref[...].astype(f32))
            dh_ref[...] = dha.astype(dh_ref.dtype)
            dw_ref[...] = dwa
            db_ref[...] = dba
            sh[j] = dhb.astype(sh.dtype)
            sw[j] = dwb
            sb[j] = dbb

        @pl.when(j >= NCT)
        def _():
            dh_ref[...] = sh[j - NCT]
            dw_ref[...] = sw[j - NCT]
            db_ref[...] = sb[j - NCT]

    def lo(j):
        return jnp.minimum(j, NCT - 1)

    in_specs = [pl.BlockSpec((L, CT), lambda j: (0, lo(j))), pl.BlockSpec((L, CT), lambda j: (0, lo(j) + NCT)),
                pl.BlockSpec((3, CT), lambda j: (0, lo(j))), pl.BlockSpec((3, CT), lambda j: (0, lo(j) + NCT)),
                pl.BlockSpec((None, 1, CT), lambda j: (layer, 0, lo(j))), pl.BlockSpec((None, 1, CT), lambda j: (layer, 0, lo(j) + NCT)),
                pl.BlockSpec((L, CT), lambda j: (0, lo(j)))]
    return pcall(
        body, plan, grid=(2 * NCT,), in_specs=in_specs,
        out_specs=[pl.BlockSpec((L, CT), lambda j: (0, j)), pl.BlockSpec((3, CT), lambda j: (0, j)), pl.BlockSpec((1, CT), lambda j: (0, j))],
        out_shape=[S((L, 2 * DFF), BF), S((3, 2 * DFF), f32), S((1, 2 * DFF), f32)],
        scratch_shapes=[pltpu.VMEM((NCT, L, CT), BF), pltpu.VMEM((NCT, 3, CT), f32), pltpu.VMEM((NCT, 1, CT), f32)],
        sem=("arbitrary",), name=f"convact_bwd{layer}", args=[hu, hu, cw, cw, cb, cb, dact])


DILS = (1, 4, 16)
AB = 128
NPAIR = 12


def _rope_tables(pos_ref, invf_ref):
    ang = pos_ref[...].astype(f32) * invf_ref[...]
    lane = lax.broadcasted_iota(jnp.int32, (1, 128), 1) % 64
    cosf = jnp.where(lane < 16, jnp.cos(ang), 1.0)
    sn = jnp.sin(ang)
    s_lo = jnp.where(lane < 8, -sn, 0.0)
    s_hi = jnp.where((lane >= 8) & (lane < 16), sn, 0.0)
    return cosf, s_lo, s_hi


def _rope(t, cosf, s_lo, s_hi):
    return t * cosf + pltpu.roll(t, 120, 1) * s_lo + pltpu.roll(t, 8, 1) * s_hi


def _rope_t(g, cosf, s_lo, s_hi):
    return g * cosf + pltpu.roll(g * s_lo, 8, 1) + pltpu.roll(g * s_hi, 120, 1)


def _att_block(q2, kp, kc, vp, vc, first):
    lane = lax.broadcasted_iota(jnp.int32, (1, 128), 1)
    qi = lax.broadcasted_iota(jnp.int32, (AB, 2 * AB), 0) + AB
    kj = lax.broadcasted_iota(jnp.int32, (AB, 2 * AB), 1)
    back = qi - kj
    valid = (back >= 0) & (back <= AB)
    if first:
        valid = valid & (kj >= AB)
    kk = jnp.concatenate([kp, kc], axis=0)
    vv = jnp.concatenate([vp, vc], axis=0)
    o2 = jnp.zeros((AB, 128), f32)
    lse2 = jnp.zeros((AB, 128), f32)
    for e in range(2):
        hm = ((lane >= 64 * e) & (lane < 64 * (e + 1))).astype(f32)
        s = dot_nt(q2 * (hm * 0.125), kk)
        s = jnp.where(valid, s, -jnp.inf)
        m = jnp.max(s, axis=-1, keepdims=True)
        p = jnp.exp(s - m)
        den = jnp.sum(p, axis=-1, keepdims=True)
        o2 = o2 + dot_nn(p, vv * hm) / den
        lse2 = lse2 + (m + jnp.log(den)) * hm
    return o2, lse2


def _att_blocks(dil):
    m = L // dil
    return [(r * m + n * AB, n == 0) for r in range(dil) for n in range(m // AB)]


def deinterleave(x, dil):
    return x if dil == 1 else x.reshape(L // dil, dil, x.shape[1]).swapaxes(0, 1).reshape(L, x.shape[1])


def interleave(x, dil):
    return x if dil == 1 else x.reshape(dil, L // dil, x.shape[1]).swapaxes(0, 1).reshape(L, x.shape[1])


def attn_fwd(qkv, pos, invf, g, plan=None):
    blocks = _att_blocks(DILS[g])

    def body(q_ref, k_ref, v_ref, pos_ref, invf_ref, o_ref, l_ref, qr, kr):
        cosf, s_lo, s_hi = _rope_tables(pos_ref, invf_ref)
        qr[...] = _rope(q_ref[...], cosf, s_lo, s_hi)
        kr[...] = _rope(k_ref[...], cosf, s_lo, s_hi)
        for off, first in blocks:
            cur, prv = pl.ds(off, AB), pl.ds(off if first else off - AB, AB)
            o2, lse2 = _att_block(qr[cur, :], kr[prv, :], kr[cur, :], v_ref[prv, :], v_ref[cur, :], first)
            o_ref[cur, :] = o2
            l_ref[cur, :] = lse2

    def sec(n):
        return pl.BlockSpec((L, 128), lambda p: (0, p + 4 * n))

    return pcall(
        body, plan, grid=(4,),
        in_specs=[sec(0), sec(1), sec(2), pl.BlockSpec((L, 1), lambda p: (0, 0)), pl.BlockSpec((1, 128), lambda p: (0, 0))],
        out_specs=[sec(0), sec(0)], out_shape=[S((L, 512), f32), S((L, 512), f32)],
        scratch_shapes=[pltpu.VMEM((L, 128), f32), pltpu.VMEM((L, 128), f32)],
        sem=("parallel",), name=f"attn_fwd{g}", args=[qkv, qkv, qkv, pos, invf])


def _att_block_bwd(q2, kp, kc, vp, vc, lse2, do2, dl2, first):
    lane = lax.broadcasted_iota(jnp.int32, (1, 128), 1)
    qi = lax.broadcasted_iota(jnp.int32, (AB, 2 * AB), 0) + AB
    kj = lax.broadcasted_iota(jnp.int32, (AB, 2 * AB), 1)
    back = qi - kj
    valid = (back >= 0) & (back <= AB)
    if first:
        valid = valid & (kj >= AB)
    kk = jnp.concatenate([kp, kc], axis=0)
    vv = jnp.concatenate([vp, vc], axis=0)
    dq2 = jnp.zeros((AB, 128), f32)
    dkk = jnp.zeros((2 * AB, 128), f32)
    dvv = jnp.zeros((2 * AB, 128), f32)
    for e in range(2):
        hb = (lane >= 64 * e) & (lane < 64 * (e + 1))
        hm = hb.astype(f32)
        qs = q2 * (hm * 0.125)
        lse = jnp.max(jnp.where(hb, lse2, -jnp.inf), axis=-1, keepdims=True)
        dls = jnp.sum(dl2 * hm, axis=-1, keepdims=True)
        p = jnp.where(valid, jnp.exp(dot_nt(qs, kk) - lse), 0.0)
        dov = do2 * hm
        dp = dot_nt(dov, vv)
        ds = p * (dp - jnp.sum(p * dp, axis=-1, keepdims=True) + dls)
        dq2 = dq2 + dot_nn(ds, kk) * (hm * 0.125)
        dkk = dkk + dot_tn(ds, qs)
        dvv = dvv + dot_tn(p, dov)
    return dq2, dkk[:AB], dkk[AB:], dvv[:AB], dvv[AB:]


def attn_bwd(qkv, pos, invf, lse, do, dl, g, plan=None):
    blocks = _att_blocks(DILS[g])

    def body(q_ref, k_ref, v_ref, pos_ref, invf_ref, l_ref, do_ref, dl_ref, d_ref, qr, kr, dqr, dkr, dvr):
        cosf, s_lo, s_hi = _rope_tables(pos_ref, invf_ref)
        qr[...] = _rope(q_ref[...], cosf, s_lo, s_hi)
        kr[...] = _rope(k_ref[...], cosf, s_lo, s_hi)
        for off, first in blocks:
            cur, prv = pl.ds(off, AB), pl.ds(off if first else off - AB, AB)
            dq2, dkp, dkc, dvp, dvc = _att_block_bwd(qr[cur, :], kr[prv, :], kr[cur, :], v_ref[prv, :], v_ref[cur, :],
                                                     l_ref[cur, :], do_ref[cur, :], dl_ref[cur, :], first)
            dqr[cur, :] = dq2
            dkr[cur, :] = dkc
            dvr[cur, :] = dvc
            if not first:
                dkr[prv, :] += dkp
                dvr[prv, :] += dvp
        d_ref[0] = _rope_t(dqr[...], cosf, s_lo, s_hi).astype(d_ref.dtype)
        d_ref[1] = _rope_t(dkr[...], cosf, s_lo, s_hi).astype(d_ref.dtype)
        d_ref[2] = dvr[...].astype(d_ref.dtype)

    def sec(n):
        return pl.BlockSpec((L, 128), lambda p: (0, p + 4 * n))

    return pcall(
        body, plan, grid=(4,),
        in_specs=[sec(0), sec(1), sec(2), pl.BlockSpec((L, 1), lambda p: (0, 0)), pl.BlockSpec((1, 128), lambda p: (0, 0)),
                  sec(0), sec(0), sec(0)],
        out_specs=pl.BlockSpec((3, L, 128), lambda p: (0, 0, p)), out_shape=S((3, L, 512), BF),
        scratch_shapes=[pltpu.VMEM((L, 128), f32)] * 5,
        sem=("parallel",), name=f"attn_bwd{g}", args=[qkv, qkv, qkv, pos, invf, lse, do, dl])


def _merge(o0, o1, o2, l0, l1, l2):
    m = jnp.maximum(jnp.maximum(l0, l1), l2)
    e0, e1, e2 = jnp.exp(l0 - m), jnp.exp(l1 - m), jnp.exp(l2 - m)
    return (e0 * o0 + e1 * o1 + e2 * o2) / (e0 + e1 + e2)


def attn_merge_fwd(os_, ls_):
    def body(o0, o1, o2, l0, l1, l2, o_ref):
        o_ref[...] = _merge(o0[...], o1[...], o2[...], l0[...], l1[...], l2[...]).astype(o_ref.dtype)

    blk = pl.BlockSpec((TR, 512), lambda i: (i, 0))
    return pl.pallas_call(
        body, grid=(L // TR,), in_specs=[blk] * 6, out_specs=blk, out_shape=S((L, 512), BF),
        compiler_params=_cp(("parallel",)), name="attn_merge_fwd")(*os_, *ls_)


def attn_merge_bwd(os_, ls_, do, plan=None):
    def body(o0, o1, o2, l0, l1, l2, g_ref, *outs):
        _, vjp = jax.vjp(_merge, o0[...], o1[...], o2[...], l0[...], l1[...], l2[...])
        for o_ref, v in zip(outs, vjp(g_ref[...])):
            o_ref[...] = v.astype(o_ref.dtype)

    blk = pl.BlockSpec((TR, 512), lambda i: (i, 0))
    outs = pcall(body, plan, grid=(L // TR,), in_specs=[blk] * 7, out_specs=[blk] * 6,
                 out_shape=[S((L, 512), BF)] * 3 + [S((L, 512), f32)] * 3,
                 sem=("parallel",), name="attn_merge_bwd", args=[*os_, *ls_, do])
    return outs[:3], outs[3:]


def _invf_lanes():
    half = 8
    inv = ROPE_THETA ** (-np.arange(half, dtype=np.float32) * 2.0 / 16.0)
    lane = np.arange(128) % 64
    return jnp.asarray(np.where(lane < 16, inv[lane % 8], 0.0).astype(np.float32)[None, :])


def hosted(C, host, fn):
    p = C.plan(host) if C is not None else None
    out = fn(p)
    if p is not None:
        C.done(p)
    return out


def _ffn_fwd(h, g_row, W, cb, layer, C):
    hn = rms_fwd(h, g_row, f"rms_ffn{layer}")
    hu = hosted(C, f"ffn_in{layer}", lambda p: matmul(hn, W[("ffn_w_in", layer)], mode="nn", tm=1024, tn=1408, tk=1024,
                                                      plan=p, name=f"ffn_in{layer}"))
    act = hosted(C, f"convact_fwd{layer}", lambda p: convact_fwd(hu, W[("ffn_conv_w", layer)], cb, layer, plan=p))
    h2 = matmul(act, W[("ffn_w_out", layer)], mode="nn", tm=1024, tn=1024, tk=2816, add=h, name=f"ffn_out{layer}")
    return h2, (hn, hu, act)


def _ffn_bwd(dh, h, g_row, W, cb, saved, layer, C, G):
    hn, hu, act = saved
    w_in, w_out = W[("ffn_w_in", layer)], W[("ffn_w_out", layer)]
    dact = matmul(dh, w_out, mode="nt", tm=1024, tn=1408, tk=1024, name=f"ffn_out_dx{layer}")
    G[("ffn_w_out", layer)] = matmul(act, dh, mode="tn", tm=1408, tn=1024, tk=L, out_dtype=BF, name=f"ffn_out_dw{layer}")
    dhu, G[("ffn_conv_w", layer)], g_cb = hosted(
        C, f"convact_bwd{layer}", lambda p: convact_bwd(hu, W[("ffn_conv_w", layer)], cb, dact, layer, plan=p))
    dhn = hosted(C, f"ffn_in_dx{layer}", lambda p: matmul(dhu, w_in, mode="nt", tm=1024, tn=1024, tk=2816, plan=p,
                                                         name=f"ffn_in_dx{layer}"))
    G[("ffn_w_in", layer)] = hosted(C, f"ffn_in_dw{layer}", lambda p: matmul(
        hn, dhu, mode="tn", tm=1024, tn=1408, tk=L, out_dtype=BF, plan=p, name=f"ffn_in_dw{layer}"))
    dh2, g_norm = rms_bwd(h, g_row, [dhn], dh, f"rms_ffn_bwd{layer}")
    return dh2, g_cb, g_norm


def local_step(x, pos, tgt, sm, W, C=None):
    G = C.grads if C is not None else {}
    nm, nf = sm["norm_mix"], sm["norm_ffn"]
    invf = _invf_lanes()
    are = sm["s5_A_re"].reshape(NST, 1)
    aim = sm["s5_A_im"].reshape(NST, 1)
    ldt = sm["s5_log_dt"].reshape(1, 32)
    bre = sm["s5_B_re"].reshape(NST, 16)
    bim = sm["s5_B_im"].reshape(NST, 16)
    cre = jnp.swapaxes(sm["s5_C_re"][0], 1, 2).reshape(NST, 16)
    cim = jnp.swapaxes(sm["s5_C_im"][0], 1, 2).reshape(NST, 16)
    drow = sm["s5_D"].reshape(1, S5W)
    wbr, wbi, wcr, wci, abr, abi = s5_params_fwd(are, aim, ldt, bre, bim, cre, cim)
    hn0 = rms_fwd(x, nm[0:1], "rms_mix0")
    cb3 = sm["ffn_conv_b3"]
    proj = hosted(C, "mix_in", lambda p: matmul(hn0, W[("mix_w_in", 0)], mode="nn", tm=1024, tn=1280, tk=1024, plan=p, name="mix_in"))
    xs_re, xs_im, y5 = hosted(C, "s5_scan_fwd", lambda p: s5_scan_fwd(proj, wbr, wbi, wcr, wci, abr, abi, drow, plan=p))
    oa = s5_glu_fwd(y5, W[("s5_glu_w", 0)], sm["s5_glu_b"])
    ob, ssave = hosted(C, "hgrn_fwd", lambda p: hgrn_fwd(proj, sm["hgrn_gamma"], sm["hgrn_norm"], plan=p))
    cat = jnp.concatenate([oa, ob], axis=1)
    h1 = matmul(cat, W[("mix_w_out", 0)], mode="nn", tm=1024, tn=1024, tk=1024, add=x, name="mix_out")
    h2, ffn0 = _ffn_fwd(h1, nf[0:1], W, cb3, 0, C)
    hn2 = rms_fwd(h2, nm[1:2], "rms_mix1")
    wqkv = W[("att_w_qkv", 0)]
    hn2_g, pos_g, qkv_g, o_g, l_g, lc_g = [], [], [], [], [], []
    for g, dil in enumerate(DILS):
        hn2_g.append(deinterleave(hn2, dil))
        pos_g.append(deinterleave(pos, dil))
        qkv_g.append(hosted(C, f"att_qkv{g}", lambda p: matmul(
            hn2_g[g], wqkv, mode="nn", tm=1024, tn=512, tk=1024, dims=(L, 1536, D),
            b_spec=pl.BlockSpec((D, 512), lambda i, j, k, g=g: (0, 3 * j + g)), plan=p, name=f"att_qkv{g}")))
        o_c, l_c = hosted(C, f"attn_fwd{g}", lambda p: attn_fwd(qkv_g[g], pos_g[g], invf, g, plan=p))
        lc_g.append(l_c)
        o_g.append(interleave(o_c, dil))
        l_g.append(interleave(l_c, dil))
    o = attn_merge_fwd(o_g, l_g)
    h3 = matmul(o, W[("att_w_o", 0)], mode="nn", tm=1024, tn=1024, tk=512, add=h2, name="att_o")
    h4, ffn1 = _ffn_fwd(h3, nf[1:2], W, cb3, 1, C)
    loss, dh, g_nfinal = loss_head(h4, sm["norm_final"].reshape(1, D), tgt)
    dh, g_cb1, g_nf1 = _ffn_bwd(dh, h3, nf[1:2], W, cb3, ffn1, 1, C, G)
    do = matmul(dh, W[("att_w_o", 0)], mode="nt", tm=1024, tn=512, tk=1024, name="att_o_dx")
    G[("att_w_o", 0)] = matmul(o, dh, mode="tn", tm=512, tn=1024, tk=L, out_dtype=BF, name="att_o_dw")
    do_g, dl_g = hosted(C, "attn_merge_bwd", lambda p: attn_merge_bwd(o_g, l_g, do, plan=p))
    dhn2_g, gq = [], []
    for g, dil in enumerate(DILS):
        d3 = hosted(C, f"attn_bwd{g}", lambda p: attn_bwd(qkv_g[g], pos_g[g], invf, lc_g[g], deinterleave(do_g[g], dil),
                                                        deinterleave(dl_g[g], dil), g, plan=p))
        dx = matmul(d3, wqkv, mode="nt", tm=1024, tn=1024, tk=512, dims=(L, D, 1536),
                    a_spec=pl.BlockSpec((None, 1024, 512), lambda i, j, k: (k, i, 0)),
                    b_spec=pl.BlockSpec((D, 512), lambda i, j, k, g=g: (0, 3 * k + g)), name=f"att_qkv_dx{g}")
        dhn2_g.append(interleave(dx, dil))
        gq.append(matmul(hn2_g[g], d3, mode="tn", tm=1024, tn=512, tk=L, out_dtype=BF, dims=(D, 1536, L),
                         b_spec=pl.BlockSpec((None, L, 512), lambda i, j, k: (j, k, 0)), name=f"att_qkv_dw{g}"))
    G[("att_w_qkv", 0)] = jnp.concatenate([gq[g][:, 512 * s:512 * (s + 1)] for s in range(3) for g in range(3)], axis=1)
    dh, g_nm1 = rms_bwd(h2, nm[1:2], dhn2_g, dh, "rms_mix_bwd1")
    dh, g_cb0, g_nf0 = _ffn_bwd(dh, h1, nf[0:1], W, cb3, ffn0, 0, C, G)
    dmix = matmul(dh, W[("mix_w_out", 0)], mode="nt", tm=1024, tn=1024, tk=1024, name="mix_out_dx")
    G[("mix_w_out", 0)] = matmul(cat, dh, mode="tn", tm=1024, tn=1024, tk=L, out_dtype=BF, name="mix_out_dw")
    dy5, g_glu_w, g_glu_b = s5_glu_bwd(y5, W[("s5_glu_w", 0)], sm["s5_glu_b"], dmix)
    G[("s5_glu_w", 0)] = g_glu_w.astype(BF)
    du, gwbr, gwbi, gwcr, gwci, gabr, gabi, g_d = hosted(C, "s5_scan_bwd", lambda p: s5_scan_bwd(
        dy5, proj, xs_re, xs_im, wbr, wbi, wcr, wci, abr, abi, drow, plan=p))
    g_are, g_aim, g_ldt, g_bre, g_bim, g_cre, g_cim = s5_params_bwd(are, aim, ldt, bre, bim, cre, cim,
                                                                   (gwbr, gwbi, gwcr, gwci, gabr, gabi))
    dproj, g_gamma, g_hnorm = hosted(C, "hgrn_bwd", lambda p: hgrn_bwd(proj, sm["hgrn_gamma"], sm["hgrn_norm"], ssave, dmix, du,
                                                                       plan=p))
    dhn0 = hosted(C, "mix_in_dx", lambda p: matmul(dproj, W[("mix_w_in", 0)], mode="nt", tm=1024, tn=1024, tk=2560, plan=p,
                                                  name="mix_in_dx"))
    G[("mix_w_in", 0)] = matmul(hn0, dproj, mode="tn", tm=1024, tn=1280, tk=L, out_dtype=BF, name="mix_in_dw")
    gx, g_nm0 = hosted(C, "rms_mix_bwd0", lambda p: rms_bwd(x, nm[0:1], [dhn0], dh, "rms_mix_bwd0", plan=p))
    small = {
        "norm_mix": jnp.concatenate([g_nm0, g_nm1], axis=0), "norm_ffn": jnp.concatenate([g_nf0, g_nf1], axis=0),
        "norm_final": g_nfinal.reshape(D),
        "s5_A_re": g_are.reshape(1, 32, 64), "s5_A_im": g_aim.reshape(1, 32, 64), "s5_log_dt": g_ldt.reshape(1, 32),
        "s5_B_re": g_bre.reshape(1, 32, 64, 16), "s5_B_im": g_bim.reshape(1, 32, 64, 16),
        "s5_C_re": jnp.swapaxes(g_cre.reshape(1, 32, 64, 16), 2, 3), "s5_C_im": jnp.swapaxes(g_cim.reshape(1, 32, 64, 16), 2, 3),
        "s5_D": g_d.reshape(1, 32, 16), "s5_glu_b": g_glu_b, "hgrn_gamma": g_gamma, "hgrn_norm": g_hnorm,
        "ffn_conv_b": jnp.concatenate([g_cb0, g_cb1], axis=0),
    }
    return loss, gx, G, small


BIG = ("mix_w_in", "mix_w_out", "s5_glu_w", "att_w_qkv", "att_w_o", "ffn_w_in", "ffn_w_out", "ffn_conv_w")
SMALL = ("norm_mix", "norm_ffn", "norm_final", "s5_A_re", "s5_A_im", "s5_log_dt", "s5_B_re", "s5_B_im", "s5_C_re", "s5_C_im",
         "s5_D", "s5_glu_b", "hgrn_gamma", "hgrn_norm", "ffn_conv_b")


def cast_bf16(w, name, plan=None):
    nl, r, c = w.shape
    w2 = w.reshape(nl * r, c)
    tr = 256 if (nl * r) % 256 == 0 else nl * r

    def body(w_ref, o_ref):
        o_ref[...] = w_ref[...].astype(BF)

    out = pcall(body, plan, grid=(nl * r // tr,), in_specs=[pl.BlockSpec((tr, c), lambda i: (i, 0))],
                out_specs=pl.BlockSpec((tr, c), lambda i: (i, 0)), out_shape=S((nl * r, c), BF),
                sem=("parallel",), name=name, args=[w2])
    return out.reshape(nl, r, c)


SCHEDULE = {
    "cast_ffn_w_in": [("G", "mix_w_in", 0)],
    "mix_in": [("G", "mix_w_out", 0), ("G", "s5_glu_w", 0)],
    "s5_scan_fwd": [("G", "ffn_w_in", 0, (0, 2))],
    "hgrn_fwd": [("G", "ffn_w_in", 0, (1, 2)), ("G", "ffn_conv_w", 0), ("G", "ffn_conv_w", 1), ("G", "att_w_qkv", 0, (0, 2))],
    "ffn_in0": [("G", "ffn_w_out", 0)],
    "convact_fwd0": [("G", "att_w_qkv", 0, (1, 2))],
    "att_qkv0": [("G", "att_w_o", 0)],
    "attn_fwd0": [("G", "ffn_w_in", 1, (0, 2))],
    "attn_fwd1": [("G", "ffn_w_in", 1, (1, 2))],
    "attn_fwd2": [("G", "ffn_w_out", 1)],
    "convact_bwd1": [("A", "ffn_w_out", 1)],
    "ffn_in_dx1": [("B", "ffn_w_out", 1)],
    "attn_merge_bwd": [("A", "att_w_o", 0), ("A", "ffn_conv_w", 1)],
    "attn_bwd0": [("A", "ffn_w_in", 1, (0, 2))],
    "attn_bwd1": [("A", "ffn_w_in", 1, (1, 2)), ("B", "att_w_o", 0), ("B", "ffn_conv_w", 1)],
    "attn_bwd2": [("B", "ffn_w_in", 1)],
    "convact_bwd0": [("A", "att_w_qkv", 0, (0, 2))],
    "ffn_in_dw0": [("A", "att_w_qkv", 0, (1, 2))],
    "s5_scan_bwd": [("A", "ffn_w_out", 0), ("A", "mix_w_out", 0), ("A", "s5_glu_w", 0), ("A", "ffn_conv_w", 0),
                    ("B", "att_w_qkv", 0)],
    "hgrn_bwd": [("A", "ffn_w_in", 0), ("B", "ffn_w_out", 0), ("B", "mix_w_out", 0), ("B", "s5_glu_w", 0), ("B", "ffn_conv_w", 0)],
    "mix_in_dx": [("B", "ffn_w_in", 0)],
    "adam_ffn_w_in": [("A", "mix_w_in", 0, (0, 2)), ("A", "small", 0)],
    "adam_ffn_w_out": [("A", "mix_w_in", 0, (1, 2))],
    "adam_att_w_qkv": [("B", "mix_w_in", 0), ("B", "small", 0)],
}


class Comm:
    def __init__(self, shards, shapes):
        self.shards, self.shapes = shards, shapes
        self.W, self.grads, self.slots = {}, {}, {}
        self.small = None

    def plan(self, host):
        items = SCHEDULE.get(host)
        if not items:
            return None
        p = Plan()
        for it in items:
            kind, name, l = it[:3]
            part, parts = it[3] if len(it) > 3 else (0, 1)
            if name == "small":
                kdst = p.buf("slots:small", arr=self.slots.get("small"), shape=S((8,) + self.small.shape, f32), write=True)
                if kind == "A":
                    ReduceOp(p, p.buf("g:small", arr=self.small), kdst, None, self.small.shape, False, 0, 0, whole=True)
                else:
                    ForwardOp(p, kdst, None, whole=True)
                continue
            nl, R, C_ = self.shapes[name]
            rows = name in ROW_SHARDED
            r0, nr = part * (R // parts), R // parts
            if kind == "G":
                sh = self.shards[name]
                kdst = p.buf(f"W:{name}:{l}", arr=self.W.get((name, l)), shape=S((4 * R, C_) if rows else (R, 4 * C_), sh.dtype),
                             write=True)
                GatherOp(p, p.buf("shard:" + name, arr=sh), kdst, l, self.shapes[name], rows, r0, nr, split=(nr % 32 == 0))
            else:
                g = self.grads[(name, l)]
                kdst = p.buf("slots:" + name, arr=self.slots.get(name), shape=S((8, nl, R, C_), g.dtype), write=True)
                if kind == "A":
                    ReduceOp(p, p.buf(f"g:{name}:{l}", arr=g), kdst, l, self.shapes[name], rows, r0, nr)
                else:
                    ForwardOp(p, kdst, l)
        return p

    def done(self, p):
        for k, arr in p.out.items():
            tag, name = k.split(":")[:2]
            if tag == "W":
                self.W[(name, int(k.split(":")[2]))] = arr
            else:
                self.slots[name] = arr


def _adamw(w, g, m, v):
    m = B1 * m + (1.0 - B1) * g
    v = B2 * v + (1.0 - B2) * jnp.square(g)
    m_hat = m / (1.0 - B1 ** STEP)
    v_hat = v / (1.0 - B2 ** STEP)
    return -LR * (m_hat / (jnp.sqrt(v_hat) + AEPS) + WD * w), m, v


def adam_big(w, m, v, slots, name, plan=None):
    nl, R, C = w.shape
    tr = 128 if R % 128 == 0 else (64 if R % 64 == 0 else R)

    def body(w_ref, m_ref, v_ref, s_ref, g_ref, d_ref, nm_ref, nv_ref):
        g = s_ref[0].astype(f32)
        for s in range(1, 8):
            g = g + s_ref[s].astype(f32)
        d, nm_, nv_ = _adamw(w_ref[...], g, m_ref[...], v_ref[...])
        g_ref[...] = g
        d_ref[...] = d
        nm_ref[...] = nm_
        nv_ref[...] = nv_

    blk = pl.BlockSpec((None, tr, C), lambda l, i: (l, i, 0))
    return pcall(body, plan, grid=(nl, R // tr),
                 in_specs=[blk, blk, blk, pl.BlockSpec((8, None, tr, C), lambda l, i: (0, l, i, 0))],
                 out_specs=[blk] * 4, out_shape=[S((nl, R, C), f32)] * 4,
                 sem=("parallel", "parallel"), name=name, args=[w, m, v, slots])


def adam_small(w, m, v, slots):
    R = w.shape[0]
    tr = 256

    def body(w_ref, m_ref, v_ref, s_ref, g_ref, d_ref, nm_ref, nv_ref):
        g = s_ref[0]
        for s in range(1, 8):
            g = g + s_ref[s]
        d, nm_, nv_ = _adamw(w_ref[...], g, m_ref[...], v_ref[...])
        g_ref[...] = g
        d_ref[...] = d
        nm_ref[...] = nm_
        nv_ref[...] = nv_

    blk = pl.BlockSpec((tr, 128), lambda i: (i, 0))
    return pl.pallas_call(
        body, grid=(R // tr,), in_specs=[blk, blk, blk, pl.BlockSpec((8, tr, 128), lambda i: (0, i, 0))],
        out_specs=[blk] * 4, out_shape=[S((R, 128), f32)] * 4,
        compiler_params=_cp(("parallel",)), name="adam_small")(w, m, v, slots)


def _pack(d):
    flat = jnp.concatenate([d[n].reshape(-1) for n in SMALL])
    n = flat.shape[0]
    rows = -(-n // (256 * 128)) * 256
    return jnp.pad(flat, (0, rows * 128 - n)).reshape(rows, 128)


def _unpack(p, like):
    flat = p.reshape(-1)
    out, off = {}, 0
    for n in SMALL:
        sz = math.prod(like[n].shape)
        out[n] = flat[off:off + sz].reshape(like[n].shape)
        off += sz
    return out


def kernel(x, positions, norm_mix, norm_ffn, norm_final, mix_w_in, mix_w_out, s5_A_re, s5_A_im, s5_log_dt, s5_B_re, s5_B_im, s5_C_re, s5_C_im, s5_D, s5_glu_w, s5_glu_b, hgrn_gamma, hgrn_norm, att_w_qkv, att_w_o, ffn_w_in, ffn_conv_w, ffn_conv_b, ffn_w_out, loss_target, m_norm_mix, m_norm_ffn, m_norm_final, m_mix_w_in, m_mix_w_out, m_s5_A_re, m_s5_A_im, m_s5_log_dt, m_s5_B_re, m_s5_B_im, m_s5_C_re, m_s5_C_im, m_s5_D, m_s5_glu_w, m_s5_glu_b, m_hgrn_gamma, m_hgrn_norm, m_att_w_qkv, m_att_w_o, m_ffn_w_in, m_ffn_conv_w, m_ffn_conv_b, m_ffn_w_out, v_norm_mix, v_norm_ffn, v_norm_final, v_mix_w_in, v_mix_w_out, v_s5_A_re, v_s5_A_im, v_s5_log_dt, v_s5_B_re, v_s5_B_im, v_s5_C_re, v_s5_C_im, v_s5_D, v_s5_glu_w, v_s5_glu_b, v_hgrn_gamma, v_hgrn_norm, v_att_w_qkv, v_att_w_o, v_ffn_w_in, v_ffn_conv_w, v_ffn_conv_b, v_ffn_w_out):
    a = dict(locals())
    weights = BIG + SMALL
    w = {n: a[n] for n in weights}
    m = {n: a["m_" + n] for n in weights}
    v = {n: a["v_" + n] for n in weights}
    shards = {"ffn_conv_w": ffn_conv_w}
    C = Comm(shards, {n: w[n].shape for n in BIG})
    for n in ("mix_w_in", "ffn_w_in", "mix_w_out", "s5_glu_w", "ffn_w_out", "att_w_qkv", "att_w_o"):
        shards[n] = hosted(C, "cast_" + n, lambda p: cast_bf16(w[n], "cast_" + n, plan=p))
    sm = {n: w[n] for n in SMALL}
    sm["ffn_conv_b3"] = ffn_conv_b.reshape(2, 1, 2 * DFF)
    loss, gx, _, gsmall = local_step(x[0], positions.reshape(L, 1), loss_target[0], sm, C.W, C)
    C.small = _pack(gsmall)
    res = {}
    for n in ("ffn_w_in", "ffn_w_out", "att_w_qkv", "att_w_o", "mix_w_out", "s5_glu_w", "ffn_conv_w", "mix_w_in"):
        res[n] = hosted(C, "adam_" + n, lambda p: adam_big(w[n], m[n], v[n], C.slots[n], "adam_" + n, plan=p))
    packed = adam_small(_pack({n: w[n] for n in SMALL}), _pack({n: m[n] for n in SMALL}), _pack({n: v[n] for n in SMALL}),
                        C.slots["small"])
    small_out = [_unpack(p, {n: w[n] for n in SMALL}) for p in packed]
    for n in SMALL:
        res[n] = tuple(so[n] for so in small_out)
    total = lax.psum(loss[0, 0], ("x", "y", "c"))
    order = ("norm_mix", "norm_ffn", "norm_final", "mix_w_in", "mix_w_out", "s5_A_re", "s5_A_im", "s5_log_dt", "s5_B_re", "s5_B_im",
             "s5_C_re", "s5_C_im", "s5_D", "s5_glu_w", "s5_glu_b", "hgrn_gamma", "hgrn_norm", "att_w_qkv", "att_w_o", "ffn_w_in",
             "ffn_conv_w", "ffn_conv_b", "ffn_w_out")
    return (total, gx[None], *[res[n][0] for n in order], *[res[n][1] for n in order], *[res[n][2] for n in order],
            *[res[n][3] for n in order])
```

```python
import functools
import math

import numpy as np
import jax
import jax.numpy as jnp
from jax import lax
from jax.experimental import pallas as pl
from jax.experimental.pallas import tpu as pltpu

f32 = jnp.float32
BF = jnp.bfloat16
HI = lax.Precision.HIGHEST
S = jax.ShapeDtypeStruct
MESH = pl.DeviceIdType.MESH

L = 2048
D = 1024
EPS = 1e-6
S5W = 512
NST = 2048
HGC = 64
DFF = 2816
ROPE_THETA = 500000.0
LR, B1, B2, AEPS, WD, STEP = 0.001, 0.9, 0.999, 1e-08, 0.01, 10
VMEM_LIMIT = 56 * 1024 * 1024


def _cp(sem=None):
    return pltpu.CompilerParams(dimension_semantics=sem, vmem_limit_bytes=VMEM_LIMIT)


ANY = pl.BlockSpec(memory_space=pl.ANY)
ROW_SHARDED = ("mix_w_out", "s5_glu_w", "ffn_w_out")


def _coords():
    x, y, c = lax.axis_index("x"), lax.axis_index("y"), lax.axis_index("c")
    return x, y, c, 2 * x + y, [(1 - x, y), (x, 1 - y), (1 - x, 1 - y)]


def _rows(start, n):
    return pl.ds(start if isinstance(start, int) else pl.multiple_of(start, 8), n)


def _cols(q, n):
    return pl.ds(pl.multiple_of(q * n, 128), n)


class Plan:
    def __init__(self):
        self.bufs, self.ops, self.nsem, self.out = {}, [], 0, {}

    def buf(self, key, arr=None, shape=None, write=False):
        b = self.bufs.setdefault(key, dict(arr=arr, shape=shape, write=False))
        b["write"] = b["write"] or write
        return key

    def add(self, op):
        op.base = self.nsem
        self.nsem += op.nsem
        self.ops.append(op)


class GatherOp:
    nsem = 13

    def __init__(self, plan, ksrc, kdst, l, shard_shape, rows, r0, nr, split):
        self.ksrc, self.kdst, self.l, (_, self.R, self.C), self.rows, self.r0, self.nr, self.split = (
            ksrc, kdst, l, shard_shape, rows, r0, nr, split)
        self.h = nr // 2 if split else nr
        plan.add(self)

    def _dst(self, R_, q, start, n):
        if self.rows:
            return R_[self.kdst].at[_rows(q * self.R + start, n), :]
        return R_[self.kdst].at[_rows(start, n), _cols(q, self.C)]

    def _mine(self, c):
        return self.r0 + (c * self.h if self.split else 0)

    def _theirs(self, c):
        return self.r0 + ((1 - c) * self.h if self.split else 0)

    def _copies(self, R_, sems):
        x, y, c, me, others = _coords()
        src = R_[self.ksrc]
        local = pltpu.make_async_copy(src.at[self.l, _rows(self.r0, self.nr), :], self._dst(R_, me, self.r0, self.nr),
                                      sems.at[self.base + 12])
        send, fwd = [], []
        for k, (px, py) in enumerate(others):
            q = 2 * px + py
            send.append((
                pltpu.make_async_remote_copy(src.at[self.l, _rows(self._mine(c), self.h), :], self._dst(R_, me, self._mine(c), self.h),
                                             sems.at[self.base + k], sems.at[self.base + 3 + k], device_id=(px, py, c), device_id_type=MESH),
                pltpu.make_async_remote_copy(src.at[self.l, _rows(self._mine(c), self.h), :], self._dst(R_, q, self._mine(c), self.h),
                                             sems.at[self.base + k], sems.at[self.base + 3 + k], device_id=(px, py, c), device_id_type=MESH)))
            fwd.append((
                pltpu.make_async_remote_copy(self._dst(R_, q, self._mine(c), self.h), self._dst(R_, q, self._mine(c), self.h),
                                             sems.at[self.base + 6 + k], sems.at[self.base + 9 + k], device_id=(x, y, 1 - c), device_id_type=MESH),
                pltpu.make_async_remote_copy(self._dst(R_, q, self._theirs(c), self.h), self._dst(R_, q, self._theirs(c), self.h),
                                             sems.at[self.base + 6 + k], sems.at[self.base + 9 + k], device_id=(x, y, 1 - c), device_id_type=MESH)))
        return local, send, fwd

    def start(self, R_, sems):
        local, send, _ = self._copies(R_, sems)
        local.start()
        for out, _ in send:
            out.start()

    def finish(self, R_, sems):
        local, send, fwd = self._copies(R_, sems)
        for k in range(3):
            send[k][1].wait_recv()
            if self.split:
                fwd[k][0].start()
        for k in range(3):
            if self.split:
                fwd[k][1].wait_recv()
                fwd[k][0].wait_send()
            send[k][0].wait_send()
        local.wait()


class ReduceOp:
    nsem = 7

    def __init__(self, plan, ksrc, kdst, l, shard_shape, rows, r0, nr, whole=False):
        self.ksrc, self.kdst, self.l, (self.R, self.C), self.rows, self.r0, self.nr, self.whole = (
            ksrc, kdst, l, shard_shape[-2:], rows, r0, nr, whole)
        plan.add(self)

    def _piece(self, R_, q):
        g = R_[self.ksrc]
        if self.whole:
            return g
        if self.rows:
            return g.at[_rows(q * self.R + self.r0, self.nr), :]
        return g.at[_rows(self.r0, self.nr), _cols(q, self.C)]

    def _slot(self, R_, s):
        if self.whole:
            return R_[self.kdst].at[s]
        return R_[self.kdst].at[s, self.l, _rows(self.r0, self.nr), :]

    def _copies(self, R_, sems):
        x, y, c, me, others = _coords()
        local = pltpu.make_async_copy(self._piece(R_, me), self._slot(R_, 2 * me + c), sems.at[self.base + 6])
        send = []
        for k, (px, py) in enumerate(others):
            q = 2 * px + py
            send.append((
                pltpu.make_async_remote_copy(self._piece(R_, q), self._slot(R_, 2 * me + c), sems.at[self.base + k],
                                             sems.at[self.base + 3 + k], device_id=(px, py, c), device_id_type=MESH),
                pltpu.make_async_remote_copy(self._piece(R_, q), self._slot(R_, 2 * q + c), sems.at[self.base + k],
                                             sems.at[self.base + 3 + k], device_id=(px, py, c), device_id_type=MESH)))
        return local, send

    def start(self, R_, sems):
        local, send = self._copies(R_, sems)
        local.start()
        for out, _ in send:
            out.start()

    def finish(self, R_, sems):
        local, send = self._copies(R_, sems)
        local.wait()
        for out, inn in send:
            inn.wait_recv()
            out.wait_send()


class ForwardOp:
    nsem = 8

    def __init__(self, plan, kdst, l, whole=False):
        self.kdst, self.l, self.whole = kdst, l, whole
        plan.add(self)

    def _slot(self, R_, s):
        return R_[self.kdst].at[s] if self.whole else R_[self.kdst].at[s, self.l]

    def _copies(self, R_, sems):
        x, y, c, me, others = _coords()
        return [(pltpu.make_async_remote_copy(self._slot(R_, 2 * q + c), self._slot(R_, 2 * q + c), sems.at[self.base + q],
                                              sems.at[self.base + 4 + q], device_id=(x, y, 1 - c), device_id_type=MESH),
                 pltpu.make_async_remote_copy(self._slot(R_, 2 * q + 1 - c), self._slot(R_, 2 * q + 1 - c), sems.at[self.base + q],
                                              sems.at[self.base + 4 + q], device_id=(x, y, 1 - c), device_id_type=MESH))
                for q in range(4)]

    def start(self, R_, sems):
        for out, _ in self._copies(R_, sems):
            out.start()

    def finish(self, R_, sems):
        for out, inn in self._copies(R_, sems):
            inn.wait_recv()
            out.wait_send()


def pcall(body, plan, *, grid, in_specs, out_specs, out_shape, scratch_shapes=(), sem, name, args):
    multi = isinstance(out_shape, (list, tuple))
    if plan is None or not plan.ops:
        return pl.pallas_call(body, grid=grid, in_specs=in_specs, out_specs=out_specs, out_shape=out_shape,
                              scratch_shapes=list(scratch_shapes), compiler_params=_cp(sem), name=name)(*args)
    outs = list(out_shape) if multi else [out_shape]
    ospecs = list(out_specs) if multi else [out_specs]
    kin = [k for k, b in plan.bufs.items() if b["arr"] is not None]
    kout = [k for k, b in plan.bufs.items() if b["write"]]
    n_in, n_out, n_scr = len(in_specs), len(outs), len(scratch_shapes)

    def wrapped(*refs):
        o0 = n_in + len(kin)
        s0 = o0 + n_out + len(kout)
        R_ = dict(zip(kin, refs[n_in:o0]))
        R_.update(zip(kout, refs[o0 + n_out:s0]))
        sems = refs[s0 + n_scr]
        first = functools.reduce(jnp.logical_and, [pl.program_id(d) == 0 for d in range(len(grid))])
        last = functools.reduce(jnp.logical_and, [pl.program_id(d) == grid[d] - 1 for d in range(len(grid))])

        @pl.when(first)
        def _():
            for op in plan.ops:
                op.start(R_, sems)

        body(*refs[:n_in], *refs[o0:o0 + n_out], *refs[s0:s0 + n_scr])

        @pl.when(last)
        def _():
            for op in plan.ops:
                op.finish(R_, sems)

    def shape_of(k):
        b = plan.bufs[k]
        return S(b["arr"].shape, b["arr"].dtype) if b["arr"] is not None else b["shape"]

    res = pl.pallas_call(
        wrapped, grid=grid, in_specs=list(in_specs) + [ANY] * len(kin), out_specs=ospecs + [ANY] * len(kout),
        out_shape=outs + [shape_of(k) for k in kout],
        scratch_shapes=list(scratch_shapes) + [pltpu.SemaphoreType.DMA((plan.nsem,))],
        input_output_aliases={n_in + kin.index(k): n_out + kout.index(k) for k in kout if plan.bufs[k]["arr"] is not None},
        compiler_params=pltpu.CompilerParams(dimension_semantics=("arbitrary",) * len(grid), vmem_limit_bytes=VMEM_LIMIT,
                                             has_side_effects=True),
        name=name)(*args, *[plan.bufs[k]["arr"] for k in kin])
    plan.out = dict(zip(kout, res[n_out:]))
    return list(res[:n_out]) if multi else res[0]


def _dg(a, b, ca, cb):
    return lax.dot_general(a.astype(BF), b.astype(BF), (((ca,), (cb,)), ((), ())), preferred_element_type=f32)


@jax.custom_vjp
def dot_nn(a, b):
    return _dg(a, b, 1, 0)


@jax.custom_vjp
def dot_nt(a, b):
    return _dg(a, b, 1, 1)


@jax.custom_vjp
def dot_tn(a, b):
    return _dg(a, b, 0, 0)


dot_nn.defvjp(lambda a, b: (dot_nn(a, b), (a, b)),
              lambda r, g: (dot_nt(g, r[1]).astype(r[0].dtype), dot_tn(r[0], g).astype(r[1].dtype)))
dot_nt.defvjp(lambda a, b: (dot_nt(a, b), (a, b)),
              lambda r, g: (dot_nn(g, r[1]).astype(r[0].dtype), dot_tn(g, r[0]).astype(r[1].dtype)))
dot_tn.defvjp(lambda a, b: (dot_tn(a, b), (a, b)),
              lambda r, g: (dot_nt(r[1], g).astype(r[0].dtype), dot_nn(r[0], g).astype(r[1].dtype)))


def matmul(a, b, *, mode, tm, tn, tk, out_dtype=f32, add=None, b_lead=None, a_spec=None, b_spec=None, dims=None, plan=None, name):
    a_over, b_over = a_spec, b_spec
    if mode == "nn":
        (M, K), N = a.shape[-2:], b.shape[-1]
        a_spec = pl.BlockSpec((tm, tk), lambda i, j, k: (i, k))
        b_blk, b_idx, ca, cb = (tk, tn), (lambda i, j, k: (k, j)), 1, 0
    elif mode == "nt":
        (M, K), N = a.shape[-2:], b.shape[-2]
        a_spec = pl.BlockSpec((tm, tk), lambda i, j, k: (i, k))
        b_blk, b_idx, ca, cb = (tn, tk), (lambda i, j, k: (j, k)), 1, 1
    else:
        (K, M), N = a.shape[-2:], b.shape[-1]
        a_spec = pl.BlockSpec((tk, tm), lambda i, j, k: (k, i))
        b_blk, b_idx, ca, cb = (tk, tn), (lambda i, j, k: (k, j)), 0, 0
    if dims is not None:
        M, N, K = dims
    assert M % tm == 0 and N % tn == 0 and K % tk == 0, (name, M, N, K, tm, tn, tk)
    if b_lead is None:
        b_spec = pl.BlockSpec(b_blk, b_idx)
    else:
        b_spec = pl.BlockSpec((None,) + b_blk, lambda i, j, k: (b_lead,) + b_idx(i, j, k))
    if a_over is not None:
        a_spec = a_over
    if b_over is not None:
        b_spec = b_over
    nk = K // tk
    has_add = add is not None

    def body(*refs):
        a_ref, b_ref = refs[0], refs[1]
        add_ref = refs[2] if has_add else None
        o_ref = refs[2 + has_add]
        p = _dg(a_ref[...], b_ref[...], ca, cb)

        def fin(v):
            if has_add:
                v = v + add_ref[...].astype(f32)
            o_ref[...] = v.astype(o_ref.dtype)

        if nk == 1:
            fin(p)
        else:
            acc = refs[3 + has_add]
            k = pl.program_id(2)

            @pl.when(k == 0)
            def _():
                acc[...] = p

            @pl.when(k > 0)
            def _():
                acc[...] += p

            @pl.when(k == nk - 1)
            def _():
                fin(acc[...])

    in_specs = [a_spec, b_spec]
    args = [a, b]
    if has_add:
        in_specs.append(pl.BlockSpec((tm, tn), lambda i, j, k: (i, j)))
        args.append(add)
    return pcall(body, plan, grid=(M // tm, N // tn, nk), in_specs=in_specs,
                 out_specs=pl.BlockSpec((tm, tn), lambda i, j, k: (i, j)), out_shape=S((M, N), out_dtype),
                 scratch_shapes=[pltpu.VMEM((tm, tn), f32)] if nk > 1 else [],
                 sem=("parallel", "parallel", "arbitrary"), name=name, args=args)


def _rms(xv, gv):
    return xv * lax.rsqrt(jnp.mean(xv * xv, axis=-1, keepdims=True) + EPS) * gv


TR = 256


def rms_fwd(x, g, name):
    def body(x_ref, g_ref, o_ref):
        o_ref[...] = _rms(x_ref[...], g_ref[...]).astype(o_ref.dtype)

    return pl.pallas_call(
        body, grid=(L // TR,),
        in_specs=[pl.BlockSpec((TR, D), lambda i: (i, 0)), pl.BlockSpec((1, D), lambda i: (0, 0))],
        out_specs=pl.BlockSpec((TR, D), lambda i: (i, 0)), out_shape=S((L, D), BF),
        compiler_params=_cp(("parallel",)), name=name)(x, g)


def rms_bwd(x, g, dys, dres, name, plan=None):
    nd = len(dys)

    def body(*refs):
        x_ref, g_ref = refs[0], refs[1]
        dr_ref, dh_ref, dg_ref = refs[2 + nd:]
        dy = refs[2][...].astype(f32)
        for r in refs[3:2 + nd]:
            dy = dy + r[...].astype(f32)
        _, vjp = jax.vjp(_rms, x_ref[...], g_ref[...])
        dx, dg = vjp(dy)
        dh_ref[...] = dr_ref[...] + dx

        @pl.when(pl.program_id(0) == 0)
        def _():
            dg_ref[...] = jnp.zeros_like(dg_ref)

        dg_ref[...] += dg

    row = pl.BlockSpec((TR, D), lambda i: (i, 0))
    vec = pl.BlockSpec((1, D), lambda i: (0, 0))
    return pcall(body, plan, grid=(L // TR,), in_specs=[row, vec] + [row] * (nd + 1), out_specs=[row, vec],
                 out_shape=[S((L, D), f32), S((1, D), f32)], sem=("arbitrary",), name=name, args=[x, g, *dys, dres])


def loss_head(h, g, tgt):
    def f(hv, gv, tv):
        y = _rms(hv, gv)
        return 0.5 * jnp.sum(jnp.mean(jnp.square(y - tv), axis=-1))

    def body(h_ref, g_ref, t_ref, l_ref, dh_ref, dg_ref):
        val, vjp = jax.vjp(f, h_ref[...], g_ref[...], t_ref[...])
        dh, dg, _ = vjp(jnp.ones((), f32))
        dh_ref[...] = dh

        @pl.when(pl.program_id(0) == 0)
        def _():
            dg_ref[...] = jnp.zeros_like(dg_ref)
            l_ref[...] = jnp.zeros_like(l_ref)

        dg_ref[...] += dg
        l_ref[...] += jnp.full((1, 128), val, f32)

    row = pl.BlockSpec((TR, D), lambda i: (i, 0))
    vec = pl.BlockSpec((1, D), lambda i: (0, 0))
    return pl.pallas_call(
        body, grid=(L // TR,), in_specs=[row, vec, row],
        out_specs=[pl.BlockSpec((1, 128), lambda i: (0, 0)), row, vec],
        out_shape=[S((1, 128), f32), S((L, D), f32), S((1, D), f32)],
        compiler_params=_cp(("arbitrary",)), name="loss_head")(h, g, tgt)


def _col_to_row(c):
    n = c.shape[0]
    t = jnp.broadcast_to(c, (n, 128)).T
    r = lax.broadcasted_iota(jnp.int32, (128, n), 0)
    return jnp.sum(jnp.where(r == 0, t, 0.0), axis=0, keepdims=True)


def _s5_param_map(are, aim, ldt_row, bre, bim, cre, cim):
    n = NST
    gi = lax.broadcasted_iota(jnp.int32, (n, 32), 0) // 64
    gj = lax.broadcasted_iota(jnp.int32, (n, 32), 1)
    ldt = jnp.sum(jnp.where(gi == gj, ldt_row, 0.0), axis=1, keepdims=True)
    dt = jnp.exp(ldt)
    mag = jnp.exp(are * dt)
    abr = mag * jnp.cos(aim * dt)
    abi = mag * jnp.sin(aim * dt)
    den = are * are + aim * aim
    nr, ni = abr - 1.0, abi
    cr = (nr * are + ni * aim) / den
    ci = (ni * are - nr * aim) / den
    bbr = cr * bre - ci * bim
    bbi = cr * bim + ci * bre
    tc = lax.broadcasted_iota(jnp.int32, (16, 128), 0)
    tl = lax.broadcasted_iota(jnp.int32, (16, 128), 1)
    T = (tl % 16 == tc).astype(f32)
    mr = (lax.broadcasted_iota(jnp.int32, (n, 128), 0) // 64) % 8
    mc = lax.broadcasted_iota(jnp.int32, (n, 128), 1) // 16
    mask = (mr == mc).astype(f32)

    def expand(v):
        return jnp.dot(v, T, precision=HI, preferred_element_type=f32) * mask

    return expand(bbr), expand(bbi), expand(cre), expand(cim), _col_to_row(abr), _col_to_row(abi)


def s5_params_fwd(are, aim, ldt_row, bre, bim, cre, cim):
    def body(*refs):
        outs = _s5_param_map(*[r[...] for r in refs[:7]])
        for o_ref, o in zip(refs[7:], outs):
            o_ref[...] = o

    return pl.pallas_call(
        body, out_shape=[S((NST, 128), f32)] * 4 + [S((1, NST), f32)] * 2,
        compiler_params=_cp(), name="s5_params_fwd")(are, aim, ldt_row, bre, bim, cre, cim)


def s5_params_bwd(are, aim, ldt_row, bre, bim, cre, cim, cots):
    def body(*refs):
        _, vjp = jax.vjp(_s5_param_map, *[r[...] for r in refs[:7]])
        gs = vjp(tuple(r[...] for r in refs[7:13]))
        for o_ref, o in zip(refs[13:], gs):
            o_ref[...] = o

    return pl.pallas_call(
        body, out_shape=[S((NST, 1), f32)] * 2 + [S((1, 32), f32)] + [S((NST, 16), f32)] * 4,
        compiler_params=_cp(), name="s5_params_bwd")(are, aim, ldt_row, bre, bim, cre, cim, *cots)


def _cpowers(ar, ai):
    out = [(ar, ai)]
    for _ in range(7):
        pr, pi = out[-1]
        out.append((pr * ar - pi * ai, pr * ai + pi * ar))
    return out


def _ctable(pw, rid, power):
    tr_ = jnp.zeros(rid.shape, f32)
    ti_ = jnp.zeros(rid.shape, f32)
    for r in range(8):
        pr, pi = pw[power(r) - 1]
        tr_ = jnp.where(rid == r, pr, tr_)
        ti_ = jnp.where(rid == r, pi, ti_)
    return tr_, ti_


NT5 = 4
RC = 256


def s5_scan_fwd(proj, wbr, wbi, wcr, wci, abr, abi, drow, plan=None):
    def body(u_ref, wbr_ref, wbi_ref, wcr_ref, wci_ref, ar_ref, ai_ref, d_ref, xr_ref, xi_ref, y_ref):
        wbr_v, wbi_v = wbr_ref[...], wbi_ref[...]
        for r in range(L // RC):
            rows = pl.ds(r * RC, RC)
            ub = u_ref[rows, :]
            xr_ref[rows, :] = dot_nt(ub, wbr_v)
            xi_ref[rows, :] = dot_nt(ub, wbi_v)
        pw = _cpowers(ar_ref[...], ai_ref[...])
        rid = lax.broadcasted_iota(jnp.int32, (8, 512), 0)
        tr_, ti_ = _ctable(pw, rid, lambda r: r + 1)

        def group(j, c):
            cr, ci = c
            rows = pl.ds(pl.multiple_of(j * 8, 8), 8)
            br, bi = xr_ref[rows, :], xi_ref[rows, :]
            for s in (1, 2, 4):
                pr, pi = pw[s - 1]
                sr = jnp.where(rid >= s, pltpu.roll(br, s, 0), 0.0)
                si = jnp.where(rid >= s, pltpu.roll(bi, s, 0), 0.0)
                br, bi = br + pr * sr - pi * si, bi + pr * si + pi * sr
            br, bi = br + tr_ * cr - ti_ * ci, bi + tr_ * ci + ti_ * cr
            xr_ref[rows, :] = br
            xi_ref[rows, :] = bi
            return br[7:8], bi[7:8]

        z = jnp.zeros((1, 512), f32)
        lax.fori_loop(0, L // 8, group, (z, z), unroll=2)
        wcr_v, wci_v, dv = wcr_ref[...], wci_ref[...], d_ref[...]
        for r in range(L // RC):
            rows = pl.ds(r * RC, RC)
            y_ref[rows, :] = (dot_nn(xr_ref[rows, :], wcr_v) - dot_nn(xi_ref[rows, :], wci_v)
                              + dv * u_ref[rows, :])

    wspec = pl.BlockSpec((512, 128), lambda j: (j, 0))
    aspec = pl.BlockSpec((1, 512), lambda j: (0, j))
    return pcall(
        body, plan, grid=(NT5,),
        in_specs=[pl.BlockSpec((L, 128), lambda j: (0, j)), wspec, wspec, wspec, wspec, aspec, aspec,
                  pl.BlockSpec((1, 128), lambda j: (0, j))],
        out_specs=[pl.BlockSpec((L, 512), lambda j: (0, j)), pl.BlockSpec((L, 512), lambda j: (0, j)),
                   pl.BlockSpec((L, 128), lambda j: (0, j))],
        out_shape=[S((L, NST), f32), S((L, NST), f32), S((L, S5W), f32)],
        sem=("parallel",), name="s5_scan_fwd", args=[proj, wbr, wbi, wcr, wci, abr, abi, drow])


def s5_scan_bwd(dy, proj, xs_re, xs_im, wbr, wbi, wcr, wci, abr, abi, drow, plan=None):
    def body(dy_ref, u_ref, xr_ref, xi_ref, wbr_ref, wbi_ref, wcr_ref, wci_ref, ar_ref, ai_ref, d_ref,
             du_ref, gwbr_ref, gwbi_ref, gwcr_ref, gwci_ref, gar_ref, gai_ref, gd_ref, lr_ref, li_ref):
        wcr_v, wci_v = wcr_ref[...], wci_ref[...]
        gwcr = jnp.zeros((512, 128), f32)
        gwci = jnp.zeros((512, 128), f32)
        gd = jnp.zeros((1, 128), f32)
        for r in range(L // RC):
            rows = pl.ds(r * RC, RC)
            dyv = dy_ref[rows, :]
            lr_ref[rows, :] = dot_nt(dyv, wcr_v)
            li_ref[rows, :] = -dot_nt(dyv, wci_v)
            gwcr += dot_tn(xr_ref[rows, :], dyv)
            gwci -= dot_tn(xi_ref[rows, :], dyv)
            gd += jnp.sum(dyv * u_ref[rows, :], axis=0, keepdims=True)
        gwcr_ref[...] = gwcr
        gwci_ref[...] = gwci
        gd_ref[...] = gd
        pw = _cpowers(ar_ref[...], -ai_ref[...])
        rid = lax.broadcasted_iota(jnp.int32, (8, 512), 0)
        tr_, ti_ = _ctable(pw, rid, lambda r: 8 - r)

        def group(i, c):
            cr, ci, gar, gai = c
            j = L // 8 - 1 - i
            rows = pl.ds(pl.multiple_of(j * 8, 8), 8)
            br, bi = lr_ref[rows, :], li_ref[rows, :]
            for s in (1, 2, 4):
                pr, pi = pw[s - 1]
                sr = jnp.where(rid < 8 - s, pltpu.roll(br, 8 - s, 0), 0.0)
                si = jnp.where(rid < 8 - s, pltpu.roll(bi, 8 - s, 0), 0.0)
                br, bi = br + pr * sr - pi * si, bi + pr * si + pi * sr
            br, bi = br + tr_ * cr - ti_ * ci, bi + tr_ * ci + ti_ * cr
            lr_ref[rows, :] = br
            li_ref[rows, :] = bi
            nr = jnp.where(rid < 7, pltpu.roll(br, 7, 0), cr)
            ni = jnp.where(rid < 7, pltpu.roll(bi, 7, 0), ci)
            xr, xi = xr_ref[rows, :], xi_ref[rows, :]
            return br[0:1], bi[0:1], gar + xr * nr + xi * ni, gai + xr * ni - xi * nr

        z = jnp.zeros((1, 512), f32)
        z8 = jnp.zeros((8, 512), f32)
        _, _, gar, gai = lax.fori_loop(0, L // 8, group, (z, z, z8, z8), unroll=2)
        gar_ref[...] = jnp.sum(gar, axis=0, keepdims=True)
        gai_ref[...] = jnp.sum(gai, axis=0, keepdims=True)
        wbr_v, wbi_v, dv = wbr_ref[...], wbi_ref[...], d_ref[...]
        gwbr = jnp.zeros((512, 128), f32)
        gwbi = jnp.zeros((512, 128), f32)
        for r in range(L // RC):
            rows = pl.ds(r * RC, RC)
            lrv, liv, uv = lr_ref[rows, :], li_ref[rows, :], u_ref[rows, :]
            du_ref[rows, :] = (dot_nn(lrv, wbr_v) + dot_nn(liv, wbi_v) + dv * dy_ref[rows, :]).astype(du_ref.dtype)
            gwbr += dot_tn(lrv, uv)
            gwbi += dot_tn(liv, uv)
        gwbr_ref[...] = gwbr
        gwbi_ref[...] = gwbi

    wspec = pl.BlockSpec((512, 128), lambda j: (j, 0))
    aspec = pl.BlockSpec((1, 512), lambda j: (0, j))
    col = pl.BlockSpec((L, 128), lambda j: (0, j))
    st = pl.BlockSpec((L, 512), lambda j: (0, j))
    dspec = pl.BlockSpec((1, 128), lambda j: (0, j))
    return pcall(
        body, plan, grid=(NT5,),
        in_specs=[col, col, st, st, wspec, wspec, wspec, wspec, aspec, aspec, dspec],
        out_specs=[col, wspec, wspec, wspec, wspec, aspec, aspec, dspec],
        out_shape=[S((L, S5W), BF)] + [S((NST, 128), f32)] * 4 + [S((1, NST), f32)] * 2 + [S((1, S5W), f32)],
        scratch_shapes=[pltpu.VMEM((L, 512), f32), pltpu.VMEM((L, 512), f32)],
        sem=("parallel",), name="s5_scan_bwd", args=[dy, proj, xs_re, xs_im, wbr, wbi, wcr, wci, abr, abi, drow])


def _glu(y, w, b):
    z = jax.nn.gelu(y)
    return z * jax.nn.sigmoid(dot_nn(z, w) + b)


def s5_glu_fwd(y, w, b):
    def body(y_ref, w_ref, b_ref, o_ref):
        o_ref[...] = _glu(y_ref[...], w_ref[...], b_ref[...]).astype(o_ref.dtype)

    return pl.pallas_call(
        body, grid=(L // TR,),
        in_specs=[pl.BlockSpec((TR, S5W), lambda i: (i, 0)), pl.BlockSpec((S5W, S5W), lambda i: (0, 0)),
                  pl.BlockSpec((1, S5W), lambda i: (0, 0))],
        out_specs=pl.BlockSpec((TR, S5W), lambda i: (i, 0)), out_shape=S((L, S5W), BF),
        compiler_params=_cp(("parallel",)), name="s5_glu_fwd")(y, w, b)


def s5_glu_bwd(y, w, b, dmix):
    def body(y_ref, w_ref, b_ref, g_ref, dy_ref, dw_ref, db_ref):
        _, vjp = jax.vjp(_glu, y_ref[...], w_ref[...].astype(f32), b_ref[...])
        dy, dw, db = vjp(g_ref[...])
        dy_ref[...] = dy

        @pl.when(pl.program_id(0) == 0)
        def _():
            dw_ref[...] = jnp.zeros_like(dw_ref)
            db_ref[...] = jnp.zeros_like(db_ref)

        dw_ref[...] += dw
        db_ref[...] += db

    row = pl.BlockSpec((TR, S5W), lambda i: (i, 0))
    return pl.pallas_call(
        body, grid=(L // TR,),
        in_specs=[row, pl.BlockSpec((S5W, S5W), lambda i: (0, 0)), pl.BlockSpec((1, S5W), lambda i: (0, 0)), row],
        out_specs=[row, pl.BlockSpec((S5W, S5W), lambda i: (0, 0)), pl.BlockSpec((1, S5W), lambda i: (0, 0))],
        out_shape=[S((L, S5W), f32), S((S5W, S5W), f32), S((1, S5W), f32)],
        compiler_params=_cp(("arbitrary",)), name="s5_glu_bwd")(y, w, b, dmix)


def _dg3(a, b, ca, cb):
    ah, bh = a.astype(BF), b.astype(BF)
    al, bl = (a - ah.astype(f32)).astype(BF), (b - bh.astype(f32)).astype(BF)
    return _dg(ah, bh, ca, cb) + _dg(ah, bl, ca, cb) + _dg(al, bh, ca, cb)


@jax.custom_vjp
def hi_nn(a, b):
    return _dg3(a, b, 1, 0)


@jax.custom_vjp
def hi_nt(a, b):
    return _dg3(a, b, 1, 1)


@jax.custom_vjp
def hi_tn(a, b):
    return _dg3(a, b, 0, 0)


hi_nn.defvjp(lambda a, b: (hi_nn(a, b), (a, b)), lambda r, g: (hi_nt(g, r[1]), hi_tn(r[0], g)))
hi_nt.defvjp(lambda a, b: (hi_nt(a, b), (a, b)), lambda r, g: (hi_nn(g, r[1]), hi_tn(g, r[0])))
hi_tn.defvjp(lambda a, b: (hi_tn(a, b), (a, b)), lambda r, g: (hi_nt(r[1], g), hi_nn(r[0], g)))


def _hgrn_chunk(St, xq, xf, xi, xg, gam, ng):
    lb = jax.nn.sigmoid(gam[0:1] - gam[1:2])
    q = jax.nn.silu(xq)
    f = lb + (1.0 - lb) * jax.nn.sigmoid(xf)
    k = 1.0 - f
    g = jnp.log(f)
    ti = lax.broadcasted_iota(jnp.int32, (HGC, HGC), 0)
    si = lax.broadcasted_iota(jnp.int32, (HGC, HGC), 1)
    causal = si <= ti
    b = jnp.dot(causal.astype(f32), g, precision=HI, preferred_element_type=f32)
    qe = q * jnp.exp(b)
    o = hi_nt(qe, St)
    att = jnp.where(causal, hi_nt(qe, k * jnp.exp(-b)), 0.0)
    o = o + hi_nn(att, xi)
    bl = b[HGC - 1:HGC]
    St_new = St * jnp.exp(bl) + hi_tn(xi, k * jnp.exp(bl - b))
    o = o * lax.rsqrt(jnp.mean(o * o, axis=-1, keepdims=True) + EPS) * ng
    return St_new, o * jax.nn.silu(xg)


NCH = L // HGC


def hgrn_fwd(proj, gamma, hnorm, plan=None):
    def body(q_ref, f_ref, i_ref, g_ref, gam_ref, ng_ref, o_ref, ss_ref, st):
        @pl.when(pl.program_id(0) == 0)
        def _():
            st[...] = jnp.zeros_like(st)

        for h in range(4):
            sl = slice(h * 128, (h + 1) * 128)
            s0 = st[h]
            ss_ref[0, h] = s0
            s1, o = _hgrn_chunk(s0, q_ref[:, sl], f_ref[:, sl], i_ref[:, sl], g_ref[:, sl], gam_ref[:, sl], ng_ref[:, sl])
            st[h] = s1
            o_ref[:, sl] = o.astype(o_ref.dtype)

    def pj(n):
        return pl.BlockSpec((HGC, 512), lambda c: (c, n))

    return pcall(
        body, plan, grid=(NCH,),
        in_specs=[pj(1), pj(2), pj(3), pj(4), pl.BlockSpec((2, 512), lambda c: (0, 0)), pl.BlockSpec((1, 512), lambda c: (0, 0))],
        out_specs=[pl.BlockSpec((HGC, 512), lambda c: (c, 0)), pl.BlockSpec((1, 4, 128, 128), lambda c: (c, 0, 0, 0))],
        out_shape=[S((L, 512), BF), S((NCH, 4, 128, 128), f32)],
        scratch_shapes=[pltpu.VMEM((4, 128, 128), f32)],
        sem=("arbitrary",), name="hgrn_fwd", args=[proj, proj, proj, proj, gamma, hnorm])


def hgrn_bwd(proj, gamma, hnorm, ssave, dmix, du, plan=None):
    def body(q_ref, f_ref, i_ref, g_ref, gam_ref, ng_ref, ss_ref, do_ref, du_ref, dp_ref, dgam_ref, dng_ref, dst):
        @pl.when(pl.program_id(0) == 0)
        def _():
            dst[...] = jnp.zeros_like(dst)
            dgam_ref[...] = jnp.zeros_like(dgam_ref)
            dng_ref[...] = jnp.zeros_like(dng_ref)

        dp_ref[:, 0:512] = du_ref[...]
        for h in range(4):
            sl = slice(h * 128, (h + 1) * 128)
            _, vjp = jax.vjp(_hgrn_chunk, ss_ref[0, h], q_ref[:, sl], f_ref[:, sl], i_ref[:, sl], g_ref[:, sl],
                             gam_ref[:, sl], ng_ref[:, sl])
            ds, dq, df, di, dg, dgam, dng = vjp((dst[h], do_ref[:, sl]))
            dst[h] = ds
            for n, v in enumerate((dq, df, di, dg)):
                dp_ref[:, 512 * (n + 1) + h * 128: 512 * (n + 1) + (h + 1) * 128] = v.astype(dp_ref.dtype)
            dgam_ref[:, sl] += dgam
            dng_ref[:, sl] += dng

    def pj(n):
        return pl.BlockSpec((HGC, 512), lambda i: (NCH - 1 - i, n))

    return pcall(
        body, plan, grid=(NCH,),
        in_specs=[pj(1), pj(2), pj(3), pj(4), pl.BlockSpec((2, 512), lambda i: (0, 0)), pl.BlockSpec((1, 512), lambda i: (0, 0)),
                  pl.BlockSpec((1, 4, 128, 128), lambda i: (NCH - 1 - i, 0, 0, 0)), pj(1), pj(0)],
        out_specs=[pl.BlockSpec((HGC, 2560), lambda i: (NCH - 1 - i, 0)), pl.BlockSpec((2, 512), lambda i: (0, 0)),
                   pl.BlockSpec((1, 512), lambda i: (0, 0))],
        out_shape=[S((L, 2560), BF), S((2, 512), f32), S((1, 512), f32)],
        scratch_shapes=[pltpu.VMEM((4, 128, 128), f32)],
        sem=("arbitrary",), name="hgrn_bwd", args=[proj, proj, proj, proj, gamma, hnorm, ssave, dmix, du])


def _shift(x, k):
    return jnp.concatenate([jnp.zeros((k, x.shape[1]), x.dtype), x[:-k]], axis=0)


def _convact(ha, hb, wa, wb, ba, bb):
    ca = wa[2:3] * ha + wa[1:2] * _shift(ha, 1) + wa[0:1] * _shift(ha, 2) + ba
    cb = wb[2:3] * hb + wb[1:2] * _shift(hb, 1) + wb[0:1] * _shift(hb, 2) + bb
    return jax.nn.silu(ca) * cb


CT = 128
NCT = DFF // CT


def convact_fwd(hu, cw, cb, layer, plan=None):
    def body(ha_ref, hb_ref, wa_ref, wb_ref, ba_ref, bb_ref, o_ref):
        o_ref[...] = _convact(ha_ref[...], hb_ref[...], wa_ref[...], wb_ref[...], ba_ref[...], bb_ref[...]).astype(o_ref.dtype)

    def h(off):
        return pl.BlockSpec((L, CT), lambda j: (0, j + off))

    def w(off):
        return pl.BlockSpec((3, CT), lambda j: (0, j + off))

    def b(off):
        return pl.BlockSpec((None, 1, CT), lambda j: (layer, 0, j + off))

    return pcall(body, plan, grid=(NCT,), in_specs=[h(0), h(NCT), w(0), w(NCT), b(0), b(NCT)],
                 out_specs=pl.BlockSpec((L, CT), lambda j: (0, j)), out_shape=S((L, DFF), BF),
                 sem=("parallel",), name=f"convact_fwd{layer}", args=[hu, hu, cw, cw, cb, cb])


def convact_bwd(hu, cw, cb, dact, layer, plan=None):
    def body(ha_ref, hb_ref, wa_ref, wb_ref, ba_ref, bb_ref, g_ref, dh_ref, dw_ref, db_ref, sh, sw, sb):
        j = pl.program_id(0)

        @pl.when(j < NCT)
        def _():
            _, vjp = jax.vjp(_convact, ha_ref[...], hb_ref[...], wa_ref[...], wb_ref[...], ba_ref[...], bb_ref[...])
            dha, dhb, dwa, dwb, dba, dbb = vjp(g_ref[...].astype(f32))
            dh_ref[...] = dha.astype(dh_ref.dtype)
            dw_ref[...] = dwa
            db_ref[...] = dba
            sh[j] = dhb.astype(sh.dtype)
            sw[j] = dwb
            sb[j] = dbb

        @pl.when(j >= NCT)
        def _():
            dh_ref[...] = sh[j - NCT]
            dw_ref[...] = sw[j - NCT]
            db_ref[...] = sb[j - NCT]

    def lo(j):
        return jnp.minimum(j, NCT - 1)

    in_specs = [pl.BlockSpec((L, CT), lambda j: (0, lo(j))), pl.BlockSpec((L, CT), lambda j: (0, lo(j) + NCT)),
                pl.BlockSpec((3, CT), lambda j: (0, lo(j))), pl.BlockSpec((3, CT), lambda j: (0, lo(j) + NCT)),
                pl.BlockSpec((None, 1, CT), lambda j: (layer, 0, lo(j))), pl.BlockSpec((None, 1, CT), lambda j: (layer, 0, lo(j) + NCT)),
                pl.BlockSpec((L, CT), lambda j: (0, lo(j)))]
    return pcall(
        body, plan, grid=(2 * NCT,), in_specs=in_specs,
        out_specs=[pl.BlockSpec((L, CT), lambda j: (0, j)), pl.BlockSpec((3, CT), lambda j: (0, j)), pl.BlockSpec((1, CT), lambda j: (0, j))],
        out_shape=[S((L, 2 * DFF), BF), S((3, 2 * DFF), f32), S((1, 2 * DFF), f32)],
        scratch_shapes=[pltpu.VMEM((NCT, L, CT), BF), pltpu.VMEM((NCT, 3, CT), f32), pltpu.VMEM((NCT, 1, CT), f32)],
        sem=("arbitrary",), name=f"convact_bwd{layer}", args=[hu, hu, cw, cw, cb, cb, dact])


DILS = (1, 4, 16)
AB = 128
NPAIR = 12


def _rope_tables(pos_ref, invf_ref):
    ang = pos_ref[...].astype(f32) * invf_ref[...]
    lane = lax.broadcasted_iota(jnp.int32, (1, 128), 1) % 64
    cosf = jnp.where(lane < 16, jnp.cos(ang), 1.0)
    sn = jnp.sin(ang)
    s_lo = jnp.where(lane < 8, -sn, 0.0)
    s_hi = jnp.where((lane >= 8) & (lane < 16), sn, 0.0)
    return cosf, s_lo, s_hi


def _rope(t, cosf, s_lo, s_hi):
    return t * cosf + pltpu.roll(t, 120, 1) * s_lo + pltpu.roll(t, 8, 1) * s_hi


def _rope_t(g, cosf, s_lo, s_hi):
    return g * cosf + pltpu.roll(g * s_lo, 8, 1) + pltpu.roll(g * s_hi, 120, 1)


def _att_block(q2, kp, kc, vp, vc, first):
    lane = lax.broadcasted_iota(jnp.int32, (1, 128), 1)
    qi = lax.broadcasted_iota(jnp.int32, (AB, 2 * AB), 0) + AB
    kj = lax.broadcasted_iota(jnp.int32, (AB, 2 * AB), 1)
    back = qi - kj
    valid = (back >= 0) & (back <= AB)
    if first:
        valid = valid & (kj >= AB)
    kk = jnp.concatenate([kp, kc], axis=0)
    vv = jnp.concatenate([vp, vc], axis=0)
    o2 = jnp.zeros((AB, 128), f32)
    lse2 = jnp.zeros((AB, 128), f32)
    for e in range(2):
        hm = ((lane >= 64 * e) & (lane < 64 * (e + 1))).astype(f32)
        s = dot_nt(q2 * (hm * 0.125), kk)
        s = jnp.where(valid, s, -jnp.inf)
        m = jnp.max(s, axis=-1, keepdims=True)
        p = jnp.exp(s - m)
        den = jnp.sum(p, axis=-1, keepdims=True)
        o2 = o2 + dot_nn(p, vv * hm) / den
        lse2 = lse2 + (m + jnp.log(den)) * hm
    return o2, lse2


def _att_blocks(dil):
    m = L // dil
    return [(r * m + n * AB, n == 0) for r in range(dil) for n in range(m // AB)]


def deinterleave(x, dil):
    return x if dil == 1 else x.reshape(L // dil, dil, x.shape[1]).swapaxes(0, 1).reshape(L, x.shape[1])


def attn_fwd(qkv, pos, invf, g, plan=None):
    blocks = _att_blocks(DILS[g])

    def body(q_ref, k_ref, v_ref, pos_ref, invf_ref, o_ref, l_ref, qr, kr):
        cosf, s_lo, s_hi = _rope_tables(pos_ref, invf_ref)
        qr[...] = _rope(q_ref[...], cosf, s_lo, s_hi)
        kr[...] = _rope(k_ref[...], cosf, s_lo, s_hi)
        for off, first in blocks:
            cur, prv = pl.ds(off, AB), pl.ds(off if first else off - AB, AB)
            o2, lse2 = _att_block(qr[cur, :], kr[prv, :], kr[cur, :], v_ref[prv, :], v_ref[cur, :], first)
            o_ref[cur, :] = o2
            l_ref[cur, :] = lse2

    def sec(n):
        return pl.BlockSpec((L, 128), lambda p: (0, p + 4 * n))

    return pcall(
        body, plan, grid=(4,),
        in_specs=[sec(0), sec(1), sec(2), pl.BlockSpec((L, 1), lambda p: (0, 0)), pl.BlockSpec((1, 128), lambda p: (0, 0))],
        out_specs=[sec(0), sec(0)], out_shape=[S((L, 512), f32), S((L, 512), f32)],
        scratch_shapes=[pltpu.VMEM((L, 128), f32), pltpu.VMEM((L, 128), f32)],
        sem=("parallel",), name=f"attn_fwd{g}", args=[qkv, qkv, qkv, pos, invf])


def _att_block_bwd(q2, kp, kc, vp, vc, lse2, do2, dl2, first):
    lane = lax.broadcasted_iota(jnp.int32, (1, 128), 1)
    qi = lax.broadcasted_iota(jnp.int32, (AB, 2 * AB), 0) + AB
    kj = lax.broadcasted_iota(jnp.int32, (AB, 2 * AB), 1)
    back = qi - kj
    valid = (back >= 0) & (back <= AB)
    if first:
        valid = valid & (kj >= AB)
    kk = jnp.concatenate([kp, kc], axis=0)
    vv = jnp.concatenate([vp, vc], axis=0)
    dq2 = jnp.zeros((AB, 128), f32)
    dkk = jnp.zeros((2 * AB, 128), f32)
    dvv = jnp.zeros((2 * AB, 128), f32)
    for e in range(2):
        hb = (lane >= 64 * e) & (lane < 64 * (e + 1))
        hm = hb.astype(f32)
        qs = q2 * (hm * 0.125)
        lse = jnp.max(jnp.where(hb, lse2, -jnp.inf), axis=-1, keepdims=True)
        dls = jnp.sum(dl2 * hm, axis=-1, keepdims=True)
        p = jnp.where(valid, jnp.exp(dot_nt(qs, kk) - lse), 0.0)
        dov = do2 * hm
        dp = dot_nt(dov, vv)
        ds = p * (dp - jnp.sum(p * dp, axis=-1, keepdims=True) + dls)
        dq2 = dq2 + dot_nn(ds, kk) * (hm * 0.125)
        dkk = dkk + dot_tn(ds, qs)
        dvv = dvv + dot_tn(p, dov)
    return dq2, dkk[:AB], dkk[AB:], dvv[:AB], dvv[AB:]


def attn_bwd(qkv, pos, invf, lse, do, dl, g, plan=None):
    blocks = _att_blocks(DILS[g])

    def body(q_ref, k_ref, v_ref, pos_ref, invf_ref, l_ref, do_ref, dl_ref, d_ref, qr, kr, dqr, dkr, dvr):
        cosf, s_lo, s_hi = _rope_tables(pos_ref, invf_ref)
        qr[...] = _rope(q_ref[...], cosf, s_lo, s_hi)
        kr[...] = _rope(k_ref[...], cosf, s_lo, s_hi)
        for off, first in blocks:
            cur, prv = pl.ds(off, AB), pl.ds(off if first else off - AB, AB)
            dq2, dkp, dkc, dvp, dvc = _att_block_bwd(qr[cur, :], kr[prv, :], kr[cur, :], v_ref[prv, :], v_ref[cur, :],
                                                     l_ref[cur, :], do_ref[cur, :], dl_ref[cur, :], first)
            dqr[cur, :] = dq2
            dkr[cur, :] = dkc
            dvr[cur, :] = dvc
            if not first:
                dkr[prv, :] += dkp
                dvr[prv, :] += dvp
        d_ref[0] = _rope_t(dqr[...], cosf, s_lo, s_hi).astype(d_ref.dtype)
        d_ref[1] = _rope_t(dkr[...], cosf, s_lo, s_hi).astype(d_ref.dtype)
        d_ref[2] = dvr[...].astype(d_ref.dtype)

    def sec(n):
        return pl.BlockSpec((L, 128), lambda p: (0, p + 4 * n))

    return pcall(
        body, plan, grid=(4,),
        in_specs=[sec(0), sec(1), sec(2), pl.BlockSpec((L, 1), lambda p: (0, 0)), pl.BlockSpec((1, 128), lambda p: (0, 0)),
                  sec(0), sec(0), sec(0)],
        out_specs=pl.BlockSpec((3, L, 128), lambda p: (0, 0, p)), out_shape=S((3, L, 512), BF),
        scratch_shapes=[pltpu.VMEM((L, 128), f32)] * 5,
        sem=("parallel",), name=f"attn_bwd{g}", args=[qkv, qkv, qkv, pos, invf, lse, do, dl])


def _merge(o0, o1, o2, l0, l1, l2):
    m = jnp.maximum(jnp.maximum(l0, l1), l2)
    e0, e1, e2 = jnp.exp(l0 - m), jnp.exp(l1 - m), jnp.exp(l2 - m)
    return (e0 * o0 + e1 * o1 + e2 * o2) / (e0 + e1 + e2)


def _to_token_major(src_ref, scr, i, dil, slab):
    n = TR // dil
    for r in range(dil):
        rows = pl.ds(pl.multiple_of(r * (L // dil) + i * n, n), n)
        scr[pl.ds(r, n, stride=dil), :] = src_ref[rows, slab * 128:(slab + 1) * 128].astype(f32)
    return scr[...]


def _to_class_major(val, dst_ref, scr, i, dil, slab):
    n = TR // dil
    scr[...] = val
    for r in range(dil):
        rows = pl.ds(pl.multiple_of(r * (L // dil) + i * n, n), n)
        dst_ref[rows, slab * 128:(slab + 1) * 128] = scr[pl.ds(r, n, stride=dil), :].astype(dst_ref.dtype)


def rms_fwd_classes(x, g, name):
    def body(x_ref, g_ref, o_ref, o1_ref, o2_ref, scr):
        i = pl.program_id(0)
        y = _rms(x_ref[...], g_ref[...])
        o_ref[...] = y.astype(o_ref.dtype)
        for s in range(D // 128):
            ys = y[:, s * 128:(s + 1) * 128]
            _to_class_major(ys, o1_ref, scr, i, DILS[1], s)
            _to_class_major(ys, o2_ref, scr, i, DILS[2], s)

    row = pl.BlockSpec((TR, D), lambda i: (i, 0))
    full = pl.BlockSpec((L, D), lambda i: (0, 0))
    return pl.pallas_call(
        body, grid=(L // TR,), in_specs=[row, pl.BlockSpec((1, D), lambda i: (0, 0))], out_specs=[row, full, full],
        out_shape=[S((L, D), BF)] * 3, scratch_shapes=[pltpu.VMEM((TR, 128), f32)],
        compiler_params=_cp(("arbitrary",)), name=name)(x, g)


def rms_bwd_classes(x, g, dy0, dyc, dres, name):
    def body(x_ref, g_ref, dy0_ref, d1_ref, d2_ref, dr_ref, dh_ref, dg_ref, scr, dyf):
        i = pl.program_id(0)
        for s in range(D // 128):
            sl = slice(s * 128, (s + 1) * 128)
            dyf[:, sl] = (dy0_ref[:, sl] + _to_token_major(d1_ref, scr.at[0], i, DILS[1], s)
                          + _to_token_major(d2_ref, scr.at[1], i, DILS[2], s))
        _, vjp = jax.vjp(_rms, x_ref[...], g_ref[...])
        dx, dg = vjp(dyf[...])
        dh_ref[...] = dr_ref[...] + dx

        @pl.when(i == 0)
        def _():
            dg_ref[...] = jnp.zeros_like(dg_ref)

        dg_ref[...] += dg

    row = pl.BlockSpec((TR, D), lambda i: (i, 0))
    vec = pl.BlockSpec((1, D), lambda i: (0, 0))
    full = pl.BlockSpec((L, D), lambda i: (0, 0))
    return pl.pallas_call(
        body, grid=(L // TR,), in_specs=[row, vec, row, full, full, row], out_specs=[row, vec],
        out_shape=[S((L, D), f32), S((1, D), f32)],
        scratch_shapes=[pltpu.VMEM((2, TR, 128), f32), pltpu.VMEM((TR, D), f32)],
        compiler_params=_cp(("arbitrary",)), name=name)(x, g, dy0, dyc[0], dyc[1], dres)


def attn_merge_fwd(o0, l0, oc, lc):
    def body(o0_ref, l0_ref, o1_ref, l1_ref, o2_ref, l2_ref, o_ref, scr):
        i = pl.program_id(0)
        for s in range(4):
            sl = slice(s * 128, (s + 1) * 128)
            o1 = _to_token_major(o1_ref, scr.at[0], i, DILS[1], s)
            l1 = _to_token_major(l1_ref, scr.at[1], i, DILS[1], s)
            o2 = _to_token_major(o2_ref, scr.at[2], i, DILS[2], s)
            l2 = _to_token_major(l2_ref, scr.at[3], i, DILS[2], s)
            o_ref[:, sl] = _merge(o0_ref[:, sl], o1, o2, l0_ref[:, sl], l1, l2).astype(o_ref.dtype)

    blk = pl.BlockSpec((TR, 512), lambda i: (i, 0))
    full = pl.BlockSpec((L, 512), lambda i: (0, 0))
    return pl.pallas_call(
        body, grid=(L // TR,), in_specs=[blk, blk, full, full, full, full], out_specs=blk, out_shape=S((L, 512), BF),
        scratch_shapes=[pltpu.VMEM((4, TR, 128), f32)],
        compiler_params=_cp(("arbitrary",)), name="attn_merge_fwd")(o0, l0, oc[0], lc[0], oc[1], lc[1])


def attn_merge_bwd(o0, l0, oc, lc, do, plan=None):
    def body(o0_ref, l0_ref, o1_ref, l1_ref, o2_ref, l2_ref, g_ref, do0, dl0, do1, dl1, do2, dl2, scr):
        i = pl.program_id(0)
        for s in range(4):
            sl = slice(s * 128, (s + 1) * 128)
            o1 = _to_token_major(o1_ref, scr.at[0], i, DILS[1], s)
            l1 = _to_token_major(l1_ref, scr.at[1], i, DILS[1], s)
            o2 = _to_token_major(o2_ref, scr.at[2], i, DILS[2], s)
            l2 = _to_token_major(l2_ref, scr.at[3], i, DILS[2], s)
            _, vjp = jax.vjp(_merge, o0_ref[:, sl], o1, o2, l0_ref[:, sl], l1, l2)
            g0, g1, g2, h0, h1, h2 = vjp(g_ref[:, sl].astype(f32))
            do0[:, sl] = g0.astype(do0.dtype)
            dl0[:, sl] = h0
            _to_class_major(g1, do1, scr.at[0], i, DILS[1], s)
            _to_class_major(h1, dl1, scr.at[1], i, DILS[1], s)
            _to_class_major(g2, do2, scr.at[2], i, DILS[2], s)
            _to_class_major(h2, dl2, scr.at[3], i, DILS[2], s)

    blk = pl.BlockSpec((TR, 512), lambda i: (i, 0))
    full = pl.BlockSpec((L, 512), lambda i: (0, 0))
    outs = pcall(body, plan, grid=(L // TR,), in_specs=[blk, blk, full, full, full, full, blk],
                 out_specs=[blk, blk, full, full, full, full],
                 out_shape=[S((L, 512), BF), S((L, 512), f32)] * 3, scratch_shapes=[pltpu.VMEM((4, TR, 128), f32)],
                 sem=("arbitrary",), name="attn_merge_bwd", args=[o0, l0, oc[0], lc[0], oc[1], lc[1], do])
    return [outs[0], outs[2], outs[4]], [outs[1], outs[3], outs[5]]


def _invf_lanes():
    half = 8
    inv = ROPE_THETA ** (-np.arange(half, dtype=np.float32) * 2.0 / 16.0)
    lane = np.arange(128) % 64
    return jnp.asarray(np.where(lane < 16, inv[lane % 8], 0.0).astype(np.float32)[None, :])


def hosted(C, host, fn):
    p = C.plan(host) if C is not None else None
    out = fn(p)
    if p is not None:
        C.done(p)
    return out


def _ffn_fwd(h, g_row, W, cb, layer, C):
    hn = rms_fwd(h, g_row, f"rms_ffn{layer}")
    hu = hosted(C, f"ffn_in{layer}", lambda p: matmul(hn, W[("ffn_w_in", layer)], mode="nn", tm=1024, tn=1408, tk=1024,
                                                      plan=p, name=f"ffn_in{layer}"))
    act = hosted(C, f"convact_fwd{layer}", lambda p: convact_fwd(hu, W[("ffn_conv_w", layer)], cb, layer, plan=p))
    h2 = matmul(act, W[("ffn_w_out", layer)], mode="nn", tm=1024, tn=1024, tk=2816, add=h, name=f"ffn_out{layer}")
    return h2, (hn, hu, act)


def _ffn_bwd(dh, h, g_row, W, cb, saved, layer, C, G):
    hn, hu, act = saved
    w_in, w_out = W[("ffn_w_in", layer)], W[("ffn_w_out", layer)]
    dact = matmul(dh, w_out, mode="nt", tm=1024, tn=1408, tk=1024, name=f"ffn_out_dx{layer}")
    G[("ffn_w_out", layer)] = matmul(act, dh, mode="tn", tm=1408, tn=1024, tk=L, out_dtype=BF, name=f"ffn_out_dw{layer}")
    dhu, G[("ffn_conv_w", layer)], g_cb = hosted(
        C, f"convact_bwd{layer}", lambda p: convact_bwd(hu, W[("ffn_conv_w", layer)], cb, dact, layer, plan=p))
    dhn = hosted(C, f"ffn_in_dx{layer}", lambda p: matmul(dhu, w_in, mode="nt", tm=1024, tn=1024, tk=2816, plan=p,
                                                         name=f"ffn_in_dx{layer}"))
    G[("ffn_w_in", layer)] = hosted(C, f"ffn_in_dw{layer}", lambda p: matmul(
        hn, dhu, mode="tn", tm=1024, tn=1408, tk=L, out_dtype=BF, plan=p, name=f"ffn_in_dw{layer}"))
    dh2, g_norm = rms_bwd(h, g_row, [dhn], dh, f"rms_ffn_bwd{layer}")
    return dh2, g_cb, g_norm


def local_step(x, pos, tgt, sm, W, C=None):
    G = C.grads if C is not None else {}
    nm, nf = sm["norm_mix"], sm["norm_ffn"]
    invf = _invf_lanes()
    are = sm["s5_A_re"].reshape(NST, 1)
    aim = sm["s5_A_im"].reshape(NST, 1)
    ldt = sm["s5_log_dt"].reshape(1, 32)
    bre = sm["s5_B_re"].reshape(NST, 16)
    bim = sm["s5_B_im"].reshape(NST, 16)
    cre = jnp.swapaxes(sm["s5_C_re"][0], 1, 2).reshape(NST, 16)
    cim = jnp.swapaxes(sm["s5_C_im"][0], 1, 2).reshape(NST, 16)
    drow = sm["s5_D"].reshape(1, S5W)
    wbr, wbi, wcr, wci, abr, abi = s5_params_fwd(are, aim, ldt, bre, bim, cre, cim)
    hn0 = rms_fwd(x, nm[0:1], "rms_mix0")
    cb3 = sm["ffn_conv_b3"]
    proj = hosted(C, "mix_in", lambda p: matmul(hn0, W[("mix_w_in", 0)], mode="nn", tm=1024, tn=1280, tk=1024, plan=p, name="mix_in"))
    xs_re, xs_im, y5 = hosted(C, "s5_scan_fwd", lambda p: s5_scan_fwd(proj, wbr, wbi, wcr, wci, abr, abi, drow, plan=p))
    oa = s5_glu_fwd(y5, W[("s5_glu_w", 0)], sm["s5_glu_b"])
    ob, ssave = hosted(C, "hgrn_fwd", lambda p: hgrn_fwd(proj, sm["hgrn_gamma"], sm["hgrn_norm"], plan=p))
    cat = jnp.concatenate([oa, ob], axis=1)
    h1 = matmul(cat, W[("mix_w_out", 0)], mode="nn", tm=1024, tn=1024, tk=1024, add=x, name="mix_out")
    h2, ffn0 = _ffn_fwd(h1, nf[0:1], W, cb3, 0, C)
    hn2_g = rms_fwd_classes(h2, nm[1:2], "rms_mix1")
    wqkv = W[("att_w_qkv", 0)]
    pos_g, qkv_g, oc_g, lc_g = [], [], [], []
    for g, dil in enumerate(DILS):
        pos_g.append(deinterleave(pos, dil))
        qkv_g.append(hosted(C, f"att_qkv{g}", lambda p: matmul(
            hn2_g[g], wqkv, mode="nn", tm=1024, tn=512, tk=1024, dims=(L, 1536, D),
            b_spec=pl.BlockSpec((D, 512), lambda i, j, k, g=g: (0, 3 * j + g)), plan=p, name=f"att_qkv{g}")))
        o_c, l_c = hosted(C, f"attn_fwd{g}", lambda p: attn_fwd(qkv_g[g], pos_g[g], invf, g, plan=p))
        oc_g.append(o_c)
        lc_g.append(l_c)
    o = attn_merge_fwd(oc_g[0], lc_g[0], oc_g[1:], lc_g[1:])
    h3 = matmul(o, W[("att_w_o", 0)], mode="nn", tm=1024, tn=1024, tk=512, add=h2, name="att_o")
    h4, ffn1 = _ffn_fwd(h3, nf[1:2], W, cb3, 1, C)
    loss, dh, g_nfinal = loss_head(h4, sm["norm_final"].reshape(1, D), tgt)
    dh, g_cb1, g_nf1 = _ffn_bwd(dh, h3, nf[1:2], W, cb3, ffn1, 1, C, G)
    do = matmul(dh, W[("att_w_o", 0)], mode="nt", tm=1024, tn=512, tk=1024, name="att_o_dx")
    G[("att_w_o", 0)] = matmul(o, dh, mode="tn", tm=512, tn=1024, tk=L, out_dtype=BF, name="att_o_dw")
    do_g, dl_g = hosted(C, "attn_merge_bwd", lambda p: attn_merge_bwd(oc_g[0], lc_g[0], oc_g[1:], lc_g[1:], do, plan=p))
    dhn2_g, gq = [], []
    for g, dil in enumerate(DILS):
        d3 = hosted(C, f"attn_bwd{g}", lambda p: attn_bwd(qkv_g[g], pos_g[g], invf, lc_g[g], do_g[g], dl_g[g], g, plan=p))
        dx = matmul(d3, wqkv, mode="nt", tm=1024, tn=1024, tk=512, dims=(L, D, 1536),
                    a_spec=pl.BlockSpec((None, 1024, 512), lambda i, j, k: (k, i, 0)),
                    b_spec=pl.BlockSpec((D, 512), lambda i, j, k, g=g: (0, 3 * k + g)), name=f"att_qkv_dx{g}")
        dhn2_g.append(dx)
        gq.append(matmul(hn2_g[g], d3, mode="tn", tm=1024, tn=512, tk=L, out_dtype=BF, dims=(D, 1536, L),
                         b_spec=pl.BlockSpec((None, L, 512), lambda i, j, k: (j, k, 0)), name=f"att_qkv_dw{g}"))
    G[("att_w_qkv", 0)] = jnp.concatenate([gq[g][:, 512 * s:512 * (s + 1)] for s in range(3) for g in range(3)], axis=1)
    dh, g_nm1 = rms_bwd_classes(h2, nm[1:2], dhn2_g[0], dhn2_g[1:], dh, "rms_mix_bwd1")
    dh, g_cb0, g_nf0 = _ffn_bwd(dh, h1, nf[0:1], W, cb3, ffn0, 0, C, G)
    dmix = matmul(dh, W[("mix_w_out", 0)], mode="nt", tm=1024, tn=1024, tk=1024, name="mix_out_dx")
    G[("mix_w_out", 0)] = matmul(cat, dh, mode="tn", tm=1024, tn=1024, tk=L, out_dtype=BF, name="mix_out_dw")
    dy5, g_glu_w, g_glu_b = s5_glu_bwd(y5, W[("s5_glu_w", 0)], sm["s5_glu_b"], dmix)
    G[("s5_glu_w", 0)] = g_glu_w.astype(BF)
    du, gwbr, gwbi, gwcr, gwci, gabr, gabi, g_d = hosted(C, "s5_scan_bwd", lambda p: s5_scan_bwd(
        dy5, proj, xs_re, xs_im, wbr, wbi, wcr, wci, abr, abi, drow, plan=p))
    g_are, g_aim, g_ldt, g_bre, g_bim, g_cre, g_cim = s5_params_bwd(are, aim, ldt, bre, bim, cre, cim,
                                                                   (gwbr, gwbi, gwcr, gwci, gabr, gabi))
    dproj, g_gamma, g_hnorm = hosted(C, "hgrn_bwd", lambda p: hgrn_bwd(proj, sm["hgrn_gamma"], sm["hgrn_norm"], ssave, dmix, du,
                                                                       plan=p))
    dhn0 = hosted(C, "mix_in_dx", lambda p: matmul(dproj, W[("mix_w_in", 0)], mode="nt", tm=1024, tn=1024, tk=2560, plan=p,
                                                  name="mix_in_dx"))
    G[("mix_w_in", 0)] = matmul(hn0, dproj, mode="tn", tm=1024, tn=1280, tk=L, out_dtype=BF, name="mix_in_dw")
    gx, g_nm0 = hosted(C, "rms_mix_bwd0", lambda p: rms_bwd(x, nm[0:1], [dhn0], dh, "rms_mix_bwd0", plan=p))
    small = {
        "norm_mix": jnp.concatenate([g_nm0, g_nm1], axis=0), "norm_ffn": jnp.concatenate([g_nf0, g_nf1], axis=0),
        "norm_final": g_nfinal.reshape(D),
        "s5_A_re": g_are.reshape(1, 32, 64), "s5_A_im": g_aim.reshape(1, 32, 64), "s5_log_dt": g_ldt.reshape(1, 32),
        "s5_B_re": g_bre.reshape(1, 32, 64, 16), "s5_B_im": g_bim.reshape(1, 32, 64, 16),
        "s5_C_re": jnp.swapaxes(g_cre.reshape(1, 32, 64, 16), 2, 3), "s5_C_im": jnp.swapaxes(g_cim.reshape(1, 32, 64, 16), 2, 3),
        "s5_D": g_d.reshape(1, 32, 16), "s5_glu_b": g_glu_b, "hgrn_gamma": g_gamma, "hgrn_norm": g_hnorm,
        "ffn_conv_b": jnp.concatenate([g_cb0, g_cb1], axis=0),
    }
    return loss, gx, G, small


BIG = ("mix_w_in", "mix_w_out", "s5_glu_w", "att_w_qkv", "att_w_o", "ffn_w_in", "ffn_w_out", "ffn_conv_w")
SMALL = ("norm_mix", "norm_ffn", "norm_final", "s5_A_re", "s5_A_im", "s5_log_dt", "s5_B_re", "s5_B_im", "s5_C_re", "s5_C_im",
         "s5_D", "s5_glu_b", "hgrn_gamma", "hgrn_norm", "ffn_conv_b")


def cast_bf16(w, name, plan=None):
    nl, r, c = w.shape
    w2 = w.reshape(nl * r, c)
    tr = 256 if (nl * r) % 256 == 0 else nl * r

    def body(w_ref, o_ref):
        o_ref[...] = w_ref[...].astype(BF)

    out = pcall(body, plan, grid=(nl * r // tr,), in_specs=[pl.BlockSpec((tr, c), lambda i: (i, 0))],
                out_specs=pl.BlockSpec((tr, c), lambda i: (i, 0)), out_shape=S((nl * r, c), BF),
                sem=("parallel",), name=name, args=[w2])
    return out.reshape(nl, r, c)


SCHEDULE = {
    "cast_ffn_w_in": [("G", "mix_w_in", 0)],
    "mix_in": [("G", "mix_w_out", 0), ("G", "s5_glu_w", 0)],
    "s5_scan_fwd": [("G", "ffn_w_in", 0, (0, 2))],
    "hgrn_fwd": [("G", "ffn_w_in", 0, (1, 2)), ("G", "ffn_conv_w", 0), ("G", "ffn_conv_w", 1), ("G", "att_w_qkv", 0, (0, 2))],
    "ffn_in0": [("G", "ffn_w_out", 0)],
    "convact_fwd0": [("G", "att_w_qkv", 0, (1, 2))],
    "att_qkv0": [("G", "att_w_o", 0)],
    "attn_fwd0": [("G", "ffn_w_in", 1, (0, 2))],
    "attn_fwd1": [("G", "ffn_w_in", 1, (1, 2))],
    "attn_fwd2": [("G", "ffn_w_out", 1)],
    "convact_bwd1": [("A", "ffn_w_out", 1)],
    "ffn_in_dx1": [("B", "ffn_w_out", 1)],
    "attn_merge_bwd": [("A", "att_w_o", 0), ("A", "ffn_conv_w", 1)],
    "attn_bwd0": [("A", "ffn_w_in", 1, (0, 2))],
    "attn_bwd1": [("A", "ffn_w_in", 1, (1, 2)), ("B", "att_w_o", 0), ("B", "ffn_conv_w", 1)],
    "attn_bwd2": [("B", "ffn_w_in", 1)],
    "convact_bwd0": [("A", "att_w_qkv", 0, (0, 2))],
    "ffn_in_dw0": [("A", "att_w_qkv", 0, (1, 2))],
    "s5_scan_bwd": [("A", "ffn_w_out", 0), ("A", "mix_w_out", 0), ("A", "s5_glu_w", 0), ("A", "ffn_conv_w", 0),
                    ("B", "att_w_qkv", 0)],
    "hgrn_bwd": [("A", "ffn_w_in", 0), ("B", "ffn_w_out", 0), ("B", "mix_w_out", 0), ("B", "s5_glu_w", 0), ("B", "ffn_conv_w", 0)],
    "mix_in_dx": [("B", "ffn_w_in", 0)],
    "adam_ffn_w_in": [("A", "mix_w_in", 0, (0, 2)), ("A", "small", 0)],
    "adam_ffn_w_out": [("A", "mix_w_in", 0, (1, 2))],
    "adam_att_w_qkv": [("B", "mix_w_in", 0), ("B", "small", 0)],
}


class Comm:
    def __init__(self, shards, shapes):
        self.shards, self.shapes = shards, shapes
        self.W, self.grads, self.slots = {}, {}, {}
        self.small = None

    def plan(self, host):
        items = SCHEDULE.get(host)
        if not items:
            return None
        p = Plan()
        for it in items:
            kind, name, l = it[:3]
            part, parts = it[3] if len(it) > 3 else (0, 1)
            if name == "small":
                kdst = p.buf("slots:small", arr=self.slots.get("small"), shape=S((8,) + self.small.shape, f32), write=True)
                if kind == "A":
                    ReduceOp(p, p.buf("g:small", arr=self.small), kdst, None, self.small.shape, False, 0, 0, whole=True)
                else:
                    ForwardOp(p, kdst, None, whole=True)
                continue
            nl, R, C_ = self.shapes[name]
            rows = name in ROW_SHARDED
            r0, nr = part * (R // parts), R // parts
            if kind == "G":
                sh = self.shards[name]
                kdst = p.buf(f"W:{name}:{l}", arr=self.W.get((name, l)), shape=S((4 * R, C_) if rows else (R, 4 * C_), sh.dtype),
                             write=True)
                GatherOp(p, p.buf("shard:" + name, arr=sh), kdst, l, self.shapes[name], rows, r0, nr, split=(nr % 32 == 0))
            else:
                g = self.grads[(name, l)]
                kdst = p.buf("slots:" + name, arr=self.slots.get(name), shape=S((8, nl, R, C_), g.dtype), write=True)
                if kind == "A":
                    ReduceOp(p, p.buf(f"g:{name}:{l}", arr=g), kdst, l, self.shapes[name], rows, r0, nr)
                else:
                    ForwardOp(p, kdst, l)
        return p

    def done(self, p):
        for k, arr in p.out.items():
            tag, name = k.split(":")[:2]
            if tag == "W":
                self.W[(name, int(k.split(":")[2]))] = arr
            else:
                self.slots[name] = arr


def _adamw(w, g, m, v):
    m = B1 * m + (1.0 - B1) * g
    v = B2 * v + (1.0 - B2) * jnp.square(g)
    m_hat = m / (1.0 - B1 ** STEP)
    v_hat = v / (1.0 - B2 ** STEP)
    return -LR * (m_hat / (jnp.sqrt(v_hat) + AEPS) + WD * w), m, v


def adam_big(w, m, v, slots, name, plan=None):
    nl, R, C = w.shape
    tr = 128 if R % 128 == 0 else (64 if R % 64 == 0 else R)

    def body(w_ref, m_ref, v_ref, s_ref, g_ref, d_ref, nm_ref, nv_ref):
        g = s_ref[0].astype(f32)
        for s in range(1, 8):
            g = g + s_ref[s].astype(f32)
        d, nm_, nv_ = _adamw(w_ref[...], g, m_ref[...], v_ref[...])
        g_ref[...] = g
        d_ref[...] = d
        nm_ref[...] = nm_
        nv_ref[...] = nv_

    blk = pl.BlockSpec((None, tr, C), lambda l, i: (l, i, 0))
    return pcall(body, plan, grid=(nl, R // tr),
                 in_specs=[blk, blk, blk, pl.BlockSpec((8, None, tr, C), lambda l, i: (0, l, i, 0))],
                 out_specs=[blk] * 4, out_shape=[S((nl, R, C), f32)] * 4,
                 sem=("parallel", "parallel"), name=name, args=[w, m, v, slots])


def adam_small(w, m, v, slots):
    R = w.shape[0]
    tr = 256

    def body(w_ref, m_ref, v_ref, s_ref, g_ref, d_ref, nm_ref, nv_ref):
        g = s_ref[0]
        for s in range(1, 8):
            g = g + s_ref[s]
        d, nm_, nv_ = _adamw(w_ref[...], g, m_ref[...], v_ref[...])
        g_ref[...] = g
        d_ref[...] = d
        nm_ref[...] = nm_
        nv_ref[...] = nv_

    blk = pl.BlockSpec((tr, 128), lambda i: (i, 0))
    return pl.pallas_call(
        body, grid=(R // tr,), in_specs=[blk, blk, blk, pl.BlockSpec((8, tr, 128), lambda i: (0, i, 0))],
        out_specs=[blk] * 4, out_shape=[S((R, 128), f32)] * 4,
        compiler_params=_cp(("parallel",)), name="adam_small")(w, m, v, slots)


def _pack(d):
    flat = jnp.concatenate([d[n].reshape(-1) for n in SMALL])
    n = flat.shape[0]
    rows = -(-n // (256 * 128)) * 256
    return jnp.pad(flat, (0, rows * 128 - n)).reshape(rows, 128)


def _unpack(p, like):
    flat = p.reshape(-1)
    out, off = {}, 0
    for n in SMALL:
        sz = math.prod(like[n].shape)
        out[n] = flat[off:off + sz].reshape(like[n].shape)
        off += sz
    return out


def kernel(x, positions, norm_mix, norm_ffn, norm_final, mix_w_in, mix_w_out, s5_A_re, s5_A_im, s5_log_dt, s5_B_re, s5_B_im, s5_C_re, s5_C_im, s5_D, s5_glu_w, s5_glu_b, hgrn_gamma, hgrn_norm, att_w_qkv, att_w_o, ffn_w_in, ffn_conv_w, ffn_conv_b, ffn_w_out, loss_target, m_norm_mix, m_norm_ffn, m_norm_final, m_mix_w_in, m_mix_w_out, m_s5_A_re, m_s5_A_im, m_s5_log_dt, m_s5_B_re, m_s5_B_im, m_s5_C_re, m_s5_C_im, m_s5_D, m_s5_glu_w, m_s5_glu_b, m_hgrn_gamma, m_hgrn_norm, m_att_w_qkv, m_att_w_o, m_ffn_w_in, m_ffn_conv_w, m_ffn_conv_b, m_ffn_w_out, v_norm_mix, v_norm_ffn, v_norm_final, v_mix_w_in, v_mix_w_out, v_s5_A_re, v_s5_A_im, v_s5_log_dt, v_s5_B_re, v_s5_B_im, v_s5_C_re, v_s5_C_im, v_s5_D, v_s5_glu_w, v_s5_glu_b, v_hgrn_gamma, v_hgrn_norm, v_att_w_qkv, v_att_w_o, v_ffn_w_in, v_ffn_conv_w, v_ffn_conv_b, v_ffn_w_out):
    a = dict(locals())
    weights = BIG + SMALL
    w = {n: a[n] for n in weights}
    m = {n: a["m_" + n] for n in weights}
    v = {n: a["v_" + n] for n in weights}
    shards = {"ffn_conv_w": ffn_conv_w}
    C = Comm(shards, {n: w[n].shape for n in BIG})
    for n in ("mix_w_in", "ffn_w_in", "mix_w_out", "s5_glu_w", "ffn_w_out", "att_w_qkv", "att_w_o"):
        shards[n] = hosted(C, "cast_" + n, lambda p: cast_bf16(w[n], "cast_" + n, plan=p))
    sm = {n: w[n] for n in SMALL}
    sm["ffn_conv_b3"] = ffn_conv_b.reshape(2, 1, 2 * DFF)
    loss, gx, _, gsmall = local_step(x[0], positions.reshape(L, 1), loss_target[0], sm, C.W, C)
    C.small = _pack(gsmall)
    res = {}
    for n in ("ffn_w_in", "ffn_w_out", "att_w_qkv", "att_w_o", "mix_w_out", "s5_glu_w", "ffn_conv_w", "mix_w_in"):
        res[n] = hosted(C, "adam_" + n, lambda p: adam_big(w[n], m[n], v[n], C.slots[n], "adam_" + n, plan=p))
    packed = adam_small(_pack({n: w[n] for n in SMALL}), _pack({n: m[n] for n in SMALL}), _pack({n: v[n] for n in SMALL}),
                        C.slots["small"])
    small_out = [_unpack(p, {n: w[n] for n in SMALL}) for p in packed]
    for n in SMALL:
        res[n] = tuple(so[n] for so in small_out)
    total = lax.psum(loss[0, 0], ("x", "y", "c"))
    order = ("norm_mix", "norm_ffn", "norm_final", "mix_w_in", "mix_w_out", "s5_A_re", "s5_A_im", "s5_log_dt", "s5_B_re", "s5_B_im",
             "s5_C_re", "s5_C_im", "s5_D", "s5_glu_w", "s5_glu_b", "hgrn_gamma", "hgrn_norm", "att_w_qkv", "att_w_o", "ffn_w_in",
             "ffn_conv_w", "ffn_conv_b", "ffn_w_out")
    return (total, gx[None], *[res[n][0] for n in order], *[res[n][1] for n in order], *[res[n][2] for n in order],
            *[res[n][3] for n in order])
```

```python
import functools
import math

import numpy as np
import jax
import jax.numpy as jnp
from jax import lax
from jax.experimental import pallas as pl
from jax.experimental.pallas import tpu as pltpu

f32 = jnp.float32
BF = jnp.bfloat16
HI = lax.Precision.HIGHEST
S = jax.ShapeDtypeStruct
MESH = pl.DeviceIdType.MESH

L = 2048
D = 1024
EPS = 1e-6
S5W = 512
NST = 2048
HGC = 64
DFF = 2816
ROPE_THETA = 500000.0
LR, B1, B2, AEPS, WD, STEP = 0.001, 0.9, 0.999, 1e-08, 0.01, 10
VMEM_LIMIT = 56 * 1024 * 1024


def _cp(sem=None):
    return pltpu.CompilerParams(dimension_semantics=sem, vmem_limit_bytes=VMEM_LIMIT)


ANY = pl.BlockSpec(memory_space=pl.ANY)
ROW_SHARDED = ("mix_w_out", "s5_glu_w", "ffn_w_out")


def _coords():
    x, y, c = lax.axis_index("x"), lax.axis_index("y"), lax.axis_index("c")
    return x, y, c, 2 * x + y, [(1 - x, y), (x, 1 - y), (1 - x, 1 - y)]


def _rows(start, n):
    return pl.ds(start if isinstance(start, int) else pl.multiple_of(start, 8), n)


def _cols(q, n):
    return pl.ds(pl.multiple_of(q * n, 128), n)


class Plan:
    def __init__(self):
        self.bufs, self.ops, self.nsem, self.out = {}, [], 0, {}

    def buf(self, key, arr=None, shape=None, write=False):
        b = self.bufs.setdefault(key, dict(arr=arr, shape=shape, write=False))
        b["write"] = b["write"] or write
        return key

    def add(self, op):
        op.base = self.nsem
        self.nsem += op.nsem
        self.ops.append(op)


class GatherOp:
    nsem = 13

    def __init__(self, plan, ksrc, kdst, l, shard_shape, rows, r0, nr, split):
        self.ksrc, self.kdst, self.l, (_, self.R, self.C), self.rows, self.r0, self.nr, self.split = (
            ksrc, kdst, l, shard_shape, rows, r0, nr, split)
        self.h = nr // 2 if split else nr
        plan.add(self)

    def _dst(self, R_, q, start, n):
        if self.rows:
            return R_[self.kdst].at[_rows(q * self.R + start, n), :]
        return R_[self.kdst].at[_rows(start, n), _cols(q, self.C)]

    def _mine(self, c):
        return self.r0 + (c * self.h if self.split else 0)

    def _theirs(self, c):
        return self.r0 + ((1 - c) * self.h if self.split else 0)

    def _copies(self, R_, sems):
        x, y, c, me, others = _coords()
        src = R_[self.ksrc]
        local = pltpu.make_async_copy(src.at[self.l, _rows(self.r0, self.nr), :], self._dst(R_, me, self.r0, self.nr),
                                      sems.at[self.base + 12])
        send, fwd = [], []
        for k, (px, py) in enumerate(others):
            q = 2 * px + py
            send.append((
                pltpu.make_async_remote_copy(src.at[self.l, _rows(self._mine(c), self.h), :], self._dst(R_, me, self._mine(c), self.h),
                                             sems.at[self.base + k], sems.at[self.base + 3 + k], device_id=(px, py, c), device_id_type=MESH),
                pltpu.make_async_remote_copy(src.at[self.l, _rows(self._mine(c), self.h), :], self._dst(R_, q, self._mine(c), self.h),
                                             sems.at[self.base + k], sems.at[self.base + 3 + k], device_id=(px, py, c), device_id_type=MESH)))
            fwd.append((
                pltpu.make_async_remote_copy(self._dst(R_, q, self._mine(c), self.h), self._dst(R_, q, self._mine(c), self.h),
                                             sems.at[self.base + 6 + k], sems.at[self.base + 9 + k], device_id=(x, y, 1 - c), device_id_type=MESH),
                pltpu.make_async_remote_copy(self._dst(R_, q, self._theirs(c), self.h), self._dst(R_, q, self._theirs(c), self.h),
                                             sems.at[self.base + 6 + k], sems.at[self.base + 9 + k], device_id=(x, y, 1 - c), device_id_type=MESH)))
        return local, send, fwd

    def start(self, R_, sems):
        local, send, _ = self._copies(R_, sems)
        local.start()
        for out, _ in send:
            out.start()

    def finish(self, R_, sems):
        local, send, fwd = self._copies(R_, sems)
        for k in range(3):
            send[k][1].wait_recv()
            if self.split:
                fwd[k][0].start()
        for k in range(3):
            if self.split:
                fwd[k][1].wait_recv()
                fwd[k][0].wait_send()
            send[k][0].wait_send()
        local.wait()


class ReduceOp:
    nsem = 7

    def __init__(self, plan, ksrc, kdst, l, shard_shape, rows, r0, nr, whole=False):
        self.ksrc, self.kdst, self.l, (self.R, self.C), self.rows, self.r0, self.nr, self.whole = (
            ksrc, kdst, l, shard_shape[-2:], rows, r0, nr, whole)
        plan.add(self)

    def _piece(self, R_, q):
        g = R_[self.ksrc]
        if self.whole:
            return g
        if self.rows:
            return g.at[_rows(q * self.R + self.r0, self.nr), :]
        return g.at[_rows(self.r0, self.nr), _cols(q, self.C)]

    def _slot(self, R_, s):
        if self.whole:
            return R_[self.kdst].at[s]
        return R_[self.kdst].at[s, self.l, _rows(self.r0, self.nr), :]

    def _copies(self, R_, sems):
        x, y, c, me, others = _coords()
        local = pltpu.make_async_copy(self._piece(R_, me), self._slot(R_, 2 * me + c), sems.at[self.base + 6])
        send = []
        for k, (px, py) in enumerate(others):
            q = 2 * px + py
            send.append((
                pltpu.make_async_remote_copy(self._piece(R_, q), self._slot(R_, 2 * me + c), sems.at[self.base + k],
                                             sems.at[self.base + 3 + k], device_id=(px, py, c), device_id_type=MESH),
                pltpu.make_async_remote_copy(self._piece(R_, q), self._slot(R_, 2 * q + c), sems.at[self.base + k],
                                             sems.at[self.base + 3 + k], device_id=(px, py, c), device_id_type=MESH)))
        return local, send

    def start(self, R_, sems):
        local, send = self._copies(R_, sems)
        local.start()
        for out, _ in send:
            out.start()

    def finish(self, R_, sems):
        local, send = self._copies(R_, sems)
        local.wait()
        for out, inn in send:
            inn.wait_recv()
            out.wait_send()


class ForwardOp:
    nsem = 8

    def __init__(self, plan, kdst, l, whole=False):
        self.kdst, self.l, self.whole = kdst, l, whole
        plan.add(self)

    def _slot(self, R_, s):
        return R_[self.kdst].at[s] if self.whole else R_[self.kdst].at[s, self.l]

    def _copies(self, R_, sems):
        x, y, c, me, others = _coords()
        return [(pltpu.make_async_remote_copy(self._slot(R_, 2 * q + c), self._slot(R_, 2 * q + c), sems.at[self.base + q],
                                              sems.at[self.base + 4 + q], device_id=(x, y, 1 - c), device_id_type=MESH),
                 pltpu.make_async_remote_copy(self._slot(R_, 2 * q + 1 - c), self._slot(R_, 2 * q + 1 - c), sems.at[self.base + q],
                                              sems.at[self.base + 4 + q], device_id=(x, y, 1 - c), device_id_type=MESH))
                for q in range(4)]

    def start(self, R_, sems):
        for out, _ in self._copies(R_, sems):
            out.start()

    def finish(self, R_, sems):
        for out, inn in self._copies(R_, sems):
            inn.wait_recv()
            out.wait_send()


def pcall(body, plan, *, grid, in_specs, out_specs, out_shape, scratch_shapes=(), sem, name, args):
    multi = isinstance(out_shape, (list, tuple))
    if plan is None or not plan.ops:
        return pl.pallas_call(body, grid=grid, in_specs=in_specs, out_specs=out_specs, out_shape=out_shape,
                              scratch_shapes=list(scratch_shapes), compiler_params=_cp(sem), name=name)(*args)
    outs = list(out_shape) if multi else [out_shape]
    ospecs = list(out_specs) if multi else [out_specs]
    kin = [k for k, b in plan.bufs.items() if b["arr"] is not None]
    kout = [k for k, b in plan.bufs.items() if b["write"]]
    n_in, n_out, n_scr = len(in_specs), len(outs), len(scratch_shapes)

    def wrapped(*refs):
        o0 = n_in + len(kin)
        s0 = o0 + n_out + len(kout)
        R_ = dict(zip(kin, refs[n_in:o0]))
        R_.update(zip(kout, refs[o0 + n_out:s0]))
        sems = refs[s0 + n_scr]
        first = functools.reduce(jnp.logical_and, [pl.program_id(d) == 0 for d in range(len(grid))])
        last = functools.reduce(jnp.logical_and, [pl.program_id(d) == grid[d] - 1 for d in range(len(grid))])

        @pl.when(first)
        def _():
            for op in plan.ops:
                op.start(R_, sems)

        body(*refs[:n_in], *refs[o0:o0 + n_out], *refs[s0:s0 + n_scr])

        @pl.when(last)
        def _():
            for op in plan.ops:
                op.finish(R_, sems)

    def shape_of(k):
        b = plan.bufs[k]
        return S(b["arr"].shape, b["arr"].dtype) if b["arr"] is not None else b["shape"]

    res = pl.pallas_call(
        wrapped, grid=grid, in_specs=list(in_specs) + [ANY] * len(kin), out_specs=ospecs + [ANY] * len(kout),
        out_shape=outs + [shape_of(k) for k in kout],
        scratch_shapes=list(scratch_shapes) + [pltpu.SemaphoreType.DMA((plan.nsem,))],
        input_output_aliases={n_in + kin.index(k): n_out + kout.index(k) for k in kout if plan.bufs[k]["arr"] is not None},
        compiler_params=pltpu.CompilerParams(dimension_semantics=("arbitrary",) * len(grid), vmem_limit_bytes=VMEM_LIMIT,
                                             has_side_effects=True),
        name=name)(*args, *[plan.bufs[k]["arr"] for k in kin])
    plan.out = dict(zip(kout, res[n_out:]))
    return list(res[:n_out]) if multi else res[0]


def _dg(a, b, ca, cb):
    return lax.dot_general(a.astype(BF), b.astype(BF), (((ca,), (cb,)), ((), ())), preferred_element_type=f32)


@jax.custom_vjp
def dot_nn(a, b):
    return _dg(a, b, 1, 0)


@jax.custom_vjp
def dot_nt(a, b):
    return _dg(a, b, 1, 1)


@jax.custom_vjp
def dot_tn(a, b):
    return _dg(a, b, 0, 0)


dot_nn.defvjp(lambda a, b: (dot_nn(a, b), (a, b)),
              lambda r, g: (dot_nt(g, r[1]).astype(r[0].dtype), dot_tn(r[0], g).astype(r[1].dtype)))
dot_nt.defvjp(lambda a, b: (dot_nt(a, b), (a, b)),
              lambda r, g: (dot_nn(g, r[1]).astype(r[0].dtype), dot_tn(g, r[0]).astype(r[1].dtype)))
dot_tn.defvjp(lambda a, b: (dot_tn(a, b), (a, b)),
              lambda r, g: (dot_nt(r[1], g).astype(r[0].dtype), dot_nn(r[0], g).astype(r[1].dtype)))


def matmul(a, b, *, mode, tm, tn, tk, out_dtype=f32, add=None, b_lead=None, a_spec=None, b_spec=None, dims=None, plan=None, name):
    a_over, b_over = a_spec, b_spec
    if mode == "nn":
        (M, K), N = a.shape[-2:], b.shape[-1]
        a_spec = pl.BlockSpec((tm, tk), lambda i, j, k: (i, k))
        b_blk, b_idx, ca, cb = (tk, tn), (lambda i, j, k: (k, j)), 1, 0
    elif mode == "nt":
        (M, K), N = a.shape[-2:], b.shape[-2]
        a_spec = pl.BlockSpec((tm, tk), lambda i, j, k: (i, k))
        b_blk, b_idx, ca, cb = (tn, tk), (lambda i, j, k: (j, k)), 1, 1
    else:
        (K, M), N = a.shape[-2:], b.shape[-1]
        a_spec = pl.BlockSpec((tk, tm), lambda i, j, k: (k, i))
        b_blk, b_idx, ca, cb = (tk, tn), (lambda i, j, k: (k, j)), 0, 0
    if dims is not None:
        M, N, K = dims
    assert M % tm == 0 and N % tn == 0 and K % tk == 0, (name, M, N, K, tm, tn, tk)
    if b_lead is None:
        b_spec = pl.BlockSpec(b_blk, b_idx)
    else:
        b_spec = pl.BlockSpec((None,) + b_blk, lambda i, j, k: (b_lead,) + b_idx(i, j, k))
    if a_over is not None:
        a_spec = a_over
    if b_over is not None:
        b_spec = b_over
    nk = K // tk
    has_add = add is not None

    def body(*refs):
        a_ref, b_ref = refs[0], refs[1]
        add_ref = refs[2] if has_add else None
        o_ref = refs[2 + has_add]
        p = _dg(a_ref[...], b_ref[...], ca, cb)

        def fin(v):
            if has_add:
                v = v + add_ref[...].astype(f32)
            o_ref[...] = v.astype(o_ref.dtype)

        if nk == 1:
            fin(p)
        else:
            acc = refs[3 + has_add]
            k = pl.program_id(2)

            @pl.when(k == 0)
            def _():
                acc[...] = p

            @pl.when(k > 0)
            def _():
                acc[...] += p

            @pl.when(k == nk - 1)
            def _():
                fin(acc[...])

    in_specs = [a_spec, b_spec]
    args = [a, b]
    if has_add:
        in_specs.append(pl.BlockSpec((tm, tn), lambda i, j, k: (i, j)))
        args.append(add)
    return pcall(body, plan, grid=(M // tm, N // tn, nk), in_specs=in_specs,
                 out_specs=pl.BlockSpec((tm, tn), lambda i, j, k: (i, j)), out_shape=S((M, N), out_dtype),
                 scratch_shapes=[pltpu.VMEM((tm, tn), f32)] if nk > 1 else [],
                 sem=("parallel", "parallel", "arbitrary"), name=name, args=args)


def _rms(xv, gv):
    return xv * lax.rsqrt(jnp.mean(xv * xv, axis=-1, keepdims=True) + EPS) * gv


TR = 256


def rms_fwd(x, g, name):
    def body(x_ref, g_ref, o_ref):
        o_ref[...] = _rms(x_ref[...], g_ref[...]).astype(o_ref.dtype)

    return pl.pallas_call(
        body, grid=(L // TR,),
        in_specs=[pl.BlockSpec((TR, D), lambda i: (i, 0)), pl.BlockSpec((1, D), lambda i: (0, 0))],
        out_specs=pl.BlockSpec((TR, D), lambda i: (i, 0)), out_shape=S((L, D), BF),
        compiler_params=_cp(("parallel",)), name=name)(x, g)


def rms_bwd(x, g, dys, dres, name, plan=None):
    nd = len(dys)

    def body(*refs):
        x_ref, g_ref = refs[0], refs[1]
        dr_ref, dh_ref, dg_ref = refs[2 + nd:]
        dy = refs[2][...].astype(f32)
        for r in refs[3:2 + nd]:
            dy = dy + r[...].astype(f32)
        _, vjp = jax.vjp(_rms, x_ref[...], g_ref[...])
        dx, dg = vjp(dy)
        dh_ref[...] = dr_ref[...] + dx

        @pl.when(pl.program_id(0) == 0)
        def _():
            dg_ref[...] = jnp.zeros_like(dg_ref)

        dg_ref[...] += dg

    row = pl.BlockSpec((TR, D), lambda i: (i, 0))
    vec = pl.BlockSpec((1, D), lambda i: (0, 0))
    return pcall(body, plan, grid=(L // TR,), in_specs=[row, vec] + [row] * (nd + 1), out_specs=[row, vec],
                 out_shape=[S((L, D), f32), S((1, D), f32)], sem=("arbitrary",), name=name, args=[x, g, *dys, dres])


def loss_head(h, g, tgt):
    def f(hv, gv, tv):
        y = _rms(hv, gv)
        return 0.5 * jnp.sum(jnp.mean(jnp.square(y - tv), axis=-1))

    def body(h_ref, g_ref, t_ref, l_ref, dh_ref, dg_ref):
        val, vjp = jax.vjp(f, h_ref[...], g_ref[...], t_ref[...])
        dh, dg, _ = vjp(jnp.ones((), f32))
        dh_ref[...] = dh

        @pl.when(pl.program_id(0) == 0)
        def _():
            dg_ref[...] = jnp.zeros_like(dg_ref)
            l_ref[...] = jnp.zeros_like(l_ref)

        dg_ref[...] += dg
        l_ref[...] += jnp.full((1, 128), val, f32)

    row = pl.BlockSpec((TR, D), lambda i: (i, 0))
    vec = pl.BlockSpec((1, D), lambda i: (0, 0))
    return pl.pallas_call(
        body, grid=(L // TR,), in_specs=[row, vec, row],
        out_specs=[pl.BlockSpec((1, 128), lambda i: (0, 0)), row, vec],
        out_shape=[S((1, 128), f32), S((L, D), f32), S((1, D), f32)],
        compiler_params=_cp(("arbitrary",)), name="loss_head")(h, g, tgt)


def _col_to_row(c):
    n = c.shape[0]
    t = jnp.broadcast_to(c, (n, 128)).T
    r = lax.broadcasted_iota(jnp.int32, (128, n), 0)
    return jnp.sum(jnp.where(r == 0, t, 0.0), axis=0, keepdims=True)


def _s5_param_map(are, aim, ldt_row, bre, bim, cre, cim):
    n = NST
    gi = lax.broadcasted_iota(jnp.int32, (n, 32), 0) // 64
    gj = lax.broadcasted_iota(jnp.int32, (n, 32), 1)
    ldt = jnp.sum(jnp.where(gi == gj, ldt_row, 0.0), axis=1, keepdims=True)
    dt = jnp.exp(ldt)
    mag = jnp.exp(are * dt)
    abr = mag * jnp.cos(aim * dt)
    abi = mag * jnp.sin(aim * dt)
    den = are * are + aim * aim
    nr, ni = abr - 1.0, abi
    cr = (nr * are + ni * aim) / den
    ci = (ni * are - nr * aim) / den
    bbr = cr * bre - ci * bim
    bbi = cr * bim + ci * bre
    tc = lax.broadcasted_iota(jnp.int32, (16, 128), 0)
    tl = lax.broadcasted_iota(jnp.int32, (16, 128), 1)
    T = (tl % 16 == tc).astype(f32)
    mr = (lax.broadcasted_iota(jnp.int32, (n, 128), 0) // 64) % 8
    mc = lax.broadcasted_iota(jnp.int32, (n, 128), 1) // 16
    mask = (mr == mc).astype(f32)

    def expand(v):
        return jnp.dot(v, T, precision=HI, preferred_element_type=f32) * mask

    return expand(bbr), expand(bbi), expand(cre), expand(cim), _col_to_row(abr), _col_to_row(abi)


def s5_params_fwd(are, aim, ldt_row, bre, bim, cre, cim):
    def body(*refs):
        outs = _s5_param_map(*[r[...] for r in refs[:7]])
        for o_ref, o in zip(refs[7:], outs):
            o_ref[...] = o

    return pl.pallas_call(
        body, out_shape=[S((NST, 128), f32)] * 4 + [S((1, NST), f32)] * 2,
        compiler_params=_cp(), name="s5_params_fwd")(are, aim, ldt_row, bre, bim, cre, cim)


def s5_params_bwd(are, aim, ldt_row, bre, bim, cre, cim, cots):
    def body(*refs):
        _, vjp = jax.vjp(_s5_param_map, *[r[...] for r in refs[:7]])
        gs = vjp(tuple(r[...] for r in refs[7:13]))
        for o_ref, o in zip(refs[13:], gs):
            o_ref[...] = o

    return pl.pallas_call(
        body, out_shape=[S((NST, 1), f32)] * 2 + [S((1, 32), f32)] + [S((NST, 16), f32)] * 4,
        compiler_params=_cp(), name="s5_params_bwd")(are, aim, ldt_row, bre, bim, cre, cim, *cots)


def _cpowers(ar, ai):
    out = [(ar, ai)]
    for _ in range(7):
        pr, pi = out[-1]
        out.append((pr * ar - pi * ai, pr * ai + pi * ar))
    return out


def _ctable(pw, rid, power):
    tr_ = jnp.zeros(rid.shape, f32)
    ti_ = jnp.zeros(rid.shape, f32)
    for r in range(8):
        pr, pi = pw[power(r) - 1]
        tr_ = jnp.where(rid == r, pr, tr_)
        ti_ = jnp.where(rid == r, pi, ti_)
    return tr_, ti_


NT5 = 4
RC = 256


def s5_scan_fwd(proj, wbr, wbi, wcr, wci, abr, abi, drow, plan=None):
    def body(u_ref, wbr_ref, wbi_ref, wcr_ref, wci_ref, ar_ref, ai_ref, d_ref, xr_ref, xi_ref, y_ref):
        wbr_v, wbi_v = wbr_ref[...], wbi_ref[...]
        for r in range(L // RC):
            rows = pl.ds(r * RC, RC)
            ub = u_ref[rows, :]
            xr_ref[rows, :] = dot_nt(ub, wbr_v)
            xi_ref[rows, :] = dot_nt(ub, wbi_v)
        pw = _cpowers(ar_ref[...], ai_ref[...])
        rid = lax.broadcasted_iota(jnp.int32, (8, 512), 0)
        tr_, ti_ = _ctable(pw, rid, lambda r: r + 1)

        def group(j, c):
            cr, ci = c
            rows = pl.ds(pl.multiple_of(j * 8, 8), 8)
            br, bi = xr_ref[rows, :], xi_ref[rows, :]
            for s in (1, 2, 4):
                pr, pi = pw[s - 1]
                sr = jnp.where(rid >= s, pltpu.roll(br, s, 0), 0.0)
                si = jnp.where(rid >= s, pltpu.roll(bi, s, 0), 0.0)
                br, bi = br + pr * sr - pi * si, bi + pr * si + pi * sr
            br, bi = br + tr_ * cr - ti_ * ci, bi + tr_ * ci + ti_ * cr
            xr_ref[rows, :] = br
            xi_ref[rows, :] = bi
            return br[7:8], bi[7:8]

        z = jnp.zeros((1, 512), f32)
        lax.fori_loop(0, L // 8, group, (z, z), unroll=2)
        wcr_v, wci_v, dv = wcr_ref[...], wci_ref[...], d_ref[...]
        for r in range(L // RC):
            rows = pl.ds(r * RC, RC)
            y_ref[rows, :] = (dot_nn(xr_ref[rows, :], wcr_v) - dot_nn(xi_ref[rows, :], wci_v)
                              + dv * u_ref[rows, :])

    wspec = pl.BlockSpec((512, 128), lambda j: (j, 0))
    aspec = pl.BlockSpec((1, 512), lambda j: (0, j))
    return pcall(
        body, plan, grid=(NT5,),
        in_specs=[pl.BlockSpec((L, 128), lambda j: (0, j)), wspec, wspec, wspec, wspec, aspec, aspec,
                  pl.BlockSpec((1, 128), lambda j: (0, j))],
        out_specs=[pl.BlockSpec((L, 512), lambda j: (0, j)), pl.BlockSpec((L, 512), lambda j: (0, j)),
                   pl.BlockSpec((L, 128), lambda j: (0, j))],
        out_shape=[S((L, NST), f32), S((L, NST), f32), S((L, S5W), f32)],
        sem=("parallel",), name="s5_scan_fwd", args=[proj, wbr, wbi, wcr, wci, abr, abi, drow])


def s5_scan_bwd(dy, proj, xs_re, xs_im, wbr, wbi, wcr, wci, abr, abi, drow, plan=None):
    def body(dy_ref, u_ref, xr_ref, xi_ref, wbr_ref, wbi_ref, wcr_ref, wci_ref, ar_ref, ai_ref, d_ref,
             du_ref, gwbr_ref, gwbi_ref, gwcr_ref, gwci_ref, gar_ref, gai_ref, gd_ref, lr_ref, li_ref):
        wcr_v, wci_v = wcr_ref[...], wci_ref[...]
        gwcr = jnp.zeros((512, 128), f32)
        gwci = jnp.zeros((512, 128), f32)
        gd = jnp.zeros((1, 128), f32)
        for r in range(L // RC):
            rows = pl.ds(r * RC, RC)
            dyv = dy_ref[rows, :]
            lr_ref[rows, :] = dot_nt(dyv, wcr_v)
            li_ref[rows, :] = -dot_nt(dyv, wci_v)
            gwcr += dot_tn(xr_ref[rows, :], dyv)
            gwci -= dot_tn(xi_ref[rows, :], dyv)
            gd += jnp.sum(dyv * u_ref[rows, :], axis=0, keepdims=True)
        gwcr_ref[...] = gwcr
        gwci_ref[...] = gwci
        gd_ref[...] = gd
        pw = _cpowers(ar_ref[...], -ai_ref[...])
        rid = lax.broadcasted_iota(jnp.int32, (8, 512), 0)
        tr_, ti_ = _ctable(pw, rid, lambda r: 8 - r)

        def group(i, c):
            cr, ci, gar, gai = c
            j = L // 8 - 1 - i
            rows = pl.ds(pl.multiple_of(j * 8, 8), 8)
            br, bi = lr_ref[rows, :], li_ref[rows, :]
            for s in (1, 2, 4):
                pr, pi = pw[s - 1]
                sr = jnp.where(rid < 8 - s, pltpu.roll(br, 8 - s, 0), 0.0)
                si = jnp.where(rid < 8 - s, pltpu.roll(bi, 8 - s, 0), 0.0)
                br, bi = br + pr * sr - pi * si, bi + pr * si + pi * sr
            br, bi = br + tr_ * cr - ti_ * ci, bi + tr_ * ci + ti_ * cr
            lr_ref[rows, :] = br
            li_ref[rows, :] = bi
            nr = jnp.where(rid < 7, pltpu.roll(br, 7, 0), cr)
            ni = jnp.where(rid < 7, pltpu.roll(bi, 7, 0), ci)
            xr, xi = xr_ref[rows, :], xi_ref[rows, :]
            return br[0:1], bi[0:1], gar + xr * nr + xi * ni, gai + xr * ni - xi * nr

        z = jnp.zeros((1, 512), f32)
        z8 = jnp.zeros((8, 512), f32)
        _, _, gar, gai = lax.fori_loop(0, L // 8, group, (z, z, z8, z8), unroll=2)
        gar_ref[...] = jnp.sum(gar, axis=0, keepdims=True)
        gai_ref[...] = jnp.sum(gai, axis=0, keepdims=True)
        wbr_v, wbi_v, dv = wbr_ref[...], wbi_ref[...], d_ref[...]
        gwbr = jnp.zeros((512, 128), f32)
        gwbi = jnp.zeros((512, 128), f32)
        for r in range(L // RC):
            rows = pl.ds(r * RC, RC)
            lrv, liv, uv = lr_ref[rows, :], li_ref[rows, :], u_ref[rows, :]
            du_ref[rows, :] = (dot_nn(lrv, wbr_v) + dot_nn(liv, wbi_v) + dv * dy_ref[rows, :]).astype(du_ref.dtype)
            gwbr += dot_tn(lrv, uv)
            gwbi += dot_tn(liv, uv)
        gwbr_ref[...] = gwbr
        gwbi_ref[...] = gwbi

    wspec = pl.BlockSpec((512, 128), lambda j: (j, 0))
    aspec = pl.BlockSpec((1, 512), lambda j: (0, j))
    col = pl.BlockSpec((L, 128), lambda j: (0, j))
    st = pl.BlockSpec((L, 512), lambda j: (0, j))
    dspec = pl.BlockSpec((1, 128), lambda j: (0, j))
    return pcall(
        body, plan, grid=(NT5,),
        in_specs=[col, col, st, st, wspec, wspec, wspec, wspec, aspec, aspec, dspec],
        out_specs=[col, wspec, wspec, wspec, wspec, aspec, aspec, dspec],
        out_shape=[S((L, S5W), BF)] + [S((NST, 128), f32)] * 4 + [S((1, NST), f32)] * 2 + [S((1, S5W), f32)],
        scratch_shapes=[pltpu.VMEM((L, 512), f32), pltpu.VMEM((L, 512), f32)],
        sem=("parallel",), name="s5_scan_bwd", args=[dy, proj, xs_re, xs_im, wbr, wbi, wcr, wci, abr, abi, drow])


def _glu(y, w, b):
    z = jax.nn.gelu(y)
    return z * jax.nn.sigmoid(dot_nn(z, w) + b)


def s5_glu_fwd(y, w, b):
    def body(y_ref, w_ref, b_ref, o_ref):
        o_ref[...] = _glu(y_ref[...], w_ref[...], b_ref[...]).astype(o_ref.dtype)

    return pl.pallas_call(
        body, grid=(L // TR,),
        in_specs=[pl.BlockSpec((TR, S5W), lambda i: (i, 0)), pl.BlockSpec((S5W, S5W), lambda i: (0, 0)),
                  pl.BlockSpec((1, S5W), lambda i: (0, 0))],
        out_specs=pl.BlockSpec((TR, S5W), lambda i: (i, 0)), out_shape=S((L, S5W), BF),
        compiler_params=_cp(("parallel",)), name="s5_glu_fwd")(y, w, b)


def s5_glu_bwd(y, w, b, dmix):
    def body(y_ref, w_ref, b_ref, g_ref, dy_ref, dw_ref, db_ref):
        _, vjp = jax.vjp(_glu, y_ref[...], w_ref[...].astype(f32), b_ref[...])
        dy, dw, db = vjp(g_ref[...])
        dy_ref[...] = dy

        @pl.when(pl.program_id(0) == 0)
        def _():
            dw_ref[...] = jnp.zeros_like(dw_ref)
            db_ref[...] = jnp.zeros_like(db_ref)

        dw_ref[...] += dw
        db_ref[...] += db

    row = pl.BlockSpec((TR, S5W), lambda i: (i, 0))
    return pl.pallas_call(
        body, grid=(L // TR,),
        in_specs=[row, pl.BlockSpec((S5W, S5W), lambda i: (0, 0)), pl.BlockSpec((1, S5W), lambda i: (0, 0)), row],
        out_specs=[row, pl.BlockSpec((S5W, S5W), lambda i: (0, 0)), pl.BlockSpec((1, S5W), lambda i: (0, 0))],
        out_shape=[S((L, S5W), f32), S((S5W, S5W), f32), S((1, S5W), f32)],
        compiler_params=_cp(("arbitrary",)), name="s5_glu_bwd")(y, w, b, dmix)


def _dg3(a, b, ca, cb):
    ah, bh = a.astype(BF), b.astype(BF)
    al, bl = (a - ah.astype(f32)).astype(BF), (b - bh.astype(f32)).astype(BF)
    return _dg(ah, bh, ca, cb) + _dg(ah, bl, ca, cb) + _dg(al, bh, ca, cb)


@jax.custom_vjp
def hi_nn(a, b):
    return _dg3(a, b, 1, 0)


@jax.custom_vjp
def hi_nt(a, b):
    return _dg3(a, b, 1, 1)


@jax.custom_vjp
def hi_tn(a, b):
    return _dg3(a, b, 0, 0)


hi_nn.defvjp(lambda a, b: (hi_nn(a, b), (a, b)), lambda r, g: (hi_nt(g, r[1]), hi_tn(r[0], g)))
hi_nt.defvjp(lambda a, b: (hi_nt(a, b), (a, b)), lambda r, g: (hi_nn(g, r[1]), hi_tn(g, r[0])))
hi_tn.defvjp(lambda a, b: (hi_tn(a, b), (a, b)), lambda r, g: (hi_nt(r[1], g), hi_nn(r[0], g)))


def _hgrn_chunk(St, xq, xf, xi, xg, gam, ng):
    lb = jax.nn.sigmoid(gam[0:1] - gam[1:2])
    q = jax.nn.silu(xq)
    f = lb + (1.0 - lb) * jax.nn.sigmoid(xf)
    k = 1.0 - f
    g = jnp.log(f)
    ti = lax.broadcasted_iota(jnp.int32, (HGC, HGC), 0)
    si = lax.broadcasted_iota(jnp.int32, (HGC, HGC), 1)
    causal = si <= ti
    b = jnp.dot(causal.astype(f32), g, precision=HI, preferred_element_type=f32)
    qe = q * jnp.exp(b)
    o = dot_nt(qe, St)
    att = jnp.where(causal, hi_nt(qe, k * jnp.exp(-b)), 0.0)
    o = o + dot_nn(att, xi)
    bl = b[HGC - 1:HGC]
    St_new = St * jnp.exp(bl) + dot_tn(xi, k * jnp.exp(bl - b))
    o = o * lax.rsqrt(jnp.mean(o * o, axis=-1, keepdims=True) + EPS) * ng
    return St_new, o * jax.nn.silu(xg)


NCH = L // HGC


def hgrn_fwd(proj, gamma, hnorm, plan=None):
    def body(q_ref, f_ref, i_ref, g_ref, gam_ref, ng_ref, o_ref, ss_ref, st):
        @pl.when(pl.program_id(0) == 0)
        def _():
            st[...] = jnp.zeros_like(st)

        for h in range(4):
            sl = slice(h * 128, (h + 1) * 128)
            s0 = st[h]
            ss_ref[0, h] = s0
            s1, o = _hgrn_chunk(s0, q_ref[:, sl], f_ref[:, sl], i_ref[:, sl], g_ref[:, sl], gam_ref[:, sl], ng_ref[:, sl])
            st[h] = s1
            o_ref[:, sl] = o.astype(o_ref.dtype)

    def pj(n):
        return pl.BlockSpec((HGC, 512), lambda c: (c, n))

    return pcall(
        body, plan, grid=(NCH,),
        in_specs=[pj(1), pj(2), pj(3), pj(4), pl.BlockSpec((2, 512), lambda c: (0, 0)), pl.BlockSpec((1, 512), lambda c: (0, 0))],
        out_specs=[pl.BlockSpec((HGC, 512), lambda c: (c, 0)), pl.BlockSpec((1, 4, 128, 128), lambda c: (c, 0, 0, 0))],
        out_shape=[S((L, 512), BF), S((NCH, 4, 128, 128), f32)],
        scratch_shapes=[pltpu.VMEM((4, 128, 128), f32)],
        sem=("arbitrary",), name="hgrn_fwd", args=[proj, proj, proj, proj, gamma, hnorm])


def hgrn_bwd(proj, gamma, hnorm, ssave, dmix, du, plan=None):
    def body(q_ref, f_ref, i_ref, g_ref, gam_ref, ng_ref, ss_ref, do_ref, du_ref, dp_ref, dgam_ref, dng_ref, dst):
        @pl.when(pl.program_id(0) == 0)
        def _():
            dst[...] = jnp.zeros_like(dst)
            dgam_ref[...] = jnp.zeros_like(dgam_ref)
            dng_ref[...] = jnp.zeros_like(dng_ref)

        dp_ref[:, 0:512] = du_ref[...]
        for h in range(4):
            sl = slice(h * 128, (h + 1) * 128)
            _, vjp = jax.vjp(_hgrn_chunk, ss_ref[0, h], q_ref[:, sl], f_ref[:, sl], i_ref[:, sl], g_ref[:, sl],
                             gam_ref[:, sl], ng_ref[:, sl])
            ds, dq, df, di, dg, dgam, dng = vjp((dst[h], do_ref[:, sl]))
            dst[h] = ds
            for n, v in enumerate((dq, df, di, dg)):
                dp_ref[:, 512 * (n + 1) + h * 128: 512 * (n + 1) + (h + 1) * 128] = v.astype(dp_ref.dtype)
            dgam_ref[:, sl] += dgam
            dng_ref[:, sl] += dng

    def pj(n):
        return pl.BlockSpec((HGC, 512), lambda i: (NCH - 1 - i, n))

    return pcall(
        body, plan, grid=(NCH,),
        in_specs=[pj(1), pj(2), pj(3), pj(4), pl.BlockSpec((2, 512), lambda i: (0, 0)), pl.BlockSpec((1, 512), lambda i: (0, 0)),
                  pl.BlockSpec((1, 4, 128, 128), lambda i: (NCH - 1 - i, 0, 0, 0)), pj(1), pj(0)],
        out_specs=[pl.BlockSpec((HGC, 2560), lambda i: (NCH - 1 - i, 0)), pl.BlockSpec((2, 512), lambda i: (0, 0)),
                   pl.BlockSpec((1, 512), lambda i: (0, 0))],
        out_shape=[S((L, 2560), BF), S((2, 512), f32), S((1, 512), f32)],
        scratch_shapes=[pltpu.VMEM((4, 128, 128), f32)],
        sem=("arbitrary",), name="hgrn_bwd", args=[proj, proj, proj, proj, gamma, hnorm, ssave, dmix, du])


def _earlier(h_ref, k):
    rid = lax.broadcasted_iota(jnp.int32, (8, h_ref.shape[1]), 0)
    head = jnp.where(rid >= k, pltpu.roll(h_ref[pl.ds(0, 8), :], k, 0), 0.0)
    return jnp.concatenate([head, h_ref[pl.ds(8 - k, L - 8), :]], axis=0)


def _later(s_ref, k):
    return s_ref[pl.ds(k, L), :]


def _conv3(h_ref, w, b):
    h1, h2 = _earlier(h_ref, 1), _earlier(h_ref, 2)
    return w[2:3] * h_ref[...] + w[1:2] * h1 + w[0:1] * h2 + b, h1, h2


CT = 128
NCT = DFF // CT


def convact_fwd(hu, cw, cb, layer, plan=None):
    def body(ha_ref, hb_ref, wa_ref, wb_ref, ba_ref, bb_ref, o_ref):
        ca = _conv3(ha_ref, wa_ref[...], ba_ref[...])[0]
        cb_ = _conv3(hb_ref, wb_ref[...], bb_ref[...])[0]
        o_ref[...] = (jax.nn.silu(ca) * cb_).astype(o_ref.dtype)

    def h(off):
        return pl.BlockSpec((L, CT), lambda j: (0, j + off))

    def w(off):
        return pl.BlockSpec((3, CT), lambda j: (0, j + off))

    def b(off):
        return pl.BlockSpec((None, 1, CT), lambda j: (layer, 0, j + off))

    return pcall(body, plan, grid=(NCT,), in_specs=[h(0), h(NCT), w(0), w(NCT), b(0), b(NCT)],
                 out_specs=pl.BlockSpec((L, CT), lambda j: (0, j)), out_shape=S((L, DFF), BF),
                 sem=("parallel",), name=f"convact_fwd{layer}", args=[hu, hu, cw, cw, cb, cb])


def convact_bwd(hu, cw, cb, dact, layer, plan=None):
    def body(ha_ref, hb_ref, wa_ref, wb_ref, ba_ref, bb_ref, g_ref, dh_ref, dw_ref, db_ref, sh, sw, sb, da_scr, db_scr):
        j = pl.program_id(0)

        def taps_bwd(dc, scr, h_ref, h1, h2, w):
            scr[pl.ds(0, L), :] = dc
            scr[pl.ds(L, 8), :] = jnp.zeros((8, CT), f32)
            dh = w[2:3] * dc + w[1:2] * _later(scr, 1) + w[0:1] * _later(scr, 2)
            dw = [jnp.sum(dc * t, axis=0, keepdims=True) for t in (h2, h1, h_ref[...])]
            return dh, dw, jnp.sum(dc, axis=0, keepdims=True)

        @pl.when(j < NCT)
        def _():
            wa, wb = wa_ref[...], wb_ref[...]
            ca, a1, a2 = _conv3(ha_ref, wa, ba_ref[...])
            cb_, b1, b2 = _conv3(hb_ref, wb, bb_ref[...])
            g = g_ref[...].astype(f32)
            sg = jax.nn.sigmoid(ca)
            dca = g * cb_ * (sg * (1.0 + ca * (1.0 - sg)))
            dcb = g * (ca * sg)
            dha, dwa, dba = taps_bwd(dca, da_scr, ha_ref, a1, a2, wa)
            dhb, dwb, dbb = taps_bwd(dcb, db_scr, hb_ref, b1, b2, wb)
            dh_ref[...] = dha.astype(dh_ref.dtype)
            sh[j] = dhb.astype(sh.dtype)
            for k in range(3):
                dw_ref[k:k + 1, :] = dwa[k]
                sw[j, k:k + 1, :] = dwb[k]
            db_ref[...] = dba
            sb[j] = dbb

        @pl.when(j >= NCT)
        def _():
            dh_ref[...] = sh[j - NCT]
            dw_ref[...] = sw[j - NCT]
            db_ref[...] = sb[j - NCT]

    def lo(j):
        return jnp.minimum(j, NCT - 1)

    in_specs = [pl.BlockSpec((L, CT), lambda j: (0, lo(j))), pl.BlockSpec((L, CT), lambda j: (0, lo(j) + NCT)),
                pl.BlockSpec((3, CT), lambda j: (0, lo(j))), pl.BlockSpec((3, CT), lambda j: (0, lo(j) + NCT)),
                pl.BlockSpec((None, 1, CT), lambda j: (layer, 0, lo(j))), pl.BlockSpec((None, 1, CT), lambda j: (layer, 0, lo(j) + NCT)),
                pl.BlockSpec((L, CT), lambda j: (0, lo(j)))]
    return pcall(
        body, plan, grid=(2 * NCT,), in_specs=in_specs,
        out_specs=[pl.BlockSpec((L, CT), lambda j: (0, j)), pl.BlockSpec((3, CT), lambda j: (0, j)), pl.BlockSpec((1, CT), lambda j: (0, j))],
        out_shape=[S((L, 2 * DFF), BF), S((3, 2 * DFF), f32), S((1, 2 * DFF), f32)],
        scratch_shapes=[pltpu.VMEM((NCT, L, CT), BF), pltpu.VMEM((NCT, 3, CT), f32), pltpu.VMEM((NCT, 1, CT), f32),
                        pltpu.VMEM((L + 8, CT), f32), pltpu.VMEM((L + 8, CT), f32)],
        sem=("arbitrary",), name=f"convact_bwd{layer}", args=[hu, hu, cw, cw, cb, cb, dact])


DILS = (1, 4, 16)
AB = 128
NPAIR = 12


def _rope_tables(pos_ref, invf_ref):
    ang = pos_ref[...].astype(f32) * invf_ref[...]
    lane = lax.broadcasted_iota(jnp.int32, (1, 128), 1) % 64
    cosf = jnp.where(lane < 16, jnp.cos(ang), 1.0)
    sn = jnp.sin(ang)
    s_lo = jnp.where(lane < 8, -sn, 0.0)
    s_hi = jnp.where((lane >= 8) & (lane < 16), sn, 0.0)
    return cosf, s_lo, s_hi


def _rope(t, cosf, s_lo, s_hi):
    return t * cosf + pltpu.roll(t, 120, 1) * s_lo + pltpu.roll(t, 8, 1) * s_hi


def _rope_t(g, cosf, s_lo, s_hi):
    return g * cosf + pltpu.roll(g * s_lo, 8, 1) + pltpu.roll(g * s_hi, 120, 1)


def _att_block(q2, kp, kc, vp, vc, first):
    lane = lax.broadcasted_iota(jnp.int32, (1, 128), 1)
    qi = lax.broadcasted_iota(jnp.int32, (AB, 2 * AB), 0) + AB
    kj = lax.broadcasted_iota(jnp.int32, (AB, 2 * AB), 1)
    back = qi - kj
    valid = (back >= 0) & (back <= AB)
    if first:
        valid = valid & (kj >= AB)
    kk = jnp.concatenate([kp, kc], axis=0)
    vv = jnp.concatenate([vp, vc], axis=0)
    o2 = jnp.zeros((AB, 128), f32)
    lse2 = jnp.zeros((AB, 128), f32)
    for e in range(2):
        hm = ((lane >= 64 * e) & (lane < 64 * (e + 1))).astype(f32)
        s = dot_nt(q2 * (hm * 0.125), kk)
        s = jnp.where(valid, s, -jnp.inf)
        m = jnp.max(s, axis=-1, keepdims=True)
        p = jnp.exp(s - m)
        den = jnp.sum(p, axis=-1, keepdims=True)
        o2 = o2 + dot_nn(p, vv * hm) / den
        lse2 = lse2 + (m + jnp.log(den)) * hm
    return o2, lse2


def _att_blocks(dil):
    m = L // dil
    return [(r * m + n * AB, n == 0) for r in range(dil) for n in range(m // AB)]


def deinterleave(x, dil):
    return x if dil == 1 else x.reshape(L // dil, dil, x.shape[1]).swapaxes(0, 1).reshape(L, x.shape[1])


def attn_fwd(qkv, pos, invf, g, plan=None):
    blocks = _att_blocks(DILS[g])

    def body(q_ref, k_ref, v_ref, pos_ref, invf_ref, o_ref, l_ref, qr, kr):
        cosf, s_lo, s_hi = _rope_tables(pos_ref, invf_ref)
        qr[...] = _rope(q_ref[...], cosf, s_lo, s_hi)
        kr[...] = _rope(k_ref[...], cosf, s_lo, s_hi)
        for off, first in blocks:
            cur, prv = pl.ds(off, AB), pl.ds(off if first else off - AB, AB)
            o2, lse2 = _att_block(qr[cur, :], kr[prv, :], kr[cur, :], v_ref[prv, :], v_ref[cur, :], first)
            o_ref[cur, :] = o2
            l_ref[cur, :] = lse2

    def sec(n):
        return pl.BlockSpec((L, 128), lambda p: (0, p + 4 * n))

    return pcall(
        body, plan, grid=(4,),
        in_specs=[sec(0), sec(1), sec(2), pl.BlockSpec((L, 1), lambda p: (0, 0)), pl.BlockSpec((1, 128), lambda p: (0, 0))],
        out_specs=[sec(0), sec(0)], out_shape=[S((L, 512), f32), S((L, 512), f32)],
        scratch_shapes=[pltpu.VMEM((L, 128), f32), pltpu.VMEM((L, 128), f32)],
        sem=("parallel",), name=f"attn_fwd{g}", args=[qkv, qkv, qkv, pos, invf])


def _att_block_bwd(q2, kp, kc, vp, vc, lse2, do2, dl2, first):
    lane = lax.broadcasted_iota(jnp.int32, (1, 128), 1)
    qi = lax.broadcasted_iota(jnp.int32, (AB, 2 * AB), 0) + AB
    kj = lax.broadcasted_iota(jnp.int32, (AB, 2 * AB), 1)
    back = qi - kj
    valid = (back >= 0) & (back <= AB)
    if first:
        valid = valid & (kj >= AB)
    kk = jnp.concatenate([kp, kc], axis=0)
    vv = jnp.concatenate([vp, vc], axis=0)
    dq2 = jnp.zeros((AB, 128), f32)
    dkk = jnp.zeros((2 * AB, 128), f32)
    dvv = jnp.zeros((2 * AB, 128), f32)
    for e in range(2):
        hb = (lane >= 64 * e) & (lane < 64 * (e + 1))
        hm = hb.astype(f32)
        qs = q2 * (hm * 0.125)
        lse = jnp.max(jnp.where(hb, lse2, -jnp.inf), axis=-1, keepdims=True)
        dls = jnp.sum(dl2 * hm, axis=-1, keepdims=True)
        p = jnp.where(valid, jnp.exp(dot_nt(qs, kk) - lse), 0.0)
        dov = do2 * hm
        dp = dot_nt(dov, vv)
        ds = p * (dp - jnp.sum(p * dp, axis=-1, keepdims=True) + dls)
        dq2 = dq2 + dot_nn(ds, kk) * (hm * 0.125)
        dkk = dkk + dot_tn(ds, qs)
        dvv = dvv + dot_tn(p, dov)
    return dq2, dkk[:AB], dkk[AB:], dvv[:AB], dvv[AB:]


def attn_bwd(qkv, pos, invf, lse, do, dl, g, plan=None):
    blocks = _att_blocks(DILS[g])

    def body(q_ref, k_ref, v_ref, pos_ref, invf_ref, l_ref, do_ref, dl_ref, d_ref, qr, kr, dqr, dkr, dvr):
        cosf, s_lo, s_hi = _rope_tables(pos_ref, invf_ref)
        qr[...] = _rope(q_ref[...], cosf, s_lo, s_hi)
        kr[...] = _rope(k_ref[...], cosf, s_lo, s_hi)
        for off, first in blocks:
            cur, prv = pl.ds(off, AB), pl.ds(off if first else off - AB, AB)
            dq2, dkp, dkc, dvp, dvc = _att_block_bwd(qr[cur, :], kr[prv, :], kr[cur, :], v_ref[prv, :], v_ref[cur, :],
                                                     l_ref[cur, :], do_ref[cur, :], dl_ref[cur, :], first)
            dqr[cur, :] = dq2
            dkr[cur, :] = dkc
            dvr[cur, :] = dvc
            if not first:
                dkr[prv, :] += dkp
                dvr[prv, :] += dvp
        d_ref[0] = _rope_t(dqr[...], cosf, s_lo, s_hi).astype(d_ref.dtype)
        d_ref[1] = _rope_t(dkr[...], cosf, s_lo, s_hi).astype(d_ref.dtype)
        d_ref[2] = dvr[...].astype(d_ref.dtype)

    def sec(n):
        return pl.BlockSpec((L, 128), lambda p: (0, p + 4 * n))

    return pcall(
        body, plan, grid=(4,),
        in_specs=[sec(0), sec(1), sec(2), pl.BlockSpec((L, 1), lambda p: (0, 0)), pl.BlockSpec((1, 128), lambda p: (0, 0)),
                  sec(0), sec(0), sec(0)],
        out_specs=pl.BlockSpec((3, L, 128), lambda p: (0, 0, p)), out_shape=S((3, L, 512), BF),
        scratch_shapes=[pltpu.VMEM((L, 128), f32)] * 5,
        sem=("parallel",), name=f"attn_bwd{g}", args=[qkv, qkv, qkv, pos, invf, lse, do, dl])


def _merge(o0, o1, o2, l0, l1, l2):
    m = jnp.maximum(jnp.maximum(l0, l1), l2)
    e0, e1, e2 = jnp.exp(l0 - m), jnp.exp(l1 - m), jnp.exp(l2 - m)
    return (e0 * o0 + e1 * o1 + e2 * o2) / (e0 + e1 + e2)


def _to_token_major(src_ref, scr, i, dil, slab):
    n = TR // dil
    for r in range(dil):
        rows = pl.ds(pl.multiple_of(r * (L // dil) + i * n, n), n)
        scr[pl.ds(r, n, stride=dil), :] = src_ref[rows, slab * 128:(slab + 1) * 128].astype(f32)
    return scr[...]


def _to_class_major(val, dst_ref, scr, i, dil, slab):
    n = TR // dil
    scr[...] = val
    for r in range(dil):
        rows = pl.ds(pl.multiple_of(r * (L // dil) + i * n, n), n)
        dst_ref[rows, slab * 128:(slab + 1) * 128] = scr[pl.ds(r, n, stride=dil), :].astype(dst_ref.dtype)


def rms_fwd_classes(x, g, name):
    def body(x_ref, g_ref, o_ref, o1_ref, o2_ref, scr):
        i = pl.program_id(0)
        y = _rms(x_ref[...], g_ref[...])
        o_ref[...] = y.astype(o_ref.dtype)
        for s in range(D // 128):
            ys = y[:, s * 128:(s + 1) * 128]
            _to_class_major(ys, o1_ref, scr, i, DILS[1], s)
            _to_class_major(ys, o2_ref, scr, i, DILS[2], s)

    row = pl.BlockSpec((TR, D), lambda i: (i, 0))
    full = pl.BlockSpec((L, D), lambda i: (0, 0))
    return pl.pallas_call(
        body, grid=(L // TR,), in_specs=[row, pl.BlockSpec((1, D), lambda i: (0, 0))], out_specs=[row, full, full],
        out_shape=[S((L, D), BF)] * 3, scratch_shapes=[pltpu.VMEM((TR, 128), f32)],
        compiler_params=_cp(("arbitrary",)), name=name)(x, g)


def rms_bwd_classes(x, g, dy0, dyc, dres, name):
    def body(x_ref, g_ref, dy0_ref, d1_ref, d2_ref, dr_ref, dh_ref, dg_ref, scr, dyf):
        i = pl.program_id(0)
        for s in range(D // 128):
            sl = slice(s * 128, (s + 1) * 128)
            dyf[:, sl] = (dy0_ref[:, sl] + _to_token_major(d1_ref, scr.at[0], i, DILS[1], s)
                          + _to_token_major(d2_ref, scr.at[1], i, DILS[2], s))
        _, vjp = jax.vjp(_rms, x_ref[...], g_ref[...])
        dx, dg = vjp(dyf[...])
        dh_ref[...] = dr_ref[...] + dx

        @pl.when(i == 0)
        def _():
            dg_ref[...] = jnp.zeros_like(dg_ref)

        dg_ref[...] += dg

    row = pl.BlockSpec((TR, D), lambda i: (i, 0))
    vec = pl.BlockSpec((1, D), lambda i: (0, 0))
    full = pl.BlockSpec((L, D), lambda i: (0, 0))
    return pl.pallas_call(
        body, grid=(L // TR,), in_specs=[row, vec, row, full, full, row], out_specs=[row, vec],
        out_shape=[S((L, D), f32), S((1, D), f32)],
        scratch_shapes=[pltpu.VMEM((2, TR, 128), f32), pltpu.VMEM((TR, D), f32)],
        compiler_params=_cp(("arbitrary",)), name=name)(x, g, dy0, dyc[0], dyc[1], dres)


def attn_merge_fwd(o0, l0, oc, lc):
    def body(o0_ref, l0_ref, o1_ref, l1_ref, o2_ref, l2_ref, o_ref, scr):
        i = pl.program_id(0)
        for s in range(4):
            sl = slice(s * 128, (s + 1) * 128)
            o1 = _to_token_major(o1_ref, scr.at[0], i, DILS[1], s)
            l1 = _to_token_major(l1_ref, scr.at[1], i, DILS[1], s)
            o2 = _to_token_major(o2_ref, scr.at[2], i, DILS[2], s)
            l2 = _to_token_major(l2_ref, scr.at[3], i, DILS[2], s)
            o_ref[:, sl] = _merge(o0_ref[:, sl], o1, o2, l0_ref[:, sl], l1, l2).astype(o_ref.dtype)

    blk = pl.BlockSpec((TR, 512), lambda i: (i, 0))
    full = pl.BlockSpec((L, 512), lambda i: (0, 0))
    return pl.pallas_call(
        body, grid=(L // TR,), in_specs=[blk, blk, full, full, full, full], out_specs=blk, out_shape=S((L, 512), BF),
        scratch_shapes=[pltpu.VMEM((4, TR, 128), f32)],
        compiler_params=_cp(("arbitrary",)), name="attn_merge_fwd")(o0, l0, oc[0], lc[0], oc[1], lc[1])


def attn_merge_bwd(o0, l0, oc, lc, do, plan=None):
    def body(o0_ref, l0_ref, o1_ref, l1_ref, o2_ref, l2_ref, g_ref, do0, dl0, do1, dl1, do2, dl2, scr):
        i = pl.program_id(0)
        for s in range(4):
            sl = slice(s * 128, (s + 1) * 128)
            o1 = _to_token_major(o1_ref, scr.at[0], i, DILS[1], s)
            l1 = _to_token_major(l1_ref, scr.at[1], i, DILS[1], s)
            o2 = _to_token_major(o2_ref, scr.at[2], i, DILS[2], s)
            l2 = _to_token_major(l2_ref, scr.at[3], i, DILS[2], s)
            _, vjp = jax.vjp(_merge, o0_ref[:, sl], o1, o2, l0_ref[:, sl], l1, l2)
            g0, g1, g2, h0, h1, h2 = vjp(g_ref[:, sl].astype(f32))
            do0[:, sl] = g0.astype(do0.dtype)
            dl0[:, sl] = h0
            _to_class_major(g1, do1, scr.at[0], i, DILS[1], s)
            _to_class_major(h1, dl1, scr.at[1], i, DILS[1], s)
            _to_class_major(g2, do2, scr.at[2], i, DILS[2], s)
            _to_class_major(h2, dl2, scr.at[3], i, DILS[2], s)

    blk = pl.BlockSpec((TR, 512), lambda i: (i, 0))
    full = pl.BlockSpec((L, 512), lambda i: (0, 0))
    outs = pcall(body, plan, grid=(L // TR,), in_specs=[blk, blk, full, full, full, full, blk],
                 out_specs=[blk, blk, full, full, full, full],
                 out_shape=[S((L, 512), BF), S((L, 512), f32)] * 3, scratch_shapes=[pltpu.VMEM((4, TR, 128), f32)],
                 sem=("arbitrary",), name="attn_merge_bwd", args=[o0, l0, oc[0], lc[0], oc[1], lc[1], do])
    return [outs[0], outs[2], outs[4]], [outs[1], outs[3], outs[5]]


def _invf_lanes():
    half = 8
    inv = ROPE_THETA ** (-np.arange(half, dtype=np.float32) * 2.0 / 16.0)
    lane = np.arange(128) % 64
    return jnp.asarray(np.where(lane < 16, inv[lane % 8], 0.0).astype(np.float32)[None, :])


def hosted(C, host, fn):
    p = C.plan(host) if C is not None else None
    out = fn(p)
    if p is not None:
        C.done(p)
    return out


def _ffn_fwd(h, g_row, W, cb, layer, C):
    hn = rms_fwd(h, g_row, f"rms_ffn{layer}")
    hu = hosted(C, f"ffn_in{layer}", lambda p: matmul(hn, W[("ffn_w_in", layer)], mode="nn", tm=1024, tn=1408, tk=1024,
                                                      plan=p, name=f"ffn_in{layer}"))
    act = hosted(C, f"convact_fwd{layer}", lambda p: convact_fwd(hu, W[("ffn_conv_w", layer)], cb, layer, plan=p))
    h2 = matmul(act, W[("ffn_w_out", layer)], mode="nn", tm=1024, tn=1024, tk=2816, add=h, name=f"ffn_out{layer}")
    return h2, (hn, hu, act)


def _ffn_bwd(dh, h, g_row, W, cb, saved, layer, C, G):
    hn, hu, act = saved
    w_in, w_out = W[("ffn_w_in", layer)], W[("ffn_w_out", layer)]
    dact = matmul(dh, w_out, mode="nt", tm=1024, tn=1408, tk=1024, name=f"ffn_out_dx{layer}")
    G[("ffn_w_out", layer)] = matmul(act, dh, mode="tn", tm=1408, tn=1024, tk=L, out_dtype=BF, name=f"ffn_out_dw{layer}")
    dhu, G[("ffn_conv_w", layer)], g_cb = hosted(
        C, f"convact_bwd{layer}", lambda p: convact_bwd(hu, W[("ffn_conv_w", layer)], cb, dact, layer, plan=p))
    dhn = hosted(C, f"ffn_in_dx{layer}", lambda p: matmul(dhu, w_in, mode="nt", tm=1024, tn=1024, tk=2816, plan=p,
                                                         name=f"ffn_in_dx{layer}"))
    G[("ffn_w_in", layer)] = hosted(C, f"ffn_in_dw{layer}", lambda p: matmul(
        hn, dhu, mode="tn", tm=1024, tn=1408, tk=L, out_dtype=BF, plan=p, name=f"ffn_in_dw{layer}"))
    dh2, g_norm = rms_bwd(h, g_row, [dhn], dh, f"rms_ffn_bwd{layer}")
    return dh2, g_cb, g_norm


def local_step(x, pos, tgt, sm, W, C=None):
    G = C.grads if C is not None else {}
    nm, nf = sm["norm_mix"], sm["norm_ffn"]
    invf = _invf_lanes()
    are = sm["s5_A_re"].reshape(NST, 1)
    aim = sm["s5_A_im"].reshape(NST, 1)
    ldt = sm["s5_log_dt"].reshape(1, 32)
    bre = sm["s5_B_re"].reshape(NST, 16)
    bim = sm["s5_B_im"].reshape(NST, 16)
    cre = jnp.swapaxes(sm["s5_C_re"][0], 1, 2).reshape(NST, 16)
    cim = jnp.swapaxes(sm["s5_C_im"][0], 1, 2).reshape(NST, 16)
    drow = sm["s5_D"].reshape(1, S5W)
    wbr, wbi, wcr, wci, abr, abi = s5_params_fwd(are, aim, ldt, bre, bim, cre, cim)
    hn0 = rms_fwd(x, nm[0:1], "rms_mix0")
    cb3 = sm["ffn_conv_b3"]
    proj = hosted(C, "mix_in", lambda p: matmul(hn0, W[("mix_w_in", 0)], mode="nn", tm=1024, tn=1280, tk=1024, plan=p, name="mix_in"))
    xs_re, xs_im, y5 = hosted(C, "s5_scan_fwd", lambda p: s5_scan_fwd(proj, wbr, wbi, wcr, wci, abr, abi, drow, plan=p))
    oa = s5_glu_fwd(y5, W[("s5_glu_w", 0)], sm["s5_glu_b"])
    ob, ssave = hosted(C, "hgrn_fwd", lambda p: hgrn_fwd(proj, sm["hgrn_gamma"], sm["hgrn_norm"], plan=p))
    cat = jnp.concatenate([oa, ob], axis=1)
    h1 = matmul(cat, W[("mix_w_out", 0)], mode="nn", tm=1024, tn=1024, tk=1024, add=x, name="mix_out")
    h2, ffn0 = _ffn_fwd(h1, nf[0:1], W, cb3, 0, C)
    hn2_g = rms_fwd_classes(h2, nm[1:2], "rms_mix1")
    wqkv = W[("att_w_qkv", 0)]
    pos_g, qkv_g, oc_g, lc_g = [], [], [], []
    for g, dil in enumerate(DILS):
        pos_g.append(deinterleave(pos, dil))
        qkv_g.append(hosted(C, f"att_qkv{g}", lambda p: matmul(
            hn2_g[g], wqkv, mode="nn", tm=1024, tn=512, tk=1024, dims=(L, 1536, D),
            b_spec=pl.BlockSpec((D, 512), lambda i, j, k, g=g: (0, 3 * j + g)), plan=p, name=f"att_qkv{g}")))
        o_c, l_c = hosted(C, f"attn_fwd{g}", lambda p: attn_fwd(qkv_g[g], pos_g[g], invf, g, plan=p))
        oc_g.append(o_c)
        lc_g.append(l_c)
    o = attn_merge_fwd(oc_g[0], lc_g[0], oc_g[1:], lc_g[1:])
    h3 = matmul(o, W[("att_w_o", 0)], mode="nn", tm=1024, tn=1024, tk=512, add=h2, name="att_o")
    h4, ffn1 = _ffn_fwd(h3, nf[1:2], W, cb3, 1, C)
    loss, dh, g_nfinal = loss_head(h4, sm["norm_final"].reshape(1, D), tgt)
    dh, g_cb1, g_nf1 = _ffn_bwd(dh, h3, nf[1:2], W, cb3, ffn1, 1, C, G)
    do = matmul(dh, W[("att_w_o", 0)], mode="nt", tm=1024, tn=512, tk=1024, name="att_o_dx")
    G[("att_w_o", 0)] = matmul(o, dh, mode="tn", tm=512, tn=1024, tk=L, out_dtype=BF, name="att_o_dw")
    do_g, dl_g = hosted(C, "attn_merge_bwd", lambda p: attn_merge_bwd(oc_g[0], lc_g[0], oc_g[1:], lc_g[1:], do, plan=p))
    dhn2_g, gq = [], []
    for g, dil in enumerate(DILS):
        d3 = hosted(C, f"attn_bwd{g}", lambda p: attn_bwd(qkv_g[g], pos_g[g], invf, lc_g[g], do_g[g], dl_g[g], g, plan=p))
        dx = matmul(d3, wqkv, mode="nt", tm=1024, tn=1024, tk=512, dims=(L, D, 1536),
                    a_spec=pl.BlockSpec((None, 1024, 512), lambda i, j, k: (k, i, 0)),
                    b_spec=pl.BlockSpec((D, 512), lambda i, j, k, g=g: (0, 3 * k + g)), name=f"att_qkv_dx{g}")
        dhn2_g.append(dx)
        gq.append(matmul(hn2_g[g], d3, mode="tn", tm=1024, tn=512, tk=L, out_dtype=BF, dims=(D, 1536, L),
                         b_spec=pl.BlockSpec((None, L, 512), lambda i, j, k: (j, k, 0)), name=f"att_qkv_dw{g}"))
    G[("att_w_qkv", 0)] = jnp.concatenate([gq[g][:, 512 * s:512 * (s + 1)] for s in range(3) for g in range(3)], axis=1)
    dh, g_nm1 = rms_bwd_classes(h2, nm[1:2], dhn2_g[0], dhn2_g[1:], dh, "rms_mix_bwd1")
    dh, g_cb0, g_nf0 = _ffn_bwd(dh, h1, nf[0:1], W, cb3, ffn0, 0, C, G)
    dmix = matmul(dh, W[("mix_w_out", 0)], mode="nt", tm=1024, tn=1024, tk=1024, name="mix_out_dx")
    G[("mix_w_out", 0)] = matmul(cat, dh, mode="tn", tm=1024, tn=1024, tk=L, out_dtype=BF, name="mix_out_dw")
    dy5, g_glu_w, g_glu_b = s5_glu_bwd(y5, W[("s5_glu_w", 0)], sm["s5_glu_b"], dmix)
    G[("s5_glu_w", 0)] = g_glu_w.astype(BF)
    du, gwbr, gwbi, gwcr, gwci, gabr, gabi, g_d = hosted(C, "s5_scan_bwd", lambda p: s5_scan_bwd(
        dy5, proj, xs_re, xs_im, wbr, wbi, wcr, wci, abr, abi, drow, plan=p))
    g_are, g_aim, g_ldt, g_bre, g_bim, g_cre, g_cim = s5_params_bwd(are, aim, ldt, bre, bim, cre, cim,
                                                                   (gwbr, gwbi, gwcr, gwci, gabr, gabi))
    dproj, g_gamma, g_hnorm = hosted(C, "hgrn_bwd", lambda p: hgrn_bwd(proj, sm["hgrn_gamma"], sm["hgrn_norm"], ssave, dmix, du,
                                                                       plan=p))
    dhn0 = hosted(C, "mix_in_dx", lambda p: matmul(dproj, W[("mix_w_in", 0)], mode="nt", tm=1024, tn=1024, tk=2560, plan=p,
                                                  name="mix_in_dx"))
    G[("mix_w_in", 0)] = matmul(hn0, dproj, mode="tn", tm=1024, tn=1280, tk=L, out_dtype=BF, name="mix_in_dw")
    gx, g_nm0 = hosted(C, "rms_mix_bwd0", lambda p: rms_bwd(x, nm[0:1], [dhn0], dh, "rms_mix_bwd0", plan=p))
    small = {
        "norm_mix": jnp.concatenate([g_nm0, g_nm1], axis=0), "norm_ffn": jnp.concatenate([g_nf0, g_nf1], axis=0),
        "norm_final": g_nfinal.reshape(D),
        "s5_A_re": g_are.reshape(1, 32, 64), "s5_A_im": g_aim.reshape(1, 32, 64), "s5_log_dt": g_ldt.reshape(1, 32),
        "s5_B_re": g_bre.reshape(1, 32, 64, 16), "s5_B_im": g_bim.reshape(1, 32, 64, 16),
        "s5_C_re": jnp.swapaxes(g_cre.reshape(1, 32, 64, 16), 2, 3), "s5_C_im": jnp.swapaxes(g_cim.reshape(1, 32, 64, 16), 2, 3),
        "s5_D": g_d.reshape(1, 32, 16), "s5_glu_b": g_glu_b, "hgrn_gamma": g_gamma, "hgrn_norm": g_hnorm,
        "ffn_conv_b": jnp.concatenate([g_cb0, g_cb1], axis=0),
    }
    return loss, gx, G, small


BIG = ("mix_w_in", "mix_w_out", "s5_glu_w", "att_w_qkv", "att_w_o", "ffn_w_in", "ffn_w_out", "ffn_conv_w")
SMALL = ("norm_mix", "norm_ffn", "norm_final", "s5_A_re", "s5_A_im", "s5_log_dt", "s5_B_re", "s5_B_im", "s5_C_re", "s5_C_im",
         "s5_D", "s5_glu_b", "hgrn_gamma", "hgrn_norm", "ffn_conv_b")


def cast_bf16(w, name, plan=None):
    nl, r, c = w.shape
    w2 = w.reshape(nl * r, c)
    tr = 256 if (nl * r) % 256 == 0 else nl * r

    def body(w_ref, o_ref):
        o_ref[...] = w_ref[...].astype(BF)

    out = pcall(body, plan, grid=(nl * r // tr,), in_specs=[pl.BlockSpec((tr, c), lambda i: (i, 0))],
                out_specs=pl.BlockSpec((tr, c), lambda i: (i, 0)), out_shape=S((nl * r, c), BF),
                sem=("parallel",), name=name, args=[w2])
    return out.reshape(nl, r, c)


SCHEDULE = {
    "cast_ffn_w_in": [("G", "mix_w_in", 0)],
    "mix_in": [("G", "mix_w_out", 0), ("G", "s5_glu_w", 0)],
    "s5_scan_fwd": [("G", "ffn_w_in", 0, (0, 2))],
    "hgrn_fwd": [("G", "ffn_w_in", 0, (1, 2)), ("G", "ffn_conv_w", 0), ("G", "ffn_conv_w", 1), ("G", "att_w_qkv", 0, (0, 2))],
    "ffn_in0": [("G", "ffn_w_out", 0)],
    "convact_fwd0": [("G", "att_w_qkv", 0, (1, 2))],
    "att_qkv0": [("G", "att_w_o", 0)],
    "attn_fwd0": [("G", "ffn_w_in", 1, (0, 2))],
    "attn_fwd1": [("G", "ffn_w_in", 1, (1, 2))],
    "attn_fwd2": [("G", "ffn_w_out", 1)],
    "convact_bwd1": [("A", "ffn_w_out", 1)],
    "ffn_in_dx1": [("B", "ffn_w_out", 1)],
    "attn_merge_bwd": [("A", "att_w_o", 0), ("A", "ffn_conv_w", 1)],
    "attn_bwd0": [("A", "ffn_w_in", 1, (0, 2))],
    "attn_bwd1": [("A", "ffn_w_in", 1, (1, 2)), ("B", "att_w_o", 0), ("B", "ffn_conv_w", 1)],
    "attn_bwd2": [("B", "ffn_w_in", 1)],
    "convact_bwd0": [("A", "att_w_qkv", 0, (0, 2))],
    "ffn_in_dw0": [("A", "att_w_qkv", 0, (1, 2))],
    "s5_scan_bwd": [("A", "ffn_w_out", 0), ("A", "mix_w_out", 0), ("A", "s5_glu_w", 0), ("A", "ffn_conv_w", 0),
                    ("B", "att_w_qkv", 0)],
    "hgrn_bwd": [("A", "ffn_w_in", 0), ("B", "ffn_w_out", 0), ("B", "mix_w_out", 0), ("B", "s5_glu_w", 0), ("B", "ffn_conv_w", 0)],
    "mix_in_dx": [("B", "ffn_w_in", 0)],
    "adam_ffn_w_in": [("A", "mix_w_in", 0, (0, 2)), ("A", "small", 0)],
    "adam_ffn_w_out": [("A", "mix_w_in", 0, (1, 2))],
    "adam_att_w_qkv": [("B", "mix_w_in", 0), ("B", "small", 0)],
}


class Comm:
    def __init__(self, shards, shapes):
        self.shards, self.shapes = shards, shapes
        self.W, self.grads, self.slots = {}, {}, {}
        self.small = None

    def plan(self, host):
        items = SCHEDULE.get(host)
        if not items:
            return None
        p = Plan()
        for it in items:
            kind, name, l = it[:3]
            part, parts = it[3] if len(it) > 3 else (0, 1)
            if name == "small":
                kdst = p.buf("slots:small", arr=self.slots.get("small"), shape=S((8,) + self.small.shape, f32), write=True)
                if kind == "A":
                    ReduceOp(p, p.buf("g:small", arr=self.small), kdst, None, self.small.shape, False, 0, 0, whole=True)
                else:
                    ForwardOp(p, kdst, None, whole=True)
                continue
            nl, R, C_ = self.shapes[name]
            rows = name in ROW_SHARDED
            r0, nr = part * (R // parts), R // parts
            if kind == "G":
                sh = self.shards[name]
                kdst = p.buf(f"W:{name}:{l}", arr=self.W.get((name, l)), shape=S((4 * R, C_) if rows else (R, 4 * C_), sh.dtype),
                             write=True)
                GatherOp(p, p.buf("shard:" + name, arr=sh), kdst, l, self.shapes[name], rows, r0, nr, split=(nr % 32 == 0))
            else:
                g = self.grads[(name, l)]
                kdst = p.buf("slots:" + name, arr=self.slots.get(name), shape=S((8, nl, R, C_), g.dtype), write=True)
                if kind == "A":
                    ReduceOp(p, p.buf(f"g:{name}:{l}", arr=g), kdst, l, self.shapes[name], rows, r0, nr)
                else:
                    ForwardOp(p, kdst, l)
        return p

    def done(self, p):
        for k, arr in p.out.items():
            tag, name = k.split(":")[:2]
            if tag == "W":
                self.W[(name, int(k.split(":")[2]))] = arr
            else:
                self.slots[name] = arr


def _adamw(w, g, m, v):
    m = B1 * m + (1.0 - B1) * g
    v = B2 * v + (1.0 - B2) * jnp.square(g)
    m_hat = m / (1.0 - B1 ** STEP)
    v_hat = v / (1.0 - B2 ** STEP)
    return -LR * (m_hat / (jnp.sqrt(v_hat) + AEPS) + WD * w), m, v


def adam_big(w, m, v, slots, name, plan=None):
    nl, R, C = w.shape
    tr = 128 if R % 128 == 0 else (64 if R % 64 == 0 else R)

    def body(w_ref, m_ref, v_ref, s_ref, g_ref, d_ref, nm_ref, nv_ref):
        g = s_ref[0].astype(f32)
        for s in range(1, 8):
            g = g + s_ref[s].astype(f32)
        d, nm_, nv_ = _adamw(w_ref[...], g, m_ref[...], v_ref[...])
        g_ref[...] = g
        d_ref[...] = d
        nm_ref[...] = nm_
        nv_ref[...] = nv_

    blk = pl.BlockSpec((None, tr, C), lambda l, i: (l, i, 0))
    return pcall(body, plan, grid=(nl, R // tr),
                 in_specs=[blk, blk, blk, pl.BlockSpec((8, None, tr, C), lambda l, i: (0, l, i, 0))],
                 out_specs=[blk] * 4, out_shape=[S((nl, R, C), f32)] * 4,
                 sem=("parallel", "parallel"), name=name, args=[w, m, v, slots])


def adam_small(w, m, v, slots):
    R = w.shape[0]
    tr = 256

    def body(w_ref, m_ref, v_ref, s_ref, g_ref, d_ref, nm_ref, nv_ref):
        g = s_ref[0]
        for s in range(1, 8):
            g = g + s_ref[s]
        d, nm_, nv_ = _adamw(w_ref[...], g, m_ref[...], v_ref[...])
        g_ref[...] = g
        d_ref[...] = d
        nm_ref[...] = nm_
        nv_ref[...] = nv_

    blk = pl.BlockSpec((tr, 128), lambda i: (i, 0))
    return pl.pallas_call(
        body, grid=(R // tr,), in_specs=[blk, blk, blk, pl.BlockSpec((8, tr, 128), lambda i: (0, i, 0))],
        out_specs=[blk] * 4, out_shape=[S((R, 128), f32)] * 4,
        compiler_params=_cp(("parallel",)), name="adam_small")(w, m, v, slots)


def _pack(d):
    flat = jnp.concatenate([d[n].reshape(-1) for n in SMALL])
    n = flat.shape[0]
    rows = -(-n // (256 * 128)) * 256
    return jnp.pad(flat, (0, rows * 128 - n)).reshape(rows, 128)


def _unpack(p, like):
    flat = p.reshape(-1)
    out, off = {}, 0
    for n in SMALL:
        sz = math.prod(like[n].shape)
        out[n] = flat[off:off + sz].reshape(like[n].shape)
        off += sz
    return out


def kernel(x, positions, norm_mix, norm_ffn, norm_final, mix_w_in, mix_w_out, s5_A_re, s5_A_im, s5_log_dt, s5_B_re, s5_B_im, s5_C_re, s5_C_im, s5_D, s5_glu_w, s5_glu_b, hgrn_gamma, hgrn_norm, att_w_qkv, att_w_o, ffn_w_in, ffn_conv_w, ffn_conv_b, ffn_w_out, loss_target, m_norm_mix, m_norm_ffn, m_norm_final, m_mix_w_in, m_mix_w_out, m_s5_A_re, m_s5_A_im, m_s5_log_dt, m_s5_B_re, m_s5_B_im, m_s5_C_re, m_s5_C_im, m_s5_D, m_s5_glu_w, m_s5_glu_b, m_hgrn_gamma, m_hgrn_norm, m_att_w_qkv, m_att_w_o, m_ffn_w_in, m_ffn_conv_w, m_ffn_conv_b, m_ffn_w_out, v_norm_mix, v_norm_ffn, v_norm_final, v_mix_w_in, v_mix_w_out, v_s5_A_re, v_s5_A_im, v_s5_log_dt, v_s5_B_re, v_s5_B_im, v_s5_C_re, v_s5_C_im, v_s5_D, v_s5_glu_w, v_s5_glu_b, v_hgrn_gamma, v_hgrn_norm, v_att_w_qkv, v_att_w_o, v_ffn_w_in, v_ffn_conv_w, v_ffn_conv_b, v_ffn_w_out):
    a = dict(locals())
    weights = BIG + SMALL
    w = {n: a[n] for n in weights}
    m = {n: a["m_" + n] for n in weights}
    v = {n: a["v_" + n] for n in weights}
    shards = {"ffn_conv_w": ffn_conv_w}
    C = Comm(shards, {n: w[n].shape for n in BIG})
    for n in ("mix_w_in", "ffn_w_in", "mix_w_out", "s5_glu_w", "ffn_w_out", "att_w_qkv", "att_w_o"):
        shards[n] = hosted(C, "cast_" + n, lambda p: cast_bf16(w[n], "cast_" + n, plan=p))
    sm = {n: w[n] for n in SMALL}
    sm["ffn_conv_b3"] = ffn_conv_b.reshape(2, 1, 2 * DFF)
    loss, gx, _, gsmall = local_step(x[0], positions.reshape(L, 1), loss_target[0], sm, C.W, C)
    C.small = _pack(gsmall)
    res = {}
    for n in ("ffn_w_in", "ffn_w_out", "att_w_qkv", "att_w_o", "mix_w_out", "s5_glu_w", "ffn_conv_w", "mix_w_in"):
        res[n] = hosted(C, "adam_" + n, lambda p: adam_big(w[n], m[n], v[n], C.slots[n], "adam_" + n, plan=p))
    packed = adam_small(_pack({n: w[n] for n in SMALL}), _pack({n: m[n] for n in SMALL}), _pack({n: v[n] for n in SMALL}),
                        C.slots["small"])
    small_out = [_unpack(p, {n: w[n] for n in SMALL}) for p in packed]
    for n in SMALL:
        res[n] = tuple(so[n] for so in small_out)
    total = lax.psum(loss[0, 0], ("x", "y", "c"))
    order = ("norm_mix", "norm_ffn", "norm_final", "mix_w_in", "mix_w_out", "s5_A_re", "s5_A_im", "s5_log_dt", "s5_B_re", "s5_B_im",
             "s5_C_re", "s5_C_im", "s5_D", "s5_glu_w", "s5_glu_b", "hgrn_gamma", "hgrn_norm", "att_w_qkv", "att_w_o", "ffn_w_in",
             "ffn_conv_w", "ffn_conv_b", "ffn_w_out")
    return (total, gx[None], *[res[n][0] for n in order], *[res[n][1] for n in order], *[res[n][2] for n in order],
            *[res[n][3] for n in order])
```

```python
import functools
import math

import numpy as np
import jax
import jax.numpy as jnp
from jax import lax
from jax.experimental import pallas as pl
from jax.experimental.pallas import tpu as pltpu

f32 = jnp.float32
BF = jnp.bfloat16
HI = lax.Precision.HIGHEST
S = jax.ShapeDtypeStruct
MESH = pl.DeviceIdType.MESH

L = 2048
D = 1024
EPS = 1e-6
S5W = 512
NST = 2048
HGC = 64
DFF = 2816
ROPE_THETA = 500000.0
LR, B1, B2, AEPS, WD, STEP = 0.001, 0.9, 0.999, 1e-08, 0.01, 10
VMEM_LIMIT = 56 * 1024 * 1024


def _cp(sem=None):
    return pltpu.CompilerParams(dimension_semantics=sem, vmem_limit_bytes=VMEM_LIMIT)


ANY = pl.BlockSpec(memory_space=pl.ANY)
ROW_SHARDED = ("mix_w_out", "s5_glu_w", "ffn_w_out")


def _coords():
    x, y, c = lax.axis_index("x"), lax.axis_index("y"), lax.axis_index("c")
    return x, y, c, 2 * x + y, [(1 - x, y), (x, 1 - y), (1 - x, 1 - y)]


def _rows(start, n):
    return pl.ds(start if isinstance(start, int) else pl.multiple_of(start, 8), n)


def _cols(q, n):
    return pl.ds(pl.multiple_of(q * n, 128), n)


class Plan:
    def __init__(self):
        self.bufs, self.ops, self.nsem, self.out = {}, [], 0, {}

    def buf(self, key, arr=None, shape=None, write=False):
        b = self.bufs.setdefault(key, dict(arr=arr, shape=shape, write=False))
        b["write"] = b["write"] or write
        return key

    def add(self, op):
        op.base = self.nsem
        self.nsem += op.nsem
        self.ops.append(op)


class GatherOp:
    nsem = 13

    def __init__(self, plan, ksrc, kdst, l, shard_shape, rows, r0, nr, split):
        self.ksrc, self.kdst, self.l, (_, self.R, self.C), self.rows, self.r0, self.nr, self.split = (
            ksrc, kdst, l, shard_shape, rows, r0, nr, split)
        self.h = nr // 2 if split else nr
        plan.add(self)

    def _dst(self, R_, q, start, n):
        if self.rows:
            return R_[self.kdst].at[_rows(q * self.R + start, n), :]
        return R_[self.kdst].at[_rows(start, n), _cols(q, self.C)]

    def _mine(self, c):
        return self.r0 + (c * self.h if self.split else 0)

    def _theirs(self, c):
        return self.r0 + ((1 - c) * self.h if self.split else 0)

    def _copies(self, R_, sems):
        x, y, c, me, others = _coords()
        src = R_[self.ksrc]
        local = pltpu.make_async_copy(src.at[self.l, _rows(self.r0, self.nr), :], self._dst(R_, me, self.r0, self.nr),
                                      sems.at[self.base + 12])
        send, fwd = [], []
        for k, (px, py) in enumerate(others):
            q = 2 * px + py
            send.append((
                pltpu.make_async_remote_copy(src.at[self.l, _rows(self._mine(c), self.h), :], self._dst(R_, me, self._mine(c), self.h),
                                             sems.at[self.base + k], sems.at[self.base + 3 + k], device_id=(px, py, c), device_id_type=MESH),
                pltpu.make_async_remote_copy(src.at[self.l, _rows(self._mine(c), self.h), :], self._dst(R_, q, self._mine(c), self.h),
                                             sems.at[self.base + k], sems.at[self.base + 3 + k], device_id=(px, py, c), device_id_type=MESH)))
            fwd.append((
                pltpu.make_async_remote_copy(self._dst(R_, q, self._mine(c), self.h), self._dst(R_, q, self._mine(c), self.h),
                                             sems.at[self.base + 6 + k], sems.at[self.base + 9 + k], device_id=(x, y, 1 - c), device_id_type=MESH),
                pltpu.make_async_remote_copy(self._dst(R_, q, self._theirs(c), self.h), self._dst(R_, q, self._theirs(c), self.h),
                                             sems.at[self.base + 6 + k], sems.at[self.base + 9 + k], device_id=(x, y, 1 - c), device_id_type=MESH)))
        return local, send, fwd

    def start(self, R_, sems):
        local, send, _ = self._copies(R_, sems)
        local.start()
        for out, _ in send:
            out.start()

    def finish(self, R_, sems):
        local, send, fwd = self._copies(R_, sems)
        for k in range(3):
            send[k][1].wait_recv()
            if self.split:
                fwd[k][0].start()
        for k in range(3):
            if self.split:
                fwd[k][1].wait_recv()
                fwd[k][0].wait_send()
            send[k][0].wait_send()
        local.wait()


class ReduceOp:
    nsem = 7

    def __init__(self, plan, ksrc, kdst, l, shard_shape, rows, r0, nr, whole=False):
        self.ksrc, self.kdst, self.l, (self.R, self.C), self.rows, self.r0, self.nr, self.whole = (
            ksrc, kdst, l, shard_shape[-2:], rows, r0, nr, whole)
        plan.add(self)

    def _piece(self, R_, q):
        g = R_[self.ksrc]
        if self.whole:
            return g
        if self.rows:
            return g.at[_rows(q * self.R + self.r0, self.nr), :]
        return g.at[_rows(self.r0, self.nr), _cols(q, self.C)]

    def _slot(self, R_, s):
        if self.whole:
            return R_[self.kdst].at[s]
        return R_[self.kdst].at[s, self.l, _rows(self.r0, self.nr), :]

    def _copies(self, R_, sems):
        x, y, c, me, others = _coords()
        local = pltpu.make_async_copy(self._piece(R_, me), self._slot(R_, 2 * me + c), sems.at[self.base + 6])
        send = []
        for k, (px, py) in enumerate(others):
            q = 2 * px + py
            send.append((
                pltpu.make_async_remote_copy(self._piece(R_, q), self._slot(R_, 2 * me + c), sems.at[self.base + k],
                                             sems.at[self.base + 3 + k], device_id=(px, py, c), device_id_type=MESH),
                pltpu.make_async_remote_copy(self._piece(R_, q), self._slot(R_, 2 * q + c), sems.at[self.base + k],
                                             sems.at[self.base + 3 + k], device_id=(px, py, c), device_id_type=MESH)))
        return local, send

    def start(self, R_, sems):
        local, send = self._copies(R_, sems)
        local.start()
        for out, _ in send:
            out.start()

    def finish(self, R_, sems):
        local, send = self._copies(R_, sems)
        local.wait()
        for out, inn in send:
            inn.wait_recv()
            out.wait_send()


class ForwardOp:
    nsem = 8

    def __init__(self, plan, kdst, l, whole=False):
        self.kdst, self.l, self.whole = kdst, l, whole
        plan.add(self)

    def _slot(self, R_, s):
        return R_[self.kdst].at[s] if self.whole else R_[self.kdst].at[s, self.l]

    def _copies(self, R_, sems):
        x, y, c, me, others = _coords()
        return [(pltpu.make_async_remote_copy(self._slot(R_, 2 * q + c), self._slot(R_, 2 * q + c), sems.at[self.base + q],
                                              sems.at[self.base + 4 + q], device_id=(x, y, 1 - c), device_id_type=MESH),
                 pltpu.make_async_remote_copy(self._slot(R_, 2 * q + 1 - c), self._slot(R_, 2 * q + 1 - c), sems.at[self.base + q],
                                              sems.at[self.base + 4 + q], device_id=(x, y, 1 - c), device_id_type=MESH))
                for q in range(4)]

    def start(self, R_, sems):
        for out, _ in self._copies(R_, sems):
            out.start()

    def finish(self, R_, sems):
        for out, inn in self._copies(R_, sems):
            inn.wait_recv()
            out.wait_send()


def pcall(body, plan, *, grid, in_specs, out_specs, out_shape, scratch_shapes=(), sem, name, args):
    multi = isinstance(out_shape, (list, tuple))
    if plan is None or not plan.ops:
        return pl.pallas_call(body, grid=grid, in_specs=in_specs, out_specs=out_specs, out_shape=out_shape,
                              scratch_shapes=list(scratch_shapes), compiler_params=_cp(sem), name=name)(*args)
    outs = list(out_shape) if multi else [out_shape]
    ospecs = list(out_specs) if multi else [out_specs]
    kin = [k for k, b in plan.bufs.items() if b["arr"] is not None]
    kout = [k for k, b in plan.bufs.items() if b["write"]]
    n_in, n_out, n_scr = len(in_specs), len(outs), len(scratch_shapes)

    def wrapped(*refs):
        o0 = n_in + len(kin)
        s0 = o0 + n_out + len(kout)
        R_ = dict(zip(kin, refs[n_in:o0]))
        R_.update(zip(kout, refs[o0 + n_out:s0]))
        sems = refs[s0 + n_scr]
        first = functools.reduce(jnp.logical_and, [pl.program_id(d) == 0 for d in range(len(grid))])
        last = functools.reduce(jnp.logical_and, [pl.program_id(d) == grid[d] - 1 for d in range(len(grid))])

        @pl.when(first)
        def _():
            for op in plan.ops:
                op.start(R_, sems)

        body(*refs[:n_in], *refs[o0:o0 + n_out], *refs[s0:s0 + n_scr])

        @pl.when(last)
        def _():
            for op in plan.ops:
                op.finish(R_, sems)

    def shape_of(k):
        b = plan.bufs[k]
        return S(b["arr"].shape, b["arr"].dtype) if b["arr"] is not None else b["shape"]

    res = pl.pallas_call(
        wrapped, grid=grid, in_specs=list(in_specs) + [ANY] * len(kin), out_specs=ospecs + [ANY] * len(kout),
        out_shape=outs + [shape_of(k) for k in kout],
        scratch_shapes=list(scratch_shapes) + [pltpu.SemaphoreType.DMA((plan.nsem,))],
        input_output_aliases={n_in + kin.index(k): n_out + kout.index(k) for k in kout if plan.bufs[k]["arr"] is not None},
        compiler_params=pltpu.CompilerParams(dimension_semantics=("arbitrary",) * len(grid), vmem_limit_bytes=VMEM_LIMIT,
                                             has_side_effects=True),
        name=name)(*args, *[plan.bufs[k]["arr"] for k in kin])
    plan.out = dict(zip(kout, res[n_out:]))
    return list(res[:n_out]) if multi else res[0]


def _dg(a, b, ca, cb):
    return lax.dot_general(a.astype(BF), b.astype(BF), (((ca,), (cb,)), ((), ())), preferred_element_type=f32)


@jax.custom_vjp
def dot_nn(a, b):
    return _dg(a, b, 1, 0)


@jax.custom_vjp
def dot_nt(a, b):
    return _dg(a, b, 1, 1)


@jax.custom_vjp
def dot_tn(a, b):
    return _dg(a, b, 0, 0)


dot_nn.defvjp(lambda a, b: (dot_nn(a, b), (a, b)),
              lambda r, g: (dot_nt(g, r[1]).astype(r[0].dtype), dot_tn(r[0], g).astype(r[1].dtype)))
dot_nt.defvjp(lambda a, b: (dot_nt(a, b), (a, b)),
              lambda r, g: (dot_nn(g, r[1]).astype(r[0].dtype), dot_tn(g, r[0]).astype(r[1].dtype)))
dot_tn.defvjp(lambda a, b: (dot_tn(a, b), (a, b)),
              lambda r, g: (dot_nt(r[1], g).astype(r[0].dtype), dot_nn(r[0], g).astype(r[1].dtype)))


def matmul(a, b, *, mode, tm, tn, tk, out_dtype=f32, add=None, b_lead=None, a_spec=None, b_spec=None, dims=None, plan=None, name):
    a_over, b_over = a_spec, b_spec
    if mode == "nn":
        (M, K), N = a.shape[-2:], b.shape[-1]
        a_spec = pl.BlockSpec((tm, tk), lambda i, j, k: (i, k))
        b_blk, b_idx, ca, cb = (tk, tn), (lambda i, j, k: (k, j)), 1, 0
    elif mode == "nt":
        (M, K), N = a.shape[-2:], b.shape[-2]
        a_spec = pl.BlockSpec((tm, tk), lambda i, j, k: (i, k))
        b_blk, b_idx, ca, cb = (tn, tk), (lambda i, j, k: (j, k)), 1, 1
    else:
        (K, M), N = a.shape[-2:], b.shape[-1]
        a_spec = pl.BlockSpec((tk, tm), lambda i, j, k: (k, i))
        b_blk, b_idx, ca, cb = (tk, tn), (lambda i, j, k: (k, j)), 0, 0
    if dims is not None:
        M, N, K = dims
    assert M % tm == 0 and N % tn == 0 and K % tk == 0, (name, M, N, K, tm, tn, tk)
    if b_lead is None:
        b_spec = pl.BlockSpec(b_blk, b_idx)
    else:
        b_spec = pl.BlockSpec((None,) + b_blk, lambda i, j, k: (b_lead,) + b_idx(i, j, k))
    if a_over is not None:
        a_spec = a_over
    if b_over is not None:
        b_spec = b_over
    nk = K // tk
    has_add = add is not None

    def body(*refs):
        a_ref, b_ref = refs[0], refs[1]
        add_ref = refs[2] if has_add else None
        o_ref = refs[2 + has_add]
        p = _dg(a_ref[...], b_ref[...], ca, cb)

        def fin(v):
            if has_add:
                v = v + add_ref[...].astype(f32)
            o_ref[...] = v.astype(o_ref.dtype)

        if nk == 1:
            fin(p)
        else:
            acc = refs[3 + has_add]
            k = pl.program_id(2)

            @pl.when(k == 0)
            def _():
                acc[...] = p

            @pl.when(k > 0)
            def _():
                acc[...] += p

            @pl.when(k == nk - 1)
            def _():
                fin(acc[...])

    in_specs = [a_spec, b_spec]
    args = [a, b]
    if has_add:
        in_specs.append(pl.BlockSpec((tm, tn), lambda i, j, k: (i, j)))
        args.append(add)
    return pcall(body, plan, grid=(M // tm, N // tn, nk), in_specs=in_specs,
                 out_specs=pl.BlockSpec((tm, tn), lambda i, j, k: (i, j)), out_shape=S((M, N), out_dtype),
                 scratch_shapes=[pltpu.VMEM((tm, tn), f32)] if nk > 1 else [],
                 sem=("parallel", "parallel", "arbitrary"), name=name, args=args)


def _rms(xv, gv):
    return xv * lax.rsqrt(jnp.mean(xv * xv, axis=-1, keepdims=True) + EPS) * gv


TR = 256


def rms_fwd(x, g, name):
    def body(x_ref, g_ref, o_ref):
        o_ref[...] = _rms(x_ref[...], g_ref[...]).astype(o_ref.dtype)

    return pl.pallas_call(
        body, grid=(L // TR,),
        in_specs=[pl.BlockSpec((TR, D), lambda i: (i, 0)), pl.BlockSpec((1, D), lambda i: (0, 0))],
        out_specs=pl.BlockSpec((TR, D), lambda i: (i, 0)), out_shape=S((L, D), BF),
        compiler_params=_cp(("parallel",)), name=name)(x, g)


def rms_bwd(x, g, dys, dres, name, plan=None):
    nd = len(dys)

    def body(*refs):
        x_ref, g_ref = refs[0], refs[1]
        dr_ref, dh_ref, dg_ref = refs[2 + nd:]
        dy = refs[2][...].astype(f32)
        for r in refs[3:2 + nd]:
            dy = dy + r[...].astype(f32)
        _, vjp = jax.vjp(_rms, x_ref[...], g_ref[...])
        dx, dg = vjp(dy)
        dh_ref[...] = dr_ref[...] + dx

        @pl.when(pl.program_id(0) == 0)
        def _():
            dg_ref[...] = jnp.zeros_like(dg_ref)

        dg_ref[...] += dg

    row = pl.BlockSpec((TR, D), lambda i: (i, 0))
    vec = pl.BlockSpec((1, D), lambda i: (0, 0))
    return pcall(body, plan, grid=(L // TR,), in_specs=[row, vec] + [row] * (nd + 1), out_specs=[row, vec],
                 out_shape=[S((L, D), f32), S((1, D), f32)], sem=("arbitrary",), name=name, args=[x, g, *dys, dres])


def loss_head(h, g, tgt):
    def f(hv, gv, tv):
        y = _rms(hv, gv)
        return 0.5 * jnp.sum(jnp.mean(jnp.square(y - tv), axis=-1))

    def body(h_ref, g_ref, t_ref, l_ref, dh_ref, dg_ref):
        val, vjp = jax.vjp(f, h_ref[...], g_ref[...], t_ref[...])
        dh, dg, _ = vjp(jnp.ones((), f32))
        dh_ref[...] = dh

        @pl.when(pl.program_id(0) == 0)
        def _():
            dg_ref[...] = jnp.zeros_like(dg_ref)
            l_ref[...] = jnp.zeros_like(l_ref)

        dg_ref[...] += dg
        l_ref[...] += jnp.full((1, 128), val, f32)

    row = pl.BlockSpec((TR, D), lambda i: (i, 0))
    vec = pl.BlockSpec((1, D), lambda i: (0, 0))
    return pl.pallas_call(
        body, grid=(L // TR,), in_specs=[row, vec, row],
        out_specs=[pl.BlockSpec((1, 128), lambda i: (0, 0)), row, vec],
        out_shape=[S((1, 128), f32), S((L, D), f32), S((1, D), f32)],
        compiler_params=_cp(("arbitrary",)), name="loss_head")(h, g, tgt)


def _col_to_row(c):
    n = c.shape[0]
    t = jnp.broadcast_to(c, (n, 128)).T
    r = lax.broadcasted_iota(jnp.int32, (128, n), 0)
    return jnp.sum(jnp.where(r == 0, t, 0.0), axis=0, keepdims=True)


def _s5_param_map(are, aim, ldt_row, bre, bim, cre, cim):
    n = NST
    gi = lax.broadcasted_iota(jnp.int32, (n, 32), 0) // 64
    gj = lax.broadcasted_iota(jnp.int32, (n, 32), 1)
    ldt = jnp.sum(jnp.where(gi == gj, ldt_row, 0.0), axis=1, keepdims=True)
    dt = jnp.exp(ldt)
    mag = jnp.exp(are * dt)
    abr = mag * jnp.cos(aim * dt)
    abi = mag * jnp.sin(aim * dt)
    den = are * are + aim * aim
    nr, ni = abr - 1.0, abi
    cr = (nr * are + ni * aim) / den
    ci = (ni * are - nr * aim) / den
    bbr = cr * bre - ci * bim
    bbi = cr * bim + ci * bre
    tc = lax.broadcasted_iota(jnp.int32, (16, 128), 0)
    tl = lax.broadcasted_iota(jnp.int32, (16, 128), 1)
    T = (tl % 16 == tc).astype(f32)
    mr = (lax.broadcasted_iota(jnp.int32, (n, 128), 0) // 64) % 8
    mc = lax.broadcasted_iota(jnp.int32, (n, 128), 1) // 16
    mask = (mr == mc).astype(f32)

    def expand(v):
        return jnp.dot(v, T, precision=HI, preferred_element_type=f32) * mask

    return expand(bbr), expand(bbi), expand(cre), expand(cim), _col_to_row(abr), _col_to_row(abi)


def s5_params_fwd(are, aim, ldt_row, bre, bim, cre, cim):
    def body(*refs):
        outs = _s5_param_map(*[r[...] for r in refs[:7]])
        for o_ref, o in zip(refs[7:], outs):
            o_ref[...] = o

    return pl.pallas_call(
        body, out_shape=[S((NST, 128), f32)] * 4 + [S((1, NST), f32)] * 2,
        compiler_params=_cp(), name="s5_params_fwd")(are, aim, ldt_row, bre, bim, cre, cim)


def s5_params_bwd(are, aim, ldt_row, bre, bim, cre, cim, cots):
    def body(*refs):
        _, vjp = jax.vjp(_s5_param_map, *[r[...] for r in refs[:7]])
        gs = vjp(tuple(r[...] for r in refs[7:13]))
        for o_ref, o in zip(refs[13:], gs):
            o_ref[...] = o

    return pl.pallas_call(
        body, out_shape=[S((NST, 1), f32)] * 2 + [S((1, 32), f32)] + [S((NST, 16), f32)] * 4,
        compiler_params=_cp(), name="s5_params_bwd")(are, aim, ldt_row, bre, bim, cre, cim, *cots)


def _cpowers(ar, ai):
    out = [(ar, ai)]
    for _ in range(7):
        pr, pi = out[-1]
        out.append((pr * ar - pi * ai, pr * ai + pi * ar))
    return out


def _ctable(pw, rid, power):
    tr_ = jnp.zeros(rid.shape, f32)
    ti_ = jnp.zeros(rid.shape, f32)
    for r in range(8):
        pr, pi = pw[power(r) - 1]
        tr_ = jnp.where(rid == r, pr, tr_)
        ti_ = jnp.where(rid == r, pi, ti_)
    return tr_, ti_


NT5 = 4
RC = 256


def s5_scan_fwd(proj, wbr, wbi, wcr, wci, abr, abi, drow, plan=None):
    def body(u_ref, wbr_ref, wbi_ref, wcr_ref, wci_ref, ar_ref, ai_ref, d_ref, xr_ref, xi_ref, y_ref):
        wbr_v, wbi_v = wbr_ref[...], wbi_ref[...]
        for r in range(L // RC):
            rows = pl.ds(r * RC, RC)
            ub = u_ref[rows, :]
            xr_ref[rows, :] = dot_nt(ub, wbr_v)
            xi_ref[rows, :] = dot_nt(ub, wbi_v)
        pw = _cpowers(ar_ref[...], ai_ref[...])
        rid = lax.broadcasted_iota(jnp.int32, (8, 512), 0)
        tr_, ti_ = _ctable(pw, rid, lambda r: r + 1)

        def group(j, c):
            cr, ci = c
            rows = pl.ds(pl.multiple_of(j * 8, 8), 8)
            br, bi = xr_ref[rows, :], xi_ref[rows, :]
            for s in (1, 2, 4):
                pr, pi = pw[s - 1]
                sr = jnp.where(rid >= s, pltpu.roll(br, s, 0), 0.0)
                si = jnp.where(rid >= s, pltpu.roll(bi, s, 0), 0.0)
                br, bi = br + pr * sr - pi * si, bi + pr * si + pi * sr
            br, bi = br + tr_ * cr - ti_ * ci, bi + tr_ * ci + ti_ * cr
            xr_ref[rows, :] = br
            xi_ref[rows, :] = bi
            return br[7:8], bi[7:8]

        z = jnp.zeros((1, 512), f32)
        lax.fori_loop(0, L // 8, group, (z, z), unroll=2)
        wcr_v, wci_v, dv = wcr_ref[...], wci_ref[...], d_ref[...]
        for r in range(L // RC):
            rows = pl.ds(r * RC, RC)
            y_ref[rows, :] = (dot_nn(xr_ref[rows, :], wcr_v) - dot_nn(xi_ref[rows, :], wci_v)
                              + dv * u_ref[rows, :])

    wspec = pl.BlockSpec((512, 128), lambda j: (j, 0))
    aspec = pl.BlockSpec((1, 512), lambda j: (0, j))
    return pcall(
        body, plan, grid=(NT5,),
        in_specs=[pl.BlockSpec((L, 128), lambda j: (0, j)), wspec, wspec, wspec, wspec, aspec, aspec,
                  pl.BlockSpec((1, 128), lambda j: (0, j))],
        out_specs=[pl.BlockSpec((L, 512), lambda j: (0, j)), pl.BlockSpec((L, 512), lambda j: (0, j)),
                   pl.BlockSpec((L, 128), lambda j: (0, j))],
        out_shape=[S((L, NST), f32), S((L, NST), f32), S((L, S5W), f32)],
        sem=("parallel",), name="s5_scan_fwd", args=[proj, wbr, wbi, wcr, wci, abr, abi, drow])


def s5_scan_bwd(dy, proj, xs_re, xs_im, wbr, wbi, wcr, wci, abr, abi, drow, plan=None):
    def body(dy_ref, u_ref, xr_ref, xi_ref, wbr_ref, wbi_ref, wcr_ref, wci_ref, ar_ref, ai_ref, d_ref,
             du_ref, gwbr_ref, gwbi_ref, gwcr_ref, gwci_ref, gar_ref, gai_ref, gd_ref, lr_ref, li_ref):
        wcr_v, wci_v = wcr_ref[...], wci_ref[...]
        gwcr = jnp.zeros((512, 128), f32)
        gwci = jnp.zeros((512, 128), f32)
        gd = jnp.zeros((1, 128), f32)
        for r in range(L // RC):
            rows = pl.ds(r * RC, RC)
            dyv = dy_ref[rows, :]
            lr_ref[rows, :] = dot_nt(dyv, wcr_v)
            li_ref[rows, :] = -dot_nt(dyv, wci_v)
            gwcr += dot_tn(xr_ref[rows, :], dyv)
            gwci -= dot_tn(xi_ref[rows, :], dyv)
            gd += jnp.sum(dyv * u_ref[rows, :], axis=0, keepdims=True)
        gwcr_ref[...] = gwcr
        gwci_ref[...] = gwci
        gd_ref[...] = gd
        pw = _cpowers(ar_ref[...], -ai_ref[...])
        rid = lax.broadcasted_iota(jnp.int32, (8, 512), 0)
        tr_, ti_ = _ctable(pw, rid, lambda r: 8 - r)

        def group(i, c):
            cr, ci, gar, gai = c
            j = L // 8 - 1 - i
            rows = pl.ds(pl.multiple_of(j * 8, 8), 8)
            br, bi = lr_ref[rows, :], li_ref[rows, :]
            for s in (1, 2, 4):
                pr, pi = pw[s - 1]
                sr = jnp.where(rid < 8 - s, pltpu.roll(br, 8 - s, 0), 0.0)
                si = jnp.where(rid < 8 - s, pltpu.roll(bi, 8 - s, 0), 0.0)
                br, bi = br + pr * sr - pi * si, bi + pr * si + pi * sr
            br, bi = br + tr_ * cr - ti_ * ci, bi + tr_ * ci + ti_ * cr
            lr_ref[rows, :] = br
            li_ref[rows, :] = bi
            nr = jnp.where(rid < 7, pltpu.roll(br, 7, 0), cr)
            ni = jnp.where(rid < 7, pltpu.roll(bi, 7, 0), ci)
            xr, xi = xr_ref[rows, :], xi_ref[rows, :]
            return br[0:1], bi[0:1], gar + xr * nr + xi * ni, gai + xr * ni - xi * nr

        z = jnp.zeros((1, 512), f32)
        z8 = jnp.zeros((8, 512), f32)
        _, _, gar, gai = lax.fori_loop(0, L // 8, group, (z, z, z8, z8), unroll=2)
        gar_ref[...] = jnp.sum(gar, axis=0, keepdims=True)
        gai_ref[...] = jnp.sum(gai, axis=0, keepdims=True)
        wbr_v, wbi_v, dv = wbr_ref[...], wbi_ref[...], d_ref[...]
        gwbr = jnp.zeros((512, 128), f32)
        gwbi = jnp.zeros((512, 128), f32)
        for r in range(L // RC):
            rows = pl.ds(r * RC, RC)
            lrv, liv, uv = lr_ref[rows, :], li_ref[rows, :], u_ref[rows, :]
            du_ref[rows, :] = (dot_nn(lrv, wbr_v) + dot_nn(liv, wbi_v) + dv * dy_ref[rows, :]).astype(du_ref.dtype)
            gwbr += dot_tn(lrv, uv)
            gwbi += dot_tn(liv, uv)
        gwbr_ref[...] = gwbr
        gwbi_ref[...] = gwbi

    wspec = pl.BlockSpec((512, 128), lambda j: (j, 0))
    aspec = pl.BlockSpec((1, 512), lambda j: (0, j))
    col = pl.BlockSpec((L, 128), lambda j: (0, j))
    st = pl.BlockSpec((L, 512), lambda j: (0, j))
    dspec = pl.BlockSpec((1, 128), lambda j: (0, j))
    return pcall(
        body, plan, grid=(NT5,),
        in_specs=[col, col, st, st, wspec, wspec, wspec, wspec, aspec, aspec, dspec],
        out_specs=[col, wspec, wspec, wspec, wspec, aspec, aspec, dspec],
        out_shape=[S((L, S5W), BF)] + [S((NST, 128), f32)] * 4 + [S((1, NST), f32)] * 2 + [S((1, S5W), f32)],
        scratch_shapes=[pltpu.VMEM((L, 512), f32), pltpu.VMEM((L, 512), f32)],
        sem=("parallel",), name="s5_scan_bwd", args=[dy, proj, xs_re, xs_im, wbr, wbi, wcr, wci, abr, abi, drow])


def _glu(y, w, b):
    z = jax.nn.gelu(y)
    return z * jax.nn.sigmoid(dot_nn(z, w) + b)


def s5_glu_fwd(y, w, b):
    def body(y_ref, w_ref, b_ref, o_ref):
        o_ref[...] = _glu(y_ref[...], w_ref[...], b_ref[...]).astype(o_ref.dtype)

    return pl.pallas_call(
        body, grid=(L // TR,),
        in_specs=[pl.BlockSpec((TR, S5W), lambda i: (i, 0)), pl.BlockSpec((S5W, S5W), lambda i: (0, 0)),
                  pl.BlockSpec((1, S5W), lambda i: (0, 0))],
        out_specs=pl.BlockSpec((TR, S5W), lambda i: (i, 0)), out_shape=S((L, S5W), BF),
        compiler_params=_cp(("parallel",)), name="s5_glu_fwd")(y, w, b)


def s5_glu_bwd(y, w, b, dmix):
    def body(y_ref, w_ref, b_ref, g_ref, dy_ref, dw_ref, db_ref):
        _, vjp = jax.vjp(_glu, y_ref[...], w_ref[...].astype(f32), b_ref[...])
        dy, dw, db = vjp(g_ref[...])
        dy_ref[...] = dy

        @pl.when(pl.program_id(0) == 0)
        def _():
            dw_ref[...] = jnp.zeros_like(dw_ref)
            db_ref[...] = jnp.zeros_like(db_ref)

        dw_ref[...] += dw
        db_ref[...] += db

    row = pl.BlockSpec((TR, S5W), lambda i: (i, 0))
    return pl.pallas_call(
        body, grid=(L // TR,),
        in_specs=[row, pl.BlockSpec((S5W, S5W), lambda i: (0, 0)), pl.BlockSpec((1, S5W), lambda i: (0, 0)), row],
        out_specs=[row, pl.BlockSpec((S5W, S5W), lambda i: (0, 0)), pl.BlockSpec((1, S5W), lambda i: (0, 0))],
        out_shape=[S((L, S5W), f32), S((S5W, S5W), f32), S((1, S5W), f32)],
        compiler_params=_cp(("arbitrary",)), name="s5_glu_bwd")(y, w, b, dmix)


def _dg3(a, b, ca, cb):
    ah, bh = a.astype(BF), b.astype(BF)
    al, bl = (a - ah.astype(f32)).astype(BF), (b - bh.astype(f32)).astype(BF)
    return _dg(ah, bh, ca, cb) + _dg(ah, bl, ca, cb) + _dg(al, bh, ca, cb)


@jax.custom_vjp
def hi_nn(a, b):
    return _dg3(a, b, 1, 0)


@jax.custom_vjp
def hi_nt(a, b):
    return _dg3(a, b, 1, 1)


@jax.custom_vjp
def hi_tn(a, b):
    return _dg3(a, b, 0, 0)


hi_nn.defvjp(lambda a, b: (hi_nn(a, b), (a, b)), lambda r, g: (hi_nt(g, r[1]), hi_tn(r[0], g)))
hi_nt.defvjp(lambda a, b: (hi_nt(a, b), (a, b)), lambda r, g: (hi_nn(g, r[1]), hi_tn(g, r[0])))
hi_tn.defvjp(lambda a, b: (hi_tn(a, b), (a, b)), lambda r, g: (hi_nt(r[1], g), hi_nn(r[0], g)))


def _hgrn_chunk(St, xq, xf, xi, xg, gam, ng):
    lb = jax.nn.sigmoid(gam[0:1] - gam[1:2])
    q = jax.nn.silu(xq)
    f = lb + (1.0 - lb) * jax.nn.sigmoid(xf)
    k = 1.0 - f
    g = jnp.log(f)
    ti = lax.broadcasted_iota(jnp.int32, (HGC, HGC), 0)
    si = lax.broadcasted_iota(jnp.int32, (HGC, HGC), 1)
    causal = si <= ti
    b = jnp.dot(causal.astype(f32), g, precision=HI, preferred_element_type=f32)
    qe = q * jnp.exp(b)
    o = dot_nt(qe, St)
    att = jnp.where(causal, hi_nt(qe, k * jnp.exp(-b)), 0.0)
    o = o + dot_nn(att, xi)
    bl = b[HGC - 1:HGC]
    St_new = St * jnp.exp(bl) + dot_tn(xi, k * jnp.exp(bl - b))
    o = o * lax.rsqrt(jnp.mean(o * o, axis=-1, keepdims=True) + EPS) * ng
    return St_new, o * jax.nn.silu(xg)


NCH = L // HGC


def hgrn_fwd(proj, gamma, hnorm, plan=None):
    def body(q_ref, f_ref, i_ref, g_ref, gam_ref, ng_ref, o_ref, ss_ref, st):
        @pl.when(pl.program_id(0) == 0)
        def _():
            st[...] = jnp.zeros_like(st)

        for h in range(4):
            sl = slice(h * 128, (h + 1) * 128)
            s0 = st[h]
            ss_ref[0, h] = s0
            s1, o = _hgrn_chunk(s0, q_ref[:, sl], f_ref[:, sl], i_ref[:, sl], g_ref[:, sl], gam_ref[:, sl], ng_ref[:, sl])
            st[h] = s1
            o_ref[:, sl] = o.astype(o_ref.dtype)

    def pj(n):
        return pl.BlockSpec((HGC, 512), lambda c: (c, n))

    return pcall(
        body, plan, grid=(NCH,),
        in_specs=[pj(1), pj(2), pj(3), pj(4), pl.BlockSpec((2, 512), lambda c: (0, 0)), pl.BlockSpec((1, 512), lambda c: (0, 0))],
        out_specs=[pl.BlockSpec((HGC, 512), lambda c: (c, 0)), pl.BlockSpec((1, 4, 128, 128), lambda c: (c, 0, 0, 0))],
        out_shape=[S((L, 512), BF), S((NCH, 4, 128, 128), f32)],
        scratch_shapes=[pltpu.VMEM((4, 128, 128), f32)],
        sem=("arbitrary",), name="hgrn_fwd", args=[proj, proj, proj, proj, gamma, hnorm])


def hgrn_bwd(proj, gamma, hnorm, ssave, dmix, du, plan=None):
    def body(q_ref, f_ref, i_ref, g_ref, gam_ref, ng_ref, ss_ref, do_ref, du_ref, dp_ref, dgam_ref, dng_ref, dst):
        @pl.when(pl.program_id(0) == 0)
        def _():
            dst[...] = jnp.zeros_like(dst)
            dgam_ref[...] = jnp.zeros_like(dgam_ref)
            dng_ref[...] = jnp.zeros_like(dng_ref)

        dp_ref[:, 0:512] = du_ref[...]
        for h in range(4):
            sl = slice(h * 128, (h + 1) * 128)
            _, vjp = jax.vjp(_hgrn_chunk, ss_ref[0, h], q_ref[:, sl], f_ref[:, sl], i_ref[:, sl], g_ref[:, sl],
                             gam_ref[:, sl], ng_ref[:, sl])
            ds, dq, df, di, dg, dgam, dng = vjp((dst[h], do_ref[:, sl]))
            dst[h] = ds
            for n, v in enumerate((dq, df, di, dg)):
                dp_ref[:, 512 * (n + 1) + h * 128: 512 * (n + 1) + (h + 1) * 128] = v.astype(dp_ref.dtype)
            dgam_ref[:, sl] += dgam
            dng_ref[:, sl] += dng

    def pj(n):
        return pl.BlockSpec((HGC, 512), lambda i: (NCH - 1 - i, n))

    return pcall(
        body, plan, grid=(NCH,),
        in_specs=[pj(1), pj(2), pj(3), pj(4), pl.BlockSpec((2, 512), lambda i: (0, 0)), pl.BlockSpec((1, 512), lambda i: (0, 0)),
                  pl.BlockSpec((1, 4, 128, 128), lambda i: (NCH - 1 - i, 0, 0, 0)), pj(1), pj(0)],
        out_specs=[pl.BlockSpec((HGC, 2560), lambda i: (NCH - 1 - i, 0)), pl.BlockSpec((2, 512), lambda i: (0, 0)),
                   pl.BlockSpec((1, 512), lambda i: (0, 0))],
        out_shape=[S((L, 2560), BF), S((2, 512), f32), S((1, 512), f32)],
        scratch_shapes=[pltpu.VMEM((4, 128, 128), f32)],
        sem=("arbitrary",), name="hgrn_bwd", args=[proj, proj, proj, proj, gamma, hnorm, ssave, dmix, du])


def _earlier(h_ref, k, r0, n):
    if r0 > 0:
        return h_ref[pl.ds(r0 - k, n), :]
    rid = lax.broadcasted_iota(jnp.int32, (8, h_ref.shape[1]), 0)
    head = jnp.where(rid >= k, pltpu.roll(h_ref[pl.ds(0, 8), :], k, 0), 0.0)
    return jnp.concatenate([head, h_ref[pl.ds(8 - k, n - 8), :]], axis=0)


def _conv3_rows(h_ref, w, b, r0, n=None):
    n = CR if n is None else n
    h1, h2 = _earlier(h_ref, 1, r0, n), _earlier(h_ref, 2, r0, n)
    return w[2:3] * h_ref[pl.ds(r0, n), :] + w[1:2] * h1 + w[0:1] * h2 + b, h1, h2


CT = 128
NCT = DFF // CT
CR = 64


def convact_fwd(hu, cw, cb, layer, plan=None):
    def body(ha_ref, hb_ref, wa_ref, wb_ref, ba_ref, bb_ref, o_ref):
        ca = _conv3_rows(ha_ref, wa_ref[...], ba_ref[...], 0, L)[0]
        cb_ = _conv3_rows(hb_ref, wb_ref[...], bb_ref[...], 0, L)[0]
        o_ref[...] = (jax.nn.silu(ca) * cb_).astype(o_ref.dtype)

    def h(off):
        return pl.BlockSpec((L, CT), lambda j: (0, j + off))

    def w(off):
        return pl.BlockSpec((3, CT), lambda j: (0, j + off))

    def b(off):
        return pl.BlockSpec((None, 1, CT), lambda j: (layer, 0, j + off))

    return pcall(body, plan, grid=(NCT,), in_specs=[h(0), h(NCT), w(0), w(NCT), b(0), b(NCT)],
                 out_specs=pl.BlockSpec((L, CT), lambda j: (0, j)), out_shape=S((L, DFF), BF),
                 sem=("parallel",), name=f"convact_fwd{layer}", args=[hu, hu, cw, cw, cb, cb])


def convact_bwd(hu, cw, cb, dact, layer, plan=None):
    def body(ha_ref, hb_ref, wa_ref, wb_ref, ba_ref, bb_ref, g_ref, dh_ref, dw_ref, db_ref, sh, sw, sb, da_scr, db_scr):
        j = pl.program_id(0)

        def fold(x):
            return functools.reduce(jnp.add, [x[8 * m:8 * m + 8] for m in range(CR // 8)])

        @pl.when(j < NCT)
        def _():
            wa, wb, ba, bb = wa_ref[...], wb_ref[...], ba_ref[...], bb_ref[...]
            da_scr[pl.ds(L, 8), :] = jnp.zeros((8, CT), f32)
            db_scr[pl.ds(L, 8), :] = jnp.zeros((8, CT), f32)
            acc = [jnp.zeros((8, CT), f32) for _ in range(8)]
            for c in range(L // CR):
                r0 = c * CR
                ca, a1, a2 = _conv3_rows(ha_ref, wa, ba, r0)
                cb_, b1, b2 = _conv3_rows(hb_ref, wb, bb, r0)
                g = g_ref[pl.ds(r0, CR), :].astype(f32)
                sg = jax.nn.sigmoid(ca)
                dca = g * cb_ * (sg * (1.0 + ca * (1.0 - sg)))
                dcb = g * (ca * sg)
                da_scr[pl.ds(r0, CR), :] = dca
                db_scr[pl.ds(r0, CR), :] = dcb
                terms = (dca * a2, dca * a1, dca * ha_ref[pl.ds(r0, CR), :], dca,
                         dcb * b2, dcb * b1, dcb * hb_ref[pl.ds(r0, CR), :], dcb)
                acc = [a + fold(t) for a, t in zip(acc, terms)]
            rows = [jnp.sum(a, axis=0, keepdims=True) for a in acc]
            for k in range(3):
                dw_ref[k:k + 1, :] = rows[k]
                sw[j, k:k + 1, :] = rows[4 + k]
            db_ref[...] = rows[3]
            sb[j] = rows[7]
            for c in range(L // CR):
                r0 = c * CR
                for scr, w, out in ((da_scr, wa, dh_ref), (db_scr, wb, sh.at[j])):
                    dh = (w[2:3] * scr[pl.ds(r0, CR), :] + w[1:2] * scr[pl.ds(r0 + 1, CR), :]
                          + w[0:1] * scr[pl.ds(r0 + 2, CR), :])
                    out[pl.ds(r0, CR), :] = dh.astype(out.dtype)

        @pl.when(j >= NCT)
        def _():
            dh_ref[...] = sh[j - NCT]
            dw_ref[...] = sw[j - NCT]
            db_ref[...] = sb[j - NCT]

    def lo(j):
        return jnp.minimum(j, NCT - 1)

    in_specs = [pl.BlockSpec((L, CT), lambda j: (0, lo(j))), pl.BlockSpec((L, CT), lambda j: (0, lo(j) + NCT)),
                pl.BlockSpec((3, CT), lambda j: (0, lo(j))), pl.BlockSpec((3, CT), lambda j: (0, lo(j) + NCT)),
                pl.BlockSpec((None, 1, CT), lambda j: (layer, 0, lo(j))), pl.BlockSpec((None, 1, CT), lambda j: (layer, 0, lo(j) + NCT)),
                pl.BlockSpec((L, CT), lambda j: (0, lo(j)))]
    return pcall(
        body, plan, grid=(2 * NCT,), in_specs=in_specs,
        out_specs=[pl.BlockSpec((L, CT), lambda j: (0, j)), pl.BlockSpec((3, CT), lambda j: (0, j)), pl.BlockSpec((1, CT), lambda j: (0, j))],
        out_shape=[S((L, 2 * DFF), BF), S((3, 2 * DFF), f32), S((1, 2 * DFF), f32)],
        scratch_shapes=[pltpu.VMEM((NCT, L, CT), BF), pltpu.VMEM((NCT, 3, CT), f32), pltpu.VMEM((NCT, 1, CT), f32),
                        pltpu.VMEM((L + 8, CT), f32), pltpu.VMEM((L + 8, CT), f32)],
        sem=("arbitrary",), name=f"convact_bwd{layer}", args=[hu, hu, cw, cw, cb, cb, dact])


DILS = (1, 4, 16)
AB = 128
NPAIR = 12


def _rope_tables(pos_ref, invf_ref):
    ang = pos_ref[...].astype(f32) * invf_ref[...]
    lane = lax.broadcasted_iota(jnp.int32, (1, 128), 1) % 64
    cosf = jnp.where(lane < 16, jnp.cos(ang), 1.0)
    sn = jnp.sin(ang)
    s_lo = jnp.where(lane < 8, -sn, 0.0)
    s_hi = jnp.where((lane >= 8) & (lane < 16), sn, 0.0)
    return cosf, s_lo, s_hi


def _rope(t, cosf, s_lo, s_hi):
    return t * cosf + pltpu.roll(t, 120, 1) * s_lo + pltpu.roll(t, 8, 1) * s_hi


def _rope_t(g, cosf, s_lo, s_hi):
    return g * cosf + pltpu.roll(g * s_lo, 8, 1) + pltpu.roll(g * s_hi, 120, 1)


def _att_block(q2, kp, kc, vp, vc, first):
    lane = lax.broadcasted_iota(jnp.int32, (1, 128), 1)
    qi = lax.broadcasted_iota(jnp.int32, (AB, 2 * AB), 0) + AB
    kj = lax.broadcasted_iota(jnp.int32, (AB, 2 * AB), 1)
    back = qi - kj
    valid = (back >= 0) & (back <= AB)
    if first:
        valid = valid & (kj >= AB)
    kk = jnp.concatenate([kp, kc], axis=0)
    vv = jnp.concatenate([vp, vc], axis=0)
    o2 = jnp.zeros((AB, 128), f32)
    lse2 = jnp.zeros((AB, 128), f32)
    for e in range(2):
        hm = ((lane >= 64 * e) & (lane < 64 * (e + 1))).astype(f32)
        s = dot_nt(q2 * (hm * 0.125), kk)
        s = jnp.where(valid, s, -jnp.inf)
        m = jnp.max(s, axis=-1, keepdims=True)
        p = jnp.exp(s - m)
        den = jnp.sum(p, axis=-1, keepdims=True)
        o2 = o2 + dot_nn(p, vv * hm) / den
        lse2 = lse2 + (m + jnp.log(den)) * hm
    return o2, lse2


def _att_blocks(dil):
    m = L // dil
    return [(r * m + n * AB, n == 0) for r in range(dil) for n in range(m // AB)]


def deinterleave(x, dil):
    return x if dil == 1 else x.reshape(L // dil, dil, x.shape[1]).swapaxes(0, 1).reshape(L, x.shape[1])


def attn_fwd(qkv, pos, invf, g, plan=None):
    blocks = _att_blocks(DILS[g])

    def body(q_ref, k_ref, v_ref, pos_ref, invf_ref, o_ref, l_ref, qr, kr):
        cosf, s_lo, s_hi = _rope_tables(pos_ref, invf_ref)
        qr[...] = _rope(q_ref[...], cosf, s_lo, s_hi)
        kr[...] = _rope(k_ref[...], cosf, s_lo, s_hi)
        for off, first in blocks:
            cur, prv = pl.ds(off, AB), pl.ds(off if first else off - AB, AB)
            o2, lse2 = _att_block(qr[cur, :], kr[prv, :], kr[cur, :], v_ref[prv, :], v_ref[cur, :], first)
            o_ref[cur, :] = o2
            l_ref[cur, :] = lse2

    def sec(n):
        return pl.BlockSpec((L, 128), lambda p: (0, p + 4 * n))

    return pcall(
        body, plan, grid=(4,),
        in_specs=[sec(0), sec(1), sec(2), pl.BlockSpec((L, 1), lambda p: (0, 0)), pl.BlockSpec((1, 128), lambda p: (0, 0))],
        out_specs=[sec(0), sec(0)], out_shape=[S((L, 512), f32), S((L, 512), f32)],
        scratch_shapes=[pltpu.VMEM((L, 128), f32), pltpu.VMEM((L, 128), f32)],
        sem=("parallel",), name=f"attn_fwd{g}", args=[qkv, qkv, qkv, pos, invf])


def _att_block_bwd(q2, kp, kc, vp, vc, lse2, do2, dl2, first):
    lane = lax.broadcasted_iota(jnp.int32, (1, 128), 1)
    qi = lax.broadcasted_iota(jnp.int32, (AB, 2 * AB), 0) + AB
    kj = lax.broadcasted_iota(jnp.int32, (AB, 2 * AB), 1)
    back = qi - kj
    valid = (back >= 0) & (back <= AB)
    if first:
        valid = valid & (kj >= AB)
    kk = jnp.concatenate([kp, kc], axis=0)
    vv = jnp.concatenate([vp, vc], axis=0)
    dq2 = jnp.zeros((AB, 128), f32)
    dkk = jnp.zeros((2 * AB, 128), f32)
    dvv = jnp.zeros((2 * AB, 128), f32)
    for e in range(2):
        hb = (lane >= 64 * e) & (lane < 64 * (e + 1))
        hm = hb.astype(f32)
        qs = q2 * (hm * 0.125)
        lse = jnp.max(jnp.where(hb, lse2, -jnp.inf), axis=-1, keepdims=True)
        dls = jnp.sum(dl2 * hm, axis=-1, keepdims=True)
        p = jnp.where(valid, jnp.exp(dot_nt(qs, kk) - lse), 0.0)
        dov = do2 * hm
        dp = dot_nt(dov, vv)
        ds = p * (dp - jnp.sum(p * dp, axis=-1, keepdims=True) + dls)
        dq2 = dq2 + dot_nn(ds, kk) * (hm * 0.125)
        dkk = dkk + dot_tn(ds, qs)
        dvv = dvv + dot_tn(p, dov)
    return dq2, dkk[:AB], dkk[AB:], dvv[:AB], dvv[AB:]


def attn_bwd(qkv, pos, invf, lse, do, dl, g, plan=None):
    blocks = _att_blocks(DILS[g])

    def body(q_ref, k_ref, v_ref, pos_ref, invf_ref, l_ref, do_ref, dl_ref, d_ref, qr, kr, dqr, dkr, dvr):
        cosf, s_lo, s_hi = _rope_tables(pos_ref, invf_ref)
        qr[...] = _rope(q_ref[...], cosf, s_lo, s_hi)
        kr[...] = _rope(k_ref[...], cosf, s_lo, s_hi)
        for off, first in blocks:
            cur, prv = pl.ds(off, AB), pl.ds(off if first else off - AB, AB)
            dq2, dkp, dkc, dvp, dvc = _att_block_bwd(qr[cur, :], kr[prv, :], kr[cur, :], v_ref[prv, :], v_ref[cur, :],
                                                     l_ref[cur, :], do_ref[cur, :], dl_ref[cur, :], first)
            dqr[cur, :] = dq2
            dkr[cur, :] = dkc
            dvr[cur, :] = dvc
            if not first:
                dkr[prv, :] += dkp
                dvr[prv, :] += dvp
        d_ref[0] = _rope_t(dqr[...], cosf, s_lo, s_hi).astype(d_ref.dtype)
        d_ref[1] = _rope_t(dkr[...], cosf, s_lo, s_hi).astype(d_ref.dtype)
        d_ref[2] = dvr[...].astype(d_ref.dtype)

    def sec(n):
        return pl.BlockSpec((L, 128), lambda p: (0, p + 4 * n))

    return pcall(
        body, plan, grid=(4,),
        in_specs=[sec(0), sec(1), sec(2), pl.BlockSpec((L, 1), lambda p: (0, 0)), pl.BlockSpec((1, 128), lambda p: (0, 0)),
                  sec(0), sec(0), sec(0)],
        out_specs=pl.BlockSpec((3, L, 128), lambda p: (0, 0, p)), out_shape=S((3, L, 512), BF),
        scratch_shapes=[pltpu.VMEM((L, 128), f32)] * 5,
        sem=("parallel",), name=f"attn_bwd{g}", args=[qkv, qkv, qkv, pos, invf, lse, do, dl])


def _merge(o0, o1, o2, l0, l1, l2):
    m = jnp.maximum(jnp.maximum(l0, l1), l2)
    e0, e1, e2 = jnp.exp(l0 - m), jnp.exp(l1 - m), jnp.exp(l2 - m)
    return (e0 * o0 + e1 * o1 + e2 * o2) / (e0 + e1 + e2)


def _to_token_major(src_ref, scr, i, dil, slab):
    n = TR // dil
    for r in range(dil):
        rows = pl.ds(pl.multiple_of(r * (L // dil) + i * n, n), n)
        scr[pl.ds(r, n, stride=dil), :] = src_ref[rows, slab * 128:(slab + 1) * 128].astype(f32)
    return scr[...]


def _to_class_major(val, dst_ref, scr, i, dil, slab):
    n = TR // dil
    scr[...] = val
    for r in range(dil):
        rows = pl.ds(pl.multiple_of(r * (L // dil) + i * n, n), n)
        dst_ref[rows, slab * 128:(slab + 1) * 128] = scr[pl.ds(r, n, stride=dil), :].astype(dst_ref.dtype)


def rms_fwd_classes(x, g, name):
    def body(x_ref, g_ref, o_ref, o1_ref, o2_ref, scr):
        i = pl.program_id(0)
        y = _rms(x_ref[...], g_ref[...])
        o_ref[...] = y.astype(o_ref.dtype)
        for s in range(D // 128):
            ys = y[:, s * 128:(s + 1) * 128]
            _to_class_major(ys, o1_ref, scr, i, DILS[1], s)
            _to_class_major(ys, o2_ref, scr, i, DILS[2], s)

    row = pl.BlockSpec((TR, D), lambda i: (i, 0))
    full = pl.BlockSpec((L, D), lambda i: (0, 0))
    return pl.pallas_call(
        body, grid=(L // TR,), in_specs=[row, pl.BlockSpec((1, D), lambda i: (0, 0))], out_specs=[row, full, full],
        out_shape=[S((L, D), BF)] * 3, scratch_shapes=[pltpu.VMEM((TR, 128), f32)],
        compiler_params=_cp(("arbitrary",)), name=name)(x, g)


def rms_bwd_classes(x, g, dy0, dyc, dres, name):
    def body(x_ref, g_ref, dy0_ref, d1_ref, d2_ref, dr_ref, dh_ref, dg_ref, scr, dyf):
        i = pl.program_id(0)
        for s in range(D // 128):
            sl = slice(s * 128, (s + 1) * 128)
            dyf[:, sl] = (dy0_ref[:, sl] + _to_token_major(d1_ref, scr.at[0], i, DILS[1], s)
                          + _to_token_major(d2_ref, scr.at[1], i, DILS[2], s))
        _, vjp = jax.vjp(_rms, x_ref[...], g_ref[...])
        dx, dg = vjp(dyf[...])
        dh_ref[...] = dr_ref[...] + dx

        @pl.when(i == 0)
        def _():
            dg_ref[...] = jnp.zeros_like(dg_ref)

        dg_ref[...] += dg

    row = pl.BlockSpec((TR, D), lambda i: (i, 0))
    vec = pl.BlockSpec((1, D), lambda i: (0, 0))
    full = pl.BlockSpec((L, D), lambda i: (0, 0))
    return pl.pallas_call(
        body, grid=(L // TR,), in_specs=[row, vec, row, full, full, row], out_specs=[row, vec],
        out_shape=[S((L, D), f32), S((1, D), f32)],
        scratch_shapes=[pltpu.VMEM((2, TR, 128), f32), pltpu.VMEM((TR, D), f32)],
        compiler_params=_cp(("arbitrary",)), name=name)(x, g, dy0, dyc[0], dyc[1], dres)


def attn_merge_fwd(o0, l0, oc, lc):
    def body(o0_ref, l0_ref, o1_ref, l1_ref, o2_ref, l2_ref, o_ref, scr):
        i = pl.program_id(0)
        for s in range(4):
            sl = slice(s * 128, (s + 1) * 128)
            o1 = _to_token_major(o1_ref, scr.at[0], i, DILS[1], s)
            l1 = _to_token_major(l1_ref, scr.at[1], i, DILS[1], s)
            o2 = _to_token_major(o2_ref, scr.at[2], i, DILS[2], s)
            l2 = _to_token_major(l2_ref, scr.at[3], i, DILS[2], s)
            o_ref[:, sl] = _merge(o0_ref[:, sl], o1, o2, l0_ref[:, sl], l1, l2).astype(o_ref.dtype)

    blk = pl.BlockSpec((TR, 512), lambda i: (i, 0))
    full = pl.BlockSpec((L, 512), lambda i: (0, 0))
    return pl.pallas_call(
        body, grid=(L // TR,), in_specs=[blk, blk, full, full, full, full], out_specs=blk, out_shape=S((L, 512), BF),
        scratch_shapes=[pltpu.VMEM((4, TR, 128), f32)],
        compiler_params=_cp(("arbitrary",)), name="attn_merge_fwd")(o0, l0, oc[0], lc[0], oc[1], lc[1])


def attn_merge_bwd(o0, l0, oc, lc, do, plan=None):
    def body(o0_ref, l0_ref, o1_ref, l1_ref, o2_ref, l2_ref, g_ref, do0, dl0, do1, dl1, do2, dl2, scr):
        i = pl.program_id(0)
        for s in range(4):
            sl = slice(s * 128, (s + 1) * 128)
            o1 = _to_token_major(o1_ref, scr.at[0], i, DILS[1], s)
            l1 = _to_token_major(l1_ref, scr.at[1], i, DILS[1], s)
            o2 = _to_token_major(o2_ref, scr.at[2], i, DILS[2], s)
            l2 = _to_token_major(l2_ref, scr.at[3], i, DILS[2], s)
            _, vjp = jax.vjp(_merge, o0_ref[:, sl], o1, o2, l0_ref[:, sl], l1, l2)
            g0, g1, g2, h0, h1, h2 = vjp(g_ref[:, sl].astype(f32))
            do0[:, sl] = g0.astype(do0.dtype)
            dl0[:, sl] = h0
            _to_class_major(g1, do1, scr.at[0], i, DILS[1], s)
            _to_class_major(h1, dl1, scr.at[1], i, DILS[1], s)
            _to_class_major(g2, do2, scr.at[2], i, DILS[2], s)
            _to_class_major(h2, dl2, scr.at[3], i, DILS[2], s)

    blk = pl.BlockSpec((TR, 512), lambda i: (i, 0))
    full = pl.BlockSpec((L, 512), lambda i: (0, 0))
    outs = pcall(body, plan, grid=(L // TR,), in_specs=[blk, blk, full, full, full, full, blk],
                 out_specs=[blk, blk, full, full, full, full],
                 out_shape=[S((L, 512), BF), S((L, 512), f32)] * 3, scratch_shapes=[pltpu.VMEM((4, TR, 128), f32)],
                 sem=("arbitrary",), name="attn_merge_bwd", args=[o0, l0, oc[0], lc[0], oc[1], lc[1], do])
    return [outs[0], outs[2], outs[4]], [outs[1], outs[3], outs[5]]


def _invf_lanes():
    half = 8
    inv = ROPE_THETA ** (-np.arange(half, dtype=np.float32) * 2.0 / 16.0)
    lane = np.arange(128) % 64
    return jnp.asarray(np.where(lane < 16, inv[lane % 8], 0.0).astype(np.float32)[None, :])


def hosted(C, host, fn):
    p = C.plan(host) if C is not None else None
    out = fn(p)
    if p is not None:
        C.done(p)
    return out


def _ffn_fwd(h, g_row, W, cb, layer, C):
    hn = rms_fwd(h, g_row, f"rms_ffn{layer}")
    hu = hosted(C, f"ffn_in{layer}", lambda p: matmul(hn, W[("ffn_w_in", layer)], mode="nn", tm=1024, tn=1408, tk=1024,
                                                      plan=p, name=f"ffn_in{layer}"))
    act = hosted(C, f"convact_fwd{layer}", lambda p: convact_fwd(hu, W[("ffn_conv_w", layer)], cb, layer, plan=p))
    h2 = matmul(act, W[("ffn_w_out", layer)], mode="nn", tm=1024, tn=1024, tk=2816, add=h, name=f"ffn_out{layer}")
    return h2, (hn, hu, act)


def _ffn_bwd(dh, h, g_row, W, cb, saved, layer, C, G):
    hn, hu, act = saved
    w_in, w_out = W[("ffn_w_in", layer)], W[("ffn_w_out", layer)]
    dact = matmul(dh, w_out, mode="nt", tm=1024, tn=1408, tk=1024, name=f"ffn_out_dx{layer}")
    G[("ffn_w_out", layer)] = matmul(act, dh, mode="tn", tm=1408, tn=1024, tk=L, out_dtype=BF, name=f"ffn_out_dw{layer}")
    dhu, G[("ffn_conv_w", layer)], g_cb = hosted(
        C, f"convact_bwd{layer}", lambda p: convact_bwd(hu, W[("ffn_conv_w", layer)], cb, dact, layer, plan=p))
    dhn = hosted(C, f"ffn_in_dx{layer}", lambda p: matmul(dhu, w_in, mode="nt", tm=1024, tn=1024, tk=2816, plan=p,
                                                         name=f"ffn_in_dx{layer}"))
    G[("ffn_w_in", layer)] = hosted(C, f"ffn_in_dw{layer}", lambda p: matmul(
        hn, dhu, mode="tn", tm=1024, tn=1408, tk=L, out_dtype=BF, plan=p, name=f"ffn_in_dw{layer}"))
    dh2, g_norm = rms_bwd(h, g_row, [dhn], dh, f"rms_ffn_bwd{layer}")
    return dh2, g_cb, g_norm


def local_step(x, pos, tgt, sm, W, C=None):
    G = C.grads if C is not None else {}
    nm, nf = sm["norm_mix"], sm["norm_ffn"]
    invf = _invf_lanes()
    are = sm["s5_A_re"].reshape(NST, 1)
    aim = sm["s5_A_im"].reshape(NST, 1)
    ldt = sm["s5_log_dt"].reshape(1, 32)
    bre = sm["s5_B_re"].reshape(NST, 16)
    bim = sm["s5_B_im"].reshape(NST, 16)
    cre = jnp.swapaxes(sm["s5_C_re"][0], 1, 2).reshape(NST, 16)
    cim = jnp.swapaxes(sm["s5_C_im"][0], 1, 2).reshape(NST, 16)
    drow = sm["s5_D"].reshape(1, S5W)
    wbr, wbi, wcr, wci, abr, abi = s5_params_fwd(are, aim, ldt, bre, bim, cre, cim)
    hn0 = rms_fwd(x, nm[0:1], "rms_mix0")
    cb3 = sm["ffn_conv_b3"]
    proj = hosted(C, "mix_in", lambda p: matmul(hn0, W[("mix_w_in", 0)], mode="nn", tm=1024, tn=1280, tk=1024, plan=p, name="mix_in"))
    xs_re, xs_im, y5 = hosted(C, "s5_scan_fwd", lambda p: s5_scan_fwd(proj, wbr, wbi, wcr, wci, abr, abi, drow, plan=p))
    oa = s5_glu_fwd(y5, W[("s5_glu_w", 0)], sm["s5_glu_b"])
    ob, ssave = hosted(C, "hgrn_fwd", lambda p: hgrn_fwd(proj, sm["hgrn_gamma"], sm["hgrn_norm"], plan=p))
    cat = jnp.concatenate([oa, ob], axis=1)
    h1 = matmul(cat, W[("mix_w_out", 0)], mode="nn", tm=1024, tn=1024, tk=1024, add=x, name="mix_out")
    h2, ffn0 = _ffn_fwd(h1, nf[0:1], W, cb3, 0, C)
    hn2_g = rms_fwd_classes(h2, nm[1:2], "rms_mix1")
    wqkv = W[("att_w_qkv", 0)]
    pos_g, qkv_g, oc_g, lc_g = [], [], [], []
    for g, dil in enumerate(DILS):
        pos_g.append(deinterleave(pos, dil))
        qkv_g.append(hosted(C, f"att_qkv{g}", lambda p: matmul(
            hn2_g[g], wqkv, mode="nn", tm=1024, tn=512, tk=1024, dims=(L, 1536, D),
            b_spec=pl.BlockSpec((D, 512), lambda i, j, k, g=g: (0, 3 * j + g)), plan=p, name=f"att_qkv{g}")))
        o_c, l_c = hosted(C, f"attn_fwd{g}", lambda p: attn_fwd(qkv_g[g], pos_g[g], invf, g, plan=p))
        oc_g.append(o_c)
        lc_g.append(l_c)
    o = attn_merge_fwd(oc_g[0], lc_g[0], oc_g[1:], lc_g[1:])
    h3 = matmul(o, W[("att_w_o", 0)], mode="nn", tm=1024, tn=1024, tk=512, add=h2, name="att_o")
    h4, ffn1 = _ffn_fwd(h3, nf[1:2], W, cb3, 1, C)
    loss, dh, g_nfinal = loss_head(h4, sm["norm_final"].reshape(1, D), tgt)
    dh, g_cb1, g_nf1 = _ffn_bwd(dh, h3, nf[1:2], W, cb3, ffn1, 1, C, G)
    do = matmul(dh, W[("att_w_o", 0)], mode="nt", tm=1024, tn=512, tk=1024, name="att_o_dx")
    G[("att_w_o", 0)] = matmul(o, dh, mode="tn", tm=512, tn=1024, tk=L, out_dtype=BF, name="att_o_dw")
    do_g, dl_g = hosted(C, "attn_merge_bwd", lambda p: attn_merge_bwd(oc_g[0], lc_g[0], oc_g[1:], lc_g[1:], do, plan=p))
    dhn2_g, gq = [], []
    for g, dil in enumerate(DILS):
        d3 = hosted(C, f"attn_bwd{g}", lambda p: attn_bwd(qkv_g[g], pos_g[g], invf, lc_g[g], do_g[g], dl_g[g], g, plan=p))
        dx = matmul(d3, wqkv, mode="nt", tm=1024, tn=1024, tk=512, dims=(L, D, 1536),
                    a_spec=pl.BlockSpec((None, 1024, 512), lambda i, j, k: (k, i, 0)),
                    b_spec=pl.BlockSpec((D, 512), lambda i, j, k, g=g: (0, 3 * k + g)), name=f"att_qkv_dx{g}")
        dhn2_g.append(dx)
        gq.append(matmul(hn2_g[g], d3, mode="tn", tm=1024, tn=512, tk=L, out_dtype=BF, dims=(D, 1536, L),
                         b_spec=pl.BlockSpec((None, L, 512), lambda i, j, k: (j, k, 0)), name=f"att_qkv_dw{g}"))
    G[("att_w_qkv", 0)] = jnp.concatenate([gq[g][:, 512 * s:512 * (s + 1)] for s in range(3) for g in range(3)], axis=1)
    dh, g_nm1 = rms_bwd_classes(h2, nm[1:2], dhn2_g[0], dhn2_g[1:], dh, "rms_mix_bwd1")
    dh, g_cb0, g_nf0 = _ffn_bwd(dh, h1, nf[0:1], W, cb3, ffn0, 0, C, G)
    dmix = matmul(dh, W[("mix_w_out", 0)], mode="nt", tm=1024, tn=1024, tk=1024, name="mix_out_dx")
    G[("mix_w_out", 0)] = matmul(cat, dh, mode="tn", tm=1024, tn=1024, tk=L, out_dtype=BF, name="mix_out_dw")
    dy5, g_glu_w, g_glu_b = s5_glu_bwd(y5, W[("s5_glu_w", 0)], sm["s5_glu_b"], dmix)
    G[("s5_glu_w", 0)] = g_glu_w.astype(BF)
    du, gwbr, gwbi, gwcr, gwci, gabr, gabi, g_d = hosted(C, "s5_scan_bwd", lambda p: s5_scan_bwd(
        dy5, proj, xs_re, xs_im, wbr, wbi, wcr, wci, abr, abi, drow, plan=p))
    g_are, g_aim, g_ldt, g_bre, g_bim, g_cre, g_cim = s5_params_bwd(are, aim, ldt, bre, bim, cre, cim,
                                                                   (gwbr, gwbi, gwcr, gwci, gabr, gabi))
    dproj, g_gamma, g_hnorm = hosted(C, "hgrn_bwd", lambda p: hgrn_bwd(proj, sm["hgrn_gamma"], sm["hgrn_norm"], ssave, dmix, du,
                                                                       plan=p))
    dhn0 = hosted(C, "mix_in_dx", lambda p: matmul(dproj, W[("mix_w_in", 0)], mode="nt", tm=1024, tn=1024, tk=2560, plan=p,
                                                  name="mix_in_dx"))
    G[("mix_w_in", 0)] = matmul(hn0, dproj, mode="tn", tm=1024, tn=1280, tk=L, out_dtype=BF, name="mix_in_dw")
    gx, g_nm0 = hosted(C, "rms_mix_bwd0", lambda p: rms_bwd(x, nm[0:1], [dhn0], dh, "rms_mix_bwd0", plan=p))
    small = {
        "norm_mix": jnp.concatenate([g_nm0, g_nm1], axis=0), "norm_ffn": jnp.concatenate([g_nf0, g_nf1], axis=0),
        "norm_final": g_nfinal.reshape(D),
        "s5_A_re": g_are.reshape(1, 32, 64), "s5_A_im": g_aim.reshape(1, 32, 64), "s5_log_dt": g_ldt.reshape(1, 32),
        "s5_B_re": g_bre.reshape(1, 32, 64, 16), "s5_B_im": g_bim.reshape(1, 32, 64, 16),
        "s5_C_re": jnp.swapaxes(g_cre.reshape(1, 32, 64, 16), 2, 3), "s5_C_im": jnp.swapaxes(g_cim.reshape(1, 32, 64, 16), 2, 3),
        "s5_D": g_d.reshape(1, 32, 16), "s5_glu_b": g_glu_b, "hgrn_gamma": g_gamma, "hgrn_norm": g_hnorm,
        "ffn_conv_b": jnp.concatenate([g_cb0, g_cb1], axis=0),
    }
    return loss, gx, G, small


BIG = ("mix_w_in", "mix_w_out", "s5_glu_w", "att_w_qkv", "att_w_o", "ffn_w_in", "ffn_w_out", "ffn_conv_w")
SMALL = ("norm_mix", "norm_ffn", "norm_final", "s5_A_re", "s5_A_im", "s5_log_dt", "s5_B_re", "s5_B_im", "s5_C_re", "s5_C_im",
         "s5_D", "s5_glu_b", "hgrn_gamma", "hgrn_norm", "ffn_conv_b")


def cast_bf16(w, name, plan=None):
    nl, r, c = w.shape
    w2 = w.reshape(nl * r, c)
    tr = 256 if (nl * r) % 256 == 0 else nl * r

    def body(w_ref, o_ref):
        o_ref[...] = w_ref[...].astype(BF)

    out = pcall(body, plan, grid=(nl * r // tr,), in_specs=[pl.BlockSpec((tr, c), lambda i: (i, 0))],
                out_specs=pl.BlockSpec((tr, c), lambda i: (i, 0)), out_shape=S((nl * r, c), BF),
                sem=("parallel",), name=name, args=[w2])
    return out.reshape(nl, r, c)


SCHEDULE = {
    "cast_ffn_w_in": [("G", "mix_w_in", 0)],
    "mix_in": [("G", "mix_w_out", 0), ("G", "s5_glu_w", 0)],
    "s5_scan_fwd": [("G", "ffn_w_in", 0, (0, 2))],
    "hgrn_fwd": [("G", "ffn_w_in", 0, (1, 2)), ("G", "ffn_conv_w", 0), ("G", "ffn_conv_w", 1), ("G", "att_w_qkv", 0, (0, 2))],
    "ffn_in0": [("G", "ffn_w_out", 0)],
    "convact_fwd0": [("G", "att_w_qkv", 0, (1, 2))],
    "att_qkv0": [("G", "att_w_o", 0)],
    "attn_fwd0": [("G", "ffn_w_in", 1, (0, 2))],
    "attn_fwd1": [("G", "ffn_w_in", 1, (1, 2))],
    "attn_fwd2": [("G", "ffn_w_out", 1)],
    "convact_bwd1": [("A", "ffn_w_out", 1)],
    "ffn_in_dx1": [("B", "ffn_w_out", 1)],
    "attn_merge_bwd": [("A", "att_w_o", 0), ("A", "ffn_conv_w", 1)],
    "attn_bwd0": [("A", "ffn_w_in", 1, (0, 2))],
    "attn_bwd1": [("A", "ffn_w_in", 1, (1, 2)), ("B", "att_w_o", 0), ("B", "ffn_conv_w", 1)],
    "attn_bwd2": [("B", "ffn_w_in", 1)],
    "convact_bwd0": [("A", "att_w_qkv", 0, (0, 2))],
    "ffn_in_dw0": [("A", "att_w_qkv", 0, (1, 2))],
    "s5_scan_bwd": [("A", "ffn_w_out", 0), ("A", "mix_w_out", 0), ("A", "s5_glu_w", 0), ("A", "ffn_conv_w", 0),
                    ("B", "att_w_qkv", 0)],
    "hgrn_bwd": [("A", "ffn_w_in", 0), ("B", "ffn_w_out", 0), ("B", "mix_w_out", 0), ("B", "s5_glu_w", 0), ("B", "ffn_conv_w", 0)],
    "mix_in_dx": [("B", "ffn_w_in", 0)],
    "adam_ffn_w_in": [("A", "mix_w_in", 0, (0, 2)), ("A", "small", 0)],
    "adam_ffn_w_out": [("A", "mix_w_in", 0, (1, 2))],
    "adam_att_w_qkv": [("B", "mix_w_in", 0), ("B", "small", 0)],
}


class Comm:
    def __init__(self, shards, shapes):
        self.shards, self.shapes = shards, shapes
        self.W, self.grads, self.slots = {}, {}, {}
        self.small = None

    def plan(self, host):
        items = SCHEDULE.get(host)
        if not items:
            return None
        p = Plan()
        for it in items:
            kind, name, l = it[:3]
            part, parts = it[3] if len(it) > 3 else (0, 1)
            if name == "small":
                kdst = p.buf("slots:small", arr=self.slots.get("small"), shape=S((8,) + self.small.shape, f32), write=True)
                if kind == "A":
                    ReduceOp(p, p.buf("g:small", arr=self.small), kdst, None, self.small.shape, False, 0, 0, whole=True)
                else:
                    ForwardOp(p, kdst, None, whole=True)
                continue
            nl, R, C_ = self.shapes[name]
            rows = name in ROW_SHARDED
            r0, nr = part * (R // parts), R // parts
            if kind == "G":
                sh = self.shards[name]
                kdst = p.buf(f"W:{name}:{l}", arr=self.W.get((name, l)), shape=S((4 * R, C_) if rows else (R, 4 * C_), sh.dtype),
                             write=True)
                GatherOp(p, p.buf("shard:" + name, arr=sh), kdst, l, self.shapes[name], rows, r0, nr, split=(nr % 32 == 0))
            else:
                g = self.grads[(name, l)]
                kdst = p.buf("slots:" + name, arr=self.slots.get(name), shape=S((8, nl, R, C_), g.dtype), write=True)
                if kind == "A":
                    ReduceOp(p, p.buf(f"g:{name}:{l}", arr=g), kdst, l, self.shapes[name], rows, r0, nr)
                else:
                    ForwardOp(p, kdst, l)
        return p

    def done(self, p):
        for k, arr in p.out.items():
            tag, name = k.split(":")[:2]
            if tag == "W":
                self.W[(name, int(k.split(":")[2]))] = arr
            else:
                self.slots[name] = arr


def _adamw(w, g, m, v):
    m = B1 * m + (1.0 - B1) * g
    v = B2 * v + (1.0 - B2) * jnp.square(g)
    m_hat = m / (1.0 - B1 ** STEP)
    v_hat = v / (1.0 - B2 ** STEP)
    return -LR * (m_hat / (jnp.sqrt(v_hat) + AEPS) + WD * w), m, v


def adam_big(w, m, v, slots, name, plan=None):
    nl, R, C = w.shape
    tr = 128 if R % 128 == 0 else (64 if R % 64 == 0 else R)

    def body(w_ref, m_ref, v_ref, s_ref, g_ref, d_ref, nm_ref, nv_ref):
        g = s_ref[0].astype(f32)
        for s in range(1, 8):
            g = g + s_ref[s].astype(f32)
        d, nm_, nv_ = _adamw(w_ref[...], g, m_ref[...], v_ref[...])
        g_ref[...] = g
        d_ref[...] = d
        nm_ref[...] = nm_
        nv_ref[...] = nv_

    blk = pl.BlockSpec((None, tr, C), lambda l, i: (l, i, 0))
    return pcall(body, plan, grid=(nl, R // tr),
                 in_specs=[blk, blk, blk, pl.BlockSpec((8, None, tr, C), lambda l, i: (0, l, i, 0))],
                 out_specs=[blk] * 4, out_shape=[S((nl, R, C), f32)] * 4,
                 sem=("parallel", "parallel"), name=name, args=[w, m, v, slots])


def adam_small(w, m, v, slots):
    R = w.shape[0]
    tr = 256

    def body(w_ref, m_ref, v_ref, s_ref, g_ref, d_ref, nm_ref, nv_ref):
        g = s_ref[0]
        for s in range(1, 8):
            g = g + s_ref[s]
        d, nm_, nv_ = _adamw(w_ref[...], g, m_ref[...], v_ref[...])
        g_ref[...] = g
        d_ref[...] = d
        nm_ref[...] = nm_
        nv_ref[...] = nv_

    blk = pl.BlockSpec((tr, 128), lambda i: (i, 0))
    return pl.pallas_call(
        body, grid=(R // tr,), in_specs=[blk, blk, blk, pl.BlockSpec((8, tr, 128), lambda i: (0, i, 0))],
        out_specs=[blk] * 4, out_shape=[S((R, 128), f32)] * 4,
        compiler_params=_cp(("parallel",)), name="adam_small")(w, m, v, slots)


def _pack(d):
    flat = jnp.concatenate([d[n].reshape(-1) for n in SMALL])
    n = flat.shape[0]
    rows = -(-n // (256 * 128)) * 256
    return jnp.pad(flat, (0, rows * 128 - n)).reshape(rows, 128)


def _unpack(p, like):
    flat = p.reshape(-1)
    out, off = {}, 0
    for n in SMALL:
        sz = math.prod(like[n].shape)
        out[n] = flat[off:off + sz].reshape(like[n].shape)
        off += sz
    return out


def kernel(x, positions, norm_mix, norm_ffn, norm_final, mix_w_in, mix_w_out, s5_A_re, s5_A_im, s5_log_dt, s5_B_re, s5_B_im, s5_C_re, s5_C_im, s5_D, s5_glu_w, s5_glu_b, hgrn_gamma, hgrn_norm, att_w_qkv, att_w_o, ffn_w_in, ffn_conv_w, ffn_conv_b, ffn_w_out, loss_target, m_norm_mix, m_norm_ffn, m_norm_final, m_mix_w_in, m_mix_w_out, m_s5_A_re, m_s5_A_im, m_s5_log_dt, m_s5_B_re, m_s5_B_im, m_s5_C_re, m_s5_C_im, m_s5_D, m_s5_glu_w, m_s5_glu_b, m_hgrn_gamma, m_hgrn_norm, m_att_w_qkv, m_att_w_o, m_ffn_w_in, m_ffn_conv_w, m_ffn_conv_b, m_ffn_w_out, v_norm_mix, v_norm_ffn, v_norm_final, v_mix_w_in, v_mix_w_out, v_s5_A_re, v_s5_A_im, v_s5_log_dt, v_s5_B_re, v_s5_B_im, v_s5_C_re, v_s5_C_im, v_s5_D, v_s5_glu_w, v_s5_glu_b, v_hgrn_gamma, v_hgrn_norm, v_att_w_qkv, v_att_w_o, v_ffn_w_in, v_ffn_conv_w, v_ffn_conv_b, v_ffn_w_out):
    a = dict(locals())
    weights = BIG + SMALL
    w = {n: a[n] for n in weights}
    m = {n: a["m_" + n] for n in weights}
    v = {n: a["v_" + n] for n in weights}
    shards = {"ffn_conv_w": ffn_conv_w}
    C = Comm(shards, {n: w[n].shape for n in BIG})
    for n in ("mix_w_in", "ffn_w_in", "mix_w_out", "s5_glu_w", "ffn_w_out", "att_w_qkv", "att_w_o"):
        shards[n] = hosted(C, "cast_" + n, lambda p: cast_bf16(w[n], "cast_" + n, plan=p))
    sm = {n: w[n] for n in SMALL}
    sm["ffn_conv_b3"] = ffn_conv_b.reshape(2, 1, 2 * DFF)
    loss, gx, _, gsmall = local_step(x[0], positions.reshape(L, 1), loss_target[0], sm, C.W, C)
    C.small = _pack(gsmall)
    res = {}
    for n in ("ffn_w_in", "ffn_w_out", "att_w_qkv", "att_w_o", "mix_w_out", "s5_glu_w", "ffn_conv_w", "mix_w_in"):
        res[n] = hosted(C, "adam_" + n, lambda p: adam_big(w[n], m[n], v[n], C.slots[n], "adam_" + n, plan=p))
    packed = adam_small(_pack({n: w[n] for n in SMALL}), _pack({n: m[n] for n in SMALL}), _pack({n: v[n] for n in SMALL}),
                        C.slots["small"])
    small_out = [_unpack(p, {n: w[n] for n in SMALL}) for p in packed]
    for n in SMALL:
        res[n] = tuple(so[n] for so in small_out)
    total = lax.psum(loss[0, 0], ("x", "y", "c"))
    order = ("norm_mix", "norm_ffn", "norm_final", "mix_w_in", "mix_w_out", "s5_A_re", "s5_A_im", "s5_log_dt", "s5_B_re", "s5_B_im",
             "s5_C_re", "s5_C_im", "s5_D", "s5_glu_w", "s5_glu_b", "hgrn_gamma", "hgrn_norm", "att_w_qkv", "att_w_o", "ffn_w_in",
             "ffn_conv_w", "ffn_conv_b", "ffn_w_out")
    return (total, gx[None], *[res[n][0] for n in order], *[res[n][1] for n in order], *[res[n][2] for n in order],
            *[res[n][3] for n in order])
```

```python
import functools
import math

import numpy as np
import jax
import jax.numpy as jnp
from jax import lax
from jax.experimental import pallas as pl
from jax.experimental.pallas import tpu as pltpu

f32 = jnp.float32
BF = jnp.bfloat16
HI = lax.Precision.HIGHEST
S = jax.ShapeDtypeStruct
MESH = pl.DeviceIdType.MESH

L = 2048
D = 1024
EPS = 1e-6
S5W = 512
NST = 2048
HGC = 64
DFF = 2816
ROPE_THETA = 500000.0
LR, B1, B2, AEPS, WD, STEP = 0.001, 0.9, 0.999, 1e-08, 0.01, 10
VMEM_LIMIT = 56 * 1024 * 1024


def _cp(sem=None):
    return pltpu.CompilerParams(dimension_semantics=sem, vmem_limit_bytes=VMEM_LIMIT)


ANY = pl.BlockSpec(memory_space=pl.ANY)
ROW_SHARDED = ("mix_w_out", "s5_glu_w", "ffn_w_out")


def _coords():
    x, y, c = lax.axis_index("x"), lax.axis_index("y"), lax.axis_index("c")
    return x, y, c, 2 * x + y, [(1 - x, y), (x, 1 - y), (1 - x, 1 - y)]


def _rows(start, n):
    return pl.ds(start if isinstance(start, int) else pl.multiple_of(start, 8), n)


def _cols(q, n):
    return pl.ds(pl.multiple_of(q * n, 128), n)


class Plan:
    def __init__(self):
        self.bufs, self.ops, self.nsem, self.out = {}, [], 0, {}

    def buf(self, key, arr=None, shape=None, write=False):
        b = self.bufs.setdefault(key, dict(arr=arr, shape=shape, write=False))
        b["write"] = b["write"] or write
        return key

    def add(self, op):
        op.base = self.nsem
        self.nsem += op.nsem
        self.ops.append(op)


class GatherOp:
    nsem = 13

    def __init__(self, plan, ksrc, kdst, l, shard_shape, rows, r0, nr, split):
        self.ksrc, self.kdst, self.l, (_, self.R, self.C), self.rows, self.r0, self.nr, self.split = (
            ksrc, kdst, l, shard_shape, rows, r0, nr, split)
        self.h = nr // 2 if split else nr
        plan.add(self)

    def _dst(self, R_, q, start, n):
        if self.rows:
            return R_[self.kdst].at[_rows(q * self.R + start, n), :]
        return R_[self.kdst].at[_rows(start, n), _cols(q, self.C)]

    def _mine(self, c):
        return self.r0 + (c * self.h if self.split else 0)

    def _theirs(self, c):
        return self.r0 + ((1 - c) * self.h if self.split else 0)

    def _copies(self, R_, sems):
        x, y, c, me, others = _coords()
        src = R_[self.ksrc]
        local = pltpu.make_async_copy(src.at[self.l, _rows(self.r0, self.nr), :], self._dst(R_, me, self.r0, self.nr),
                                      sems.at[self.base + 12])
        send, fwd = [], []
        for k, (px, py) in enumerate(others):
            q = 2 * px + py
            send.append((
                pltpu.make_async_remote_copy(src.at[self.l, _rows(self._mine(c), self.h), :], self._dst(R_, me, self._mine(c), self.h),
                                             sems.at[self.base + k], sems.at[self.base + 3 + k], device_id=(px, py, c), device_id_type=MESH),
                pltpu.make_async_remote_copy(src.at[self.l, _rows(self._mine(c), self.h), :], self._dst(R_, q, self._mine(c), self.h),
                                             sems.at[self.base + k], sems.at[self.base + 3 + k], device_id=(px, py, c), device_id_type=MESH)))
            fwd.append((
                pltpu.make_async_remote_copy(self._dst(R_, q, self._mine(c), self.h), self._dst(R_, q, self._mine(c), self.h),
                                             sems.at[self.base + 6 + k], sems.at[self.base + 9 + k], device_id=(x, y, 1 - c), device_id_type=MESH),
                pltpu.make_async_remote_copy(self._dst(R_, q, self._theirs(c), self.h), self._dst(R_, q, self._theirs(c), self.h),
                                             sems.at[self.base + 6 + k], sems.at[self.base + 9 + k], device_id=(x, y, 1 - c), device_id_type=MESH)))
        return local, send, fwd

    def start(self, R_, sems):
        local, send, _ = self._copies(R_, sems)
        local.start()
        for out, _ in send:
            out.start()

    def finish(self, R_, sems):
        local, send, fwd = self._copies(R_, sems)
        for k in range(3):
            send[k][1].wait_recv()
            if self.split:
                fwd[k][0].start()
        for k in range(3):
            if self.split:
                fwd[k][1].wait_recv()
                fwd[k][0].wait_send()
            send[k][0].wait_send()
        local.wait()


class ReduceOp:
    nsem = 7

    def __init__(self, plan, ksrc, kdst, l, shard_shape, rows, r0, nr, whole=False, half=False):
        self.ksrc, self.kdst, self.l, (self.R, self.C), self.rows, self.r0, self.nr, self.whole, self.half = (
            ksrc, kdst, l, shard_shape[-2:], rows, r0, nr, whole, half)
        plan.add(self)

    def _piece(self, R_, q):
        g = R_[self.ksrc]
        if self.whole:
            return g
        if self.rows:
            return g.at[_rows(q * (self.R // 2 if self.half else self.R) + self.r0, self.nr), :]
        return g.at[_rows(self.r0, self.nr), _cols(q, self.C)]

    def _slot(self, R_, q, c):
        if self.whole:
            return R_[self.kdst].at[2 * q + c]
        if self.half:
            return R_[self.kdst].at[q, self.l, _rows(c * (self.R // 2) + self.r0, self.nr), :]
        return R_[self.kdst].at[2 * q + c, self.l, _rows(self.r0, self.nr), :]

    def _copies(self, R_, sems):
        x, y, c, me, others = _coords()
        local = pltpu.make_async_copy(self._piece(R_, me), self._slot(R_, me, c), sems.at[self.base + 6])
        send = []
        for k, (px, py) in enumerate(others):
            q = 2 * px + py
            send.append((
                pltpu.make_async_remote_copy(self._piece(R_, q), self._slot(R_, me, c), sems.at[self.base + k],
                                             sems.at[self.base + 3 + k], device_id=(px, py, c), device_id_type=MESH),
                pltpu.make_async_remote_copy(self._piece(R_, q), self._slot(R_, q, c), sems.at[self.base + k],
                                             sems.at[self.base + 3 + k], device_id=(px, py, c), device_id_type=MESH)))
        return local, send

    def start(self, R_, sems):
        local, send = self._copies(R_, sems)
        local.start()
        for out, _ in send:
            out.start()

    def finish(self, R_, sems):
        local, send = self._copies(R_, sems)
        local.wait()
        for out, inn in send:
            inn.wait_recv()
            out.wait_send()


class ForwardOp:
    nsem = 8

    def __init__(self, plan, kdst, l, whole=False):
        self.kdst, self.l, self.whole = kdst, l, whole
        plan.add(self)

    def _slot(self, R_, s):
        return R_[self.kdst].at[s] if self.whole else R_[self.kdst].at[s, self.l]

    def _copies(self, R_, sems):
        x, y, c, me, others = _coords()
        return [(pltpu.make_async_remote_copy(self._slot(R_, 2 * q + c), self._slot(R_, 2 * q + c), sems.at[self.base + q],
                                              sems.at[self.base + 4 + q], device_id=(x, y, 1 - c), device_id_type=MESH),
                 pltpu.make_async_remote_copy(self._slot(R_, 2 * q + 1 - c), self._slot(R_, 2 * q + 1 - c), sems.at[self.base + q],
                                              sems.at[self.base + 4 + q], device_id=(x, y, 1 - c), device_id_type=MESH))
                for q in range(4)]

    def start(self, R_, sems):
        for out, _ in self._copies(R_, sems):
            out.start()

    def finish(self, R_, sems):
        for out, inn in self._copies(R_, sems):
            inn.wait_recv()
            out.wait_send()


class PairOp:
    nsem = 8

    def __init__(self, plan, ksrc, kdst, shard_shape, rows):
        self.ksrc, self.kdst, (self.R, self.C), self.rows = ksrc, kdst, shard_shape[-2:], rows
        plan.add(self)

    def _copies(self, R_, sems):
        x, y, c, me, others = _coords()
        g, dst, h = R_[self.ksrc], R_[self.kdst], self.R // 2
        out = []
        for q in range(4 if self.rows else 1):
            src = g.at[_rows(q * self.R + (1 - c) * h, h), :]
            land = dst.at[_rows(q * h, h), :]
            out.append(pltpu.make_async_remote_copy(src, land, sems.at[self.base + q], sems.at[self.base + 4 + q],
                                                    device_id=(x, y, 1 - c), device_id_type=MESH))
        return out

    def start(self, R_, sems):
        for cp in self._copies(R_, sems):
            cp.start()

    def finish(self, R_, sems):
        for cp in self._copies(R_, sems):
            cp.wait_recv()
            cp.wait_send()


class HalfForwardOp:
    nsem = 2

    def __init__(self, plan, kdst, l, shard_shape):
        self.kdst, self.l, self.R = kdst, l, shard_shape[-2]
        plan.add(self)

    def _copy(self, R_, sems, core):
        x, y, c, me, others = _coords()
        part = R_[self.kdst].at[:, self.l, _rows((c if core == "mine" else 1 - c) * (self.R // 2), self.R // 2), :]
        return pltpu.make_async_remote_copy(part, part, sems.at[self.base], sems.at[self.base + 1],
                                            device_id=(x, y, 1 - c), device_id_type=MESH)

    def start(self, R_, sems):
        self._copy(R_, sems, "mine").start()

    def finish(self, R_, sems):
        self._copy(R_, sems, "theirs").wait_recv()
        self._copy(R_, sems, "mine").wait_send()


def pair_sum(g, gsib, rows, R, name):
    h = R // 2
    W = g.shape[1]
    tr = h if h * W * 2 <= 2 ** 21 else 128
    nq = 4 if rows else 1

    def body(c_ref, a_ref, b_ref, o_ref):
        o_ref[...] = (a_ref[...].astype(f32) + b_ref[...].astype(f32)).astype(o_ref.dtype)

    half = pl.BlockSpec((tr, W), lambda q, i, c_ref: (q * (h // tr) + i, 0))
    mine = pl.BlockSpec((tr, W), lambda q, i, c_ref: (q * (R // tr) + c_ref[0] * (h // tr) + i, 0))
    return pl.pallas_call(
        body, grid_spec=pltpu.PrefetchScalarGridSpec(num_scalar_prefetch=1, grid=(nq, h // tr), in_specs=[mine, half],
                                                     out_specs=half),
        out_shape=S(gsib.shape, g.dtype), compiler_params=_cp(("parallel", "parallel")),
        name=name)(lax.axis_index("c").reshape(1).astype(jnp.int32), g, gsib)


def pcall(body, plan, *, grid, in_specs, out_specs, out_shape, scratch_shapes=(), sem, name, args):
    multi = isinstance(out_shape, (list, tuple))
    if plan is None or not plan.ops:
        return pl.pallas_call(body, grid=grid, in_specs=in_specs, out_specs=out_specs, out_shape=out_shape,
                              scratch_shapes=list(scratch_shapes), compiler_params=_cp(sem), name=name)(*args)
    outs = list(out_shape) if multi else [out_shape]
    ospecs = list(out_specs) if multi else [out_specs]
    kin = [k for k, b in plan.bufs.items() if b["arr"] is not None]
    kout = [k for k, b in plan.bufs.items() if b["write"]]
    n_in, n_out, n_scr = len(in_specs), len(outs), len(scratch_shapes)

    def wrapped(*refs):
        o0 = n_in + len(kin)
        s0 = o0 + n_out + len(kout)
        R_ = dict(zip(kin, refs[n_in:o0]))
        R_.update(zip(kout, refs[o0 + n_out:s0]))
        sems = refs[s0 + n_scr]
        first = functools.reduce(jnp.logical_and, [pl.program_id(d) == 0 for d in range(len(grid))])
        last = functools.reduce(jnp.logical_and, [pl.program_id(d) == grid[d] - 1 for d in range(len(grid))])

        @pl.when(first)
        def _():
            for op in plan.ops:
                op.start(R_, sems)

        body(*refs[:n_in], *refs[o0:o0 + n_out], *refs[s0:s0 + n_scr])

        @pl.when(last)
        def _():
            for op in plan.ops:
                op.finish(R_, sems)

    def shape_of(k):
        b = plan.bufs[k]
        return S(b["arr"].shape, b["arr"].dtype) if b["arr"] is not None else b["shape"]

    res = pl.pallas_call(
        wrapped, grid=grid, in_specs=list(in_specs) + [ANY] * len(kin), out_specs=ospecs + [ANY] * len(kout),
        out_shape=outs + [shape_of(k) for k in kout],
        scratch_shapes=list(scratch_shapes) + [pltpu.SemaphoreType.DMA((plan.nsem,))],
        input_output_aliases={n_in + kin.index(k): n_out + kout.index(k) for k in kout if plan.bufs[k]["arr"] is not None},
        compiler_params=pltpu.CompilerParams(dimension_semantics=("arbitrary",) * len(grid), vmem_limit_bytes=VMEM_LIMIT,
                                             has_side_effects=True),
        name=name)(*args, *[plan.bufs[k]["arr"] for k in kin])
    plan.out = dict(zip(kout, res[n_out:]))
    return list(res[:n_out]) if multi else res[0]


def _dg(a, b, ca, cb):
    return lax.dot_general(a.astype(BF), b.astype(BF), (((ca,), (cb,)), ((), ())), preferred_element_type=f32)


@jax.custom_vjp
def dot_nn(a, b):
    return _dg(a, b, 1, 0)


@jax.custom_vjp
def dot_nt(a, b):
    return _dg(a, b, 1, 1)


@jax.custom_vjp
def dot_tn(a, b):
    return _dg(a, b, 0, 0)


dot_nn.defvjp(lambda a, b: (dot_nn(a, b), (a, b)),
              lambda r, g: (dot_nt(g, r[1]).astype(r[0].dtype), dot_tn(r[0], g).astype(r[1].dtype)))
dot_nt.defvjp(lambda a, b: (dot_nt(a, b), (a, b)),
              lambda r, g: (dot_nn(g, r[1]).astype(r[0].dtype), dot_tn(g, r[0]).astype(r[1].dtype)))
dot_tn.defvjp(lambda a, b: (dot_tn(a, b), (a, b)),
              lambda r, g: (dot_nt(r[1], g).astype(r[0].dtype), dot_nn(r[0], g).astype(r[1].dtype)))


def matmul(a, b, *, mode, tm, tn, tk, out_dtype=f32, add=None, b_lead=None, a_spec=None, b_spec=None, dims=None, plan=None, name):
    a_over, b_over = a_spec, b_spec
    if mode == "nn":
        (M, K), N = a.shape[-2:], b.shape[-1]
        a_spec = pl.BlockSpec((tm, tk), lambda i, j, k: (i, k))
        b_blk, b_idx, ca, cb = (tk, tn), (lambda i, j, k: (k, j)), 1, 0
    elif mode == "nt":
        (M, K), N = a.shape[-2:], b.shape[-2]
        a_spec = pl.BlockSpec((tm, tk), lambda i, j, k: (i, k))
        b_blk, b_idx, ca, cb = (tn, tk), (lambda i, j, k: (j, k)), 1, 1
    else:
        (K, M), N = a.shape[-2:], b.shape[-1]
        a_spec = pl.BlockSpec((tk, tm), lambda i, j, k: (k, i))
        b_blk, b_idx, ca, cb = (tk, tn), (lambda i, j, k: (k, j)), 0, 0
    if dims is not None:
        M, N, K = dims
    assert M % tm == 0 and N % tn == 0 and K % tk == 0, (name, M, N, K, tm, tn, tk)
    if b_lead is None:
        b_spec = pl.BlockSpec(b_blk, b_idx)
    else:
        b_spec = pl.BlockSpec((None,) + b_blk, lambda i, j, k: (b_lead,) + b_idx(i, j, k))
    if a_over is not None:
        a_spec = a_over
    if b_over is not None:
        b_spec = b_over
    nk = K // tk
    has_add = add is not None

    def body(*refs):
        a_ref, b_ref = refs[0], refs[1]
        add_ref = refs[2] if has_add else None
        o_ref = refs[2 + has_add]
        p = _dg(a_ref[...], b_ref[...], ca, cb)

        def fin(v):
            if has_add:
                v = v + add_ref[...].astype(f32)
            o_ref[...] = v.astype(o_ref.dtype)

        if nk == 1:
            fin(p)
        else:
            acc = refs[3 + has_add]
            k = pl.program_id(2)

            @pl.when(k == 0)
            def _():
                acc[...] = p

            @pl.when(k > 0)
            def _():
                acc[...] += p

            @pl.when(k == nk - 1)
            def _():
                fin(acc[...])

    in_specs = [a_spec, b_spec]
    args = [a, b]
    if has_add:
        in_specs.append(pl.BlockSpec((tm, tn), lambda i, j, k: (i, j)))
        args.append(add)
    return pcall(body, plan, grid=(M // tm, N // tn, nk), in_specs=in_specs,
                 out_specs=pl.BlockSpec((tm, tn), lambda i, j, k: (i, j)), out_shape=S((M, N), out_dtype),
                 scratch_shapes=[pltpu.VMEM((tm, tn), f32)] if nk > 1 else [],
                 sem=("parallel", "parallel", "arbitrary"), name=name, args=args)


def _rms(xv, gv):
    return xv * lax.rsqrt(jnp.mean(xv * xv, axis=-1, keepdims=True) + EPS) * gv


TR = 256


def rms_fwd(x, g, name):
    def body(x_ref, g_ref, o_ref):
        o_ref[...] = _rms(x_ref[...], g_ref[...]).astype(o_ref.dtype)

    return pl.pallas_call(
        body, grid=(L // TR,),
        in_specs=[pl.BlockSpec((TR, D), lambda i: (i, 0)), pl.BlockSpec((1, D), lambda i: (0, 0))],
        out_specs=pl.BlockSpec((TR, D), lambda i: (i, 0)), out_shape=S((L, D), BF),
        compiler_params=_cp(("parallel",)), name=name)(x, g)


def rms_bwd(x, g, dys, dres, name, plan=None):
    nd = len(dys)

    def body(*refs):
        x_ref, g_ref = refs[0], refs[1]
        dr_ref, dh_ref, dg_ref = refs[2 + nd:]
        dy = refs[2][...].astype(f32)
        for r in refs[3:2 + nd]:
            dy = dy + r[...].astype(f32)
        _, vjp = jax.vjp(_rms, x_ref[...], g_ref[...])
        dx, dg = vjp(dy)
        dh_ref[...] = dr_ref[...] + dx

        @pl.when(pl.program_id(0) == 0)
        def _():
            dg_ref[...] = jnp.zeros_like(dg_ref)

        dg_ref[...] += dg

    row = pl.BlockSpec((TR, D), lambda i: (i, 0))
    vec = pl.BlockSpec((1, D), lambda i: (0, 0))
    return pcall(body, plan, grid=(L // TR,), in_specs=[row, vec] + [row] * (nd + 1), out_specs=[row, vec],
                 out_shape=[S((L, D), f32), S((1, D), f32)], sem=("arbitrary",), name=name, args=[x, g, *dys, dres])


def loss_head(h, g, tgt):
    def f(hv, gv, tv):
        y = _rms(hv, gv)
        return 0.5 * jnp.sum(jnp.mean(jnp.square(y - tv), axis=-1))

    def body(h_ref, g_ref, t_ref, l_ref, dh_ref, dg_ref):
        val, vjp = jax.vjp(f, h_ref[...], g_ref[...], t_ref[...])
        dh, dg, _ = vjp(jnp.ones((), f32))
        dh_ref[...] = dh

        @pl.when(pl.program_id(0) == 0)
        def _():
            dg_ref[...] = jnp.zeros_like(dg_ref)
            l_ref[...] = jnp.zeros_like(l_ref)

        dg_ref[...] += dg
        l_ref[...] += jnp.full((1, 128), val, f32)

    row = pl.BlockSpec((TR, D), lambda i: (i, 0))
    vec = pl.BlockSpec((1, D), lambda i: (0, 0))
    return pl.pallas_call(
        body, grid=(L // TR,), in_specs=[row, vec, row],
        out_specs=[pl.BlockSpec((1, 128), lambda i: (0, 0)), row, vec],
        out_shape=[S((1, 128), f32), S((L, D), f32), S((1, D), f32)],
        compiler_params=_cp(("arbitrary",)), name="loss_head")(h, g, tgt)


def _col_to_row(c):
    n = c.shape[0]
    t = jnp.broadcast_to(c, (n, 128)).T
    r = lax.broadcasted_iota(jnp.int32, (128, n), 0)
    return jnp.sum(jnp.where(r == 0, t, 0.0), axis=0, keepdims=True)


def _s5_param_map(are, aim, ldt_row, bre, bim, cre, cim):
    n = NST
    gi = lax.broadcasted_iota(jnp.int32, (n, 32), 0) // 64
    gj = lax.broadcasted_iota(jnp.int32, (n, 32), 1)
    ldt = jnp.sum(jnp.where(gi == gj, ldt_row, 0.0), axis=1, keepdims=True)
    dt = jnp.exp(ldt)
    mag = jnp.exp(are * dt)
    abr = mag * jnp.cos(aim * dt)
    abi = mag * jnp.sin(aim * dt)
    den = are * are + aim * aim
    nr, ni = abr - 1.0, abi
    cr = (nr * are + ni * aim) / den
    ci = (ni * are - nr * aim) / den
    bbr = cr * bre - ci * bim
    bbi = cr * bim + ci * bre
    tc = lax.broadcasted_iota(jnp.int32, (16, 128), 0)
    tl = lax.broadcasted_iota(jnp.int32, (16, 128), 1)
    T = (tl % 16 == tc).astype(f32)
    mr = (lax.broadcasted_iota(jnp.int32, (n, 128), 0) // 64) % 8
    mc = lax.broadcasted_iota(jnp.int32, (n, 128), 1) // 16
    mask = (mr == mc).astype(f32)

    def expand(v):
        return jnp.dot(v, T, precision=HI, preferred_element_type=f32) * mask

    return expand(bbr), expand(bbi), expand(cre), expand(cim), _col_to_row(abr), _col_to_row(abi)


def s5_params_fwd(are, aim, ldt_row, bre, bim, cre, cim):
    def body(*refs):
        outs = _s5_param_map(*[r[...] for r in refs[:7]])
        for o_ref, o in zip(refs[7:], outs):
            o_ref[...] = o

    return pl.pallas_call(
        body, out_shape=[S((NST, 128), f32)] * 4 + [S((1, NST), f32)] * 2,
        compiler_params=_cp(), name="s5_params_fwd")(are, aim, ldt_row, bre, bim, cre, cim)


def s5_params_bwd(are, aim, ldt_row, bre, bim, cre, cim, cots):
    def body(*refs):
        _, vjp = jax.vjp(_s5_param_map, *[r[...] for r in refs[:7]])
        gs = vjp(tuple(r[...] for r in refs[7:13]))
        for o_ref, o in zip(refs[13:], gs):
            o_ref[...] = o

    return pl.pallas_call(
        body, out_shape=[S((NST, 1), f32)] * 2 + [S((1, 32), f32)] + [S((NST, 16), f32)] * 4,
        compiler_params=_cp(), name="s5_params_bwd")(are, aim, ldt_row, bre, bim, cre, cim, *cots)


def _cpowers(ar, ai):
    out = [(ar, ai)]
    for _ in range(7):
        pr, pi = out[-1]
        out.append((pr * ar - pi * ai, pr * ai + pi * ar))
    return out


def _ctable(pw, rid, power):
    tr_ = jnp.zeros(rid.shape, f32)
    ti_ = jnp.zeros(rid.shape, f32)
    for r in range(8):
        pr, pi = pw[power(r) - 1]
        tr_ = jnp.where(rid == r, pr, tr_)
        ti_ = jnp.where(rid == r, pi, ti_)
    return tr_, ti_


NT5 = 4
RC = 256


def s5_scan_fwd(proj, wbr, wbi, wcr, wci, abr, abi, drow, plan=None):
    def body(u_ref, wbr_ref, wbi_ref, wcr_ref, wci_ref, ar_ref, ai_ref, d_ref, xr_ref, xi_ref, y_ref):
        wbr_v, wbi_v = wbr_ref[...], wbi_ref[...]
        for r in range(L // RC):
            rows = pl.ds(r * RC, RC)
            ub = u_ref[rows, :]
            xr_ref[rows, :] = dot_nt(ub, wbr_v)
            xi_ref[rows, :] = dot_nt(ub, wbi_v)
        pw = _cpowers(ar_ref[...], ai_ref[...])
        rid = lax.broadcasted_iota(jnp.int32, (8, 512), 0)
        tr_, ti_ = _ctable(pw, rid, lambda r: r + 1)

        def group(j, c):
            cr, ci = c
            rows = pl.ds(pl.multiple_of(j * 8, 8), 8)
            br, bi = xr_ref[rows, :], xi_ref[rows, :]
            for s in (1, 2, 4):
                pr, pi = pw[s - 1]
                sr = jnp.where(rid >= s, pltpu.roll(br, s, 0), 0.0)
                si = jnp.where(rid >= s, pltpu.roll(bi, s, 0), 0.0)
                br, bi = br + pr * sr - pi * si, bi + pr * si + pi * sr
            br, bi = br + tr_ * cr - ti_ * ci, bi + tr_ * ci + ti_ * cr
            xr_ref[rows, :] = br
            xi_ref[rows, :] = bi
            return br[7:8], bi[7:8]

        z = jnp.zeros((1, 512), f32)
        lax.fori_loop(0, L // 8, group, (z, z), unroll=2)
        wcr_v, wci_v, dv = wcr_ref[...], wci_ref[...], d_ref[...]
        for r in range(L // RC):
            rows = pl.ds(r * RC, RC)
            y_ref[rows, :] = (dot_nn(xr_ref[rows, :], wcr_v) - dot_nn(xi_ref[rows, :], wci_v)
                              + dv * u_ref[rows, :])

    wspec = pl.BlockSpec((512, 128), lambda j: (j, 0))
    aspec = pl.BlockSpec((1, 512), lambda j: (0, j))
    return pcall(
        body, plan, grid=(NT5,),
        in_specs=[pl.BlockSpec((L, 128), lambda j: (0, j)), wspec, wspec, wspec, wspec, aspec, aspec,
                  pl.BlockSpec((1, 128), lambda j: (0, j))],
        out_specs=[pl.BlockSpec((L, 512), lambda j: (0, j)), pl.BlockSpec((L, 512), lambda j: (0, j)),
                   pl.BlockSpec((L, 128), lambda j: (0, j))],
        out_shape=[S((L, NST), f32), S((L, NST), f32), S((L, S5W), f32)],
        sem=("parallel",), name="s5_scan_fwd", args=[proj, wbr, wbi, wcr, wci, abr, abi, drow])


def s5_scan_bwd(dy, proj, xs_re, xs_im, wbr, wbi, wcr, wci, abr, abi, drow, plan=None):
    def body(dy_ref, u_ref, xr_ref, xi_ref, wbr_ref, wbi_ref, wcr_ref, wci_ref, ar_ref, ai_ref, d_ref,
             du_ref, gwbr_ref, gwbi_ref, gwcr_ref, gwci_ref, gar_ref, gai_ref, gd_ref, lr_ref, li_ref):
        wcr_v, wci_v = wcr_ref[...], wci_ref[...]
        gwcr = jnp.zeros((512, 128), f32)
        gwci = jnp.zeros((512, 128), f32)
        gd = jnp.zeros((1, 128), f32)
        for r in range(L // RC):
            rows = pl.ds(r * RC, RC)
            dyv = dy_ref[rows, :]
            lr_ref[rows, :] = dot_nt(dyv, wcr_v)
            li_ref[rows, :] = -dot_nt(dyv, wci_v)
            gwcr += dot_tn(xr_ref[rows, :], dyv)
            gwci -= dot_tn(xi_ref[rows, :], dyv)
            gd += jnp.sum(dyv * u_ref[rows, :], axis=0, keepdims=True)
        gwcr_ref[...] = gwcr
        gwci_ref[...] = gwci
        gd_ref[...] = gd
        pw = _cpowers(ar_ref[...], -ai_ref[...])
        rid = lax.broadcasted_iota(jnp.int32, (8, 512), 0)
        tr_, ti_ = _ctable(pw, rid, lambda r: 8 - r)

        def group(i, c):
            cr, ci, gar, gai = c
            j = L // 8 - 1 - i
            rows = pl.ds(pl.multiple_of(j * 8, 8), 8)
            br, bi = lr_ref[rows, :], li_ref[rows, :]
            for s in (1, 2, 4):
                pr, pi = pw[s - 1]
                sr = jnp.where(rid < 8 - s, pltpu.roll(br, 8 - s, 0), 0.0)
                si = jnp.where(rid < 8 - s, pltpu.roll(bi, 8 - s, 0), 0.0)
                br, bi = br + pr * sr - pi * si, bi + pr * si + pi * sr
            br, bi = br + tr_ * cr - ti_ * ci, bi + tr_ * ci + ti_ * cr
            lr_ref[rows, :] = br
            li_ref[rows, :] = bi
            nr = jnp.where(rid < 7, pltpu.roll(br, 7, 0), cr)
            ni = jnp.where(rid < 7, pltpu.roll(bi, 7, 0), ci)
            xr, xi = xr_ref[rows, :], xi_ref[rows, :]
            return br[0:1], bi[0:1], gar + xr * nr + xi * ni, gai + xr * ni - xi * nr

        z = jnp.zeros((1, 512), f32)
        z8 = jnp.zeros((8, 512), f32)
        _, _, gar, gai = lax.fori_loop(0, L // 8, group, (z, z, z8, z8), unroll=2)
        gar_ref[...] = jnp.sum(gar, axis=0, keepdims=True)
        gai_ref[...] = jnp.sum(gai, axis=0, keepdims=True)
        wbr_v, wbi_v, dv = wbr_ref[...], wbi_ref[...], d_ref[...]
        gwbr = jnp.zeros((512, 128), f32)
        gwbi = jnp.zeros((512, 128), f32)
        for r in range(L // RC):
            rows = pl.ds(r * RC, RC)
            lrv, liv, uv = lr_ref[rows, :], li_ref[rows, :], u_ref[rows, :]
            du_ref[rows, :] = (dot_nn(lrv, wbr_v) + dot_nn(liv, wbi_v) + dv * dy_ref[rows, :]).astype(du_ref.dtype)
            gwbr += dot_tn(lrv, uv)
            gwbi += dot_tn(liv, uv)
        gwbr_ref[...] = gwbr
        gwbi_ref[...] = gwbi

    wspec = pl.BlockSpec((512, 128), lambda j: (j, 0))
    aspec = pl.BlockSpec((1, 512), lambda j: (0, j))
    col = pl.BlockSpec((L, 128), lambda j: (0, j))
    st = pl.BlockSpec((L, 512), lambda j: (0, j))
    dspec = pl.BlockSpec((1, 128), lambda j: (0, j))
    return pcall(
        body, plan, grid=(NT5,),
        in_specs=[col, col, st, st, wspec, wspec, wspec, wspec, aspec, aspec, dspec],
        out_specs=[col, wspec, wspec, wspec, wspec, aspec, aspec, dspec],
        out_shape=[S((L, S5W), BF)] + [S((NST, 128), f32)] * 4 + [S((1, NST), f32)] * 2 + [S((1, S5W), f32)],
        scratch_shapes=[pltpu.VMEM((L, 512), f32), pltpu.VMEM((L, 512), f32)],
        sem=("parallel",), name="s5_scan_bwd", args=[dy, proj, xs_re, xs_im, wbr, wbi, wcr, wci, abr, abi, drow])


def _glu(y, w, b):
    z = jax.nn.gelu(y)
    return z * jax.nn.sigmoid(dot_nn(z, w) + b)


def s5_glu_fwd(y, w, b):
    def body(y_ref, w_ref, b_ref, o_ref):
        o_ref[...] = _glu(y_ref[...], w_ref[...], b_ref[...]).astype(o_ref.dtype)

    return pl.pallas_call(
        body, grid=(L // TR,),
        in_specs=[pl.BlockSpec((TR, S5W), lambda i: (i, 0)), pl.BlockSpec((S5W, S5W), lambda i: (0, 0)),
                  pl.BlockSpec((1, S5W), lambda i: (0, 0))],
        out_specs=pl.BlockSpec((TR, S5W), lambda i: (i, 0)), out_shape=S((L, S5W), BF),
        compiler_params=_cp(("parallel",)), name="s5_glu_fwd")(y, w, b)


def s5_glu_bwd(y, w, b, dmix):
    def body(y_ref, w_ref, b_ref, g_ref, dy_ref, dw_ref, db_ref):
        _, vjp = jax.vjp(_glu, y_ref[...], w_ref[...].astype(f32), b_ref[...])
        dy, dw, db = vjp(g_ref[...])
        dy_ref[...] = dy

        @pl.when(pl.program_id(0) == 0)
        def _():
            dw_ref[...] = jnp.zeros_like(dw_ref)
            db_ref[...] = jnp.zeros_like(db_ref)

        dw_ref[...] += dw
        db_ref[...] += db

    row = pl.BlockSpec((TR, S5W), lambda i: (i, 0))
    return pl.pallas_call(
        body, grid=(L // TR,),
        in_specs=[row, pl.BlockSpec((S5W, S5W), lambda i: (0, 0)), pl.BlockSpec((1, S5W), lambda i: (0, 0)), row],
        out_specs=[row, pl.BlockSpec((S5W, S5W), lambda i: (0, 0)), pl.BlockSpec((1, S5W), lambda i: (0, 0))],
        out_shape=[S((L, S5W), f32), S((S5W, S5W), f32), S((1, S5W), f32)],
        compiler_params=_cp(("arbitrary",)), name="s5_glu_bwd")(y, w, b, dmix)


def _dg3(a, b, ca, cb):
    ah, bh = a.astype(BF), b.astype(BF)
    al, bl = (a - ah.astype(f32)).astype(BF), (b - bh.astype(f32)).astype(BF)
    return _dg(ah, bh, ca, cb) + _dg(ah, bl, ca, cb) + _dg(al, bh, ca, cb)


@jax.custom_vjp
def hi_nn(a, b):
    return _dg3(a, b, 1, 0)


@jax.custom_vjp
def hi_nt(a, b):
    return _dg3(a, b, 1, 1)


@jax.custom_vjp
def hi_tn(a, b):
    return _dg3(a, b, 0, 0)


hi_nn.defvjp(lambda a, b: (hi_nn(a, b), (a, b)), lambda r, g: (hi_nt(g, r[1]), hi_tn(r[0], g)))
hi_nt.defvjp(lambda a, b: (hi_nt(a, b), (a, b)), lambda r, g: (hi_nn(g, r[1]), hi_tn(g, r[0])))
hi_tn.defvjp(lambda a, b: (hi_tn(a, b), (a, b)), lambda r, g: (hi_nt(r[1], g), hi_nn(r[0], g)))


def _hgrn_chunk(St, xq, xf, xi, xg, gam, ng):
    lb = jax.nn.sigmoid(gam[0:1] - gam[1:2])
    q = jax.nn.silu(xq)
    f = lb + (1.0 - lb) * jax.nn.sigmoid(xf)
    k = 1.0 - f
    g = jnp.log(f)
    ti = lax.broadcasted_iota(jnp.int32, (HGC, HGC), 0)
    si = lax.broadcasted_iota(jnp.int32, (HGC, HGC), 1)
    causal = si <= ti
    b = jnp.dot(causal.astype(f32), g, precision=HI, preferred_element_type=f32)
    qe = q * jnp.exp(b)
    o = dot_nt(qe, St)
    att = jnp.where(causal, hi_nt(qe, k * jnp.exp(-b)), 0.0)
    o = o + dot_nn(att, xi)
    bl = b[HGC - 1:HGC]
    St_new = St * jnp.exp(bl) + dot_tn(xi, k * jnp.exp(bl - b))
    o = o * lax.rsqrt(jnp.mean(o * o, axis=-1, keepdims=True) + EPS) * ng
    return St_new, o * jax.nn.silu(xg)


NCH = L // HGC


def hgrn_fwd(proj, gamma, hnorm, plan=None):
    def body(q_ref, f_ref, i_ref, g_ref, gam_ref, ng_ref, o_ref, ss_ref, st):
        @pl.when(pl.program_id(0) == 0)
        def _():
            st[...] = jnp.zeros_like(st)

        for h in range(4):
            sl = slice(h * 128, (h + 1) * 128)
            s0 = st[h]
            ss_ref[0, h] = s0
            s1, o = _hgrn_chunk(s0, q_ref[:, sl], f_ref[:, sl], i_ref[:, sl], g_ref[:, sl], gam_ref[:, sl], ng_ref[:, sl])
            st[h] = s1
            o_ref[:, sl] = o.astype(o_ref.dtype)

    def pj(n):
        return pl.BlockSpec((HGC, 512), lambda c: (c, n))

    return pcall(
        body, plan, grid=(NCH,),
        in_specs=[pj(1), pj(2), pj(3), pj(4), pl.BlockSpec((2, 512), lambda c: (0, 0)), pl.BlockSpec((1, 512), lambda c: (0, 0))],
        out_specs=[pl.BlockSpec((HGC, 512), lambda c: (c, 0)), pl.BlockSpec((1, 4, 128, 128), lambda c: (c, 0, 0, 0))],
        out_shape=[S((L, 512), BF), S((NCH, 4, 128, 128), f32)],
        scratch_shapes=[pltpu.VMEM((4, 128, 128), f32)],
        sem=("arbitrary",), name="hgrn_fwd", args=[proj, proj, proj, proj, gamma, hnorm])


def hgrn_bwd(proj, gamma, hnorm, ssave, dmix, du, plan=None):
    def body(q_ref, f_ref, i_ref, g_ref, gam_ref, ng_ref, ss_ref, do_ref, du_ref, dp_ref, dgam_ref, dng_ref, dst):
        @pl.when(pl.program_id(0) == 0)
        def _():
            dst[...] = jnp.zeros_like(dst)
            dgam_ref[...] = jnp.zeros_like(dgam_ref)
            dng_ref[...] = jnp.zeros_like(dng_ref)

        dp_ref[:, 0:512] = du_ref[...]
        for h in range(4):
            sl = slice(h * 128, (h + 1) * 128)
            _, vjp = jax.vjp(_hgrn_chunk, ss_ref[0, h], q_ref[:, sl], f_ref[:, sl], i_ref[:, sl], g_ref[:, sl],
                             gam_ref[:, sl], ng_ref[:, sl])
            ds, dq, df, di, dg, dgam, dng = vjp((dst[h], do_ref[:, sl]))
            dst[h] = ds
            for n, v in enumerate((dq, df, di, dg)):
                dp_ref[:, 512 * (n + 1) + h * 128: 512 * (n + 1) + (h + 1) * 128] = v.astype(dp_ref.dtype)
            dgam_ref[:, sl] += dgam
            dng_ref[:, sl] += dng

    def pj(n):
        return pl.BlockSpec((HGC, 512), lambda i: (NCH - 1 - i, n))

    return pcall(
        body, plan, grid=(NCH,),
        in_specs=[pj(1), pj(2), pj(3), pj(4), pl.BlockSpec((2, 512), lambda i: (0, 0)), pl.BlockSpec((1, 512), lambda i: (0, 0)),
                  pl.BlockSpec((1, 4, 128, 128), lambda i: (NCH - 1 - i, 0, 0, 0)), pj(1), pj(0)],
        out_specs=[pl.BlockSpec((HGC, 2560), lambda i: (NCH - 1 - i, 0)), pl.BlockSpec((2, 512), lambda i: (0, 0)),
                   pl.BlockSpec((1, 512), lambda i: (0, 0))],
        out_shape=[S((L, 2560), BF), S((2, 512), f32), S((1, 512), f32)],
        scratch_shapes=[pltpu.VMEM((4, 128, 128), f32)],
        sem=("arbitrary",), name="hgrn_bwd", args=[proj, proj, proj, proj, gamma, hnorm, ssave, dmix, du])


def _earlier(h_ref, k, r0, n):
    if r0 > 0:
        return h_ref[pl.ds(r0 - k, n), :]
    rid = lax.broadcasted_iota(jnp.int32, (8, h_ref.shape[1]), 0)
    head = jnp.where(rid >= k, pltpu.roll(h_ref[pl.ds(0, 8), :], k, 0), 0.0)
    return jnp.concatenate([head, h_ref[pl.ds(8 - k, n - 8), :]], axis=0)


def _conv3_rows(h_ref, w, b, r0, n=None):
    n = CR if n is None else n
    h1, h2 = _earlier(h_ref, 1, r0, n), _earlier(h_ref, 2, r0, n)
    return w[2:3] * h_ref[pl.ds(r0, n), :] + w[1:2] * h1 + w[0:1] * h2 + b, h1, h2


CT = 128
NCT = DFF // CT
CR = 64


def convact_fwd(hu, cw, cb, layer, plan=None):
    def body(ha_ref, hb_ref, wa_ref, wb_ref, ba_ref, bb_ref, o_ref):
        ca = _conv3_rows(ha_ref, wa_ref[...], ba_ref[...], 0, L)[0]
        cb_ = _conv3_rows(hb_ref, wb_ref[...], bb_ref[...], 0, L)[0]
        o_ref[...] = (jax.nn.silu(ca) * cb_).astype(o_ref.dtype)

    def h(off):
        return pl.BlockSpec((L, CT), lambda j: (0, j + off))

    def w(off):
        return pl.BlockSpec((3, CT), lambda j: (0, j + off))

    def b(off):
        return pl.BlockSpec((None, 1, CT), lambda j: (layer, 0, j + off))

    return pcall(body, plan, grid=(NCT,), in_specs=[h(0), h(NCT), w(0), w(NCT), b(0), b(NCT)],
                 out_specs=pl.BlockSpec((L, CT), lambda j: (0, j)), out_shape=S((L, DFF), BF),
                 sem=("parallel",), name=f"convact_fwd{layer}", args=[hu, hu, cw, cw, cb, cb])


def convact_bwd(hu, cw, cb, dact, layer, plan=None):
    def body(ha_ref, hb_ref, wa_ref, wb_ref, ba_ref, bb_ref, g_ref, dh_ref, dw_ref, db_ref, sh, sw, sb, da_scr, db_scr):
        j = pl.program_id(0)

        def fold(x):
            return functools.reduce(jnp.add, [x[8 * m:8 * m + 8] for m in range(CR // 8)])

        @pl.when(j < NCT)
        def _():
            wa, wb, ba, bb = wa_ref[...], wb_ref[...], ba_ref[...], bb_ref[...]
            da_scr[pl.ds(L, 8), :] = jnp.zeros((8, CT), f32)
            db_scr[pl.ds(L, 8), :] = jnp.zeros((8, CT), f32)
            acc = [jnp.zeros((8, CT), f32) for _ in range(8)]
            for c in range(L // CR):
                r0 = c * CR
                ca, a1, a2 = _conv3_rows(ha_ref, wa, ba, r0)
                cb_, b1, b2 = _conv3_rows(hb_ref, wb, bb, r0)
                g = g_ref[pl.ds(r0, CR), :].astype(f32)
                sg = jax.nn.sigmoid(ca)
                dca = g * cb_ * (sg * (1.0 + ca * (1.0 - sg)))
                dcb = g * (ca * sg)
                da_scr[pl.ds(r0, CR), :] = dca
                db_scr[pl.ds(r0, CR), :] = dcb
                terms = (dca * a2, dca * a1, dca * ha_ref[pl.ds(r0, CR), :], dca,
                         dcb * b2, dcb * b1, dcb * hb_ref[pl.ds(r0, CR), :], dcb)
                acc = [a + fold(t) for a, t in zip(acc, terms)]
            rows = [jnp.sum(a, axis=0, keepdims=True) for a in acc]
            for k in range(3):
                dw_ref[k:k + 1, :] = rows[k]
                sw[j, k:k + 1, :] = rows[4 + k]
            db_ref[...] = rows[3]
            sb[j] = rows[7]
            for c in range(L // CR):
                r0 = c * CR
                for scr, w, out in ((da_scr, wa, dh_ref), (db_scr, wb, sh.at[j])):
                    dh = (w[2:3] * scr[pl.ds(r0, CR), :] + w[1:2] * scr[pl.ds(r0 + 1, CR), :]
                          + w[0:1] * scr[pl.ds(r0 + 2, CR), :])
                    out[pl.ds(r0, CR), :] = dh.astype(out.dtype)

        @pl.when(j >= NCT)
        def _():
            dh_ref[...] = sh[j - NCT]
            dw_ref[...] = sw[j - NCT]
            db_ref[...] = sb[j - NCT]

    def lo(j):
        return jnp.minimum(j, NCT - 1)

    in_specs = [pl.BlockSpec((L, CT), lambda j: (0, lo(j))), pl.BlockSpec((L, CT), lambda j: (0, lo(j) + NCT)),
                pl.BlockSpec((3, CT), lambda j: (0, lo(j))), pl.BlockSpec((3, CT), lambda j: (0, lo(j) + NCT)),
                pl.BlockSpec((None, 1, CT), lambda j: (layer, 0, lo(j))), pl.BlockSpec((None, 1, CT), lambda j: (layer, 0, lo(j) + NCT)),
                pl.BlockSpec((L, CT), lambda j: (0, lo(j)))]
    return pcall(
        body, plan, grid=(2 * NCT,), in_specs=in_specs,
        out_specs=[pl.BlockSpec((L, CT), lambda j: (0, j)), pl.BlockSpec((3, CT), lambda j: (0, j)), pl.BlockSpec((1, CT), lambda j: (0, j))],
        out_shape=[S((L, 2 * DFF), BF), S((3, 2 * DFF), f32), S((1, 2 * DFF), f32)],
        scratch_shapes=[pltpu.VMEM((NCT, L, CT), BF), pltpu.VMEM((NCT, 3, CT), f32), pltpu.VMEM((NCT, 1, CT), f32),
                        pltpu.VMEM((L + 8, CT), f32), pltpu.VMEM((L + 8, CT), f32)],
        sem=("arbitrary",), name=f"convact_bwd{layer}", args=[hu, hu, cw, cw, cb, cb, dact])


DILS = (1, 4, 16)
AB = 128
NPAIR = 12


def _rope_tables(pos_ref, invf_ref):
    ang = pos_ref[...].astype(f32) * invf_ref[...]
    lane = lax.broadcasted_iota(jnp.int32, (1, 128), 1) % 64
    cosf = jnp.where(lane < 16, jnp.cos(ang), 1.0)
    sn = jnp.sin(ang)
    s_lo = jnp.where(lane < 8, -sn, 0.0)
    s_hi = jnp.where((lane >= 8) & (lane < 16), sn, 0.0)
    return cosf, s_lo, s_hi


def _rope(t, cosf, s_lo, s_hi):
    return t * cosf + pltpu.roll(t, 120, 1) * s_lo + pltpu.roll(t, 8, 1) * s_hi


def _rope_t(g, cosf, s_lo, s_hi):
    return g * cosf + pltpu.roll(g * s_lo, 8, 1) + pltpu.roll(g * s_hi, 120, 1)


def _att_block(q2, kp, kc, vp, vc, first):
    lane = lax.broadcasted_iota(jnp.int32, (1, 128), 1)
    qi = lax.broadcasted_iota(jnp.int32, (AB, 2 * AB), 0) + AB
    kj = lax.broadcasted_iota(jnp.int32, (AB, 2 * AB), 1)
    back = qi - kj
    valid = (back >= 0) & (back <= AB)
    if first:
        valid = valid & (kj >= AB)
    kk = jnp.concatenate([kp, kc], axis=0)
    vv = jnp.concatenate([vp, vc], axis=0)
    o2 = jnp.zeros((AB, 128), f32)
    lse2 = jnp.zeros((AB, 128), f32)
    for e in range(2):
        hm = ((lane >= 64 * e) & (lane < 64 * (e + 1))).astype(f32)
        s = dot_nt(q2 * (hm * 0.125), kk)
        s = jnp.where(valid, s, -jnp.inf)
        m = jnp.max(s, axis=-1, keepdims=True)
        p = jnp.exp(s - m)
        den = jnp.sum(p, axis=-1, keepdims=True)
        o2 = o2 + dot_nn(p, vv * hm) / den
        lse2 = lse2 + (m + jnp.log(den)) * hm
    return o2, lse2


def _att_blocks(dil):
    m = L // dil
    return [(r * m + n * AB, n == 0) for r in range(dil) for n in range(m // AB)]


def deinterleave(x, dil):
    return x if dil == 1 else x.reshape(L // dil, dil, x.shape[1]).swapaxes(0, 1).reshape(L, x.shape[1])


def attn_fwd(qkv, pos, invf, g, plan=None):
    blocks = _att_blocks(DILS[g])

    def body(q_ref, k_ref, v_ref, pos_ref, invf_ref, o_ref, l_ref, qr, kr):
        cosf, s_lo, s_hi = _rope_tables(pos_ref, invf_ref)
        qr[...] = _rope(q_ref[...], cosf, s_lo, s_hi)
        kr[...] = _rope(k_ref[...], cosf, s_lo, s_hi)
        for off, first in blocks:
            cur, prv = pl.ds(off, AB), pl.ds(off if first else off - AB, AB)
            o2, lse2 = _att_block(qr[cur, :], kr[prv, :], kr[cur, :], v_ref[prv, :], v_ref[cur, :], first)
            o_ref[cur, :] = o2
            l_ref[cur, :] = lse2

    def sec(n):
        return pl.BlockSpec((L, 128), lambda p: (0, p + 4 * n))

    return pcall(
        body, plan, grid=(4,),
        in_specs=[sec(0), sec(1), sec(2), pl.BlockSpec((L, 1), lambda p: (0, 0)), pl.BlockSpec((1, 128), lambda p: (0, 0))],
        out_specs=[sec(0), sec(0)], out_shape=[S((L, 512), f32), S((L, 512), f32)],
        scratch_shapes=[pltpu.VMEM((L, 128), f32), pltpu.VMEM((L, 128), f32)],
        sem=("parallel",), name=f"attn_fwd{g}", args=[qkv, qkv, qkv, pos, invf])


def _att_block_bwd(q2, kp, kc, vp, vc, lse2, do2, dl2, first):
    lane = lax.broadcasted_iota(jnp.int32, (1, 128), 1)
    qi = lax.broadcasted_iota(jnp.int32, (AB, 2 * AB), 0) + AB
    kj = lax.broadcasted_iota(jnp.int32, (AB, 2 * AB), 1)
    back = qi - kj
    valid = (back >= 0) & (back <= AB)
    if first:
        valid = valid & (kj >= AB)
    kk = jnp.concatenate([kp, kc], axis=0)
    vv = jnp.concatenate([vp, vc], axis=0)
    dq2 = jnp.zeros((AB, 128), f32)
    dkk = jnp.zeros((2 * AB, 128), f32)
    dvv = jnp.zeros((2 * AB, 128), f32)
    for e in range(2):
        hb = (lane >= 64 * e) & (lane < 64 * (e + 1))
        hm = hb.astype(f32)
        qs = q2 * (hm * 0.125)
        lse = jnp.max(jnp.where(hb, lse2, -jnp.inf), axis=-1, keepdims=True)
        dls = jnp.sum(dl2 * hm, axis=-1, keepdims=True)
        p = jnp.where(valid, jnp.exp(dot_nt(qs, kk) - lse), 0.0)
        dov = do2 * hm
        dp = dot_nt(dov, vv)
        ds = p * (dp - jnp.sum(p * dp, axis=-1, keepdims=True) + dls)
        dq2 = dq2 + dot_nn(ds, kk) * (hm * 0.125)
        dkk = dkk + dot_tn(ds, qs)
        dvv = dvv + dot_tn(p, dov)
    return dq2, dkk[:AB], dkk[AB:], dvv[:AB], dvv[AB:]


def attn_bwd(qkv, pos, invf, lse, do, dl, g, plan=None):
    blocks = _att_blocks(DILS[g])

    def body(q_ref, k_ref, v_ref, pos_ref, invf_ref, l_ref, do_ref, dl_ref, d_ref, qr, kr, dqr, dkr, dvr):
        cosf, s_lo, s_hi = _rope_tables(pos_ref, invf_ref)
        qr[...] = _rope(q_ref[...], cosf, s_lo, s_hi)
        kr[...] = _rope(k_ref[...], cosf, s_lo, s_hi)
        for off, first in blocks:
            cur, prv = pl.ds(off, AB), pl.ds(off if first else off - AB, AB)
            dq2, dkp, dkc, dvp, dvc = _att_block_bwd(qr[cur, :], kr[prv, :], kr[cur, :], v_ref[prv, :], v_ref[cur, :],
                                                     l_ref[cur, :], do_ref[cur, :], dl_ref[cur, :], first)
            dqr[cur, :] = dq2
            dkr[cur, :] = dkc
            dvr[cur, :] = dvc
            if not first:
                dkr[prv, :] += dkp
                dvr[prv, :] += dvp
        d_ref[0] = _rope_t(dqr[...], cosf, s_lo, s_hi).astype(d_ref.dtype)
        d_ref[1] = _rope_t(dkr[...], cosf, s_lo, s_hi).astype(d_ref.dtype)
        d_ref[2] = dvr[...].astype(d_ref.dtype)

    def sec(n):
        return pl.BlockSpec((L, 128), lambda p: (0, p + 4 * n))

    return pcall(
        body, plan, grid=(4,),
        in_specs=[sec(0), sec(1), sec(2), pl.BlockSpec((L, 1), lambda p: (0, 0)), pl.BlockSpec((1, 128), lambda p: (0, 0)),
                  sec(0), sec(0), sec(0)],
        out_specs=pl.BlockSpec((3, L, 128), lambda p: (0, 0, p)), out_shape=S((3, L, 512), BF),
        scratch_shapes=[pltpu.VMEM((L, 128), f32)] * 5,
        sem=("parallel",), name=f"attn_bwd{g}", args=[qkv, qkv, qkv, pos, invf, lse, do, dl])


def _merge(o0, o1, o2, l0, l1, l2):
    m = jnp.maximum(jnp.maximum(l0, l1), l2)
    e0, e1, e2 = jnp.exp(l0 - m), jnp.exp(l1 - m), jnp.exp(l2 - m)
    return (e0 * o0 + e1 * o1 + e2 * o2) / (e0 + e1 + e2)


def _to_token_major(src_ref, scr, i, dil, slab):
    n = TR // dil
    for r in range(dil):
        rows = pl.ds(pl.multiple_of(r * (L // dil) + i * n, n), n)
        scr[pl.ds(r, n, stride=dil), :] = src_ref[rows, slab * 128:(slab + 1) * 128].astype(f32)
    return scr[...]


def _to_class_major(val, dst_ref, scr, i, dil, slab):
    n = TR // dil
    scr[...] = val
    for r in range(dil):
        rows = pl.ds(pl.multiple_of(r * (L // dil) + i * n, n), n)
        dst_ref[rows, slab * 128:(slab + 1) * 128] = scr[pl.ds(r, n, stride=dil), :].astype(dst_ref.dtype)


def rms_fwd_classes(x, g, name):
    def body(x_ref, g_ref, o_ref, o1_ref, o2_ref, scr):
        i = pl.program_id(0)
        y = _rms(x_ref[...], g_ref[...])
        o_ref[...] = y.astype(o_ref.dtype)
        for s in range(D // 128):
            ys = y[:, s * 128:(s + 1) * 128]
            _to_class_major(ys, o1_ref, scr, i, DILS[1], s)
            _to_class_major(ys, o2_ref, scr, i, DILS[2], s)

    row = pl.BlockSpec((TR, D), lambda i: (i, 0))
    full = pl.BlockSpec((L, D), lambda i: (0, 0))
    return pl.pallas_call(
        body, grid=(L // TR,), in_specs=[row, pl.BlockSpec((1, D), lambda i: (0, 0))], out_specs=[row, full, full],
        out_shape=[S((L, D), BF)] * 3, scratch_shapes=[pltpu.VMEM((TR, 128), f32)],
        compiler_params=_cp(("arbitrary",)), name=name)(x, g)


def rms_bwd_classes(x, g, dy0, dyc, dres, name, plan=None):
    def body(x_ref, g_ref, dy0_ref, d1_ref, d2_ref, dr_ref, dh_ref, dg_ref, scr, dyf):
        i = pl.program_id(0)
        for s in range(D // 128):
            sl = slice(s * 128, (s + 1) * 128)
            dyf[:, sl] = (dy0_ref[:, sl] + _to_token_major(d1_ref, scr.at[0], i, DILS[1], s)
                          + _to_token_major(d2_ref, scr.at[1], i, DILS[2], s))
        _, vjp = jax.vjp(_rms, x_ref[...], g_ref[...])
        dx, dg = vjp(dyf[...])
        dh_ref[...] = dr_ref[...] + dx

        @pl.when(i == 0)
        def _():
            dg_ref[...] = jnp.zeros_like(dg_ref)

        dg_ref[...] += dg

    row = pl.BlockSpec((TR, D), lambda i: (i, 0))
    vec = pl.BlockSpec((1, D), lambda i: (0, 0))
    full = pl.BlockSpec((L, D), lambda i: (0, 0))
    return pcall(body, plan, grid=(L // TR,), in_specs=[row, vec, row, full, full, row], out_specs=[row, vec],
                 out_shape=[S((L, D), f32), S((1, D), f32)],
                 scratch_shapes=[pltpu.VMEM((2, TR, 128), f32), pltpu.VMEM((TR, D), f32)],
                 sem=("arbitrary",), name=name, args=[x, g, dy0, dyc[0], dyc[1], dres])


def attn_merge_fwd(o0, l0, oc, lc):
    def body(o0_ref, l0_ref, o1_ref, l1_ref, o2_ref, l2_ref, o_ref, scr):
        i = pl.program_id(0)
        for s in range(4):
            sl = slice(s * 128, (s + 1) * 128)
            o1 = _to_token_major(o1_ref, scr.at[0], i, DILS[1], s)
            l1 = _to_token_major(l1_ref, scr.at[1], i, DILS[1], s)
            o2 = _to_token_major(o2_ref, scr.at[2], i, DILS[2], s)
            l2 = _to_token_major(l2_ref, scr.at[3], i, DILS[2], s)
            o_ref[:, sl] = _merge(o0_ref[:, sl], o1, o2, l0_ref[:, sl], l1, l2).astype(o_ref.dtype)

    blk = pl.BlockSpec((TR, 512), lambda i: (i, 0))
    full = pl.BlockSpec((L, 512), lambda i: (0, 0))
    return pl.pallas_call(
        body, grid=(L // TR,), in_specs=[blk, blk, full, full, full, full], out_specs=blk, out_shape=S((L, 512), BF),
        scratch_shapes=[pltpu.VMEM((4, TR, 128), f32)],
        compiler_params=_cp(("arbitrary",)), name="attn_merge_fwd")(o0, l0, oc[0], lc[0], oc[1], lc[1])


def attn_merge_bwd(o0, l0, oc, lc, do, plan=None):
    def body(o0_ref, l0_ref, o1_ref, l1_ref, o2_ref, l2_ref, g_ref, do0, dl0, do1, dl1, do2, dl2, scr):
        i = pl.program_id(0)
        for s in range(4):
            sl = slice(s * 128, (s + 1) * 128)
            o1 = _to_token_major(o1_ref, scr.at[0], i, DILS[1], s)
            l1 = _to_token_major(l1_ref, scr.at[1], i, DILS[1], s)
            o2 = _to_token_major(o2_ref, scr.at[2], i, DILS[2], s)
            l2 = _to_token_major(l2_ref, scr.at[3], i, DILS[2], s)
            _, vjp = jax.vjp(_merge, o0_ref[:, sl], o1, o2, l0_ref[:, sl], l1, l2)
            g0, g1, g2, h0, h1, h2 = vjp(g_ref[:, sl].astype(f32))
            do0[:, sl] = g0.astype(do0.dtype)
            dl0[:, sl] = h0
            _to_class_major(g1, do1, scr.at[0], i, DILS[1], s)
            _to_class_major(h1, dl1, scr.at[1], i, DILS[1], s)
            _to_class_major(g2, do2, scr.at[2], i, DILS[2], s)
            _to_class_major(h2, dl2, scr.at[3], i, DILS[2], s)

    blk = pl.BlockSpec((TR, 512), lambda i: (i, 0))
    full = pl.BlockSpec((L, 512), lambda i: (0, 0))
    outs = pcall(body, plan, grid=(L // TR,), in_specs=[blk, blk, full, full, full, full, blk],
                 out_specs=[blk, blk, full, full, full, full],
                 out_shape=[S((L, 512), BF), S((L, 512), f32)] * 3, scratch_shapes=[pltpu.VMEM((4, TR, 128), f32)],
                 sem=("arbitrary",), name="attn_merge_bwd", args=[o0, l0, oc[0], lc[0], oc[1], lc[1], do])
    return [outs[0], outs[2], outs[4]], [outs[1], outs[3], outs[5]]


def _invf_lanes():
    half = 8
    inv = ROPE_THETA ** (-np.arange(half, dtype=np.float32) * 2.0 / 16.0)
    lane = np.arange(128) % 64
    return jnp.asarray(np.where(lane < 16, inv[lane % 8], 0.0).astype(np.float32)[None, :])


def hosted(C, host, fn):
    p = C.plan(host) if C is not None else None
    out = fn(p)
    if p is not None:
        C.done(p)
    return out


def _ffn_fwd(h, g_row, W, cb, layer, C):
    hn = rms_fwd(h, g_row, f"rms_ffn{layer}")
    hu = hosted(C, f"ffn_in{layer}", lambda p: matmul(hn, W[("ffn_w_in", layer)], mode="nn", tm=1024, tn=1408, tk=1024,
                                                      plan=p, name=f"ffn_in{layer}"))
    act = hosted(C, f"convact_fwd{layer}", lambda p: convact_fwd(hu, W[("ffn_conv_w", layer)], cb, layer, plan=p))
    h2 = matmul(act, W[("ffn_w_out", layer)], mode="nn", tm=1024, tn=1024, tk=2816, add=h, name=f"ffn_out{layer}")
    return h2, (hn, hu, act)


def _ffn_bwd(dh, h, g_row, W, cb, saved, layer, C, G):
    hn, hu, act = saved
    w_in, w_out = W[("ffn_w_in", layer)], W[("ffn_w_out", layer)]
    dact = matmul(dh, w_out, mode="nt", tm=1024, tn=1408, tk=1024, name=f"ffn_out_dx{layer}")
    G[("ffn_w_out", layer)] = matmul(act, dh, mode="tn", tm=1408, tn=1024, tk=L, out_dtype=BF, name=f"ffn_out_dw{layer}")
    dhu, G[("ffn_conv_w", layer)], g_cb = hosted(
        C, f"convact_bwd{layer}", lambda p: convact_bwd(hu, W[("ffn_conv_w", layer)], cb, dact, layer, plan=p))
    dhn = hosted(C, f"ffn_in_dx{layer}", lambda p: matmul(dhu, w_in, mode="nt", tm=1024, tn=1024, tk=2816, plan=p,
                                                         name=f"ffn_in_dx{layer}"))
    G[("ffn_w_in", layer)] = hosted(C, f"ffn_in_dw{layer}", lambda p: matmul(
        hn, dhu, mode="tn", tm=1024, tn=1408, tk=L, out_dtype=BF, plan=p, name=f"ffn_in_dw{layer}"))
    dh2, g_norm = hosted(C, f"rms_ffn_bwd{layer}", lambda p: rms_bwd(h, g_row, [dhn], dh, f"rms_ffn_bwd{layer}", plan=p))
    return dh2, g_cb, g_norm


def local_step(x, pos, tgt, sm, W, C=None):
    G = C.grads if C is not None else {}
    nm, nf = sm["norm_mix"], sm["norm_ffn"]
    invf = _invf_lanes()
    are = sm["s5_A_re"].reshape(NST, 1)
    aim = sm["s5_A_im"].reshape(NST, 1)
    ldt = sm["s5_log_dt"].reshape(1, 32)
    bre = sm["s5_B_re"].reshape(NST, 16)
    bim = sm["s5_B_im"].reshape(NST, 16)
    cre = jnp.swapaxes(sm["s5_C_re"][0], 1, 2).reshape(NST, 16)
    cim = jnp.swapaxes(sm["s5_C_im"][0], 1, 2).reshape(NST, 16)
    drow = sm["s5_D"].reshape(1, S5W)
    wbr, wbi, wcr, wci, abr, abi = s5_params_fwd(are, aim, ldt, bre, bim, cre, cim)
    hn0 = rms_fwd(x, nm[0:1], "rms_mix0")
    cb3 = sm["ffn_conv_b3"]
    proj = hosted(C, "mix_in", lambda p: matmul(hn0, W[("mix_w_in", 0)], mode="nn", tm=1024, tn=1280, tk=1024, plan=p, name="mix_in"))
    xs_re, xs_im, y5 = hosted(C, "s5_scan_fwd", lambda p: s5_scan_fwd(proj, wbr, wbi, wcr, wci, abr, abi, drow, plan=p))
    oa = s5_glu_fwd(y5, W[("s5_glu_w", 0)], sm["s5_glu_b"])
    ob, ssave = hosted(C, "hgrn_fwd", lambda p: hgrn_fwd(proj, sm["hgrn_gamma"], sm["hgrn_norm"], plan=p))
    cat = jnp.concatenate([oa, ob], axis=1)
    h1 = matmul(cat, W[("mix_w_out", 0)], mode="nn", tm=1024, tn=1024, tk=1024, add=x, name="mix_out")
    h2, ffn0 = _ffn_fwd(h1, nf[0:1], W, cb3, 0, C)
    hn2_g = rms_fwd_classes(h2, nm[1:2], "rms_mix1")
    wqkv = W[("att_w_qkv", 0)]
    pos_g, qkv_g, oc_g, lc_g = [], [], [], []
    for g, dil in enumerate(DILS):
        pos_g.append(deinterleave(pos, dil))
        qkv_g.append(hosted(C, f"att_qkv{g}", lambda p: matmul(
            hn2_g[g], wqkv, mode="nn", tm=1024, tn=512, tk=1024, dims=(L, 1536, D),
            b_spec=pl.BlockSpec((D, 512), lambda i, j, k, g=g: (0, 3 * j + g)), plan=p, name=f"att_qkv{g}")))
        o_c, l_c = hosted(C, f"attn_fwd{g}", lambda p: attn_fwd(qkv_g[g], pos_g[g], invf, g, plan=p))
        oc_g.append(o_c)
        lc_g.append(l_c)
    o = attn_merge_fwd(oc_g[0], lc_g[0], oc_g[1:], lc_g[1:])
    h3 = matmul(o, W[("att_w_o", 0)], mode="nn", tm=1024, tn=1024, tk=512, add=h2, name="att_o")
    h4, ffn1 = _ffn_fwd(h3, nf[1:2], W, cb3, 1, C)
    loss, dh, g_nfinal = loss_head(h4, sm["norm_final"].reshape(1, D), tgt)
    dh, g_cb1, g_nf1 = _ffn_bwd(dh, h3, nf[1:2], W, cb3, ffn1, 1, C, G)
    do = matmul(dh, W[("att_w_o", 0)], mode="nt", tm=1024, tn=512, tk=1024, name="att_o_dx")
    G[("att_w_o", 0)] = matmul(o, dh, mode="tn", tm=512, tn=1024, tk=L, out_dtype=BF, name="att_o_dw")
    do_g, dl_g = hosted(C, "attn_merge_bwd", lambda p: attn_merge_bwd(oc_g[0], lc_g[0], oc_g[1:], lc_g[1:], do, plan=p))
    dhn2_g, gq = [], []
    for g, dil in enumerate(DILS):
        d3 = hosted(C, f"attn_bwd{g}", lambda p: attn_bwd(qkv_g[g], pos_g[g], invf, lc_g[g], do_g[g], dl_g[g], g, plan=p))
        dx = matmul(d3, wqkv, mode="nt", tm=1024, tn=1024, tk=512, dims=(L, D, 1536),
                    a_spec=pl.BlockSpec((None, 1024, 512), lambda i, j, k: (k, i, 0)),
                    b_spec=pl.BlockSpec((D, 512), lambda i, j, k, g=g: (0, 3 * k + g)), name=f"att_qkv_dx{g}")
        dhn2_g.append(dx)
        gq.append(matmul(hn2_g[g], d3, mode="tn", tm=1024, tn=512, tk=L, out_dtype=BF, dims=(D, 1536, L),
                         b_spec=pl.BlockSpec((None, L, 512), lambda i, j, k: (j, k, 0)), name=f"att_qkv_dw{g}"))
    G[("att_w_qkv", 0)] = jnp.concatenate([gq[g][:, 512 * s:512 * (s + 1)] for s in range(3) for g in range(3)], axis=1)
    dh, g_nm1 = hosted(C, "rms_mix_bwd1", lambda p: rms_bwd_classes(h2, nm[1:2], dhn2_g[0], dhn2_g[1:], dh, "rms_mix_bwd1", plan=p))
    dh, g_cb0, g_nf0 = _ffn_bwd(dh, h1, nf[0:1], W, cb3, ffn0, 0, C, G)
    dmix = matmul(dh, W[("mix_w_out", 0)], mode="nt", tm=1024, tn=1024, tk=1024, name="mix_out_dx")
    G[("mix_w_out", 0)] = matmul(cat, dh, mode="tn", tm=1024, tn=1024, tk=L, out_dtype=BF, name="mix_out_dw")
    dy5, g_glu_w, g_glu_b = s5_glu_bwd(y5, W[("s5_glu_w", 0)], sm["s5_glu_b"], dmix)
    G[("s5_glu_w", 0)] = g_glu_w.astype(BF)
    du, gwbr, gwbi, gwcr, gwci, gabr, gabi, g_d = hosted(C, "s5_scan_bwd", lambda p: s5_scan_bwd(
        dy5, proj, xs_re, xs_im, wbr, wbi, wcr, wci, abr, abi, drow, plan=p))
    g_are, g_aim, g_ldt, g_bre, g_bim, g_cre, g_cim = s5_params_bwd(are, aim, ldt, bre, bim, cre, cim,
                                                                   (gwbr, gwbi, gwcr, gwci, gabr, gabi))
    dproj, g_gamma, g_hnorm = hosted(C, "hgrn_bwd", lambda p: hgrn_bwd(proj, sm["hgrn_gamma"], sm["hgrn_norm"], ssave, dmix, du,
                                                                       plan=p))
    dhn0 = hosted(C, "mix_in_dx", lambda p: matmul(dproj, W[("mix_w_in", 0)], mode="nt", tm=1024, tn=1024, tk=2560, plan=p,
                                                  name="mix_in_dx"))
    G[("mix_w_in", 0)] = matmul(hn0, dproj, mode="tn", tm=1024, tn=1280, tk=L, out_dtype=BF, name="mix_in_dw")
    gx, g_nm0 = hosted(C, "rms_mix_bwd0", lambda p: rms_bwd(x, nm[0:1], [dhn0], dh, "rms_mix_bwd0", plan=p))
    small = {
        "norm_mix": jnp.concatenate([g_nm0, g_nm1], axis=0), "norm_ffn": jnp.concatenate([g_nf0, g_nf1], axis=0),
        "norm_final": g_nfinal.reshape(D),
        "s5_A_re": g_are.reshape(1, 32, 64), "s5_A_im": g_aim.reshape(1, 32, 64), "s5_log_dt": g_ldt.reshape(1, 32),
        "s5_B_re": g_bre.reshape(1, 32, 64, 16), "s5_B_im": g_bim.reshape(1, 32, 64, 16),
        "s5_C_re": jnp.swapaxes(g_cre.reshape(1, 32, 64, 16), 2, 3), "s5_C_im": jnp.swapaxes(g_cim.reshape(1, 32, 64, 16), 2, 3),
        "s5_D": g_d.reshape(1, 32, 16), "s5_glu_b": g_glu_b, "hgrn_gamma": g_gamma, "hgrn_norm": g_hnorm,
        "ffn_conv_b": jnp.concatenate([g_cb0, g_cb1], axis=0),
    }
    return loss, gx, G, small


BIG = ("mix_w_in", "mix_w_out", "s5_glu_w", "att_w_qkv", "att_w_o", "ffn_w_in", "ffn_w_out", "ffn_conv_w")
SMALL = ("norm_mix", "norm_ffn", "norm_final", "s5_A_re", "s5_A_im", "s5_log_dt", "s5_B_re", "s5_B_im", "s5_C_re", "s5_C_im",
         "s5_D", "s5_glu_b", "hgrn_gamma", "hgrn_norm", "ffn_conv_b")


def cast_bf16(w, name, plan=None):
    nl, r, c = w.shape
    w2 = w.reshape(nl * r, c)
    tr = 256 if (nl * r) % 256 == 0 else nl * r

    def body(w_ref, o_ref):
        o_ref[...] = w_ref[...].astype(BF)

    out = pcall(body, plan, grid=(nl * r // tr,), in_specs=[pl.BlockSpec((tr, c), lambda i: (i, 0))],
                out_specs=pl.BlockSpec((tr, c), lambda i: (i, 0)), out_shape=S((nl * r, c), BF),
                sem=("parallel",), name=name, args=[w2])
    return out.reshape(nl, r, c)


SCHEDULE = {
    "cast_ffn_w_in": [("G", "mix_w_in", 0)],
    "mix_in": [("G", "mix_w_out", 0), ("G", "s5_glu_w", 0)],
    "s5_scan_fwd": [("G", "ffn_w_in", 0, (0, 2))],
    "hgrn_fwd": [("G", "ffn_w_in", 0, (1, 2)), ("G", "ffn_conv_w", 0), ("G", "ffn_conv_w", 1), ("G", "att_w_qkv", 0, (0, 2))],
    "ffn_in0": [("G", "ffn_w_out", 0)],
    "convact_fwd0": [("G", "att_w_qkv", 0, (1, 2))],
    "att_qkv0": [("G", "att_w_o", 0)],
    "attn_fwd0": [("G", "ffn_w_in", 1, (0, 2))],
    "attn_fwd1": [("G", "ffn_w_in", 1, (1, 2))],
    "attn_fwd2": [("G", "ffn_w_out", 1)],
    "convact_bwd1": [("P", "ffn_w_out", 1)],
    "ffn_in_dx1": [("A", "ffn_w_out", 1)],
    "ffn_in_dw1": [("B", "ffn_w_out", 1)],
    "rms_ffn_bwd1": [("P", "ffn_w_in", 1)],
    "attn_merge_bwd": [("P", "att_w_o", 0), ("A", "ffn_conv_w", 1)],
    "attn_bwd0": [("A", "ffn_w_in", 1), ("A", "att_w_o", 0)],
    "attn_bwd1": [("B", "ffn_w_in", 1), ("B", "att_w_o", 0), ("B", "ffn_conv_w", 1)],
    "rms_mix_bwd1": [("P", "att_w_qkv", 0)],
    "convact_bwd0": [("A", "att_w_qkv", 0), ("P", "ffn_w_out", 0)],
    "ffn_in_dx0": [("A", "ffn_w_out", 0), ("B", "att_w_qkv", 0)],
    "ffn_in_dw0": [("B", "ffn_w_out", 0)],
    "rms_ffn_bwd0": [("P", "ffn_w_in", 0)],
    "s5_scan_bwd": [("A", "ffn_w_in", 0), ("P", "mix_w_out", 0), ("P", "s5_glu_w", 0), ("A", "ffn_conv_w", 0)],
    "hgrn_bwd": [("B", "ffn_w_in", 0), ("A", "mix_w_out", 0), ("A", "s5_glu_w", 0), ("B", "ffn_conv_w", 0)],
    "mix_in_dx": [("B", "mix_w_out", 0), ("B", "s5_glu_w", 0)],
    "rms_mix_bwd0": [("P", "mix_w_in", 0)],
    "adam_ffn_w_in": [("A", "mix_w_in", 0), ("A", "small", 0)],
    "adam_ffn_w_out": [("B", "mix_w_in", 0), ("B", "small", 0)],
}


class Comm:
    def __init__(self, shards, shapes):
        self.shards, self.shapes = shards, shapes
        self.W, self.grads, self.slots = {}, {}, {}
        self.sib, self.pair = {}, {}
        self.small = None

    def plan(self, host):
        items = SCHEDULE.get(host)
        if not items:
            return None
        p = Plan()
        for it in items:
            kind, name, l = it[:3]
            part, parts = it[3] if len(it) > 3 else (0, 1)
            if name == "small":
                kdst = p.buf("slots:small", arr=self.slots.get("small"), shape=S((8,) + self.small.shape, f32), write=True)
                if kind == "A":
                    ReduceOp(p, p.buf("g:small", arr=self.small), kdst, None, self.small.shape, False, 0, 0, whole=True)
                else:
                    ForwardOp(p, kdst, None, whole=True)
                continue
            nl, R, C_ = self.shapes[name]
            rows = name in ROW_SHARDED
            r0, nr = part * (R // parts), R // parts
            if kind == "G":
                sh = self.shards[name]
                kdst = p.buf(f"W:{name}:{l}", arr=self.W.get((name, l)), shape=S((4 * R, C_) if rows else (R, 4 * C_), sh.dtype),
                             write=True)
                GatherOp(p, p.buf("shard:" + name, arr=sh), kdst, l, self.shapes[name], rows, r0, nr, split=(nr % 32 == 0))
            elif name == "ffn_conv_w":
                g = self.grads[(name, l)]
                kdst = p.buf("slots:" + name, arr=self.slots.get(name), shape=S((8, nl, R, C_), g.dtype), write=True)
                if kind == "A":
                    ReduceOp(p, p.buf(f"g:{name}:{l}", arr=g), kdst, l, self.shapes[name], rows, r0, nr)
                else:
                    ForwardOp(p, kdst, l)
            elif kind == "P":
                g = self.grads[(name, l)]
                ksib = p.buf(f"sib:{name}:{l}", shape=S((4 * R // 2, C_) if rows else (R // 2, 4 * C_), g.dtype), write=True)
                PairOp(p, p.buf(f"g:{name}:{l}", arr=g), ksib, self.shapes[name], rows)
            else:
                if (name, l) not in self.pair:
                    self.pair[(name, l)] = pair_sum(self.grads[(name, l)], self.sib[(name, l)], rows, R, f"pair_sum_{name}{l}")
                h = self.pair[(name, l)]
                kdst = p.buf("slots:" + name, arr=self.slots.get(name), shape=S((4, nl, R, C_), h.dtype), write=True)
                if kind == "A":
                    ReduceOp(p, p.buf(f"h:{name}:{l}", arr=h), kdst, l, self.shapes[name], rows, r0, nr // 2, half=True)
                else:
                    HalfForwardOp(p, kdst, l, self.shapes[name])
        return p

    def done(self, p):
        for k, arr in p.out.items():
            tag, name = k.split(":")[:2]
            if tag == "W":
                self.W[(name, int(k.split(":")[2]))] = arr
            elif tag == "sib":
                self.sib[(name, int(k.split(":")[2]))] = arr
            else:
                self.slots[name] = arr


def _adamw(w, g, m, v):
    m = B1 * m + (1.0 - B1) * g
    v = B2 * v + (1.0 - B2) * jnp.square(g)
    m_hat = m / (1.0 - B1 ** STEP)
    v_hat = v / (1.0 - B2 ** STEP)
    return -LR * (m_hat / (jnp.sqrt(v_hat) + AEPS) + WD * w), m, v


def adam_big(w, m, v, slots, name, plan=None):
    nl, R, C = w.shape
    ns = slots.shape[0]
    tr = 128 if R % 128 == 0 else (64 if R % 64 == 0 else R)

    def body(w_ref, m_ref, v_ref, s_ref, g_ref, d_ref, nm_ref, nv_ref):
        g = s_ref[0].astype(f32)
        for s in range(1, ns):
            g = g + s_ref[s].astype(f32)
        d, nm_, nv_ = _adamw(w_ref[...], g, m_ref[...], v_ref[...])
        g_ref[...] = g
        d_ref[...] = d
        nm_ref[...] = nm_
        nv_ref[...] = nv_

    blk = pl.BlockSpec((None, tr, C), lambda l, i: (l, i, 0))
    return pcall(body, plan, grid=(nl, R // tr),
                 in_specs=[blk, blk, blk, pl.BlockSpec((ns, None, tr, C), lambda l, i: (0, l, i, 0))],
                 out_specs=[blk] * 4, out_shape=[S((nl, R, C), f32)] * 4,
                 sem=("parallel", "parallel"), name=name, args=[w, m, v, slots])


def adam_small(w, m, v, slots):
    R = w.shape[0]
    tr = 256

    def body(w_ref, m_ref, v_ref, s_ref, g_ref, d_ref, nm_ref, nv_ref):
        g = s_ref[0]
        for s in range(1, 8):
            g = g + s_ref[s]
        d, nm_, nv_ = _adamw(w_ref[...], g, m_ref[...], v_ref[...])
        g_ref[...] = g
        d_ref[...] = d
        nm_ref[...] = nm_
        nv_ref[...] = nv_

    blk = pl.BlockSpec((tr, 128), lambda i: (i, 0))
    return pl.pallas_call(
        body, grid=(R // tr,), in_specs=[blk, blk, blk, pl.BlockSpec((8, tr, 128), lambda i: (0, i, 0))],
        out_specs=[blk] * 4, out_shape=[S((R, 128), f32)] * 4,
        compiler_params=_cp(("parallel",)), name="adam_small")(w, m, v, slots)


def _pack(d):
    flat = jnp.concatenate([d[n].reshape(-1) for n in SMALL])
    n = flat.shape[0]
    rows = -(-n // (256 * 128)) * 256
    return jnp.pad(flat, (0, rows * 128 - n)).reshape(rows, 128)


def _unpack(p, like):
    flat = p.reshape(-1)
    out, off = {}, 0
    for n in SMALL:
        sz = math.prod(like[n].shape)
        out[n] = flat[off:off + sz].reshape(like[n].shape)
        off += sz
    return out


def kernel(x, positions, norm_mix, norm_ffn, norm_final, mix_w_in, mix_w_out, s5_A_re, s5_A_im, s5_log_dt, s5_B_re, s5_B_im, s5_C_re, s5_C_im, s5_D, s5_glu_w, s5_glu_b, hgrn_gamma, hgrn_norm, att_w_qkv, att_w_o, ffn_w_in, ffn_conv_w, ffn_conv_b, ffn_w_out, loss_target, m_norm_mix, m_norm_ffn, m_norm_final, m_mix_w_in, m_mix_w_out, m_s5_A_re, m_s5_A_im, m_s5_log_dt, m_s5_B_re, m_s5_B_im, m_s5_C_re, m_s5_C_im, m_s5_D, m_s5_glu_w, m_s5_glu_b, m_hgrn_gamma, m_hgrn_norm, m_att_w_qkv, m_att_w_o, m_ffn_w_in, m_ffn_conv_w, m_ffn_conv_b, m_ffn_w_out, v_norm_mix, v_norm_ffn, v_norm_final, v_mix_w_in, v_mix_w_out, v_s5_A_re, v_s5_A_im, v_s5_log_dt, v_s5_B_re, v_s5_B_im, v_s5_C_re, v_s5_C_im, v_s5_D, v_s5_glu_w, v_s5_glu_b, v_hgrn_gamma, v_hgrn_norm, v_att_w_qkv, v_att_w_o, v_ffn_w_in, v_ffn_conv_w, v_ffn_conv_b, v_ffn_w_out):
    a = dict(locals())
    weights = BIG + SMALL
    w = {n: a[n] for n in weights}
    m = {n: a["m_" + n] for n in weights}
    v = {n: a["v_" + n] for n in weights}
    shards = {"ffn_conv_w": ffn_conv_w}
    C = Comm(shards, {n: w[n].shape for n in BIG})
    for n in ("mix_w_in", "ffn_w_in", "mix_w_out", "s5_glu_w", "ffn_w_out", "att_w_qkv", "att_w_o"):
        shards[n] = hosted(C, "cast_" + n, lambda p: cast_bf16(w[n], "cast_" + n, plan=p))
    sm = {n: w[n] for n in SMALL}
    sm["ffn_conv_b3"] = ffn_conv_b.reshape(2, 1, 2 * DFF)
    loss, gx, _, gsmall = local_step(x[0], positions.reshape(L, 1), loss_target[0], sm, C.W, C)
    C.small = _pack(gsmall)
    res = {}
    for n in ("ffn_w_in", "ffn_w_out", "att_w_qkv", "att_w_o", "mix_w_out", "s5_glu_w", "ffn_conv_w", "mix_w_in"):
        res[n] = hosted(C, "adam_" + n, lambda p: adam_big(w[n], m[n], v[n], C.slots[n], "adam_" + n, plan=p))
    packed = adam_small(_pack({n: w[n] for n in SMALL}), _pack({n: m[n] for n in SMALL}), _pack({n: v[n] for n in SMALL}),
                        C.slots["small"])
    small_out = [_unpack(p, {n: w[n] for n in SMALL}) for p in packed]
    for n in SMALL:
        res[n] = tuple(so[n] for so in small_out)
    total = lax.psum(loss[0, 0], ("x", "y", "c"))
    order = ("norm_mix", "norm_ffn", "norm_final", "mix_w_in", "mix_w_out", "s5_A_re", "s5_A_im", "s5_log_dt", "s5_B_re", "s5_B_im",
             "s5_C_re", "s5_C_im", "s5_D", "s5_glu_w", "s5_glu_b", "hgrn_gamma", "hgrn_norm", "att_w_qkv", "att_w_o", "ffn_w_in",
             "ffn_conv_w", "ffn_conv_b", "ffn_w_out")
    return (total, gx[None], *[res[n][0] for n in order], *[res[n][1] for n in order], *[res[n][2] for n in order],
            *[res[n][3] for n in order])
```

```python
import functools
import math

import numpy as np
import jax
import jax.numpy as jnp
from jax import lax
from jax.experimental import pallas as pl
from jax.experimental.pallas import tpu as pltpu

f32 = jnp.float32
BF = jnp.bfloat16
HI = lax.Precision.HIGHEST
S = jax.ShapeDtypeStruct
MESH = pl.DeviceIdType.MESH

L = 2048
D = 1024
EPS = 1e-6
S5W = 512
NST = 2048
HGC = 64
DFF = 2816
ROPE_THETA = 500000.0
LR, B1, B2, AEPS, WD, STEP = 0.001, 0.9, 0.999, 1e-08, 0.01, 10
VMEM_LIMIT = 56 * 1024 * 1024


def _cp(sem=None):
    return pltpu.CompilerParams(dimension_semantics=sem, vmem_limit_bytes=VMEM_LIMIT)


ANY = pl.BlockSpec(memory_space=pl.ANY)
ROW_SHARDED = ("mix_w_out", "s5_glu_w", "ffn_w_out")


def _coords():
    x, y, c = lax.axis_index("x"), lax.axis_index("y"), lax.axis_index("c")
    return x, y, c, 2 * x + y, [(1 - x, y), (x, 1 - y), (1 - x, 1 - y)]


def _rows(start, n):
    return pl.ds(start if isinstance(start, int) else pl.multiple_of(start, 8), n)


def _cols(q, n):
    return pl.ds(pl.multiple_of(q * n, 128), n)


class Plan:
    def __init__(self):
        self.bufs, self.ops, self.nsem, self.out = {}, [], 0, {}

    def buf(self, key, arr=None, shape=None, write=False):
        b = self.bufs.setdefault(key, dict(arr=arr, shape=shape, write=False))
        b["write"] = b["write"] or write
        return key

    def add(self, op):
        op.base = self.nsem
        self.nsem += op.nsem
        self.ops.append(op)


class GatherOp:
    nsem = 13

    def __init__(self, plan, ksrc, kdst, l, shard_shape, rows, r0, nr, split):
        self.ksrc, self.kdst, self.l, (_, self.R, self.C), self.rows, self.r0, self.nr, self.split = (
            ksrc, kdst, l, shard_shape, rows, r0, nr, split)
        self.h = nr // 2 if split else nr
        plan.add(self)

    def _dst(self, R_, q, start, n):
        if self.rows:
            return R_[self.kdst].at[_rows(q * self.R + start, n), :]
        return R_[self.kdst].at[_rows(start, n), _cols(q, self.C)]

    def _mine(self, c):
        return self.r0 + (c * self.h if self.split else 0)

    def _theirs(self, c):
        return self.r0 + ((1 - c) * self.h if self.split else 0)

    def _copies(self, R_, sems):
        x, y, c, me, others = _coords()
        src = R_[self.ksrc]
        local = pltpu.make_async_copy(src.at[self.l, _rows(self.r0, self.nr), :], self._dst(R_, me, self.r0, self.nr),
                                      sems.at[self.base + 12])
        send, fwd = [], []
        for k, (px, py) in enumerate(others):
            q = 2 * px + py
            send.append((
                pltpu.make_async_remote_copy(src.at[self.l, _rows(self._mine(c), self.h), :], self._dst(R_, me, self._mine(c), self.h),
                                             sems.at[self.base + k], sems.at[self.base + 3 + k], device_id=(px, py, c), device_id_type=MESH),
                pltpu.make_async_remote_copy(src.at[self.l, _rows(self._mine(c), self.h), :], self._dst(R_, q, self._mine(c), self.h),
                                             sems.at[self.base + k], sems.at[self.base + 3 + k], device_id=(px, py, c), device_id_type=MESH)))
            fwd.append((
                pltpu.make_async_remote_copy(self._dst(R_, q, self._mine(c), self.h), self._dst(R_, q, self._mine(c), self.h),
                                             sems.at[self.base + 6 + k], sems.at[self.base + 9 + k], device_id=(x, y, 1 - c), device_id_type=MESH),
                pltpu.make_async_remote_copy(self._dst(R_, q, self._theirs(c), self.h), self._dst(R_, q, self._theirs(c), self.h),
                                             sems.at[self.base + 6 + k], sems.at[self.base + 9 + k], device_id=(x, y, 1 - c), device_id_type=MESH)))
        return local, send, fwd

    def start(self, R_, sems):
        local, send, _ = self._copies(R_, sems)
        local.start()
        for out, _ in send:
            out.start()

    def finish(self, R_, sems):
        local, send, fwd = self._copies(R_, sems)
        for k in range(3):
            send[k][1].wait_recv()
            if self.split:
                fwd[k][0].start()
        for k in range(3):
            if self.split:
                fwd[k][1].wait_recv()
                fwd[k][0].wait_send()
            send[k][0].wait_send()
        local.wait()


class ReduceOp:
    nsem = 7

    def __init__(self, plan, ksrc, kdst, l, shard_shape, rows, r0, nr, whole=False, half=False):
        self.ksrc, self.kdst, self.l, (self.R, self.C), self.rows, self.r0, self.nr, self.whole, self.half = (
            ksrc, kdst, l, shard_shape[-2:], rows, r0, nr, whole, half)
        plan.add(self)

    def _piece(self, R_, q):
        g = R_[self.ksrc]
        if self.whole:
            return g
        if self.rows:
            return g.at[_rows(q * (self.R // 2 if self.half else self.R) + self.r0, self.nr), :]
        return g.at[_rows(self.r0, self.nr), _cols(q, self.C)]

    def _slot(self, R_, q, c):
        if self.whole:
            return R_[self.kdst].at[2 * q + c]
        if self.half:
            return R_[self.kdst].at[q, self.l, _rows(c * (self.R // 2) + self.r0, self.nr), :]
        return R_[self.kdst].at[2 * q + c, self.l, _rows(self.r0, self.nr), :]

    def _copies(self, R_, sems):
        x, y, c, me, others = _coords()
        local = pltpu.make_async_copy(self._piece(R_, me), self._slot(R_, me, c), sems.at[self.base + 6])
        send = []
        for k, (px, py) in enumerate(others):
            q = 2 * px + py
            send.append((
                pltpu.make_async_remote_copy(self._piece(R_, q), self._slot(R_, me, c), sems.at[self.base + k],
                                             sems.at[self.base + 3 + k], device_id=(px, py, c), device_id_type=MESH),
                pltpu.make_async_remote_copy(self._piece(R_, q), self._slot(R_, q, c), sems.at[self.base + k],
                                             sems.at[self.base + 3 + k], device_id=(px, py, c), device_id_type=MESH)))
        return local, send

    def start(self, R_, sems):
        local, send = self._copies(R_, sems)
        local.start()
        for out, _ in send:
            out.start()

    def finish(self, R_, sems):
        local, send = self._copies(R_, sems)
        local.wait()
        for out, inn in send:
            inn.wait_recv()
            out.wait_send()


class ForwardOp:
    nsem = 8

    def __init__(self, plan, kdst, l, whole=False):
        self.kdst, self.l, self.whole = kdst, l, whole
        plan.add(self)

    def _slot(self, R_, s):
        return R_[self.kdst].at[s] if self.whole else R_[self.kdst].at[s, self.l]

    def _copies(self, R_, sems):
        x, y, c, me, others = _coords()
        return [(pltpu.make_async_remote_copy(self._slot(R_, 2 * q + c), self._slot(R_, 2 * q + c), sems.at[self.base + q],
                                              sems.at[self.base + 4 + q], device_id=(x, y, 1 - c), device_id_type=MESH),
                 pltpu.make_async_remote_copy(self._slot(R_, 2 * q + 1 - c), self._slot(R_, 2 * q + 1 - c), sems.at[self.base + q],
                                              sems.at[self.base + 4 + q], device_id=(x, y, 1 - c), device_id_type=MESH))
                for q in range(4)]

    def start(self, R_, sems):
        for out, _ in self._copies(R_, sems):
            out.start()

    def finish(self, R_, sems):
        for out, inn in self._copies(R_, sems):
            inn.wait_recv()
            out.wait_send()


class PairOp:
    nsem = 8

    def __init__(self, plan, ksrc, kdst, shard_shape, rows):
        self.ksrc, self.kdst, (self.R, self.C), self.rows = ksrc, kdst, shard_shape[-2:], rows
        plan.add(self)

    def _copies(self, R_, sems):
        x, y, c, me, others = _coords()
        g, dst, h = R_[self.ksrc], R_[self.kdst], self.R // 2
        out = []
        for q in range(4 if self.rows else 1):
            src = g.at[_rows(q * self.R + (1 - c) * h, h), :]
            land = dst.at[_rows(q * h, h), :]
            out.append(pltpu.make_async_remote_copy(src, land, sems.at[self.base + q], sems.at[self.base + 4 + q],
                                                    device_id=(x, y, 1 - c), device_id_type=MESH))
        return out

    def start(self, R_, sems):
        for cp in self._copies(R_, sems):
            cp.start()

    def finish(self, R_, sems):
        for cp in self._copies(R_, sems):
            cp.wait_recv()
            cp.wait_send()


class HalfForwardOp:
    nsem = 2

    def __init__(self, plan, kdst, l, shard_shape):
        self.kdst, self.l, self.R = kdst, l, shard_shape[-2]
        plan.add(self)

    def _copy(self, R_, sems, core):
        x, y, c, me, others = _coords()
        part = R_[self.kdst].at[:, self.l, _rows((c if core == "mine" else 1 - c) * (self.R // 2), self.R // 2), :]
        return pltpu.make_async_remote_copy(part, part, sems.at[self.base], sems.at[self.base + 1],
                                            device_id=(x, y, 1 - c), device_id_type=MESH)

    def start(self, R_, sems):
        self._copy(R_, sems, "mine").start()

    def finish(self, R_, sems):
        self._copy(R_, sems, "theirs").wait_recv()
        self._copy(R_, sems, "mine").wait_send()


def pair_sum(g, gsib, rows, R, name):
    h = R // 2
    W = g.shape[1]
    tr = h if h * W * 2 <= 2 ** 21 else 128
    nq = 4 if rows else 1

    def body(c_ref, a_ref, b_ref, o_ref):
        o_ref[...] = (a_ref[...].astype(f32) + b_ref[...].astype(f32)).astype(o_ref.dtype)

    half = pl.BlockSpec((tr, W), lambda q, i, c_ref: (q * (h // tr) + i, 0))
    mine = pl.BlockSpec((tr, W), lambda q, i, c_ref: (q * (R // tr) + c_ref[0] * (h // tr) + i, 0))
    return pl.pallas_call(
        body, grid_spec=pltpu.PrefetchScalarGridSpec(num_scalar_prefetch=1, grid=(nq, h // tr), in_specs=[mine, half],
                                                     out_specs=half),
        out_shape=S(gsib.shape, g.dtype), compiler_params=_cp(("parallel", "parallel")),
        name=name)(lax.axis_index("c").reshape(1).astype(jnp.int32), g, gsib)


def pcall(body, plan, *, grid, in_specs, out_specs, out_shape, scratch_shapes=(), sem, name, args):
    multi = isinstance(out_shape, (list, tuple))
    if plan is None or not plan.ops:
        return pl.pallas_call(body, grid=grid, in_specs=in_specs, out_specs=out_specs, out_shape=out_shape,
                              scratch_shapes=list(scratch_shapes), compiler_params=_cp(sem), name=name)(*args)
    outs = list(out_shape) if multi else [out_shape]
    ospecs = list(out_specs) if multi else [out_specs]
    kin = [k for k, b in plan.bufs.items() if b["arr"] is not None]
    kout = [k for k, b in plan.bufs.items() if b["write"]]
    n_in, n_out, n_scr = len(in_specs), len(outs), len(scratch_shapes)

    def wrapped(*refs):
        o0 = n_in + len(kin)
        s0 = o0 + n_out + len(kout)
        R_ = dict(zip(kin, refs[n_in:o0]))
        R_.update(zip(kout, refs[o0 + n_out:s0]))
        sems = refs[s0 + n_scr]
        first = functools.reduce(jnp.logical_and, [pl.program_id(d) == 0 for d in range(len(grid))])
        last = functools.reduce(jnp.logical_and, [pl.program_id(d) == grid[d] - 1 for d in range(len(grid))])

        @pl.when(first)
        def _():
            for op in plan.ops:
                op.start(R_, sems)

        body(*refs[:n_in], *refs[o0:o0 + n_out], *refs[s0:s0 + n_scr])

        @pl.when(last)
        def _():
            for op in plan.ops:
                op.finish(R_, sems)

    def shape_of(k):
        b = plan.bufs[k]
        return S(b["arr"].shape, b["arr"].dtype) if b["arr"] is not None else b["shape"]

    res = pl.pallas_call(
        wrapped, grid=grid, in_specs=list(in_specs) + [ANY] * len(kin), out_specs=ospecs + [ANY] * len(kout),
        out_shape=outs + [shape_of(k) for k in kout],
        scratch_shapes=list(scratch_shapes) + [pltpu.SemaphoreType.DMA((plan.nsem,))],
        input_output_aliases={n_in + kin.index(k): n_out + kout.index(k) for k in kout if plan.bufs[k]["arr"] is not None},
        compiler_params=pltpu.CompilerParams(dimension_semantics=("arbitrary",) * len(grid), vmem_limit_bytes=VMEM_LIMIT,
                                             has_side_effects=True),
        name=name)(*args, *[plan.bufs[k]["arr"] for k in kin])
    plan.out = dict(zip(kout, res[n_out:]))
    return list(res[:n_out]) if multi else res[0]


def _dg(a, b, ca, cb):
    return lax.dot_general(a.astype(BF), b.astype(BF), (((ca,), (cb,)), ((), ())), preferred_element_type=f32)


@jax.custom_vjp
def dot_nn(a, b):
    return _dg(a, b, 1, 0)


@jax.custom_vjp
def dot_nt(a, b):
    return _dg(a, b, 1, 1)


@jax.custom_vjp
def dot_tn(a, b):
    return _dg(a, b, 0, 0)


dot_nn.defvjp(lambda a, b: (dot_nn(a, b), (a, b)),
              lambda r, g: (dot_nt(g, r[1]).astype(r[0].dtype), dot_tn(r[0], g).astype(r[1].dtype)))
dot_nt.defvjp(lambda a, b: (dot_nt(a, b), (a, b)),
              lambda r, g: (dot_nn(g, r[1]).astype(r[0].dtype), dot_tn(g, r[0]).astype(r[1].dtype)))
dot_tn.defvjp(lambda a, b: (dot_tn(a, b), (a, b)),
              lambda r, g: (dot_nt(r[1], g).astype(r[0].dtype), dot_nn(r[0], g).astype(r[1].dtype)))


def matmul(a, b, *, mode, tm, tn, tk, out_dtype=f32, add=None, b_lead=None, a_spec=None, b_spec=None, dims=None, plan=None, name):
    a_over, b_over = a_spec, b_spec
    if mode == "nn":
        (M, K), N = a.shape[-2:], b.shape[-1]
        a_spec = pl.BlockSpec((tm, tk), lambda i, j, k: (i, k))
        b_blk, b_idx, ca, cb = (tk, tn), (lambda i, j, k: (k, j)), 1, 0
    elif mode == "nt":
        (M, K), N = a.shape[-2:], b.shape[-2]
        a_spec = pl.BlockSpec((tm, tk), lambda i, j, k: (i, k))
        b_blk, b_idx, ca, cb = (tn, tk), (lambda i, j, k: (j, k)), 1, 1
    else:
        (K, M), N = a.shape[-2:], b.shape[-1]
        a_spec = pl.BlockSpec((tk, tm), lambda i, j, k: (k, i))
        b_blk, b_idx, ca, cb = (tk, tn), (lambda i, j, k: (k, j)), 0, 0
    if dims is not None:
        M, N, K = dims
    assert M % tm == 0 and N % tn == 0 and K % tk == 0, (name, M, N, K, tm, tn, tk)
    if b_lead is None:
        b_spec = pl.BlockSpec(b_blk, b_idx)
    else:
        b_spec = pl.BlockSpec((None,) + b_blk, lambda i, j, k: (b_lead,) + b_idx(i, j, k))
    if a_over is not None:
        a_spec = a_over
    if b_over is not None:
        b_spec = b_over
    nk = K // tk
    has_add = add is not None

    def body(*refs):
        a_ref, b_ref = refs[0], refs[1]
        add_ref = refs[2] if has_add else None
        o_ref = refs[2 + has_add]
        p = _dg(a_ref[...], b_ref[...], ca, cb)

        def fin(v):
            if has_add:
                v = v + add_ref[...].astype(f32)
            o_ref[...] = v.astype(o_ref.dtype)

        if nk == 1:
            fin(p)
        else:
            acc = refs[3 + has_add]
            k = pl.program_id(2)

            @pl.when(k == 0)
            def _():
                acc[...] = p

            @pl.when(k > 0)
            def _():
                acc[...] += p

            @pl.when(k == nk - 1)
            def _():
                fin(acc[...])

    in_specs = [a_spec, b_spec]
    args = [a, b]
    if has_add:
        in_specs.append(pl.BlockSpec((tm, tn), lambda i, j, k: (i, j)))
        args.append(add)
    return pcall(body, plan, grid=(M // tm, N // tn, nk), in_specs=in_specs,
                 out_specs=pl.BlockSpec((tm, tn), lambda i, j, k: (i, j)), out_shape=S((M, N), out_dtype),
                 scratch_shapes=[pltpu.VMEM((tm, tn), f32)] if nk > 1 else [],
                 sem=("parallel", "parallel", "arbitrary"), name=name, args=args)


def _rms(xv, gv):
    return xv * lax.rsqrt(jnp.mean(xv * xv, axis=-1, keepdims=True) + EPS) * gv


TR = 256


def rms_fwd(x, g, name):
    def body(x_ref, g_ref, o_ref):
        o_ref[...] = _rms(x_ref[...], g_ref[...]).astype(o_ref.dtype)

    return pl.pallas_call(
        body, grid=(L // TR,),
        in_specs=[pl.BlockSpec((TR, D), lambda i: (i, 0)), pl.BlockSpec((1, D), lambda i: (0, 0))],
        out_specs=pl.BlockSpec((TR, D), lambda i: (i, 0)), out_shape=S((L, D), BF),
        compiler_params=_cp(("parallel",)), name=name)(x, g)


def rms_bwd(x, g, dys, dres, name, plan=None):
    nd = len(dys)

    def body(*refs):
        x_ref, g_ref = refs[0], refs[1]
        dr_ref, dh_ref, dg_ref = refs[2 + nd:]
        dy = refs[2][...].astype(f32)
        for r in refs[3:2 + nd]:
            dy = dy + r[...].astype(f32)
        _, vjp = jax.vjp(_rms, x_ref[...], g_ref[...])
        dx, dg = vjp(dy)
        dh_ref[...] = dr_ref[...] + dx

        @pl.when(pl.program_id(0) == 0)
        def _():
            dg_ref[...] = jnp.zeros_like(dg_ref)

        dg_ref[...] += dg

    row = pl.BlockSpec((TR, D), lambda i: (i, 0))
    vec = pl.BlockSpec((1, D), lambda i: (0, 0))
    return pcall(body, plan, grid=(L // TR,), in_specs=[row, vec] + [row] * (nd + 1), out_specs=[row, vec],
                 out_shape=[S((L, D), f32), S((1, D), f32)], sem=("arbitrary",), name=name, args=[x, g, *dys, dres])


def loss_head(h, g, tgt):
    def f(hv, gv, tv):
        y = _rms(hv, gv)
        return 0.5 * jnp.sum(jnp.mean(jnp.square(y - tv), axis=-1))

    def body(h_ref, g_ref, t_ref, l_ref, dh_ref, dg_ref):
        val, vjp = jax.vjp(f, h_ref[...], g_ref[...], t_ref[...])
        dh, dg, _ = vjp(jnp.ones((), f32))
        dh_ref[...] = dh

        @pl.when(pl.program_id(0) == 0)
        def _():
            dg_ref[...] = jnp.zeros_like(dg_ref)
            l_ref[...] = jnp.zeros_like(l_ref)

        dg_ref[...] += dg
        l_ref[...] += jnp.full((1, 128), val, f32)

    row = pl.BlockSpec((TR, D), lambda i: (i, 0))
    vec = pl.BlockSpec((1, D), lambda i: (0, 0))
    return pl.pallas_call(
        body, grid=(L // TR,), in_specs=[row, vec, row],
        out_specs=[pl.BlockSpec((1, 128), lambda i: (0, 0)), row, vec],
        out_shape=[S((1, 128), f32), S((L, D), f32), S((1, D), f32)],
        compiler_params=_cp(("arbitrary",)), name="loss_head")(h, g, tgt)


def _col_to_row(c):
    n = c.shape[0]
    t = jnp.broadcast_to(c, (n, 128)).T
    r = lax.broadcasted_iota(jnp.int32, (128, n), 0)
    return jnp.sum(jnp.where(r == 0, t, 0.0), axis=0, keepdims=True)


def _s5_param_map(are, aim, ldt_row, bre, bim, cre, cim):
    n = NST
    gi = lax.broadcasted_iota(jnp.int32, (n, 32), 0) // 64
    gj = lax.broadcasted_iota(jnp.int32, (n, 32), 1)
    ldt = jnp.sum(jnp.where(gi == gj, ldt_row, 0.0), axis=1, keepdims=True)
    dt = jnp.exp(ldt)
    mag = jnp.exp(are * dt)
    abr = mag * jnp.cos(aim * dt)
    abi = mag * jnp.sin(aim * dt)
    den = are * are + aim * aim
    nr, ni = abr - 1.0, abi
    cr = (nr * are + ni * aim) / den
    ci = (ni * are - nr * aim) / den
    bbr = cr * bre - ci * bim
    bbi = cr * bim + ci * bre
    tc = lax.broadcasted_iota(jnp.int32, (16, 128), 0)
    tl = lax.broadcasted_iota(jnp.int32, (16, 128), 1)
    T = (tl % 16 == tc).astype(f32)
    mr = (lax.broadcasted_iota(jnp.int32, (n, 128), 0) // 64) % 8
    mc = lax.broadcasted_iota(jnp.int32, (n, 128), 1) // 16
    mask = (mr == mc).astype(f32)

    def expand(v):
        return jnp.dot(v, T, precision=HI, preferred_element_type=f32) * mask

    return expand(bbr), expand(bbi), expand(cre), expand(cim), _col_to_row(abr), _col_to_row(abi)


def s5_params_fwd(are, aim, ldt_row, bre, bim, cre, cim):
    def body(*refs):
        outs = _s5_param_map(*[r[...] for r in refs[:7]])
        for o_ref, o in zip(refs[7:], outs):
            o_ref[...] = o

    return pl.pallas_call(
        body, out_shape=[S((NST, 128), f32)] * 4 + [S((1, NST), f32)] * 2,
        compiler_params=_cp(), name="s5_params_fwd")(are, aim, ldt_row, bre, bim, cre, cim)


def s5_params_bwd(are, aim, ldt_row, bre, bim, cre, cim, cots):
    def body(*refs):
        _, vjp = jax.vjp(_s5_param_map, *[r[...] for r in refs[:7]])
        gs = vjp(tuple(r[...] for r in refs[7:13]))
        for o_ref, o in zip(refs[13:], gs):
            o_ref[...] = o

    return pl.pallas_call(
        body, out_shape=[S((NST, 1), f32)] * 2 + [S((1, 32), f32)] + [S((NST, 16), f32)] * 4,
        compiler_params=_cp(), name="s5_params_bwd")(are, aim, ldt_row, bre, bim, cre, cim, *cots)


def _cpowers(ar, ai):
    out = [(ar, ai)]
    for _ in range(7):
        pr, pi = out[-1]
        out.append((pr * ar - pi * ai, pr * ai + pi * ar))
    return out


def _ctable(pw, rid, power):
    tr_ = jnp.zeros(rid.shape, f32)
    ti_ = jnp.zeros(rid.shape, f32)
    for r in range(8):
        pr, pi = pw[power(r) - 1]
        tr_ = jnp.where(rid == r, pr, tr_)
        ti_ = jnp.where(rid == r, pi, ti_)
    return tr_, ti_


NT5 = 4
RC = 256


def s5_scan_fwd(proj, wbr, wbi, wcr, wci, abr, abi, drow, plan=None):
    def body(u_ref, wbr_ref, wbi_ref, wcr_ref, wci_ref, ar_ref, ai_ref, d_ref, xr_ref, xi_ref, y_ref):
        wbr_v, wbi_v = wbr_ref[...], wbi_ref[...]
        for r in range(L // RC):
            rows = pl.ds(r * RC, RC)
            ub = u_ref[rows, :]
            xr_ref[rows, :] = dot_nt(ub, wbr_v)
            xi_ref[rows, :] = dot_nt(ub, wbi_v)
        pw = _cpowers(ar_ref[...], ai_ref[...])
        rid = lax.broadcasted_iota(jnp.int32, (8, 512), 0)
        tr_, ti_ = _ctable(pw, rid, lambda r: r + 1)

        def group(j, c):
            cr, ci = c
            rows = pl.ds(pl.multiple_of(j * 8, 8), 8)
            br, bi = xr_ref[rows, :], xi_ref[rows, :]
            for s in (1, 2, 4):
                pr, pi = pw[s - 1]
                sr = jnp.where(rid >= s, pltpu.roll(br, s, 0), 0.0)
                si = jnp.where(rid >= s, pltpu.roll(bi, s, 0), 0.0)
                br, bi = br + pr * sr - pi * si, bi + pr * si + pi * sr
            br, bi = br + tr_ * cr - ti_ * ci, bi + tr_ * ci + ti_ * cr
            xr_ref[rows, :] = br
            xi_ref[rows, :] = bi
            return br[7:8], bi[7:8]

        z = jnp.zeros((1, 512), f32)
        lax.fori_loop(0, L // 8, group, (z, z), unroll=2)
        wcr_v, wci_v, dv = wcr_ref[...], wci_ref[...], d_ref[...]
        for r in range(L // RC):
            rows = pl.ds(r * RC, RC)
            y_ref[rows, :] = (dot_nn(xr_ref[rows, :], wcr_v) - dot_nn(xi_ref[rows, :], wci_v)
                              + dv * u_ref[rows, :])

    wspec = pl.BlockSpec((512, 128), lambda j: (j, 0))
    aspec = pl.BlockSpec((1, 512), lambda j: (0, j))
    return pcall(
        body, plan, grid=(NT5,),
        in_specs=[pl.BlockSpec((L, 128), lambda j: (0, j)), wspec, wspec, wspec, wspec, aspec, aspec,
                  pl.BlockSpec((1, 128), lambda j: (0, j))],
        out_specs=[pl.BlockSpec((L, 512), lambda j: (0, j)), pl.BlockSpec((L, 512), lambda j: (0, j)),
                   pl.BlockSpec((L, 128), lambda j: (0, j))],
        out_shape=[S((L, NST), f32), S((L, NST), f32), S((L, S5W), f32)],
        sem=("parallel",), name="s5_scan_fwd", args=[proj, wbr, wbi, wcr, wci, abr, abi, drow])


def s5_scan_bwd(dy, proj, xs_re, xs_im, wbr, wbi, wcr, wci, abr, abi, drow, plan=None):
    def body(dy_ref, u_ref, xr_ref, xi_ref, wbr_ref, wbi_ref, wcr_ref, wci_ref, ar_ref, ai_ref, d_ref,
             du_ref, gwbr_ref, gwbi_ref, gwcr_ref, gwci_ref, gar_ref, gai_ref, gd_ref, lr_ref, li_ref):
        wcr_v, wci_v = wcr_ref[...], wci_ref[...]
        gwcr = jnp.zeros((512, 128), f32)
        gwci = jnp.zeros((512, 128), f32)
        gd = jnp.zeros((1, 128), f32)
        for r in range(L // RC):
            rows = pl.ds(r * RC, RC)
            dyv = dy_ref[rows, :]
            lr_ref[rows, :] = dot_nt(dyv, wcr_v)
            li_ref[rows, :] = -dot_nt(dyv, wci_v)
            gwcr += dot_tn(xr_ref[rows, :], dyv)
            gwci -= dot_tn(xi_ref[rows, :], dyv)
            gd += jnp.sum(dyv * u_ref[rows, :], axis=0, keepdims=True)
        gwcr_ref[...] = gwcr
        gwci_ref[...] = gwci
        gd_ref[...] = gd
        pw = _cpowers(ar_ref[...], -ai_ref[...])
        rid = lax.broadcasted_iota(jnp.int32, (8, 512), 0)
        tr_, ti_ = _ctable(pw, rid, lambda r: 8 - r)

        def group(i, c):
            cr, ci, gar, gai = c
            j = L // 8 - 1 - i
            rows = pl.ds(pl.multiple_of(j * 8, 8), 8)
            br, bi = lr_ref[rows, :], li_ref[rows, :]
            for s in (1, 2, 4):
                pr, pi = pw[s - 1]
                sr = jnp.where(rid < 8 - s, pltpu.roll(br, 8 - s, 0), 0.0)
                si = jnp.where(rid < 8 - s, pltpu.roll(bi, 8 - s, 0), 0.0)
                br, bi = br + pr * sr - pi * si, bi + pr * si + pi * sr
            br, bi = br + tr_ * cr - ti_ * ci, bi + tr_ * ci + ti_ * cr
            lr_ref[rows, :] = br
            li_ref[rows, :] = bi
            nr = jnp.where(rid < 7, pltpu.roll(br, 7, 0), cr)
            ni = jnp.where(rid < 7, pltpu.roll(bi, 7, 0), ci)
            xr, xi = xr_ref[rows, :], xi_ref[rows, :]
            return br[0:1], bi[0:1], gar + xr * nr + xi * ni, gai + xr * ni - xi * nr

        z = jnp.zeros((1, 512), f32)
        z8 = jnp.zeros((8, 512), f32)
        _, _, gar, gai = lax.fori_loop(0, L // 8, group, (z, z, z8, z8), unroll=2)
        gar_ref[...] = jnp.sum(gar, axis=0, keepdims=True)
        gai_ref[...] = jnp.sum(gai, axis=0, keepdims=True)
        wbr_v, wbi_v, dv = wbr_ref[...], wbi_ref[...], d_ref[...]
        gwbr = jnp.zeros((512, 128), f32)
        gwbi = jnp.zeros((512, 128), f32)
        for r in range(L // RC):
            rows = pl.ds(r * RC, RC)
            lrv, liv, uv = lr_ref[rows, :], li_ref[rows, :], u_ref[rows, :]
            du_ref[rows, :] = (dot_nn(lrv, wbr_v) + dot_nn(liv, wbi_v) + dv * dy_ref[rows, :]).astype(du_ref.dtype)
            gwbr += dot_tn(lrv, uv)
            gwbi += dot_tn(liv, uv)
        gwbr_ref[...] = gwbr
        gwbi_ref[...] = gwbi

    wspec = pl.BlockSpec((512, 128), lambda j: (j, 0))
    aspec = pl.BlockSpec((1, 512), lambda j: (0, j))
    col = pl.BlockSpec((L, 128), lambda j: (0, j))
    st = pl.BlockSpec((L, 512), lambda j: (0, j))
    dspec = pl.BlockSpec((1, 128), lambda j: (0, j))
    return pcall(
        body, plan, grid=(NT5,),
        in_specs=[col, col, st, st, wspec, wspec, wspec, wspec, aspec, aspec, dspec],
        out_specs=[col, wspec, wspec, wspec, wspec, aspec, aspec, dspec],
        out_shape=[S((L, S5W), BF)] + [S((NST, 128), f32)] * 4 + [S((1, NST), f32)] * 2 + [S((1, S5W), f32)],
        scratch_shapes=[pltpu.VMEM((L, 512), f32), pltpu.VMEM((L, 512), f32)],
        sem=("parallel",), name="s5_scan_bwd", args=[dy, proj, xs_re, xs_im, wbr, wbi, wcr, wci, abr, abi, drow])


def _glu(y, w, b):
    z = jax.nn.gelu(y)
    return z * jax.nn.sigmoid(dot_nn(z, w) + b)


def s5_glu_fwd(y, w, b):
    def body(y_ref, w_ref, b_ref, o_ref):
        o_ref[...] = _glu(y_ref[...], w_ref[...], b_ref[...]).astype(o_ref.dtype)

    return pl.pallas_call(
        body, grid=(L // TR,),
        in_specs=[pl.BlockSpec((TR, S5W), lambda i: (i, 0)), pl.BlockSpec((S5W, S5W), lambda i: (0, 0)),
                  pl.BlockSpec((1, S5W), lambda i: (0, 0))],
        out_specs=pl.BlockSpec((TR, S5W), lambda i: (i, 0)), out_shape=S((L, S5W), BF),
        compiler_params=_cp(("parallel",)), name="s5_glu_fwd")(y, w, b)


def s5_glu_bwd(y, w, b, dmix):
    def body(y_ref, w_ref, b_ref, g_ref, dy_ref, dw_ref, db_ref):
        _, vjp = jax.vjp(_glu, y_ref[...], w_ref[...].astype(f32), b_ref[...])
        dy, dw, db = vjp(g_ref[...])
        dy_ref[...] = dy

        @pl.when(pl.program_id(0) == 0)
        def _():
            dw_ref[...] = jnp.zeros_like(dw_ref)
            db_ref[...] = jnp.zeros_like(db_ref)

        dw_ref[...] += dw
        db_ref[...] += db

    row = pl.BlockSpec((TR, S5W), lambda i: (i, 0))
    return pl.pallas_call(
        body, grid=(L // TR,),
        in_specs=[row, pl.BlockSpec((S5W, S5W), lambda i: (0, 0)), pl.BlockSpec((1, S5W), lambda i: (0, 0)), row],
        out_specs=[row, pl.BlockSpec((S5W, S5W), lambda i: (0, 0)), pl.BlockSpec((1, S5W), lambda i: (0, 0))],
        out_shape=[S((L, S5W), f32), S((S5W, S5W), f32), S((1, S5W), f32)],
        compiler_params=_cp(("arbitrary",)), name="s5_glu_bwd")(y, w, b, dmix)


def _dg3(a, b, ca, cb):
    ah, bh = a.astype(BF), b.astype(BF)
    al, bl = (a - ah.astype(f32)).astype(BF), (b - bh.astype(f32)).astype(BF)
    return _dg(ah, bh, ca, cb) + _dg(ah, bl, ca, cb) + _dg(al, bh, ca, cb)


@jax.custom_vjp
def hi_nn(a, b):
    return _dg3(a, b, 1, 0)


@jax.custom_vjp
def hi_nt(a, b):
    return _dg3(a, b, 1, 1)


@jax.custom_vjp
def hi_tn(a, b):
    return _dg3(a, b, 0, 0)


hi_nn.defvjp(lambda a, b: (hi_nn(a, b), (a, b)), lambda r, g: (hi_nt(g, r[1]), hi_tn(r[0], g)))
hi_nt.defvjp(lambda a, b: (hi_nt(a, b), (a, b)), lambda r, g: (hi_nn(g, r[1]), hi_tn(g, r[0])))
hi_tn.defvjp(lambda a, b: (hi_tn(a, b), (a, b)), lambda r, g: (hi_nt(r[1], g), hi_nn(r[0], g)))


def _hgrn_chunk(St, xq, xf, xi, xg, gam, ng):
    lb = jax.nn.sigmoid(gam[0:1] - gam[1:2])
    q = jax.nn.silu(xq)
    f = lb + (1.0 - lb) * jax.nn.sigmoid(xf)
    k = 1.0 - f
    g = jnp.log(f)
    ti = lax.broadcasted_iota(jnp.int32, (HGC, HGC), 0)
    si = lax.broadcasted_iota(jnp.int32, (HGC, HGC), 1)
    causal = si <= ti
    b = jnp.dot(causal.astype(f32), g, precision=HI, preferred_element_type=f32)
    qe = q * jnp.exp(b)
    o = dot_nt(qe, St)
    att = jnp.where(causal, hi_nt(qe, k * jnp.exp(-b)), 0.0)
    o = o + dot_nn(att, xi)
    bl = b[HGC - 1:HGC]
    St_new = St * jnp.exp(bl) + dot_tn(xi, k * jnp.exp(bl - b))
    o = o * lax.rsqrt(jnp.mean(o * o, axis=-1, keepdims=True) + EPS) * ng
    return St_new, o * jax.nn.silu(xg)


NCH = L // HGC


def hgrn_fwd(proj, gamma, hnorm, plan=None):
    def body(q_ref, f_ref, i_ref, g_ref, gam_ref, ng_ref, o_ref, ss_ref, st):
        @pl.when(pl.program_id(0) == 0)
        def _():
            st[...] = jnp.zeros_like(st)

        for h in range(4):
            sl = slice(h * 128, (h + 1) * 128)
            s0 = st[h]
            ss_ref[0, h] = s0
            s1, o = _hgrn_chunk(s0, q_ref[:, sl], f_ref[:, sl], i_ref[:, sl], g_ref[:, sl], gam_ref[:, sl], ng_ref[:, sl])
            st[h] = s1
            o_ref[:, sl] = o.astype(o_ref.dtype)

    def pj(n):
        return pl.BlockSpec((HGC, 512), lambda c: (c, n))

    return pcall(
        body, plan, grid=(NCH,),
        in_specs=[pj(1), pj(2), pj(3), pj(4), pl.BlockSpec((2, 512), lambda c: (0, 0)), pl.BlockSpec((1, 512), lambda c: (0, 0))],
        out_specs=[pl.BlockSpec((HGC, 512), lambda c: (c, 0)), pl.BlockSpec((1, 4, 128, 128), lambda c: (c, 0, 0, 0))],
        out_shape=[S((L, 512), BF), S((NCH, 4, 128, 128), f32)],
        scratch_shapes=[pltpu.VMEM((4, 128, 128), f32)],
        sem=("arbitrary",), name="hgrn_fwd", args=[proj, proj, proj, proj, gamma, hnorm])


def hgrn_bwd(proj, gamma, hnorm, ssave, dmix, du, plan=None):
    def body(q_ref, f_ref, i_ref, g_ref, gam_ref, ng_ref, ss_ref, do_ref, du_ref, dp_ref, dgam_ref, dng_ref, dst):
        @pl.when(pl.program_id(0) == 0)
        def _():
            dst[...] = jnp.zeros_like(dst)
            dgam_ref[...] = jnp.zeros_like(dgam_ref)
            dng_ref[...] = jnp.zeros_like(dng_ref)

        dp_ref[:, 0:512] = du_ref[...]
        for h in range(4):
            sl = slice(h * 128, (h + 1) * 128)
            _, vjp = jax.vjp(_hgrn_chunk, ss_ref[0, h], q_ref[:, sl], f_ref[:, sl], i_ref[:, sl], g_ref[:, sl],
                             gam_ref[:, sl], ng_ref[:, sl])
            ds, dq, df, di, dg, dgam, dng = vjp((dst[h], do_ref[:, sl]))
            dst[h] = ds
            for n, v in enumerate((dq, df, di, dg)):
                dp_ref[:, 512 * (n + 1) + h * 128: 512 * (n + 1) + (h + 1) * 128] = v.astype(dp_ref.dtype)
            dgam_ref[:, sl] += dgam
            dng_ref[:, sl] += dng

    def pj(n):
        return pl.BlockSpec((HGC, 512), lambda i: (NCH - 1 - i, n))

    return pcall(
        body, plan, grid=(NCH,),
        in_specs=[pj(1), pj(2), pj(3), pj(4), pl.BlockSpec((2, 512), lambda i: (0, 0)), pl.BlockSpec((1, 512), lambda i: (0, 0)),
                  pl.BlockSpec((1, 4, 128, 128), lambda i: (NCH - 1 - i, 0, 0, 0)), pj(1), pj(0)],
        out_specs=[pl.BlockSpec((HGC, 2560), lambda i: (NCH - 1 - i, 0)), pl.BlockSpec((2, 512), lambda i: (0, 0)),
                   pl.BlockSpec((1, 512), lambda i: (0, 0))],
        out_shape=[S((L, 2560), BF), S((2, 512), f32), S((1, 512), f32)],
        scratch_shapes=[pltpu.VMEM((4, 128, 128), f32)],
        sem=("arbitrary",), name="hgrn_bwd", args=[proj, proj, proj, proj, gamma, hnorm, ssave, dmix, du])


def _earlier(h_ref, k, r0, n):
    if r0 > 0:
        return h_ref[pl.ds(r0 - k, n), :]
    rid = lax.broadcasted_iota(jnp.int32, (8, h_ref.shape[1]), 0)
    head = jnp.where(rid >= k, pltpu.roll(h_ref[pl.ds(0, 8), :], k, 0), 0.0)
    return jnp.concatenate([head, h_ref[pl.ds(8 - k, n - 8), :]], axis=0)


def _conv3_rows(h_ref, w, b, r0, n=None):
    n = CR if n is None else n
    h1, h2 = _earlier(h_ref, 1, r0, n), _earlier(h_ref, 2, r0, n)
    return w[2:3] * h_ref[pl.ds(r0, n), :] + w[1:2] * h1 + w[0:1] * h2 + b, h1, h2


CT = 128
NCT = DFF // CT
CR = 64


def convact_fwd(hu, cw, cb, layer, plan=None):
    def body(ha_ref, hb_ref, wa_ref, wb_ref, ba_ref, bb_ref, o_ref):
        ca = _conv3_rows(ha_ref, wa_ref[...], ba_ref[...], 0, L)[0]
        cb_ = _conv3_rows(hb_ref, wb_ref[...], bb_ref[...], 0, L)[0]
        o_ref[...] = (jax.nn.silu(ca) * cb_).astype(o_ref.dtype)

    def h(off):
        return pl.BlockSpec((L, CT), lambda j: (0, j + off))

    def w(off):
        return pl.BlockSpec((3, CT), lambda j: (0, j + off))

    def b(off):
        return pl.BlockSpec((None, 1, CT), lambda j: (layer, 0, j + off))

    return pcall(body, plan, grid=(NCT,), in_specs=[h(0), h(NCT), w(0), w(NCT), b(0), b(NCT)],
                 out_specs=pl.BlockSpec((L, CT), lambda j: (0, j)), out_shape=S((L, DFF), BF),
                 sem=("parallel",), name=f"convact_fwd{layer}", args=[hu, hu, cw, cw, cb, cb])


def convact_bwd(hu, cw, cb, dact, layer, plan=None):
    def body(ha_ref, hb_ref, wa_ref, wb_ref, ba_ref, bb_ref, g_ref, dh_ref, dw_ref, db_ref, sh, sw, sb, da_scr, db_scr):
        j = pl.program_id(0)

        def fold(x):
            return functools.reduce(jnp.add, [x[8 * m:8 * m + 8] for m in range(CR // 8)])

        @pl.when(j < NCT)
        def _():
            wa, wb, ba, bb = wa_ref[...], wb_ref[...], ba_ref[...], bb_ref[...]
            da_scr[pl.ds(L, 8), :] = jnp.zeros((8, CT), f32)
            db_scr[pl.ds(L, 8), :] = jnp.zeros((8, CT), f32)
            acc = [jnp.zeros((8, CT), f32) for _ in range(8)]
            for c in range(L // CR):
                r0 = c * CR
                ca, a1, a2 = _conv3_rows(ha_ref, wa, ba, r0)
                cb_, b1, b2 = _conv3_rows(hb_ref, wb, bb, r0)
                g = g_ref[pl.ds(r0, CR), :].astype(f32)
                sg = jax.nn.sigmoid(ca)
                dca = g * cb_ * (sg * (1.0 + ca * (1.0 - sg)))
                dcb = g * (ca * sg)
                da_scr[pl.ds(r0, CR), :] = dca
                db_scr[pl.ds(r0, CR), :] = dcb
                terms = (dca * a2, dca * a1, dca * ha_ref[pl.ds(r0, CR), :], dca,
                         dcb * b2, dcb * b1, dcb * hb_ref[pl.ds(r0, CR), :], dcb)
                acc = [a + fold(t) for a, t in zip(acc, terms)]
            rows = [jnp.sum(a, axis=0, keepdims=True) for a in acc]
            for k in range(3):
                dw_ref[k:k + 1, :] = rows[k]
                sw[j, k:k + 1, :] = rows[4 + k]
            db_ref[...] = rows[3]
            sb[j] = rows[7]
            for c in range(L // CR):
                r0 = c * CR
                for scr, w, out in ((da_scr, wa, dh_ref), (db_scr, wb, sh.at[j])):
                    dh = (w[2:3] * scr[pl.ds(r0, CR), :] + w[1:2] * scr[pl.ds(r0 + 1, CR), :]
                          + w[0:1] * scr[pl.ds(r0 + 2, CR), :])
                    out[pl.ds(r0, CR), :] = dh.astype(out.dtype)

        @pl.when(j >= NCT)
        def _():
            dh_ref[...] = sh[j - NCT]
            dw_ref[...] = sw[j - NCT]
            db_ref[...] = sb[j - NCT]

    def lo(j):
        return jnp.minimum(j, NCT - 1)

    in_specs = [pl.BlockSpec((L, CT), lambda j: (0, lo(j))), pl.BlockSpec((L, CT), lambda j: (0, lo(j) + NCT)),
                pl.BlockSpec((3, CT), lambda j: (0, lo(j))), pl.BlockSpec((3, CT), lambda j: (0, lo(j) + NCT)),
                pl.BlockSpec((None, 1, CT), lambda j: (layer, 0, lo(j))), pl.BlockSpec((None, 1, CT), lambda j: (layer, 0, lo(j) + NCT)),
                pl.BlockSpec((L, CT), lambda j: (0, lo(j)))]
    return pcall(
        body, plan, grid=(2 * NCT,), in_specs=in_specs,
        out_specs=[pl.BlockSpec((L, CT), lambda j: (0, j)), pl.BlockSpec((3, CT), lambda j: (0, j)), pl.BlockSpec((1, CT), lambda j: (0, j))],
        out_shape=[S((L, 2 * DFF), BF), S((3, 2 * DFF), f32), S((1, 2 * DFF), f32)],
        scratch_shapes=[pltpu.VMEM((NCT, L, CT), BF), pltpu.VMEM((NCT, 3, CT), f32), pltpu.VMEM((NCT, 1, CT), f32),
                        pltpu.VMEM((L + 8, CT), f32), pltpu.VMEM((L + 8, CT), f32)],
        sem=("arbitrary",), name=f"convact_bwd{layer}", args=[hu, hu, cw, cw, cb, cb, dact])


DILS = (1, 4, 16)
AB = 128
NPAIR = 12


def _rope_tables(pos_ref, invf_ref):
    ang = pos_ref[...].astype(f32) * invf_ref[...]
    lane = lax.broadcasted_iota(jnp.int32, (1, 128), 1) % 64
    cosf = jnp.where(lane < 16, jnp.cos(ang), 1.0)
    sn = jnp.sin(ang)
    s_lo = jnp.where(lane < 8, -sn, 0.0)
    s_hi = jnp.where((lane >= 8) & (lane < 16), sn, 0.0)
    return cosf, s_lo, s_hi


def _rope(t, cosf, s_lo, s_hi):
    return t * cosf + pltpu.roll(t, 120, 1) * s_lo + pltpu.roll(t, 8, 1) * s_hi


def _rope_t(g, cosf, s_lo, s_hi):
    return g * cosf + pltpu.roll(g * s_lo, 8, 1) + pltpu.roll(g * s_hi, 120, 1)


def _att_block(q2, kp, kc, vp, vc, first):
    lane = lax.broadcasted_iota(jnp.int32, (1, 128), 1)
    qi = lax.broadcasted_iota(jnp.int32, (AB, 2 * AB), 0) + AB
    kj = lax.broadcasted_iota(jnp.int32, (AB, 2 * AB), 1)
    back = qi - kj
    valid = (back >= 0) & (back <= AB)
    if first:
        valid = valid & (kj >= AB)
    kk = jnp.concatenate([kp, kc], axis=0)
    vv = jnp.concatenate([vp, vc], axis=0)
    o2 = jnp.zeros((AB, 128), f32)
    lse2 = jnp.zeros((AB, 128), f32)
    for e in range(2):
        hm = ((lane >= 64 * e) & (lane < 64 * (e + 1))).astype(f32)
        s = dot_nt(q2 * (hm * 0.125), kk)
        s = jnp.where(valid, s, -jnp.inf)
        m = jnp.max(s, axis=-1, keepdims=True)
        p = jnp.exp(s - m)
        den = jnp.sum(p, axis=-1, keepdims=True)
        o2 = o2 + dot_nn(p, vv * hm) / den
        lse2 = lse2 + (m + jnp.log(den)) * hm
    return o2, lse2


def _att_blocks(dil):
    m = L // dil
    return [(r * m + n * AB, n == 0) for r in range(dil) for n in range(m // AB)]


def deinterleave(x, dil):
    return x if dil == 1 else x.reshape(L // dil, dil, x.shape[1]).swapaxes(0, 1).reshape(L, x.shape[1])


def attn_fwd(qkv, pos, invf, g, plan=None):
    blocks = _att_blocks(DILS[g])

    def body(q_ref, k_ref, v_ref, pos_ref, invf_ref, o_ref, l_ref, qr, kr):
        cosf, s_lo, s_hi = _rope_tables(pos_ref, invf_ref)
        qr[...] = _rope(q_ref[...], cosf, s_lo, s_hi)
        kr[...] = _rope(k_ref[...], cosf, s_lo, s_hi)
        for off, first in blocks:
            cur, prv = pl.ds(off, AB), pl.ds(off if first else off - AB, AB)
            o2, lse2 = _att_block(qr[cur, :], kr[prv, :], kr[cur, :], v_ref[prv, :], v_ref[cur, :], first)
            o_ref[cur, :] = o2
            l_ref[cur, :] = lse2

    def sec(n):
        return pl.BlockSpec((L, 128), lambda p: (0, p + 4 * n))

    return pcall(
        body, plan, grid=(4,),
        in_specs=[sec(0), sec(1), sec(2), pl.BlockSpec((L, 1), lambda p: (0, 0)), pl.BlockSpec((1, 128), lambda p: (0, 0))],
        out_specs=[sec(0), sec(0)], out_shape=[S((L, 512), f32), S((L, 512), f32)],
        scratch_shapes=[pltpu.VMEM((L, 128), f32), pltpu.VMEM((L, 128), f32)],
        sem=("parallel",), name=f"attn_fwd{g}", args=[qkv, qkv, qkv, pos, invf])


def _att_block_bwd(q2, kp, kc, vp, vc, lse2, do2, dl2, first):
    lane = lax.broadcasted_iota(jnp.int32, (1, 128), 1)
    qi = lax.broadcasted_iota(jnp.int32, (AB, 2 * AB), 0) + AB
    kj = lax.broadcasted_iota(jnp.int32, (AB, 2 * AB), 1)
    back = qi - kj
    valid = (back >= 0) & (back <= AB)
    if first:
        valid = valid & (kj >= AB)
    kk = jnp.concatenate([kp, kc], axis=0)
    vv = jnp.concatenate([vp, vc], axis=0)
    dq2 = jnp.zeros((AB, 128), f32)
    dkk = jnp.zeros((2 * AB, 128), f32)
    dvv = jnp.zeros((2 * AB, 128), f32)
    for e in range(2):
        hb = (lane >= 64 * e) & (lane < 64 * (e + 1))
        hm = hb.astype(f32)
        qs = q2 * (hm * 0.125)
        lse = jnp.max(jnp.where(hb, lse2, -jnp.inf), axis=-1, keepdims=True)
        dls = jnp.sum(dl2 * hm, axis=-1, keepdims=True)
        p = jnp.where(valid, jnp.exp(dot_nt(qs, kk) - lse), 0.0)
        dov = do2 * hm
        dp = dot_nt(dov, vv)
        ds = p * (dp - jnp.sum(p * dp, axis=-1, keepdims=True) + dls)
        dq2 = dq2 + dot_nn(ds, kk) * (hm * 0.125)
        dkk = dkk + dot_tn(ds, qs)
        dvv = dvv + dot_tn(p, dov)
    return dq2, dkk[:AB], dkk[AB:], dvv[:AB], dvv[AB:]


def attn_bwd(qkv, pos, invf, lse, do, dl, g, plan=None):
    blocks = _att_blocks(DILS[g])

    def body(q_ref, k_ref, v_ref, pos_ref, invf_ref, l_ref, do_ref, dl_ref, d_ref, qr, kr, dqr, dkr, dvr):
        cosf, s_lo, s_hi = _rope_tables(pos_ref, invf_ref)
        qr[...] = _rope(q_ref[...], cosf, s_lo, s_hi)
        kr[...] = _rope(k_ref[...], cosf, s_lo, s_hi)
        for off, first in blocks:
            cur, prv = pl.ds(off, AB), pl.ds(off if first else off - AB, AB)
            dq2, dkp, dkc, dvp, dvc = _att_block_bwd(qr[cur, :], kr[prv, :], kr[cur, :], v_ref[prv, :], v_ref[cur, :],
                                                     l_ref[cur, :], do_ref[cur, :], dl_ref[cur, :], first)
            dqr[cur, :] = dq2
            dkr[cur, :] = dkc
            dvr[cur, :] = dvc
            if not first:
                dkr[prv, :] += dkp
                dvr[prv, :] += dvp
        d_ref[0] = _rope_t(dqr[...], cosf, s_lo, s_hi).astype(d_ref.dtype)
        d_ref[1] = _rope_t(dkr[...], cosf, s_lo, s_hi).astype(d_ref.dtype)
        d_ref[2] = dvr[...].astype(d_ref.dtype)

    def sec(n):
        return pl.BlockSpec((L, 128), lambda p: (0, p + 4 * n))

    return pcall(
        body, plan, grid=(4,),
        in_specs=[sec(0), sec(1), sec(2), pl.BlockSpec((L, 1), lambda p: (0, 0)), pl.BlockSpec((1, 128), lambda p: (0, 0)),
                  sec(0), sec(0), sec(0)],
        out_specs=pl.BlockSpec((3, L, 128), lambda p: (0, 0, p)), out_shape=S((3, L, 512), BF),
        scratch_shapes=[pltpu.VMEM((L, 128), f32)] * 5,
        sem=("parallel",), name=f"attn_bwd{g}", args=[qkv, qkv, qkv, pos, invf, lse, do, dl])


def _merge(o0, o1, o2, l0, l1, l2):
    m = jnp.maximum(jnp.maximum(l0, l1), l2)
    e0, e1, e2 = jnp.exp(l0 - m), jnp.exp(l1 - m), jnp.exp(l2 - m)
    return (e0 * o0 + e1 * o1 + e2 * o2) / (e0 + e1 + e2)


def _to_token_major(src_ref, scr, i, dil, slab):
    n = TR // dil
    for r in range(dil):
        rows = pl.ds(pl.multiple_of(r * (L // dil) + i * n, n), n)
        scr[pl.ds(r, n, stride=dil), :] = src_ref[rows, slab * 128:(slab + 1) * 128].astype(f32)
    return scr[...]


def _to_class_major(val, dst_ref, scr, i, dil, slab):
    n = TR // dil
    scr[...] = val
    for r in range(dil):
        rows = pl.ds(pl.multiple_of(r * (L // dil) + i * n, n), n)
        dst_ref[rows, slab * 128:(slab + 1) * 128] = scr[pl.ds(r, n, stride=dil), :].astype(dst_ref.dtype)


def rms_fwd_classes(x, g, name):
    def body(x_ref, g_ref, o_ref, o1_ref, o2_ref, scr):
        i = pl.program_id(0)
        y = _rms(x_ref[...], g_ref[...])
        o_ref[...] = y.astype(o_ref.dtype)
        for s in range(D // 128):
            ys = y[:, s * 128:(s + 1) * 128]
            _to_class_major(ys, o1_ref, scr, i, DILS[1], s)
            _to_class_major(ys, o2_ref, scr, i, DILS[2], s)

    row = pl.BlockSpec((TR, D), lambda i: (i, 0))
    full = pl.BlockSpec((L, D), lambda i: (0, 0))
    return pl.pallas_call(
        body, grid=(L // TR,), in_specs=[row, pl.BlockSpec((1, D), lambda i: (0, 0))], out_specs=[row, full, full],
        out_shape=[S((L, D), BF)] * 3, scratch_shapes=[pltpu.VMEM((TR, 128), f32)],
        compiler_params=_cp(("arbitrary",)), name=name)(x, g)


def rms_bwd_classes(x, g, dy0, dyc, dres, name, plan=None):
    def body(x_ref, g_ref, dy0_ref, d1_ref, d2_ref, dr_ref, dh_ref, dg_ref, scr, dyf):
        i = pl.program_id(0)
        for s in range(D // 128):
            sl = slice(s * 128, (s + 1) * 128)
            dyf[:, sl] = (dy0_ref[:, sl] + _to_token_major(d1_ref, scr.at[0], i, DILS[1], s)
                          + _to_token_major(d2_ref, scr.at[1], i, DILS[2], s))
        _, vjp = jax.vjp(_rms, x_ref[...], g_ref[...])
        dx, dg = vjp(dyf[...])
        dh_ref[...] = dr_ref[...] + dx

        @pl.when(i == 0)
        def _():
            dg_ref[...] = jnp.zeros_like(dg_ref)

        dg_ref[...] += dg

    row = pl.BlockSpec((TR, D), lambda i: (i, 0))
    vec = pl.BlockSpec((1, D), lambda i: (0, 0))
    full = pl.BlockSpec((L, D), lambda i: (0, 0))
    return pcall(body, plan, grid=(L // TR,), in_specs=[row, vec, row, full, full, row], out_specs=[row, vec],
                 out_shape=[S((L, D), f32), S((1, D), f32)],
                 scratch_shapes=[pltpu.VMEM((2, TR, 128), f32), pltpu.VMEM((TR, D), f32)],
                 sem=("arbitrary",), name=name, args=[x, g, dy0, dyc[0], dyc[1], dres])


def attn_merge_fwd(o0, l0, oc, lc):
    def body(o0_ref, l0_ref, o1_ref, l1_ref, o2_ref, l2_ref, o_ref, scr):
        i = pl.program_id(0)
        for s in range(4):
            sl = slice(s * 128, (s + 1) * 128)
            o1 = _to_token_major(o1_ref, scr.at[0], i, DILS[1], s)
            l1 = _to_token_major(l1_ref, scr.at[1], i, DILS[1], s)
            o2 = _to_token_major(o2_ref, scr.at[2], i, DILS[2], s)
            l2 = _to_token_major(l2_ref, scr.at[3], i, DILS[2], s)
            o_ref[:, sl] = _merge(o0_ref[:, sl], o1, o2, l0_ref[:, sl], l1, l2).astype(o_ref.dtype)

    blk = pl.BlockSpec((TR, 512), lambda i: (i, 0))
    full = pl.BlockSpec((L, 512), lambda i: (0, 0))
    return pl.pallas_call(
        body, grid=(L // TR,), in_specs=[blk, blk, full, full, full, full], out_specs=blk, out_shape=S((L, 512), BF),
        scratch_shapes=[pltpu.VMEM((4, TR, 128), f32)],
        compiler_params=_cp(("arbitrary",)), name="attn_merge_fwd")(o0, l0, oc[0], lc[0], oc[1], lc[1])


def attn_merge_bwd(o0, l0, oc, lc, do, plan=None):
    def body(o0_ref, l0_ref, o1_ref, l1_ref, o2_ref, l2_ref, g_ref, do0, dl0, do1, dl1, do2, dl2, scr):
        i = pl.program_id(0)
        for s in range(4):
            sl = slice(s * 128, (s + 1) * 128)
            o1 = _to_token_major(o1_ref, scr.at[0], i, DILS[1], s)
            l1 = _to_token_major(l1_ref, scr.at[1], i, DILS[1], s)
            o2 = _to_token_major(o2_ref, scr.at[2], i, DILS[2], s)
            l2 = _to_token_major(l2_ref, scr.at[3], i, DILS[2], s)
            _, vjp = jax.vjp(_merge, o0_ref[:, sl], o1, o2, l0_ref[:, sl], l1, l2)
            g0, g1, g2, h0, h1, h2 = vjp(g_ref[:, sl].astype(f32))
            do0[:, sl] = g0.astype(do0.dtype)
            dl0[:, sl] = h0
            _to_class_major(g1, do1, scr.at[0], i, DILS[1], s)
            _to_class_major(h1, dl1, scr.at[1], i, DILS[1], s)
            _to_class_major(g2, do2, scr.at[2], i, DILS[2], s)
            _to_class_major(h2, dl2, scr.at[3], i, DILS[2], s)

    blk = pl.BlockSpec((TR, 512), lambda i: (i, 0))
    full = pl.BlockSpec((L, 512), lambda i: (0, 0))
    outs = pcall(body, plan, grid=(L // TR,), in_specs=[blk, blk, full, full, full, full, blk],
                 out_specs=[blk, blk, full, full, full, full],
                 out_shape=[S((L, 512), BF), S((L, 512), f32)] * 3, scratch_shapes=[pltpu.VMEM((4, TR, 128), f32)],
                 sem=("arbitrary",), name="attn_merge_bwd", args=[o0, l0, oc[0], lc[0], oc[1], lc[1], do])
    return [outs[0], outs[2], outs[4]], [outs[1], outs[3], outs[5]]


def _invf_lanes():
    half = 8
    inv = ROPE_THETA ** (-np.arange(half, dtype=np.float32) * 2.0 / 16.0)
    lane = np.arange(128) % 64
    return jnp.asarray(np.where(lane < 16, inv[lane % 8], 0.0).astype(np.float32)[None, :])


def hosted(C, host, fn):
    p = C.plan(host) if C is not None else None
    out = fn(p)
    if p is not None:
        C.done(p)
    return out


def _ffn_fwd(h, g_row, W, cb, layer, C):
    hn = rms_fwd(h, g_row, f"rms_ffn{layer}")
    hu = hosted(C, f"ffn_in{layer}", lambda p: matmul(hn, W[("ffn_w_in", layer)], mode="nn", tm=1024, tn=1408, tk=1024,
                                                      plan=p, name=f"ffn_in{layer}"))
    act = hosted(C, f"convact_fwd{layer}", lambda p: convact_fwd(hu, W[("ffn_conv_w", layer)], cb, layer, plan=p))
    h2 = matmul(act, W[("ffn_w_out", layer)], mode="nn", tm=1024, tn=1024, tk=2816, add=h, name=f"ffn_out{layer}")
    return h2, (hn, hu, act)


def _ffn_bwd(dh, h, g_row, W, cb, saved, layer, C, G):
    hn, hu, act = saved
    w_in, w_out = W[("ffn_w_in", layer)], W[("ffn_w_out", layer)]
    dact = hosted(C, f"ffn_out_dx{layer}", lambda p: matmul(dh, w_out, mode="nt", tm=1024, tn=1408, tk=1024, plan=p,
                                                          name=f"ffn_out_dx{layer}"))
    G[("ffn_w_out", layer)] = hosted(C, f"ffn_out_dw{layer}", lambda p: matmul(
        act, dh, mode="tn", tm=1408, tn=1024, tk=L, out_dtype=BF, plan=p, name=f"ffn_out_dw{layer}"))
    dhu, G[("ffn_conv_w", layer)], g_cb = hosted(
        C, f"convact_bwd{layer}", lambda p: convact_bwd(hu, W[("ffn_conv_w", layer)], cb, dact, layer, plan=p))
    dhn = hosted(C, f"ffn_in_dx{layer}", lambda p: matmul(dhu, w_in, mode="nt", tm=1024, tn=1024, tk=2816, plan=p,
                                                         name=f"ffn_in_dx{layer}"))
    G[("ffn_w_in", layer)] = hosted(C, f"ffn_in_dw{layer}", lambda p: matmul(
        hn, dhu, mode="tn", tm=1024, tn=1408, tk=L, out_dtype=BF, plan=p, name=f"ffn_in_dw{layer}"))
    dh2, g_norm = hosted(C, f"rms_ffn_bwd{layer}", lambda p: rms_bwd(h, g_row, [dhn], dh, f"rms_ffn_bwd{layer}", plan=p))
    return dh2, g_cb, g_norm


def local_step(x, pos, tgt, sm, W, C=None):
    G = C.grads if C is not None else {}
    nm, nf = sm["norm_mix"], sm["norm_ffn"]
    invf = _invf_lanes()
    are = sm["s5_A_re"].reshape(NST, 1)
    aim = sm["s5_A_im"].reshape(NST, 1)
    ldt = sm["s5_log_dt"].reshape(1, 32)
    bre = sm["s5_B_re"].reshape(NST, 16)
    bim = sm["s5_B_im"].reshape(NST, 16)
    cre = jnp.swapaxes(sm["s5_C_re"][0], 1, 2).reshape(NST, 16)
    cim = jnp.swapaxes(sm["s5_C_im"][0], 1, 2).reshape(NST, 16)
    drow = sm["s5_D"].reshape(1, S5W)
    wbr, wbi, wcr, wci, abr, abi = s5_params_fwd(are, aim, ldt, bre, bim, cre, cim)
    hn0 = rms_fwd(x, nm[0:1], "rms_mix0")
    cb3 = sm["ffn_conv_b3"]
    proj = hosted(C, "mix_in", lambda p: matmul(hn0, W[("mix_w_in", 0)], mode="nn", tm=1024, tn=1280, tk=1024, plan=p, name="mix_in"))
    xs_re, xs_im, y5 = hosted(C, "s5_scan_fwd", lambda p: s5_scan_fwd(proj, wbr, wbi, wcr, wci, abr, abi, drow, plan=p))
    oa = s5_glu_fwd(y5, W[("s5_glu_w", 0)], sm["s5_glu_b"])
    ob, ssave = hosted(C, "hgrn_fwd", lambda p: hgrn_fwd(proj, sm["hgrn_gamma"], sm["hgrn_norm"], plan=p))
    cat = jnp.concatenate([oa, ob], axis=1)
    h1 = matmul(cat, W[("mix_w_out", 0)], mode="nn", tm=1024, tn=1024, tk=1024, add=x, name="mix_out")
    h2, ffn0 = _ffn_fwd(h1, nf[0:1], W, cb3, 0, C)
    hn2_g = rms_fwd_classes(h2, nm[1:2], "rms_mix1")
    wqkv = W[("att_w_qkv", 0)]
    pos_g, qkv_g, oc_g, lc_g = [], [], [], []
    for g, dil in enumerate(DILS):
        pos_g.append(deinterleave(pos, dil))
        qkv_g.append(hosted(C, f"att_qkv{g}", lambda p: matmul(
            hn2_g[g], wqkv, mode="nn", tm=1024, tn=512, tk=1024, dims=(L, 1536, D),
            b_spec=pl.BlockSpec((D, 512), lambda i, j, k, g=g: (0, 3 * j + g)), plan=p, name=f"att_qkv{g}")))
        o_c, l_c = hosted(C, f"attn_fwd{g}", lambda p: attn_fwd(qkv_g[g], pos_g[g], invf, g, plan=p))
        oc_g.append(o_c)
        lc_g.append(l_c)
    o = attn_merge_fwd(oc_g[0], lc_g[0], oc_g[1:], lc_g[1:])
    h3 = matmul(o, W[("att_w_o", 0)], mode="nn", tm=1024, tn=1024, tk=512, add=h2, name="att_o")
    h4, ffn1 = _ffn_fwd(h3, nf[1:2], W, cb3, 1, C)
    loss, dh, g_nfinal = loss_head(h4, sm["norm_final"].reshape(1, D), tgt)
    dh, g_cb1, g_nf1 = _ffn_bwd(dh, h3, nf[1:2], W, cb3, ffn1, 1, C, G)
    do = matmul(dh, W[("att_w_o", 0)], mode="nt", tm=1024, tn=512, tk=1024, name="att_o_dx")
    G[("att_w_o", 0)] = matmul(o, dh, mode="tn", tm=512, tn=1024, tk=L, out_dtype=BF, name="att_o_dw")
    do_g, dl_g = hosted(C, "attn_merge_bwd", lambda p: attn_merge_bwd(oc_g[0], lc_g[0], oc_g[1:], lc_g[1:], do, plan=p))
    dhn2_g, gq = [], []
    for g, dil in enumerate(DILS):
        d3 = hosted(C, f"attn_bwd{g}", lambda p: attn_bwd(qkv_g[g], pos_g[g], invf, lc_g[g], do_g[g], dl_g[g], g, plan=p))
        dx = matmul(d3, wqkv, mode="nt", tm=1024, tn=1024, tk=512, dims=(L, D, 1536),
                    a_spec=pl.BlockSpec((None, 1024, 512), lambda i, j, k: (k, i, 0)),
                    b_spec=pl.BlockSpec((D, 512), lambda i, j, k, g=g: (0, 3 * k + g)), name=f"att_qkv_dx{g}")
        dhn2_g.append(dx)
        gq.append(matmul(hn2_g[g], d3, mode="tn", tm=1024, tn=512, tk=L, out_dtype=BF, dims=(D, 1536, L),
                         b_spec=pl.BlockSpec((None, L, 512), lambda i, j, k: (j, k, 0)), name=f"att_qkv_dw{g}"))
    G[("att_w_qkv", 0)] = jnp.concatenate([gq[g][:, 512 * s:512 * (s + 1)] for s in range(3) for g in range(3)], axis=1)
    dh, g_nm1 = hosted(C, "rms_mix_bwd1", lambda p: rms_bwd_classes(h2, nm[1:2], dhn2_g[0], dhn2_g[1:], dh, "rms_mix_bwd1", plan=p))
    dh, g_cb0, g_nf0 = _ffn_bwd(dh, h1, nf[0:1], W, cb3, ffn0, 0, C, G)
    dmix = matmul(dh, W[("mix_w_out", 0)], mode="nt", tm=1024, tn=1024, tk=1024, name="mix_out_dx")
    G[("mix_w_out", 0)] = matmul(cat, dh, mode="tn", tm=1024, tn=1024, tk=L, out_dtype=BF, name="mix_out_dw")
    dy5, g_glu_w, g_glu_b = s5_glu_bwd(y5, W[("s5_glu_w", 0)], sm["s5_glu_b"], dmix)
    G[("s5_glu_w", 0)] = g_glu_w.astype(BF)
    du, gwbr, gwbi, gwcr, gwci, gabr, gabi, g_d = hosted(C, "s5_scan_bwd", lambda p: s5_scan_bwd(
        dy5, proj, xs_re, xs_im, wbr, wbi, wcr, wci, abr, abi, drow, plan=p))
    g_are, g_aim, g_ldt, g_bre, g_bim, g_cre, g_cim = s5_params_bwd(are, aim, ldt, bre, bim, cre, cim,
                                                                   (gwbr, gwbi, gwcr, gwci, gabr, gabi))
    dproj, g_gamma, g_hnorm = hosted(C, "hgrn_bwd", lambda p: hgrn_bwd(proj, sm["hgrn_gamma"], sm["hgrn_norm"], ssave, dmix, du,
                                                                       plan=p))
    dhn0 = hosted(C, "mix_in_dx", lambda p: matmul(dproj, W[("mix_w_in", 0)], mode="nt", tm=1024, tn=1024, tk=2560, plan=p,
                                                  name="mix_in_dx"))
    G[("mix_w_in", 0)] = matmul(hn0, dproj, mode="tn", tm=1024, tn=1280, tk=L, out_dtype=BF, name="mix_in_dw")
    gx, g_nm0 = hosted(C, "rms_mix_bwd0", lambda p: rms_bwd(x, nm[0:1], [dhn0], dh, "rms_mix_bwd0", plan=p))
    small = {
        "norm_mix": jnp.concatenate([g_nm0, g_nm1], axis=0), "norm_ffn": jnp.concatenate([g_nf0, g_nf1], axis=0),
        "norm_final": g_nfinal.reshape(D),
        "s5_A_re": g_are.reshape(1, 32, 64), "s5_A_im": g_aim.reshape(1, 32, 64), "s5_log_dt": g_ldt.reshape(1, 32),
        "s5_B_re": g_bre.reshape(1, 32, 64, 16), "s5_B_im": g_bim.reshape(1, 32, 64, 16),
        "s5_C_re": jnp.swapaxes(g_cre.reshape(1, 32, 64, 16), 2, 3), "s5_C_im": jnp.swapaxes(g_cim.reshape(1, 32, 64, 16), 2, 3),
        "s5_D": g_d.reshape(1, 32, 16), "s5_glu_b": g_glu_b, "hgrn_gamma": g_gamma, "hgrn_norm": g_hnorm,
        "ffn_conv_b": jnp.concatenate([g_cb0, g_cb1], axis=0),
    }
    return loss, gx, G, small


BIG = ("mix_w_in", "mix_w_out", "s5_glu_w", "att_w_qkv", "att_w_o", "ffn_w_in", "ffn_w_out", "ffn_conv_w")
SMALL = ("norm_mix", "norm_ffn", "norm_final", "s5_A_re", "s5_A_im", "s5_log_dt", "s5_B_re", "s5_B_im", "s5_C_re", "s5_C_im",
         "s5_D", "s5_glu_b", "hgrn_gamma", "hgrn_norm", "ffn_conv_b")


def cast_bf16(w, name, plan=None):
    nl, r, c = w.shape
    w2 = w.reshape(nl * r, c)
    tr = 256 if (nl * r) % 256 == 0 else nl * r

    def body(w_ref, o_ref):
        o_ref[...] = w_ref[...].astype(BF)

    out = pcall(body, plan, grid=(nl * r // tr,), in_specs=[pl.BlockSpec((tr, c), lambda i: (i, 0))],
                out_specs=pl.BlockSpec((tr, c), lambda i: (i, 0)), out_shape=S((nl * r, c), BF),
                sem=("parallel",), name=name, args=[w2])
    return out.reshape(nl, r, c)


SCHEDULE = {
    "cast_ffn_w_in": [("G", "mix_w_in", 0)],
    "mix_in": [("G", "mix_w_out", 0), ("G", "s5_glu_w", 0)],
    "s5_scan_fwd": [("G", "ffn_w_in", 0, (0, 2))],
    "hgrn_fwd": [("G", "ffn_w_in", 0, (1, 2)), ("G", "ffn_conv_w", 0), ("G", "ffn_conv_w", 1), ("G", "att_w_qkv", 0, (0, 2))],
    "ffn_in0": [("G", "ffn_w_out", 0)],
    "convact_fwd0": [("G", "att_w_qkv", 0, (1, 2))],
    "att_qkv0": [("G", "att_w_o", 0)],
    "attn_fwd0": [("G", "ffn_w_in", 1, (0, 2))],
    "attn_fwd1": [("G", "ffn_w_in", 1, (1, 2))],
    "attn_fwd2": [("G", "ffn_w_out", 1)],
    "convact_bwd1": [("P", "ffn_w_out", 1)],
    "ffn_in_dx1": [("A", "ffn_w_out", 1, (0, 2))],
    "ffn_in_dw1": [("A", "ffn_w_out", 1, (1, 2))],
    "rms_ffn_bwd1": [("P", "ffn_w_in", 1)],
    "attn_merge_bwd": [("P", "att_w_o", 0), ("A", "ffn_conv_w", 1), ("B", "ffn_w_out", 1)],
    "attn_bwd0": [("A", "ffn_w_in", 1, (0, 2)), ("A", "att_w_o", 0)],
    "attn_bwd1": [("A", "ffn_w_in", 1, (1, 2)), ("B", "att_w_o", 0), ("B", "ffn_conv_w", 1)],
    "attn_bwd2": [("B", "ffn_w_in", 1)],
    "rms_mix_bwd1": [("P", "att_w_qkv", 0)],
    "ffn_out_dx0": [("A", "att_w_qkv", 0, (0, 4))],
    "ffn_out_dw0": [("A", "att_w_qkv", 0, (1, 4))],
    "convact_bwd0": [("A", "att_w_qkv", 0, (2, 4)), ("A", "att_w_qkv", 0, (3, 4)), ("P", "ffn_w_out", 0)],
    "ffn_in_dx0": [("A", "ffn_w_out", 0, (0, 2)), ("B", "att_w_qkv", 0)],
    "ffn_in_dw0": [("A", "ffn_w_out", 0, (1, 2))],
    "rms_ffn_bwd0": [("P", "ffn_w_in", 0), ("B", "ffn_w_out", 0)],
    "s5_scan_bwd": [("A", "ffn_w_in", 0, (0, 2)), ("P", "mix_w_out", 0), ("P", "s5_glu_w", 0), ("A", "ffn_conv_w", 0)],
    "hgrn_bwd": [("A", "ffn_w_in", 0, (1, 2)), ("A", "mix_w_out", 0), ("A", "s5_glu_w", 0), ("B", "ffn_conv_w", 0)],
    "mix_in_dx": [("B", "ffn_w_in", 0), ("B", "mix_w_out", 0), ("B", "s5_glu_w", 0)],
    "rms_mix_bwd0": [("P", "mix_w_in", 0)],
    "adam_ffn_w_in": [("A", "mix_w_in", 0), ("A", "small", 0)],
    "adam_ffn_w_out": [("B", "mix_w_in", 0), ("B", "small", 0)],
}


class Comm:
    def __init__(self, shards, shapes):
        self.shards, self.shapes = shards, shapes
        self.W, self.grads, self.slots = {}, {}, {}
        self.sib, self.pair = {}, {}
        self.small = None

    def plan(self, host):
        items = SCHEDULE.get(host)
        if not items:
            return None
        p = Plan()
        for it in items:
            kind, name, l = it[:3]
            part, parts = it[3] if len(it) > 3 else (0, 1)
            if name == "small":
                kdst = p.buf("slots:small", arr=self.slots.get("small"), shape=S((8,) + self.small.shape, f32), write=True)
                if kind == "A":
                    ReduceOp(p, p.buf("g:small", arr=self.small), kdst, None, self.small.shape, False, 0, 0, whole=True)
                else:
                    ForwardOp(p, kdst, None, whole=True)
                continue
            nl, R, C_ = self.shapes[name]
            rows = name in ROW_SHARDED
            r0, nr = part * (R // parts), R // parts
            if kind == "G":
                sh = self.shards[name]
                kdst = p.buf(f"W:{name}:{l}", arr=self.W.get((name, l)), shape=S((4 * R, C_) if rows else (R, 4 * C_), sh.dtype),
                             write=True)
                GatherOp(p, p.buf("shard:" + name, arr=sh), kdst, l, self.shapes[name], rows, r0, nr, split=(nr % 32 == 0))
            elif name == "ffn_conv_w":
                g = self.grads[(name, l)]
                kdst = p.buf("slots:" + name, arr=self.slots.get(name), shape=S((8, nl, R, C_), g.dtype), write=True)
                if kind == "A":
                    ReduceOp(p, p.buf(f"g:{name}:{l}", arr=g), kdst, l, self.shapes[name], rows, r0, nr)
                else:
                    ForwardOp(p, kdst, l)
            elif kind == "P":
                g = self.grads[(name, l)]
                ksib = p.buf(f"sib:{name}:{l}", shape=S((4 * R // 2, C_) if rows else (R // 2, 4 * C_), g.dtype), write=True)
                PairOp(p, p.buf(f"g:{name}:{l}", arr=g), ksib, self.shapes[name], rows)
            else:
                if (name, l) not in self.pair:
                    self.pair[(name, l)] = pair_sum(self.grads[(name, l)], self.sib[(name, l)], rows, R, f"pair_sum_{name}{l}")
                h = self.pair[(name, l)]
                kdst = p.buf("slots:" + name, arr=self.slots.get(name), shape=S((4, nl, R, C_), h.dtype), write=True)
                if kind == "A":
                    ReduceOp(p, p.buf(f"h:{name}:{l}", arr=h), kdst, l, self.shapes[name], rows, r0 // 2, nr // 2, half=True)
                else:
                    HalfForwardOp(p, kdst, l, self.shapes[name])
        return p

    def done(self, p):
        for k, arr in p.out.items():
            tag, name = k.split(":")[:2]
            if tag == "W":
                self.W[(name, int(k.split(":")[2]))] = arr
            elif tag == "sib":
                self.sib[(name, int(k.split(":")[2]))] = arr
            else:
                self.slots[name] = arr


def _adamw(w, g, m, v):
    m = B1 * m + (1.0 - B1) * g
    v = B2 * v + (1.0 - B2) * jnp.square(g)
    m_hat = m / (1.0 - B1 ** STEP)
    v_hat = v / (1.0 - B2 ** STEP)
    return -LR * (m_hat / (jnp.sqrt(v_hat) + AEPS) + WD * w), m, v


def adam_big(w, m, v, slots, name, plan=None):
    nl, R, C = w.shape
    ns = slots.shape[0]
    tr = 128 if R % 128 == 0 else (64 if R % 64 == 0 else R)

    def body(w_ref, m_ref, v_ref, s_ref, g_ref, d_ref, nm_ref, nv_ref):
        g = s_ref[0].astype(f32)
        for s in range(1, ns):
            g = g + s_ref[s].astype(f32)
        d, nm_, nv_ = _adamw(w_ref[...], g, m_ref[...], v_ref[...])
        g_ref[...] = g
        d_ref[...] = d
        nm_ref[...] = nm_
        nv_ref[...] = nv_

    blk = pl.BlockSpec((None, tr, C), lambda l, i: (l, i, 0))
    return pcall(body, plan, grid=(nl, R // tr),
                 in_specs=[blk, blk, blk, pl.BlockSpec((ns, None, tr, C), lambda l, i: (0, l, i, 0))],
                 out_specs=[blk] * 4, out_shape=[S((nl, R, C), f32)] * 4,
                 sem=("parallel", "parallel"), name=name, args=[w, m, v, slots])


def adam_small(w, m, v, slots):
    R = w.shape[0]
    tr = 256

    def body(w_ref, m_ref, v_ref, s_ref, g_ref, d_ref, nm_ref, nv_ref):
        g = s_ref[0]
        for s in range(1, 8):
            g = g + s_ref[s]
        d, nm_, nv_ = _adamw(w_ref[...], g, m_ref[...], v_ref[...])
        g_ref[...] = g
        d_ref[...] = d
        nm_ref[...] = nm_
        nv_ref[...] = nv_

    blk = pl.BlockSpec((tr, 128), lambda i: (i, 0))
    return pl.pallas_call(
        body, grid=(R // tr,), in_specs=[blk, blk, blk, pl.BlockSpec((8, tr, 128), lambda i: (0, i, 0))],
        out_specs=[blk] * 4, out_shape=[S((R, 128), f32)] * 4,
        compiler_params=_cp(("parallel",)), name="adam_small")(w, m, v, slots)


def _pack(d):
    flat = jnp.concatenate([d[n].reshape(-1) for n in SMALL])
    n = flat.shape[0]
    rows = -(-n // (256 * 128)) * 256
    return jnp.pad(flat, (0, rows * 128 - n)).reshape(rows, 128)


def _unpack(p, like):
    flat = p.reshape(-1)
    out, off = {}, 0
    for n in SMALL:
        sz = math.prod(like[n].shape)
        out[n] = flat[off:off + sz].reshape(like[n].shape)
        off += sz
    return out


def kernel(x, positions, norm_mix, norm_ffn, norm_final, mix_w_in, mix_w_out, s5_A_re, s5_A_im, s5_log_dt, s5_B_re, s5_B_im, s5_C_re, s5_C_im, s5_D, s5_glu_w, s5_glu_b, hgrn_gamma, hgrn_norm, att_w_qkv, att_w_o, ffn_w_in, ffn_conv_w, ffn_conv_b, ffn_w_out, loss_target, m_norm_mix, m_norm_ffn, m_norm_final, m_mix_w_in, m_mix_w_out, m_s5_A_re, m_s5_A_im, m_s5_log_dt, m_s5_B_re, m_s5_B_im, m_s5_C_re, m_s5_C_im, m_s5_D, m_s5_glu_w, m_s5_glu_b, m_hgrn_gamma, m_hgrn_norm, m_att_w_qkv, m_att_w_o, m_ffn_w_in, m_ffn_conv_w, m_ffn_conv_b, m_ffn_w_out, v_norm_mix, v_norm_ffn, v_norm_final, v_mix_w_in, v_mix_w_out, v_s5_A_re, v_s5_A_im, v_s5_log_dt, v_s5_B_re, v_s5_B_im, v_s5_C_re, v_s5_C_im, v_s5_D, v_s5_glu_w, v_s5_glu_b, v_hgrn_gamma, v_hgrn_norm, v_att_w_qkv, v_att_w_o, v_ffn_w_in, v_ffn_conv_w, v_ffn_conv_b, v_ffn_w_out):
    a = dict(locals())
    weights = BIG + SMALL
    w = {n: a[n] for n in weights}
    m = {n: a["m_" + n] for n in weights}
    v = {n: a["v_" + n] for n in weights}
    shards = {"ffn_conv_w": ffn_conv_w}
    C = Comm(shards, {n: w[n].shape for n in BIG})
    for n in ("mix_w_in", "ffn_w_in", "mix_w_out", "s5_glu_w", "ffn_w_out", "att_w_qkv", "att_w_o"):
        shards[n] = hosted(C, "cast_" + n, lambda p: cast_bf16(w[n], "cast_" + n, plan=p))
    sm = {n: w[n] for n in SMALL}
    sm["ffn_conv_b3"] = ffn_conv_b.reshape(2, 1, 2 * DFF)
    loss, gx, _, gsmall = local_step(x[0], positions.reshape(L, 1), loss_target[0], sm, C.W, C)
    C.small = _pack(gsmall)
    res = {}
    for n in ("ffn_w_in", "ffn_w_out", "att_w_qkv", "att_w_o", "mix_w_out", "s5_glu_w", "ffn_conv_w", "mix_w_in"):
        res[n] = hosted(C, "adam_" + n, lambda p: adam_big(w[n], m[n], v[n], C.slots[n], "adam_" + n, plan=p))
    packed = adam_small(_pack({n: w[n] for n in SMALL}), _pack({n: m[n] for n in SMALL}), _pack({n: v[n] for n in SMALL}),
                        C.slots["small"])
    small_out = [_unpack(p, {n: w[n] for n in SMALL}) for p in packed]
    for n in SMALL:
        res[n] = tuple(so[n] for so in small_out)
    total = lax.psum(loss[0, 0], ("x", "y", "c"))
    order = ("norm_mix", "norm_ffn", "norm_final", "mix_w_in", "mix_w_out", "s5_A_re", "s5_A_im", "s5_log_dt", "s5_B_re", "s5_B_im",
             "s5_C_re", "s5_C_im", "s5_D", "s5_glu_w", "s5_glu_b", "hgrn_gamma", "hgrn_norm", "att_w_qkv", "att_w_o", "ffn_w_in",
             "ffn_conv_w", "ffn_conv_b", "ffn_w_out")
    return (total, gx[None], *[res[n][0] for n in order], *[res[n][1] for n in order], *[res[n][2] for n in order],
            *[res[n][3] for n in order])
```

```python
import functools
import math

import numpy as np
import jax
import jax.numpy as jnp
from jax import lax
from jax.experimental import pallas as pl
from jax.experimental.pallas import tpu as pltpu

f32 = jnp.float32
BF = jnp.bfloat16
HI = lax.Precision.HIGHEST
S = jax.ShapeDtypeStruct
MESH = pl.DeviceIdType.MESH

L = 2048
D = 1024
EPS = 1e-6
S5W = 512
NST = 2048
HGC = 64
DFF = 2816
ROPE_THETA = 500000.0
LR, B1, B2, AEPS, WD, STEP = 0.001, 0.9, 0.999, 1e-08, 0.01, 10
VMEM_LIMIT = 56 * 1024 * 1024


def _cp(sem=None):
    return pltpu.CompilerParams(dimension_semantics=sem, vmem_limit_bytes=VMEM_LIMIT)


ANY = pl.BlockSpec(memory_space=pl.ANY)
ROW_SHARDED = ("mix_w_out", "s5_glu_w", "ffn_w_out")


def _coords():
    x, y, c = lax.axis_index("x"), lax.axis_index("y"), lax.axis_index("c")
    return x, y, c, 2 * x + y, [(1 - x, y), (x, 1 - y), (1 - x, 1 - y)]


def _rows(start, n):
    return pl.ds(start if isinstance(start, int) else pl.multiple_of(start, 8), n)


def _cols(q, n):
    return pl.ds(pl.multiple_of(q * n, 128), n)


class Plan:
    def __init__(self):
        self.bufs, self.ops, self.nsem, self.out = {}, [], 0, {}

    def buf(self, key, arr=None, shape=None, write=False):
        b = self.bufs.setdefault(key, dict(arr=arr, shape=shape, write=False))
        b["write"] = b["write"] or write
        return key

    def add(self, op):
        op.base = self.nsem
        self.nsem += op.nsem
        self.ops.append(op)


class GatherOp:
    nsem = 13

    def __init__(self, plan, ksrc, kdst, l, shard_shape, rows, r0, nr, split):
        self.ksrc, self.kdst, self.l, (_, self.R, self.C), self.rows, self.r0, self.nr, self.split = (
            ksrc, kdst, l, shard_shape, rows, r0, nr, split)
        self.h = nr // 2 if split else nr
        plan.add(self)

    def _dst(self, R_, q, start, n):
        if self.rows:
            return R_[self.kdst].at[_rows(q * self.R + start, n), :]
        return R_[self.kdst].at[_rows(start, n), _cols(q, self.C)]

    def _mine(self, c):
        return self.r0 + (c * self.h if self.split else 0)

    def _theirs(self, c):
        return self.r0 + ((1 - c) * self.h if self.split else 0)

    def _copies(self, R_, sems):
        x, y, c, me, others = _coords()
        src = R_[self.ksrc]
        local = pltpu.make_async_copy(src.at[self.l, _rows(self.r0, self.nr), :], self._dst(R_, me, self.r0, self.nr),
                                      sems.at[self.base + 12])
        send, fwd = [], []
        for k, (px, py) in enumerate(others):
            q = 2 * px + py
            send.append((
                pltpu.make_async_remote_copy(src.at[self.l, _rows(self._mine(c), self.h), :], self._dst(R_, me, self._mine(c), self.h),
                                             sems.at[self.base + k], sems.at[self.base + 3 + k], device_id=(px, py, c), device_id_type=MESH),
                pltpu.make_async_remote_copy(src.at[self.l, _rows(self._mine(c), self.h), :], self._dst(R_, q, self._mine(c), self.h),
                                             sems.at[self.base + k], sems.at[self.base + 3 + k], device_id=(px, py, c), device_id_type=MESH)))
            fwd.append((
                pltpu.make_async_remote_copy(self._dst(R_, q, self._mine(c), self.h), self._dst(R_, q, self._mine(c), self.h),
                                             sems.at[self.base + 6 + k], sems.at[self.base + 9 + k], device_id=(x, y, 1 - c), device_id_type=MESH),
                pltpu.make_async_remote_copy(self._dst(R_, q, self._theirs(c), self.h), self._dst(R_, q, self._theirs(c), self.h),
                                             sems.at[self.base + 6 + k], sems.at[self.base + 9 + k], device_id=(x, y, 1 - c), device_id_type=MESH)))
        return local, send, fwd

    def start(self, R_, sems):
        local, send, _ = self._copies(R_, sems)
        local.start()
        for out, _ in send:
            out.start()

    def finish(self, R_, sems):
        local, send, fwd = self._copies(R_, sems)
        for k in range(3):
            send[k][1].wait_recv()
            if self.split:
                fwd[k][0].start()
        for k in range(3):
            if self.split:
                fwd[k][1].wait_recv()
                fwd[k][0].wait_send()
            send[k][0].wait_send()
        local.wait()


class ReduceOp:
    nsem = 7

    def __init__(self, plan, ksrc, kdst, l, shard_shape, rows, r0, nr, whole=False, half=False):
        self.ksrc, self.kdst, self.l, (self.R, self.C), self.rows, self.r0, self.nr, self.whole, self.half = (
            ksrc, kdst, l, shard_shape[-2:], rows, r0, nr, whole, half)
        plan.add(self)

    def _piece(self, R_, q):
        g = R_[self.ksrc]
        if self.whole:
            return g
        if self.rows:
            return g.at[_rows(q * (self.R // 2 if self.half else self.R) + self.r0, self.nr), :]
        return g.at[_rows(self.r0, self.nr), _cols(q, self.C)]

    def _slot(self, R_, q, c):
        if self.whole:
            return R_[self.kdst].at[2 * q + c]
        if self.half:
            return R_[self.kdst].at[q, self.l, _rows(c * (self.R // 2) + self.r0, self.nr), :]
        return R_[self.kdst].at[2 * q + c, self.l, _rows(self.r0, self.nr), :]

    def _copies(self, R_, sems):
        x, y, c, me, others = _coords()
        local = pltpu.make_async_copy(self._piece(R_, me), self._slot(R_, me, c), sems.at[self.base + 6])
        send = []
        for k, (px, py) in enumerate(others):
            q = 2 * px + py
            send.append((
                pltpu.make_async_remote_copy(self._piece(R_, q), self._slot(R_, me, c), sems.at[self.base + k],
                                             sems.at[self.base + 3 + k], device_id=(px, py, c), device_id_type=MESH),
                pltpu.make_async_remote_copy(self._piece(R_, q), self._slot(R_, q, c), sems.at[self.base + k],
                                             sems.at[self.base + 3 + k], device_id=(px, py, c), device_id_type=MESH)))
        return local, send

    def start(self, R_, sems):
        local, send = self._copies(R_, sems)
        local.start()
        for out, _ in send:
            out.start()

    def finish(self, R_, sems):
        local, send = self._copies(R_, sems)
        local.wait()
        for out, inn in send:
            inn.wait_recv()
            out.wait_send()


class ForwardOp:
    nsem = 8

    def __init__(self, plan, kdst, l, whole=False):
        self.kdst, self.l, self.whole = kdst, l, whole
        plan.add(self)

    def _slot(self, R_, s):
        return R_[self.kdst].at[s] if self.whole else R_[self.kdst].at[s, self.l]

    def _copies(self, R_, sems):
        x, y, c, me, others = _coords()
        return [(pltpu.make_async_remote_copy(self._slot(R_, 2 * q + c), self._slot(R_, 2 * q + c), sems.at[self.base + q],
                                              sems.at[self.base + 4 + q], device_id=(x, y, 1 - c), device_id_type=MESH),
                 pltpu.make_async_remote_copy(self._slot(R_, 2 * q + 1 - c), self._slot(R_, 2 * q + 1 - c), sems.at[self.base + q],
                                              sems.at[self.base + 4 + q], device_id=(x, y, 1 - c), device_id_type=MESH))
                for q in range(4)]

    def start(self, R_, sems):
        for out, _ in self._copies(R_, sems):
            out.start()

    def finish(self, R_, sems):
        for out, inn in self._copies(R_, sems):
            inn.wait_recv()
            out.wait_send()


class PairOp:
    nsem = 8

    def __init__(self, plan, ksrc, kdst, shard_shape, rows):
        self.ksrc, self.kdst, (self.R, self.C), self.rows = ksrc, kdst, shard_shape[-2:], rows
        plan.add(self)

    def _copies(self, R_, sems):
        x, y, c, me, others = _coords()
        g, dst, h = R_[self.ksrc], R_[self.kdst], self.R // 2
        out = []
        for q in range(4 if self.rows else 1):
            src = g.at[_rows(q * self.R + (1 - c) * h, h), :]
            land = dst.at[_rows(q * h, h), :]
            out.append(pltpu.make_async_remote_copy(src, land, sems.at[self.base + q], sems.at[self.base + 4 + q],
                                                    device_id=(x, y, 1 - c), device_id_type=MESH))
        return out

    def start(self, R_, sems):
        for cp in self._copies(R_, sems):
            cp.start()

    def finish(self, R_, sems):
        for cp in self._copies(R_, sems):
            cp.wait_recv()
            cp.wait_send()


class HalfForwardOp:
    nsem = 2

    def __init__(self, plan, kdst, l, shard_shape):
        self.kdst, self.l, self.R = kdst, l, shard_shape[-2]
        plan.add(self)

    def _copy(self, R_, sems, core):
        x, y, c, me, others = _coords()
        part = R_[self.kdst].at[:, self.l, _rows((c if core == "mine" else 1 - c) * (self.R // 2), self.R // 2), :]
        return pltpu.make_async_remote_copy(part, part, sems.at[self.base], sems.at[self.base + 1],
                                            device_id=(x, y, 1 - c), device_id_type=MESH)

    def start(self, R_, sems):
        self._copy(R_, sems, "mine").start()

    def finish(self, R_, sems):
        self._copy(R_, sems, "theirs").wait_recv()
        self._copy(R_, sems, "mine").wait_send()


def pair_sum(g, gsib, rows, R, name):
    h = R // 2
    W = g.shape[1]
    tr = h if h * W * 2 <= 2 ** 21 else 128
    nq = 4 if rows else 1

    def body(c_ref, a_ref, b_ref, o_ref):
        o_ref[...] = (a_ref[...].astype(f32) + b_ref[...].astype(f32)).astype(o_ref.dtype)

    half = pl.BlockSpec((tr, W), lambda q, i, c_ref: (q * (h // tr) + i, 0))
    mine = pl.BlockSpec((tr, W), lambda q, i, c_ref: (q * (R // tr) + c_ref[0] * (h // tr) + i, 0))
    return pl.pallas_call(
        body, grid_spec=pltpu.PrefetchScalarGridSpec(num_scalar_prefetch=1, grid=(nq, h // tr), in_specs=[mine, half],
                                                     out_specs=half),
        out_shape=S(gsib.shape, g.dtype), compiler_params=_cp(("parallel", "parallel")),
        name=name)(lax.axis_index("c").reshape(1).astype(jnp.int32), g, gsib)


def pcall(body, plan, *, grid, in_specs, out_specs, out_shape, scratch_shapes=(), sem, name, args):
    multi = isinstance(out_shape, (list, tuple))
    if plan is None or not plan.ops:
        return pl.pallas_call(body, grid=grid, in_specs=in_specs, out_specs=out_specs, out_shape=out_shape,
                              scratch_shapes=list(scratch_shapes), compiler_params=_cp(sem), name=name)(*args)
    outs = list(out_shape) if multi else [out_shape]
    ospecs = list(out_specs) if multi else [out_specs]
    kin = [k for k, b in plan.bufs.items() if b["arr"] is not None]
    kout = [k for k, b in plan.bufs.items() if b["write"]]
    n_in, n_out, n_scr = len(in_specs), len(outs), len(scratch_shapes)

    def wrapped(*refs):
        o0 = n_in + len(kin)
        s0 = o0 + n_out + len(kout)
        R_ = dict(zip(kin, refs[n_in:o0]))
        R_.update(zip(kout, refs[o0 + n_out:s0]))
        sems = refs[s0 + n_scr]
        first = functools.reduce(jnp.logical_and, [pl.program_id(d) == 0 for d in range(len(grid))])
        last = functools.reduce(jnp.logical_and, [pl.program_id(d) == grid[d] - 1 for d in range(len(grid))])

        @pl.when(first)
        def _():
            for op in plan.ops:
                op.start(R_, sems)

        body(*refs[:n_in], *refs[o0:o0 + n_out], *refs[s0:s0 + n_scr])

        @pl.when(last)
        def _():
            for op in plan.ops:
                op.finish(R_, sems)

    def shape_of(k):
        b = plan.bufs[k]
        return S(b["arr"].shape, b["arr"].dtype) if b["arr"] is not None else b["shape"]

    res = pl.pallas_call(
        wrapped, grid=grid, in_specs=list(in_specs) + [ANY] * len(kin), out_specs=ospecs + [ANY] * len(kout),
        out_shape=outs + [shape_of(k) for k in kout],
        scratch_shapes=list(scratch_shapes) + [pltpu.SemaphoreType.DMA((plan.nsem,))],
        input_output_aliases={n_in + kin.index(k): n_out + kout.index(k) for k in kout if plan.bufs[k]["arr"] is not None},
        compiler_params=pltpu.CompilerParams(dimension_semantics=("arbitrary",) * len(grid), vmem_limit_bytes=VMEM_LIMIT,
                                             has_side_effects=True),
        name=name)(*args, *[plan.bufs[k]["arr"] for k in kin])
    plan.out = dict(zip(kout, res[n_out:]))
    return list(res[:n_out]) if multi else res[0]


def _dg(a, b, ca, cb):
    return lax.dot_general(a.astype(BF), b.astype(BF), (((ca,), (cb,)), ((), ())), preferred_element_type=f32)


@jax.custom_vjp
def dot_nn(a, b):
    return _dg(a, b, 1, 0)


@jax.custom_vjp
def dot_nt(a, b):
    return _dg(a, b, 1, 1)


@jax.custom_vjp
def dot_tn(a, b):
    return _dg(a, b, 0, 0)


dot_nn.defvjp(lambda a, b: (dot_nn(a, b), (a, b)),
              lambda r, g: (dot_nt(g, r[1]).astype(r[0].dtype), dot_tn(r[0], g).astype(r[1].dtype)))
dot_nt.defvjp(lambda a, b: (dot_nt(a, b), (a, b)),
              lambda r, g: (dot_nn(g, r[1]).astype(r[0].dtype), dot_tn(g, r[0]).astype(r[1].dtype)))
dot_tn.defvjp(lambda a, b: (dot_tn(a, b), (a, b)),
              lambda r, g: (dot_nt(r[1], g).astype(r[0].dtype), dot_nn(r[0], g).astype(r[1].dtype)))


def matmul(a, b, *, mode, tm, tn, tk, out_dtype=f32, add=None, b_lead=None, a_spec=None, b_spec=None, dims=None, plan=None, name):
    a_over, b_over = a_spec, b_spec
    if mode == "nn":
        (M, K), N = a.shape[-2:], b.shape[-1]
        a_spec = pl.BlockSpec((tm, tk), lambda i, j, k: (i, k))
        b_blk, b_idx, ca, cb = (tk, tn), (lambda i, j, k: (k, j)), 1, 0
    elif mode == "nt":
        (M, K), N = a.shape[-2:], b.shape[-2]
        a_spec = pl.BlockSpec((tm, tk), lambda i, j, k: (i, k))
        b_blk, b_idx, ca, cb = (tn, tk), (lambda i, j, k: (j, k)), 1, 1
    else:
        (K, M), N = a.shape[-2:], b.shape[-1]
        a_spec = pl.BlockSpec((tk, tm), lambda i, j, k: (k, i))
        b_blk, b_idx, ca, cb = (tk, tn), (lambda i, j, k: (k, j)), 0, 0
    if dims is not None:
        M, N, K = dims
    assert M % tm == 0 and N % tn == 0 and K % tk == 0, (name, M, N, K, tm, tn, tk)
    if b_lead is None:
        b_spec = pl.BlockSpec(b_blk, b_idx)
    else:
        b_spec = pl.BlockSpec((None,) + b_blk, lambda i, j, k: (b_lead,) + b_idx(i, j, k))
    if a_over is not None:
        a_spec = a_over
    if b_over is not None:
        b_spec = b_over
    nk = K // tk
    has_add = add is not None

    def body(*refs):
        a_ref, b_ref = refs[0], refs[1]
        add_ref = refs[2] if has_add else None
        o_ref = refs[2 + has_add]
        p = _dg(a_ref[...], b_ref[...], ca, cb)

        def fin(v):
            if has_add:
                v = v + add_ref[...].astype(f32)
            o_ref[...] = v.astype(o_ref.dtype)

        if nk == 1:
            fin(p)
        else:
            acc = refs[3 + has_add]
            k = pl.program_id(2)

            @pl.when(k == 0)
            def _():
                acc[...] = p

            @pl.when(k > 0)
            def _():
                acc[...] += p

            @pl.when(k == nk - 1)
            def _():
                fin(acc[...])

    in_specs = [a_spec, b_spec]
    args = [a, b]
    if has_add:
        in_specs.append(pl.BlockSpec((tm, tn), lambda i, j, k: (i, j)))
        args.append(add)
    return pcall(body, plan, grid=(M // tm, N // tn, nk), in_specs=in_specs,
                 out_specs=pl.BlockSpec((tm, tn), lambda i, j, k: (i, j)), out_shape=S((M, N), out_dtype),
                 scratch_shapes=[pltpu.VMEM((tm, tn), f32)] if nk > 1 else [],
                 sem=("parallel", "parallel", "arbitrary"), name=name, args=args)


def _rms(xv, gv):
    return xv * lax.rsqrt(jnp.mean(xv * xv, axis=-1, keepdims=True) + EPS) * gv


TR = 256


def rms_fwd(x, g, name):
    def body(x_ref, g_ref, o_ref):
        o_ref[...] = _rms(x_ref[...], g_ref[...]).astype(o_ref.dtype)

    return pl.pallas_call(
        body, grid=(L // TR,),
        in_specs=[pl.BlockSpec((TR, D), lambda i: (i, 0)), pl.BlockSpec((1, D), lambda i: (0, 0))],
        out_specs=pl.BlockSpec((TR, D), lambda i: (i, 0)), out_shape=S((L, D), BF),
        compiler_params=_cp(("parallel",)), name=name)(x, g)


def rms_bwd(x, g, dys, dres, name, plan=None):
    nd = len(dys)

    def body(*refs):
        x_ref, g_ref = refs[0], refs[1]
        dr_ref, dh_ref, dg_ref = refs[2 + nd:]
        dy = refs[2][...].astype(f32)
        for r in refs[3:2 + nd]:
            dy = dy + r[...].astype(f32)
        _, vjp = jax.vjp(_rms, x_ref[...], g_ref[...])
        dx, dg = vjp(dy)
        dh_ref[...] = dr_ref[...] + dx

        @pl.when(pl.program_id(0) == 0)
        def _():
            dg_ref[...] = jnp.zeros_like(dg_ref)

        dg_ref[...] += dg

    row = pl.BlockSpec((TR, D), lambda i: (i, 0))
    vec = pl.BlockSpec((1, D), lambda i: (0, 0))
    return pcall(body, plan, grid=(L // TR,), in_specs=[row, vec] + [row] * (nd + 1), out_specs=[row, vec],
                 out_shape=[S((L, D), f32), S((1, D), f32)], sem=("arbitrary",), name=name, args=[x, g, *dys, dres])


def loss_head(h, g, tgt):
    def f(hv, gv, tv):
        y = _rms(hv, gv)
        return 0.5 * jnp.sum(jnp.mean(jnp.square(y - tv), axis=-1))

    def body(h_ref, g_ref, t_ref, l_ref, dh_ref, dg_ref):
        val, vjp = jax.vjp(f, h_ref[...], g_ref[...], t_ref[...])
        dh, dg, _ = vjp(jnp.ones((), f32))
        dh_ref[...] = dh

        @pl.when(pl.program_id(0) == 0)
        def _():
            dg_ref[...] = jnp.zeros_like(dg_ref)
            l_ref[...] = jnp.zeros_like(l_ref)

        dg_ref[...] += dg
        l_ref[...] += jnp.full((1, 128), val, f32)

    row = pl.BlockSpec((TR, D), lambda i: (i, 0))
    vec = pl.BlockSpec((1, D), lambda i: (0, 0))
    return pl.pallas_call(
        body, grid=(L // TR,), in_specs=[row, vec, row],
        out_specs=[pl.BlockSpec((1, 128), lambda i: (0, 0)), row, vec],
        out_shape=[S((1, 128), f32), S((L, D), f32), S((1, D), f32)],
        compiler_params=_cp(("arbitrary",)), name="loss_head")(h, g, tgt)


def _col_to_row(c):
    n = c.shape[0]
    t = jnp.broadcast_to(c, (n, 128)).T
    r = lax.broadcasted_iota(jnp.int32, (128, n), 0)
    return jnp.sum(jnp.where(r == 0, t, 0.0), axis=0, keepdims=True)


def _s5_param_map(are, aim, ldt_row, bre, bim, cre, cim):
    n = NST
    gi = lax.broadcasted_iota(jnp.int32, (n, 32), 0) // 64
    gj = lax.broadcasted_iota(jnp.int32, (n, 32), 1)
    ldt = jnp.sum(jnp.where(gi == gj, ldt_row, 0.0), axis=1, keepdims=True)
    dt = jnp.exp(ldt)
    mag = jnp.exp(are * dt)
    abr = mag * jnp.cos(aim * dt)
    abi = mag * jnp.sin(aim * dt)
    den = are * are + aim * aim
    nr, ni = abr - 1.0, abi
    cr = (nr * are + ni * aim) / den
    ci = (ni * are - nr * aim) / den
    bbr = cr * bre - ci * bim
    bbi = cr * bim + ci * bre
    tc = lax.broadcasted_iota(jnp.int32, (16, 128), 0)
    tl = lax.broadcasted_iota(jnp.int32, (16, 128), 1)
    T = (tl % 16 == tc).astype(f32)
    mr = (lax.broadcasted_iota(jnp.int32, (n, 128), 0) // 64) % 8
    mc = lax.broadcasted_iota(jnp.int32, (n, 128), 1) // 16
    mask = (mr == mc).astype(f32)

    def expand(v):
        return jnp.dot(v, T, precision=HI, preferred_element_type=f32) * mask

    return expand(bbr), expand(bbi), expand(cre), expand(cim), _col_to_row(abr), _col_to_row(abi)


def s5_params_fwd(are, aim, ldt_row, bre, bim, cre, cim):
    def body(*refs):
        outs = _s5_param_map(*[r[...] for r in refs[:7]])
        for o_ref, o in zip(refs[7:], outs):
            o_ref[...] = o

    return pl.pallas_call(
        body, out_shape=[S((NST, 128), f32)] * 4 + [S((1, NST), f32)] * 2,
        compiler_params=_cp(), name="s5_params_fwd")(are, aim, ldt_row, bre, bim, cre, cim)


def s5_params_bwd(are, aim, ldt_row, bre, bim, cre, cim, cots):
    def body(*refs):
        _, vjp = jax.vjp(_s5_param_map, *[r[...] for r in refs[:7]])
        gs = vjp(tuple(r[...] for r in refs[7:13]))
        for o_ref, o in zip(refs[13:], gs):
            o_ref[...] = o

    return pl.pallas_call(
        body, out_shape=[S((NST, 1), f32)] * 2 + [S((1, 32), f32)] + [S((NST, 16), f32)] * 4,
        compiler_params=_cp(), name="s5_params_bwd")(are, aim, ldt_row, bre, bim, cre, cim, *cots)


def _cpowers(ar, ai):
    out = [(ar, ai)]
    for _ in range(7):
        pr, pi = out[-1]
        out.append((pr * ar - pi * ai, pr * ai + pi * ar))
    return out


def _ctable(pw, rid, power):
    tr_ = jnp.zeros(rid.shape, f32)
    ti_ = jnp.zeros(rid.shape, f32)
    for r in range(8):
        pr, pi = pw[power(r) - 1]
        tr_ = jnp.where(rid == r, pr, tr_)
        ti_ = jnp.where(rid == r, pi, ti_)
    return tr_, ti_


NT5 = 4
RC = 256


def s5_scan_fwd(proj, wbr, wbi, wcr, wci, abr, abi, drow, plan=None):
    def body(u_ref, wbr_ref, wbi_ref, wcr_ref, wci_ref, ar_ref, ai_ref, d_ref, xr_ref, xi_ref, y_ref):
        wbr_v, wbi_v = wbr_ref[...], wbi_ref[...]
        for r in range(L // RC):
            rows = pl.ds(r * RC, RC)
            ub = u_ref[rows, :]
            xr_ref[rows, :] = dot_nt(ub, wbr_v)
            xi_ref[rows, :] = dot_nt(ub, wbi_v)
        pw = _cpowers(ar_ref[...], ai_ref[...])
        rid = lax.broadcasted_iota(jnp.int32, (8, 512), 0)
        tr_, ti_ = _ctable(pw, rid, lambda r: r + 1)

        def group(j, c):
            cr, ci = c
            rows = pl.ds(pl.multiple_of(j * 8, 8), 8)
            br, bi = xr_ref[rows, :], xi_ref[rows, :]
            for s in (1, 2, 4):
                pr, pi = pw[s - 1]
                sr = jnp.where(rid >= s, pltpu.roll(br, s, 0), 0.0)
                si = jnp.where(rid >= s, pltpu.roll(bi, s, 0), 0.0)
                br, bi = br + pr * sr - pi * si, bi + pr * si + pi * sr
            br, bi = br + tr_ * cr - ti_ * ci, bi + tr_ * ci + ti_ * cr
            xr_ref[rows, :] = br
            xi_ref[rows, :] = bi
            return br[7:8], bi[7:8]

        z = jnp.zeros((1, 512), f32)
        lax.fori_loop(0, L // 8, group, (z, z), unroll=2)
        wcr_v, wci_v, dv = wcr_ref[...], wci_ref[...], d_ref[...]
        for r in range(L // RC):
            rows = pl.ds(r * RC, RC)
            y_ref[rows, :] = (dot_nn(xr_ref[rows, :], wcr_v) - dot_nn(xi_ref[rows, :], wci_v)
                              + dv * u_ref[rows, :])

    wspec = pl.BlockSpec((512, 128), lambda j: (j, 0))
    aspec = pl.BlockSpec((1, 512), lambda j: (0, j))
    return pcall(
        body, plan, grid=(NT5,),
        in_specs=[pl.BlockSpec((L, 128), lambda j: (0, j)), wspec, wspec, wspec, wspec, aspec, aspec,
                  pl.BlockSpec((1, 128), lambda j: (0, j))],
        out_specs=[pl.BlockSpec((L, 512), lambda j: (0, j)), pl.BlockSpec((L, 512), lambda j: (0, j)),
                   pl.BlockSpec((L, 128), lambda j: (0, j))],
        out_shape=[S((L, NST), f32), S((L, NST), f32), S((L, S5W), f32)],
        sem=("parallel",), name="s5_scan_fwd", args=[proj, wbr, wbi, wcr, wci, abr, abi, drow])


def s5_scan_bwd(dy, proj, xs_re, xs_im, wbr, wbi, wcr, wci, abr, abi, drow, plan=None):
    def body(dy_ref, u_ref, xr_ref, xi_ref, wbr_ref, wbi_ref, wcr_ref, wci_ref, ar_ref, ai_ref, d_ref,
             du_ref, gwbr_ref, gwbi_ref, gwcr_ref, gwci_ref, gar_ref, gai_ref, gd_ref, lr_ref, li_ref):
        wcr_v, wci_v = wcr_ref[...], wci_ref[...]
        gwcr = jnp.zeros((512, 128), f32)
        gwci = jnp.zeros((512, 128), f32)
        gd = jnp.zeros((1, 128), f32)
        for r in range(L // RC):
            rows = pl.ds(r * RC, RC)
            dyv = dy_ref[rows, :]
            lr_ref[rows, :] = dot_nt(dyv, wcr_v)
            li_ref[rows, :] = -dot_nt(dyv, wci_v)
            gwcr += dot_tn(xr_ref[rows, :], dyv)
            gwci -= dot_tn(xi_ref[rows, :], dyv)
            gd += jnp.sum(dyv * u_ref[rows, :], axis=0, keepdims=True)
        gwcr_ref[...] = gwcr
        gwci_ref[...] = gwci
        gd_ref[...] = gd
        pw = _cpowers(ar_ref[...], -ai_ref[...])
        rid = lax.broadcasted_iota(jnp.int32, (8, 512), 0)
        tr_, ti_ = _ctable(pw, rid, lambda r: 8 - r)

        def group(i, c):
            cr, ci, gar, gai = c
            j = L // 8 - 1 - i
            rows = pl.ds(pl.multiple_of(j * 8, 8), 8)
            br, bi = lr_ref[rows, :], li_ref[rows, :]
            for s in (1, 2, 4):
                pr, pi = pw[s - 1]
                sr = jnp.where(rid < 8 - s, pltpu.roll(br, 8 - s, 0), 0.0)
                si = jnp.where(rid < 8 - s, pltpu.roll(bi, 8 - s, 0), 0.0)
                br, bi = br + pr * sr - pi * si, bi + pr * si + pi * sr
            br, bi = br + tr_ * cr - ti_ * ci, bi + tr_ * ci + ti_ * cr
            lr_ref[rows, :] = br
            li_ref[rows, :] = bi
            nr = jnp.where(rid < 7, pltpu.roll(br, 7, 0), cr)
            ni = jnp.where(rid < 7, pltpu.roll(bi, 7, 0), ci)
            xr, xi = xr_ref[rows, :], xi_ref[rows, :]
            return br[0:1], bi[0:1], gar + xr * nr + xi * ni, gai + xr * ni - xi * nr

        z = jnp.zeros((1, 512), f32)
        z8 = jnp.zeros((8, 512), f32)
        _, _, gar, gai = lax.fori_loop(0, L // 8, group, (z, z, z8, z8), unroll=2)
        gar_ref[...] = jnp.sum(gar, axis=0, keepdims=True)
        gai_ref[...] = jnp.sum(gai, axis=0, keepdims=True)
        wbr_v, wbi_v, dv = wbr_ref[...], wbi_ref[...], d_ref[...]
        gwbr = jnp.zeros((512, 128), f32)
        gwbi = jnp.zeros((512, 128), f32)
        for r in range(L // RC):
            rows = pl.ds(r * RC, RC)
            lrv, liv, uv = lr_ref[rows, :], li_ref[rows, :], u_ref[rows, :]
            du_ref[rows, :] = (dot_nn(lrv, wbr_v) + dot_nn(liv, wbi_v) + dv * dy_ref[rows, :]).astype(du_ref.dtype)
            gwbr += dot_tn(lrv, uv)
            gwbi += dot_tn(liv, uv)
        gwbr_ref[...] = gwbr
        gwbi_ref[...] = gwbi

    wspec = pl.BlockSpec((512, 128), lambda j: (j, 0))
    aspec = pl.BlockSpec((1, 512), lambda j: (0, j))
    col = pl.BlockSpec((L, 128), lambda j: (0, j))
    st = pl.BlockSpec((L, 512), lambda j: (0, j))
    dspec = pl.BlockSpec((1, 128), lambda j: (0, j))
    return pcall(
        body, plan, grid=(NT5,),
        in_specs=[col, col, st, st, wspec, wspec, wspec, wspec, aspec, aspec, dspec],
        out_specs=[col, wspec, wspec, wspec, wspec, aspec, aspec, dspec],
        out_shape=[S((L, S5W), BF)] + [S((NST, 128), f32)] * 4 + [S((1, NST), f32)] * 2 + [S((1, S5W), f32)],
        scratch_shapes=[pltpu.VMEM((L, 512), f32), pltpu.VMEM((L, 512), f32)],
        sem=("parallel",), name="s5_scan_bwd", args=[dy, proj, xs_re, xs_im, wbr, wbi, wcr, wci, abr, abi, drow])


def _glu(y, w, b):
    z = jax.nn.gelu(y)
    return z * jax.nn.sigmoid(dot_nn(z, w) + b)


def s5_glu_fwd(y, w, b):
    def body(y_ref, w_ref, b_ref, o_ref):
        o_ref[...] = _glu(y_ref[...], w_ref[...], b_ref[...]).astype(o_ref.dtype)

    return pl.pallas_call(
        body, grid=(L // TR,),
        in_specs=[pl.BlockSpec((TR, S5W), lambda i: (i, 0)), pl.BlockSpec((S5W, S5W), lambda i: (0, 0)),
                  pl.BlockSpec((1, S5W), lambda i: (0, 0))],
        out_specs=pl.BlockSpec((TR, S5W), lambda i: (i, 0)), out_shape=S((L, S5W), BF),
        compiler_params=_cp(("parallel",)), name="s5_glu_fwd")(y, w, b)


def s5_glu_bwd(y, w, b, dmix):
    def body(y_ref, w_ref, b_ref, g_ref, dy_ref, dw_ref, db_ref):
        _, vjp = jax.vjp(_glu, y_ref[...], w_ref[...].astype(f32), b_ref[...])
        dy, dw, db = vjp(g_ref[...])
        dy_ref[...] = dy

        @pl.when(pl.program_id(0) == 0)
        def _():
            dw_ref[...] = jnp.zeros_like(dw_ref)
            db_ref[...] = jnp.zeros_like(db_ref)

        dw_ref[...] += dw
        db_ref[...] += db

    row = pl.BlockSpec((TR, S5W), lambda i: (i, 0))
    return pl.pallas_call(
        body, grid=(L // TR,),
        in_specs=[row, pl.BlockSpec((S5W, S5W), lambda i: (0, 0)), pl.BlockSpec((1, S5W), lambda i: (0, 0)), row],
        out_specs=[row, pl.BlockSpec((S5W, S5W), lambda i: (0, 0)), pl.BlockSpec((1, S5W), lambda i: (0, 0))],
        out_shape=[S((L, S5W), f32), S((S5W, S5W), f32), S((1, S5W), f32)],
        compiler_params=_cp(("arbitrary",)), name="s5_glu_bwd")(y, w, b, dmix)


def _dg3(a, b, ca, cb):
    ah, bh = a.astype(BF), b.astype(BF)
    al, bl = (a - ah.astype(f32)).astype(BF), (b - bh.astype(f32)).astype(BF)
    return _dg(ah, bh, ca, cb) + _dg(ah, bl, ca, cb) + _dg(al, bh, ca, cb)


@jax.custom_vjp
def hi_nn(a, b):
    return _dg3(a, b, 1, 0)


@jax.custom_vjp
def hi_nt(a, b):
    return _dg3(a, b, 1, 1)


@jax.custom_vjp
def hi_tn(a, b):
    return _dg3(a, b, 0, 0)


hi_nn.defvjp(lambda a, b: (hi_nn(a, b), (a, b)), lambda r, g: (hi_nt(g, r[1]), hi_tn(r[0], g)))
hi_nt.defvjp(lambda a, b: (hi_nt(a, b), (a, b)), lambda r, g: (hi_nn(g, r[1]), hi_tn(g, r[0])))
hi_tn.defvjp(lambda a, b: (hi_tn(a, b), (a, b)), lambda r, g: (hi_nt(r[1], g), hi_nn(r[0], g)))


def _hgrn_chunk(St, xq, xf, xi, xg, gam, ng):
    lb = jax.nn.sigmoid(gam[0:1] - gam[1:2])
    q = jax.nn.silu(xq)
    f = lb + (1.0 - lb) * jax.nn.sigmoid(xf)
    k = 1.0 - f
    g = jnp.log(f)
    ti = lax.broadcasted_iota(jnp.int32, (HGC, HGC), 0)
    si = lax.broadcasted_iota(jnp.int32, (HGC, HGC), 1)
    causal = si <= ti
    b = jnp.dot(causal.astype(f32), g, precision=HI, preferred_element_type=f32)
    qe = q * jnp.exp(b)
    o = dot_nt(qe, St)
    att = jnp.where(causal, hi_nt(qe, k * jnp.exp(-b)), 0.0)
    o = o + dot_nn(att, xi)
    bl = b[HGC - 1:HGC]
    St_new = St * jnp.exp(bl) + dot_tn(xi, k * jnp.exp(bl - b))
    o = o * lax.rsqrt(jnp.mean(o * o, axis=-1, keepdims=True) + EPS) * ng
    return St_new, o * jax.nn.silu(xg)


NCH = L // HGC


def hgrn_fwd(proj, gamma, hnorm, plan=None):
    def body(q_ref, f_ref, i_ref, g_ref, gam_ref, ng_ref, o_ref, ss_ref, st):
        @pl.when(pl.program_id(0) == 0)
        def _():
            st[...] = jnp.zeros_like(st)

        for h in range(4):
            sl = slice(h * 128, (h + 1) * 128)
            s0 = st[h]
            ss_ref[0, h] = s0
            s1, o = _hgrn_chunk(s0, q_ref[:, sl], f_ref[:, sl], i_ref[:, sl], g_ref[:, sl], gam_ref[:, sl], ng_ref[:, sl])
            st[h] = s1
            o_ref[:, sl] = o.astype(o_ref.dtype)

    def pj(n):
        return pl.BlockSpec((HGC, 512), lambda c: (c, n))

    return pcall(
        body, plan, grid=(NCH,),
        in_specs=[pj(1), pj(2), pj(3), pj(4), pl.BlockSpec((2, 512), lambda c: (0, 0)), pl.BlockSpec((1, 512), lambda c: (0, 0))],
        out_specs=[pl.BlockSpec((HGC, 512), lambda c: (c, 0)), pl.BlockSpec((1, 4, 128, 128), lambda c: (c, 0, 0, 0))],
        out_shape=[S((L, 512), BF), S((NCH, 4, 128, 128), f32)],
        scratch_shapes=[pltpu.VMEM((4, 128, 128), f32)],
        sem=("arbitrary",), name="hgrn_fwd", args=[proj, proj, proj, proj, gamma, hnorm])


def hgrn_bwd(proj, gamma, hnorm, ssave, dmix, du, plan=None):
    def body(q_ref, f_ref, i_ref, g_ref, gam_ref, ng_ref, ss_ref, do_ref, du_ref, dp_ref, dgam_ref, dng_ref, dst):
        @pl.when(pl.program_id(0) == 0)
        def _():
            dst[...] = jnp.zeros_like(dst)
            dgam_ref[...] = jnp.zeros_like(dgam_ref)
            dng_ref[...] = jnp.zeros_like(dng_ref)

        dp_ref[:, 0:512] = du_ref[...]
        for h in range(4):
            sl = slice(h * 128, (h + 1) * 128)
            _, vjp = jax.vjp(_hgrn_chunk, ss_ref[0, h], q_ref[:, sl], f_ref[:, sl], i_ref[:, sl], g_ref[:, sl],
                             gam_ref[:, sl], ng_ref[:, sl])
            ds, dq, df, di, dg, dgam, dng = vjp((dst[h], do_ref[:, sl]))
            dst[h] = ds
            for n, v in enumerate((dq, df, di, dg)):
                dp_ref[:, 512 * (n + 1) + h * 128: 512 * (n + 1) + (h + 1) * 128] = v.astype(dp_ref.dtype)
            dgam_ref[:, sl] += dgam
            dng_ref[:, sl] += dng

    def pj(n):
        return pl.BlockSpec((HGC, 512), lambda i: (NCH - 1 - i, n))

    return pcall(
        body, plan, grid=(NCH,),
        in_specs=[pj(1), pj(2), pj(3), pj(4), pl.BlockSpec((2, 512), lambda i: (0, 0)), pl.BlockSpec((1, 512), lambda i: (0, 0)),
                  pl.BlockSpec((1, 4, 128, 128), lambda i: (NCH - 1 - i, 0, 0, 0)), pj(1), pj(0)],
        out_specs=[pl.BlockSpec((HGC, 2560), lambda i: (NCH - 1 - i, 0)), pl.BlockSpec((2, 512), lambda i: (0, 0)),
                   pl.BlockSpec((1, 512), lambda i: (0, 0))],
        out_shape=[S((L, 2560), BF), S((2, 512), f32), S((1, 512), f32)],
        scratch_shapes=[pltpu.VMEM((4, 128, 128), f32)],
        sem=("arbitrary",), name="hgrn_bwd", args=[proj, proj, proj, proj, gamma, hnorm, ssave, dmix, du])


def _earlier(h_ref, k, r0, n):
    if r0 > 0:
        return h_ref[pl.ds(r0 - k, n), :]
    rid = lax.broadcasted_iota(jnp.int32, (8, h_ref.shape[1]), 0)
    head = jnp.where(rid >= k, pltpu.roll(h_ref[pl.ds(0, 8), :], k, 0), 0.0)
    return jnp.concatenate([head, h_ref[pl.ds(8 - k, n - 8), :]], axis=0)


def _conv3_rows(h_ref, w, b, r0, n=None):
    n = CR if n is None else n
    h1, h2 = _earlier(h_ref, 1, r0, n), _earlier(h_ref, 2, r0, n)
    return w[2:3] * h_ref[pl.ds(r0, n), :] + w[1:2] * h1 + w[0:1] * h2 + b, h1, h2


CT = 128
NCT = DFF // CT
CR = 64


def convact_fwd(hu, cw, cb, layer, plan=None):
    def body(ha_ref, hb_ref, wa_ref, wb_ref, ba_ref, bb_ref, o_ref):
        ca = _conv3_rows(ha_ref, wa_ref[...], ba_ref[...], 0, L)[0]
        cb_ = _conv3_rows(hb_ref, wb_ref[...], bb_ref[...], 0, L)[0]
        o_ref[...] = (jax.nn.silu(ca) * cb_).astype(o_ref.dtype)

    def h(off):
        return pl.BlockSpec((L, CT), lambda j: (0, j + off))

    def w(off):
        return pl.BlockSpec((3, CT), lambda j: (0, j + off))

    def b(off):
        return pl.BlockSpec((None, 1, CT), lambda j: (layer, 0, j + off))

    return pcall(body, plan, grid=(NCT,), in_specs=[h(0), h(NCT), w(0), w(NCT), b(0), b(NCT)],
                 out_specs=pl.BlockSpec((L, CT), lambda j: (0, j)), out_shape=S((L, DFF), BF),
                 sem=("parallel",), name=f"convact_fwd{layer}", args=[hu, hu, cw, cw, cb, cb])


def convact_bwd(hu, cw, cb, dact, layer, plan=None):
    def body(ha_ref, hb_ref, wa_ref, wb_ref, ba_ref, bb_ref, g_ref, dh_ref, dw_ref, db_ref, sh, sw, sb, da_scr, db_scr):
        j = pl.program_id(0)

        def fold(x):
            return functools.reduce(jnp.add, [x[8 * m:8 * m + 8] for m in range(CR // 8)])

        @pl.when(j < NCT)
        def _():
            wa, wb, ba, bb = wa_ref[...], wb_ref[...], ba_ref[...], bb_ref[...]
            da_scr[pl.ds(L, 8), :] = jnp.zeros((8, CT), f32)
            db_scr[pl.ds(L, 8), :] = jnp.zeros((8, CT), f32)
            acc = [jnp.zeros((8, CT), f32) for _ in range(8)]
            for c in range(L // CR):
                r0 = c * CR
                ca, a1, a2 = _conv3_rows(ha_ref, wa, ba, r0)
                cb_, b1, b2 = _conv3_rows(hb_ref, wb, bb, r0)
                g = g_ref[pl.ds(r0, CR), :].astype(f32)
                sg = jax.nn.sigmoid(ca)
                dca = g * cb_ * (sg * (1.0 + ca * (1.0 - sg)))
                dcb = g * (ca * sg)
                da_scr[pl.ds(r0, CR), :] = dca
                db_scr[pl.ds(r0, CR), :] = dcb
                terms = (dca * a2, dca * a1, dca * ha_ref[pl.ds(r0, CR), :], dca,
                         dcb * b2, dcb * b1, dcb * hb_ref[pl.ds(r0, CR), :], dcb)
                acc = [a + fold(t) for a, t in zip(acc, terms)]
            rows = [jnp.sum(a, axis=0, keepdims=True) for a in acc]
            for k in range(3):
                dw_ref[k:k + 1, :] = rows[k]
                sw[j, k:k + 1, :] = rows[4 + k]
            db_ref[...] = rows[3]
            sb[j] = rows[7]
            for c in range(L // CR):
                r0 = c * CR
                for scr, w, out in ((da_scr, wa, dh_ref), (db_scr, wb, sh.at[j])):
                    dh = (w[2:3] * scr[pl.ds(r0, CR), :] + w[1:2] * scr[pl.ds(r0 + 1, CR), :]
                          + w[0:1] * scr[pl.ds(r0 + 2, CR), :])
                    out[pl.ds(r0, CR), :] = dh.astype(out.dtype)

        @pl.when(j >= NCT)
        def _():
            dh_ref[...] = sh[j - NCT]
            dw_ref[...] = sw[j - NCT]
            db_ref[...] = sb[j - NCT]

    def lo(j):
        return jnp.minimum(j, NCT - 1)

    in_specs = [pl.BlockSpec((L, CT), lambda j: (0, lo(j))), pl.BlockSpec((L, CT), lambda j: (0, lo(j) + NCT)),
                pl.BlockSpec((3, CT), lambda j: (0, lo(j))), pl.BlockSpec((3, CT), lambda j: (0, lo(j) + NCT)),
                pl.BlockSpec((None, 1, CT), lambda j: (layer, 0, lo(j))), pl.BlockSpec((None, 1, CT), lambda j: (layer, 0, lo(j) + NCT)),
                pl.BlockSpec((L, CT), lambda j: (0, lo(j)))]
    return pcall(
        body, plan, grid=(2 * NCT,), in_specs=in_specs,
        out_specs=[pl.BlockSpec((L, CT), lambda j: (0, j)), pl.BlockSpec((3, CT), lambda j: (0, j)), pl.BlockSpec((1, CT), lambda j: (0, j))],
        out_shape=[S((L, 2 * DFF), BF), S((3, 2 * DFF), f32), S((1, 2 * DFF), f32)],
        scratch_shapes=[pltpu.VMEM((NCT, L, CT), BF), pltpu.VMEM((NCT, 3, CT), f32), pltpu.VMEM((NCT, 1, CT), f32),
                        pltpu.VMEM((L + 8, CT), f32), pltpu.VMEM((L + 8, CT), f32)],
        sem=("arbitrary",), name=f"convact_bwd{layer}", args=[hu, hu, cw, cw, cb, cb, dact])


DILS = (1, 4, 16)
AB = 128
NPAIR = 12


def _rope_tables(pos_ref, invf_ref):
    ang = pos_ref[...].astype(f32) * invf_ref[...]
    lane = lax.broadcasted_iota(jnp.int32, (1, 128), 1) % 64
    cosf = jnp.where(lane < 16, jnp.cos(ang), 1.0)
    sn = jnp.sin(ang)
    s_lo = jnp.where(lane < 8, -sn, 0.0)
    s_hi = jnp.where((lane >= 8) & (lane < 16), sn, 0.0)
    return cosf, s_lo, s_hi


def _rope(t, cosf, s_lo, s_hi):
    return t * cosf + pltpu.roll(t, 120, 1) * s_lo + pltpu.roll(t, 8, 1) * s_hi


def _rope_t(g, cosf, s_lo, s_hi):
    return g * cosf + pltpu.roll(g * s_lo, 8, 1) + pltpu.roll(g * s_hi, 120, 1)


def _att_block(q2, kp, kc, vp, vc, first):
    lane = lax.broadcasted_iota(jnp.int32, (1, 128), 1)
    qi = lax.broadcasted_iota(jnp.int32, (AB, 2 * AB), 0) + AB
    kj = lax.broadcasted_iota(jnp.int32, (AB, 2 * AB), 1)
    back = qi - kj
    valid = (back >= 0) & (back <= AB)
    if first:
        valid = valid & (kj >= AB)
    kk = jnp.concatenate([kp, kc], axis=0)
    vv = jnp.concatenate([vp, vc], axis=0)
    o2 = jnp.zeros((AB, 128), f32)
    lse2 = jnp.zeros((AB, 128), f32)
    for e in range(2):
        hm = ((lane >= 64 * e) & (lane < 64 * (e + 1))).astype(f32)
        s = dot_nt(q2 * (hm * 0.125), kk)
        s = jnp.where(valid, s, -jnp.inf)
        m = jnp.max(s, axis=-1, keepdims=True)
        p = jnp.exp(s - m)
        den = jnp.sum(p, axis=-1, keepdims=True)
        o2 = o2 + dot_nn(p, vv * hm) / den
        lse2 = lse2 + (m + jnp.log(den)) * hm
    return o2, lse2


def _att_blocks(dil):
    m = L // dil
    return [(r * m + n * AB, n == 0) for r in range(dil) for n in range(m // AB)]


def deinterleave(x, dil):
    return x if dil == 1 else x.reshape(L // dil, dil, x.shape[1]).swapaxes(0, 1).reshape(L, x.shape[1])


def attn_fwd(qkv, pos, invf, g, plan=None):
    blocks = _att_blocks(DILS[g])

    def body(q_ref, k_ref, v_ref, pos_ref, invf_ref, o_ref, l_ref, qr, kr):
        cosf, s_lo, s_hi = _rope_tables(pos_ref, invf_ref)
        qr[...] = _rope(q_ref[...], cosf, s_lo, s_hi)
        kr[...] = _rope(k_ref[...], cosf, s_lo, s_hi)
        for off, first in blocks:
            cur, prv = pl.ds(off, AB), pl.ds(off if first else off - AB, AB)
            o2, lse2 = _att_block(qr[cur, :], kr[prv, :], kr[cur, :], v_ref[prv, :], v_ref[cur, :], first)
            o_ref[cur, :] = o2
            l_ref[cur, :] = lse2

    def sec(n):
        return pl.BlockSpec((L, 128), lambda p: (0, p + 4 * n))

    return pcall(
        body, plan, grid=(4,),
        in_specs=[sec(0), sec(1), sec(2), pl.BlockSpec((L, 1), lambda p: (0, 0)), pl.BlockSpec((1, 128), lambda p: (0, 0))],
        out_specs=[sec(0), sec(0)], out_shape=[S((L, 512), f32), S((L, 512), f32)],
        scratch_shapes=[pltpu.VMEM((L, 128), f32), pltpu.VMEM((L, 128), f32)],
        sem=("parallel",), name=f"attn_fwd{g}", args=[qkv, qkv, qkv, pos, invf])


def _att_block_bwd(q2, kp, kc, vp, vc, lse2, do2, dl2, first):
    lane = lax.broadcasted_iota(jnp.int32, (1, 128), 1)
    qi = lax.broadcasted_iota(jnp.int32, (AB, 2 * AB), 0) + AB
    kj = lax.broadcasted_iota(jnp.int32, (AB, 2 * AB), 1)
    back = qi - kj
    valid = (back >= 0) & (back <= AB)
    if first:
        valid = valid & (kj >= AB)
    kk = jnp.concatenate([kp, kc], axis=0)
    vv = jnp.concatenate([vp, vc], axis=0)
    dq2 = jnp.zeros((AB, 128), f32)
    dkk = jnp.zeros((2 * AB, 128), f32)
    dvv = jnp.zeros((2 * AB, 128), f32)
    for e in range(2):
        hb = (lane >= 64 * e) & (lane < 64 * (e + 1))
        hm = hb.astype(f32)
        qs = q2 * (hm * 0.125)
        lse = jnp.max(jnp.where(hb, lse2, -jnp.inf), axis=-1, keepdims=True)
        dls = jnp.sum(dl2 * hm, axis=-1, keepdims=True)
        p = jnp.where(valid, jnp.exp(dot_nt(qs, kk) - lse), 0.0)
        dov = do2 * hm
        dp = dot_nt(dov, vv)
        ds = p * (dp - jnp.sum(p * dp, axis=-1, keepdims=True) + dls)
        dq2 = dq2 + dot_nn(ds, kk) * (hm * 0.125)
        dkk = dkk + dot_tn(ds, qs)
        dvv = dvv + dot_tn(p, dov)
    return dq2, dkk[:AB], dkk[AB:], dvv[:AB], dvv[AB:]


def attn_bwd(qkv, pos, invf, lse, do, dl, g, plan=None):
    blocks = _att_blocks(DILS[g])

    def body(q_ref, k_ref, v_ref, pos_ref, invf_ref, l_ref, do_ref, dl_ref, d_ref, qr, kr, dqr, dkr, dvr):
        cosf, s_lo, s_hi = _rope_tables(pos_ref, invf_ref)
        qr[...] = _rope(q_ref[...], cosf, s_lo, s_hi)
        kr[...] = _rope(k_ref[...], cosf, s_lo, s_hi)
        for off, first in blocks:
            cur, prv = pl.ds(off, AB), pl.ds(off if first else off - AB, AB)
            dq2, dkp, dkc, dvp, dvc = _att_block_bwd(qr[cur, :], kr[prv, :], kr[cur, :], v_ref[prv, :], v_ref[cur, :],
                                                     l_ref[cur, :], do_ref[cur, :], dl_ref[cur, :], first)
            dqr[cur, :] = dq2
            dkr[cur, :] = dkc
            dvr[cur, :] = dvc
            if not first:
                dkr[prv, :] += dkp
                dvr[prv, :] += dvp
        d_ref[0] = _rope_t(dqr[...], cosf, s_lo, s_hi).astype(d_ref.dtype)
        d_ref[1] = _rope_t(dkr[...], cosf, s_lo, s_hi).astype(d_ref.dtype)
        d_ref[2] = dvr[...].astype(d_ref.dtype)

    def sec(n):
        return pl.BlockSpec((L, 128), lambda p: (0, p + 4 * n))

    return pcall(
        body, plan, grid=(4,),
        in_specs=[sec(0), sec(1), sec(2), pl.BlockSpec((L, 1), lambda p: (0, 0)), pl.BlockSpec((1, 128), lambda p: (0, 0)),
                  sec(0), sec(0), sec(0)],
        out_specs=pl.BlockSpec((3, L, 128), lambda p: (0, 0, p)), out_shape=S((3, L, 512), BF),
        scratch_shapes=[pltpu.VMEM((L, 128), f32)] * 5,
        sem=("parallel",), name=f"attn_bwd{g}", args=[qkv, qkv, qkv, pos, invf, lse, do, dl])


def _merge(o0, o1, o2, l0, l1, l2):
    m = jnp.maximum(jnp.maximum(l0, l1), l2)
    e0, e1, e2 = jnp.exp(l0 - m), jnp.exp(l1 - m), jnp.exp(l2 - m)
    return (e0 * o0 + e1 * o1 + e2 * o2) / (e0 + e1 + e2)


def _to_token_major(src_ref, scr, i, dil, slab):
    n = TR // dil
    for r in range(dil):
        rows = pl.ds(pl.multiple_of(r * (L // dil) + i * n, n), n)
        scr[pl.ds(r, n, stride=dil), :] = src_ref[rows, slab * 128:(slab + 1) * 128].astype(f32)
    return scr[...]


def _to_class_major(val, dst_ref, scr, i, dil, slab):
    n = TR // dil
    scr[...] = val
    for r in range(dil):
        rows = pl.ds(pl.multiple_of(r * (L // dil) + i * n, n), n)
        dst_ref[rows, slab * 128:(slab + 1) * 128] = scr[pl.ds(r, n, stride=dil), :].astype(dst_ref.dtype)


def rms_fwd_classes(x, g, name):
    def body(x_ref, g_ref, o_ref, o1_ref, o2_ref, scr):
        i = pl.program_id(0)
        y = _rms(x_ref[...], g_ref[...])
        o_ref[...] = y.astype(o_ref.dtype)
        for s in range(D // 128):
            ys = y[:, s * 128:(s + 1) * 128]
            _to_class_major(ys, o1_ref, scr, i, DILS[1], s)
            _to_class_major(ys, o2_ref, scr, i, DILS[2], s)

    row = pl.BlockSpec((TR, D), lambda i: (i, 0))
    full = pl.BlockSpec((L, D), lambda i: (0, 0))
    return pl.pallas_call(
        body, grid=(L // TR,), in_specs=[row, pl.BlockSpec((1, D), lambda i: (0, 0))], out_specs=[row, full, full],
        out_shape=[S((L, D), BF)] * 3, scratch_shapes=[pltpu.VMEM((TR, 128), f32)],
        compiler_params=_cp(("arbitrary",)), name=name)(x, g)


def rms_bwd_classes(x, g, dy0, dyc, dres, name, plan=None):
    def body(x_ref, g_ref, dy0_ref, d1_ref, d2_ref, dr_ref, dh_ref, dg_ref, scr, dyf):
        i = pl.program_id(0)
        for s in range(D // 128):
            sl = slice(s * 128, (s + 1) * 128)
            dyf[:, sl] = (dy0_ref[:, sl] + _to_token_major(d1_ref, scr.at[0], i, DILS[1], s)
                          + _to_token_major(d2_ref, scr.at[1], i, DILS[2], s))
        _, vjp = jax.vjp(_rms, x_ref[...], g_ref[...])
        dx, dg = vjp(dyf[...])
        dh_ref[...] = dr_ref[...] + dx

        @pl.when(i == 0)
        def _():
            dg_ref[...] = jnp.zeros_like(dg_ref)

        dg_ref[...] += dg

    row = pl.BlockSpec((TR, D), lambda i: (i, 0))
    vec = pl.BlockSpec((1, D), lambda i: (0, 0))
    full = pl.BlockSpec((L, D), lambda i: (0, 0))
    return pcall(body, plan, grid=(L // TR,), in_specs=[row, vec, row, full, full, row], out_specs=[row, vec],
                 out_shape=[S((L, D), f32), S((1, D), f32)],
                 scratch_shapes=[pltpu.VMEM((2, TR, 128), f32), pltpu.VMEM((TR, D), f32)],
                 sem=("arbitrary",), name=name, args=[x, g, dy0, dyc[0], dyc[1], dres])


def attn_merge_fwd(o0, l0, oc, lc, plan=None):
    def body(o0_ref, l0_ref, o1_ref, l1_ref, o2_ref, l2_ref, o_ref, scr):
        i = pl.program_id(0)
        for s in range(4):
            sl = slice(s * 128, (s + 1) * 128)
            o1 = _to_token_major(o1_ref, scr.at[0], i, DILS[1], s)
            l1 = _to_token_major(l1_ref, scr.at[1], i, DILS[1], s)
            o2 = _to_token_major(o2_ref, scr.at[2], i, DILS[2], s)
            l2 = _to_token_major(l2_ref, scr.at[3], i, DILS[2], s)
            o_ref[:, sl] = _merge(o0_ref[:, sl], o1, o2, l0_ref[:, sl], l1, l2).astype(o_ref.dtype)

    blk = pl.BlockSpec((TR, 512), lambda i: (i, 0))
    full = pl.BlockSpec((L, 512), lambda i: (0, 0))
    return pcall(body, plan, grid=(L // TR,), in_specs=[blk, blk, full, full, full, full], out_specs=blk,
                 out_shape=S((L, 512), BF), scratch_shapes=[pltpu.VMEM((4, TR, 128), f32)],
                 sem=("arbitrary",), name="attn_merge_fwd", args=[o0, l0, oc[0], lc[0], oc[1], lc[1]])


def attn_merge_bwd(o0, l0, oc, lc, do, plan=None):
    def body(o0_ref, l0_ref, o1_ref, l1_ref, o2_ref, l2_ref, g_ref, do0, dl0, do1, dl1, do2, dl2, scr):
        i = pl.program_id(0)
        for s in range(4):
            sl = slice(s * 128, (s + 1) * 128)
            o1 = _to_token_major(o1_ref, scr.at[0], i, DILS[1], s)
            l1 = _to_token_major(l1_ref, scr.at[1], i, DILS[1], s)
            o2 = _to_token_major(o2_ref, scr.at[2], i, DILS[2], s)
            l2 = _to_token_major(l2_ref, scr.at[3], i, DILS[2], s)
            _, vjp = jax.vjp(_merge, o0_ref[:, sl], o1, o2, l0_ref[:, sl], l1, l2)
            g0, g1, g2, h0, h1, h2 = vjp(g_ref[:, sl].astype(f32))
            do0[:, sl] = g0.astype(do0.dtype)
            dl0[:, sl] = h0
            _to_class_major(g1, do1, scr.at[0], i, DILS[1], s)
            _to_class_major(h1, dl1, scr.at[1], i, DILS[1], s)
            _to_class_major(g2, do2, scr.at[2], i, DILS[2], s)
            _to_class_major(h2, dl2, scr.at[3], i, DILS[2], s)

    blk = pl.BlockSpec((TR, 512), lambda i: (i, 0))
    full = pl.BlockSpec((L, 512), lambda i: (0, 0))
    outs = pcall(body, plan, grid=(L // TR,), in_specs=[blk, blk, full, full, full, full, blk],
                 out_specs=[blk, blk, full, full, full, full],
                 out_shape=[S((L, 512), BF), S((L, 512), f32)] * 3, scratch_shapes=[pltpu.VMEM((4, TR, 128), f32)],
                 sem=("arbitrary",), name="attn_merge_bwd", args=[o0, l0, oc[0], lc[0], oc[1], lc[1], do])
    return [outs[0], outs[2], outs[4]], [outs[1], outs[3], outs[5]]


def _invf_lanes():
    half = 8
    inv = ROPE_THETA ** (-np.arange(half, dtype=np.float32) * 2.0 / 16.0)
    lane = np.arange(128) % 64
    return jnp.asarray(np.where(lane < 16, inv[lane % 8], 0.0).astype(np.float32)[None, :])


def hosted(C, host, fn):
    p = C.plan(host) if C is not None else None
    out = fn(p)
    if p is not None:
        C.done(p)
    return out


def _ffn_fwd(h, g_row, W, cb, layer, C):
    hn = rms_fwd(h, g_row, f"rms_ffn{layer}")
    hu = hosted(C, f"ffn_in{layer}", lambda p: matmul(hn, W[("ffn_w_in", layer)], mode="nn", tm=1024, tn=1408, tk=1024,
                                                      plan=p, name=f"ffn_in{layer}"))
    act = hosted(C, f"convact_fwd{layer}", lambda p: convact_fwd(hu, W[("ffn_conv_w", layer)], cb, layer, plan=p))
    h2 = hosted(C, f"ffn_out{layer}", lambda p: matmul(act, W[("ffn_w_out", layer)], mode="nn", tm=1024, tn=1024, tk=2816,
                                                       add=h, plan=p, name=f"ffn_out{layer}"))
    return h2, (hn, hu, act)


def _ffn_bwd(dh, h, g_row, W, cb, saved, layer, C, G):
    hn, hu, act = saved
    w_in, w_out = W[("ffn_w_in", layer)], W[("ffn_w_out", layer)]
    dact = hosted(C, f"ffn_out_dx{layer}", lambda p: matmul(dh, w_out, mode="nt", tm=1024, tn=1408, tk=1024, plan=p,
                                                          name=f"ffn_out_dx{layer}"))
    G[("ffn_w_out", layer)] = hosted(C, f"ffn_out_dw{layer}", lambda p: matmul(
        act, dh, mode="tn", tm=1408, tn=1024, tk=L, out_dtype=BF, plan=p, name=f"ffn_out_dw{layer}"))
    dhu, G[("ffn_conv_w", layer)], g_cb = hosted(
        C, f"convact_bwd{layer}", lambda p: convact_bwd(hu, W[("ffn_conv_w", layer)], cb, dact, layer, plan=p))
    dhn = hosted(C, f"ffn_in_dx{layer}", lambda p: matmul(dhu, w_in, mode="nt", tm=1024, tn=1024, tk=2816, plan=p,
                                                         name=f"ffn_in_dx{layer}"))
    G[("ffn_w_in", layer)] = hosted(C, f"ffn_in_dw{layer}", lambda p: matmul(
        hn, dhu, mode="tn", tm=1024, tn=1408, tk=L, out_dtype=BF, plan=p, name=f"ffn_in_dw{layer}"))
    dh2, g_norm = hosted(C, f"rms_ffn_bwd{layer}", lambda p: rms_bwd(h, g_row, [dhn], dh, f"rms_ffn_bwd{layer}", plan=p))
    return dh2, g_cb, g_norm


def local_step(x, pos, tgt, sm, W, C=None):
    G = C.grads if C is not None else {}
    nm, nf = sm["norm_mix"], sm["norm_ffn"]
    invf = _invf_lanes()
    are = sm["s5_A_re"].reshape(NST, 1)
    aim = sm["s5_A_im"].reshape(NST, 1)
    ldt = sm["s5_log_dt"].reshape(1, 32)
    bre = sm["s5_B_re"].reshape(NST, 16)
    bim = sm["s5_B_im"].reshape(NST, 16)
    cre = jnp.swapaxes(sm["s5_C_re"][0], 1, 2).reshape(NST, 16)
    cim = jnp.swapaxes(sm["s5_C_im"][0], 1, 2).reshape(NST, 16)
    drow = sm["s5_D"].reshape(1, S5W)
    wbr, wbi, wcr, wci, abr, abi = s5_params_fwd(are, aim, ldt, bre, bim, cre, cim)
    hn0 = rms_fwd(x, nm[0:1], "rms_mix0")
    cb3 = sm["ffn_conv_b3"]
    proj = hosted(C, "mix_in", lambda p: matmul(hn0, W[("mix_w_in", 0)], mode="nn", tm=1024, tn=1280, tk=1024, plan=p, name="mix_in"))
    xs_re, xs_im, y5 = hosted(C, "s5_scan_fwd", lambda p: s5_scan_fwd(proj, wbr, wbi, wcr, wci, abr, abi, drow, plan=p))
    oa = s5_glu_fwd(y5, W[("s5_glu_w", 0)], sm["s5_glu_b"])
    ob, ssave = hosted(C, "hgrn_fwd", lambda p: hgrn_fwd(proj, sm["hgrn_gamma"], sm["hgrn_norm"], plan=p))
    cat = jnp.concatenate([oa, ob], axis=1)
    h1 = matmul(cat, W[("mix_w_out", 0)], mode="nn", tm=1024, tn=1024, tk=1024, add=x, name="mix_out")
    h2, ffn0 = _ffn_fwd(h1, nf[0:1], W, cb3, 0, C)
    hn2_g = rms_fwd_classes(h2, nm[1:2], "rms_mix1")
    wqkv = W[("att_w_qkv", 0)]
    pos_g, qkv_g, oc_g, lc_g = [], [], [], []
    for g, dil in enumerate(DILS):
        pos_g.append(deinterleave(pos, dil))
        qkv_g.append(hosted(C, f"att_qkv{g}", lambda p: matmul(
            hn2_g[g], wqkv, mode="nn", tm=1024, tn=512, tk=1024, dims=(L, 1536, D),
            b_spec=pl.BlockSpec((D, 512), lambda i, j, k, g=g: (0, 3 * j + g)), plan=p, name=f"att_qkv{g}")))
        o_c, l_c = hosted(C, f"attn_fwd{g}", lambda p: attn_fwd(qkv_g[g], pos_g[g], invf, g, plan=p))
        oc_g.append(o_c)
        lc_g.append(l_c)
    o = hosted(C, "attn_merge_fwd", lambda p: attn_merge_fwd(oc_g[0], lc_g[0], oc_g[1:], lc_g[1:], plan=p))
    h3 = matmul(o, W[("att_w_o", 0)], mode="nn", tm=1024, tn=1024, tk=512, add=h2, name="att_o")
    h4, ffn1 = _ffn_fwd(h3, nf[1:2], W, cb3, 1, C)
    loss, dh, g_nfinal = loss_head(h4, sm["norm_final"].reshape(1, D), tgt)
    dh, g_cb1, g_nf1 = _ffn_bwd(dh, h3, nf[1:2], W, cb3, ffn1, 1, C, G)
    do = matmul(dh, W[("att_w_o", 0)], mode="nt", tm=1024, tn=512, tk=1024, name="att_o_dx")
    G[("att_w_o", 0)] = matmul(o, dh, mode="tn", tm=512, tn=1024, tk=L, out_dtype=BF, name="att_o_dw")
    do_g, dl_g = hosted(C, "attn_merge_bwd", lambda p: attn_merge_bwd(oc_g[0], lc_g[0], oc_g[1:], lc_g[1:], do, plan=p))
    dhn2_g, gq = [], []
    for g, dil in enumerate(DILS):
        d3 = hosted(C, f"attn_bwd{g}", lambda p: attn_bwd(qkv_g[g], pos_g[g], invf, lc_g[g], do_g[g], dl_g[g], g, plan=p))
        dx = matmul(d3, wqkv, mode="nt", tm=1024, tn=1024, tk=512, dims=(L, D, 1536),
                    a_spec=pl.BlockSpec((None, 1024, 512), lambda i, j, k: (k, i, 0)),
                    b_spec=pl.BlockSpec((D, 512), lambda i, j, k, g=g: (0, 3 * k + g)), name=f"att_qkv_dx{g}")
        dhn2_g.append(dx)
        gq.append(matmul(hn2_g[g], d3, mode="tn", tm=1024, tn=512, tk=L, out_dtype=BF, dims=(D, 1536, L),
                         b_spec=pl.BlockSpec((None, L, 512), lambda i, j, k: (j, k, 0)), name=f"att_qkv_dw{g}"))
    G[("att_w_qkv", 0)] = jnp.concatenate([gq[g][:, 512 * s:512 * (s + 1)] for s in range(3) for g in range(3)], axis=1)
    dh, g_nm1 = hosted(C, "rms_mix_bwd1", lambda p: rms_bwd_classes(h2, nm[1:2], dhn2_g[0], dhn2_g[1:], dh, "rms_mix_bwd1", plan=p))
    dh, g_cb0, g_nf0 = _ffn_bwd(dh, h1, nf[0:1], W, cb3, ffn0, 0, C, G)
    dmix = matmul(dh, W[("mix_w_out", 0)], mode="nt", tm=1024, tn=1024, tk=1024, name="mix_out_dx")
    G[("mix_w_out", 0)] = matmul(cat, dh, mode="tn", tm=1024, tn=1024, tk=L, out_dtype=BF, name="mix_out_dw")
    dy5, g_glu_w, g_glu_b = s5_glu_bwd(y5, W[("s5_glu_w", 0)], sm["s5_glu_b"], dmix)
    G[("s5_glu_w", 0)] = g_glu_w.astype(BF)
    du, gwbr, gwbi, gwcr, gwci, gabr, gabi, g_d = hosted(C, "s5_scan_bwd", lambda p: s5_scan_bwd(
        dy5, proj, xs_re, xs_im, wbr, wbi, wcr, wci, abr, abi, drow, plan=p))
    g_are, g_aim, g_ldt, g_bre, g_bim, g_cre, g_cim = s5_params_bwd(are, aim, ldt, bre, bim, cre, cim,
                                                                   (gwbr, gwbi, gwcr, gwci, gabr, gabi))
    dproj, g_gamma, g_hnorm = hosted(C, "hgrn_bwd", lambda p: hgrn_bwd(proj, sm["hgrn_gamma"], sm["hgrn_norm"], ssave, dmix, du,
                                                                       plan=p))
    dhn0 = hosted(C, "mix_in_dx", lambda p: matmul(dproj, W[("mix_w_in", 0)], mode="nt", tm=1024, tn=1024, tk=2560, plan=p,
                                                  name="mix_in_dx"))
    G[("mix_w_in", 0)] = matmul(hn0, dproj, mode="tn", tm=1024, tn=1280, tk=L, out_dtype=BF, name="mix_in_dw")
    gx, g_nm0 = hosted(C, "rms_mix_bwd0", lambda p: rms_bwd(x, nm[0:1], [dhn0], dh, "rms_mix_bwd0", plan=p))
    small = {
        "norm_mix": jnp.concatenate([g_nm0, g_nm1], axis=0), "norm_ffn": jnp.concatenate([g_nf0, g_nf1], axis=0),
        "norm_final": g_nfinal.reshape(D),
        "s5_A_re": g_are.reshape(1, 32, 64), "s5_A_im": g_aim.reshape(1, 32, 64), "s5_log_dt": g_ldt.reshape(1, 32),
        "s5_B_re": g_bre.reshape(1, 32, 64, 16), "s5_B_im": g_bim.reshape(1, 32, 64, 16),
        "s5_C_re": jnp.swapaxes(g_cre.reshape(1, 32, 64, 16), 2, 3), "s5_C_im": jnp.swapaxes(g_cim.reshape(1, 32, 64, 16), 2, 3),
        "s5_D": g_d.reshape(1, 32, 16), "s5_glu_b": g_glu_b, "hgrn_gamma": g_gamma, "hgrn_norm": g_hnorm,
        "ffn_conv_b": jnp.concatenate([g_cb0, g_cb1], axis=0),
    }
    return loss, gx, G, small


BIG = ("mix_w_in", "mix_w_out", "s5_glu_w", "att_w_qkv", "att_w_o", "ffn_w_in", "ffn_w_out", "ffn_conv_w")
SMALL = ("norm_mix", "norm_ffn", "norm_final", "s5_A_re", "s5_A_im", "s5_log_dt", "s5_B_re", "s5_B_im", "s5_C_re", "s5_C_im",
         "s5_D", "s5_glu_b", "hgrn_gamma", "hgrn_norm", "ffn_conv_b")


def cast_bf16(w, name, plan=None):
    nl, r, c = w.shape
    w2 = w.reshape(nl * r, c)
    tr = 256 if (nl * r) % 256 == 0 else nl * r

    def body(w_ref, o_ref):
        o_ref[...] = w_ref[...].astype(BF)

    out = pcall(body, plan, grid=(nl * r // tr,), in_specs=[pl.BlockSpec((tr, c), lambda i: (i, 0))],
                out_specs=pl.BlockSpec((tr, c), lambda i: (i, 0)), out_shape=S((nl * r, c), BF),
                sem=("parallel",), name=name, args=[w2])
    return out.reshape(nl, r, c)


SCHEDULE = {
    "cast_ffn_w_in": [("G", "mix_w_in", 0)],
    "mix_in": [("G", "mix_w_out", 0), ("G", "s5_glu_w", 0)],
    "s5_scan_fwd": [("G", "ffn_w_in", 0, (0, 4))],
    "hgrn_fwd": [("G", "ffn_w_in", 0, (1, 4)), ("G", "ffn_w_in", 0, (2, 4)), ("G", "ffn_w_in", 0, (3, 4)),
                 ("G", "ffn_conv_w", 0), ("G", "ffn_conv_w", 1)],
    "ffn_in0": [("G", "ffn_w_out", 0)],
    "convact_fwd0": [("G", "att_w_qkv", 0, (0, 2))],
    "ffn_out0": [("G", "att_w_qkv", 0, (1, 2))],
    "att_qkv0": [("G", "att_w_o", 0)],
    "attn_fwd0": [("G", "ffn_w_in", 1, (0, 4))],
    "attn_fwd1": [("G", "ffn_w_in", 1, (1, 4))],
    "attn_fwd2": [("G", "ffn_w_in", 1, (2, 4))],
    "attn_merge_fwd": [("G", "ffn_w_in", 1, (3, 4))],
    "ffn_in1": [("G", "ffn_w_out", 1)],
    "convact_bwd1": [("P", "ffn_w_out", 1)],
    "ffn_in_dx1": [("A", "ffn_w_out", 1, (0, 2))],
    "ffn_in_dw1": [("A", "ffn_w_out", 1, (1, 2))],
    "rms_ffn_bwd1": [("P", "ffn_w_in", 1)],
    "attn_merge_bwd": [("P", "att_w_o", 0), ("A", "ffn_conv_w", 1), ("B", "ffn_w_out", 1)],
    "attn_bwd0": [("A", "ffn_w_in", 1, (0, 2)), ("A", "att_w_o", 0)],
    "attn_bwd1": [("A", "ffn_w_in", 1, (1, 2)), ("B", "att_w_o", 0), ("B", "ffn_conv_w", 1)],
    "attn_bwd2": [("B", "ffn_w_in", 1)],
    "rms_mix_bwd1": [("P", "att_w_qkv", 0)],
    "ffn_out_dx0": [("A", "att_w_qkv", 0, (0, 4))],
    "ffn_out_dw0": [("A", "att_w_qkv", 0, (1, 4))],
    "convact_bwd0": [("A", "att_w_qkv", 0, (2, 4)), ("A", "att_w_qkv", 0, (3, 4)), ("P", "ffn_w_out", 0)],
    "ffn_in_dx0": [("A", "ffn_w_out", 0, (0, 2)), ("B", "att_w_qkv", 0)],
    "ffn_in_dw0": [("A", "ffn_w_out", 0, (1, 2))],
    "rms_ffn_bwd0": [("P", "ffn_w_in", 0), ("B", "ffn_w_out", 0)],
    "s5_scan_bwd": [("A", "ffn_w_in", 0, (0, 2)), ("P", "mix_w_out", 0), ("P", "s5_glu_w", 0), ("A", "ffn_conv_w", 0)],
    "hgrn_bwd": [("A", "ffn_w_in", 0, (1, 2)), ("A", "mix_w_out", 0), ("A", "s5_glu_w", 0), ("B", "ffn_conv_w", 0)],
    "mix_in_dx": [("B", "ffn_w_in", 0), ("B", "mix_w_out", 0), ("B", "s5_glu_w", 0)],
    "rms_mix_bwd0": [("P", "mix_w_in", 0)],
    "adam_att_w_o": [("A", "mix_w_in", 0), ("A", "small", 0)],
    "adam_s5_glu_w": [("B", "mix_w_in", 0), ("B", "small", 0)],
}


class Comm:
    def __init__(self, shards, shapes):
        self.shards, self.shapes = shards, shapes
        self.W, self.grads, self.slots = {}, {}, {}
        self.sib, self.pair = {}, {}
        self.small = None

    def plan(self, host):
        items = SCHEDULE.get(host)
        if not items:
            return None
        p = Plan()
        for it in items:
            kind, name, l = it[:3]
            part, parts = it[3] if len(it) > 3 else (0, 1)
            if name == "small":
                kdst = p.buf("slots:small", arr=self.slots.get("small"), shape=S((8,) + self.small.shape, f32), write=True)
                if kind == "A":
                    ReduceOp(p, p.buf("g:small", arr=self.small), kdst, None, self.small.shape, False, 0, 0, whole=True)
                else:
                    ForwardOp(p, kdst, None, whole=True)
                continue
            nl, R, C_ = self.shapes[name]
            rows = name in ROW_SHARDED
            r0, nr = part * (R // parts), R // parts
            if kind == "G":
                sh = self.shards[name]
                kdst = p.buf(f"W:{name}:{l}", arr=self.W.get((name, l)), shape=S((4 * R, C_) if rows else (R, 4 * C_), sh.dtype),
                             write=True)
                GatherOp(p, p.buf("shard:" + name, arr=sh), kdst, l, self.shapes[name], rows, r0, nr, split=(nr % 32 == 0))
            elif name == "ffn_conv_w":
                g = self.grads[(name, l)]
                kdst = p.buf("slots:" + name, arr=self.slots.get(name), shape=S((8, nl, R, C_), g.dtype), write=True)
                if kind == "A":
                    ReduceOp(p, p.buf(f"g:{name}:{l}", arr=g), kdst, l, self.shapes[name], rows, r0, nr)
                else:
                    ForwardOp(p, kdst, l)
            elif kind == "P":
                g = self.grads[(name, l)]
                ksib = p.buf(f"sib:{name}:{l}", shape=S((4 * R // 2, C_) if rows else (R // 2, 4 * C_), g.dtype), write=True)
                PairOp(p, p.buf(f"g:{name}:{l}", arr=g), ksib, self.shapes[name], rows)
            else:
                if (name, l) not in self.pair:
                    self.pair[(name, l)] = pair_sum(self.grads[(name, l)], self.sib[(name, l)], rows, R, f"pair_sum_{name}{l}")
                h = self.pair[(name, l)]
                kdst = p.buf("slots:" + name, arr=self.slots.get(name), shape=S((4, nl, R, C_), h.dtype), write=True)
                if kind == "A":
                    ReduceOp(p, p.buf(f"h:{name}:{l}", arr=h), kdst, l, self.shapes[name], rows, r0 // 2, nr // 2, half=True)
                else:
                    HalfForwardOp(p, kdst, l, self.shapes[name])
        return p

    def done(self, p):
        for k, arr in p.out.items():
            tag, name = k.split(":")[:2]
            if tag == "W":
                self.W[(name, int(k.split(":")[2]))] = arr
            elif tag == "sib":
                self.sib[(name, int(k.split(":")[2]))] = arr
            else:
                self.slots[name] = arr


def _adamw(w, g, m, v):
    m = B1 * m + (1.0 - B1) * g
    v = B2 * v + (1.0 - B2) * jnp.square(g)
    m_hat = m / (1.0 - B1 ** STEP)
    v_hat = v / (1.0 - B2 ** STEP)
    return -LR * (m_hat / (jnp.sqrt(v_hat) + AEPS) + WD * w), m, v


def adam_big(w, m, v, slots, name, plan=None):
    nl, R, C = w.shape
    ns = slots.shape[0]
    tr = 128 if R % 128 == 0 else (64 if R % 64 == 0 else R)

    def body(w_ref, m_ref, v_ref, s_ref, g_ref, d_ref, nm_ref, nv_ref):
        g = s_ref[0].astype(f32)
        for s in range(1, ns):
            g = g + s_ref[s].astype(f32)
        d, nm_, nv_ = _adamw(w_ref[...], g, m_ref[...], v_ref[...])
        g_ref[...] = g
        d_ref[...] = d
        nm_ref[...] = nm_
        nv_ref[...] = nv_

    blk = pl.BlockSpec((None, tr, C), lambda l, i: (l, i, 0))
    return pcall(body, plan, grid=(nl, R // tr),
                 in_specs=[blk, blk, blk, pl.BlockSpec((ns, None, tr, C), lambda l, i: (0, l, i, 0))],
                 out_specs=[blk] * 4, out_shape=[S((nl, R, C), f32)] * 4,
                 sem=("parallel", "parallel"), name=name, args=[w, m, v, slots])


def adam_small(w, m, v, slots):
    R = w.shape[0]
    tr = 256

    def body(w_ref, m_ref, v_ref, s_ref, g_ref, d_ref, nm_ref, nv_ref):
        g = s_ref[0]
        for s in range(1, 8):
            g = g + s_ref[s]
        d, nm_, nv_ = _adamw(w_ref[...], g, m_ref[...], v_ref[...])
        g_ref[...] = g
        d_ref[...] = d
        nm_ref[...] = nm_
        nv_ref[...] = nv_

    blk = pl.BlockSpec((tr, 128), lambda i: (i, 0))
    return pl.pallas_call(
        body, grid=(R // tr,), in_specs=[blk, blk, blk, pl.BlockSpec((8, tr, 128), lambda i: (0, i, 0))],
        out_specs=[blk] * 4, out_shape=[S((R, 128), f32)] * 4,
        compiler_params=_cp(("parallel",)), name="adam_small")(w, m, v, slots)


def _pack(d):
    flat = jnp.concatenate([d[n].reshape(-1) for n in SMALL])
    n = flat.shape[0]
    rows = -(-n // (256 * 128)) * 256
    return jnp.pad(flat, (0, rows * 128 - n)).reshape(rows, 128)


def _unpack(p, like):
    flat = p.reshape(-1)
    out, off = {}, 0
    for n in SMALL:
        sz = math.prod(like[n].shape)
        out[n] = flat[off:off + sz].reshape(like[n].shape)
        off += sz
    return out


def kernel(x, positions, norm_mix, norm_ffn, norm_final, mix_w_in, mix_w_out, s5_A_re, s5_A_im, s5_log_dt, s5_B_re, s5_B_im, s5_C_re, s5_C_im, s5_D, s5_glu_w, s5_glu_b, hgrn_gamma, hgrn_norm, att_w_qkv, att_w_o, ffn_w_in, ffn_conv_w, ffn_conv_b, ffn_w_out, loss_target, m_norm_mix, m_norm_ffn, m_norm_final, m_mix_w_in, m_mix_w_out, m_s5_A_re, m_s5_A_im, m_s5_log_dt, m_s5_B_re, m_s5_B_im, m_s5_C_re, m_s5_C_im, m_s5_D, m_s5_glu_w, m_s5_glu_b, m_hgrn_gamma, m_hgrn_norm, m_att_w_qkv, m_att_w_o, m_ffn_w_in, m_ffn_conv_w, m_ffn_conv_b, m_ffn_w_out, v_norm_mix, v_norm_ffn, v_norm_final, v_mix_w_in, v_mix_w_out, v_s5_A_re, v_s5_A_im, v_s5_log_dt, v_s5_B_re, v_s5_B_im, v_s5_C_re, v_s5_C_im, v_s5_D, v_s5_glu_w, v_s5_glu_b, v_hgrn_gamma, v_hgrn_norm, v_att_w_qkv, v_att_w_o, v_ffn_w_in, v_ffn_conv_w, v_ffn_conv_b, v_ffn_w_out):
    a = dict(locals())
    weights = BIG + SMALL
    w = {n: a[n] for n in weights}
    m = {n: a["m_" + n] for n in weights}
    v = {n: a["v_" + n] for n in weights}
    shards = {"ffn_conv_w": ffn_conv_w}
    C = Comm(shards, {n: w[n].shape for n in BIG})
    for n in ("mix_w_in", "ffn_w_in", "mix_w_out", "s5_glu_w", "ffn_w_out", "att_w_qkv", "att_w_o"):
        shards[n] = hosted(C, "cast_" + n, lambda p: cast_bf16(w[n], "cast_" + n, plan=p))
    sm = {n: w[n] for n in SMALL}
    sm["ffn_conv_b3"] = ffn_conv_b.reshape(2, 1, 2 * DFF)
    loss, gx, _, gsmall = local_step(x[0], positions.reshape(L, 1), loss_target[0], sm, C.W, C)
    C.small = _pack(gsmall)
    res = {}
    for n in ("att_w_o", "s5_glu_w", "ffn_w_in", "ffn_w_out", "att_w_qkv", "mix_w_out", "ffn_conv_w", "mix_w_in"):
        res[n] = hosted(C, "adam_" + n, lambda p: adam_big(w[n], m[n], v[n], C.slots[n], "adam_" + n, plan=p))
    packed = adam_small(_pack({n: w[n] for n in SMALL}), _pack({n: m[n] for n in SMALL}), _pack({n: v[n] for n in SMALL}),
                        C.slots["small"])
    small_out = [_unpack(p, {n: w[n] for n in SMALL}) for p in packed]
    for n in SMALL:
        res[n] = tuple(so[n] for so in small_out)
    total = lax.psum(loss[0, 0], ("x", "y", "c"))
    order = ("norm_mix", "norm_ffn", "norm_final", "mix_w_in", "mix_w_out", "s5_A_re", "s5_A_im", "s5_log_dt", "s5_B_re", "s5_B_im",
             "s5_C_re", "s5_C_im", "s5_D", "s5_glu_w", "s5_glu_b", "hgrn_gamma", "hgrn_norm", "att_w_qkv", "att_w_o", "ffn_w_in",
             "ffn_conv_w", "ffn_conv_b", "ffn_w_out")
    return (total, gx[None], *[res[n][0] for n in order], *[res[n][1] for n in order], *[res[n][2] for n in order],
            *[res[n][3] for n in order])
```

```python
import functools
import math

import numpy as np
import jax
import jax.numpy as jnp
from jax import lax
from jax.experimental import pallas as pl
from jax.experimental.pallas import tpu as pltpu

f32 = jnp.float32
BF = jnp.bfloat16
HI = lax.Precision.HIGHEST
S = jax.ShapeDtypeStruct
MESH = pl.DeviceIdType.MESH

L = 2048
D = 1024
EPS = 1e-6
S5W = 512
NST = 2048
HGC = 64
DFF = 2816
ROPE_THETA = 500000.0
LR, B1, B2, AEPS, WD, STEP = 0.001, 0.9, 0.999, 1e-08, 0.01, 10
VMEM_LIMIT = 56 * 1024 * 1024


def _cp(sem=None):
    return pltpu.CompilerParams(dimension_semantics=sem, vmem_limit_bytes=VMEM_LIMIT)


ANY = pl.BlockSpec(memory_space=pl.ANY)
ROW_SHARDED = ("mix_w_out", "s5_glu_w", "ffn_w_out")


def _coords():
    x, y, c = lax.axis_index("x"), lax.axis_index("y"), lax.axis_index("c")
    return x, y, c, 2 * x + y, [(1 - x, y), (x, 1 - y), (1 - x, 1 - y)]


def _rows(start, n):
    return pl.ds(start if isinstance(start, int) else pl.multiple_of(start, 8), n)


def _cols(q, n):
    return pl.ds(pl.multiple_of(q * n, 128), n)


class Plan:
    def __init__(self):
        self.bufs, self.ops, self.nsem, self.out = {}, [], 0, {}

    def buf(self, key, arr=None, shape=None, write=False):
        b = self.bufs.setdefault(key, dict(arr=arr, shape=shape, write=False))
        b["write"] = b["write"] or write
        return key

    def add(self, op):
        op.base = self.nsem
        self.nsem += op.nsem
        self.ops.append(op)


class GatherOp:
    nsem = 13

    def __init__(self, plan, ksrc, kdst, l, shard_shape, rows, r0, nr, split):
        self.ksrc, self.kdst, self.l, (_, self.R, self.C), self.rows, self.r0, self.nr, self.split = (
            ksrc, kdst, l, shard_shape, rows, r0, nr, split)
        self.h = nr // 2 if split else nr
        plan.add(self)

    def _dst(self, R_, q, start, n):
        if self.rows:
            return R_[self.kdst].at[_rows(q * self.R + start, n), :]
        return R_[self.kdst].at[_rows(start, n), _cols(q, self.C)]

    def _mine(self, c):
        return self.r0 + (c * self.h if self.split else 0)

    def _theirs(self, c):
        return self.r0 + ((1 - c) * self.h if self.split else 0)

    def _copies(self, R_, sems):
        x, y, c, me, others = _coords()
        src = R_[self.ksrc]
        local = pltpu.make_async_copy(src.at[self.l, _rows(self.r0, self.nr), :], self._dst(R_, me, self.r0, self.nr),
                                      sems.at[self.base + 12])
        send, fwd = [], []
        for k, (px, py) in enumerate(others):
            q = 2 * px + py
            send.append((
                pltpu.make_async_remote_copy(src.at[self.l, _rows(self._mine(c), self.h), :], self._dst(R_, me, self._mine(c), self.h),
                                             sems.at[self.base + k], sems.at[self.base + 3 + k], device_id=(px, py, c), device_id_type=MESH),
                pltpu.make_async_remote_copy(src.at[self.l, _rows(self._mine(c), self.h), :], self._dst(R_, q, self._mine(c), self.h),
                                             sems.at[self.base + k], sems.at[self.base + 3 + k], device_id=(px, py, c), device_id_type=MESH)))
            fwd.append((
                pltpu.make_async_remote_copy(self._dst(R_, q, self._mine(c), self.h), self._dst(R_, q, self._mine(c), self.h),
                                             sems.at[self.base + 6 + k], sems.at[self.base + 9 + k], device_id=(x, y, 1 - c), device_id_type=MESH),
                pltpu.make_async_remote_copy(self._dst(R_, q, self._theirs(c), self.h), self._dst(R_, q, self._theirs(c), self.h),
                                             sems.at[self.base + 6 + k], sems.at[self.base + 9 + k], device_id=(x, y, 1 - c), device_id_type=MESH)))
        return local, send, fwd

    def start(self, R_, sems):
        local, send, _ = self._copies(R_, sems)
        local.start()
        for out, _ in send:
            out.start()

    def finish(self, R_, sems):
        local, send, fwd = self._copies(R_, sems)
        for k in range(3):
            send[k][1].wait_recv()
            if self.split:
                fwd[k][0].start()
        for k in range(3):
            if self.split:
                fwd[k][1].wait_recv()
                fwd[k][0].wait_send()
            send[k][0].wait_send()
        local.wait()


class ReduceOp:
    nsem = 7

    def __init__(self, plan, ksrc, kdst, l, shard_shape, rows, r0, nr, whole=False, half=False):
        self.ksrc, self.kdst, self.l, (self.R, self.C), self.rows, self.r0, self.nr, self.whole, self.half = (
            ksrc, kdst, l, shard_shape[-2:], rows, r0, nr, whole, half)
        plan.add(self)

    def _piece(self, R_, q):
        g = R_[self.ksrc]
        if self.whole:
            return g
        if self.rows:
            return g.at[_rows(q * (self.R // 2 if self.half else self.R) + self.r0, self.nr), :]
        return g.at[_rows(self.r0, self.nr), _cols(q, self.C)]

    def _slot(self, R_, q, c):
        if self.whole:
            return R_[self.kdst].at[2 * q + c]
        if self.half:
            return R_[self.kdst].at[q, self.l, _rows(c * (self.R // 2) + self.r0, self.nr), :]
        return R_[self.kdst].at[2 * q + c, self.l, _rows(self.r0, self.nr), :]

    def _copies(self, R_, sems):
        x, y, c, me, others = _coords()
        local = pltpu.make_async_copy(self._piece(R_, me), self._slot(R_, me, c), sems.at[self.base + 6])
        send = []
        for k, (px, py) in enumerate(others):
            q = 2 * px + py
            send.append((
                pltpu.make_async_remote_copy(self._piece(R_, q), self._slot(R_, me, c), sems.at[self.base + k],
                                             sems.at[self.base + 3 + k], device_id=(px, py, c), device_id_type=MESH),
                pltpu.make_async_remote_copy(self._piece(R_, q), self._slot(R_, q, c), sems.at[self.base + k],
                                             sems.at[self.base + 3 + k], device_id=(px, py, c), device_id_type=MESH)))
        return local, send

    def start(self, R_, sems):
        local, send = self._copies(R_, sems)
        local.start()
        for out, _ in send:
            out.start()

    def finish(self, R_, sems):
        local, send = self._copies(R_, sems)
        local.wait()
        for out, inn in send:
            inn.wait_recv()
            out.wait_send()


class ForwardOp:
    nsem = 8

    def __init__(self, plan, kdst, l, whole=False):
        self.kdst, self.l, self.whole = kdst, l, whole
        plan.add(self)

    def _slot(self, R_, s):
        return R_[self.kdst].at[s] if self.whole else R_[self.kdst].at[s, self.l]

    def _copies(self, R_, sems):
        x, y, c, me, others = _coords()
        return [(pltpu.make_async_remote_copy(self._slot(R_, 2 * q + c), self._slot(R_, 2 * q + c), sems.at[self.base + q],
                                              sems.at[self.base + 4 + q], device_id=(x, y, 1 - c), device_id_type=MESH),
                 pltpu.make_async_remote_copy(self._slot(R_, 2 * q + 1 - c), self._slot(R_, 2 * q + 1 - c), sems.at[self.base + q],
                                              sems.at[self.base + 4 + q], device_id=(x, y, 1 - c), device_id_type=MESH))
                for q in range(4)]

    def start(self, R_, sems):
        for out, _ in self._copies(R_, sems):
            out.start()

    def finish(self, R_, sems):
        for out, inn in self._copies(R_, sems):
            inn.wait_recv()
            out.wait_send()


class PairOp:
    nsem = 8

    def __init__(self, plan, ksrc, kdst, shard_shape, rows):
        self.ksrc, self.kdst, (self.R, self.C), self.rows = ksrc, kdst, shard_shape[-2:], rows
        plan.add(self)

    def _copies(self, R_, sems):
        x, y, c, me, others = _coords()
        g, dst, h = R_[self.ksrc], R_[self.kdst], self.R // 2
        out = []
        for q in range(4 if self.rows else 1):
            src = g.at[_rows(q * self.R + (1 - c) * h, h), :]
            land = dst.at[_rows(q * h, h), :]
            out.append(pltpu.make_async_remote_copy(src, land, sems.at[self.base + q], sems.at[self.base + 4 + q],
                                                    device_id=(x, y, 1 - c), device_id_type=MESH))
        return out

    def start(self, R_, sems):
        for cp in self._copies(R_, sems):
            cp.start()

    def finish(self, R_, sems):
        for cp in self._copies(R_, sems):
            cp.wait_recv()
            cp.wait_send()


class HalfForwardOp:
    nsem = 2

    def __init__(self, plan, kdst, l, shard_shape):
        self.kdst, self.l, self.R = kdst, l, shard_shape[-2]
        plan.add(self)

    def _copy(self, R_, sems, core):
        x, y, c, me, others = _coords()
        part = R_[self.kdst].at[:, self.l, _rows((c if core == "mine" else 1 - c) * (self.R // 2), self.R // 2), :]
        return pltpu.make_async_remote_copy(part, part, sems.at[self.base], sems.at[self.base + 1],
                                            device_id=(x, y, 1 - c), device_id_type=MESH)

    def start(self, R_, sems):
        self._copy(R_, sems, "mine").start()

    def finish(self, R_, sems):
        self._copy(R_, sems, "theirs").wait_recv()
        self._copy(R_, sems, "mine").wait_send()


def pair_sum(g, gsib, rows, R, name):
    h = R // 2
    W = g.shape[1]
    tr = h if h * W * 2 <= 2 ** 21 else 128
    nq = 4 if rows else 1

    def body(c_ref, a_ref, b_ref, o_ref):
        o_ref[...] = (a_ref[...].astype(f32) + b_ref[...].astype(f32)).astype(o_ref.dtype)

    half = pl.BlockSpec((tr, W), lambda q, i, c_ref: (q * (h // tr) + i, 0))
    mine = pl.BlockSpec((tr, W), lambda q, i, c_ref: (q * (R // tr) + c_ref[0] * (h // tr) + i, 0))
    return pl.pallas_call(
        body, grid_spec=pltpu.PrefetchScalarGridSpec(num_scalar_prefetch=1, grid=(nq, h // tr), in_specs=[mine, half],
                                                     out_specs=half),
        out_shape=S(gsib.shape, g.dtype), compiler_params=_cp(("parallel", "parallel")),
        name=name)(lax.axis_index("c").reshape(1).astype(jnp.int32), g, gsib)


def pcall(body, plan, *, grid, in_specs, out_specs, out_shape, scratch_shapes=(), sem, name, args):
    multi = isinstance(out_shape, (list, tuple))
    if plan is None or not plan.ops:
        return pl.pallas_call(body, grid=grid, in_specs=in_specs, out_specs=out_specs, out_shape=out_shape,
                              scratch_shapes=list(scratch_shapes), compiler_params=_cp(sem), name=name)(*args)
    outs = list(out_shape) if multi else [out_shape]
    ospecs = list(out_specs) if multi else [out_specs]
    kin = [k for k, b in plan.bufs.items() if b["arr"] is not None]
    kout = [k for k, b in plan.bufs.items() if b["write"]]
    n_in, n_out, n_scr = len(in_specs), len(outs), len(scratch_shapes)

    def wrapped(*refs):
        o0 = n_in + len(kin)
        s0 = o0 + n_out + len(kout)
        R_ = dict(zip(kin, refs[n_in:o0]))
        R_.update(zip(kout, refs[o0 + n_out:s0]))
        sems = refs[s0 + n_scr]
        first = functools.reduce(jnp.logical_and, [pl.program_id(d) == 0 for d in range(len(grid))])
        last = functools.reduce(jnp.logical_and, [pl.program_id(d) == grid[d] - 1 for d in range(len(grid))])

        @pl.when(first)
        def _():
            for op in plan.ops:
                op.start(R_, sems)

        body(*refs[:n_in], *refs[o0:o0 + n_out], *refs[s0:s0 + n_scr])

        @pl.when(last)
        def _():
            for op in plan.ops:
                op.finish(R_, sems)

    def shape_of(k):
        b = plan.bufs[k]
        return S(b["arr"].shape, b["arr"].dtype) if b["arr"] is not None else b["shape"]

    res = pl.pallas_call(
        wrapped, grid=grid, in_specs=list(in_specs) + [ANY] * len(kin), out_specs=ospecs + [ANY] * len(kout),
        out_shape=outs + [shape_of(k) for k in kout],
        scratch_shapes=list(scratch_shapes) + [pltpu.SemaphoreType.DMA((plan.nsem,))],
        input_output_aliases={n_in + kin.index(k): n_out + kout.index(k) for k in kout if plan.bufs[k]["arr"] is not None},
        compiler_params=pltpu.CompilerParams(dimension_semantics=("arbitrary",) * len(grid), vmem_limit_bytes=VMEM_LIMIT,
                                             has_side_effects=True),
        name=name)(*args, *[plan.bufs[k]["arr"] for k in kin])
    plan.out = dict(zip(kout, res[n_out:]))
    return list(res[:n_out]) if multi else res[0]


def _dg(a, b, ca, cb):
    return lax.dot_general(a.astype(BF), b.astype(BF), (((ca,), (cb,)), ((), ())), preferred_element_type=f32)


@jax.custom_vjp
def dot_nn(a, b):
    return _dg(a, b, 1, 0)


@jax.custom_vjp
def dot_nt(a, b):
    return _dg(a, b, 1, 1)


@jax.custom_vjp
def dot_tn(a, b):
    return _dg(a, b, 0, 0)


dot_nn.defvjp(lambda a, b: (dot_nn(a, b), (a, b)),
              lambda r, g: (dot_nt(g, r[1]).astype(r[0].dtype), dot_tn(r[0], g).astype(r[1].dtype)))
dot_nt.defvjp(lambda a, b: (dot_nt(a, b), (a, b)),
              lambda r, g: (dot_nn(g, r[1]).astype(r[0].dtype), dot_tn(g, r[0]).astype(r[1].dtype)))
dot_tn.defvjp(lambda a, b: (dot_tn(a, b), (a, b)),
              lambda r, g: (dot_nt(r[1], g).astype(r[0].dtype), dot_nn(r[0], g).astype(r[1].dtype)))


def matmul(a, b, *, mode, tm, tn, tk, out_dtype=f32, add=None, b_lead=None, a_spec=None, b_spec=None, dims=None, plan=None, name):
    a_over, b_over = a_spec, b_spec
    if mode == "nn":
        (M, K), N = a.shape[-2:], b.shape[-1]
        a_spec = pl.BlockSpec((tm, tk), lambda i, j, k: (i, k))
        b_blk, b_idx, ca, cb = (tk, tn), (lambda i, j, k: (k, j)), 1, 0
    elif mode == "nt":
        (M, K), N = a.shape[-2:], b.shape[-2]
        a_spec = pl.BlockSpec((tm, tk), lambda i, j, k: (i, k))
        b_blk, b_idx, ca, cb = (tn, tk), (lambda i, j, k: (j, k)), 1, 1
    else:
        (K, M), N = a.shape[-2:], b.shape[-1]
        a_spec = pl.BlockSpec((tk, tm), lambda i, j, k: (k, i))
        b_blk, b_idx, ca, cb = (tk, tn), (lambda i, j, k: (k, j)), 0, 0
    if dims is not None:
        M, N, K = dims
    assert M % tm == 0 and N % tn == 0 and K % tk == 0, (name, M, N, K, tm, tn, tk)
    if b_lead is None:
        b_spec = pl.BlockSpec(b_blk, b_idx)
    else:
        b_spec = pl.BlockSpec((None,) + b_blk, lambda i, j, k: (b_lead,) + b_idx(i, j, k))
    if a_over is not None:
        a_spec = a_over
    if b_over is not None:
        b_spec = b_over
    nk = K // tk
    has_add = add is not None

    def body(*refs):
        a_ref, b_ref = refs[0], refs[1]
        add_ref = refs[2] if has_add else None
        o_ref = refs[2 + has_add]
        p = _dg(a_ref[...], b_ref[...], ca, cb)

        def fin(v):
            if has_add:
                v = v + add_ref[...].astype(f32)
            o_ref[...] = v.astype(o_ref.dtype)

        if nk == 1:
            fin(p)
        else:
            acc = refs[3 + has_add]
            k = pl.program_id(2)

            @pl.when(k == 0)
            def _():
                acc[...] = p

            @pl.when(k > 0)
            def _():
                acc[...] += p

            @pl.when(k == nk - 1)
            def _():
                fin(acc[...])

    in_specs = [a_spec, b_spec]
    args = [a, b]
    if has_add:
        in_specs.append(pl.BlockSpec((tm, tn), lambda i, j, k: (i, j)))
        args.append(add)
    return pcall(body, plan, grid=(M // tm, N // tn, nk), in_specs=in_specs,
                 out_specs=pl.BlockSpec((tm, tn), lambda i, j, k: (i, j)), out_shape=S((M, N), out_dtype),
                 scratch_shapes=[pltpu.VMEM((tm, tn), f32)] if nk > 1 else [],
                 sem=("parallel", "parallel", "arbitrary"), name=name, args=args)


def _rms(xv, gv):
    return xv * lax.rsqrt(jnp.mean(xv * xv, axis=-1, keepdims=True) + EPS) * gv


TR = 256


def rms_fwd(x, g, name):
    def body(x_ref, g_ref, o_ref):
        o_ref[...] = _rms(x_ref[...], g_ref[...]).astype(o_ref.dtype)

    return pl.pallas_call(
        body, grid=(L // TR,),
        in_specs=[pl.BlockSpec((TR, D), lambda i: (i, 0)), pl.BlockSpec((1, D), lambda i: (0, 0))],
        out_specs=pl.BlockSpec((TR, D), lambda i: (i, 0)), out_shape=S((L, D), BF),
        compiler_params=_cp(("parallel",)), name=name)(x, g)


def rms_bwd(x, g, dys, dres, name, plan=None):
    nd = len(dys)

    def body(*refs):
        x_ref, g_ref = refs[0], refs[1]
        dr_ref, dh_ref, dg_ref = refs[2 + nd:]
        dy = refs[2][...].astype(f32)
        for r in refs[3:2 + nd]:
            dy = dy + r[...].astype(f32)
        _, vjp = jax.vjp(_rms, x_ref[...], g_ref[...])
        dx, dg = vjp(dy)
        dh_ref[...] = dr_ref[...] + dx

        @pl.when(pl.program_id(0) == 0)
        def _():
            dg_ref[...] = jnp.zeros_like(dg_ref)

        dg_ref[...] += dg

    row = pl.BlockSpec((TR, D), lambda i: (i, 0))
    vec = pl.BlockSpec((1, D), lambda i: (0, 0))
    return pcall(body, plan, grid=(L // TR,), in_specs=[row, vec] + [row] * (nd + 1), out_specs=[row, vec],
                 out_shape=[S((L, D), f32), S((1, D), f32)], sem=("arbitrary",), name=name, args=[x, g, *dys, dres])


def loss_head(h, g, tgt):
    def f(hv, gv, tv):
        y = _rms(hv, gv)
        return 0.5 * jnp.sum(jnp.mean(jnp.square(y - tv), axis=-1))

    def body(h_ref, g_ref, t_ref, l_ref, dh_ref, dg_ref):
        val, vjp = jax.vjp(f, h_ref[...], g_ref[...], t_ref[...])
        dh, dg, _ = vjp(jnp.ones((), f32))
        dh_ref[...] = dh

        @pl.when(pl.program_id(0) == 0)
        def _():
            dg_ref[...] = jnp.zeros_like(dg_ref)
            l_ref[...] = jnp.zeros_like(l_ref)

        dg_ref[...] += dg
        l_ref[...] += jnp.full((1, 128), val, f32)

    row = pl.BlockSpec((TR, D), lambda i: (i, 0))
    vec = pl.BlockSpec((1, D), lambda i: (0, 0))
    return pl.pallas_call(
        body, grid=(L // TR,), in_specs=[row, vec, row],
        out_specs=[pl.BlockSpec((1, 128), lambda i: (0, 0)), row, vec],
        out_shape=[S((1, 128), f32), S((L, D), f32), S((1, D), f32)],
        compiler_params=_cp(("arbitrary",)), name="loss_head")(h, g, tgt)


def _col_to_row(c):
    n = c.shape[0]
    t = jnp.broadcast_to(c, (n, 128)).T
    r = lax.broadcasted_iota(jnp.int32, (128, n), 0)
    return jnp.sum(jnp.where(r == 0, t, 0.0), axis=0, keepdims=True)


def _s5_param_map(are, aim, ldt_row, bre, bim, cre, cim):
    n = NST
    gi = lax.broadcasted_iota(jnp.int32, (n, 32), 0) // 64
    gj = lax.broadcasted_iota(jnp.int32, (n, 32), 1)
    ldt = jnp.sum(jnp.where(gi == gj, ldt_row, 0.0), axis=1, keepdims=True)
    dt = jnp.exp(ldt)
    mag = jnp.exp(are * dt)
    abr = mag * jnp.cos(aim * dt)
    abi = mag * jnp.sin(aim * dt)
    den = are * are + aim * aim
    nr, ni = abr - 1.0, abi
    cr = (nr * are + ni * aim) / den
    ci = (ni * are - nr * aim) / den
    bbr = cr * bre - ci * bim
    bbi = cr * bim + ci * bre
    tc = lax.broadcasted_iota(jnp.int32, (16, 128), 0)
    tl = lax.broadcasted_iota(jnp.int32, (16, 128), 1)
    T = (tl % 16 == tc).astype(f32)
    mr = (lax.broadcasted_iota(jnp.int32, (n, 128), 0) // 64) % 8
    mc = lax.broadcasted_iota(jnp.int32, (n, 128), 1) // 16
    mask = (mr == mc).astype(f32)

    def expand(v):
        return jnp.dot(v, T, precision=HI, preferred_element_type=f32) * mask

    return expand(bbr), expand(bbi), expand(cre), expand(cim), _col_to_row(abr), _col_to_row(abi)


def s5_params_fwd(are, aim, ldt_row, bre, bim, cre, cim):
    def body(*refs):
        outs = _s5_param_map(*[r[...] for r in refs[:7]])
        for o_ref, o in zip(refs[7:], outs):
            o_ref[...] = o

    return pl.pallas_call(
        body, out_shape=[S((NST, 128), f32)] * 4 + [S((1, NST), f32)] * 2,
        compiler_params=_cp(), name="s5_params_fwd")(are, aim, ldt_row, bre, bim, cre, cim)


def s5_params_bwd(are, aim, ldt_row, bre, bim, cre, cim, cots):
    def body(*refs):
        _, vjp = jax.vjp(_s5_param_map, *[r[...] for r in refs[:7]])
        gs = vjp(tuple(r[...] for r in refs[7:13]))
        for o_ref, o in zip(refs[13:], gs):
            o_ref[...] = o

    return pl.pallas_call(
        body, out_shape=[S((NST, 1), f32)] * 2 + [S((1, 32), f32)] + [S((NST, 16), f32)] * 4,
        compiler_params=_cp(), name="s5_params_bwd")(are, aim, ldt_row, bre, bim, cre, cim, *cots)


def _cpowers(ar, ai):
    out = [(ar, ai)]
    for _ in range(7):
        pr, pi = out[-1]
        out.append((pr * ar - pi * ai, pr * ai + pi * ar))
    return out


def _ctable(pw, rid, power):
    tr_ = jnp.zeros(rid.shape, f32)
    ti_ = jnp.zeros(rid.shape, f32)
    for r in range(8):
        pr, pi = pw[power(r) - 1]
        tr_ = jnp.where(rid == r, pr, tr_)
        ti_ = jnp.where(rid == r, pi, ti_)
    return tr_, ti_


NT5 = 4
RC = 256


def s5_scan_fwd(proj, wbr, wbi, wcr, wci, abr, abi, drow, plan=None):
    def body(u_ref, wbr_ref, wbi_ref, wcr_ref, wci_ref, ar_ref, ai_ref, d_ref, xr_ref, xi_ref, y_ref):
        wbr_v, wbi_v = wbr_ref[...], wbi_ref[...]
        for r in range(L // RC):
            rows = pl.ds(r * RC, RC)
            ub = u_ref[rows, :]
            xr_ref[rows, :] = dot_nt(ub, wbr_v)
            xi_ref[rows, :] = dot_nt(ub, wbi_v)
        pw = _cpowers(ar_ref[...], ai_ref[...])
        rid = lax.broadcasted_iota(jnp.int32, (8, 512), 0)
        tr_, ti_ = _ctable(pw, rid, lambda r: r + 1)

        def group(j, c):
            cr, ci = c
            rows = pl.ds(pl.multiple_of(j * 8, 8), 8)
            br, bi = xr_ref[rows, :], xi_ref[rows, :]
            for s in (1, 2, 4):
                pr, pi = pw[s - 1]
                sr = jnp.where(rid >= s, pltpu.roll(br, s, 0), 0.0)
                si = jnp.where(rid >= s, pltpu.roll(bi, s, 0), 0.0)
                br, bi = br + pr * sr - pi * si, bi + pr * si + pi * sr
            br, bi = br + tr_ * cr - ti_ * ci, bi + tr_ * ci + ti_ * cr
            xr_ref[rows, :] = br
            xi_ref[rows, :] = bi
            return br[7:8], bi[7:8]

        z = jnp.zeros((1, 512), f32)
        lax.fori_loop(0, L // 8, group, (z, z), unroll=2)
        wcr_v, wci_v, dv = wcr_ref[...], wci_ref[...], d_ref[...]
        for r in range(L // RC):
            rows = pl.ds(r * RC, RC)
            y_ref[rows, :] = (dot_nn(xr_ref[rows, :], wcr_v) - dot_nn(xi_ref[rows, :], wci_v)
                              + dv * u_ref[rows, :])

    wspec = pl.BlockSpec((512, 128), lambda j: (j, 0))
    aspec = pl.BlockSpec((1, 512), lambda j: (0, j))
    return pcall(
        body, plan, grid=(NT5,),
        in_specs=[pl.BlockSpec((L, 128), lambda j: (0, j)), wspec, wspec, wspec, wspec, aspec, aspec,
                  pl.BlockSpec((1, 128), lambda j: (0, j))],
        out_specs=[pl.BlockSpec((L, 512), lambda j: (0, j)), pl.BlockSpec((L, 512), lambda j: (0, j)),
                   pl.BlockSpec((L, 128), lambda j: (0, j))],
        out_shape=[S((L, NST), f32), S((L, NST), f32), S((L, S5W), f32)],
        sem=("parallel",), name="s5_scan_fwd", args=[proj, wbr, wbi, wcr, wci, abr, abi, drow])


def s5_scan_bwd(dy, proj, xs_re, xs_im, wbr, wbi, wcr, wci, abr, abi, drow, plan=None):
    def body(dy_ref, u_ref, xr_ref, xi_ref, wbr_ref, wbi_ref, wcr_ref, wci_ref, ar_ref, ai_ref, d_ref,
             du_ref, gwbr_ref, gwbi_ref, gwcr_ref, gwci_ref, gar_ref, gai_ref, gd_ref, lr_ref, li_ref):
        wcr_v, wci_v = wcr_ref[...], wci_ref[...]
        gwcr = jnp.zeros((512, 128), f32)
        gwci = jnp.zeros((512, 128), f32)
        gd = jnp.zeros((1, 128), f32)
        for r in range(L // RC):
            rows = pl.ds(r * RC, RC)
            dyv = dy_ref[rows, :]
            lr_ref[rows, :] = dot_nt(dyv, wcr_v)
            li_ref[rows, :] = -dot_nt(dyv, wci_v)
            gwcr += dot_tn(xr_ref[rows, :], dyv)
            gwci -= dot_tn(xi_ref[rows, :], dyv)
            gd += jnp.sum(dyv * u_ref[rows, :], axis=0, keepdims=True)
        gwcr_ref[...] = gwcr
        gwci_ref[...] = gwci
        gd_ref[...] = gd
        pw = _cpowers(ar_ref[...], -ai_ref[...])
        rid = lax.broadcasted_iota(jnp.int32, (8, 512), 0)
        tr_, ti_ = _ctable(pw, rid, lambda r: 8 - r)

        def group(i, c):
            cr, ci, gar, gai = c
            j = L // 8 - 1 - i
            rows = pl.ds(pl.multiple_of(j * 8, 8), 8)
            br, bi = lr_ref[rows, :], li_ref[rows, :]
            for s in (1, 2, 4):
                pr, pi = pw[s - 1]
                sr = jnp.where(rid < 8 - s, pltpu.roll(br, 8 - s, 0), 0.0)
                si = jnp.where(rid < 8 - s, pltpu.roll(bi, 8 - s, 0), 0.0)
                br, bi = br + pr * sr - pi * si, bi + pr * si + pi * sr
            br, bi = br + tr_ * cr - ti_ * ci, bi + tr_ * ci + ti_ * cr
            lr_ref[rows, :] = br
            li_ref[rows, :] = bi
            nr = jnp.where(rid < 7, pltpu.roll(br, 7, 0), cr)
            ni = jnp.where(rid < 7, pltpu.roll(bi, 7, 0), ci)
            xr, xi = xr_ref[rows, :], xi_ref[rows, :]
            return br[0:1], bi[0:1], gar + xr * nr + xi * ni, gai + xr * ni - xi * nr

        z = jnp.zeros((1, 512), f32)
        z8 = jnp.zeros((8, 512), f32)
        _, _, gar, gai = lax.fori_loop(0, L // 8, group, (z, z, z8, z8), unroll=2)
        gar_ref[...] = jnp.sum(gar, axis=0, keepdims=True)
        gai_ref[...] = jnp.sum(gai, axis=0, keepdims=True)
        wbr_v, wbi_v, dv = wbr_ref[...], wbi_ref[...], d_ref[...]
        gwbr = jnp.zeros((512, 128), f32)
        gwbi = jnp.zeros((512, 128), f32)
        for r in range(L // RC):
            rows = pl.ds(r * RC, RC)
            lrv, liv, uv = lr_ref[rows, :], li_ref[rows, :], u_ref[rows, :]
            du_ref[rows, :] = (dot_nn(lrv, wbr_v) + dot_nn(liv, wbi_v) + dv * dy_ref[rows, :]).astype(du_ref.dtype)
            gwbr += dot_tn(lrv, uv)
            gwbi += dot_tn(liv, uv)
        gwbr_ref[...] = gwbr
        gwbi_ref[...] = gwbi

    wspec = pl.BlockSpec((512, 128), lambda j: (j, 0))
    aspec = pl.BlockSpec((1, 512), lambda j: (0, j))
    col = pl.BlockSpec((L, 128), lambda j: (0, j))
    st = pl.BlockSpec((L, 512), lambda j: (0, j))
    dspec = pl.BlockSpec((1, 128), lambda j: (0, j))
    return pcall(
        body, plan, grid=(NT5,),
        in_specs=[col, col, st, st, wspec, wspec, wspec, wspec, aspec, aspec, dspec],
        out_specs=[col, wspec, wspec, wspec, wspec, aspec, aspec, dspec],
        out_shape=[S((L, S5W), BF)] + [S((NST, 128), f32)] * 4 + [S((1, NST), f32)] * 2 + [S((1, S5W), f32)],
        scratch_shapes=[pltpu.VMEM((L, 512), f32), pltpu.VMEM((L, 512), f32)],
        sem=("parallel",), name="s5_scan_bwd", args=[dy, proj, xs_re, xs_im, wbr, wbi, wcr, wci, abr, abi, drow])


def _glu(y, w, b):
    z = jax.nn.gelu(y)
    return z * jax.nn.sigmoid(dot_nn(z, w) + b)


def s5_glu_fwd(y, w, b):
    def body(y_ref, w_ref, b_ref, o_ref):
        o_ref[...] = _glu(y_ref[...], w_ref[...], b_ref[...]).astype(o_ref.dtype)

    return pl.pallas_call(
        body, grid=(L // TR,),
        in_specs=[pl.BlockSpec((TR, S5W), lambda i: (i, 0)), pl.BlockSpec((S5W, S5W), lambda i: (0, 0)),
                  pl.BlockSpec((1, S5W), lambda i: (0, 0))],
        out_specs=pl.BlockSpec((TR, S5W), lambda i: (i, 0)), out_shape=S((L, S5W), BF),
        compiler_params=_cp(("parallel",)), name="s5_glu_fwd")(y, w, b)


def s5_glu_bwd(y, w, b, dmix):
    def body(y_ref, w_ref, b_ref, g_ref, dy_ref, dw_ref, db_ref):
        _, vjp = jax.vjp(_glu, y_ref[...], w_ref[...].astype(f32), b_ref[...])
        dy, dw, db = vjp(g_ref[...])
        dy_ref[...] = dy

        @pl.when(pl.program_id(0) == 0)
        def _():
            dw_ref[...] = jnp.zeros_like(dw_ref)
            db_ref[...] = jnp.zeros_like(db_ref)

        dw_ref[...] += dw
        db_ref[...] += db

    row = pl.BlockSpec((TR, S5W), lambda i: (i, 0))
    return pl.pallas_call(
        body, grid=(L // TR,),
        in_specs=[row, pl.BlockSpec((S5W, S5W), lambda i: (0, 0)), pl.BlockSpec((1, S5W), lambda i: (0, 0)), row],
        out_specs=[row, pl.BlockSpec((S5W, S5W), lambda i: (0, 0)), pl.BlockSpec((1, S5W), lambda i: (0, 0))],
        out_shape=[S((L, S5W), f32), S((S5W, S5W), f32), S((1, S5W), f32)],
        compiler_params=_cp(("arbitrary",)), name="s5_glu_bwd")(y, w, b, dmix)


def _dg3(a, b, ca, cb):
    ah, bh = a.astype(BF), b.astype(BF)
    al, bl = (a - ah.astype(f32)).astype(BF), (b - bh.astype(f32)).astype(BF)
    return _dg(ah, bh, ca, cb) + _dg(ah, bl, ca, cb) + _dg(al, bh, ca, cb)


@jax.custom_vjp
def hi_nn(a, b):
    return _dg3(a, b, 1, 0)


@jax.custom_vjp
def hi_nt(a, b):
    return _dg3(a, b, 1, 1)


@jax.custom_vjp
def hi_tn(a, b):
    return _dg3(a, b, 0, 0)


hi_nn.defvjp(lambda a, b: (hi_nn(a, b), (a, b)), lambda r, g: (hi_nt(g, r[1]), hi_tn(r[0], g)))
hi_nt.defvjp(lambda a, b: (hi_nt(a, b), (a, b)), lambda r, g: (hi_nn(g, r[1]), hi_tn(g, r[0])))
hi_tn.defvjp(lambda a, b: (hi_tn(a, b), (a, b)), lambda r, g: (hi_nt(r[1], g), hi_nn(r[0], g)))


def _hgrn_chunk(St, xq, xf, xi, xg, gam, ng):
    lb = jax.nn.sigmoid(gam[0:1] - gam[1:2])
    q = jax.nn.silu(xq)
    f = lb + (1.0 - lb) * jax.nn.sigmoid(xf)
    k = 1.0 - f
    g = jnp.log(f)
    ti = lax.broadcasted_iota(jnp.int32, (HGC, HGC), 0)
    si = lax.broadcasted_iota(jnp.int32, (HGC, HGC), 1)
    causal = si <= ti
    b = jnp.dot(causal.astype(f32), g, precision=HI, preferred_element_type=f32)
    qe = q * jnp.exp(b)
    o = dot_nt(qe, St)
    parts = []
    for i in range(HGC // 16):
        r, n = slice(16 * i, 16 * i + 16), 16 * i + 16
        base = b[16 * i - 1:16 * i] if i > 0 else jnp.zeros((1, b.shape[1]), f32)
        sc = hi_nt(q[r] * jnp.exp(b[r] - base), k[:n] * jnp.exp(base - b[:n]))
        parts.append(dot_nn(jnp.where(causal[r, :n], sc, 0.0), xi[:n]))
    o = o + jnp.concatenate(parts, axis=0)
    bl = b[HGC - 1:HGC]
    St_new = St * jnp.exp(bl) + dot_tn(xi, k * jnp.exp(bl - b))
    o = o * lax.rsqrt(jnp.mean(o * o, axis=-1, keepdims=True) + EPS) * ng
    return St_new, o * jax.nn.silu(xg)


NCH = L // HGC


def hgrn_fwd(proj, gamma, hnorm, plan=None):
    def body(q_ref, f_ref, i_ref, g_ref, gam_ref, ng_ref, o_ref, ss_ref, st):
        @pl.when(pl.program_id(0) == 0)
        def _():
            st[...] = jnp.zeros_like(st)

        for h in range(4):
            sl = slice(h * 128, (h + 1) * 128)
            s0 = st[h]
            ss_ref[0, h] = s0
            s1, o = _hgrn_chunk(s0, q_ref[:, sl], f_ref[:, sl], i_ref[:, sl], g_ref[:, sl], gam_ref[:, sl], ng_ref[:, sl])
            st[h] = s1
            o_ref[:, sl] = o.astype(o_ref.dtype)

    def pj(n):
        return pl.BlockSpec((HGC, 512), lambda c: (c, n))

    return pcall(
        body, plan, grid=(NCH,),
        in_specs=[pj(1), pj(2), pj(3), pj(4), pl.BlockSpec((2, 512), lambda c: (0, 0)), pl.BlockSpec((1, 512), lambda c: (0, 0))],
        out_specs=[pl.BlockSpec((HGC, 512), lambda c: (c, 0)), pl.BlockSpec((1, 4, 128, 128), lambda c: (c, 0, 0, 0))],
        out_shape=[S((L, 512), BF), S((NCH, 4, 128, 128), f32)],
        scratch_shapes=[pltpu.VMEM((4, 128, 128), f32)],
        sem=("arbitrary",), name="hgrn_fwd", args=[proj, proj, proj, proj, gamma, hnorm])


def hgrn_bwd(proj, gamma, hnorm, ssave, dmix, du, plan=None):
    def body(q_ref, f_ref, i_ref, g_ref, gam_ref, ng_ref, ss_ref, do_ref, du_ref, dp_ref, dgam_ref, dng_ref, dst):
        @pl.when(pl.program_id(0) == 0)
        def _():
            dst[...] = jnp.zeros_like(dst)
            dgam_ref[...] = jnp.zeros_like(dgam_ref)
            dng_ref[...] = jnp.zeros_like(dng_ref)

        dp_ref[:, 0:512] = du_ref[...]
        for h in range(4):
            sl = slice(h * 128, (h + 1) * 128)
            _, vjp = jax.vjp(_hgrn_chunk, ss_ref[0, h], q_ref[:, sl], f_ref[:, sl], i_ref[:, sl], g_ref[:, sl],
                             gam_ref[:, sl], ng_ref[:, sl])
            ds, dq, df, di, dg, dgam, dng = vjp((dst[h], do_ref[:, sl]))
            dst[h] = ds
            for n, v in enumerate((dq, df, di, dg)):
                dp_ref[:, 512 * (n + 1) + h * 128: 512 * (n + 1) + (h + 1) * 128] = v.astype(dp_ref.dtype)
            dgam_ref[:, sl] += dgam
            dng_ref[:, sl] += dng

    def pj(n):
        return pl.BlockSpec((HGC, 512), lambda i: (NCH - 1 - i, n))

    return pcall(
        body, plan, grid=(NCH,),
        in_specs=[pj(1), pj(2), pj(3), pj(4), pl.BlockSpec((2, 512), lambda i: (0, 0)), pl.BlockSpec((1, 512), lambda i: (0, 0)),
                  pl.BlockSpec((1, 4, 128, 128), lambda i: (NCH - 1 - i, 0, 0, 0)), pj(1), pj(0)],
        out_specs=[pl.BlockSpec((HGC, 2560), lambda i: (NCH - 1 - i, 0)), pl.BlockSpec((2, 512), lambda i: (0, 0)),
                   pl.BlockSpec((1, 512), lambda i: (0, 0))],
        out_shape=[S((L, 2560), BF), S((2, 512), f32), S((1, 512), f32)],
        scratch_shapes=[pltpu.VMEM((4, 128, 128), f32)],
        sem=("arbitrary",), name="hgrn_bwd", args=[proj, proj, proj, proj, gamma, hnorm, ssave, dmix, du])


def _earlier(h_ref, k, r0, n):
    if r0 > 0:
        return h_ref[pl.ds(r0 - k, n), :]
    rid = lax.broadcasted_iota(jnp.int32, (8, h_ref.shape[1]), 0)
    head = jnp.where(rid >= k, pltpu.roll(h_ref[pl.ds(0, 8), :], k, 0), 0.0)
    return jnp.concatenate([head, h_ref[pl.ds(8 - k, n - 8), :]], axis=0)


def _conv3_rows(h_ref, w, b, r0, n=None):
    n = CR if n is None else n
    h1, h2 = _earlier(h_ref, 1, r0, n), _earlier(h_ref, 2, r0, n)
    return w[2:3] * h_ref[pl.ds(r0, n), :] + w[1:2] * h1 + w[0:1] * h2 + b, h1, h2


CT = 128
NCT = DFF // CT
CR = 64


def convact_fwd(hu, cw, cb, layer, plan=None):
    def body(ha_ref, hb_ref, wa_ref, wb_ref, ba_ref, bb_ref, o_ref):
        ca = _conv3_rows(ha_ref, wa_ref[...], ba_ref[...], 0, L)[0]
        cb_ = _conv3_rows(hb_ref, wb_ref[...], bb_ref[...], 0, L)[0]
        o_ref[...] = (jax.nn.silu(ca) * cb_).astype(o_ref.dtype)

    def h(off):
        return pl.BlockSpec((L, CT), lambda j: (0, j + off))

    def w(off):
        return pl.BlockSpec((3, CT), lambda j: (0, j + off))

    def b(off):
        return pl.BlockSpec((None, 1, CT), lambda j: (layer, 0, j + off))

    return pcall(body, plan, grid=(NCT,), in_specs=[h(0), h(NCT), w(0), w(NCT), b(0), b(NCT)],
                 out_specs=pl.BlockSpec((L, CT), lambda j: (0, j)), out_shape=S((L, DFF), BF),
                 sem=("parallel",), name=f"convact_fwd{layer}", args=[hu, hu, cw, cw, cb, cb])


def convact_bwd(hu, cw, cb, dact, layer, plan=None):
    def body(ha_ref, hb_ref, wa_ref, wb_ref, ba_ref, bb_ref, g_ref, dh_ref, dw_ref, db_ref, sh, sw, sb, da_scr, db_scr):
        j = pl.program_id(0)

        def fold(x):
            return functools.reduce(jnp.add, [x[8 * m:8 * m + 8] for m in range(CR // 8)])

        @pl.when(j < NCT)
        def _():
            wa, wb, ba, bb = wa_ref[...], wb_ref[...], ba_ref[...], bb_ref[...]
            da_scr[pl.ds(L, 8), :] = jnp.zeros((8, CT), f32)
            db_scr[pl.ds(L, 8), :] = jnp.zeros((8, CT), f32)
            acc = [jnp.zeros((8, CT), f32) for _ in range(8)]
            for c in range(L // CR):
                r0 = c * CR
                ca, a1, a2 = _conv3_rows(ha_ref, wa, ba, r0)
                cb_, b1, b2 = _conv3_rows(hb_ref, wb, bb, r0)
                g = g_ref[pl.ds(r0, CR), :].astype(f32)
                sg = jax.nn.sigmoid(ca)
                dca = g * cb_ * (sg * (1.0 + ca * (1.0 - sg)))
                dcb = g * (ca * sg)
                da_scr[pl.ds(r0, CR), :] = dca
                db_scr[pl.ds(r0, CR), :] = dcb
                terms = (dca * a2, dca * a1, dca * ha_ref[pl.ds(r0, CR), :], dca,
                         dcb * b2, dcb * b1, dcb * hb_ref[pl.ds(r0, CR), :], dcb)
                acc = [a + fold(t) for a, t in zip(acc, terms)]
            rows = [jnp.sum(a, axis=0, keepdims=True) for a in acc]
            for k in range(3):
                dw_ref[k:k + 1, :] = rows[k]
                sw[j, k:k + 1, :] = rows[4 + k]
            db_ref[...] = rows[3]
            sb[j] = rows[7]
            for c in range(L // CR):
                r0 = c * CR
                for scr, w, out in ((da_scr, wa, dh_ref), (db_scr, wb, sh.at[j])):
                    dh = (w[2:3] * scr[pl.ds(r0, CR), :] + w[1:2] * scr[pl.ds(r0 + 1, CR), :]
                          + w[0:1] * scr[pl.ds(r0 + 2, CR), :])
                    out[pl.ds(r0, CR), :] = dh.astype(out.dtype)

        @pl.when(j >= NCT)
        def _():
            dh_ref[...] = sh[j - NCT]
            dw_ref[...] = sw[j - NCT]
            db_ref[...] = sb[j - NCT]

    def lo(j):
        return jnp.minimum(j, NCT - 1)

    in_specs = [pl.BlockSpec((L, CT), lambda j: (0, lo(j))), pl.BlockSpec((L, CT), lambda j: (0, lo(j) + NCT)),
                pl.BlockSpec((3, CT), lambda j: (0, lo(j))), pl.BlockSpec((3, CT), lambda j: (0, lo(j) + NCT)),
                pl.BlockSpec((None, 1, CT), lambda j: (layer, 0, lo(j))), pl.BlockSpec((None, 1, CT), lambda j: (layer, 0, lo(j) + NCT)),
                pl.BlockSpec((L, CT), lambda j: (0, lo(j)))]
    return pcall(
        body, plan, grid=(2 * NCT,), in_specs=in_specs,
        out_specs=[pl.BlockSpec((L, CT), lambda j: (0, j)), pl.BlockSpec((3, CT), lambda j: (0, j)), pl.BlockSpec((1, CT), lambda j: (0, j))],
        out_shape=[S((L, 2 * DFF), BF), S((3, 2 * DFF), f32), S((1, 2 * DFF), f32)],
        scratch_shapes=[pltpu.VMEM((NCT, L, CT), BF), pltpu.VMEM((NCT, 3, CT), f32), pltpu.VMEM((NCT, 1, CT), f32),
                        pltpu.VMEM((L + 8, CT), f32), pltpu.VMEM((L + 8, CT), f32)],
        sem=("arbitrary",), name=f"convact_bwd{layer}", args=[hu, hu, cw, cw, cb, cb, dact])


DILS = (1, 4, 16)
AB = 128
NPAIR = 12


def _rope_tables(pos_ref, invf_ref):
    ang = pos_ref[...].astype(f32) * invf_ref[...]
    lane = lax.broadcasted_iota(jnp.int32, (1, 128), 1) % 64
    cosf = jnp.where(lane < 16, jnp.cos(ang), 1.0)
    sn = jnp.sin(ang)
    s_lo = jnp.where(lane < 8, -sn, 0.0)
    s_hi = jnp.where((lane >= 8) & (lane < 16), sn, 0.0)
    return cosf, s_lo, s_hi


def _rope(t, cosf, s_lo, s_hi):
    return t * cosf + pltpu.roll(t, 120, 1) * s_lo + pltpu.roll(t, 8, 1) * s_hi


def _rope_t(g, cosf, s_lo, s_hi):
    return g * cosf + pltpu.roll(g * s_lo, 8, 1) + pltpu.roll(g * s_hi, 120, 1)


def _att_block(q2, kp, kc, vp, vc, first):
    lane = lax.broadcasted_iota(jnp.int32, (1, 128), 1)
    qi = lax.broadcasted_iota(jnp.int32, (AB, 2 * AB), 0) + AB
    kj = lax.broadcasted_iota(jnp.int32, (AB, 2 * AB), 1)
    back = qi - kj
    valid = (back >= 0) & (back <= AB)
    if first:
        valid = valid & (kj >= AB)
    kk = jnp.concatenate([kp, kc], axis=0)
    vv = jnp.concatenate([vp, vc], axis=0)
    o2 = jnp.zeros((AB, 128), f32)
    lse2 = jnp.zeros((AB, 128), f32)
    for e in range(2):
        hm = ((lane >= 64 * e) & (lane < 64 * (e + 1))).astype(f32)
        s = dot_nt(q2 * (hm * 0.125), kk)
        s = jnp.where(valid, s, -jnp.inf)
        m = jnp.max(s, axis=-1, keepdims=True)
        p = jnp.exp(s - m)
        den = jnp.sum(p, axis=-1, keepdims=True)
        o2 = o2 + dot_nn(p, vv * hm) / den
        lse2 = lse2 + (m + jnp.log(den)) * hm
    return o2, lse2


def _att_blocks(dil):
    m = L // dil
    return [(r * m + n * AB, n == 0) for r in range(dil) for n in range(m // AB)]


def deinterleave(x, dil):
    return x if dil == 1 else x.reshape(L // dil, dil, x.shape[1]).swapaxes(0, 1).reshape(L, x.shape[1])


def attn_fwd(qkv, pos, invf, g, plan=None):
    blocks = _att_blocks(DILS[g])

    def body(q_ref, k_ref, v_ref, pos_ref, invf_ref, o_ref, l_ref, qr, kr):
        cosf, s_lo, s_hi = _rope_tables(pos_ref, invf_ref)
        qr[...] = _rope(q_ref[...], cosf, s_lo, s_hi)
        kr[...] = _rope(k_ref[...], cosf, s_lo, s_hi)
        for off, first in blocks:
            cur, prv = pl.ds(off, AB), pl.ds(off if first else off - AB, AB)
            o2, lse2 = _att_block(qr[cur, :], kr[prv, :], kr[cur, :], v_ref[prv, :], v_ref[cur, :], first)
            o_ref[cur, :] = o2
            l_ref[cur, :] = lse2

    def sec(n):
        return pl.BlockSpec((L, 128), lambda p: (0, p + 4 * n))

    return pcall(
        body, plan, grid=(4,),
        in_specs=[sec(0), sec(1), sec(2), pl.BlockSpec((L, 1), lambda p: (0, 0)), pl.BlockSpec((1, 128), lambda p: (0, 0))],
        out_specs=[sec(0), sec(0)], out_shape=[S((L, 512), f32), S((L, 512), f32)],
        scratch_shapes=[pltpu.VMEM((L, 128), f32), pltpu.VMEM((L, 128), f32)],
        sem=("parallel",), name=f"attn_fwd{g}", args=[qkv, qkv, qkv, pos, invf])


def _att_block_bwd(q2, kp, kc, vp, vc, lse2, do2, dl2, first):
    lane = lax.broadcasted_iota(jnp.int32, (1, 128), 1)
    qi = lax.broadcasted_iota(jnp.int32, (AB, 2 * AB), 0) + AB
    kj = lax.broadcasted_iota(jnp.int32, (AB, 2 * AB), 1)
    back = qi - kj
    valid = (back >= 0) & (back <= AB)
    if first:
        valid = valid & (kj >= AB)
    kk = jnp.concatenate([kp, kc], axis=0)
    vv = jnp.concatenate([vp, vc], axis=0)
    dq2 = jnp.zeros((AB, 128), f32)
    dkk = jnp.zeros((2 * AB, 128), f32)
    dvv = jnp.zeros((2 * AB, 128), f32)
    for e in range(2):
        hb = (lane >= 64 * e) & (lane < 64 * (e + 1))
        hm = hb.astype(f32)
        qs = q2 * (hm * 0.125)
        lse = jnp.max(jnp.where(hb, lse2, -jnp.inf), axis=-1, keepdims=True)
        dls = jnp.sum(dl2 * hm, axis=-1, keepdims=True)
        p = jnp.where(valid, jnp.exp(dot_nt(qs, kk) - lse), 0.0)
        dov = do2 * hm
        dp = dot_nt(dov, vv)
        ds = p * (dp - jnp.sum(p * dp, axis=-1, keepdims=True) + dls)
        dq2 = dq2 + dot_nn(ds, kk) * (hm * 0.125)
        dkk = dkk + dot_tn(ds, qs)
        dvv = dvv + dot_tn(p, dov)
    return dq2, dkk[:AB], dkk[AB:], dvv[:AB], dvv[AB:]


def attn_bwd(qkv, pos, invf, lse, do, dl, g, plan=None):
    blocks = _att_blocks(DILS[g])

    def body(q_ref, k_ref, v_ref, pos_ref, invf_ref, l_ref, do_ref, dl_ref, d_ref, qr, kr, dqr, dkr, dvr):
        cosf, s_lo, s_hi = _rope_tables(pos_ref, invf_ref)
        qr[...] = _rope(q_ref[...], cosf, s_lo, s_hi)
        kr[...] = _rope(k_ref[...], cosf, s_lo, s_hi)
        for off, first in blocks:
            cur, prv = pl.ds(off, AB), pl.ds(off if first else off - AB, AB)
            dq2, dkp, dkc, dvp, dvc = _att_block_bwd(qr[cur, :], kr[prv, :], kr[cur, :], v_ref[prv, :], v_ref[cur, :],
                                                     l_ref[cur, :], do_ref[cur, :], dl_ref[cur, :], first)
            dqr[cur, :] = dq2
            dkr[cur, :] = dkc
            dvr[cur, :] = dvc
            if not first:
                dkr[prv, :] += dkp
                dvr[prv, :] += dvp
        d_ref[0] = _rope_t(dqr[...], cosf, s_lo, s_hi).astype(d_ref.dtype)
        d_ref[1] = _rope_t(dkr[...], cosf, s_lo, s_hi).astype(d_ref.dtype)
        d_ref[2] = dvr[...].astype(d_ref.dtype)

    def sec(n):
        return pl.BlockSpec((L, 128), lambda p: (0, p + 4 * n))

    return pcall(
        body, plan, grid=(4,),
        in_specs=[sec(0), sec(1), sec(2), pl.BlockSpec((L, 1), lambda p: (0, 0)), pl.BlockSpec((1, 128), lambda p: (0, 0)),
                  sec(0), sec(0), sec(0)],
        out_specs=pl.BlockSpec((3, L, 128), lambda p: (0, 0, p)), out_shape=S((3, L, 512), BF),
        scratch_shapes=[pltpu.VMEM((L, 128), f32)] * 5,
        sem=("parallel",), name=f"attn_bwd{g}", args=[qkv, qkv, qkv, pos, invf, lse, do, dl])


def _merge(o0, o1, o2, l0, l1, l2):
    m = jnp.maximum(jnp.maximum(l0, l1), l2)
    e0, e1, e2 = jnp.exp(l0 - m), jnp.exp(l1 - m), jnp.exp(l2 - m)
    return (e0 * o0 + e1 * o1 + e2 * o2) / (e0 + e1 + e2)


def _to_token_major(src_ref, scr, i, dil, slab):
    n = TR // dil
    for r in range(dil):
        rows = pl.ds(pl.multiple_of(r * (L // dil) + i * n, n), n)
        scr[pl.ds(r, n, stride=dil), :] = src_ref[rows, slab * 128:(slab + 1) * 128].astype(f32)
    return scr[...]


def _to_class_major(val, dst_ref, scr, i, dil, slab):
    n = TR // dil
    scr[...] = val
    for r in range(dil):
        rows = pl.ds(pl.multiple_of(r * (L // dil) + i * n, n), n)
        dst_ref[rows, slab * 128:(slab + 1) * 128] = scr[pl.ds(r, n, stride=dil), :].astype(dst_ref.dtype)


def rms_fwd_classes(x, g, name):
    def body(x_ref, g_ref, o_ref, o1_ref, o2_ref, scr):
        i = pl.program_id(0)
        y = _rms(x_ref[...], g_ref[...])
        o_ref[...] = y.astype(o_ref.dtype)
        for s in range(D // 128):
            ys = y[:, s * 128:(s + 1) * 128]
            _to_class_major(ys, o1_ref, scr, i, DILS[1], s)
            _to_class_major(ys, o2_ref, scr, i, DILS[2], s)

    row = pl.BlockSpec((TR, D), lambda i: (i, 0))
    full = pl.BlockSpec((L, D), lambda i: (0, 0))
    return pl.pallas_call(
        body, grid=(L // TR,), in_specs=[row, pl.BlockSpec((1, D), lambda i: (0, 0))], out_specs=[row, full, full],
        out_shape=[S((L, D), BF)] * 3, scratch_shapes=[pltpu.VMEM((TR, 128), f32)],
        compiler_params=_cp(("arbitrary",)), name=name)(x, g)


def rms_bwd_classes(x, g, dy0, dyc, dres, name, plan=None):
    def body(x_ref, g_ref, dy0_ref, d1_ref, d2_ref, dr_ref, dh_ref, dg_ref, scr, dyf):
        i = pl.program_id(0)
        for s in range(D // 128):
            sl = slice(s * 128, (s + 1) * 128)
            dyf[:, sl] = (dy0_ref[:, sl] + _to_token_major(d1_ref, scr.at[0], i, DILS[1], s)
                          + _to_token_major(d2_ref, scr.at[1], i, DILS[2], s))
        _, vjp = jax.vjp(_rms, x_ref[...], g_ref[...])
        dx, dg = vjp(dyf[...])
        dh_ref[...] = dr_ref[...] + dx

        @pl.when(i == 0)
        def _():
            dg_ref[...] = jnp.zeros_like(dg_ref)

        dg_ref[...] += dg

    row = pl.BlockSpec((TR, D), lambda i: (i, 0))
    vec = pl.BlockSpec((1, D), lambda i: (0, 0))
    full = pl.BlockSpec((L, D), lambda i: (0, 0))
    return pcall(body, plan, grid=(L // TR,), in_specs=[row, vec, row, full, full, row], out_specs=[row, vec],
                 out_shape=[S((L, D), f32), S((1, D), f32)],
                 scratch_shapes=[pltpu.VMEM((2, TR, 128), f32), pltpu.VMEM((TR, D), f32)],
                 sem=("arbitrary",), name=name, args=[x, g, dy0, dyc[0], dyc[1], dres])


def attn_merge_fwd(o0, l0, oc, lc, plan=None):
    def body(o0_ref, l0_ref, o1_ref, l1_ref, o2_ref, l2_ref, o_ref, scr):
        i = pl.program_id(0)
        for s in range(4):
            sl = slice(s * 128, (s + 1) * 128)
            o1 = _to_token_major(o1_ref, scr.at[0], i, DILS[1], s)
            l1 = _to_token_major(l1_ref, scr.at[1], i, DILS[1], s)
            o2 = _to_token_major(o2_ref, scr.at[2], i, DILS[2], s)
            l2 = _to_token_major(l2_ref, scr.at[3], i, DILS[2], s)
            o_ref[:, sl] = _merge(o0_ref[:, sl], o1, o2, l0_ref[:, sl], l1, l2).astype(o_ref.dtype)

    blk = pl.BlockSpec((TR, 512), lambda i: (i, 0))
    full = pl.BlockSpec((L, 512), lambda i: (0, 0))
    return pcall(body, plan, grid=(L // TR,), in_specs=[blk, blk, full, full, full, full], out_specs=blk,
                 out_shape=S((L, 512), BF), scratch_shapes=[pltpu.VMEM((4, TR, 128), f32)],
                 sem=("arbitrary",), name="attn_merge_fwd", args=[o0, l0, oc[0], lc[0], oc[1], lc[1]])


def attn_merge_bwd(o0, l0, oc, lc, do, plan=None):
    def body(o0_ref, l0_ref, o1_ref, l1_ref, o2_ref, l2_ref, g_ref, do0, dl0, do1, dl1, do2, dl2, scr):
        i = pl.program_id(0)
        for s in range(4):
            sl = slice(s * 128, (s + 1) * 128)
            o1 = _to_token_major(o1_ref, scr.at[0], i, DILS[1], s)
            l1 = _to_token_major(l1_ref, scr.at[1], i, DILS[1], s)
            o2 = _to_token_major(o2_ref, scr.at[2], i, DILS[2], s)
            l2 = _to_token_major(l2_ref, scr.at[3], i, DILS[2], s)
            _, vjp = jax.vjp(_merge, o0_ref[:, sl], o1, o2, l0_ref[:, sl], l1, l2)
            g0, g1, g2, h0, h1, h2 = vjp(g_ref[:, sl].astype(f32))
            do0[:, sl] = g0.astype(do0.dtype)
            dl0[:, sl] = h0
            _to_class_major(g1, do1, scr.at[0], i, DILS[1], s)
            _to_class_major(h1, dl1, scr.at[1], i, DILS[1], s)
            _to_class_major(g2, do2, scr.at[2], i, DILS[2], s)
            _to_class_major(h2, dl2, scr.at[3], i, DILS[2], s)

    blk = pl.BlockSpec((TR, 512), lambda i: (i, 0))
    full = pl.BlockSpec((L, 512), lambda i: (0, 0))
    outs = pcall(body, plan, grid=(L // TR,), in_specs=[blk, blk, full, full, full, full, blk],
                 out_specs=[blk, blk, full, full, full, full],
                 out_shape=[S((L, 512), BF), S((L, 512), f32)] * 3, scratch_shapes=[pltpu.VMEM((4, TR, 128), f32)],
                 sem=("arbitrary",), name="attn_merge_bwd", args=[o0, l0, oc[0], lc[0], oc[1], lc[1], do])
    return [outs[0], outs[2], outs[4]], [outs[1], outs[3], outs[5]]


def _invf_lanes():
    half = 8
    inv = ROPE_THETA ** (-np.arange(half, dtype=np.float32) * 2.0 / 16.0)
    lane = np.arange(128) % 64
    return jnp.asarray(np.where(lane < 16, inv[lane % 8], 0.0).astype(np.float32)[None, :])


def hosted(C, host, fn):
    p = C.plan(host) if C is not None else None
    out = fn(p)
    if p is not None:
        C.done(p)
    return out


def _ffn_fwd(h, g_row, W, cb, layer, C):
    hn = rms_fwd(h, g_row, f"rms_ffn{layer}")
    hu = hosted(C, f"ffn_in{layer}", lambda p: matmul(hn, W[("ffn_w_in", layer)], mode="nn", tm=1024, tn=1408, tk=1024,
                                                      plan=p, name=f"ffn_in{layer}"))
    act = hosted(C, f"convact_fwd{layer}", lambda p: convact_fwd(hu, W[("ffn_conv_w", layer)], cb, layer, plan=p))
    h2 = hosted(C, f"ffn_out{layer}", lambda p: matmul(act, W[("ffn_w_out", layer)], mode="nn", tm=1024, tn=1024, tk=2816,
                                                       add=h, plan=p, name=f"ffn_out{layer}"))
    return h2, (hn, hu, act)


def _ffn_bwd(dh, h, g_row, W, cb, saved, layer, C, G):
    hn, hu, act = saved
    w_in, w_out = W[("ffn_w_in", layer)], W[("ffn_w_out", layer)]
    dact = hosted(C, f"ffn_out_dx{layer}", lambda p: matmul(dh, w_out, mode="nt", tm=1024, tn=1408, tk=1024, plan=p,
                                                          name=f"ffn_out_dx{layer}"))
    G[("ffn_w_out", layer)] = hosted(C, f"ffn_out_dw{layer}", lambda p: matmul(
        act, dh, mode="tn", tm=1408, tn=1024, tk=L, out_dtype=BF, plan=p, name=f"ffn_out_dw{layer}"))
    dhu, G[("ffn_conv_w", layer)], g_cb = hosted(
        C, f"convact_bwd{layer}", lambda p: convact_bwd(hu, W[("ffn_conv_w", layer)], cb, dact, layer, plan=p))
    dhn = hosted(C, f"ffn_in_dx{layer}", lambda p: matmul(dhu, w_in, mode="nt", tm=1024, tn=1024, tk=2816, plan=p,
                                                         name=f"ffn_in_dx{layer}"))
    G[("ffn_w_in", layer)] = hosted(C, f"ffn_in_dw{layer}", lambda p: matmul(
        hn, dhu, mode="tn", tm=1024, tn=1408, tk=L, out_dtype=BF, plan=p, name=f"ffn_in_dw{layer}"))
    dh2, g_norm = hosted(C, f"rms_ffn_bwd{layer}", lambda p: rms_bwd(h, g_row, [dhn], dh, f"rms_ffn_bwd{layer}", plan=p))
    return dh2, g_cb, g_norm


def local_step(x, pos, tgt, sm, W, C=None):
    G = C.grads if C is not None else {}
    nm, nf = sm["norm_mix"], sm["norm_ffn"]
    invf = _invf_lanes()
    are = sm["s5_A_re"].reshape(NST, 1)
    aim = sm["s5_A_im"].reshape(NST, 1)
    ldt = sm["s5_log_dt"].reshape(1, 32)
    bre = sm["s5_B_re"].reshape(NST, 16)
    bim = sm["s5_B_im"].reshape(NST, 16)
    cre = jnp.swapaxes(sm["s5_C_re"][0], 1, 2).reshape(NST, 16)
    cim = jnp.swapaxes(sm["s5_C_im"][0], 1, 2).reshape(NST, 16)
    drow = sm["s5_D"].reshape(1, S5W)
    wbr, wbi, wcr, wci, abr, abi = s5_params_fwd(are, aim, ldt, bre, bim, cre, cim)
    hn0 = rms_fwd(x, nm[0:1], "rms_mix0")
    cb3 = sm["ffn_conv_b3"]
    proj = hosted(C, "mix_in", lambda p: matmul(hn0, W[("mix_w_in", 0)], mode="nn", tm=1024, tn=1280, tk=1024, plan=p, name="mix_in"))
    xs_re, xs_im, y5 = hosted(C, "s5_scan_fwd", lambda p: s5_scan_fwd(proj, wbr, wbi, wcr, wci, abr, abi, drow, plan=p))
    oa = s5_glu_fwd(y5, W[("s5_glu_w", 0)], sm["s5_glu_b"])
    ob, ssave = hosted(C, "hgrn_fwd", lambda p: hgrn_fwd(proj, sm["hgrn_gamma"], sm["hgrn_norm"], plan=p))
    cat = jnp.concatenate([oa, ob], axis=1)
    h1 = matmul(cat, W[("mix_w_out", 0)], mode="nn", tm=1024, tn=1024, tk=1024, add=x, name="mix_out")
    h2, ffn0 = _ffn_fwd(h1, nf[0:1], W, cb3, 0, C)
    hn2_g = rms_fwd_classes(h2, nm[1:2], "rms_mix1")
    wqkv = W[("att_w_qkv", 0)]
    pos_g, qkv_g, oc_g, lc_g = [], [], [], []
    for g, dil in enumerate(DILS):
        pos_g.append(deinterleave(pos, dil))
        qkv_g.append(hosted(C, f"att_qkv{g}", lambda p: matmul(
            hn2_g[g], wqkv, mode="nn", tm=1024, tn=512, tk=1024, dims=(L, 1536, D),
            b_spec=pl.BlockSpec((D, 512), lambda i, j, k, g=g: (0, 3 * j + g)), plan=p, name=f"att_qkv{g}")))
        o_c, l_c = hosted(C, f"attn_fwd{g}", lambda p: attn_fwd(qkv_g[g], pos_g[g], invf, g, plan=p))
        oc_g.append(o_c)
        lc_g.append(l_c)
    o = hosted(C, "attn_merge_fwd", lambda p: attn_merge_fwd(oc_g[0], lc_g[0], oc_g[1:], lc_g[1:], plan=p))
    h3 = matmul(o, W[("att_w_o", 0)], mode="nn", tm=1024, tn=1024, tk=512, add=h2, name="att_o")
    h4, ffn1 = _ffn_fwd(h3, nf[1:2], W, cb3, 1, C)
    loss, dh, g_nfinal = loss_head(h4, sm["norm_final"].reshape(1, D), tgt)
    dh, g_cb1, g_nf1 = _ffn_bwd(dh, h3, nf[1:2], W, cb3, ffn1, 1, C, G)
    do = matmul(dh, W[("att_w_o", 0)], mode="nt", tm=1024, tn=512, tk=1024, name="att_o_dx")
    G[("att_w_o", 0)] = matmul(o, dh, mode="tn", tm=512, tn=1024, tk=L, out_dtype=BF, name="att_o_dw")
    do_g, dl_g = hosted(C, "attn_merge_bwd", lambda p: attn_merge_bwd(oc_g[0], lc_g[0], oc_g[1:], lc_g[1:], do, plan=p))
    dhn2_g, gq = [], []
    for g, dil in enumerate(DILS):
        d3 = hosted(C, f"attn_bwd{g}", lambda p: attn_bwd(qkv_g[g], pos_g[g], invf, lc_g[g], do_g[g], dl_g[g], g, plan=p))
        dx = matmul(d3, wqkv, mode="nt", tm=1024, tn=1024, tk=512, dims=(L, D, 1536),
                    a_spec=pl.BlockSpec((None, 1024, 512), lambda i, j, k: (k, i, 0)),
                    b_spec=pl.BlockSpec((D, 512), lambda i, j, k, g=g: (0, 3 * k + g)), name=f"att_qkv_dx{g}")
        dhn2_g.append(dx)
        gq.append(matmul(hn2_g[g], d3, mode="tn", tm=1024, tn=512, tk=L, out_dtype=BF, dims=(D, 1536, L),
                         b_spec=pl.BlockSpec((None, L, 512), lambda i, j, k: (j, k, 0)), name=f"att_qkv_dw{g}"))
    G[("att_w_qkv", 0)] = jnp.concatenate([gq[g][:, 512 * s:512 * (s + 1)] for s in range(3) for g in range(3)], axis=1)
    dh, g_nm1 = hosted(C, "rms_mix_bwd1", lambda p: rms_bwd_classes(h2, nm[1:2], dhn2_g[0], dhn2_g[1:], dh, "rms_mix_bwd1", plan=p))
    dh, g_cb0, g_nf0 = _ffn_bwd(dh, h1, nf[0:1], W, cb3, ffn0, 0, C, G)
    dmix = matmul(dh, W[("mix_w_out", 0)], mode="nt", tm=1024, tn=1024, tk=1024, name="mix_out_dx")
    G[("mix_w_out", 0)] = matmul(cat, dh, mode="tn", tm=1024, tn=1024, tk=L, out_dtype=BF, name="mix_out_dw")
    dy5, g_glu_w, g_glu_b = s5_glu_bwd(y5, W[("s5_glu_w", 0)], sm["s5_glu_b"], dmix)
    G[("s5_glu_w", 0)] = g_glu_w.astype(BF)
    du, gwbr, gwbi, gwcr, gwci, gabr, gabi, g_d = hosted(C, "s5_scan_bwd", lambda p: s5_scan_bwd(
        dy5, proj, xs_re, xs_im, wbr, wbi, wcr, wci, abr, abi, drow, plan=p))
    g_are, g_aim, g_ldt, g_bre, g_bim, g_cre, g_cim = s5_params_bwd(are, aim, ldt, bre, bim, cre, cim,
                                                                   (gwbr, gwbi, gwcr, gwci, gabr, gabi))
    small = {
        "norm_ffn": jnp.concatenate([g_nf0, g_nf1], axis=0), "norm_final": g_nfinal.reshape(D),
        "s5_A_re": g_are.reshape(1, 32, 64), "s5_A_im": g_aim.reshape(1, 32, 64), "s5_log_dt": g_ldt.reshape(1, 32),
        "s5_B_re": g_bre.reshape(1, 32, 64, 16), "s5_B_im": g_bim.reshape(1, 32, 64, 16),
        "s5_C_re": jnp.swapaxes(g_cre.reshape(1, 32, 64, 16), 2, 3), "s5_C_im": jnp.swapaxes(g_cim.reshape(1, 32, 64, 16), 2, 3),
        "s5_D": g_d.reshape(1, 32, 16), "s5_glu_b": g_glu_b, "ffn_conv_b": jnp.concatenate([g_cb0, g_cb1], axis=0),
    }
    if C is not None:
        C.small["small_early"] = _pack(small, SMALL_EARLY)
    dproj, g_gamma, g_hnorm = hosted(C, "hgrn_bwd", lambda p: hgrn_bwd(proj, sm["hgrn_gamma"], sm["hgrn_norm"], ssave, dmix, du,
                                                                       plan=p))
    dhn0 = hosted(C, "mix_in_dx", lambda p: matmul(dproj, W[("mix_w_in", 0)], mode="nt", tm=1024, tn=1024, tk=2560, plan=p,
                                                  name="mix_in_dx"))
    G[("mix_w_in", 0)] = matmul(hn0, dproj, mode="tn", tm=1024, tn=1280, tk=L, out_dtype=BF, name="mix_in_dw")
    gx, g_nm0 = hosted(C, "rms_mix_bwd0", lambda p: rms_bwd(x, nm[0:1], [dhn0], dh, "rms_mix_bwd0", plan=p))
    small.update({"norm_mix": jnp.concatenate([g_nm0, g_nm1], axis=0), "hgrn_gamma": g_gamma, "hgrn_norm": g_hnorm})
    if C is not None:
        C.small["small_late"] = _pack(small, SMALL_LATE)
    return loss, gx, G, small


BIG = ("mix_w_in", "mix_w_out", "s5_glu_w", "att_w_qkv", "att_w_o", "ffn_w_in", "ffn_w_out", "ffn_conv_w")
SMALL = ("norm_mix", "norm_ffn", "norm_final", "s5_A_re", "s5_A_im", "s5_log_dt", "s5_B_re", "s5_B_im", "s5_C_re", "s5_C_im",
         "s5_D", "s5_glu_b", "hgrn_gamma", "hgrn_norm", "ffn_conv_b")
SMALL_LATE = ("norm_mix", "hgrn_gamma", "hgrn_norm")
SMALL_EARLY = tuple(n for n in SMALL if n not in SMALL_LATE)


def cast_bf16(w, name, plan=None):
    nl, r, c = w.shape
    w2 = w.reshape(nl * r, c)
    tr = 256 if (nl * r) % 256 == 0 else nl * r

    def body(w_ref, o_ref):
        o_ref[...] = w_ref[...].astype(BF)

    out = pcall(body, plan, grid=(nl * r // tr,), in_specs=[pl.BlockSpec((tr, c), lambda i: (i, 0))],
                out_specs=pl.BlockSpec((tr, c), lambda i: (i, 0)), out_shape=S((nl * r, c), BF),
                sem=("parallel",), name=name, args=[w2])
    return out.reshape(nl, r, c)


SCHEDULE = {
    "cast_ffn_w_in": [("G", "mix_w_in", 0)],
    "mix_in": [("G", "mix_w_out", 0), ("G", "s5_glu_w", 0)],
    "s5_scan_fwd": [("G", "ffn_w_in", 0, (0, 2))],
    "hgrn_fwd": [("G", "ffn_w_in", 0, (1, 2)), ("G", "ffn_conv_w", 0), ("G", "ffn_conv_w", 1), ("G", "att_w_qkv", 0, (0, 2))],
    "ffn_in0": [("G", "ffn_w_out", 0)],
    "convact_fwd0": [("G", "att_w_qkv", 0, (1, 2))],
    "att_qkv0": [("G", "att_w_o", 0)],
    "attn_fwd0": [("G", "ffn_w_in", 1, (0, 2))],
    "attn_fwd1": [("G", "ffn_w_in", 1, (1, 2))],
    "attn_fwd2": [("G", "ffn_w_out", 1)],
    "convact_bwd1": [("P", "ffn_w_out", 1)],
    "ffn_in_dx1": [("A", "ffn_w_out", 1, (0, 2))],
    "ffn_in_dw1": [("A", "ffn_w_out", 1, (1, 2))],
    "rms_ffn_bwd1": [("P", "ffn_w_in", 1)],
    "attn_merge_bwd": [("P", "att_w_o", 0), ("A", "ffn_conv_w", 1), ("B", "ffn_w_out", 1)],
    "attn_bwd0": [("A", "ffn_w_in", 1, (0, 2)), ("A", "att_w_o", 0)],
    "attn_bwd1": [("A", "ffn_w_in", 1, (1, 2)), ("B", "att_w_o", 0), ("B", "ffn_conv_w", 1)],
    "attn_bwd2": [("B", "ffn_w_in", 1)],
    "rms_mix_bwd1": [("P", "att_w_qkv", 0)],
    "ffn_out_dx0": [("A", "att_w_qkv", 0, (0, 4))],
    "ffn_out_dw0": [("A", "att_w_qkv", 0, (1, 4))],
    "convact_bwd0": [("A", "att_w_qkv", 0, (2, 4)), ("A", "att_w_qkv", 0, (3, 4)), ("P", "ffn_w_out", 0)],
    "ffn_in_dx0": [("A", "ffn_w_out", 0, (0, 2)), ("B", "att_w_qkv", 0)],
    "ffn_in_dw0": [("A", "ffn_w_out", 0, (1, 2))],
    "rms_ffn_bwd0": [("P", "ffn_w_in", 0), ("B", "ffn_w_out", 0)],
    "s5_scan_bwd": [("A", "ffn_w_in", 0, (0, 2)), ("P", "mix_w_out", 0), ("P", "s5_glu_w", 0), ("A", "ffn_conv_w", 0)],
    "hgrn_bwd": [("A", "ffn_w_in", 0, (1, 2)), ("A", "mix_w_out", 0), ("A", "s5_glu_w", 0), ("B", "ffn_conv_w", 0),
                 ("A", "small_early", 0)],
    "mix_in_dx": [("B", "ffn_w_in", 0), ("B", "mix_w_out", 0), ("B", "s5_glu_w", 0), ("B", "small_early", 0)],
    "rms_mix_bwd0": [("P", "mix_w_in", 0)],
    "adam_att_w_o": [("A", "mix_w_in", 0), ("A", "small_late", 0)],
    "adam_s5_glu_w": [("B", "mix_w_in", 0), ("B", "small_late", 0)],
}


class Comm:
    def __init__(self, shards, shapes):
        self.shards, self.shapes = shards, shapes
        self.W, self.grads, self.slots = {}, {}, {}
        self.sib, self.pair = {}, {}
        self.small = {}

    def plan(self, host):
        items = SCHEDULE.get(host)
        if not items:
            return None
        p = Plan()
        for it in items:
            kind, name, l = it[:3]
            part, parts = it[3] if len(it) > 3 else (0, 1)
            if name.startswith("small"):
                sg = self.small[name]
                kdst = p.buf("slots:" + name, arr=self.slots.get(name), shape=S((8,) + sg.shape, f32), write=True)
                if kind == "A":
                    ReduceOp(p, p.buf("g:" + name, arr=sg), kdst, None, sg.shape, False, 0, 0, whole=True)
                else:
                    ForwardOp(p, kdst, None, whole=True)
                continue
            nl, R, C_ = self.shapes[name]
            rows = name in ROW_SHARDED
            r0, nr = part * (R // parts), R // parts
            if kind == "G":
                sh = self.shards[name]
                kdst = p.buf(f"W:{name}:{l}", arr=self.W.get((name, l)), shape=S((4 * R, C_) if rows else (R, 4 * C_), sh.dtype),
                             write=True)
                GatherOp(p, p.buf("shard:" + name, arr=sh), kdst, l, self.shapes[name], rows, r0, nr, split=(nr % 32 == 0))
            elif name == "ffn_conv_w":
                g = self.grads[(name, l)]
                kdst = p.buf("slots:" + name, arr=self.slots.get(name), shape=S((8, nl, R, C_), g.dtype), write=True)
                if kind == "A":
                    ReduceOp(p, p.buf(f"g:{name}:{l}", arr=g), kdst, l, self.shapes[name], rows, r0, nr)
                else:
                    ForwardOp(p, kdst, l)
            elif kind == "P":
                g = self.grads[(name, l)]
                ksib = p.buf(f"sib:{name}:{l}", shape=S((4 * R // 2, C_) if rows else (R // 2, 4 * C_), g.dtype), write=True)
                PairOp(p, p.buf(f"g:{name}:{l}", arr=g), ksib, self.shapes[name], rows)
            else:
                if (name, l) not in self.pair:
                    self.pair[(name, l)] = pair_sum(self.grads[(name, l)], self.sib[(name, l)], rows, R, f"pair_sum_{name}{l}")
                h = self.pair[(name, l)]
                kdst = p.buf("slots:" + name, arr=self.slots.get(name), shape=S((4, nl, R, C_), h.dtype), write=True)
                if kind == "A":
                    ReduceOp(p, p.buf(f"h:{name}:{l}", arr=h), kdst, l, self.shapes[name], rows, r0 // 2, nr // 2, half=True)
                else:
                    HalfForwardOp(p, kdst, l, self.shapes[name])
        return p

    def done(self, p):
        for k, arr in p.out.items():
            tag, name = k.split(":")[:2]
            if tag == "W":
                self.W[(name, int(k.split(":")[2]))] = arr
            elif tag == "sib":
                self.sib[(name, int(k.split(":")[2]))] = arr
            else:
                self.slots[name] = arr


def _adamw(w, g, m, v):
    m = B1 * m + (1.0 - B1) * g
    v = B2 * v + (1.0 - B2) * jnp.square(g)
    m_hat = m / (1.0 - B1 ** STEP)
    v_hat = v / (1.0 - B2 ** STEP)
    return -LR * (m_hat / (jnp.sqrt(v_hat) + AEPS) + WD * w), m, v


def adam_big(w, m, v, slots, name, plan=None):
    nl, R, C = w.shape
    ns = slots.shape[0]
    tr = 128 if R % 128 == 0 else (64 if R % 64 == 0 else R)

    def body(w_ref, m_ref, v_ref, s_ref, g_ref, d_ref, nm_ref, nv_ref):
        g = s_ref[0].astype(f32)
        for s in range(1, ns):
            g = g + s_ref[s].astype(f32)
        d, nm_, nv_ = _adamw(w_ref[...], g, m_ref[...], v_ref[...])
        g_ref[...] = g
        d_ref[...] = d
        nm_ref[...] = nm_
        nv_ref[...] = nv_

    blk = pl.BlockSpec((None, tr, C), lambda l, i: (l, i, 0))
    return pcall(body, plan, grid=(nl, R // tr),
                 in_specs=[blk, blk, blk, pl.BlockSpec((ns, None, tr, C), lambda l, i: (0, l, i, 0))],
                 out_specs=[blk] * 4, out_shape=[S((nl, R, C), f32)] * 4,
                 sem=("parallel", "parallel"), name=name, args=[w, m, v, slots])


def adam_small(w, m, v, slots, name):
    R = w.shape[0]
    tr = 256

    def body(w_ref, m_ref, v_ref, s_ref, g_ref, d_ref, nm_ref, nv_ref):
        g = s_ref[0]
        for s in range(1, 8):
            g = g + s_ref[s]
        d, nm_, nv_ = _adamw(w_ref[...], g, m_ref[...], v_ref[...])
        g_ref[...] = g
        d_ref[...] = d
        nm_ref[...] = nm_
        nv_ref[...] = nv_

    blk = pl.BlockSpec((tr, 128), lambda i: (i, 0))
    return pl.pallas_call(
        body, grid=(R // tr,), in_specs=[blk, blk, blk, pl.BlockSpec((8, tr, 128), lambda i: (0, i, 0))],
        out_specs=[blk] * 4, out_shape=[S((R, 128), f32)] * 4,
        compiler_params=_cp(("parallel",)), name=name)(w, m, v, slots)


def _pack(d, names):
    flat = jnp.concatenate([d[n].reshape(-1) for n in names])
    n = flat.shape[0]
    rows = -(-n // (256 * 128)) * 256
    return jnp.pad(flat, (0, rows * 128 - n)).reshape(rows, 128)


def _unpack(p, like, names):
    flat = p.reshape(-1)
    out, off = {}, 0
    for n in names:
        sz = math.prod(like[n].shape)
        out[n] = flat[off:off + sz].reshape(like[n].shape)
        off += sz
    return out


def kernel(x, positions, norm_mix, norm_ffn, norm_final, mix_w_in, mix_w_out, s5_A_re, s5_A_im, s5_log_dt, s5_B_re, s5_B_im, s5_C_re, s5_C_im, s5_D, s5_glu_w, s5_glu_b, hgrn_gamma, hgrn_norm, att_w_qkv, att_w_o, ffn_w_in, ffn_conv_w, ffn_conv_b, ffn_w_out, loss_target, m_norm_mix, m_norm_ffn, m_norm_final, m_mix_w_in, m_mix_w_out, m_s5_A_re, m_s5_A_im, m_s5_log_dt, m_s5_B_re, m_s5_B_im, m_s5_C_re, m_s5_C_im, m_s5_D, m_s5_glu_w, m_s5_glu_b, m_hgrn_gamma, m_hgrn_norm, m_att_w_qkv, m_att_w_o, m_ffn_w_in, m_ffn_conv_w, m_ffn_conv_b, m_ffn_w_out, v_norm_mix, v_norm_ffn, v_norm_final, v_mix_w_in, v_mix_w_out, v_s5_A_re, v_s5_A_im, v_s5_log_dt, v_s5_B_re, v_s5_B_im, v_s5_C_re, v_s5_C_im, v_s5_D, v_s5_glu_w, v_s5_glu_b, v_hgrn_gamma, v_hgrn_norm, v_att_w_qkv, v_att_w_o, v_ffn_w_in, v_ffn_conv_w, v_ffn_conv_b, v_ffn_w_out):
    a = dict(locals())
    weights = BIG + SMALL
    w = {n: a[n] for n in weights}
    m = {n: a["m_" + n] for n in weights}
    v = {n: a["v_" + n] for n in weights}
    shards = {"ffn_conv_w": ffn_conv_w}
    C = Comm(shards, {n: w[n].shape for n in BIG})
    for n in ("mix_w_in", "ffn_w_in", "mix_w_out", "s5_glu_w", "ffn_w_out", "att_w_qkv", "att_w_o"):
        shards[n] = hosted(C, "cast_" + n, lambda p: cast_bf16(w[n], "cast_" + n, plan=p))
    sm = {n: w[n] for n in SMALL}
    sm["ffn_conv_b3"] = ffn_conv_b.reshape(2, 1, 2 * DFF)
    loss, gx, _, _ = local_step(x[0], positions.reshape(L, 1), loss_target[0], sm, C.W, C)
    res = {}
    for n in ("att_w_o", "s5_glu_w", "ffn_w_in", "ffn_w_out", "att_w_qkv", "mix_w_out", "ffn_conv_w", "mix_w_in"):
        res[n] = hosted(C, "adam_" + n, lambda p: adam_big(w[n], m[n], v[n], C.slots[n], "adam_" + n, plan=p))
    for names, key in ((SMALL_EARLY, "small_early"), (SMALL_LATE, "small_late")):
        packed = adam_small(_pack(w, names), _pack(m, names), _pack(v, names), C.slots[key], "adam_" + key)
        small_out = [_unpack(p, w, names) for p in packed]
        for n in names:
            res[n] = tuple(so[n] for so in small_out)
    total = lax.psum(loss[0, 0], ("x", "y", "c"))
    order = ("norm_mix", "norm_ffn", "norm_final", "mix_w_in", "mix_w_out", "s5_A_re", "s5_A_im", "s5_log_dt", "s5_B_re", "s5_B_im",
             "s5_C_re", "s5_C_im", "s5_D", "s5_glu_w", "s5_glu_b", "hgrn_gamma", "hgrn_norm", "att_w_qkv", "att_w_o", "ffn_w_in",
             "ffn_conv_w", "ffn_conv_b", "ffn_w_out")
    return (total, gx[None], *[res[n][0] for n in order], *[res[n][1] for n in order], *[res[n][2] for n in order],
            *[res[n][3] for n in order])
```

```python
import functools
import math

import numpy as np
import jax
import jax.numpy as jnp
from jax import lax
from jax.experimental import pallas as pl
from jax.experimental.pallas import tpu as pltpu

f32 = jnp.float32
BF = jnp.bfloat16
HI = lax.Precision.HIGHEST
S = jax.ShapeDtypeStruct
MESH = pl.DeviceIdType.MESH

L = 2048
D = 1024
EPS = 1e-6
S5W = 512
NST = 2048
HGC = 64
HGB = 32
DFF = 2816
ROPE_THETA = 500000.0
LR, B1, B2, AEPS, WD, STEP = 0.001, 0.9, 0.999, 1e-08, 0.01, 10
VMEM_LIMIT = 56 * 1024 * 1024


def _cp(sem=None):
    return pltpu.CompilerParams(dimension_semantics=sem, vmem_limit_bytes=VMEM_LIMIT)


ANY = pl.BlockSpec(memory_space=pl.ANY)
ROW_SHARDED = ("mix_w_out", "s5_glu_w", "ffn_w_out")


def _coords():
    x, y, c = lax.axis_index("x"), lax.axis_index("y"), lax.axis_index("c")
    return x, y, c, 2 * x + y, [(1 - x, y), (x, 1 - y), (1 - x, 1 - y)]


def _rows(start, n):
    return pl.ds(start if isinstance(start, int) else pl.multiple_of(start, 8), n)


def _cols(q, n):
    return pl.ds(pl.multiple_of(q * n, 128), n)


class Plan:
    def __init__(self):
        self.bufs, self.ops, self.nsem, self.out = {}, [], 0, {}

    def buf(self, key, arr=None, shape=None, write=False):
        b = self.bufs.setdefault(key, dict(arr=arr, shape=shape, write=False))
        b["write"] = b["write"] or write
        return key

    def add(self, op):
        op.base = self.nsem
        self.nsem += op.nsem
        self.ops.append(op)


class GatherOp:
    nsem = 13

    def __init__(self, plan, ksrc, kdst, l, shard_shape, rows, r0, nr, split):
        self.ksrc, self.kdst, self.l, (_, self.R, self.C), self.rows, self.r0, self.nr, self.split = (
            ksrc, kdst, l, shard_shape, rows, r0, nr, split)
        self.h = nr // 2 if split else nr
        plan.add(self)

    def _dst(self, R_, q, start, n):
        if self.rows:
            return R_[self.kdst].at[_rows(q * self.R + start, n), :]
        return R_[self.kdst].at[_rows(start, n), _cols(q, self.C)]

    def _mine(self, c):
        return self.r0 + (c * self.h if self.split else 0)

    def _theirs(self, c):
        return self.r0 + ((1 - c) * self.h if self.split else 0)

    def _copies(self, R_, sems):
        x, y, c, me, others = _coords()
        src = R_[self.ksrc]
        local = pltpu.make_async_copy(src.at[self.l, _rows(self.r0, self.nr), :], self._dst(R_, me, self.r0, self.nr),
                                      sems.at[self.base + 12])
        send, fwd = [], []
        for k, (px, py) in enumerate(others):
            q = 2 * px + py
            send.append((
                pltpu.make_async_remote_copy(src.at[self.l, _rows(self._mine(c), self.h), :], self._dst(R_, me, self._mine(c), self.h),
                                             sems.at[self.base + k], sems.at[self.base + 3 + k], device_id=(px, py, c), device_id_type=MESH),
                pltpu.make_async_remote_copy(src.at[self.l, _rows(self._mine(c), self.h), :], self._dst(R_, q, self._mine(c), self.h),
                                             sems.at[self.base + k], sems.at[self.base + 3 + k], device_id=(px, py, c), device_id_type=MESH)))
            fwd.append((
                pltpu.make_async_remote_copy(self._dst(R_, q, self._mine(c), self.h), self._dst(R_, q, self._mine(c), self.h),
                                             sems.at[self.base + 6 + k], sems.at[self.base + 9 + k], device_id=(x, y, 1 - c), device_id_type=MESH),
                pltpu.make_async_remote_copy(self._dst(R_, q, self._theirs(c), self.h), self._dst(R_, q, self._theirs(c), self.h),
                                             sems.at[self.base + 6 + k], sems.at[self.base + 9 + k], device_id=(x, y, 1 - c), device_id_type=MESH)))
        return local, send, fwd

    def start(self, R_, sems):
        local, send, _ = self._copies(R_, sems)
        local.start()
        for out, _ in send:
            out.start()

    def finish(self, R_, sems):
        local, send, fwd = self._copies(R_, sems)
        for k in range(3):
            send[k][1].wait_recv()
            if self.split:
                fwd[k][0].start()
        for k in range(3):
            if self.split:
                fwd[k][1].wait_recv()
                fwd[k][0].wait_send()
            send[k][0].wait_send()
        local.wait()


class ReduceOp:
    nsem = 7

    def __init__(self, plan, ksrc, kdst, l, shard_shape, rows, r0, nr, whole=False, half=False):
        self.ksrc, self.kdst, self.l, (self.R, self.C), self.rows, self.r0, self.nr, self.whole, self.half = (
            ksrc, kdst, l, shard_shape[-2:], rows, r0, nr, whole, half)
        plan.add(self)

    def _piece(self, R_, q):
        g = R_[self.ksrc]
        if self.whole:
            return g
        if self.rows:
            return g.at[_rows(q * (self.R // 2 if self.half else self.R) + self.r0, self.nr), :]
        return g.at[_rows(self.r0, self.nr), _cols(q, self.C)]

    def _slot(self, R_, q, c):
        if self.whole:
            return R_[self.kdst].at[2 * q + c]
        if self.half:
            return R_[self.kdst].at[q, self.l, _rows(c * (self.R // 2) + self.r0, self.nr), :]
        return R_[self.kdst].at[2 * q + c, self.l, _rows(self.r0, self.nr), :]

    def _copies(self, R_, sems):
        x, y, c, me, others = _coords()
        local = pltpu.make_async_copy(self._piece(R_, me), self._slot(R_, me, c), sems.at[self.base + 6])
        send = []
        for k, (px, py) in enumerate(others):
            q = 2 * px + py
            send.append((
                pltpu.make_async_remote_copy(self._piece(R_, q), self._slot(R_, me, c), sems.at[self.base + k],
                                             sems.at[self.base + 3 + k], device_id=(px, py, c), device_id_type=MESH),
                pltpu.make_async_remote_copy(self._piece(R_, q), self._slot(R_, q, c), sems.at[self.base + k],
                                             sems.at[self.base + 3 + k], device_id=(px, py, c), device_id_type=MESH)))
        return local, send

    def start(self, R_, sems):
        local, send = self._copies(R_, sems)
        local.start()
        for out, _ in send:
            out.start()

    def finish(self, R_, sems):
        local, send = self._copies(R_, sems)
        local.wait()
        for out, inn in send:
            inn.wait_recv()
            out.wait_send()


class ForwardOp:
    nsem = 8

    def __init__(self, plan, kdst, l, whole=False):
        self.kdst, self.l, self.whole = kdst, l, whole
        plan.add(self)

    def _slot(self, R_, s):
        return R_[self.kdst].at[s] if self.whole else R_[self.kdst].at[s, self.l]

    def _copies(self, R_, sems):
        x, y, c, me, others = _coords()
        return [(pltpu.make_async_remote_copy(self._slot(R_, 2 * q + c), self._slot(R_, 2 * q + c), sems.at[self.base + q],
                                              sems.at[self.base + 4 + q], device_id=(x, y, 1 - c), device_id_type=MESH),
                 pltpu.make_async_remote_copy(self._slot(R_, 2 * q + 1 - c), self._slot(R_, 2 * q + 1 - c), sems.at[self.base + q],
                                              sems.at[self.base + 4 + q], device_id=(x, y, 1 - c), device_id_type=MESH))
                for q in range(4)]

    def start(self, R_, sems):
        for out, _ in self._copies(R_, sems):
            out.start()

    def finish(self, R_, sems):
        for out, inn in self._copies(R_, sems):
            inn.wait_recv()
            out.wait_send()


class PairOp:
    nsem = 8

    def __init__(self, plan, ksrc, kdst, shard_shape, rows):
        self.ksrc, self.kdst, (self.R, self.C), self.rows = ksrc, kdst, shard_shape[-2:], rows
        plan.add(self)

    def _copies(self, R_, sems):
        x, y, c, me, others = _coords()
        g, dst, h = R_[self.ksrc], R_[self.kdst], self.R // 2
        out = []
        for q in range(4 if self.rows else 1):
            src = g.at[_rows(q * self.R + (1 - c) * h, h), :]
            land = dst.at[_rows(q * h, h), :]
            out.append(pltpu.make_async_remote_copy(src, land, sems.at[self.base + q], sems.at[self.base + 4 + q],
                                                    device_id=(x, y, 1 - c), device_id_type=MESH))
        return out

    def start(self, R_, sems):
        for cp in self._copies(R_, sems):
            cp.start()

    def finish(self, R_, sems):
        for cp in self._copies(R_, sems):
            cp.wait_recv()
            cp.wait_send()


class HalfForwardOp:
    nsem = 2

    def __init__(self, plan, kdst, l, shard_shape):
        self.kdst, self.l, self.R = kdst, l, shard_shape[-2]
        plan.add(self)

    def _copy(self, R_, sems, core):
        x, y, c, me, others = _coords()
        part = R_[self.kdst].at[:, self.l, _rows((c if core == "mine" else 1 - c) * (self.R // 2), self.R // 2), :]
        return pltpu.make_async_remote_copy(part, part, sems.at[self.base], sems.at[self.base + 1],
                                            device_id=(x, y, 1 - c), device_id_type=MESH)

    def start(self, R_, sems):
        self._copy(R_, sems, "mine").start()

    def finish(self, R_, sems):
        self._copy(R_, sems, "theirs").wait_recv()
        self._copy(R_, sems, "mine").wait_send()


def pair_sum(g, gsib, rows, R, name):
    h = R // 2
    W = g.shape[1]
    tr = h if h * W * 2 <= 2 ** 21 else 128
    nq = 4 if rows else 1

    def body(c_ref, a_ref, b_ref, o_ref):
        o_ref[...] = (a_ref[...].astype(f32) + b_ref[...].astype(f32)).astype(o_ref.dtype)

    half = pl.BlockSpec((tr, W), lambda q, i, c_ref: (q * (h // tr) + i, 0))
    mine = pl.BlockSpec((tr, W), lambda q, i, c_ref: (q * (R // tr) + c_ref[0] * (h // tr) + i, 0))
    return pl.pallas_call(
        body, grid_spec=pltpu.PrefetchScalarGridSpec(num_scalar_prefetch=1, grid=(nq, h // tr), in_specs=[mine, half],
                                                     out_specs=half),
        out_shape=S(gsib.shape, g.dtype), compiler_params=_cp(("parallel", "parallel")),
        name=name)(lax.axis_index("c").reshape(1).astype(jnp.int32), g, gsib)


def pcall(body, plan, *, grid, in_specs, out_specs, out_shape, scratch_shapes=(), sem, name, args):
    multi = isinstance(out_shape, (list, tuple))
    if plan is None or not plan.ops:
        return pl.pallas_call(body, grid=grid, in_specs=in_specs, out_specs=out_specs, out_shape=out_shape,
                              scratch_shapes=list(scratch_shapes), compiler_params=_cp(sem), name=name)(*args)
    outs = list(out_shape) if multi else [out_shape]
    ospecs = list(out_specs) if multi else [out_specs]
    kin = [k for k, b in plan.bufs.items() if b["arr"] is not None]
    kout = [k for k, b in plan.bufs.items() if b["write"]]
    n_in, n_out, n_scr = len(in_specs), len(outs), len(scratch_shapes)

    def wrapped(*refs):
        o0 = n_in + len(kin)
        s0 = o0 + n_out + len(kout)
        R_ = dict(zip(kin, refs[n_in:o0]))
        R_.update(zip(kout, refs[o0 + n_out:s0]))
        sems = refs[s0 + n_scr]
        first = functools.reduce(jnp.logical_and, [pl.program_id(d) == 0 for d in range(len(grid))])
        last = functools.reduce(jnp.logical_and, [pl.program_id(d) == grid[d] - 1 for d in range(len(grid))])

        @pl.when(first)
        def _():
            for op in plan.ops:
                op.start(R_, sems)

        body(*refs[:n_in], *refs[o0:o0 + n_out], *refs[s0:s0 + n_scr])

        @pl.when(last)
        def _():
            for op in plan.ops:
                op.finish(R_, sems)

    def shape_of(k):
        b = plan.bufs[k]
        return S(b["arr"].shape, b["arr"].dtype) if b["arr"] is not None else b["shape"]

    res = pl.pallas_call(
        wrapped, grid=grid, in_specs=list(in_specs) + [ANY] * len(kin), out_specs=ospecs + [ANY] * len(kout),
        out_shape=outs + [shape_of(k) for k in kout],
        scratch_shapes=list(scratch_shapes) + [pltpu.SemaphoreType.DMA((plan.nsem,))],
        input_output_aliases={n_in + kin.index(k): n_out + kout.index(k) for k in kout if plan.bufs[k]["arr"] is not None},
        compiler_params=pltpu.CompilerParams(dimension_semantics=("arbitrary",) * len(grid), vmem_limit_bytes=VMEM_LIMIT,
                                             has_side_effects=True),
        name=name)(*args, *[plan.bufs[k]["arr"] for k in kin])
    plan.out = dict(zip(kout, res[n_out:]))
    return list(res[:n_out]) if multi else res[0]


def _dg(a, b, ca, cb):
    return lax.dot_general(a.astype(BF), b.astype(BF), (((ca,), (cb,)), ((), ())), preferred_element_type=f32)


@jax.custom_vjp
def dot_nn(a, b):
    return _dg(a, b, 1, 0)


@jax.custom_vjp
def dot_nt(a, b):
    return _dg(a, b, 1, 1)


@jax.custom_vjp
def dot_tn(a, b):
    return _dg(a, b, 0, 0)


dot_nn.defvjp(lambda a, b: (dot_nn(a, b), (a, b)),
              lambda r, g: (dot_nt(g, r[1]).astype(r[0].dtype), dot_tn(r[0], g).astype(r[1].dtype)))
dot_nt.defvjp(lambda a, b: (dot_nt(a, b), (a, b)),
              lambda r, g: (dot_nn(g, r[1]).astype(r[0].dtype), dot_tn(g, r[0]).astype(r[1].dtype)))
dot_tn.defvjp(lambda a, b: (dot_tn(a, b), (a, b)),
              lambda r, g: (dot_nt(r[1], g).astype(r[0].dtype), dot_nn(r[0], g).astype(r[1].dtype)))


def matmul(a, b, *, mode, tm, tn, tk, out_dtype=f32, add=None, b_lead=None, a_spec=None, b_spec=None, dims=None, plan=None, name):
    a_over, b_over = a_spec, b_spec
    if mode == "nn":
        (M, K), N = a.shape[-2:], b.shape[-1]
        a_spec = pl.BlockSpec((tm, tk), lambda i, j, k: (i, k))
        b_blk, b_idx, ca, cb = (tk, tn), (lambda i, j, k: (k, j)), 1, 0
    elif mode == "nt":
        (M, K), N = a.shape[-2:], b.shape[-2]
        a_spec = pl.BlockSpec((tm, tk), lambda i, j, k: (i, k))
        b_blk, b_idx, ca, cb = (tn, tk), (lambda i, j, k: (j, k)), 1, 1
    else:
        (K, M), N = a.shape[-2:], b.shape[-1]
        a_spec = pl.BlockSpec((tk, tm), lambda i, j, k: (k, i))
        b_blk, b_idx, ca, cb = (tk, tn), (lambda i, j, k: (k, j)), 0, 0
    if dims is not None:
        M, N, K = dims
    assert M % tm == 0 and N % tn == 0 and K % tk == 0, (name, M, N, K, tm, tn, tk)
    if b_lead is None:
        b_spec = pl.BlockSpec(b_blk, b_idx)
    else:
        b_spec = pl.BlockSpec((None,) + b_blk, lambda i, j, k: (b_lead,) + b_idx(i, j, k))
    if a_over is not None:
        a_spec = a_over
    if b_over is not None:
        b_spec = b_over
    nk = K // tk
    has_add = add is not None

    def body(*refs):
        a_ref, b_ref = refs[0], refs[1]
        add_ref = refs[2] if has_add else None
        o_ref = refs[2 + has_add]
        p = _dg(a_ref[...], b_ref[...], ca, cb)

        def fin(v):
            if has_add:
                v = v + add_ref[...].astype(f32)
            o_ref[...] = v.astype(o_ref.dtype)

        if nk == 1:
            fin(p)
        else:
            acc = refs[3 + has_add]
            k = pl.program_id(2)

            @pl.when(k == 0)
            def _():
                acc[...] = p

            @pl.when(k > 0)
            def _():
                acc[...] += p

            @pl.when(k == nk - 1)
            def _():
                fin(acc[...])

    in_specs = [a_spec, b_spec]
    args = [a, b]
    if has_add:
        in_specs.append(pl.BlockSpec((tm, tn), lambda i, j, k: (i, j)))
        args.append(add)
    return pcall(body, plan, grid=(M // tm, N // tn, nk), in_specs=in_specs,
                 out_specs=pl.BlockSpec((tm, tn), lambda i, j, k: (i, j)), out_shape=S((M, N), out_dtype),
                 scratch_shapes=[pltpu.VMEM((tm, tn), f32)] if nk > 1 else [],
                 sem=("parallel", "parallel", "arbitrary"), name=name, args=args)


def _rms(xv, gv):
    return xv * lax.rsqrt(jnp.mean(xv * xv, axis=-1, keepdims=True) + EPS) * gv


TR = 256


def rms_fwd(x, g, name):
    def body(x_ref, g_ref, o_ref):
        o_ref[...] = _rms(x_ref[...], g_ref[...]).astype(o_ref.dtype)

    return pl.pallas_call(
        body, grid=(L // TR,),
        in_specs=[pl.BlockSpec((TR, D), lambda i: (i, 0)), pl.BlockSpec((1, D), lambda i: (0, 0))],
        out_specs=pl.BlockSpec((TR, D), lambda i: (i, 0)), out_shape=S((L, D), BF),
        compiler_params=_cp(("parallel",)), name=name)(x, g)


def rms_bwd(x, g, dys, dres, name, plan=None):
    nd = len(dys)

    def body(*refs):
        x_ref, g_ref = refs[0], refs[1]
        dr_ref, dh_ref, dg_ref = refs[2 + nd:]
        dy = refs[2][...].astype(f32)
        for r in refs[3:2 + nd]:
            dy = dy + r[...].astype(f32)
        _, vjp = jax.vjp(_rms, x_ref[...], g_ref[...])
        dx, dg = vjp(dy)
        dh_ref[...] = dr_ref[...] + dx

        @pl.when(pl.program_id(0) == 0)
        def _():
            dg_ref[...] = jnp.zeros_like(dg_ref)

        dg_ref[...] += dg

    row = pl.BlockSpec((TR, D), lambda i: (i, 0))
    vec = pl.BlockSpec((1, D), lambda i: (0, 0))
    return pcall(body, plan, grid=(L // TR,), in_specs=[row, vec] + [row] * (nd + 1), out_specs=[row, vec],
                 out_shape=[S((L, D), f32), S((1, D), f32)], sem=("arbitrary",), name=name, args=[x, g, *dys, dres])


def loss_head(h, g, tgt):
    def f(hv, gv, tv):
        y = _rms(hv, gv)
        return 0.5 * jnp.sum(jnp.mean(jnp.square(y - tv), axis=-1))

    def body(h_ref, g_ref, t_ref, l_ref, dh_ref, dg_ref):
        val, vjp = jax.vjp(f, h_ref[...], g_ref[...], t_ref[...])
        dh, dg, _ = vjp(jnp.ones((), f32))
        dh_ref[...] = dh

        @pl.when(pl.program_id(0) == 0)
        def _():
            dg_ref[...] = jnp.zeros_like(dg_ref)
            l_ref[...] = jnp.zeros_like(l_ref)

        dg_ref[...] += dg
        l_ref[...] += jnp.full((1, 128), val, f32)

    row = pl.BlockSpec((TR, D), lambda i: (i, 0))
    vec = pl.BlockSpec((1, D), lambda i: (0, 0))
    return pl.pallas_call(
        body, grid=(L // TR,), in_specs=[row, vec, row],
        out_specs=[pl.BlockSpec((1, 128), lambda i: (0, 0)), row, vec],
        out_shape=[S((1, 128), f32), S((L, D), f32), S((1, D), f32)],
        compiler_params=_cp(("arbitrary",)), name="loss_head")(h, g, tgt)


def _col_to_row(c):
    n = c.shape[0]
    t = jnp.broadcast_to(c, (n, 128)).T
    r = lax.broadcasted_iota(jnp.int32, (128, n), 0)
    return jnp.sum(jnp.where(r == 0, t, 0.0), axis=0, keepdims=True)


def _s5_param_map(are, aim, ldt_row, bre, bim, cre, cim):
    n = NST
    gi = lax.broadcasted_iota(jnp.int32, (n, 32), 0) // 64
    gj = lax.broadcasted_iota(jnp.int32, (n, 32), 1)
    ldt = jnp.sum(jnp.where(gi == gj, ldt_row, 0.0), axis=1, keepdims=True)
    dt = jnp.exp(ldt)
    mag = jnp.exp(are * dt)
    abr = mag * jnp.cos(aim * dt)
    abi = mag * jnp.sin(aim * dt)
    den = are * are + aim * aim
    nr, ni = abr - 1.0, abi
    cr = (nr * are + ni * aim) / den
    ci = (ni * are - nr * aim) / den
    bbr = cr * bre - ci * bim
    bbi = cr * bim + ci * bre
    tc = lax.broadcasted_iota(jnp.int32, (16, 128), 0)
    tl = lax.broadcasted_iota(jnp.int32, (16, 128), 1)
    T = (tl % 16 == tc).astype(f32)
    mr = (lax.broadcasted_iota(jnp.int32, (n, 128), 0) // 64) % 8
    mc = lax.broadcasted_iota(jnp.int32, (n, 128), 1) // 16
    mask = (mr == mc).astype(f32)

    def expand(v):
        return jnp.dot(v, T, precision=HI, preferred_element_type=f32) * mask

    return expand(bbr), expand(bbi), expand(cre), expand(cim), _col_to_row(abr), _col_to_row(abi)


def s5_params_fwd(are, aim, ldt_row, bre, bim, cre, cim):
    def body(*refs):
        outs = _s5_param_map(*[r[...] for r in refs[:7]])
        for o_ref, o in zip(refs[7:], outs):
            o_ref[...] = o

    return pl.pallas_call(
        body, out_shape=[S((NST, 128), f32)] * 4 + [S((1, NST), f32)] * 2,
        compiler_params=_cp(), name="s5_params_fwd")(are, aim, ldt_row, bre, bim, cre, cim)


def s5_params_bwd(are, aim, ldt_row, bre, bim, cre, cim, cots):
    def body(*refs):
        _, vjp = jax.vjp(_s5_param_map, *[r[...] for r in refs[:7]])
        gs = vjp(tuple(r[...] for r in refs[7:13]))
        for o_ref, o in zip(refs[13:], gs):
            o_ref[...] = o

    return pl.pallas_call(
        body, out_shape=[S((NST, 1), f32)] * 2 + [S((1, 32), f32)] + [S((NST, 16), f32)] * 4,
        compiler_params=_cp(), name="s5_params_bwd")(are, aim, ldt_row, bre, bim, cre, cim, *cots)


def _cpowers(ar, ai):
    out = [(ar, ai)]
    for _ in range(7):
        pr, pi = out[-1]
        out.append((pr * ar - pi * ai, pr * ai + pi * ar))
    return out


def _ctable(pw, rid, power):
    tr_ = jnp.zeros(rid.shape, f32)
    ti_ = jnp.zeros(rid.shape, f32)
    for r in range(8):
        pr, pi = pw[power(r) - 1]
        tr_ = jnp.where(rid == r, pr, tr_)
        ti_ = jnp.where(rid == r, pi, ti_)
    return tr_, ti_


NT5 = 4
RC = 256


def s5_scan_fwd(proj, wbr, wbi, wcr, wci, abr, abi, drow, plan=None):
    def body(u_ref, wbr_ref, wbi_ref, wcr_ref, wci_ref, ar_ref, ai_ref, d_ref, xr_ref, xi_ref, y_ref):
        wbr_v, wbi_v = wbr_ref[...], wbi_ref[...]
        for r in range(L // RC):
            rows = pl.ds(r * RC, RC)
            ub = u_ref[rows, :]
            xr_ref[rows, :] = dot_nt(ub, wbr_v)
            xi_ref[rows, :] = dot_nt(ub, wbi_v)
        pw = _cpowers(ar_ref[...], ai_ref[...])
        rid = lax.broadcasted_iota(jnp.int32, (8, 512), 0)
        tr_, ti_ = _ctable(pw, rid, lambda r: r + 1)

        def group(j, c):
            cr, ci = c
            rows = pl.ds(pl.multiple_of(j * 8, 8), 8)
            br, bi = xr_ref[rows, :], xi_ref[rows, :]
            for s in (1, 2, 4):
                pr, pi = pw[s - 1]
                sr = jnp.where(rid >= s, pltpu.roll(br, s, 0), 0.0)
                si = jnp.where(rid >= s, pltpu.roll(bi, s, 0), 0.0)
                br, bi = br + pr * sr - pi * si, bi + pr * si + pi * sr
            br, bi = br + tr_ * cr - ti_ * ci, bi + tr_ * ci + ti_ * cr
            xr_ref[rows, :] = br
            xi_ref[rows, :] = bi
            return br[7:8], bi[7:8]

        z = jnp.zeros((1, 512), f32)
        lax.fori_loop(0, L // 8, group, (z, z), unroll=2)
        wcr_v, wci_v, dv = wcr_ref[...], wci_ref[...], d_ref[...]
        for r in range(L // RC):
            rows = pl.ds(r * RC, RC)
            y_ref[rows, :] = (dot_nn(xr_ref[rows, :], wcr_v) - dot_nn(xi_ref[rows, :], wci_v)
                              + dv * u_ref[rows, :])

    wspec = pl.BlockSpec((512, 128), lambda j: (j, 0))
    aspec = pl.BlockSpec((1, 512), lambda j: (0, j))
    return pcall(
        body, plan, grid=(NT5,),
        in_specs=[pl.BlockSpec((L, 128), lambda j: (0, j)), wspec, wspec, wspec, wspec, aspec, aspec,
                  pl.BlockSpec((1, 128), lambda j: (0, j))],
        out_specs=[pl.BlockSpec((L, 512), lambda j: (0, j)), pl.BlockSpec((L, 512), lambda j: (0, j)),
                   pl.BlockSpec((L, 128), lambda j: (0, j))],
        out_shape=[S((L, NST), f32), S((L, NST), f32), S((L, S5W), f32)],
        sem=("parallel",), name="s5_scan_fwd", args=[proj, wbr, wbi, wcr, wci, abr, abi, drow])


def s5_scan_bwd(dy, proj, xs_re, xs_im, wbr, wbi, wcr, wci, abr, abi, drow, plan=None):
    def body(dy_ref, u_ref, xr_ref, xi_ref, wbr_ref, wbi_ref, wcr_ref, wci_ref, ar_ref, ai_ref, d_ref,
             du_ref, gwbr_ref, gwbi_ref, gwcr_ref, gwci_ref, gar_ref, gai_ref, gd_ref, lr_ref, li_ref):
        wcr_v, wci_v = wcr_ref[...], wci_ref[...]
        gwcr = jnp.zeros((512, 128), f32)
        gwci = jnp.zeros((512, 128), f32)
        gd = jnp.zeros((1, 128), f32)
        for r in range(L // RC):
            rows = pl.ds(r * RC, RC)
            dyv = dy_ref[rows, :]
            lr_ref[rows, :] = dot_nt(dyv, wcr_v)
            li_ref[rows, :] = -dot_nt(dyv, wci_v)
            gwcr += dot_tn(xr_ref[rows, :], dyv)
            gwci -= dot_tn(xi_ref[rows, :], dyv)
            gd += jnp.sum(dyv * u_ref[rows, :], axis=0, keepdims=True)
        gwcr_ref[...] = gwcr
        gwci_ref[...] = gwci
        gd_ref[...] = gd
        pw = _cpowers(ar_ref[...], -ai_ref[...])
        rid = lax.broadcasted_iota(jnp.int32, (8, 512), 0)
        tr_, ti_ = _ctable(pw, rid, lambda r: 8 - r)

        def group(i, c):
            cr, ci, gar, gai = c
            j = L // 8 - 1 - i
            rows = pl.ds(pl.multiple_of(j * 8, 8), 8)
            br, bi = lr_ref[rows, :], li_ref[rows, :]
            for s in (1, 2, 4):
                pr, pi = pw[s - 1]
                sr = jnp.where(rid < 8 - s, pltpu.roll(br, 8 - s, 0), 0.0)
                si = jnp.where(rid < 8 - s, pltpu.roll(bi, 8 - s, 0), 0.0)
                br, bi = br + pr * sr - pi * si, bi + pr * si + pi * sr
            br, bi = br + tr_ * cr - ti_ * ci, bi + tr_ * ci + ti_ * cr
            lr_ref[rows, :] = br
            li_ref[rows, :] = bi
            nr = jnp.where(rid < 7, pltpu.roll(br, 7, 0), cr)
            ni = jnp.where(rid < 7, pltpu.roll(bi, 7, 0), ci)
            xr, xi = xr_ref[rows, :], xi_ref[rows, :]
            return br[0:1], bi[0:1], gar + xr * nr + xi * ni, gai + xr * ni - xi * nr

        z = jnp.zeros((1, 512), f32)
        z8 = jnp.zeros((8, 512), f32)
        _, _, gar, gai = lax.fori_loop(0, L // 8, group, (z, z, z8, z8), unroll=2)
        gar_ref[...] = jnp.sum(gar, axis=0, keepdims=True)
        gai_ref[...] = jnp.sum(gai, axis=0, keepdims=True)
        wbr_v, wbi_v, dv = wbr_ref[...], wbi_ref[...], d_ref[...]
        gwbr = jnp.zeros((512, 128), f32)
        gwbi = jnp.zeros((512, 128), f32)
        for r in range(L // RC):
            rows = pl.ds(r * RC, RC)
            lrv, liv, uv = lr_ref[rows, :], li_ref[rows, :], u_ref[rows, :]
            du_ref[rows, :] = (dot_nn(lrv, wbr_v) + dot_nn(liv, wbi_v) + dv * dy_ref[rows, :]).astype(du_ref.dtype)
            gwbr += dot_tn(lrv, uv)
            gwbi += dot_tn(liv, uv)
        gwbr_ref[...] = gwbr
        gwbi_ref[...] = gwbi

    wspec = pl.BlockSpec((512, 128), lambda j: (j, 0))
    aspec = pl.BlockSpec((1, 512), lambda j: (0, j))
    col = pl.BlockSpec((L, 128), lambda j: (0, j))
    st = pl.BlockSpec((L, 512), lambda j: (0, j))
    dspec = pl.BlockSpec((1, 128), lambda j: (0, j))
    return pcall(
        body, plan, grid=(NT5,),
        in_specs=[col, col, st, st, wspec, wspec, wspec, wspec, aspec, aspec, dspec],
        out_specs=[col, wspec, wspec, wspec, wspec, aspec, aspec, dspec],
        out_shape=[S((L, S5W), BF)] + [S((NST, 128), f32)] * 4 + [S((1, NST), f32)] * 2 + [S((1, S5W), f32)],
        scratch_shapes=[pltpu.VMEM((L, 512), f32), pltpu.VMEM((L, 512), f32)],
        sem=("parallel",), name="s5_scan_bwd", args=[dy, proj, xs_re, xs_im, wbr, wbi, wcr, wci, abr, abi, drow])


def _glu(y, w, b):
    z = jax.nn.gelu(y)
    return z * jax.nn.sigmoid(dot_nn(z, w) + b)


def s5_glu_fwd(y, w, b):
    def body(y_ref, w_ref, b_ref, o_ref):
        o_ref[...] = _glu(y_ref[...], w_ref[...], b_ref[...]).astype(o_ref.dtype)

    return pl.pallas_call(
        body, grid=(L // TR,),
        in_specs=[pl.BlockSpec((TR, S5W), lambda i: (i, 0)), pl.BlockSpec((S5W, S5W), lambda i: (0, 0)),
                  pl.BlockSpec((1, S5W), lambda i: (0, 0))],
        out_specs=pl.BlockSpec((TR, S5W), lambda i: (i, 0)), out_shape=S((L, S5W), BF),
        compiler_params=_cp(("parallel",)), name="s5_glu_fwd")(y, w, b)


def s5_glu_bwd(y, w, b, dmix):
    def body(y_ref, w_ref, b_ref, g_ref, dy_ref, dw_ref, db_ref):
        _, vjp = jax.vjp(_glu, y_ref[...], w_ref[...].astype(f32), b_ref[...])
        dy, dw, db = vjp(g_ref[...])
        dy_ref[...] = dy

        @pl.when(pl.program_id(0) == 0)
        def _():
            dw_ref[...] = jnp.zeros_like(dw_ref)
            db_ref[...] = jnp.zeros_like(db_ref)

        dw_ref[...] += dw
        db_ref[...] += db

    row = pl.BlockSpec((TR, S5W), lambda i: (i, 0))
    return pl.pallas_call(
        body, grid=(L // TR,),
        in_specs=[row, pl.BlockSpec((S5W, S5W), lambda i: (0, 0)), pl.BlockSpec((1, S5W), lambda i: (0, 0)), row],
        out_specs=[row, pl.BlockSpec((S5W, S5W), lambda i: (0, 0)), pl.BlockSpec((1, S5W), lambda i: (0, 0))],
        out_shape=[S((L, S5W), f32), S((S5W, S5W), f32), S((1, S5W), f32)],
        compiler_params=_cp(("arbitrary",)), name="s5_glu_bwd")(y, w, b, dmix)


def _dg3(a, b, ca, cb):
    ah, bh = a.astype(BF), b.astype(BF)
    al, bl = (a - ah.astype(f32)).astype(BF), (b - bh.astype(f32)).astype(BF)
    return _dg(ah, bh, ca, cb) + _dg(ah, bl, ca, cb) + _dg(al, bh, ca, cb)


@jax.custom_vjp
def hi_nn(a, b):
    return _dg3(a, b, 1, 0)


@jax.custom_vjp
def hi_nt(a, b):
    return _dg3(a, b, 1, 1)


@jax.custom_vjp
def hi_tn(a, b):
    return _dg3(a, b, 0, 0)


hi_nn.defvjp(lambda a, b: (hi_nn(a, b), (a, b)), lambda r, g: (hi_nt(g, r[1]), hi_tn(r[0], g)))
hi_nt.defvjp(lambda a, b: (hi_nt(a, b), (a, b)), lambda r, g: (hi_nn(g, r[1]), hi_tn(g, r[0])))
hi_tn.defvjp(lambda a, b: (hi_tn(a, b), (a, b)), lambda r, g: (hi_nt(r[1], g), hi_nn(r[0], g)))


def _hgrn_chunk(St, xq, xf, xi, xg, gam, ng):
    lb = jax.nn.sigmoid(gam[0:1] - gam[1:2])
    q = jax.nn.silu(xq)
    f = lb + (1.0 - lb) * jax.nn.sigmoid(xf)
    k = 1.0 - f
    g = jnp.log(f)
    ti = lax.broadcasted_iota(jnp.int32, (HGC, HGC), 0)
    si = lax.broadcasted_iota(jnp.int32, (HGC, HGC), 1)
    causal = si <= ti
    b = jnp.dot(causal.astype(f32), g, precision=HI, preferred_element_type=f32)
    qe = q * jnp.exp(b)
    o = dot_nt(qe, St)
    parts = []
    for i in range(HGC // HGB):
        r, n, mid = slice(HGB * i, HGB * (i + 1)), HGB * (i + 1), HGB * i + HGB // 2
        base = b[mid:mid + 1]
        sc = hi_nt(q[r] * jnp.exp(b[r] - base), k[:n] * jnp.exp(base - b[:n]))
        parts.append(dot_nn(jnp.where(causal[r, :n], sc, 0.0), xi[:n]))
    o = o + jnp.concatenate(parts, axis=0)
    bl = b[HGC - 1:HGC]
    St_new = St * jnp.exp(bl) + dot_tn(xi, k * jnp.exp(bl - b))
    o = o * lax.rsqrt(jnp.mean(o * o, axis=-1, keepdims=True) + EPS) * ng
    return St_new, o * jax.nn.silu(xg)


NCH = L // HGC


def hgrn_fwd(proj, gamma, hnorm, plan=None):
    def body(q_ref, f_ref, i_ref, g_ref, gam_ref, ng_ref, o_ref, ss_ref, st):
        @pl.when(pl.program_id(0) == 0)
        def _():
            st[...] = jnp.zeros_like(st)

        for h in range(4):
            sl = slice(h * 128, (h + 1) * 128)
            s0 = st[h]
            ss_ref[0, h] = s0
            s1, o = _hgrn_chunk(s0, q_ref[:, sl], f_ref[:, sl], i_ref[:, sl], g_ref[:, sl], gam_ref[:, sl], ng_ref[:, sl])
            st[h] = s1
            o_ref[:, sl] = o.astype(o_ref.dtype)

    def pj(n):
        return pl.BlockSpec((HGC, 512), lambda c: (c, n))

    return pcall(
        body, plan, grid=(NCH,),
        in_specs=[pj(1), pj(2), pj(3), pj(4), pl.BlockSpec((2, 512), lambda c: (0, 0)), pl.BlockSpec((1, 512), lambda c: (0, 0))],
        out_specs=[pl.BlockSpec((HGC, 512), lambda c: (c, 0)), pl.BlockSpec((1, 4, 128, 128), lambda c: (c, 0, 0, 0))],
        out_shape=[S((L, 512), BF), S((NCH, 4, 128, 128), f32)],
        scratch_shapes=[pltpu.VMEM((4, 128, 128), f32)],
        sem=("arbitrary",), name="hgrn_fwd", args=[proj, proj, proj, proj, gamma, hnorm])


def hgrn_bwd(proj, gamma, hnorm, ssave, dmix, du, plan=None):
    def body(q_ref, f_ref, i_ref, g_ref, gam_ref, ng_ref, ss_ref, do_ref, du_ref, dp_ref, dgam_ref, dng_ref, dst):
        @pl.when(pl.program_id(0) == 0)
        def _():
            dst[...] = jnp.zeros_like(dst)
            dgam_ref[...] = jnp.zeros_like(dgam_ref)
            dng_ref[...] = jnp.zeros_like(dng_ref)

        dp_ref[:, 0:512] = du_ref[...]
        for h in range(4):
            sl = slice(h * 128, (h + 1) * 128)
            _, vjp = jax.vjp(_hgrn_chunk, ss_ref[0, h], q_ref[:, sl], f_ref[:, sl], i_ref[:, sl], g_ref[:, sl],
                             gam_ref[:, sl], ng_ref[:, sl])
            ds, dq, df, di, dg, dgam, dng = vjp((dst[h], do_ref[:, sl]))
            dst[h] = ds
            for n, v in enumerate((dq, df, di, dg)):
                dp_ref[:, 512 * (n + 1) + h * 128: 512 * (n + 1) + (h + 1) * 128] = v.astype(dp_ref.dtype)
            dgam_ref[:, sl] += dgam
            dng_ref[:, sl] += dng

    def pj(n):
        return pl.BlockSpec((HGC, 512), lambda i: (NCH - 1 - i, n))

    return pcall(
        body, plan, grid=(NCH,),
        in_specs=[pj(1), pj(2), pj(3), pj(4), pl.BlockSpec((2, 512), lambda i: (0, 0)), pl.BlockSpec((1, 512), lambda i: (0, 0)),
                  pl.BlockSpec((1, 4, 128, 128), lambda i: (NCH - 1 - i, 0, 0, 0)), pj(1), pj(0)],
        out_specs=[pl.BlockSpec((HGC, 2560), lambda i: (NCH - 1 - i, 0)), pl.BlockSpec((2, 512), lambda i: (0, 0)),
                   pl.BlockSpec((1, 512), lambda i: (0, 0))],
        out_shape=[S((L, 2560), BF), S((2, 512), f32), S((1, 512), f32)],
        scratch_shapes=[pltpu.VMEM((4, 128, 128), f32)],
        sem=("arbitrary",), name="hgrn_bwd", args=[proj, proj, proj, proj, gamma, hnorm, ssave, dmix, du])


def _earlier(h_ref, k, r0, n):
    if r0 > 0:
        return h_ref[pl.ds(r0 - k, n), :]
    rid = lax.broadcasted_iota(jnp.int32, (8, h_ref.shape[1]), 0)
    head = jnp.where(rid >= k, pltpu.roll(h_ref[pl.ds(0, 8), :], k, 0), 0.0)
    return jnp.concatenate([head, h_ref[pl.ds(8 - k, n - 8), :]], axis=0)


def _conv3_rows(h_ref, w, b, r0, n=None):
    n = CR if n is None else n
    h1, h2 = _earlier(h_ref, 1, r0, n), _earlier(h_ref, 2, r0, n)
    return w[2:3] * h_ref[pl.ds(r0, n), :] + w[1:2] * h1 + w[0:1] * h2 + b, h1, h2


CT = 128
NCT = DFF // CT
CR = 64


def convact_fwd(hu, cw, cb, layer, plan=None):
    def body(ha_ref, hb_ref, wa_ref, wb_ref, ba_ref, bb_ref, o_ref):
        ca = _conv3_rows(ha_ref, wa_ref[...], ba_ref[...], 0, L)[0]
        cb_ = _conv3_rows(hb_ref, wb_ref[...], bb_ref[...], 0, L)[0]
        o_ref[...] = (jax.nn.silu(ca) * cb_).astype(o_ref.dtype)

    def h(off):
        return pl.BlockSpec((L, CT), lambda j: (0, j + off))

    def w(off):
        return pl.BlockSpec((3, CT), lambda j: (0, j + off))

    def b(off):
        return pl.BlockSpec((None, 1, CT), lambda j: (layer, 0, j + off))

    return pcall(body, plan, grid=(NCT,), in_specs=[h(0), h(NCT), w(0), w(NCT), b(0), b(NCT)],
                 out_specs=pl.BlockSpec((L, CT), lambda j: (0, j)), out_shape=S((L, DFF), BF),
                 sem=("parallel",), name=f"convact_fwd{layer}", args=[hu, hu, cw, cw, cb, cb])


def convact_bwd(hu, cw, cb, dact, layer, plan=None):
    def body(ha_ref, hb_ref, wa_ref, wb_ref, ba_ref, bb_ref, g_ref, dh_ref, dw_ref, db_ref, sh, sw, sb, da_scr, db_scr):
        j = pl.program_id(0)

        def fold(x):
            return functools.reduce(jnp.add, [x[8 * m:8 * m + 8] for m in range(CR // 8)])

        @pl.when(j < NCT)
        def _():
            wa, wb, ba, bb = wa_ref[...], wb_ref[...], ba_ref[...], bb_ref[...]
            da_scr[pl.ds(L, 8), :] = jnp.zeros((8, CT), f32)
            db_scr[pl.ds(L, 8), :] = jnp.zeros((8, CT), f32)
            acc = [jnp.zeros((8, CT), f32) for _ in range(8)]
            for c in range(L // CR):
                r0 = c * CR
                ca, a1, a2 = _conv3_rows(ha_ref, wa, ba, r0)
                cb_, b1, b2 = _conv3_rows(hb_ref, wb, bb, r0)
                g = g_ref[pl.ds(r0, CR), :].astype(f32)
                sg = jax.nn.sigmoid(ca)
                dca = g * cb_ * (sg * (1.0 + ca * (1.0 - sg)))
                dcb = g * (ca * sg)
                da_scr[pl.ds(r0, CR), :] = dca
                db_scr[pl.ds(r0, CR), :] = dcb
                terms = (dca * a2, dca * a1, dca * ha_ref[pl.ds(r0, CR), :], dca,
                         dcb * b2, dcb * b1, dcb * hb_ref[pl.ds(r0, CR), :], dcb)
                acc = [a + fold(t) for a, t in zip(acc, terms)]
            rows = [jnp.sum(a, axis=0, keepdims=True) for a in acc]
            for k in range(3):
                dw_ref[k:k + 1, :] = rows[k]
                sw[j, k:k + 1, :] = rows[4 + k]
            db_ref[...] = rows[3]
            sb[j] = rows[7]
            for c in range(L // CR):
                r0 = c * CR
                for scr, w, out in ((da_scr, wa, dh_ref), (db_scr, wb, sh.at[j])):
                    dh = (w[2:3] * scr[pl.ds(r0, CR), :] + w[1:2] * scr[pl.ds(r0 + 1, CR), :]
                          + w[0:1] * scr[pl.ds(r0 + 2, CR), :])
                    out[pl.ds(r0, CR), :] = dh.astype(out.dtype)

        @pl.when(j >= NCT)
        def _():
            dh_ref[...] = sh[j - NCT]
            dw_ref[...] = sw[j - NCT]
            db_ref[...] = sb[j - NCT]

    def lo(j):
        return jnp.minimum(j, NCT - 1)

    in_specs = [pl.BlockSpec((L, CT), lambda j: (0, lo(j))), pl.BlockSpec((L, CT), lambda j: (0, lo(j) + NCT)),
                pl.BlockSpec((3, CT), lambda j: (0, lo(j))), pl.BlockSpec((3, CT), lambda j: (0, lo(j) + NCT)),
                pl.BlockSpec((None, 1, CT), lambda j: (layer, 0, lo(j))), pl.BlockSpec((None, 1, CT), lambda j: (layer, 0, lo(j) + NCT)),
                pl.BlockSpec((L, CT), lambda j: (0, lo(j)))]
    return pcall(
        body, plan, grid=(2 * NCT,), in_specs=in_specs,
        out_specs=[pl.BlockSpec((L, CT), lambda j: (0, j)), pl.BlockSpec((3, CT), lambda j: (0, j)), pl.BlockSpec((1, CT), lambda j: (0, j))],
        out_shape=[S((L, 2 * DFF), BF), S((3, 2 * DFF), f32), S((1, 2 * DFF), f32)],
        scratch_shapes=[pltpu.VMEM((NCT, L, CT), BF), pltpu.VMEM((NCT, 3, CT), f32), pltpu.VMEM((NCT, 1, CT), f32),
                        pltpu.VMEM((L + 8, CT), f32), pltpu.VMEM((L + 8, CT), f32)],
        sem=("arbitrary",), name=f"convact_bwd{layer}", args=[hu, hu, cw, cw, cb, cb, dact])


DILS = (1, 4, 16)
AB = 128
NPAIR = 12


def _rope_tables(pos_ref, invf_ref):
    ang = pos_ref[...].astype(f32) * invf_ref[...]
    lane = lax.broadcasted_iota(jnp.int32, (1, 128), 1) % 64
    cosf = jnp.where(lane < 16, jnp.cos(ang), 1.0)
    sn = jnp.sin(ang)
    s_lo = jnp.where(lane < 8, -sn, 0.0)
    s_hi = jnp.where((lane >= 8) & (lane < 16), sn, 0.0)
    return cosf, s_lo, s_hi


def _rope(t, cosf, s_lo, s_hi):
    return t * cosf + pltpu.roll(t, 120, 1) * s_lo + pltpu.roll(t, 8, 1) * s_hi


def _rope_t(g, cosf, s_lo, s_hi):
    return g * cosf + pltpu.roll(g * s_lo, 8, 1) + pltpu.roll(g * s_hi, 120, 1)


def _att_block(q2, kp, kc, vp, vc, first):
    lane = lax.broadcasted_iota(jnp.int32, (1, 128), 1)
    qi = lax.broadcasted_iota(jnp.int32, (AB, 2 * AB), 0) + AB
    kj = lax.broadcasted_iota(jnp.int32, (AB, 2 * AB), 1)
    back = qi - kj
    valid = (back >= 0) & (back <= AB)
    if first:
        valid = valid & (kj >= AB)
    kk = jnp.concatenate([kp, kc], axis=0)
    vv = jnp.concatenate([vp, vc], axis=0)
    o2 = jnp.zeros((AB, 128), f32)
    lse2 = jnp.zeros((AB, 128), f32)
    for e in range(2):
        hm = ((lane >= 64 * e) & (lane < 64 * (e + 1))).astype(f32)
        s = dot_nt(q2 * (hm * 0.125), kk)
        s = jnp.where(valid, s, -jnp.inf)
        m = jnp.max(s, axis=-1, keepdims=True)
        p = jnp.exp(s - m)
        den = jnp.sum(p, axis=-1, keepdims=True)
        o2 = o2 + dot_nn(p, vv * hm) / den
        lse2 = lse2 + (m + jnp.log(den)) * hm
    return o2, lse2


def _att_blocks(dil):
    m = L // dil
    return [(r * m + n * AB, n == 0) for r in range(dil) for n in range(m // AB)]


def deinterleave(x, dil):
    return x if dil == 1 else x.reshape(L // dil, dil, x.shape[1]).swapaxes(0, 1).reshape(L, x.shape[1])


def attn_fwd(qkv, pos, invf, g, plan=None):
    blocks = _att_blocks(DILS[g])

    def body(q_ref, k_ref, v_ref, pos_ref, invf_ref, o_ref, l_ref, qr, kr):
        cosf, s_lo, s_hi = _rope_tables(pos_ref, invf_ref)
        qr[...] = _rope(q_ref[...], cosf, s_lo, s_hi)
        kr[...] = _rope(k_ref[...], cosf, s_lo, s_hi)
        for off, first in blocks:
            cur, prv = pl.ds(off, AB), pl.ds(off if first else off - AB, AB)
            o2, lse2 = _att_block(qr[cur, :], kr[prv, :], kr[cur, :], v_ref[prv, :], v_ref[cur, :], first)
            o_ref[cur, :] = o2
            l_ref[cur, :] = lse2

    def sec(n):
        return pl.BlockSpec((L, 128), lambda p: (0, p + 4 * n))

    return pcall(
        body, plan, grid=(4,),
        in_specs=[sec(0), sec(1), sec(2), pl.BlockSpec((L, 1), lambda p: (0, 0)), pl.BlockSpec((1, 128), lambda p: (0, 0))],
        out_specs=[sec(0), sec(0)], out_shape=[S((L, 512), f32), S((L, 512), f32)],
        scratch_shapes=[pltpu.VMEM((L, 128), f32), pltpu.VMEM((L, 128), f32)],
        sem=("parallel",), name=f"attn_fwd{g}", args=[qkv, qkv, qkv, pos, invf])


def _att_block_bwd(q2, kp, kc, vp, vc, lse2, do2, dl2, first):
    lane = lax.broadcasted_iota(jnp.int32, (1, 128), 1)
    qi = lax.broadcasted_iota(jnp.int32, (AB, 2 * AB), 0) + AB
    kj = lax.broadcasted_iota(jnp.int32, (AB, 2 * AB), 1)
    back = qi - kj
    valid = (back >= 0) & (back <= AB)
    if first:
        valid = valid & (kj >= AB)
    kk = jnp.concatenate([kp, kc], axis=0)
    vv = jnp.concatenate([vp, vc], axis=0)
    dq2 = jnp.zeros((AB, 128), f32)
    dkk = jnp.zeros((2 * AB, 128), f32)
    dvv = jnp.zeros((2 * AB, 128), f32)
    for e in range(2):
        hb = (lane >= 64 * e) & (lane < 64 * (e + 1))
        hm = hb.astype(f32)
        qs = q2 * (hm * 0.125)
        lse = jnp.max(jnp.where(hb, lse2, -jnp.inf), axis=-1, keepdims=True)
        dls = jnp.sum(dl2 * hm, axis=-1, keepdims=True)
        p = jnp.where(valid, jnp.exp(dot_nt(qs, kk) - lse), 0.0)
        dov = do2 * hm
        dp = dot_nt(dov, vv)
        ds = p * (dp - jnp.sum(p * dp, axis=-1, keepdims=True) + dls)
        dq2 = dq2 + dot_nn(ds, kk) * (hm * 0.125)
        dkk = dkk + dot_tn(ds, qs)
        dvv = dvv + dot_tn(p, dov)
    return dq2, dkk[:AB], dkk[AB:], dvv[:AB], dvv[AB:]


def attn_bwd(qkv, pos, invf, lse, do, dl, g, plan=None):
    blocks = _att_blocks(DILS[g])

    def body(q_ref, k_ref, v_ref, pos_ref, invf_ref, l_ref, do_ref, dl_ref, d_ref, qr, kr, dqr, dkr, dvr):
        cosf, s_lo, s_hi = _rope_tables(pos_ref, invf_ref)
        qr[...] = _rope(q_ref[...], cosf, s_lo, s_hi)
        kr[...] = _rope(k_ref[...], cosf, s_lo, s_hi)
        for off, first in blocks:
            cur, prv = pl.ds(off, AB), pl.ds(off if first else off - AB, AB)
            dq2, dkp, dkc, dvp, dvc = _att_block_bwd(qr[cur, :], kr[prv, :], kr[cur, :], v_ref[prv, :], v_ref[cur, :],
                                                     l_ref[cur, :], do_ref[cur, :], dl_ref[cur, :], first)
            dqr[cur, :] = dq2
            dkr[cur, :] = dkc
            dvr[cur, :] = dvc
            if not first:
                dkr[prv, :] += dkp
                dvr[prv, :] += dvp
        d_ref[0] = _rope_t(dqr[...], cosf, s_lo, s_hi).astype(d_ref.dtype)
        d_ref[1] = _rope_t(dkr[...], cosf, s_lo, s_hi).astype(d_ref.dtype)
        d_ref[2] = dvr[...].astype(d_ref.dtype)

    def sec(n):
        return pl.BlockSpec((L, 128), lambda p: (0, p + 4 * n))

    return pcall(
        body, plan, grid=(4,),
        in_specs=[sec(0), sec(1), sec(2), pl.BlockSpec((L, 1), lambda p: (0, 0)), pl.BlockSpec((1, 128), lambda p: (0, 0)),
                  sec(0), sec(0), sec(0)],
        out_specs=pl.BlockSpec((3, L, 128), lambda p: (0, 0, p)), out_shape=S((3, L, 512), BF),
        scratch_shapes=[pltpu.VMEM((L, 128), f32)] * 5,
        sem=("parallel",), name=f"attn_bwd{g}", args=[qkv, qkv, qkv, pos, invf, lse, do, dl])


def _merge(o0, o1, o2, l0, l1, l2):
    m = jnp.maximum(jnp.maximum(l0, l1), l2)
    e0, e1, e2 = jnp.exp(l0 - m), jnp.exp(l1 - m), jnp.exp(l2 - m)
    return (e0 * o0 + e1 * o1 + e2 * o2) / (e0 + e1 + e2)


def _to_token_major(src_ref, scr, i, dil, slab):
    n = TR // dil
    for r in range(dil):
        rows = pl.ds(pl.multiple_of(r * (L // dil) + i * n, n), n)
        scr[pl.ds(r, n, stride=dil), :] = src_ref[rows, slab * 128:(slab + 1) * 128].astype(f32)
    return scr[...]


def _to_class_major(val, dst_ref, scr, i, dil, slab):
    n = TR // dil
    scr[...] = val
    for r in range(dil):
        rows = pl.ds(pl.multiple_of(r * (L // dil) + i * n, n), n)
        dst_ref[rows, slab * 128:(slab + 1) * 128] = scr[pl.ds(r, n, stride=dil), :].astype(dst_ref.dtype)


def rms_fwd_classes(x, g, name):
    def body(x_ref, g_ref, o_ref, o1_ref, o2_ref, scr):
        i = pl.program_id(0)
        y = _rms(x_ref[...], g_ref[...])
        o_ref[...] = y.astype(o_ref.dtype)
        for s in range(D // 128):
            ys = y[:, s * 128:(s + 1) * 128]
            _to_class_major(ys, o1_ref, scr, i, DILS[1], s)
            _to_class_major(ys, o2_ref, scr, i, DILS[2], s)

    row = pl.BlockSpec((TR, D), lambda i: (i, 0))
    full = pl.BlockSpec((L, D), lambda i: (0, 0))
    return pl.pallas_call(
        body, grid=(L // TR,), in_specs=[row, pl.BlockSpec((1, D), lambda i: (0, 0))], out_specs=[row, full, full],
        out_shape=[S((L, D), BF)] * 3, scratch_shapes=[pltpu.VMEM((TR, 128), f32)],
        compiler_params=_cp(("arbitrary",)), name=name)(x, g)


def rms_bwd_classes(x, g, dy0, dyc, dres, name, plan=None):
    def body(x_ref, g_ref, dy0_ref, d1_ref, d2_ref, dr_ref, dh_ref, dg_ref, scr, dyf):
        i = pl.program_id(0)
        for s in range(D // 128):
            sl = slice(s * 128, (s + 1) * 128)
            dyf[:, sl] = (dy0_ref[:, sl] + _to_token_major(d1_ref, scr.at[0], i, DILS[1], s)
                          + _to_token_major(d2_ref, scr.at[1], i, DILS[2], s))
        _, vjp = jax.vjp(_rms, x_ref[...], g_ref[...])
        dx, dg = vjp(dyf[...])
        dh_ref[...] = dr_ref[...] + dx

        @pl.when(i == 0)
        def _():
            dg_ref[...] = jnp.zeros_like(dg_ref)

        dg_ref[...] += dg

    row = pl.BlockSpec((TR, D), lambda i: (i, 0))
    vec = pl.BlockSpec((1, D), lambda i: (0, 0))
    full = pl.BlockSpec((L, D), lambda i: (0, 0))
    return pcall(body, plan, grid=(L // TR,), in_specs=[row, vec, row, full, full, row], out_specs=[row, vec],
                 out_shape=[S((L, D), f32), S((1, D), f32)],
                 scratch_shapes=[pltpu.VMEM((2, TR, 128), f32), pltpu.VMEM((TR, D), f32)],
                 sem=("arbitrary",), name=name, args=[x, g, dy0, dyc[0], dyc[1], dres])


def attn_merge_fwd(o0, l0, oc, lc, plan=None):
    def body(o0_ref, l0_ref, o1_ref, l1_ref, o2_ref, l2_ref, o_ref, scr):
        i = pl.program_id(0)
        for s in range(4):
            sl = slice(s * 128, (s + 1) * 128)
            o1 = _to_token_major(o1_ref, scr.at[0], i, DILS[1], s)
            l1 = _to_token_major(l1_ref, scr.at[1], i, DILS[1], s)
            o2 = _to_token_major(o2_ref, scr.at[2], i, DILS[2], s)
            l2 = _to_token_major(l2_ref, scr.at[3], i, DILS[2], s)
            o_ref[:, sl] = _merge(o0_ref[:, sl], o1, o2, l0_ref[:, sl], l1, l2).astype(o_ref.dtype)

    blk = pl.BlockSpec((TR, 512), lambda i: (i, 0))
    full = pl.BlockSpec((L, 512), lambda i: (0, 0))
    return pcall(body, plan, grid=(L // TR,), in_specs=[blk, blk, full, full, full, full], out_specs=blk,
                 out_shape=S((L, 512), BF), scratch_shapes=[pltpu.VMEM((4, TR, 128), f32)],
                 sem=("arbitrary",), name="attn_merge_fwd", args=[o0, l0, oc[0], lc[0], oc[1], lc[1]])


def attn_merge_bwd(o0, l0, oc, lc, do, plan=None):
    def body(o0_ref, l0_ref, o1_ref, l1_ref, o2_ref, l2_ref, g_ref, do0, dl0, do1, dl1, do2, dl2, scr):
        i = pl.program_id(0)
        for s in range(4):
            sl = slice(s * 128, (s + 1) * 128)
            o1 = _to_token_major(o1_ref, scr.at[0], i, DILS[1], s)
            l1 = _to_token_major(l1_ref, scr.at[1], i, DILS[1], s)
            o2 = _to_token_major(o2_ref, scr.at[2], i, DILS[2], s)
            l2 = _to_token_major(l2_ref, scr.at[3], i, DILS[2], s)
            _, vjp = jax.vjp(_merge, o0_ref[:, sl], o1, o2, l0_ref[:, sl], l1, l2)
            g0, g1, g2, h0, h1, h2 = vjp(g_ref[:, sl].astype(f32))
            do0[:, sl] = g0.astype(do0.dtype)
            dl0[:, sl] = h0
            _to_class_major(g1, do1, scr.at[0], i, DILS[1], s)
            _to_class_major(h1, dl1, scr.at[1], i, DILS[1], s)
            _to_class_major(g2, do2, scr.at[2], i, DILS[2], s)
            _to_class_major(h2, dl2, scr.at[3], i, DILS[2], s)

    blk = pl.BlockSpec((TR, 512), lambda i: (i, 0))
    full = pl.BlockSpec((L, 512), lambda i: (0, 0))
    outs = pcall(body, plan, grid=(L // TR,), in_specs=[blk, blk, full, full, full, full, blk],
                 out_specs=[blk, blk, full, full, full, full],
                 out_shape=[S((L, 512), BF), S((L, 512), f32)] * 3, scratch_shapes=[pltpu.VMEM((4, TR, 128), f32)],
                 sem=("arbitrary",), name="attn_merge_bwd", args=[o0, l0, oc[0], lc[0], oc[1], lc[1], do])
    return [outs[0], outs[2], outs[4]], [outs[1], outs[3], outs[5]]


def _invf_lanes():
    half = 8
    inv = ROPE_THETA ** (-np.arange(half, dtype=np.float32) * 2.0 / 16.0)
    lane = np.arange(128) % 64
    return jnp.asarray(np.where(lane < 16, inv[lane % 8], 0.0).astype(np.float32)[None, :])


def hosted(C, host, fn):
    p = C.plan(host) if C is not None else None
    out = fn(p)
    if p is not None:
        C.done(p)
    return out


def _ffn_fwd(h, g_row, W, cb, layer, C):
    hn = rms_fwd(h, g_row, f"rms_ffn{layer}")
    hu = hosted(C, f"ffn_in{layer}", lambda p: matmul(hn, W[("ffn_w_in", layer)], mode="nn", tm=1024, tn=1408, tk=1024,
                                                      plan=p, name=f"ffn_in{layer}"))
    act = hosted(C, f"convact_fwd{layer}", lambda p: convact_fwd(hu, W[("ffn_conv_w", layer)], cb, layer, plan=p))
    h2 = hosted(C, f"ffn_out{layer}", lambda p: matmul(act, W[("ffn_w_out", layer)], mode="nn", tm=1024, tn=1024, tk=2816,
                                                       add=h, plan=p, name=f"ffn_out{layer}"))
    return h2, (hn, hu, act)


def _ffn_bwd(dh, h, g_row, W, cb, saved, layer, C, G):
    hn, hu, act = saved
    w_in, w_out = W[("ffn_w_in", layer)], W[("ffn_w_out", layer)]
    dact = hosted(C, f"ffn_out_dx{layer}", lambda p: matmul(dh, w_out, mode="nt", tm=1024, tn=1408, tk=1024, plan=p,
                                                          name=f"ffn_out_dx{layer}"))
    G[("ffn_w_out", layer)] = hosted(C, f"ffn_out_dw{layer}", lambda p: matmul(
        act, dh, mode="tn", tm=1408, tn=1024, tk=L, out_dtype=BF, plan=p, name=f"ffn_out_dw{layer}"))
    dhu, G[("ffn_conv_w", layer)], g_cb = hosted(
        C, f"convact_bwd{layer}", lambda p: convact_bwd(hu, W[("ffn_conv_w", layer)], cb, dact, layer, plan=p))
    dhn = hosted(C, f"ffn_in_dx{layer}", lambda p: matmul(dhu, w_in, mode="nt", tm=1024, tn=1024, tk=2816, plan=p,
                                                         name=f"ffn_in_dx{layer}"))
    G[("ffn_w_in", layer)] = hosted(C, f"ffn_in_dw{layer}", lambda p: matmul(
        hn, dhu, mode="tn", tm=1024, tn=1408, tk=L, out_dtype=BF, plan=p, name=f"ffn_in_dw{layer}"))
    dh2, g_norm = hosted(C, f"rms_ffn_bwd{layer}", lambda p: rms_bwd(h, g_row, [dhn], dh, f"rms_ffn_bwd{layer}", plan=p))
    return dh2, g_cb, g_norm


def local_step(x, pos, tgt, sm, W, C=None):
    G = C.grads if C is not None else {}
    nm, nf = sm["norm_mix"], sm["norm_ffn"]
    invf = _invf_lanes()
    are = sm["s5_A_re"].reshape(NST, 1)
    aim = sm["s5_A_im"].reshape(NST, 1)
    ldt = sm["s5_log_dt"].reshape(1, 32)
    bre = sm["s5_B_re"].reshape(NST, 16)
    bim = sm["s5_B_im"].reshape(NST, 16)
    cre = jnp.swapaxes(sm["s5_C_re"][0], 1, 2).reshape(NST, 16)
    cim = jnp.swapaxes(sm["s5_C_im"][0], 1, 2).reshape(NST, 16)
    drow = sm["s5_D"].reshape(1, S5W)
    wbr, wbi, wcr, wci, abr, abi = s5_params_fwd(are, aim, ldt, bre, bim, cre, cim)
    hn0 = rms_fwd(x, nm[0:1], "rms_mix0")
    cb3 = sm["ffn_conv_b3"]
    proj = hosted(C, "mix_in", lambda p: matmul(hn0, W[("mix_w_in", 0)], mode="nn", tm=1024, tn=1280, tk=1024, plan=p, name="mix_in"))
    xs_re, xs_im, y5 = hosted(C, "s5_scan_fwd", lambda p: s5_scan_fwd(proj, wbr, wbi, wcr, wci, abr, abi, drow, plan=p))
    oa = s5_glu_fwd(y5, W[("s5_glu_w", 0)], sm["s5_glu_b"])
    ob, ssave = hosted(C, "hgrn_fwd", lambda p: hgrn_fwd(proj, sm["hgrn_gamma"], sm["hgrn_norm"], plan=p))
    cat = jnp.concatenate([oa, ob], axis=1)
    h1 = matmul(cat, W[("mix_w_out", 0)], mode="nn", tm=1024, tn=1024, tk=1024, add=x, name="mix_out")
    h2, ffn0 = _ffn_fwd(h1, nf[0:1], W, cb3, 0, C)
    hn2_g = rms_fwd_classes(h2, nm[1:2], "rms_mix1")
    wqkv = W[("att_w_qkv", 0)]
    pos_g, qkv_g, oc_g, lc_g = [], [], [], []
    for g, dil in enumerate(DILS):
        pos_g.append(deinterleave(pos, dil))
        qkv_g.append(hosted(C, f"att_qkv{g}", lambda p: matmul(
            hn2_g[g], wqkv, mode="nn", tm=1024, tn=512, tk=1024, dims=(L, 1536, D),
            b_spec=pl.BlockSpec((D, 512), lambda i, j, k, g=g: (0, 3 * j + g)), plan=p, name=f"att_qkv{g}")))
        o_c, l_c = hosted(C, f"attn_fwd{g}", lambda p: attn_fwd(qkv_g[g], pos_g[g], invf, g, plan=p))
        oc_g.append(o_c)
        lc_g.append(l_c)
    o = hosted(C, "attn_merge_fwd", lambda p: attn_merge_fwd(oc_g[0], lc_g[0], oc_g[1:], lc_g[1:], plan=p))
    h3 = matmul(o, W[("att_w_o", 0)], mode="nn", tm=1024, tn=1024, tk=512, add=h2, name="att_o")
    h4, ffn1 = _ffn_fwd(h3, nf[1:2], W, cb3, 1, C)
    loss, dh, g_nfinal = loss_head(h4, sm["norm_final"].reshape(1, D), tgt)
    dh, g_cb1, g_nf1 = _ffn_bwd(dh, h3, nf[1:2], W, cb3, ffn1, 1, C, G)
    do = matmul(dh, W[("att_w_o", 0)], mode="nt", tm=1024, tn=512, tk=1024, name="att_o_dx")
    G[("att_w_o", 0)] = matmul(o, dh, mode="tn", tm=512, tn=1024, tk=L, out_dtype=BF, name="att_o_dw")
    do_g, dl_g = hosted(C, "attn_merge_bwd", lambda p: attn_merge_bwd(oc_g[0], lc_g[0], oc_g[1:], lc_g[1:], do, plan=p))
    dhn2_g, gq = [], []
    for g, dil in enumerate(DILS):
        d3 = hosted(C, f"attn_bwd{g}", lambda p: attn_bwd(qkv_g[g], pos_g[g], invf, lc_g[g], do_g[g], dl_g[g], g, plan=p))
        dx = matmul(d3, wqkv, mode="nt", tm=1024, tn=1024, tk=512, dims=(L, D, 1536),
                    a_spec=pl.BlockSpec((None, 1024, 512), lambda i, j, k: (k, i, 0)),
                    b_spec=pl.BlockSpec((D, 512), lambda i, j, k, g=g: (0, 3 * k + g)), name=f"att_qkv_dx{g}")
        dhn2_g.append(dx)
        gq.append(matmul(hn2_g[g], d3, mode="tn", tm=1024, tn=512, tk=L, out_dtype=BF, dims=(D, 1536, L),
                         b_spec=pl.BlockSpec((None, L, 512), lambda i, j, k: (j, k, 0)), name=f"att_qkv_dw{g}"))
    G[("att_w_qkv", 0)] = jnp.concatenate([gq[g][:, 512 * s:512 * (s + 1)] for s in range(3) for g in range(3)], axis=1)
    dh, g_nm1 = hosted(C, "rms_mix_bwd1", lambda p: rms_bwd_classes(h2, nm[1:2], dhn2_g[0], dhn2_g[1:], dh, "rms_mix_bwd1", plan=p))
    dh, g_cb0, g_nf0 = _ffn_bwd(dh, h1, nf[0:1], W, cb3, ffn0, 0, C, G)
    dmix = matmul(dh, W[("mix_w_out", 0)], mode="nt", tm=1024, tn=1024, tk=1024, name="mix_out_dx")
    G[("mix_w_out", 0)] = matmul(cat, dh, mode="tn", tm=1024, tn=1024, tk=L, out_dtype=BF, name="mix_out_dw")
    dy5, g_glu_w, g_glu_b = s5_glu_bwd(y5, W[("s5_glu_w", 0)], sm["s5_glu_b"], dmix)
    G[("s5_glu_w", 0)] = g_glu_w.astype(BF)
    du, gwbr, gwbi, gwcr, gwci, gabr, gabi, g_d = hosted(C, "s5_scan_bwd", lambda p: s5_scan_bwd(
        dy5, proj, xs_re, xs_im, wbr, wbi, wcr, wci, abr, abi, drow, plan=p))
    g_are, g_aim, g_ldt, g_bre, g_bim, g_cre, g_cim = s5_params_bwd(are, aim, ldt, bre, bim, cre, cim,
                                                                   (gwbr, gwbi, gwcr, gwci, gabr, gabi))
    small = {
        "norm_ffn": jnp.concatenate([g_nf0, g_nf1], axis=0), "norm_final": g_nfinal.reshape(D),
        "s5_A_re": g_are.reshape(1, 32, 64), "s5_A_im": g_aim.reshape(1, 32, 64), "s5_log_dt": g_ldt.reshape(1, 32),
        "s5_B_re": g_bre.reshape(1, 32, 64, 16), "s5_B_im": g_bim.reshape(1, 32, 64, 16),
        "s5_C_re": jnp.swapaxes(g_cre.reshape(1, 32, 64, 16), 2, 3), "s5_C_im": jnp.swapaxes(g_cim.reshape(1, 32, 64, 16), 2, 3),
        "s5_D": g_d.reshape(1, 32, 16), "s5_glu_b": g_glu_b, "ffn_conv_b": jnp.concatenate([g_cb0, g_cb1], axis=0),
    }
    if C is not None:
        C.small["small_early"] = _pack(small, SMALL_EARLY)
    dproj, g_gamma, g_hnorm = hosted(C, "hgrn_bwd", lambda p: hgrn_bwd(proj, sm["hgrn_gamma"], sm["hgrn_norm"], ssave, dmix, du,
                                                                       plan=p))
    dhn0 = hosted(C, "mix_in_dx", lambda p: matmul(dproj, W[("mix_w_in", 0)], mode="nt", tm=1024, tn=1024, tk=2560, plan=p,
                                                  name="mix_in_dx"))
    G[("mix_w_in", 0)] = matmul(hn0, dproj, mode="tn", tm=1024, tn=1280, tk=L, out_dtype=BF, name="mix_in_dw")
    gx, g_nm0 = hosted(C, "rms_mix_bwd0", lambda p: rms_bwd(x, nm[0:1], [dhn0], dh, "rms_mix_bwd0", plan=p))
    small.update({"norm_mix": jnp.concatenate([g_nm0, g_nm1], axis=0), "hgrn_gamma": g_gamma, "hgrn_norm": g_hnorm})
    if C is not None:
        C.small["small_late"] = _pack(small, SMALL_LATE)
    return loss, gx, G, small


BIG = ("mix_w_in", "mix_w_out", "s5_glu_w", "att_w_qkv", "att_w_o", "ffn_w_in", "ffn_w_out", "ffn_conv_w")
SMALL = ("norm_mix", "norm_ffn", "norm_final", "s5_A_re", "s5_A_im", "s5_log_dt", "s5_B_re", "s5_B_im", "s5_C_re", "s5_C_im",
         "s5_D", "s5_glu_b", "hgrn_gamma", "hgrn_norm", "ffn_conv_b")
SMALL_LATE = ("norm_mix", "hgrn_gamma", "hgrn_norm")
SMALL_EARLY = tuple(n for n in SMALL if n not in SMALL_LATE)


def cast_bf16(w, name, plan=None):
    nl, r, c = w.shape
    w2 = w.reshape(nl * r, c)
    tr = 256 if (nl * r) % 256 == 0 else nl * r

    def body(w_ref, o_ref):
        o_ref[...] = w_ref[...].astype(BF)

    out = pcall(body, plan, grid=(nl * r // tr,), in_specs=[pl.BlockSpec((tr, c), lambda i: (i, 0))],
                out_specs=pl.BlockSpec((tr, c), lambda i: (i, 0)), out_shape=S((nl * r, c), BF),
                sem=("parallel",), name=name, args=[w2])
    return out.reshape(nl, r, c)


SCHEDULE = {
    "cast_ffn_w_in": [("G", "mix_w_in", 0)],
    "mix_in": [("G", "mix_w_out", 0), ("G", "s5_glu_w", 0)],
    "s5_scan_fwd": [("G", "ffn_w_in", 0, (0, 2))],
    "hgrn_fwd": [("G", "ffn_w_in", 0, (1, 2)), ("G", "ffn_conv_w", 0), ("G", "ffn_conv_w", 1), ("G", "att_w_qkv", 0, (0, 2))],
    "ffn_in0": [("G", "ffn_w_out", 0)],
    "convact_fwd0": [("G", "att_w_qkv", 0, (1, 2))],
    "att_qkv0": [("G", "att_w_o", 0)],
    "attn_fwd0": [("G", "ffn_w_in", 1, (0, 2))],
    "attn_fwd1": [("G", "ffn_w_in", 1, (1, 2))],
    "attn_fwd2": [("G", "ffn_w_out", 1)],
    "convact_bwd1": [("P", "ffn_w_out", 1)],
    "ffn_in_dx1": [("A", "ffn_w_out", 1, (0, 2))],
    "ffn_in_dw1": [("A", "ffn_w_out", 1, (1, 2))],
    "rms_ffn_bwd1": [("P", "ffn_w_in", 1)],
    "attn_merge_bwd": [("P", "att_w_o", 0), ("A", "ffn_conv_w", 1), ("B", "ffn_w_out", 1)],
    "attn_bwd0": [("A", "ffn_w_in", 1, (0, 2)), ("A", "att_w_o", 0)],
    "attn_bwd1": [("A", "ffn_w_in", 1, (1, 2)), ("B", "att_w_o", 0), ("B", "ffn_conv_w", 1)],
    "attn_bwd2": [("B", "ffn_w_in", 1)],
    "rms_mix_bwd1": [("P", "att_w_qkv", 0)],
    "ffn_out_dx0": [("A", "att_w_qkv", 0, (0, 4))],
    "ffn_out_dw0": [("A", "att_w_qkv", 0, (1, 4))],
    "convact_bwd0": [("A", "att_w_qkv", 0, (2, 4)), ("A", "att_w_qkv", 0, (3, 4)), ("P", "ffn_w_out", 0)],
    "ffn_in_dx0": [("A", "ffn_w_out", 0, (0, 2)), ("B", "att_w_qkv", 0)],
    "ffn_in_dw0": [("A", "ffn_w_out", 0, (1, 2))],
    "rms_ffn_bwd0": [("P", "ffn_w_in", 0), ("B", "ffn_w_out", 0)],
    "s5_scan_bwd": [("A", "ffn_w_in", 0, (0, 2)), ("P", "mix_w_out", 0), ("P", "s5_glu_w", 0), ("A", "ffn_conv_w", 0)],
    "hgrn_bwd": [("A", "ffn_w_in", 0, (1, 2)), ("A", "mix_w_out", 0), ("A", "s5_glu_w", 0), ("B", "ffn_conv_w", 0),
                 ("A", "small_early", 0)],
    "mix_in_dx": [("B", "ffn_w_in", 0), ("B", "mix_w_out", 0), ("B", "s5_glu_w", 0), ("B", "small_early", 0)],
    "rms_mix_bwd0": [("P", "mix_w_in", 0)],
    "adam_att_w_o": [("A", "mix_w_in", 0), ("A", "small_late", 0)],
    "adam_s5_glu_w": [("B", "mix_w_in", 0), ("B", "small_late", 0)],
}


class Comm:
    def __init__(self, shards, shapes):
        self.shards, self.shapes = shards, shapes
        self.W, self.grads, self.slots = {}, {}, {}
        self.sib, self.pair = {}, {}
        self.small = {}

    def plan(self, host):
        items = SCHEDULE.get(host)
        if not items:
            return None
        p = Plan()
        for it in items:
            kind, name, l = it[:3]
            part, parts = it[3] if len(it) > 3 else (0, 1)
            if name.startswith("small"):
                sg = self.small[name]
                kdst = p.buf("slots:" + name, arr=self.slots.get(name), shape=S((8,) + sg.shape, f32), write=True)
                if kind == "A":
                    ReduceOp(p, p.buf("g:" + name, arr=sg), kdst, None, sg.shape, False, 0, 0, whole=True)
                else:
                    ForwardOp(p, kdst, None, whole=True)
                continue
            nl, R, C_ = self.shapes[name]
            rows = name in ROW_SHARDED
            r0, nr = part * (R // parts), R // parts
            if kind == "G":
                sh = self.shards[name]
                kdst = p.buf(f"W:{name}:{l}", arr=self.W.get((name, l)), shape=S((4 * R, C_) if rows else (R, 4 * C_), sh.dtype),
                             write=True)
                GatherOp(p, p.buf("shard:" + name, arr=sh), kdst, l, self.shapes[name], rows, r0, nr, split=(nr % 32 == 0))
            elif name == "ffn_conv_w":
                g = self.grads[(name, l)]
                kdst = p.buf("slots:" + name, arr=self.slots.get(name), shape=S((8, nl, R, C_), g.dtype), write=True)
                if kind == "A":
                    ReduceOp(p, p.buf(f"g:{name}:{l}", arr=g), kdst, l, self.shapes[name], rows, r0, nr)
                else:
                    ForwardOp(p, kdst, l)
            elif kind == "P":
                g = self.grads[(name, l)]
                ksib = p.buf(f"sib:{name}:{l}", shape=S((4 * R // 2, C_) if rows else (R // 2, 4 * C_), g.dtype), write=True)
                PairOp(p, p.buf(f"g:{name}:{l}", arr=g), ksib, self.shapes[name], rows)
            else:
                if (name, l) not in self.pair:
                    self.pair[(name, l)] = pair_sum(self.grads[(name, l)], self.sib[(name, l)], rows, R, f"pair_sum_{name}{l}")
                h = self.pair[(name, l)]
                kdst = p.buf("slots:" + name, arr=self.slots.get(name), shape=S((4, nl, R, C_), h.dtype), write=True)
                if kind == "A":
                    ReduceOp(p, p.buf(f"h:{name}:{l}", arr=h), kdst, l, self.shapes[name], rows, r0 // 2, nr // 2, half=True)
                else:
                    HalfForwardOp(p, kdst, l, self.shapes[name])
        return p

    def done(self, p):
        for k, arr in p.out.items():
            tag, name = k.split(":")[:2]
            if tag == "W":
                self.W[(name, int(k.split(":")[2]))] = arr
            elif tag == "sib":
                self.sib[(name, int(k.split(":")[2]))] = arr
            else:
                self.slots[name] = arr


def _adamw(w, g, m, v):
    m = B1 * m + (1.0 - B1) * g
    v = B2 * v + (1.0 - B2) * jnp.square(g)
    m_hat = m / (1.0 - B1 ** STEP)
    v_hat = v / (1.0 - B2 ** STEP)
    return -LR * (m_hat / (jnp.sqrt(v_hat) + AEPS) + WD * w), m, v


def adam_big(w, m, v, slots, name, plan=None):
    nl, R, C = w.shape
    ns = slots.shape[0]
    tr = 128 if R % 128 == 0 else (64 if R % 64 == 0 else R)

    def body(w_ref, m_ref, v_ref, s_ref, g_ref, d_ref, nm_ref, nv_ref):
        g = s_ref[0].astype(f32)
        for s in range(1, ns):
            g = g + s_ref[s].astype(f32)
        d, nm_, nv_ = _adamw(w_ref[...], g, m_ref[...], v_ref[...])
        g_ref[...] = g
        d_ref[...] = d
        nm_ref[...] = nm_
        nv_ref[...] = nv_

    blk = pl.BlockSpec((None, tr, C), lambda l, i: (l, i, 0))
    return pcall(body, plan, grid=(nl, R // tr),
                 in_specs=[blk, blk, blk, pl.BlockSpec((ns, None, tr, C), lambda l, i: (0, l, i, 0))],
                 out_specs=[blk] * 4, out_shape=[S((nl, R, C), f32)] * 4,
                 sem=("parallel", "parallel"), name=name, args=[w, m, v, slots])


def adam_small(w, m, v, slots, name):
    R = w.shape[0]
    tr = 256

    def body(w_ref, m_ref, v_ref, s_ref, g_ref, d_ref, nm_ref, nv_ref):
        g = s_ref[0]
        for s in range(1, 8):
            g = g + s_ref[s]
        d, nm_, nv_ = _adamw(w_ref[...], g, m_ref[...], v_ref[...])
        g_ref[...] = g
        d_ref[...] = d
        nm_ref[...] = nm_
        nv_ref[...] = nv_

    blk = pl.BlockSpec((tr, 128), lambda i: (i, 0))
    return pl.pallas_call(
        body, grid=(R // tr,), in_specs=[blk, blk, blk, pl.BlockSpec((8, tr, 128), lambda i: (0, i, 0))],
        out_specs=[blk] * 4, out_shape=[S((R, 128), f32)] * 4,
        compiler_params=_cp(("parallel",)), name=name)(w, m, v, slots)


def _pack(d, names):
    flat = jnp.concatenate([d[n].reshape(-1) for n in names])
    n = flat.shape[0]
    rows = -(-n // (256 * 128)) * 256
    return jnp.pad(flat, (0, rows * 128 - n)).reshape(rows, 128)


def _unpack(p, like, names):
    flat = p.reshape(-1)
    out, off = {}, 0
    for n in names:
        sz = math.prod(like[n].shape)
        out[n] = flat[off:off + sz].reshape(like[n].shape)
        off += sz
    return out


def kernel(x, positions, norm_mix, norm_ffn, norm_final, mix_w_in, mix_w_out, s5_A_re, s5_A_im, s5_log_dt, s5_B_re, s5_B_im, s5_C_re, s5_C_im, s5_D, s5_glu_w, s5_glu_b, hgrn_gamma, hgrn_norm, att_w_qkv, att_w_o, ffn_w_in, ffn_conv_w, ffn_conv_b, ffn_w_out, loss_target, m_norm_mix, m_norm_ffn, m_norm_final, m_mix_w_in, m_mix_w_out, m_s5_A_re, m_s5_A_im, m_s5_log_dt, m_s5_B_re, m_s5_B_im, m_s5_C_re, m_s5_C_im, m_s5_D, m_s5_glu_w, m_s5_glu_b, m_hgrn_gamma, m_hgrn_norm, m_att_w_qkv, m_att_w_o, m_ffn_w_in, m_ffn_conv_w, m_ffn_conv_b, m_ffn_w_out, v_norm_mix, v_norm_ffn, v_norm_final, v_mix_w_in, v_mix_w_out, v_s5_A_re, v_s5_A_im, v_s5_log_dt, v_s5_B_re, v_s5_B_im, v_s5_C_re, v_s5_C_im, v_s5_D, v_s5_glu_w, v_s5_glu_b, v_hgrn_gamma, v_hgrn_norm, v_att_w_qkv, v_att_w_o, v_ffn_w_in, v_ffn_conv_w, v_ffn_conv_b, v_ffn_w_out):
    a = dict(locals())
    weights = BIG + SMALL
    w = {n: a[n] for n in weights}
    m = {n: a["m_" + n] for n in weights}
    v = {n: a["v_" + n] for n in weights}
    shards = {"ffn_conv_w": ffn_conv_w}
    C = Comm(shards, {n: w[n].shape for n in BIG})
    for n in ("mix_w_in", "ffn_w_in", "mix_w_out", "s5_glu_w", "ffn_w_out", "att_w_qkv", "att_w_o"):
        shards[n] = hosted(C, "cast_" + n, lambda p: cast_bf16(w[n], "cast_" + n, plan=p))
    sm = {n: w[n] for n in SMALL}
    sm["ffn_conv_b3"] = ffn_conv_b.reshape(2, 1, 2 * DFF)
    loss, gx, _, _ = local_step(x[0], positions.reshape(L, 1), loss_target[0], sm, C.W, C)
    res = {}
    for n in ("att_w_o", "s5_glu_w", "ffn_w_in", "ffn_w_out", "att_w_qkv", "mix_w_out", "ffn_conv_w", "mix_w_in"):
        res[n] = hosted(C, "adam_" + n, lambda p: adam_big(w[n], m[n], v[n], C.slots[n], "adam_" + n, plan=p))
    for names, key in ((SMALL_EARLY, "small_early"), (SMALL_LATE, "small_late")):
        packed = adam_small(_pack(w, names), _pack(m, names), _pack(v, names), C.slots[key], "adam_" + key)
        small_out = [_unpack(p, w, names) for p in packed]
        for n in names:
            res[n] = tuple(so[n] for so in small_out)
    total = lax.psum(loss[0, 0], ("x", "y", "c"))
    order = ("norm_mix", "norm_ffn", "norm_final", "mix_w_in", "mix_w_out", "s5_A_re", "s5_A_im", "s5_log_dt", "s5_B_re", "s5_B_im",
             "s5_C_re", "s5_C_im", "s5_D", "s5_glu_w", "s5_glu_b", "hgrn_gamma", "hgrn_norm", "att_w_qkv", "att_w_o", "ffn_w_in",
             "ffn_conv_w", "ffn_conv_b", "ffn_w_out")
    return (total, gx[None], *[res[n][0] for n in order], *[res[n][1] for n in order], *[res[n][2] for n in order],
            *[res[n][3] for n in order])
```

```python
import functools
import math

import numpy as np
import jax
import jax.numpy as jnp
from jax import lax
from jax.experimental import pallas as pl
from jax.experimental.pallas import tpu as pltpu

f32 = jnp.float32
BF = jnp.bfloat16
HI = lax.Precision.HIGHEST
S = jax.ShapeDtypeStruct
MESH = pl.DeviceIdType.MESH

L = 2048
D = 1024
EPS = 1e-6
S5W = 512
NST = 2048
HGC = 64
HGB = 32
DFF = 2816
ROPE_THETA = 500000.0
LR, B1, B2, AEPS, WD, STEP = 0.001, 0.9, 0.999, 1e-08, 0.01, 10
VMEM_LIMIT = 56 * 1024 * 1024


def _cp(sem=None):
    return pltpu.CompilerParams(dimension_semantics=sem, vmem_limit_bytes=VMEM_LIMIT)


ANY = pl.BlockSpec(memory_space=pl.ANY)
ROW_SHARDED = ("mix_w_out", "s5_glu_w", "ffn_w_out")


def _coords():
    x, y, c = lax.axis_index("x"), lax.axis_index("y"), lax.axis_index("c")
    return x, y, c, 2 * x + y, [(1 - x, y), (x, 1 - y), (1 - x, 1 - y)]


def _rows(start, n):
    return pl.ds(start if isinstance(start, int) else pl.multiple_of(start, 8), n)


def _cols(q, n):
    return pl.ds(pl.multiple_of(q * n, 128), n)


class Plan:
    def __init__(self):
        self.bufs, self.ops, self.nsem, self.out = {}, [], 0, {}

    def buf(self, key, arr=None, shape=None, write=False):
        b = self.bufs.setdefault(key, dict(arr=arr, shape=shape, write=False))
        b["write"] = b["write"] or write
        return key

    def add(self, op):
        op.base = self.nsem
        self.nsem += op.nsem
        self.ops.append(op)


class GatherOp:
    nsem = 13

    def __init__(self, plan, ksrc, kdst, l, shard_shape, rows, r0, nr, split):
        self.ksrc, self.kdst, self.l, (_, self.R, self.C), self.rows, self.r0, self.nr, self.split = (
            ksrc, kdst, l, shard_shape, rows, r0, nr, split)
        self.h = nr // 2 if split else nr
        plan.add(self)

    def _dst(self, R_, q, start, n):
        if self.rows:
            return R_[self.kdst].at[_rows(q * self.R + start, n), :]
        return R_[self.kdst].at[_rows(start, n), _cols(q, self.C)]

    def _mine(self, c):
        return self.r0 + (c * self.h if self.split else 0)

    def _theirs(self, c):
        return self.r0 + ((1 - c) * self.h if self.split else 0)

    def _copies(self, R_, sems):
        x, y, c, me, others = _coords()
        src = R_[self.ksrc]
        local = pltpu.make_async_copy(src.at[self.l, _rows(self.r0, self.nr), :], self._dst(R_, me, self.r0, self.nr),
                                      sems.at[self.base + 12])
        send, fwd = [], []
        for k, (px, py) in enumerate(others):
            q = 2 * px + py
            send.append((
                pltpu.make_async_remote_copy(src.at[self.l, _rows(self._mine(c), self.h), :], self._dst(R_, me, self._mine(c), self.h),
                                             sems.at[self.base + k], sems.at[self.base + 3 + k], device_id=(px, py, c), device_id_type=MESH),
                pltpu.make_async_remote_copy(src.at[self.l, _rows(self._mine(c), self.h), :], self._dst(R_, q, self._mine(c), self.h),
                                             sems.at[self.base + k], sems.at[self.base + 3 + k], device_id=(px, py, c), device_id_type=MESH)))
            fwd.append((
                pltpu.make_async_remote_copy(self._dst(R_, q, self._mine(c), self.h), self._dst(R_, q, self._mine(c), self.h),
                                             sems.at[self.base + 6 + k], sems.at[self.base + 9 + k], device_id=(x, y, 1 - c), device_id_type=MESH),
                pltpu.make_async_remote_copy(self._dst(R_, q, self._theirs(c), self.h), self._dst(R_, q, self._theirs(c), self.h),
                                             sems.at[self.base + 6 + k], sems.at[self.base + 9 + k], device_id=(x, y, 1 - c), device_id_type=MESH)))
        return local, send, fwd

    def start(self, R_, sems):
        local, send, _ = self._copies(R_, sems)
        local.start()
        for out, _ in send:
            out.start()

    def finish(self, R_, sems):
        local, send, fwd = self._copies(R_, sems)
        for k in range(3):
            send[k][1].wait_recv()
            if self.split:
                fwd[k][0].start()
        for k in range(3):
            if self.split:
                fwd[k][1].wait_recv()
                fwd[k][0].wait_send()
            send[k][0].wait_send()
        local.wait()


class ReduceOp:
    nsem = 7

    def __init__(self, plan, ksrc, kdst, l, shard_shape, rows, r0, nr, whole=False, half=False):
        self.ksrc, self.kdst, self.l, (self.R, self.C), self.rows, self.r0, self.nr, self.whole, self.half = (
            ksrc, kdst, l, shard_shape[-2:], rows, r0, nr, whole, half)
        plan.add(self)

    def _piece(self, R_, q):
        g = R_[self.ksrc]
        if self.whole:
            return g
        if self.rows:
            return g.at[_rows(q * (self.R // 2 if self.half else self.R) + self.r0, self.nr), :]
        return g.at[_rows(self.r0, self.nr), _cols(q, self.C)]

    def _slot(self, R_, q, c):
        if self.whole:
            return R_[self.kdst].at[2 * q + c]
        if self.half:
            return R_[self.kdst].at[q, self.l, _rows(c * (self.R // 2) + self.r0, self.nr), :]
        return R_[self.kdst].at[2 * q + c, self.l, _rows(self.r0, self.nr), :]

    def _copies(self, R_, sems):
        x, y, c, me, others = _coords()
        local = pltpu.make_async_copy(self._piece(R_, me), self._slot(R_, me, c), sems.at[self.base + 6])
        send = []
        for k, (px, py) in enumerate(others):
            q = 2 * px + py
            send.append((
                pltpu.make_async_remote_copy(self._piece(R_, q), self._slot(R_, me, c), sems.at[self.base + k],
                                             sems.at[self.base + 3 + k], device_id=(px, py, c), device_id_type=MESH),
                pltpu.make_async_remote_copy(self._piece(R_, q), self._slot(R_, q, c), sems.at[self.base + k],
                                             sems.at[self.base + 3 + k], device_id=(px, py, c), device_id_type=MESH)))
        return local, send

    def start(self, R_, sems):
        local, send = self._copies(R_, sems)
        local.start()
        for out, _ in send:
            out.start()

    def finish(self, R_, sems):
        local, send = self._copies(R_, sems)
        local.wait()
        for out, inn in send:
            inn.wait_recv()
            out.wait_send()


class ForwardOp:
    nsem = 8

    def __init__(self, plan, kdst, l, whole=False):
        self.kdst, self.l, self.whole = kdst, l, whole
        plan.add(self)

    def _slot(self, R_, s):
        return R_[self.kdst].at[s] if self.whole else R_[self.kdst].at[s, self.l]

    def _copies(self, R_, sems):
        x, y, c, me, others = _coords()
        return [(pltpu.make_async_remote_copy(self._slot(R_, 2 * q + c), self._slot(R_, 2 * q + c), sems.at[self.base + q],
                                              sems.at[self.base + 4 + q], device_id=(x, y, 1 - c), device_id_type=MESH),
                 pltpu.make_async_remote_copy(self._slot(R_, 2 * q + 1 - c), self._slot(R_, 2 * q + 1 - c), sems.at[self.base + q],
                                              sems.at[self.base + 4 + q], device_id=(x, y, 1 - c), device_id_type=MESH))
                for q in range(4)]

    def start(self, R_, sems):
        for out, _ in self._copies(R_, sems):
            out.start()

    def finish(self, R_, sems):
        for out, inn in self._copies(R_, sems):
            inn.wait_recv()
            out.wait_send()


class PairOp:
    nsem = 8

    def __init__(self, plan, ksrc, kdst, shard_shape, rows):
        self.ksrc, self.kdst, (self.R, self.C), self.rows = ksrc, kdst, shard_shape[-2:], rows
        plan.add(self)

    def _copies(self, R_, sems):
        x, y, c, me, others = _coords()
        g, dst, h = R_[self.ksrc], R_[self.kdst], self.R // 2
        out = []
        for q in range(4 if self.rows else 1):
            src = g.at[_rows(q * self.R + (1 - c) * h, h), :]
            land = dst.at[_rows(q * h, h), :]
            out.append(pltpu.make_async_remote_copy(src, land, sems.at[self.base + q], sems.at[self.base + 4 + q],
                                                    device_id=(x, y, 1 - c), device_id_type=MESH))
        return out

    def start(self, R_, sems):
        for cp in self._copies(R_, sems):
            cp.start()

    def finish(self, R_, sems):
        for cp in self._copies(R_, sems):
            cp.wait_recv()
            cp.wait_send()


class HalfForwardOp:
    nsem = 2

    def __init__(self, plan, kdst, l, shard_shape):
        self.kdst, self.l, self.R = kdst, l, shard_shape[-2]
        plan.add(self)

    def _copy(self, R_, sems, core):
        x, y, c, me, others = _coords()
        part = R_[self.kdst].at[:, self.l, _rows((c if core == "mine" else 1 - c) * (self.R // 2), self.R // 2), :]
        return pltpu.make_async_remote_copy(part, part, sems.at[self.base], sems.at[self.base + 1],
                                            device_id=(x, y, 1 - c), device_id_type=MESH)

    def start(self, R_, sems):
        self._copy(R_, sems, "mine").start()

    def finish(self, R_, sems):
        self._copy(R_, sems, "theirs").wait_recv()
        self._copy(R_, sems, "mine").wait_send()


def pair_sum(g, gsib, rows, R, name):
    h = R // 2
    W = g.shape[1]
    tr = h if h * W * 2 <= 2 ** 21 else 128
    nq = 4 if rows else 1

    def body(c_ref, a_ref, b_ref, o_ref):
        o_ref[...] = (a_ref[...].astype(f32) + b_ref[...].astype(f32)).astype(o_ref.dtype)

    half = pl.BlockSpec((tr, W), lambda q, i, c_ref: (q * (h // tr) + i, 0))
    mine = pl.BlockSpec((tr, W), lambda q, i, c_ref: (q * (R // tr) + c_ref[0] * (h // tr) + i, 0))
    return pl.pallas_call(
        body, grid_spec=pltpu.PrefetchScalarGridSpec(num_scalar_prefetch=1, grid=(nq, h // tr), in_specs=[mine, half],
                                                     out_specs=half),
        out_shape=S(gsib.shape, g.dtype), compiler_params=_cp(("parallel", "parallel")),
        name=name)(lax.axis_index("c").reshape(1).astype(jnp.int32), g, gsib)


def pcall(body, plan, *, grid, in_specs, out_specs, out_shape, scratch_shapes=(), sem, name, args):
    multi = isinstance(out_shape, (list, tuple))
    if plan is None or not plan.ops:
        return pl.pallas_call(body, grid=grid, in_specs=in_specs, out_specs=out_specs, out_shape=out_shape,
                              scratch_shapes=list(scratch_shapes), compiler_params=_cp(sem), name=name)(*args)
    outs = list(out_shape) if multi else [out_shape]
    ospecs = list(out_specs) if multi else [out_specs]
    kin = [k for k, b in plan.bufs.items() if b["arr"] is not None]
    kout = [k for k, b in plan.bufs.items() if b["write"]]
    n_in, n_out, n_scr = len(in_specs), len(outs), len(scratch_shapes)

    def wrapped(*refs):
        o0 = n_in + len(kin)
        s0 = o0 + n_out + len(kout)
        R_ = dict(zip(kin, refs[n_in:o0]))
        R_.update(zip(kout, refs[o0 + n_out:s0]))
        sems = refs[s0 + n_scr]
        first = functools.reduce(jnp.logical_and, [pl.program_id(d) == 0 for d in range(len(grid))])
        last = functools.reduce(jnp.logical_and, [pl.program_id(d) == grid[d] - 1 for d in range(len(grid))])

        @pl.when(first)
        def _():
            for op in plan.ops:
                op.start(R_, sems)

        body(*refs[:n_in], *refs[o0:o0 + n_out], *refs[s0:s0 + n_scr])

        @pl.when(last)
        def _():
            for op in plan.ops:
                op.finish(R_, sems)

    def shape_of(k):
        b = plan.bufs[k]
        return S(b["arr"].shape, b["arr"].dtype) if b["arr"] is not None else b["shape"]

    res = pl.pallas_call(
        wrapped, grid=grid, in_specs=list(in_specs) + [ANY] * len(kin), out_specs=ospecs + [ANY] * len(kout),
        out_shape=outs + [shape_of(k) for k in kout],
        scratch_shapes=list(scratch_shapes) + [pltpu.SemaphoreType.DMA((plan.nsem,))],
        input_output_aliases={n_in + kin.index(k): n_out + kout.index(k) for k in kout if plan.bufs[k]["arr"] is not None},
        compiler_params=pltpu.CompilerParams(dimension_semantics=("arbitrary",) * len(grid), vmem_limit_bytes=VMEM_LIMIT,
                                             has_side_effects=True),
        name=name)(*args, *[plan.bufs[k]["arr"] for k in kin])
    plan.out = dict(zip(kout, res[n_out:]))
    return list(res[:n_out]) if multi else res[0]


def _dg(a, b, ca, cb):
    return lax.dot_general(a.astype(BF), b.astype(BF), (((ca,), (cb,)), ((), ())), preferred_element_type=f32)


@jax.custom_vjp
def dot_nn(a, b):
    return _dg(a, b, 1, 0)


@jax.custom_vjp
def dot_nt(a, b):
    return _dg(a, b, 1, 1)


@jax.custom_vjp
def dot_tn(a, b):
    return _dg(a, b, 0, 0)


dot_nn.defvjp(lambda a, b: (dot_nn(a, b), (a, b)),
              lambda r, g: (dot_nt(g, r[1]).astype(r[0].dtype), dot_tn(r[0], g).astype(r[1].dtype)))
dot_nt.defvjp(lambda a, b: (dot_nt(a, b), (a, b)),
              lambda r, g: (dot_nn(g, r[1]).astype(r[0].dtype), dot_tn(g, r[0]).astype(r[1].dtype)))
dot_tn.defvjp(lambda a, b: (dot_tn(a, b), (a, b)),
              lambda r, g: (dot_nt(r[1], g).astype(r[0].dtype), dot_nn(r[0], g).astype(r[1].dtype)))


def matmul(a, b, *, mode, tm, tn, tk, out_dtype=f32, add=None, b_lead=None, a_spec=None, b_spec=None, dims=None, plan=None, name):
    a_over, b_over = a_spec, b_spec
    if mode == "nn":
        (M, K), N = a.shape[-2:], b.shape[-1]
        a_spec = pl.BlockSpec((tm, tk), lambda i, j, k: (i, k))
        b_blk, b_idx, ca, cb = (tk, tn), (lambda i, j, k: (k, j)), 1, 0
    elif mode == "nt":
        (M, K), N = a.shape[-2:], b.shape[-2]
        a_spec = pl.BlockSpec((tm, tk), lambda i, j, k: (i, k))
        b_blk, b_idx, ca, cb = (tn, tk), (lambda i, j, k: (j, k)), 1, 1
    else:
        (K, M), N = a.shape[-2:], b.shape[-1]
        a_spec = pl.BlockSpec((tk, tm), lambda i, j, k: (k, i))
        b_blk, b_idx, ca, cb = (tk, tn), (lambda i, j, k: (k, j)), 0, 0
    if dims is not None:
        M, N, K = dims
    assert M % tm == 0 and N % tn == 0 and K % tk == 0, (name, M, N, K, tm, tn, tk)
    if b_lead is None:
        b_spec = pl.BlockSpec(b_blk, b_idx)
    else:
        b_spec = pl.BlockSpec((None,) + b_blk, lambda i, j, k: (b_lead,) + b_idx(i, j, k))
    if a_over is not None:
        a_spec = a_over
    if b_over is not None:
        b_spec = b_over
    nk = K // tk
    has_add = add is not None

    def body(*refs):
        a_ref, b_ref = refs[0], refs[1]
        add_ref = refs[2] if has_add else None
        o_ref = refs[2 + has_add]
        p = _dg(a_ref[...], b_ref[...], ca, cb)

        def fin(v):
            if has_add:
                v = v + add_ref[...].astype(f32)
            o_ref[...] = v.astype(o_ref.dtype)

        if nk == 1:
            fin(p)
        else:
            acc = refs[3 + has_add]
            k = pl.program_id(2)

            @pl.when(k == 0)
            def _():
                acc[...] = p

            @pl.when(k > 0)
            def _():
                acc[...] += p

            @pl.when(k == nk - 1)
            def _():
                fin(acc[...])

    in_specs = [a_spec, b_spec]
    args = [a, b]
    if has_add:
        in_specs.append(pl.BlockSpec((tm, tn), lambda i, j, k: (i, j)))
        args.append(add)
    return pcall(body, plan, grid=(M // tm, N // tn, nk), in_specs=in_specs,
                 out_specs=pl.BlockSpec((tm, tn), lambda i, j, k: (i, j)), out_shape=S((M, N), out_dtype),
                 scratch_shapes=[pltpu.VMEM((tm, tn), f32)] if nk > 1 else [],
                 sem=("parallel", "parallel", "arbitrary"), name=name, args=args)


def _rms(xv, gv):
    return xv * lax.rsqrt(jnp.mean(xv * xv, axis=-1, keepdims=True) + EPS) * gv


TR = 256


def rms_fwd(x, g, name):
    def body(x_ref, g_ref, o_ref):
        o_ref[...] = _rms(x_ref[...], g_ref[...]).astype(o_ref.dtype)

    return pl.pallas_call(
        body, grid=(L // TR,),
        in_specs=[pl.BlockSpec((TR, D), lambda i: (i, 0)), pl.BlockSpec((1, D), lambda i: (0, 0))],
        out_specs=pl.BlockSpec((TR, D), lambda i: (i, 0)), out_shape=S((L, D), BF),
        compiler_params=_cp(("parallel",)), name=name)(x, g)


def rms_bwd(x, g, dys, dres, name, plan=None):
    nd = len(dys)

    def body(*refs):
        x_ref, g_ref = refs[0], refs[1]
        dr_ref, dh_ref, dg_ref = refs[2 + nd:]
        dy = refs[2][...].astype(f32)
        for r in refs[3:2 + nd]:
            dy = dy + r[...].astype(f32)
        _, vjp = jax.vjp(_rms, x_ref[...], g_ref[...])
        dx, dg = vjp(dy)
        dh_ref[...] = dr_ref[...] + dx

        @pl.when(pl.program_id(0) == 0)
        def _():
            dg_ref[...] = jnp.zeros_like(dg_ref)

        dg_ref[...] += dg

    row = pl.BlockSpec((TR, D), lambda i: (i, 0))
    vec = pl.BlockSpec((1, D), lambda i: (0, 0))
    return pcall(body, plan, grid=(L // TR,), in_specs=[row, vec] + [row] * (nd + 1), out_specs=[row, vec],
                 out_shape=[S((L, D), f32), S((1, D), f32)], sem=("arbitrary",), name=name, args=[x, g, *dys, dres])


def loss_head(h, g, tgt):
    def f(hv, gv, tv):
        y = _rms(hv, gv)
        return 0.5 * jnp.sum(jnp.mean(jnp.square(y - tv), axis=-1))

    def body(h_ref, g_ref, t_ref, l_ref, dh_ref, dg_ref):
        val, vjp = jax.vjp(f, h_ref[...], g_ref[...], t_ref[...])
        dh, dg, _ = vjp(jnp.ones((), f32))
        dh_ref[...] = dh

        @pl.when(pl.program_id(0) == 0)
        def _():
            dg_ref[...] = jnp.zeros_like(dg_ref)
            l_ref[...] = jnp.zeros_like(l_ref)

        dg_ref[...] += dg
        l_ref[...] += jnp.full((1, 128), val, f32)

    row = pl.BlockSpec((TR, D), lambda i: (i, 0))
    vec = pl.BlockSpec((1, D), lambda i: (0, 0))
    return pl.pallas_call(
        body, grid=(L // TR,), in_specs=[row, vec, row],
        out_specs=[pl.BlockSpec((1, 128), lambda i: (0, 0)), row, vec],
        out_shape=[S((1, 128), f32), S((L, D), f32), S((1, D), f32)],
        compiler_params=_cp(("arbitrary",)), name="loss_head")(h, g, tgt)


def _col_to_row(c):
    n = c.shape[0]
    t = jnp.broadcast_to(c, (n, 128)).T
    r = lax.broadcasted_iota(jnp.int32, (128, n), 0)
    return jnp.sum(jnp.where(r == 0, t, 0.0), axis=0, keepdims=True)


def _s5_param_map(are, aim, ldt_row, bre, bim, cre, cim):
    n = NST
    gi = lax.broadcasted_iota(jnp.int32, (n, 32), 0) // 64
    gj = lax.broadcasted_iota(jnp.int32, (n, 32), 1)
    ldt = jnp.sum(jnp.where(gi == gj, ldt_row, 0.0), axis=1, keepdims=True)
    dt = jnp.exp(ldt)
    mag = jnp.exp(are * dt)
    abr = mag * jnp.cos(aim * dt)
    abi = mag * jnp.sin(aim * dt)
    den = are * are + aim * aim
    nr, ni = abr - 1.0, abi
    cr = (nr * are + ni * aim) / den
    ci = (ni * are - nr * aim) / den
    bbr = cr * bre - ci * bim
    bbi = cr * bim + ci * bre
    tc = lax.broadcasted_iota(jnp.int32, (16, 128), 0)
    tl = lax.broadcasted_iota(jnp.int32, (16, 128), 1)
    T = (tl % 16 == tc).astype(f32)
    mr = (lax.broadcasted_iota(jnp.int32, (n, 128), 0) // 64) % 8
    mc = lax.broadcasted_iota(jnp.int32, (n, 128), 1) // 16
    mask = (mr == mc).astype(f32)

    def expand(v):
        return jnp.dot(v, T, precision=HI, preferred_element_type=f32) * mask

    return expand(bbr), expand(bbi), expand(cre), expand(cim), _col_to_row(abr), _col_to_row(abi)


def s5_params_fwd(are, aim, ldt_row, bre, bim, cre, cim):
    def body(*refs):
        outs = _s5_param_map(*[r[...] for r in refs[:7]])
        for o_ref, o in zip(refs[7:], outs):
            o_ref[...] = o

    return pl.pallas_call(
        body, out_shape=[S((NST, 128), f32)] * 4 + [S((1, NST), f32)] * 2,
        compiler_params=_cp(), name="s5_params_fwd")(are, aim, ldt_row, bre, bim, cre, cim)


def s5_params_bwd(are, aim, ldt_row, bre, bim, cre, cim, cots):
    def body(*refs):
        _, vjp = jax.vjp(_s5_param_map, *[r[...] for r in refs[:7]])
        gs = vjp(tuple(r[...] for r in refs[7:13]))
        for o_ref, o in zip(refs[13:], gs):
            o_ref[...] = o

    return pl.pallas_call(
        body, out_shape=[S((NST, 1), f32)] * 2 + [S((1, 32), f32)] + [S((NST, 16), f32)] * 4,
        compiler_params=_cp(), name="s5_params_bwd")(are, aim, ldt_row, bre, bim, cre, cim, *cots)


def _cpowers(ar, ai):
    out = [(ar, ai)]
    for _ in range(7):
        pr, pi = out[-1]
        out.append((pr * ar - pi * ai, pr * ai + pi * ar))
    return out


def _ctable(pw, rid, power):
    tr_ = jnp.zeros(rid.shape, f32)
    ti_ = jnp.zeros(rid.shape, f32)
    for r in range(8):
        pr, pi = pw[power(r) - 1]
        tr_ = jnp.where(rid == r, pr, tr_)
        ti_ = jnp.where(rid == r, pi, ti_)
    return tr_, ti_


NT5 = 4
RC = 256


def s5_scan_fwd(proj, wbr, wbi, wcr, wci, abr, abi, drow, plan=None):
    def body(u_ref, wbr_ref, wbi_ref, wcr_ref, wci_ref, ar_ref, ai_ref, d_ref, xr_ref, xi_ref, y_ref):
        wbr_v, wbi_v = wbr_ref[...], wbi_ref[...]
        for r in range(L // RC):
            rows = pl.ds(r * RC, RC)
            ub = u_ref[rows, :]
            xr_ref[rows, :] = dot_nt(ub, wbr_v)
            xi_ref[rows, :] = dot_nt(ub, wbi_v)
        pw = _cpowers(ar_ref[...], ai_ref[...])
        rid = lax.broadcasted_iota(jnp.int32, (8, 512), 0)
        tr_, ti_ = _ctable(pw, rid, lambda r: r + 1)

        def group(j, c):
            cr, ci = c
            rows = pl.ds(pl.multiple_of(j * 8, 8), 8)
            br, bi = xr_ref[rows, :], xi_ref[rows, :]
            for s in (1, 2, 4):
                pr, pi = pw[s - 1]
                sr = jnp.where(rid >= s, pltpu.roll(br, s, 0), 0.0)
                si = jnp.where(rid >= s, pltpu.roll(bi, s, 0), 0.0)
                br, bi = br + pr * sr - pi * si, bi + pr * si + pi * sr
            br, bi = br + tr_ * cr - ti_ * ci, bi + tr_ * ci + ti_ * cr
            xr_ref[rows, :] = br
            xi_ref[rows, :] = bi
            return br[7:8], bi[7:8]

        z = jnp.zeros((1, 512), f32)
        lax.fori_loop(0, L // 8, group, (z, z), unroll=2)
        wcr_v, wci_v, dv = wcr_ref[...], wci_ref[...], d_ref[...]
        for r in range(L // RC):
            rows = pl.ds(r * RC, RC)
            y_ref[rows, :] = (dot_nn(xr_ref[rows, :], wcr_v) - dot_nn(xi_ref[rows, :], wci_v)
                              + dv * u_ref[rows, :])

    wspec = pl.BlockSpec((512, 128), lambda j: (j, 0))
    aspec = pl.BlockSpec((1, 512), lambda j: (0, j))
    return pcall(
        body, plan, grid=(NT5,),
        in_specs=[pl.BlockSpec((L, 128), lambda j: (0, j)), wspec, wspec, wspec, wspec, aspec, aspec,
                  pl.BlockSpec((1, 128), lambda j: (0, j))],
        out_specs=[pl.BlockSpec((L, 512), lambda j: (0, j)), pl.BlockSpec((L, 512), lambda j: (0, j)),
                   pl.BlockSpec((L, 128), lambda j: (0, j))],
        out_shape=[S((L, NST), f32), S((L, NST), f32), S((L, S5W), f32)],
        sem=("parallel",), name="s5_scan_fwd", args=[proj, wbr, wbi, wcr, wci, abr, abi, drow])


def s5_scan_bwd(dy, proj, xs_re, xs_im, wbr, wbi, wcr, wci, abr, abi, drow, plan=None):
    def body(dy_ref, u_ref, xr_ref, xi_ref, wbr_ref, wbi_ref, wcr_ref, wci_ref, ar_ref, ai_ref, d_ref,
             du_ref, gwbr_ref, gwbi_ref, gwcr_ref, gwci_ref, gar_ref, gai_ref, gd_ref, lr_ref, li_ref):
        wcr_v, wci_v = wcr_ref[...], wci_ref[...]
        gwcr = jnp.zeros((512, 128), f32)
        gwci = jnp.zeros((512, 128), f32)
        gd = jnp.zeros((1, 128), f32)
        for r in range(L // RC):
            rows = pl.ds(r * RC, RC)
            dyv = dy_ref[rows, :]
            lr_ref[rows, :] = dot_nt(dyv, wcr_v)
            li_ref[rows, :] = -dot_nt(dyv, wci_v)
            gwcr += dot_tn(xr_ref[rows, :], dyv)
            gwci -= dot_tn(xi_ref[rows, :], dyv)
            gd += jnp.sum(dyv * u_ref[rows, :], axis=0, keepdims=True)
        gwcr_ref[...] = gwcr
        gwci_ref[...] = gwci
        gd_ref[...] = gd
        pw = _cpowers(ar_ref[...], -ai_ref[...])
        rid = lax.broadcasted_iota(jnp.int32, (8, 512), 0)
        tr_, ti_ = _ctable(pw, rid, lambda r: 8 - r)

        def group(i, c):
            cr, ci, gar, gai = c
            j = L // 8 - 1 - i
            rows = pl.ds(pl.multiple_of(j * 8, 8), 8)
            br, bi = lr_ref[rows, :], li_ref[rows, :]
            for s in (1, 2, 4):
                pr, pi = pw[s - 1]
                sr = jnp.where(rid < 8 - s, pltpu.roll(br, 8 - s, 0), 0.0)
                si = jnp.where(rid < 8 - s, pltpu.roll(bi, 8 - s, 0), 0.0)
                br, bi = br + pr * sr - pi * si, bi + pr * si + pi * sr
            br, bi = br + tr_ * cr - ti_ * ci, bi + tr_ * ci + ti_ * cr
            lr_ref[rows, :] = br
            li_ref[rows, :] = bi
            nr = jnp.where(rid < 7, pltpu.roll(br, 7, 0), cr)
            ni = jnp.where(rid < 7, pltpu.roll(bi, 7, 0), ci)
            xr, xi = xr_ref[rows, :], xi_ref[rows, :]
            return br[0:1], bi[0:1], gar + xr * nr + xi * ni, gai + xr * ni - xi * nr

        z = jnp.zeros((1, 512), f32)
        z8 = jnp.zeros((8, 512), f32)
        _, _, gar, gai = lax.fori_loop(0, L // 8, group, (z, z, z8, z8), unroll=2)
        gar_ref[...] = jnp.sum(gar, axis=0, keepdims=True)
        gai_ref[...] = jnp.sum(gai, axis=0, keepdims=True)
        wbr_v, wbi_v, dv = wbr_ref[...], wbi_ref[...], d_ref[...]
        gwbr = jnp.zeros((512, 128), f32)
        gwbi = jnp.zeros((512, 128), f32)
        for r in range(L // RC):
            rows = pl.ds(r * RC, RC)
            lrv, liv, uv = lr_ref[rows, :], li_ref[rows, :], u_ref[rows, :]
            du_ref[rows, :] = (dot_nn(lrv, wbr_v) + dot_nn(liv, wbi_v) + dv * dy_ref[rows, :]).astype(du_ref.dtype)
            gwbr += dot_tn(lrv, uv)
            gwbi += dot_tn(liv, uv)
        gwbr_ref[...] = gwbr
        gwbi_ref[...] = gwbi

    wspec = pl.BlockSpec((512, 128), lambda j: (j, 0))
    aspec = pl.BlockSpec((1, 512), lambda j: (0, j))
    col = pl.BlockSpec((L, 128), lambda j: (0, j))
    st = pl.BlockSpec((L, 512), lambda j: (0, j))
    dspec = pl.BlockSpec((1, 128), lambda j: (0, j))
    return pcall(
        body, plan, grid=(NT5,),
        in_specs=[col, col, st, st, wspec, wspec, wspec, wspec, aspec, aspec, dspec],
        out_specs=[col, wspec, wspec, wspec, wspec, aspec, aspec, dspec],
        out_shape=[S((L, S5W), BF)] + [S((NST, 128), f32)] * 4 + [S((1, NST), f32)] * 2 + [S((1, S5W), f32)],
        scratch_shapes=[pltpu.VMEM((L, 512), f32), pltpu.VMEM((L, 512), f32)],
        sem=("parallel",), name="s5_scan_bwd", args=[dy, proj, xs_re, xs_im, wbr, wbi, wcr, wci, abr, abi, drow])


def _glu(y, w, b):
    z = jax.nn.gelu(y)
    return z * jax.nn.sigmoid(dot_nn(z, w) + b)


def s5_glu_fwd(y, w, b):
    def body(y_ref, w_ref, b_ref, o_ref):
        o_ref[...] = _glu(y_ref[...], w_ref[...], b_ref[...]).astype(o_ref.dtype)

    return pl.pallas_call(
        body, grid=(L // TR,),
        in_specs=[pl.BlockSpec((TR, S5W), lambda i: (i, 0)), pl.BlockSpec((S5W, S5W), lambda i: (0, 0)),
                  pl.BlockSpec((1, S5W), lambda i: (0, 0))],
        out_specs=pl.BlockSpec((TR, S5W), lambda i: (i, 0)), out_shape=S((L, S5W), BF),
        compiler_params=_cp(("parallel",)), name="s5_glu_fwd")(y, w, b)


def s5_glu_bwd(y, w, b, dmix):
    def body(y_ref, w_ref, b_ref, g_ref, dy_ref, dw_ref, db_ref):
        _, vjp = jax.vjp(_glu, y_ref[...], w_ref[...].astype(f32), b_ref[...])
        dy, dw, db = vjp(g_ref[...])
        dy_ref[...] = dy

        @pl.when(pl.program_id(0) == 0)
        def _():
            dw_ref[...] = jnp.zeros_like(dw_ref)
            db_ref[...] = jnp.zeros_like(db_ref)

        dw_ref[...] += dw
        db_ref[...] += db

    row = pl.BlockSpec((TR, S5W), lambda i: (i, 0))
    return pl.pallas_call(
        body, grid=(L // TR,),
        in_specs=[row, pl.BlockSpec((S5W, S5W), lambda i: (0, 0)), pl.BlockSpec((1, S5W), lambda i: (0, 0)), row],
        out_specs=[row, pl.BlockSpec((S5W, S5W), lambda i: (0, 0)), pl.BlockSpec((1, S5W), lambda i: (0, 0))],
        out_shape=[S((L, S5W), f32), S((S5W, S5W), f32), S((1, S5W), f32)],
        compiler_params=_cp(("arbitrary",)), name="s5_glu_bwd")(y, w, b, dmix)


def _dg3(a, b, ca, cb):
    ah, bh = a.astype(BF), b.astype(BF)
    al, bl = (a - ah.astype(f32)).astype(BF), (b - bh.astype(f32)).astype(BF)
    return _dg(ah, bh, ca, cb) + _dg(ah, bl, ca, cb) + _dg(al, bh, ca, cb)


@jax.custom_vjp
def hi_nn(a, b):
    return _dg3(a, b, 1, 0)


@jax.custom_vjp
def hi_nt(a, b):
    return _dg3(a, b, 1, 1)


@jax.custom_vjp
def hi_tn(a, b):
    return _dg3(a, b, 0, 0)


hi_nn.defvjp(lambda a, b: (hi_nn(a, b), (a, b)), lambda r, g: (hi_nt(g, r[1]), hi_tn(r[0], g)))
hi_nt.defvjp(lambda a, b: (hi_nt(a, b), (a, b)), lambda r, g: (hi_nn(g, r[1]), hi_tn(g, r[0])))
hi_tn.defvjp(lambda a, b: (hi_tn(a, b), (a, b)), lambda r, g: (hi_nt(r[1], g), hi_nn(r[0], g)))


def _hgrn_chunk(St, xq, xf, xi, xg, gam, ng):
    lb = jax.nn.sigmoid(gam[0:1] - gam[1:2])
    q = jax.nn.silu(xq)
    f = lb + (1.0 - lb) * jax.nn.sigmoid(xf)
    k = 1.0 - f
    g = jnp.log(f)
    ti = lax.broadcasted_iota(jnp.int32, (HGC, HGC), 0)
    si = lax.broadcasted_iota(jnp.int32, (HGC, HGC), 1)
    causal = si <= ti
    b = jnp.dot(causal.astype(f32), g, precision=HI, preferred_element_type=f32)
    qe = q * jnp.exp(b)
    o = dot_nt(qe, St)
    parts = []
    for i in range(HGC // HGB):
        r, n, mid = slice(HGB * i, HGB * (i + 1)), HGB * (i + 1), HGB * i + HGB // 2
        base = b[mid:mid + 1]
        sc = hi_nt(q[r] * jnp.exp(b[r] - base), k[:n] * jnp.exp(base - b[:n]))
        parts.append(dot_nn(jnp.where(causal[r, :n], sc, 0.0), xi[:n]))
    o = o + jnp.concatenate(parts, axis=0)
    bl = b[HGC - 1:HGC]
    St_new = St * jnp.exp(bl) + dot_tn(xi, k * jnp.exp(bl - b))
    o = o * lax.rsqrt(jnp.mean(o * o, axis=-1, keepdims=True) + EPS) * ng
    return St_new, o * jax.nn.silu(xg)


NCH = L // HGC


def hgrn_fwd(proj, gamma, hnorm, plan=None):
    def body(q_ref, f_ref, i_ref, g_ref, gam_ref, ng_ref, o_ref, ss_ref, st):
        @pl.when(pl.program_id(0) == 0)
        def _():
            st[...] = jnp.zeros_like(st)

        for h in range(4):
            sl = slice(h * 128, (h + 1) * 128)
            s0 = st[h]
            ss_ref[0, h] = s0
            s1, o = _hgrn_chunk(s0, q_ref[:, sl], f_ref[:, sl], i_ref[:, sl], g_ref[:, sl], gam_ref[:, sl], ng_ref[:, sl])
            st[h] = s1
            o_ref[:, sl] = o.astype(o_ref.dtype)

    def pj(n):
        return pl.BlockSpec((HGC, 512), lambda c: (c, n))

    return pcall(
        body, plan, grid=(NCH,),
        in_specs=[pj(1), pj(2), pj(3), pj(4), pl.BlockSpec((2, 512), lambda c: (0, 0)), pl.BlockSpec((1, 512), lambda c: (0, 0))],
        out_specs=[pl.BlockSpec((HGC, 512), lambda c: (c, 0)), pl.BlockSpec((1, 4, 128, 128), lambda c: (c, 0, 0, 0))],
        out_shape=[S((L, 512), BF), S((NCH, 4, 128, 128), f32)],
        scratch_shapes=[pltpu.VMEM((4, 128, 128), f32)],
        sem=("arbitrary",), name="hgrn_fwd", args=[proj, proj, proj, proj, gamma, hnorm])


def hgrn_bwd(proj, gamma, hnorm, ssave, dmix, du, plan=None):
    def body(q_ref, f_ref, i_ref, g_ref, gam_ref, ng_ref, ss_ref, do_ref, du_ref, dp_ref, dgam_ref, dng_ref, dst):
        @pl.when(pl.program_id(0) == 0)
        def _():
            dst[...] = jnp.zeros_like(dst)
            dgam_ref[...] = jnp.zeros_like(dgam_ref)
            dng_ref[...] = jnp.zeros_like(dng_ref)

        dp_ref[:, 0:512] = du_ref[...]
        for h in range(4):
            sl = slice(h * 128, (h + 1) * 128)
            _, vjp = jax.vjp(_hgrn_chunk, ss_ref[0, h], q_ref[:, sl], f_ref[:, sl], i_ref[:, sl], g_ref[:, sl],
                             gam_ref[:, sl], ng_ref[:, sl])
            ds, dq, df, di, dg, dgam, dng = vjp((dst[h], do_ref[:, sl]))
            dst[h] = ds
            for n, v in enumerate((dq, df, di, dg)):
                dp_ref[:, 512 * (n + 1) + h * 128: 512 * (n + 1) + (h + 1) * 128] = v.astype(dp_ref.dtype)
            dgam_ref[:, sl] += dgam
            dng_ref[:, sl] += dng

    def pj(n):
        return pl.BlockSpec((HGC, 512), lambda i: (NCH - 1 - i, n))

    return pcall(
        body, plan, grid=(NCH,),
        in_specs=[pj(1), pj(2), pj(3), pj(4), pl.BlockSpec((2, 512), lambda i: (0, 0)), pl.BlockSpec((1, 512), lambda i: (0, 0)),
                  pl.BlockSpec((1, 4, 128, 128), lambda i: (NCH - 1 - i, 0, 0, 0)), pj(1), pj(0)],
        out_specs=[pl.BlockSpec((HGC, 2560), lambda i: (NCH - 1 - i, 0)), pl.BlockSpec((2, 512), lambda i: (0, 0)),
                   pl.BlockSpec((1, 512), lambda i: (0, 0))],
        out_shape=[S((L, 2560), BF), S((2, 512), f32), S((1, 512), f32)],
        scratch_shapes=[pltpu.VMEM((4, 128, 128), f32)],
        sem=("arbitrary",), name="hgrn_bwd", args=[proj, proj, proj, proj, gamma, hnorm, ssave, dmix, du])


def _earlier(h_ref, k, r0, n):
    if r0 > 0:
        return h_ref[pl.ds(r0 - k, n), :]
    rid = lax.broadcasted_iota(jnp.int32, (8, h_ref.shape[1]), 0)
    head = jnp.where(rid >= k, pltpu.roll(h_ref[pl.ds(0, 8), :], k, 0), 0.0)
    return jnp.concatenate([head, h_ref[pl.ds(8 - k, n - 8), :]], axis=0)


def _conv3_rows(h_ref, w, b, r0, n=None):
    n = CR if n is None else n
    h1, h2 = _earlier(h_ref, 1, r0, n), _earlier(h_ref, 2, r0, n)
    return w[2:3] * h_ref[pl.ds(r0, n), :] + w[1:2] * h1 + w[0:1] * h2 + b, h1, h2


CT = 128
NCT = DFF // CT
CR = 64


def convact_fwd(hu, cw, cb, layer, plan=None):
    def body(ha_ref, hb_ref, wa_ref, wb_ref, ba_ref, bb_ref, o_ref):
        ca = _conv3_rows(ha_ref, wa_ref[...], ba_ref[...], 0, L)[0]
        cb_ = _conv3_rows(hb_ref, wb_ref[...], bb_ref[...], 0, L)[0]
        o_ref[...] = (jax.nn.silu(ca) * cb_).astype(o_ref.dtype)

    def h(off):
        return pl.BlockSpec((L, CT), lambda j: (0, j + off))

    def w(off):
        return pl.BlockSpec((3, CT), lambda j: (0, j + off))

    def b(off):
        return pl.BlockSpec((None, 1, CT), lambda j: (layer, 0, j + off))

    return pcall(body, plan, grid=(NCT,), in_specs=[h(0), h(NCT), w(0), w(NCT), b(0), b(NCT)],
                 out_specs=pl.BlockSpec((L, CT), lambda j: (0, j)), out_shape=S((L, DFF), BF),
                 sem=("parallel",), name=f"convact_fwd{layer}", args=[hu, hu, cw, cw, cb, cb])


def convact_bwd(hu, cw, cb, dact, layer, plan=None):
    def body(ha_ref, hb_ref, wa_ref, wb_ref, ba_ref, bb_ref, g_ref, dh_ref, dw_ref, db_ref, sh, sw, sb, da_scr, db_scr):
        j = pl.program_id(0)

        def fold(x):
            return functools.reduce(jnp.add, [x[8 * m:8 * m + 8] for m in range(CR // 8)])

        @pl.when(j < NCT)
        def _():
            wa, wb, ba, bb = wa_ref[...], wb_ref[...], ba_ref[...], bb_ref[...]
            da_scr[pl.ds(L, 8), :] = jnp.zeros((8, CT), f32)
            db_scr[pl.ds(L, 8), :] = jnp.zeros((8, CT), f32)
            acc = [jnp.zeros((8, CT), f32) for _ in range(8)]
            for c in range(L // CR):
                r0 = c * CR
                ca, a1, a2 = _conv3_rows(ha_ref, wa, ba, r0)
                cb_, b1, b2 = _conv3_rows(hb_ref, wb, bb, r0)
                g = g_ref[pl.ds(r0, CR), :].astype(f32)
                sg = jax.nn.sigmoid(ca)
                dca = g * cb_ * (sg * (1.0 + ca * (1.0 - sg)))
                dcb = g * (ca * sg)
                da_scr[pl.ds(r0, CR), :] = dca
                db_scr[pl.ds(r0, CR), :] = dcb
                terms = (dca * a2, dca * a1, dca * ha_ref[pl.ds(r0, CR), :], dca,
                         dcb * b2, dcb * b1, dcb * hb_ref[pl.ds(r0, CR), :], dcb)
                acc = [a + fold(t) for a, t in zip(acc, terms)]
            rows = [jnp.sum(a, axis=0, keepdims=True) for a in acc]
            for k in range(3):
                dw_ref[k:k + 1, :] = rows[k]
                sw[j, k:k + 1, :] = rows[4 + k]
            db_ref[...] = rows[3]
            sb[j] = rows[7]
            for c in range(L // CR):
                r0 = c * CR
                for scr, w, out in ((da_scr, wa, dh_ref), (db_scr, wb, sh.at[j])):
                    dh = (w[2:3] * scr[pl.ds(r0, CR), :] + w[1:2] * scr[pl.ds(r0 + 1, CR), :]
                          + w[0:1] * scr[pl.ds(r0 + 2, CR), :])
                    out[pl.ds(r0, CR), :] = dh.astype(out.dtype)

        @pl.when(j >= NCT)
        def _():
            dh_ref[...] = sh[j - NCT]
            dw_ref[...] = sw[j - NCT]
            db_ref[...] = sb[j - NCT]

    def lo(j):
        return jnp.minimum(j, NCT - 1)

    in_specs = [pl.BlockSpec((L, CT), lambda j: (0, lo(j))), pl.BlockSpec((L, CT), lambda j: (0, lo(j) + NCT)),
                pl.BlockSpec((3, CT), lambda j: (0, lo(j))), pl.BlockSpec((3, CT), lambda j: (0, lo(j) + NCT)),
                pl.BlockSpec((None, 1, CT), lambda j: (layer, 0, lo(j))), pl.BlockSpec((None, 1, CT), lambda j: (layer, 0, lo(j) + NCT)),
                pl.BlockSpec((L, CT), lambda j: (0, lo(j)))]
    return pcall(
        body, plan, grid=(2 * NCT,), in_specs=in_specs,
        out_specs=[pl.BlockSpec((L, CT), lambda j: (0, j)), pl.BlockSpec((3, CT), lambda j: (0, j)), pl.BlockSpec((1, CT), lambda j: (0, j))],
        out_shape=[S((L, 2 * DFF), BF), S((3, 2 * DFF), f32), S((1, 2 * DFF), f32)],
        scratch_shapes=[pltpu.VMEM((NCT, L, CT), BF), pltpu.VMEM((NCT, 3, CT), f32), pltpu.VMEM((NCT, 1, CT), f32),
                        pltpu.VMEM((L + 8, CT), f32), pltpu.VMEM((L + 8, CT), f32)],
        sem=("arbitrary",), name=f"convact_bwd{layer}", args=[hu, hu, cw, cw, cb, cb, dact])


DILS = (1, 4, 16)
AB = 128
NPAIR = 12


def _rope_tables(pos_ref, invf_ref):
    ang = pos_ref[...].astype(f32) * invf_ref[...]
    lane = lax.broadcasted_iota(jnp.int32, (1, 128), 1) % 64
    cosf = jnp.where(lane < 16, jnp.cos(ang), 1.0)
    sn = jnp.sin(ang)
    s_lo = jnp.where(lane < 8, -sn, 0.0)
    s_hi = jnp.where((lane >= 8) & (lane < 16), sn, 0.0)
    return cosf, s_lo, s_hi


def _rope(t, cosf, s_lo, s_hi):
    return t * cosf + pltpu.roll(t, 120, 1) * s_lo + pltpu.roll(t, 8, 1) * s_hi


def _rope_t(g, cosf, s_lo, s_hi):
    return g * cosf + pltpu.roll(g * s_lo, 8, 1) + pltpu.roll(g * s_hi, 120, 1)


def _att_block(q2, kp, kc, vp, vc, first):
    lane = lax.broadcasted_iota(jnp.int32, (1, 128), 1)
    qi = lax.broadcasted_iota(jnp.int32, (AB, 2 * AB), 0) + AB
    kj = lax.broadcasted_iota(jnp.int32, (AB, 2 * AB), 1)
    back = qi - kj
    valid = (back >= 0) & (back <= AB)
    if first:
        valid = valid & (kj >= AB)
    kk = jnp.concatenate([kp, kc], axis=0)
    vv = jnp.concatenate([vp, vc], axis=0)
    o2 = jnp.zeros((AB, 128), f32)
    lse2 = jnp.zeros((AB, 128), f32)
    for e in range(2):
        hm = ((lane >= 64 * e) & (lane < 64 * (e + 1))).astype(f32)
        s = dot_nt(q2 * (hm * 0.125), kk)
        s = jnp.where(valid, s, -jnp.inf)
        m = jnp.max(s, axis=-1, keepdims=True)
        p = jnp.exp(s - m)
        den = jnp.sum(p, axis=-1, keepdims=True)
        o2 = o2 + dot_nn(p, vv * hm) / den
        lse2 = lse2 + (m + jnp.log(den)) * hm
    return o2, lse2


def _att_blocks(dil):
    m = L // dil
    return [(r * m + n * AB, n == 0) for r in range(dil) for n in range(m // AB)]


def deinterleave(x, dil):
    return x if dil == 1 else x.reshape(L // dil, dil, x.shape[1]).swapaxes(0, 1).reshape(L, x.shape[1])


def attn_fwd(qkv, pos, invf, g, plan=None):
    blocks = _att_blocks(DILS[g])

    def body(q_ref, k_ref, v_ref, pos_ref, invf_ref, o_ref, l_ref, qr, kr):
        cosf, s_lo, s_hi = _rope_tables(pos_ref, invf_ref)
        qr[...] = _rope(q_ref[...], cosf, s_lo, s_hi)
        kr[...] = _rope(k_ref[...], cosf, s_lo, s_hi)
        for off, first in blocks:
            cur, prv = pl.ds(off, AB), pl.ds(off if first else off - AB, AB)
            o2, lse2 = _att_block(qr[cur, :], kr[prv, :], kr[cur, :], v_ref[prv, :], v_ref[cur, :], first)
            o_ref[cur, :] = o2
            l_ref[cur, :] = lse2

    def sec(n):
        return pl.BlockSpec((L, 128), lambda p: (0, p + 4 * n))

    return pcall(
        body, plan, grid=(4,),
        in_specs=[sec(0), sec(1), sec(2), pl.BlockSpec((L, 1), lambda p: (0, 0)), pl.BlockSpec((1, 128), lambda p: (0, 0))],
        out_specs=[sec(0), sec(0)], out_shape=[S((L, 512), f32), S((L, 512), f32)],
        scratch_shapes=[pltpu.VMEM((L, 128), f32), pltpu.VMEM((L, 128), f32)],
        sem=("parallel",), name=f"attn_fwd{g}", args=[qkv, qkv, qkv, pos, invf])


def _att_block_bwd(q2, kp, kc, vp, vc, lse2, do2, dl2, first):
    lane = lax.broadcasted_iota(jnp.int32, (1, 128), 1)
    qi = lax.broadcasted_iota(jnp.int32, (AB, 2 * AB), 0) + AB
    kj = lax.broadcasted_iota(jnp.int32, (AB, 2 * AB), 1)
    back = qi - kj
    valid = (back >= 0) & (back <= AB)
    if first:
        valid = valid & (kj >= AB)
    kk = jnp.concatenate([kp, kc], axis=0)
    vv = jnp.concatenate([vp, vc], axis=0)
    dq2 = jnp.zeros((AB, 128), f32)
    dkk = jnp.zeros((2 * AB, 128), f32)
    dvv = jnp.zeros((2 * AB, 128), f32)
    for e in range(2):
        hb = (lane >= 64 * e) & (lane < 64 * (e + 1))
        hm = hb.astype(f32)
        qs = q2 * (hm * 0.125)
        lse = jnp.max(jnp.where(hb, lse2, -jnp.inf), axis=-1, keepdims=True)
        dls = jnp.sum(dl2 * hm, axis=-1, keepdims=True)
        p = jnp.where(valid, jnp.exp(dot_nt(qs, kk) - lse), 0.0)
        dov = do2 * hm
        dp = dot_nt(dov, vv)
        ds = p * (dp - jnp.sum(p * dp, axis=-1, keepdims=True) + dls)
        dq2 = dq2 + dot_nn(ds, kk) * (hm * 0.125)
        dkk = dkk + dot_tn(ds, qs)
        dvv = dvv + dot_tn(p, dov)
    return dq2, dkk[:AB], dkk[AB:], dvv[:AB], dvv[AB:]


def attn_bwd(qkv, pos, invf, lse, do, dl, g, plan=None):
    blocks = _att_blocks(DILS[g])

    def body(q_ref, k_ref, v_ref, pos_ref, invf_ref, l_ref, do_ref, dl_ref, d_ref, qr, kr, dqr, dkr, dvr):
        cosf, s_lo, s_hi = _rope_tables(pos_ref, invf_ref)
        qr[...] = _rope(q_ref[...], cosf, s_lo, s_hi)
        kr[...] = _rope(k_ref[...], cosf, s_lo, s_hi)
        for off, first in blocks:
            cur, prv = pl.ds(off, AB), pl.ds(off if first else off - AB, AB)
            dq2, dkp, dkc, dvp, dvc = _att_block_bwd(qr[cur, :], kr[prv, :], kr[cur, :], v_ref[prv, :], v_ref[cur, :],
                                                     l_ref[cur, :], do_ref[cur, :], dl_ref[cur, :], first)
            dqr[cur, :] = dq2
            dkr[cur, :] = dkc
            dvr[cur, :] = dvc
            if not first:
                dkr[prv, :] += dkp
                dvr[prv, :] += dvp
        d_ref[0] = _rope_t(dqr[...], cosf, s_lo, s_hi).astype(d_ref.dtype)
        d_ref[1] = _rope_t(dkr[...], cosf, s_lo, s_hi).astype(d_ref.dtype)
        d_ref[2] = dvr[...].astype(d_ref.dtype)

    def sec(n):
        return pl.BlockSpec((L, 128), lambda p: (0, p + 4 * n))

    return pcall(
        body, plan, grid=(4,),
        in_specs=[sec(0), sec(1), sec(2), pl.BlockSpec((L, 1), lambda p: (0, 0)), pl.BlockSpec((1, 128), lambda p: (0, 0)),
                  sec(0), sec(0), sec(0)],
        out_specs=pl.BlockSpec((3, L, 128), lambda p: (0, 0, p)), out_shape=S((3, L, 512), BF),
        scratch_shapes=[pltpu.VMEM((L, 128), f32)] * 5,
        sem=("parallel",), name=f"attn_bwd{g}", args=[qkv, qkv, qkv, pos, invf, lse, do, dl])


def _merge(o0, o1, o2, l0, l1, l2):
    m = jnp.maximum(jnp.maximum(l0, l1), l2)
    e0, e1, e2 = jnp.exp(l0 - m), jnp.exp(l1 - m), jnp.exp(l2 - m)
    return (e0 * o0 + e1 * o1 + e2 * o2) / (e0 + e1 + e2)


def _to_token_major(src_ref, scr, i, dil, slab):
    n = TR // dil
    for r in range(dil):
        rows = pl.ds(pl.multiple_of(r * (L // dil) + i * n, n), n)
        scr[pl.ds(r, n, stride=dil), :] = src_ref[rows, slab * 128:(slab + 1) * 128].astype(f32)
    return scr[...]


def _to_class_major(val, dst_ref, scr, i, dil, slab):
    n = TR // dil
    scr[...] = val
    for r in range(dil):
        rows = pl.ds(pl.multiple_of(r * (L // dil) + i * n, n), n)
        dst_ref[rows, slab * 128:(slab + 1) * 128] = scr[pl.ds(r, n, stride=dil), :].astype(dst_ref.dtype)


def rms_fwd_classes(x, g, name):
    def body(x_ref, g_ref, o_ref, o1_ref, o2_ref, scr):
        i = pl.program_id(0)
        y = _rms(x_ref[...], g_ref[...])
        o_ref[...] = y.astype(o_ref.dtype)
        for s in range(D // 128):
            ys = y[:, s * 128:(s + 1) * 128]
            _to_class_major(ys, o1_ref, scr, i, DILS[1], s)
            _to_class_major(ys, o2_ref, scr, i, DILS[2], s)

    row = pl.BlockSpec((TR, D), lambda i: (i, 0))
    full = pl.BlockSpec((L, D), lambda i: (0, 0))
    return pl.pallas_call(
        body, grid=(L // TR,), in_specs=[row, pl.BlockSpec((1, D), lambda i: (0, 0))], out_specs=[row, full, full],
        out_shape=[S((L, D), BF)] * 3, scratch_shapes=[pltpu.VMEM((TR, 128), f32)],
        compiler_params=_cp(("arbitrary",)), name=name)(x, g)


def rms_bwd_classes(x, g, dy0, dyc, dres, name, plan=None):
    def body(x_ref, g_ref, dy0_ref, d1_ref, d2_ref, dr_ref, dh_ref, dg_ref, scr, dyf):
        i = pl.program_id(0)
        for s in range(D // 128):
            sl = slice(s * 128, (s + 1) * 128)
            dyf[:, sl] = (dy0_ref[:, sl] + _to_token_major(d1_ref, scr.at[0], i, DILS[1], s)
                          + _to_token_major(d2_ref, scr.at[1], i, DILS[2], s))
        _, vjp = jax.vjp(_rms, x_ref[...], g_ref[...])
        dx, dg = vjp(dyf[...])
        dh_ref[...] = dr_ref[...] + dx

        @pl.when(i == 0)
        def _():
            dg_ref[...] = jnp.zeros_like(dg_ref)

        dg_ref[...] += dg

    row = pl.BlockSpec((TR, D), lambda i: (i, 0))
    vec = pl.BlockSpec((1, D), lambda i: (0, 0))
    full = pl.BlockSpec((L, D), lambda i: (0, 0))
    return pcall(body, plan, grid=(L // TR,), in_specs=[row, vec, row, full, full, row], out_specs=[row, vec],
                 out_shape=[S((L, D), f32), S((1, D), f32)],
                 scratch_shapes=[pltpu.VMEM((2, TR, 128), f32), pltpu.VMEM((TR, D), f32)],
                 sem=("arbitrary",), name=name, args=[x, g, dy0, dyc[0], dyc[1], dres])


def attn_merge_fwd(o0, l0, oc, lc, plan=None):
    def body(o0_ref, l0_ref, o1_ref, l1_ref, o2_ref, l2_ref, o_ref, scr):
        i = pl.program_id(0)
        for s in range(4):
            sl = slice(s * 128, (s + 1) * 128)
            o1 = _to_token_major(o1_ref, scr.at[0], i, DILS[1], s)
            l1 = _to_token_major(l1_ref, scr.at[1], i, DILS[1], s)
            o2 = _to_token_major(o2_ref, scr.at[2], i, DILS[2], s)
            l2 = _to_token_major(l2_ref, scr.at[3], i, DILS[2], s)
            o_ref[:, sl] = _merge(o0_ref[:, sl], o1, o2, l0_ref[:, sl], l1, l2).astype(o_ref.dtype)

    blk = pl.BlockSpec((TR, 512), lambda i: (i, 0))
    full = pl.BlockSpec((L, 512), lambda i: (0, 0))
    return pcall(body, plan, grid=(L // TR,), in_specs=[blk, blk, full, full, full, full], out_specs=blk,
                 out_shape=S((L, 512), BF), scratch_shapes=[pltpu.VMEM((4, TR, 128), f32)],
                 sem=("arbitrary",), name="attn_merge_fwd", args=[o0, l0, oc[0], lc[0], oc[1], lc[1]])


def attn_merge_bwd(o0, l0, oc, lc, do, plan=None):
    def body(o0_ref, l0_ref, o1_ref, l1_ref, o2_ref, l2_ref, g_ref, do0, dl0, do1, dl1, do2, dl2, scr):
        i = pl.program_id(0)
        for s in range(4):
            sl = slice(s * 128, (s + 1) * 128)
            o1 = _to_token_major(o1_ref, scr.at[0], i, DILS[1], s)
            l1 = _to_token_major(l1_ref, scr.at[1], i, DILS[1], s)
            o2 = _to_token_major(o2_ref, scr.at[2], i, DILS[2], s)
            l2 = _to_token_major(l2_ref, scr.at[3], i, DILS[2], s)
            _, vjp = jax.vjp(_merge, o0_ref[:, sl], o1, o2, l0_ref[:, sl], l1, l2)
            g0, g1, g2, h0, h1, h2 = vjp(g_ref[:, sl].astype(f32))
            do0[:, sl] = g0.astype(do0.dtype)
            dl0[:, sl] = h0
            _to_class_major(g1, do1, scr.at[0], i, DILS[1], s)
            _to_class_major(h1, dl1, scr.at[1], i, DILS[1], s)
            _to_class_major(g2, do2, scr.at[2], i, DILS[2], s)
            _to_class_major(h2, dl2, scr.at[3], i, DILS[2], s)

    blk = pl.BlockSpec((TR, 512), lambda i: (i, 0))
    full = pl.BlockSpec((L, 512), lambda i: (0, 0))
    outs = pcall(body, plan, grid=(L // TR,), in_specs=[blk, blk, full, full, full, full, blk],
                 out_specs=[blk, blk, full, full, full, full],
                 out_shape=[S((L, 512), BF), S((L, 512), f32)] * 3, scratch_shapes=[pltpu.VMEM((4, TR, 128), f32)],
                 sem=("arbitrary",), name="attn_merge_bwd", args=[o0, l0, oc[0], lc[0], oc[1], lc[1], do])
    return [outs[0], outs[2], outs[4]], [outs[1], outs[3], outs[5]]


def _invf_lanes():
    half = 8
    inv = ROPE_THETA ** (-np.arange(half, dtype=np.float32) * 2.0 / 16.0)
    lane = np.arange(128) % 64
    return jnp.asarray(np.where(lane < 16, inv[lane % 8], 0.0).astype(np.float32)[None, :])


def hosted(C, host, fn):
    p = C.plan(host) if C is not None else None
    out = fn(p)
    if p is not None:
        C.done(p)
    return out


def _ffn_fwd(h, g_row, W, cb, layer, C):
    hn = rms_fwd(h, g_row, f"rms_ffn{layer}")
    hu = hosted(C, f"ffn_in{layer}", lambda p: matmul(hn, W[("ffn_w_in", layer)], mode="nn", tm=1024, tn=1408, tk=1024,
                                                      plan=p, name=f"ffn_in{layer}"))
    act = hosted(C, f"convact_fwd{layer}", lambda p: convact_fwd(hu, W[("ffn_conv_w", layer)], cb, layer, plan=p))
    h2 = hosted(C, f"ffn_out{layer}", lambda p: matmul(act, W[("ffn_w_out", layer)], mode="nn", tm=1024, tn=1024, tk=2816,
                                                       add=h, plan=p, name=f"ffn_out{layer}"))
    return h2, (hn, hu, act)


def _ffn_bwd(dh, h, g_row, W, cb, saved, layer, C, G):
    hn, hu, act = saved
    w_in, w_out = W[("ffn_w_in", layer)], W[("ffn_w_out", layer)]
    dact = hosted(C, f"ffn_out_dx{layer}", lambda p: matmul(dh, w_out, mode="nt", tm=1024, tn=1408, tk=1024, plan=p,
                                                          name=f"ffn_out_dx{layer}"))
    G[("ffn_w_out", layer)] = hosted(C, f"ffn_out_dw{layer}", lambda p: matmul(
        act, dh, mode="tn", tm=1408, tn=1024, tk=L, out_dtype=BF, plan=p, name=f"ffn_out_dw{layer}"))
    dhu, G[("ffn_conv_w", layer)], g_cb = hosted(
        C, f"convact_bwd{layer}", lambda p: convact_bwd(hu, W[("ffn_conv_w", layer)], cb, dact, layer, plan=p))
    dhn = hosted(C, f"ffn_in_dx{layer}", lambda p: matmul(dhu, w_in, mode="nt", tm=1024, tn=1024, tk=2816, plan=p,
                                                         name=f"ffn_in_dx{layer}"))
    G[("ffn_w_in", layer)] = hosted(C, f"ffn_in_dw{layer}", lambda p: matmul(
        hn, dhu, mode="tn", tm=1024, tn=1408, tk=L, out_dtype=BF, plan=p, name=f"ffn_in_dw{layer}"))
    dh2, g_norm = hosted(C, f"rms_ffn_bwd{layer}", lambda p: rms_bwd(h, g_row, [dhn], dh, f"rms_ffn_bwd{layer}", plan=p))
    return dh2, g_cb, g_norm


def local_step(x, pos, tgt, sm, W, C=None):
    G = C.grads if C is not None else {}
    nm, nf = sm["norm_mix"], sm["norm_ffn"]
    invf = _invf_lanes()
    are = sm["s5_A_re"].reshape(NST, 1)
    aim = sm["s5_A_im"].reshape(NST, 1)
    ldt = sm["s5_log_dt"].reshape(1, 32)
    bre = sm["s5_B_re"].reshape(NST, 16)
    bim = sm["s5_B_im"].reshape(NST, 16)
    cre = jnp.swapaxes(sm["s5_C_re"][0], 1, 2).reshape(NST, 16)
    cim = jnp.swapaxes(sm["s5_C_im"][0], 1, 2).reshape(NST, 16)
    drow = sm["s5_D"].reshape(1, S5W)
    wbr, wbi, wcr, wci, abr, abi = s5_params_fwd(are, aim, ldt, bre, bim, cre, cim)
    hn0 = rms_fwd(x, nm[0:1], "rms_mix0")
    cb3 = sm["ffn_conv_b3"]
    proj = hosted(C, "mix_in", lambda p: matmul(hn0, W[("mix_w_in", 0)], mode="nn", tm=1024, tn=1280, tk=1024, plan=p, name="mix_in"))
    xs_re, xs_im, y5 = hosted(C, "s5_scan_fwd", lambda p: s5_scan_fwd(proj, wbr, wbi, wcr, wci, abr, abi, drow, plan=p))
    oa = s5_glu_fwd(y5, W[("s5_glu_w", 0)], sm["s5_glu_b"])
    ob, ssave = hosted(C, "hgrn_fwd", lambda p: hgrn_fwd(proj, sm["hgrn_gamma"], sm["hgrn_norm"], plan=p))
    cat = jnp.concatenate([oa, ob], axis=1)
    h1 = matmul(cat, W[("mix_w_out", 0)], mode="nn", tm=1024, tn=1024, tk=1024, add=x, name="mix_out")
    h2, ffn0 = _ffn_fwd(h1, nf[0:1], W, cb3, 0, C)
    hn2_g = rms_fwd_classes(h2, nm[1:2], "rms_mix1")
    wqkv = W[("att_w_qkv", 0)]
    pos_g, qkv_g, oc_g, lc_g = [], [], [], []
    for g, dil in enumerate(DILS):
        pos_g.append(deinterleave(pos, dil))
        qkv_g.append(hosted(C, f"att_qkv{g}", lambda p: matmul(
            hn2_g[g], wqkv, mode="nn", tm=1024, tn=512, tk=1024, dims=(L, 1536, D),
            b_spec=pl.BlockSpec((D, 512), lambda i, j, k, g=g: (0, 3 * j + g)), plan=p, name=f"att_qkv{g}")))
        o_c, l_c = hosted(C, f"attn_fwd{g}", lambda p: attn_fwd(qkv_g[g], pos_g[g], invf, g, plan=p))
        oc_g.append(o_c)
        lc_g.append(l_c)
    o = hosted(C, "attn_merge_fwd", lambda p: attn_merge_fwd(oc_g[0], lc_g[0], oc_g[1:], lc_g[1:], plan=p))
    h3 = matmul(o, W[("att_w_o", 0)], mode="nn", tm=1024, tn=1024, tk=512, add=h2, name="att_o")
    h4, ffn1 = _ffn_fwd(h3, nf[1:2], W, cb3, 1, C)
    loss, dh, g_nfinal = loss_head(h4, sm["norm_final"].reshape(1, D), tgt)
    dh, g_cb1, g_nf1 = _ffn_bwd(dh, h3, nf[1:2], W, cb3, ffn1, 1, C, G)
    do = matmul(dh, W[("att_w_o", 0)], mode="nt", tm=1024, tn=512, tk=1024, name="att_o_dx")
    G[("att_w_o", 0)] = matmul(o, dh, mode="tn", tm=512, tn=1024, tk=L, out_dtype=BF, name="att_o_dw")
    do_g, dl_g = hosted(C, "attn_merge_bwd", lambda p: attn_merge_bwd(oc_g[0], lc_g[0], oc_g[1:], lc_g[1:], do, plan=p))
    dhn2_g, gq = [], []
    for g, dil in enumerate(DILS):
        d3 = hosted(C, f"attn_bwd{g}", lambda p: attn_bwd(qkv_g[g], pos_g[g], invf, lc_g[g], do_g[g], dl_g[g], g, plan=p))
        dx = matmul(d3, wqkv, mode="nt", tm=1024, tn=1024, tk=512, dims=(L, D, 1536),
                    a_spec=pl.BlockSpec((None, 1024, 512), lambda i, j, k: (k, i, 0)),
                    b_spec=pl.BlockSpec((D, 512), lambda i, j, k, g=g: (0, 3 * k + g)), name=f"att_qkv_dx{g}")
        dhn2_g.append(dx)
        gq.append(matmul(hn2_g[g], d3, mode="tn", tm=1024, tn=512, tk=L, out_dtype=BF, dims=(D, 1536, L),
                         b_spec=pl.BlockSpec((None, L, 512), lambda i, j, k: (j, k, 0)), name=f"att_qkv_dw{g}"))
    G[("att_w_qkv", 0)] = jnp.concatenate([gq[g][:, 512 * s:512 * (s + 1)] for s in range(3) for g in range(3)], axis=1)
    dh, g_nm1 = hosted(C, "rms_mix_bwd1", lambda p: rms_bwd_classes(h2, nm[1:2], dhn2_g[0], dhn2_g[1:], dh, "rms_mix_bwd1", plan=p))
    dh, g_cb0, g_nf0 = _ffn_bwd(dh, h1, nf[0:1], W, cb3, ffn0, 0, C, G)
    dmix = matmul(dh, W[("mix_w_out", 0)], mode="nt", tm=1024, tn=1024, tk=1024, name="mix_out_dx")
    G[("mix_w_out", 0)] = matmul(cat, dh, mode="tn", tm=1024, tn=1024, tk=L, out_dtype=BF, name="mix_out_dw")
    dy5, g_glu_w, g_glu_b = s5_glu_bwd(y5, W[("s5_glu_w", 0)], sm["s5_glu_b"], dmix)
    G[("s5_glu_w", 0)] = g_glu_w.astype(BF)
    du, gwbr, gwbi, gwcr, gwci, gabr, gabi, g_d = hosted(C, "s5_scan_bwd", lambda p: s5_scan_bwd(
        dy5, proj, xs_re, xs_im, wbr, wbi, wcr, wci, abr, abi, drow, plan=p))
    g_are, g_aim, g_ldt, g_bre, g_bim, g_cre, g_cim = s5_params_bwd(are, aim, ldt, bre, bim, cre, cim,
                                                                   (gwbr, gwbi, gwcr, gwci, gabr, gabi))
    small = {
        "norm_ffn": jnp.concatenate([g_nf0, g_nf1], axis=0), "norm_final": g_nfinal.reshape(D),
        "s5_A_re": g_are.reshape(1, 32, 64), "s5_A_im": g_aim.reshape(1, 32, 64), "s5_log_dt": g_ldt.reshape(1, 32),
        "s5_B_re": g_bre.reshape(1, 32, 64, 16), "s5_B_im": g_bim.reshape(1, 32, 64, 16),
        "s5_C_re": jnp.swapaxes(g_cre.reshape(1, 32, 64, 16), 2, 3), "s5_C_im": jnp.swapaxes(g_cim.reshape(1, 32, 64, 16), 2, 3),
        "s5_D": g_d.reshape(1, 32, 16), "s5_glu_b": g_glu_b, "ffn_conv_b": jnp.concatenate([g_cb0, g_cb1], axis=0),
    }
    if C is not None:
        C.small["small_early"] = _pack(small, SMALL_EARLY)
    dproj, g_gamma, g_hnorm = hosted(C, "hgrn_bwd", lambda p: hgrn_bwd(proj, sm["hgrn_gamma"], sm["hgrn_norm"], ssave, dmix, du,
                                                                       plan=p))
    dhn0 = hosted(C, "mix_in_dx", lambda p: matmul(dproj, W[("mix_w_in", 0)], mode="nt", tm=1024, tn=1024, tk=2560, plan=p,
                                                  name="mix_in_dx"))
    G[("mix_w_in", 0)] = matmul(hn0, dproj, mode="tn", tm=1024, tn=1280, tk=L, out_dtype=BF, name="mix_in_dw")
    gx, g_nm0 = hosted(C, "rms_mix_bwd0", lambda p: rms_bwd(x, nm[0:1], [dhn0], dh, "rms_mix_bwd0", plan=p))
    small.update({"norm_mix": jnp.concatenate([g_nm0, g_nm1], axis=0), "hgrn_gamma": g_gamma, "hgrn_norm": g_hnorm})
    if C is not None:
        C.small["small_late"] = _pack(small, SMALL_LATE)
    return loss, gx, G, small


BIG = ("mix_w_in", "mix_w_out", "s5_glu_w", "att_w_qkv", "att_w_o", "ffn_w_in", "ffn_w_out", "ffn_conv_w")
SMALL = ("norm_mix", "norm_ffn", "norm_final", "s5_A_re", "s5_A_im", "s5_log_dt", "s5_B_re", "s5_B_im", "s5_C_re", "s5_C_im",
         "s5_D", "s5_glu_b", "hgrn_gamma", "hgrn_norm", "ffn_conv_b")
SMALL_LATE = ("norm_mix", "hgrn_gamma", "hgrn_norm")
SMALL_EARLY = tuple(n for n in SMALL if n not in SMALL_LATE)


def cast_bf16(w, name, plan=None):
    nl, r, c = w.shape
    w2 = w.reshape(nl * r, c)
    tr = 256 if (nl * r) % 256 == 0 else nl * r

    def body(w_ref, o_ref):
        o_ref[...] = w_ref[...].astype(BF)

    out = pcall(body, plan, grid=(nl * r // tr,), in_specs=[pl.BlockSpec((tr, c), lambda i: (i, 0))],
                out_specs=pl.BlockSpec((tr, c), lambda i: (i, 0)), out_shape=S((nl * r, c), BF),
                sem=("parallel",), name=name, args=[w2])
    return out.reshape(nl, r, c)


SCHEDULE = {
    "cast_ffn_w_in": [("G", "mix_w_in", 0)],
    "mix_in": [("G", "mix_w_out", 0), ("G", "s5_glu_w", 0)],
    "s5_scan_fwd": [("G", "ffn_w_in", 0, (0, 2))],
    "hgrn_fwd": [("G", "ffn_w_in", 0, (1, 2)), ("G", "ffn_conv_w", 0), ("G", "ffn_conv_w", 1), ("G", "att_w_qkv", 0, (0, 2))],
    "ffn_in0": [("G", "ffn_w_out", 0)],
    "convact_fwd0": [("G", "att_w_qkv", 0, (1, 2))],
    "att_qkv0": [("G", "att_w_o", 0)],
    "attn_fwd0": [("G", "ffn_w_in", 1, (0, 2))],
    "attn_fwd1": [("G", "ffn_w_in", 1, (1, 2))],
    "attn_fwd2": [("G", "ffn_w_out", 1)],
    "convact_bwd1": [("P", "ffn_w_out", 1)],
    "ffn_in_dx1": [("A", "ffn_w_out", 1, (0, 2))],
    "ffn_in_dw1": [("A", "ffn_w_out", 1, (1, 2))],
    "rms_ffn_bwd1": [("P", "ffn_w_in", 1)],
    "attn_merge_bwd": [("P", "att_w_o", 0), ("A", "ffn_conv_w", 1), ("B", "ffn_w_out", 1)],
    "attn_bwd0": [("A", "ffn_w_in", 1, (0, 2)), ("A", "att_w_o", 0)],
    "attn_bwd1": [("A", "ffn_w_in", 1, (1, 2)), ("B", "att_w_o", 0), ("B", "ffn_conv_w", 1)],
    "attn_bwd2": [("B", "ffn_w_in", 1)],
    "rms_mix_bwd1": [("P", "att_w_qkv", 0)],
    "ffn_out_dx0": [("A", "att_w_qkv", 0, (0, 4))],
    "ffn_out_dw0": [("A", "att_w_qkv", 0, (1, 4))],
    "convact_bwd0": [("A", "att_w_qkv", 0, (2, 4)), ("A", "att_w_qkv", 0, (3, 4)), ("P", "ffn_w_out", 0)],
    "ffn_in_dx0": [("A", "ffn_w_out", 0, (0, 2)), ("B", "att_w_qkv", 0)],
    "ffn_in_dw0": [("A", "ffn_w_out", 0, (1, 2))],
    "rms_ffn_bwd0": [("P", "ffn_w_in", 0), ("B", "ffn_w_out", 0)],
    "s5_scan_bwd": [("A", "ffn_w_in", 0, (0, 2)), ("P", "mix_w_out", 0), ("P", "s5_glu_w", 0), ("A", "ffn_conv_w", 0)],
    "hgrn_bwd": [("A", "ffn_w_in", 0, (1, 2)), ("A", "mix_w_out", 0), ("A", "s5_glu_w", 0), ("B", "ffn_conv_w", 0),
                 ("A", "small_early", 0)],
    "mix_in_dx": [("B", "ffn_w_in", 0), ("B", "mix_w_out", 0), ("B", "s5_glu_w", 0), ("B", "small_early", 0)],
    "rms_mix_bwd0": [("P", "mix_w_in", 0)],
    "adam_att_w_o": [("A", "mix_w_in", 0), ("A", "small_late", 0)],
    "adam_s5_glu_w": [("B", "mix_w_in", 0), ("B", "small_late", 0)],
}


class Comm:
    def __init__(self, shards, shapes):
        self.shards, self.shapes = shards, shapes
        self.W, self.grads, self.slots = {}, {}, {}
        self.sib, self.pair = {}, {}
        self.small = {}

    def plan(self, host):
        items = SCHEDULE.get(host)
        if not items:
            return None
        p = Plan()
        for it in items:
            kind, name, l = it[:3]
            part, parts = it[3] if len(it) > 3 else (0, 1)
            if name.startswith("small"):
                sg = self.small[name]
                kdst = p.buf("slots:" + name, arr=self.slots.get(name), shape=S((8,) + sg.shape, f32), write=True)
                if kind == "A":
                    ReduceOp(p, p.buf("g:" + name, arr=sg), kdst, None, sg.shape, False, 0, 0, whole=True)
                else:
                    ForwardOp(p, kdst, None, whole=True)
                continue
            nl, R, C_ = self.shapes[name]
            rows = name in ROW_SHARDED
            r0, nr = part * (R // parts), R // parts
            if kind == "G":
                sh = self.shards[name]
                kdst = p.buf(f"W:{name}:{l}", arr=self.W.get((name, l)), shape=S((4 * R, C_) if rows else (R, 4 * C_), sh.dtype),
                             write=True)
                GatherOp(p, p.buf("shard:" + name, arr=sh), kdst, l, self.shapes[name], rows, r0, nr, split=(nr % 32 == 0))
            elif name == "ffn_conv_w":
                g = self.grads[(name, l)]
                kdst = p.buf("slots:" + name, arr=self.slots.get(name), shape=S((8, nl, R, C_), g.dtype), write=True)
                if kind == "A":
                    ReduceOp(p, p.buf(f"g:{name}:{l}", arr=g), kdst, l, self.shapes[name], rows, r0, nr)
                else:
                    ForwardOp(p, kdst, l)
            elif kind == "P":
                g = self.grads[(name, l)]
                ksib = p.buf(f"sib:{name}:{l}", shape=S((4 * R // 2, C_) if rows else (R // 2, 4 * C_), g.dtype), write=True)
                PairOp(p, p.buf(f"g:{name}:{l}", arr=g), ksib, self.shapes[name], rows)
            else:
                if (name, l) not in self.pair:
                    self.pair[(name, l)] = pair_sum(self.grads[(name, l)], self.sib[(name, l)], rows, R, f"pair_sum_{name}{l}")
                h = self.pair[(name, l)]
                kdst = p.buf("slots:" + name, arr=self.slots.get(name), shape=S((4, nl, R, C_), h.dtype), write=True)
                if kind == "A":
                    ReduceOp(p, p.buf(f"h:{name}:{l}", arr=h), kdst, l, self.shapes[name], rows, r0 // 2, nr // 2, half=True)
                else:
                    HalfForwardOp(p, kdst, l, self.shapes[name])
        return p

    def done(self, p):
        for k, arr in p.out.items():
            tag, name = k.split(":")[:2]
            if tag == "W":
                self.W[(name, int(k.split(":")[2]))] = arr
            elif tag == "sib":
                self.sib[(name, int(k.split(":")[2]))] = arr
            else:
                self.slots[name] = arr


def _adamw(w, g, m, v):
    m = B1 * m + (1.0 - B1) * g
    v = B2 * v + (1.0 - B2) * jnp.square(g)
    m_hat = m / (1.0 - B1 ** STEP)
    v_hat = v / (1.0 - B2 ** STEP)
    return -LR * (m_hat / (jnp.sqrt(v_hat) + AEPS) + WD * w), m, v


def adam_big(w, m, v, slots, name, plan=None):
    nl, R, C = w.shape
    ns = slots.shape[0]
    tr = 128 if R % 128 == 0 else (64 if R % 64 == 0 else R)

    def body(w_ref, m_ref, v_ref, s_ref, g_ref, d_ref, nm_ref, nv_ref):
        g = s_ref[0].astype(f32)
        for s in range(1, ns):
            g = g + s_ref[s].astype(f32)
        d, nm_, nv_ = _adamw(w_ref[...], g, m_ref[...], v_ref[...])
        g_ref[...] = g
        d_ref[...] = d
        nm_ref[...] = nm_
        nv_ref[...] = nv_

    blk = pl.BlockSpec((None, tr, C), lambda l, i: (l, i, 0))
    return pcall(body, plan, grid=(nl, R // tr),
                 in_specs=[blk, blk, blk, pl.BlockSpec((ns, None, tr, C), lambda l, i: (0, l, i, 0))],
                 out_specs=[blk] * 4, out_shape=[S((nl, R, C), f32)] * 4,
                 sem=("parallel", "parallel"), name=name, args=[w, m, v, slots])


def sum_slots(slots, name):
    R = slots.shape[1]

    def body(s_ref, g_ref):
        g = s_ref[0]
        for s in range(1, 8):
            g = g + s_ref[s]
        g_ref[...] = g

    return pl.pallas_call(
        body, grid=(R // 256,), in_specs=[pl.BlockSpec((8, 256, 128), lambda i: (0, i, 0))],
        out_specs=pl.BlockSpec((256, 128), lambda i: (i, 0)), out_shape=S((R, 128), f32),
        compiler_params=_cp(("parallel",)), name=name)(slots)


SMALL2D = {"norm_mix": (2, 1024), "norm_ffn": (2, 1024), "norm_final": (1, 1024), "s5_A_re": (32, 64), "s5_A_im": (32, 64),
           "s5_log_dt": (1, 32), "s5_B_re": (2048, 16), "s5_B_im": (2048, 16), "s5_C_re": (512, 64), "s5_C_im": (512, 64),
           "s5_D": (32, 16), "s5_glu_b": (1, 512), "hgrn_gamma": (2, 512), "hgrn_norm": (1, 512), "ffn_conv_b": (2, 5632)}


def adam_small(w, m, v, g, names, name):
    n = len(names)

    def body(*refs):
        for i in range(n):
            w_ref, m_ref, v_ref, g_ref = refs[4 * i:4 * i + 4]
            d_ref, nm_ref, nv_ref = refs[4 * n + 3 * i:4 * n + 3 * i + 3]
            d, nm_, nv_ = _adamw(w_ref[...], g_ref[...], m_ref[...], v_ref[...])
            d_ref[...] = d
            nm_ref[...] = nm_
            nv_ref[...] = nv_

    args = [t[k] for k in names for t in (w, m, v, g)]
    outs = pl.pallas_call(body, out_shape=[S(SMALL2D[k], f32) for k in names for _ in range(3)],
                          compiler_params=_cp(), name=name)(*args)
    return {k: tuple(outs[3 * i:3 * i + 3]) for i, k in enumerate(names)}


def _pack(d, names):
    flat = jnp.concatenate([d[n].reshape(-1) for n in names])
    n = flat.shape[0]
    rows = -(-n // (256 * 128)) * 256
    return jnp.pad(flat, (0, rows * 128 - n)).reshape(rows, 128)


def _unpack(p, like, names):
    flat = p.reshape(-1)
    out, off = {}, 0
    for n in names:
        sz = math.prod(like[n].shape)
        out[n] = flat[off:off + sz].reshape(like[n].shape)
        off += sz
    return out


def kernel(x, positions, norm_mix, norm_ffn, norm_final, mix_w_in, mix_w_out, s5_A_re, s5_A_im, s5_log_dt, s5_B_re, s5_B_im, s5_C_re, s5_C_im, s5_D, s5_glu_w, s5_glu_b, hgrn_gamma, hgrn_norm, att_w_qkv, att_w_o, ffn_w_in, ffn_conv_w, ffn_conv_b, ffn_w_out, loss_target, m_norm_mix, m_norm_ffn, m_norm_final, m_mix_w_in, m_mix_w_out, m_s5_A_re, m_s5_A_im, m_s5_log_dt, m_s5_B_re, m_s5_B_im, m_s5_C_re, m_s5_C_im, m_s5_D, m_s5_glu_w, m_s5_glu_b, m_hgrn_gamma, m_hgrn_norm, m_att_w_qkv, m_att_w_o, m_ffn_w_in, m_ffn_conv_w, m_ffn_conv_b, m_ffn_w_out, v_norm_mix, v_norm_ffn, v_norm_final, v_mix_w_in, v_mix_w_out, v_s5_A_re, v_s5_A_im, v_s5_log_dt, v_s5_B_re, v_s5_B_im, v_s5_C_re, v_s5_C_im, v_s5_D, v_s5_glu_w, v_s5_glu_b, v_hgrn_gamma, v_hgrn_norm, v_att_w_qkv, v_att_w_o, v_ffn_w_in, v_ffn_conv_w, v_ffn_conv_b, v_ffn_w_out):
    a = dict(locals())
    weights = BIG + SMALL
    w = {n: a[n] for n in weights}
    m = {n: a["m_" + n] for n in weights}
    v = {n: a["v_" + n] for n in weights}
    shards = {"ffn_conv_w": ffn_conv_w}
    C = Comm(shards, {n: w[n].shape for n in BIG})
    for n in ("mix_w_in", "ffn_w_in", "mix_w_out", "s5_glu_w", "ffn_w_out", "att_w_qkv", "att_w_o"):
        shards[n] = hosted(C, "cast_" + n, lambda p: cast_bf16(w[n], "cast_" + n, plan=p))
    sm = {n: w[n] for n in SMALL}
    sm["ffn_conv_b3"] = ffn_conv_b.reshape(2, 1, 2 * DFF)
    loss, gx, _, _ = local_step(x[0], positions.reshape(L, 1), loss_target[0], sm, C.W, C)
    res = {}
    for n in ("att_w_o", "s5_glu_w", "ffn_w_in", "ffn_w_out", "att_w_qkv", "mix_w_out", "ffn_conv_w", "mix_w_in"):
        res[n] = hosted(C, "adam_" + n, lambda p: adam_big(w[n], m[n], v[n], C.slots[n], "adam_" + n, plan=p))
    for names, key in ((SMALL_EARLY, "small_early"), (SMALL_LATE, "small_late")):
        g = _unpack(sum_slots(C.slots[key], "sum_" + key), w, names)

        def two_d(t):
            return {n: t[n].reshape(SMALL2D[n]) for n in names}

        upd = adam_small(two_d(w), two_d(m), two_d(v), two_d(g), names, "adam_" + key)
        for n in names:
            res[n] = (g[n],) + tuple(t.reshape(w[n].shape) for t in upd[n])
    total = lax.psum(loss[0, 0], ("x", "y", "c"))
    order = ("norm_mix", "norm_ffn", "norm_final", "mix_w_in", "mix_w_out", "s5_A_re", "s5_A_im", "s5_log_dt", "s5_B_re", "s5_B_im",
             "s5_C_re", "s5_C_im", "s5_D", "s5_glu_w", "s5_glu_b", "hgrn_gamma", "hgrn_norm", "att_w_qkv", "att_w_o", "ffn_w_in",
             "ffn_conv_w", "ffn_conv_b", "ffn_w_out")
    return (total, gx[None], *[res[n][0] for n in order], *[res[n][1] for n in order], *[res[n][2] for n in order],
            *[res[n][3] for n in order])
```

```python
import functools
import math

import numpy as np
import jax
import jax.numpy as jnp
from jax import lax
from jax.experimental import pallas as pl
from jax.experimental.pallas import tpu as pltpu

f32 = jnp.float32
BF = jnp.bfloat16
HI = lax.Precision.HIGHEST
S = jax.ShapeDtypeStruct
MESH = pl.DeviceIdType.MESH

L = 2048
D = 1024
EPS = 1e-6
S5W = 512
NST = 2048
HGC = 64
HGB = 32
DFF = 2816
ROPE_THETA = 500000.0
LR, B1, B2, AEPS, WD, STEP = 0.001, 0.9, 0.999, 1e-08, 0.01, 10
VMEM_LIMIT = 56 * 1024 * 1024


def _cp(sem=None):
    return pltpu.CompilerParams(dimension_semantics=sem, vmem_limit_bytes=VMEM_LIMIT)


ANY = pl.BlockSpec(memory_space=pl.ANY)
ROW_SHARDED = ("mix_w_out", "s5_glu_w", "ffn_w_out")


def _coords():
    x, y, c = lax.axis_index("x"), lax.axis_index("y"), lax.axis_index("c")
    return x, y, c, 2 * x + y, [(1 - x, y), (x, 1 - y), (1 - x, 1 - y)]


def _rows(start, n):
    return pl.ds(start if isinstance(start, int) else pl.multiple_of(start, 8), n)


def _cols(q, n):
    return pl.ds(pl.multiple_of(q * n, 128), n)


class Plan:
    def __init__(self):
        self.bufs, self.ops, self.nsem, self.out = {}, [], 0, {}

    def buf(self, key, arr=None, shape=None, write=False):
        b = self.bufs.setdefault(key, dict(arr=arr, shape=shape, write=False))
        b["write"] = b["write"] or write
        return key

    def add(self, op):
        op.base = self.nsem
        self.nsem += op.nsem
        self.ops.append(op)


class GatherOp:
    nsem = 13

    def __init__(self, plan, ksrc, kdst, l, shard_shape, rows, r0, nr, split):
        self.ksrc, self.kdst, self.l, (_, self.R, self.C), self.rows, self.r0, self.nr, self.split = (
            ksrc, kdst, l, shard_shape, rows, r0, nr, split)
        self.h = nr // 2 if split else nr
        plan.add(self)

    def _dst(self, R_, q, start, n):
        if self.rows:
            return R_[self.kdst].at[_rows(q * self.R + start, n), :]
        return R_[self.kdst].at[_rows(start, n), _cols(q, self.C)]

    def _mine(self, c):
        return self.r0 + (c * self.h if self.split else 0)

    def _theirs(self, c):
        return self.r0 + ((1 - c) * self.h if self.split else 0)

    def _copies(self, R_, sems):
        x, y, c, me, others = _coords()
        src = R_[self.ksrc]
        local = pltpu.make_async_copy(src.at[self.l, _rows(self.r0, self.nr), :], self._dst(R_, me, self.r0, self.nr),
                                      sems.at[self.base + 12])
        send, fwd = [], []
        for k, (px, py) in enumerate(others):
            q = 2 * px + py
            send.append((
                pltpu.make_async_remote_copy(src.at[self.l, _rows(self._mine(c), self.h), :], self._dst(R_, me, self._mine(c), self.h),
                                             sems.at[self.base + k], sems.at[self.base + 3 + k], device_id=(px, py, c), device_id_type=MESH),
                pltpu.make_async_remote_copy(src.at[self.l, _rows(self._mine(c), self.h), :], self._dst(R_, q, self._mine(c), self.h),
                                             sems.at[self.base + k], sems.at[self.base + 3 + k], device_id=(px, py, c), device_id_type=MESH)))
            fwd.append((
                pltpu.make_async_remote_copy(self._dst(R_, q, self._mine(c), self.h), self._dst(R_, q, self._mine(c), self.h),
                                             sems.at[self.base + 6 + k], sems.at[self.base + 9 + k], device_id=(x, y, 1 - c), device_id_type=MESH),
                pltpu.make_async_remote_copy(self._dst(R_, q, self._theirs(c), self.h), self._dst(R_, q, self._theirs(c), self.h),
                                             sems.at[self.base + 6 + k], sems.at[self.base + 9 + k], device_id=(x, y, 1 - c), device_id_type=MESH)))
        return local, send, fwd

    def start(self, R_, sems):
        local, send, _ = self._copies(R_, sems)
        local.start()
        for out, _ in send:
            out.start()

    def finish(self, R_, sems):
        local, send, fwd = self._copies(R_, sems)
        for k in range(3):
            send[k][1].wait_recv()
            if self.split:
                fwd[k][0].start()
        for k in range(3):
            if self.split:
                fwd[k][1].wait_recv()
                fwd[k][0].wait_send()
            send[k][0].wait_send()
        local.wait()


class ReduceOp:
    nsem = 7

    def __init__(self, plan, ksrc, kdst, l, shard_shape, rows, r0, nr, whole=False, half=False):
        self.ksrc, self.kdst, self.l, (self.R, self.C), self.rows, self.r0, self.nr, self.whole, self.half = (
            ksrc, kdst, l, shard_shape[-2:], rows, r0, nr, whole, half)
        plan.add(self)

    def _piece(self, R_, q):
        g = R_[self.ksrc]
        if self.whole:
            return g
        if self.rows:
            return g.at[_rows(q * (self.R // 2 if self.half else self.R) + self.r0, self.nr), :]
        return g.at[_rows(self.r0, self.nr), _cols(q, self.C)]

    def _slot(self, R_, q, c):
        if self.whole:
            return R_[self.kdst].at[2 * q + c]
        if self.half:
            return R_[self.kdst].at[q, self.l, _rows(c * (self.R // 2) + self.r0, self.nr), :]
        return R_[self.kdst].at[2 * q + c, self.l, _rows(self.r0, self.nr), :]

    def _copies(self, R_, sems):
        x, y, c, me, others = _coords()
        local = pltpu.make_async_copy(self._piece(R_, me), self._slot(R_, me, c), sems.at[self.base + 6])
        send = []
        for k, (px, py) in enumerate(others):
            q = 2 * px + py
            send.append((
                pltpu.make_async_remote_copy(self._piece(R_, q), self._slot(R_, me, c), sems.at[self.base + k],
                                             sems.at[self.base + 3 + k], device_id=(px, py, c), device_id_type=MESH),
                pltpu.make_async_remote_copy(self._piece(R_, q), self._slot(R_, q, c), sems.at[self.base + k],
                                             sems.at[self.base + 3 + k], device_id=(px, py, c), device_id_type=MESH)))
        return local, send

    def start(self, R_, sems):
        local, send = self._copies(R_, sems)
        local.start()
        for out, _ in send:
            out.start()

    def finish(self, R_, sems):
        local, send = self._copies(R_, sems)
        local.wait()
        for out, inn in send:
            inn.wait_recv()
            out.wait_send()


class ForwardOp:
    nsem = 8

    def __init__(self, plan, kdst, l, whole=False):
        self.kdst, self.l, self.whole = kdst, l, whole
        plan.add(self)

    def _slot(self, R_, s):
        return R_[self.kdst].at[s] if self.whole else R_[self.kdst].at[s, self.l]

    def _copies(self, R_, sems):
        x, y, c, me, others = _coords()
        return [(pltpu.make_async_remote_copy(self._slot(R_, 2 * q + c), self._slot(R_, 2 * q + c), sems.at[self.base + q],
                                              sems.at[self.base + 4 + q], device_id=(x, y, 1 - c), device_id_type=MESH),
                 pltpu.make_async_remote_copy(self._slot(R_, 2 * q + 1 - c), self._slot(R_, 2 * q + 1 - c), sems.at[self.base + q],
                                              sems.at[self.base + 4 + q], device_id=(x, y, 1 - c), device_id_type=MESH))
                for q in range(4)]

    def start(self, R_, sems):
        for out, _ in self._copies(R_, sems):
            out.start()

    def finish(self, R_, sems):
        for out, inn in self._copies(R_, sems):
            inn.wait_recv()
            out.wait_send()


class PairOp:
    nsem = 8

    def __init__(self, plan, ksrc, kdst, shard_shape, rows):
        self.ksrc, self.kdst, (self.R, self.C), self.rows = ksrc, kdst, shard_shape[-2:], rows
        plan.add(self)

    def _copies(self, R_, sems):
        x, y, c, me, others = _coords()
        g, dst, h = R_[self.ksrc], R_[self.kdst], self.R // 2
        out = []
        for q in range(4 if self.rows else 1):
            src = g.at[_rows(q * self.R + (1 - c) * h, h), :]
            land = dst.at[_rows(q * h, h), :]
            out.append(pltpu.make_async_remote_copy(src, land, sems.at[self.base + q], sems.at[self.base + 4 + q],
                                                    device_id=(x, y, 1 - c), device_id_type=MESH))
        return out

    def start(self, R_, sems):
        for cp in self._copies(R_, sems):
            cp.start()

    def finish(self, R_, sems):
        for cp in self._copies(R_, sems):
            cp.wait_recv()
            cp.wait_send()


class HalfForwardOp:
    nsem = 2

    def __init__(self, plan, kdst, l, shard_shape):
        self.kdst, self.l, self.R = kdst, l, shard_shape[-2]
        plan.add(self)

    def _copy(self, R_, sems, core):
        x, y, c, me, others = _coords()
        part = R_[self.kdst].at[:, self.l, _rows((c if core == "mine" else 1 - c) * (self.R // 2), self.R // 2), :]
        return pltpu.make_async_remote_copy(part, part, sems.at[self.base], sems.at[self.base + 1],
                                            device_id=(x, y, 1 - c), device_id_type=MESH)

    def start(self, R_, sems):
        self._copy(R_, sems, "mine").start()

    def finish(self, R_, sems):
        self._copy(R_, sems, "theirs").wait_recv()
        self._copy(R_, sems, "mine").wait_send()


def pair_sum(g, gsib, rows, R, name):
    h = R // 2
    W = g.shape[1]
    tr = h if h * W * 2 <= 2 ** 21 else 128
    nq = 4 if rows else 1

    def body(c_ref, a_ref, b_ref, o_ref):
        o_ref[...] = (a_ref[...].astype(f32) + b_ref[...].astype(f32)).astype(o_ref.dtype)

    half = pl.BlockSpec((tr, W), lambda q, i, c_ref: (q * (h // tr) + i, 0))
    mine = pl.BlockSpec((tr, W), lambda q, i, c_ref: (q * (R // tr) + c_ref[0] * (h // tr) + i, 0))
    return pl.pallas_call(
        body, grid_spec=pltpu.PrefetchScalarGridSpec(num_scalar_prefetch=1, grid=(nq, h // tr), in_specs=[mine, half],
                                                     out_specs=half),
        out_shape=S(gsib.shape, g.dtype), compiler_params=_cp(("parallel", "parallel")),
        name=name)(lax.axis_index("c").reshape(1).astype(jnp.int32), g, gsib)


def pcall(body, plan, *, grid, in_specs, out_specs, out_shape, scratch_shapes=(), sem, name, args):
    multi = isinstance(out_shape, (list, tuple))
    if plan is None or not plan.ops:
        return pl.pallas_call(body, grid=grid, in_specs=in_specs, out_specs=out_specs, out_shape=out_shape,
                              scratch_shapes=list(scratch_shapes), compiler_params=_cp(sem), name=name)(*args)
    outs = list(out_shape) if multi else [out_shape]
    ospecs = list(out_specs) if multi else [out_specs]
    kin = [k for k, b in plan.bufs.items() if b["arr"] is not None]
    kout = [k for k, b in plan.bufs.items() if b["write"]]
    n_in, n_out, n_scr = len(in_specs), len(outs), len(scratch_shapes)

    def wrapped(*refs):
        o0 = n_in + len(kin)
        s0 = o0 + n_out + len(kout)
        R_ = dict(zip(kin, refs[n_in:o0]))
        R_.update(zip(kout, refs[o0 + n_out:s0]))
        sems = refs[s0 + n_scr]
        first = functools.reduce(jnp.logical_and, [pl.program_id(d) == 0 for d in range(len(grid))])
        last = functools.reduce(jnp.logical_and, [pl.program_id(d) == grid[d] - 1 for d in range(len(grid))])

        @pl.when(first)
        def _():
            for op in plan.ops:
                op.start(R_, sems)

        body(*refs[:n_in], *refs[o0:o0 + n_out], *refs[s0:s0 + n_scr])

        @pl.when(last)
        def _():
            for op in plan.ops:
                op.finish(R_, sems)

    def shape_of(k):
        b = plan.bufs[k]
        return S(b["arr"].shape, b["arr"].dtype) if b["arr"] is not None else b["shape"]

    res = pl.pallas_call(
        wrapped, grid=grid, in_specs=list(in_specs) + [ANY] * len(kin), out_specs=ospecs + [ANY] * len(kout),
        out_shape=outs + [shape_of(k) for k in kout],
        scratch_shapes=list(scratch_shapes) + [pltpu.SemaphoreType.DMA((plan.nsem,))],
        input_output_aliases={n_in + kin.index(k): n_out + kout.index(k) for k in kout if plan.bufs[k]["arr"] is not None},
        compiler_params=pltpu.CompilerParams(dimension_semantics=("arbitrary",) * len(grid), vmem_limit_bytes=VMEM_LIMIT,
                                             has_side_effects=True),
        name=name)(*args, *[plan.bufs[k]["arr"] for k in kin])
    plan.out = dict(zip(kout, res[n_out:]))
    return list(res[:n_out]) if multi else res[0]


def _dg(a, b, ca, cb):
    return lax.dot_general(a.astype(BF), b.astype(BF), (((ca,), (cb,)), ((), ())), preferred_element_type=f32)


@jax.custom_vjp
def dot_nn(a, b):
    return _dg(a, b, 1, 0)


@jax.custom_vjp
def dot_nt(a, b):
    return _dg(a, b, 1, 1)


@jax.custom_vjp
def dot_tn(a, b):
    return _dg(a, b, 0, 0)


dot_nn.defvjp(lambda a, b: (dot_nn(a, b), (a, b)),
              lambda r, g: (dot_nt(g, r[1]).astype(r[0].dtype), dot_tn(r[0], g).astype(r[1].dtype)))
dot_nt.defvjp(lambda a, b: (dot_nt(a, b), (a, b)),
              lambda r, g: (dot_nn(g, r[1]).astype(r[0].dtype), dot_tn(g, r[0]).astype(r[1].dtype)))
dot_tn.defvjp(lambda a, b: (dot_tn(a, b), (a, b)),
              lambda r, g: (dot_nt(r[1], g).astype(r[0].dtype), dot_nn(r[0], g).astype(r[1].dtype)))


def matmul(a, b, *, mode, tm, tn, tk, out_dtype=f32, add=None, b_lead=None, a_spec=None, b_spec=None, dims=None, plan=None, name):
    a_over, b_over = a_spec, b_spec
    if mode == "nn":
        (M, K), N = a.shape[-2:], b.shape[-1]
        a_spec = pl.BlockSpec((tm, tk), lambda i, j, k: (i, k))
        b_blk, b_idx, ca, cb = (tk, tn), (lambda i, j, k: (k, j)), 1, 0
    elif mode == "nt":
        (M, K), N = a.shape[-2:], b.shape[-2]
        a_spec = pl.BlockSpec((tm, tk), lambda i, j, k: (i, k))
        b_blk, b_idx, ca, cb = (tn, tk), (lambda i, j, k: (j, k)), 1, 1
    else:
        (K, M), N = a.shape[-2:], b.shape[-1]
        a_spec = pl.BlockSpec((tk, tm), lambda i, j, k: (k, i))
        b_blk, b_idx, ca, cb = (tk, tn), (lambda i, j, k: (k, j)), 0, 0
    if dims is not None:
        M, N, K = dims
    assert M % tm == 0 and N % tn == 0 and K % tk == 0, (name, M, N, K, tm, tn, tk)
    if b_lead is None:
        b_spec = pl.BlockSpec(b_blk, b_idx)
    else:
        b_spec = pl.BlockSpec((None,) + b_blk, lambda i, j, k: (b_lead,) + b_idx(i, j, k))
    if a_over is not None:
        a_spec = a_over
    if b_over is not None:
        b_spec = b_over
    nk = K // tk
    has_add = add is not None

    def body(*refs):
        a_ref, b_ref = refs[0], refs[1]
        add_ref = refs[2] if has_add else None
        o_ref = refs[2 + has_add]
        p = _dg(a_ref[...], b_ref[...], ca, cb)

        def fin(v):
            if has_add:
                v = v + add_ref[...].astype(f32)
            o_ref[...] = v.astype(o_ref.dtype)

        if nk == 1:
            fin(p)
        else:
            acc = refs[3 + has_add]
            k = pl.program_id(2)

            @pl.when(k == 0)
            def _():
                acc[...] = p

            @pl.when(k > 0)
            def _():
                acc[...] += p

            @pl.when(k == nk - 1)
            def _():
                fin(acc[...])

    in_specs = [a_spec, b_spec]
    args = [a, b]
    if has_add:
        in_specs.append(pl.BlockSpec((tm, tn), lambda i, j, k: (i, j)))
        args.append(add)
    return pcall(body, plan, grid=(M // tm, N // tn, nk), in_specs=in_specs,
                 out_specs=pl.BlockSpec((tm, tn), lambda i, j, k: (i, j)), out_shape=S((M, N), out_dtype),
                 scratch_shapes=[pltpu.VMEM((tm, tn), f32)] if nk > 1 else [],
                 sem=("parallel", "parallel", "arbitrary"), name=name, args=args)


def _rms(xv, gv):
    return xv * lax.rsqrt(jnp.mean(xv * xv, axis=-1, keepdims=True) + EPS) * gv


TR = 256


def rms_fwd(x, g, name):
    def body(x_ref, g_ref, o_ref):
        o_ref[...] = _rms(x_ref[...], g_ref[...]).astype(o_ref.dtype)

    return pl.pallas_call(
        body, grid=(L // TR,),
        in_specs=[pl.BlockSpec((TR, D), lambda i: (i, 0)), pl.BlockSpec((1, D), lambda i: (0, 0))],
        out_specs=pl.BlockSpec((TR, D), lambda i: (i, 0)), out_shape=S((L, D), BF),
        compiler_params=_cp(("parallel",)), name=name)(x, g)


def rms_bwd(x, g, dys, dres, name, plan=None):
    nd = len(dys)

    def body(*refs):
        x_ref, g_ref = refs[0], refs[1]
        dr_ref, dh_ref, dg_ref = refs[2 + nd:]
        dy = refs[2][...].astype(f32)
        for r in refs[3:2 + nd]:
            dy = dy + r[...].astype(f32)
        _, vjp = jax.vjp(_rms, x_ref[...], g_ref[...])
        dx, dg = vjp(dy)
        dh_ref[...] = dr_ref[...] + dx

        @pl.when(pl.program_id(0) == 0)
        def _():
            dg_ref[...] = jnp.zeros_like(dg_ref)

        dg_ref[...] += dg

    row = pl.BlockSpec((TR, D), lambda i: (i, 0))
    vec = pl.BlockSpec((1, D), lambda i: (0, 0))
    return pcall(body, plan, grid=(L // TR,), in_specs=[row, vec] + [row] * (nd + 1), out_specs=[row, vec],
                 out_shape=[S((L, D), f32), S((1, D), f32)], sem=("arbitrary",), name=name, args=[x, g, *dys, dres])


def loss_head(h, g, tgt):
    def f(hv, gv, tv):
        y = _rms(hv, gv)
        return 0.5 * jnp.sum(jnp.mean(jnp.square(y - tv), axis=-1))

    def body(h_ref, g_ref, t_ref, l_ref, dh_ref, dg_ref):
        val, vjp = jax.vjp(f, h_ref[...], g_ref[...], t_ref[...])
        dh, dg, _ = vjp(jnp.ones((), f32))
        dh_ref[...] = dh

        @pl.when(pl.program_id(0) == 0)
        def _():
            dg_ref[...] = jnp.zeros_like(dg_ref)
            l_ref[...] = jnp.zeros_like(l_ref)

        dg_ref[...] += dg
        l_ref[...] += jnp.full((1, 128), val, f32)

    row = pl.BlockSpec((TR, D), lambda i: (i, 0))
    vec = pl.BlockSpec((1, D), lambda i: (0, 0))
    return pl.pallas_call(
        body, grid=(L // TR,), in_specs=[row, vec, row],
        out_specs=[pl.BlockSpec((1, 128), lambda i: (0, 0)), row, vec],
        out_shape=[S((1, 128), f32), S((L, D), f32), S((1, D), f32)],
        compiler_params=_cp(("arbitrary",)), name="loss_head")(h, g, tgt)


def _col_to_row(c):
    n = c.shape[0]
    t = jnp.broadcast_to(c, (n, 128)).T
    r = lax.broadcasted_iota(jnp.int32, (128, n), 0)
    return jnp.sum(jnp.where(r == 0, t, 0.0), axis=0, keepdims=True)


def _s5_param_map(are, aim, ldt_row, bre, bim, cre, cim):
    n = NST
    gi = lax.broadcasted_iota(jnp.int32, (n, 32), 0) // 64
    gj = lax.broadcasted_iota(jnp.int32, (n, 32), 1)
    ldt = jnp.sum(jnp.where(gi == gj, ldt_row, 0.0), axis=1, keepdims=True)
    dt = jnp.exp(ldt)
    mag = jnp.exp(are * dt)
    abr = mag * jnp.cos(aim * dt)
    abi = mag * jnp.sin(aim * dt)
    den = are * are + aim * aim
    nr, ni = abr - 1.0, abi
    cr = (nr * are + ni * aim) / den
    ci = (ni * are - nr * aim) / den
    bbr = cr * bre - ci * bim
    bbi = cr * bim + ci * bre
    tc = lax.broadcasted_iota(jnp.int32, (16, 128), 0)
    tl = lax.broadcasted_iota(jnp.int32, (16, 128), 1)
    T = (tl % 16 == tc).astype(f32)
    mr = (lax.broadcasted_iota(jnp.int32, (n, 128), 0) // 64) % 8
    mc = lax.broadcasted_iota(jnp.int32, (n, 128), 1) // 16
    mask = (mr == mc).astype(f32)

    def expand(v):
        return jnp.dot(v, T, precision=HI, preferred_element_type=f32) * mask

    return expand(bbr), expand(bbi), expand(cre), expand(cim), _col_to_row(abr), _col_to_row(abi)


def s5_params_fwd(are, aim, ldt_row, bre, bim, cre, cim):
    def body(*refs):
        outs = _s5_param_map(*[r[...] for r in refs[:7]])
        for o_ref, o in zip(refs[7:], outs):
            o_ref[...] = o

    return pl.pallas_call(
        body, out_shape=[S((NST, 128), f32)] * 4 + [S((1, NST), f32)] * 2,
        compiler_params=_cp(), name="s5_params_fwd")(are, aim, ldt_row, bre, bim, cre, cim)


def s5_params_bwd(are, aim, ldt_row, bre, bim, cre, cim, cots):
    def body(*refs):
        _, vjp = jax.vjp(_s5_param_map, *[r[...] for r in refs[:7]])
        gs = vjp(tuple(r[...] for r in refs[7:13]))
        for o_ref, o in zip(refs[13:], gs):
            o_ref[...] = o

    return pl.pallas_call(
        body, out_shape=[S((NST, 1), f32)] * 2 + [S((1, 32), f32)] + [S((NST, 16), f32)] * 4,
        compiler_params=_cp(), name="s5_params_bwd")(are, aim, ldt_row, bre, bim, cre, cim, *cots)


def _cpowers(ar, ai):
    out = [(ar, ai)]
    for _ in range(7):
        pr, pi = out[-1]
        out.append((pr * ar - pi * ai, pr * ai + pi * ar))
    return out


def _ctable(pw, rid, power):
    tr_ = jnp.zeros(rid.shape, f32)
    ti_ = jnp.zeros(rid.shape, f32)
    for r in range(8):
        pr, pi = pw[power(r) - 1]
        tr_ = jnp.where(rid == r, pr, tr_)
        ti_ = jnp.where(rid == r, pi, ti_)
    return tr_, ti_


NT5 = 4
RC = 256


def s5_scan_fwd(proj, wbr, wbi, wcr, wci, abr, abi, drow, plan=None):
    def body(u_ref, wbr_ref, wbi_ref, wcr_ref, wci_ref, ar_ref, ai_ref, d_ref, xr_ref, xi_ref, y_ref):
        wbr_v, wbi_v = wbr_ref[...], wbi_ref[...]
        for r in range(L // RC):
            rows = pl.ds(r * RC, RC)
            ub = u_ref[rows, :]
            xr_ref[rows, :] = dot_nt(ub, wbr_v)
            xi_ref[rows, :] = dot_nt(ub, wbi_v)
        pw = _cpowers(ar_ref[...], ai_ref[...])
        rid = lax.broadcasted_iota(jnp.int32, (8, 512), 0)
        tr_, ti_ = _ctable(pw, rid, lambda r: r + 1)

        def group(j, c):
            cr, ci = c
            rows = pl.ds(pl.multiple_of(j * 8, 8), 8)
            br, bi = xr_ref[rows, :], xi_ref[rows, :]
            for s in (1, 2, 4):
                pr, pi = pw[s - 1]
                sr = jnp.where(rid >= s, pltpu.roll(br, s, 0), 0.0)
                si = jnp.where(rid >= s, pltpu.roll(bi, s, 0), 0.0)
                br, bi = br + pr * sr - pi * si, bi + pr * si + pi * sr
            br, bi = br + tr_ * cr - ti_ * ci, bi + tr_ * ci + ti_ * cr
            xr_ref[rows, :] = br
            xi_ref[rows, :] = bi
            return br[7:8], bi[7:8]

        z = jnp.zeros((1, 512), f32)
        lax.fori_loop(0, L // 8, group, (z, z), unroll=2)
        wcr_v, wci_v, dv = wcr_ref[...], wci_ref[...], d_ref[...]
        for r in range(L // RC):
            rows = pl.ds(r * RC, RC)
            y_ref[rows, :] = (dot_nn(xr_ref[rows, :], wcr_v) - dot_nn(xi_ref[rows, :], wci_v)
                              + dv * u_ref[rows, :])

    wspec = pl.BlockSpec((512, 128), lambda j: (j, 0))
    aspec = pl.BlockSpec((1, 512), lambda j: (0, j))
    return pcall(
        body, plan, grid=(NT5,),
        in_specs=[pl.BlockSpec((L, 128), lambda j: (0, j)), wspec, wspec, wspec, wspec, aspec, aspec,
                  pl.BlockSpec((1, 128), lambda j: (0, j))],
        out_specs=[pl.BlockSpec((L, 512), lambda j: (0, j)), pl.BlockSpec((L, 512), lambda j: (0, j)),
                   pl.BlockSpec((L, 128), lambda j: (0, j))],
        out_shape=[S((L, NST), f32), S((L, NST), f32), S((L, S5W), f32)],
        sem=("parallel",), name="s5_scan_fwd", args=[proj, wbr, wbi, wcr, wci, abr, abi, drow])


def s5_scan_bwd(dy, proj, xs_re, xs_im, wbr, wbi, wcr, wci, abr, abi, drow, plan=None):
    def body(dy_ref, u_ref, xr_ref, xi_ref, wbr_ref, wbi_ref, wcr_ref, wci_ref, ar_ref, ai_ref, d_ref,
             du_ref, gwbr_ref, gwbi_ref, gwcr_ref, gwci_ref, gar_ref, gai_ref, gd_ref, lr_ref, li_ref):
        wcr_v, wci_v = wcr_ref[...], wci_ref[...]
        gwcr = jnp.zeros((512, 128), f32)
        gwci = jnp.zeros((512, 128), f32)
        gd = jnp.zeros((1, 128), f32)
        for r in range(L // RC):
            rows = pl.ds(r * RC, RC)
            dyv = dy_ref[rows, :]
            lr_ref[rows, :] = dot_nt(dyv, wcr_v)
            li_ref[rows, :] = -dot_nt(dyv, wci_v)
            gwcr += dot_tn(xr_ref[rows, :], dyv)
            gwci -= dot_tn(xi_ref[rows, :], dyv)
            gd += jnp.sum(dyv * u_ref[rows, :], axis=0, keepdims=True)
        gwcr_ref[...] = gwcr
        gwci_ref[...] = gwci
        gd_ref[...] = gd
        pw = _cpowers(ar_ref[...], -ai_ref[...])
        rid = lax.broadcasted_iota(jnp.int32, (8, 512), 0)
        tr_, ti_ = _ctable(pw, rid, lambda r: 8 - r)

        def group(i, c):
            cr, ci, gar, gai = c
            j = L // 8 - 1 - i
            rows = pl.ds(pl.multiple_of(j * 8, 8), 8)
            br, bi = lr_ref[rows, :], li_ref[rows, :]
            for s in (1, 2, 4):
                pr, pi = pw[s - 1]
                sr = jnp.where(rid < 8 - s, pltpu.roll(br, 8 - s, 0), 0.0)
                si = jnp.where(rid < 8 - s, pltpu.roll(bi, 8 - s, 0), 0.0)
                br, bi = br + pr * sr - pi * si, bi + pr * si + pi * sr
            br, bi = br + tr_ * cr - ti_ * ci, bi + tr_ * ci + ti_ * cr
            lr_ref[rows, :] = br
            li_ref[rows, :] = bi
            nr = jnp.where(rid < 7, pltpu.roll(br, 7, 0), cr)
            ni = jnp.where(rid < 7, pltpu.roll(bi, 7, 0), ci)
            xr, xi = xr_ref[rows, :], xi_ref[rows, :]
            return br[0:1], bi[0:1], gar + xr * nr + xi * ni, gai + xr * ni - xi * nr

        z = jnp.zeros((1, 512), f32)
        z8 = jnp.zeros((8, 512), f32)
        _, _, gar, gai = lax.fori_loop(0, L // 8, group, (z, z, z8, z8), unroll=2)
        gar_ref[...] = jnp.sum(gar, axis=0, keepdims=True)
        gai_ref[...] = jnp.sum(gai, axis=0, keepdims=True)
        wbr_v, wbi_v, dv = wbr_ref[...], wbi_ref[...], d_ref[...]
        gwbr = jnp.zeros((512, 128), f32)
        gwbi = jnp.zeros((512, 128), f32)
        for r in range(L // RC):
            rows = pl.ds(r * RC, RC)
            lrv, liv, uv = lr_ref[rows, :], li_ref[rows, :], u_ref[rows, :]
            du_ref[rows, :] = (dot_nn(lrv, wbr_v) + dot_nn(liv, wbi_v) + dv * dy_ref[rows, :]).astype(du_ref.dtype)
            gwbr += dot_tn(lrv, uv)
            gwbi += dot_tn(liv, uv)
        gwbr_ref[...] = gwbr
        gwbi_ref[...] = gwbi

    wspec = pl.BlockSpec((512, 128), lambda j: (j, 0))
    aspec = pl.BlockSpec((1, 512), lambda j: (0, j))
    col = pl.BlockSpec((L, 128), lambda j: (0, j))
    st = pl.BlockSpec((L, 512), lambda j: (0, j))
    dspec = pl.BlockSpec((1, 128), lambda j: (0, j))
    return pcall(
        body, plan, grid=(NT5,),
        in_specs=[col, col, st, st, wspec, wspec, wspec, wspec, aspec, aspec, dspec],
        out_specs=[col, wspec, wspec, wspec, wspec, aspec, aspec, dspec],
        out_shape=[S((L, S5W), BF)] + [S((NST, 128), f32)] * 4 + [S((1, NST), f32)] * 2 + [S((1, S5W), f32)],
        scratch_shapes=[pltpu.VMEM((L, 512), f32), pltpu.VMEM((L, 512), f32)],
        sem=("parallel",), name="s5_scan_bwd", args=[dy, proj, xs_re, xs_im, wbr, wbi, wcr, wci, abr, abi, drow])


def _glu(y, w, b):
    z = jax.nn.gelu(y)
    return z * jax.nn.sigmoid(dot_nn(z, w) + b)


def s5_glu_fwd(y, w, b):
    def body(y_ref, w_ref, b_ref, o_ref):
        o_ref[...] = _glu(y_ref[...], w_ref[...], b_ref[...]).astype(o_ref.dtype)

    return pl.pallas_call(
        body, grid=(L // TR,),
        in_specs=[pl.BlockSpec((TR, S5W), lambda i: (i, 0)), pl.BlockSpec((S5W, S5W), lambda i: (0, 0)),
                  pl.BlockSpec((1, S5W), lambda i: (0, 0))],
        out_specs=pl.BlockSpec((TR, S5W), lambda i: (i, 0)), out_shape=S((L, S5W), BF),
        compiler_params=_cp(("parallel",)), name="s5_glu_fwd")(y, w, b)


def s5_glu_bwd(y, w, b, dmix):
    def body(y_ref, w_ref, b_ref, g_ref, dy_ref, dw_ref, db_ref):
        _, vjp = jax.vjp(_glu, y_ref[...], w_ref[...].astype(f32), b_ref[...])
        dy, dw, db = vjp(g_ref[...])
        dy_ref[...] = dy

        @pl.when(pl.program_id(0) == 0)
        def _():
            dw_ref[...] = jnp.zeros_like(dw_ref)
            db_ref[...] = jnp.zeros_like(db_ref)

        dw_ref[...] += dw
        db_ref[...] += db

    row = pl.BlockSpec((TR, S5W), lambda i: (i, 0))
    return pl.pallas_call(
        body, grid=(L // TR,),
        in_specs=[row, pl.BlockSpec((S5W, S5W), lambda i: (0, 0)), pl.BlockSpec((1, S5W), lambda i: (0, 0)), row],
        out_specs=[row, pl.BlockSpec((S5W, S5W), lambda i: (0, 0)), pl.BlockSpec((1, S5W), lambda i: (0, 0))],
        out_shape=[S((L, S5W), f32), S((S5W, S5W), f32), S((1, S5W), f32)],
        compiler_params=_cp(("arbitrary",)), name="s5_glu_bwd")(y, w, b, dmix)


def _dg3(a, b, ca, cb):
    ah, bh = a.astype(BF), b.astype(BF)
    al, bl = (a - ah.astype(f32)).astype(BF), (b - bh.astype(f32)).astype(BF)
    return _dg(ah, bh, ca, cb) + _dg(ah, bl, ca, cb) + _dg(al, bh, ca, cb)


@jax.custom_vjp
def hi_nn(a, b):
    return _dg3(a, b, 1, 0)


@jax.custom_vjp
def hi_nt(a, b):
    return _dg3(a, b, 1, 1)


@jax.custom_vjp
def hi_tn(a, b):
    return _dg3(a, b, 0, 0)


hi_nn.defvjp(lambda a, b: (hi_nn(a, b), (a, b)), lambda r, g: (hi_nt(g, r[1]), hi_tn(r[0], g)))
hi_nt.defvjp(lambda a, b: (hi_nt(a, b), (a, b)), lambda r, g: (hi_nn(g, r[1]), hi_tn(g, r[0])))
hi_tn.defvjp(lambda a, b: (hi_tn(a, b), (a, b)), lambda r, g: (hi_nt(r[1], g), hi_nn(r[0], g)))


def _hgrn_chunk(St, xq, xf, xi, xg, gam, ng):
    lb = jax.nn.sigmoid(gam[0:1] - gam[1:2])
    q = jax.nn.silu(xq)
    f = lb + (1.0 - lb) * jax.nn.sigmoid(xf)
    k = 1.0 - f
    g = jnp.log(f)
    ti = lax.broadcasted_iota(jnp.int32, (HGC, HGC), 0)
    si = lax.broadcasted_iota(jnp.int32, (HGC, HGC), 1)
    causal = si <= ti
    b = jnp.dot(causal.astype(f32), g, precision=HI, preferred_element_type=f32)
    qe = q * jnp.exp(b)
    o = dot_nt(qe, St)
    parts = []
    for i in range(HGC // HGB):
        r, n, mid = slice(HGB * i, HGB * (i + 1)), HGB * (i + 1), HGB * i + HGB // 2
        base = b[mid:mid + 1]
        sc = hi_nt(q[r] * jnp.exp(b[r] - base), k[:n] * jnp.exp(base - b[:n]))
        parts.append(dot_nn(jnp.where(causal[r, :n], sc, 0.0), xi[:n]))
    o = o + jnp.concatenate(parts, axis=0)
    bl = b[HGC - 1:HGC]
    St_new = St * jnp.exp(bl) + dot_tn(xi, k * jnp.exp(bl - b))
    o = o * lax.rsqrt(jnp.mean(o * o, axis=-1, keepdims=True) + EPS) * ng
    return St_new, o * jax.nn.silu(xg)


NCH = L // HGC


def hgrn_fwd(proj, gamma, hnorm, plan=None):
    def body(q_ref, f_ref, i_ref, g_ref, gam_ref, ng_ref, o_ref, ss_ref, st):
        @pl.when(pl.program_id(0) == 0)
        def _():
            st[...] = jnp.zeros_like(st)

        for h in range(4):
            sl = slice(h * 128, (h + 1) * 128)
            s0 = st[h]
            ss_ref[0, h] = s0
            s1, o = _hgrn_chunk(s0, q_ref[:, sl], f_ref[:, sl], i_ref[:, sl], g_ref[:, sl], gam_ref[:, sl], ng_ref[:, sl])
            st[h] = s1
            o_ref[:, sl] = o.astype(o_ref.dtype)

    def pj(n):
        return pl.BlockSpec((HGC, 512), lambda c: (c, n))

    return pcall(
        body, plan, grid=(NCH,),
        in_specs=[pj(1), pj(2), pj(3), pj(4), pl.BlockSpec((2, 512), lambda c: (0, 0)), pl.BlockSpec((1, 512), lambda c: (0, 0))],
        out_specs=[pl.BlockSpec((HGC, 512), lambda c: (c, 0)), pl.BlockSpec((1, 4, 128, 128), lambda c: (c, 0, 0, 0))],
        out_shape=[S((L, 512), BF), S((NCH, 4, 128, 128), f32)],
        scratch_shapes=[pltpu.VMEM((4, 128, 128), f32)],
        sem=("arbitrary",), name="hgrn_fwd", args=[proj, proj, proj, proj, gamma, hnorm])


def hgrn_bwd(proj, gamma, hnorm, ssave, dmix, du, plan=None):
    def body(q_ref, f_ref, i_ref, g_ref, gam_ref, ng_ref, ss_ref, do_ref, du_ref, dp_ref, dgam_ref, dng_ref, dst):
        @pl.when(pl.program_id(0) == 0)
        def _():
            dst[...] = jnp.zeros_like(dst)
            dgam_ref[...] = jnp.zeros_like(dgam_ref)
            dng_ref[...] = jnp.zeros_like(dng_ref)

        dp_ref[:, 0:512] = du_ref[...]
        for h in range(4):
            sl = slice(h * 128, (h + 1) * 128)
            _, vjp = jax.vjp(_hgrn_chunk, ss_ref[0, h], q_ref[:, sl], f_ref[:, sl], i_ref[:, sl], g_ref[:, sl],
                             gam_ref[:, sl], ng_ref[:, sl])
            ds, dq, df, di, dg, dgam, dng = vjp((dst[h], do_ref[:, sl]))
            dst[h] = ds
            for n, v in enumerate((dq, df, di, dg)):
                dp_ref[:, 512 * (n + 1) + h * 128: 512 * (n + 1) + (h + 1) * 128] = v.astype(dp_ref.dtype)
            dgam_ref[:, sl] += dgam
            dng_ref[:, sl] += dng

    def pj(n):
        return pl.BlockSpec((HGC, 512), lambda i: (NCH - 1 - i, n))

    return pcall(
        body, plan, grid=(NCH,),
        in_specs=[pj(1), pj(2), pj(3), pj(4), pl.BlockSpec((2, 512), lambda i: (0, 0)), pl.BlockSpec((1, 512), lambda i: (0, 0)),
                  pl.BlockSpec((1, 4, 128, 128), lambda i: (NCH - 1 - i, 0, 0, 0)), pj(1), pj(0)],
        out_specs=[pl.BlockSpec((HGC, 2560), lambda i: (NCH - 1 - i, 0)), pl.BlockSpec((2, 512), lambda i: (0, 0)),
                   pl.BlockSpec((1, 512), lambda i: (0, 0))],
        out_shape=[S((L, 2560), BF), S((2, 512), f32), S((1, 512), f32)],
        scratch_shapes=[pltpu.VMEM((4, 128, 128), f32)],
        sem=("arbitrary",), name="hgrn_bwd", args=[proj, proj, proj, proj, gamma, hnorm, ssave, dmix, du])


def _earlier(h_ref, k, r0, n):
    if r0 > 0:
        return h_ref[pl.ds(r0 - k, n), :]
    rid = lax.broadcasted_iota(jnp.int32, (8, h_ref.shape[1]), 0)
    head = jnp.where(rid >= k, pltpu.roll(h_ref[pl.ds(0, 8), :], k, 0), 0.0)
    return jnp.concatenate([head, h_ref[pl.ds(8 - k, n - 8), :]], axis=0)


def _conv3_rows(h_ref, w, b, r0, n=None):
    n = CR if n is None else n
    h1, h2 = _earlier(h_ref, 1, r0, n), _earlier(h_ref, 2, r0, n)
    return w[2:3] * h_ref[pl.ds(r0, n), :] + w[1:2] * h1 + w[0:1] * h2 + b, h1, h2


CT = 128
NCT = DFF // CT
CR = 64


def convact_fwd(hu, cw, cb, layer, plan=None):
    def body(ha_ref, hb_ref, wa_ref, wb_ref, ba_ref, bb_ref, o_ref):
        ca = _conv3_rows(ha_ref, wa_ref[...], ba_ref[...], 0, L)[0]
        cb_ = _conv3_rows(hb_ref, wb_ref[...], bb_ref[...], 0, L)[0]
        o_ref[...] = (jax.nn.silu(ca) * cb_).astype(o_ref.dtype)

    def h(off):
        return pl.BlockSpec((L, CT), lambda j: (0, j + off))

    def w(off):
        return pl.BlockSpec((3, CT), lambda j: (0, j + off))

    def b(off):
        return pl.BlockSpec((None, 1, CT), lambda j: (layer, 0, j + off))

    return pcall(body, plan, grid=(NCT,), in_specs=[h(0), h(NCT), w(0), w(NCT), b(0), b(NCT)],
                 out_specs=pl.BlockSpec((L, CT), lambda j: (0, j)), out_shape=S((L, DFF), BF),
                 sem=("parallel",), name=f"convact_fwd{layer}", args=[hu, hu, cw, cw, cb, cb])


def convact_bwd(hu, cw, cb, dact, layer, plan=None):
    def body(ha_ref, hb_ref, wa_ref, wb_ref, ba_ref, bb_ref, g_ref, dh_ref, dw_ref, db_ref, sh, sw, sb, da_scr, db_scr):
        j = pl.program_id(0)

        def fold(x):
            return functools.reduce(jnp.add, [x[8 * m:8 * m + 8] for m in range(CR // 8)])

        @pl.when(j < NCT)
        def _():
            wa, wb, ba, bb = wa_ref[...], wb_ref[...], ba_ref[...], bb_ref[...]
            da_scr[pl.ds(L, 8), :] = jnp.zeros((8, CT), f32)
            db_scr[pl.ds(L, 8), :] = jnp.zeros((8, CT), f32)
            acc = [jnp.zeros((8, CT), f32) for _ in range(8)]
            for c in range(L // CR):
                r0 = c * CR
                ca, a1, a2 = _conv3_rows(ha_ref, wa, ba, r0)
                cb_, b1, b2 = _conv3_rows(hb_ref, wb, bb, r0)
                g = g_ref[pl.ds(r0, CR), :].astype(f32)
                sg = jax.nn.sigmoid(ca)
                dca = g * cb_ * (sg * (1.0 + ca * (1.0 - sg)))
                dcb = g * (ca * sg)
                da_scr[pl.ds(r0, CR), :] = dca
                db_scr[pl.ds(r0, CR), :] = dcb
                terms = (dca * a2, dca * a1, dca * ha_ref[pl.ds(r0, CR), :], dca,
                         dcb * b2, dcb * b1, dcb * hb_ref[pl.ds(r0, CR), :], dcb)
                acc = [a + fold(t) for a, t in zip(acc, terms)]
            rows = [jnp.sum(a, axis=0, keepdims=True) for a in acc]
            for k in range(3):
                dw_ref[k:k + 1, :] = rows[k]
                sw[j, k:k + 1, :] = rows[4 + k]
            db_ref[...] = rows[3]
            sb[j] = rows[7]
            for c in range(L // CR):
                r0 = c * CR
                for scr, w, out in ((da_scr, wa, dh_ref), (db_scr, wb, sh.at[j])):
                    dh = (w[2:3] * scr[pl.ds(r0, CR), :] + w[1:2] * scr[pl.ds(r0 + 1, CR), :]
                          + w[0:1] * scr[pl.ds(r0 + 2, CR), :])
                    out[pl.ds(r0, CR), :] = dh.astype(out.dtype)

        @pl.when(j >= NCT)
        def _():
            dh_ref[...] = sh[j - NCT]
            dw_ref[...] = sw[j - NCT]
            db_ref[...] = sb[j - NCT]

    def lo(j):
        return jnp.minimum(j, NCT - 1)

    in_specs = [pl.BlockSpec((L, CT), lambda j: (0, lo(j))), pl.BlockSpec((L, CT), lambda j: (0, lo(j) + NCT)),
                pl.BlockSpec((3, CT), lambda j: (0, lo(j))), pl.BlockSpec((3, CT), lambda j: (0, lo(j) + NCT)),
                pl.BlockSpec((None, 1, CT), lambda j: (layer, 0, lo(j))), pl.BlockSpec((None, 1, CT), lambda j: (layer, 0, lo(j) + NCT)),
                pl.BlockSpec((L, CT), lambda j: (0, lo(j)))]
    return pcall(
        body, plan, grid=(2 * NCT,), in_specs=in_specs,
        out_specs=[pl.BlockSpec((L, CT), lambda j: (0, j)), pl.BlockSpec((3, CT), lambda j: (0, j)), pl.BlockSpec((1, CT), lambda j: (0, j))],
        out_shape=[S((L, 2 * DFF), BF), S((3, 2 * DFF), f32), S((1, 2 * DFF), f32)],
        scratch_shapes=[pltpu.VMEM((NCT, L, CT), BF), pltpu.VMEM((NCT, 3, CT), f32), pltpu.VMEM((NCT, 1, CT), f32),
                        pltpu.VMEM((L + 8, CT), f32), pltpu.VMEM((L + 8, CT), f32)],
        sem=("arbitrary",), name=f"convact_bwd{layer}", args=[hu, hu, cw, cw, cb, cb, dact])


DILS = (1, 4, 16)
AB = 128


def _rope_tables(pos_ref, invf_ref):
    ang = pos_ref[...].astype(f32) * invf_ref[...]
    lane = lax.broadcasted_iota(jnp.int32, (1, 128), 1) % 64
    cosf = jnp.where(lane < 16, jnp.cos(ang), 1.0)
    sn = jnp.sin(ang)
    s_lo = jnp.where(lane < 8, -sn, 0.0)
    s_hi = jnp.where((lane >= 8) & (lane < 16), sn, 0.0)
    return cosf, s_lo, s_hi


def _rope(t, cosf, s_lo, s_hi):
    return t * cosf + pltpu.roll(t, 120, 1) * s_lo + pltpu.roll(t, 8, 1) * s_hi


def _rope_t(g, cosf, s_lo, s_hi):
    return g * cosf + pltpu.roll(g * s_lo, 8, 1) + pltpu.roll(g * s_hi, 120, 1)


def _att_block(q2, kp, kc, vp, vc, first):
    lane = lax.broadcasted_iota(jnp.int32, (1, 128), 1)
    qi = lax.broadcasted_iota(jnp.int32, (AB, 2 * AB), 0) + AB
    kj = lax.broadcasted_iota(jnp.int32, (AB, 2 * AB), 1)
    back = qi - kj
    valid = (back >= 0) & (back <= AB)
    if first:
        valid = valid & (kj >= AB)
    kk = jnp.concatenate([kp, kc], axis=0)
    vv = jnp.concatenate([vp, vc], axis=0)
    o2 = jnp.zeros((AB, 128), f32)
    lse2 = jnp.zeros((AB, 128), f32)
    for e in range(2):
        hm = ((lane >= 64 * e) & (lane < 64 * (e + 1))).astype(f32)
        s = dot_nt(q2 * (hm * 0.125), kk)
        s = jnp.where(valid, s, -jnp.inf)
        m = jnp.max(s, axis=-1, keepdims=True)
        p = jnp.exp(s - m)
        den = jnp.sum(p, axis=-1, keepdims=True)
        o2 = o2 + dot_nn(p, vv * hm) / den
        lse2 = lse2 + (m + jnp.log(den)) * hm
    return o2, lse2


def _att_blocks(dil):
    m = L // dil
    return [(r * m + n * AB, n == 0) for r in range(dil) for n in range(m // AB)]


def deinterleave(x, dil):
    return x if dil == 1 else x.reshape(L // dil, dil, x.shape[1]).swapaxes(0, 1).reshape(L, x.shape[1])


def attn_fwd(qkv, pos, invf, g, plan=None):
    blocks = _att_blocks(DILS[g])

    def body(q_ref, k_ref, v_ref, pos_ref, invf_ref, o_ref, l_ref, qr, kr):
        cosf, s_lo, s_hi = _rope_tables(pos_ref, invf_ref)
        qr[...] = _rope(q_ref[...], cosf, s_lo, s_hi)
        kr[...] = _rope(k_ref[...], cosf, s_lo, s_hi)
        for off, first in blocks:
            cur, prv = pl.ds(off, AB), pl.ds(off if first else off - AB, AB)
            o2, lse2 = _att_block(qr[cur, :], kr[prv, :], kr[cur, :], v_ref[prv, :], v_ref[cur, :], first)
            o_ref[cur, :] = o2
            l_ref[cur, :] = lse2

    def sec(n):
        return pl.BlockSpec((L, 128), lambda p: (0, p + 4 * n))

    return pcall(
        body, plan, grid=(4,),
        in_specs=[sec(0), sec(1), sec(2), pl.BlockSpec((L, 1), lambda p: (0, 0)), pl.BlockSpec((1, 128), lambda p: (0, 0))],
        out_specs=[sec(0), sec(0)], out_shape=[S((L, 512), f32), S((L, 512), f32)],
        scratch_shapes=[pltpu.VMEM((L, 128), f32), pltpu.VMEM((L, 128), f32)],
        sem=("parallel",), name=f"attn_fwd{g}", args=[qkv, qkv, qkv, pos, invf])


def _att_block_bwd(q2, kp, kc, vp, vc, lse2, do2, dl2, first):
    lane = lax.broadcasted_iota(jnp.int32, (1, 128), 1)
    qi = lax.broadcasted_iota(jnp.int32, (AB, 2 * AB), 0) + AB
    kj = lax.broadcasted_iota(jnp.int32, (AB, 2 * AB), 1)
    back = qi - kj
    valid = (back >= 0) & (back <= AB)
    if first:
        valid = valid & (kj >= AB)
    kk = jnp.concatenate([kp, kc], axis=0)
    vv = jnp.concatenate([vp, vc], axis=0)
    dq2 = jnp.zeros((AB, 128), f32)
    dkk = jnp.zeros((2 * AB, 128), f32)
    dvv = jnp.zeros((2 * AB, 128), f32)
    for e in range(2):
        hb = (lane >= 64 * e) & (lane < 64 * (e + 1))
        hm = hb.astype(f32)
        qs = q2 * (hm * 0.125)
        lse = jnp.max(jnp.where(hb, lse2, -jnp.inf), axis=-1, keepdims=True)
        dls = jnp.sum(dl2 * hm, axis=-1, keepdims=True)
        p = jnp.where(valid, jnp.exp(dot_nt(qs, kk) - lse), 0.0)
        dov = do2 * hm
        dp = dot_nt(dov, vv)
        ds = p * (dp - jnp.sum(p * dp, axis=-1, keepdims=True) + dls)
        dq2 = dq2 + dot_nn(ds, kk) * (hm * 0.125)
        dkk = dkk + dot_tn(ds, qs)
        dvv = dvv + dot_tn(p, dov)
    return dq2, dkk[:AB], dkk[AB:], dvv[:AB], dvv[AB:]


def attn_bwd(qkv, pos, invf, lse, do, dl, g, plan=None):
    blocks = _att_blocks(DILS[g])

    def body(q_ref, k_ref, v_ref, pos_ref, invf_ref, l_ref, do_ref, dl_ref, d_ref, qr, kr, dqr, dkr, dvr):
        cosf, s_lo, s_hi = _rope_tables(pos_ref, invf_ref)
        qr[...] = _rope(q_ref[...], cosf, s_lo, s_hi)
        kr[...] = _rope(k_ref[...], cosf, s_lo, s_hi)
        for off, first in blocks:
            cur, prv = pl.ds(off, AB), pl.ds(off if first else off - AB, AB)
            dq2, dkp, dkc, dvp, dvc = _att_block_bwd(qr[cur, :], kr[prv, :], kr[cur, :], v_ref[prv, :], v_ref[cur, :],
                                                     l_ref[cur, :], do_ref[cur, :], dl_ref[cur, :], first)
            dqr[cur, :] = dq2
            dkr[cur, :] = dkc
            dvr[cur, :] = dvc
            if not first:
                dkr[prv, :] += dkp
                dvr[prv, :] += dvp
        d_ref[0] = _rope_t(dqr[...], cosf, s_lo, s_hi).astype(d_ref.dtype)
        d_ref[1] = _rope_t(dkr[...], cosf, s_lo, s_hi).astype(d_ref.dtype)
        d_ref[2] = dvr[...].astype(d_ref.dtype)

    def sec(n):
        return pl.BlockSpec((L, 128), lambda p: (0, p + 4 * n))

    return pcall(
        body, plan, grid=(4,),
        in_specs=[sec(0), sec(1), sec(2), pl.BlockSpec((L, 1), lambda p: (0, 0)), pl.BlockSpec((1, 128), lambda p: (0, 0)),
                  sec(0), sec(0), sec(0)],
        out_specs=pl.BlockSpec((3, L, 128), lambda p: (0, 0, p)), out_shape=S((3, L, 512), BF),
        scratch_shapes=[pltpu.VMEM((L, 128), f32)] * 5,
        sem=("parallel",), name=f"attn_bwd{g}", args=[qkv, qkv, qkv, pos, invf, lse, do, dl])


def _merge(o0, o1, o2, l0, l1, l2):
    m = jnp.maximum(jnp.maximum(l0, l1), l2)
    e0, e1, e2 = jnp.exp(l0 - m), jnp.exp(l1 - m), jnp.exp(l2 - m)
    return (e0 * o0 + e1 * o1 + e2 * o2) / (e0 + e1 + e2)


def _to_token_major(src_ref, scr, i, dil, slab):
    n = TR // dil
    for r in range(dil):
        rows = pl.ds(pl.multiple_of(r * (L // dil) + i * n, n), n)
        scr[pl.ds(r, n, stride=dil), :] = src_ref[rows, slab * 128:(slab + 1) * 128].astype(f32)
    return scr[...]


def _to_class_major(val, dst_ref, scr, i, dil, slab):
    n = TR // dil
    scr[...] = val
    for r in range(dil):
        rows = pl.ds(pl.multiple_of(r * (L // dil) + i * n, n), n)
        dst_ref[rows, slab * 128:(slab + 1) * 128] = scr[pl.ds(r, n, stride=dil), :].astype(dst_ref.dtype)


def rms_fwd_classes(x, g, name):
    def body(x_ref, g_ref, o_ref, o1_ref, o2_ref, scr):
        i = pl.program_id(0)
        y = _rms(x_ref[...], g_ref[...])
        o_ref[...] = y.astype(o_ref.dtype)
        for s in range(D // 128):
            ys = y[:, s * 128:(s + 1) * 128]
            _to_class_major(ys, o1_ref, scr, i, DILS[1], s)
            _to_class_major(ys, o2_ref, scr, i, DILS[2], s)

    row = pl.BlockSpec((TR, D), lambda i: (i, 0))
    full = pl.BlockSpec((L, D), lambda i: (0, 0))
    return pl.pallas_call(
        body, grid=(L // TR,), in_specs=[row, pl.BlockSpec((1, D), lambda i: (0, 0))], out_specs=[row, full, full],
        out_shape=[S((L, D), BF)] * 3, scratch_shapes=[pltpu.VMEM((TR, 128), f32)],
        compiler_params=_cp(("arbitrary",)), name=name)(x, g)


def rms_bwd_classes(x, g, dy0, dyc, dres, name, plan=None):
    def body(x_ref, g_ref, dy0_ref, d1_ref, d2_ref, dr_ref, dh_ref, dg_ref, scr, dyf):
        i = pl.program_id(0)
        for s in range(D // 128):
            sl = slice(s * 128, (s + 1) * 128)
            dyf[:, sl] = (dy0_ref[:, sl] + _to_token_major(d1_ref, scr.at[0], i, DILS[1], s)
                          + _to_token_major(d2_ref, scr.at[1], i, DILS[2], s))
        _, vjp = jax.vjp(_rms, x_ref[...], g_ref[...])
        dx, dg = vjp(dyf[...])
        dh_ref[...] = dr_ref[...] + dx

        @pl.when(i == 0)
        def _():
            dg_ref[...] = jnp.zeros_like(dg_ref)

        dg_ref[...] += dg

    row = pl.BlockSpec((TR, D), lambda i: (i, 0))
    vec = pl.BlockSpec((1, D), lambda i: (0, 0))
    full = pl.BlockSpec((L, D), lambda i: (0, 0))
    return pcall(body, plan, grid=(L // TR,), in_specs=[row, vec, row, full, full, row], out_specs=[row, vec],
                 out_shape=[S((L, D), f32), S((1, D), f32)],
                 scratch_shapes=[pltpu.VMEM((2, TR, 128), f32), pltpu.VMEM((TR, D), f32)],
                 sem=("arbitrary",), name=name, args=[x, g, dy0, dyc[0], dyc[1], dres])


def attn_merge_fwd(o0, l0, oc, lc, plan=None):
    def body(o0_ref, l0_ref, o1_ref, l1_ref, o2_ref, l2_ref, o_ref, scr):
        i = pl.program_id(0)
        for s in range(4):
            sl = slice(s * 128, (s + 1) * 128)
            o1 = _to_token_major(o1_ref, scr.at[0], i, DILS[1], s)
            l1 = _to_token_major(l1_ref, scr.at[1], i, DILS[1], s)
            o2 = _to_token_major(o2_ref, scr.at[2], i, DILS[2], s)
            l2 = _to_token_major(l2_ref, scr.at[3], i, DILS[2], s)
            o_ref[:, sl] = _merge(o0_ref[:, sl], o1, o2, l0_ref[:, sl], l1, l2).astype(o_ref.dtype)

    blk = pl.BlockSpec((TR, 512), lambda i: (i, 0))
    full = pl.BlockSpec((L, 512), lambda i: (0, 0))
    return pcall(body, plan, grid=(L // TR,), in_specs=[blk, blk, full, full, full, full], out_specs=blk,
                 out_shape=S((L, 512), BF), scratch_shapes=[pltpu.VMEM((4, TR, 128), f32)],
                 sem=("arbitrary",), name="attn_merge_fwd", args=[o0, l0, oc[0], lc[0], oc[1], lc[1]])


def attn_merge_bwd(o0, l0, oc, lc, do, plan=None):
    def body(o0_ref, l0_ref, o1_ref, l1_ref, o2_ref, l2_ref, g_ref, do0, dl0, do1, dl1, do2, dl2, scr):
        i = pl.program_id(0)
        for s in range(4):
            sl = slice(s * 128, (s + 1) * 128)
            o1 = _to_token_major(o1_ref, scr.at[0], i, DILS[1], s)
            l1 = _to_token_major(l1_ref, scr.at[1], i, DILS[1], s)
            o2 = _to_token_major(o2_ref, scr.at[2], i, DILS[2], s)
            l2 = _to_token_major(l2_ref, scr.at[3], i, DILS[2], s)
            _, vjp = jax.vjp(_merge, o0_ref[:, sl], o1, o2, l0_ref[:, sl], l1, l2)
            g0, g1, g2, h0, h1, h2 = vjp(g_ref[:, sl].astype(f32))
            do0[:, sl] = g0.astype(do0.dtype)
            dl0[:, sl] = h0
            _to_class_major(g1, do1, scr.at[0], i, DILS[1], s)
            _to_class_major(h1, dl1, scr.at[1], i, DILS[1], s)
            _to_class_major(g2, do2, scr.at[2], i, DILS[2], s)
            _to_class_major(h2, dl2, scr.at[3], i, DILS[2], s)

    blk = pl.BlockSpec((TR, 512), lambda i: (i, 0))
    full = pl.BlockSpec((L, 512), lambda i: (0, 0))
    outs = pcall(body, plan, grid=(L // TR,), in_specs=[blk, blk, full, full, full, full, blk],
                 out_specs=[blk, blk, full, full, full, full],
                 out_shape=[S((L, 512), BF), S((L, 512), f32)] * 3, scratch_shapes=[pltpu.VMEM((4, TR, 128), f32)],
                 sem=("arbitrary",), name="attn_merge_bwd", args=[o0, l0, oc[0], lc[0], oc[1], lc[1], do])
    return [outs[0], outs[2], outs[4]], [outs[1], outs[3], outs[5]]


def _invf_lanes():
    half = 8
    inv = ROPE_THETA ** (-np.arange(half, dtype=np.float32) * 2.0 / 16.0)
    lane = np.arange(128) % 64
    return jnp.asarray(np.where(lane < 16, inv[lane % 8], 0.0).astype(np.float32)[None, :])


def hosted(C, host, fn):
    p = C.plan(host) if C is not None else None
    out = fn(p)
    if p is not None:
        C.done(p)
    return out


def _ffn_fwd(h, g_row, W, cb, layer, C):
    hn = rms_fwd(h, g_row, f"rms_ffn{layer}")
    hu = hosted(C, f"ffn_in{layer}", lambda p: matmul(hn, W[("ffn_w_in", layer)], mode="nn", tm=1024, tn=1408, tk=1024,
                                                      plan=p, name=f"ffn_in{layer}"))
    act = hosted(C, f"convact_fwd{layer}", lambda p: convact_fwd(hu, W[("ffn_conv_w", layer)], cb, layer, plan=p))
    h2 = hosted(C, f"ffn_out{layer}", lambda p: matmul(act, W[("ffn_w_out", layer)], mode="nn", tm=1024, tn=1024, tk=2816,
                                                       add=h, plan=p, name=f"ffn_out{layer}"))
    return h2, (hn, hu, act)


def _ffn_bwd(dh, h, g_row, W, cb, saved, layer, C, G):
    hn, hu, act = saved
    w_in, w_out = W[("ffn_w_in", layer)], W[("ffn_w_out", layer)]
    dact = hosted(C, f"ffn_out_dx{layer}", lambda p: matmul(dh, w_out, mode="nt", tm=1024, tn=1408, tk=1024, out_dtype=BF,
                                                          plan=p, name=f"ffn_out_dx{layer}"))
    G[("ffn_w_out", layer)] = hosted(C, f"ffn_out_dw{layer}", lambda p: matmul(
        act, dh, mode="tn", tm=1408, tn=1024, tk=L, out_dtype=BF, plan=p, name=f"ffn_out_dw{layer}"))
    dhu, G[("ffn_conv_w", layer)], g_cb = hosted(
        C, f"convact_bwd{layer}", lambda p: convact_bwd(hu, W[("ffn_conv_w", layer)], cb, dact, layer, plan=p))
    dhn = hosted(C, f"ffn_in_dx{layer}", lambda p: matmul(dhu, w_in, mode="nt", tm=1024, tn=1024, tk=2816, plan=p,
                                                         name=f"ffn_in_dx{layer}"))
    G[("ffn_w_in", layer)] = hosted(C, f"ffn_in_dw{layer}", lambda p: matmul(
        hn, dhu, mode="tn", tm=1024, tn=1408, tk=L, out_dtype=BF, plan=p, name=f"ffn_in_dw{layer}"))
    dh2, g_norm = hosted(C, f"rms_ffn_bwd{layer}", lambda p: rms_bwd(h, g_row, [dhn], dh, f"rms_ffn_bwd{layer}", plan=p))
    return dh2, g_cb, g_norm


def local_step(x, pos, tgt, sm, W, C=None):
    G = C.grads if C is not None else {}
    nm, nf = sm["norm_mix"], sm["norm_ffn"]
    invf = _invf_lanes()
    are = sm["s5_A_re"].reshape(NST, 1)
    aim = sm["s5_A_im"].reshape(NST, 1)
    ldt = sm["s5_log_dt"].reshape(1, 32)
    bre = sm["s5_B_re"].reshape(NST, 16)
    bim = sm["s5_B_im"].reshape(NST, 16)
    cre = jnp.swapaxes(sm["s5_C_re"][0], 1, 2).reshape(NST, 16)
    cim = jnp.swapaxes(sm["s5_C_im"][0], 1, 2).reshape(NST, 16)
    drow = sm["s5_D"].reshape(1, S5W)
    wbr, wbi, wcr, wci, abr, abi = s5_params_fwd(are, aim, ldt, bre, bim, cre, cim)
    hn0 = rms_fwd(x, nm[0:1], "rms_mix0")
    cb3 = sm["ffn_conv_b3"]
    proj = hosted(C, "mix_in", lambda p: matmul(hn0, W[("mix_w_in", 0)], mode="nn", tm=1024, tn=1280, tk=1024, plan=p, name="mix_in"))
    xs_re, xs_im, y5 = hosted(C, "s5_scan_fwd", lambda p: s5_scan_fwd(proj, wbr, wbi, wcr, wci, abr, abi, drow, plan=p))
    oa = s5_glu_fwd(y5, W[("s5_glu_w", 0)], sm["s5_glu_b"])
    ob, ssave = hosted(C, "hgrn_fwd", lambda p: hgrn_fwd(proj, sm["hgrn_gamma"], sm["hgrn_norm"], plan=p))
    cat = jnp.concatenate([oa, ob], axis=1)
    h1 = matmul(cat, W[("mix_w_out", 0)], mode="nn", tm=1024, tn=1024, tk=1024, add=x, name="mix_out")
    h2, ffn0 = _ffn_fwd(h1, nf[0:1], W, cb3, 0, C)
    hn2_g = rms_fwd_classes(h2, nm[1:2], "rms_mix1")
    wqkv = W[("att_w_qkv", 0)]
    pos_g, qkv_g, oc_g, lc_g = [], [], [], []
    for g, dil in enumerate(DILS):
        pos_g.append(deinterleave(pos, dil))
        qkv_g.append(hosted(C, f"att_qkv{g}", lambda p: matmul(
            hn2_g[g], wqkv, mode="nn", tm=1024, tn=512, tk=1024, dims=(L, 1536, D),
            b_spec=pl.BlockSpec((D, 512), lambda i, j, k, g=g: (0, 3 * j + g)), plan=p, name=f"att_qkv{g}")))
        o_c, l_c = hosted(C, f"attn_fwd{g}", lambda p: attn_fwd(qkv_g[g], pos_g[g], invf, g, plan=p))
        oc_g.append(o_c)
        lc_g.append(l_c)
    o = hosted(C, "attn_merge_fwd", lambda p: attn_merge_fwd(oc_g[0], lc_g[0], oc_g[1:], lc_g[1:], plan=p))
    h3 = matmul(o, W[("att_w_o", 0)], mode="nn", tm=1024, tn=1024, tk=512, add=h2, name="att_o")
    h4, ffn1 = _ffn_fwd(h3, nf[1:2], W, cb3, 1, C)
    loss, dh, g_nfinal = loss_head(h4, sm["norm_final"].reshape(1, D), tgt)
    dh, g_cb1, g_nf1 = _ffn_bwd(dh, h3, nf[1:2], W, cb3, ffn1, 1, C, G)
    do = matmul(dh, W[("att_w_o", 0)], mode="nt", tm=1024, tn=512, tk=1024, name="att_o_dx")
    G[("att_w_o", 0)] = matmul(o, dh, mode="tn", tm=512, tn=1024, tk=L, out_dtype=BF, name="att_o_dw")
    do_g, dl_g = hosted(C, "attn_merge_bwd", lambda p: attn_merge_bwd(oc_g[0], lc_g[0], oc_g[1:], lc_g[1:], do, plan=p))
    dhn2_g, gq = [], []
    for g, dil in enumerate(DILS):
        d3 = hosted(C, f"attn_bwd{g}", lambda p: attn_bwd(qkv_g[g], pos_g[g], invf, lc_g[g], do_g[g], dl_g[g], g, plan=p))
        dx = matmul(d3, wqkv, mode="nt", tm=1024, tn=1024, tk=512, dims=(L, D, 1536),
                    a_spec=pl.BlockSpec((None, 1024, 512), lambda i, j, k: (k, i, 0)),
                    b_spec=pl.BlockSpec((D, 512), lambda i, j, k, g=g: (0, 3 * k + g)), name=f"att_qkv_dx{g}")
        dhn2_g.append(dx)
        gq.append(matmul(hn2_g[g], d3, mode="tn", tm=1024, tn=512, tk=L, out_dtype=BF, dims=(D, 1536, L),
                         b_spec=pl.BlockSpec((None, L, 512), lambda i, j, k: (j, k, 0)), name=f"att_qkv_dw{g}"))
    G[("att_w_qkv", 0)] = jnp.concatenate([gq[g][:, 512 * s:512 * (s + 1)] for s in range(3) for g in range(3)], axis=1)
    dh, g_nm1 = hosted(C, "rms_mix_bwd1", lambda p: rms_bwd_classes(h2, nm[1:2], dhn2_g[0], dhn2_g[1:], dh, "rms_mix_bwd1", plan=p))
    dh, g_cb0, g_nf0 = _ffn_bwd(dh, h1, nf[0:1], W, cb3, ffn0, 0, C, G)
    dmix = matmul(dh, W[("mix_w_out", 0)], mode="nt", tm=1024, tn=1024, tk=1024, name="mix_out_dx")
    G[("mix_w_out", 0)] = matmul(cat, dh, mode="tn", tm=1024, tn=1024, tk=L, out_dtype=BF, name="mix_out_dw")
    dy5, g_glu_w, g_glu_b = s5_glu_bwd(y5, W[("s5_glu_w", 0)], sm["s5_glu_b"], dmix)
    G[("s5_glu_w", 0)] = g_glu_w.astype(BF)
    du, gwbr, gwbi, gwcr, gwci, gabr, gabi, g_d = hosted(C, "s5_scan_bwd", lambda p: s5_scan_bwd(
        dy5, proj, xs_re, xs_im, wbr, wbi, wcr, wci, abr, abi, drow, plan=p))
    g_are, g_aim, g_ldt, g_bre, g_bim, g_cre, g_cim = s5_params_bwd(are, aim, ldt, bre, bim, cre, cim,
                                                                   (gwbr, gwbi, gwcr, gwci, gabr, gabi))
    small = {
        "norm_ffn": jnp.concatenate([g_nf0, g_nf1], axis=0), "norm_final": g_nfinal.reshape(D),
        "s5_A_re": g_are.reshape(1, 32, 64), "s5_A_im": g_aim.reshape(1, 32, 64), "s5_log_dt": g_ldt.reshape(1, 32),
        "s5_B_re": g_bre.reshape(1, 32, 64, 16), "s5_B_im": g_bim.reshape(1, 32, 64, 16),
        "s5_C_re": jnp.swapaxes(g_cre.reshape(1, 32, 64, 16), 2, 3), "s5_C_im": jnp.swapaxes(g_cim.reshape(1, 32, 64, 16), 2, 3),
        "s5_D": g_d.reshape(1, 32, 16), "s5_glu_b": g_glu_b, "ffn_conv_b": jnp.concatenate([g_cb0, g_cb1], axis=0),
    }
    if C is not None:
        C.small["small_early"] = _pack(small, SMALL_EARLY)
    dproj, g_gamma, g_hnorm = hosted(C, "hgrn_bwd", lambda p: hgrn_bwd(proj, sm["hgrn_gamma"], sm["hgrn_norm"], ssave, dmix, du,
                                                                       plan=p))
    dhn0 = hosted(C, "mix_in_dx", lambda p: matmul(dproj, W[("mix_w_in", 0)], mode="nt", tm=1024, tn=1024, tk=2560, plan=p,
                                                  name="mix_in_dx"))
    G[("mix_w_in", 0)] = matmul(hn0, dproj, mode="tn", tm=1024, tn=1280, tk=L, out_dtype=BF, name="mix_in_dw")
    gx, g_nm0 = hosted(C, "rms_mix_bwd0", lambda p: rms_bwd(x, nm[0:1], [dhn0], dh, "rms_mix_bwd0", plan=p))
    small.update({"norm_mix": jnp.concatenate([g_nm0, g_nm1], axis=0), "hgrn_gamma": g_gamma, "hgrn_norm": g_hnorm})
    if C is not None:
        C.small["small_late"] = _pack(small, SMALL_LATE)
    return loss, gx, G, small


BIG = ("mix_w_in", "mix_w_out", "s5_glu_w", "att_w_qkv", "att_w_o", "ffn_w_in", "ffn_w_out", "ffn_conv_w")
SMALL = ("norm_mix", "norm_ffn", "norm_final", "s5_A_re", "s5_A_im", "s5_log_dt", "s5_B_re", "s5_B_im", "s5_C_re", "s5_C_im",
         "s5_D", "s5_glu_b", "hgrn_gamma", "hgrn_norm", "ffn_conv_b")
SMALL_LATE = ("norm_mix", "hgrn_gamma", "hgrn_norm")
SMALL_EARLY = tuple(n for n in SMALL if n not in SMALL_LATE)


def cast_bf16(w, name, plan=None):
    nl, r, c = w.shape
    w2 = w.reshape(nl * r, c)
    tr = 256 if (nl * r) % 256 == 0 else nl * r

    def body(w_ref, o_ref):
        o_ref[...] = w_ref[...].astype(BF)

    out = pcall(body, plan, grid=(nl * r // tr,), in_specs=[pl.BlockSpec((tr, c), lambda i: (i, 0))],
                out_specs=pl.BlockSpec((tr, c), lambda i: (i, 0)), out_shape=S((nl * r, c), BF),
                sem=("parallel",), name=name, args=[w2])
    return out.reshape(nl, r, c)


SCHEDULE = {
    "cast_ffn_w_in": [("G", "mix_w_in", 0)],
    "mix_in": [("G", "mix_w_out", 0), ("G", "s5_glu_w", 0)],
    "s5_scan_fwd": [("G", "ffn_w_in", 0, (0, 2))],
    "hgrn_fwd": [("G", "ffn_w_in", 0, (1, 2)), ("G", "ffn_conv_w", 0), ("G", "ffn_conv_w", 1), ("G", "att_w_qkv", 0, (0, 2))],
    "ffn_in0": [("G", "ffn_w_out", 0)],
    "convact_fwd0": [("G", "att_w_qkv", 0, (1, 2))],
    "att_qkv0": [("G", "att_w_o", 0)],
    "attn_fwd0": [("G", "ffn_w_in", 1, (0, 2))],
    "attn_fwd1": [("G", "ffn_w_in", 1, (1, 2))],
    "attn_fwd2": [("G", "ffn_w_out", 1)],
    "convact_bwd1": [("P", "ffn_w_out", 1)],
    "ffn_in_dx1": [("A", "ffn_w_out", 1, (0, 2))],
    "ffn_in_dw1": [("A", "ffn_w_out", 1, (1, 2))],
    "rms_ffn_bwd1": [("P", "ffn_w_in", 1)],
    "attn_merge_bwd": [("P", "att_w_o", 0), ("A", "ffn_conv_w", 1), ("B", "ffn_w_out", 1)],
    "attn_bwd0": [("A", "ffn_w_in", 1, (0, 2)), ("A", "att_w_o", 0)],
    "attn_bwd1": [("A", "ffn_w_in", 1, (1, 2)), ("B", "att_w_o", 0), ("B", "ffn_conv_w", 1)],
    "attn_bwd2": [("B", "ffn_w_in", 1)],
    "rms_mix_bwd1": [("P", "att_w_qkv", 0)],
    "ffn_out_dx0": [("A", "att_w_qkv", 0, (0, 4))],
    "ffn_out_dw0": [("A", "att_w_qkv", 0, (1, 4))],
    "convact_bwd0": [("A", "att_w_qkv", 0, (2, 4)), ("A", "att_w_qkv", 0, (3, 4)), ("P", "ffn_w_out", 0)],
    "ffn_in_dx0": [("A", "ffn_w_out", 0, (0, 2)), ("B", "att_w_qkv", 0)],
    "ffn_in_dw0": [("A", "ffn_w_out", 0, (1, 2))],
    "rms_ffn_bwd0": [("P", "ffn_w_in", 0), ("B", "ffn_w_out", 0)],
    "s5_scan_bwd": [("A", "ffn_w_in", 0, (0, 2)), ("P", "mix_w_out", 0), ("P", "s5_glu_w", 0), ("A", "ffn_conv_w", 0)],
    "hgrn_bwd": [("A", "ffn_w_in", 0, (1, 2)), ("A", "mix_w_out", 0), ("A", "s5_glu_w", 0), ("B", "ffn_conv_w", 0),
                 ("A", "small_early", 0)],
    "mix_in_dx": [("B", "ffn_w_in", 0), ("B", "mix_w_out", 0), ("B", "s5_glu_w", 0), ("B", "small_early", 0)],
    "rms_mix_bwd0": [("P", "mix_w_in", 0)],
    "adam_att_w_o": [("A", "mix_w_in", 0), ("A", "small_late", 0)],
    "adam_s5_glu_w": [("B", "mix_w_in", 0), ("B", "small_late", 0)],
}


class Comm:
    def __init__(self, shards, shapes):
        self.shards, self.shapes = shards, shapes
        self.W, self.grads, self.slots = {}, {}, {}
        self.sib, self.pair = {}, {}
        self.small = {}

    def plan(self, host):
        items = SCHEDULE.get(host)
        if not items:
            return None
        p = Plan()
        for it in items:
            kind, name, l = it[:3]
            part, parts = it[3] if len(it) > 3 else (0, 1)
            if name.startswith("small"):
                sg = self.small[name]
                kdst = p.buf("slots:" + name, arr=self.slots.get(name), shape=S((8,) + sg.shape, f32), write=True)
                if kind == "A":
                    ReduceOp(p, p.buf("g:" + name, arr=sg), kdst, None, sg.shape, False, 0, 0, whole=True)
                else:
                    ForwardOp(p, kdst, None, whole=True)
                continue
            nl, R, C_ = self.shapes[name]
            rows = name in ROW_SHARDED
            r0, nr = part * (R // parts), R // parts
            if kind == "G":
                sh = self.shards[name]
                kdst = p.buf(f"W:{name}:{l}", arr=self.W.get((name, l)), shape=S((4 * R, C_) if rows else (R, 4 * C_), sh.dtype),
                             write=True)
                GatherOp(p, p.buf("shard:" + name, arr=sh), kdst, l, self.shapes[name], rows, r0, nr, split=(nr % 32 == 0))
            elif name == "ffn_conv_w":
                g = self.grads[(name, l)]
                kdst = p.buf("slots:" + name, arr=self.slots.get(name), shape=S((8, nl, R, C_), g.dtype), write=True)
                if kind == "A":
                    ReduceOp(p, p.buf(f"g:{name}:{l}", arr=g), kdst, l, self.shapes[name], rows, r0, nr)
                else:
                    ForwardOp(p, kdst, l)
            elif kind == "P":
                g = self.grads[(name, l)]
                ksib = p.buf(f"sib:{name}:{l}", shape=S((4 * R // 2, C_) if rows else (R // 2, 4 * C_), g.dtype), write=True)
                PairOp(p, p.buf(f"g:{name}:{l}", arr=g), ksib, self.shapes[name], rows)
            else:
                if (name, l) not in self.pair:
                    self.pair[(name, l)] = pair_sum(self.grads[(name, l)], self.sib[(name, l)], rows, R, f"pair_sum_{name}{l}")
                h = self.pair[(name, l)]
                kdst = p.buf("slots:" + name, arr=self.slots.get(name), shape=S((4, nl, R, C_), h.dtype), write=True)
                if kind == "A":
                    ReduceOp(p, p.buf(f"h:{name}:{l}", arr=h), kdst, l, self.shapes[name], rows, r0 // 2, nr // 2, half=True)
                else:
                    HalfForwardOp(p, kdst, l, self.shapes[name])
        return p

    def done(self, p):
        for k, arr in p.out.items():
            tag, name = k.split(":")[:2]
            if tag == "W":
                self.W[(name, int(k.split(":")[2]))] = arr
            elif tag == "sib":
                self.sib[(name, int(k.split(":")[2]))] = arr
            else:
                self.slots[name] = arr


def _adamw(w, g, m, v):
    m = B1 * m + (1.0 - B1) * g
    v = B2 * v + (1.0 - B2) * jnp.square(g)
    m_hat = m / (1.0 - B1 ** STEP)
    v_hat = v / (1.0 - B2 ** STEP)
    return -LR * (m_hat / (jnp.sqrt(v_hat) + AEPS) + WD * w), m, v


def adam_big(w, m, v, slots, name, plan=None):
    nl, R, C = w.shape
    ns = slots.shape[0]
    tr = 128 if R % 128 == 0 else (64 if R % 64 == 0 else R)

    def body(w_ref, m_ref, v_ref, s_ref, g_ref, d_ref, nm_ref, nv_ref):
        g = s_ref[0].astype(f32)
        for s in range(1, ns):
            g = g + s_ref[s].astype(f32)
        d, nm_, nv_ = _adamw(w_ref[...], g, m_ref[...], v_ref[...])
        g_ref[...] = g
        d_ref[...] = d
        nm_ref[...] = nm_
        nv_ref[...] = nv_

    blk = pl.BlockSpec((None, tr, C), lambda l, i: (l, i, 0))
    return pcall(body, plan, grid=(nl, R // tr),
                 in_specs=[blk, blk, blk, pl.BlockSpec((ns, None, tr, C), lambda l, i: (0, l, i, 0))],
                 out_specs=[blk] * 4, out_shape=[S((nl, R, C), f32)] * 4,
                 sem=("parallel", "parallel"), name=name, args=[w, m, v, slots])


def sum_slots(slots, name):
    R = slots.shape[1]

    def body(s_ref, g_ref):
        g = s_ref[0]
        for s in range(1, 8):
            g = g + s_ref[s]
        g_ref[...] = g

    return pl.pallas_call(
        body, grid=(R // 256,), in_specs=[pl.BlockSpec((8, 256, 128), lambda i: (0, i, 0))],
        out_specs=pl.BlockSpec((256, 128), lambda i: (i, 0)), out_shape=S((R, 128), f32),
        compiler_params=_cp(("parallel",)), name=name)(slots)


SMALL2D = {"norm_mix": (2, 1024), "norm_ffn": (2, 1024), "norm_final": (1, 1024), "s5_A_re": (32, 64), "s5_A_im": (32, 64),
           "s5_log_dt": (1, 32), "s5_B_re": (2048, 16), "s5_B_im": (2048, 16), "s5_C_re": (512, 64), "s5_C_im": (512, 64),
           "s5_D": (32, 16), "s5_glu_b": (1, 512), "hgrn_gamma": (2, 512), "hgrn_norm": (1, 512), "ffn_conv_b": (2, 5632)}


def adam_small(w, m, v, g, names, name):
    n = len(names)

    def body(*refs):
        for i in range(n):
            w_ref, m_ref, v_ref, g_ref = refs[4 * i:4 * i + 4]
            d_ref, nm_ref, nv_ref = refs[4 * n + 3 * i:4 * n + 3 * i + 3]
            d, nm_, nv_ = _adamw(w_ref[...], g_ref[...], m_ref[...], v_ref[...])
            d_ref[...] = d
            nm_ref[...] = nm_
            nv_ref[...] = nv_

    args = [t[k] for k in names for t in (w, m, v, g)]
    outs = pl.pallas_call(body, out_shape=[S(SMALL2D[k], f32) for k in names for _ in range(3)],
                          compiler_params=_cp(), name=name)(*args)
    return {k: tuple(outs[3 * i:3 * i + 3]) for i, k in enumerate(names)}


def _pack(d, names):
    flat = jnp.concatenate([d[n].reshape(-1) for n in names])
    n = flat.shape[0]
    rows = -(-n // (256 * 128)) * 256
    return jnp.pad(flat, (0, rows * 128 - n)).reshape(rows, 128)


def _unpack(p, like, names):
    flat = p.reshape(-1)
    out, off = {}, 0
    for n in names:
        sz = math.prod(like[n].shape)
        out[n] = flat[off:off + sz].reshape(like[n].shape)
        off += sz
    return out


def kernel(x, positions, norm_mix, norm_ffn, norm_final, mix_w_in, mix_w_out, s5_A_re, s5_A_im, s5_log_dt, s5_B_re, s5_B_im, s5_C_re, s5_C_im, s5_D, s5_glu_w, s5_glu_b, hgrn_gamma, hgrn_norm, att_w_qkv, att_w_o, ffn_w_in, ffn_conv_w, ffn_conv_b, ffn_w_out, loss_target, m_norm_mix, m_norm_ffn, m_norm_final, m_mix_w_in, m_mix_w_out, m_s5_A_re, m_s5_A_im, m_s5_log_dt, m_s5_B_re, m_s5_B_im, m_s5_C_re, m_s5_C_im, m_s5_D, m_s5_glu_w, m_s5_glu_b, m_hgrn_gamma, m_hgrn_norm, m_att_w_qkv, m_att_w_o, m_ffn_w_in, m_ffn_conv_w, m_ffn_conv_b, m_ffn_w_out, v_norm_mix, v_norm_ffn, v_norm_final, v_mix_w_in, v_mix_w_out, v_s5_A_re, v_s5_A_im, v_s5_log_dt, v_s5_B_re, v_s5_B_im, v_s5_C_re, v_s5_C_im, v_s5_D, v_s5_glu_w, v_s5_glu_b, v_hgrn_gamma, v_hgrn_norm, v_att_w_qkv, v_att_w_o, v_ffn_w_in, v_ffn_conv_w, v_ffn_conv_b, v_ffn_w_out):
    a = dict(locals())
    weights = BIG + SMALL
    w = {n: a[n] for n in weights}
    m = {n: a["m_" + n] for n in weights}
    v = {n: a["v_" + n] for n in weights}
    shards = {"ffn_conv_w": ffn_conv_w}
    C = Comm(shards, {n: w[n].shape for n in BIG})
    for n in ("mix_w_in", "ffn_w_in", "mix_w_out", "s5_glu_w", "ffn_w_out", "att_w_qkv", "att_w_o"):
        shards[n] = hosted(C, "cast_" + n, lambda p: cast_bf16(w[n], "cast_" + n, plan=p))
    sm = {n: w[n] for n in SMALL}
    sm["ffn_conv_b3"] = ffn_conv_b.reshape(2, 1, 2 * DFF)
    loss, gx, _, _ = local_step(x[0], positions.reshape(L, 1), loss_target[0], sm, C.W, C)
    res = {}
    for n in ("att_w_o", "s5_glu_w", "ffn_w_in", "ffn_w_out", "att_w_qkv", "mix_w_out", "ffn_conv_w", "mix_w_in"):
        res[n] = hosted(C, "adam_" + n, lambda p: adam_big(w[n], m[n], v[n], C.slots[n], "adam_" + n, plan=p))
    for names, key in ((SMALL_EARLY, "small_early"), (SMALL_LATE, "small_late")):
        g = _unpack(sum_slots(C.slots[key], "sum_" + key), w, names)

        def two_d(t):
            return {n: t[n].reshape(SMALL2D[n]) for n in names}

        upd = adam_small(two_d(w), two_d(m), two_d(v), two_d(g), names, "adam_" + key)
        for n in names:
            res[n] = (g[n],) + tuple(t.reshape(w[n].shape) for t in upd[n])
    total = lax.psum(loss[0, 0], ("x", "y", "c"))
    order = ("norm_mix", "norm_ffn", "norm_final", "mix_w_in", "mix_w_out", "s5_A_re", "s5_A_im", "s5_log_dt", "s5_B_re", "s5_B_im",
             "s5_C_re", "s5_C_im", "s5_D", "s5_glu_w", "s5_glu_b", "hgrn_gamma", "hgrn_norm", "att_w_qkv", "att_w_o", "ffn_w_in",
             "ffn_conv_w", "ffn_conv_b", "ffn_w_out")
    return (total, gx[None], *[res[n][0] for n in order], *[res[n][1] for n in order], *[res[n][2] for n in order],
            *[res[n][3] for n in order])
```

```python
import functools
import math

import numpy as np
import jax
import jax.numpy as jnp
from jax import lax
from jax.experimental import pallas as pl
from jax.experimental.pallas import tpu as pltpu

f32 = jnp.float32
BF = jnp.bfloat16
HI = lax.Precision.HIGHEST
S = jax.ShapeDtypeStruct
MESH = pl.DeviceIdType.MESH

L = 2048
D = 1024
EPS = 1e-6
S5W = 512
NST = 2048
HGC = 64
HGB = 32
DFF = 2816
ROPE_THETA = 500000.0
LR, B1, B2, AEPS, WD, STEP = 0.001, 0.9, 0.999, 1e-08, 0.01, 10
VMEM_LIMIT = 56 * 1024 * 1024


def _cp(sem=None):
    return pltpu.CompilerParams(dimension_semantics=sem, vmem_limit_bytes=VMEM_LIMIT)


ANY = pl.BlockSpec(memory_space=pl.ANY)
ROW_SHARDED = ("mix_w_out", "s5_glu_w", "ffn_w_out")


def _coords():
    x, y, c = lax.axis_index("x"), lax.axis_index("y"), lax.axis_index("c")
    return x, y, c, 2 * x + y, [(1 - x, y), (x, 1 - y), (1 - x, 1 - y)]


def _rows(start, n):
    return pl.ds(start if isinstance(start, int) else pl.multiple_of(start, 8), n)


def _cols(q, n):
    return pl.ds(pl.multiple_of(q * n, 128), n)


class Plan:
    def __init__(self):
        self.bufs, self.ops, self.nsem, self.out = {}, [], 0, {}

    def buf(self, key, arr=None, shape=None, write=False):
        b = self.bufs.setdefault(key, dict(arr=arr, shape=shape, write=False))
        b["write"] = b["write"] or write
        return key

    def add(self, op):
        op.base = self.nsem
        self.nsem += op.nsem
        self.ops.append(op)


class GatherOp:
    nsem = 13

    def __init__(self, plan, ksrc, kdst, l, shard_shape, rows, r0, nr, split):
        self.ksrc, self.kdst, self.l, (_, self.R, self.C), self.rows, self.r0, self.nr, self.split = (
            ksrc, kdst, l, shard_shape, rows, r0, nr, split)
        self.h = nr // 2 if split else nr
        plan.add(self)

    def _dst(self, R_, q, start, n):
        if self.rows:
            return R_[self.kdst].at[_rows(q * self.R + start, n), :]
        return R_[self.kdst].at[_rows(start, n), _cols(q, self.C)]

    def _mine(self, c):
        return self.r0 + (c * self.h if self.split else 0)

    def _theirs(self, c):
        return self.r0 + ((1 - c) * self.h if self.split else 0)

    def _copies(self, R_, sems):
        x, y, c, me, others = _coords()
        src = R_[self.ksrc]
        local = pltpu.make_async_copy(src.at[self.l, _rows(self.r0, self.nr), :], self._dst(R_, me, self.r0, self.nr),
                                      sems.at[self.base + 12])
        send, fwd = [], []
        for k, (px, py) in enumerate(others):
            q = 2 * px + py
            send.append((
                pltpu.make_async_remote_copy(src.at[self.l, _rows(self._mine(c), self.h), :], self._dst(R_, me, self._mine(c), self.h),
                                             sems.at[self.base + k], sems.at[self.base + 3 + k], device_id=(px, py, c), device_id_type=MESH),
                pltpu.make_async_remote_copy(src.at[self.l, _rows(self._mine(c), self.h), :], self._dst(R_, q, self._mine(c), self.h),
                                             sems.at[self.base + k], sems.at[self.base + 3 + k], device_id=(px, py, c), device_id_type=MESH)))
            fwd.append((
                pltpu.make_async_remote_copy(self._dst(R_, q, self._mine(c), self.h), self._dst(R_, q, self._mine(c), self.h),
                                             sems.at[self.base + 6 + k], sems.at[self.base + 9 + k], device_id=(x, y, 1 - c), device_id_type=MESH),
                pltpu.make_async_remote_copy(self._dst(R_, q, self._theirs(c), self.h), self._dst(R_, q, self._theirs(c), self.h),
                                             sems.at[self.base + 6 + k], sems.at[self.base + 9 + k], device_id=(x, y, 1 - c), device_id_type=MESH)))
        return local, send, fwd

    def start(self, R_, sems):
        local, send, _ = self._copies(R_, sems)
        local.start()
        for out, _ in send:
            out.start()

    def finish(self, R_, sems):
        local, send, fwd = self._copies(R_, sems)
        for k in range(3):
            send[k][1].wait_recv()
            if self.split:
                fwd[k][0].start()
        for k in range(3):
            if self.split:
                fwd[k][1].wait_recv()
                fwd[k][0].wait_send()
            send[k][0].wait_send()
        local.wait()


class ReduceOp:
    nsem = 7

    def __init__(self, plan, ksrc, kdst, l, shard_shape, rows, r0, nr, whole=False, half=False):
        self.ksrc, self.kdst, self.l, (self.R, self.C), self.rows, self.r0, self.nr, self.whole, self.half = (
            ksrc, kdst, l, shard_shape[-2:], rows, r0, nr, whole, half)
        plan.add(self)

    def _piece(self, R_, q):
        g = R_[self.ksrc]
        if self.whole:
            return g
        if self.rows:
            return g.at[_rows(q * (self.R // 2 if self.half else self.R) + self.r0, self.nr), :]
        return g.at[_rows(self.r0, self.nr), _cols(q, self.C)]

    def _slot(self, R_, q, c):
        if self.whole:
            return R_[self.kdst].at[2 * q + c]
        if self.half:
            return R_[self.kdst].at[q, self.l, _rows(c * (self.R // 2) + self.r0, self.nr), :]
        return R_[self.kdst].at[2 * q + c, self.l, _rows(self.r0, self.nr), :]

    def _copies(self, R_, sems):
        x, y, c, me, others = _coords()
        local = pltpu.make_async_copy(self._piece(R_, me), self._slot(R_, me, c), sems.at[self.base + 6])
        send = []
        for k, (px, py) in enumerate(others):
            q = 2 * px + py
            send.append((
                pltpu.make_async_remote_copy(self._piece(R_, q), self._slot(R_, me, c), sems.at[self.base + k],
                                             sems.at[self.base + 3 + k], device_id=(px, py, c), device_id_type=MESH),
                pltpu.make_async_remote_copy(self._piece(R_, q), self._slot(R_, q, c), sems.at[self.base + k],
                                             sems.at[self.base + 3 + k], device_id=(px, py, c), device_id_type=MESH)))
        return local, send

    def start(self, R_, sems):
        local, send = self._copies(R_, sems)
        local.start()
        for out, _ in send:
            out.start()

    def finish(self, R_, sems):
        local, send = self._copies(R_, sems)
        local.wait()
        for out, inn in send:
            inn.wait_recv()
            out.wait_send()


class ForwardOp:
    nsem = 8

    def __init__(self, plan, kdst, l, whole=False):
        self.kdst, self.l, self.whole = kdst, l, whole
        plan.add(self)

    def _slot(self, R_, s):
        return R_[self.kdst].at[s] if self.whole else R_[self.kdst].at[s, self.l]

    def _copies(self, R_, sems):
        x, y, c, me, others = _coords()
        return [(pltpu.make_async_remote_copy(self._slot(R_, 2 * q + c), self._slot(R_, 2 * q + c), sems.at[self.base + q],
                                              sems.at[self.base + 4 + q], device_id=(x, y, 1 - c), device_id_type=MESH),
                 pltpu.make_async_remote_copy(self._slot(R_, 2 * q + 1 - c), self._slot(R_, 2 * q + 1 - c), sems.at[self.base + q],
                                              sems.at[self.base + 4 + q], device_id=(x, y, 1 - c), device_id_type=MESH))
                for q in range(4)]

    def start(self, R_, sems):
        for out, _ in self._copies(R_, sems):
            out.start()

    def finish(self, R_, sems):
        for out, inn in self._copies(R_, sems):
            inn.wait_recv()
            out.wait_send()


class PairOp:
    nsem = 8

    def __init__(self, plan, ksrc, kdst, shard_shape, rows):
        self.ksrc, self.kdst, (self.R, self.C), self.rows = ksrc, kdst, shard_shape[-2:], rows
        plan.add(self)

    def _copies(self, R_, sems):
        x, y, c, me, others = _coords()
        g, dst, h = R_[self.ksrc], R_[self.kdst], self.R // 2
        out = []
        for q in range(4 if self.rows else 1):
            src = g.at[_rows(q * self.R + (1 - c) * h, h), :]
            land = dst.at[_rows(q * h, h), :]
            out.append(pltpu.make_async_remote_copy(src, land, sems.at[self.base + q], sems.at[self.base + 4 + q],
                                                    device_id=(x, y, 1 - c), device_id_type=MESH))
        return out

    def start(self, R_, sems):
        for cp in self._copies(R_, sems):
            cp.start()

    def finish(self, R_, sems):
        for cp in self._copies(R_, sems):
            cp.wait_recv()
            cp.wait_send()


class HalfForwardOp:
    nsem = 2

    def __init__(self, plan, kdst, l, shard_shape):
        self.kdst, self.l, self.R = kdst, l, shard_shape[-2]
        plan.add(self)

    def _copy(self, R_, sems, core):
        x, y, c, me, others = _coords()
        part = R_[self.kdst].at[:, self.l, _rows((c if core == "mine" else 1 - c) * (self.R // 2), self.R // 2), :]
        return pltpu.make_async_remote_copy(part, part, sems.at[self.base], sems.at[self.base + 1],
                                            device_id=(x, y, 1 - c), device_id_type=MESH)

    def start(self, R_, sems):
        self._copy(R_, sems, "mine").start()

    def finish(self, R_, sems):
        self._copy(R_, sems, "theirs").wait_recv()
        self._copy(R_, sems, "mine").wait_send()


def pair_sum(g, gsib, rows, R, name):
    h = R // 2
    W = g.shape[1]
    tr = h if h * W * 2 <= 2 ** 21 else 128
    nq = 4 if rows else 1

    def body(c_ref, a_ref, b_ref, o_ref):
        o_ref[...] = (a_ref[...].astype(f32) + b_ref[...].astype(f32)).astype(o_ref.dtype)

    half = pl.BlockSpec((tr, W), lambda q, i, c_ref: (q * (h // tr) + i, 0))
    mine = pl.BlockSpec((tr, W), lambda q, i, c_ref: (q * (R // tr) + c_ref[0] * (h // tr) + i, 0))
    return pl.pallas_call(
        body, grid_spec=pltpu.PrefetchScalarGridSpec(num_scalar_prefetch=1, grid=(nq, h // tr), in_specs=[mine, half],
                                                     out_specs=half),
        out_shape=S(gsib.shape, g.dtype), compiler_params=_cp(("parallel", "parallel")),
        name=name)(lax.axis_index("c").reshape(1).astype(jnp.int32), g, gsib)


def pcall(body, plan, *, grid, in_specs, out_specs, out_shape, scratch_shapes=(), sem, name, args):
    multi = isinstance(out_shape, (list, tuple))
    if plan is None or not plan.ops:
        return pl.pallas_call(body, grid=grid, in_specs=in_specs, out_specs=out_specs, out_shape=out_shape,
                              scratch_shapes=list(scratch_shapes), compiler_params=_cp(sem), name=name)(*args)
    outs = list(out_shape) if multi else [out_shape]
    ospecs = list(out_specs) if multi else [out_specs]
    kin = [k for k, b in plan.bufs.items() if b["arr"] is not None]
    kout = [k for k, b in plan.bufs.items() if b["write"]]
    n_in, n_out, n_scr = len(in_specs), len(outs), len(scratch_shapes)

    def wrapped(*refs):
        o0 = n_in + len(kin)
        s0 = o0 + n_out + len(kout)
        R_ = dict(zip(kin, refs[n_in:o0]))
        R_.update(zip(kout, refs[o0 + n_out:s0]))
        sems = refs[s0 + n_scr]
        first = functools.reduce(jnp.logical_and, [pl.program_id(d) == 0 for d in range(len(grid))])
        last = functools.reduce(jnp.logical_and, [pl.program_id(d) == grid[d] - 1 for d in range(len(grid))])

        @pl.when(first)
        def _():
            for op in plan.ops:
                op.start(R_, sems)

        body(*refs[:n_in], *refs[o0:o0 + n_out], *refs[s0:s0 + n_scr])

        @pl.when(last)
        def _():
            for op in plan.ops:
                op.finish(R_, sems)

    def shape_of(k):
        b = plan.bufs[k]
        return S(b["arr"].shape, b["arr"].dtype) if b["arr"] is not None else b["shape"]

    res = pl.pallas_call(
        wrapped, grid=grid, in_specs=list(in_specs) + [ANY] * len(kin), out_specs=ospecs + [ANY] * len(kout),
        out_shape=outs + [shape_of(k) for k in kout],
        scratch_shapes=list(scratch_shapes) + [pltpu.SemaphoreType.DMA((plan.nsem,))],
        input_output_aliases={n_in + kin.index(k): n_out + kout.index(k) for k in kout if plan.bufs[k]["arr"] is not None},
        compiler_params=pltpu.CompilerParams(dimension_semantics=("arbitrary",) * len(grid), vmem_limit_bytes=VMEM_LIMIT,
                                             has_side_effects=True),
        name=name)(*args, *[plan.bufs[k]["arr"] for k in kin])
    plan.out = dict(zip(kout, res[n_out:]))
    return list(res[:n_out]) if multi else res[0]


def _dg(a, b, ca, cb):
    return lax.dot_general(a.astype(BF), b.astype(BF), (((ca,), (cb,)), ((), ())), preferred_element_type=f32)


@jax.custom_vjp
def dot_nn(a, b):
    return _dg(a, b, 1, 0)


@jax.custom_vjp
def dot_nt(a, b):
    return _dg(a, b, 1, 1)


@jax.custom_vjp
def dot_tn(a, b):
    return _dg(a, b, 0, 0)


dot_nn.defvjp(lambda a, b: (dot_nn(a, b), (a, b)),
              lambda r, g: (dot_nt(g, r[1]).astype(r[0].dtype), dot_tn(r[0], g).astype(r[1].dtype)))
dot_nt.defvjp(lambda a, b: (dot_nt(a, b), (a, b)),
              lambda r, g: (dot_nn(g, r[1]).astype(r[0].dtype), dot_tn(g, r[0]).astype(r[1].dtype)))
dot_tn.defvjp(lambda a, b: (dot_tn(a, b), (a, b)),
              lambda r, g: (dot_nt(r[1], g).astype(r[0].dtype), dot_nn(r[0], g).astype(r[1].dtype)))


def matmul(a, b, *, mode, tm, tn, tk, out_dtype=f32, add=None, b_lead=None, a_spec=None, b_spec=None, dims=None, plan=None, name):
    a_over, b_over = a_spec, b_spec
    if mode == "nn":
        (M, K), N = a.shape[-2:], b.shape[-1]
        a_spec = pl.BlockSpec((tm, tk), lambda i, j, k: (i, k))
        b_blk, b_idx, ca, cb = (tk, tn), (lambda i, j, k: (k, j)), 1, 0
    elif mode == "nt":
        (M, K), N = a.shape[-2:], b.shape[-2]
        a_spec = pl.BlockSpec((tm, tk), lambda i, j, k: (i, k))
        b_blk, b_idx, ca, cb = (tn, tk), (lambda i, j, k: (j, k)), 1, 1
    else:
        (K, M), N = a.shape[-2:], b.shape[-1]
        a_spec = pl.BlockSpec((tk, tm), lambda i, j, k: (k, i))
        b_blk, b_idx, ca, cb = (tk, tn), (lambda i, j, k: (k, j)), 0, 0
    if dims is not None:
        M, N, K = dims
    assert M % tm == 0 and N % tn == 0 and K % tk == 0, (name, M, N, K, tm, tn, tk)
    if b_lead is None:
        b_spec = pl.BlockSpec(b_blk, b_idx)
    else:
        b_spec = pl.BlockSpec((None,) + b_blk, lambda i, j, k: (b_lead,) + b_idx(i, j, k))
    if a_over is not None:
        a_spec = a_over
    if b_over is not None:
        b_spec = b_over
    nk = K // tk
    has_add = add is not None

    def body(*refs):
        a_ref, b_ref = refs[0], refs[1]
        add_ref = refs[2] if has_add else None
        o_ref = refs[2 + has_add]
        p = _dg(a_ref[...], b_ref[...], ca, cb)

        def fin(v):
            if has_add:
                v = v + add_ref[...].astype(f32)
            o_ref[...] = v.astype(o_ref.dtype)

        if nk == 1:
            fin(p)
        else:
            acc = refs[3 + has_add]
            k = pl.program_id(2)

            @pl.when(k == 0)
            def _():
                acc[...] = p

            @pl.when(k > 0)
            def _():
                acc[...] += p

            @pl.when(k == nk - 1)
            def _():
                fin(acc[...])

    in_specs = [a_spec, b_spec]
    args = [a, b]
    if has_add:
        in_specs.append(pl.BlockSpec((tm, tn), lambda i, j, k: (i, j)))
        args.append(add)
    return pcall(body, plan, grid=(M // tm, N // tn, nk), in_specs=in_specs,
                 out_specs=pl.BlockSpec((tm, tn), lambda i, j, k: (i, j)), out_shape=S((M, N), out_dtype),
                 scratch_shapes=[pltpu.VMEM((tm, tn), f32)] if nk > 1 else [],
                 sem=("parallel", "parallel", "arbitrary"), name=name, args=args)


def _rms(xv, gv):
    return xv * lax.rsqrt(jnp.mean(xv * xv, axis=-1, keepdims=True) + EPS) * gv


TR = 256


def rms_fwd(x, g, name):
    def body(x_ref, g_ref, o_ref):
        o_ref[...] = _rms(x_ref[...], g_ref[...]).astype(o_ref.dtype)

    return pl.pallas_call(
        body, grid=(L // TR,),
        in_specs=[pl.BlockSpec((TR, D), lambda i: (i, 0)), pl.BlockSpec((1, D), lambda i: (0, 0))],
        out_specs=pl.BlockSpec((TR, D), lambda i: (i, 0)), out_shape=S((L, D), BF),
        compiler_params=_cp(("parallel",)), name=name)(x, g)


def rms_bwd(x, g, dys, dres, name, plan=None):
    nd = len(dys)

    def body(*refs):
        x_ref, g_ref = refs[0], refs[1]
        dr_ref, dh_ref, dg_ref = refs[2 + nd:]
        dy = refs[2][...].astype(f32)
        for r in refs[3:2 + nd]:
            dy = dy + r[...].astype(f32)
        _, vjp = jax.vjp(_rms, x_ref[...], g_ref[...])
        dx, dg = vjp(dy)
        dh_ref[...] = dr_ref[...] + dx

        @pl.when(pl.program_id(0) == 0)
        def _():
            dg_ref[...] = jnp.zeros_like(dg_ref)

        dg_ref[...] += dg

    row = pl.BlockSpec((TR, D), lambda i: (i, 0))
    vec = pl.BlockSpec((1, D), lambda i: (0, 0))
    return pcall(body, plan, grid=(L // TR,), in_specs=[row, vec] + [row] * (nd + 1), out_specs=[row, vec],
                 out_shape=[S((L, D), f32), S((1, D), f32)], sem=("arbitrary",), name=name, args=[x, g, *dys, dres])


def loss_head(h, g, tgt):
    def f(hv, gv, tv):
        y = _rms(hv, gv)
        return 0.5 * jnp.sum(jnp.mean(jnp.square(y - tv), axis=-1))

    def body(h_ref, g_ref, t_ref, l_ref, dh_ref, dg_ref):
        val, vjp = jax.vjp(f, h_ref[...], g_ref[...], t_ref[...])
        dh, dg, _ = vjp(jnp.ones((), f32))
        dh_ref[...] = dh

        @pl.when(pl.program_id(0) == 0)
        def _():
            dg_ref[...] = jnp.zeros_like(dg_ref)
            l_ref[...] = jnp.zeros_like(l_ref)

        dg_ref[...] += dg
        l_ref[...] += jnp.full((1, 128), val, f32)

    row = pl.BlockSpec((TR, D), lambda i: (i, 0))
    vec = pl.BlockSpec((1, D), lambda i: (0, 0))
    return pl.pallas_call(
        body, grid=(L // TR,), in_specs=[row, vec, row],
        out_specs=[pl.BlockSpec((1, 128), lambda i: (0, 0)), row, vec],
        out_shape=[S((1, 128), f32), S((L, D), f32), S((1, D), f32)],
        compiler_params=_cp(("arbitrary",)), name="loss_head")(h, g, tgt)


def _col_to_row(c):
    n = c.shape[0]
    t = jnp.broadcast_to(c, (n, 128)).T
    r = lax.broadcasted_iota(jnp.int32, (128, n), 0)
    return jnp.sum(jnp.where(r == 0, t, 0.0), axis=0, keepdims=True)


def _s5_param_map(are, aim, ldt_row, bre, bim, cre, cim):
    n = NST
    gi = lax.broadcasted_iota(jnp.int32, (n, 32), 0) // 64
    gj = lax.broadcasted_iota(jnp.int32, (n, 32), 1)
    ldt = jnp.sum(jnp.where(gi == gj, ldt_row, 0.0), axis=1, keepdims=True)
    dt = jnp.exp(ldt)
    mag = jnp.exp(are * dt)
    abr = mag * jnp.cos(aim * dt)
    abi = mag * jnp.sin(aim * dt)
    den = are * are + aim * aim
    nr, ni = abr - 1.0, abi
    cr = (nr * are + ni * aim) / den
    ci = (ni * are - nr * aim) / den
    bbr = cr * bre - ci * bim
    bbi = cr * bim + ci * bre
    tc = lax.broadcasted_iota(jnp.int32, (16, 128), 0)
    tl = lax.broadcasted_iota(jnp.int32, (16, 128), 1)
    T = (tl % 16 == tc).astype(f32)
    mr = (lax.broadcasted_iota(jnp.int32, (n, 128), 0) // 64) % 8
    mc = lax.broadcasted_iota(jnp.int32, (n, 128), 1) // 16
    mask = (mr == mc).astype(f32)

    def expand(v):
        return jnp.dot(v, T, precision=HI, preferred_element_type=f32) * mask

    return expand(bbr), expand(bbi), expand(cre), expand(cim), _col_to_row(abr), _col_to_row(abi)


def s5_params_fwd(are, aim, ldt_row, bre, bim, cre, cim):
    def body(*refs):
        outs = _s5_param_map(*[r[...] for r in refs[:7]])
        for o_ref, o in zip(refs[7:], outs):
            o_ref[...] = o

    return pl.pallas_call(
        body, out_shape=[S((NST, 128), f32)] * 4 + [S((1, NST), f32)] * 2,
        compiler_params=_cp(), name="s5_params_fwd")(are, aim, ldt_row, bre, bim, cre, cim)


def s5_params_bwd(are, aim, ldt_row, bre, bim, cre, cim, cots):
    def body(*refs):
        _, vjp = jax.vjp(_s5_param_map, *[r[...] for r in refs[:7]])
        gs = vjp(tuple(r[...] for r in refs[7:13]))
        for o_ref, o in zip(refs[13:], gs):
            o_ref[...] = o

    return pl.pallas_call(
        body, out_shape=[S((NST, 1), f32)] * 2 + [S((1, 32), f32)] + [S((NST, 16), f32)] * 4,
        compiler_params=_cp(), name="s5_params_bwd")(are, aim, ldt_row, bre, bim, cre, cim, *cots)


def _cpowers(ar, ai):
    out = [(ar, ai)]
    for _ in range(7):
        pr, pi = out[-1]
        out.append((pr * ar - pi * ai, pr * ai + pi * ar))
    return out


def _ctable(pw, rid, power):
    tr_ = jnp.zeros(rid.shape, f32)
    ti_ = jnp.zeros(rid.shape, f32)
    for r in range(8):
        pr, pi = pw[power(r) - 1]
        tr_ = jnp.where(rid == r, pr, tr_)
        ti_ = jnp.where(rid == r, pi, ti_)
    return tr_, ti_


NT5 = 4
RC = 256


def s5_scan_fwd(proj, wbr, wbi, wcr, wci, abr, abi, drow, plan=None):
    def body(u_ref, wbr_ref, wbi_ref, wcr_ref, wci_ref, ar_ref, ai_ref, d_ref, xr_ref, xi_ref, y_ref):
        wbr_v, wbi_v = wbr_ref[...], wbi_ref[...]
        for r in range(L // RC):
            rows = pl.ds(r * RC, RC)
            ub = u_ref[rows, :]
            xr_ref[rows, :] = dot_nt(ub, wbr_v)
            xi_ref[rows, :] = dot_nt(ub, wbi_v)
        pw = _cpowers(ar_ref[...], ai_ref[...])
        rid = lax.broadcasted_iota(jnp.int32, (8, 512), 0)
        tr_, ti_ = _ctable(pw, rid, lambda r: r + 1)

        def group(j, c):
            cr, ci = c
            rows = pl.ds(pl.multiple_of(j * 8, 8), 8)
            br, bi = xr_ref[rows, :], xi_ref[rows, :]
            for s in (1, 2, 4):
                pr, pi = pw[s - 1]
                sr = jnp.where(rid >= s, pltpu.roll(br, s, 0), 0.0)
                si = jnp.where(rid >= s, pltpu.roll(bi, s, 0), 0.0)
                br, bi = br + pr * sr - pi * si, bi + pr * si + pi * sr
            br, bi = br + tr_ * cr - ti_ * ci, bi + tr_ * ci + ti_ * cr
            xr_ref[rows, :] = br
            xi_ref[rows, :] = bi
            return br[7:8], bi[7:8]

        z = jnp.zeros((1, 512), f32)
        lax.fori_loop(0, L // 8, group, (z, z), unroll=2)
        wcr_v, wci_v, dv = wcr_ref[...], wci_ref[...], d_ref[...]
        for r in range(L // RC):
            rows = pl.ds(r * RC, RC)
            y_ref[rows, :] = (dot_nn(xr_ref[rows, :], wcr_v) - dot_nn(xi_ref[rows, :], wci_v)
                              + dv * u_ref[rows, :])

    wspec = pl.BlockSpec((512, 128), lambda j: (j, 0))
    aspec = pl.BlockSpec((1, 512), lambda j: (0, j))
    return pcall(
        body, plan, grid=(NT5,),
        in_specs=[pl.BlockSpec((L, 128), lambda j: (0, j)), wspec, wspec, wspec, wspec, aspec, aspec,
                  pl.BlockSpec((1, 128), lambda j: (0, j))],
        out_specs=[pl.BlockSpec((L, 512), lambda j: (0, j)), pl.BlockSpec((L, 512), lambda j: (0, j)),
                   pl.BlockSpec((L, 128), lambda j: (0, j))],
        out_shape=[S((L, NST), f32), S((L, NST), f32), S((L, S5W), f32)],
        sem=("parallel",), name="s5_scan_fwd", args=[proj, wbr, wbi, wcr, wci, abr, abi, drow])


def s5_scan_bwd(dy, proj, xs_re, xs_im, wbr, wbi, wcr, wci, abr, abi, drow, plan=None):
    def body(dy_ref, u_ref, xr_ref, xi_ref, wbr_ref, wbi_ref, wcr_ref, wci_ref, ar_ref, ai_ref, d_ref,
             du_ref, gwbr_ref, gwbi_ref, gwcr_ref, gwci_ref, gar_ref, gai_ref, gd_ref, lr_ref, li_ref):
        wcr_v, wci_v = wcr_ref[...], wci_ref[...]
        gwcr = jnp.zeros((512, 128), f32)
        gwci = jnp.zeros((512, 128), f32)
        gd = jnp.zeros((1, 128), f32)
        for r in range(L // RC):
            rows = pl.ds(r * RC, RC)
            dyv = dy_ref[rows, :]
            lr_ref[rows, :] = dot_nt(dyv, wcr_v)
            li_ref[rows, :] = -dot_nt(dyv, wci_v)
            gwcr += dot_tn(xr_ref[rows, :], dyv)
            gwci -= dot_tn(xi_ref[rows, :], dyv)
            gd += jnp.sum(dyv * u_ref[rows, :], axis=0, keepdims=True)
        gwcr_ref[...] = gwcr
        gwci_ref[...] = gwci
        gd_ref[...] = gd
        pw = _cpowers(ar_ref[...], -ai_ref[...])
        rid = lax.broadcasted_iota(jnp.int32, (8, 512), 0)
        tr_, ti_ = _ctable(pw, rid, lambda r: 8 - r)

        def group(i, c):
            cr, ci, gar, gai = c
            j = L // 8 - 1 - i
            rows = pl.ds(pl.multiple_of(j * 8, 8), 8)
            br, bi = lr_ref[rows, :], li_ref[rows, :]
            for s in (1, 2, 4):
                pr, pi = pw[s - 1]
                sr = jnp.where(rid < 8 - s, pltpu.roll(br, 8 - s, 0), 0.0)
                si = jnp.where(rid < 8 - s, pltpu.roll(bi, 8 - s, 0), 0.0)
                br, bi = br + pr * sr - pi * si, bi + pr * si + pi * sr
            br, bi = br + tr_ * cr - ti_ * ci, bi + tr_ * ci + ti_ * cr
            lr_ref[rows, :] = br
            li_ref[rows, :] = bi
            nr = jnp.where(rid < 7, pltpu.roll(br, 7, 0), cr)
            ni = jnp.where(rid < 7, pltpu.roll(bi, 7, 0), ci)
            xr, xi = xr_ref[rows, :], xi_ref[rows, :]
            return br[0:1], bi[0:1], gar + xr * nr + xi * ni, gai + xr * ni - xi * nr

        z = jnp.zeros((1, 512), f32)
        z8 = jnp.zeros((8, 512), f32)
        _, _, gar, gai = lax.fori_loop(0, L // 8, group, (z, z, z8, z8), unroll=2)
        gar_ref[...] = jnp.sum(gar, axis=0, keepdims=True)
        gai_ref[...] = jnp.sum(gai, axis=0, keepdims=True)
        wbr_v, wbi_v, dv = wbr_ref[...], wbi_ref[...], d_ref[...]
        gwbr = jnp.zeros((512, 128), f32)
        gwbi = jnp.zeros((512, 128), f32)
        for r in range(L // RC):
            rows = pl.ds(r * RC, RC)
            lrv, liv, uv = lr_ref[rows, :], li_ref[rows, :], u_ref[rows, :]
            du_ref[rows, :] = (dot_nn(lrv, wbr_v) + dot_nn(liv, wbi_v) + dv * dy_ref[rows, :]).astype(du_ref.dtype)
            gwbr += dot_tn(lrv, uv)
            gwbi += dot_tn(liv, uv)
        gwbr_ref[...] = gwbr
        gwbi_ref[...] = gwbi

    wspec = pl.BlockSpec((512, 128), lambda j: (j, 0))
    aspec = pl.BlockSpec((1, 512), lambda j: (0, j))
    col = pl.BlockSpec((L, 128), lambda j: (0, j))
    st = pl.BlockSpec((L, 512), lambda j: (0, j))
    dspec = pl.BlockSpec((1, 128), lambda j: (0, j))
    return pcall(
        body, plan, grid=(NT5,),
        in_specs=[col, col, st, st, wspec, wspec, wspec, wspec, aspec, aspec, dspec],
        out_specs=[col, wspec, wspec, wspec, wspec, aspec, aspec, dspec],
        out_shape=[S((L, S5W), BF)] + [S((NST, 128), f32)] * 4 + [S((1, NST), f32)] * 2 + [S((1, S5W), f32)],
        scratch_shapes=[pltpu.VMEM((L, 512), f32), pltpu.VMEM((L, 512), f32)],
        sem=("parallel",), name="s5_scan_bwd", args=[dy, proj, xs_re, xs_im, wbr, wbi, wcr, wci, abr, abi, drow])


def _glu(y, w, b):
    z = jax.nn.gelu(y)
    return z * jax.nn.sigmoid(dot_nn(z, w) + b)


def s5_glu_fwd(y, w, b):
    def body(y_ref, w_ref, b_ref, o_ref):
        o_ref[...] = _glu(y_ref[...], w_ref[...], b_ref[...]).astype(o_ref.dtype)

    return pl.pallas_call(
        body, grid=(L // TR,),
        in_specs=[pl.BlockSpec((TR, S5W), lambda i: (i, 0)), pl.BlockSpec((S5W, S5W), lambda i: (0, 0)),
                  pl.BlockSpec((1, S5W), lambda i: (0, 0))],
        out_specs=pl.BlockSpec((TR, S5W), lambda i: (i, 0)), out_shape=S((L, S5W), BF),
        compiler_params=_cp(("parallel",)), name="s5_glu_fwd")(y, w, b)


def s5_glu_bwd(y, w, b, dmix):
    def body(y_ref, w_ref, b_ref, g_ref, dy_ref, dw_ref, db_ref):
        _, vjp = jax.vjp(_glu, y_ref[...], w_ref[...].astype(f32), b_ref[...])
        dy, dw, db = vjp(g_ref[...])
        dy_ref[...] = dy

        @pl.when(pl.program_id(0) == 0)
        def _():
            dw_ref[...] = jnp.zeros_like(dw_ref)
            db_ref[...] = jnp.zeros_like(db_ref)

        dw_ref[...] += dw
        db_ref[...] += db

    row = pl.BlockSpec((TR, S5W), lambda i: (i, 0))
    return pl.pallas_call(
        body, grid=(L // TR,),
        in_specs=[row, pl.BlockSpec((S5W, S5W), lambda i: (0, 0)), pl.BlockSpec((1, S5W), lambda i: (0, 0)), row],
        out_specs=[row, pl.BlockSpec((S5W, S5W), lambda i: (0, 0)), pl.BlockSpec((1, S5W), lambda i: (0, 0))],
        out_shape=[S((L, S5W), f32), S((S5W, S5W), f32), S((1, S5W), f32)],
        compiler_params=_cp(("arbitrary",)), name="s5_glu_bwd")(y, w, b, dmix)


def _dg3(a, b, ca, cb):
    ah, bh = a.astype(BF), b.astype(BF)
    al, bl = (a - ah.astype(f32)).astype(BF), (b - bh.astype(f32)).astype(BF)
    return _dg(ah, bh, ca, cb) + _dg(ah, bl, ca, cb) + _dg(al, bh, ca, cb)


@jax.custom_vjp
def hi_nn(a, b):
    return _dg3(a, b, 1, 0)


@jax.custom_vjp
def hi_nt(a, b):
    return _dg3(a, b, 1, 1)


@jax.custom_vjp
def hi_tn(a, b):
    return _dg3(a, b, 0, 0)


hi_nn.defvjp(lambda a, b: (hi_nn(a, b), (a, b)), lambda r, g: (hi_nt(g, r[1]), hi_tn(r[0], g)))
hi_nt.defvjp(lambda a, b: (hi_nt(a, b), (a, b)), lambda r, g: (hi_nn(g, r[1]), hi_tn(g, r[0])))
hi_tn.defvjp(lambda a, b: (hi_tn(a, b), (a, b)), lambda r, g: (hi_nt(r[1], g), hi_nn(r[0], g)))


def _hgrn_chunk(St, xq, xf, xi, xg, gam, ng):
    lb = jax.nn.sigmoid(gam[0:1] - gam[1:2])
    q = jax.nn.silu(xq)
    f = lb + (1.0 - lb) * jax.nn.sigmoid(xf)
    k = 1.0 - f
    g = jnp.log(f)
    ti = lax.broadcasted_iota(jnp.int32, (HGC, HGC), 0)
    si = lax.broadcasted_iota(jnp.int32, (HGC, HGC), 1)
    causal = si <= ti
    b = jnp.dot(causal.astype(f32), g, precision=HI, preferred_element_type=f32)
    qe = q * jnp.exp(b)
    o = dot_nt(qe, St)
    parts = []
    for i in range(HGC // HGB):
        r, n, mid = slice(HGB * i, HGB * (i + 1)), HGB * (i + 1), HGB * i + HGB // 2
        base = b[mid:mid + 1]
        sc = hi_nt(q[r] * jnp.exp(b[r] - base), k[:n] * jnp.exp(base - b[:n]))
        parts.append(dot_nn(jnp.where(causal[r, :n], sc, 0.0), xi[:n]))
    o = o + jnp.concatenate(parts, axis=0)
    bl = b[HGC - 1:HGC]
    St_new = St * jnp.exp(bl) + dot_tn(xi, k * jnp.exp(bl - b))
    o = o * lax.rsqrt(jnp.mean(o * o, axis=-1, keepdims=True) + EPS) * ng
    return St_new, o * jax.nn.silu(xg)


NCH = L // HGC


def hgrn_fwd(proj, gamma, hnorm, plan=None):
    def body(q_ref, f_ref, i_ref, g_ref, gam_ref, ng_ref, o_ref, ss_ref, st):
        @pl.when(pl.program_id(0) == 0)
        def _():
            st[...] = jnp.zeros_like(st)

        for h in range(4):
            sl = slice(h * 128, (h + 1) * 128)
            s0 = st[h]
            ss_ref[0, h] = s0
            s1, o = _hgrn_chunk(s0, q_ref[:, sl], f_ref[:, sl], i_ref[:, sl], g_ref[:, sl], gam_ref[:, sl], ng_ref[:, sl])
            st[h] = s1
            o_ref[:, sl] = o.astype(o_ref.dtype)

    def pj(n):
        return pl.BlockSpec((HGC, 512), lambda c: (c, n))

    return pcall(
        body, plan, grid=(NCH,),
        in_specs=[pj(1), pj(2), pj(3), pj(4), pl.BlockSpec((2, 512), lambda c: (0, 0)), pl.BlockSpec((1, 512), lambda c: (0, 0))],
        out_specs=[pl.BlockSpec((HGC, 512), lambda c: (c, 0)), pl.BlockSpec((1, 4, 128, 128), lambda c: (c, 0, 0, 0))],
        out_shape=[S((L, 512), BF), S((NCH, 4, 128, 128), f32)],
        scratch_shapes=[pltpu.VMEM((4, 128, 128), f32)],
        sem=("arbitrary",), name="hgrn_fwd", args=[proj, proj, proj, proj, gamma, hnorm])


def hgrn_bwd(proj, gamma, hnorm, ssave, dmix, du, plan=None):
    def body(q_ref, f_ref, i_ref, g_ref, gam_ref, ng_ref, ss_ref, do_ref, du_ref, dp_ref, dgam_ref, dng_ref, dst):
        @pl.when(pl.program_id(0) == 0)
        def _():
            dst[...] = jnp.zeros_like(dst)
            dgam_ref[...] = jnp.zeros_like(dgam_ref)
            dng_ref[...] = jnp.zeros_like(dng_ref)

        dp_ref[:, 0:512] = du_ref[...]
        for h in range(4):
            sl = slice(h * 128, (h + 1) * 128)
            _, vjp = jax.vjp(_hgrn_chunk, ss_ref[0, h], q_ref[:, sl], f_ref[:, sl], i_ref[:, sl], g_ref[:, sl],
                             gam_ref[:, sl], ng_ref[:, sl])
            ds, dq, df, di, dg, dgam, dng = vjp((dst[h], do_ref[:, sl]))
            dst[h] = ds
            for n, v in enumerate((dq, df, di, dg)):
                dp_ref[:, 512 * (n + 1) + h * 128: 512 * (n + 1) + (h + 1) * 128] = v.astype(dp_ref.dtype)
            dgam_ref[:, sl] += dgam
            dng_ref[:, sl] += dng

    def pj(n):
        return pl.BlockSpec((HGC, 512), lambda i: (NCH - 1 - i, n))

    return pcall(
        body, plan, grid=(NCH,),
        in_specs=[pj(1), pj(2), pj(3), pj(4), pl.BlockSpec((2, 512), lambda i: (0, 0)), pl.BlockSpec((1, 512), lambda i: (0, 0)),
                  pl.BlockSpec((1, 4, 128, 128), lambda i: (NCH - 1 - i, 0, 0, 0)), pj(1), pj(0)],
        out_specs=[pl.BlockSpec((HGC, 2560), lambda i: (NCH - 1 - i, 0)), pl.BlockSpec((2, 512), lambda i: (0, 0)),
                   pl.BlockSpec((1, 512), lambda i: (0, 0))],
        out_shape=[S((L, 2560), BF), S((2, 512), f32), S((1, 512), f32)],
        scratch_shapes=[pltpu.VMEM((4, 128, 128), f32)],
        sem=("arbitrary",), name="hgrn_bwd", args=[proj, proj, proj, proj, gamma, hnorm, ssave, dmix, du])


def _earlier(h_ref, k, r0, n):
    if r0 > 0:
        return h_ref[pl.ds(r0 - k, n), :]
    rid = lax.broadcasted_iota(jnp.int32, (8, h_ref.shape[1]), 0)
    head = jnp.where(rid >= k, pltpu.roll(h_ref[pl.ds(0, 8), :], k, 0), 0.0)
    return jnp.concatenate([head, h_ref[pl.ds(8 - k, n - 8), :]], axis=0)


def _conv3_rows(h_ref, w, b, r0, n=None):
    n = CR if n is None else n
    h1, h2 = _earlier(h_ref, 1, r0, n), _earlier(h_ref, 2, r0, n)
    return w[2:3] * h_ref[pl.ds(r0, n), :] + w[1:2] * h1 + w[0:1] * h2 + b, h1, h2


CT = 128
NCT = DFF // CT
CR = 64


def convact_fwd(hu, cw, cb, layer, plan=None):
    def body(ha_ref, hb_ref, wa_ref, wb_ref, ba_ref, bb_ref, o_ref, sa, sb_):
        sa[...] = ha_ref[...].astype(f32)
        sb_[...] = hb_ref[...].astype(f32)
        ca = _conv3_rows(sa, wa_ref[...], ba_ref[...], 0, L)[0]
        cb_ = _conv3_rows(sb_, wb_ref[...], bb_ref[...], 0, L)[0]
        o_ref[...] = (jax.nn.silu(ca) * cb_).astype(o_ref.dtype)

    def h(off):
        return pl.BlockSpec((L, CT), lambda j: (0, j + off))

    def w(off):
        return pl.BlockSpec((3, CT), lambda j: (0, j + off))

    def b(off):
        return pl.BlockSpec((None, 1, CT), lambda j: (layer, 0, j + off))

    return pcall(body, plan, grid=(NCT,), in_specs=[h(0), h(NCT), w(0), w(NCT), b(0), b(NCT)],
                 out_specs=pl.BlockSpec((L, CT), lambda j: (0, j)), out_shape=S((L, DFF), BF),
                 scratch_shapes=[pltpu.VMEM((L, CT), f32), pltpu.VMEM((L, CT), f32)],
                 sem=("parallel",), name=f"convact_fwd{layer}", args=[hu, hu, cw, cw, cb, cb])


def convact_bwd(hu, cw, cb, dact, layer, plan=None):
    def body(ha_ref, hb_ref, wa_ref, wb_ref, ba_ref, bb_ref, g_ref, dh_ref, dw_ref, db_ref, sh, sw, sb, da_scr, db_scr, ha, hb):
        j = pl.program_id(0)

        def fold(x):
            return functools.reduce(jnp.add, [x[8 * m:8 * m + 8] for m in range(CR // 8)])

        @pl.when(j < NCT)
        def _():
            wa, wb, ba, bb = wa_ref[...], wb_ref[...], ba_ref[...], bb_ref[...]
            ha[...] = ha_ref[...].astype(f32)
            hb[...] = hb_ref[...].astype(f32)
            da_scr[pl.ds(L, 8), :] = jnp.zeros((8, CT), f32)
            db_scr[pl.ds(L, 8), :] = jnp.zeros((8, CT), f32)
            acc = [jnp.zeros((8, CT), f32) for _ in range(8)]
            for c in range(L // CR):
                r0 = c * CR
                ca, a1, a2 = _conv3_rows(ha, wa, ba, r0)
                cb_, b1, b2 = _conv3_rows(hb, wb, bb, r0)
                g = g_ref[pl.ds(r0, CR), :].astype(f32)
                sg = jax.nn.sigmoid(ca)
                dca = g * cb_ * (sg * (1.0 + ca * (1.0 - sg)))
                dcb = g * (ca * sg)
                da_scr[pl.ds(r0, CR), :] = dca
                db_scr[pl.ds(r0, CR), :] = dcb
                terms = (dca * a2, dca * a1, dca * ha[pl.ds(r0, CR), :], dca,
                         dcb * b2, dcb * b1, dcb * hb[pl.ds(r0, CR), :], dcb)
                acc = [a + fold(t) for a, t in zip(acc, terms)]
            rows = [jnp.sum(a, axis=0, keepdims=True) for a in acc]
            for k in range(3):
                dw_ref[k:k + 1, :] = rows[k]
                sw[j, k:k + 1, :] = rows[4 + k]
            db_ref[...] = rows[3]
            sb[j] = rows[7]
            for c in range(L // CR):
                r0 = c * CR
                for scr, w, out in ((da_scr, wa, dh_ref), (db_scr, wb, sh.at[j])):
                    dh = (w[2:3] * scr[pl.ds(r0, CR), :] + w[1:2] * scr[pl.ds(r0 + 1, CR), :]
                          + w[0:1] * scr[pl.ds(r0 + 2, CR), :])
                    out[pl.ds(r0, CR), :] = dh.astype(out.dtype)

        @pl.when(j >= NCT)
        def _():
            dh_ref[...] = sh[j - NCT]
            dw_ref[...] = sw[j - NCT]
            db_ref[...] = sb[j - NCT]

    def lo(j):
        return jnp.minimum(j, NCT - 1)

    in_specs = [pl.BlockSpec((L, CT), lambda j: (0, lo(j))), pl.BlockSpec((L, CT), lambda j: (0, lo(j) + NCT)),
                pl.BlockSpec((3, CT), lambda j: (0, lo(j))), pl.BlockSpec((3, CT), lambda j: (0, lo(j) + NCT)),
                pl.BlockSpec((None, 1, CT), lambda j: (layer, 0, lo(j))), pl.BlockSpec((None, 1, CT), lambda j: (layer, 0, lo(j) + NCT)),
                pl.BlockSpec((L, CT), lambda j: (0, lo(j)))]
    return pcall(
        body, plan, grid=(2 * NCT,), in_specs=in_specs,
        out_specs=[pl.BlockSpec((L, CT), lambda j: (0, j)), pl.BlockSpec((3, CT), lambda j: (0, j)), pl.BlockSpec((1, CT), lambda j: (0, j))],
        out_shape=[S((L, 2 * DFF), BF), S((3, 2 * DFF), f32), S((1, 2 * DFF), f32)],
        scratch_shapes=[pltpu.VMEM((NCT, L, CT), BF), pltpu.VMEM((NCT, 3, CT), f32), pltpu.VMEM((NCT, 1, CT), f32),
                        pltpu.VMEM((L + 8, CT), f32), pltpu.VMEM((L + 8, CT), f32),
                        pltpu.VMEM((L, CT), f32), pltpu.VMEM((L, CT), f32)],
        sem=("arbitrary",), name=f"convact_bwd{layer}", args=[hu, hu, cw, cw, cb, cb, dact])


DILS = (1, 4, 16)
AB = 128


def _rope_tables(pos_ref, invf_ref):
    ang = pos_ref[...].astype(f32) * invf_ref[...]
    lane = lax.broadcasted_iota(jnp.int32, (1, 128), 1) % 64
    cosf = jnp.where(lane < 16, jnp.cos(ang), 1.0)
    sn = jnp.sin(ang)
    s_lo = jnp.where(lane < 8, -sn, 0.0)
    s_hi = jnp.where((lane >= 8) & (lane < 16), sn, 0.0)
    return cosf, s_lo, s_hi


def _rope(t, cosf, s_lo, s_hi):
    return t * cosf + pltpu.roll(t, 120, 1) * s_lo + pltpu.roll(t, 8, 1) * s_hi


def _rope_t(g, cosf, s_lo, s_hi):
    return g * cosf + pltpu.roll(g * s_lo, 8, 1) + pltpu.roll(g * s_hi, 120, 1)


def _att_block(q2, kp, kc, vp, vc, first):
    lane = lax.broadcasted_iota(jnp.int32, (1, 128), 1)
    qi = lax.broadcasted_iota(jnp.int32, (AB, 2 * AB), 0) + AB
    kj = lax.broadcasted_iota(jnp.int32, (AB, 2 * AB), 1)
    back = qi - kj
    valid = (back >= 0) & (back <= AB)
    if first:
        valid = valid & (kj >= AB)
    kk = jnp.concatenate([kp, kc], axis=0)
    vv = jnp.concatenate([vp, vc], axis=0)
    o2 = jnp.zeros((AB, 128), f32)
    lse2 = jnp.zeros((AB, 128), f32)
    for e in range(2):
        hm = ((lane >= 64 * e) & (lane < 64 * (e + 1))).astype(f32)
        s = dot_nt(q2 * (hm * 0.125), kk)
        s = jnp.where(valid, s, -jnp.inf)
        m = jnp.max(s, axis=-1, keepdims=True)
        p = jnp.exp(s - m)
        den = jnp.sum(p, axis=-1, keepdims=True)
        o2 = o2 + dot_nn(p, vv * hm) / den
        lse2 = lse2 + (m + jnp.log(den)) * hm
    return o2, lse2


def _att_blocks(dil):
    m = L // dil
    return [(r * m + n * AB, n == 0) for r in range(dil) for n in range(m // AB)]


def deinterleave(x, dil):
    return x if dil == 1 else x.reshape(L // dil, dil, x.shape[1]).swapaxes(0, 1).reshape(L, x.shape[1])


def attn_fwd(qkv, pos, invf, g, plan=None):
    blocks = _att_blocks(DILS[g])

    def body(q_ref, k_ref, v_ref, pos_ref, invf_ref, o_ref, l_ref, qr, kr):
        cosf, s_lo, s_hi = _rope_tables(pos_ref, invf_ref)
        qr[...] = _rope(q_ref[...], cosf, s_lo, s_hi)
        kr[...] = _rope(k_ref[...], cosf, s_lo, s_hi)
        for off, first in blocks:
            cur, prv = pl.ds(off, AB), pl.ds(off if first else off - AB, AB)
            o2, lse2 = _att_block(qr[cur, :], kr[prv, :], kr[cur, :], v_ref[prv, :], v_ref[cur, :], first)
            o_ref[cur, :] = o2
            l_ref[cur, :] = lse2

    def sec(n):
        return pl.BlockSpec((L, 128), lambda p: (0, p + 4 * n))

    return pcall(
        body, plan, grid=(4,),
        in_specs=[sec(0), sec(1), sec(2), pl.BlockSpec((L, 1), lambda p: (0, 0)), pl.BlockSpec((1, 128), lambda p: (0, 0))],
        out_specs=[sec(0), sec(0)], out_shape=[S((L, 512), f32), S((L, 512), f32)],
        scratch_shapes=[pltpu.VMEM((L, 128), f32), pltpu.VMEM((L, 128), f32)],
        sem=("parallel",), name=f"attn_fwd{g}", args=[qkv, qkv, qkv, pos, invf])


def _att_block_bwd(q2, kp, kc, vp, vc, lse2, do2, dl2, first):
    lane = lax.broadcasted_iota(jnp.int32, (1, 128), 1)
    qi = lax.broadcasted_iota(jnp.int32, (AB, 2 * AB), 0) + AB
    kj = lax.broadcasted_iota(jnp.int32, (AB, 2 * AB), 1)
    back = qi - kj
    valid = (back >= 0) & (back <= AB)
    if first:
        valid = valid & (kj >= AB)
    kk = jnp.concatenate([kp, kc], axis=0)
    vv = jnp.concatenate([vp, vc], axis=0)
    dq2 = jnp.zeros((AB, 128), f32)
    dkk = jnp.zeros((2 * AB, 128), f32)
    dvv = jnp.zeros((2 * AB, 128), f32)
    for e in range(2):
        hb = (lane >= 64 * e) & (lane < 64 * (e + 1))
        hm = hb.astype(f32)
        qs = q2 * (hm * 0.125)
        lse = jnp.max(jnp.where(hb, lse2, -jnp.inf), axis=-1, keepdims=True)
        dls = jnp.sum(dl2 * hm, axis=-1, keepdims=True)
        p = jnp.where(valid, jnp.exp(dot_nt(qs, kk) - lse), 0.0)
        dov = do2 * hm
        dp = dot_nt(dov, vv)
        ds = p * (dp - jnp.sum(p * dp, axis=-1, keepdims=True) + dls)
        dq2 = dq2 + dot_nn(ds, kk) * (hm * 0.125)
        dkk = dkk + dot_tn(ds, qs)
        dvv = dvv + dot_tn(p, dov)
    return dq2, dkk[:AB], dkk[AB:], dvv[:AB], dvv[AB:]


def attn_bwd(qkv, pos, invf, lse, do, dl, g, plan=None):
    blocks = _att_blocks(DILS[g])

    def body(q_ref, k_ref, v_ref, pos_ref, invf_ref, l_ref, do_ref, dl_ref, d_ref, qr, kr, dqr, dkr, dvr):
        cosf, s_lo, s_hi = _rope_tables(pos_ref, invf_ref)
        qr[...] = _rope(q_ref[...], cosf, s_lo, s_hi)
        kr[...] = _rope(k_ref[...], cosf, s_lo, s_hi)
        for off, first in blocks:
            cur, prv = pl.ds(off, AB), pl.ds(off if first else off - AB, AB)
            dq2, dkp, dkc, dvp, dvc = _att_block_bwd(qr[cur, :], kr[prv, :], kr[cur, :], v_ref[prv, :], v_ref[cur, :],
                                                     l_ref[cur, :], do_ref[cur, :], dl_ref[cur, :], first)
            dqr[cur, :] = dq2
            dkr[cur, :] = dkc
            dvr[cur, :] = dvc
            if not first:
                dkr[prv, :] += dkp
                dvr[prv, :] += dvp
        d_ref[0] = _rope_t(dqr[...], cosf, s_lo, s_hi).astype(d_ref.dtype)
        d_ref[1] = _rope_t(dkr[...], cosf, s_lo, s_hi).astype(d_ref.dtype)
        d_ref[2] = dvr[...].astype(d_ref.dtype)

    def sec(n):
        return pl.BlockSpec((L, 128), lambda p: (0, p + 4 * n))

    return pcall(
        body, plan, grid=(4,),
        in_specs=[sec(0), sec(1), sec(2), pl.BlockSpec((L, 1), lambda p: (0, 0)), pl.BlockSpec((1, 128), lambda p: (0, 0)),
                  sec(0), sec(0), sec(0)],
        out_specs=pl.BlockSpec((3, L, 128), lambda p: (0, 0, p)), out_shape=S((3, L, 512), BF),
        scratch_shapes=[pltpu.VMEM((L, 128), f32)] * 5,
        sem=("parallel",), name=f"attn_bwd{g}", args=[qkv, qkv, qkv, pos, invf, lse, do, dl])


def _merge(o0, o1, o2, l0, l1, l2):
    m = jnp.maximum(jnp.maximum(l0, l1), l2)
    e0, e1, e2 = jnp.exp(l0 - m), jnp.exp(l1 - m), jnp.exp(l2 - m)
    return (e0 * o0 + e1 * o1 + e2 * o2) / (e0 + e1 + e2)


def _to_token_major(src_ref, scr, i, dil, slab):
    n = TR // dil
    for r in range(dil):
        rows = pl.ds(pl.multiple_of(r * (L // dil) + i * n, n), n)
        scr[pl.ds(r, n, stride=dil), :] = src_ref[rows, slab * 128:(slab + 1) * 128].astype(f32)
    return scr[...]


def _to_class_major(val, dst_ref, scr, i, dil, slab):
    n = TR // dil
    scr[...] = val
    for r in range(dil):
        rows = pl.ds(pl.multiple_of(r * (L // dil) + i * n, n), n)
        dst_ref[rows, slab * 128:(slab + 1) * 128] = scr[pl.ds(r, n, stride=dil), :].astype(dst_ref.dtype)


def rms_fwd_classes(x, g, name):
    def body(x_ref, g_ref, o_ref, o1_ref, o2_ref, scr):
        i = pl.program_id(0)
        y = _rms(x_ref[...], g_ref[...])
        o_ref[...] = y.astype(o_ref.dtype)
        for s in range(D // 128):
            ys = y[:, s * 128:(s + 1) * 128]
            _to_class_major(ys, o1_ref, scr, i, DILS[1], s)
            _to_class_major(ys, o2_ref, scr, i, DILS[2], s)

    row = pl.BlockSpec((TR, D), lambda i: (i, 0))
    full = pl.BlockSpec((L, D), lambda i: (0, 0))
    return pl.pallas_call(
        body, grid=(L // TR,), in_specs=[row, pl.BlockSpec((1, D), lambda i: (0, 0))], out_specs=[row, full, full],
        out_shape=[S((L, D), BF)] * 3, scratch_shapes=[pltpu.VMEM((TR, 128), f32)],
        compiler_params=_cp(("arbitrary",)), name=name)(x, g)


def rms_bwd_classes(x, g, dy0, dyc, dres, name, plan=None):
    def body(x_ref, g_ref, dy0_ref, d1_ref, d2_ref, dr_ref, dh_ref, dg_ref, scr, dyf):
        i = pl.program_id(0)
        for s in range(D // 128):
            sl = slice(s * 128, (s + 1) * 128)
            dyf[:, sl] = (dy0_ref[:, sl] + _to_token_major(d1_ref, scr.at[0], i, DILS[1], s)
                          + _to_token_major(d2_ref, scr.at[1], i, DILS[2], s))
        _, vjp = jax.vjp(_rms, x_ref[...], g_ref[...])
        dx, dg = vjp(dyf[...])
        dh_ref[...] = dr_ref[...] + dx

        @pl.when(i == 0)
        def _():
            dg_ref[...] = jnp.zeros_like(dg_ref)

        dg_ref[...] += dg

    row = pl.BlockSpec((TR, D), lambda i: (i, 0))
    vec = pl.BlockSpec((1, D), lambda i: (0, 0))
    full = pl.BlockSpec((L, D), lambda i: (0, 0))
    return pcall(body, plan, grid=(L // TR,), in_specs=[row, vec, row, full, full, row], out_specs=[row, vec],
                 out_shape=[S((L, D), f32), S((1, D), f32)],
                 scratch_shapes=[pltpu.VMEM((2, TR, 128), f32), pltpu.VMEM((TR, D), f32)],
                 sem=("arbitrary",), name=name, args=[x, g, dy0, dyc[0], dyc[1], dres])


def attn_merge_fwd(o0, l0, oc, lc, plan=None):
    def body(o0_ref, l0_ref, o1_ref, l1_ref, o2_ref, l2_ref, o_ref, scr):
        i = pl.program_id(0)
        for s in range(4):
            sl = slice(s * 128, (s + 1) * 128)
            o1 = _to_token_major(o1_ref, scr.at[0], i, DILS[1], s)
            l1 = _to_token_major(l1_ref, scr.at[1], i, DILS[1], s)
            o2 = _to_token_major(o2_ref, scr.at[2], i, DILS[2], s)
            l2 = _to_token_major(l2_ref, scr.at[3], i, DILS[2], s)
            o_ref[:, sl] = _merge(o0_ref[:, sl], o1, o2, l0_ref[:, sl], l1, l2).astype(o_ref.dtype)

    blk = pl.BlockSpec((TR, 512), lambda i: (i, 0))
    full = pl.BlockSpec((L, 512), lambda i: (0, 0))
    return pcall(body, plan, grid=(L // TR,), in_specs=[blk, blk, full, full, full, full], out_specs=blk,
                 out_shape=S((L, 512), BF), scratch_shapes=[pltpu.VMEM((4, TR, 128), f32)],
                 sem=("arbitrary",), name="attn_merge_fwd", args=[o0, l0, oc[0], lc[0], oc[1], lc[1]])


def attn_merge_bwd(o0, l0, oc, lc, do, plan=None):
    def body(o0_ref, l0_ref, o1_ref, l1_ref, o2_ref, l2_ref, g_ref, do0, dl0, do1, dl1, do2, dl2, scr):
        i = pl.program_id(0)
        for s in range(4):
            sl = slice(s * 128, (s + 1) * 128)
            o1 = _to_token_major(o1_ref, scr.at[0], i, DILS[1], s)
            l1 = _to_token_major(l1_ref, scr.at[1], i, DILS[1], s)
            o2 = _to_token_major(o2_ref, scr.at[2], i, DILS[2], s)
            l2 = _to_token_major(l2_ref, scr.at[3], i, DILS[2], s)
            _, vjp = jax.vjp(_merge, o0_ref[:, sl], o1, o2, l0_ref[:, sl], l1, l2)
            g0, g1, g2, h0, h1, h2 = vjp(g_ref[:, sl].astype(f32))
            do0[:, sl] = g0.astype(do0.dtype)
            dl0[:, sl] = h0
            _to_class_major(g1, do1, scr.at[0], i, DILS[1], s)
            _to_class_major(h1, dl1, scr.at[1], i, DILS[1], s)
            _to_class_major(g2, do2, scr.at[2], i, DILS[2], s)
            _to_class_major(h2, dl2, scr.at[3], i, DILS[2], s)

    blk = pl.BlockSpec((TR, 512), lambda i: (i, 0))
    full = pl.BlockSpec((L, 512), lambda i: (0, 0))
    outs = pcall(body, plan, grid=(L // TR,), in_specs=[blk, blk, full, full, full, full, blk],
                 out_specs=[blk, blk, full, full, full, full],
                 out_shape=[S((L, 512), BF), S((L, 512), f32)] * 3, scratch_shapes=[pltpu.VMEM((4, TR, 128), f32)],
                 sem=("arbitrary",), name="attn_merge_bwd", args=[o0, l0, oc[0], lc[0], oc[1], lc[1], do])
    return [outs[0], outs[2], outs[4]], [outs[1], outs[3], outs[5]]


def _invf_lanes():
    half = 8
    inv = ROPE_THETA ** (-np.arange(half, dtype=np.float32) * 2.0 / 16.0)
    lane = np.arange(128) % 64
    return jnp.asarray(np.where(lane < 16, inv[lane % 8], 0.0).astype(np.float32)[None, :])


def hosted(C, host, fn):
    p = C.plan(host) if C is not None else None
    out = fn(p)
    if p is not None:
        C.done(p)
    return out


def _ffn_fwd(h, g_row, W, cb, layer, C):
    hn = rms_fwd(h, g_row, f"rms_ffn{layer}")
    hu = hosted(C, f"ffn_in{layer}", lambda p: matmul(hn, W[("ffn_w_in", layer)], mode="nn", tm=1024, tn=1408, tk=1024,
                                                      out_dtype=BF, plan=p, name=f"ffn_in{layer}"))
    act = hosted(C, f"convact_fwd{layer}", lambda p: convact_fwd(hu, W[("ffn_conv_w", layer)], cb, layer, plan=p))
    h2 = hosted(C, f"ffn_out{layer}", lambda p: matmul(act, W[("ffn_w_out", layer)], mode="nn", tm=1024, tn=1024, tk=2816,
                                                       add=h, plan=p, name=f"ffn_out{layer}"))
    return h2, (hn, hu, act)


def _ffn_bwd(dh, h, g_row, W, cb, saved, layer, C, G):
    hn, hu, act = saved
    w_in, w_out = W[("ffn_w_in", layer)], W[("ffn_w_out", layer)]
    dact = hosted(C, f"ffn_out_dx{layer}", lambda p: matmul(dh, w_out, mode="nt", tm=1024, tn=1408, tk=1024, out_dtype=BF,
                                                          plan=p, name=f"ffn_out_dx{layer}"))
    G[("ffn_w_out", layer)] = hosted(C, f"ffn_out_dw{layer}", lambda p: matmul(
        act, dh, mode="tn", tm=1408, tn=1024, tk=L, out_dtype=BF, plan=p, name=f"ffn_out_dw{layer}"))
    dhu, G[("ffn_conv_w", layer)], g_cb = hosted(
        C, f"convact_bwd{layer}", lambda p: convact_bwd(hu, W[("ffn_conv_w", layer)], cb, dact, layer, plan=p))
    dhn = hosted(C, f"ffn_in_dx{layer}", lambda p: matmul(dhu, w_in, mode="nt", tm=1024, tn=1024, tk=2816, plan=p,
                                                         name=f"ffn_in_dx{layer}"))
    G[("ffn_w_in", layer)] = hosted(C, f"ffn_in_dw{layer}", lambda p: matmul(
        hn, dhu, mode="tn", tm=1024, tn=1408, tk=L, out_dtype=BF, plan=p, name=f"ffn_in_dw{layer}"))
    dh2, g_norm = hosted(C, f"rms_ffn_bwd{layer}", lambda p: rms_bwd(h, g_row, [dhn], dh, f"rms_ffn_bwd{layer}", plan=p))
    return dh2, g_cb, g_norm


def local_step(x, pos, tgt, sm, W, C=None):
    G = C.grads if C is not None else {}
    nm, nf = sm["norm_mix"], sm["norm_ffn"]
    invf = _invf_lanes()
    are = sm["s5_A_re"].reshape(NST, 1)
    aim = sm["s5_A_im"].reshape(NST, 1)
    ldt = sm["s5_log_dt"].reshape(1, 32)
    bre = sm["s5_B_re"].reshape(NST, 16)
    bim = sm["s5_B_im"].reshape(NST, 16)
    cre = jnp.swapaxes(sm["s5_C_re"][0], 1, 2).reshape(NST, 16)
    cim = jnp.swapaxes(sm["s5_C_im"][0], 1, 2).reshape(NST, 16)
    drow = sm["s5_D"].reshape(1, S5W)
    wbr, wbi, wcr, wci, abr, abi = s5_params_fwd(are, aim, ldt, bre, bim, cre, cim)
    hn0 = rms_fwd(x, nm[0:1], "rms_mix0")
    cb3 = sm["ffn_conv_b3"]
    proj = hosted(C, "mix_in", lambda p: matmul(hn0, W[("mix_w_in", 0)], mode="nn", tm=1024, tn=1280, tk=1024, plan=p, name="mix_in"))
    xs_re, xs_im, y5 = hosted(C, "s5_scan_fwd", lambda p: s5_scan_fwd(proj, wbr, wbi, wcr, wci, abr, abi, drow, plan=p))
    oa = s5_glu_fwd(y5, W[("s5_glu_w", 0)], sm["s5_glu_b"])
    ob, ssave = hosted(C, "hgrn_fwd", lambda p: hgrn_fwd(proj, sm["hgrn_gamma"], sm["hgrn_norm"], plan=p))
    cat = jnp.concatenate([oa, ob], axis=1)
    h1 = matmul(cat, W[("mix_w_out", 0)], mode="nn", tm=1024, tn=1024, tk=1024, add=x, name="mix_out")
    h2, ffn0 = _ffn_fwd(h1, nf[0:1], W, cb3, 0, C)
    hn2_g = rms_fwd_classes(h2, nm[1:2], "rms_mix1")
    wqkv = W[("att_w_qkv", 0)]
    pos_g, qkv_g, oc_g, lc_g = [], [], [], []
    for g, dil in enumerate(DILS):
        pos_g.append(deinterleave(pos, dil))
        qkv_g.append(hosted(C, f"att_qkv{g}", lambda p: matmul(
            hn2_g[g], wqkv, mode="nn", tm=1024, tn=512, tk=1024, dims=(L, 1536, D),
            b_spec=pl.BlockSpec((D, 512), lambda i, j, k, g=g: (0, 3 * j + g)), plan=p, name=f"att_qkv{g}")))
        o_c, l_c = hosted(C, f"attn_fwd{g}", lambda p: attn_fwd(qkv_g[g], pos_g[g], invf, g, plan=p))
        oc_g.append(o_c)
        lc_g.append(l_c)
    o = hosted(C, "attn_merge_fwd", lambda p: attn_merge_fwd(oc_g[0], lc_g[0], oc_g[1:], lc_g[1:], plan=p))
    h3 = matmul(o, W[("att_w_o", 0)], mode="nn", tm=1024, tn=1024, tk=512, add=h2, name="att_o")
    h4, ffn1 = _ffn_fwd(h3, nf[1:2], W, cb3, 1, C)
    loss, dh, g_nfinal = loss_head(h4, sm["norm_final"].reshape(1, D), tgt)
    dh, g_cb1, g_nf1 = _ffn_bwd(dh, h3, nf[1:2], W, cb3, ffn1, 1, C, G)
    do = matmul(dh, W[("att_w_o", 0)], mode="nt", tm=1024, tn=512, tk=1024, name="att_o_dx")
    G[("att_w_o", 0)] = matmul(o, dh, mode="tn", tm=512, tn=1024, tk=L, out_dtype=BF, name="att_o_dw")
    do_g, dl_g = hosted(C, "attn_merge_bwd", lambda p: attn_merge_bwd(oc_g[0], lc_g[0], oc_g[1:], lc_g[1:], do, plan=p))
    dhn2_g, gq = [], []
    for g, dil in enumerate(DILS):
        d3 = hosted(C, f"attn_bwd{g}", lambda p: attn_bwd(qkv_g[g], pos_g[g], invf, lc_g[g], do_g[g], dl_g[g], g, plan=p))
        dx = matmul(d3, wqkv, mode="nt", tm=1024, tn=1024, tk=512, dims=(L, D, 1536),
                    a_spec=pl.BlockSpec((None, 1024, 512), lambda i, j, k: (k, i, 0)),
                    b_spec=pl.BlockSpec((D, 512), lambda i, j, k, g=g: (0, 3 * k + g)), name=f"att_qkv_dx{g}")
        dhn2_g.append(dx)
        gq.append(matmul(hn2_g[g], d3, mode="tn", tm=1024, tn=512, tk=L, out_dtype=BF, dims=(D, 1536, L),
                         b_spec=pl.BlockSpec((None, L, 512), lambda i, j, k: (j, k, 0)), name=f"att_qkv_dw{g}"))
    G[("att_w_qkv", 0)] = jnp.concatenate([gq[g][:, 512 * s:512 * (s + 1)] for s in range(3) for g in range(3)], axis=1)
    dh, g_nm1 = hosted(C, "rms_mix_bwd1", lambda p: rms_bwd_classes(h2, nm[1:2], dhn2_g[0], dhn2_g[1:], dh, "rms_mix_bwd1", plan=p))
    dh, g_cb0, g_nf0 = _ffn_bwd(dh, h1, nf[0:1], W, cb3, ffn0, 0, C, G)
    dmix = matmul(dh, W[("mix_w_out", 0)], mode="nt", tm=1024, tn=1024, tk=1024, name="mix_out_dx")
    G[("mix_w_out", 0)] = matmul(cat, dh, mode="tn", tm=1024, tn=1024, tk=L, out_dtype=BF, name="mix_out_dw")
    dy5, g_glu_w, g_glu_b = s5_glu_bwd(y5, W[("s5_glu_w", 0)], sm["s5_glu_b"], dmix)
    G[("s5_glu_w", 0)] = g_glu_w.astype(BF)
    du, gwbr, gwbi, gwcr, gwci, gabr, gabi, g_d = hosted(C, "s5_scan_bwd", lambda p: s5_scan_bwd(
        dy5, proj, xs_re, xs_im, wbr, wbi, wcr, wci, abr, abi, drow, plan=p))
    g_are, g_aim, g_ldt, g_bre, g_bim, g_cre, g_cim = s5_params_bwd(are, aim, ldt, bre, bim, cre, cim,
                                                                   (gwbr, gwbi, gwcr, gwci, gabr, gabi))
    small = {
        "norm_ffn": jnp.concatenate([g_nf0, g_nf1], axis=0), "norm_final": g_nfinal.reshape(D),
        "s5_A_re": g_are.reshape(1, 32, 64), "s5_A_im": g_aim.reshape(1, 32, 64), "s5_log_dt": g_ldt.reshape(1, 32),
        "s5_B_re": g_bre.reshape(1, 32, 64, 16), "s5_B_im": g_bim.reshape(1, 32, 64, 16),
        "s5_C_re": jnp.swapaxes(g_cre.reshape(1, 32, 64, 16), 2, 3), "s5_C_im": jnp.swapaxes(g_cim.reshape(1, 32, 64, 16), 2, 3),
        "s5_D": g_d.reshape(1, 32, 16), "s5_glu_b": g_glu_b, "ffn_conv_b": jnp.concatenate([g_cb0, g_cb1], axis=0),
    }
    if C is not None:
        C.small["small_early"] = _pack(small, SMALL_EARLY)
    dproj, g_gamma, g_hnorm = hosted(C, "hgrn_bwd", lambda p: hgrn_bwd(proj, sm["hgrn_gamma"], sm["hgrn_norm"], ssave, dmix, du,
                                                                       plan=p))
    dhn0 = hosted(C, "mix_in_dx", lambda p: matmul(dproj, W[("mix_w_in", 0)], mode="nt", tm=1024, tn=1024, tk=2560, plan=p,
                                                  name="mix_in_dx"))
    G[("mix_w_in", 0)] = matmul(hn0, dproj, mode="tn", tm=1024, tn=1280, tk=L, out_dtype=BF, name="mix_in_dw")
    gx, g_nm0 = hosted(C, "rms_mix_bwd0", lambda p: rms_bwd(x, nm[0:1], [dhn0], dh, "rms_mix_bwd0", plan=p))
    small.update({"norm_mix": jnp.concatenate([g_nm0, g_nm1], axis=0), "hgrn_gamma": g_gamma, "hgrn_norm": g_hnorm})
    if C is not None:
        C.small["small_late"] = _pack(small, SMALL_LATE)
    return loss, gx, G, small


BIG = ("mix_w_in", "mix_w_out", "s5_glu_w", "att_w_qkv", "att_w_o", "ffn_w_in", "ffn_w_out", "ffn_conv_w")
SMALL = ("norm_mix", "norm_ffn", "norm_final", "s5_A_re", "s5_A_im", "s5_log_dt", "s5_B_re", "s5_B_im", "s5_C_re", "s5_C_im",
         "s5_D", "s5_glu_b", "hgrn_gamma", "hgrn_norm", "ffn_conv_b")
SMALL_LATE = ("norm_mix", "hgrn_gamma", "hgrn_norm")
SMALL_EARLY = tuple(n for n in SMALL if n not in SMALL_LATE)


def cast_bf16(w, name, plan=None):
    nl, r, c = w.shape
    w2 = w.reshape(nl * r, c)
    tr = 256 if (nl * r) % 256 == 0 else nl * r

    def body(w_ref, o_ref):
        o_ref[...] = w_ref[...].astype(BF)

    out = pcall(body, plan, grid=(nl * r // tr,), in_specs=[pl.BlockSpec((tr, c), lambda i: (i, 0))],
                out_specs=pl.BlockSpec((tr, c), lambda i: (i, 0)), out_shape=S((nl * r, c), BF),
                sem=("parallel",), name=name, args=[w2])
    return out.reshape(nl, r, c)


SCHEDULE = {
    "cast_ffn_w_in": [("G", "mix_w_in", 0)],
    "mix_in": [("G", "mix_w_out", 0), ("G", "s5_glu_w", 0)],
    "s5_scan_fwd": [("G", "ffn_w_in", 0, (0, 2))],
    "hgrn_fwd": [("G", "ffn_w_in", 0, (1, 2)), ("G", "ffn_conv_w", 0), ("G", "ffn_conv_w", 1), ("G", "att_w_qkv", 0, (0, 2))],
    "ffn_in0": [("G", "ffn_w_out", 0)],
    "convact_fwd0": [("G", "att_w_qkv", 0, (1, 2))],
    "att_qkv0": [("G", "att_w_o", 0)],
    "attn_fwd0": [("G", "ffn_w_in", 1, (0, 2))],
    "attn_fwd1": [("G", "ffn_w_in", 1, (1, 2))],
    "attn_fwd2": [("G", "ffn_w_out", 1)],
    "convact_bwd1": [("P", "ffn_w_out", 1)],
    "ffn_in_dx1": [("A", "ffn_w_out", 1, (0, 2))],
    "ffn_in_dw1": [("A", "ffn_w_out", 1, (1, 2))],
    "rms_ffn_bwd1": [("P", "ffn_w_in", 1)],
    "attn_merge_bwd": [("P", "att_w_o", 0), ("A", "ffn_conv_w", 1), ("B", "ffn_w_out", 1)],
    "attn_bwd0": [("A", "ffn_w_in", 1, (0, 2)), ("A", "att_w_o", 0)],
    "attn_bwd1": [("A", "ffn_w_in", 1, (1, 2)), ("B", "att_w_o", 0), ("B", "ffn_conv_w", 1)],
    "attn_bwd2": [("B", "ffn_w_in", 1)],
    "rms_mix_bwd1": [("P", "att_w_qkv", 0)],
    "ffn_out_dx0": [("A", "att_w_qkv", 0, (0, 4))],
    "ffn_out_dw0": [("A", "att_w_qkv", 0, (1, 4))],
    "convact_bwd0": [("A", "att_w_qkv", 0, (2, 4)), ("A", "att_w_qkv", 0, (3, 4)), ("P", "ffn_w_out", 0)],
    "ffn_in_dx0": [("A", "ffn_w_out", 0, (0, 2)), ("B", "att_w_qkv", 0)],
    "ffn_in_dw0": [("A", "ffn_w_out", 0, (1, 2))],
    "rms_ffn_bwd0": [("P", "ffn_w_in", 0), ("B", "ffn_w_out", 0)],
    "s5_scan_bwd": [("A", "ffn_w_in", 0, (0, 2)), ("P", "mix_w_out", 0), ("P", "s5_glu_w", 0), ("A", "ffn_conv_w", 0)],
    "hgrn_bwd": [("A", "ffn_w_in", 0, (1, 2)), ("A", "mix_w_out", 0), ("A", "s5_glu_w", 0), ("B", "ffn_conv_w", 0),
                 ("A", "small_early", 0)],
    "mix_in_dx": [("B", "ffn_w_in", 0), ("B", "mix_w_out", 0), ("B", "s5_glu_w", 0), ("B", "small_early", 0)],
    "rms_mix_bwd0": [("P", "mix_w_in", 0)],
    "adam_att_w_o": [("A", "mix_w_in", 0), ("A", "small_late", 0)],
    "adam_s5_glu_w": [("B", "mix_w_in", 0), ("B", "small_late", 0)],
}


class Comm:
    def __init__(self, shards, shapes):
        self.shards, self.shapes = shards, shapes
        self.W, self.grads, self.slots = {}, {}, {}
        self.sib, self.pair = {}, {}
        self.small = {}

    def plan(self, host):
        items = SCHEDULE.get(host)
        if not items:
            return None
        p = Plan()
        for it in items:
            kind, name, l = it[:3]
            part, parts = it[3] if len(it) > 3 else (0, 1)
            if name.startswith("small"):
                sg = self.small[name]
                kdst = p.buf("slots:" + name, arr=self.slots.get(name), shape=S((8,) + sg.shape, f32), write=True)
                if kind == "A":
                    ReduceOp(p, p.buf("g:" + name, arr=sg), kdst, None, sg.shape, False, 0, 0, whole=True)
                else:
                    ForwardOp(p, kdst, None, whole=True)
                continue
            nl, R, C_ = self.shapes[name]
            rows = name in ROW_SHARDED
            r0, nr = part * (R // parts), R // parts
            if kind == "G":
                sh = self.shards[name]
                kdst = p.buf(f"W:{name}:{l}", arr=self.W.get((name, l)), shape=S((4 * R, C_) if rows else (R, 4 * C_), sh.dtype),
                             write=True)
                GatherOp(p, p.buf("shard:" + name, arr=sh), kdst, l, self.shapes[name], rows, r0, nr, split=(nr % 32 == 0))
            elif name == "ffn_conv_w":
                g = self.grads[(name, l)]
                kdst = p.buf("slots:" + name, arr=self.slots.get(name), shape=S((8, nl, R, C_), g.dtype), write=True)
                if kind == "A":
                    ReduceOp(p, p.buf(f"g:{name}:{l}", arr=g), kdst, l, self.shapes[name], rows, r0, nr)
                else:
                    ForwardOp(p, kdst, l)
            elif kind == "P":
                g = self.grads[(name, l)]
                ksib = p.buf(f"sib:{name}:{l}", shape=S((4 * R // 2, C_) if rows else (R // 2, 4 * C_), g.dtype), write=True)
                PairOp(p, p.buf(f"g:{name}:{l}", arr=g), ksib, self.shapes[name], rows)
            else:
                if (name, l) not in self.pair:
                    self.pair[(name, l)] = pair_sum(self.grads[(name, l)], self.sib[(name, l)], rows, R, f"pair_sum_{name}{l}")
                h = self.pair[(name, l)]
                kdst = p.buf("slots:" + name, arr=self.slots.get(name), shape=S((4, nl, R, C_), h.dtype), write=True)
                if kind == "A":
                    ReduceOp(p, p.buf(f"h:{name}:{l}", arr=h), kdst, l, self.shapes[name], rows, r0 // 2, nr // 2, half=True)
                else:
                    HalfForwardOp(p, kdst, l, self.shapes[name])
        return p

    def done(self, p):
        for k, arr in p.out.items():
            tag, name = k.split(":")[:2]
            if tag == "W":
                self.W[(name, int(k.split(":")[2]))] = arr
            elif tag == "sib":
                self.sib[(name, int(k.split(":")[2]))] = arr
            else:
                self.slots[name] = arr


def _adamw(w, g, m, v):
    m = B1 * m + (1.0 - B1) * g
    v = B2 * v + (1.0 - B2) * jnp.square(g)
    m_hat = m / (1.0 - B1 ** STEP)
    v_hat = v / (1.0 - B2 ** STEP)
    return -LR * (m_hat / (jnp.sqrt(v_hat) + AEPS) + WD * w), m, v


def adam_big(w, m, v, slots, name, plan=None):
    nl, R, C = w.shape
    ns = slots.shape[0]
    tr = 128 if R % 128 == 0 else (64 if R % 64 == 0 else R)

    def body(w_ref, m_ref, v_ref, s_ref, g_ref, d_ref, nm_ref, nv_ref):
        g = s_ref[0].astype(f32)
        for s in range(1, ns):
            g = g + s_ref[s].astype(f32)
        d, nm_, nv_ = _adamw(w_ref[...], g, m_ref[...], v_ref[...])
        g_ref[...] = g
        d_ref[...] = d
        nm_ref[...] = nm_
        nv_ref[...] = nv_

    blk = pl.BlockSpec((None, tr, C), lambda l, i: (l, i, 0))
    return pcall(body, plan, grid=(nl, R // tr),
                 in_specs=[blk, blk, blk, pl.BlockSpec((ns, None, tr, C), lambda l, i: (0, l, i, 0))],
                 out_specs=[blk] * 4, out_shape=[S((nl, R, C), f32)] * 4,
                 sem=("parallel", "parallel"), name=name, args=[w, m, v, slots])


def sum_slots(slots, name):
    R = slots.shape[1]

    def body(s_ref, g_ref):
        g = s_ref[0]
        for s in range(1, 8):
            g = g + s_ref[s]
        g_ref[...] = g

    return pl.pallas_call(
        body, grid=(R // 256,), in_specs=[pl.BlockSpec((8, 256, 128), lambda i: (0, i, 0))],
        out_specs=pl.BlockSpec((256, 128), lambda i: (i, 0)), out_shape=S((R, 128), f32),
        compiler_params=_cp(("parallel",)), name=name)(slots)


SMALL2D = {"norm_mix": (2, 1024), "norm_ffn": (2, 1024), "norm_final": (1, 1024), "s5_A_re": (32, 64), "s5_A_im": (32, 64),
           "s5_log_dt": (1, 32), "s5_B_re": (2048, 16), "s5_B_im": (2048, 16), "s5_C_re": (512, 64), "s5_C_im": (512, 64),
           "s5_D": (32, 16), "s5_glu_b": (1, 512), "hgrn_gamma": (2, 512), "hgrn_norm": (1, 512), "ffn_conv_b": (2, 5632)}


def adam_small(w, m, v, g, names, name):
    n = len(names)

    def body(*refs):
        for i in range(n):
            w_ref, m_ref, v_ref, g_ref = refs[4 * i:4 * i + 4]
            d_ref, nm_ref, nv_ref = refs[4 * n + 3 * i:4 * n + 3 * i + 3]
            d, nm_, nv_ = _adamw(w_ref[...], g_ref[...], m_ref[...], v_ref[...])
            d_ref[...] = d
            nm_ref[...] = nm_
            nv_ref[...] = nv_

    args = [t[k] for k in names for t in (w, m, v, g)]
    outs = pl.pallas_call(body, out_shape=[S(SMALL2D[k], f32) for k in names for _ in range(3)],
                          compiler_params=_cp(), name=name)(*args)
    return {k: tuple(outs[3 * i:3 * i + 3]) for i, k in enumerate(names)}


def _pack(d, names):
    flat = jnp.concatenate([d[n].reshape(-1) for n in names])
    n = flat.shape[0]
    rows = -(-n // (256 * 128)) * 256
    return jnp.pad(flat, (0, rows * 128 - n)).reshape(rows, 128)


def _unpack(p, like, names):
    flat = p.reshape(-1)
    out, off = {}, 0
    for n in names:
        sz = math.prod(like[n].shape)
        out[n] = flat[off:off + sz].reshape(like[n].shape)
        off += sz
    return out


def kernel(x, positions, norm_mix, norm_ffn, norm_final, mix_w_in, mix_w_out, s5_A_re, s5_A_im, s5_log_dt, s5_B_re, s5_B_im, s5_C_re, s5_C_im, s5_D, s5_glu_w, s5_glu_b, hgrn_gamma, hgrn_norm, att_w_qkv, att_w_o, ffn_w_in, ffn_conv_w, ffn_conv_b, ffn_w_out, loss_target, m_norm_mix, m_norm_ffn, m_norm_final, m_mix_w_in, m_mix_w_out, m_s5_A_re, m_s5_A_im, m_s5_log_dt, m_s5_B_re, m_s5_B_im, m_s5_C_re, m_s5_C_im, m_s5_D, m_s5_glu_w, m_s5_glu_b, m_hgrn_gamma, m_hgrn_norm, m_att_w_qkv, m_att_w_o, m_ffn_w_in, m_ffn_conv_w, m_ffn_conv_b, m_ffn_w_out, v_norm_mix, v_norm_ffn, v_norm_final, v_mix_w_in, v_mix_w_out, v_s5_A_re, v_s5_A_im, v_s5_log_dt, v_s5_B_re, v_s5_B_im, v_s5_C_re, v_s5_C_im, v_s5_D, v_s5_glu_w, v_s5_glu_b, v_hgrn_gamma, v_hgrn_norm, v_att_w_qkv, v_att_w_o, v_ffn_w_in, v_ffn_conv_w, v_ffn_conv_b, v_ffn_w_out):
    a = dict(locals())
    weights = BIG + SMALL
    w = {n: a[n] for n in weights}
    m = {n: a["m_" + n] for n in weights}
    v = {n: a["v_" + n] for n in weights}
    shards = {"ffn_conv_w": ffn_conv_w}
    C = Comm(shards, {n: w[n].shape for n in BIG})
    for n in ("mix_w_in", "ffn_w_in", "mix_w_out", "s5_glu_w", "ffn_w_out", "att_w_qkv", "att_w_o"):
        shards[n] = hosted(C, "cast_" + n, lambda p: cast_bf16(w[n], "cast_" + n, plan=p))
    sm = {n: w[n] for n in SMALL}
    sm["ffn_conv_b3"] = ffn_conv_b.reshape(2, 1, 2 * DFF)
    loss, gx, _, _ = local_step(x[0], positions.reshape(L, 1), loss_target[0], sm, C.W, C)
    res = {}
    for n in ("att_w_o", "s5_glu_w", "ffn_w_in", "ffn_w_out", "att_w_qkv", "mix_w_out", "ffn_conv_w", "mix_w_in"):
        res[n] = hosted(C, "adam_" + n, lambda p: adam_big(w[n], m[n], v[n], C.slots[n], "adam_" + n, plan=p))
    for names, key in ((SMALL_EARLY, "small_early"), (SMALL_LATE, "small_late")):
        g = _unpack(sum_slots(C.slots[key], "sum_" + key), w, names)

        def two_d(t):
            return {n: t[n].reshape(SMALL2D[n]) for n in names}

        upd = adam_small(two_d(w), two_d(m), two_d(v), two_d(g), names, "adam_" + key)
        for n in names:
            res[n] = (g[n],) + tuple(t.reshape(w[n].shape) for t in upd[n])
    total = lax.psum(loss[0, 0], ("x", "y", "c"))
    order = ("norm_mix", "norm_ffn", "norm_final", "mix_w_in", "mix_w_out", "s5_A_re", "s5_A_im", "s5_log_dt", "s5_B_re", "s5_B_im",
             "s5_C_re", "s5_C_im", "s5_D", "s5_glu_w", "s5_glu_b", "hgrn_gamma", "hgrn_norm", "att_w_qkv", "att_w_o", "ffn_w_in",
             "ffn_conv_w", "ffn_conv_b", "ffn_w_out")
    return (total, gx[None], *[res[n][0] for n in order], *[res[n][1] for n in order], *[res[n][2] for n in order],
            *[res[n][3] for n in order])
```

```python
import functools
import math

import numpy as np
import jax
import jax.numpy as jnp
from jax import lax
from jax.experimental import pallas as pl
from jax.experimental.pallas import tpu as pltpu

f32 = jnp.float32
BF = jnp.bfloat16
HI = lax.Precision.HIGHEST
S = jax.ShapeDtypeStruct
MESH = pl.DeviceIdType.MESH

L = 2048
D = 1024
EPS = 1e-6
S5W = 512
NST = 2048
HGC = 64
HGB = 32
DFF = 2816
ROPE_THETA = 500000.0
LR, B1, B2, AEPS, WD, STEP = 0.001, 0.9, 0.999, 1e-08, 0.01, 10
VMEM_LIMIT = 56 * 1024 * 1024


def _cp(sem=None):
    return pltpu.CompilerParams(dimension_semantics=sem, vmem_limit_bytes=VMEM_LIMIT)


ANY = pl.BlockSpec(memory_space=pl.ANY)
ROW_SHARDED = ("mix_w_out", "s5_glu_w", "ffn_w_out")


def _coords():
    x, y, c = lax.axis_index("x"), lax.axis_index("y"), lax.axis_index("c")
    return x, y, c, 2 * x + y, [(1 - x, y), (x, 1 - y), (1 - x, 1 - y)]


def _rows(start, n):
    return pl.ds(start if isinstance(start, int) else pl.multiple_of(start, 8), n)


def _cols(q, n):
    return pl.ds(pl.multiple_of(q * n, 128), n)


class Plan:
    def __init__(self):
        self.bufs, self.ops, self.nsem, self.out = {}, [], 0, {}

    def buf(self, key, arr=None, shape=None, write=False):
        b = self.bufs.setdefault(key, dict(arr=arr, shape=shape, write=False))
        b["write"] = b["write"] or write
        return key

    def add(self, op):
        op.base = self.nsem
        self.nsem += op.nsem
        self.ops.append(op)


class GatherOp:
    nsem = 13

    def __init__(self, plan, ksrc, kdst, l, shard_shape, rows, r0, nr, split):
        self.ksrc, self.kdst, self.l, (_, self.R, self.C), self.rows, self.r0, self.nr, self.split = (
            ksrc, kdst, l, shard_shape, rows, r0, nr, split)
        self.h = nr // 2 if split else nr
        plan.add(self)

    def _dst(self, R_, q, start, n):
        if self.rows:
            return R_[self.kdst].at[_rows(q * self.R + start, n), :]
        return R_[self.kdst].at[_rows(start, n), _cols(q, self.C)]

    def _mine(self, c):
        return self.r0 + (c * self.h if self.split else 0)

    def _theirs(self, c):
        return self.r0 + ((1 - c) * self.h if self.split else 0)

    def _copies(self, R_, sems):
        x, y, c, me, others = _coords()
        src = R_[self.ksrc]
        local = pltpu.make_async_copy(src.at[self.l, _rows(self.r0, self.nr), :], self._dst(R_, me, self.r0, self.nr),
                                      sems.at[self.base + 12])
        send, fwd = [], []
        for k, (px, py) in enumerate(others):
            q = 2 * px + py
            send.append((
                pltpu.make_async_remote_copy(src.at[self.l, _rows(self._mine(c), self.h), :], self._dst(R_, me, self._mine(c), self.h),
                                             sems.at[self.base + k], sems.at[self.base + 3 + k], device_id=(px, py, c), device_id_type=MESH),
                pltpu.make_async_remote_copy(src.at[self.l, _rows(self._mine(c), self.h), :], self._dst(R_, q, self._mine(c), self.h),
                                             sems.at[self.base + k], sems.at[self.base + 3 + k], device_id=(px, py, c), device_id_type=MESH)))
            fwd.append((
                pltpu.make_async_remote_copy(self._dst(R_, q, self._mine(c), self.h), self._dst(R_, q, self._mine(c), self.h),
                                             sems.at[self.base + 6 + k], sems.at[self.base + 9 + k], device_id=(x, y, 1 - c), device_id_type=MESH),
                pltpu.make_async_remote_copy(self._dst(R_, q, self._theirs(c), self.h), self._dst(R_, q, self._theirs(c), self.h),
                                             sems.at[self.base + 6 + k], sems.at[self.base + 9 + k], device_id=(x, y, 1 - c), device_id_type=MESH)))
        return local, send, fwd

    def start(self, R_, sems):
        local, send, _ = self._copies(R_, sems)
        local.start()
        for out, _ in send:
            out.start()

    def finish(self, R_, sems):
        local, send, fwd = self._copies(R_, sems)
        for k in range(3):
            send[k][1].wait_recv()
            if self.split:
                fwd[k][0].start()
        for k in range(3):
            if self.split:
                fwd[k][1].wait_recv()
                fwd[k][0].wait_send()
            send[k][0].wait_send()
        local.wait()


class ReduceOp:
    nsem = 7

    def __init__(self, plan, ksrc, kdst, l, shard_shape, rows, r0, nr, whole=False, half=False):
        self.ksrc, self.kdst, self.l, (self.R, self.C), self.rows, self.r0, self.nr, self.whole, self.half = (
            ksrc, kdst, l, shard_shape[-2:], rows, r0, nr, whole, half)
        plan.add(self)

    def _piece(self, R_, q):
        g = R_[self.ksrc]
        if self.whole:
            return g
        if self.rows:
            return g.at[_rows(q * (self.R // 2 if self.half else self.R) + self.r0, self.nr), :]
        return g.at[_rows(self.r0, self.nr), _cols(q, self.C)]

    def _slot(self, R_, q, c):
        if self.whole:
            return R_[self.kdst].at[2 * q + c]
        if self.half:
            return R_[self.kdst].at[q, self.l, _rows(c * (self.R // 2) + self.r0, self.nr), :]
        return R_[self.kdst].at[2 * q + c, self.l, _rows(self.r0, self.nr), :]

    def _copies(self, R_, sems):
        x, y, c, me, others = _coords()
        local = pltpu.make_async_copy(self._piece(R_, me), self._slot(R_, me, c), sems.at[self.base + 6])
        send = []
        for k, (px, py) in enumerate(others):
            q = 2 * px + py
            send.append((
                pltpu.make_async_remote_copy(self._piece(R_, q), self._slot(R_, me, c), sems.at[self.base + k],
                                             sems.at[self.base + 3 + k], device_id=(px, py, c), device_id_type=MESH),
                pltpu.make_async_remote_copy(self._piece(R_, q), self._slot(R_, q, c), sems.at[self.base + k],
                                             sems.at[self.base + 3 + k], device_id=(px, py, c), device_id_type=MESH)))
        return local, send

    def start(self, R_, sems):
        local, send = self._copies(R_, sems)
        local.start()
        for out, _ in send:
            out.start()

    def finish(self, R_, sems):
        local, send = self._copies(R_, sems)
        local.wait()
        for out, inn in send:
            inn.wait_recv()
            out.wait_send()


class ForwardOp:
    nsem = 8

    def __init__(self, plan, kdst, l, whole=False):
        self.kdst, self.l, self.whole = kdst, l, whole
        plan.add(self)

    def _slot(self, R_, s):
        return R_[self.kdst].at[s] if self.whole else R_[self.kdst].at[s, self.l]

    def _copies(self, R_, sems):
        x, y, c, me, others = _coords()
        return [(pltpu.make_async_remote_copy(self._slot(R_, 2 * q + c), self._slot(R_, 2 * q + c), sems.at[self.base + q],
                                              sems.at[self.base + 4 + q], device_id=(x, y, 1 - c), device_id_type=MESH),
                 pltpu.make_async_remote_copy(self._slot(R_, 2 * q + 1 - c), self._slot(R_, 2 * q + 1 - c), sems.at[self.base + q],
                                              sems.at[self.base + 4 + q], device_id=(x, y, 1 - c), device_id_type=MESH))
                for q in range(4)]

    def start(self, R_, sems):
        for out, _ in self._copies(R_, sems):
            out.start()

    def finish(self, R_, sems):
        for out, inn in self._copies(R_, sems):
            inn.wait_recv()
            out.wait_send()


class PairOp:
    nsem = 8

    def __init__(self, plan, ksrc, kdst, shard_shape, rows):
        self.ksrc, self.kdst, (self.R, self.C), self.rows = ksrc, kdst, shard_shape[-2:], rows
        plan.add(self)

    def _copies(self, R_, sems):
        x, y, c, me, others = _coords()
        g, dst, h = R_[self.ksrc], R_[self.kdst], self.R // 2
        out = []
        for q in range(4 if self.rows else 1):
            src = g.at[_rows(q * self.R + (1 - c) * h, h), :]
            land = dst.at[_rows(q * h, h), :]
            out.append(pltpu.make_async_remote_copy(src, land, sems.at[self.base + q], sems.at[self.base + 4 + q],
                                                    device_id=(x, y, 1 - c), device_id_type=MESH))
        return out

    def start(self, R_, sems):
        for cp in self._copies(R_, sems):
            cp.start()

    def finish(self, R_, sems):
        for cp in self._copies(R_, sems):
            cp.wait_recv()
            cp.wait_send()


class HalfForwardOp:
    nsem = 2

    def __init__(self, plan, kdst, l, shard_shape):
        self.kdst, self.l, self.R = kdst, l, shard_shape[-2]
        plan.add(self)

    def _copy(self, R_, sems, core):
        x, y, c, me, others = _coords()
        part = R_[self.kdst].at[:, self.l, _rows((c if core == "mine" else 1 - c) * (self.R // 2), self.R // 2), :]
        return pltpu.make_async_remote_copy(part, part, sems.at[self.base], sems.at[self.base + 1],
                                            device_id=(x, y, 1 - c), device_id_type=MESH)

    def start(self, R_, sems):
        self._copy(R_, sems, "mine").start()

    def finish(self, R_, sems):
        self._copy(R_, sems, "theirs").wait_recv()
        self._copy(R_, sems, "mine").wait_send()


def pair_sum(g, gsib, rows, R, name):
    h = R // 2
    W = g.shape[1]
    tr = h if h * W * 2 <= 2 ** 21 else 128
    nq = 4 if rows else 1

    def body(c_ref, a_ref, b_ref, o_ref):
        o_ref[...] = (a_ref[...].astype(f32) + b_ref[...].astype(f32)).astype(o_ref.dtype)

    half = pl.BlockSpec((tr, W), lambda q, i, c_ref: (q * (h // tr) + i, 0))
    mine = pl.BlockSpec((tr, W), lambda q, i, c_ref: (q * (R // tr) + c_ref[0] * (h // tr) + i, 0))
    return pl.pallas_call(
        body, grid_spec=pltpu.PrefetchScalarGridSpec(num_scalar_prefetch=1, grid=(nq, h // tr), in_specs=[mine, half],
                                                     out_specs=half),
        out_shape=S(gsib.shape, g.dtype), compiler_params=_cp(("parallel", "parallel")),
        name=name)(lax.axis_index("c").reshape(1).astype(jnp.int32), g, gsib)


def pcall(body, plan, *, grid, in_specs, out_specs, out_shape, scratch_shapes=(), sem, name, args):
    multi = isinstance(out_shape, (list, tuple))
    if plan is None or not plan.ops:
        return pl.pallas_call(body, grid=grid, in_specs=in_specs, out_specs=out_specs, out_shape=out_shape,
                              scratch_shapes=list(scratch_shapes), compiler_params=_cp(sem), name=name)(*args)
    outs = list(out_shape) if multi else [out_shape]
    ospecs = list(out_specs) if multi else [out_specs]
    kin = [k for k, b in plan.bufs.items() if b["arr"] is not None]
    kout = [k for k, b in plan.bufs.items() if b["write"]]
    n_in, n_out, n_scr = len(in_specs), len(outs), len(scratch_shapes)

    def wrapped(*refs):
        o0 = n_in + len(kin)
        s0 = o0 + n_out + len(kout)
        R_ = dict(zip(kin, refs[n_in:o0]))
        R_.update(zip(kout, refs[o0 + n_out:s0]))
        sems = refs[s0 + n_scr]
        first = functools.reduce(jnp.logical_and, [pl.program_id(d) == 0 for d in range(len(grid))])
        last = functools.reduce(jnp.logical_and, [pl.program_id(d) == grid[d] - 1 for d in range(len(grid))])

        @pl.when(first)
        def _():
            for op in plan.ops:
                op.start(R_, sems)

        body(*refs[:n_in], *refs[o0:o0 + n_out], *refs[s0:s0 + n_scr])

        @pl.when(last)
        def _():
            for op in plan.ops:
                op.finish(R_, sems)

    def shape_of(k):
        b = plan.bufs[k]
        return S(b["arr"].shape, b["arr"].dtype) if b["arr"] is not None else b["shape"]

    res = pl.pallas_call(
        wrapped, grid=grid, in_specs=list(in_specs) + [ANY] * len(kin), out_specs=ospecs + [ANY] * len(kout),
        out_shape=outs + [shape_of(k) for k in kout],
        scratch_shapes=list(scratch_shapes) + [pltpu.SemaphoreType.DMA((plan.nsem,))],
        input_output_aliases={n_in + kin.index(k): n_out + kout.index(k) for k in kout if plan.bufs[k]["arr"] is not None},
        compiler_params=pltpu.CompilerParams(dimension_semantics=("arbitrary",) * len(grid), vmem_limit_bytes=VMEM_LIMIT,
                                             has_side_effects=True),
        name=name)(*args, *[plan.bufs[k]["arr"] for k in kin])
    plan.out = dict(zip(kout, res[n_out:]))
    return list(res[:n_out]) if multi else res[0]


def _dg(a, b, ca, cb):
    return lax.dot_general(a.astype(BF), b.astype(BF), (((ca,), (cb,)), ((), ())), preferred_element_type=f32)


@jax.custom_vjp
def dot_nn(a, b):
    return _dg(a, b, 1, 0)


@jax.custom_vjp
def dot_nt(a, b):
    return _dg(a, b, 1, 1)


@jax.custom_vjp
def dot_tn(a, b):
    return _dg(a, b, 0, 0)


dot_nn.defvjp(lambda a, b: (dot_nn(a, b), (a, b)),
              lambda r, g: (dot_nt(g, r[1]).astype(r[0].dtype), dot_tn(r[0], g).astype(r[1].dtype)))
dot_nt.defvjp(lambda a, b: (dot_nt(a, b), (a, b)),
              lambda r, g: (dot_nn(g, r[1]).astype(r[0].dtype), dot_tn(g, r[0]).astype(r[1].dtype)))
dot_tn.defvjp(lambda a, b: (dot_tn(a, b), (a, b)),
              lambda r, g: (dot_nt(r[1], g).astype(r[0].dtype), dot_nn(r[0], g).astype(r[1].dtype)))


def matmul(a, b, *, mode, tm, tn, tk, out_dtype=f32, add=None, b_lead=None, a_spec=None, b_spec=None, dims=None, plan=None, name):
    a_over, b_over = a_spec, b_spec
    if mode == "nn":
        (M, K), N = a.shape[-2:], b.shape[-1]
        a_spec = pl.BlockSpec((tm, tk), lambda i, j, k: (i, k))
        b_blk, b_idx, ca, cb = (tk, tn), (lambda i, j, k: (k, j)), 1, 0
    elif mode == "nt":
        (M, K), N = a.shape[-2:], b.shape[-2]
        a_spec = pl.BlockSpec((tm, tk), lambda i, j, k: (i, k))
        b_blk, b_idx, ca, cb = (tn, tk), (lambda i, j, k: (j, k)), 1, 1
    else:
        (K, M), N = a.shape[-2:], b.shape[-1]
        a_spec = pl.BlockSpec((tk, tm), lambda i, j, k: (k, i))
        b_blk, b_idx, ca, cb = (tk, tn), (lambda i, j, k: (k, j)), 0, 0
    if dims is not None:
        M, N, K = dims
    assert M % tm == 0 and N % tn == 0 and K % tk == 0, (name, M, N, K, tm, tn, tk)
    if b_lead is None:
        b_spec = pl.BlockSpec(b_blk, b_idx)
    else:
        b_spec = pl.BlockSpec((None,) + b_blk, lambda i, j, k: (b_lead,) + b_idx(i, j, k))
    if a_over is not None:
        a_spec = a_over
    if b_over is not None:
        b_spec = b_over
    nk = K // tk
    has_add = add is not None

    def body(*refs):
        a_ref, b_ref = refs[0], refs[1]
        add_ref = refs[2] if has_add else None
        o_ref = refs[2 + has_add]
        p = _dg(a_ref[...], b_ref[...], ca, cb)

        def fin(v):
            if has_add:
                v = v + add_ref[...].astype(f32)
            o_ref[...] = v.astype(o_ref.dtype)

        if nk == 1:
            fin(p)
        else:
            acc = refs[3 + has_add]
            k = pl.program_id(2)

            @pl.when(k == 0)
            def _():
                acc[...] = p

            @pl.when(k > 0)
            def _():
                acc[...] += p

            @pl.when(k == nk - 1)
            def _():
                fin(acc[...])

    in_specs = [a_spec, b_spec]
    args = [a, b]
    if has_add:
        in_specs.append(pl.BlockSpec((tm, tn), lambda i, j, k: (i, j)))
        args.append(add)
    return pcall(body, plan, grid=(M // tm, N // tn, nk), in_specs=in_specs,
                 out_specs=pl.BlockSpec((tm, tn), lambda i, j, k: (i, j)), out_shape=S((M, N), out_dtype),
                 scratch_shapes=[pltpu.VMEM((tm, tn), f32)] if nk > 1 else [],
                 sem=("parallel", "parallel", "arbitrary"), name=name, args=args)


def _rms(xv, gv):
    return xv * lax.rsqrt(jnp.mean(xv * xv, axis=-1, keepdims=True) + EPS) * gv


TR = 256


def rms_fwd(x, g, name):
    def body(x_ref, g_ref, o_ref):
        o_ref[...] = _rms(x_ref[...], g_ref[...]).astype(o_ref.dtype)

    return pl.pallas_call(
        body, grid=(L // TR,),
        in_specs=[pl.BlockSpec((TR, D), lambda i: (i, 0)), pl.BlockSpec((1, D), lambda i: (0, 0))],
        out_specs=pl.BlockSpec((TR, D), lambda i: (i, 0)), out_shape=S((L, D), BF),
        compiler_params=_cp(("parallel",)), name=name)(x, g)


def rms_bwd(x, g, dys, dres, name, plan=None):
    nd = len(dys)

    def body(*refs):
        x_ref, g_ref = refs[0], refs[1]
        dr_ref, dh_ref, dg_ref = refs[2 + nd:]
        dy = refs[2][...].astype(f32)
        for r in refs[3:2 + nd]:
            dy = dy + r[...].astype(f32)
        _, vjp = jax.vjp(_rms, x_ref[...], g_ref[...])
        dx, dg = vjp(dy)
        dh_ref[...] = dr_ref[...] + dx

        @pl.when(pl.program_id(0) == 0)
        def _():
            dg_ref[...] = jnp.zeros_like(dg_ref)

        dg_ref[...] += dg

    row = pl.BlockSpec((TR, D), lambda i: (i, 0))
    vec = pl.BlockSpec((1, D), lambda i: (0, 0))
    return pcall(body, plan, grid=(L // TR,), in_specs=[row, vec] + [row] * (nd + 1), out_specs=[row, vec],
                 out_shape=[S((L, D), f32), S((1, D), f32)], sem=("arbitrary",), name=name, args=[x, g, *dys, dres])


def loss_head(h, g, tgt):
    def f(hv, gv, tv):
        y = _rms(hv, gv)
        return 0.5 * jnp.sum(jnp.mean(jnp.square(y - tv), axis=-1))

    def body(h_ref, g_ref, t_ref, l_ref, dh_ref, dg_ref):
        val, vjp = jax.vjp(f, h_ref[...], g_ref[...], t_ref[...])
        dh, dg, _ = vjp(jnp.ones((), f32))
        dh_ref[...] = dh

        @pl.when(pl.program_id(0) == 0)
        def _():
            dg_ref[...] = jnp.zeros_like(dg_ref)
            l_ref[...] = jnp.zeros_like(l_ref)

        dg_ref[...] += dg
        l_ref[...] += jnp.full((1, 128), val, f32)

    row = pl.BlockSpec((TR, D), lambda i: (i, 0))
    vec = pl.BlockSpec((1, D), lambda i: (0, 0))
    return pl.pallas_call(
        body, grid=(L // TR,), in_specs=[row, vec, row],
        out_specs=[pl.BlockSpec((1, 128), lambda i: (0, 0)), row, vec],
        out_shape=[S((1, 128), f32), S((L, D), f32), S((1, D), f32)],
        compiler_params=_cp(("arbitrary",)), name="loss_head")(h, g, tgt)


def _col_to_row(c):
    n = c.shape[0]
    t = jnp.broadcast_to(c, (n, 128)).T
    r = lax.broadcasted_iota(jnp.int32, (128, n), 0)
    return jnp.sum(jnp.where(r == 0, t, 0.0), axis=0, keepdims=True)


def _s5_param_map(are, aim, ldt_row, bre, bim, cre, cim):
    n = NST
    gi = lax.broadcasted_iota(jnp.int32, (n, 32), 0) // 64
    gj = lax.broadcasted_iota(jnp.int32, (n, 32), 1)
    ldt = jnp.sum(jnp.where(gi == gj, ldt_row, 0.0), axis=1, keepdims=True)
    dt = jnp.exp(ldt)
    mag = jnp.exp(are * dt)
    abr = mag * jnp.cos(aim * dt)
    abi = mag * jnp.sin(aim * dt)
    den = are * are + aim * aim
    nr, ni = abr - 1.0, abi
    cr = (nr * are + ni * aim) / den
    ci = (ni * are - nr * aim) / den
    bbr = cr * bre - ci * bim
    bbi = cr * bim + ci * bre
    tc = lax.broadcasted_iota(jnp.int32, (16, 128), 0)
    tl = lax.broadcasted_iota(jnp.int32, (16, 128), 1)
    T = (tl % 16 == tc).astype(f32)
    mr = (lax.broadcasted_iota(jnp.int32, (n, 128), 0) // 64) % 8
    mc = lax.broadcasted_iota(jnp.int32, (n, 128), 1) // 16
    mask = (mr == mc).astype(f32)

    def expand(v):
        return jnp.dot(v, T, precision=HI, preferred_element_type=f32) * mask

    return expand(bbr), expand(bbi), expand(cre), expand(cim), _col_to_row(abr), _col_to_row(abi)


def s5_params_fwd(are, aim, ldt_row, bre, bim, cre, cim):
    def body(*refs):
        outs = _s5_param_map(*[r[...] for r in refs[:7]])
        for o_ref, o in zip(refs[7:], outs):
            o_ref[...] = o

    return pl.pallas_call(
        body, out_shape=[S((NST, 128), f32)] * 4 + [S((1, NST), f32)] * 2,
        compiler_params=_cp(), name="s5_params_fwd")(are, aim, ldt_row, bre, bim, cre, cim)


def s5_params_bwd(are, aim, ldt_row, bre, bim, cre, cim, cots):
    def body(*refs):
        _, vjp = jax.vjp(_s5_param_map, *[r[...] for r in refs[:7]])
        gs = vjp(tuple(r[...] for r in refs[7:13]))
        for o_ref, o in zip(refs[13:], gs):
            o_ref[...] = o

    return pl.pallas_call(
        body, out_shape=[S((NST, 1), f32)] * 2 + [S((1, 32), f32)] + [S((NST, 16), f32)] * 4,
        compiler_params=_cp(), name="s5_params_bwd")(are, aim, ldt_row, bre, bim, cre, cim, *cots)


def _cpowers(ar, ai):
    out = [(ar, ai)]
    for _ in range(7):
        pr, pi = out[-1]
        out.append((pr * ar - pi * ai, pr * ai + pi * ar))
    return out


def _ctable(pw, rid, power):
    tr_ = jnp.zeros(rid.shape, f32)
    ti_ = jnp.zeros(rid.shape, f32)
    for r in range(8):
        pr, pi = pw[power(r) - 1]
        tr_ = jnp.where(rid == r, pr, tr_)
        ti_ = jnp.where(rid == r, pi, ti_)
    return tr_, ti_


NT5 = 4
RC = 256


def s5_scan_fwd(proj, wbr, wbi, wcr, wci, abr, abi, drow, plan=None):
    def body(u_ref, wbr_ref, wbi_ref, wcr_ref, wci_ref, ar_ref, ai_ref, d_ref, xr_ref, xi_ref, y_ref):
        wbr_v, wbi_v = wbr_ref[...], wbi_ref[...]
        for r in range(L // RC):
            rows = pl.ds(r * RC, RC)
            ub = u_ref[rows, :]
            xr_ref[rows, :] = dot_nt(ub, wbr_v)
            xi_ref[rows, :] = dot_nt(ub, wbi_v)
        pw = _cpowers(ar_ref[...], ai_ref[...])
        rid = lax.broadcasted_iota(jnp.int32, (8, 512), 0)
        tr_, ti_ = _ctable(pw, rid, lambda r: r + 1)

        def group(j, c):
            cr, ci = c
            rows = pl.ds(pl.multiple_of(j * 8, 8), 8)
            br, bi = xr_ref[rows, :], xi_ref[rows, :]
            for s in (1, 2, 4):
                pr, pi = pw[s - 1]
                sr = jnp.where(rid >= s, pltpu.roll(br, s, 0), 0.0)
                si = jnp.where(rid >= s, pltpu.roll(bi, s, 0), 0.0)
                br, bi = br + pr * sr - pi * si, bi + pr * si + pi * sr
            br, bi = br + tr_ * cr - ti_ * ci, bi + tr_ * ci + ti_ * cr
            xr_ref[rows, :] = br
            xi_ref[rows, :] = bi
            return br[7:8], bi[7:8]

        z = jnp.zeros((1, 512), f32)
        lax.fori_loop(0, L // 8, group, (z, z), unroll=2)
        wcr_v, wci_v, dv = wcr_ref[...], wci_ref[...], d_ref[...]
        for r in range(L // RC):
            rows = pl.ds(r * RC, RC)
            y_ref[rows, :] = (dot_nn(xr_ref[rows, :], wcr_v) - dot_nn(xi_ref[rows, :], wci_v)
                              + dv * u_ref[rows, :])

    wspec = pl.BlockSpec((512, 128), lambda j: (j, 0))
    aspec = pl.BlockSpec((1, 512), lambda j: (0, j))
    return pcall(
        body, plan, grid=(NT5,),
        in_specs=[pl.BlockSpec((L, 128), lambda j: (0, j)), wspec, wspec, wspec, wspec, aspec, aspec,
                  pl.BlockSpec((1, 128), lambda j: (0, j))],
        out_specs=[pl.BlockSpec((L, 512), lambda j: (0, j)), pl.BlockSpec((L, 512), lambda j: (0, j)),
                   pl.BlockSpec((L, 128), lambda j: (0, j))],
        out_shape=[S((L, NST), f32), S((L, NST), f32), S((L, S5W), f32)],
        sem=("parallel",), name="s5_scan_fwd", args=[proj, wbr, wbi, wcr, wci, abr, abi, drow])


def s5_scan_bwd(dy, proj, xs_re, xs_im, wbr, wbi, wcr, wci, abr, abi, drow, plan=None):
    def body(dy_ref, u_ref, xr_ref, xi_ref, wbr_ref, wbi_ref, wcr_ref, wci_ref, ar_ref, ai_ref, d_ref,
             du_ref, gwbr_ref, gwbi_ref, gwcr_ref, gwci_ref, gar_ref, gai_ref, gd_ref, lr_ref, li_ref):
        wcr_v, wci_v = wcr_ref[...], wci_ref[...]
        gwcr = jnp.zeros((512, 128), f32)
        gwci = jnp.zeros((512, 128), f32)
        gd = jnp.zeros((1, 128), f32)
        for r in range(L // RC):
            rows = pl.ds(r * RC, RC)
            dyv = dy_ref[rows, :]
            lr_ref[rows, :] = dot_nt(dyv, wcr_v)
            li_ref[rows, :] = -dot_nt(dyv, wci_v)
            gwcr += dot_tn(xr_ref[rows, :], dyv)
            gwci -= dot_tn(xi_ref[rows, :], dyv)
            gd += jnp.sum(dyv * u_ref[rows, :], axis=0, keepdims=True)
        gwcr_ref[...] = gwcr
        gwci_ref[...] = gwci
        gd_ref[...] = gd
        pw = _cpowers(ar_ref[...], -ai_ref[...])
        rid = lax.broadcasted_iota(jnp.int32, (8, 512), 0)
        tr_, ti_ = _ctable(pw, rid, lambda r: 8 - r)

        def group(i, c):
            cr, ci, gar, gai = c
            j = L // 8 - 1 - i
            rows = pl.ds(pl.multiple_of(j * 8, 8), 8)
            br, bi = lr_ref[rows, :], li_ref[rows, :]
            for s in (1, 2, 4):
                pr, pi = pw[s - 1]
                sr = jnp.where(rid < 8 - s, pltpu.roll(br, 8 - s, 0), 0.0)
                si = jnp.where(rid < 8 - s, pltpu.roll(bi, 8 - s, 0), 0.0)
                br, bi = br + pr * sr - pi * si, bi + pr * si + pi * sr
            br, bi = br + tr_ * cr - ti_ * ci, bi + tr_ * ci + ti_ * cr
            lr_ref[rows, :] = br
            li_ref[rows, :] = bi
            nr = jnp.where(rid < 7, pltpu.roll(br, 7, 0), cr)
            ni = jnp.where(rid < 7, pltpu.roll(bi, 7, 0), ci)
            xr, xi = xr_ref[rows, :], xi_ref[rows, :]
            return br[0:1], bi[0:1], gar + xr * nr + xi * ni, gai + xr * ni - xi * nr

        z = jnp.zeros((1, 512), f32)
        z8 = jnp.zeros((8, 512), f32)
        _, _, gar, gai = lax.fori_loop(0, L // 8, group, (z, z, z8, z8), unroll=2)
        gar_ref[...] = jnp.sum(gar, axis=0, keepdims=True)
        gai_ref[...] = jnp.sum(gai, axis=0, keepdims=True)
        wbr_v, wbi_v, dv = wbr_ref[...], wbi_ref[...], d_ref[...]
        gwbr = jnp.zeros((512, 128), f32)
        gwbi = jnp.zeros((512, 128), f32)
        for r in range(L // RC):
            rows = pl.ds(r * RC, RC)
            lrv, liv, uv = lr_ref[rows, :], li_ref[rows, :], u_ref[rows, :]
            du_ref[rows, :] = (dot_nn(lrv, wbr_v) + dot_nn(liv, wbi_v) + dv * dy_ref[rows, :]).astype(du_ref.dtype)
            gwbr += dot_tn(lrv, uv)
            gwbi += dot_tn(liv, uv)
        gwbr_ref[...] = gwbr
        gwbi_ref[...] = gwbi

    wspec = pl.BlockSpec((512, 128), lambda j: (j, 0))
    aspec = pl.BlockSpec((1, 512), lambda j: (0, j))
    col = pl.BlockSpec((L, 128), lambda j: (0, j))
    st = pl.BlockSpec((L, 512), lambda j: (0, j))
    dspec = pl.BlockSpec((1, 128), lambda j: (0, j))
    return pcall(
        body, plan, grid=(NT5,),
        in_specs=[col, col, st, st, wspec, wspec, wspec, wspec, aspec, aspec, dspec],
        out_specs=[col, wspec, wspec, wspec, wspec, aspec, aspec, dspec],
        out_shape=[S((L, S5W), BF)] + [S((NST, 128), f32)] * 4 + [S((1, NST), f32)] * 2 + [S((1, S5W), f32)],
        scratch_shapes=[pltpu.VMEM((L, 512), f32), pltpu.VMEM((L, 512), f32)],
        sem=("parallel",), name="s5_scan_bwd", args=[dy, proj, xs_re, xs_im, wbr, wbi, wcr, wci, abr, abi, drow])


def _glu(y, w, b):
    z = jax.nn.gelu(y)
    return z * jax.nn.sigmoid(dot_nn(z, w) + b)


def s5_glu_fwd(y, w, b):
    def body(y_ref, w_ref, b_ref, o_ref):
        o_ref[...] = _glu(y_ref[...], w_ref[...], b_ref[...]).astype(o_ref.dtype)

    return pl.pallas_call(
        body, grid=(L // TR,),
        in_specs=[pl.BlockSpec((TR, S5W), lambda i: (i, 0)), pl.BlockSpec((S5W, S5W), lambda i: (0, 0)),
                  pl.BlockSpec((1, S5W), lambda i: (0, 0))],
        out_specs=pl.BlockSpec((TR, S5W), lambda i: (i, 0)), out_shape=S((L, S5W), BF),
        compiler_params=_cp(("parallel",)), name="s5_glu_fwd")(y, w, b)


def s5_glu_bwd(y, w, b, dmix):
    def body(y_ref, w_ref, b_ref, g_ref, dy_ref, dw_ref, db_ref):
        _, vjp = jax.vjp(_glu, y_ref[...], w_ref[...].astype(f32), b_ref[...])
        dy, dw, db = vjp(g_ref[...])
        dy_ref[...] = dy

        @pl.when(pl.program_id(0) == 0)
        def _():
            dw_ref[...] = jnp.zeros_like(dw_ref)
            db_ref[...] = jnp.zeros_like(db_ref)

        dw_ref[...] += dw
        db_ref[...] += db

    row = pl.BlockSpec((TR, S5W), lambda i: (i, 0))
    return pl.pallas_call(
        body, grid=(L // TR,),
        in_specs=[row, pl.BlockSpec((S5W, S5W), lambda i: (0, 0)), pl.BlockSpec((1, S5W), lambda i: (0, 0)), row],
        out_specs=[row, pl.BlockSpec((S5W, S5W), lambda i: (0, 0)), pl.BlockSpec((1, S5W), lambda i: (0, 0))],
        out_shape=[S((L, S5W), f32), S((S5W, S5W), f32), S((1, S5W), f32)],
        compiler_params=_cp(("arbitrary",)), name="s5_glu_bwd")(y, w, b, dmix)


def _dg3(a, b, ca, cb):
    ah, bh = a.astype(BF), b.astype(BF)
    al, bl = (a - ah.astype(f32)).astype(BF), (b - bh.astype(f32)).astype(BF)
    return _dg(ah, bh, ca, cb) + _dg(ah, bl, ca, cb) + _dg(al, bh, ca, cb)


@jax.custom_vjp
def hi_nn(a, b):
    return _dg3(a, b, 1, 0)


@jax.custom_vjp
def hi_nt(a, b):
    return _dg3(a, b, 1, 1)


@jax.custom_vjp
def hi_tn(a, b):
    return _dg3(a, b, 0, 0)


hi_nn.defvjp(lambda a, b: (hi_nn(a, b), (a, b)), lambda r, g: (hi_nt(g, r[1]), hi_tn(r[0], g)))
hi_nt.defvjp(lambda a, b: (hi_nt(a, b), (a, b)), lambda r, g: (hi_nn(g, r[1]), hi_tn(g, r[0])))
hi_tn.defvjp(lambda a, b: (hi_tn(a, b), (a, b)), lambda r, g: (hi_nt(r[1], g), hi_nn(r[0], g)))


def _hgrn_chunk(St, xq, xf, xi, xg, gam, ng):
    lb = jax.nn.sigmoid(gam[0:1] - gam[1:2])
    q = jax.nn.silu(xq)
    f = lb + (1.0 - lb) * jax.nn.sigmoid(xf)
    k = 1.0 - f
    g = jnp.log(f)
    ti = lax.broadcasted_iota(jnp.int32, (HGC, HGC), 0)
    si = lax.broadcasted_iota(jnp.int32, (HGC, HGC), 1)
    causal = si <= ti
    b = jnp.dot(causal.astype(f32), g, precision=HI, preferred_element_type=f32)
    qe = q * jnp.exp(b)
    o = dot_nt(qe, St)
    parts = []
    for i in range(HGC // HGB):
        r, n, mid = slice(HGB * i, HGB * (i + 1)), HGB * (i + 1), HGB * i + HGB // 2
        base = b[mid:mid + 1]
        sc = hi_nt(q[r] * jnp.exp(b[r] - base), k[:n] * jnp.exp(base - b[:n]))
        parts.append(dot_nn(jnp.where(causal[r, :n], sc, 0.0), xi[:n]))
    o = o + jnp.concatenate(parts, axis=0)
    bl = b[HGC - 1:HGC]
    St_new = St * jnp.exp(bl) + dot_tn(xi, k * jnp.exp(bl - b))
    o = o * lax.rsqrt(jnp.mean(o * o, axis=-1, keepdims=True) + EPS) * ng
    return St_new, o * jax.nn.silu(xg)


NCH = L // HGC


def hgrn_fwd(proj, gamma, hnorm, plan=None):
    def body(q_ref, f_ref, i_ref, g_ref, gam_ref, ng_ref, o_ref, ss_ref, st):
        @pl.when(pl.program_id(0) == 0)
        def _():
            st[...] = jnp.zeros_like(st)

        for h in range(4):
            sl = slice(h * 128, (h + 1) * 128)
            s0 = st[h]
            ss_ref[0, h] = s0
            s1, o = _hgrn_chunk(s0, q_ref[:, sl], f_ref[:, sl], i_ref[:, sl], g_ref[:, sl], gam_ref[:, sl], ng_ref[:, sl])
            st[h] = s1
            o_ref[:, sl] = o.astype(o_ref.dtype)

    def pj(n):
        return pl.BlockSpec((HGC, 512), lambda c: (c, n))

    return pcall(
        body, plan, grid=(NCH,),
        in_specs=[pj(1), pj(2), pj(3), pj(4), pl.BlockSpec((2, 512), lambda c: (0, 0)), pl.BlockSpec((1, 512), lambda c: (0, 0))],
        out_specs=[pl.BlockSpec((HGC, 512), lambda c: (c, 0)), pl.BlockSpec((1, 4, 128, 128), lambda c: (c, 0, 0, 0))],
        out_shape=[S((L, 512), BF), S((NCH, 4, 128, 128), f32)],
        scratch_shapes=[pltpu.VMEM((4, 128, 128), f32)],
        sem=("arbitrary",), name="hgrn_fwd", args=[proj, proj, proj, proj, gamma, hnorm])


def hgrn_bwd(proj, gamma, hnorm, ssave, dmix, du, plan=None):
    def body(q_ref, f_ref, i_ref, g_ref, gam_ref, ng_ref, ss_ref, do_ref, du_ref, dp_ref, dgam_ref, dng_ref, dst):
        @pl.when(pl.program_id(0) == 0)
        def _():
            dst[...] = jnp.zeros_like(dst)
            dgam_ref[...] = jnp.zeros_like(dgam_ref)
            dng_ref[...] = jnp.zeros_like(dng_ref)

        dp_ref[:, 0:512] = du_ref[...]
        for h in range(4):
            sl = slice(h * 128, (h + 1) * 128)
            _, vjp = jax.vjp(_hgrn_chunk, ss_ref[0, h], q_ref[:, sl], f_ref[:, sl], i_ref[:, sl], g_ref[:, sl],
                             gam_ref[:, sl], ng_ref[:, sl])
            ds, dq, df, di, dg, dgam, dng = vjp((dst[h], do_ref[:, sl]))
            dst[h] = ds
            for n, v in enumerate((dq, df, di, dg)):
                dp_ref[:, 512 * (n + 1) + h * 128: 512 * (n + 1) + (h + 1) * 128] = v.astype(dp_ref.dtype)
            dgam_ref[:, sl] += dgam
            dng_ref[:, sl] += dng

    def pj(n):
        return pl.BlockSpec((HGC, 512), lambda i: (NCH - 1 - i, n))

    return pcall(
        body, plan, grid=(NCH,),
        in_specs=[pj(1), pj(2), pj(3), pj(4), pl.BlockSpec((2, 512), lambda i: (0, 0)), pl.BlockSpec((1, 512), lambda i: (0, 0)),
                  pl.BlockSpec((1, 4, 128, 128), lambda i: (NCH - 1 - i, 0, 0, 0)), pj(1), pj(0)],
        out_specs=[pl.BlockSpec((HGC, 2560), lambda i: (NCH - 1 - i, 0)), pl.BlockSpec((2, 512), lambda i: (0, 0)),
                   pl.BlockSpec((1, 512), lambda i: (0, 0))],
        out_shape=[S((L, 2560), BF), S((2, 512), f32), S((1, 512), f32)],
        scratch_shapes=[pltpu.VMEM((4, 128, 128), f32)],
        sem=("arbitrary",), name="hgrn_bwd", args=[proj, proj, proj, proj, gamma, hnorm, ssave, dmix, du])


def _earlier(h_ref, k, r0, n):
    if r0 > 0:
        return h_ref[pl.ds(r0 - k, n), :]
    rid = lax.broadcasted_iota(jnp.int32, (8, h_ref.shape[1]), 0)
    head = jnp.where(rid >= k, pltpu.roll(h_ref[pl.ds(0, 8), :], k, 0), 0.0)
    return jnp.concatenate([head, h_ref[pl.ds(8 - k, n - 8), :]], axis=0)


def _conv3_rows(h_ref, w, b, r0, n=None):
    n = CR if n is None else n
    h1, h2 = _earlier(h_ref, 1, r0, n), _earlier(h_ref, 2, r0, n)
    return w[2:3] * h_ref[pl.ds(r0, n), :] + w[1:2] * h1 + w[0:1] * h2 + b, h1, h2


CT = 128
NCT = DFF // CT
CR = 64


def convact_fwd(hu, cw, cb, layer, plan=None):
    def body(ha_ref, hb_ref, wa_ref, wb_ref, ba_ref, bb_ref, o_ref):
        ca = _conv3_rows(ha_ref, wa_ref[...], ba_ref[...], 0, L)[0]
        cb_ = _conv3_rows(hb_ref, wb_ref[...], bb_ref[...], 0, L)[0]
        o_ref[...] = (jax.nn.silu(ca) * cb_).astype(o_ref.dtype)

    def h(off):
        return pl.BlockSpec((L, CT), lambda j: (0, j + off))

    def w(off):
        return pl.BlockSpec((3, CT), lambda j: (0, j + off))

    def b(off):
        return pl.BlockSpec((None, 1, CT), lambda j: (layer, 0, j + off))

    return pcall(body, plan, grid=(NCT,), in_specs=[h(0), h(NCT), w(0), w(NCT), b(0), b(NCT)],
                 out_specs=pl.BlockSpec((L, CT), lambda j: (0, j)), out_shape=S((L, DFF), BF),
                 sem=("parallel",), name=f"convact_fwd{layer}", args=[hu, hu, cw, cw, cb, cb])


def convact_bwd(hu, cw, cb, dact, layer, plan=None):
    def body(ha_ref, hb_ref, wa_ref, wb_ref, ba_ref, bb_ref, g_ref, dh_ref, dw_ref, db_ref, sh, sw, sb, da_scr, db_scr):
        j = pl.program_id(0)

        def fold(x):
            return functools.reduce(jnp.add, [x[8 * m:8 * m + 8] for m in range(CR // 8)])

        @pl.when(j < NCT)
        def _():
            wa, wb, ba, bb = wa_ref[...], wb_ref[...], ba_ref[...], bb_ref[...]
            da_scr[pl.ds(L, 8), :] = jnp.zeros((8, CT), f32)
            db_scr[pl.ds(L, 8), :] = jnp.zeros((8, CT), f32)
            acc = [jnp.zeros((8, CT), f32) for _ in range(8)]
            for c in range(L // CR):
                r0 = c * CR
                ca, a1, a2 = _conv3_rows(ha_ref, wa, ba, r0)
                cb_, b1, b2 = _conv3_rows(hb_ref, wb, bb, r0)
                g = g_ref[pl.ds(r0, CR), :].astype(f32)
                sg = jax.nn.sigmoid(ca)
                dca = g * cb_ * (sg * (1.0 + ca * (1.0 - sg)))
                dcb = g * (ca * sg)
                da_scr[pl.ds(r0, CR), :] = dca
                db_scr[pl.ds(r0, CR), :] = dcb
                terms = (dca * a2, dca * a1, dca * ha_ref[pl.ds(r0, CR), :], dca,
                         dcb * b2, dcb * b1, dcb * hb_ref[pl.ds(r0, CR), :], dcb)
                acc = [a + fold(t) for a, t in zip(acc, terms)]
            rows = [jnp.sum(a, axis=0, keepdims=True) for a in acc]
            for k in range(3):
                dw_ref[k:k + 1, :] = rows[k]
                sw[j, k:k + 1, :] = rows[4 + k]
            db_ref[...] = rows[3]
            sb[j] = rows[7]
            for c in range(L // CR):
                r0 = c * CR
                for scr, w, out in ((da_scr, wa, dh_ref), (db_scr, wb, sh.at[j])):
                    dh = (w[2:3] * scr[pl.ds(r0, CR), :] + w[1:2] * scr[pl.ds(r0 + 1, CR), :]
                          + w[0:1] * scr[pl.ds(r0 + 2, CR), :])
                    out[pl.ds(r0, CR), :] = dh.astype(out.dtype)

        @pl.when(j >= NCT)
        def _():
            dh_ref[...] = sh[j - NCT]
            dw_ref[...] = sw[j - NCT]
            db_ref[...] = sb[j - NCT]

    def lo(j):
        return jnp.minimum(j, NCT - 1)

    in_specs = [pl.BlockSpec((L, CT), lambda j: (0, lo(j))), pl.BlockSpec((L, CT), lambda j: (0, lo(j) + NCT)),
                pl.BlockSpec((3, CT), lambda j: (0, lo(j))), pl.BlockSpec((3, CT), lambda j: (0, lo(j) + NCT)),
                pl.BlockSpec((None, 1, CT), lambda j: (layer, 0, lo(j))), pl.BlockSpec((None, 1, CT), lambda j: (layer, 0, lo(j) + NCT)),
                pl.BlockSpec((L, CT), lambda j: (0, lo(j)))]
    return pcall(
        body, plan, grid=(2 * NCT,), in_specs=in_specs,
        out_specs=[pl.BlockSpec((L, CT), lambda j: (0, j)), pl.BlockSpec((3, CT), lambda j: (0, j)), pl.BlockSpec((1, CT), lambda j: (0, j))],
        out_shape=[S((L, 2 * DFF), BF), S((3, 2 * DFF), f32), S((1, 2 * DFF), f32)],
        scratch_shapes=[pltpu.VMEM((NCT, L, CT), BF), pltpu.VMEM((NCT, 3, CT), f32), pltpu.VMEM((NCT, 1, CT), f32),
                        pltpu.VMEM((L + 8, CT), f32), pltpu.VMEM((L + 8, CT), f32)],
        sem=("arbitrary",), name=f"convact_bwd{layer}", args=[hu, hu, cw, cw, cb, cb, dact])


DILS = (1, 4, 16)
AB = 128


def _rope_tables(pos_ref, invf_ref):
    ang = pos_ref[...].astype(f32) * invf_ref[...]
    lane = lax.broadcasted_iota(jnp.int32, (1, 128), 1) % 64
    cosf = jnp.where(lane < 16, jnp.cos(ang), 1.0)
    sn = jnp.sin(ang)
    s_lo = jnp.where(lane < 8, -sn, 0.0)
    s_hi = jnp.where((lane >= 8) & (lane < 16), sn, 0.0)
    return cosf, s_lo, s_hi


def _rope(t, cosf, s_lo, s_hi):
    return t * cosf + pltpu.roll(t, 120, 1) * s_lo + pltpu.roll(t, 8, 1) * s_hi


def _rope_t(g, cosf, s_lo, s_hi):
    return g * cosf + pltpu.roll(g * s_lo, 8, 1) + pltpu.roll(g * s_hi, 120, 1)


def _att_block(q2, kp, kc, vp, vc, first):
    lane = lax.broadcasted_iota(jnp.int32, (1, 128), 1)
    qi = lax.broadcasted_iota(jnp.int32, (AB, 2 * AB), 0) + AB
    kj = lax.broadcasted_iota(jnp.int32, (AB, 2 * AB), 1)
    back = qi - kj
    valid = (back >= 0) & (back <= AB)
    if first:
        valid = valid & (kj >= AB)
    kk = jnp.concatenate([kp, kc], axis=0)
    vv = jnp.concatenate([vp, vc], axis=0)
    o2 = jnp.zeros((AB, 128), f32)
    lse2 = jnp.zeros((AB, 128), f32)
    for e in range(2):
        hm = ((lane >= 64 * e) & (lane < 64 * (e + 1))).astype(f32)
        s = dot_nt(q2 * (hm * 0.125), kk)
        s = jnp.where(valid, s, -jnp.inf)
        m = jnp.max(s, axis=-1, keepdims=True)
        p = jnp.exp(s - m)
        den = jnp.sum(p, axis=-1, keepdims=True)
        o2 = o2 + dot_nn(p, vv * hm) / den
        lse2 = lse2 + (m + jnp.log(den)) * hm
    return o2, lse2


def _att_blocks(dil):
    m = L // dil
    return [(r * m + n * AB, n == 0) for r in range(dil) for n in range(m // AB)]


def deinterleave(x, dil):
    return x if dil == 1 else x.reshape(L // dil, dil, x.shape[1]).swapaxes(0, 1).reshape(L, x.shape[1])


def attn_fwd(qkv, pos, invf, g, plan=None):
    blocks = _att_blocks(DILS[g])

    def body(q_ref, k_ref, v_ref, pos_ref, invf_ref, o_ref, l_ref, qr, kr):
        cosf, s_lo, s_hi = _rope_tables(pos_ref, invf_ref)
        qr[...] = _rope(q_ref[...], cosf, s_lo, s_hi)
        kr[...] = _rope(k_ref[...], cosf, s_lo, s_hi)
        for off, first in blocks:
            cur, prv = pl.ds(off, AB), pl.ds(off if first else off - AB, AB)
            o2, lse2 = _att_block(qr[cur, :], kr[prv, :], kr[cur, :], v_ref[prv, :], v_ref[cur, :], first)
            o_ref[cur, :] = o2
            l_ref[cur, :] = lse2

    def sec(n):
        return pl.BlockSpec((L, 128), lambda p: (0, p + 4 * n))

    return pcall(
        body, plan, grid=(4,),
        in_specs=[sec(0), sec(1), sec(2), pl.BlockSpec((L, 1), lambda p: (0, 0)), pl.BlockSpec((1, 128), lambda p: (0, 0))],
        out_specs=[sec(0), sec(0)], out_shape=[S((L, 512), f32), S((L, 512), f32)],
        scratch_shapes=[pltpu.VMEM((L, 128), f32), pltpu.VMEM((L, 128), f32)],
        sem=("parallel",), name=f"attn_fwd{g}", args=[qkv, qkv, qkv, pos, invf])


def _att_block_bwd(q2, kp, kc, vp, vc, lse2, do2, dl2, first):
    lane = lax.broadcasted_iota(jnp.int32, (1, 128), 1)
    qi = lax.broadcasted_iota(jnp.int32, (AB, 2 * AB), 0) + AB
    kj = lax.broadcasted_iota(jnp.int32, (AB, 2 * AB), 1)
    back = qi - kj
    valid = (back >= 0) & (back <= AB)
    if first:
        valid = valid & (kj >= AB)
    kk = jnp.concatenate([kp, kc], axis=0)
    vv = jnp.concatenate([vp, vc], axis=0)
    dq2 = jnp.zeros((AB, 128), f32)
    dkk = jnp.zeros((2 * AB, 128), f32)
    dvv = jnp.zeros((2 * AB, 128), f32)
    for e in range(2):
        hb = (lane >= 64 * e) & (lane < 64 * (e + 1))
        hm = hb.astype(f32)
        qs = q2 * (hm * 0.125)
        lse = jnp.max(jnp.where(hb, lse2, -jnp.inf), axis=-1, keepdims=True)
        dls = jnp.sum(dl2 * hm, axis=-1, keepdims=True)
        p = jnp.where(valid, jnp.exp(dot_nt(qs, kk) - lse), 0.0)
        dov = do2 * hm
        dp = dot_nt(dov, vv)
        ds = p * (dp - jnp.sum(p * dp, axis=-1, keepdims=True) + dls)
        dq2 = dq2 + dot_nn(ds, kk) * (hm * 0.125)
        dkk = dkk + dot_tn(ds, qs)
        dvv = dvv + dot_tn(p, dov)
    return dq2, dkk[:AB], dkk[AB:], dvv[:AB], dvv[AB:]


def attn_bwd(qkv, pos, invf, lse, do, dl, g, plan=None):
    blocks = _att_blocks(DILS[g])

    def body(q_ref, k_ref, v_ref, pos_ref, invf_ref, l_ref, do_ref, dl_ref, d_ref, qr, kr, dqr, dkr, dvr):
        cosf, s_lo, s_hi = _rope_tables(pos_ref, invf_ref)
        qr[...] = _rope(q_ref[...], cosf, s_lo, s_hi)
        kr[...] = _rope(k_ref[...], cosf, s_lo, s_hi)
        for off, first in blocks:
            cur, prv = pl.ds(off, AB), pl.ds(off if first else off - AB, AB)
            dq2, dkp, dkc, dvp, dvc = _att_block_bwd(qr[cur, :], kr[prv, :], kr[cur, :], v_ref[prv, :], v_ref[cur, :],
                                                     l_ref[cur, :], do_ref[cur, :], dl_ref[cur, :], first)
            dqr[cur, :] = dq2
            dkr[cur, :] = dkc
            dvr[cur, :] = dvc
            if not first:
                dkr[prv, :] += dkp
                dvr[prv, :] += dvp
        d_ref[0] = _rope_t(dqr[...], cosf, s_lo, s_hi).astype(d_ref.dtype)
        d_ref[1] = _rope_t(dkr[...], cosf, s_lo, s_hi).astype(d_ref.dtype)
        d_ref[2] = dvr[...].astype(d_ref.dtype)

    def sec(n):
        return pl.BlockSpec((L, 128), lambda p: (0, p + 4 * n))

    return pcall(
        body, plan, grid=(4,),
        in_specs=[sec(0), sec(1), sec(2), pl.BlockSpec((L, 1), lambda p: (0, 0)), pl.BlockSpec((1, 128), lambda p: (0, 0)),
                  sec(0), sec(0), sec(0)],
        out_specs=pl.BlockSpec((3, L, 128), lambda p: (0, 0, p)), out_shape=S((3, L, 512), BF),
        scratch_shapes=[pltpu.VMEM((L, 128), f32)] * 5,
        sem=("parallel",), name=f"attn_bwd{g}", args=[qkv, qkv, qkv, pos, invf, lse, do, dl])


def _merge(o0, o1, o2, l0, l1, l2):
    m = jnp.maximum(jnp.maximum(l0, l1), l2)
    e0, e1, e2 = jnp.exp(l0 - m), jnp.exp(l1 - m), jnp.exp(l2 - m)
    return (e0 * o0 + e1 * o1 + e2 * o2) / (e0 + e1 + e2)


def _to_token_major(src_ref, scr, i, dil, slab):
    n = TR // dil
    for r in range(dil):
        rows = pl.ds(pl.multiple_of(r * (L // dil) + i * n, n), n)
        scr[pl.ds(r, n, stride=dil), :] = src_ref[rows, slab * 128:(slab + 1) * 128].astype(f32)
    return scr[...]


def _to_class_major(val, dst_ref, scr, i, dil, slab):
    n = TR // dil
    scr[...] = val
    for r in range(dil):
        rows = pl.ds(pl.multiple_of(r * (L // dil) + i * n, n), n)
        dst_ref[rows, slab * 128:(slab + 1) * 128] = scr[pl.ds(r, n, stride=dil), :].astype(dst_ref.dtype)


def rms_fwd_classes(x, g, name):
    def body(x_ref, g_ref, o_ref, o1_ref, o2_ref, scr):
        i = pl.program_id(0)
        y = _rms(x_ref[...], g_ref[...])
        o_ref[...] = y.astype(o_ref.dtype)
        for s in range(D // 128):
            ys = y[:, s * 128:(s + 1) * 128]
            _to_class_major(ys, o1_ref, scr, i, DILS[1], s)
            _to_class_major(ys, o2_ref, scr, i, DILS[2], s)

    row = pl.BlockSpec((TR, D), lambda i: (i, 0))
    full = pl.BlockSpec((L, D), lambda i: (0, 0))
    return pl.pallas_call(
        body, grid=(L // TR,), in_specs=[row, pl.BlockSpec((1, D), lambda i: (0, 0))], out_specs=[row, full, full],
        out_shape=[S((L, D), BF)] * 3, scratch_shapes=[pltpu.VMEM((TR, 128), f32)],
        compiler_params=_cp(("arbitrary",)), name=name)(x, g)


def rms_bwd_classes(x, g, dy0, dyc, dres, name, plan=None):
    def body(x_ref, g_ref, dy0_ref, d1_ref, d2_ref, dr_ref, dh_ref, dg_ref, scr, dyf):
        i = pl.program_id(0)
        for s in range(D // 128):
            sl = slice(s * 128, (s + 1) * 128)
            dyf[:, sl] = (dy0_ref[:, sl] + _to_token_major(d1_ref, scr.at[0], i, DILS[1], s)
                          + _to_token_major(d2_ref, scr.at[1], i, DILS[2], s))
        _, vjp = jax.vjp(_rms, x_ref[...], g_ref[...])
        dx, dg = vjp(dyf[...])
        dh_ref[...] = dr_ref[...] + dx

        @pl.when(i == 0)
        def _():
            dg_ref[...] = jnp.zeros_like(dg_ref)

        dg_ref[...] += dg

    row = pl.BlockSpec((TR, D), lambda i: (i, 0))
    vec = pl.BlockSpec((1, D), lambda i: (0, 0))
    full = pl.BlockSpec((L, D), lambda i: (0, 0))
    return pcall(body, plan, grid=(L // TR,), in_specs=[row, vec, row, full, full, row], out_specs=[row, vec],
                 out_shape=[S((L, D), f32), S((1, D), f32)],
                 scratch_shapes=[pltpu.VMEM((2, TR, 128), f32), pltpu.VMEM((TR, D), f32)],
                 sem=("arbitrary",), name=name, args=[x, g, dy0, dyc[0], dyc[1], dres])


def attn_merge_fwd(o0, l0, oc, lc, plan=None):
    def body(o0_ref, l0_ref, o1_ref, l1_ref, o2_ref, l2_ref, o_ref, scr):
        i = pl.program_id(0)
        for s in range(4):
            sl = slice(s * 128, (s + 1) * 128)
            o1 = _to_token_major(o1_ref, scr.at[0], i, DILS[1], s)
            l1 = _to_token_major(l1_ref, scr.at[1], i, DILS[1], s)
            o2 = _to_token_major(o2_ref, scr.at[2], i, DILS[2], s)
            l2 = _to_token_major(l2_ref, scr.at[3], i, DILS[2], s)
            o_ref[:, sl] = _merge(o0_ref[:, sl], o1, o2, l0_ref[:, sl], l1, l2).astype(o_ref.dtype)

    blk = pl.BlockSpec((TR, 512), lambda i: (i, 0))
    full = pl.BlockSpec((L, 512), lambda i: (0, 0))
    return pcall(body, plan, grid=(L // TR,), in_specs=[blk, blk, full, full, full, full], out_specs=blk,
                 out_shape=S((L, 512), BF), scratch_shapes=[pltpu.VMEM((4, TR, 128), f32)],
                 sem=("arbitrary",), name="attn_merge_fwd", args=[o0, l0, oc[0], lc[0], oc[1], lc[1]])


def attn_merge_bwd(o0, l0, oc, lc, do, plan=None):
    def body(o0_ref, l0_ref, o1_ref, l1_ref, o2_ref, l2_ref, g_ref, do0, dl0, do1, dl1, do2, dl2, scr):
        i = pl.program_id(0)
        for s in range(4):
            sl = slice(s * 128, (s + 1) * 128)
            o1 = _to_token_major(o1_ref, scr.at[0], i, DILS[1], s)
            l1 = _to_token_major(l1_ref, scr.at[1], i, DILS[1], s)
            o2 = _to_token_major(o2_ref, scr.at[2], i, DILS[2], s)
            l2 = _to_token_major(l2_ref, scr.at[3], i, DILS[2], s)
            _, vjp = jax.vjp(_merge, o0_ref[:, sl], o1, o2, l0_ref[:, sl], l1, l2)
            g0, g1, g2, h0, h1, h2 = vjp(g_ref[:, sl].astype(f32))
            do0[:, sl] = g0.astype(do0.dtype)
            dl0[:, sl] = h0
            _to_class_major(g1, do1, scr.at[0], i, DILS[1], s)
            _to_class_major(h1, dl1, scr.at[1], i, DILS[1], s)
            _to_class_major(g2, do2, scr.at[2], i, DILS[2], s)
            _to_class_major(h2, dl2, scr.at[3], i, DILS[2], s)

    blk = pl.BlockSpec((TR, 512), lambda i: (i, 0))
    full = pl.BlockSpec((L, 512), lambda i: (0, 0))
    outs = pcall(body, plan, grid=(L // TR,), in_specs=[blk, blk, full, full, full, full, blk],
                 out_specs=[blk, blk, full, full, full, full],
                 out_shape=[S((L, 512), BF), S((L, 512), f32)] * 3, scratch_shapes=[pltpu.VMEM((4, TR, 128), f32)],
                 sem=("arbitrary",), name="attn_merge_bwd", args=[o0, l0, oc[0], lc[0], oc[1], lc[1], do])
    return [outs[0], outs[2], outs[4]], [outs[1], outs[3], outs[5]]


def _invf_lanes():
    half = 8
    inv = ROPE_THETA ** (-np.arange(half, dtype=np.float32) * 2.0 / 16.0)
    lane = np.arange(128) % 64
    return jnp.asarray(np.where(lane < 16, inv[lane % 8], 0.0).astype(np.float32)[None, :])


def hosted(C, host, fn):
    p = C.plan(host) if C is not None else None
    out = fn(p)
    if p is not None:
        C.done(p)
    return out


def _ffn_fwd(h, g_row, W, cb, layer, C):
    hn = rms_fwd(h, g_row, f"rms_ffn{layer}")
    hu = hosted(C, f"ffn_in{layer}", lambda p: matmul(hn, W[("ffn_w_in", layer)], mode="nn", tm=1024, tn=1408, tk=1024,
                                                      plan=p, name=f"ffn_in{layer}"))
    act = hosted(C, f"convact_fwd{layer}", lambda p: convact_fwd(hu, W[("ffn_conv_w", layer)], cb, layer, plan=p))
    h2 = hosted(C, f"ffn_out{layer}", lambda p: matmul(act, W[("ffn_w_out", layer)], mode="nn", tm=1024, tn=1024, tk=2816,
                                                       add=h, plan=p, name=f"ffn_out{layer}"))
    return h2, (hn, hu, act)


def _ffn_bwd(dh, h, g_row, W, cb, saved, layer, C, G):
    hn, hu, act = saved
    w_in, w_out = W[("ffn_w_in", layer)], W[("ffn_w_out", layer)]
    dact = hosted(C, f"ffn_out_dx{layer}", lambda p: matmul(dh, w_out, mode="nt", tm=1024, tn=1408, tk=1024, out_dtype=BF,
                                                          plan=p, name=f"ffn_out_dx{layer}"))
    G[("ffn_w_out", layer)] = hosted(C, f"ffn_out_dw{layer}", lambda p: matmul(
        act, dh, mode="tn", tm=1408, tn=1024, tk=L, out_dtype=BF, plan=p, name=f"ffn_out_dw{layer}"))
    dhu, G[("ffn_conv_w", layer)], g_cb = hosted(
        C, f"convact_bwd{layer}", lambda p: convact_bwd(hu, W[("ffn_conv_w", layer)], cb, dact, layer, plan=p))
    dhn = hosted(C, f"ffn_in_dx{layer}", lambda p: matmul(dhu, w_in, mode="nt", tm=1024, tn=1024, tk=2816, plan=p,
                                                         name=f"ffn_in_dx{layer}"))
    G[("ffn_w_in", layer)] = hosted(C, f"ffn_in_dw{layer}", lambda p: matmul(
        hn, dhu, mode="tn", tm=1024, tn=1408, tk=L, out_dtype=BF, plan=p, name=f"ffn_in_dw{layer}"))
    dh2, g_norm = hosted(C, f"rms_ffn_bwd{layer}", lambda p: rms_bwd(h, g_row, [dhn], dh, f"rms_ffn_bwd{layer}", plan=p))
    return dh2, g_cb, g_norm


def local_step(x, pos, tgt, sm, W, C=None):
    G = C.grads if C is not None else {}
    nm, nf = sm["norm_mix"], sm["norm_ffn"]
    invf = _invf_lanes()
    are = sm["s5_A_re"].reshape(NST, 1)
    aim = sm["s5_A_im"].reshape(NST, 1)
    ldt = sm["s5_log_dt"].reshape(1, 32)
    bre = sm["s5_B_re"].reshape(NST, 16)
    bim = sm["s5_B_im"].reshape(NST, 16)
    cre = jnp.swapaxes(sm["s5_C_re"][0], 1, 2).reshape(NST, 16)
    cim = jnp.swapaxes(sm["s5_C_im"][0], 1, 2).reshape(NST, 16)
    drow = sm["s5_D"].reshape(1, S5W)
    wbr, wbi, wcr, wci, abr, abi = s5_params_fwd(are, aim, ldt, bre, bim, cre, cim)
    hn0 = rms_fwd(x, nm[0:1], "rms_mix0")
    cb3 = sm["ffn_conv_b3"]
    proj = hosted(C, "mix_in", lambda p: matmul(hn0, W[("mix_w_in", 0)], mode="nn", tm=1024, tn=1280, tk=1024, plan=p, name="mix_in"))
    xs_re, xs_im, y5 = hosted(C, "s5_scan_fwd", lambda p: s5_scan_fwd(proj, wbr, wbi, wcr, wci, abr, abi, drow, plan=p))
    oa = s5_glu_fwd(y5, W[("s5_glu_w", 0)], sm["s5_glu_b"])
    ob, ssave = hosted(C, "hgrn_fwd", lambda p: hgrn_fwd(proj, sm["hgrn_gamma"], sm["hgrn_norm"], plan=p))
    cat = jnp.concatenate([oa, ob], axis=1)
    h1 = matmul(cat, W[("mix_w_out", 0)], mode="nn", tm=1024, tn=1024, tk=1024, add=x, name="mix_out")
    h2, ffn0 = _ffn_fwd(h1, nf[0:1], W, cb3, 0, C)
    hn2_g = rms_fwd_classes(h2, nm[1:2], "rms_mix1")
    wqkv = W[("att_w_qkv", 0)]
    pos_g, qkv_g, oc_g, lc_g = [], [], [], []
    for g, dil in enumerate(DILS):
        pos_g.append(deinterleave(pos, dil))
        qkv_g.append(hosted(C, f"att_qkv{g}", lambda p: matmul(
            hn2_g[g], wqkv, mode="nn", tm=1024, tn=512, tk=1024, dims=(L, 1536, D),
            b_spec=pl.BlockSpec((D, 512), lambda i, j, k, g=g: (0, 3 * j + g)), plan=p, name=f"att_qkv{g}")))
        o_c, l_c = hosted(C, f"attn_fwd{g}", lambda p: attn_fwd(qkv_g[g], pos_g[g], invf, g, plan=p))
        oc_g.append(o_c)
        lc_g.append(l_c)
    o = hosted(C, "attn_merge_fwd", lambda p: attn_merge_fwd(oc_g[0], lc_g[0], oc_g[1:], lc_g[1:], plan=p))
    h3 = matmul(o, W[("att_w_o", 0)], mode="nn", tm=1024, tn=1024, tk=512, add=h2, name="att_o")
    h4, ffn1 = _ffn_fwd(h3, nf[1:2], W, cb3, 1, C)
    loss, dh, g_nfinal = loss_head(h4, sm["norm_final"].reshape(1, D), tgt)
    dh, g_cb1, g_nf1 = _ffn_bwd(dh, h3, nf[1:2], W, cb3, ffn1, 1, C, G)
    do = matmul(dh, W[("att_w_o", 0)], mode="nt", tm=1024, tn=512, tk=1024, name="att_o_dx")
    G[("att_w_o", 0)] = matmul(o, dh, mode="tn", tm=512, tn=1024, tk=L, out_dtype=BF, name="att_o_dw")
    do_g, dl_g = hosted(C, "attn_merge_bwd", lambda p: attn_merge_bwd(oc_g[0], lc_g[0], oc_g[1:], lc_g[1:], do, plan=p))
    dhn2_g, gq = [], []
    for g, dil in enumerate(DILS):
        d3 = hosted(C, f"attn_bwd{g}", lambda p: attn_bwd(qkv_g[g], pos_g[g], invf, lc_g[g], do_g[g], dl_g[g], g, plan=p))
        dx = matmul(d3, wqkv, mode="nt", tm=1024, tn=1024, tk=512, dims=(L, D, 1536),
                    a_spec=pl.BlockSpec((None, 1024, 512), lambda i, j, k: (k, i, 0)),
                    b_spec=pl.BlockSpec((D, 512), lambda i, j, k, g=g: (0, 3 * k + g)), name=f"att_qkv_dx{g}")
        dhn2_g.append(dx)
        gq.append(matmul(hn2_g[g], d3, mode="tn", tm=1024, tn=512, tk=L, out_dtype=BF, dims=(D, 1536, L),
                         b_spec=pl.BlockSpec((None, L, 512), lambda i, j, k: (j, k, 0)), name=f"att_qkv_dw{g}"))
    G[("att_w_qkv", 0)] = jnp.concatenate([gq[g][:, 512 * s:512 * (s + 1)] for s in range(3) for g in range(3)], axis=1)
    dh, g_nm1 = hosted(C, "rms_mix_bwd1", lambda p: rms_bwd_classes(h2, nm[1:2], dhn2_g[0], dhn2_g[1:], dh, "rms_mix_bwd1", plan=p))
    dh, g_cb0, g_nf0 = _ffn_bwd(dh, h1, nf[0:1], W, cb3, ffn0, 0, C, G)
    dmix = matmul(dh, W[("mix_w_out", 0)], mode="nt", tm=1024, tn=1024, tk=1024, name="mix_out_dx")
    G[("mix_w_out", 0)] = matmul(cat, dh, mode="tn", tm=1024, tn=1024, tk=L, out_dtype=BF, name="mix_out_dw")
    dy5, g_glu_w, g_glu_b = s5_glu_bwd(y5, W[("s5_glu_w", 0)], sm["s5_glu_b"], dmix)
    G[("s5_glu_w", 0)] = g_glu_w.astype(BF)
    du, gwbr, gwbi, gwcr, gwci, gabr, gabi, g_d = hosted(C, "s5_scan_bwd", lambda p: s5_scan_bwd(
        dy5, proj, xs_re, xs_im, wbr, wbi, wcr, wci, abr, abi, drow, plan=p))
    g_are, g_aim, g_ldt, g_bre, g_bim, g_cre, g_cim = s5_params_bwd(are, aim, ldt, bre, bim, cre, cim,
                                                                   (gwbr, gwbi, gwcr, gwci, gabr, gabi))
    small = {
        "norm_ffn": jnp.concatenate([g_nf0, g_nf1], axis=0), "norm_final": g_nfinal.reshape(D),
        "s5_A_re": g_are.reshape(1, 32, 64), "s5_A_im": g_aim.reshape(1, 32, 64), "s5_log_dt": g_ldt.reshape(1, 32),
        "s5_B_re": g_bre.reshape(1, 32, 64, 16), "s5_B_im": g_bim.reshape(1, 32, 64, 16),
        "s5_C_re": jnp.swapaxes(g_cre.reshape(1, 32, 64, 16), 2, 3), "s5_C_im": jnp.swapaxes(g_cim.reshape(1, 32, 64, 16), 2, 3),
        "s5_D": g_d.reshape(1, 32, 16), "s5_glu_b": g_glu_b, "ffn_conv_b": jnp.concatenate([g_cb0, g_cb1], axis=0),
    }
    if C is not None:
        C.small["small_early"] = _pack(small, SMALL_EARLY)
    dproj, g_gamma, g_hnorm = hosted(C, "hgrn_bwd", lambda p: hgrn_bwd(proj, sm["hgrn_gamma"], sm["hgrn_norm"], ssave, dmix, du,
                                                                       plan=p))
    dhn0 = hosted(C, "mix_in_dx", lambda p: matmul(dproj, W[("mix_w_in", 0)], mode="nt", tm=1024, tn=1024, tk=2560, plan=p,
                                                  name="mix_in_dx"))
    G[("mix_w_in", 0)] = matmul(hn0, dproj, mode="tn", tm=1024, tn=1280, tk=L, out_dtype=BF, name="mix_in_dw")
    gx, g_nm0 = hosted(C, "rms_mix_bwd0", lambda p: rms_bwd(x, nm[0:1], [dhn0], dh, "rms_mix_bwd0", plan=p))
    small.update({"norm_mix": jnp.concatenate([g_nm0, g_nm1], axis=0), "hgrn_gamma": g_gamma, "hgrn_norm": g_hnorm})
    if C is not None:
        C.small["small_late"] = _pack(small, SMALL_LATE)
    return loss, gx, G, small


BIG = ("mix_w_in", "mix_w_out", "s5_glu_w", "att_w_qkv", "att_w_o", "ffn_w_in", "ffn_w_out", "ffn_conv_w")
SMALL = ("norm_mix", "norm_ffn", "norm_final", "s5_A_re", "s5_A_im", "s5_log_dt", "s5_B_re", "s5_B_im", "s5_C_re", "s5_C_im",
         "s5_D", "s5_glu_b", "hgrn_gamma", "hgrn_norm", "ffn_conv_b")
SMALL_LATE = ("norm_mix", "hgrn_gamma", "hgrn_norm")
SMALL_EARLY = tuple(n for n in SMALL if n not in SMALL_LATE)


def cast_bf16(w, name, plan=None):
    nl, r, c = w.shape
    w2 = w.reshape(nl * r, c)
    tr = 256 if (nl * r) % 256 == 0 else nl * r

    def body(w_ref, o_ref):
        o_ref[...] = w_ref[...].astype(BF)

    out = pcall(body, plan, grid=(nl * r // tr,), in_specs=[pl.BlockSpec((tr, c), lambda i: (i, 0))],
                out_specs=pl.BlockSpec((tr, c), lambda i: (i, 0)), out_shape=S((nl * r, c), BF),
                sem=("parallel",), name=name, args=[w2])
    return out.reshape(nl, r, c)


DIRECT = ("ffn_conv_w", "att_w_o", "s5_glu_w", "mix_w_out")

SCHEDULE = {
    "cast_ffn_w_in": [("G", "mix_w_in", 0)],
    "mix_in": [("G", "mix_w_out", 0), ("G", "s5_glu_w", 0)],
    "s5_scan_fwd": [("G", "ffn_w_in", 0, (0, 2))],
    "hgrn_fwd": [("G", "ffn_w_in", 0, (1, 2)), ("G", "ffn_conv_w", 0), ("G", "ffn_conv_w", 1), ("G", "att_w_qkv", 0, (0, 2))],
    "ffn_in0": [("G", "ffn_w_out", 0)],
    "convact_fwd0": [("G", "att_w_qkv", 0, (1, 2))],
    "att_qkv0": [("G", "att_w_o", 0)],
    "attn_fwd0": [("G", "ffn_w_in", 1, (0, 2))],
    "attn_fwd1": [("G", "ffn_w_in", 1, (1, 2))],
    "attn_fwd2": [("G", "ffn_w_out", 1)],
    "convact_bwd1": [("P", "ffn_w_out", 1)],
    "ffn_in_dx1": [("A", "ffn_w_out", 1, (0, 2))],
    "ffn_in_dw1": [("A", "ffn_w_out", 1, (1, 2))],
    "rms_ffn_bwd1": [("P", "ffn_w_in", 1)],
    "attn_merge_bwd": [("A", "ffn_conv_w", 1), ("B", "ffn_w_out", 1)],
    "attn_bwd0": [("A", "ffn_w_in", 1, (0, 2)), ("A", "att_w_o", 0)],
    "attn_bwd1": [("A", "ffn_w_in", 1, (1, 2)), ("B", "att_w_o", 0), ("B", "ffn_conv_w", 1)],
    "attn_bwd2": [("B", "ffn_w_in", 1)],
    "rms_mix_bwd1": [("P", "att_w_qkv", 0)],
    "ffn_out_dx0": [("A", "att_w_qkv", 0, (0, 4))],
    "ffn_out_dw0": [("A", "att_w_qkv", 0, (1, 4))],
    "convact_bwd0": [("A", "att_w_qkv", 0, (2, 4)), ("A", "att_w_qkv", 0, (3, 4)), ("P", "ffn_w_out", 0)],
    "ffn_in_dx0": [("A", "ffn_w_out", 0, (0, 2)), ("B", "att_w_qkv", 0)],
    "ffn_in_dw0": [("A", "ffn_w_out", 0, (1, 2))],
    "rms_ffn_bwd0": [("P", "ffn_w_in", 0), ("B", "ffn_w_out", 0)],
    "s5_scan_bwd": [("A", "ffn_w_in", 0, (0, 2)), ("A", "ffn_conv_w", 0)],
    "hgrn_bwd": [("A", "ffn_w_in", 0, (1, 2)), ("A", "mix_w_out", 0), ("A", "s5_glu_w", 0), ("B", "ffn_conv_w", 0),
                 ("A", "small_early", 0)],
    "mix_in_dx": [("B", "ffn_w_in", 0), ("B", "mix_w_out", 0), ("B", "s5_glu_w", 0), ("B", "small_early", 0)],
    "rms_mix_bwd0": [("P", "mix_w_in", 0)],
    "adam_att_w_o": [("A", "mix_w_in", 0), ("A", "small_late", 0)],
    "adam_s5_glu_w": [("B", "mix_w_in", 0), ("B", "small_late", 0)],
}


class Comm:
    def __init__(self, shards, shapes):
        self.shards, self.shapes = shards, shapes
        self.W, self.grads, self.slots = {}, {}, {}
        self.sib, self.pair = {}, {}
        self.small = {}

    def plan(self, host):
        items = SCHEDULE.get(host)
        if not items:
            return None
        p = Plan()
        for it in items:
            kind, name, l = it[:3]
            part, parts = it[3] if len(it) > 3 else (0, 1)
            if name.startswith("small"):
                sg = self.small[name]
                kdst = p.buf("slots:" + name, arr=self.slots.get(name), shape=S((8,) + sg.shape, f32), write=True)
                if kind == "A":
                    ReduceOp(p, p.buf("g:" + name, arr=sg), kdst, None, sg.shape, False, 0, 0, whole=True)
                else:
                    ForwardOp(p, kdst, None, whole=True)
                continue
            nl, R, C_ = self.shapes[name]
            rows = name in ROW_SHARDED
            r0, nr = part * (R // parts), R // parts
            if kind == "G":
                sh = self.shards[name]
                kdst = p.buf(f"W:{name}:{l}", arr=self.W.get((name, l)), shape=S((4 * R, C_) if rows else (R, 4 * C_), sh.dtype),
                             write=True)
                GatherOp(p, p.buf("shard:" + name, arr=sh), kdst, l, self.shapes[name], rows, r0, nr, split=(nr % 32 == 0))
            elif name in DIRECT:
                g = self.grads[(name, l)]
                kdst = p.buf("slots:" + name, arr=self.slots.get(name), shape=S((8, nl, R, C_), g.dtype), write=True)
                if kind == "A":
                    ReduceOp(p, p.buf(f"g:{name}:{l}", arr=g), kdst, l, self.shapes[name], rows, r0, nr)
                else:
                    ForwardOp(p, kdst, l)
            elif kind == "P":
                g = self.grads[(name, l)]
                ksib = p.buf(f"sib:{name}:{l}", shape=S((4 * R // 2, C_) if rows else (R // 2, 4 * C_), g.dtype), write=True)
                PairOp(p, p.buf(f"g:{name}:{l}", arr=g), ksib, self.shapes[name], rows)
            else:
                if (name, l) not in self.pair:
                    self.pair[(name, l)] = pair_sum(self.grads[(name, l)], self.sib[(name, l)], rows, R, f"pair_sum_{name}{l}")
                h = self.pair[(name, l)]
                kdst = p.buf("slots:" + name, arr=self.slots.get(name), shape=S((4, nl, R, C_), h.dtype), write=True)
                if kind == "A":
                    ReduceOp(p, p.buf(f"h:{name}:{l}", arr=h), kdst, l, self.shapes[name], rows, r0 // 2, nr // 2, half=True)
                else:
                    HalfForwardOp(p, kdst, l, self.shapes[name])
        return p

    def done(self, p):
        for k, arr in p.out.items():
            tag, name = k.split(":")[:2]
            if tag == "W":
                self.W[(name, int(k.split(":")[2]))] = arr
            elif tag == "sib":
                self.sib[(name, int(k.split(":")[2]))] = arr
            else:
                self.slots[name] = arr


def _adamw(w, g, m, v):
    m = B1 * m + (1.0 - B1) * g
    v = B2 * v + (1.0 - B2) * jnp.square(g)
    m_hat = m / (1.0 - B1 ** STEP)
    v_hat = v / (1.0 - B2 ** STEP)
    return -LR * (m_hat / (jnp.sqrt(v_hat) + AEPS) + WD * w), m, v


def adam_big(w, m, v, slots, name, plan=None):
    nl, R, C = w.shape
    ns = slots.shape[0]
    tr = 128 if R % 128 == 0 else (64 if R % 64 == 0 else R)

    def body(w_ref, m_ref, v_ref, s_ref, g_ref, d_ref, nm_ref, nv_ref):
        g = s_ref[0].astype(f32)
        for s in range(1, ns):
            g = g + s_ref[s].astype(f32)
        d, nm_, nv_ = _adamw(w_ref[...], g, m_ref[...], v_ref[...])
        g_ref[...] = g
        d_ref[...] = d
        nm_ref[...] = nm_
        nv_ref[...] = nv_

    blk = pl.BlockSpec((None, tr, C), lambda l, i: (l, i, 0))
    return pcall(body, plan, grid=(nl, R // tr),
                 in_specs=[blk, blk, blk, pl.BlockSpec((ns, None, tr, C), lambda l, i: (0, l, i, 0))],
                 out_specs=[blk] * 4, out_shape=[S((nl, R, C), f32)] * 4,
                 sem=("parallel", "parallel"), name=name, args=[w, m, v, slots])


def sum_slots(slots, name):
    R = slots.shape[1]

    def body(s_ref, g_ref):
        g = s_ref[0]
        for s in range(1, 8):
            g = g + s_ref[s]
        g_ref[...] = g

    return pl.pallas_call(
        body, grid=(R // 256,), in_specs=[pl.BlockSpec((8, 256, 128), lambda i: (0, i, 0))],
        out_specs=pl.BlockSpec((256, 128), lambda i: (i, 0)), out_shape=S((R, 128), f32),
        compiler_params=_cp(("parallel",)), name=name)(slots)


SMALL2D = {"norm_mix": (2, 1024), "norm_ffn": (2, 1024), "norm_final": (1, 1024), "s5_A_re": (32, 64), "s5_A_im": (32, 64),
           "s5_log_dt": (1, 32), "s5_B_re": (2048, 16), "s5_B_im": (2048, 16), "s5_C_re": (512, 64), "s5_C_im": (512, 64),
           "s5_D": (32, 16), "s5_glu_b": (1, 512), "hgrn_gamma": (2, 512), "hgrn_norm": (1, 512), "ffn_conv_b": (2, 5632)}


def adam_small(w, m, v, g, names, name):
    n = len(names)

    def body(*refs):
        for i in range(n):
            w_ref, m_ref, v_ref, g_ref = refs[4 * i:4 * i + 4]
            d_ref, nm_ref, nv_ref = refs[4 * n + 3 * i:4 * n + 3 * i + 3]
            d, nm_, nv_ = _adamw(w_ref[...], g_ref[...], m_ref[...], v_ref[...])
            d_ref[...] = d
            nm_ref[...] = nm_
            nv_ref[...] = nv_

    args = [t[k] for k in names for t in (w, m, v, g)]
    outs = pl.pallas_call(body, out_shape=[S(SMALL2D[k], f32) for k in names for _ in range(3)],
                          compiler_params=_cp(), name=name)(*args)
    return {k: tuple(outs[3 * i:3 * i + 3]) for i, k in enumerate(names)}


def _pack(d, names):
    flat = jnp.concatenate([d[n].reshape(-1) for n in names])
    n = flat.shape[0]
    rows = -(-n // (256 * 128)) * 256
    return jnp.pad(flat, (0, rows * 128 - n)).reshape(rows, 128)


def _unpack(p, like, names):
    flat = p.reshape(-1)
    out, off = {}, 0
    for n in names:
        sz = math.prod(like[n].shape)
        out[n] = flat[off:off + sz].reshape(like[n].shape)
        off += sz
    return out


def kernel(x, positions, norm_mix, norm_ffn, norm_final, mix_w_in, mix_w_out, s5_A_re, s5_A_im, s5_log_dt, s5_B_re, s5_B_im, s5_C_re, s5_C_im, s5_D, s5_glu_w, s5_glu_b, hgrn_gamma, hgrn_norm, att_w_qkv, att_w_o, ffn_w_in, ffn_conv_w, ffn_conv_b, ffn_w_out, loss_target, m_norm_mix, m_norm_ffn, m_norm_final, m_mix_w_in, m_mix_w_out, m_s5_A_re, m_s5_A_im, m_s5_log_dt, m_s5_B_re, m_s5_B_im, m_s5_C_re, m_s5_C_im, m_s5_D, m_s5_glu_w, m_s5_glu_b, m_hgrn_gamma, m_hgrn_norm, m_att_w_qkv, m_att_w_o, m_ffn_w_in, m_ffn_conv_w, m_ffn_conv_b, m_ffn_w_out, v_norm_mix, v_norm_ffn, v_norm_final, v_mix_w_in, v_mix_w_out, v_s5_A_re, v_s5_A_im, v_s5_log_dt, v_s5_B_re, v_s5_B_im, v_s5_C_re, v_s5_C_im, v_s5_D, v_s5_glu_w, v_s5_glu_b, v_hgrn_gamma, v_hgrn_norm, v_att_w_qkv, v_att_w_o, v_ffn_w_in, v_ffn_conv_w, v_ffn_conv_b, v_ffn_w_out):
    a = dict(locals())
    weights = BIG + SMALL
    w = {n: a[n] for n in weights}
    m = {n: a["m_" + n] for n in weights}
    v = {n: a["v_" + n] for n in weights}
    shards = {"ffn_conv_w": ffn_conv_w}
    C = Comm(shards, {n: w[n].shape for n in BIG})
    for n in ("mix_w_in", "ffn_w_in", "mix_w_out", "s5_glu_w", "ffn_w_out", "att_w_qkv", "att_w_o"):
        shards[n] = hosted(C, "cast_" + n, lambda p: cast_bf16(w[n], "cast_" + n, plan=p))
    sm = {n: w[n] for n in SMALL}
    sm["ffn_conv_b3"] = ffn_conv_b.reshape(2, 1, 2 * DFF)
    loss, gx, _, _ = local_step(x[0], positions.reshape(L, 1), loss_target[0], sm, C.W, C)
    res = {}
    for n in ("att_w_o", "s5_glu_w", "ffn_w_in", "ffn_w_out", "att_w_qkv", "mix_w_out", "ffn_conv_w", "mix_w_in"):
        res[n] = hosted(C, "adam_" + n, lambda p: adam_big(w[n], m[n], v[n], C.slots[n], "adam_" + n, plan=p))
    for names, key in ((SMALL_EARLY, "small_early"), (SMALL_LATE, "small_late")):
        g = _unpack(sum_slots(C.slots[key], "sum_" + key), w, names)

        def two_d(t):
            return {n: t[n].reshape(SMALL2D[n]) for n in names}

        upd = adam_small(two_d(w), two_d(m), two_d(v), two_d(g), names, "adam_" + key)
        for n in names:
            res[n] = (g[n],) + tuple(t.reshape(w[n].shape) for t in upd[n])
    total = lax.psum(loss[0, 0], ("x", "y", "c"))
    order = ("norm_mix", "norm_ffn", "norm_final", "mix_w_in", "mix_w_out", "s5_A_re", "s5_A_im", "s5_log_dt", "s5_B_re", "s5_B_im",
             "s5_C_re", "s5_C_im", "s5_D", "s5_glu_w", "s5_glu_b", "hgrn_gamma", "hgrn_norm", "att_w_qkv", "att_w_o", "ffn_w_in",
             "ffn_conv_w", "ffn_conv_b", "ffn_w_out")
    return (total, gx[None], *[res[n][0] for n in order], *[res[n][1] for n in order], *[res[n][2] for n in order],
            *[res[n][3] for n in order])
```

```python
import functools
import math

import numpy as np
import jax
import jax.numpy as jnp
from jax import lax
from jax.experimental import pallas as pl
from jax.experimental.pallas import tpu as pltpu

f32 = jnp.float32
BF = jnp.bfloat16
HI = lax.Precision.HIGHEST
S = jax.ShapeDtypeStruct
MESH = pl.DeviceIdType.MESH

L = 2048
D = 1024
EPS = 1e-6
S5W = 512
NST = 2048
HGC = 64
HGB = 32
DFF = 2816
ROPE_THETA = 500000.0
LR, B1, B2, AEPS, WD, STEP = 0.001, 0.9, 0.999, 1e-08, 0.01, 10
VMEM_LIMIT = 56 * 1024 * 1024


def _cp(sem=None):
    return pltpu.CompilerParams(dimension_semantics=sem, vmem_limit_bytes=VMEM_LIMIT)


ANY = pl.BlockSpec(memory_space=pl.ANY)
ROW_SHARDED = ("mix_w_out", "s5_glu_w", "ffn_w_out")


def _coords():
    x, y, c = lax.axis_index("x"), lax.axis_index("y"), lax.axis_index("c")
    return x, y, c, 2 * x + y, [(1 - x, y), (x, 1 - y), (1 - x, 1 - y)]


def _rows(start, n):
    return pl.ds(start if isinstance(start, int) else pl.multiple_of(start, 8), n)


def _cols(q, n):
    return pl.ds(pl.multiple_of(q * n, 128), n)


class Plan:
    def __init__(self):
        self.bufs, self.ops, self.nsem, self.out = {}, [], 0, {}

    def buf(self, key, arr=None, shape=None, write=False):
        b = self.bufs.setdefault(key, dict(arr=arr, shape=shape, write=False))
        b["write"] = b["write"] or write
        return key

    def add(self, op):
        op.base = self.nsem
        self.nsem += op.nsem
        self.ops.append(op)


class GatherOp:
    nsem = 13

    def __init__(self, plan, ksrc, kdst, l, shard_shape, rows, r0, nr, split):
        self.ksrc, self.kdst, self.l, (_, self.R, self.C), self.rows, self.r0, self.nr, self.split = (
            ksrc, kdst, l, shard_shape, rows, r0, nr, split)
        self.h = nr // 2 if split else nr
        plan.add(self)

    def _dst(self, R_, q, start, n):
        if self.rows:
            return R_[self.kdst].at[_rows(q * self.R + start, n), :]
        return R_[self.kdst].at[_rows(start, n), _cols(q, self.C)]

    def _mine(self, c):
        return self.r0 + (c * self.h if self.split else 0)

    def _theirs(self, c):
        return self.r0 + ((1 - c) * self.h if self.split else 0)

    def _copies(self, R_, sems):
        x, y, c, me, others = _coords()
        src = R_[self.ksrc]
        local = pltpu.make_async_copy(src.at[self.l, _rows(self.r0, self.nr), :], self._dst(R_, me, self.r0, self.nr),
                                      sems.at[self.base + 12])
        send, fwd = [], []
        for k, (px, py) in enumerate(others):
            q = 2 * px + py
            send.append((
                pltpu.make_async_remote_copy(src.at[self.l, _rows(self._mine(c), self.h), :], self._dst(R_, me, self._mine(c), self.h),
                                             sems.at[self.base + k], sems.at[self.base + 3 + k], device_id=(px, py, c), device_id_type=MESH),
                pltpu.make_async_remote_copy(src.at[self.l, _rows(self._mine(c), self.h), :], self._dst(R_, q, self._mine(c), self.h),
                                             sems.at[self.base + k], sems.at[self.base + 3 + k], device_id=(px, py, c), device_id_type=MESH)))
            fwd.append((
                pltpu.make_async_remote_copy(self._dst(R_, q, self._mine(c), self.h), self._dst(R_, q, self._mine(c), self.h),
                                             sems.at[self.base + 6 + k], sems.at[self.base + 9 + k], device_id=(x, y, 1 - c), device_id_type=MESH),
                pltpu.make_async_remote_copy(self._dst(R_, q, self._theirs(c), self.h), self._dst(R_, q, self._theirs(c), self.h),
                                             sems.at[self.base + 6 + k], sems.at[self.base + 9 + k], device_id=(x, y, 1 - c), device_id_type=MESH)))
        return local, send, fwd

    def start(self, R_, sems):
        local, send, _ = self._copies(R_, sems)
        local.start()
        for out, _ in send:
            out.start()

    def finish(self, R_, sems):
        local, send, fwd = self._copies(R_, sems)
        for k in range(3):
            send[k][1].wait_recv()
            if self.split:
                fwd[k][0].start()
        for k in range(3):
            if self.split:
                fwd[k][1].wait_recv()
                fwd[k][0].wait_send()
            send[k][0].wait_send()
        local.wait()


class ReduceOp:
    nsem = 7

    def __init__(self, plan, ksrc, kdst, l, shard_shape, rows, r0, nr, whole=False, half=False):
        self.ksrc, self.kdst, self.l, (self.R, self.C), self.rows, self.r0, self.nr, self.whole, self.half = (
            ksrc, kdst, l, shard_shape[-2:], rows, r0, nr, whole, half)
        plan.add(self)

    def _piece(self, R_, q):
        g = R_[self.ksrc]
        if self.whole:
            return g
        if self.rows:
            return g.at[_rows(q * (self.R // 2 if self.half else self.R) + self.r0, self.nr), :]
        return g.at[_rows(self.r0, self.nr), _cols(q, self.C)]

    def _slot(self, R_, q, c):
        if self.whole:
            return R_[self.kdst].at[2 * q + c]
        if self.half:
            return R_[self.kdst].at[q, self.l, _rows(c * (self.R // 2) + self.r0, self.nr), :]
        return R_[self.kdst].at[2 * q + c, self.l, _rows(self.r0, self.nr), :]

    def _copies(self, R_, sems):
        x, y, c, me, others = _coords()
        local = pltpu.make_async_copy(self._piece(R_, me), self._slot(R_, me, c), sems.at[self.base + 6])
        send = []
        for k, (px, py) in enumerate(others):
            q = 2 * px + py
            send.append((
                pltpu.make_async_remote_copy(self._piece(R_, q), self._slot(R_, me, c), sems.at[self.base + k],
                                             sems.at[self.base + 3 + k], device_id=(px, py, c), device_id_type=MESH),
                pltpu.make_async_remote_copy(self._piece(R_, q), self._slot(R_, q, c), sems.at[self.base + k],
                                             sems.at[self.base + 3 + k], device_id=(px, py, c), device_id_type=MESH)))
        return local, send

    def start(self, R_, sems):
        local, send = self._copies(R_, sems)
        local.start()
        for out, _ in send:
            out.start()

    def finish(self, R_, sems):
        local, send = self._copies(R_, sems)
        local.wait()
        for out, inn in send:
            inn.wait_recv()
            out.wait_send()


class ForwardOp:
    nsem = 8

    def __init__(self, plan, kdst, l, whole=False):
        self.kdst, self.l, self.whole = kdst, l, whole
        plan.add(self)

    def _slot(self, R_, s):
        return R_[self.kdst].at[s] if self.whole else R_[self.kdst].at[s, self.l]

    def _copies(self, R_, sems):
        x, y, c, me, others = _coords()
        return [(pltpu.make_async_remote_copy(self._slot(R_, 2 * q + c), self._slot(R_, 2 * q + c), sems.at[self.base + q],
                                              sems.at[self.base + 4 + q], device_id=(x, y, 1 - c), device_id_type=MESH),
                 pltpu.make_async_remote_copy(self._slot(R_, 2 * q + 1 - c), self._slot(R_, 2 * q + 1 - c), sems.at[self.base + q],
                                              sems.at[self.base + 4 + q], device_id=(x, y, 1 - c), device_id_type=MESH))
                for q in range(4)]

    def start(self, R_, sems):
        for out, _ in self._copies(R_, sems):
            out.start()

    def finish(self, R_, sems):
        for out, inn in self._copies(R_, sems):
            inn.wait_recv()
            out.wait_send()


class PairOp:
    nsem = 8

    def __init__(self, plan, ksrc, kdst, shard_shape, rows):
        self.ksrc, self.kdst, (self.R, self.C), self.rows = ksrc, kdst, shard_shape[-2:], rows
        plan.add(self)

    def _copies(self, R_, sems):
        x, y, c, me, others = _coords()
        g, dst, h = R_[self.ksrc], R_[self.kdst], self.R // 2
        out = []
        for q in range(4 if self.rows else 1):
            src = g.at[_rows(q * self.R + (1 - c) * h, h), :]
            land = dst.at[_rows(q * h, h), :]
            out.append(pltpu.make_async_remote_copy(src, land, sems.at[self.base + q], sems.at[self.base + 4 + q],
                                                    device_id=(x, y, 1 - c), device_id_type=MESH))
        return out

    def start(self, R_, sems):
        for cp in self._copies(R_, sems):
            cp.start()

    def finish(self, R_, sems):
        for cp in self._copies(R_, sems):
            cp.wait_recv()
            cp.wait_send()


class HalfForwardOp:
    nsem = 2

    def __init__(self, plan, kdst, l, shard_shape):
        self.kdst, self.l, self.R = kdst, l, shard_shape[-2]
        plan.add(self)

    def _copy(self, R_, sems, core):
        x, y, c, me, others = _coords()
        part = R_[self.kdst].at[:, self.l, _rows((c if core == "mine" else 1 - c) * (self.R // 2), self.R // 2), :]
        return pltpu.make_async_remote_copy(part, part, sems.at[self.base], sems.at[self.base + 1],
                                            device_id=(x, y, 1 - c), device_id_type=MESH)

    def start(self, R_, sems):
        self._copy(R_, sems, "mine").start()

    def finish(self, R_, sems):
        self._copy(R_, sems, "theirs").wait_recv()
        self._copy(R_, sems, "mine").wait_send()


def pair_sum(g, gsib, rows, R, name):
    h = R // 2
    W = g.shape[1]
    tr = h if h * W * 2 <= 2 ** 21 else 128
    nq = 4 if rows else 1

    def body(c_ref, a_ref, b_ref, o_ref):
        o_ref[...] = (a_ref[...].astype(f32) + b_ref[...].astype(f32)).astype(o_ref.dtype)

    half = pl.BlockSpec((tr, W), lambda q, i, c_ref: (q * (h // tr) + i, 0))
    mine = pl.BlockSpec((tr, W), lambda q, i, c_ref: (q * (R // tr) + c_ref[0] * (h // tr) + i, 0))
    return pl.pallas_call(
        body, grid_spec=pltpu.PrefetchScalarGridSpec(num_scalar_prefetch=1, grid=(nq, h // tr), in_specs=[mine, half],
                                                     out_specs=half),
        out_shape=S(gsib.shape, g.dtype), compiler_params=_cp(("parallel", "parallel")),
        name=name)(lax.axis_index("c").reshape(1).astype(jnp.int32), g, gsib)


def pcall(body, plan, *, grid, in_specs, out_specs, out_shape, scratch_shapes=(), sem, name, args):
    multi = isinstance(out_shape, (list, tuple))
    if plan is None or not plan.ops:
        return pl.pallas_call(body, grid=grid, in_specs=in_specs, out_specs=out_specs, out_shape=out_shape,
                              scratch_shapes=list(scratch_shapes), compiler_params=_cp(sem), name=name)(*args)
    outs = list(out_shape) if multi else [out_shape]
    ospecs = list(out_specs) if multi else [out_specs]
    kin = [k for k, b in plan.bufs.items() if b["arr"] is not None]
    kout = [k for k, b in plan.bufs.items() if b["write"]]
    n_in, n_out, n_scr = len(in_specs), len(outs), len(scratch_shapes)

    def wrapped(*refs):
        o0 = n_in + len(kin)
        s0 = o0 + n_out + len(kout)
        R_ = dict(zip(kin, refs[n_in:o0]))
        R_.update(zip(kout, refs[o0 + n_out:s0]))
        sems = refs[s0 + n_scr]
        first = functools.reduce(jnp.logical_and, [pl.program_id(d) == 0 for d in range(len(grid))])
        last = functools.reduce(jnp.logical_and, [pl.program_id(d) == grid[d] - 1 for d in range(len(grid))])

        @pl.when(first)
        def _():
            for op in plan.ops:
                op.start(R_, sems)

        body(*refs[:n_in], *refs[o0:o0 + n_out], *refs[s0:s0 + n_scr])

        @pl.when(last)
        def _():
            for op in plan.ops:
                op.finish(R_, sems)

    def shape_of(k):
        b = plan.bufs[k]
        return S(b["arr"].shape, b["arr"].dtype) if b["arr"] is not None else b["shape"]

    res = pl.pallas_call(
        wrapped, grid=grid, in_specs=list(in_specs) + [ANY] * len(kin), out_specs=ospecs + [ANY] * len(kout),
        out_shape=outs + [shape_of(k) for k in kout],
        scratch_shapes=list(scratch_shapes) + [pltpu.SemaphoreType.DMA((plan.nsem,))],
        input_output_aliases={n_in + kin.index(k): n_out + kout.index(k) for k in kout if plan.bufs[k]["arr"] is not None},
        compiler_params=pltpu.CompilerParams(dimension_semantics=("arbitrary",) * len(grid), vmem_limit_bytes=VMEM_LIMIT,
                                             has_side_effects=True),
        name=name)(*args, *[plan.bufs[k]["arr"] for k in kin])
    plan.out = dict(zip(kout, res[n_out:]))
    return list(res[:n_out]) if multi else res[0]


def _dg(a, b, ca, cb):
    return lax.dot_general(a.astype(BF), b.astype(BF), (((ca,), (cb,)), ((), ())), preferred_element_type=f32)


@jax.custom_vjp
def dot_nn(a, b):
    return _dg(a, b, 1, 0)


@jax.custom_vjp
def dot_nt(a, b):
    return _dg(a, b, 1, 1)


@jax.custom_vjp
def dot_tn(a, b):
    return _dg(a, b, 0, 0)


dot_nn.defvjp(lambda a, b: (dot_nn(a, b), (a, b)),
              lambda r, g: (dot_nt(g, r[1]).astype(r[0].dtype), dot_tn(r[0], g).astype(r[1].dtype)))
dot_nt.defvjp(lambda a, b: (dot_nt(a, b), (a, b)),
              lambda r, g: (dot_nn(g, r[1]).astype(r[0].dtype), dot_tn(g, r[0]).astype(r[1].dtype)))
dot_tn.defvjp(lambda a, b: (dot_tn(a, b), (a, b)),
              lambda r, g: (dot_nt(r[1], g).astype(r[0].dtype), dot_nn(r[0], g).astype(r[1].dtype)))


def matmul(a, b, *, mode, tm, tn, tk, out_dtype=f32, add=None, b_lead=None, a_spec=None, b_spec=None, dims=None, plan=None, name):
    a_over, b_over = a_spec, b_spec
    if mode == "nn":
        (M, K), N = a.shape[-2:], b.shape[-1]
        a_spec = pl.BlockSpec((tm, tk), lambda i, j, k: (i, k))
        b_blk, b_idx, ca, cb = (tk, tn), (lambda i, j, k: (k, j)), 1, 0
    elif mode == "nt":
        (M, K), N = a.shape[-2:], b.shape[-2]
        a_spec = pl.BlockSpec((tm, tk), lambda i, j, k: (i, k))
        b_blk, b_idx, ca, cb = (tn, tk), (lambda i, j, k: (j, k)), 1, 1
    else:
        (K, M), N = a.shape[-2:], b.shape[-1]
        a_spec = pl.BlockSpec((tk, tm), lambda i, j, k: (k, i))
        b_blk, b_idx, ca, cb = (tk, tn), (lambda i, j, k: (k, j)), 0, 0
    if dims is not None:
        M, N, K = dims
    assert M % tm == 0 and N % tn == 0 and K % tk == 0, (name, M, N, K, tm, tn, tk)
    if b_lead is None:
        b_spec = pl.BlockSpec(b_blk, b_idx)
    else:
        b_spec = pl.BlockSpec((None,) + b_blk, lambda i, j, k: (b_lead,) + b_idx(i, j, k))
    if a_over is not None:
        a_spec = a_over
    if b_over is not None:
        b_spec = b_over
    nk = K // tk
    has_add = add is not None

    def body(*refs):
        a_ref, b_ref = refs[0], refs[1]
        add_ref = refs[2] if has_add else None
        o_ref = refs[2 + has_add]
        p = _dg(a_ref[...], b_ref[...], ca, cb)

        def fin(v):
            if has_add:
                v = v + add_ref[...].astype(f32)
            o_ref[...] = v.astype(o_ref.dtype)

        if nk == 1:
            fin(p)
        else:
            acc = refs[3 + has_add]
            k = pl.program_id(2)

            @pl.when(k == 0)
            def _():
                acc[...] = p

            @pl.when(k > 0)
            def _():
                acc[...] += p

            @pl.when(k == nk - 1)
            def _():
                fin(acc[...])

    in_specs = [a_spec, b_spec]
    args = [a, b]
    if has_add:
        in_specs.append(pl.BlockSpec((tm, tn), lambda i, j, k: (i, j)))
        args.append(add)
    return pcall(body, plan, grid=(M // tm, N // tn, nk), in_specs=in_specs,
                 out_specs=pl.BlockSpec((tm, tn), lambda i, j, k: (i, j)), out_shape=S((M, N), out_dtype),
                 scratch_shapes=[pltpu.VMEM((tm, tn), f32)] if nk > 1 else [],
                 sem=("parallel", "parallel", "arbitrary"), name=name, args=args)


def _rms(xv, gv):
    return xv * lax.rsqrt(jnp.mean(xv * xv, axis=-1, keepdims=True) + EPS) * gv


TR = 512


def rms_fwd(x, g, name):
    def body(x_ref, g_ref, o_ref):
        o_ref[...] = _rms(x_ref[...], g_ref[...]).astype(o_ref.dtype)

    return pl.pallas_call(
        body, grid=(L // TR,),
        in_specs=[pl.BlockSpec((TR, D), lambda i: (i, 0)), pl.BlockSpec((1, D), lambda i: (0, 0))],
        out_specs=pl.BlockSpec((TR, D), lambda i: (i, 0)), out_shape=S((L, D), BF),
        compiler_params=_cp(("parallel",)), name=name)(x, g)


def rms_bwd(x, g, dys, dres, name, plan=None):
    nd = len(dys)

    def body(*refs):
        x_ref, g_ref = refs[0], refs[1]
        dr_ref, dh_ref, dg_ref = refs[2 + nd:]
        dy = refs[2][...].astype(f32)
        for r in refs[3:2 + nd]:
            dy = dy + r[...].astype(f32)
        _, vjp = jax.vjp(_rms, x_ref[...], g_ref[...])
        dx, dg = vjp(dy)
        dh_ref[...] = dr_ref[...] + dx

        @pl.when(pl.program_id(0) == 0)
        def _():
            dg_ref[...] = jnp.zeros_like(dg_ref)

        dg_ref[...] += dg

    row = pl.BlockSpec((TR, D), lambda i: (i, 0))
    vec = pl.BlockSpec((1, D), lambda i: (0, 0))
    return pcall(body, plan, grid=(L // TR,), in_specs=[row, vec] + [row] * (nd + 1), out_specs=[row, vec],
                 out_shape=[S((L, D), f32), S((1, D), f32)], sem=("arbitrary",), name=name, args=[x, g, *dys, dres])


def loss_head(h, g, tgt):
    def f(hv, gv, tv):
        y = _rms(hv, gv)
        return 0.5 * jnp.sum(jnp.mean(jnp.square(y - tv), axis=-1))

    def body(h_ref, g_ref, t_ref, l_ref, dh_ref, dg_ref):
        val, vjp = jax.vjp(f, h_ref[...], g_ref[...], t_ref[...])
        dh, dg, _ = vjp(jnp.ones((), f32))
        dh_ref[...] = dh

        @pl.when(pl.program_id(0) == 0)
        def _():
            dg_ref[...] = jnp.zeros_like(dg_ref)
            l_ref[...] = jnp.zeros_like(l_ref)

        dg_ref[...] += dg
        l_ref[...] += jnp.full((1, 128), val, f32)

    row = pl.BlockSpec((TR, D), lambda i: (i, 0))
    vec = pl.BlockSpec((1, D), lambda i: (0, 0))
    return pl.pallas_call(
        body, grid=(L // TR,), in_specs=[row, vec, row],
        out_specs=[pl.BlockSpec((1, 128), lambda i: (0, 0)), row, vec],
        out_shape=[S((1, 128), f32), S((L, D), f32), S((1, D), f32)],
        compiler_params=_cp(("arbitrary",)), name="loss_head")(h, g, tgt)


def _col_to_row(c):
    n = c.shape[0]
    t = jnp.broadcast_to(c, (n, 128)).T
    r = lax.broadcasted_iota(jnp.int32, (128, n), 0)
    return jnp.sum(jnp.where(r == 0, t, 0.0), axis=0, keepdims=True)


def _s5_param_map(are, aim, ldt_row, bre, bim, cre, cim):
    n = NST
    gi = lax.broadcasted_iota(jnp.int32, (n, 32), 0) // 64
    gj = lax.broadcasted_iota(jnp.int32, (n, 32), 1)
    ldt = jnp.sum(jnp.where(gi == gj, ldt_row, 0.0), axis=1, keepdims=True)
    dt = jnp.exp(ldt)
    mag = jnp.exp(are * dt)
    abr = mag * jnp.cos(aim * dt)
    abi = mag * jnp.sin(aim * dt)
    den = are * are + aim * aim
    nr, ni = abr - 1.0, abi
    cr = (nr * are + ni * aim) / den
    ci = (ni * are - nr * aim) / den
    bbr = cr * bre - ci * bim
    bbi = cr * bim + ci * bre
    tc = lax.broadcasted_iota(jnp.int32, (16, 128), 0)
    tl = lax.broadcasted_iota(jnp.int32, (16, 128), 1)
    T = (tl % 16 == tc).astype(f32)
    mr = (lax.broadcasted_iota(jnp.int32, (n, 128), 0) // 64) % 8
    mc = lax.broadcasted_iota(jnp.int32, (n, 128), 1) // 16
    mask = (mr == mc).astype(f32)

    def expand(v):
        return jnp.dot(v, T, precision=HI, preferred_element_type=f32) * mask

    return expand(bbr), expand(bbi), expand(cre), expand(cim), _col_to_row(abr), _col_to_row(abi)


def s5_params_fwd(are, aim, ldt_row, bre, bim, cre, cim):
    def body(*refs):
        outs = _s5_param_map(*[r[...] for r in refs[:7]])
        for o_ref, o in zip(refs[7:], outs):
            o_ref[...] = o

    return pl.pallas_call(
        body, out_shape=[S((NST, 128), f32)] * 4 + [S((1, NST), f32)] * 2,
        compiler_params=_cp(), name="s5_params_fwd")(are, aim, ldt_row, bre, bim, cre, cim)


def s5_params_bwd(are, aim, ldt_row, bre, bim, cre, cim, cots):
    def body(*refs):
        _, vjp = jax.vjp(_s5_param_map, *[r[...] for r in refs[:7]])
        gs = vjp(tuple(r[...] for r in refs[7:13]))
        for o_ref, o in zip(refs[13:], gs):
            o_ref[...] = o

    return pl.pallas_call(
        body, out_shape=[S((NST, 1), f32)] * 2 + [S((1, 32), f32)] + [S((NST, 16), f32)] * 4,
        compiler_params=_cp(), name="s5_params_bwd")(are, aim, ldt_row, bre, bim, cre, cim, *cots)


def _cpowers(ar, ai):
    out = [(ar, ai)]
    for _ in range(7):
        pr, pi = out[-1]
        out.append((pr * ar - pi * ai, pr * ai + pi * ar))
    return out


def _ctable(pw, rid, power):
    tr_ = jnp.zeros(rid.shape, f32)
    ti_ = jnp.zeros(rid.shape, f32)
    for r in range(8):
        pr, pi = pw[power(r) - 1]
        tr_ = jnp.where(rid == r, pr, tr_)
        ti_ = jnp.where(rid == r, pi, ti_)
    return tr_, ti_


NT5 = 4
RC = 256


def s5_scan_fwd(proj, wbr, wbi, wcr, wci, abr, abi, drow, plan=None):
    def body(u_ref, wbr_ref, wbi_ref, wcr_ref, wci_ref, ar_ref, ai_ref, d_ref, xr_ref, xi_ref, y_ref):
        wbr_v, wbi_v = wbr_ref[...], wbi_ref[...]
        for r in range(L // RC):
            rows = pl.ds(r * RC, RC)
            ub = u_ref[rows, :]
            xr_ref[rows, :] = dot_nt(ub, wbr_v)
            xi_ref[rows, :] = dot_nt(ub, wbi_v)
        pw = _cpowers(ar_ref[...], ai_ref[...])
        rid = lax.broadcasted_iota(jnp.int32, (8, 512), 0)
        tr_, ti_ = _ctable(pw, rid, lambda r: r + 1)

        def group(j, c):
            cr, ci = c
            rows = pl.ds(pl.multiple_of(j * 8, 8), 8)
            br, bi = xr_ref[rows, :], xi_ref[rows, :]
            for s in (1, 2, 4):
                pr, pi = pw[s - 1]
                sr = jnp.where(rid >= s, pltpu.roll(br, s, 0), 0.0)
                si = jnp.where(rid >= s, pltpu.roll(bi, s, 0), 0.0)
                br, bi = br + pr * sr - pi * si, bi + pr * si + pi * sr
            br, bi = br + tr_ * cr - ti_ * ci, bi + tr_ * ci + ti_ * cr
            xr_ref[rows, :] = br
            xi_ref[rows, :] = bi
            return br[7:8], bi[7:8]

        z = jnp.zeros((1, 512), f32)
        lax.fori_loop(0, L // 8, group, (z, z), unroll=2)
        wcr_v, wci_v, dv = wcr_ref[...], wci_ref[...], d_ref[...]
        for r in range(L // RC):
            rows = pl.ds(r * RC, RC)
            y_ref[rows, :] = (dot_nn(xr_ref[rows, :], wcr_v) - dot_nn(xi_ref[rows, :], wci_v)
                              + dv * u_ref[rows, :])

    wspec = pl.BlockSpec((512, 128), lambda j: (j, 0))
    aspec = pl.BlockSpec((1, 512), lambda j: (0, j))
    return pcall(
        body, plan, grid=(NT5,),
        in_specs=[pl.BlockSpec((L, 128), lambda j: (0, j)), wspec, wspec, wspec, wspec, aspec, aspec,
                  pl.BlockSpec((1, 128), lambda j: (0, j))],
        out_specs=[pl.BlockSpec((L, 512), lambda j: (0, j)), pl.BlockSpec((L, 512), lambda j: (0, j)),
                   pl.BlockSpec((L, 128), lambda j: (0, j))],
        out_shape=[S((L, NST), f32), S((L, NST), f32), S((L, S5W), f32)],
        sem=("parallel",), name="s5_scan_fwd", args=[proj, wbr, wbi, wcr, wci, abr, abi, drow])


def s5_scan_bwd(dy, proj, xs_re, xs_im, wbr, wbi, wcr, wci, abr, abi, drow, plan=None):
    def body(dy_ref, u_ref, xr_ref, xi_ref, wbr_ref, wbi_ref, wcr_ref, wci_ref, ar_ref, ai_ref, d_ref,
             du_ref, gwbr_ref, gwbi_ref, gwcr_ref, gwci_ref, gar_ref, gai_ref, gd_ref, lr_ref, li_ref):
        wcr_v, wci_v = wcr_ref[...], wci_ref[...]
        gwcr = jnp.zeros((512, 128), f32)
        gwci = jnp.zeros((512, 128), f32)
        gd = jnp.zeros((1, 128), f32)
        for r in range(L // RC):
            rows = pl.ds(r * RC, RC)
            dyv = dy_ref[rows, :]
            lr_ref[rows, :] = dot_nt(dyv, wcr_v)
            li_ref[rows, :] = -dot_nt(dyv, wci_v)
            gwcr += dot_tn(xr_ref[rows, :], dyv)
            gwci -= dot_tn(xi_ref[rows, :], dyv)
            gd += jnp.sum(dyv * u_ref[rows, :], axis=0, keepdims=True)
        gwcr_ref[...] = gwcr
        gwci_ref[...] = gwci
        gd_ref[...] = gd
        pw = _cpowers(ar_ref[...], -ai_ref[...])
        rid = lax.broadcasted_iota(jnp.int32, (8, 512), 0)
        tr_, ti_ = _ctable(pw, rid, lambda r: 8 - r)

        def group(i, c):
            cr, ci, gar, gai = c
            j = L // 8 - 1 - i
            rows = pl.ds(pl.multiple_of(j * 8, 8), 8)
            br, bi = lr_ref[rows, :], li_ref[rows, :]
            for s in (1, 2, 4):
                pr, pi = pw[s - 1]
                sr = jnp.where(rid < 8 - s, pltpu.roll(br, 8 - s, 0), 0.0)
                si = jnp.where(rid < 8 - s, pltpu.roll(bi, 8 - s, 0), 0.0)
                br, bi = br + pr * sr - pi * si, bi + pr * si + pi * sr
            br, bi = br + tr_ * cr - ti_ * ci, bi + tr_ * ci + ti_ * cr
            lr_ref[rows, :] = br
            li_ref[rows, :] = bi
            nr = jnp.where(rid < 7, pltpu.roll(br, 7, 0), cr)
            ni = jnp.where(rid < 7, pltpu.roll(bi, 7, 0), ci)
            xr, xi = xr_ref[rows, :], xi_ref[rows, :]
            return br[0:1], bi[0:1], gar + xr * nr + xi * ni, gai + xr * ni - xi * nr

        z = jnp.zeros((1, 512), f32)
        z8 = jnp.zeros((8, 512), f32)
        _, _, gar, gai = lax.fori_loop(0, L // 8, group, (z, z, z8, z8), unroll=2)
        gar_ref[...] = jnp.sum(gar, axis=0, keepdims=True)
        gai_ref[...] = jnp.sum(gai, axis=0, keepdims=True)
        wbr_v, wbi_v, dv = wbr_ref[...], wbi_ref[...], d_ref[...]
        gwbr = jnp.zeros((512, 128), f32)
        gwbi = jnp.zeros((512, 128), f32)
        for r in range(L // RC):
            rows = pl.ds(r * RC, RC)
            lrv, liv, uv = lr_ref[rows, :], li_ref[rows, :], u_ref[rows, :]
            du_ref[rows, :] = (dot_nn(lrv, wbr_v) + dot_nn(liv, wbi_v) + dv * dy_ref[rows, :]).astype(du_ref.dtype)
            gwbr += dot_tn(lrv, uv)
            gwbi += dot_tn(liv, uv)
        gwbr_ref[...] = gwbr
        gwbi_ref[...] = gwbi

    wspec = pl.BlockSpec((512, 128), lambda j: (j, 0))
    aspec = pl.BlockSpec((1, 512), lambda j: (0, j))
    col = pl.BlockSpec((L, 128), lambda j: (0, j))
    st = pl.BlockSpec((L, 512), lambda j: (0, j))
    dspec = pl.BlockSpec((1, 128), lambda j: (0, j))
    return pcall(
        body, plan, grid=(NT5,),
        in_specs=[col, col, st, st, wspec, wspec, wspec, wspec, aspec, aspec, dspec],
        out_specs=[col, wspec, wspec, wspec, wspec, aspec, aspec, dspec],
        out_shape=[S((L, S5W), BF)] + [S((NST, 128), f32)] * 4 + [S((1, NST), f32)] * 2 + [S((1, S5W), f32)],
        scratch_shapes=[pltpu.VMEM((L, 512), f32), pltpu.VMEM((L, 512), f32)],
        sem=("parallel",), name="s5_scan_bwd", args=[dy, proj, xs_re, xs_im, wbr, wbi, wcr, wci, abr, abi, drow])


def _glu(y, w, b):
    z = jax.nn.gelu(y)
    return z * jax.nn.sigmoid(dot_nn(z, w) + b)


def s5_glu_fwd(y, w, b):
    def body(y_ref, w_ref, b_ref, o_ref):
        o_ref[...] = _glu(y_ref[...], w_ref[...], b_ref[...]).astype(o_ref.dtype)

    return pl.pallas_call(
        body, grid=(L // TR,),
        in_specs=[pl.BlockSpec((TR, S5W), lambda i: (i, 0)), pl.BlockSpec((S5W, S5W), lambda i: (0, 0)),
                  pl.BlockSpec((1, S5W), lambda i: (0, 0))],
        out_specs=pl.BlockSpec((TR, S5W), lambda i: (i, 0)), out_shape=S((L, S5W), BF),
        compiler_params=_cp(("parallel",)), name="s5_glu_fwd")(y, w, b)


def s5_glu_bwd(y, w, b, dmix):
    def body(y_ref, w_ref, b_ref, g_ref, dy_ref, dw_ref, db_ref):
        _, vjp = jax.vjp(_glu, y_ref[...], w_ref[...].astype(f32), b_ref[...])
        dy, dw, db = vjp(g_ref[...])
        dy_ref[...] = dy

        @pl.when(pl.program_id(0) == 0)
        def _():
            dw_ref[...] = jnp.zeros_like(dw_ref)
            db_ref[...] = jnp.zeros_like(db_ref)

        dw_ref[...] += dw
        db_ref[...] += db

    row = pl.BlockSpec((TR, S5W), lambda i: (i, 0))
    return pl.pallas_call(
        body, grid=(L // TR,),
        in_specs=[row, pl.BlockSpec((S5W, S5W), lambda i: (0, 0)), pl.BlockSpec((1, S5W), lambda i: (0, 0)), row],
        out_specs=[row, pl.BlockSpec((S5W, S5W), lambda i: (0, 0)), pl.BlockSpec((1, S5W), lambda i: (0, 0))],
        out_shape=[S((L, S5W), f32), S((S5W, S5W), f32), S((1, S5W), f32)],
        compiler_params=_cp(("arbitrary",)), name="s5_glu_bwd")(y, w, b, dmix)


def _dg3(a, b, ca, cb):
    ah, bh = a.astype(BF), b.astype(BF)
    al, bl = (a - ah.astype(f32)).astype(BF), (b - bh.astype(f32)).astype(BF)
    return _dg(ah, bh, ca, cb) + _dg(ah, bl, ca, cb) + _dg(al, bh, ca, cb)


@jax.custom_vjp
def hi_nn(a, b):
    return _dg3(a, b, 1, 0)


@jax.custom_vjp
def hi_nt(a, b):
    return _dg3(a, b, 1, 1)


@jax.custom_vjp
def hi_tn(a, b):
    return _dg3(a, b, 0, 0)


hi_nn.defvjp(lambda a, b: (hi_nn(a, b), (a, b)), lambda r, g: (hi_nt(g, r[1]), hi_tn(r[0], g)))
hi_nt.defvjp(lambda a, b: (hi_nt(a, b), (a, b)), lambda r, g: (hi_nn(g, r[1]), hi_tn(g, r[0])))
hi_tn.defvjp(lambda a, b: (hi_tn(a, b), (a, b)), lambda r, g: (hi_nt(r[1], g), hi_nn(r[0], g)))


def _hgrn_chunk(St, xq, xf, xi, xg, gam, ng):
    lb = jax.nn.sigmoid(gam[0:1] - gam[1:2])
    q = jax.nn.silu(xq)
    f = lb + (1.0 - lb) * jax.nn.sigmoid(xf)
    k = 1.0 - f
    g = jnp.log(f)
    ti = lax.broadcasted_iota(jnp.int32, (HGC, HGC), 0)
    si = lax.broadcasted_iota(jnp.int32, (HGC, HGC), 1)
    causal = si <= ti
    b = jnp.dot(causal.astype(f32), g, precision=HI, preferred_element_type=f32)
    qe = q * jnp.exp(b)
    o = dot_nt(qe, St)
    parts = []
    for i in range(HGC // HGB):
        r, n, mid = slice(HGB * i, HGB * (i + 1)), HGB * (i + 1), HGB * i + HGB // 2
        base = b[mid:mid + 1]
        sc = hi_nt(q[r] * jnp.exp(b[r] - base), k[:n] * jnp.exp(base - b[:n]))
        parts.append(dot_nn(jnp.where(causal[r, :n], sc, 0.0), xi[:n]))
    o = o + jnp.concatenate(parts, axis=0)
    bl = b[HGC - 1:HGC]
    St_new = St * jnp.exp(bl) + dot_tn(xi, k * jnp.exp(bl - b))
    o = o * lax.rsqrt(jnp.mean(o * o, axis=-1, keepdims=True) + EPS) * ng
    return St_new, o * jax.nn.silu(xg)


NCH = L // HGC


def hgrn_fwd(proj, gamma, hnorm, plan=None):
    def body(q_ref, f_ref, i_ref, g_ref, gam_ref, ng_ref, o_ref, ss_ref, st):
        @pl.when(pl.program_id(0) == 0)
        def _():
            st[...] = jnp.zeros_like(st)

        for h in range(4):
            sl = slice(h * 128, (h + 1) * 128)
            s0 = st[h]
            ss_ref[0, h] = s0
            s1, o = _hgrn_chunk(s0, q_ref[:, sl], f_ref[:, sl], i_ref[:, sl], g_ref[:, sl], gam_ref[:, sl], ng_ref[:, sl])
            st[h] = s1
            o_ref[:, sl] = o.astype(o_ref.dtype)

    def pj(n):
        return pl.BlockSpec((HGC, 512), lambda c: (c, n))

    return pcall(
        body, plan, grid=(NCH,),
        in_specs=[pj(1), pj(2), pj(3), pj(4), pl.BlockSpec((2, 512), lambda c: (0, 0)), pl.BlockSpec((1, 512), lambda c: (0, 0))],
        out_specs=[pl.BlockSpec((HGC, 512), lambda c: (c, 0)), pl.BlockSpec((1, 4, 128, 128), lambda c: (c, 0, 0, 0))],
        out_shape=[S((L, 512), BF), S((NCH, 4, 128, 128), f32)],
        scratch_shapes=[pltpu.VMEM((4, 128, 128), f32)],
        sem=("arbitrary",), name="hgrn_fwd", args=[proj, proj, proj, proj, gamma, hnorm])


def hgrn_bwd(proj, gamma, hnorm, ssave, dmix, du, plan=None):
    def body(q_ref, f_ref, i_ref, g_ref, gam_ref, ng_ref, ss_ref, do_ref, du_ref, dp_ref, dgam_ref, dng_ref, dst):
        @pl.when(pl.program_id(0) == 0)
        def _():
            dst[...] = jnp.zeros_like(dst)
            dgam_ref[...] = jnp.zeros_like(dgam_ref)
            dng_ref[...] = jnp.zeros_like(dng_ref)

        dp_ref[:, 0:512] = du_ref[...]
        for h in range(4):
            sl = slice(h * 128, (h + 1) * 128)
            _, vjp = jax.vjp(_hgrn_chunk, ss_ref[0, h], q_ref[:, sl], f_ref[:, sl], i_ref[:, sl], g_ref[:, sl],
                             gam_ref[:, sl], ng_ref[:, sl])
            ds, dq, df, di, dg, dgam, dng = vjp((dst[h], do_ref[:, sl]))
            dst[h] = ds
            for n, v in enumerate((dq, df, di, dg)):
                dp_ref[:, 512 * (n + 1) + h * 128: 512 * (n + 1) + (h + 1) * 128] = v.astype(dp_ref.dtype)
            dgam_ref[:, sl] += dgam
            dng_ref[:, sl] += dng

    def pj(n):
        return pl.BlockSpec((HGC, 512), lambda i: (NCH - 1 - i, n))

    return pcall(
        body, plan, grid=(NCH,),
        in_specs=[pj(1), pj(2), pj(3), pj(4), pl.BlockSpec((2, 512), lambda i: (0, 0)), pl.BlockSpec((1, 512), lambda i: (0, 0)),
                  pl.BlockSpec((1, 4, 128, 128), lambda i: (NCH - 1 - i, 0, 0, 0)), pj(1), pj(0)],
        out_specs=[pl.BlockSpec((HGC, 2560), lambda i: (NCH - 1 - i, 0)), pl.BlockSpec((2, 512), lambda i: (0, 0)),
                   pl.BlockSpec((1, 512), lambda i: (0, 0))],
        out_shape=[S((L, 2560), BF), S((2, 512), f32), S((1, 512), f32)],
        scratch_shapes=[pltpu.VMEM((4, 128, 128), f32)],
        sem=("arbitrary",), name="hgrn_bwd", args=[proj, proj, proj, proj, gamma, hnorm, ssave, dmix, du])


def _earlier(h_ref, k, r0, n):
    if r0 > 0:
        return h_ref[pl.ds(r0 - k, n), :]
    rid = lax.broadcasted_iota(jnp.int32, (8, h_ref.shape[1]), 0)
    head = jnp.where(rid >= k, pltpu.roll(h_ref[pl.ds(0, 8), :], k, 0), 0.0)
    return jnp.concatenate([head, h_ref[pl.ds(8 - k, n - 8), :]], axis=0)


def _conv3_rows(h_ref, w, b, r0, n=None):
    n = CR if n is None else n
    h1, h2 = _earlier(h_ref, 1, r0, n), _earlier(h_ref, 2, r0, n)
    return w[2:3] * h_ref[pl.ds(r0, n), :] + w[1:2] * h1 + w[0:1] * h2 + b, h1, h2


CT = 128
NCT = DFF // CT
CR = 64


def convact_fwd(hu, cw, cb, layer, plan=None):
    def body(ha_ref, hb_ref, wa_ref, wb_ref, ba_ref, bb_ref, o_ref):
        ca = _conv3_rows(ha_ref, wa_ref[...], ba_ref[...], 0, L)[0]
        cb_ = _conv3_rows(hb_ref, wb_ref[...], bb_ref[...], 0, L)[0]
        o_ref[...] = (jax.nn.silu(ca) * cb_).astype(o_ref.dtype)

    def h(off):
        return pl.BlockSpec((L, CT), lambda j: (0, j + off))

    def w(off):
        return pl.BlockSpec((3, CT), lambda j: (0, j + off))

    def b(off):
        return pl.BlockSpec((None, 1, CT), lambda j: (layer, 0, j + off))

    return pcall(body, plan, grid=(NCT,), in_specs=[h(0), h(NCT), w(0), w(NCT), b(0), b(NCT)],
                 out_specs=pl.BlockSpec((L, CT), lambda j: (0, j)), out_shape=S((L, DFF), BF),
                 sem=("parallel",), name=f"convact_fwd{layer}", args=[hu, hu, cw, cw, cb, cb])


def convact_bwd(hu, cw, cb, dact, layer, plan=None):
    def body(ha_ref, hb_ref, wa_ref, wb_ref, ba_ref, bb_ref, g_ref, dh_ref, dw_ref, db_ref, sh, sw, sb, da_scr, db_scr):
        j = pl.program_id(0)

        def fold(x):
            return functools.reduce(jnp.add, [x[8 * m:8 * m + 8] for m in range(CR // 8)])

        @pl.when(j < NCT)
        def _():
            wa, wb, ba, bb = wa_ref[...], wb_ref[...], ba_ref[...], bb_ref[...]
            da_scr[pl.ds(L, 8), :] = jnp.zeros((8, CT), f32)
            db_scr[pl.ds(L, 8), :] = jnp.zeros((8, CT), f32)
            acc = [jnp.zeros((8, CT), f32) for _ in range(8)]
            for c in range(L // CR):
                r0 = c * CR
                ca, a1, a2 = _conv3_rows(ha_ref, wa, ba, r0)
                cb_, b1, b2 = _conv3_rows(hb_ref, wb, bb, r0)
                g = g_ref[pl.ds(r0, CR), :].astype(f32)
                sg = jax.nn.sigmoid(ca)
                dca = g * cb_ * (sg * (1.0 + ca * (1.0 - sg)))
                dcb = g * (ca * sg)
                da_scr[pl.ds(r0, CR), :] = dca
                db_scr[pl.ds(r0, CR), :] = dcb
                terms = (dca * a2, dca * a1, dca * ha_ref[pl.ds(r0, CR), :], dca,
                         dcb * b2, dcb * b1, dcb * hb_ref[pl.ds(r0, CR), :], dcb)
                acc = [a + fold(t) for a, t in zip(acc, terms)]
            rows = [jnp.sum(a, axis=0, keepdims=True) for a in acc]
            for k in range(3):
                dw_ref[k:k + 1, :] = rows[k]
                sw[j, k:k + 1, :] = rows[4 + k]
            db_ref[...] = rows[3]
            sb[j] = rows[7]
            for c in range(L // CR):
                r0 = c * CR
                for scr, w, out in ((da_scr, wa, dh_ref), (db_scr, wb, sh.at[j])):
                    dh = (w[2:3] * scr[pl.ds(r0, CR), :] + w[1:2] * scr[pl.ds(r0 + 1, CR), :]
                          + w[0:1] * scr[pl.ds(r0 + 2, CR), :])
                    out[pl.ds(r0, CR), :] = dh.astype(out.dtype)

        @pl.when(j >= NCT)
        def _():
            dh_ref[...] = sh[j - NCT]
            dw_ref[...] = sw[j - NCT]
            db_ref[...] = sb[j - NCT]

    def lo(j):
        return jnp.minimum(j, NCT - 1)

    in_specs = [pl.BlockSpec((L, CT), lambda j: (0, lo(j))), pl.BlockSpec((L, CT), lambda j: (0, lo(j) + NCT)),
                pl.BlockSpec((3, CT), lambda j: (0, lo(j))), pl.BlockSpec((3, CT), lambda j: (0, lo(j) + NCT)),
                pl.BlockSpec((None, 1, CT), lambda j: (layer, 0, lo(j))), pl.BlockSpec((None, 1, CT), lambda j: (layer, 0, lo(j) + NCT)),
                pl.BlockSpec((L, CT), lambda j: (0, lo(j)))]
    return pcall(
        body, plan, grid=(2 * NCT,), in_specs=in_specs,
        out_specs=[pl.BlockSpec((L, CT), lambda j: (0, j)), pl.BlockSpec((3, CT), lambda j: (0, j)), pl.BlockSpec((1, CT), lambda j: (0, j))],
        out_shape=[S((L, 2 * DFF), BF), S((3, 2 * DFF), f32), S((1, 2 * DFF), f32)],
        scratch_shapes=[pltpu.VMEM((NCT, L, CT), BF), pltpu.VMEM((NCT, 3, CT), f32), pltpu.VMEM((NCT, 1, CT), f32),
                        pltpu.VMEM((L + 8, CT), f32), pltpu.VMEM((L + 8, CT), f32)],
        sem=("arbitrary",), name=f"convact_bwd{layer}", args=[hu, hu, cw, cw, cb, cb, dact])


DILS = (1, 4, 16)
AB = 128


def _rope_tables(pos_ref, invf_ref):
    ang = pos_ref[...].astype(f32) * invf_ref[...]
    lane = lax.broadcasted_iota(jnp.int32, (1, 128), 1) % 64
    cosf = jnp.where(lane < 16, jnp.cos(ang), 1.0)
    sn = jnp.sin(ang)
    s_lo = jnp.where(lane < 8, -sn, 0.0)
    s_hi = jnp.where((lane >= 8) & (lane < 16), sn, 0.0)
    return cosf, s_lo, s_hi


def _rope(t, cosf, s_lo, s_hi):
    return t * cosf + pltpu.roll(t, 120, 1) * s_lo + pltpu.roll(t, 8, 1) * s_hi


def _rope_t(g, cosf, s_lo, s_hi):
    return g * cosf + pltpu.roll(g * s_lo, 8, 1) + pltpu.roll(g * s_hi, 120, 1)


def _att_block(q2, kp, kc, vp, vc, first):
    lane = lax.broadcasted_iota(jnp.int32, (1, 128), 1)
    qi = lax.broadcasted_iota(jnp.int32, (AB, 2 * AB), 0) + AB
    kj = lax.broadcasted_iota(jnp.int32, (AB, 2 * AB), 1)
    back = qi - kj
    valid = (back >= 0) & (back <= AB)
    if first:
        valid = valid & (kj >= AB)
    kk = jnp.concatenate([kp, kc], axis=0)
    vv = jnp.concatenate([vp, vc], axis=0)
    o2 = jnp.zeros((AB, 128), f32)
    lse2 = jnp.zeros((AB, 128), f32)
    for e in range(2):
        hm = ((lane >= 64 * e) & (lane < 64 * (e + 1))).astype(f32)
        s = dot_nt(q2 * (hm * 0.125), kk)
        s = jnp.where(valid, s, -jnp.inf)
        m = jnp.max(s, axis=-1, keepdims=True)
        p = jnp.exp(s - m)
        den = jnp.sum(p, axis=-1, keepdims=True)
        o2 = o2 + dot_nn(p, vv * hm) / den
        lse2 = lse2 + (m + jnp.log(den)) * hm
    return o2, lse2


def _att_blocks(dil):
    m = L // dil
    return [(r * m + n * AB, n == 0) for r in range(dil) for n in range(m // AB)]


def deinterleave(x, dil):
    return x if dil == 1 else x.reshape(L // dil, dil, x.shape[1]).swapaxes(0, 1).reshape(L, x.shape[1])


def attn_fwd(qkv, pos, invf, g, plan=None):
    blocks = _att_blocks(DILS[g])

    def body(q_ref, k_ref, v_ref, pos_ref, invf_ref, o_ref, l_ref, qr, kr):
        cosf, s_lo, s_hi = _rope_tables(pos_ref, invf_ref)
        qr[...] = _rope(q_ref[...], cosf, s_lo, s_hi)
        kr[...] = _rope(k_ref[...], cosf, s_lo, s_hi)
        for off, first in blocks:
            cur, prv = pl.ds(off, AB), pl.ds(off if first else off - AB, AB)
            o2, lse2 = _att_block(qr[cur, :], kr[prv, :], kr[cur, :], v_ref[prv, :], v_ref[cur, :], first)
            o_ref[cur, :] = o2
            l_ref[cur, :] = lse2

    def sec(n):
        return pl.BlockSpec((L, 128), lambda p: (0, p + 4 * n))

    return pcall(
        body, plan, grid=(4,),
        in_specs=[sec(0), sec(1), sec(2), pl.BlockSpec((L, 1), lambda p: (0, 0)), pl.BlockSpec((1, 128), lambda p: (0, 0))],
        out_specs=[sec(0), sec(0)], out_shape=[S((L, 512), f32), S((L, 512), f32)],
        scratch_shapes=[pltpu.VMEM((L, 128), f32), pltpu.VMEM((L, 128), f32)],
        sem=("parallel",), name=f"attn_fwd{g}", args=[qkv, qkv, qkv, pos, invf])


def _att_block_bwd(q2, kp, kc, vp, vc, lse2, do2, dl2, first):
    lane = lax.broadcasted_iota(jnp.int32, (1, 128), 1)
    qi = lax.broadcasted_iota(jnp.int32, (AB, 2 * AB), 0) + AB
    kj = lax.broadcasted_iota(jnp.int32, (AB, 2 * AB), 1)
    back = qi - kj
    valid = (back >= 0) & (back <= AB)
    if first:
        valid = valid & (kj >= AB)
    kk = jnp.concatenate([kp, kc], axis=0)
    vv = jnp.concatenate([vp, vc], axis=0)
    dq2 = jnp.zeros((AB, 128), f32)
    dkk = jnp.zeros((2 * AB, 128), f32)
    dvv = jnp.zeros((2 * AB, 128), f32)
    for e in range(2):
        hb = (lane >= 64 * e) & (lane < 64 * (e + 1))
        hm = hb.astype(f32)
        qs = q2 * (hm * 0.125)
        lse = jnp.max(jnp.where(hb, lse2, -jnp.inf), axis=-1, keepdims=True)
        dls = jnp.sum(dl2 * hm, axis=-1, keepdims=True)
        p = jnp.where(valid, jnp.exp(dot_nt(qs, kk) - lse), 0.0)
        dov = do2 * hm
        dp = dot_nt(dov, vv)
        ds = p * (dp - jnp.sum(p * dp, axis=-1, keepdims=True) + dls)
        dq2 = dq2 + dot_nn(ds, kk) * (hm * 0.125)
        dkk = dkk + dot_tn(ds, qs)
        dvv = dvv + dot_tn(p, dov)
    return dq2, dkk[:AB], dkk[AB:], dvv[:AB], dvv[AB:]


def attn_bwd(qkv, pos, invf, lse, do, dl, g, plan=None):
    blocks = _att_blocks(DILS[g])

    def body(q_ref, k_ref, v_ref, pos_ref, invf_ref, l_ref, do_ref, dl_ref, d_ref, qr, kr, dqr, dkr, dvr):
        cosf, s_lo, s_hi = _rope_tables(pos_ref, invf_ref)
        qr[...] = _rope(q_ref[...], cosf, s_lo, s_hi)
        kr[...] = _rope(k_ref[...], cosf, s_lo, s_hi)
        for off, first in blocks:
            cur, prv = pl.ds(off, AB), pl.ds(off if first else off - AB, AB)
            dq2, dkp, dkc, dvp, dvc = _att_block_bwd(qr[cur, :], kr[prv, :], kr[cur, :], v_ref[prv, :], v_ref[cur, :],
                                                     l_ref[cur, :], do_ref[cur, :], dl_ref[cur, :], first)
            dqr[cur, :] = dq2
            dkr[cur, :] = dkc
            dvr[cur, :] = dvc
            if not first:
                dkr[prv, :] += dkp
                dvr[prv, :] += dvp
        d_ref[0] = _rope_t(dqr[...], cosf, s_lo, s_hi).astype(d_ref.dtype)
        d_ref[1] = _rope_t(dkr[...], cosf, s_lo, s_hi).astype(d_ref.dtype)
        d_ref[2] = dvr[...].astype(d_ref.dtype)

    def sec(n):
        return pl.BlockSpec((L, 128), lambda p: (0, p + 4 * n))

    return pcall(
        body, plan, grid=(4,),
        in_specs=[sec(0), sec(1), sec(2), pl.BlockSpec((L, 1), lambda p: (0, 0)), pl.BlockSpec((1, 128), lambda p: (0, 0)),
                  sec(0), sec(0), sec(0)],
        out_specs=pl.BlockSpec((3, L, 128), lambda p: (0, 0, p)), out_shape=S((3, L, 512), BF),
        scratch_shapes=[pltpu.VMEM((L, 128), f32)] * 5,
        sem=("parallel",), name=f"attn_bwd{g}", args=[qkv, qkv, qkv, pos, invf, lse, do, dl])


def _merge(o0, o1, o2, l0, l1, l2):
    m = jnp.maximum(jnp.maximum(l0, l1), l2)
    e0, e1, e2 = jnp.exp(l0 - m), jnp.exp(l1 - m), jnp.exp(l2 - m)
    return (e0 * o0 + e1 * o1 + e2 * o2) / (e0 + e1 + e2)


def _to_token_major(src_ref, scr, i, dil, slab):
    n = TR // dil
    for r in range(dil):
        rows = pl.ds(pl.multiple_of(r * (L // dil) + i * n, n), n)
        scr[pl.ds(r, n, stride=dil), :] = src_ref[rows, slab * 128:(slab + 1) * 128].astype(f32)
    return scr[...]


def _to_class_major(val, dst_ref, scr, i, dil, slab):
    n = TR // dil
    scr[...] = val
    for r in range(dil):
        rows = pl.ds(pl.multiple_of(r * (L // dil) + i * n, n), n)
        dst_ref[rows, slab * 128:(slab + 1) * 128] = scr[pl.ds(r, n, stride=dil), :].astype(dst_ref.dtype)


def rms_fwd_classes(x, g, name):
    def body(x_ref, g_ref, o_ref, o1_ref, o2_ref, scr):
        i = pl.program_id(0)
        y = _rms(x_ref[...], g_ref[...])
        o_ref[...] = y.astype(o_ref.dtype)
        for s in range(D // 128):
            ys = y[:, s * 128:(s + 1) * 128]
            _to_class_major(ys, o1_ref, scr, i, DILS[1], s)
            _to_class_major(ys, o2_ref, scr, i, DILS[2], s)

    row = pl.BlockSpec((TR, D), lambda i: (i, 0))
    full = pl.BlockSpec((L, D), lambda i: (0, 0))
    return pl.pallas_call(
        body, grid=(L // TR,), in_specs=[row, pl.BlockSpec((1, D), lambda i: (0, 0))], out_specs=[row, full, full],
        out_shape=[S((L, D), BF)] * 3, scratch_shapes=[pltpu.VMEM((TR, 128), f32)],
        compiler_params=_cp(("arbitrary",)), name=name)(x, g)


def rms_bwd_classes(x, g, dy0, dyc, dres, name, plan=None):
    def body(x_ref, g_ref, dy0_ref, d1_ref, d2_ref, dr_ref, dh_ref, dg_ref, scr, dyf):
        i = pl.program_id(0)
        for s in range(D // 128):
            sl = slice(s * 128, (s + 1) * 128)
            dyf[:, sl] = (dy0_ref[:, sl] + _to_token_major(d1_ref, scr.at[0], i, DILS[1], s)
                          + _to_token_major(d2_ref, scr.at[1], i, DILS[2], s))
        _, vjp = jax.vjp(_rms, x_ref[...], g_ref[...])
        dx, dg = vjp(dyf[...])
        dh_ref[...] = dr_ref[...] + dx

        @pl.when(i == 0)
        def _():
            dg_ref[...] = jnp.zeros_like(dg_ref)

        dg_ref[...] += dg

    row = pl.BlockSpec((TR, D), lambda i: (i, 0))
    vec = pl.BlockSpec((1, D), lambda i: (0, 0))
    full = pl.BlockSpec((L, D), lambda i: (0, 0))
    return pcall(body, plan, grid=(L // TR,), in_specs=[row, vec, row, full, full, row], out_specs=[row, vec],
                 out_shape=[S((L, D), f32), S((1, D), f32)],
                 scratch_shapes=[pltpu.VMEM((2, TR, 128), f32), pltpu.VMEM((TR, D), f32)],
                 sem=("arbitrary",), name=name, args=[x, g, dy0, dyc[0], dyc[1], dres])


def attn_merge_fwd(o0, l0, oc, lc, plan=None):
    def body(o0_ref, l0_ref, o1_ref, l1_ref, o2_ref, l2_ref, o_ref, scr):
        i = pl.program_id(0)
        for s in range(4):
            sl = slice(s * 128, (s + 1) * 128)
            o1 = _to_token_major(o1_ref, scr.at[0], i, DILS[1], s)
            l1 = _to_token_major(l1_ref, scr.at[1], i, DILS[1], s)
            o2 = _to_token_major(o2_ref, scr.at[2], i, DILS[2], s)
            l2 = _to_token_major(l2_ref, scr.at[3], i, DILS[2], s)
            o_ref[:, sl] = _merge(o0_ref[:, sl], o1, o2, l0_ref[:, sl], l1, l2).astype(o_ref.dtype)

    blk = pl.BlockSpec((TR, 512), lambda i: (i, 0))
    full = pl.BlockSpec((L, 512), lambda i: (0, 0))
    return pcall(body, plan, grid=(L // TR,), in_specs=[blk, blk, full, full, full, full], out_specs=blk,
                 out_shape=S((L, 512), BF), scratch_shapes=[pltpu.VMEM((4, TR, 128), f32)],
                 sem=("arbitrary",), name="attn_merge_fwd", args=[o0, l0, oc[0], lc[0], oc[1], lc[1]])


def attn_merge_bwd(o0, l0, oc, lc, do, plan=None):
    def body(o0_ref, l0_ref, o1_ref, l1_ref, o2_ref, l2_ref, g_ref, do0, dl0, do1, dl1, do2, dl2, scr):
        i = pl.program_id(0)
        for s in range(4):
            sl = slice(s * 128, (s + 1) * 128)
            o1 = _to_token_major(o1_ref, scr.at[0], i, DILS[1], s)
            l1 = _to_token_major(l1_ref, scr.at[1], i, DILS[1], s)
            o2 = _to_token_major(o2_ref, scr.at[2], i, DILS[2], s)
            l2 = _to_token_major(l2_ref, scr.at[3], i, DILS[2], s)
            _, vjp = jax.vjp(_merge, o0_ref[:, sl], o1, o2, l0_ref[:, sl], l1, l2)
            g0, g1, g2, h0, h1, h2 = vjp(g_ref[:, sl].astype(f32))
            do0[:, sl] = g0.astype(do0.dtype)
            dl0[:, sl] = h0
            _to_class_major(g1, do1, scr.at[0], i, DILS[1], s)
            _to_class_major(h1, dl1, scr.at[1], i, DILS[1], s)
            _to_class_major(g2, do2, scr.at[2], i, DILS[2], s)
            _to_class_major(h2, dl2, scr.at[3], i, DILS[2], s)

    blk = pl.BlockSpec((TR, 512), lambda i: (i, 0))
    full = pl.BlockSpec((L, 512), lambda i: (0, 0))
    outs = pcall(body, plan, grid=(L // TR,), in_specs=[blk, blk, full, full, full, full, blk],
                 out_specs=[blk, blk, full, full, full, full],
                 out_shape=[S((L, 512), BF), S((L, 512), f32)] * 3, scratch_shapes=[pltpu.VMEM((4, TR, 128), f32)],
                 sem=("arbitrary",), name="attn_merge_bwd", args=[o0, l0, oc[0], lc[0], oc[1], lc[1], do])
    return [outs[0], outs[2], outs[4]], [outs[1], outs[3], outs[5]]


def _invf_lanes():
    half = 8
    inv = ROPE_THETA ** (-np.arange(half, dtype=np.float32) * 2.0 / 16.0)
    lane = np.arange(128) % 64
    return jnp.asarray(np.where(lane < 16, inv[lane % 8], 0.0).astype(np.float32)[None, :])


def hosted(C, host, fn):
    p = C.plan(host) if C is not None else None
    out = fn(p)
    if p is not None:
        C.done(p)
    return out


def _ffn_fwd(h, g_row, W, cb, layer, C):
    hn = rms_fwd(h, g_row, f"rms_ffn{layer}")
    hu = hosted(C, f"ffn_in{layer}", lambda p: matmul(hn, W[("ffn_w_in", layer)], mode="nn", tm=1024, tn=1408, tk=1024,
                                                      plan=p, name=f"ffn_in{layer}"))
    act = hosted(C, f"convact_fwd{layer}", lambda p: convact_fwd(hu, W[("ffn_conv_w", layer)], cb, layer, plan=p))
    h2 = hosted(C, f"ffn_out{layer}", lambda p: matmul(act, W[("ffn_w_out", layer)], mode="nn", tm=1024, tn=1024, tk=2816,
                                                       add=h, plan=p, name=f"ffn_out{layer}"))
    return h2, (hn, hu, act)


def _ffn_bwd(dh, h, g_row, W, cb, saved, layer, C, G):
    hn, hu, act = saved
    w_in, w_out = W[("ffn_w_in", layer)], W[("ffn_w_out", layer)]
    dact = hosted(C, f"ffn_out_dx{layer}", lambda p: matmul(dh, w_out, mode="nt", tm=1024, tn=1408, tk=1024, out_dtype=BF,
                                                          plan=p, name=f"ffn_out_dx{layer}"))
    G[("ffn_w_out", layer)] = hosted(C, f"ffn_out_dw{layer}", lambda p: matmul(
        act, dh, mode="tn", tm=1408, tn=1024, tk=L, out_dtype=BF, plan=p, name=f"ffn_out_dw{layer}"))
    dhu, G[("ffn_conv_w", layer)], g_cb = hosted(
        C, f"convact_bwd{layer}", lambda p: convact_bwd(hu, W[("ffn_conv_w", layer)], cb, dact, layer, plan=p))
    dhn = hosted(C, f"ffn_in_dx{layer}", lambda p: matmul(dhu, w_in, mode="nt", tm=1024, tn=1024, tk=2816, plan=p,
                                                         name=f"ffn_in_dx{layer}"))
    G[("ffn_w_in", layer)] = hosted(C, f"ffn_in_dw{layer}", lambda p: matmul(
        hn, dhu, mode="tn", tm=1024, tn=1408, tk=L, out_dtype=BF, plan=p, name=f"ffn_in_dw{layer}"))
    dh2, g_norm = hosted(C, f"rms_ffn_bwd{layer}", lambda p: rms_bwd(h, g_row, [dhn], dh, f"rms_ffn_bwd{layer}", plan=p))
    return dh2, g_cb, g_norm


def local_step(x, pos, tgt, sm, W, C=None):
    G = C.grads if C is not None else {}
    nm, nf = sm["norm_mix"], sm["norm_ffn"]
    invf = _invf_lanes()
    are = sm["s5_A_re"].reshape(NST, 1)
    aim = sm["s5_A_im"].reshape(NST, 1)
    ldt = sm["s5_log_dt"].reshape(1, 32)
    bre = sm["s5_B_re"].reshape(NST, 16)
    bim = sm["s5_B_im"].reshape(NST, 16)
    cre = jnp.swapaxes(sm["s5_C_re"][0], 1, 2).reshape(NST, 16)
    cim = jnp.swapaxes(sm["s5_C_im"][0], 1, 2).reshape(NST, 16)
    drow = sm["s5_D"].reshape(1, S5W)
    wbr, wbi, wcr, wci, abr, abi = s5_params_fwd(are, aim, ldt, bre, bim, cre, cim)
    hn0 = rms_fwd(x, nm[0:1], "rms_mix0")
    cb3 = sm["ffn_conv_b3"]
    proj = hosted(C, "mix_in", lambda p: matmul(hn0, W[("mix_w_in", 0)], mode="nn", tm=1024, tn=1280, tk=1024, plan=p, name="mix_in"))
    xs_re, xs_im, y5 = hosted(C, "s5_scan_fwd", lambda p: s5_scan_fwd(proj, wbr, wbi, wcr, wci, abr, abi, drow, plan=p))
    oa = s5_glu_fwd(y5, W[("s5_glu_w", 0)], sm["s5_glu_b"])
    ob, ssave = hosted(C, "hgrn_fwd", lambda p: hgrn_fwd(proj, sm["hgrn_gamma"], sm["hgrn_norm"], plan=p))
    cat = jnp.concatenate([oa, ob], axis=1)
    h1 = matmul(cat, W[("mix_w_out", 0)], mode="nn", tm=1024, tn=1024, tk=1024, add=x, name="mix_out")
    h2, ffn0 = _ffn_fwd(h1, nf[0:1], W, cb3, 0, C)
    hn2_g = rms_fwd_classes(h2, nm[1:2], "rms_mix1")
    wqkv = W[("att_w_qkv", 0)]
    pos_g, qkv_g, oc_g, lc_g = [], [], [], []
    for g, dil in enumerate(DILS):
        pos_g.append(deinterleave(pos, dil))
        qkv_g.append(hosted(C, f"att_qkv{g}", lambda p: matmul(
            hn2_g[g], wqkv, mode="nn", tm=1024, tn=512, tk=1024, dims=(L, 1536, D),
            b_spec=pl.BlockSpec((D, 512), lambda i, j, k, g=g: (0, 3 * j + g)), plan=p, name=f"att_qkv{g}")))
        o_c, l_c = hosted(C, f"attn_fwd{g}", lambda p: attn_fwd(qkv_g[g], pos_g[g], invf, g, plan=p))
        oc_g.append(o_c)
        lc_g.append(l_c)
    o = hosted(C, "attn_merge_fwd", lambda p: attn_merge_fwd(oc_g[0], lc_g[0], oc_g[1:], lc_g[1:], plan=p))
    h3 = matmul(o, W[("att_w_o", 0)], mode="nn", tm=1024, tn=1024, tk=512, add=h2, name="att_o")
    h4, ffn1 = _ffn_fwd(h3, nf[1:2], W, cb3, 1, C)
    loss, dh, g_nfinal = loss_head(h4, sm["norm_final"].reshape(1, D), tgt)
    dh, g_cb1, g_nf1 = _ffn_bwd(dh, h3, nf[1:2], W, cb3, ffn1, 1, C, G)
    do = matmul(dh, W[("att_w_o", 0)], mode="nt", tm=1024, tn=512, tk=1024, name="att_o_dx")
    G[("att_w_o", 0)] = matmul(o, dh, mode="tn", tm=512, tn=1024, tk=L, out_dtype=BF, name="att_o_dw")
    do_g, dl_g = hosted(C, "attn_merge_bwd", lambda p: attn_merge_bwd(oc_g[0], lc_g[0], oc_g[1:], lc_g[1:], do, plan=p))
    dhn2_g, gq = [], []
    for g, dil in enumerate(DILS):
        d3 = hosted(C, f"attn_bwd{g}", lambda p: attn_bwd(qkv_g[g], pos_g[g], invf, lc_g[g], do_g[g], dl_g[g], g, plan=p))
        dx = matmul(d3, wqkv, mode="nt", tm=1024, tn=1024, tk=512, dims=(L, D, 1536),
                    a_spec=pl.BlockSpec((None, 1024, 512), lambda i, j, k: (k, i, 0)),
                    b_spec=pl.BlockSpec((D, 512), lambda i, j, k, g=g: (0, 3 * k + g)), name=f"att_qkv_dx{g}")
        dhn2_g.append(dx)
        gq.append(matmul(hn2_g[g], d3, mode="tn", tm=1024, tn=512, tk=L, out_dtype=BF, dims=(D, 1536, L),
                         b_spec=pl.BlockSpec((None, L, 512), lambda i, j, k: (j, k, 0)), name=f"att_qkv_dw{g}"))
    G[("att_w_qkv", 0)] = jnp.concatenate([gq[g][:, 512 * s:512 * (s + 1)] for s in range(3) for g in range(3)], axis=1)
    dh, g_nm1 = hosted(C, "rms_mix_bwd1", lambda p: rms_bwd_classes(h2, nm[1:2], dhn2_g[0], dhn2_g[1:], dh, "rms_mix_bwd1", plan=p))
    dh, g_cb0, g_nf0 = _ffn_bwd(dh, h1, nf[0:1], W, cb3, ffn0, 0, C, G)
    dmix = matmul(dh, W[("mix_w_out", 0)], mode="nt", tm=1024, tn=1024, tk=1024, name="mix_out_dx")
    G[("mix_w_out", 0)] = matmul(cat, dh, mode="tn", tm=1024, tn=1024, tk=L, out_dtype=BF, name="mix_out_dw")
    dy5, g_glu_w, g_glu_b = s5_glu_bwd(y5, W[("s5_glu_w", 0)], sm["s5_glu_b"], dmix)
    G[("s5_glu_w", 0)] = g_glu_w.astype(BF)
    du, gwbr, gwbi, gwcr, gwci, gabr, gabi, g_d = hosted(C, "s5_scan_bwd", lambda p: s5_scan_bwd(
        dy5, proj, xs_re, xs_im, wbr, wbi, wcr, wci, abr, abi, drow, plan=p))
    g_are, g_aim, g_ldt, g_bre, g_bim, g_cre, g_cim = s5_params_bwd(are, aim, ldt, bre, bim, cre, cim,
                                                                   (gwbr, gwbi, gwcr, gwci, gabr, gabi))
    small = {
        "norm_ffn": jnp.concatenate([g_nf0, g_nf1], axis=0), "norm_final": g_nfinal.reshape(D),
        "s5_A_re": g_are.reshape(1, 32, 64), "s5_A_im": g_aim.reshape(1, 32, 64), "s5_log_dt": g_ldt.reshape(1, 32),
        "s5_B_re": g_bre.reshape(1, 32, 64, 16), "s5_B_im": g_bim.reshape(1, 32, 64, 16),
        "s5_C_re": jnp.swapaxes(g_cre.reshape(1, 32, 64, 16), 2, 3), "s5_C_im": jnp.swapaxes(g_cim.reshape(1, 32, 64, 16), 2, 3),
        "s5_D": g_d.reshape(1, 32, 16), "s5_glu_b": g_glu_b, "ffn_conv_b": jnp.concatenate([g_cb0, g_cb1], axis=0),
    }
    if C is not None:
        C.small["small_early"] = _pack(small, SMALL_EARLY)
    dproj, g_gamma, g_hnorm = hosted(C, "hgrn_bwd", lambda p: hgrn_bwd(proj, sm["hgrn_gamma"], sm["hgrn_norm"], ssave, dmix, du,
                                                                       plan=p))
    dhn0 = hosted(C, "mix_in_dx", lambda p: matmul(dproj, W[("mix_w_in", 0)], mode="nt", tm=1024, tn=1024, tk=2560, plan=p,
                                                  name="mix_in_dx"))
    G[("mix_w_in", 0)] = matmul(hn0, dproj, mode="tn", tm=1024, tn=1280, tk=L, out_dtype=BF, name="mix_in_dw")
    gx, g_nm0 = hosted(C, "rms_mix_bwd0", lambda p: rms_bwd(x, nm[0:1], [dhn0], dh, "rms_mix_bwd0", plan=p))
    small.update({"norm_mix": jnp.concatenate([g_nm0, g_nm1], axis=0), "hgrn_gamma": g_gamma, "hgrn_norm": g_hnorm})
    if C is not None:
        C.small["small_late"] = _pack(small, SMALL_LATE)
    return loss, gx, G, small


BIG = ("mix_w_in", "mix_w_out", "s5_glu_w", "att_w_qkv", "att_w_o", "ffn_w_in", "ffn_w_out", "ffn_conv_w")
SMALL = ("norm_mix", "norm_ffn", "norm_final", "s5_A_re", "s5_A_im", "s5_log_dt", "s5_B_re", "s5_B_im", "s5_C_re", "s5_C_im",
         "s5_D", "s5_glu_b", "hgrn_gamma", "hgrn_norm", "ffn_conv_b")
SMALL_LATE = ("norm_mix", "hgrn_gamma", "hgrn_norm")
SMALL_EARLY = tuple(n for n in SMALL if n not in SMALL_LATE)


def cast_bf16(w, name, plan=None):
    nl, r, c = w.shape
    w2 = w.reshape(nl * r, c)
    tr = 256 if (nl * r) % 256 == 0 else nl * r

    def body(w_ref, o_ref):
        o_ref[...] = w_ref[...].astype(BF)

    out = pcall(body, plan, grid=(nl * r // tr,), in_specs=[pl.BlockSpec((tr, c), lambda i: (i, 0))],
                out_specs=pl.BlockSpec((tr, c), lambda i: (i, 0)), out_shape=S((nl * r, c), BF),
                sem=("parallel",), name=name, args=[w2])
    return out.reshape(nl, r, c)


DIRECT = ("ffn_conv_w", "att_w_o", "s5_glu_w", "mix_w_out")

SCHEDULE = {
    "cast_ffn_w_in": [("G", "mix_w_in", 0)],
    "mix_in": [("G", "mix_w_out", 0), ("G", "s5_glu_w", 0)],
    "s5_scan_fwd": [("G", "ffn_w_in", 0, (0, 2))],
    "hgrn_fwd": [("G", "ffn_w_in", 0, (1, 2)), ("G", "ffn_conv_w", 0), ("G", "ffn_conv_w", 1), ("G", "att_w_qkv", 0, (0, 2))],
    "ffn_in0": [("G", "ffn_w_out", 0)],
    "convact_fwd0": [("G", "att_w_qkv", 0, (1, 2))],
    "att_qkv0": [("G", "att_w_o", 0)],
    "attn_fwd0": [("G", "ffn_w_in", 1, (0, 2))],
    "attn_fwd1": [("G", "ffn_w_in", 1, (1, 2))],
    "attn_fwd2": [("G", "ffn_w_out", 1)],
    "convact_bwd1": [("P", "ffn_w_out", 1)],
    "ffn_in_dx1": [("A", "ffn_w_out", 1, (0, 2))],
    "ffn_in_dw1": [("A", "ffn_w_out", 1, (1, 2))],
    "rms_ffn_bwd1": [("P", "ffn_w_in", 1)],
    "attn_merge_bwd": [("A", "ffn_conv_w", 1), ("B", "ffn_w_out", 1)],
    "attn_bwd0": [("A", "ffn_w_in", 1, (0, 2)), ("A", "att_w_o", 0)],
    "attn_bwd1": [("A", "ffn_w_in", 1, (1, 2)), ("B", "att_w_o", 0), ("B", "ffn_conv_w", 1)],
    "attn_bwd2": [("B", "ffn_w_in", 1)],
    "rms_mix_bwd1": [("P", "att_w_qkv", 0)],
    "ffn_out_dx0": [("A", "att_w_qkv", 0, (0, 4))],
    "ffn_out_dw0": [("A", "att_w_qkv", 0, (1, 4))],
    "convact_bwd0": [("A", "att_w_qkv", 0, (2, 4)), ("A", "att_w_qkv", 0, (3, 4)), ("P", "ffn_w_out", 0)],
    "ffn_in_dx0": [("A", "ffn_w_out", 0, (0, 2)), ("B", "att_w_qkv", 0)],
    "ffn_in_dw0": [("A", "ffn_w_out", 0, (1, 2))],
    "rms_ffn_bwd0": [("P", "ffn_w_in", 0), ("B", "ffn_w_out", 0)],
    "s5_scan_bwd": [("A", "ffn_w_in", 0, (0, 2)), ("A", "ffn_conv_w", 0)],
    "hgrn_bwd": [("A", "ffn_w_in", 0, (1, 2)), ("A", "mix_w_out", 0), ("A", "s5_glu_w", 0), ("B", "ffn_conv_w", 0),
                 ("A", "small_early", 0)],
    "mix_in_dx": [("B", "ffn_w_in", 0), ("B", "mix_w_out", 0), ("B", "s5_glu_w", 0), ("B", "small_early", 0)],
    "rms_mix_bwd0": [("P", "mix_w_in", 0)],
    "adam_att_w_o": [("A", "mix_w_in", 0), ("A", "small_late", 0)],
    "adam_s5_glu_w": [("B", "mix_w_in", 0), ("B", "small_late", 0)],
}


class Comm:
    def __init__(self, shards, shapes):
        self.shards, self.shapes = shards, shapes
        self.W, self.grads, self.slots = {}, {}, {}
        self.sib, self.pair = {}, {}
        self.small = {}

    def plan(self, host):
        items = SCHEDULE.get(host)
        if not items:
            return None
        p = Plan()
        for it in items:
            kind, name, l = it[:3]
            part, parts = it[3] if len(it) > 3 else (0, 1)
            if name.startswith("small"):
                sg = self.small[name]
                kdst = p.buf("slots:" + name, arr=self.slots.get(name), shape=S((8,) + sg.shape, f32), write=True)
                if kind == "A":
                    ReduceOp(p, p.buf("g:" + name, arr=sg), kdst, None, sg.shape, False, 0, 0, whole=True)
                else:
                    ForwardOp(p, kdst, None, whole=True)
                continue
            nl, R, C_ = self.shapes[name]
            rows = name in ROW_SHARDED
            r0, nr = part * (R // parts), R // parts
            if kind == "G":
                sh = self.shards[name]
                kdst = p.buf(f"W:{name}:{l}", arr=self.W.get((name, l)), shape=S((4 * R, C_) if rows else (R, 4 * C_), sh.dtype),
                             write=True)
                GatherOp(p, p.buf("shard:" + name, arr=sh), kdst, l, self.shapes[name], rows, r0, nr, split=(nr % 32 == 0))
            elif name in DIRECT:
                g = self.grads[(name, l)]
                kdst = p.buf("slots:" + name, arr=self.slots.get(name), shape=S((8, nl, R, C_), g.dtype), write=True)
                if kind == "A":
                    ReduceOp(p, p.buf(f"g:{name}:{l}", arr=g), kdst, l, self.shapes[name], rows, r0, nr)
                else:
                    ForwardOp(p, kdst, l)
            elif kind == "P":
                g = self.grads[(name, l)]
                ksib = p.buf(f"sib:{name}:{l}", shape=S((4 * R // 2, C_) if rows else (R // 2, 4 * C_), g.dtype), write=True)
                PairOp(p, p.buf(f"g:{name}:{l}", arr=g), ksib, self.shapes[name], rows)
            else:
                if (name, l) not in self.pair:
                    self.pair[(name, l)] = pair_sum(self.grads[(name, l)], self.sib[(name, l)], rows, R, f"pair_sum_{name}{l}")
                h = self.pair[(name, l)]
                kdst = p.buf("slots:" + name, arr=self.slots.get(name), shape=S((4, nl, R, C_), h.dtype), write=True)
                if kind == "A":
                    ReduceOp(p, p.buf(f"h:{name}:{l}", arr=h), kdst, l, self.shapes[name], rows, r0 // 2, nr // 2, half=True)
                else:
                    HalfForwardOp(p, kdst, l, self.shapes[name])
        return p

    def done(self, p):
        for k, arr in p.out.items():
            tag, name = k.split(":")[:2]
            if tag == "W":
                self.W[(name, int(k.split(":")[2]))] = arr
            elif tag == "sib":
                self.sib[(name, int(k.split(":")[2]))] = arr
            else:
                self.slots[name] = arr


def _adamw(w, g, m, v):
    m = B1 * m + (1.0 - B1) * g
    v = B2 * v + (1.0 - B2) * jnp.square(g)
    m_hat = m / (1.0 - B1 ** STEP)
    v_hat = v / (1.0 - B2 ** STEP)
    return -LR * (m_hat / (jnp.sqrt(v_hat) + AEPS) + WD * w), m, v


def adam_big(w, m, v, slots, name, plan=None):
    nl, R, C = w.shape
    ns = slots.shape[0]
    tr = 128 if R % 128 == 0 else (64 if R % 64 == 0 else R)

    def body(w_ref, m_ref, v_ref, s_ref, g_ref, d_ref, nm_ref, nv_ref):
        g = s_ref[0].astype(f32)
        for s in range(1, ns):
            g = g + s_ref[s].astype(f32)
        d, nm_, nv_ = _adamw(w_ref[...], g, m_ref[...], v_ref[...])
        g_ref[...] = g
        d_ref[...] = d
        nm_ref[...] = nm_
        nv_ref[...] = nv_

    blk = pl.BlockSpec((None, tr, C), lambda l, i: (l, i, 0))
    return pcall(body, plan, grid=(nl, R // tr),
                 in_specs=[blk, blk, blk, pl.BlockSpec((ns, None, tr, C), lambda l, i: (0, l, i, 0))],
                 out_specs=[blk] * 4, out_shape=[S((nl, R, C), f32)] * 4,
                 sem=("parallel", "parallel"), name=name, args=[w, m, v, slots])


def sum_slots(slots, name):
    R = slots.shape[1]

    def body(s_ref, g_ref):
        g = s_ref[0]
        for s in range(1, 8):
            g = g + s_ref[s]
        g_ref[...] = g

    return pl.pallas_call(
        body, grid=(R // 256,), in_specs=[pl.BlockSpec((8, 256, 128), lambda i: (0, i, 0))],
        out_specs=pl.BlockSpec((256, 128), lambda i: (i, 0)), out_shape=S((R, 128), f32),
        compiler_params=_cp(("parallel",)), name=name)(slots)


SMALL2D = {"norm_mix": (2, 1024), "norm_ffn": (2, 1024), "norm_final": (1, 1024), "s5_A_re": (32, 64), "s5_A_im": (32, 64),
           "s5_log_dt": (1, 32), "s5_B_re": (2048, 16), "s5_B_im": (2048, 16), "s5_C_re": (512, 64), "s5_C_im": (512, 64),
           "s5_D": (32, 16), "s5_glu_b": (1, 512), "hgrn_gamma": (2, 512), "hgrn_norm": (1, 512), "ffn_conv_b": (2, 5632)}


def adam_small(w, m, v, g, names, name):
    n = len(names)

    def body(*refs):
        for i in range(n):
            w_ref, m_ref, v_ref, g_ref = refs[4 * i:4 * i + 4]
            d_ref, nm_ref, nv_ref = refs[4 * n + 3 * i:4 * n + 3 * i + 3]
            d, nm_, nv_ = _adamw(w_ref[...], g_ref[...], m_ref[...], v_ref[...])
            d_ref[...] = d
            nm_ref[...] = nm_
            nv_ref[...] = nv_

    args = [t[k] for k in names for t in (w, m, v, g)]
    outs = pl.pallas_call(body, out_shape=[S(SMALL2D[k], f32) for k in names for _ in range(3)],
                          compiler_params=_cp(), name=name)(*args)
    return {k: tuple(outs[3 * i:3 * i + 3]) for i, k in enumerate(names)}


def _pack(d, names):
    flat = jnp.concatenate([d[n].reshape(-1) for n in names])
    n = flat.shape[0]
    rows = -(-n // (256 * 128)) * 256
    return jnp.pad(flat, (0, rows * 128 - n)).reshape(rows, 128)


def _unpack(p, like, names):
    flat = p.reshape(-1)
    out, off = {}, 0
    for n in names:
        sz = math.prod(like[n].shape)
        out[n] = flat[off:off + sz].reshape(like[n].shape)
        off += sz
    return out


def kernel(x, positions, norm_mix, norm_ffn, norm_final, mix_w_in, mix_w_out, s5_A_re, s5_A_im, s5_log_dt, s5_B_re, s5_B_im, s5_C_re, s5_C_im, s5_D, s5_glu_w, s5_glu_b, hgrn_gamma, hgrn_norm, att_w_qkv, att_w_o, ffn_w_in, ffn_conv_w, ffn_conv_b, ffn_w_out, loss_target, m_norm_mix, m_norm_ffn, m_norm_final, m_mix_w_in, m_mix_w_out, m_s5_A_re, m_s5_A_im, m_s5_log_dt, m_s5_B_re, m_s5_B_im, m_s5_C_re, m_s5_C_im, m_s5_D, m_s5_glu_w, m_s5_glu_b, m_hgrn_gamma, m_hgrn_norm, m_att_w_qkv, m_att_w_o, m_ffn_w_in, m_ffn_conv_w, m_ffn_conv_b, m_ffn_w_out, v_norm_mix, v_norm_ffn, v_norm_final, v_mix_w_in, v_mix_w_out, v_s5_A_re, v_s5_A_im, v_s5_log_dt, v_s5_B_re, v_s5_B_im, v_s5_C_re, v_s5_C_im, v_s5_D, v_s5_glu_w, v_s5_glu_b, v_hgrn_gamma, v_hgrn_norm, v_att_w_qkv, v_att_w_o, v_ffn_w_in, v_ffn_conv_w, v_ffn_conv_b, v_ffn_w_out):
    a = dict(locals())
    weights = BIG + SMALL
    w = {n: a[n] for n in weights}
    m = {n: a["m_" + n] for n in weights}
    v = {n: a["v_" + n] for n in weights}
    shards = {"ffn_conv_w": ffn_conv_w}
    C = Comm(shards, {n: w[n].shape for n in BIG})
    for n in ("mix_w_in", "ffn_w_in", "mix_w_out", "s5_glu_w", "ffn_w_out", "att_w_qkv", "att_w_o"):
        shards[n] = hosted(C, "cast_" + n, lambda p: cast_bf16(w[n], "cast_" + n, plan=p))
    sm = {n: w[n] for n in SMALL}
    sm["ffn_conv_b3"] = ffn_conv_b.reshape(2, 1, 2 * DFF)
    loss, gx, _, _ = local_step(x[0], positions.reshape(L, 1), loss_target[0], sm, C.W, C)
    res = {}
    for n in ("att_w_o", "s5_glu_w", "ffn_w_in", "ffn_w_out", "att_w_qkv", "mix_w_out", "ffn_conv_w", "mix_w_in"):
        res[n] = hosted(C, "adam_" + n, lambda p: adam_big(w[n], m[n], v[n], C.slots[n], "adam_" + n, plan=p))
    for names, key in ((SMALL_EARLY, "small_early"), (SMALL_LATE, "small_late")):
        g = _unpack(sum_slots(C.slots[key], "sum_" + key), w, names)

        def two_d(t):
            return {n: t[n].reshape(SMALL2D[n]) for n in names}

        upd = adam_small(two_d(w), two_d(m), two_d(v), two_d(g), names, "adam_" + key)
        for n in names:
            res[n] = (g[n],) + tuple(t.reshape(w[n].shape) for t in upd[n])
    total = lax.psum(loss[0, 0], ("x", "y", "c"))
    order = ("norm_mix", "norm_ffn", "norm_final", "mix_w_in", "mix_w_out", "s5_A_re", "s5_A_im", "s5_log_dt", "s5_B_re", "s5_B_im",
             "s5_C_re", "s5_C_im", "s5_D", "s5_glu_w", "s5_glu_b", "hgrn_gamma", "hgrn_norm", "att_w_qkv", "att_w_o", "ffn_w_in",
             "ffn_conv_w", "ffn_conv_b", "ffn_w_out")
    return (total, gx[None], *[res[n][0] for n in order], *[res[n][1] for n in order], *[res[n][2] for n in order],
            *[res[n][3] for n in order])
```

```python
import functools
import math

import numpy as np
import jax
import jax.numpy as jnp
from jax import lax
from jax.experimental import pallas as pl
from jax.experimental.pallas import tpu as pltpu

f32 = jnp.float32
BF = jnp.bfloat16
HI = lax.Precision.HIGHEST
S = jax.ShapeDtypeStruct
MESH = pl.DeviceIdType.MESH

L = 2048
D = 1024
EPS = 1e-6
S5W = 512
NST = 2048
HGC = 128
HGB = 32
DFF = 2816
ROPE_THETA = 500000.0
LR, B1, B2, AEPS, WD, STEP = 0.001, 0.9, 0.999, 1e-08, 0.01, 10
VMEM_LIMIT = 56 * 1024 * 1024


def _cp(sem=None):
    return pltpu.CompilerParams(dimension_semantics=sem, vmem_limit_bytes=VMEM_LIMIT)


ANY = pl.BlockSpec(memory_space=pl.ANY)
ROW_SHARDED = ("mix_w_out", "s5_glu_w", "ffn_w_out")


def _coords():
    x, y, c = lax.axis_index("x"), lax.axis_index("y"), lax.axis_index("c")
    return x, y, c, 2 * x + y, [(1 - x, y), (x, 1 - y), (1 - x, 1 - y)]


def _rows(start, n):
    return pl.ds(start if isinstance(start, int) else pl.multiple_of(start, 8), n)


def _cols(q, n):
    return pl.ds(pl.multiple_of(q * n, 128), n)


class Plan:
    def __init__(self):
        self.bufs, self.ops, self.nsem, self.out = {}, [], 0, {}

    def buf(self, key, arr=None, shape=None, write=False):
        b = self.bufs.setdefault(key, dict(arr=arr, shape=shape, write=False))
        b["write"] = b["write"] or write
        return key

    def add(self, op):
        op.base = self.nsem
        self.nsem += op.nsem
        self.ops.append(op)


class GatherOp:
    nsem = 13

    def __init__(self, plan, ksrc, kdst, l, shard_shape, rows, r0, nr, split):
        self.ksrc, self.kdst, self.l, (_, self.R, self.C), self.rows, self.r0, self.nr, self.split = (
            ksrc, kdst, l, shard_shape, rows, r0, nr, split)
        self.h = nr // 2 if split else nr
        plan.add(self)

    def _dst(self, R_, q, start, n):
        if self.rows:
            return R_[self.kdst].at[_rows(q * self.R + start, n), :]
        return R_[self.kdst].at[_rows(start, n), _cols(q, self.C)]

    def _mine(self, c):
        return self.r0 + (c * self.h if self.split else 0)

    def _theirs(self, c):
        return self.r0 + ((1 - c) * self.h if self.split else 0)

    def _copies(self, R_, sems):
        x, y, c, me, others = _coords()
        src = R_[self.ksrc]
        local = pltpu.make_async_copy(src.at[self.l, _rows(self.r0, self.nr), :], self._dst(R_, me, self.r0, self.nr),
                                      sems.at[self.base + 12])
        send, fwd = [], []
        for k, (px, py) in enumerate(others):
            q = 2 * px + py
            send.append((
                pltpu.make_async_remote_copy(src.at[self.l, _rows(self._mine(c), self.h), :], self._dst(R_, me, self._mine(c), self.h),
                                             sems.at[self.base + k], sems.at[self.base + 3 + k], device_id=(px, py, c), device_id_type=MESH),
                pltpu.make_async_remote_copy(src.at[self.l, _rows(self._mine(c), self.h), :], self._dst(R_, q, self._mine(c), self.h),
                                             sems.at[self.base + k], sems.at[self.base + 3 + k], device_id=(px, py, c), device_id_type=MESH)))
            fwd.append((
                pltpu.make_async_remote_copy(self._dst(R_, q, self._mine(c), self.h), self._dst(R_, q, self._mine(c), self.h),
                                             sems.at[self.base + 6 + k], sems.at[self.base + 9 + k], device_id=(x, y, 1 - c), device_id_type=MESH),
                pltpu.make_async_remote_copy(self._dst(R_, q, self._theirs(c), self.h), self._dst(R_, q, self._theirs(c), self.h),
                                             sems.at[self.base + 6 + k], sems.at[self.base + 9 + k], device_id=(x, y, 1 - c), device_id_type=MESH)))
        return local, send, fwd

    def start(self, R_, sems):
        local, send, _ = self._copies(R_, sems)
        local.start()
        for out, _ in send:
            out.start()

    def finish(self, R_, sems):
        local, send, fwd = self._copies(R_, sems)
        for k in range(3):
            send[k][1].wait_recv()
            if self.split:
                fwd[k][0].start()
        for k in range(3):
            if self.split:
                fwd[k][1].wait_recv()
                fwd[k][0].wait_send()
            send[k][0].wait_send()
        local.wait()


class ReduceOp:
    nsem = 7

    def __init__(self, plan, ksrc, kdst, l, shard_shape, rows, r0, nr, whole=False, half=False):
        self.ksrc, self.kdst, self.l, (self.R, self.C), self.rows, self.r0, self.nr, self.whole, self.half = (
            ksrc, kdst, l, shard_shape[-2:], rows, r0, nr, whole, half)
        plan.add(self)

    def _piece(self, R_, q):
        g = R_[self.ksrc]
        if self.whole:
            return g
        if self.rows:
            return g.at[_rows(q * (self.R // 2 if self.half else self.R) + self.r0, self.nr), :]
        return g.at[_rows(self.r0, self.nr), _cols(q, self.C)]

    def _slot(self, R_, q, c):
        if self.whole:
            return R_[self.kdst].at[2 * q + c]
        if self.half:
            return R_[self.kdst].at[q, self.l, _rows(c * (self.R // 2) + self.r0, self.nr), :]
        return R_[self.kdst].at[2 * q + c, self.l, _rows(self.r0, self.nr), :]

    def _copies(self, R_, sems):
        x, y, c, me, others = _coords()
        local = pltpu.make_async_copy(self._piece(R_, me), self._slot(R_, me, c), sems.at[self.base + 6])
        send = []
        for k, (px, py) in enumerate(others):
            q = 2 * px + py
            send.append((
                pltpu.make_async_remote_copy(self._piece(R_, q), self._slot(R_, me, c), sems.at[self.base + k],
                                             sems.at[self.base + 3 + k], device_id=(px, py, c), device_id_type=MESH),
                pltpu.make_async_remote_copy(self._piece(R_, q), self._slot(R_, q, c), sems.at[self.base + k],
                                             sems.at[self.base + 3 + k], device_id=(px, py, c), device_id_type=MESH)))
        return local, send

    def start(self, R_, sems):
        local, send = self._copies(R_, sems)
        local.start()
        for out, _ in send:
            out.start()

    def finish(self, R_, sems):
        local, send = self._copies(R_, sems)
        local.wait()
        for out, inn in send:
            inn.wait_recv()
            out.wait_send()


class ForwardOp:
    nsem = 8

    def __init__(self, plan, kdst, l, whole=False):
        self.kdst, self.l, self.whole = kdst, l, whole
        plan.add(self)

    def _slot(self, R_, s):
        return R_[self.kdst].at[s] if self.whole else R_[self.kdst].at[s, self.l]

    def _copies(self, R_, sems):
        x, y, c, me, others = _coords()
        return [(pltpu.make_async_remote_copy(self._slot(R_, 2 * q + c), self._slot(R_, 2 * q + c), sems.at[self.base + q],
                                              sems.at[self.base + 4 + q], device_id=(x, y, 1 - c), device_id_type=MESH),
                 pltpu.make_async_remote_copy(self._slot(R_, 2 * q + 1 - c), self._slot(R_, 2 * q + 1 - c), sems.at[self.base + q],
                                              sems.at[self.base + 4 + q], device_id=(x, y, 1 - c), device_id_type=MESH))
                for q in range(4)]

    def start(self, R_, sems):
        for out, _ in self._copies(R_, sems):
            out.start()

    def finish(self, R_, sems):
        for out, inn in self._copies(R_, sems):
            inn.wait_recv()
            out.wait_send()


class PairOp:
    nsem = 8

    def __init__(self, plan, ksrc, kdst, shard_shape, rows):
        self.ksrc, self.kdst, (self.R, self.C), self.rows = ksrc, kdst, shard_shape[-2:], rows
        plan.add(self)

    def _copies(self, R_, sems):
        x, y, c, me, others = _coords()
        g, dst, h = R_[self.ksrc], R_[self.kdst], self.R // 2
        out = []
        for q in range(4 if self.rows else 1):
            src = g.at[_rows(q * self.R + (1 - c) * h, h), :]
            land = dst.at[_rows(q * h, h), :]
            out.append(pltpu.make_async_remote_copy(src, land, sems.at[self.base + q], sems.at[self.base + 4 + q],
                                                    device_id=(x, y, 1 - c), device_id_type=MESH))
        return out

    def start(self, R_, sems):
        for cp in self._copies(R_, sems):
            cp.start()

    def finish(self, R_, sems):
        for cp in self._copies(R_, sems):
            cp.wait_recv()
            cp.wait_send()


class HalfForwardOp:
    nsem = 2

    def __init__(self, plan, kdst, l, shard_shape):
        self.kdst, self.l, self.R = kdst, l, shard_shape[-2]
        plan.add(self)

    def _copy(self, R_, sems, core):
        x, y, c, me, others = _coords()
        part = R_[self.kdst].at[:, self.l, _rows((c if core == "mine" else 1 - c) * (self.R // 2), self.R // 2), :]
        return pltpu.make_async_remote_copy(part, part, sems.at[self.base], sems.at[self.base + 1],
                                            device_id=(x, y, 1 - c), device_id_type=MESH)

    def start(self, R_, sems):
        self._copy(R_, sems, "mine").start()

    def finish(self, R_, sems):
        self._copy(R_, sems, "theirs").wait_recv()
        self._copy(R_, sems, "mine").wait_send()


def pair_sum(g, gsib, rows, R, name):
    h = R // 2
    W = g.shape[1]
    tr = h if h * W * 2 <= 2 ** 21 else 128
    nq = 4 if rows else 1

    def body(c_ref, a_ref, b_ref, o_ref):
        o_ref[...] = (a_ref[...].astype(f32) + b_ref[...].astype(f32)).astype(o_ref.dtype)

    half = pl.BlockSpec((tr, W), lambda q, i, c_ref: (q * (h // tr) + i, 0))
    mine = pl.BlockSpec((tr, W), lambda q, i, c_ref: (q * (R // tr) + c_ref[0] * (h // tr) + i, 0))
    return pl.pallas_call(
        body, grid_spec=pltpu.PrefetchScalarGridSpec(num_scalar_prefetch=1, grid=(nq, h // tr), in_specs=[mine, half],
                                                     out_specs=half),
        out_shape=S(gsib.shape, g.dtype), compiler_params=_cp(("parallel", "parallel")),
        name=name)(lax.axis_index("c").reshape(1).astype(jnp.int32), g, gsib)


def pcall(body, plan, *, grid, in_specs, out_specs, out_shape, scratch_shapes=(), sem, name, args):
    multi = isinstance(out_shape, (list, tuple))
    if plan is None or not plan.ops:
        return pl.pallas_call(body, grid=grid, in_specs=in_specs, out_specs=out_specs, out_shape=out_shape,
                              scratch_shapes=list(scratch_shapes), compiler_params=_cp(sem), name=name)(*args)
    outs = list(out_shape) if multi else [out_shape]
    ospecs = list(out_specs) if multi else [out_specs]
    kin = [k for k, b in plan.bufs.items() if b["arr"] is not None]
    kout = [k for k, b in plan.bufs.items() if b["write"]]
    n_in, n_out, n_scr = len(in_specs), len(outs), len(scratch_shapes)

    def wrapped(*refs):
        o0 = n_in + len(kin)
        s0 = o0 + n_out + len(kout)
        R_ = dict(zip(kin, refs[n_in:o0]))
        R_.update(zip(kout, refs[o0 + n_out:s0]))
        sems = refs[s0 + n_scr]
        first = functools.reduce(jnp.logical_and, [pl.program_id(d) == 0 for d in range(len(grid))])
        last = functools.reduce(jnp.logical_and, [pl.program_id(d) == grid[d] - 1 for d in range(len(grid))])

        @pl.when(first)
        def _():
            for op in plan.ops:
                op.start(R_, sems)

        body(*refs[:n_in], *refs[o0:o0 + n_out], *refs[s0:s0 + n_scr])

        @pl.when(last)
        def _():
            for op in plan.ops:
                op.finish(R_, sems)

    def shape_of(k):
        b = plan.bufs[k]
        return S(b["arr"].shape, b["arr"].dtype) if b["arr"] is not None else b["shape"]

    res = pl.pallas_call(
        wrapped, grid=grid, in_specs=list(in_specs) + [ANY] * len(kin), out_specs=ospecs + [ANY] * len(kout),
        out_shape=outs + [shape_of(k) for k in kout],
        scratch_shapes=list(scratch_shapes) + [pltpu.SemaphoreType.DMA((plan.nsem,))],
        input_output_aliases={n_in + kin.index(k): n_out + kout.index(k) for k in kout if plan.bufs[k]["arr"] is not None},
        compiler_params=pltpu.CompilerParams(dimension_semantics=("arbitrary",) * len(grid), vmem_limit_bytes=VMEM_LIMIT,
                                             has_side_effects=True),
        name=name)(*args, *[plan.bufs[k]["arr"] for k in kin])
    plan.out = dict(zip(kout, res[n_out:]))
    return list(res[:n_out]) if multi else res[0]


def _dg(a, b, ca, cb):
    return lax.dot_general(a.astype(BF), b.astype(BF), (((ca,), (cb,)), ((), ())), preferred_element_type=f32)


@jax.custom_vjp
def dot_nn(a, b):
    return _dg(a, b, 1, 0)


@jax.custom_vjp
def dot_nt(a, b):
    return _dg(a, b, 1, 1)


@jax.custom_vjp
def dot_tn(a, b):
    return _dg(a, b, 0, 0)


dot_nn.defvjp(lambda a, b: (dot_nn(a, b), (a, b)),
              lambda r, g: (dot_nt(g, r[1]).astype(r[0].dtype), dot_tn(r[0], g).astype(r[1].dtype)))
dot_nt.defvjp(lambda a, b: (dot_nt(a, b), (a, b)),
              lambda r, g: (dot_nn(g, r[1]).astype(r[0].dtype), dot_tn(g, r[0]).astype(r[1].dtype)))
dot_tn.defvjp(lambda a, b: (dot_tn(a, b), (a, b)),
              lambda r, g: (dot_nt(r[1], g).astype(r[0].dtype), dot_nn(r[0], g).astype(r[1].dtype)))


def matmul(a, b, *, mode, tm, tn, tk, out_dtype=f32, add=None, b_lead=None, a_spec=None, b_spec=None, dims=None, plan=None, name):
    a_over, b_over = a_spec, b_spec
    if mode == "nn":
        (M, K), N = a.shape[-2:], b.shape[-1]
        a_spec = pl.BlockSpec((tm, tk), lambda i, j, k: (i, k))
        b_blk, b_idx, ca, cb = (tk, tn), (lambda i, j, k: (k, j)), 1, 0
    elif mode == "nt":
        (M, K), N = a.shape[-2:], b.shape[-2]
        a_spec = pl.BlockSpec((tm, tk), lambda i, j, k: (i, k))
        b_blk, b_idx, ca, cb = (tn, tk), (lambda i, j, k: (j, k)), 1, 1
    else:
        (K, M), N = a.shape[-2:], b.shape[-1]
        a_spec = pl.BlockSpec((tk, tm), lambda i, j, k: (k, i))
        b_blk, b_idx, ca, cb = (tk, tn), (lambda i, j, k: (k, j)), 0, 0
    if dims is not None:
        M, N, K = dims
    assert M % tm == 0 and N % tn == 0 and K % tk == 0, (name, M, N, K, tm, tn, tk)
    if b_lead is None:
        b_spec = pl.BlockSpec(b_blk, b_idx)
    else:
        b_spec = pl.BlockSpec((None,) + b_blk, lambda i, j, k: (b_lead,) + b_idx(i, j, k))
    if a_over is not None:
        a_spec = a_over
    if b_over is not None:
        b_spec = b_over
    nk = K // tk
    has_add = add is not None

    def body(*refs):
        a_ref, b_ref = refs[0], refs[1]
        add_ref = refs[2] if has_add else None
        o_ref = refs[2 + has_add]
        p = _dg(a_ref[...], b_ref[...], ca, cb)

        def fin(v):
            if has_add:
                v = v + add_ref[...].astype(f32)
            o_ref[...] = v.astype(o_ref.dtype)

        if nk == 1:
            fin(p)
        else:
            acc = refs[3 + has_add]
            k = pl.program_id(2)

            @pl.when(k == 0)
            def _():
                acc[...] = p

            @pl.when(k > 0)
            def _():
                acc[...] += p

            @pl.when(k == nk - 1)
            def _():
                fin(acc[...])

    in_specs = [a_spec, b_spec]
    args = [a, b]
    if has_add:
        in_specs.append(pl.BlockSpec((tm, tn), lambda i, j, k: (i, j)))
        args.append(add)
    return pcall(body, plan, grid=(M // tm, N // tn, nk), in_specs=in_specs,
                 out_specs=pl.BlockSpec((tm, tn), lambda i, j, k: (i, j)), out_shape=S((M, N), out_dtype),
                 scratch_shapes=[pltpu.VMEM((tm, tn), f32)] if nk > 1 else [],
                 sem=("parallel", "parallel", "arbitrary"), name=name, args=args)


def _rms(xv, gv):
    return xv * lax.rsqrt(jnp.mean(xv * xv, axis=-1, keepdims=True) + EPS) * gv


TR = 512


def rms_fwd(x, g, name):
    def body(x_ref, g_ref, o_ref):
        o_ref[...] = _rms(x_ref[...], g_ref[...]).astype(o_ref.dtype)

    return pl.pallas_call(
        body, grid=(L // TR,),
        in_specs=[pl.BlockSpec((TR, D), lambda i: (i, 0)), pl.BlockSpec((1, D), lambda i: (0, 0))],
        out_specs=pl.BlockSpec((TR, D), lambda i: (i, 0)), out_shape=S((L, D), BF),
        compiler_params=_cp(("parallel",)), name=name)(x, g)


def rms_bwd(x, g, dys, dres, name, plan=None):
    nd = len(dys)

    def body(*refs):
        x_ref, g_ref = refs[0], refs[1]
        dr_ref, dh_ref, dg_ref = refs[2 + nd:]
        dy = refs[2][...].astype(f32)
        for r in refs[3:2 + nd]:
            dy = dy + r[...].astype(f32)
        _, vjp = jax.vjp(_rms, x_ref[...], g_ref[...])
        dx, dg = vjp(dy)
        dh_ref[...] = dr_ref[...] + dx

        @pl.when(pl.program_id(0) == 0)
        def _():
            dg_ref[...] = jnp.zeros_like(dg_ref)

        dg_ref[...] += dg

    row = pl.BlockSpec((TR, D), lambda i: (i, 0))
    vec = pl.BlockSpec((1, D), lambda i: (0, 0))
    return pcall(body, plan, grid=(L // TR,), in_specs=[row, vec] + [row] * (nd + 1), out_specs=[row, vec],
                 out_shape=[S((L, D), f32), S((1, D), f32)], sem=("arbitrary",), name=name, args=[x, g, *dys, dres])


def loss_head(h, g, tgt):
    def f(hv, gv, tv):
        y = _rms(hv, gv)
        return 0.5 * jnp.sum(jnp.mean(jnp.square(y - tv), axis=-1))

    def body(h_ref, g_ref, t_ref, l_ref, dh_ref, dg_ref):
        val, vjp = jax.vjp(f, h_ref[...], g_ref[...], t_ref[...])
        dh, dg, _ = vjp(jnp.ones((), f32))
        dh_ref[...] = dh

        @pl.when(pl.program_id(0) == 0)
        def _():
            dg_ref[...] = jnp.zeros_like(dg_ref)
            l_ref[...] = jnp.zeros_like(l_ref)

        dg_ref[...] += dg
        l_ref[...] += jnp.full((1, 128), val, f32)

    row = pl.BlockSpec((TR, D), lambda i: (i, 0))
    vec = pl.BlockSpec((1, D), lambda i: (0, 0))
    return pl.pallas_call(
        body, grid=(L // TR,), in_specs=[row, vec, row],
        out_specs=[pl.BlockSpec((1, 128), lambda i: (0, 0)), row, vec],
        out_shape=[S((1, 128), f32), S((L, D), f32), S((1, D), f32)],
        compiler_params=_cp(("arbitrary",)), name="loss_head")(h, g, tgt)


def _col_to_row(c):
    n = c.shape[0]
    t = jnp.broadcast_to(c, (n, 128)).T
    r = lax.broadcasted_iota(jnp.int32, (128, n), 0)
    return jnp.sum(jnp.where(r == 0, t, 0.0), axis=0, keepdims=True)


def _s5_param_map(are, aim, ldt_row, bre, bim, cre, cim):
    n = NST
    gi = lax.broadcasted_iota(jnp.int32, (n, 32), 0) // 64
    gj = lax.broadcasted_iota(jnp.int32, (n, 32), 1)
    ldt = jnp.sum(jnp.where(gi == gj, ldt_row, 0.0), axis=1, keepdims=True)
    dt = jnp.exp(ldt)
    mag = jnp.exp(are * dt)
    abr = mag * jnp.cos(aim * dt)
    abi = mag * jnp.sin(aim * dt)
    den = are * are + aim * aim
    nr, ni = abr - 1.0, abi
    cr = (nr * are + ni * aim) / den
    ci = (ni * are - nr * aim) / den
    bbr = cr * bre - ci * bim
    bbi = cr * bim + ci * bre
    tc = lax.broadcasted_iota(jnp.int32, (16, 128), 0)
    tl = lax.broadcasted_iota(jnp.int32, (16, 128), 1)
    T = (tl % 16 == tc).astype(f32)
    mr = (lax.broadcasted_iota(jnp.int32, (n, 128), 0) // 64) % 8
    mc = lax.broadcasted_iota(jnp.int32, (n, 128), 1) // 16
    mask = (mr == mc).astype(f32)

    def expand(v):
        return jnp.dot(v, T, precision=HI, preferred_element_type=f32) * mask

    return expand(bbr), expand(bbi), expand(cre), expand(cim), _col_to_row(abr), _col_to_row(abi)


def s5_params_fwd(are, aim, ldt_row, bre, bim, cre, cim):
    def body(*refs):
        outs = _s5_param_map(*[r[...] for r in refs[:7]])
        for o_ref, o in zip(refs[7:], outs):
            o_ref[...] = o

    return pl.pallas_call(
        body, out_shape=[S((NST, 128), f32)] * 4 + [S((1, NST), f32)] * 2,
        compiler_params=_cp(), name="s5_params_fwd")(are, aim, ldt_row, bre, bim, cre, cim)


def s5_params_bwd(are, aim, ldt_row, bre, bim, cre, cim, cots):
    def body(*refs):
        _, vjp = jax.vjp(_s5_param_map, *[r[...] for r in refs[:7]])
        gs = vjp(tuple(r[...] for r in refs[7:13]))
        for o_ref, o in zip(refs[13:], gs):
            o_ref[...] = o

    return pl.pallas_call(
        body, out_shape=[S((NST, 1), f32)] * 2 + [S((1, 32), f32)] + [S((NST, 16), f32)] * 4,
        compiler_params=_cp(), name="s5_params_bwd")(are, aim, ldt_row, bre, bim, cre, cim, *cots)


def _cpowers(ar, ai):
    out = [(ar, ai)]
    for _ in range(7):
        pr, pi = out[-1]
        out.append((pr * ar - pi * ai, pr * ai + pi * ar))
    return out


def _ctable(pw, rid, power):
    tr_ = jnp.zeros(rid.shape, f32)
    ti_ = jnp.zeros(rid.shape, f32)
    for r in range(8):
        pr, pi = pw[power(r) - 1]
        tr_ = jnp.where(rid == r, pr, tr_)
        ti_ = jnp.where(rid == r, pi, ti_)
    return tr_, ti_


NT5 = 4
RC = 256


def s5_scan_fwd(proj, wbr, wbi, wcr, wci, abr, abi, drow, plan=None):
    def body(u_ref, wbr_ref, wbi_ref, wcr_ref, wci_ref, ar_ref, ai_ref, d_ref, xr_ref, xi_ref, y_ref):
        wbr_v, wbi_v = wbr_ref[...], wbi_ref[...]
        for r in range(L // RC):
            rows = pl.ds(r * RC, RC)
            ub = u_ref[rows, :]
            xr_ref[rows, :] = dot_nt(ub, wbr_v)
            xi_ref[rows, :] = dot_nt(ub, wbi_v)
        pw = _cpowers(ar_ref[...], ai_ref[...])
        rid = lax.broadcasted_iota(jnp.int32, (8, 512), 0)
        tr_, ti_ = _ctable(pw, rid, lambda r: r + 1)

        def group(j, c):
            cr, ci = c
            rows = pl.ds(pl.multiple_of(j * 8, 8), 8)
            br, bi = xr_ref[rows, :], xi_ref[rows, :]
            for s in (1, 2, 4):
                pr, pi = pw[s - 1]
                sr = jnp.where(rid >= s, pltpu.roll(br, s, 0), 0.0)
                si = jnp.where(rid >= s, pltpu.roll(bi, s, 0), 0.0)
                br, bi = br + pr * sr - pi * si, bi + pr * si + pi * sr
            br, bi = br + tr_ * cr - ti_ * ci, bi + tr_ * ci + ti_ * cr
            xr_ref[rows, :] = br
            xi_ref[rows, :] = bi
            return br[7:8], bi[7:8]

        z = jnp.zeros((1, 512), f32)
        lax.fori_loop(0, L // 8, group, (z, z), unroll=2)
        wcr_v, wci_v, dv = wcr_ref[...], wci_ref[...], d_ref[...]
        for r in range(L // RC):
            rows = pl.ds(r * RC, RC)
            y_ref[rows, :] = (dot_nn(xr_ref[rows, :], wcr_v) - dot_nn(xi_ref[rows, :], wci_v)
                              + dv * u_ref[rows, :])

    wspec = pl.BlockSpec((512, 128), lambda j: (j, 0))
    aspec = pl.BlockSpec((1, 512), lambda j: (0, j))
    return pcall(
        body, plan, grid=(NT5,),
        in_specs=[pl.BlockSpec((L, 128), lambda j: (0, j)), wspec, wspec, wspec, wspec, aspec, aspec,
                  pl.BlockSpec((1, 128), lambda j: (0, j))],
        out_specs=[pl.BlockSpec((L, 512), lambda j: (0, j)), pl.BlockSpec((L, 512), lambda j: (0, j)),
                   pl.BlockSpec((L, 128), lambda j: (0, j))],
        out_shape=[S((L, NST), f32), S((L, NST), f32), S((L, S5W), f32)],
        sem=("parallel",), name="s5_scan_fwd", args=[proj, wbr, wbi, wcr, wci, abr, abi, drow])


def s5_scan_bwd(dy, proj, xs_re, xs_im, wbr, wbi, wcr, wci, abr, abi, drow, plan=None):
    def body(dy_ref, u_ref, xr_ref, xi_ref, wbr_ref, wbi_ref, wcr_ref, wci_ref, ar_ref, ai_ref, d_ref,
             du_ref, gwbr_ref, gwbi_ref, gwcr_ref, gwci_ref, gar_ref, gai_ref, gd_ref, lr_ref, li_ref):
        wcr_v, wci_v = wcr_ref[...], wci_ref[...]
        gwcr = jnp.zeros((512, 128), f32)
        gwci = jnp.zeros((512, 128), f32)
        gd = jnp.zeros((1, 128), f32)
        for r in range(L // RC):
            rows = pl.ds(r * RC, RC)
            dyv = dy_ref[rows, :]
            lr_ref[rows, :] = dot_nt(dyv, wcr_v)
            li_ref[rows, :] = -dot_nt(dyv, wci_v)
            gwcr += dot_tn(xr_ref[rows, :], dyv)
            gwci -= dot_tn(xi_ref[rows, :], dyv)
            gd += jnp.sum(dyv * u_ref[rows, :], axis=0, keepdims=True)
        gwcr_ref[...] = gwcr
        gwci_ref[...] = gwci
        gd_ref[...] = gd
        pw = _cpowers(ar_ref[...], -ai_ref[...])
        rid = lax.broadcasted_iota(jnp.int32, (8, 512), 0)
        tr_, ti_ = _ctable(pw, rid, lambda r: 8 - r)

        def group(i, c):
            cr, ci, gar, gai = c
            j = L // 8 - 1 - i
            rows = pl.ds(pl.multiple_of(j * 8, 8), 8)
            br, bi = lr_ref[rows, :], li_ref[rows, :]
            for s in (1, 2, 4):
                pr, pi = pw[s - 1]
                sr = jnp.where(rid < 8 - s, pltpu.roll(br, 8 - s, 0), 0.0)
                si = jnp.where(rid < 8 - s, pltpu.roll(bi, 8 - s, 0), 0.0)
                br, bi = br + pr * sr - pi * si, bi + pr * si + pi * sr
            br, bi = br + tr_ * cr - ti_ * ci, bi + tr_ * ci + ti_ * cr
            lr_ref[rows, :] = br
            li_ref[rows, :] = bi
            nr = jnp.where(rid < 7, pltpu.roll(br, 7, 0), cr)
            ni = jnp.where(rid < 7, pltpu.roll(bi, 7, 0), ci)
            xr, xi = xr_ref[rows, :], xi_ref[rows, :]
            return br[0:1], bi[0:1], gar + xr * nr + xi * ni, gai + xr * ni - xi * nr

        z = jnp.zeros((1, 512), f32)
        z8 = jnp.zeros((8, 512), f32)
        _, _, gar, gai = lax.fori_loop(0, L // 8, group, (z, z, z8, z8), unroll=2)
        gar_ref[...] = jnp.sum(gar, axis=0, keepdims=True)
        gai_ref[...] = jnp.sum(gai, axis=0, keepdims=True)
        wbr_v, wbi_v, dv = wbr_ref[...], wbi_ref[...], d_ref[...]
        gwbr = jnp.zeros((512, 128), f32)
        gwbi = jnp.zeros((512, 128), f32)
        for r in range(L // RC):
            rows = pl.ds(r * RC, RC)
            lrv, liv, uv = lr_ref[rows, :], li_ref[rows, :], u_ref[rows, :]
            du_ref[rows, :] = (dot_nn(lrv, wbr_v) + dot_nn(liv, wbi_v) + dv * dy_ref[rows, :]).astype(du_ref.dtype)
            gwbr += dot_tn(lrv, uv)
            gwbi += dot_tn(liv, uv)
        gwbr_ref[...] = gwbr
        gwbi_ref[...] = gwbi

    wspec = pl.BlockSpec((512, 128), lambda j: (j, 0))
    aspec = pl.BlockSpec((1, 512), lambda j: (0, j))
    col = pl.BlockSpec((L, 128), lambda j: (0, j))
    st = pl.BlockSpec((L, 512), lambda j: (0, j))
    dspec = pl.BlockSpec((1, 128), lambda j: (0, j))
    return pcall(
        body, plan, grid=(NT5,),
        in_specs=[col, col, st, st, wspec, wspec, wspec, wspec, aspec, aspec, dspec],
        out_specs=[col, wspec, wspec, wspec, wspec, aspec, aspec, dspec],
        out_shape=[S((L, S5W), BF)] + [S((NST, 128), f32)] * 4 + [S((1, NST), f32)] * 2 + [S((1, S5W), f32)],
        scratch_shapes=[pltpu.VMEM((L, 512), f32), pltpu.VMEM((L, 512), f32)],
        sem=("parallel",), name="s5_scan_bwd", args=[dy, proj, xs_re, xs_im, wbr, wbi, wcr, wci, abr, abi, drow])


def _glu(y, w, b):
    z = jax.nn.gelu(y)
    return z * jax.nn.sigmoid(dot_nn(z, w) + b)


def s5_glu_fwd(y, w, b):
    def body(y_ref, w_ref, b_ref, o_ref):
        o_ref[...] = _glu(y_ref[...], w_ref[...], b_ref[...]).astype(o_ref.dtype)

    return pl.pallas_call(
        body, grid=(L // TR,),
        in_specs=[pl.BlockSpec((TR, S5W), lambda i: (i, 0)), pl.BlockSpec((S5W, S5W), lambda i: (0, 0)),
                  pl.BlockSpec((1, S5W), lambda i: (0, 0))],
        out_specs=pl.BlockSpec((TR, S5W), lambda i: (i, 0)), out_shape=S((L, S5W), BF),
        compiler_params=_cp(("parallel",)), name="s5_glu_fwd")(y, w, b)


def s5_glu_bwd(y, w, b, dmix):
    def body(y_ref, w_ref, b_ref, g_ref, dy_ref, dw_ref, db_ref):
        _, vjp = jax.vjp(_glu, y_ref[...], w_ref[...].astype(f32), b_ref[...])
        dy, dw, db = vjp(g_ref[...])
        dy_ref[...] = dy

        @pl.when(pl.program_id(0) == 0)
        def _():
            dw_ref[...] = jnp.zeros_like(dw_ref)
            db_ref[...] = jnp.zeros_like(db_ref)

        dw_ref[...] += dw
        db_ref[...] += db

    row = pl.BlockSpec((TR, S5W), lambda i: (i, 0))
    return pl.pallas_call(
        body, grid=(L // TR,),
        in_specs=[row, pl.BlockSpec((S5W, S5W), lambda i: (0, 0)), pl.BlockSpec((1, S5W), lambda i: (0, 0)), row],
        out_specs=[row, pl.BlockSpec((S5W, S5W), lambda i: (0, 0)), pl.BlockSpec((1, S5W), lambda i: (0, 0))],
        out_shape=[S((L, S5W), f32), S((S5W, S5W), f32), S((1, S5W), f32)],
        compiler_params=_cp(("arbitrary",)), name="s5_glu_bwd")(y, w, b, dmix)


def _dg3(a, b, ca, cb):
    ah, bh = a.astype(BF), b.astype(BF)
    al, bl = (a - ah.astype(f32)).astype(BF), (b - bh.astype(f32)).astype(BF)
    return _dg(ah, bh, ca, cb) + _dg(ah, bl, ca, cb) + _dg(al, bh, ca, cb)


@jax.custom_vjp
def hi_nn(a, b):
    return _dg3(a, b, 1, 0)


@jax.custom_vjp
def hi_nt(a, b):
    return _dg3(a, b, 1, 1)


@jax.custom_vjp
def hi_tn(a, b):
    return _dg3(a, b, 0, 0)


hi_nn.defvjp(lambda a, b: (hi_nn(a, b), (a, b)), lambda r, g: (hi_nt(g, r[1]), hi_tn(r[0], g)))
hi_nt.defvjp(lambda a, b: (hi_nt(a, b), (a, b)), lambda r, g: (hi_nn(g, r[1]), hi_tn(g, r[0])))
hi_tn.defvjp(lambda a, b: (hi_tn(a, b), (a, b)), lambda r, g: (hi_nt(r[1], g), hi_nn(r[0], g)))


def _hgrn_chunk(St, xq, xf, xi, xg, gam, ng):
    lb = jax.nn.sigmoid(gam[0:1] - gam[1:2])
    q = jax.nn.silu(xq)
    f = lb + (1.0 - lb) * jax.nn.sigmoid(xf)
    k = 1.0 - f
    g = jnp.log(f)
    ti = lax.broadcasted_iota(jnp.int32, (HGC, HGC), 0)
    si = lax.broadcasted_iota(jnp.int32, (HGC, HGC), 1)
    causal = si <= ti
    b = jnp.dot(causal.astype(f32), g, precision=HI, preferred_element_type=f32)
    qe = q * jnp.exp(b)
    o = dot_nt(qe, St)
    parts = []
    for i in range(HGC // HGB):
        r, n, mid = slice(HGB * i, HGB * (i + 1)), HGB * (i + 1), HGB * i + HGB // 2
        base = b[mid:mid + 1]
        sc = hi_nt(q[r] * jnp.exp(b[r] - base), k[:n] * jnp.exp(base - b[:n]))
        parts.append(dot_nn(jnp.where(causal[r, :n], sc, 0.0), xi[:n]))
    o = o + jnp.concatenate(parts, axis=0)
    bl = b[HGC - 1:HGC]
    St_new = St * jnp.exp(bl) + dot_tn(xi, k * jnp.exp(bl - b))
    o = o * lax.rsqrt(jnp.mean(o * o, axis=-1, keepdims=True) + EPS) * ng
    return St_new, o * jax.nn.silu(xg)


NCH = L // HGC


def hgrn_fwd(proj, gamma, hnorm, plan=None):
    def body(q_ref, f_ref, i_ref, g_ref, gam_ref, ng_ref, o_ref, ss_ref, st):
        @pl.when(pl.program_id(0) == 0)
        def _():
            st[...] = jnp.zeros_like(st)

        for h in range(4):
            sl = slice(h * 128, (h + 1) * 128)
            s0 = st[h]
            ss_ref[0, h] = s0
            s1, o = _hgrn_chunk(s0, q_ref[:, sl], f_ref[:, sl], i_ref[:, sl], g_ref[:, sl], gam_ref[:, sl], ng_ref[:, sl])
            st[h] = s1
            o_ref[:, sl] = o.astype(o_ref.dtype)

    def pj(n):
        return pl.BlockSpec((HGC, 512), lambda c: (c, n))

    return pcall(
        body, plan, grid=(NCH,),
        in_specs=[pj(1), pj(2), pj(3), pj(4), pl.BlockSpec((2, 512), lambda c: (0, 0)), pl.BlockSpec((1, 512), lambda c: (0, 0))],
        out_specs=[pl.BlockSpec((HGC, 512), lambda c: (c, 0)), pl.BlockSpec((1, 4, 128, 128), lambda c: (c, 0, 0, 0))],
        out_shape=[S((L, 512), BF), S((NCH, 4, 128, 128), f32)],
        scratch_shapes=[pltpu.VMEM((4, 128, 128), f32)],
        sem=("arbitrary",), name="hgrn_fwd", args=[proj, proj, proj, proj, gamma, hnorm])


def hgrn_bwd(proj, gamma, hnorm, ssave, dmix, du, plan=None):
    def body(q_ref, f_ref, i_ref, g_ref, gam_ref, ng_ref, ss_ref, do_ref, du_ref, dp_ref, dgam_ref, dng_ref, dst):
        @pl.when(pl.program_id(0) == 0)
        def _():
            dst[...] = jnp.zeros_like(dst)
            dgam_ref[...] = jnp.zeros_like(dgam_ref)
            dng_ref[...] = jnp.zeros_like(dng_ref)

        dp_ref[:, 0:512] = du_ref[...]
        for h in range(4):
            sl = slice(h * 128, (h + 1) * 128)
            _, vjp = jax.vjp(_hgrn_chunk, ss_ref[0, h], q_ref[:, sl], f_ref[:, sl], i_ref[:, sl], g_ref[:, sl],
                             gam_ref[:, sl], ng_ref[:, sl])
            ds, dq, df, di, dg, dgam, dng = vjp((dst[h], do_ref[:, sl]))
            dst[h] = ds
            for n, v in enumerate((dq, df, di, dg)):
                dp_ref[:, 512 * (n + 1) + h * 128: 512 * (n + 1) + (h + 1) * 128] = v.astype(dp_ref.dtype)
            dgam_ref[:, sl] += dgam
            dng_ref[:, sl] += dng

    def pj(n):
        return pl.BlockSpec((HGC, 512), lambda i: (NCH - 1 - i, n))

    return pcall(
        body, plan, grid=(NCH,),
        in_specs=[pj(1), pj(2), pj(3), pj(4), pl.BlockSpec((2, 512), lambda i: (0, 0)), pl.BlockSpec((1, 512), lambda i: (0, 0)),
                  pl.BlockSpec((1, 4, 128, 128), lambda i: (NCH - 1 - i, 0, 0, 0)), pj(1), pj(0)],
        out_specs=[pl.BlockSpec((HGC, 2560), lambda i: (NCH - 1 - i, 0)), pl.BlockSpec((2, 512), lambda i: (0, 0)),
                   pl.BlockSpec((1, 512), lambda i: (0, 0))],
        out_shape=[S((L, 2560), BF), S((2, 512), f32), S((1, 512), f32)],
        scratch_shapes=[pltpu.VMEM((4, 128, 128), f32)],
        sem=("arbitrary",), name="hgrn_bwd", args=[proj, proj, proj, proj, gamma, hnorm, ssave, dmix, du])


def _earlier(h_ref, k, r0, n):
    if r0 > 0:
        return h_ref[pl.ds(r0 - k, n), :]
    rid = lax.broadcasted_iota(jnp.int32, (8, h_ref.shape[1]), 0)
    head = jnp.where(rid >= k, pltpu.roll(h_ref[pl.ds(0, 8), :], k, 0), 0.0)
    return jnp.concatenate([head, h_ref[pl.ds(8 - k, n - 8), :]], axis=0)


def _conv3_rows(h_ref, w, b, r0, n=None):
    n = CR if n is None else n
    h1, h2 = _earlier(h_ref, 1, r0, n), _earlier(h_ref, 2, r0, n)
    return w[2:3] * h_ref[pl.ds(r0, n), :] + w[1:2] * h1 + w[0:1] * h2 + b, h1, h2


CT = 128
NCT = DFF // CT
CR = 64


def convact_fwd(hu, cw, cb, layer, plan=None):
    def body(ha_ref, hb_ref, wa_ref, wb_ref, ba_ref, bb_ref, o_ref):
        ca = _conv3_rows(ha_ref, wa_ref[...], ba_ref[...], 0, L)[0]
        cb_ = _conv3_rows(hb_ref, wb_ref[...], bb_ref[...], 0, L)[0]
        o_ref[...] = (jax.nn.silu(ca) * cb_).astype(o_ref.dtype)

    def h(off):
        return pl.BlockSpec((L, CT), lambda j: (0, j + off))

    def w(off):
        return pl.BlockSpec((3, CT), lambda j: (0, j + off))

    def b(off):
        return pl.BlockSpec((None, 1, CT), lambda j: (layer, 0, j + off))

    return pcall(body, plan, grid=(NCT,), in_specs=[h(0), h(NCT), w(0), w(NCT), b(0), b(NCT)],
                 out_specs=pl.BlockSpec((L, CT), lambda j: (0, j)), out_shape=S((L, DFF), BF),
                 sem=("parallel",), name=f"convact_fwd{layer}", args=[hu, hu, cw, cw, cb, cb])


def convact_bwd(hu, cw, cb, dact, layer, plan=None):
    def body(ha_ref, hb_ref, wa_ref, wb_ref, ba_ref, bb_ref, g_ref, dh_ref, dw_ref, db_ref, sh, sw, sb, da_scr, db_scr):
        j = pl.program_id(0)

        def fold(x):
            return functools.reduce(jnp.add, [x[8 * m:8 * m + 8] for m in range(CR // 8)])

        @pl.when(j < NCT)
        def _():
            wa, wb, ba, bb = wa_ref[...], wb_ref[...], ba_ref[...], bb_ref[...]
            da_scr[pl.ds(L, 8), :] = jnp.zeros((8, CT), f32)
            db_scr[pl.ds(L, 8), :] = jnp.zeros((8, CT), f32)
            acc = [jnp.zeros((8, CT), f32) for _ in range(8)]
            for c in range(L // CR):
                r0 = c * CR
                ca, a1, a2 = _conv3_rows(ha_ref, wa, ba, r0)
                cb_, b1, b2 = _conv3_rows(hb_ref, wb, bb, r0)
                g = g_ref[pl.ds(r0, CR), :].astype(f32)
                sg = jax.nn.sigmoid(ca)
                dca = g * cb_ * (sg * (1.0 + ca * (1.0 - sg)))
                dcb = g * (ca * sg)
                da_scr[pl.ds(r0, CR), :] = dca
                db_scr[pl.ds(r0, CR), :] = dcb
                terms = (dca * a2, dca * a1, dca * ha_ref[pl.ds(r0, CR), :], dca,
                         dcb * b2, dcb * b1, dcb * hb_ref[pl.ds(r0, CR), :], dcb)
                acc = [a + fold(t) for a, t in zip(acc, terms)]
            rows = [jnp.sum(a, axis=0, keepdims=True) for a in acc]
            for k in range(3):
                dw_ref[k:k + 1, :] = rows[k]
                sw[j, k:k + 1, :] = rows[4 + k]
            db_ref[...] = rows[3]
            sb[j] = rows[7]
            for c in range(L // CR):
                r0 = c * CR
                for scr, w, out in ((da_scr, wa, dh_ref), (db_scr, wb, sh.at[j])):
                    dh = (w[2:3] * scr[pl.ds(r0, CR), :] + w[1:2] * scr[pl.ds(r0 + 1, CR), :]
                          + w[0:1] * scr[pl.ds(r0 + 2, CR), :])
                    out[pl.ds(r0, CR), :] = dh.astype(out.dtype)

        @pl.when(j >= NCT)
        def _():
            dh_ref[...] = sh[j - NCT]
            dw_ref[...] = sw[j - NCT]
            db_ref[...] = sb[j - NCT]

    def lo(j):
        return jnp.minimum(j, NCT - 1)

    in_specs = [pl.BlockSpec((L, CT), lambda j: (0, lo(j))), pl.BlockSpec((L, CT), lambda j: (0, lo(j) + NCT)),
                pl.BlockSpec((3, CT), lambda j: (0, lo(j))), pl.BlockSpec((3, CT), lambda j: (0, lo(j) + NCT)),
                pl.BlockSpec((None, 1, CT), lambda j: (layer, 0, lo(j))), pl.BlockSpec((None, 1, CT), lambda j: (layer, 0, lo(j) + NCT)),
                pl.BlockSpec((L, CT), lambda j: (0, lo(j)))]
    return pcall(
        body, plan, grid=(2 * NCT,), in_specs=in_specs,
        out_specs=[pl.BlockSpec((L, CT), lambda j: (0, j)), pl.BlockSpec((3, CT), lambda j: (0, j)), pl.BlockSpec((1, CT), lambda j: (0, j))],
        out_shape=[S((L, 2 * DFF), BF), S((3, 2 * DFF), f32), S((1, 2 * DFF), f32)],
        scratch_shapes=[pltpu.VMEM((NCT, L, CT), BF), pltpu.VMEM((NCT, 3, CT), f32), pltpu.VMEM((NCT, 1, CT), f32),
                        pltpu.VMEM((L + 8, CT), f32), pltpu.VMEM((L + 8, CT), f32)],
        sem=("arbitrary",), name=f"convact_bwd{layer}", args=[hu, hu, cw, cw, cb, cb, dact])


DILS = (1, 4, 16)
AB = 128


def _rope_tables(pos_ref, invf_ref):
    ang = pos_ref[...].astype(f32) * invf_ref[...]
    lane = lax.broadcasted_iota(jnp.int32, (1, 128), 1) % 64
    cosf = jnp.where(lane < 16, jnp.cos(ang), 1.0)
    sn = jnp.sin(ang)
    s_lo = jnp.where(lane < 8, -sn, 0.0)
    s_hi = jnp.where((lane >= 8) & (lane < 16), sn, 0.0)
    return cosf, s_lo, s_hi


def _rope(t, cosf, s_lo, s_hi):
    return t * cosf + pltpu.roll(t, 120, 1) * s_lo + pltpu.roll(t, 8, 1) * s_hi


def _rope_t(g, cosf, s_lo, s_hi):
    return g * cosf + pltpu.roll(g * s_lo, 8, 1) + pltpu.roll(g * s_hi, 120, 1)


def _att_block(q2, kp, kc, vp, vc, first):
    lane = lax.broadcasted_iota(jnp.int32, (1, 128), 1)
    qi = lax.broadcasted_iota(jnp.int32, (AB, 2 * AB), 0) + AB
    kj = lax.broadcasted_iota(jnp.int32, (AB, 2 * AB), 1)
    back = qi - kj
    valid = (back >= 0) & (back <= AB)
    if first:
        valid = valid & (kj >= AB)
    kk = jnp.concatenate([kp, kc], axis=0)
    vv = jnp.concatenate([vp, vc], axis=0)
    o2 = jnp.zeros((AB, 128), f32)
    lse2 = jnp.zeros((AB, 128), f32)
    for e in range(2):
        hm = ((lane >= 64 * e) & (lane < 64 * (e + 1))).astype(f32)
        s = dot_nt(q2 * (hm * 0.125), kk)
        s = jnp.where(valid, s, -jnp.inf)
        m = jnp.max(s, axis=-1, keepdims=True)
        p = jnp.exp(s - m)
        den = jnp.sum(p, axis=-1, keepdims=True)
        o2 = o2 + dot_nn(p, vv * hm) / den
        lse2 = lse2 + (m + jnp.log(den)) * hm
    return o2, lse2


def _att_blocks(dil):
    m = L // dil
    return [(r * m + n * AB, n == 0) for r in range(dil) for n in range(m // AB)]


def deinterleave(x, dil):
    return x if dil == 1 else x.reshape(L // dil, dil, x.shape[1]).swapaxes(0, 1).reshape(L, x.shape[1])


def attn_fwd(qkv, pos, invf, g, plan=None):
    blocks = _att_blocks(DILS[g])

    def body(q_ref, k_ref, v_ref, pos_ref, invf_ref, o_ref, l_ref, qr, kr):
        cosf, s_lo, s_hi = _rope_tables(pos_ref, invf_ref)
        qr[...] = _rope(q_ref[...], cosf, s_lo, s_hi)
        kr[...] = _rope(k_ref[...], cosf, s_lo, s_hi)
        for off, first in blocks:
            cur, prv = pl.ds(off, AB), pl.ds(off if first else off - AB, AB)
            o2, lse2 = _att_block(qr[cur, :], kr[prv, :], kr[cur, :], v_ref[prv, :], v_ref[cur, :], first)
            o_ref[cur, :] = o2
            l_ref[cur, :] = lse2

    def sec(n):
        return pl.BlockSpec((L, 128), lambda p: (0, p + 4 * n))

    return pcall(
        body, plan, grid=(4,),
        in_specs=[sec(0), sec(1), sec(2), pl.BlockSpec((L, 1), lambda p: (0, 0)), pl.BlockSpec((1, 128), lambda p: (0, 0))],
        out_specs=[sec(0), sec(0)], out_shape=[S((L, 512), f32), S((L, 512), f32)],
        scratch_shapes=[pltpu.VMEM((L, 128), f32), pltpu.VMEM((L, 128), f32)],
        sem=("parallel",), name=f"attn_fwd{g}", args=[qkv, qkv, qkv, pos, invf])


def _att_block_bwd(q2, kp, kc, vp, vc, lse2, do2, dl2, first):
    lane = lax.broadcasted_iota(jnp.int32, (1, 128), 1)
    qi = lax.broadcasted_iota(jnp.int32, (AB, 2 * AB), 0) + AB
    kj = lax.broadcasted_iota(jnp.int32, (AB, 2 * AB), 1)
    back = qi - kj
    valid = (back >= 0) & (back <= AB)
    if first:
        valid = valid & (kj >= AB)
    kk = jnp.concatenate([kp, kc], axis=0)
    vv = jnp.concatenate([vp, vc], axis=0)
    dq2 = jnp.zeros((AB, 128), f32)
    dkk = jnp.zeros((2 * AB, 128), f32)
    dvv = jnp.zeros((2 * AB, 128), f32)
    for e in range(2):
        hb = (lane >= 64 * e) & (lane < 64 * (e + 1))
        hm = hb.astype(f32)
        qs = q2 * (hm * 0.125)
        lse = jnp.max(jnp.where(hb, lse2, -jnp.inf), axis=-1, keepdims=True)
        dls = jnp.sum(dl2 * hm, axis=-1, keepdims=True)
        p = jnp.where(valid, jnp.exp(dot_nt(qs, kk) - lse), 0.0)
        dov = do2 * hm
        dp = dot_nt(dov, vv)
        ds = p * (dp - jnp.sum(p * dp, axis=-1, keepdims=True) + dls)
        dq2 = dq2 + dot_nn(ds, kk) * (hm * 0.125)
        dkk = dkk + dot_tn(ds, qs)
        dvv = dvv + dot_tn(p, dov)
    return dq2, dkk[:AB], dkk[AB:], dvv[:AB], dvv[AB:]


def attn_bwd(qkv, pos, invf, lse, do, dl, g, plan=None):
    blocks = _att_blocks(DILS[g])

    def body(q_ref, k_ref, v_ref, pos_ref, invf_ref, l_ref, do_ref, dl_ref, d_ref, qr, kr, dqr, dkr, dvr):
        cosf, s_lo, s_hi = _rope_tables(pos_ref, invf_ref)
        qr[...] = _rope(q_ref[...], cosf, s_lo, s_hi)
        kr[...] = _rope(k_ref[...], cosf, s_lo, s_hi)
        for off, first in blocks:
            cur, prv = pl.ds(off, AB), pl.ds(off if first else off - AB, AB)
            dq2, dkp, dkc, dvp, dvc = _att_block_bwd(qr[cur, :], kr[prv, :], kr[cur, :], v_ref[prv, :], v_ref[cur, :],
                                                     l_ref[cur, :], do_ref[cur, :], dl_ref[cur, :], first)
            dqr[cur, :] = dq2
            dkr[cur, :] = dkc
            dvr[cur, :] = dvc
            if not first:
                dkr[prv, :] += dkp
                dvr[prv, :] += dvp
        d_ref[0] = _rope_t(dqr[...], cosf, s_lo, s_hi).astype(d_ref.dtype)
        d_ref[1] = _rope_t(dkr[...], cosf, s_lo, s_hi).astype(d_ref.dtype)
        d_ref[2] = dvr[...].astype(d_ref.dtype)

    def sec(n):
        return pl.BlockSpec((L, 128), lambda p: (0, p + 4 * n))

    return pcall(
        body, plan, grid=(4,),
        in_specs=[sec(0), sec(1), sec(2), pl.BlockSpec((L, 1), lambda p: (0, 0)), pl.BlockSpec((1, 128), lambda p: (0, 0)),
                  sec(0), sec(0), sec(0)],
        out_specs=pl.BlockSpec((3, L, 128), lambda p: (0, 0, p)), out_shape=S((3, L, 512), BF),
        scratch_shapes=[pltpu.VMEM((L, 128), f32)] * 5,
        sem=("parallel",), name=f"attn_bwd{g}", args=[qkv, qkv, qkv, pos, invf, lse, do, dl])


def _merge(o0, o1, o2, l0, l1, l2):
    m = jnp.maximum(jnp.maximum(l0, l1), l2)
    e0, e1, e2 = jnp.exp(l0 - m), jnp.exp(l1 - m), jnp.exp(l2 - m)
    return (e0 * o0 + e1 * o1 + e2 * o2) / (e0 + e1 + e2)


def _to_token_major(src_ref, scr, i, dil, slab):
    n = TR // dil
    for r in range(dil):
        rows = pl.ds(pl.multiple_of(r * (L // dil) + i * n, n), n)
        scr[pl.ds(r, n, stride=dil), :] = src_ref[rows, slab * 128:(slab + 1) * 128].astype(f32)
    return scr[...]


def _to_class_major(val, dst_ref, scr, i, dil, slab):
    n = TR // dil
    scr[...] = val
    for r in range(dil):
        rows = pl.ds(pl.multiple_of(r * (L // dil) + i * n, n), n)
        dst_ref[rows, slab * 128:(slab + 1) * 128] = scr[pl.ds(r, n, stride=dil), :].astype(dst_ref.dtype)


def rms_fwd_classes(x, g, name):
    def body(x_ref, g_ref, o_ref, o1_ref, o2_ref, scr):
        i = pl.program_id(0)
        y = _rms(x_ref[...], g_ref[...])
        o_ref[...] = y.astype(o_ref.dtype)
        for s in range(D // 128):
            ys = y[:, s * 128:(s + 1) * 128]
            _to_class_major(ys, o1_ref, scr, i, DILS[1], s)
            _to_class_major(ys, o2_ref, scr, i, DILS[2], s)

    row = pl.BlockSpec((TR, D), lambda i: (i, 0))
    full = pl.BlockSpec((L, D), lambda i: (0, 0))
    return pl.pallas_call(
        body, grid=(L // TR,), in_specs=[row, pl.BlockSpec((1, D), lambda i: (0, 0))], out_specs=[row, full, full],
        out_shape=[S((L, D), BF)] * 3, scratch_shapes=[pltpu.VMEM((TR, 128), f32)],
        compiler_params=_cp(("arbitrary",)), name=name)(x, g)


def rms_bwd_classes(x, g, dy0, dyc, dres, name, plan=None):
    def body(x_ref, g_ref, dy0_ref, d1_ref, d2_ref, dr_ref, dh_ref, dg_ref, scr, dyf):
        i = pl.program_id(0)
        for s in range(D // 128):
            sl = slice(s * 128, (s + 1) * 128)
            dyf[:, sl] = (dy0_ref[:, sl] + _to_token_major(d1_ref, scr.at[0], i, DILS[1], s)
                          + _to_token_major(d2_ref, scr.at[1], i, DILS[2], s))
        _, vjp = jax.vjp(_rms, x_ref[...], g_ref[...])
        dx, dg = vjp(dyf[...])
        dh_ref[...] = dr_ref[...] + dx

        @pl.when(i == 0)
        def _():
            dg_ref[...] = jnp.zeros_like(dg_ref)

        dg_ref[...] += dg

    row = pl.BlockSpec((TR, D), lambda i: (i, 0))
    vec = pl.BlockSpec((1, D), lambda i: (0, 0))
    full = pl.BlockSpec((L, D), lambda i: (0, 0))
    return pcall(body, plan, grid=(L // TR,), in_specs=[row, vec, row, full, full, row], out_specs=[row, vec],
                 out_shape=[S((L, D), f32), S((1, D), f32)],
                 scratch_shapes=[pltpu.VMEM((2, TR, 128), f32), pltpu.VMEM((TR, D), f32)],
                 sem=("arbitrary",), name=name, args=[x, g, dy0, dyc[0], dyc[1], dres])


def attn_merge_fwd(o0, l0, oc, lc, plan=None):
    def body(o0_ref, l0_ref, o1_ref, l1_ref, o2_ref, l2_ref, o_ref, scr):
        i = pl.program_id(0)
        for s in range(4):
            sl = slice(s * 128, (s + 1) * 128)
            o1 = _to_token_major(o1_ref, scr.at[0], i, DILS[1], s)
            l1 = _to_token_major(l1_ref, scr.at[1], i, DILS[1], s)
            o2 = _to_token_major(o2_ref, scr.at[2], i, DILS[2], s)
            l2 = _to_token_major(l2_ref, scr.at[3], i, DILS[2], s)
            o_ref[:, sl] = _merge(o0_ref[:, sl], o1, o2, l0_ref[:, sl], l1, l2).astype(o_ref.dtype)

    blk = pl.BlockSpec((TR, 512), lambda i: (i, 0))
    full = pl.BlockSpec((L, 512), lambda i: (0, 0))
    return pcall(body, plan, grid=(L // TR,), in_specs=[blk, blk, full, full, full, full], out_specs=blk,
                 out_shape=S((L, 512), BF), scratch_shapes=[pltpu.VMEM((4, TR, 128), f32)],
                 sem=("arbitrary",), name="attn_merge_fwd", args=[o0, l0, oc[0], lc[0], oc[1], lc[1]])


def attn_merge_bwd(o0, l0, oc, lc, do, plan=None):
    def body(o0_ref, l0_ref, o1_ref, l1_ref, o2_ref, l2_ref, g_ref, do0, dl0, do1, dl1, do2, dl2, scr):
        i = pl.program_id(0)
        for s in range(4):
            sl = slice(s * 128, (s + 1) * 128)
            o1 = _to_token_major(o1_ref, scr.at[0], i, DILS[1], s)
            l1 = _to_token_major(l1_ref, scr.at[1], i, DILS[1], s)
            o2 = _to_token_major(o2_ref, scr.at[2], i, DILS[2], s)
            l2 = _to_token_major(l2_ref, scr.at[3], i, DILS[2], s)
            _, vjp = jax.vjp(_merge, o0_ref[:, sl], o1, o2, l0_ref[:, sl], l1, l2)
            g0, g1, g2, h0, h1, h2 = vjp(g_ref[:, sl].astype(f32))
            do0[:, sl] = g0.astype(do0.dtype)
            dl0[:, sl] = h0
            _to_class_major(g1, do1, scr.at[0], i, DILS[1], s)
            _to_class_major(h1, dl1, scr.at[1], i, DILS[1], s)
            _to_class_major(g2, do2, scr.at[2], i, DILS[2], s)
            _to_class_major(h2, dl2, scr.at[3], i, DILS[2], s)

    blk = pl.BlockSpec((TR, 512), lambda i: (i, 0))
    full = pl.BlockSpec((L, 512), lambda i: (0, 0))
    outs = pcall(body, plan, grid=(L // TR,), in_specs=[blk, blk, full, full, full, full, blk],
                 out_specs=[blk, blk, full, full, full, full],
                 out_shape=[S((L, 512), BF), S((L, 512), f32)] * 3, scratch_shapes=[pltpu.VMEM((4, TR, 128), f32)],
                 sem=("arbitrary",), name="attn_merge_bwd", args=[o0, l0, oc[0], lc[0], oc[1], lc[1], do])
    return [outs[0], outs[2], outs[4]], [outs[1], outs[3], outs[5]]


def _invf_lanes():
    half = 8
    inv = ROPE_THETA ** (-np.arange(half, dtype=np.float32) * 2.0 / 16.0)
    lane = np.arange(128) % 64
    return jnp.asarray(np.where(lane < 16, inv[lane % 8], 0.0).astype(np.float32)[None, :])


def hosted(C, host, fn):
    p = C.plan(host) if C is not None else None
    out = fn(p)
    if p is not None:
        C.done(p)
    return out


def _ffn_fwd(h, g_row, W, cb, layer, C):
    hn = rms_fwd(h, g_row, f"rms_ffn{layer}")
    hu = hosted(C, f"ffn_in{layer}", lambda p: matmul(hn, W[("ffn_w_in", layer)], mode="nn", tm=1024, tn=1408, tk=1024,
                                                      plan=p, name=f"ffn_in{layer}"))
    act = hosted(C, f"convact_fwd{layer}", lambda p: convact_fwd(hu, W[("ffn_conv_w", layer)], cb, layer, plan=p))
    h2 = hosted(C, f"ffn_out{layer}", lambda p: matmul(act, W[("ffn_w_out", layer)], mode="nn", tm=1024, tn=1024, tk=2816,
                                                       add=h, plan=p, name=f"ffn_out{layer}"))
    return h2, (hn, hu, act)


def _ffn_bwd(dh, h, g_row, W, cb, saved, layer, C, G):
    hn, hu, act = saved
    w_in, w_out = W[("ffn_w_in", layer)], W[("ffn_w_out", layer)]
    dact = hosted(C, f"ffn_out_dx{layer}", lambda p: matmul(dh, w_out, mode="nt", tm=1024, tn=1408, tk=1024, out_dtype=BF,
                                                          plan=p, name=f"ffn_out_dx{layer}"))
    G[("ffn_w_out", layer)] = hosted(C, f"ffn_out_dw{layer}", lambda p: matmul(
        act, dh, mode="tn", tm=1408, tn=1024, tk=L, out_dtype=BF, plan=p, name=f"ffn_out_dw{layer}"))
    dhu, G[("ffn_conv_w", layer)], g_cb = hosted(
        C, f"convact_bwd{layer}", lambda p: convact_bwd(hu, W[("ffn_conv_w", layer)], cb, dact, layer, plan=p))
    dhn = hosted(C, f"ffn_in_dx{layer}", lambda p: matmul(dhu, w_in, mode="nt", tm=1024, tn=1024, tk=2816, plan=p,
                                                         name=f"ffn_in_dx{layer}"))
    G[("ffn_w_in", layer)] = hosted(C, f"ffn_in_dw{layer}", lambda p: matmul(
        hn, dhu, mode="tn", tm=1024, tn=1408, tk=L, out_dtype=BF, plan=p, name=f"ffn_in_dw{layer}"))
    dh2, g_norm = hosted(C, f"rms_ffn_bwd{layer}", lambda p: rms_bwd(h, g_row, [dhn], dh, f"rms_ffn_bwd{layer}", plan=p))
    return dh2, g_cb, g_norm


def local_step(x, pos, tgt, sm, W, C=None):
    G = C.grads if C is not None else {}
    nm, nf = sm["norm_mix"], sm["norm_ffn"]
    invf = _invf_lanes()
    are = sm["s5_A_re"].reshape(NST, 1)
    aim = sm["s5_A_im"].reshape(NST, 1)
    ldt = sm["s5_log_dt"].reshape(1, 32)
    bre = sm["s5_B_re"].reshape(NST, 16)
    bim = sm["s5_B_im"].reshape(NST, 16)
    cre = jnp.swapaxes(sm["s5_C_re"][0], 1, 2).reshape(NST, 16)
    cim = jnp.swapaxes(sm["s5_C_im"][0], 1, 2).reshape(NST, 16)
    drow = sm["s5_D"].reshape(1, S5W)
    wbr, wbi, wcr, wci, abr, abi = s5_params_fwd(are, aim, ldt, bre, bim, cre, cim)
    hn0 = rms_fwd(x, nm[0:1], "rms_mix0")
    cb3 = sm["ffn_conv_b3"]
    proj = hosted(C, "mix_in", lambda p: matmul(hn0, W[("mix_w_in", 0)], mode="nn", tm=1024, tn=1280, tk=1024, plan=p, name="mix_in"))
    xs_re, xs_im, y5 = hosted(C, "s5_scan_fwd", lambda p: s5_scan_fwd(proj, wbr, wbi, wcr, wci, abr, abi, drow, plan=p))
    oa = s5_glu_fwd(y5, W[("s5_glu_w", 0)], sm["s5_glu_b"])
    ob, ssave = hosted(C, "hgrn_fwd", lambda p: hgrn_fwd(proj, sm["hgrn_gamma"], sm["hgrn_norm"], plan=p))
    cat = jnp.concatenate([oa, ob], axis=1)
    h1 = matmul(cat, W[("mix_w_out", 0)], mode="nn", tm=1024, tn=1024, tk=1024, add=x, name="mix_out")
    h2, ffn0 = _ffn_fwd(h1, nf[0:1], W, cb3, 0, C)
    hn2_g = rms_fwd_classes(h2, nm[1:2], "rms_mix1")
    wqkv = W[("att_w_qkv", 0)]
    pos_g, qkv_g, oc_g, lc_g = [], [], [], []
    for g, dil in enumerate(DILS):
        pos_g.append(deinterleave(pos, dil))
        qkv_g.append(hosted(C, f"att_qkv{g}", lambda p: matmul(
            hn2_g[g], wqkv, mode="nn", tm=1024, tn=512, tk=1024, dims=(L, 1536, D),
            b_spec=pl.BlockSpec((D, 512), lambda i, j, k, g=g: (0, 3 * j + g)), plan=p, name=f"att_qkv{g}")))
        o_c, l_c = hosted(C, f"attn_fwd{g}", lambda p: attn_fwd(qkv_g[g], pos_g[g], invf, g, plan=p))
        oc_g.append(o_c)
        lc_g.append(l_c)
    o = hosted(C, "attn_merge_fwd", lambda p: attn_merge_fwd(oc_g[0], lc_g[0], oc_g[1:], lc_g[1:], plan=p))
    h3 = matmul(o, W[("att_w_o", 0)], mode="nn", tm=1024, tn=1024, tk=512, add=h2, name="att_o")
    h4, ffn1 = _ffn_fwd(h3, nf[1:2], W, cb3, 1, C)
    loss, dh, g_nfinal = loss_head(h4, sm["norm_final"].reshape(1, D), tgt)
    dh, g_cb1, g_nf1 = _ffn_bwd(dh, h3, nf[1:2], W, cb3, ffn1, 1, C, G)
    do = matmul(dh, W[("att_w_o", 0)], mode="nt", tm=1024, tn=512, tk=1024, name="att_o_dx")
    G[("att_w_o", 0)] = matmul(o, dh, mode="tn", tm=512, tn=1024, tk=L, out_dtype=BF, name="att_o_dw")
    do_g, dl_g = hosted(C, "attn_merge_bwd", lambda p: attn_merge_bwd(oc_g[0], lc_g[0], oc_g[1:], lc_g[1:], do, plan=p))
    dhn2_g, gq = [], []
    for g, dil in enumerate(DILS):
        d3 = hosted(C, f"attn_bwd{g}", lambda p: attn_bwd(qkv_g[g], pos_g[g], invf, lc_g[g], do_g[g], dl_g[g], g, plan=p))
        dx = matmul(d3, wqkv, mode="nt", tm=1024, tn=1024, tk=512, dims=(L, D, 1536),
                    a_spec=pl.BlockSpec((None, 1024, 512), lambda i, j, k: (k, i, 0)),
                    b_spec=pl.BlockSpec((D, 512), lambda i, j, k, g=g: (0, 3 * k + g)), name=f"att_qkv_dx{g}")
        dhn2_g.append(dx)
        gq.append(matmul(hn2_g[g], d3, mode="tn", tm=1024, tn=512, tk=L, out_dtype=BF, dims=(D, 1536, L),
                         b_spec=pl.BlockSpec((None, L, 512), lambda i, j, k: (j, k, 0)), name=f"att_qkv_dw{g}"))
    G[("att_w_qkv", 0)] = jnp.concatenate([gq[g][:, 512 * s:512 * (s + 1)] for s in range(3) for g in range(3)], axis=1)
    dh, g_nm1 = hosted(C, "rms_mix_bwd1", lambda p: rms_bwd_classes(h2, nm[1:2], dhn2_g[0], dhn2_g[1:], dh, "rms_mix_bwd1", plan=p))
    dh, g_cb0, g_nf0 = _ffn_bwd(dh, h1, nf[0:1], W, cb3, ffn0, 0, C, G)
    dmix = matmul(dh, W[("mix_w_out", 0)], mode="nt", tm=1024, tn=1024, tk=1024, name="mix_out_dx")
    G[("mix_w_out", 0)] = matmul(cat, dh, mode="tn", tm=1024, tn=1024, tk=L, out_dtype=BF, name="mix_out_dw")
    dy5, g_glu_w, g_glu_b = s5_glu_bwd(y5, W[("s5_glu_w", 0)], sm["s5_glu_b"], dmix)
    G[("s5_glu_w", 0)] = g_glu_w.astype(BF)
    du, gwbr, gwbi, gwcr, gwci, gabr, gabi, g_d = hosted(C, "s5_scan_bwd", lambda p: s5_scan_bwd(
        dy5, proj, xs_re, xs_im, wbr, wbi, wcr, wci, abr, abi, drow, plan=p))
    g_are, g_aim, g_ldt, g_bre, g_bim, g_cre, g_cim = s5_params_bwd(are, aim, ldt, bre, bim, cre, cim,
                                                                   (gwbr, gwbi, gwcr, gwci, gabr, gabi))
    small = {
        "norm_ffn": jnp.concatenate([g_nf0, g_nf1], axis=0), "norm_final": g_nfinal.reshape(D),
        "s5_A_re": g_are.reshape(1, 32, 64), "s5_A_im": g_aim.reshape(1, 32, 64), "s5_log_dt": g_ldt.reshape(1, 32),
        "s5_B_re": g_bre.reshape(1, 32, 64, 16), "s5_B_im": g_bim.reshape(1, 32, 64, 16),
        "s5_C_re": jnp.swapaxes(g_cre.reshape(1, 32, 64, 16), 2, 3), "s5_C_im": jnp.swapaxes(g_cim.reshape(1, 32, 64, 16), 2, 3),
        "s5_D": g_d.reshape(1, 32, 16), "s5_glu_b": g_glu_b, "ffn_conv_b": jnp.concatenate([g_cb0, g_cb1], axis=0),
    }
    if C is not None:
        C.small["small_early"] = _pack(small, SMALL_EARLY)
    dproj, g_gamma, g_hnorm = hosted(C, "hgrn_bwd", lambda p: hgrn_bwd(proj, sm["hgrn_gamma"], sm["hgrn_norm"], ssave, dmix, du,
                                                                       plan=p))
    dhn0 = hosted(C, "mix_in_dx", lambda p: matmul(dproj, W[("mix_w_in", 0)], mode="nt", tm=1024, tn=1024, tk=2560, plan=p,
                                                  name="mix_in_dx"))
    G[("mix_w_in", 0)] = matmul(hn0, dproj, mode="tn", tm=1024, tn=1280, tk=L, out_dtype=BF, name="mix_in_dw")
    gx, g_nm0 = hosted(C, "rms_mix_bwd0", lambda p: rms_bwd(x, nm[0:1], [dhn0], dh, "rms_mix_bwd0", plan=p))
    small.update({"norm_mix": jnp.concatenate([g_nm0, g_nm1], axis=0), "hgrn_gamma": g_gamma, "hgrn_norm": g_hnorm})
    if C is not None:
        C.small["small_late"] = _pack(small, SMALL_LATE)
    return loss, gx, G, small


BIG = ("mix_w_in", "mix_w_out", "s5_glu_w", "att_w_qkv", "att_w_o", "ffn_w_in", "ffn_w_out", "ffn_conv_w")
SMALL = ("norm_mix", "norm_ffn", "norm_final", "s5_A_re", "s5_A_im", "s5_log_dt", "s5_B_re", "s5_B_im", "s5_C_re", "s5_C_im",
         "s5_D", "s5_glu_b", "hgrn_gamma", "hgrn_norm", "ffn_conv_b")
SMALL_LATE = ("norm_mix", "hgrn_gamma", "hgrn_norm")
SMALL_EARLY = tuple(n for n in SMALL if n not in SMALL_LATE)


def cast_bf16(w, name, plan=None):
    nl, r, c = w.shape
    w2 = w.reshape(nl * r, c)
    tr = 256 if (nl * r) % 256 == 0 else nl * r

    def body(w_ref, o_ref):
        o_ref[...] = w_ref[...].astype(BF)

    out = pcall(body, plan, grid=(nl * r // tr,), in_specs=[pl.BlockSpec((tr, c), lambda i: (i, 0))],
                out_specs=pl.BlockSpec((tr, c), lambda i: (i, 0)), out_shape=S((nl * r, c), BF),
                sem=("parallel",), name=name, args=[w2])
    return out.reshape(nl, r, c)


DIRECT = ("ffn_conv_w", "att_w_o", "s5_glu_w", "mix_w_out")

SCHEDULE = {
    "cast_ffn_w_in": [("G", "mix_w_in", 0)],
    "mix_in": [("G", "mix_w_out", 0), ("G", "s5_glu_w", 0)],
    "s5_scan_fwd": [("G", "ffn_w_in", 0, (0, 2))],
    "hgrn_fwd": [("G", "ffn_w_in", 0, (1, 2)), ("G", "ffn_conv_w", 0), ("G", "ffn_conv_w", 1), ("G", "att_w_qkv", 0, (0, 2))],
    "ffn_in0": [("G", "ffn_w_out", 0)],
    "convact_fwd0": [("G", "att_w_qkv", 0, (1, 2))],
    "att_qkv0": [("G", "att_w_o", 0)],
    "attn_fwd0": [("G", "ffn_w_in", 1, (0, 2))],
    "attn_fwd1": [("G", "ffn_w_in", 1, (1, 2))],
    "attn_fwd2": [("G", "ffn_w_out", 1)],
    "convact_bwd1": [("P", "ffn_w_out", 1)],
    "ffn_in_dx1": [("A", "ffn_w_out", 1, (0, 2))],
    "ffn_in_dw1": [("A", "ffn_w_out", 1, (1, 2))],
    "rms_ffn_bwd1": [("P", "ffn_w_in", 1)],
    "attn_merge_bwd": [("A", "ffn_conv_w", 1), ("B", "ffn_w_out", 1)],
    "attn_bwd0": [("A", "ffn_w_in", 1, (0, 2)), ("A", "att_w_o", 0)],
    "attn_bwd1": [("A", "ffn_w_in", 1, (1, 2)), ("B", "att_w_o", 0), ("B", "ffn_conv_w", 1)],
    "attn_bwd2": [("B", "ffn_w_in", 1)],
    "rms_mix_bwd1": [("P", "att_w_qkv", 0)],
    "ffn_out_dx0": [("A", "att_w_qkv", 0, (0, 4))],
    "ffn_out_dw0": [("A", "att_w_qkv", 0, (1, 4))],
    "convact_bwd0": [("A", "att_w_qkv", 0, (2, 4)), ("A", "att_w_qkv", 0, (3, 4)), ("P", "ffn_w_out", 0)],
    "ffn_in_dx0": [("A", "ffn_w_out", 0, (0, 2)), ("B", "att_w_qkv", 0)],
    "ffn_in_dw0": [("A", "ffn_w_out", 0, (1, 2))],
    "rms_ffn_bwd0": [("P", "ffn_w_in", 0), ("B", "ffn_w_out", 0)],
    "s5_scan_bwd": [("A", "ffn_w_in", 0, (0, 2)), ("A", "ffn_conv_w", 0)],
    "hgrn_bwd": [("A", "ffn_w_in", 0, (1, 2)), ("A", "mix_w_out", 0), ("A", "s5_glu_w", 0), ("B", "ffn_conv_w", 0),
                 ("A", "small_early", 0)],
    "mix_in_dx": [("B", "ffn_w_in", 0), ("B", "mix_w_out", 0), ("B", "s5_glu_w", 0), ("B", "small_early", 0)],
    "rms_mix_bwd0": [("P", "mix_w_in", 0)],
    "adam_att_w_o": [("A", "mix_w_in", 0), ("A", "small_late", 0)],
    "adam_s5_glu_w": [("B", "mix_w_in", 0), ("B", "small_late", 0)],
}


class Comm:
    def __init__(self, shards, shapes):
        self.shards, self.shapes = shards, shapes
        self.W, self.grads, self.slots = {}, {}, {}
        self.sib, self.pair = {}, {}
        self.small = {}

    def plan(self, host):
        items = SCHEDULE.get(host)
        if not items:
            return None
        p = Plan()
        for it in items:
            kind, name, l = it[:3]
            part, parts = it[3] if len(it) > 3 else (0, 1)
            if name.startswith("small"):
                sg = self.small[name]
                kdst = p.buf("slots:" + name, arr=self.slots.get(name), shape=S((8,) + sg.shape, f32), write=True)
                if kind == "A":
                    ReduceOp(p, p.buf("g:" + name, arr=sg), kdst, None, sg.shape, False, 0, 0, whole=True)
                else:
                    ForwardOp(p, kdst, None, whole=True)
                continue
            nl, R, C_ = self.shapes[name]
            rows = name in ROW_SHARDED
            r0, nr = part * (R // parts), R // parts
            if kind == "G":
                sh = self.shards[name]
                kdst = p.buf(f"W:{name}:{l}", arr=self.W.get((name, l)), shape=S((4 * R, C_) if rows else (R, 4 * C_), sh.dtype),
                             write=True)
                GatherOp(p, p.buf("shard:" + name, arr=sh), kdst, l, self.shapes[name], rows, r0, nr, split=(nr % 32 == 0))
            elif name in DIRECT:
                g = self.grads[(name, l)]
                kdst = p.buf("slots:" + name, arr=self.slots.get(name), shape=S((8, nl, R, C_), g.dtype), write=True)
                if kind == "A":
                    ReduceOp(p, p.buf(f"g:{name}:{l}", arr=g), kdst, l, self.shapes[name], rows, r0, nr)
                else:
                    ForwardOp(p, kdst, l)
            elif kind == "P":
                g = self.grads[(name, l)]
                ksib = p.buf(f"sib:{name}:{l}", shape=S((4 * R // 2, C_) if rows else (R // 2, 4 * C_), g.dtype), write=True)
                PairOp(p, p.buf(f"g:{name}:{l}", arr=g), ksib, self.shapes[name], rows)
            else:
                if (name, l) not in self.pair:
                    self.pair[(name, l)] = pair_sum(self.grads[(name, l)], self.sib[(name, l)], rows, R, f"pair_sum_{name}{l}")
                h = self.pair[(name, l)]
                kdst = p.buf("slots:" + name, arr=self.slots.get(name), shape=S((4, nl, R, C_), h.dtype), write=True)
                if kind == "A":
                    ReduceOp(p, p.buf(f"h:{name}:{l}", arr=h), kdst, l, self.shapes[name], rows, r0 // 2, nr // 2, half=True)
                else:
                    HalfForwardOp(p, kdst, l, self.shapes[name])
        return p

    def done(self, p):
        for k, arr in p.out.items():
            tag, name = k.split(":")[:2]
            if tag == "W":
                self.W[(name, int(k.split(":")[2]))] = arr
            elif tag == "sib":
                self.sib[(name, int(k.split(":")[2]))] = arr
            else:
                self.slots[name] = arr


def _adamw(w, g, m, v):
    m = B1 * m + (1.0 - B1) * g
    v = B2 * v + (1.0 - B2) * jnp.square(g)
    m_hat = m / (1.0 - B1 ** STEP)
    v_hat = v / (1.0 - B2 ** STEP)
    return -LR * (m_hat / (jnp.sqrt(v_hat) + AEPS) + WD * w), m, v


def adam_big(w, m, v, slots, name, plan=None):
    nl, R, C = w.shape
    ns = slots.shape[0]
    tr = 128 if R % 128 == 0 else (64 if R % 64 == 0 else R)

    def body(w_ref, m_ref, v_ref, s_ref, g_ref, d_ref, nm_ref, nv_ref):
        g = s_ref[0].astype(f32)
        for s in range(1, ns):
            g = g + s_ref[s].astype(f32)
        d, nm_, nv_ = _adamw(w_ref[...], g, m_ref[...], v_ref[...])
        g_ref[...] = g
        d_ref[...] = d
        nm_ref[...] = nm_
        nv_ref[...] = nv_

    blk = pl.BlockSpec((None, tr, C), lambda l, i: (l, i, 0))
    return pcall(body, plan, grid=(nl, R // tr),
                 in_specs=[blk, blk, blk, pl.BlockSpec((ns, None, tr, C), lambda l, i: (0, l, i, 0))],
                 out_specs=[blk] * 4, out_shape=[S((nl, R, C), f32)] * 4,
                 sem=("parallel", "parallel"), name=name, args=[w, m, v, slots])


def sum_slots(slots, name):
    R = slots.shape[1]

    def body(s_ref, g_ref):
        g = s_ref[0]
        for s in range(1, 8):
            g = g + s_ref[s]
        g_ref[...] = g

    return pl.pallas_call(
        body, grid=(R // 256,), in_specs=[pl.BlockSpec((8, 256, 128), lambda i: (0, i, 0))],
        out_specs=pl.BlockSpec((256, 128), lambda i: (i, 0)), out_shape=S((R, 128), f32),
        compiler_params=_cp(("parallel",)), name=name)(slots)


SMALL2D = {"norm_mix": (2, 1024), "norm_ffn": (2, 1024), "norm_final": (1, 1024), "s5_A_re": (32, 64), "s5_A_im": (32, 64),
           "s5_log_dt": (1, 32), "s5_B_re": (2048, 16), "s5_B_im": (2048, 16), "s5_C_re": (512, 64), "s5_C_im": (512, 64),
           "s5_D": (32, 16), "s5_glu_b": (1, 512), "hgrn_gamma": (2, 512), "hgrn_norm": (1, 512), "ffn_conv_b": (2, 5632)}


def adam_small(w, m, v, g, names, name):
    n = len(names)

    def body(*refs):
        for i in range(n):
            w_ref, m_ref, v_ref, g_ref = refs[4 * i:4 * i + 4]
            d_ref, nm_ref, nv_ref = refs[4 * n + 3 * i:4 * n + 3 * i + 3]
            d, nm_, nv_ = _adamw(w_ref[...], g_ref[...], m_ref[...], v_ref[...])
            d_ref[...] = d
            nm_ref[...] = nm_
            nv_ref[...] = nv_

    args = [t[k] for k in names for t in (w, m, v, g)]
    outs = pl.pallas_call(body, out_shape=[S(SMALL2D[k], f32) for k in names for _ in range(3)],
                          compiler_params=_cp(), name=name)(*args)
    return {k: tuple(outs[3 * i:3 * i + 3]) for i, k in enumerate(names)}


def _pack(d, names):
    flat = jnp.concatenate([d[n].reshape(-1) for n in names])
    n = flat.shape[0]
    rows = -(-n // (256 * 128)) * 256
    return jnp.pad(flat, (0, rows * 128 - n)).reshape(rows, 128)


def _unpack(p, like, names):
    flat = p.reshape(-1)
    out, off = {}, 0
    for n in names:
        sz = math.prod(like[n].shape)
        out[n] = flat[off:off + sz].reshape(like[n].shape)
        off += sz
    return out


def kernel(x, positions, norm_mix, norm_ffn, norm_final, mix_w_in, mix_w_out, s5_A_re, s5_A_im, s5_log_dt, s5_B_re, s5_B_im, s5_C_re, s5_C_im, s5_D, s5_glu_w, s5_glu_b, hgrn_gamma, hgrn_norm, att_w_qkv, att_w_o, ffn_w_in, ffn_conv_w, ffn_conv_b, ffn_w_out, loss_target, m_norm_mix, m_norm_ffn, m_norm_final, m_mix_w_in, m_mix_w_out, m_s5_A_re, m_s5_A_im, m_s5_log_dt, m_s5_B_re, m_s5_B_im, m_s5_C_re, m_s5_C_im, m_s5_D, m_s5_glu_w, m_s5_glu_b, m_hgrn_gamma, m_hgrn_norm, m_att_w_qkv, m_att_w_o, m_ffn_w_in, m_ffn_conv_w, m_ffn_conv_b, m_ffn_w_out, v_norm_mix, v_norm_ffn, v_norm_final, v_mix_w_in, v_mix_w_out, v_s5_A_re, v_s5_A_im, v_s5_log_dt, v_s5_B_re, v_s5_B_im, v_s5_C_re, v_s5_C_im, v_s5_D, v_s5_glu_w, v_s5_glu_b, v_hgrn_gamma, v_hgrn_norm, v_att_w_qkv, v_att_w_o, v_ffn_w_in, v_ffn_conv_w, v_ffn_conv_b, v_ffn_w_out):
    a = dict(locals())
    weights = BIG + SMALL
    w = {n: a[n] for n in weights}
    m = {n: a["m_" + n] for n in weights}
    v = {n: a["v_" + n] for n in weights}
    shards = {"ffn_conv_w": ffn_conv_w}
    C = Comm(shards, {n: w[n].shape for n in BIG})
    for n in ("mix_w_in", "ffn_w_in", "mix_w_out", "s5_glu_w", "ffn_w_out", "att_w_qkv", "att_w_o"):
        shards[n] = hosted(C, "cast_" + n, lambda p: cast_bf16(w[n], "cast_" + n, plan=p))
    sm = {n: w[n] for n in SMALL}
    sm["ffn_conv_b3"] = ffn_conv_b.reshape(2, 1, 2 * DFF)
    loss, gx, _, _ = local_step(x[0], positions.reshape(L, 1), loss_target[0], sm, C.W, C)
    res = {}
    for n in ("att_w_o", "s5_glu_w", "ffn_w_in", "ffn_w_out", "att_w_qkv", "mix_w_out", "ffn_conv_w", "mix_w_in"):
        res[n] = hosted(C, "adam_" + n, lambda p: adam_big(w[n], m[n], v[n], C.slots[n], "adam_" + n, plan=p))
    for names, key in ((SMALL_EARLY, "small_early"), (SMALL_LATE, "small_late")):
        g = _unpack(sum_slots(C.slots[key], "sum_" + key), w, names)

        def two_d(t):
            return {n: t[n].reshape(SMALL2D[n]) for n in names}

        upd = adam_small(two_d(w), two_d(m), two_d(v), two_d(g), names, "adam_" + key)
        for n in names:
            res[n] = (g[n],) + tuple(t.reshape(w[n].shape) for t in upd[n])
    total = lax.psum(loss[0, 0], ("x", "y", "c"))
    order = ("norm_mix", "norm_ffn", "norm_final", "mix_w_in", "mix_w_out", "s5_A_re", "s5_A_im", "s5_log_dt", "s5_B_re", "s5_B_im",
             "s5_C_re", "s5_C_im", "s5_D", "s5_glu_w", "s5_glu_b", "hgrn_gamma", "hgrn_norm", "att_w_qkv", "att_w_o", "ffn_w_in",
             "ffn_conv_w", "ffn_conv_b", "ffn_w_out")
    return (total, gx[None], *[res[n][0] for n in order], *[res[n][1] for n in order], *[res[n][2] for n in order],
            *[res[n][3] for n in order])
```
